```python
import math
import jax, jax.numpy as jnp
from jax import lax
import numpy as np

D_MODEL = 2048
BATCH = 8
SEQ = 2048
DEPTH = 1

D_MIX = D_MODEL
SSM_D_INNER = D_MIX // 2
SSM_HEAD_DIM = 64
SSM_N_HEADS = SSM_D_INNER // SSM_HEAD_DIM
SSM_N_GROUPS = 4
SSM_D_STATE = 128
SSM_CONV = 4
SSM_CHUNK = 128
SSM_CONV_DIM = SSM_D_INNER + 2 * SSM_N_GROUPS * SSM_D_STATE
ATTN_WIDTH = D_MIX - SSM_D_INNER
ATTN_HEAD_DIM = 64
ATTN_N_HEADS = ATTN_WIDTH // ATTN_HEAD_DIM
ATTN_N_KV = 2
WINDOW = 128
ATTN_BLOCK = WINDOW
D_FF = 4 * D_MODEL
EPS = 1e-5

IN_PROJ_DIM = SSM_D_INNER + SSM_CONV_DIM + SSM_N_HEADS + ATTN_WIDTH + 2 * ATTN_N_KV * ATTN_HEAD_DIM

kernel_name = 'hymba_ssd_swa_sink_hybrid'


def rmsnorm(x, g):
    xf = x.astype(jnp.float32)
    y = xf * lax.rsqrt(jnp.mean(xf * xf, axis=-1, keepdims=True) + EPS)
    return (y * g.astype(jnp.float32)).astype(x.dtype)


def causal_dwconv(u, w, b):
    K, C = w.shape
    out = lax.conv_general_dilated(
        u, w[:, None, :].astype(u.dtype), window_strides=(1,), padding=[(K - 1, 0)],
        dimension_numbers=('NWC', 'WIO', 'NWC'), feature_group_count=C)
    return out + b.astype(u.dtype)


def ssd_chunked(x, dt, A, Bm, Cm):
    b, L, H, P = x.shape
    G, N = Bm.shape[-2:]
    R = H // G
    Q = SSM_CHUNK
    nc = L // Q
    f32 = jnp.float32
    Xd = (x.astype(f32) * dt[..., None]).reshape(b, nc, Q, G, R, P)
    a = jnp.moveaxis((dt * A).reshape(b, nc, Q, G, R), 2, -1)
    a_cs = jnp.cumsum(a, axis=-1)
    Bc = Bm.astype(f32).reshape(b, nc, Q, G, N)
    Cc = Cm.astype(f32).reshape(b, nc, Q, G, N)
    idx = jnp.arange(Q)
    causal = idx[:, None] >= idx[None, :]
    seg = a_cs[..., :, None] - a_cs[..., None, :]
    Lmat = jnp.exp(jnp.where(causal, seg, -jnp.inf))
    CB = jnp.einsum('bclgn,bcsgn->bcgls', Cc, Bc)
    y_diag = jnp.einsum('bcgls,bcgrls,bcsgrp->bclgrp', CB, Lmat, Xd)
    decay_to_end = jnp.exp(a_cs[..., -1:] - a_cs)
    chunk_states = jnp.einsum('bcsgn,bcgrs,bcsgrp->bcgrpn', Bc, decay_to_end, Xd)
    chunk_decay = jnp.exp(a_cs[..., -1])

    def step(h, inp):
        s_c, d_c = inp
        return d_c[..., None, None] * h + s_c, h

    h0 = jnp.zeros((b, G, R, P, N), f32)
    _, prev = lax.scan(step, h0, (jnp.moveaxis(chunk_states, 1, 0), jnp.moveaxis(chunk_decay, 1, 0)))
    prev = jnp.moveaxis(prev, 0, 1)
    y_off = jnp.einsum('bclgn,bcgrpn,bcgrl->bclgrp', Cc, prev, jnp.exp(a_cs))
    return (y_diag + y_off).reshape(b, L, H, P)


def ssd_mixer(z, xBC, dt_raw, conv_w, conv_b, dt_bias, A_log, D_skip, norm_g):
    b, L, _ = z.shape
    f32 = jnp.float32
    xBC = jax.nn.silu(causal_dwconv(xBC, conv_w, conv_b))
    GN = SSM_N_GROUPS * SSM_D_STATE
    xs = xBC[..., :SSM_D_INNER].reshape(b, L, SSM_N_HEADS, SSM_HEAD_DIM)
    Bm = xBC[..., SSM_D_INNER:SSM_D_INNER + GN].reshape(b, L, SSM_N_GROUPS, SSM_D_STATE)
    Cm = xBC[..., SSM_D_INNER + GN:].reshape(b, L, SSM_N_GROUPS, SSM_D_STATE)
    dt = jax.nn.softplus(dt_raw.astype(f32) + dt_bias.astype(f32))
    A = -jnp.exp(A_log.astype(f32))
    y = ssd_chunked(xs, dt, A, Bm, Cm)
    y = y + xs.astype(f32) * D_skip.astype(f32)[:, None]
    y = y.reshape(b, L, SSM_D_INNER) * jax.nn.silu(z.astype(f32))
    yg = y.reshape(b, L, SSM_N_GROUPS, SSM_D_INNER // SSM_N_GROUPS)
    yg = yg * lax.rsqrt(jnp.mean(yg * yg, axis=-1, keepdims=True) + EPS)
    y = yg.reshape(b, L, SSM_D_INNER) * norm_g.astype(f32)
    return y.astype(z.dtype)


def swa_sink_attention(q, k, v, sinks):
    b, L, Hq, Dh = q.shape
    Hkv = k.shape[2]
    R = Hq // Hkv
    Qb = ATTN_BLOCK
    nb = L // Qb
    f32 = jnp.float32
    qb = q.astype(f32).reshape(b, nb, Qb, Hkv, R, Dh)
    pad = jnp.zeros((b, Qb, Hkv, Dh), f32)
    kp = jnp.concatenate([pad, k.astype(f32)], axis=1).reshape(b, nb + 1, Qb, Hkv, Dh)
    vp = jnp.concatenate([pad, v.astype(f32)], axis=1).reshape(b, nb + 1, Qb, Hkv, Dh)
    kb = jnp.concatenate([kp[:, :-1], kp[:, 1:]], axis=2)
    vb = jnp.concatenate([vp[:, :-1], vp[:, 1:]], axis=2)
    s = jnp.einsum('bnqhrd,bnkhd->bnhrqk', qb, kb) * (Dh ** -0.5)
    i = jnp.arange(Qb)[:, None]
    j = jnp.arange(2 * Qb)[None, :]
    n = jnp.arange(nb)[:, None, None]
    diff = Qb + i - j
    valid = (diff >= 0) & (diff < WINDOW) & ((n - 1) * Qb + j >= 0)
    s = jnp.where(valid[None, :, None, None], s, -jnp.inf)
    sink = sinks.astype(f32).reshape(Hkv, R)[None, None, :, :, None, None]
    m = jnp.maximum(jnp.max(s, axis=-1, keepdims=True), sink)
    p = jnp.exp(s - m)
    denom = jnp.sum(p, axis=-1, keepdims=True) + jnp.exp(sink - m)
    o = jnp.einsum('bnhrqk,bnkhd->bnqhrd', p / denom, vb)
    return o.reshape(b, L, Hq * Dh).astype(q.dtype)


def _fwd_setup_inputs(seed: int = 0) -> dict:
    key = jax.random.key(seed)
    ks = jax.random.split(key, 16)
    f32 = jnp.float32

    def nrm(k, shape, scale):
        return jax.random.normal(k, shape, f32) * scale

    x = nrm(ks[0], (BATCH, SEQ, D_MODEL), 1.0)
    mix_norm_g = 1.0 + nrm(ks[1], (DEPTH, D_MODEL), 0.02)
    w_in = nrm(ks[2], (DEPTH, D_MODEL, IN_PROJ_DIM), D_MODEL ** -0.5)
    conv_w = nrm(ks[3], (DEPTH, SSM_CONV, SSM_CONV_DIM), SSM_CONV ** -0.5)
    conv_b = nrm(ks[4], (DEPTH, SSM_CONV_DIM), 0.01)
    dt0 = jnp.exp(jax.random.uniform(ks[5], (DEPTH, SSM_N_HEADS), f32, math.log(1e-3), math.log(1e-1)))
    dt_bias = dt0 + jnp.log(-jnp.expm1(-dt0))
    A_log = jnp.log(jax.random.uniform(ks[6], (DEPTH, SSM_N_HEADS), f32, 1.0, 16.0))
    D_skip = 1.0 + nrm(ks[7], (DEPTH, SSM_N_HEADS), 0.02)
    ssm_norm_g = 1.0 + nrm(ks[8], (DEPTH, SSM_D_INNER), 0.02)
    attn_sinks = nrm(ks[9], (DEPTH, ATTN_N_HEADS), 0.5)
    attn_out_norm_g = 1.0 + nrm(ks[10], (DEPTH, ATTN_WIDTH), 0.02)
    w_out = nrm(ks[11], (DEPTH, D_MIX, D_MODEL), D_MIX ** -0.5)
    mlp_norm_g = 1.0 + nrm(ks[12], (DEPTH, D_MODEL), 0.02)
    w_up = nrm(ks[13], (DEPTH, D_MODEL, D_FF), D_MODEL ** -0.5)
    w_down = nrm(ks[14], (DEPTH, D_FF, D_MODEL), D_FF ** -0.5)
    final_norm_g = 1.0 + nrm(ks[15], (D_MODEL,), 0.02)
    return {'x': x, 'mix_norm_g': mix_norm_g, 'w_in': w_in, 'conv_w': conv_w, 'conv_b': conv_b,
            'dt_bias': dt_bias, 'A_log': A_log, 'D_skip': D_skip, 'ssm_norm_g': ssm_norm_g,
            'attn_sinks': attn_sinks, 'attn_out_norm_g': attn_out_norm_g, 'w_out': w_out,
            'mlp_norm_g': mlp_norm_g, 'w_up': w_up, 'w_down': w_down, 'final_norm_g': final_norm_g}


def _fwd_reference(x, mix_norm_g, w_in, conv_w, conv_b, dt_bias, A_log, D_skip, ssm_norm_g,
              attn_sinks, attn_out_norm_g, w_out, mlp_norm_g, w_up, w_down, final_norm_g):
    b, L, _ = x.shape
    KVW = ATTN_N_KV * ATTN_HEAD_DIM
    splits = [SSM_D_INNER, SSM_D_INNER + SSM_CONV_DIM, SSM_D_INNER + SSM_CONV_DIM + SSM_N_HEADS,
              SSM_D_INNER + SSM_CONV_DIM + SSM_N_HEADS + ATTN_WIDTH,
              SSM_D_INNER + SSM_CONV_DIM + SSM_N_HEADS + ATTN_WIDTH + KVW]
    for l in range(DEPTH):
        h = rmsnorm(x, mix_norm_g[l])
        proj = jnp.einsum('bsd,de->bse', h, w_in[l])
        z, xBC, dt_raw, q, k, v = jnp.split(proj, splits, axis=-1)
        y_ssm = ssd_mixer(z, xBC, dt_raw, conv_w[l], conv_b[l], dt_bias[l], A_log[l],
                          D_skip[l], ssm_norm_g[l])
        y_att = swa_sink_attention(q.reshape(b, L, ATTN_N_HEADS, ATTN_HEAD_DIM),
                                   k.reshape(b, L, ATTN_N_KV, ATTN_HEAD_DIM),
                                   v.reshape(b, L, ATTN_N_KV, ATTN_HEAD_DIM), attn_sinks[l])
        y_att = rmsnorm(y_att, attn_out_norm_g[l])
        y = jnp.concatenate([y_ssm, y_att.astype(y_ssm.dtype)], axis=-1)
        x = x + jnp.einsum('bse,ed->bsd', y, w_out[l])
        h = rmsnorm(x, mlp_norm_g[l])
        u = jnp.einsum('bsd,df->bsf', h, w_up[l])
        x = x + jnp.einsum('bsf,fd->bsd', jnp.square(jax.nn.relu(u)), w_down[l])
    return rmsnorm(x, final_norm_g)


import jax as _jax
import jax.numpy as _jnp

TWIN_FORMAT = 'train_step'
FWD_PARAMS = ['x', 'mix_norm_g', 'w_in', 'conv_w', 'conv_b', 'dt_bias', 'A_log', 'D_skip', 'ssm_norm_g', 'attn_sinks', 'attn_out_norm_g', 'w_out', 'mlp_norm_g', 'w_up', 'w_down', 'final_norm_g']
TWIN_WEIGHTS = ['mix_norm_g', 'w_in', 'conv_w', 'conv_b', 'dt_bias', 'A_log', 'D_skip', 'ssm_norm_g', 'attn_sinks', 'attn_out_norm_g', 'w_out', 'mlp_norm_g', 'w_up', 'w_down', 'final_norm_g']
TWIN_DIFF_INPUT = 'x'
TWIN_INPUTS = ['x', 'mix_norm_g', 'w_in', 'conv_w', 'conv_b', 'dt_bias', 'A_log', 'D_skip', 'ssm_norm_g', 'attn_sinks', 'attn_out_norm_g', 'w_out', 'mlp_norm_g', 'w_up', 'w_down', 'final_norm_g', 'loss_target', 'm_mix_norm_g', 'm_w_in', 'm_conv_w', 'm_conv_b', 'm_dt_bias', 'm_A_log', 'm_D_skip', 'm_ssm_norm_g', 'm_attn_sinks', 'm_attn_out_norm_g', 'm_w_out', 'm_mlp_norm_g', 'm_w_up', 'm_w_down', 'm_final_norm_g', 'v_mix_norm_g', 'v_w_in', 'v_conv_w', 'v_conv_b', 'v_dt_bias', 'v_A_log', 'v_D_skip', 'v_ssm_norm_g', 'v_attn_sinks', 'v_attn_out_norm_g', 'v_w_out', 'v_mlp_norm_g', 'v_w_up', 'v_w_down', 'v_final_norm_g']
TWIN_OUTPUTS = ['loss', 'grad_x', 'grad_mix_norm_g', 'grad_w_in', 'grad_conv_w', 'grad_conv_b', 'grad_dt_bias', 'grad_A_log', 'grad_D_skip', 'grad_ssm_norm_g', 'grad_attn_sinks', 'grad_attn_out_norm_g', 'grad_w_out', 'grad_mlp_norm_g', 'grad_w_up', 'grad_w_down', 'grad_final_norm_g', 'delta_mix_norm_g', 'delta_w_in', 'delta_conv_w', 'delta_conv_b', 'delta_dt_bias', 'delta_A_log', 'delta_D_skip', 'delta_ssm_norm_g', 'delta_attn_sinks', 'delta_attn_out_norm_g', 'delta_w_out', 'delta_mlp_norm_g', 'delta_w_up', 'delta_w_down', 'delta_final_norm_g', 'new_m_mix_norm_g', 'new_m_w_in', 'new_m_conv_w', 'new_m_conv_b', 'new_m_dt_bias', 'new_m_A_log', 'new_m_D_skip', 'new_m_ssm_norm_g', 'new_m_attn_sinks', 'new_m_attn_out_norm_g', 'new_m_w_out', 'new_m_mlp_norm_g', 'new_m_w_up', 'new_m_w_down', 'new_m_final_norm_g', 'new_v_mix_norm_g', 'new_v_w_in', 'new_v_conv_w', 'new_v_conv_b', 'new_v_dt_bias', 'new_v_A_log', 'new_v_D_skip', 'new_v_ssm_norm_g', 'new_v_attn_sinks', 'new_v_attn_out_norm_g', 'new_v_w_out', 'new_v_mlp_norm_g', 'new_v_w_up', 'new_v_w_down', 'new_v_final_norm_g']
TWIN_LEAF_KINDS = {'loss': 'loss', 'grad_x': 'grad_x', 'grad_mix_norm_g': 'grad_w', 'grad_w_in': 'grad_w', 'grad_conv_w': 'grad_w', 'grad_conv_b': 'grad_w', 'grad_dt_bias': 'grad_w', 'grad_A_log': 'grad_w', 'grad_D_skip': 'grad_w', 'grad_ssm_norm_g': 'grad_w', 'grad_attn_sinks': 'grad_w', 'grad_attn_out_norm_g': 'grad_w', 'grad_w_out': 'grad_w', 'grad_mlp_norm_g': 'grad_w', 'grad_w_up': 'grad_w', 'grad_w_down': 'grad_w', 'grad_final_norm_g': 'grad_w', 'delta_mix_norm_g': 'delta_w', 'delta_w_in': 'delta_w', 'delta_conv_w': 'delta_w', 'delta_conv_b': 'delta_w', 'delta_dt_bias': 'delta_w', 'delta_A_log': 'delta_w', 'delta_D_skip': 'delta_w', 'delta_ssm_norm_g': 'delta_w', 'delta_attn_sinks': 'delta_w', 'delta_attn_out_norm_g': 'delta_w', 'delta_w_out': 'delta_w', 'delta_mlp_norm_g': 'delta_w', 'delta_w_up': 'delta_w', 'delta_w_down': 'delta_w', 'delta_final_norm_g': 'delta_w', 'new_m_mix_norm_g': 'new_m', 'new_m_w_in': 'new_m', 'new_m_conv_w': 'new_m', 'new_m_conv_b': 'new_m', 'new_m_dt_bias': 'new_m', 'new_m_A_log': 'new_m', 'new_m_D_skip': 'new_m', 'new_m_ssm_norm_g': 'new_m', 'new_m_attn_sinks': 'new_m', 'new_m_attn_out_norm_g': 'new_m', 'new_m_w_out': 'new_m', 'new_m_mlp_norm_g': 'new_m', 'new_m_w_up': 'new_m', 'new_m_w_down': 'new_m', 'new_m_final_norm_g': 'new_m', 'new_v_mix_norm_g': 'new_v', 'new_v_w_in': 'new_v', 'new_v_conv_w': 'new_v', 'new_v_conv_b': 'new_v', 'new_v_dt_bias': 'new_v', 'new_v_A_log': 'new_v', 'new_v_D_skip': 'new_v', 'new_v_ssm_norm_g': 'new_v', 'new_v_attn_sinks': 'new_v', 'new_v_attn_out_norm_g': 'new_v', 'new_v_w_out': 'new_v', 'new_v_mlp_norm_g': 'new_v', 'new_v_w_up': 'new_v', 'new_v_w_down': 'new_v', 'new_v_final_norm_g': 'new_v'}


def _forward(args):
    return _fwd_reference(*[args[k] for k in FWD_PARAMS])


def _output_shape():
    out = _jax.eval_shape(lambda: _forward(_fwd_setup_inputs(0)))
    return out.shape, out.dtype

N_MICROBATCH = 1
ADAM_LR = 0.001
ADAM_B1 = 0.9
ADAM_B2 = 0.999
ADAM_EPS = 1e-08
ADAM_WD = 0.01
ADAM_STEP = 10
PER_EXAMPLE_BATCH_AXIS = {'x': 0, 'loss_target': 0}
SHARED_INPUTS = []
_WEIGHT_DTYPES = {'mix_norm_g': _jnp.float32, 'w_in': _jnp.float32, 'conv_w': _jnp.float32, 'conv_b': _jnp.float32, 'dt_bias': _jnp.float32, 'A_log': _jnp.float32, 'D_skip': _jnp.float32, 'ssm_norm_g': _jnp.float32, 'attn_sinks': _jnp.float32, 'attn_out_norm_g': _jnp.float32, 'w_out': _jnp.float32, 'mlp_norm_g': _jnp.float32, 'w_up': _jnp.float32, 'w_down': _jnp.float32, 'final_norm_g': _jnp.float32}
MOMENT_SCALE = {'mix_norm_g': 7.781379e-02, 'w_in': 5.313940e-02, 'conv_w': 3.648138e-02, 'conv_b': 4.621880e-02, 'dt_bias': 1.063231e-01, 'A_log': 1.385354e-01, 'D_skip': 1.904627e-01, 'ssm_norm_g': 5.027399e-02, 'attn_sinks': 1.007369e-02, 'attn_out_norm_g': 4.890700e-02, 'w_out': 4.765493e-02, 'mlp_norm_g': 4.938596e-02, 'w_up': 2.375189e-02, 'w_down': 4.394164e-02, 'final_norm_g': 8.040526e+00}


def _to_microbatches(a, axis):
    t = _jnp.moveaxis(a, axis, 0)
    t = t.reshape((N_MICROBATCH, t.shape[0] // N_MICROBATCH) + t.shape[1:])
    return _jnp.moveaxis(t, 1, axis + 1)


def setup_inputs(seed: int = 0) -> dict:
    inp = _fwd_setup_inputs(seed)
    key = _jax.random.fold_in(_jax.random.key(seed), 7919)
    shape, _ = _output_shape()
    out = dict(inp)
    out["loss_target"] = _jax.random.normal(_jax.random.fold_in(key, 0), shape, _jnp.float32)
    for i, name in enumerate(TWIN_WEIGHTS):
        w = inp[name].astype(_jnp.float32)
        if MOMENT_SCALE is None:
            s = _jnp.sqrt(_jnp.mean(_jnp.square(w)) + 1e-30)
        else:
            s = MOMENT_SCALE[name]
        km, kv = _jax.random.split(_jax.random.fold_in(key, i + 1))
        out[name] = w
        out["m_" + name] = s * _jax.random.normal(km, w.shape, _jnp.float32)
        out["v_" + name] = (s * s) * _jax.random.uniform(kv, w.shape, _jnp.float32, 0.5, 1.5)
    if N_MICROBATCH > 1:
        for name, axis in PER_EXAMPLE_BATCH_AXIS.items():
            out[name] = _to_microbatches(out[name], axis)
    return {'x': out['x'], 'mix_norm_g': out['mix_norm_g'], 'w_in': out['w_in'], 'conv_w': out['conv_w'], 'conv_b': out['conv_b'], 'dt_bias': out['dt_bias'], 'A_log': out['A_log'], 'D_skip': out['D_skip'], 'ssm_norm_g': out['ssm_norm_g'], 'attn_sinks': out['attn_sinks'], 'attn_out_norm_g': out['attn_out_norm_g'], 'w_out': out['w_out'], 'mlp_norm_g': out['mlp_norm_g'], 'w_up': out['w_up'], 'w_down': out['w_down'], 'final_norm_g': out['final_norm_g'], 'loss_target': out['loss_target'], 'm_mix_norm_g': out['m_mix_norm_g'], 'm_w_in': out['m_w_in'], 'm_conv_w': out['m_conv_w'], 'm_conv_b': out['m_conv_b'], 'm_dt_bias': out['m_dt_bias'], 'm_A_log': out['m_A_log'], 'm_D_skip': out['m_D_skip'], 'm_ssm_norm_g': out['m_ssm_norm_g'], 'm_attn_sinks': out['m_attn_sinks'], 'm_attn_out_norm_g': out['m_attn_out_norm_g'], 'm_w_out': out['m_w_out'], 'm_mlp_norm_g': out['m_mlp_norm_g'], 'm_w_up': out['m_w_up'], 'm_w_down': out['m_w_down'], 'm_final_norm_g': out['m_final_norm_g'], 'v_mix_norm_g': out['v_mix_norm_g'], 'v_w_in': out['v_w_in'], 'v_conv_w': out['v_conv_w'], 'v_conv_b': out['v_conv_b'], 'v_dt_bias': out['v_dt_bias'], 'v_A_log': out['v_A_log'], 'v_D_skip': out['v_D_skip'], 'v_ssm_norm_g': out['v_ssm_norm_g'], 'v_attn_sinks': out['v_attn_sinks'], 'v_attn_out_norm_g': out['v_attn_out_norm_g'], 'v_w_out': out['v_w_out'], 'v_mlp_norm_g': out['v_mlp_norm_g'], 'v_w_up': out['v_w_up'], 'v_w_down': out['v_w_down'], 'v_final_norm_g': out['v_final_norm_g']}


def _loss(weights, diff, rest, loss_target):
    with _jax.named_scope("forward"):
        args = {**rest, TWIN_DIFF_INPUT: diff, **{k: w.astype(_WEIGHT_DTYPES[k]) for k, w in weights.items()}}
        y = _forward(args)
    with _jax.named_scope("loss_head"):
        err = _jnp.square(y.astype(_jnp.float32) - loss_target)
        return 0.5 * _jnp.sum(_jnp.mean(err, axis=-1)) if err.ndim else 0.5 * err


def _adamw(w, g, m, v):
    m = ADAM_B1 * m + (1.0 - ADAM_B1) * g
    v = ADAM_B2 * v + (1.0 - ADAM_B2) * _jnp.square(g)
    m_hat = m / (1.0 - ADAM_B1 ** ADAM_STEP)
    v_hat = v / (1.0 - ADAM_B2 ** ADAM_STEP)
    delta = -ADAM_LR * (m_hat / (_jnp.sqrt(v_hat) + ADAM_EPS) + ADAM_WD * w)
    return delta, m, v


def reference(x, mix_norm_g, w_in, conv_w, conv_b, dt_bias, A_log, D_skip, ssm_norm_g, attn_sinks, attn_out_norm_g, w_out, mlp_norm_g, w_up, w_down, final_norm_g, loss_target, m_mix_norm_g, m_w_in, m_conv_w, m_conv_b, m_dt_bias, m_A_log, m_D_skip, m_ssm_norm_g, m_attn_sinks, m_attn_out_norm_g, m_w_out, m_mlp_norm_g, m_w_up, m_w_down, m_final_norm_g, v_mix_norm_g, v_w_in, v_conv_w, v_conv_b, v_dt_bias, v_A_log, v_D_skip, v_ssm_norm_g, v_attn_sinks, v_attn_out_norm_g, v_w_out, v_mlp_norm_g, v_w_up, v_w_down, v_final_norm_g):
    given = dict(x=x, mix_norm_g=mix_norm_g, w_in=w_in, conv_w=conv_w, conv_b=conv_b, dt_bias=dt_bias, A_log=A_log, D_skip=D_skip, ssm_norm_g=ssm_norm_g, attn_sinks=attn_sinks, attn_out_norm_g=attn_out_norm_g, w_out=w_out, mlp_norm_g=mlp_norm_g, w_up=w_up, w_down=w_down, final_norm_g=final_norm_g, loss_target=loss_target, m_mix_norm_g=m_mix_norm_g, m_w_in=m_w_in, m_conv_w=m_conv_w, m_conv_b=m_conv_b, m_dt_bias=m_dt_bias, m_A_log=m_A_log, m_D_skip=m_D_skip, m_ssm_norm_g=m_ssm_norm_g, m_attn_sinks=m_attn_sinks, m_attn_out_norm_g=m_attn_out_norm_g, m_w_out=m_w_out, m_mlp_norm_g=m_mlp_norm_g, m_w_up=m_w_up, m_w_down=m_w_down, m_final_norm_g=m_final_norm_g, v_mix_norm_g=v_mix_norm_g, v_w_in=v_w_in, v_conv_w=v_conv_w, v_conv_b=v_conv_b, v_dt_bias=v_dt_bias, v_A_log=v_A_log, v_D_skip=v_D_skip, v_ssm_norm_g=v_ssm_norm_g, v_attn_sinks=v_attn_sinks, v_attn_out_norm_g=v_attn_out_norm_g, v_w_out=v_w_out, v_mlp_norm_g=v_mlp_norm_g, v_w_up=v_w_up, v_w_down=v_w_down, v_final_norm_g=v_final_norm_g)
    weights = {n: given[n] for n in TWIN_WEIGHTS}
    shared = {n: given[n] for n in SHARED_INPUTS}
    per_example = {n: given[n] for n in ['x']}
    grad_fn = _jax.value_and_grad(_loss, argnums=(0, 1))

    def one_microbatch(ex, loss_target):
        ex = dict(ex)
        diff = ex.pop(TWIN_DIFF_INPUT)
        return grad_fn(weights, diff, {**shared, **ex}, loss_target)

    if N_MICROBATCH == 1:
        loss, (grad_w, grad_x) = one_microbatch(per_example, given["loss_target"])
    else:
        def body(carry, xs):
            loss_sum, grad_sum = carry
            l_k, (gw_k, gx_k) = one_microbatch(xs[0], xs[1])
            with _jax.named_scope("update"):
                return (loss_sum + l_k, _jax.tree.map(_jnp.add, grad_sum, gw_k)), gx_k

        init = (_jnp.zeros((), _jnp.float32), _jax.tree.map(_jnp.zeros_like, weights))
        (loss, grad_w), grad_x = _jax.lax.scan(body, init, (per_example, given["loss_target"]))
    with _jax.named_scope("update"):
        delta_w, new_m, new_v = {}, {}, {}
        for n in TWIN_WEIGHTS:
            delta_w[n], new_m[n], new_v[n] = _adamw(weights[n], grad_w[n], given["m_" + n], given["v_" + n])
    return (loss, grad_x, *[grad_w[n] for n in TWIN_WEIGHTS], *[delta_w[n] for n in TWIN_WEIGHTS],
            *[new_m[n] for n in TWIN_WEIGHTS], *[new_v[n] for n in TWIN_WEIGHTS])
```

```python
import functools

import jax
import jax.numpy as jnp
from jax import lax
from jax.experimental import pallas as pl
from jax.experimental.pallas import tpu as pltpu

F32 = jnp.float32
BF16 = jnp.bfloat16
HI = lax.Precision.HIGHEST
MESH = pl.DeviceIdType.MESH

EPS = 1e-5
D_MODEL = 2048
D_INNER = 1024
N_HEADS = 16
HEAD_DIM = 64
N_GROUPS = 4
D_STATE = 128
CHUNK = 128
CONV_K = 4
CONV_DIM = 2048
ATTN_W = 1024
KV_W = 128
WINDOW = 128
D_FF = 8192
IN_PROJ = 4368
N_DEV = 8
NP = 4608
OFF_Z, OFF_X, OFF_B, OFF_C, OFF_Q, OFF_K, OFF_V, OFF_DT = 0, 1024, 2048, 2560, 3072, 4096, 4224, 4352
NAT_DT = 3072

ADAM_LR = 0.001
ADAM_B1 = 0.9
ADAM_B2 = 0.999
ADAM_EPS = 1e-08
ADAM_WD = 0.01
ADAM_STEP = 10

VMEM_LIMIT = 52 * 1024 * 1024
SMALL_ROWS = 16
NEG = -1e30


def _cparams(sem=None):
    return pltpu.CompilerParams(dimension_semantics=sem, vmem_limit_bytes=VMEM_LIMIT)


def _hdot(a, b):
    return jnp.dot(a, b, precision=HI, preferred_element_type=F32)


def _dot_nn(a, b):
    return lax.dot_general(a, b, (((1,), (0,)), ((), ())), preferred_element_type=F32)


def _dot_nt(a, b):
    return lax.dot_general(a, b, (((1,), (1,)), ((), ())), preferred_element_type=F32)


def _dot_tn(a, b):
    return lax.dot_general(a, b, (((0,), (0,)), ((), ())), preferred_element_type=F32)


def _softplus(v):
    return jnp.maximum(v, 0.0) + jnp.log1p(jnp.exp(-jnp.abs(v)))


def _sigmoid(v):
    return 1.0 / (1.0 + jnp.exp(-v))


def _matmul(a, b, *, mode, grid, a_spec, b_spec, out_shapes, out_specs, tile, name,
            extras=(), extra_specs=(), epilogue=None):
    nk = grid[2]
    n_ex = len(extras)
    n_out = len(out_shapes)
    dot = {"nn": _dot_nn, "nt": _dot_nt, "tn": _dot_tn}[mode]

    def finish(acc, ex_refs, out_refs):
        res = (acc,) if epilogue is None else epilogue(acc, *[e[...] for e in ex_refs])
        for o, r in zip(out_refs, res):
            o[...] = r.astype(o.dtype)

    def body(*refs):
        a_ref, b_ref = refs[0], refs[1]
        ex_refs = refs[2:2 + n_ex]
        out_refs = refs[2 + n_ex:2 + n_ex + n_out]
        part = dot(a_ref[...].astype(BF16), b_ref[...].astype(BF16))
        if nk == 1:
            finish(part, ex_refs, out_refs)
        else:
            acc_ref = refs[-1]
            k = pl.program_id(2)

            @pl.when(k == 0)
            def _():
                acc_ref[...] = part

            @pl.when(k > 0)
            def _():
                acc_ref[...] += part

            @pl.when(k == nk - 1)
            def _():
                finish(acc_ref[...], ex_refs, out_refs)

    scratch = [] if nk == 1 else [pltpu.VMEM(tile, F32)]
    outs = pl.pallas_call(
        body, grid=grid, in_specs=[a_spec, b_spec, *extra_specs], out_specs=list(out_specs),
        out_shape=list(out_shapes), scratch_shapes=scratch, name=name,
        compiler_params=_cparams(("parallel", "parallel", "arbitrary")),
    )(a, b, *extras)
    return outs


def _mm_simple(a, b, *, mode, M, N, K, tm, tn, tk, out_dtype, name, extras=(), epilogue=None, n_out=1,
               out_dtypes=None):
    grid = (M // tm, N // tn, K // tk)
    if mode == "nn":
        a_spec = pl.BlockSpec((tm, tk), lambda i, j, k: (i, k))
        b_spec = pl.BlockSpec((tk, tn), lambda i, j, k: (k, j))
    elif mode == "nt":
        a_spec = pl.BlockSpec((tm, tk), lambda i, j, k: (i, k))
        b_spec = pl.BlockSpec((tn, tk), lambda i, j, k: (j, k))
    else:
        a_spec = pl.BlockSpec((tk, tm), lambda i, j, k: (k, i))
        b_spec = pl.BlockSpec((tk, tn), lambda i, j, k: (k, j))
    o_spec = pl.BlockSpec((tm, tn), lambda i, j, k: (i, j))
    dts = out_dtypes if out_dtypes is not None else [out_dtype] * n_out
    return _matmul(a, b, mode=mode, grid=grid, a_spec=a_spec, b_spec=b_spec,
                   out_shapes=[jax.ShapeDtypeStruct((M, N), d) for d in dts],
                   out_specs=[o_spec] * len(dts), tile=(tm, tn), name=name,
                   extras=extras, extra_specs=[o_spec] * len(extras), epilogue=epilogue)


ROW_BLOCK = 256


def _rmsnorm_fwd(x, g, name):
    T, D = x.shape

    def body(x_ref, g_ref, o_ref):
        xf = x_ref[...]
        r = lax.rsqrt(jnp.mean(xf * xf, axis=-1, keepdims=True) + EPS)
        o_ref[...] = (xf * r * g_ref[...]).astype(BF16)

    return pl.pallas_call(
        body, grid=(T // ROW_BLOCK,),
        in_specs=[pl.BlockSpec((ROW_BLOCK, D), lambda i: (i, 0)), pl.BlockSpec((1, D), lambda i: (0, 0))],
        out_specs=pl.BlockSpec((ROW_BLOCK, D), lambda i: (i, 0)),
        out_shape=jax.ShapeDtypeStruct((T, D), BF16), name=name, compiler_params=_cparams(("parallel",)),
    )(x, g)


def _rmsnorm_bwd(dh, x, g, dres, name):
    T, D = x.shape

    def body(dh_ref, x_ref, g_ref, dres_ref, dx_ref, dxb_ref, dg_ref):
        i = pl.program_id(0)
        xf = x_ref[...]
        r = lax.rsqrt(jnp.mean(xf * xf, axis=-1, keepdims=True) + EPS)
        xh = xf * r
        d = dh_ref[...]

        @pl.when(i == 0)
        def _():
            dg_ref[...] = jnp.zeros_like(dg_ref)

        dg_ref[...] += jnp.sum(d * xh, axis=0, keepdims=True)
        dxh = d * g_ref[...]
        dx = r * (dxh - xh * jnp.mean(dxh * xh, axis=-1, keepdims=True)) + dres_ref[...]
        dx_ref[...] = dx
        dxb_ref[...] = dx.astype(BF16)

    row = pl.BlockSpec((ROW_BLOCK, D), lambda i: (i, 0))
    vec = pl.BlockSpec((1, D), lambda i: (0, 0))
    return pl.pallas_call(
        body, grid=(T // ROW_BLOCK,), in_specs=[row, row, vec, row], out_specs=[row, row, vec],
        out_shape=[jax.ShapeDtypeStruct((T, D), F32), jax.ShapeDtypeStruct((T, D), BF16),
                   jax.ShapeDtypeStruct((1, D), F32)],
        name=name, compiler_params=_cparams(("arbitrary",)),
    )(dh, x, g, dres)


def _final_loss(x3, tgt, g, name):
    T, D = x3.shape

    def body(x_ref, t_ref, g_ref, loss_ref, dg_ref, dx_ref, dxb_ref):
        i = pl.program_id(0)
        xf = x_ref[...]
        r = lax.rsqrt(jnp.mean(xf * xf, axis=-1, keepdims=True) + EPS)
        xh = xf * r
        gg = g_ref[...]
        err = xh * gg - t_ref[...]

        @pl.when(i == 0)
        def _():
            dg_ref[...] = jnp.zeros_like(dg_ref)
            loss_ref[...] = jnp.zeros_like(loss_ref)

        part = jnp.sum(jnp.sum(err * err, axis=-1, keepdims=True), axis=0, keepdims=True) * (0.5 / D)
        loss_ref[...] += jnp.broadcast_to(part, loss_ref.shape)
        dout = err * (1.0 / D)
        dg_ref[...] += jnp.sum(dout * xh, axis=0, keepdims=True)
        dxh = dout * gg
        dx = r * (dxh - xh * jnp.mean(dxh * xh, axis=-1, keepdims=True))
        dx_ref[...] = dx
        dxb_ref[...] = dx.astype(BF16)

    row = pl.BlockSpec((ROW_BLOCK, D), lambda i: (i, 0))
    vec = pl.BlockSpec((1, D), lambda i: (0, 0))
    return pl.pallas_call(
        body, grid=(T // ROW_BLOCK,), in_specs=[row, row, vec],
        out_specs=[pl.BlockSpec((1, 128), lambda i: (0, 0)), vec, row, row],
        out_shape=[jax.ShapeDtypeStruct((1, 128), F32), jax.ShapeDtypeStruct((1, D), F32),
                   jax.ShapeDtypeStruct((T, D), F32), jax.ShapeDtypeStruct((T, D), BF16)],
        name=name, compiler_params=_cparams(("arbitrary",)),
    )(x3, tgt, g)


CONV_BLOCK = 256


def _conv_apply(u, w, b):
    row = lax.broadcasted_iota(jnp.int32, u.shape, 0)
    acc = b + w[CONV_K - 1:CONV_K, :] * u
    shifted = []
    for j in range(1, CONV_K):
        uj = jnp.where(row >= j, pltpu.roll(u, j, axis=0), 0.0)
        shifted.append(uj)
        acc = acc + w[CONV_K - 1 - j:CONV_K - j, :] * uj
    return acc, shifted


def _conv_fwd(proj, conv_w, conv_b, name):
    T = proj.shape[0]
    cb0 = OFF_X // CONV_BLOCK

    def body(u_ref, w_ref, b_ref, o_ref):
        c, _ = _conv_apply(u_ref[...], w_ref[...], b_ref[...])
        o_ref[...] = c * _sigmoid(c)

    return pl.pallas_call(
        body, grid=(CONV_DIM // CONV_BLOCK,),
        in_specs=[pl.BlockSpec((T, CONV_BLOCK), lambda j: (0, cb0 + j)),
                  pl.BlockSpec((CONV_K, CONV_BLOCK), lambda j: (0, j)),
                  pl.BlockSpec((1, CONV_BLOCK), lambda j: (0, j))],
        out_specs=pl.BlockSpec((T, CONV_BLOCK), lambda j: (0, j)),
        out_shape=jax.ShapeDtypeStruct((T, CONV_DIM), F32), name=name, compiler_params=_cparams(("parallel",)),
    )(proj, conv_w, conv_b)


def _conv_bwd(proj, dact, conv_w, conv_b, name):
    T = proj.shape[0]
    cb0 = OFF_X // CONV_BLOCK

    def body(u_ref, d_ref, w_ref, b_ref, du_ref, dw_ref, db_ref):
        u = u_ref[...]
        w = w_ref[...]
        c, shifted = _conv_apply(u, w, b_ref[...])
        sg = _sigmoid(c)
        dc = d_ref[...] * sg * (1.0 + c * (1.0 - sg))
        row = lax.broadcasted_iota(jnp.int32, u.shape, 0)
        du = w[CONV_K - 1:CONV_K, :] * dc
        dw_ref[CONV_K - 1:CONV_K, :] = jnp.sum(dc * u, axis=0, keepdims=True)
        for j in range(1, CONV_K):
            dcj = jnp.where(row < T - j, pltpu.roll(dc, T - j, axis=0), 0.0)
            du = du + w[CONV_K - 1 - j:CONV_K - j, :] * dcj
            dw_ref[CONV_K - 1 - j:CONV_K - j, :] = jnp.sum(dc * shifted[j - 1], axis=0, keepdims=True)
        db_ref[...] = jnp.sum(dc, axis=0, keepdims=True)
        du_ref[...] = du.astype(BF16)

    return pl.pallas_call(
        body, grid=(CONV_DIM // CONV_BLOCK,),
        in_specs=[pl.BlockSpec((T, CONV_BLOCK), lambda j: (0, cb0 + j)),
                  pl.BlockSpec((T, CONV_BLOCK), lambda j: (0, j)),
                  pl.BlockSpec((CONV_K, CONV_BLOCK), lambda j: (0, j)),
                  pl.BlockSpec((1, CONV_BLOCK), lambda j: (0, j))],
        out_specs=[pl.BlockSpec((T, CONV_BLOCK), lambda j: (0, j)),
                   pl.BlockSpec((CONV_K, CONV_BLOCK), lambda j: (0, j)),
                   pl.BlockSpec((1, CONV_BLOCK), lambda j: (0, j))],
        out_shape=[jax.ShapeDtypeStruct((T, CONV_DIM), BF16), jax.ShapeDtypeStruct((CONV_K, CONV_DIM), F32),
                   jax.ShapeDtypeStruct((1, CONV_DIM), F32)],
        name=name, compiler_params=_cparams(("parallel",)),
    )(proj, dact, conv_w, conv_b)


GROUP_W = D_INNER // N_GROUPS
HEADS_PER_GROUP = N_HEADS // N_GROUPS


def _expand_mat():
    h = lax.broadcasted_iota(jnp.int32, (N_HEADS, D_INNER), 0)
    j = lax.broadcasted_iota(jnp.int32, (N_HEADS, D_INNER), 1)
    return (j // HEAD_DIM == h).astype(F32)


def _reduce_mat(g):
    j = lax.broadcasted_iota(jnp.int32, (GROUP_W, N_HEADS), 0)
    h = lax.broadcasted_iota(jnp.int32, (GROUP_W, N_HEADS), 1)
    return (g * HEADS_PER_GROUP + j // HEAD_DIM == h).astype(F32)


def _col16(v, h):
    lane = lax.broadcasted_iota(jnp.int32, v.shape, 1)
    return jnp.sum(jnp.where(lane == h, v, 0.0), axis=1, keepdims=True)


def _ssd_pre(dt_raw, dtT_raw, dtb, dtbT, alog, alogT):
    Q = CHUNK
    xdt = dt_raw + dtb
    dt = _softplus(xdt)
    dtT = _softplus(dtT_raw + dtbT)
    A = -jnp.exp(alog)
    AT = -jnp.exp(alogT)
    row = lax.broadcasted_iota(jnp.int32, (Q, Q), 0)
    col = lax.broadcasted_iota(jnp.int32, (Q, Q), 1)
    tril = (row >= col).astype(F32)
    triu = (row <= col).astype(F32)
    cs = _hdot(tril, dt * A)
    csT = _hdot(dtT * AT, triu)
    return xdt, dt, A, cs, csT, row >= col, triu


def _decay_matrix(cs, csT, h, causal):
    seg = _col16(cs, h) - csT[h:h + 1, :]
    return jnp.where(causal, jnp.exp(jnp.minimum(seg, 0.0)), 0.0)


def _ssd_in_specs(nc, rev):
    def cidx(c):
        return (nc - 1 - c) if rev else c

    return [
        pl.BlockSpec((CHUNK, D_INNER), lambda c: (cidx(c), 0)),
        pl.BlockSpec((CHUNK, 512), lambda c: (cidx(c), 2)),
        pl.BlockSpec((CHUNK, 512), lambda c: (cidx(c), 3)),
        pl.BlockSpec((CHUNK, D_INNER), lambda c: (cidx(c), 0)),
        pl.BlockSpec((CHUNK, 128), lambda c: (cidx(c), OFF_DT // 128)),
        pl.BlockSpec((N_HEADS, CHUNK), lambda c: (0, cidx(c))),
        pl.BlockSpec((1, N_HEADS), lambda c: (0, 0)),
        pl.BlockSpec((N_HEADS, 1), lambda c: (0, 0)),
        pl.BlockSpec((1, N_HEADS), lambda c: (0, 0)),
        pl.BlockSpec((N_HEADS, 1), lambda c: (0, 0)),
        pl.BlockSpec((1, D_INNER), lambda c: (0, 0)),
        pl.BlockSpec((1, D_INNER), lambda c: (0, 0)),
    ]


def _ssd_fwd(xbc, proj, dtT, dtb, dtbT, alog, alogT, dfull, ng, name):
    T = xbc.shape[0]
    nc = T // CHUNK
    Q = CHUNK

    def body(xs_ref, B_ref, C_ref, z_ref, dt_ref, dtT_ref, dtb_ref, dtbT_ref, al_ref, alT_ref, df_ref, ng_ref,
             y_ref, ypre_ref, hs_ref, h_scr):
        c = pl.program_id(0)

        @pl.when(c == 0)
        def _():
            h_scr[...] = jnp.zeros_like(h_scr)

        _, dt, _, cs, csT, causal, _ = _ssd_pre(dt_ref[:, :N_HEADS], dtT_ref[...], dtb_ref[...], dtbT_ref[...],
                                                al_ref[...], alT_ref[...])
        ex = _expand_mat()
        dt_full = _hdot(dt, ex)
        cs_full = _hdot(cs, ex)
        cs_last = cs_full[Q - 1:Q, :]
        xs = xs_ref[...]
        xd = xs * dt_full
        e_full = jnp.exp(cs_full)
        dec_full = jnp.exp(cs_last - cs_full)
        cd_full = jnp.exp(cs_last)
        lane_head = lax.broadcasted_iota(jnp.int32, (1, GROUP_W), 1) // HEAD_DIM
        for g in range(N_GROUPS):
            sl = slice(g * GROUP_W, (g + 1) * GROUP_W)
            Bg = B_ref[:, g * D_STATE:(g + 1) * D_STATE].astype(BF16)
            Cg = C_ref[:, g * D_STATE:(g + 1) * D_STATE].astype(BF16)
            CB = _dot_nt(Cg, Bg)
            hg = h_scr[g]
            yoff = _dot_nn(Cg, hg.astype(BF16)) * e_full[:, sl]
            xd_g = xd[:, sl]
            S = _dot_tn(Bg, (xd_g * dec_full[:, sl]).astype(BF16))
            xd_b = xd_g.astype(BF16)
            ydiag = jnp.zeros((Q, GROUP_W), F32)
            for r in range(HEADS_PER_GROUP):
                Lm = _decay_matrix(cs, csT, g * HEADS_PER_GROUP + r, causal)
                Gm = (CB * Lm).astype(BF16)
                ydiag = ydiag + _dot_nn(Gm, jnp.where(lane_head == r, xd_b, jnp.zeros_like(xd_b)))
            hs_ref[0, g] = hg
            h_scr[g] = hg * cd_full[:, sl] + S
            ypre = ydiag + yoff + xs[:, sl] * df_ref[:, sl]
            ypre_ref[:, sl] = ypre
            zg = z_ref[:, sl]
            yz = ypre * zg * _sigmoid(zg)
            rn = lax.rsqrt(jnp.mean(yz * yz, axis=-1, keepdims=True) + EPS)
            y_ref[:, sl] = (yz * rn * ng_ref[:, sl]).astype(BF16)

    return pl.pallas_call(
        body, grid=(nc,), in_specs=_ssd_in_specs(nc, False),
        out_specs=[pl.BlockSpec((CHUNK, D_INNER), lambda c: (c, 0)),
                   pl.BlockSpec((CHUNK, D_INNER), lambda c: (c, 0)),
                   pl.BlockSpec((1, N_GROUPS, D_STATE, GROUP_W), lambda c: (c, 0, 0, 0))],
        out_shape=[jax.ShapeDtypeStruct((T, D_INNER), BF16), jax.ShapeDtypeStruct((T, D_INNER), F32),
                   jax.ShapeDtypeStruct((nc, N_GROUPS, D_STATE, GROUP_W), F32)],
        scratch_shapes=[pltpu.VMEM((N_GROUPS, D_STATE, GROUP_W), F32)],
        name=name, compiler_params=_cparams(("arbitrary",)),
    )(xbc, xbc, xbc, proj, proj, dtT, dtb, dtbT, alog, alogT, dfull, ng)


def _ssd_bwd(xbc, proj, dtT, dtb, dtbT, alog, alogT, dfull, ng, ypre, hs, dy, name):
    T = xbc.shape[0]
    nc = T // CHUNK
    Q = CHUNK

    def body(xs_ref, B_ref, C_ref, z_ref, dt_ref, dtT_ref, dtb_ref, dtbT_ref, al_ref, alT_ref, df_ref, ng_ref,
             ypre_ref, hs_ref, dy_ref,
             dz_ref, dxbc_ref, ddt_ref, ddtb_ref, dal_ref, dD_ref, dng_ref, dh_scr):
        step = pl.program_id(0)

        @pl.when(step == 0)
        def _():
            dh_scr[...] = jnp.zeros_like(dh_scr)
            ddtb_ref[...] = jnp.zeros_like(ddtb_ref)
            dal_ref[...] = jnp.zeros_like(dal_ref)
            dD_ref[...] = jnp.zeros_like(dD_ref)
            dng_ref[...] = jnp.zeros_like(dng_ref)

        xdt, dt, A, cs, csT, causal, triu = _ssd_pre(dt_ref[:, :N_HEADS], dtT_ref[...], dtb_ref[...],
                                                    dtbT_ref[...], al_ref[...], alT_ref[...])
        ex = _expand_mat()
        dt_full = _hdot(dt, ex)
        cs_full = _hdot(cs, ex)
        cs_last = cs_full[Q - 1:Q, :]
        xs = xs_ref[...]
        xd = xs * dt_full
        e_full = jnp.exp(cs_full)
        dec_full = jnp.exp(cs_last - cs_full)
        cd_full = jnp.exp(cs_last)
        lane_head = lax.broadcasted_iota(jnp.int32, (1, GROUP_W), 1) // HEAD_DIM
        is_last = lax.broadcasted_iota(jnp.int32, (Q, 1), 0) == Q - 1
        dcs16 = jnp.zeros((Q, N_HEADS), F32)
        ddtx16 = jnp.zeros((Q, N_HEADS), F32)
        dD16 = jnp.zeros((8, N_HEADS), F32)
        lane16 = lax.broadcasted_iota(jnp.int32, (1, N_HEADS), 1)
        sub16 = lax.broadcasted_iota(jnp.int32, (N_HEADS, 1), 0)
        col_sums = jnp.zeros((N_HEADS, Q), F32)
        for g in range(N_GROUPS):
            sl = slice(g * GROUP_W, (g + 1) * GROUP_W)
            red = _reduce_mat(g)
            ypre_g = ypre_ref[:, sl]
            zg = z_ref[:, sl]
            sg = _sigmoid(zg)
            silu = zg * sg
            yz = ypre_g * silu
            rn = lax.rsqrt(jnp.mean(yz * yz, axis=-1, keepdims=True) + EPS)
            yh = yz * rn
            dy_g = dy_ref[:, sl]
            dng_ref[:, sl] += jnp.sum(dy_g * yh, axis=0, keepdims=True)
            dyh = dy_g * ng_ref[:, sl]
            dyz = rn * (dyh - yh * jnp.mean(dyh * yh, axis=-1, keepdims=True))
            dY = dyz * silu
            dz_ref[:, sl] = (dyz * ypre_g * sg * (1.0 + zg * (1.0 - sg))).astype(BF16)
            xs_g = xs[:, sl]
            xd_g = xd[:, sl]
            dec_g = dec_full[:, sl]
            cd_g = cd_full[:, sl]
            d_g = df_ref[:, sl]
            Bg = B_ref[:, g * D_STATE:(g + 1) * D_STATE].astype(BF16)
            Cg = C_ref[:, g * D_STATE:(g + 1) * D_STATE].astype(BF16)
            CB = _dot_nt(Cg, Bg)
            hg = hs_ref[0, g]
            hgb = hg.astype(BF16)
            yoff = _dot_nn(Cg, hgb) * e_full[:, sl]
            dhn = dh_scr[g]
            dhnb = dhn.astype(BF16)
            dYE = (dY * e_full[:, sl]).astype(BF16)
            dC = _dot_nt(dYE, hgb)
            dh_direct = _dot_tn(Cg, dYE)
            dXdd = _dot_nn(Bg, dhnb)
            dB = _dot_nt((xd_g * dec_g).astype(BF16), dhnb)
            dcd = jnp.sum(dhn * hg, axis=0, keepdims=True)
            dh_scr[g] = dh_direct + cd_g * dhn
            dYb = dY.astype(BF16)
            xd_b = xd_g.astype(BF16)
            dCB = jnp.zeros((Q, Q), F32)
            dXd = dXdd * dec_g
            for r in range(HEADS_PER_GROUP):
                h = g * HEADS_PER_GROUP + r
                Lm = _decay_matrix(cs, csT, h, causal)
                Gf = CB * Lm
                dYr = jnp.where(lane_head == r, dYb, jnp.zeros_like(dYb))
                dG = _dot_nt(dYr, xd_b)
                dCB = dCB + dG * Lm
                dXd = dXd + _dot_tn(Gf.astype(BF16), dYr)
                Mm = dG * Gf
                dcs16 = dcs16 + jnp.where(lane16 == h, jnp.sum(Mm, axis=1, keepdims=True), 0.0)
                col_sums = col_sums + jnp.where(sub16 == h, jnp.sum(Mm, axis=0, keepdims=True), 0.0)
            dCBb = dCB.astype(BF16)
            dC = dC + _dot_nn(dCBb, Bg)
            dB = dB + _dot_tn(dCBb, Cg)
            w_state = dXdd * dec_g * xd_g
            t_last = jnp.sum(w_state, axis=0, keepdims=True) + dcd * cd_g
            dcs_g = dY * yoff - w_state + jnp.where(is_last, t_last, 0.0)
            dcs16 = dcs16 + _hdot(dcs_g, red)
            ddtx16 = ddtx16 + _hdot(dXd * xs_g, red)
            dD16 = dD16 + _hdot(jnp.broadcast_to(jnp.sum(dY * xs_g, axis=0, keepdims=True), (8, GROUP_W)), red)
            dxbc_ref[:, sl] = dXd * dt_full[:, sl] + dY * d_g
            dxbc_ref[:, D_INNER + g * D_STATE:D_INNER + (g + 1) * D_STATE] = dB
            dxbc_ref[:, D_INNER + 512 + g * D_STATE:D_INNER + 512 + (g + 1) * D_STATE] = dC
        eye = (lax.broadcasted_iota(jnp.int32, (N_HEADS, N_HEADS), 0)
               == lax.broadcasted_iota(jnp.int32, (N_HEADS, N_HEADS), 1)).astype(F32)
        dcs16 = dcs16 - lax.dot_general(col_sums, eye, (((0,), (0,)), ((), ())), precision=HI,
                                        preferred_element_type=F32)
        da = _hdot(triu, dcs16)
        ddt = da * A + ddtx16
        ddt_raw = ddt * _sigmoid(xdt)
        pr = lax.broadcasted_iota(jnp.int32, (N_HEADS, 128), 0)
        pc = lax.broadcasted_iota(jnp.int32, (N_HEADS, 128), 1)
        ddt_ref[...] = _hdot(ddt_raw, (pr == pc).astype(F32))
        ddtb_ref[...] += jnp.sum(ddt_raw, axis=0, keepdims=True)
        dal_ref[...] += jnp.sum(da * dt, axis=0, keepdims=True) * A
        dD_ref[...] += dD16[0:1, :]

    def rc(c):
        return nc - 1 - c

    in_specs = _ssd_in_specs(nc, True) + [
        pl.BlockSpec((CHUNK, D_INNER), lambda c: (rc(c), 0)),
        pl.BlockSpec((1, N_GROUPS, D_STATE, GROUP_W), lambda c: (rc(c), 0, 0, 0)),
        pl.BlockSpec((CHUNK, D_INNER), lambda c: (rc(c), 0)),
    ]
    small = pl.BlockSpec((1, N_HEADS), lambda c: (0, 0))
    return pl.pallas_call(
        body, grid=(nc,), in_specs=in_specs,
        out_specs=[pl.BlockSpec((CHUNK, D_INNER), lambda c: (rc(c), 0)),
                   pl.BlockSpec((CHUNK, CONV_DIM), lambda c: (rc(c), 0)),
                   pl.BlockSpec((CHUNK, 128), lambda c: (rc(c), 0)),
                   small, small, small,
                   pl.BlockSpec((1, D_INNER), lambda c: (0, 0))],
        out_shape=[jax.ShapeDtypeStruct((T, D_INNER), BF16), jax.ShapeDtypeStruct((T, CONV_DIM), F32),
                   jax.ShapeDtypeStruct((T, 128), F32),
                   jax.ShapeDtypeStruct((1, N_HEADS), F32), jax.ShapeDtypeStruct((1, N_HEADS), F32),
                   jax.ShapeDtypeStruct((1, N_HEADS), F32), jax.ShapeDtypeStruct((1, D_INNER), F32)],
        scratch_shapes=[pltpu.VMEM((N_GROUPS, D_STATE, GROUP_W), F32)],
        name=name, compiler_params=_cparams(("arbitrary",)),
    )(xbc, xbc, xbc, proj, proj, dtT, dtb, dtbT, alog, alogT, dfull, ng, ypre, hs, dy)


N_PAIRS = ATTN_W // 128
PAIRS_PER_KV = N_PAIRS // 2
ATTN_SCALE = HEAD_DIM ** -0.5


def _kv_variants(kk):
    lo = lax.broadcasted_iota(jnp.int32, kk.shape, 1) < HEAD_DIM
    zero = jnp.zeros_like(kk)
    k00 = jnp.where(lo, kk, zero)
    k11 = jnp.where(lo, zero, kk)
    k01 = pltpu.roll(k00, HEAD_DIM, axis=1)
    k10 = pltpu.roll(k11, HEAD_DIM, axis=1)
    return [[k00.astype(BF16), k01.astype(BF16)], [k10.astype(BF16), k11.astype(BF16)]]


def _attn_valid(n):
    i = lax.broadcasted_iota(jnp.int32, (WINDOW, 2 * WINDOW), 0)
    j = lax.broadcasted_iota(jnp.int32, (WINDOW, 2 * WINDOW), 1)
    return (j > i) & (j <= i + WINDOW) & (n * WINDOW + j >= WINDOW)


def _attn_probs(qp, kvar, valid, sk):
    s = _dot_nt(qp, kvar) * ATTN_SCALE
    s = jnp.where(valid, s, NEG)
    m = jnp.maximum(jnp.max(s, axis=1, keepdims=True), sk)
    pe = jnp.exp(s - m)
    es = jnp.exp(sk - m)
    den = jnp.sum(pe, axis=1, keepdims=True) + es
    inv = 1.0 / den
    return pe * inv, es * inv


def _sink(sinks, r):
    lane = lax.broadcasted_iota(jnp.int32, sinks.shape, 1)
    return jnp.sum(jnp.where(lane == r, sinks, 0.0), axis=1, keepdims=True)


def _attn_fwd(proj, kpad, vpad, sinks, og, name):
    T = proj.shape[0]
    nb = T // WINDOW

    def body(q_ref, k_ref, v_ref, s_ref, og_ref, y_ref, o_ref):
        n = pl.program_id(0)
        start = pl.multiple_of(n * WINDOW, WINDOW)
        kv = _kv_variants(k_ref[pl.ds(start, 2 * WINDOW), :])
        vv = _kv_variants(v_ref[pl.ds(start, 2 * WINDOW), :])
        valid = _attn_valid(n)
        sinks_v = s_ref[...]
        ssq = jnp.zeros((WINDOW, 1), F32)
        for p in range(N_PAIRS):
            j = p // PAIRS_PER_KV
            qp = q_ref[:, p * 128:(p + 1) * 128].astype(BF16)
            o_pair = jnp.zeros((WINDOW, 128), F32)
            for par in range(2):
                pn, _ = _attn_probs(qp, kv[j][par], valid, _sink(sinks_v, 2 * p + par))
                o_pair = o_pair + _dot_nn(pn.astype(BF16), vv[j][par])
            o_ref[:, p * 128:(p + 1) * 128] = o_pair
            ssq = ssq + jnp.sum(o_pair * o_pair, axis=1, keepdims=True)
        rn = lax.rsqrt(ssq * (1.0 / ATTN_W) + EPS)
        y_ref[...] = (o_ref[...] * rn * og_ref[...]).astype(BF16)

    full_kv = pl.BlockSpec((T + WINDOW, KV_W), lambda n: (0, 0))
    return pl.pallas_call(
        body, grid=(nb,),
        in_specs=[pl.BlockSpec((WINDOW, ATTN_W), lambda n: (n, OFF_Q // ATTN_W)), full_kv, full_kv,
                  pl.BlockSpec((1, N_HEADS), lambda n: (0, 0)), pl.BlockSpec((1, ATTN_W), lambda n: (0, 0))],
        out_specs=[pl.BlockSpec((WINDOW, ATTN_W), lambda n: (n, 0)), pl.BlockSpec((WINDOW, ATTN_W), lambda n: (n, 0))],
        out_shape=[jax.ShapeDtypeStruct((T, ATTN_W), BF16), jax.ShapeDtypeStruct((T, ATTN_W), F32)],
        name=name, compiler_params=_cparams(("parallel",)),
    )(proj, kpad, vpad, sinks, og)


def _attn_bwd(proj, kpad, vpad, sinks, og, o, dy, name):
    T = proj.shape[0]
    nb = T // WINDOW

    def body(q_ref, k_ref, v_ref, s_ref, og_ref, o_ref, dy_ref, dq_ref, dk_ref, dv_ref, ds_ref, dog_ref):
        n = pl.program_id(0)

        @pl.when(n == 0)
        def _():
            dk_ref[...] = jnp.zeros_like(dk_ref)
            dv_ref[...] = jnp.zeros_like(dv_ref)
            ds_ref[...] = jnp.zeros_like(ds_ref)
            dog_ref[...] = jnp.zeros_like(dog_ref)

        start = pl.multiple_of(n * WINDOW, WINDOW)
        kv = _kv_variants(k_ref[pl.ds(start, 2 * WINDOW), :])
        vv = _kv_variants(v_ref[pl.ds(start, 2 * WINDOW), :])
        valid = _attn_valid(n)
        sinks_v = s_ref[...]
        of = o_ref[...]
        rn = lax.rsqrt(jnp.mean(of * of, axis=-1, keepdims=True) + EPS)
        oh = of * rn
        dyf = dy_ref[...]
        dog_ref[...] += jnp.sum(dyf * oh, axis=0, keepdims=True)
        doh = dyf * og_ref[...]
        do = rn * (doh - oh * jnp.mean(doh * oh, axis=-1, keepdims=True))
        lane = lax.broadcasted_iota(jnp.int32, (1, 128), 1)
        lane16 = lax.broadcasted_iota(jnp.int32, (1, N_HEADS), 1)
        dk_acc = [[jnp.zeros((2 * WINDOW, 128), F32) for _ in range(2)] for _ in range(2)]
        dv_acc = [[jnp.zeros((2 * WINDOW, 128), F32) for _ in range(2)] for _ in range(2)]
        dsink = jnp.zeros((1, N_HEADS), F32)
        for p in range(N_PAIRS):
            j = p // PAIRS_PER_KV
            qp = q_ref[:, p * 128:(p + 1) * 128].astype(BF16)
            do_p = do[:, p * 128:(p + 1) * 128]
            o_p = of[:, p * 128:(p + 1) * 128]
            do_b = do_p.astype(BF16)
            prod = do_p * o_p
            dq_pair = jnp.zeros((WINDOW, 128), F32)
            for par in range(2):
                r = 2 * p + par
                half = (lane < HEAD_DIM) if par == 0 else (lane >= HEAD_DIM)
                pn, ps = _attn_probs(qp, kv[j][par], valid, _sink(sinks_v, r))
                delta = jnp.sum(jnp.where(half, prod, 0.0), axis=1, keepdims=True)
                dP = _dot_nt(do_b, vv[j][par])
                dS = pn * (dP - delta)
                dsink = dsink + jnp.where(lane16 == r, -jnp.sum(ps * delta, axis=0, keepdims=True), 0.0)
                dSb = (dS * ATTN_SCALE).astype(BF16)
                dq_pair = dq_pair + _dot_nn(dSb, kv[j][par])
                dk_acc[j][par] = dk_acc[j][par] + _dot_tn(dSb, jnp.where(half, qp, jnp.zeros_like(qp)))
                dv_acc[j][par] = dv_acc[j][par] + _dot_tn(pn.astype(BF16), jnp.where(half, do_b, jnp.zeros_like(do_b)))
            dq_ref[:, p * 128:(p + 1) * 128] = dq_pair.astype(BF16)
        dkk = (dk_acc[0][0] + pltpu.roll(dk_acc[0][1], HEAD_DIM, axis=1)
               + dk_acc[1][1] + pltpu.roll(dk_acc[1][0], HEAD_DIM, axis=1))
        dvv = (dv_acc[0][0] + pltpu.roll(dv_acc[0][1], HEAD_DIM, axis=1)
               + dv_acc[1][1] + pltpu.roll(dv_acc[1][0], HEAD_DIM, axis=1))
        dk_ref[pl.ds(start, 2 * WINDOW), :] += dkk
        dv_ref[pl.ds(start, 2 * WINDOW), :] += dvv
        ds_ref[...] += dsink

    full_kv = pl.BlockSpec((T + WINDOW, KV_W), lambda n: (0, 0))
    blk = pl.BlockSpec((WINDOW, ATTN_W), lambda n: (n, 0))
    return pl.pallas_call(
        body, grid=(nb,),
        in_specs=[pl.BlockSpec((WINDOW, ATTN_W), lambda n: (n, OFF_Q // ATTN_W)), full_kv, full_kv,
                  pl.BlockSpec((1, N_HEADS), lambda n: (0, 0)), pl.BlockSpec((1, ATTN_W), lambda n: (0, 0)),
                  blk, pl.BlockSpec((WINDOW, ATTN_W), lambda n: (n, 1))],
        out_specs=[blk, full_kv, full_kv, pl.BlockSpec((1, N_HEADS), lambda n: (0, 0)),
                   pl.BlockSpec((1, ATTN_W), lambda n: (0, 0))],
        out_shape=[jax.ShapeDtypeStruct((T, ATTN_W), BF16), jax.ShapeDtypeStruct((T + WINDOW, KV_W), F32),
                   jax.ShapeDtypeStruct((T + WINDOW, KV_W), F32), jax.ShapeDtypeStruct((1, N_HEADS), F32),
                   jax.ShapeDtypeStruct((1, ATTN_W), F32)],
        name=name, compiler_params=_cparams(("arbitrary",)),
    )(proj, kpad, vpad, sinks, og, o, dy)


ANY = pl.BlockSpec(memory_space=pl.ANY)


def _coords():
    return lax.axis_index("x"), lax.axis_index("y"), lax.axis_index("c")


def _all_gather(arrs, name):
    n = len(arrs)

    def body(*refs):
        ins, outs = refs[:n], refs[n:2 * n]
        send_sems, recv_sems, local_sems = refs[2 * n:]
        x, y, c = _coords()
        me = 4 * x + 2 * y + c
        sibling = (x, y, 1 - c)
        chips = [(1 - x, y), (x, 1 - y), (1 - x, 1 - y)]

        def copy(a, k, block, to, src=None):
            dst = outs[a].at[block]
            return pltpu.make_async_remote_copy(
                src_ref=dst if src is None else src, dst_ref=dst, send_sem=send_sems.at[a, k],
                recv_sem=recv_sems.at[a, k], device_id=to, device_id_type=MESH)

        mine = [pltpu.make_async_copy(ins[a], outs[a].at[me], local_sems.at[a]) for a in range(n)]
        for cp in mine:
            cp.start()
        first = []
        for a in range(n):
            first.append(copy(a, 0, me, sibling, src=ins[a]))
            for j, chip in enumerate(chips):
                first.append(copy(a, 1 + j, me, (*chip, c), src=ins[a]))
        for cp in first:
            cp.start()
        passed = []
        for j, (px, py) in enumerate(chips):
            blk = 4 * px + 2 * py + c
            for a in range(n):
                copy(a, 1 + j, blk, sibling).wait_recv()
                fwd = copy(a, 4 + j, blk, sibling)
                fwd.start()
                passed.append(fwd)
        for a in range(n):
            copy(a, 0, 4 * x + 2 * y + (1 - c), sibling).wait_recv()
            for j, (px, py) in enumerate(chips):
                copy(a, 4 + j, 4 * px + 2 * py + (1 - c), sibling).wait_recv()
        for cp in first + passed:
            cp.wait_send()
        for cp in mine:
            cp.wait()

    return pl.pallas_call(
        body, in_specs=[ANY] * n, out_specs=[ANY] * n,
        out_shape=[jax.ShapeDtypeStruct((N_DEV,) + a.shape, a.dtype) for a in arrs],
        scratch_shapes=[pltpu.SemaphoreType.DMA((n, 7)), pltpu.SemaphoreType.DMA((n, 7)),
                        pltpu.SemaphoreType.DMA((n,))],
        name=name,
    )(*arrs)


def _exchange_pair(arrs, name):
    n = len(arrs)

    def body(*refs):
        ins, outs = refs[:n], refs[n:2 * n]
        send_sems, recv_sems = refs[2 * n:]
        x, y, c = _coords()
        cps = []
        for a in range(n):
            for q in range(4):
                cps.append(pltpu.make_async_remote_copy(
                    src_ref=ins[a].at[2 * q + (1 - c)], dst_ref=outs[a].at[q], send_sem=send_sems.at[a, q],
                    recv_sem=recv_sems.at[a, q], device_id=(x, y, 1 - c), device_id_type=MESH))
        for cp in cps:
            cp.start()
        for cp in cps:
            cp.wait()

    return pl.pallas_call(
        body, in_specs=[ANY] * n, out_specs=[ANY] * n,
        out_shape=[jax.ShapeDtypeStruct((4,) + a.shape[1:], a.dtype) for a in arrs],
        scratch_shapes=[pltpu.SemaphoreType.DMA((n, 4)), pltpu.SemaphoreType.DMA((n, 4))],
        name=name,
    )(*arrs)


def _exchange_chips(arrs, name):
    n = len(arrs)

    def body(*refs):
        ins, outs = refs[:n], refs[n:2 * n]
        send_sems, recv_sems = refs[2 * n:]
        x, y, c = _coords()
        chips = [(1 - x, y), (x, 1 - y), (1 - x, 1 - y)]
        cps = []
        for a in range(n):
            for k, (tx, ty) in enumerate(chips):
                cps.append(pltpu.make_async_remote_copy(
                    src_ref=ins[a].at[2 * tx + ty], dst_ref=outs[a].at[k], send_sem=send_sems.at[a, k],
                    recv_sem=recv_sems.at[a, k], device_id=(tx, ty, c), device_id_type=MESH))
        for cp in cps:
            cp.start()
        for cp in cps:
            cp.wait()

    return pl.pallas_call(
        body, in_specs=[ANY] * n, out_specs=[ANY] * n,
        out_shape=[jax.ShapeDtypeStruct((3,) + a.shape[1:], a.dtype) for a in arrs],
        scratch_shapes=[pltpu.SemaphoreType.DMA((n, 3)), pltpu.SemaphoreType.DMA((n, 3))],
        name=name,
    )(*arrs)


def _pair_add(g8, r1, csel, tr, name):
    _, R, C = r1.shape
    g4 = g8.reshape(4, 2, R, C)

    def body(c_ref, g_ref, r_ref, o_ref):
        o_ref[...] = (g_ref[...].astype(F32) + r_ref[...].astype(F32)).astype(BF16)

    return pl.pallas_call(
        body,
        grid_spec=pltpu.PrefetchScalarGridSpec(
            num_scalar_prefetch=1, grid=(4, R // tr),
            in_specs=[pl.BlockSpec((None, None, tr, C), lambda q, i, cs: (q, cs[0], i, 0)),
                      pl.BlockSpec((None, tr, C), lambda q, i, cs: (q, i, 0))],
            out_specs=pl.BlockSpec((None, tr, C), lambda q, i, cs: (q, i, 0))),
        out_shape=jax.ShapeDtypeStruct((4, R, C), BF16), name=name,
        compiler_params=_cparams(("parallel", "parallel")),
    )(csel, g4, r1)


def _adamw_math(w, g, m, v):
    m = ADAM_B1 * m + (1.0 - ADAM_B1) * g
    v = ADAM_B2 * v + (1.0 - ADAM_B2) * (g * g)
    m_hat = m / (1.0 - ADAM_B1 ** ADAM_STEP)
    v_hat = v / (1.0 - ADAM_B2 ** ADAM_STEP)
    delta = -ADAM_LR * (m_hat / (jnp.sqrt(v_hat) + ADAM_EPS) + ADAM_WD * w)
    return delta, m, v


def _adamw_big(w, m, v, p4, r3, qsel, tr, name):
    R, C = w.shape

    def body(q_ref, w_ref, m_ref, v_ref, p_ref, r_ref, g_out, d_out, m_out, v_out):
        g = p_ref[...].astype(F32) + r_ref[0].astype(F32) + r_ref[1].astype(F32) + r_ref[2].astype(F32)
        d, mn, vn = _adamw_math(w_ref[...], g, m_ref[...], v_ref[...])
        g_out[...] = g
        d_out[...] = d
        m_out[...] = mn
        v_out[...] = vn

    blk = pl.BlockSpec((tr, C), lambda i, qs: (i, 0))
    return pl.pallas_call(
        body,
        grid_spec=pltpu.PrefetchScalarGridSpec(
            num_scalar_prefetch=1, grid=(R // tr,),
            in_specs=[blk, blk, blk, pl.BlockSpec((None, tr, C), lambda i, qs: (qs[0], i, 0)),
                      pl.BlockSpec((3, tr, C), lambda i, qs: (0, i, 0))],
            out_specs=[blk, blk, blk, blk]),
        out_shape=[jax.ShapeDtypeStruct((R, C), F32)] * 4, name=name,
        compiler_params=_cparams(("parallel",)),
    )(qsel, w, m, v, p4, r3)


def _small_sum(parts, name):
    def body(p_ref, o_ref):
        acc = p_ref[0]
        for d in range(1, N_DEV):
            acc = acc + p_ref[d]
        o_ref[...] = acc

    return pl.pallas_call(
        body, out_shape=jax.ShapeDtypeStruct(parts.shape[1:], F32), name=name,
        compiler_params=_cparams(),
    )(parts)


def _adamw_small(w, g, m, v, name):
    def body(w_ref, g_ref, m_ref, v_ref, d_out, m_out, v_out):
        d, mn, vn = _adamw_math(w_ref[...], g_ref[...], m_ref[...], v_ref[...])
        d_out[...] = d
        m_out[...] = mn
        v_out[...] = vn

    return pl.pallas_call(
        body, out_shape=[jax.ShapeDtypeStruct(w.shape, F32)] * 3, name=name, compiler_params=_cparams(),
    )(w, g, m, v)


def _row(*pieces):
    r = jnp.concatenate([p.reshape(1, -1) for p in pieces], axis=1)
    return jnp.pad(r, ((0, 0), (0, D_MODEL - r.shape[1])))


def _pack_small(mix, convb, ssmg, attng, mlpg, fing, convw, dtb, alog, dsk, sinks, extra=None):
    last = [dtb, alog, dsk, sinks] + ([extra] if extra is not None else [])
    rows = [_row(mix), _row(convb), _row(ssmg, attng), _row(mlpg), _row(fing),
            jnp.pad(convw, ((0, 0), (0, D_MODEL - convw.shape[1]))), _row(*last)]
    packed = jnp.concatenate(rows, axis=0)
    return jnp.pad(packed, ((0, SMALL_ROWS - packed.shape[0]), (0, 0)))


def _unpack_small(p, conv_n):
    return dict(
        mix_norm_g=p[0:1, :], conv_b=p[1:2, :], ssm_norm_g=p[2:3, :D_INNER], attn_out_norm_g=p[2:3, D_INNER:],
        mlp_norm_g=p[3:4, :], final_norm_g=p[4, :], conv_w=p[5:9, :conv_n][None],
        dt_bias=p[9:10, 0:16], A_log=p[9:10, 16:32], D_skip=p[9:10, 32:48], attn_sinks=p[9:10, 48:64])


SMALL_NAMES = ["mix_norm_g", "conv_w", "conv_b", "dt_bias", "A_log", "D_skip", "ssm_norm_g", "attn_sinks",
               "attn_out_norm_g", "mlp_norm_g", "final_norm_g"]
WEIGHT_ORDER = ["mix_norm_g", "w_in", "conv_w", "conv_b", "dt_bias", "A_log", "D_skip", "ssm_norm_g", "attn_sinks",
                "attn_out_norm_g", "w_out", "mlp_norm_g", "w_up", "w_down", "final_norm_g"]


def _to_my_columns(w_nat):
    pad = jnp.zeros((w_nat.shape[0], NP - IN_PROJ), w_nat.dtype)
    return jnp.concatenate([w_nat[:, :NAT_DT], w_nat[:, NAT_DT + N_HEADS:], w_nat[:, NAT_DT:NAT_DT + N_HEADS], pad],
                           axis=1)


def _to_natural_columns(w_my):
    return jnp.concatenate([w_my[:, :NAT_DT], w_my[:, OFF_DT:OFF_DT + N_HEADS], w_my[:, NAT_DT:OFF_DT]], axis=1)


def _local_step(x, tgt, p, w_in_p, w_out_f, w_up_s, w_down_f, conv_w_f):
    T = x.shape[0]
    D = D_MODEL
    h1 = _rmsnorm_fwd(x, p["mix_norm_g"], "norm_mix")
    (proj,) = _mm_simple(h1, w_in_p, mode="nn", M=T, N=NP, K=D, tm=min(T, 1024), tn=768, tk=D, out_dtype=F32,
                         name="in_proj")
    xbc = _conv_fwd(proj, conv_w_f, p["conv_b"], "conv_fwd")
    dtT = proj[:, OFF_DT:OFF_DT + N_HEADS].T
    dtbT = p["dt_bias"].T
    alogT = p["A_log"].T
    dfull = jnp.repeat(p["D_skip"], HEAD_DIM, axis=1)
    y_ssm, ypre, hs = _ssd_fwd(xbc, proj, dtT, p["dt_bias"], dtbT, p["A_log"], alogT, dfull, p["ssm_norm_g"],
                               "ssd_fwd")
    kpad = jnp.pad(proj[:, OFF_K:OFF_K + KV_W], ((WINDOW, 0), (0, 0)))
    vpad = jnp.pad(proj[:, OFF_V:OFF_V + KV_W], ((WINDOW, 0), (0, 0)))
    y_att, o_att = _attn_fwd(proj, kpad, vpad, p["attn_sinks"], p["attn_out_norm_g"], "attn_fwd")
    ycat = jnp.concatenate([y_ssm, y_att], axis=1)
    tm = min(T, 1024)
    (x2,) = _mm_simple(ycat, w_out_f, mode="nn", M=T, N=D, K=D, tm=tm, tn=1024, tk=D, out_dtype=F32, name="out_proj",
                       extras=(x,), epilogue=lambda acc, res: (acc + res,))
    h2 = _rmsnorm_fwd(x2, p["mlp_norm_g"], "norm_mlp")
    grid = (T // tm, N_DEV, 1)
    u, act = _matmul(
        h2, w_up_s, mode="nn", grid=grid,
        a_spec=pl.BlockSpec((tm, D), lambda i, j, k: (i, 0)),
        b_spec=pl.BlockSpec((None, D, 1024), lambda i, j, k: (j, 0, 0)),
        out_shapes=[jax.ShapeDtypeStruct((T, D_FF), F32), jax.ShapeDtypeStruct((T, D_FF), BF16)],
        out_specs=[pl.BlockSpec((tm, 1024), lambda i, j, k: (i, j))] * 2, tile=(tm, 1024), name="mlp_up",
        epilogue=lambda acc: (acc, jnp.square(jnp.maximum(acc, 0.0))))
    (x3,) = _mm_simple(act, w_down_f, mode="nn", M=T, N=D, K=D_FF, tm=tm, tn=1024, tk=1024, out_dtype=F32,
                       name="mlp_down", extras=(x2,), epilogue=lambda acc, res: (acc + res,))
    loss_part, d_fin, dx3, dx3b = _final_loss(x3, tgt, p["final_norm_g"].reshape(1, D), "loss_head")
    (g_down,) = _mm_simple(act, dx3b, mode="tn", M=D_FF, N=D, K=T, tm=1024, tn=1024, tk=T, out_dtype=BF16,
                           name="grad_w_down")
    (du,) = _mm_simple(dx3b, w_down_f, mode="nt", M=T, N=D_FF, K=D, tm=tm, tn=1024, tk=D, out_dtype=BF16,
                       name="mlp_down_bwd", extras=(u,),
                       epilogue=lambda acc, uu: (acc * (2.0 * jnp.maximum(uu, 0.0)),))
    (g_up,) = _matmul(
        h2, du, mode="tn", grid=(D // 1024, N_DEV, 1),
        a_spec=pl.BlockSpec((T, 1024), lambda i, j, k: (0, i)),
        b_spec=pl.BlockSpec((T, 1024), lambda i, j, k: (0, j)),
        out_shapes=[jax.ShapeDtypeStruct((N_DEV, D, 1024), BF16)],
        out_specs=[pl.BlockSpec((None, 1024, 1024), lambda i, j, k: (j, i, 0))], tile=(1024, 1024), name="grad_w_up")
    (dh2,) = _matmul(
        du, w_up_s, mode="nt", grid=(T // tm, D // 1024, N_DEV),
        a_spec=pl.BlockSpec((tm, 1024), lambda i, j, k: (i, k)),
        b_spec=pl.BlockSpec((None, 1024, 1024), lambda i, j, k: (k, j, 0)),
        out_shapes=[jax.ShapeDtypeStruct((T, D), F32)],
        out_specs=[pl.BlockSpec((tm, 1024), lambda i, j, k: (i, j))], tile=(tm, 1024), name="mlp_up_bwd")
    dx2, dx2b, d_mlp = _rmsnorm_bwd(dh2, x2, p["mlp_norm_g"], dx3, "norm_mlp_bwd")
    (g_out,) = _mm_simple(ycat, dx2b, mode="tn", M=D, N=D, K=T, tm=1024, tn=1024, tk=T, out_dtype=BF16,
                          name="grad_w_out")
    (dy,) = _mm_simple(dx2b, w_out_f, mode="nt", M=T, N=D, K=D, tm=tm, tn=1024, tk=D, out_dtype=F32,
                       name="out_proj_bwd")
    dz, dxbc_act, ddt, d_dtb, d_alog, d_dskip, d_ssmg = _ssd_bwd(
        xbc, proj, dtT, p["dt_bias"], dtbT, p["A_log"], alogT, dfull, p["ssm_norm_g"], ypre, hs, dy, "ssd_bwd")
    dq, dkpad, dvpad, d_sinks, d_attng = _attn_bwd(proj, kpad, vpad, p["attn_sinks"], p["attn_out_norm_g"], o_att, dy,
                                                   "attn_bwd")
    dxbc, d_convw, d_convb = _conv_bwd(proj, dxbc_act, conv_w_f, p["conv_b"], "conv_bwd")
    dproj = jnp.concatenate(
        [dz, dxbc, dq, dkpad[WINDOW:].astype(BF16), dvpad[WINDOW:].astype(BF16), ddt.astype(BF16),
         jnp.zeros((T, NP - OFF_DT - 128), BF16)], axis=1)
    (g_in,) = _mm_simple(h1, dproj, mode="tn", M=D, N=NP, K=T, tm=1024, tn=768, tk=T, out_dtype=BF16,
                         name="grad_w_in")
    (dh1,) = _mm_simple(dproj, w_in_p, mode="nt", M=T, N=D, K=NP, tm=tm, tn=1024, tk=768, out_dtype=F32,
                        name="in_proj_bwd")
    dx, _, d_mix = _rmsnorm_bwd(dh1, x, p["mix_norm_g"], dx2, "norm_mix_bwd")
    small = _pack_small(d_mix, d_convb, d_ssmg, d_attng, d_mlp, d_fin, d_convw, d_dtb, d_alog, d_dskip, d_sinks,
                        extra=loss_part[:, 0:1])
    return dx, g_in, g_out, g_up, g_down, small


def kernel(x, mix_norm_g, w_in, conv_w, conv_b, dt_bias, A_log, D_skip, ssm_norm_g, attn_sinks, attn_out_norm_g, w_out, mlp_norm_g, w_up, w_down, final_norm_g, loss_target, m_mix_norm_g, m_w_in, m_conv_w, m_conv_b, m_dt_bias, m_A_log, m_D_skip, m_ssm_norm_g, m_attn_sinks, m_attn_out_norm_g, m_w_out, m_mlp_norm_g, m_w_up, m_w_down, m_final_norm_g, v_mix_norm_g, v_w_in, v_conv_w, v_conv_b, v_dt_bias, v_A_log, v_D_skip, v_ssm_norm_g, v_attn_sinks, v_attn_out_norm_g, v_w_out, v_mlp_norm_g, v_w_up, v_w_down, v_final_norm_g):
    xi, yi, ci = _coords()
    me = 4 * xi + 2 * yi + ci
    csel = jnp.reshape(ci, (1,)).astype(jnp.int32)
    qsel = jnp.reshape(2 * xi + yi, (1,)).astype(jnp.int32)
    w = dict(mix_norm_g=mix_norm_g, conv_b=conv_b, dt_bias=dt_bias, A_log=A_log, D_skip=D_skip,
             ssm_norm_g=ssm_norm_g, attn_sinks=attn_sinks, attn_out_norm_g=attn_out_norm_g, mlp_norm_g=mlp_norm_g,
             final_norm_g=final_norm_g)
    g_in, g_out, g_up, g_down, g_conv = _all_gather(
        [w_in[0].astype(BF16), w_out[0].astype(BF16), w_up[0].astype(BF16), w_down[0].astype(BF16), conv_w[0]],
        "gather_weights")
    w_in_p = _to_my_columns(jnp.transpose(g_in, (1, 0, 2)).reshape(D_MODEL, IN_PROJ))
    w_out_f = g_out.reshape(D_MODEL, D_MODEL)
    w_down_f = g_down.reshape(D_FF, D_MODEL)
    conv_w_f = jnp.transpose(g_conv, (1, 0, 2)).reshape(CONV_K, CONV_DIM)
    dx, gw_in, gw_out, gw_up, gw_down, small = _local_step(x[0], loss_target[0], w, w_in_p, w_out_f, g_up, w_down_f,
                                                           conv_w_f)
    per = IN_PROJ // N_DEV
    s_in = jnp.transpose(_to_natural_columns(gw_in).reshape(D_MODEL, N_DEV, per), (1, 0, 2))
    s_out = gw_out.reshape(N_DEV, D_MODEL // N_DEV, D_MODEL)
    s_down = gw_down.reshape(N_DEV, D_FF // N_DEV, D_MODEL)
    slabs = [s_in, s_out, gw_up, s_down]
    rows = [256, 256, 512, 256]
    from_sibling = _exchange_pair(slabs, "reduce_pair")
    chip_sums = [_pair_add(s, r, csel, tr, f"pair_add_{i}") for i, (s, r, tr) in enumerate(zip(slabs, from_sibling, rows))]
    from_chips = _exchange_chips(chip_sums, "reduce_chips")
    (all_small,) = _all_gather([small], "gather_small")
    big = {}
    for i, (name, wt, mt, vt) in enumerate([("w_in", w_in, m_w_in, v_w_in), ("w_out", w_out, m_w_out, v_w_out),
                                            ("w_up", w_up, m_w_up, v_w_up), ("w_down", w_down, m_w_down, v_w_down)]):
        g, d, mn, vn = _adamw_big(wt[0], mt[0], vt[0], chip_sums[i], from_chips[i], qsel, rows[i], f"adamw_{name}")
        big[name] = (g[None], d[None], mn[None], vn[None])
    gsum = _small_sum(all_small, "small_sum")
    loss = gsum[9, 64]
    gs = _unpack_small(gsum, CONV_DIM)
    cw = CONV_DIM // N_DEV
    g_conv_shard = lax.dynamic_slice(gsum[5:9, :], (0, me * cw), (CONV_K, cw))

    def pack(s):
        return _pack_small(s["mix_norm_g"], s["conv_b"], s["ssm_norm_g"], s["attn_out_norm_g"], s["mlp_norm_g"],
                           s["final_norm_g"], s["conv_w"][0], s["dt_bias"], s["A_log"], s["D_skip"], s["attn_sinks"])

    wp = pack(dict(w, conv_w=conv_w))
    mp = pack(dict(mix_norm_g=m_mix_norm_g, conv_b=m_conv_b, ssm_norm_g=m_ssm_norm_g,
                   attn_out_norm_g=m_attn_out_norm_g, mlp_norm_g=m_mlp_norm_g, final_norm_g=m_final_norm_g,
                   conv_w=m_conv_w, dt_bias=m_dt_bias, A_log=m_A_log, D_skip=m_D_skip, attn_sinks=m_attn_sinks))
    vp = pack(dict(mix_norm_g=v_mix_norm_g, conv_b=v_conv_b, ssm_norm_g=v_ssm_norm_g,
                   attn_out_norm_g=v_attn_out_norm_g, mlp_norm_g=v_mlp_norm_g, final_norm_g=v_final_norm_g,
                   conv_w=v_conv_w, dt_bias=v_dt_bias, A_log=v_A_log, D_skip=v_D_skip, attn_sinks=v_attn_sinks))
    gp = jnp.concatenate([gsum[0:5], jnp.pad(g_conv_shard, ((0, 0), (0, D_MODEL - cw))), gsum[9:10],
                          jnp.zeros((SMALL_ROWS - 10, D_MODEL), F32)], axis=0)
    dp, mnp, vnp = _adamw_small(wp, gp, mp, vp, "adamw_small")
    grads = dict(gs, conv_w=g_conv_shard[None])
    deltas = _unpack_small(dp, cw)
    new_m = _unpack_small(mnp, cw)
    new_v = _unpack_small(vnp, cw)
    for k, name in enumerate(["w_in", "w_out", "w_up", "w_down"]):
        grads[name], deltas[name], new_m[name], new_v[name] = big[name]
    return (loss, dx[None], *[grads[n] for n in WEIGHT_ORDER], *[deltas[n] for n in WEIGHT_ORDER],
            *[new_m[n] for n in WEIGHT_ORDER], *[new_v[n] for n in WEIGHT_ORDER])
```

```python
import functools

import jax
import jax.numpy as jnp
from jax import lax
from jax.experimental import pallas as pl
from jax.experimental.pallas import tpu as pltpu

F32 = jnp.float32
BF16 = jnp.bfloat16
HI = lax.Precision.HIGHEST
MESH = pl.DeviceIdType.MESH

EPS = 1e-5
D_MODEL = 2048
D_INNER = 1024
N_HEADS = 16
HEAD_DIM = 64
N_GROUPS = 4
D_STATE = 128
CHUNK = 128
CONV_K = 4
CONV_DIM = 2048
ATTN_W = 1024
KV_W = 128
WINDOW = 128
D_FF = 8192
IN_PROJ = 4368
N_DEV = 8
NP = 4608
OFF_Z, OFF_X, OFF_B, OFF_C, OFF_Q, OFF_K, OFF_V, OFF_DT = 0, 1024, 2048, 2560, 3072, 4096, 4224, 4352
NAT_DT = 3072

ADAM_LR = 0.001
ADAM_B1 = 0.9
ADAM_B2 = 0.999
ADAM_EPS = 1e-08
ADAM_WD = 0.01
ADAM_STEP = 10

VMEM_LIMIT = 52 * 1024 * 1024
SMALL_ROWS = 16
NEG = -1e30


def _cparams(sem=None):
    return pltpu.CompilerParams(dimension_semantics=sem, vmem_limit_bytes=VMEM_LIMIT)


def _hdot(a, b):
    return jnp.dot(a, b, precision=HI, preferred_element_type=F32)


def _dot_nn(a, b):
    return lax.dot_general(a, b, (((1,), (0,)), ((), ())), preferred_element_type=F32)


def _dot_nt(a, b):
    return lax.dot_general(a, b, (((1,), (1,)), ((), ())), preferred_element_type=F32)


def _dot_tn(a, b):
    return lax.dot_general(a, b, (((0,), (0,)), ((), ())), preferred_element_type=F32)


def _softplus(v):
    return jnp.maximum(v, 0.0) + jnp.log1p(jnp.exp(-jnp.abs(v)))


def _sigmoid(v):
    return 1.0 / (1.0 + jnp.exp(-v))


def _matmul(a, b, *, mode, grid, a_spec, b_spec, out_shapes, out_specs, tile, name,
            extras=(), extra_specs=(), epilogue=None, after=None):
    nk = grid[2]
    n_ex = len(extras)
    n_out = len(out_shapes)
    dot = {"nn": _dot_nn, "nt": _dot_nt, "tn": _dot_tn}[mode]

    def finish(acc, ex_refs, out_refs):
        res = (acc,) if epilogue is None else epilogue(acc, *[e[...] for e in ex_refs])
        for o, r in zip(out_refs, res):
            o[...] = r.astype(o.dtype)

    def body(*refs):
        a_ref, b_ref = refs[0], refs[1]
        ex_refs = refs[2:2 + n_ex]
        out_refs = refs[2 + n_ex:2 + n_ex + n_out]
        part = dot(a_ref[...].astype(BF16), b_ref[...].astype(BF16))
        if nk == 1:
            finish(part, ex_refs, out_refs)
        else:
            acc_ref = refs[-1]
            k = pl.program_id(2)

            @pl.when(k == 0)
            def _():
                acc_ref[...] = part

            @pl.when(k > 0)
            def _():
                acc_ref[...] += part

            @pl.when(k == nk - 1)
            def _():
                finish(acc_ref[...], ex_refs, out_refs)

    scratch = [] if nk == 1 else [pltpu.VMEM(tile, F32)]
    tok_specs = [] if after is None else [pl.BlockSpec((8, 128), lambda i, j, k: (0, 0))]
    tok_args = [] if after is None else [after]
    n_out = len(out_shapes)

    def body_with_token(*refs):
        body(*refs[:2 + n_ex], *refs[2 + n_ex + len(tok_args):])

    outs = pl.pallas_call(
        body_with_token, grid=grid, in_specs=[a_spec, b_spec, *extra_specs, *tok_specs], out_specs=list(out_specs),
        out_shape=list(out_shapes), scratch_shapes=scratch, name=name,
        compiler_params=_cparams(("parallel", "parallel", "arbitrary")),
    )(a, b, *extras, *tok_args)
    return outs


def _mm_simple(a, b, *, mode, M, N, K, tm, tn, tk, out_dtype, name, extras=(), epilogue=None, n_out=1,
               out_dtypes=None, after=None):
    grid = (M // tm, N // tn, K // tk)
    if mode == "nn":
        a_spec = pl.BlockSpec((tm, tk), lambda i, j, k: (i, k))
        b_spec = pl.BlockSpec((tk, tn), lambda i, j, k: (k, j))
    elif mode == "nt":
        a_spec = pl.BlockSpec((tm, tk), lambda i, j, k: (i, k))
        b_spec = pl.BlockSpec((tn, tk), lambda i, j, k: (j, k))
    else:
        a_spec = pl.BlockSpec((tk, tm), lambda i, j, k: (k, i))
        b_spec = pl.BlockSpec((tk, tn), lambda i, j, k: (k, j))
    o_spec = pl.BlockSpec((tm, tn), lambda i, j, k: (i, j))
    dts = out_dtypes if out_dtypes is not None else [out_dtype] * n_out
    return _matmul(a, b, mode=mode, grid=grid, a_spec=a_spec, b_spec=b_spec,
                   out_shapes=[jax.ShapeDtypeStruct((M, N), d) for d in dts],
                   out_specs=[o_spec] * len(dts), tile=(tm, tn), name=name,
                   extras=extras, extra_specs=[o_spec] * len(extras), epilogue=epilogue, after=after)


ROW_BLOCK = 256


def _rmsnorm_fwd(x, g, name):
    T, D = x.shape

    def body(x_ref, g_ref, o_ref):
        xf = x_ref[...]
        r = lax.rsqrt(jnp.mean(xf * xf, axis=-1, keepdims=True) + EPS)
        o_ref[...] = (xf * r * g_ref[...]).astype(BF16)

    return pl.pallas_call(
        body, grid=(T // ROW_BLOCK,),
        in_specs=[pl.BlockSpec((ROW_BLOCK, D), lambda i: (i, 0)), pl.BlockSpec((1, D), lambda i: (0, 0))],
        out_specs=pl.BlockSpec((ROW_BLOCK, D), lambda i: (i, 0)),
        out_shape=jax.ShapeDtypeStruct((T, D), BF16), name=name, compiler_params=_cparams(("parallel",)),
    )(x, g)


def _rmsnorm_bwd(dh, x, g, dres, name):
    T, D = x.shape

    def body(dh_ref, x_ref, g_ref, dres_ref, dx_ref, dxb_ref, dg_ref):
        i = pl.program_id(0)
        xf = x_ref[...]
        r = lax.rsqrt(jnp.mean(xf * xf, axis=-1, keepdims=True) + EPS)
        xh = xf * r
        d = dh_ref[...]

        @pl.when(i == 0)
        def _():
            dg_ref[...] = jnp.zeros_like(dg_ref)

        dg_ref[...] += jnp.sum(d * xh, axis=0, keepdims=True)
        dxh = d * g_ref[...]
        dx = r * (dxh - xh * jnp.mean(dxh * xh, axis=-1, keepdims=True)) + dres_ref[...]
        dx_ref[...] = dx
        dxb_ref[...] = dx.astype(BF16)

    row = pl.BlockSpec((ROW_BLOCK, D), lambda i: (i, 0))
    vec = pl.BlockSpec((1, D), lambda i: (0, 0))
    return pl.pallas_call(
        body, grid=(T // ROW_BLOCK,), in_specs=[row, row, vec, row], out_specs=[row, row, vec],
        out_shape=[jax.ShapeDtypeStruct((T, D), F32), jax.ShapeDtypeStruct((T, D), BF16),
                   jax.ShapeDtypeStruct((1, D), F32)],
        name=name, compiler_params=_cparams(("arbitrary",)),
    )(dh, x, g, dres)


def _final_loss(x3, tgt, g, name):
    T, D = x3.shape

    def body(x_ref, t_ref, g_ref, loss_ref, dg_ref, dx_ref, dxb_ref):
        i = pl.program_id(0)
        xf = x_ref[...]
        r = lax.rsqrt(jnp.mean(xf * xf, axis=-1, keepdims=True) + EPS)
        xh = xf * r
        gg = g_ref[...]
        err = xh * gg - t_ref[...]

        @pl.when(i == 0)
        def _():
            dg_ref[...] = jnp.zeros_like(dg_ref)
            loss_ref[...] = jnp.zeros_like(loss_ref)

        part = jnp.sum(jnp.sum(err * err, axis=-1, keepdims=True), axis=0, keepdims=True) * (0.5 / D)
        loss_ref[...] += jnp.broadcast_to(part, loss_ref.shape)
        dout = err * (1.0 / D)
        dg_ref[...] += jnp.sum(dout * xh, axis=0, keepdims=True)
        dxh = dout * gg
        dx = r * (dxh - xh * jnp.mean(dxh * xh, axis=-1, keepdims=True))
        dx_ref[...] = dx
        dxb_ref[...] = dx.astype(BF16)

    row = pl.BlockSpec((ROW_BLOCK, D), lambda i: (i, 0))
    vec = pl.BlockSpec((1, D), lambda i: (0, 0))
    return pl.pallas_call(
        body, grid=(T // ROW_BLOCK,), in_specs=[row, row, vec],
        out_specs=[pl.BlockSpec((1, 128), lambda i: (0, 0)), vec, row, row],
        out_shape=[jax.ShapeDtypeStruct((1, 128), F32), jax.ShapeDtypeStruct((1, D), F32),
                   jax.ShapeDtypeStruct((T, D), F32), jax.ShapeDtypeStruct((T, D), BF16)],
        name=name, compiler_params=_cparams(("arbitrary",)),
    )(x3, tgt, g)


CONV_BLOCK = 256


def _conv_apply(u, w, b):
    row = lax.broadcasted_iota(jnp.int32, u.shape, 0)
    acc = b + w[CONV_K - 1:CONV_K, :] * u
    shifted = []
    for j in range(1, CONV_K):
        uj = jnp.where(row >= j, pltpu.roll(u, j, axis=0), 0.0)
        shifted.append(uj)
        acc = acc + w[CONV_K - 1 - j:CONV_K - j, :] * uj
    return acc, shifted


def _conv_fwd(proj, conv_w, conv_b, name):
    T = proj.shape[0]
    cb0 = OFF_X // CONV_BLOCK

    def body(u_ref, w_ref, b_ref, o_ref):
        c, _ = _conv_apply(u_ref[...], w_ref[...], b_ref[...])
        o_ref[...] = c * _sigmoid(c)

    return pl.pallas_call(
        body, grid=(CONV_DIM // CONV_BLOCK,),
        in_specs=[pl.BlockSpec((T, CONV_BLOCK), lambda j: (0, cb0 + j)),
                  pl.BlockSpec((CONV_K, CONV_BLOCK), lambda j: (0, j)),
                  pl.BlockSpec((1, CONV_BLOCK), lambda j: (0, j))],
        out_specs=pl.BlockSpec((T, CONV_BLOCK), lambda j: (0, j)),
        out_shape=jax.ShapeDtypeStruct((T, CONV_DIM), F32), name=name, compiler_params=_cparams(("parallel",)),
    )(proj, conv_w, conv_b)


def _conv_bwd(proj, dact, conv_w, conv_b, name):
    T = proj.shape[0]
    cb0 = OFF_X // CONV_BLOCK

    def body(u_ref, d_ref, w_ref, b_ref, du_ref, dw_ref, db_ref):
        u = u_ref[...]
        w = w_ref[...]
        c, shifted = _conv_apply(u, w, b_ref[...])
        sg = _sigmoid(c)
        dc = d_ref[...] * sg * (1.0 + c * (1.0 - sg))
        row = lax.broadcasted_iota(jnp.int32, u.shape, 0)
        du = w[CONV_K - 1:CONV_K, :] * dc
        dw_ref[CONV_K - 1:CONV_K, :] = jnp.sum(dc * u, axis=0, keepdims=True)
        for j in range(1, CONV_K):
            dcj = jnp.where(row < T - j, pltpu.roll(dc, T - j, axis=0), 0.0)
            du = du + w[CONV_K - 1 - j:CONV_K - j, :] * dcj
            dw_ref[CONV_K - 1 - j:CONV_K - j, :] = jnp.sum(dc * shifted[j - 1], axis=0, keepdims=True)
        db_ref[...] = jnp.sum(dc, axis=0, keepdims=True)
        du_ref[...] = du.astype(BF16)

    return pl.pallas_call(
        body, grid=(CONV_DIM // CONV_BLOCK,),
        in_specs=[pl.BlockSpec((T, CONV_BLOCK), lambda j: (0, cb0 + j)),
                  pl.BlockSpec((T, CONV_BLOCK), lambda j: (0, j)),
                  pl.BlockSpec((CONV_K, CONV_BLOCK), lambda j: (0, j)),
                  pl.BlockSpec((1, CONV_BLOCK), lambda j: (0, j))],
        out_specs=[pl.BlockSpec((T, CONV_BLOCK), lambda j: (0, j)),
                   pl.BlockSpec((CONV_K, CONV_BLOCK), lambda j: (0, j)),
                   pl.BlockSpec((1, CONV_BLOCK), lambda j: (0, j))],
        out_shape=[jax.ShapeDtypeStruct((T, CONV_DIM), BF16), jax.ShapeDtypeStruct((CONV_K, CONV_DIM), F32),
                   jax.ShapeDtypeStruct((1, CONV_DIM), F32)],
        name=name, compiler_params=_cparams(("parallel",)),
    )(proj, dact, conv_w, conv_b)


GROUP_W = D_INNER // N_GROUPS
HEADS_PER_GROUP = N_HEADS // N_GROUPS


def _expand_mat():
    h = lax.broadcasted_iota(jnp.int32, (N_HEADS, D_INNER), 0)
    j = lax.broadcasted_iota(jnp.int32, (N_HEADS, D_INNER), 1)
    return (j // HEAD_DIM == h).astype(F32)


def _reduce_mat(g):
    j = lax.broadcasted_iota(jnp.int32, (GROUP_W, N_HEADS), 0)
    h = lax.broadcasted_iota(jnp.int32, (GROUP_W, N_HEADS), 1)
    return (g * HEADS_PER_GROUP + j // HEAD_DIM == h).astype(F32)


def _col16(v, h):
    lane = lax.broadcasted_iota(jnp.int32, v.shape, 1)
    return jnp.sum(jnp.where(lane == h, v, 0.0), axis=1, keepdims=True)


def _ssd_pre(dt_raw, dtT_raw, dtb, dtbT, alog, alogT):
    Q = CHUNK
    xdt = dt_raw + dtb
    dt = _softplus(xdt)
    dtT = _softplus(dtT_raw + dtbT)
    A = -jnp.exp(alog)
    AT = -jnp.exp(alogT)
    row = lax.broadcasted_iota(jnp.int32, (Q, Q), 0)
    col = lax.broadcasted_iota(jnp.int32, (Q, Q), 1)
    tril = (row >= col).astype(F32)
    triu = (row <= col).astype(F32)
    cs = _hdot(tril, dt * A)
    csT = _hdot(dtT * AT, triu)
    return xdt, dt, A, cs, csT, row >= col, triu


def _decay_matrix(cs, csT, h, causal):
    seg = _col16(cs, h) - csT[h:h + 1, :]
    return jnp.where(causal, jnp.exp(jnp.minimum(seg, 0.0)), 0.0)


def _ssd_in_specs(nc, rev):
    def cidx(c):
        return (nc - 1 - c) if rev else c

    return [
        pl.BlockSpec((CHUNK, D_INNER), lambda c: (cidx(c), 0)),
        pl.BlockSpec((CHUNK, 512), lambda c: (cidx(c), 2)),
        pl.BlockSpec((CHUNK, 512), lambda c: (cidx(c), 3)),
        pl.BlockSpec((CHUNK, D_INNER), lambda c: (cidx(c), 0)),
        pl.BlockSpec((CHUNK, 128), lambda c: (cidx(c), OFF_DT // 128)),
        pl.BlockSpec((N_HEADS, CHUNK), lambda c: (0, cidx(c))),
        pl.BlockSpec((1, N_HEADS), lambda c: (0, 0)),
        pl.BlockSpec((N_HEADS, 1), lambda c: (0, 0)),
        pl.BlockSpec((1, N_HEADS), lambda c: (0, 0)),
        pl.BlockSpec((N_HEADS, 1), lambda c: (0, 0)),
        pl.BlockSpec((1, D_INNER), lambda c: (0, 0)),
        pl.BlockSpec((1, D_INNER), lambda c: (0, 0)),
    ]


def _ssd_fwd(xbc, proj, dtT, dtb, dtbT, alog, alogT, dfull, ng, name):
    T = xbc.shape[0]
    nc = T // CHUNK
    Q = CHUNK

    def body(xs_ref, B_ref, C_ref, z_ref, dt_ref, dtT_ref, dtb_ref, dtbT_ref, al_ref, alT_ref, df_ref, ng_ref,
             y_ref, ypre_ref, hs_ref, h_scr):
        c = pl.program_id(0)

        @pl.when(c == 0)
        def _():
            h_scr[...] = jnp.zeros_like(h_scr)

        _, dt, _, cs, csT, causal, _ = _ssd_pre(dt_ref[:, :N_HEADS], dtT_ref[...], dtb_ref[...], dtbT_ref[...],
                                                al_ref[...], alT_ref[...])
        ex = _expand_mat()
        dt_full = _hdot(dt, ex)
        cs_full = _hdot(cs, ex)
        cs_last = cs_full[Q - 1:Q, :]
        xs = xs_ref[...]
        xd = xs * dt_full
        e_full = jnp.exp(cs_full)
        dec_full = jnp.exp(cs_last - cs_full)
        cd_full = jnp.exp(cs_last)
        lane_head = lax.broadcasted_iota(jnp.int32, (1, GROUP_W), 1) // HEAD_DIM
        for g in range(N_GROUPS):
            sl = slice(g * GROUP_W, (g + 1) * GROUP_W)
            Bg = B_ref[:, g * D_STATE:(g + 1) * D_STATE].astype(BF16)
            Cg = C_ref[:, g * D_STATE:(g + 1) * D_STATE].astype(BF16)
            CB = _dot_nt(Cg, Bg)
            hg = h_scr[g]
            yoff = _dot_nn(Cg, hg.astype(BF16)) * e_full[:, sl]
            xd_g = xd[:, sl]
            S = _dot_tn(Bg, (xd_g * dec_full[:, sl]).astype(BF16))
            xd_b = xd_g.astype(BF16)
            ydiag = jnp.zeros((Q, GROUP_W), F32)
            for r in range(HEADS_PER_GROUP):
                Lm = _decay_matrix(cs, csT, g * HEADS_PER_GROUP + r, causal)
                Gm = (CB * Lm).astype(BF16)
                ydiag = ydiag + _dot_nn(Gm, jnp.where(lane_head == r, xd_b, jnp.zeros_like(xd_b)))
            hs_ref[0, g] = hg
            h_scr[g] = hg * cd_full[:, sl] + S
            ypre = ydiag + yoff + xs[:, sl] * df_ref[:, sl]
            ypre_ref[:, sl] = ypre
            zg = z_ref[:, sl]
            yz = ypre * zg * _sigmoid(zg)
            rn = lax.rsqrt(jnp.mean(yz * yz, axis=-1, keepdims=True) + EPS)
            y_ref[:, sl] = (yz * rn * ng_ref[:, sl]).astype(BF16)

    return pl.pallas_call(
        body, grid=(nc,), in_specs=_ssd_in_specs(nc, False),
        out_specs=[pl.BlockSpec((CHUNK, D_INNER), lambda c: (c, 0)),
                   pl.BlockSpec((CHUNK, D_INNER), lambda c: (c, 0)),
                   pl.BlockSpec((1, N_GROUPS, D_STATE, GROUP_W), lambda c: (c, 0, 0, 0))],
        out_shape=[jax.ShapeDtypeStruct((T, D_INNER), BF16), jax.ShapeDtypeStruct((T, D_INNER), F32),
                   jax.ShapeDtypeStruct((nc, N_GROUPS, D_STATE, GROUP_W), F32)],
        scratch_shapes=[pltpu.VMEM((N_GROUPS, D_STATE, GROUP_W), F32)],
        name=name, compiler_params=_cparams(("arbitrary",)),
    )(xbc, xbc, xbc, proj, proj, dtT, dtb, dtbT, alog, alogT, dfull, ng)


def _ssd_bwd(xbc, proj, dtT, dtb, dtbT, alog, alogT, dfull, ng, ypre, hs, dy, name):
    T = xbc.shape[0]
    nc = T // CHUNK
    Q = CHUNK

    def body(xs_ref, B_ref, C_ref, z_ref, dt_ref, dtT_ref, dtb_ref, dtbT_ref, al_ref, alT_ref, df_ref, ng_ref,
             ypre_ref, hs_ref, dy_ref,
             dz_ref, dxbc_ref, ddt_ref, ddtb_ref, dal_ref, dD_ref, dng_ref, dh_scr):
        step = pl.program_id(0)

        @pl.when(step == 0)
        def _():
            dh_scr[...] = jnp.zeros_like(dh_scr)
            ddtb_ref[...] = jnp.zeros_like(ddtb_ref)
            dal_ref[...] = jnp.zeros_like(dal_ref)
            dD_ref[...] = jnp.zeros_like(dD_ref)
            dng_ref[...] = jnp.zeros_like(dng_ref)

        xdt, dt, A, cs, csT, causal, triu = _ssd_pre(dt_ref[:, :N_HEADS], dtT_ref[...], dtb_ref[...],
                                                    dtbT_ref[...], al_ref[...], alT_ref[...])
        ex = _expand_mat()
        dt_full = _hdot(dt, ex)
        cs_full = _hdot(cs, ex)
        cs_last = cs_full[Q - 1:Q, :]
        xs = xs_ref[...]
        xd = xs * dt_full
        e_full = jnp.exp(cs_full)
        dec_full = jnp.exp(cs_last - cs_full)
        cd_full = jnp.exp(cs_last)
        lane_head = lax.broadcasted_iota(jnp.int32, (1, GROUP_W), 1) // HEAD_DIM
        is_last = lax.broadcasted_iota(jnp.int32, (Q, 1), 0) == Q - 1
        dcs16 = jnp.zeros((Q, N_HEADS), F32)
        ddtx16 = jnp.zeros((Q, N_HEADS), F32)
        dD16 = jnp.zeros((8, N_HEADS), F32)
        lane16 = lax.broadcasted_iota(jnp.int32, (1, N_HEADS), 1)
        sub16 = lax.broadcasted_iota(jnp.int32, (N_HEADS, 1), 0)
        col_sums = jnp.zeros((N_HEADS, Q), F32)
        for g in range(N_GROUPS):
            sl = slice(g * GROUP_W, (g + 1) * GROUP_W)
            red = _reduce_mat(g)
            ypre_g = ypre_ref[:, sl]
            zg = z_ref[:, sl]
            sg = _sigmoid(zg)
            silu = zg * sg
            yz = ypre_g * silu
            rn = lax.rsqrt(jnp.mean(yz * yz, axis=-1, keepdims=True) + EPS)
            yh = yz * rn
            dy_g = dy_ref[:, sl]
            dng_ref[:, sl] += jnp.sum(dy_g * yh, axis=0, keepdims=True)
            dyh = dy_g * ng_ref[:, sl]
            dyz = rn * (dyh - yh * jnp.mean(dyh * yh, axis=-1, keepdims=True))
            dY = dyz * silu
            dz_ref[:, sl] = (dyz * ypre_g * sg * (1.0 + zg * (1.0 - sg))).astype(BF16)
            xs_g = xs[:, sl]
            xd_g = xd[:, sl]
            dec_g = dec_full[:, sl]
            cd_g = cd_full[:, sl]
            d_g = df_ref[:, sl]
            Bg = B_ref[:, g * D_STATE:(g + 1) * D_STATE].astype(BF16)
            Cg = C_ref[:, g * D_STATE:(g + 1) * D_STATE].astype(BF16)
            CB = _dot_nt(Cg, Bg)
            hg = hs_ref[0, g]
            hgb = hg.astype(BF16)
            yoff = _dot_nn(Cg, hgb) * e_full[:, sl]
            dhn = dh_scr[g]
            dhnb = dhn.astype(BF16)
            dYE = (dY * e_full[:, sl]).astype(BF16)
            dC = _dot_nt(dYE, hgb)
            dh_direct = _dot_tn(Cg, dYE)
            dXdd = _dot_nn(Bg, dhnb)
            dB = _dot_nt((xd_g * dec_g).astype(BF16), dhnb)
            dcd = jnp.sum(dhn * hg, axis=0, keepdims=True)
            dh_scr[g] = dh_direct + cd_g * dhn
            dYb = dY.astype(BF16)
            xd_b = xd_g.astype(BF16)
            dCB = jnp.zeros((Q, Q), F32)
            dXd = dXdd * dec_g
            for r in range(HEADS_PER_GROUP):
                h = g * HEADS_PER_GROUP + r
                Lm = _decay_matrix(cs, csT, h, causal)
                Gf = CB * Lm
                dYr = jnp.where(lane_head == r, dYb, jnp.zeros_like(dYb))
                dG = _dot_nt(dYr, xd_b)
                dCB = dCB + dG * Lm
                dXd = dXd + _dot_tn(Gf.astype(BF16), dYr)
                Mm = dG * Gf
                dcs16 = dcs16 + jnp.where(lane16 == h, jnp.sum(Mm, axis=1, keepdims=True), 0.0)
                col_sums = col_sums + jnp.where(sub16 == h, jnp.sum(Mm, axis=0, keepdims=True), 0.0)
            dCBb = dCB.astype(BF16)
            dC = dC + _dot_nn(dCBb, Bg)
            dB = dB + _dot_tn(dCBb, Cg)
            w_state = dXdd * dec_g * xd_g
            t_last = jnp.sum(w_state, axis=0, keepdims=True) + dcd * cd_g
            dcs_g = dY * yoff - w_state + jnp.where(is_last, t_last, 0.0)
            dcs16 = dcs16 + _hdot(dcs_g, red)
            ddtx16 = ddtx16 + _hdot(dXd * xs_g, red)
            dD16 = dD16 + _hdot(jnp.broadcast_to(jnp.sum(dY * xs_g, axis=0, keepdims=True), (8, GROUP_W)), red)
            dxbc_ref[:, sl] = dXd * dt_full[:, sl] + dY * d_g
            dxbc_ref[:, D_INNER + g * D_STATE:D_INNER + (g + 1) * D_STATE] = dB
            dxbc_ref[:, D_INNER + 512 + g * D_STATE:D_INNER + 512 + (g + 1) * D_STATE] = dC
        eye = (lax.broadcasted_iota(jnp.int32, (N_HEADS, N_HEADS), 0)
               == lax.broadcasted_iota(jnp.int32, (N_HEADS, N_HEADS), 1)).astype(F32)
        dcs16 = dcs16 - lax.dot_general(col_sums, eye, (((0,), (0,)), ((), ())), precision=HI,
                                        preferred_element_type=F32)
        da = _hdot(triu, dcs16)
        ddt = da * A + ddtx16
        ddt_raw = ddt * _sigmoid(xdt)
        pr = lax.broadcasted_iota(jnp.int32, (N_HEADS, 128), 0)
        pc = lax.broadcasted_iota(jnp.int32, (N_HEADS, 128), 1)
        ddt_ref[...] = _hdot(ddt_raw, (pr == pc).astype(F32))
        ddtb_ref[...] += jnp.sum(ddt_raw, axis=0, keepdims=True)
        dal_ref[...] += jnp.sum(da * dt, axis=0, keepdims=True) * A
        dD_ref[...] += dD16[0:1, :]

    def rc(c):
        return nc - 1 - c

    in_specs = _ssd_in_specs(nc, True) + [
        pl.BlockSpec((CHUNK, D_INNER), lambda c: (rc(c), 0)),
        pl.BlockSpec((1, N_GROUPS, D_STATE, GROUP_W), lambda c: (rc(c), 0, 0, 0)),
        pl.BlockSpec((CHUNK, D_INNER), lambda c: (rc(c), 0)),
    ]
    small = pl.BlockSpec((1, N_HEADS), lambda c: (0, 0))
    return pl.pallas_call(
        body, grid=(nc,), in_specs=in_specs,
        out_specs=[pl.BlockSpec((CHUNK, D_INNER), lambda c: (rc(c), 0)),
                   pl.BlockSpec((CHUNK, CONV_DIM), lambda c: (rc(c), 0)),
                   pl.BlockSpec((CHUNK, 128), lambda c: (rc(c), 0)),
                   small, small, small,
                   pl.BlockSpec((1, D_INNER), lambda c: (0, 0))],
        out_shape=[jax.ShapeDtypeStruct((T, D_INNER), BF16), jax.ShapeDtypeStruct((T, CONV_DIM), F32),
                   jax.ShapeDtypeStruct((T, 128), F32),
                   jax.ShapeDtypeStruct((1, N_HEADS), F32), jax.ShapeDtypeStruct((1, N_HEADS), F32),
                   jax.ShapeDtypeStruct((1, N_HEADS), F32), jax.ShapeDtypeStruct((1, D_INNER), F32)],
        scratch_shapes=[pltpu.VMEM((N_GROUPS, D_STATE, GROUP_W), F32)],
        name=name, compiler_params=_cparams(("arbitrary",)),
    )(xbc, xbc, xbc, proj, proj, dtT, dtb, dtbT, alog, alogT, dfull, ng, ypre, hs, dy)


N_PAIRS = ATTN_W // 128
PAIRS_PER_KV = N_PAIRS // 2
ATTN_SCALE = HEAD_DIM ** -0.5


def _kv_variants(kk):
    lo = lax.broadcasted_iota(jnp.int32, kk.shape, 1) < HEAD_DIM
    zero = jnp.zeros_like(kk)
    k00 = jnp.where(lo, kk, zero)
    k11 = jnp.where(lo, zero, kk)
    k01 = pltpu.roll(k00, HEAD_DIM, axis=1)
    k10 = pltpu.roll(k11, HEAD_DIM, axis=1)
    return [[k00.astype(BF16), k01.astype(BF16)], [k10.astype(BF16), k11.astype(BF16)]]


def _attn_valid(n):
    i = lax.broadcasted_iota(jnp.int32, (WINDOW, 2 * WINDOW), 0)
    j = lax.broadcasted_iota(jnp.int32, (WINDOW, 2 * WINDOW), 1)
    return (j > i) & (j <= i + WINDOW) & (n * WINDOW + j >= WINDOW)


def _attn_probs(qp, kvar, valid, sk):
    s = _dot_nt(qp, kvar) * ATTN_SCALE
    s = jnp.where(valid, s, NEG)
    m = jnp.maximum(jnp.max(s, axis=1, keepdims=True), sk)
    pe = jnp.exp(s - m)
    es = jnp.exp(sk - m)
    den = jnp.sum(pe, axis=1, keepdims=True) + es
    inv = 1.0 / den
    return pe * inv, es * inv


def _sink(sinks, r):
    lane = lax.broadcasted_iota(jnp.int32, sinks.shape, 1)
    return jnp.sum(jnp.where(lane == r, sinks, 0.0), axis=1, keepdims=True)


def _attn_fwd(proj, kpad, vpad, sinks, og, name):
    T = proj.shape[0]
    nb = T // WINDOW

    def body(q_ref, k_ref, v_ref, s_ref, og_ref, y_ref, o_ref):
        n = pl.program_id(0)
        start = pl.multiple_of(n * WINDOW, WINDOW)
        kv = _kv_variants(k_ref[pl.ds(start, 2 * WINDOW), :])
        vv = _kv_variants(v_ref[pl.ds(start, 2 * WINDOW), :])
        valid = _attn_valid(n)
        sinks_v = s_ref[...]
        ssq = jnp.zeros((WINDOW, 1), F32)
        for p in range(N_PAIRS):
            j = p // PAIRS_PER_KV
            qp = q_ref[:, p * 128:(p + 1) * 128].astype(BF16)
            o_pair = jnp.zeros((WINDOW, 128), F32)
            for par in range(2):
                pn, _ = _attn_probs(qp, kv[j][par], valid, _sink(sinks_v, 2 * p + par))
                o_pair = o_pair + _dot_nn(pn.astype(BF16), vv[j][par])
            o_ref[:, p * 128:(p + 1) * 128] = o_pair
            ssq = ssq + jnp.sum(o_pair * o_pair, axis=1, keepdims=True)
        rn = lax.rsqrt(ssq * (1.0 / ATTN_W) + EPS)
        y_ref[...] = (o_ref[...] * rn * og_ref[...]).astype(BF16)

    full_kv = pl.BlockSpec((T + WINDOW, KV_W), lambda n: (0, 0))
    return pl.pallas_call(
        body, grid=(nb,),
        in_specs=[pl.BlockSpec((WINDOW, ATTN_W), lambda n: (n, OFF_Q // ATTN_W)), full_kv, full_kv,
                  pl.BlockSpec((1, N_HEADS), lambda n: (0, 0)), pl.BlockSpec((1, ATTN_W), lambda n: (0, 0))],
        out_specs=[pl.BlockSpec((WINDOW, ATTN_W), lambda n: (n, 0)), pl.BlockSpec((WINDOW, ATTN_W), lambda n: (n, 0))],
        out_shape=[jax.ShapeDtypeStruct((T, ATTN_W), BF16), jax.ShapeDtypeStruct((T, ATTN_W), F32)],
        name=name, compiler_params=_cparams(("parallel",)),
    )(proj, kpad, vpad, sinks, og)


def _attn_bwd(proj, kpad, vpad, sinks, og, o, dy, name):
    T = proj.shape[0]
    nb = T // WINDOW

    def body(q_ref, k_ref, v_ref, s_ref, og_ref, o_ref, dy_ref, dq_ref, dk_ref, dv_ref, ds_ref, dog_ref):
        n = pl.program_id(0)

        @pl.when(n == 0)
        def _():
            dk_ref[...] = jnp.zeros_like(dk_ref)
            dv_ref[...] = jnp.zeros_like(dv_ref)
            ds_ref[...] = jnp.zeros_like(ds_ref)
            dog_ref[...] = jnp.zeros_like(dog_ref)

        start = pl.multiple_of(n * WINDOW, WINDOW)
        kv = _kv_variants(k_ref[pl.ds(start, 2 * WINDOW), :])
        vv = _kv_variants(v_ref[pl.ds(start, 2 * WINDOW), :])
        valid = _attn_valid(n)
        sinks_v = s_ref[...]
        of = o_ref[...]
        rn = lax.rsqrt(jnp.mean(of * of, axis=-1, keepdims=True) + EPS)
        oh = of * rn
        dyf = dy_ref[...]
        dog_ref[...] += jnp.sum(dyf * oh, axis=0, keepdims=True)
        doh = dyf * og_ref[...]
        do = rn * (doh - oh * jnp.mean(doh * oh, axis=-1, keepdims=True))
        lane = lax.broadcasted_iota(jnp.int32, (1, 128), 1)
        lane16 = lax.broadcasted_iota(jnp.int32, (1, N_HEADS), 1)
        dk_acc = [[jnp.zeros((2 * WINDOW, 128), F32) for _ in range(2)] for _ in range(2)]
        dv_acc = [[jnp.zeros((2 * WINDOW, 128), F32) for _ in range(2)] for _ in range(2)]
        dsink = jnp.zeros((1, N_HEADS), F32)
        for p in range(N_PAIRS):
            j = p // PAIRS_PER_KV
            qp = q_ref[:, p * 128:(p + 1) * 128].astype(BF16)
            do_p = do[:, p * 128:(p + 1) * 128]
            o_p = of[:, p * 128:(p + 1) * 128]
            do_b = do_p.astype(BF16)
            prod = do_p * o_p
            dq_pair = jnp.zeros((WINDOW, 128), F32)
            for par in range(2):
                r = 2 * p + par
                half = (lane < HEAD_DIM) if par == 0 else (lane >= HEAD_DIM)
                pn, ps = _attn_probs(qp, kv[j][par], valid, _sink(sinks_v, r))
                delta = jnp.sum(jnp.where(half, prod, 0.0), axis=1, keepdims=True)
                dP = _dot_nt(do_b, vv[j][par])
                dS = pn * (dP - delta)
                dsink = dsink + jnp.where(lane16 == r, -jnp.sum(ps * delta, axis=0, keepdims=True), 0.0)
                dSb = (dS * ATTN_SCALE).astype(BF16)
                dq_pair = dq_pair + _dot_nn(dSb, kv[j][par])
                dk_acc[j][par] = dk_acc[j][par] + _dot_tn(dSb, jnp.where(half, qp, jnp.zeros_like(qp)))
                dv_acc[j][par] = dv_acc[j][par] + _dot_tn(pn.astype(BF16), jnp.where(half, do_b, jnp.zeros_like(do_b)))
            dq_ref[:, p * 128:(p + 1) * 128] = dq_pair.astype(BF16)
        dkk = (dk_acc[0][0] + pltpu.roll(dk_acc[0][1], HEAD_DIM, axis=1)
               + dk_acc[1][1] + pltpu.roll(dk_acc[1][0], HEAD_DIM, axis=1))
        dvv = (dv_acc[0][0] + pltpu.roll(dv_acc[0][1], HEAD_DIM, axis=1)
               + dv_acc[1][1] + pltpu.roll(dv_acc[1][0], HEAD_DIM, axis=1))
        dk_ref[pl.ds(start, 2 * WINDOW), :] += dkk
        dv_ref[pl.ds(start, 2 * WINDOW), :] += dvv
        ds_ref[...] += dsink

    full_kv = pl.BlockSpec((T + WINDOW, KV_W), lambda n: (0, 0))
    blk = pl.BlockSpec((WINDOW, ATTN_W), lambda n: (n, 0))
    return pl.pallas_call(
        body, grid=(nb,),
        in_specs=[pl.BlockSpec((WINDOW, ATTN_W), lambda n: (n, OFF_Q // ATTN_W)), full_kv, full_kv,
                  pl.BlockSpec((1, N_HEADS), lambda n: (0, 0)), pl.BlockSpec((1, ATTN_W), lambda n: (0, 0)),
                  blk, pl.BlockSpec((WINDOW, ATTN_W), lambda n: (n, 1))],
        out_specs=[blk, full_kv, full_kv, pl.BlockSpec((1, N_HEADS), lambda n: (0, 0)),
                   pl.BlockSpec((1, ATTN_W), lambda n: (0, 0))],
        out_shape=[jax.ShapeDtypeStruct((T, ATTN_W), BF16), jax.ShapeDtypeStruct((T + WINDOW, KV_W), F32),
                   jax.ShapeDtypeStruct((T + WINDOW, KV_W), F32), jax.ShapeDtypeStruct((1, N_HEADS), F32),
                   jax.ShapeDtypeStruct((1, ATTN_W), F32)],
        name=name, compiler_params=_cparams(("arbitrary",)),
    )(proj, kpad, vpad, sinks, og, o, dy)


ANY = pl.BlockSpec(memory_space=pl.ANY)


def _coords():
    return lax.axis_index("x"), lax.axis_index("y"), lax.axis_index("c")


def _all_gather(arrs, name):
    n = len(arrs)

    def body(*refs):
        ins, outs = refs[:n], refs[n:2 * n]
        send_sems, recv_sems, local_sems = refs[2 * n:]
        x, y, c = _coords()
        me = 4 * x + 2 * y + c
        sibling = (x, y, 1 - c)
        chips = [(1 - x, y), (x, 1 - y), (1 - x, 1 - y)]

        def copy(a, k, block, to, src=None):
            dst = outs[a].at[block]
            return pltpu.make_async_remote_copy(
                src_ref=dst if src is None else src, dst_ref=dst, send_sem=send_sems.at[a, k],
                recv_sem=recv_sems.at[a, k], device_id=to, device_id_type=MESH)

        mine = [pltpu.make_async_copy(ins[a], outs[a].at[me], local_sems.at[a]) for a in range(n)]
        for cp in mine:
            cp.start()
        first = []
        for a in range(n):
            first.append(copy(a, 0, me, sibling, src=ins[a]))
            for j, chip in enumerate(chips):
                first.append(copy(a, 1 + j, me, (*chip, c), src=ins[a]))
        for cp in first:
            cp.start()
        passed = []
        for j, (px, py) in enumerate(chips):
            blk = 4 * px + 2 * py + c
            for a in range(n):
                copy(a, 1 + j, blk, sibling).wait_recv()
                fwd = copy(a, 4 + j, blk, sibling)
                fwd.start()
                passed.append(fwd)
        for a in range(n):
            copy(a, 0, 4 * x + 2 * y + (1 - c), sibling).wait_recv()
            for j, (px, py) in enumerate(chips):
                copy(a, 4 + j, 4 * px + 2 * py + (1 - c), sibling).wait_recv()
        for cp in first + passed:
            cp.wait_send()
        for cp in mine:
            cp.wait()

    return pl.pallas_call(
        body, in_specs=[ANY] * n, out_specs=[ANY] * n,
        out_shape=[jax.ShapeDtypeStruct((N_DEV,) + a.shape, a.dtype) for a in arrs],
        scratch_shapes=[pltpu.SemaphoreType.DMA((n, 7)), pltpu.SemaphoreType.DMA((n, 7)),
                        pltpu.SemaphoreType.DMA((n,))],
        name=name,
    )(*arrs)


def _exchange_pair(arrs, name):
    n = len(arrs)

    def body(*refs):
        ins, outs = refs[:n], refs[n:2 * n]
        send_sems, recv_sems = refs[2 * n:]
        x, y, c = _coords()
        cps = []
        for a in range(n):
            for q in range(4):
                cps.append(pltpu.make_async_remote_copy(
                    src_ref=ins[a].at[2 * q + (1 - c)], dst_ref=outs[a].at[q], send_sem=send_sems.at[a, q],
                    recv_sem=recv_sems.at[a, q], device_id=(x, y, 1 - c), device_id_type=MESH))
        for cp in cps:
            cp.start()
        for cp in cps:
            cp.wait()

    return pl.pallas_call(
        body, in_specs=[ANY] * n, out_specs=[ANY] * n,
        out_shape=[jax.ShapeDtypeStruct((4,) + a.shape[1:], a.dtype) for a in arrs],
        scratch_shapes=[pltpu.SemaphoreType.DMA((n, 4)), pltpu.SemaphoreType.DMA((n, 4))],
        name=name,
    )(*arrs)


def _exchange_chips(arrs, name):
    n = len(arrs)

    def body(*refs):
        ins, outs = refs[:n], refs[n:2 * n]
        send_sems, recv_sems = refs[2 * n:]
        x, y, c = _coords()
        chips = [(1 - x, y), (x, 1 - y), (1 - x, 1 - y)]
        cps = []
        for a in range(n):
            for k, (tx, ty) in enumerate(chips):
                cps.append(pltpu.make_async_remote_copy(
                    src_ref=ins[a].at[2 * tx + ty], dst_ref=outs[a].at[k], send_sem=send_sems.at[a, k],
                    recv_sem=recv_sems.at[a, k], device_id=(tx, ty, c), device_id_type=MESH))
        for cp in cps:
            cp.start()
        for cp in cps:
            cp.wait()

    return pl.pallas_call(
        body, in_specs=[ANY] * n, out_specs=[ANY] * n,
        out_shape=[jax.ShapeDtypeStruct((3,) + a.shape[1:], a.dtype) for a in arrs],
        scratch_shapes=[pltpu.SemaphoreType.DMA((n, 3)), pltpu.SemaphoreType.DMA((n, 3))],
        name=name,
    )(*arrs)


HBM = pl.BlockSpec(memory_space=pltpu.HBM)
SEM = pl.BlockSpec(memory_space=pltpu.SEMAPHORE)
EFFECT = pltpu.SideEffectType.DATAFLOW_SIDE_EFFECTING


def _in_hbm(a):
    return pltpu.with_memory_space_constraint(a, pltpu.HBM)


def _remote_start(srcs, lands, plan, n_copies, name):
    n = len(srcs)

    def body(*refs):
        src_refs, land_refs = refs[:n], refs[n:2 * n]
        send_sems, recv_sems = refs[2 * n], refs[2 * n + 1]
        token = refs[-1]
        x, y, c = _coords()
        for i, (sv, dv, dev) in enumerate(plan(src_refs, land_refs, x, y, c)):
            pltpu.make_async_remote_copy(src_ref=sv, dst_ref=dv, send_sem=send_sems.at[i], recv_sem=recv_sems.at[i],
                                         device_id=dev, device_id_type=MESH).start()
        token[...] = jnp.zeros_like(token)

    bufs = list(srcs) + list(lands)
    outs = pl.pallas_call(
        body, name=name,
        out_shape=(pltpu.SemaphoreType.DMA((n_copies,)), pltpu.SemaphoreType.DMA((n_copies,)),
                   *[pltpu.HBM(b.shape, b.dtype) for b in bufs], jax.ShapeDtypeStruct((8, 128), F32)),
        in_specs=[HBM] * (2 * n), out_specs=(SEM, SEM, *[HBM] * (2 * n), pl.BlockSpec(memory_space=pltpu.VMEM)),
        input_output_aliases={i: 2 + i for i in range(2 * n)},
        compiler_params=pltpu.CompilerParams(has_side_effects=EFFECT),
    )(*[_in_hbm(b) for b in bufs])
    return outs[0], outs[1], list(outs[2:2 + n]), list(outs[2 + n:2 + 2 * n]), outs[-1]


def _remote_wait(started, after, plan, name):
    send_sems, recv_sems, srcs, lands, _ = started
    n = len(srcs)

    def body(*refs):
        src_refs, land_refs = refs[:n], refs[n:2 * n]
        send_sems, recv_sems = refs[2 * n], refs[2 * n + 1]
        x, y, c = _coords()
        for i, (sv, dv, dev) in enumerate(plan(src_refs, land_refs, x, y, c)):
            cp = pltpu.make_async_remote_copy(src_ref=sv, dst_ref=dv, send_sem=send_sems.at[i],
                                              recv_sem=recv_sems.at[i], device_id=dev, device_id_type=MESH)
            cp.wait_send()
            cp.wait_recv()

    bufs = list(srcs) + list(lands)
    outs = pl.pallas_call(
        body, name=name, out_shape=tuple(pltpu.HBM(b.shape, b.dtype) for b in bufs),
        in_specs=[HBM] * (2 * n) + [SEM, SEM, ANY], out_specs=tuple([HBM] * (2 * n)),
        input_output_aliases={i: i for i in range(2 * n)},
        compiler_params=pltpu.CompilerParams(has_side_effects=EFFECT),
    )(*bufs, send_sems, recv_sems, after)
    return list(outs[:n]), list(outs[n:])


def _gather_plan(src_refs, land_refs, x, y, c):
    me = 4 * x + 2 * y + c
    plan = []
    for s, l in zip(src_refs, land_refs):
        for dev in [(x, y, 1 - c), (1 - x, y, c), (x, 1 - y, c), (1 - x, 1 - y, c)]:
            plan.append((s, l.at[me], dev))
    return plan


def _chips_plan(src_refs, land_refs, x, y, c):
    plan = []
    for s, l in zip(src_refs, land_refs):
        for k, (tx, ty) in enumerate([(1 - x, y), (x, 1 - y), (1 - x, 1 - y)]):
            plan.append((s.at[2 * tx + ty], l.at[k], (tx, ty, c)))
    return plan


def _gather_finish(gathered, own, name):
    n = len(gathered)

    def body(*refs):
        own_refs, outs = refs[n:2 * n], refs[2 * n:3 * n]
        send_sems, recv_sems, local_sems = refs[3 * n:]
        x, y, c = _coords()
        me = 4 * x + 2 * y + c
        mine = [pltpu.make_async_copy(own_refs[a], outs[a].at[me], local_sems.at[a]) for a in range(n)]
        for cp in mine:
            cp.start()
        cps = []
        for a in range(n):
            for j, (px, py) in enumerate([(1 - x, y), (x, 1 - y), (1 - x, 1 - y)]):
                blk = outs[a].at[4 * px + 2 * py + c]
                got = outs[a].at[4 * px + 2 * py + (1 - c)]
                cps.append((pltpu.make_async_remote_copy(
                    src_ref=blk, dst_ref=blk, send_sem=send_sems.at[a, j], recv_sem=recv_sems.at[a, j],
                    device_id=(x, y, 1 - c), device_id_type=MESH), pltpu.make_async_remote_copy(
                    src_ref=got, dst_ref=got, send_sem=send_sems.at[a, j], recv_sem=recv_sems.at[a, j],
                    device_id=(x, y, 1 - c), device_id_type=MESH)))
        for cp, _ in cps:
            cp.start()
        for cp, arrival in cps:
            cp.wait_send()
            arrival.wait_recv()
        for cp in mine:
            cp.wait()

    return pl.pallas_call(
        body, in_specs=[ANY] * (2 * n), out_specs=[ANY] * n,
        out_shape=[jax.ShapeDtypeStruct(g.shape, g.dtype) for g in gathered],
        input_output_aliases={a: a for a in range(n)},
        scratch_shapes=[pltpu.SemaphoreType.DMA((n, 3)), pltpu.SemaphoreType.DMA((n, 3)),
                        pltpu.SemaphoreType.DMA((n,))],
        name=name,
    )(*gathered, *own)


def _pair_add(g8, r1, csel, tr, name):
    _, R, C = r1.shape
    g4 = g8.reshape(4, 2, R, C)

    def body(c_ref, g_ref, r_ref, o_ref):
        o_ref[...] = (g_ref[...].astype(F32) + r_ref[...].astype(F32)).astype(BF16)

    return pl.pallas_call(
        body,
        grid_spec=pltpu.PrefetchScalarGridSpec(
            num_scalar_prefetch=1, grid=(4, R // tr),
            in_specs=[pl.BlockSpec((None, None, tr, C), lambda q, i, cs: (q, cs[0], i, 0)),
                      pl.BlockSpec((None, tr, C), lambda q, i, cs: (q, i, 0))],
            out_specs=pl.BlockSpec((None, tr, C), lambda q, i, cs: (q, i, 0))),
        out_shape=jax.ShapeDtypeStruct((4, R, C), BF16), name=name,
        compiler_params=_cparams(("parallel", "parallel")),
    )(csel, g4, r1)


def _adamw_math(w, g, m, v):
    m = ADAM_B1 * m + (1.0 - ADAM_B1) * g
    v = ADAM_B2 * v + (1.0 - ADAM_B2) * (g * g)
    m_hat = m / (1.0 - ADAM_B1 ** ADAM_STEP)
    v_hat = v / (1.0 - ADAM_B2 ** ADAM_STEP)
    delta = -ADAM_LR * (m_hat / (jnp.sqrt(v_hat) + ADAM_EPS) + ADAM_WD * w)
    return delta, m, v


def _adamw_big(w, m, v, p4, r3, qsel, tr, name):
    R, C = w.shape

    def body(q_ref, w_ref, m_ref, v_ref, p_ref, r_ref, g_out, d_out, m_out, v_out):
        g = p_ref[...].astype(F32) + r_ref[0].astype(F32) + r_ref[1].astype(F32) + r_ref[2].astype(F32)
        d, mn, vn = _adamw_math(w_ref[...], g, m_ref[...], v_ref[...])
        g_out[...] = g
        d_out[...] = d
        m_out[...] = mn
        v_out[...] = vn

    blk = pl.BlockSpec((tr, C), lambda i, qs: (i, 0))
    return pl.pallas_call(
        body,
        grid_spec=pltpu.PrefetchScalarGridSpec(
            num_scalar_prefetch=1, grid=(R // tr,),
            in_specs=[blk, blk, blk, pl.BlockSpec((None, tr, C), lambda i, qs: (qs[0], i, 0)),
                      pl.BlockSpec((3, tr, C), lambda i, qs: (0, i, 0))],
            out_specs=[blk, blk, blk, blk]),
        out_shape=[jax.ShapeDtypeStruct((R, C), F32)] * 4, name=name,
        compiler_params=_cparams(("parallel",)),
    )(qsel, w, m, v, p4, r3)


def _small_sum(parts, name):
    def body(p_ref, o_ref):
        acc = p_ref[0]
        for d in range(1, N_DEV):
            acc = acc + p_ref[d]
        o_ref[...] = acc

    return pl.pallas_call(
        body, out_shape=jax.ShapeDtypeStruct(parts.shape[1:], F32), name=name,
        compiler_params=_cparams(),
    )(parts)


def _adamw_small(w, g, m, v, name):
    def body(w_ref, g_ref, m_ref, v_ref, d_out, m_out, v_out):
        d, mn, vn = _adamw_math(w_ref[...], g_ref[...], m_ref[...], v_ref[...])
        d_out[...] = d
        m_out[...] = mn
        v_out[...] = vn

    return pl.pallas_call(
        body, out_shape=[jax.ShapeDtypeStruct(w.shape, F32)] * 3, name=name, compiler_params=_cparams(),
    )(w, g, m, v)


def _row(*pieces):
    r = jnp.concatenate([p.reshape(1, -1) for p in pieces], axis=1)
    return jnp.pad(r, ((0, 0), (0, D_MODEL - r.shape[1])))


def _pack_small(mix, convb, ssmg, attng, mlpg, fing, convw, dtb, alog, dsk, sinks, extra=None):
    last = [dtb, alog, dsk, sinks] + ([extra] if extra is not None else [])
    rows = [_row(mix), _row(convb), _row(ssmg, attng), _row(mlpg), _row(fing),
            jnp.pad(convw, ((0, 0), (0, D_MODEL - convw.shape[1]))), _row(*last)]
    packed = jnp.concatenate(rows, axis=0)
    return jnp.pad(packed, ((0, SMALL_ROWS - packed.shape[0]), (0, 0)))


def _unpack_small(p, conv_n):
    return dict(
        mix_norm_g=p[0:1, :], conv_b=p[1:2, :], ssm_norm_g=p[2:3, :D_INNER], attn_out_norm_g=p[2:3, D_INNER:],
        mlp_norm_g=p[3:4, :], final_norm_g=p[4, :], conv_w=p[5:9, :conv_n][None],
        dt_bias=p[9:10, 0:16], A_log=p[9:10, 16:32], D_skip=p[9:10, 32:48], attn_sinks=p[9:10, 48:64])


SMALL_NAMES = ["mix_norm_g", "conv_w", "conv_b", "dt_bias", "A_log", "D_skip", "ssm_norm_g", "attn_sinks",
               "attn_out_norm_g", "mlp_norm_g", "final_norm_g"]
WEIGHT_ORDER = ["mix_norm_g", "w_in", "conv_w", "conv_b", "dt_bias", "A_log", "D_skip", "ssm_norm_g", "attn_sinks",
                "attn_out_norm_g", "w_out", "mlp_norm_g", "w_up", "w_down", "final_norm_g"]


def _to_my_columns(w_nat):
    pad = jnp.zeros((w_nat.shape[0], NP - IN_PROJ), w_nat.dtype)
    return jnp.concatenate([w_nat[:, :NAT_DT], w_nat[:, NAT_DT + N_HEADS:], w_nat[:, NAT_DT:NAT_DT + N_HEADS], pad],
                           axis=1)


def _to_natural_columns(w_my):
    return jnp.concatenate([w_my[:, :NAT_DT], w_my[:, OFF_DT:OFF_DT + N_HEADS], w_my[:, NAT_DT:OFF_DT]], axis=1)


class _FixedWeights:
    def __init__(self, w_in_p, w_out_f, w_up_s, w_down_f, conv_w_f):
        self.w = (w_in_p, w_out_f, w_up_s, w_down_f, conv_w_f)
        self.grads = {}

    def mixer_weights(self, after):
        return self.w[0], self.w[1], self.w[4]

    def up_weight(self, after):
        return self.w[2]

    def down_weight(self, after):
        return self.w[3]

    def mlp_grads(self, g_up, g_down):
        self.grads.update(w_up=g_up, w_down=g_down)
        return None

    def out_grad(self, g_out):
        self.grads.update(w_out=g_out)
        return None

    def in_grad(self, g_in):
        self.grads.update(w_in=g_in)
        return None


def _local_step(x, tgt, p, hooks):
    T = x.shape[0]
    D = D_MODEL
    h1 = _rmsnorm_fwd(x, p["mix_norm_g"], "norm_mix")
    w_in_p, w_out_f, conv_w_f = hooks.mixer_weights(h1)
    (proj,) = _mm_simple(h1, w_in_p, mode="nn", M=T, N=NP, K=D, tm=min(T, 1024), tn=768, tk=D, out_dtype=F32,
                         name="in_proj")
    xbc = _conv_fwd(proj, conv_w_f, p["conv_b"], "conv_fwd")
    dtT = proj[:, OFF_DT:OFF_DT + N_HEADS].T
    dtbT = p["dt_bias"].T
    alogT = p["A_log"].T
    dfull = jnp.repeat(p["D_skip"], HEAD_DIM, axis=1)
    y_ssm, ypre, hs = _ssd_fwd(xbc, proj, dtT, p["dt_bias"], dtbT, p["A_log"], alogT, dfull, p["ssm_norm_g"],
                               "ssd_fwd")
    kpad = jnp.pad(proj[:, OFF_K:OFF_K + KV_W], ((WINDOW, 0), (0, 0)))
    vpad = jnp.pad(proj[:, OFF_V:OFF_V + KV_W], ((WINDOW, 0), (0, 0)))
    y_att, o_att = _attn_fwd(proj, kpad, vpad, p["attn_sinks"], p["attn_out_norm_g"], "attn_fwd")
    ycat = jnp.concatenate([y_ssm, y_att], axis=1)
    tm = min(T, 1024)
    (x2,) = _mm_simple(ycat, w_out_f, mode="nn", M=T, N=D, K=D, tm=tm, tn=1024, tk=D, out_dtype=F32, name="out_proj",
                       extras=(x,), epilogue=lambda acc, res: (acc + res,))
    h2 = _rmsnorm_fwd(x2, p["mlp_norm_g"], "norm_mlp")
    w_up_s = hooks.up_weight(h2)
    grid = (T // tm, N_DEV, 1)
    u, act = _matmul(
        h2, w_up_s, mode="nn", grid=grid,
        a_spec=pl.BlockSpec((tm, D), lambda i, j, k: (i, 0)),
        b_spec=pl.BlockSpec((None, D, 1024), lambda i, j, k: (j, 0, 0)),
        out_shapes=[jax.ShapeDtypeStruct((T, D_FF), F32), jax.ShapeDtypeStruct((T, D_FF), BF16)],
        out_specs=[pl.BlockSpec((tm, 1024), lambda i, j, k: (i, j))] * 2, tile=(tm, 1024), name="mlp_up",
        epilogue=lambda acc: (acc, jnp.square(jnp.maximum(acc, 0.0))))
    w_down_f = hooks.down_weight(act)
    (x3,) = _mm_simple(act, w_down_f, mode="nn", M=T, N=D, K=D_FF, tm=tm, tn=1024, tk=1024, out_dtype=F32,
                       name="mlp_down", extras=(x2,), epilogue=lambda acc, res: (acc + res,))
    loss_part, d_fin, dx3, dx3b = _final_loss(x3, tgt, p["final_norm_g"].reshape(1, D), "loss_head")
    (g_down,) = _mm_simple(act, dx3b, mode="tn", M=D_FF, N=D, K=T, tm=1024, tn=1024, tk=T, out_dtype=BF16,
                           name="grad_w_down")
    (du,) = _mm_simple(dx3b, w_down_f, mode="nt", M=T, N=D_FF, K=D, tm=tm, tn=1024, tk=D, out_dtype=BF16,
                       name="mlp_down_bwd", extras=(u,),
                       epilogue=lambda acc, uu: (acc * (2.0 * jnp.maximum(uu, 0.0)),))
    (g_up,) = _matmul(
        h2, du, mode="tn", grid=(D // 1024, N_DEV, 1),
        a_spec=pl.BlockSpec((T, 1024), lambda i, j, k: (0, i)),
        b_spec=pl.BlockSpec((T, 1024), lambda i, j, k: (0, j)),
        out_shapes=[jax.ShapeDtypeStruct((N_DEV, D, 1024), BF16)],
        out_specs=[pl.BlockSpec((None, 1024, 1024), lambda i, j, k: (j, i, 0))], tile=(1024, 1024), name="grad_w_up")
    token = hooks.mlp_grads(g_up, g_down)
    (dh2,) = _matmul(
        du, w_up_s, mode="nt", grid=(T // tm, D // 1024, N_DEV),
        a_spec=pl.BlockSpec((tm, 1024), lambda i, j, k: (i, k)),
        b_spec=pl.BlockSpec((None, 1024, 1024), lambda i, j, k: (k, j, 0)),
        out_shapes=[jax.ShapeDtypeStruct((T, D), F32)],
        out_specs=[pl.BlockSpec((tm, 1024), lambda i, j, k: (i, j))], tile=(tm, 1024), name="mlp_up_bwd",
        after=token)
    dx2, dx2b, d_mlp = _rmsnorm_bwd(dh2, x2, p["mlp_norm_g"], dx3, "norm_mlp_bwd")
    (g_out,) = _mm_simple(ycat, dx2b, mode="tn", M=D, N=D, K=T, tm=1024, tn=1024, tk=T, out_dtype=BF16,
                          name="grad_w_out")
    hooks.out_grad(g_out)
    (dy,) = _mm_simple(dx2b, w_out_f, mode="nt", M=T, N=D, K=D, tm=tm, tn=1024, tk=D, out_dtype=F32,
                       name="out_proj_bwd")
    dz, dxbc_act, ddt, d_dtb, d_alog, d_dskip, d_ssmg = _ssd_bwd(
        xbc, proj, dtT, p["dt_bias"], dtbT, p["A_log"], alogT, dfull, p["ssm_norm_g"], ypre, hs, dy, "ssd_bwd")
    dq, dkpad, dvpad, d_sinks, d_attng = _attn_bwd(proj, kpad, vpad, p["attn_sinks"], p["attn_out_norm_g"], o_att, dy,
                                                   "attn_bwd")
    dxbc, d_convw, d_convb = _conv_bwd(proj, dxbc_act, conv_w_f, p["conv_b"], "conv_bwd")
    dproj = jnp.concatenate(
        [dz, dxbc, dq, dkpad[WINDOW:].astype(BF16), dvpad[WINDOW:].astype(BF16), ddt.astype(BF16),
         jnp.zeros((T, NP - OFF_DT - 128), BF16)], axis=1)
    (g_in,) = _mm_simple(h1, dproj, mode="tn", M=D, N=NP, K=T, tm=1024, tn=768, tk=T, out_dtype=BF16,
                         name="grad_w_in")
    token = hooks.in_grad(g_in)
    (dh1,) = _mm_simple(dproj, w_in_p, mode="nt", M=T, N=D, K=NP, tm=tm, tn=1024, tk=768, out_dtype=F32,
                        name="in_proj_bwd", after=token)
    dx, _, d_mix = _rmsnorm_bwd(dh1, x, p["mix_norm_g"], dx2, "norm_mix_bwd")
    small = _pack_small(d_mix, d_convb, d_ssmg, d_attng, d_mlp, d_fin, d_convw, d_dtb, d_alog, d_dskip, d_sinks,
                        extra=loss_part[:, 0:1])
    return dx, small


def _landing(n, like):
    return lax.empty((n,) + like.shape, like.dtype)


def _gather_end(started, after, name):
    srcs, lands = _remote_wait(started, after, _gather_plan, name + "_wait")
    return _gather_finish(lands, srcs, name + "_finish")


class _ShardedWeights:
    def __init__(self, mixer_shards, up_shard, down_shard, csel):
        self.csel = csel
        self.st_mixer = _remote_start(mixer_shards, [_landing(N_DEV, s) for s in mixer_shards], _gather_plan,
                                      4 * len(mixer_shards), "gather_start_mixer")
        self.st_up = _remote_start([up_shard], [_landing(N_DEV, up_shard)], _gather_plan, 4, "gather_start_up")
        self.st_down = _remote_start([down_shard], [_landing(N_DEV, down_shard)], _gather_plan, 4,
                                     "gather_start_down")
        self.start_token = self.st_mixer[4] + self.st_up[4] + self.st_down[4]
        self.reduces = {}
        self.g_out = None

    def mixer_weights(self, after):
        g_in, g_out, g_conv = _gather_end(self.st_mixer, after, "gather_mixer")
        w_in_p = _to_my_columns(jnp.transpose(g_in, (1, 0, 2)).reshape(D_MODEL, IN_PROJ))
        return w_in_p, g_out.reshape(D_MODEL, D_MODEL), jnp.transpose(g_conv, (1, 0, 2)).reshape(CONV_K, CONV_DIM)

    def up_weight(self, after):
        return _gather_end(self.st_up, after, "gather_up")[0]

    def down_weight(self, after):
        return _gather_end(self.st_down, after, "gather_down")[0].reshape(D_FF, D_MODEL)

    def _reduce_start(self, slabs, rows, tag):
        from_sibling = _exchange_pair(slabs, f"reduce_pair_{tag}")
        sums = [_pair_add(s, r, self.csel, tr, f"pair_add_{tag}_{i}")
                for i, (s, r, tr) in enumerate(zip(slabs, from_sibling, rows))]
        lands = [lax.empty((3,) + s.shape[1:], s.dtype) for s in sums]
        self.reduces[tag] = _remote_start(sums, lands, _chips_plan, 3 * len(sums), f"reduce_start_{tag}")
        return self.reduces[tag][4]

    def mlp_grads(self, g_up, g_down):
        return self._reduce_start([g_up, g_down.reshape(N_DEV, D_FF // N_DEV, D_MODEL)], [512, 256], "mlp")

    def out_grad(self, g_out):
        self.g_out = g_out

    def in_grad(self, g_in):
        per = IN_PROJ // N_DEV
        s_in = jnp.transpose(_to_natural_columns(g_in).reshape(D_MODEL, N_DEV, per), (1, 0, 2))
        s_out = self.g_out.reshape(N_DEV, D_MODEL // N_DEV, D_MODEL)
        return self._reduce_start([s_in, s_out], [256, 256], "mixer")

    def reduce_end(self, tag, after):
        return _remote_wait(self.reduces[tag], after, _chips_plan, f"reduce_wait_{tag}")


def kernel(x, mix_norm_g, w_in, conv_w, conv_b, dt_bias, A_log, D_skip, ssm_norm_g, attn_sinks, attn_out_norm_g, w_out, mlp_norm_g, w_up, w_down, final_norm_g, loss_target, m_mix_norm_g, m_w_in, m_conv_w, m_conv_b, m_dt_bias, m_A_log, m_D_skip, m_ssm_norm_g, m_attn_sinks, m_attn_out_norm_g, m_w_out, m_mlp_norm_g, m_w_up, m_w_down, m_final_norm_g, v_mix_norm_g, v_w_in, v_conv_w, v_conv_b, v_dt_bias, v_A_log, v_D_skip, v_ssm_norm_g, v_attn_sinks, v_attn_out_norm_g, v_w_out, v_mlp_norm_g, v_w_up, v_w_down, v_final_norm_g):
    xi, yi, ci = _coords()
    me = 4 * xi + 2 * yi + ci
    csel = jnp.reshape(ci, (1,)).astype(jnp.int32)
    qsel = jnp.reshape(2 * xi + yi, (1,)).astype(jnp.int32)
    w = dict(mix_norm_g=mix_norm_g, conv_b=conv_b, dt_bias=dt_bias, A_log=A_log, D_skip=D_skip,
             ssm_norm_g=ssm_norm_g, attn_sinks=attn_sinks, attn_out_norm_g=attn_out_norm_g, mlp_norm_g=mlp_norm_g,
             final_norm_g=final_norm_g)
    hooks = _ShardedWeights([w_in[0].astype(BF16), w_out[0].astype(BF16), conv_w[0]], w_up[0].astype(BF16),
                            w_down[0].astype(BF16), csel)
    p = dict(w, mix_norm_g=mix_norm_g + hooks.start_token[0:1, 0:1])
    dx, small = _local_step(x[0], loss_target[0], p, hooks)
    (all_small,) = _all_gather([small], "gather_small")
    gsum = _small_sum(all_small, "small_sum")
    big = {}
    after = gsum
    for tag, members in [("mlp", [("w_up", w_up, m_w_up, v_w_up, 512), ("w_down", w_down, m_w_down, v_w_down, 256)]),
                         ("mixer", [("w_in", w_in, m_w_in, v_w_in, 256), ("w_out", w_out, m_w_out, v_w_out, 256)])]:
        chip_sums, from_chips = hooks.reduce_end(tag, after)
        for i, (name, wt, mt, vt, tr) in enumerate(members):
            g, d, mn, vn = _adamw_big(wt[0], mt[0], vt[0], chip_sums[i], from_chips[i], qsel, tr, f"adamw_{name}")
            big[name] = (g[None], d[None], mn[None], vn[None])
            after = g
    loss = gsum[9, 64]
    gs = _unpack_small(gsum, CONV_DIM)
    cw = CONV_DIM // N_DEV
    g_conv_shard = lax.dynamic_slice(gsum[5:9, :], (0, me * cw), (CONV_K, cw))

    def pack(s):
        return _pack_small(s["mix_norm_g"], s["conv_b"], s["ssm_norm_g"], s["attn_out_norm_g"], s["mlp_norm_g"],
                           s["final_norm_g"], s["conv_w"][0], s["dt_bias"], s["A_log"], s["D_skip"], s["attn_sinks"])

    wp = pack(dict(w, conv_w=conv_w))
    mp = pack(dict(mix_norm_g=m_mix_norm_g, conv_b=m_conv_b, ssm_norm_g=m_ssm_norm_g,
                   attn_out_norm_g=m_attn_out_norm_g, mlp_norm_g=m_mlp_norm_g, final_norm_g=m_final_norm_g,
                   conv_w=m_conv_w, dt_bias=m_dt_bias, A_log=m_A_log, D_skip=m_D_skip, attn_sinks=m_attn_sinks))
    vp = pack(dict(mix_norm_g=v_mix_norm_g, conv_b=v_conv_b, ssm_norm_g=v_ssm_norm_g,
                   attn_out_norm_g=v_attn_out_norm_g, mlp_norm_g=v_mlp_norm_g, final_norm_g=v_final_norm_g,
                   conv_w=v_conv_w, dt_bias=v_dt_bias, A_log=v_A_log, D_skip=v_D_skip, attn_sinks=v_attn_sinks))
    gp = jnp.concatenate([gsum[0:5], jnp.pad(g_conv_shard, ((0, 0), (0, D_MODEL - cw))), gsum[9:10],
                          jnp.zeros((SMALL_ROWS - 10, D_MODEL), F32)], axis=0)
    dp, mnp, vnp = _adamw_small(wp, gp, mp, vp, "adamw_small")
    grads = dict(gs, conv_w=g_conv_shard[None])
    deltas = _unpack_small(dp, cw)
    new_m = _unpack_small(mnp, cw)
    new_v = _unpack_small(vnp, cw)
    for k, name in enumerate(["w_in", "w_out", "w_up", "w_down"]):
        grads[name], deltas[name], new_m[name], new_v[name] = big[name]
    return (loss, dx[None], *[grads[n] for n in WEIGHT_ORDER], *[deltas[n] for n in WEIGHT_ORDER],
            *[new_m[n] for n in WEIGHT_ORDER], *[new_v[n] for n in WEIGHT_ORDER])
```

```python
import functools

import jax
import jax.numpy as jnp
from jax import lax
from jax.experimental import pallas as pl
from jax.experimental.pallas import tpu as pltpu

F32 = jnp.float32
BF16 = jnp.bfloat16
HI = lax.Precision.HIGHEST
MESH = pl.DeviceIdType.MESH

EPS = 1e-5
D_MODEL = 2048
D_INNER = 1024
N_HEADS = 16
HEAD_DIM = 64
N_GROUPS = 4
D_STATE = 128
CHUNK = 128
CONV_K = 4
CONV_DIM = 2048
ATTN_W = 1024
KV_W = 128
WINDOW = 128
D_FF = 8192
IN_PROJ = 4368
N_DEV = 8
NP = 4608
OFF_Z, OFF_X, OFF_B, OFF_C, OFF_Q, OFF_K, OFF_V, OFF_DT = 0, 1024, 2048, 2560, 3072, 4096, 4224, 4352
NAT_DT = 3072

ADAM_LR = 0.001
ADAM_B1 = 0.9
ADAM_B2 = 0.999
ADAM_EPS = 1e-08
ADAM_WD = 0.01
ADAM_STEP = 10

VMEM_LIMIT = 52 * 1024 * 1024
SMALL_ROWS = 16
NEG = -1e30


def _cparams(sem=None):
    return pltpu.CompilerParams(dimension_semantics=sem, vmem_limit_bytes=VMEM_LIMIT)


def _hdot(a, b):
    return jnp.dot(a, b, precision=HI, preferred_element_type=F32)


def _dot_nn(a, b):
    return lax.dot_general(a, b, (((1,), (0,)), ((), ())), preferred_element_type=F32)


def _dot_nt(a, b):
    return lax.dot_general(a, b, (((1,), (1,)), ((), ())), preferred_element_type=F32)


def _dot_tn(a, b):
    return lax.dot_general(a, b, (((0,), (0,)), ((), ())), preferred_element_type=F32)


def _softplus(v):
    return jnp.maximum(v, 0.0) + jnp.log1p(jnp.exp(-jnp.abs(v)))


def _sigmoid(v):
    return 1.0 / (1.0 + jnp.exp(-v))


def _matmul(a, b, *, mode, grid, a_spec, b_spec, out_shapes, out_specs, tile, name,
            extras=(), extra_specs=(), epilogue=None, after=None):
    nk = grid[2]
    n_ex = len(extras)
    n_out = len(out_shapes)
    dot = {"nn": _dot_nn, "nt": _dot_nt, "tn": _dot_tn}[mode]

    def finish(acc, ex_refs, out_refs):
        res = (acc,) if epilogue is None else epilogue(acc, *[e[...] for e in ex_refs])
        for o, r in zip(out_refs, res):
            o[...] = r.astype(o.dtype)

    def body(*refs):
        a_ref, b_ref = refs[0], refs[1]
        ex_refs = refs[2:2 + n_ex]
        out_refs = refs[2 + n_ex:2 + n_ex + n_out]
        part = dot(a_ref[...].astype(BF16), b_ref[...].astype(BF16))
        if nk == 1:
            finish(part, ex_refs, out_refs)
        else:
            acc_ref = refs[-1]
            k = pl.program_id(2)

            @pl.when(k == 0)
            def _():
                acc_ref[...] = part

            @pl.when(k > 0)
            def _():
                acc_ref[...] += part

            @pl.when(k == nk - 1)
            def _():
                finish(acc_ref[...], ex_refs, out_refs)

    scratch = [] if nk == 1 else [pltpu.VMEM(tile, F32)]
    tok_specs = [] if after is None else [pl.BlockSpec((8, 128), lambda i, j, k: (0, 0))]
    tok_args = [] if after is None else [after]
    n_out = len(out_shapes)

    def body_with_token(*refs):
        body(*refs[:2 + n_ex], *refs[2 + n_ex + len(tok_args):])

    outs = pl.pallas_call(
        body_with_token, grid=grid, in_specs=[a_spec, b_spec, *extra_specs, *tok_specs], out_specs=list(out_specs),
        out_shape=list(out_shapes), scratch_shapes=scratch, name=name,
        compiler_params=_cparams(("parallel", "parallel", "arbitrary")),
    )(a, b, *extras, *tok_args)
    return outs


def _mm_simple(a, b, *, mode, M, N, K, tm, tn, tk, out_dtype, name, extras=(), epilogue=None, n_out=1,
               out_dtypes=None, after=None):
    grid = (M // tm, N // tn, K // tk)
    if mode == "nn":
        a_spec = pl.BlockSpec((tm, tk), lambda i, j, k: (i, k))
        b_spec = pl.BlockSpec((tk, tn), lambda i, j, k: (k, j))
    elif mode == "nt":
        a_spec = pl.BlockSpec((tm, tk), lambda i, j, k: (i, k))
        b_spec = pl.BlockSpec((tn, tk), lambda i, j, k: (j, k))
    else:
        a_spec = pl.BlockSpec((tk, tm), lambda i, j, k: (k, i))
        b_spec = pl.BlockSpec((tk, tn), lambda i, j, k: (k, j))
    o_spec = pl.BlockSpec((tm, tn), lambda i, j, k: (i, j))
    dts = out_dtypes if out_dtypes is not None else [out_dtype] * n_out
    return _matmul(a, b, mode=mode, grid=grid, a_spec=a_spec, b_spec=b_spec,
                   out_shapes=[jax.ShapeDtypeStruct((M, N), d) for d in dts],
                   out_specs=[o_spec] * len(dts), tile=(tm, tn), name=name,
                   extras=extras, extra_specs=[o_spec] * len(extras), epilogue=epilogue, after=after)


ROW_BLOCK = 256


def _rmsnorm_fwd(x, g, name):
    T, D = x.shape

    def body(x_ref, g_ref, o_ref):
        xf = x_ref[...]
        r = lax.rsqrt(jnp.mean(xf * xf, axis=-1, keepdims=True) + EPS)
        o_ref[...] = (xf * r * g_ref[...]).astype(BF16)

    return pl.pallas_call(
        body, grid=(T // ROW_BLOCK,),
        in_specs=[pl.BlockSpec((ROW_BLOCK, D), lambda i: (i, 0)), pl.BlockSpec((1, D), lambda i: (0, 0))],
        out_specs=pl.BlockSpec((ROW_BLOCK, D), lambda i: (i, 0)),
        out_shape=jax.ShapeDtypeStruct((T, D), BF16), name=name, compiler_params=_cparams(("parallel",)),
    )(x, g)


def _rmsnorm_bwd(dh, x, g, dres, name):
    T, D = x.shape

    def body(dh_ref, x_ref, g_ref, dres_ref, dx_ref, dxb_ref, dg_ref):
        i = pl.program_id(0)
        xf = x_ref[...]
        r = lax.rsqrt(jnp.mean(xf * xf, axis=-1, keepdims=True) + EPS)
        xh = xf * r
        d = dh_ref[...]

        @pl.when(i == 0)
        def _():
            dg_ref[...] = jnp.zeros_like(dg_ref)

        dg_ref[...] += jnp.sum(d * xh, axis=0, keepdims=True)
        dxh = d * g_ref[...]
        dx = r * (dxh - xh * jnp.mean(dxh * xh, axis=-1, keepdims=True)) + dres_ref[...]
        dx_ref[...] = dx
        dxb_ref[...] = dx.astype(BF16)

    row = pl.BlockSpec((ROW_BLOCK, D), lambda i: (i, 0))
    vec = pl.BlockSpec((1, D), lambda i: (0, 0))
    return pl.pallas_call(
        body, grid=(T // ROW_BLOCK,), in_specs=[row, row, vec, row], out_specs=[row, row, vec],
        out_shape=[jax.ShapeDtypeStruct((T, D), F32), jax.ShapeDtypeStruct((T, D), BF16),
                   jax.ShapeDtypeStruct((1, D), F32)],
        name=name, compiler_params=_cparams(("arbitrary",)),
    )(dh, x, g, dres)


def _final_loss(x3, tgt, g, name):
    T, D = x3.shape

    def body(x_ref, t_ref, g_ref, loss_ref, dg_ref, dx_ref, dxb_ref):
        i = pl.program_id(0)
        xf = x_ref[...]
        r = lax.rsqrt(jnp.mean(xf * xf, axis=-1, keepdims=True) + EPS)
        xh = xf * r
        gg = g_ref[...]
        err = xh * gg - t_ref[...]

        @pl.when(i == 0)
        def _():
            dg_ref[...] = jnp.zeros_like(dg_ref)
            loss_ref[...] = jnp.zeros_like(loss_ref)

        part = jnp.sum(jnp.sum(err * err, axis=-1, keepdims=True), axis=0, keepdims=True) * (0.5 / D)
        loss_ref[...] += jnp.broadcast_to(part, loss_ref.shape)
        dout = err * (1.0 / D)
        dg_ref[...] += jnp.sum(dout * xh, axis=0, keepdims=True)
        dxh = dout * gg
        dx = r * (dxh - xh * jnp.mean(dxh * xh, axis=-1, keepdims=True))
        dx_ref[...] = dx
        dxb_ref[...] = dx.astype(BF16)

    row = pl.BlockSpec((ROW_BLOCK, D), lambda i: (i, 0))
    vec = pl.BlockSpec((1, D), lambda i: (0, 0))
    return pl.pallas_call(
        body, grid=(T // ROW_BLOCK,), in_specs=[row, row, vec],
        out_specs=[pl.BlockSpec((1, 128), lambda i: (0, 0)), vec, row, row],
        out_shape=[jax.ShapeDtypeStruct((1, 128), F32), jax.ShapeDtypeStruct((1, D), F32),
                   jax.ShapeDtypeStruct((T, D), F32), jax.ShapeDtypeStruct((T, D), BF16)],
        name=name, compiler_params=_cparams(("arbitrary",)),
    )(x3, tgt, g)


CONV_BLOCK = 256


def _conv_apply(u, w, b):
    row = lax.broadcasted_iota(jnp.int32, u.shape, 0)
    acc = b + w[CONV_K - 1:CONV_K, :] * u
    shifted = []
    for j in range(1, CONV_K):
        uj = jnp.where(row >= j, pltpu.roll(u, j, axis=0), 0.0)
        shifted.append(uj)
        acc = acc + w[CONV_K - 1 - j:CONV_K - j, :] * uj
    return acc, shifted


def _conv_fwd(proj, conv_w, conv_b, name):
    T = proj.shape[0]
    cb0 = OFF_X // CONV_BLOCK

    def body(u_ref, w_ref, b_ref, o_ref):
        c, _ = _conv_apply(u_ref[...], w_ref[...], b_ref[...])
        o_ref[...] = c * _sigmoid(c)

    return pl.pallas_call(
        body, grid=(CONV_DIM // CONV_BLOCK,),
        in_specs=[pl.BlockSpec((T, CONV_BLOCK), lambda j: (0, cb0 + j)),
                  pl.BlockSpec((CONV_K, CONV_BLOCK), lambda j: (0, j)),
                  pl.BlockSpec((1, CONV_BLOCK), lambda j: (0, j))],
        out_specs=pl.BlockSpec((T, CONV_BLOCK), lambda j: (0, j)),
        out_shape=jax.ShapeDtypeStruct((T, CONV_DIM), F32), name=name, compiler_params=_cparams(("parallel",)),
    )(proj, conv_w, conv_b)


def _conv_bwd(proj, dact, conv_w, conv_b, name):
    T = proj.shape[0]
    cb0 = OFF_X // CONV_BLOCK

    def body(u_ref, d_ref, w_ref, b_ref, du_ref, dw_ref, db_ref):
        u = u_ref[...]
        w = w_ref[...]
        c, shifted = _conv_apply(u, w, b_ref[...])
        sg = _sigmoid(c)
        dc = d_ref[...] * sg * (1.0 + c * (1.0 - sg))
        row = lax.broadcasted_iota(jnp.int32, u.shape, 0)
        du = w[CONV_K - 1:CONV_K, :] * dc
        dw_ref[CONV_K - 1:CONV_K, :] = jnp.sum(dc * u, axis=0, keepdims=True)
        for j in range(1, CONV_K):
            dcj = jnp.where(row < T - j, pltpu.roll(dc, T - j, axis=0), 0.0)
            du = du + w[CONV_K - 1 - j:CONV_K - j, :] * dcj
            dw_ref[CONV_K - 1 - j:CONV_K - j, :] = jnp.sum(dc * shifted[j - 1], axis=0, keepdims=True)
        db_ref[...] = jnp.sum(dc, axis=0, keepdims=True)
        du_ref[...] = du.astype(BF16)

    return pl.pallas_call(
        body, grid=(CONV_DIM // CONV_BLOCK,),
        in_specs=[pl.BlockSpec((T, CONV_BLOCK), lambda j: (0, cb0 + j)),
                  pl.BlockSpec((T, CONV_BLOCK), lambda j: (0, j)),
                  pl.BlockSpec((CONV_K, CONV_BLOCK), lambda j: (0, j)),
                  pl.BlockSpec((1, CONV_BLOCK), lambda j: (0, j))],
        out_specs=[pl.BlockSpec((T, CONV_BLOCK), lambda j: (0, j)),
                   pl.BlockSpec((CONV_K, CONV_BLOCK), lambda j: (0, j)),
                   pl.BlockSpec((1, CONV_BLOCK), lambda j: (0, j))],
        out_shape=[jax.ShapeDtypeStruct((T, CONV_DIM), BF16), jax.ShapeDtypeStruct((CONV_K, CONV_DIM), F32),
                   jax.ShapeDtypeStruct((1, CONV_DIM), F32)],
        name=name, compiler_params=_cparams(("parallel",)),
    )(proj, dact, conv_w, conv_b)


GROUP_W = D_INNER // N_GROUPS
HEADS_PER_GROUP = N_HEADS // N_GROUPS


def _expand_mat():
    h = lax.broadcasted_iota(jnp.int32, (N_HEADS, D_INNER), 0)
    j = lax.broadcasted_iota(jnp.int32, (N_HEADS, D_INNER), 1)
    return (j // HEAD_DIM == h).astype(F32)


def _reduce_mat(g):
    j = lax.broadcasted_iota(jnp.int32, (GROUP_W, N_HEADS), 0)
    h = lax.broadcasted_iota(jnp.int32, (GROUP_W, N_HEADS), 1)
    return (g * HEADS_PER_GROUP + j // HEAD_DIM == h).astype(F32)


def _col16(v, h):
    lane = lax.broadcasted_iota(jnp.int32, v.shape, 1)
    return jnp.sum(jnp.where(lane == h, v, 0.0), axis=1, keepdims=True)


def _ssd_pre(dt_raw, dtT_raw, dtb, dtbT, alog, alogT):
    Q = CHUNK
    xdt = dt_raw + dtb
    dt = _softplus(xdt)
    dtT = _softplus(dtT_raw + dtbT)
    A = -jnp.exp(alog)
    AT = -jnp.exp(alogT)
    row = lax.broadcasted_iota(jnp.int32, (Q, Q), 0)
    col = lax.broadcasted_iota(jnp.int32, (Q, Q), 1)
    tril = (row >= col).astype(F32)
    triu = (row <= col).astype(F32)
    cs = _hdot(tril, dt * A)
    csT = _hdot(dtT * AT, triu)
    return xdt, dt, A, cs, csT, row >= col, triu


def _decay_matrix(cs, csT, h, causal):
    seg = _col16(cs, h) - csT[h:h + 1, :]
    return jnp.where(causal, jnp.exp(jnp.minimum(seg, 0.0)), 0.0)


def _ssd_in_specs(nc, rev):
    def cidx(c):
        return (nc - 1 - c) if rev else c

    return [
        pl.BlockSpec((CHUNK, D_INNER), lambda c: (cidx(c), 0)),
        pl.BlockSpec((CHUNK, 512), lambda c: (cidx(c), 2)),
        pl.BlockSpec((CHUNK, 512), lambda c: (cidx(c), 3)),
        pl.BlockSpec((CHUNK, D_INNER), lambda c: (cidx(c), 0)),
        pl.BlockSpec((CHUNK, 128), lambda c: (cidx(c), OFF_DT // 128)),
        pl.BlockSpec((N_HEADS, CHUNK), lambda c: (0, cidx(c))),
        pl.BlockSpec((1, N_HEADS), lambda c: (0, 0)),
        pl.BlockSpec((N_HEADS, 1), lambda c: (0, 0)),
        pl.BlockSpec((1, N_HEADS), lambda c: (0, 0)),
        pl.BlockSpec((N_HEADS, 1), lambda c: (0, 0)),
        pl.BlockSpec((1, D_INNER), lambda c: (0, 0)),
        pl.BlockSpec((1, D_INNER), lambda c: (0, 0)),
    ]


def _ssd_fwd(xbc, proj, dtT, dtb, dtbT, alog, alogT, dfull, ng, name):
    T = xbc.shape[0]
    nc = T // CHUNK
    Q = CHUNK

    def body(xs_ref, B_ref, C_ref, z_ref, dt_ref, dtT_ref, dtb_ref, dtbT_ref, al_ref, alT_ref, df_ref, ng_ref,
             y_ref, ypre_ref, hs_ref, h_scr):
        c = pl.program_id(0)

        @pl.when(c == 0)
        def _():
            h_scr[...] = jnp.zeros_like(h_scr)

        _, dt, _, cs, csT, causal, _ = _ssd_pre(dt_ref[:, :N_HEADS], dtT_ref[...], dtb_ref[...], dtbT_ref[...],
                                                al_ref[...], alT_ref[...])
        ex = _expand_mat()
        dt_full = _hdot(dt, ex)
        cs_full = _hdot(cs, ex)
        cs_last = cs_full[Q - 1:Q, :]
        xs = xs_ref[...]
        xd = xs * dt_full
        e_full = jnp.exp(cs_full)
        dec_full = jnp.exp(cs_last - cs_full)
        cd_full = jnp.exp(cs_last)
        lane_head = lax.broadcasted_iota(jnp.int32, (1, GROUP_W), 1) // HEAD_DIM
        for g in range(N_GROUPS):
            sl = slice(g * GROUP_W, (g + 1) * GROUP_W)
            Bg = B_ref[:, g * D_STATE:(g + 1) * D_STATE].astype(BF16)
            Cg = C_ref[:, g * D_STATE:(g + 1) * D_STATE].astype(BF16)
            CB = _dot_nt(Cg, Bg)
            hg = h_scr[g]
            yoff = _dot_nn(Cg, hg.astype(BF16)) * e_full[:, sl]
            xd_g = xd[:, sl]
            S = _dot_tn(Bg, (xd_g * dec_full[:, sl]).astype(BF16))
            xd_b = xd_g.astype(BF16)
            ydiag = jnp.zeros((Q, GROUP_W), F32)
            for r in range(HEADS_PER_GROUP):
                Lm = _decay_matrix(cs, csT, g * HEADS_PER_GROUP + r, causal)
                Gm = (CB * Lm).astype(BF16)
                ydiag = ydiag + _dot_nn(Gm, jnp.where(lane_head == r, xd_b, jnp.zeros_like(xd_b)))
            hs_ref[0, g] = hg
            h_scr[g] = hg * cd_full[:, sl] + S
            ypre = ydiag + yoff + xs[:, sl] * df_ref[:, sl]
            ypre_ref[:, sl] = ypre
            zg = z_ref[:, sl]
            yz = ypre * zg * _sigmoid(zg)
            rn = lax.rsqrt(jnp.mean(yz * yz, axis=-1, keepdims=True) + EPS)
            y_ref[:, sl] = (yz * rn * ng_ref[:, sl]).astype(BF16)

    return pl.pallas_call(
        body, grid=(nc,), in_specs=_ssd_in_specs(nc, False),
        out_specs=[pl.BlockSpec((CHUNK, D_INNER), lambda c: (c, 0)),
                   pl.BlockSpec((CHUNK, D_INNER), lambda c: (c, 0)),
                   pl.BlockSpec((1, N_GROUPS, D_STATE, GROUP_W), lambda c: (c, 0, 0, 0))],
        out_shape=[jax.ShapeDtypeStruct((T, D_INNER), BF16), jax.ShapeDtypeStruct((T, D_INNER), F32),
                   jax.ShapeDtypeStruct((nc, N_GROUPS, D_STATE, GROUP_W), F32)],
        scratch_shapes=[pltpu.VMEM((N_GROUPS, D_STATE, GROUP_W), F32)],
        name=name, compiler_params=_cparams(("arbitrary",)),
    )(xbc, xbc, xbc, proj, proj, dtT, dtb, dtbT, alog, alogT, dfull, ng)


def _ssd_bwd(xbc, proj, dtT, dtb, dtbT, alog, alogT, dfull, ng, ypre, hs, dy, name):
    T = xbc.shape[0]
    nc = T // CHUNK
    Q = CHUNK

    def body(xs_ref, B_ref, C_ref, z_ref, dt_ref, dtT_ref, dtb_ref, dtbT_ref, al_ref, alT_ref, df_ref, ng_ref,
             ypre_ref, hs_ref, dy_ref,
             dz_ref, dxbc_ref, ddt_ref, ddtb_ref, dal_ref, dD_ref, dng_ref, dh_scr):
        step = pl.program_id(0)

        @pl.when(step == 0)
        def _():
            dh_scr[...] = jnp.zeros_like(dh_scr)
            ddtb_ref[...] = jnp.zeros_like(ddtb_ref)
            dal_ref[...] = jnp.zeros_like(dal_ref)
            dD_ref[...] = jnp.zeros_like(dD_ref)
            dng_ref[...] = jnp.zeros_like(dng_ref)

        xdt, dt, A, cs, csT, causal, triu = _ssd_pre(dt_ref[:, :N_HEADS], dtT_ref[...], dtb_ref[...],
                                                    dtbT_ref[...], al_ref[...], alT_ref[...])
        ex = _expand_mat()
        dt_full = _hdot(dt, ex)
        cs_full = _hdot(cs, ex)
        cs_last = cs_full[Q - 1:Q, :]
        xs = xs_ref[...]
        xd = xs * dt_full
        e_full = jnp.exp(cs_full)
        dec_full = jnp.exp(cs_last - cs_full)
        cd_full = jnp.exp(cs_last)
        lane_head = lax.broadcasted_iota(jnp.int32, (1, GROUP_W), 1) // HEAD_DIM
        is_last = lax.broadcasted_iota(jnp.int32, (Q, 1), 0) == Q - 1
        dcs16 = jnp.zeros((Q, N_HEADS), F32)
        ddtx16 = jnp.zeros((Q, N_HEADS), F32)
        dD16 = jnp.zeros((8, N_HEADS), F32)
        lane16 = lax.broadcasted_iota(jnp.int32, (1, N_HEADS), 1)
        sub16 = lax.broadcasted_iota(jnp.int32, (N_HEADS, 1), 0)
        col_sums = jnp.zeros((N_HEADS, Q), F32)
        for g in range(N_GROUPS):
            sl = slice(g * GROUP_W, (g + 1) * GROUP_W)
            red = _reduce_mat(g)
            ypre_g = ypre_ref[:, sl]
            zg = z_ref[:, sl]
            sg = _sigmoid(zg)
            silu = zg * sg
            yz = ypre_g * silu
            rn = lax.rsqrt(jnp.mean(yz * yz, axis=-1, keepdims=True) + EPS)
            yh = yz * rn
            dy_g = dy_ref[:, sl]
            dng_ref[:, sl] += jnp.sum(dy_g * yh, axis=0, keepdims=True)
            dyh = dy_g * ng_ref[:, sl]
            dyz = rn * (dyh - yh * jnp.mean(dyh * yh, axis=-1, keepdims=True))
            dY = dyz * silu
            dz_ref[:, sl] = (dyz * ypre_g * sg * (1.0 + zg * (1.0 - sg))).astype(BF16)
            xs_g = xs[:, sl]
            xd_g = xd[:, sl]
            dec_g = dec_full[:, sl]
            cd_g = cd_full[:, sl]
            d_g = df_ref[:, sl]
            Bg = B_ref[:, g * D_STATE:(g + 1) * D_STATE].astype(BF16)
            Cg = C_ref[:, g * D_STATE:(g + 1) * D_STATE].astype(BF16)
            CB = _dot_nt(Cg, Bg)
            hg = hs_ref[0, g]
            hgb = hg.astype(BF16)
            yoff = _dot_nn(Cg, hgb) * e_full[:, sl]
            dhn = dh_scr[g]
            dhnb = dhn.astype(BF16)
            dYE = (dY * e_full[:, sl]).astype(BF16)
            dC = _dot_nt(dYE, hgb)
            dh_direct = _dot_tn(Cg, dYE)
            dXdd = _dot_nn(Bg, dhnb)
            dB = _dot_nt((xd_g * dec_g).astype(BF16), dhnb)
            dcd = jnp.sum(dhn * hg, axis=0, keepdims=True)
            dh_scr[g] = dh_direct + cd_g * dhn
            dYb = dY.astype(BF16)
            xd_b = xd_g.astype(BF16)
            dCB = jnp.zeros((Q, Q), F32)
            dXd = dXdd * dec_g
            for r in range(HEADS_PER_GROUP):
                h = g * HEADS_PER_GROUP + r
                Lm = _decay_matrix(cs, csT, h, causal)
                Gf = CB * Lm
                dYr = jnp.where(lane_head == r, dYb, jnp.zeros_like(dYb))
                dG = _dot_nt(dYr, xd_b)
                dCB = dCB + dG * Lm
                dXd = dXd + _dot_tn(Gf.astype(BF16), dYr)
                Mm = dG * Gf
                dcs16 = dcs16 + jnp.where(lane16 == h, jnp.sum(Mm, axis=1, keepdims=True), 0.0)
                col_sums = col_sums + jnp.where(sub16 == h, jnp.sum(Mm, axis=0, keepdims=True), 0.0)
            dCBb = dCB.astype(BF16)
            dC = dC + _dot_nn(dCBb, Bg)
            dB = dB + _dot_tn(dCBb, Cg)
            w_state = dXdd * dec_g * xd_g
            t_last = jnp.sum(w_state, axis=0, keepdims=True) + dcd * cd_g
            dcs_g = dY * yoff - w_state + jnp.where(is_last, t_last, 0.0)
            dcs16 = dcs16 + _hdot(dcs_g, red)
            ddtx16 = ddtx16 + _hdot(dXd * xs_g, red)
            dD16 = dD16 + _hdot(jnp.broadcast_to(jnp.sum(dY * xs_g, axis=0, keepdims=True), (8, GROUP_W)), red)
            dxbc_ref[:, sl] = dXd * dt_full[:, sl] + dY * d_g
            dxbc_ref[:, D_INNER + g * D_STATE:D_INNER + (g + 1) * D_STATE] = dB
            dxbc_ref[:, D_INNER + 512 + g * D_STATE:D_INNER + 512 + (g + 1) * D_STATE] = dC
        eye = (lax.broadcasted_iota(jnp.int32, (N_HEADS, N_HEADS), 0)
               == lax.broadcasted_iota(jnp.int32, (N_HEADS, N_HEADS), 1)).astype(F32)
        dcs16 = dcs16 - lax.dot_general(col_sums, eye, (((0,), (0,)), ((), ())), precision=HI,
                                        preferred_element_type=F32)
        da = _hdot(triu, dcs16)
        ddt = da * A + ddtx16
        ddt_raw = ddt * _sigmoid(xdt)
        pr = lax.broadcasted_iota(jnp.int32, (N_HEADS, 128), 0)
        pc = lax.broadcasted_iota(jnp.int32, (N_HEADS, 128), 1)
        ddt_ref[...] = _hdot(ddt_raw, (pr == pc).astype(F32))
        ddtb_ref[...] += jnp.sum(ddt_raw, axis=0, keepdims=True)
        dal_ref[...] += jnp.sum(da * dt, axis=0, keepdims=True) * A
        dD_ref[...] += dD16[0:1, :]

    def rc(c):
        return nc - 1 - c

    in_specs = _ssd_in_specs(nc, True) + [
        pl.BlockSpec((CHUNK, D_INNER), lambda c: (rc(c), 0)),
        pl.BlockSpec((1, N_GROUPS, D_STATE, GROUP_W), lambda c: (rc(c), 0, 0, 0)),
        pl.BlockSpec((CHUNK, D_INNER), lambda c: (rc(c), 0)),
    ]
    small = pl.BlockSpec((1, N_HEADS), lambda c: (0, 0))
    return pl.pallas_call(
        body, grid=(nc,), in_specs=in_specs,
        out_specs=[pl.BlockSpec((CHUNK, D_INNER), lambda c: (rc(c), 0)),
                   pl.BlockSpec((CHUNK, CONV_DIM), lambda c: (rc(c), 0)),
                   pl.BlockSpec((CHUNK, 128), lambda c: (rc(c), 0)),
                   small, small, small,
                   pl.BlockSpec((1, D_INNER), lambda c: (0, 0))],
        out_shape=[jax.ShapeDtypeStruct((T, D_INNER), BF16), jax.ShapeDtypeStruct((T, CONV_DIM), F32),
                   jax.ShapeDtypeStruct((T, 128), F32),
                   jax.ShapeDtypeStruct((1, N_HEADS), F32), jax.ShapeDtypeStruct((1, N_HEADS), F32),
                   jax.ShapeDtypeStruct((1, N_HEADS), F32), jax.ShapeDtypeStruct((1, D_INNER), F32)],
        scratch_shapes=[pltpu.VMEM((N_GROUPS, D_STATE, GROUP_W), F32)],
        name=name, compiler_params=_cparams(("arbitrary",)),
    )(xbc, xbc, xbc, proj, proj, dtT, dtb, dtbT, alog, alogT, dfull, ng, ypre, hs, dy)


N_PAIRS = ATTN_W // 128
PAIRS_PER_KV = N_PAIRS // 2
ATTN_SCALE = HEAD_DIM ** -0.5


def _kv_variants(kk):
    lo = lax.broadcasted_iota(jnp.int32, kk.shape, 1) < HEAD_DIM
    zero = jnp.zeros_like(kk)
    k00 = jnp.where(lo, kk, zero)
    k11 = jnp.where(lo, zero, kk)
    k01 = pltpu.roll(k00, HEAD_DIM, axis=1)
    k10 = pltpu.roll(k11, HEAD_DIM, axis=1)
    return [[k00.astype(BF16), k01.astype(BF16)], [k10.astype(BF16), k11.astype(BF16)]]


def _attn_valid(n):
    i = lax.broadcasted_iota(jnp.int32, (WINDOW, 2 * WINDOW), 0)
    j = lax.broadcasted_iota(jnp.int32, (WINDOW, 2 * WINDOW), 1)
    return (j > i) & (j <= i + WINDOW) & (n * WINDOW + j >= WINDOW)


def _attn_probs(qp, kvar, valid, sk):
    s = _dot_nt(qp, kvar) * ATTN_SCALE
    s = jnp.where(valid, s, NEG)
    m = jnp.maximum(jnp.max(s, axis=1, keepdims=True), sk)
    pe = jnp.exp(s - m)
    es = jnp.exp(sk - m)
    den = jnp.sum(pe, axis=1, keepdims=True) + es
    inv = 1.0 / den
    return pe * inv, es * inv


def _sink(sinks, r):
    lane = lax.broadcasted_iota(jnp.int32, sinks.shape, 1)
    return jnp.sum(jnp.where(lane == r, sinks, 0.0), axis=1, keepdims=True)


def _attn_fwd(proj, kpad, vpad, sinks, og, name):
    T = proj.shape[0]
    nb = T // WINDOW

    def body(q_ref, k_ref, v_ref, s_ref, og_ref, y_ref, o_ref):
        n = pl.program_id(0)
        start = pl.multiple_of(n * WINDOW, WINDOW)
        kv = _kv_variants(k_ref[pl.ds(start, 2 * WINDOW), :])
        vv = _kv_variants(v_ref[pl.ds(start, 2 * WINDOW), :])
        valid = _attn_valid(n)
        sinks_v = s_ref[...]
        ssq = jnp.zeros((WINDOW, 1), F32)
        for p in range(N_PAIRS):
            j = p // PAIRS_PER_KV
            qp = q_ref[:, p * 128:(p + 1) * 128].astype(BF16)
            o_pair = jnp.zeros((WINDOW, 128), F32)
            for par in range(2):
                pn, _ = _attn_probs(qp, kv[j][par], valid, _sink(sinks_v, 2 * p + par))
                o_pair = o_pair + _dot_nn(pn.astype(BF16), vv[j][par])
            o_ref[:, p * 128:(p + 1) * 128] = o_pair
            ssq = ssq + jnp.sum(o_pair * o_pair, axis=1, keepdims=True)
        rn = lax.rsqrt(ssq * (1.0 / ATTN_W) + EPS)
        y_ref[...] = (o_ref[...] * rn * og_ref[...]).astype(BF16)

    full_kv = pl.BlockSpec((T + WINDOW, KV_W), lambda n: (0, 0))
    return pl.pallas_call(
        body, grid=(nb,),
        in_specs=[pl.BlockSpec((WINDOW, ATTN_W), lambda n: (n, OFF_Q // ATTN_W)), full_kv, full_kv,
                  pl.BlockSpec((1, N_HEADS), lambda n: (0, 0)), pl.BlockSpec((1, ATTN_W), lambda n: (0, 0))],
        out_specs=[pl.BlockSpec((WINDOW, ATTN_W), lambda n: (n, 0)), pl.BlockSpec((WINDOW, ATTN_W), lambda n: (n, 0))],
        out_shape=[jax.ShapeDtypeStruct((T, ATTN_W), BF16), jax.ShapeDtypeStruct((T, ATTN_W), F32)],
        name=name, compiler_params=_cparams(("parallel",)),
    )(proj, kpad, vpad, sinks, og)


def _attn_bwd(proj, kpad, vpad, sinks, og, o, dy, name):
    T = proj.shape[0]
    nb = T // WINDOW

    def body(q_ref, k_ref, v_ref, s_ref, og_ref, o_ref, dy_ref, dq_ref, dk_ref, dv_ref, ds_ref, dog_ref):
        n = pl.program_id(0)

        @pl.when(n == 0)
        def _():
            dk_ref[...] = jnp.zeros_like(dk_ref)
            dv_ref[...] = jnp.zeros_like(dv_ref)
            ds_ref[...] = jnp.zeros_like(ds_ref)
            dog_ref[...] = jnp.zeros_like(dog_ref)

        start = pl.multiple_of(n * WINDOW, WINDOW)
        kv = _kv_variants(k_ref[pl.ds(start, 2 * WINDOW), :])
        vv = _kv_variants(v_ref[pl.ds(start, 2 * WINDOW), :])
        valid = _attn_valid(n)
        sinks_v = s_ref[...]
        of = o_ref[...]
        rn = lax.rsqrt(jnp.mean(of * of, axis=-1, keepdims=True) + EPS)
        oh = of * rn
        dyf = dy_ref[...]
        dog_ref[...] += jnp.sum(dyf * oh, axis=0, keepdims=True)
        doh = dyf * og_ref[...]
        do = rn * (doh - oh * jnp.mean(doh * oh, axis=-1, keepdims=True))
        lane = lax.broadcasted_iota(jnp.int32, (1, 128), 1)
        lane16 = lax.broadcasted_iota(jnp.int32, (1, N_HEADS), 1)
        dk_acc = [[jnp.zeros((2 * WINDOW, 128), F32) for _ in range(2)] for _ in range(2)]
        dv_acc = [[jnp.zeros((2 * WINDOW, 128), F32) for _ in range(2)] for _ in range(2)]
        dsink = jnp.zeros((1, N_HEADS), F32)
        for p in range(N_PAIRS):
            j = p // PAIRS_PER_KV
            qp = q_ref[:, p * 128:(p + 1) * 128].astype(BF16)
            do_p = do[:, p * 128:(p + 1) * 128]
            o_p = of[:, p * 128:(p + 1) * 128]
            do_b = do_p.astype(BF16)
            prod = do_p * o_p
            dq_pair = jnp.zeros((WINDOW, 128), F32)
            for par in range(2):
                r = 2 * p + par
                half = (lane < HEAD_DIM) if par == 0 else (lane >= HEAD_DIM)
                pn, ps = _attn_probs(qp, kv[j][par], valid, _sink(sinks_v, r))
                delta = jnp.sum(jnp.where(half, prod, 0.0), axis=1, keepdims=True)
                dP = _dot_nt(do_b, vv[j][par])
                dS = pn * (dP - delta)
                dsink = dsink + jnp.where(lane16 == r, -jnp.sum(ps * delta, axis=0, keepdims=True), 0.0)
                dSb = (dS * ATTN_SCALE).astype(BF16)
                dq_pair = dq_pair + _dot_nn(dSb, kv[j][par])
                dk_acc[j][par] = dk_acc[j][par] + _dot_tn(dSb, jnp.where(half, qp, jnp.zeros_like(qp)))
                dv_acc[j][par] = dv_acc[j][par] + _dot_tn(pn.astype(BF16), jnp.where(half, do_b, jnp.zeros_like(do_b)))
            dq_ref[:, p * 128:(p + 1) * 128] = dq_pair.astype(BF16)
        dkk = (dk_acc[0][0] + pltpu.roll(dk_acc[0][1], HEAD_DIM, axis=1)
               + dk_acc[1][1] + pltpu.roll(dk_acc[1][0], HEAD_DIM, axis=1))
        dvv = (dv_acc[0][0] + pltpu.roll(dv_acc[0][1], HEAD_DIM, axis=1)
               + dv_acc[1][1] + pltpu.roll(dv_acc[1][0], HEAD_DIM, axis=1))
        dk_ref[pl.ds(start, 2 * WINDOW), :] += dkk
        dv_ref[pl.ds(start, 2 * WINDOW), :] += dvv
        ds_ref[...] += dsink

    full_kv = pl.BlockSpec((T + WINDOW, KV_W), lambda n: (0, 0))
    blk = pl.BlockSpec((WINDOW, ATTN_W), lambda n: (n, 0))
    return pl.pallas_call(
        body, grid=(nb,),
        in_specs=[pl.BlockSpec((WINDOW, ATTN_W), lambda n: (n, OFF_Q // ATTN_W)), full_kv, full_kv,
                  pl.BlockSpec((1, N_HEADS), lambda n: (0, 0)), pl.BlockSpec((1, ATTN_W), lambda n: (0, 0)),
                  blk, pl.BlockSpec((WINDOW, ATTN_W), lambda n: (n, 1))],
        out_specs=[blk, full_kv, full_kv, pl.BlockSpec((1, N_HEADS), lambda n: (0, 0)),
                   pl.BlockSpec((1, ATTN_W), lambda n: (0, 0))],
        out_shape=[jax.ShapeDtypeStruct((T, ATTN_W), BF16), jax.ShapeDtypeStruct((T + WINDOW, KV_W), F32),
                   jax.ShapeDtypeStruct((T + WINDOW, KV_W), F32), jax.ShapeDtypeStruct((1, N_HEADS), F32),
                   jax.ShapeDtypeStruct((1, ATTN_W), F32)],
        name=name, compiler_params=_cparams(("arbitrary",)),
    )(proj, kpad, vpad, sinks, og, o, dy)


ANY = pl.BlockSpec(memory_space=pl.ANY)


def _coords():
    return lax.axis_index("x"), lax.axis_index("y"), lax.axis_index("c")


def _all_gather(arrs, name):
    n = len(arrs)

    def body(*refs):
        ins, outs = refs[:n], refs[n:2 * n]
        send_sems, recv_sems, local_sems = refs[2 * n:]
        x, y, c = _coords()
        me = 4 * x + 2 * y + c
        sibling = (x, y, 1 - c)
        chips = [(1 - x, y), (x, 1 - y), (1 - x, 1 - y)]

        def copy(a, k, block, to, src=None):
            dst = outs[a].at[block]
            return pltpu.make_async_remote_copy(
                src_ref=dst if src is None else src, dst_ref=dst, send_sem=send_sems.at[a, k],
                recv_sem=recv_sems.at[a, k], device_id=to, device_id_type=MESH)

        mine = [pltpu.make_async_copy(ins[a], outs[a].at[me], local_sems.at[a]) for a in range(n)]
        for cp in mine:
            cp.start()
        first = []
        for a in range(n):
            first.append(copy(a, 0, me, sibling, src=ins[a]))
            for j, chip in enumerate(chips):
                first.append(copy(a, 1 + j, me, (*chip, c), src=ins[a]))
        for cp in first:
            cp.start()
        passed = []
        for j, (px, py) in enumerate(chips):
            blk = 4 * px + 2 * py + c
            for a in range(n):
                copy(a, 1 + j, blk, sibling).wait_recv()
                fwd = copy(a, 4 + j, blk, sibling)
                fwd.start()
                passed.append(fwd)
        for a in range(n):
            copy(a, 0, 4 * x + 2 * y + (1 - c), sibling).wait_recv()
            for j, (px, py) in enumerate(chips):
                copy(a, 4 + j, 4 * px + 2 * py + (1 - c), sibling).wait_recv()
        for cp in first + passed:
            cp.wait_send()
        for cp in mine:
            cp.wait()

    return pl.pallas_call(
        body, in_specs=[ANY] * n, out_specs=[ANY] * n,
        out_shape=[jax.ShapeDtypeStruct((N_DEV,) + a.shape, a.dtype) for a in arrs],
        scratch_shapes=[pltpu.SemaphoreType.DMA((n, 7)), pltpu.SemaphoreType.DMA((n, 7)),
                        pltpu.SemaphoreType.DMA((n,))],
        name=name,
    )(*arrs)


def _exchange_pair(arrs, name):
    n = len(arrs)

    def body(*refs):
        ins, outs = refs[:n], refs[n:2 * n]
        send_sems, recv_sems = refs[2 * n:]
        x, y, c = _coords()
        cps = []
        for a in range(n):
            for q in range(4):
                cps.append(pltpu.make_async_remote_copy(
                    src_ref=ins[a].at[2 * q + (1 - c)], dst_ref=outs[a].at[q], send_sem=send_sems.at[a, q],
                    recv_sem=recv_sems.at[a, q], device_id=(x, y, 1 - c), device_id_type=MESH))
        for cp in cps:
            cp.start()
        for cp in cps:
            cp.wait()

    return pl.pallas_call(
        body, in_specs=[ANY] * n, out_specs=[ANY] * n,
        out_shape=[jax.ShapeDtypeStruct((4,) + a.shape[1:], a.dtype) for a in arrs],
        scratch_shapes=[pltpu.SemaphoreType.DMA((n, 4)), pltpu.SemaphoreType.DMA((n, 4))],
        name=name,
    )(*arrs)


def _exchange_chips(arrs, name):
    n = len(arrs)

    def body(*refs):
        ins, outs = refs[:n], refs[n:2 * n]
        send_sems, recv_sems = refs[2 * n:]
        x, y, c = _coords()
        chips = [(1 - x, y), (x, 1 - y), (1 - x, 1 - y)]
        cps = []
        for a in range(n):
            for k, (tx, ty) in enumerate(chips):
                cps.append(pltpu.make_async_remote_copy(
                    src_ref=ins[a].at[2 * tx + ty], dst_ref=outs[a].at[k], send_sem=send_sems.at[a, k],
                    recv_sem=recv_sems.at[a, k], device_id=(tx, ty, c), device_id_type=MESH))
        for cp in cps:
            cp.start()
        for cp in cps:
            cp.wait()

    return pl.pallas_call(
        body, in_specs=[ANY] * n, out_specs=[ANY] * n,
        out_shape=[jax.ShapeDtypeStruct((3,) + a.shape[1:], a.dtype) for a in arrs],
        scratch_shapes=[pltpu.SemaphoreType.DMA((n, 3)), pltpu.SemaphoreType.DMA((n, 3))],
        name=name,
    )(*arrs)


HBM = pl.BlockSpec(memory_space=pltpu.HBM)
SEM = pl.BlockSpec(memory_space=pltpu.SEMAPHORE)
EFFECT = pltpu.SideEffectType.DATAFLOW_SIDE_EFFECTING


def _in_hbm(a):
    return pltpu.with_memory_space_constraint(a, pltpu.HBM)


def _remote_start(srcs, lands, plan, n_copies, name):
    n = len(srcs)

    def body(*refs):
        src_refs, land_refs = refs[:n], refs[n:2 * n]
        send_sems, recv_sems = refs[2 * n], refs[2 * n + 1]
        token = refs[-1]
        x, y, c = _coords()
        for i, (sv, dv, dev) in enumerate(plan(src_refs, land_refs, x, y, c)):
            pltpu.make_async_remote_copy(src_ref=sv, dst_ref=dv, send_sem=send_sems.at[i], recv_sem=recv_sems.at[i],
                                         device_id=dev, device_id_type=MESH).start()
        token[...] = jnp.zeros_like(token)

    bufs = list(srcs) + list(lands)
    outs = pl.pallas_call(
        body, name=name,
        out_shape=(pltpu.SemaphoreType.DMA((n_copies,)), pltpu.SemaphoreType.DMA((n_copies,)),
                   *[pltpu.HBM(b.shape, b.dtype) for b in bufs], jax.ShapeDtypeStruct((8, 128), F32)),
        in_specs=[HBM] * (2 * n), out_specs=(SEM, SEM, *[HBM] * (2 * n), pl.BlockSpec(memory_space=pltpu.VMEM)),
        input_output_aliases={i: 2 + i for i in range(2 * n)},
        compiler_params=pltpu.CompilerParams(has_side_effects=EFFECT),
    )(*[_in_hbm(b) for b in bufs])
    return outs[0], outs[1], list(outs[2:2 + n]), list(outs[2 + n:2 + 2 * n]), outs[-1]


def _remote_wait(started, after, plan, name):
    send_sems, recv_sems, srcs, lands, _ = started
    n = len(srcs)

    def body(*refs):
        src_refs, land_refs = refs[:n], refs[n:2 * n]
        send_sems, recv_sems = refs[2 * n], refs[2 * n + 1]
        x, y, c = _coords()
        for i, (sv, dv, dev) in enumerate(plan(src_refs, land_refs, x, y, c)):
            cp = pltpu.make_async_remote_copy(src_ref=sv, dst_ref=dv, send_sem=send_sems.at[i],
                                              recv_sem=recv_sems.at[i], device_id=dev, device_id_type=MESH)
            cp.wait_send()
            cp.wait_recv()

    bufs = list(srcs) + list(lands)
    outs = pl.pallas_call(
        body, name=name, out_shape=tuple(pltpu.HBM(b.shape, b.dtype) for b in bufs),
        in_specs=[HBM] * (2 * n) + [SEM, SEM, ANY], out_specs=tuple([HBM] * (2 * n)),
        input_output_aliases={i: i for i in range(2 * n)},
        compiler_params=pltpu.CompilerParams(has_side_effects=EFFECT),
    )(*bufs, send_sems, recv_sems, after)
    return list(outs[:n]), list(outs[n:])


def _gather_plan(src_refs, land_refs, x, y, c):
    me = 4 * x + 2 * y + c
    plan = []
    for s, l in zip(src_refs, land_refs):
        for dev in [(x, y, 1 - c), (1 - x, y, c), (x, 1 - y, c), (1 - x, 1 - y, c)]:
            plan.append((s, l.at[me], dev))
    return plan


def _chips_plan(src_refs, land_refs, x, y, c):
    plan = []
    for s, l in zip(src_refs, land_refs):
        for k, (tx, ty) in enumerate([(1 - x, y), (x, 1 - y), (1 - x, 1 - y)]):
            plan.append((s.at[2 * tx + ty], l.at[k], (tx, ty, c)))
    return plan


def _everyone_plan(src_refs, land_refs, x, y, c):
    me = 4 * x + 2 * y + c
    plan = []
    for s, l in zip(src_refs, land_refs):
        for fx, fy, fc in [(0, 0, 1), (1, 0, 0), (1, 0, 1), (0, 1, 0), (0, 1, 1), (1, 1, 0), (1, 1, 1)]:
            dev = ((1 - x) if fx else x, (1 - y) if fy else y, (1 - c) if fc else c)
            plan.append((s, l.at[me], dev))
    return plan


def _gather_finish(gathered, name):
    n = len(gathered)

    def body(*refs):
        outs = refs[n:2 * n]
        send_sems, recv_sems = refs[2 * n:]
        x, y, c = _coords()
        cps = []
        for a in range(n):
            for j, (px, py) in enumerate([(1 - x, y), (x, 1 - y), (1 - x, 1 - y)]):
                blk = outs[a].at[4 * px + 2 * py + c]
                got = outs[a].at[4 * px + 2 * py + (1 - c)]
                cps.append((pltpu.make_async_remote_copy(
                    src_ref=blk, dst_ref=blk, send_sem=send_sems.at[a, j], recv_sem=recv_sems.at[a, j],
                    device_id=(x, y, 1 - c), device_id_type=MESH), pltpu.make_async_remote_copy(
                    src_ref=got, dst_ref=got, send_sem=send_sems.at[a, j], recv_sem=recv_sems.at[a, j],
                    device_id=(x, y, 1 - c), device_id_type=MESH)))
        for cp, _ in cps:
            cp.start()
        for cp, arrival in cps:
            cp.wait_send()
            arrival.wait_recv()

    return pl.pallas_call(
        body, in_specs=[ANY] * n, out_specs=[ANY] * n,
        out_shape=[jax.ShapeDtypeStruct(g.shape, g.dtype) for g in gathered],
        input_output_aliases={a: a for a in range(n)},
        scratch_shapes=[pltpu.SemaphoreType.DMA((n, 3)), pltpu.SemaphoreType.DMA((n, 3))],
        name=name,
    )(*gathered)


def _pair_add(g8, r1, csel, tr, name):
    _, R, C = r1.shape
    g4 = g8.reshape(4, 2, R, C)

    def body(c_ref, g_ref, r_ref, o_ref):
        o_ref[...] = (g_ref[...].astype(F32) + r_ref[...].astype(F32)).astype(BF16)

    return pl.pallas_call(
        body,
        grid_spec=pltpu.PrefetchScalarGridSpec(
            num_scalar_prefetch=1, grid=(4, R // tr),
            in_specs=[pl.BlockSpec((None, None, tr, C), lambda q, i, cs: (q, cs[0], i, 0)),
                      pl.BlockSpec((None, tr, C), lambda q, i, cs: (q, i, 0))],
            out_specs=pl.BlockSpec((None, tr, C), lambda q, i, cs: (q, i, 0))),
        out_shape=jax.ShapeDtypeStruct((4, R, C), BF16), name=name,
        compiler_params=_cparams(("parallel", "parallel")),
    )(csel, g4, r1)


def _adamw_math(w, g, m, v):
    m = ADAM_B1 * m + (1.0 - ADAM_B1) * g
    v = ADAM_B2 * v + (1.0 - ADAM_B2) * (g * g)
    m_hat = m / (1.0 - ADAM_B1 ** ADAM_STEP)
    v_hat = v / (1.0 - ADAM_B2 ** ADAM_STEP)
    delta = -ADAM_LR * (m_hat / (jnp.sqrt(v_hat) + ADAM_EPS) + ADAM_WD * w)
    return delta, m, v


def _adamw_big(w, m, v, p4, r3, qsel, tr, name):
    R, C = w.shape

    def body(q_ref, w_ref, m_ref, v_ref, p_ref, r_ref, g_out, d_out, m_out, v_out):
        g = p_ref[...].astype(F32) + r_ref[0].astype(F32) + r_ref[1].astype(F32) + r_ref[2].astype(F32)
        d, mn, vn = _adamw_math(w_ref[...], g, m_ref[...], v_ref[...])
        g_out[...] = g
        d_out[...] = d
        m_out[...] = mn
        v_out[...] = vn

    blk = pl.BlockSpec((tr, C), lambda i, qs: (i, 0))
    return pl.pallas_call(
        body,
        grid_spec=pltpu.PrefetchScalarGridSpec(
            num_scalar_prefetch=1, grid=(R // tr,),
            in_specs=[blk, blk, blk, pl.BlockSpec((None, tr, C), lambda i, qs: (qs[0], i, 0)),
                      pl.BlockSpec((3, tr, C), lambda i, qs: (0, i, 0))],
            out_specs=[blk, blk, blk, blk]),
        out_shape=[jax.ShapeDtypeStruct((R, C), F32)] * 4, name=name,
        compiler_params=_cparams(("parallel",)),
    )(qsel, w, m, v, p4, r3)


def _small_sum(parts, name):
    def body(p_ref, o_ref):
        acc = p_ref[0]
        for d in range(1, N_DEV):
            acc = acc + p_ref[d]
        o_ref[...] = acc

    return pl.pallas_call(
        body, out_shape=jax.ShapeDtypeStruct(parts.shape[1:], F32), name=name,
        compiler_params=_cparams(),
    )(parts)


def _adamw_small(w, g, m, v, name):
    def body(w_ref, g_ref, m_ref, v_ref, d_out, m_out, v_out):
        d, mn, vn = _adamw_math(w_ref[...], g_ref[...], m_ref[...], v_ref[...])
        d_out[...] = d
        m_out[...] = mn
        v_out[...] = vn

    return pl.pallas_call(
        body, out_shape=[jax.ShapeDtypeStruct(w.shape, F32)] * 3, name=name, compiler_params=_cparams(),
    )(w, g, m, v)


def _row(*pieces):
    r = jnp.concatenate([p.reshape(1, -1) for p in pieces], axis=1)
    return jnp.pad(r, ((0, 0), (0, D_MODEL - r.shape[1])))


def _pack_small(mix, convb, ssmg, attng, mlpg, fing, convw, dtb, alog, dsk, sinks, extra=None):
    last = [dtb, alog, dsk, sinks] + ([extra] if extra is not None else [])
    rows = [_row(mix), _row(convb), _row(ssmg, attng), _row(mlpg), _row(fing),
            jnp.pad(convw, ((0, 0), (0, D_MODEL - convw.shape[1]))), _row(*last)]
    packed = jnp.concatenate(rows, axis=0)
    return jnp.pad(packed, ((0, SMALL_ROWS - packed.shape[0]), (0, 0)))


def _unpack_small(p, conv_n):
    return dict(
        mix_norm_g=p[0:1, :], conv_b=p[1:2, :], ssm_norm_g=p[2:3, :D_INNER], attn_out_norm_g=p[2:3, D_INNER:],
        mlp_norm_g=p[3:4, :], final_norm_g=p[4, :], conv_w=p[5:9, :conv_n][None],
        dt_bias=p[9:10, 0:16], A_log=p[9:10, 16:32], D_skip=p[9:10, 32:48], attn_sinks=p[9:10, 48:64])


SMALL_NAMES = ["mix_norm_g", "conv_w", "conv_b", "dt_bias", "A_log", "D_skip", "ssm_norm_g", "attn_sinks",
               "attn_out_norm_g", "mlp_norm_g", "final_norm_g"]
WEIGHT_ORDER = ["mix_norm_g", "w_in", "conv_w", "conv_b", "dt_bias", "A_log", "D_skip", "ssm_norm_g", "attn_sinks",
                "attn_out_norm_g", "w_out", "mlp_norm_g", "w_up", "w_down", "final_norm_g"]


def _to_my_columns(w_nat):
    pad = jnp.zeros((w_nat.shape[0], NP - IN_PROJ), w_nat.dtype)
    return jnp.concatenate([w_nat[:, :NAT_DT], w_nat[:, NAT_DT + N_HEADS:], w_nat[:, NAT_DT:NAT_DT + N_HEADS], pad],
                           axis=1)


def _to_natural_columns(w_my):
    return jnp.concatenate([w_my[:, :NAT_DT], w_my[:, OFF_DT:OFF_DT + N_HEADS], w_my[:, NAT_DT:OFF_DT]], axis=1)


class _FixedWeights:
    def __init__(self, w_in_p, w_out_f, w_up_s, w_down_f, conv_w_f):
        self.w = (w_in_p, w_out_f, w_up_s, w_down_f, conv_w_f)
        self.grads = {}

    def mixer_weights(self, after):
        return self.w[0], self.w[1], self.w[4]

    def up_weight(self, after):
        return self.w[2]

    def down_weight(self, after):
        return self.w[3]

    def mlp_grads(self, g_up, g_down):
        self.grads.update(w_up=g_up, w_down=g_down)
        return None

    def out_grad(self, g_out):
        self.grads.update(w_out=g_out)
        return None

    def in_grad(self, g_in):
        self.grads.update(w_in=g_in)
        return None


def _local_step(x, tgt, p, hooks):
    T = x.shape[0]
    D = D_MODEL
    h1 = _rmsnorm_fwd(x, p["mix_norm_g"], "norm_mix")
    w_in_p, w_out_f, conv_w_f = hooks.mixer_weights(h1)
    (proj,) = _mm_simple(h1, w_in_p, mode="nn", M=T, N=NP, K=D, tm=min(T, 1024), tn=768, tk=D, out_dtype=F32,
                         name="in_proj")
    xbc = _conv_fwd(proj, conv_w_f, p["conv_b"], "conv_fwd")
    dtT = proj[:, OFF_DT:OFF_DT + N_HEADS].T
    dtbT = p["dt_bias"].T
    alogT = p["A_log"].T
    dfull = jnp.repeat(p["D_skip"], HEAD_DIM, axis=1)
    y_ssm, ypre, hs = _ssd_fwd(xbc, proj, dtT, p["dt_bias"], dtbT, p["A_log"], alogT, dfull, p["ssm_norm_g"],
                               "ssd_fwd")
    kpad = jnp.pad(proj[:, OFF_K:OFF_K + KV_W], ((WINDOW, 0), (0, 0)))
    vpad = jnp.pad(proj[:, OFF_V:OFF_V + KV_W], ((WINDOW, 0), (0, 0)))
    y_att, o_att = _attn_fwd(proj, kpad, vpad, p["attn_sinks"], p["attn_out_norm_g"], "attn_fwd")
    ycat = jnp.concatenate([y_ssm, y_att], axis=1)
    tm = min(T, 1024)
    (x2,) = _mm_simple(ycat, w_out_f, mode="nn", M=T, N=D, K=D, tm=tm, tn=1024, tk=D, out_dtype=F32, name="out_proj",
                       extras=(x,), epilogue=lambda acc, res: (acc + res,))
    h2 = _rmsnorm_fwd(x2, p["mlp_norm_g"], "norm_mlp")
    w_up_s = hooks.up_weight(h2)
    grid = (T // tm, N_DEV, 1)
    u, act = _matmul(
        h2, w_up_s, mode="nn", grid=grid,
        a_spec=pl.BlockSpec((tm, D), lambda i, j, k: (i, 0)),
        b_spec=pl.BlockSpec((None, D, 1024), lambda i, j, k: (j, 0, 0)),
        out_shapes=[jax.ShapeDtypeStruct((T, D_FF), F32), jax.ShapeDtypeStruct((T, D_FF), BF16)],
        out_specs=[pl.BlockSpec((tm, 1024), lambda i, j, k: (i, j))] * 2, tile=(tm, 1024), name="mlp_up",
        epilogue=lambda acc: (acc, jnp.square(jnp.maximum(acc, 0.0))))
    w_down_f = hooks.down_weight(act)
    (x3,) = _mm_simple(act, w_down_f, mode="nn", M=T, N=D, K=D_FF, tm=tm, tn=1024, tk=1024, out_dtype=F32,
                       name="mlp_down", extras=(x2,), epilogue=lambda acc, res: (acc + res,))
    loss_part, d_fin, dx3, dx3b = _final_loss(x3, tgt, p["final_norm_g"].reshape(1, D), "loss_head")
    (g_down,) = _mm_simple(act, dx3b, mode="tn", M=D_FF, N=D, K=T, tm=1024, tn=1024, tk=T, out_dtype=BF16,
                           name="grad_w_down")
    (du,) = _mm_simple(dx3b, w_down_f, mode="nt", M=T, N=D_FF, K=D, tm=tm, tn=1024, tk=D, out_dtype=BF16,
                       name="mlp_down_bwd", extras=(u,),
                       epilogue=lambda acc, uu: (acc * (2.0 * jnp.maximum(uu, 0.0)),))
    (g_up,) = _matmul(
        h2, du, mode="tn", grid=(D // 1024, N_DEV, 1),
        a_spec=pl.BlockSpec((T, 1024), lambda i, j, k: (0, i)),
        b_spec=pl.BlockSpec((T, 1024), lambda i, j, k: (0, j)),
        out_shapes=[jax.ShapeDtypeStruct((N_DEV, D, 1024), BF16)],
        out_specs=[pl.BlockSpec((None, 1024, 1024), lambda i, j, k: (j, i, 0))], tile=(1024, 1024), name="grad_w_up")
    token = hooks.mlp_grads(g_up, g_down)
    (dh2,) = _matmul(
        du, w_up_s, mode="nt", grid=(T // tm, D // 1024, N_DEV),
        a_spec=pl.BlockSpec((tm, 1024), lambda i, j, k: (i, k)),
        b_spec=pl.BlockSpec((None, 1024, 1024), lambda i, j, k: (k, j, 0)),
        out_shapes=[jax.ShapeDtypeStruct((T, D), F32)],
        out_specs=[pl.BlockSpec((tm, 1024), lambda i, j, k: (i, j))], tile=(tm, 1024), name="mlp_up_bwd",
        after=token)
    dx2, dx2b, d_mlp = _rmsnorm_bwd(dh2, x2, p["mlp_norm_g"], dx3, "norm_mlp_bwd")
    (g_out,) = _mm_simple(ycat, dx2b, mode="tn", M=D, N=D, K=T, tm=1024, tn=1024, tk=T, out_dtype=BF16,
                          name="grad_w_out")
    hooks.out_grad(g_out)
    (dy,) = _mm_simple(dx2b, w_out_f, mode="nt", M=T, N=D, K=D, tm=tm, tn=1024, tk=D, out_dtype=F32,
                       name="out_proj_bwd")
    dz, dxbc_act, ddt, d_dtb, d_alog, d_dskip, d_ssmg = _ssd_bwd(
        xbc, proj, dtT, p["dt_bias"], dtbT, p["A_log"], alogT, dfull, p["ssm_norm_g"], ypre, hs, dy, "ssd_bwd")
    dq, dkpad, dvpad, d_sinks, d_attng = _attn_bwd(proj, kpad, vpad, p["attn_sinks"], p["attn_out_norm_g"], o_att, dy,
                                                   "attn_bwd")
    dxbc, d_convw, d_convb = _conv_bwd(proj, dxbc_act, conv_w_f, p["conv_b"], "conv_bwd")
    dproj = jnp.concatenate(
        [dz, dxbc, dq, dkpad[WINDOW:].astype(BF16), dvpad[WINDOW:].astype(BF16), ddt.astype(BF16),
         jnp.zeros((T, NP - OFF_DT - 128), BF16)], axis=1)
    (g_in,) = _mm_simple(h1, dproj, mode="tn", M=D, N=NP, K=T, tm=1024, tn=768, tk=T, out_dtype=BF16,
                         name="grad_w_in")
    token = hooks.in_grad(g_in)
    (dh1,) = _mm_simple(dproj, w_in_p, mode="nt", M=T, N=D, K=NP, tm=tm, tn=1024, tk=768, out_dtype=F32,
                        name="in_proj_bwd", after=token)
    dx, _, d_mix = _rmsnorm_bwd(dh1, x, p["mix_norm_g"], dx2, "norm_mix_bwd")
    small = _pack_small(d_mix, d_convb, d_ssmg, d_attng, d_mlp, d_fin, d_convw, d_dtb, d_alog, d_dskip, d_sinks,
                        extra=loss_part[:, 0:1])
    return dx, small


def _landing(own, me):
    zone = lax.empty((N_DEV,) + own.shape, own.dtype)
    return lax.dynamic_update_slice(zone, own[None], (me,) + (0,) * own.ndim)


def _gather_end(started, after, plan, name):
    _, lands = _remote_wait(started, after, plan, name + "_wait")
    return _gather_finish(lands, name + "_finish")


def _zero_after(v):
    return jnp.minimum(jnp.abs(v), 0.0)


class _ShardedWeights:
    def __init__(self, w_in, w_out, conv_w, w_up, w_down, me, csel):
        self.me, self.csel = me, csel
        self.w_up, self.w_down = w_up, w_down
        shards = [w_in.astype(BF16), w_out.astype(BF16), conv_w]
        self.st_mixer = _remote_start(shards, [_landing(s, me) for s in shards], _gather_plan, 4 * len(shards),
                                      "gather_start_mixer")
        self.start_token = self.st_mixer[4]
        self.st_up = self.st_down = None
        self.reduces = {}
        self.g_out = None

    def mixer_weights(self, after):
        g_in, g_out, g_conv = _gather_end(self.st_mixer, after, _gather_plan, "gather_mixer")
        zero = _zero_after(g_conv[0, 0, 0])
        up, down = (self.w_up + zero).astype(BF16), (self.w_down + zero).astype(BF16)
        self.st_up = _remote_start([up], [_landing(up, self.me)], _gather_plan, 4, "gather_start_up")
        self.st_down = _remote_start([down], [_landing(down, self.me)], _gather_plan, 4, "gather_start_down")
        per = IN_PROJ // N_DEV
        k, off = NAT_DT // per, NAT_DT % per
        pieces = [g_in[i] for i in range(k)] + [g_in[k][:, :off], g_in[k][:, off + N_HEADS:]]
        pieces += [g_in[i] for i in range(k + 1, N_DEV)]
        pieces += [g_in[k][:, off:off + N_HEADS], jnp.zeros((D_MODEL, NP - IN_PROJ), BF16)]
        started = (self.st_up[4][0:1, 0:1] + self.st_down[4][0:1, 0:1]).astype(BF16)
        w_in_p = jnp.concatenate(pieces, axis=1) + started
        conv_w_f = jnp.concatenate([g_conv[i] for i in range(N_DEV)], axis=1)
        return w_in_p, g_out.reshape(D_MODEL, D_MODEL), conv_w_f

    def up_weight(self, after):
        return _gather_end(self.st_up, after, _gather_plan, "gather_up")[0]

    def down_weight(self, after):
        return _gather_end(self.st_down, after, _gather_plan, "gather_down")[0].reshape(D_FF, D_MODEL)

    def _reduce_start(self, slabs, rows, tag):
        from_sibling = _exchange_pair(slabs, f"reduce_pair_{tag}")
        sums = [_pair_add(s, r, self.csel, tr, f"pair_add_{tag}_{i}")
                for i, (s, r, tr) in enumerate(zip(slabs, from_sibling, rows))]
        lands = [lax.empty((3,) + s.shape[1:], s.dtype) for s in sums]
        self.reduces[tag] = _remote_start(sums, lands, _chips_plan, 3 * len(sums), f"reduce_start_{tag}")
        return self.reduces[tag][4]

    def mlp_grads(self, g_up, g_down):
        return self._reduce_start([g_up, g_down.reshape(N_DEV, D_FF // N_DEV, D_MODEL)], [512, 256], "mlp")

    def out_grad(self, g_out):
        self.g_out = g_out

    def in_grad(self, g_in):
        per = IN_PROJ // N_DEV
        k, off = NAT_DT // per, NAT_DT % per

        def slab(i):
            if i < k:
                return g_in[:, i * per:(i + 1) * per]
            if i == k:
                return jnp.concatenate([g_in[:, k * per:NAT_DT], g_in[:, OFF_DT:OFF_DT + N_HEADS],
                                        g_in[:, NAT_DT:NAT_DT + per - off - N_HEADS]], axis=1)
            return g_in[:, i * per - N_HEADS:(i + 1) * per - N_HEADS]

        s_in = jnp.stack([slab(i) for i in range(N_DEV)])
        s_out = self.g_out.reshape(N_DEV, D_MODEL // N_DEV, D_MODEL)
        return self._reduce_start([s_in, s_out], [256, 256], "mixer")

    def small_start(self, small):
        self.st_small = _remote_start([small], [_landing(small, self.me)], _everyone_plan, N_DEV - 1, "gather_start_small")

    def small_end(self, after):
        return _remote_wait(self.st_small, after, _everyone_plan, "gather_small_wait")[1][0]

    def reduce_end(self, tag, after):
        return _remote_wait(self.reduces[tag], after, _chips_plan, f"reduce_wait_{tag}")


def kernel(x, mix_norm_g, w_in, conv_w, conv_b, dt_bias, A_log, D_skip, ssm_norm_g, attn_sinks, attn_out_norm_g, w_out, mlp_norm_g, w_up, w_down, final_norm_g, loss_target, m_mix_norm_g, m_w_in, m_conv_w, m_conv_b, m_dt_bias, m_A_log, m_D_skip, m_ssm_norm_g, m_attn_sinks, m_attn_out_norm_g, m_w_out, m_mlp_norm_g, m_w_up, m_w_down, m_final_norm_g, v_mix_norm_g, v_w_in, v_conv_w, v_conv_b, v_dt_bias, v_A_log, v_D_skip, v_ssm_norm_g, v_attn_sinks, v_attn_out_norm_g, v_w_out, v_mlp_norm_g, v_w_up, v_w_down, v_final_norm_g):
    xi, yi, ci = _coords()
    me = 4 * xi + 2 * yi + ci
    csel = jnp.reshape(ci, (1,)).astype(jnp.int32)
    qsel = jnp.reshape(2 * xi + yi, (1,)).astype(jnp.int32)
    w = dict(mix_norm_g=mix_norm_g, conv_b=conv_b, dt_bias=dt_bias, A_log=A_log, D_skip=D_skip,
             ssm_norm_g=ssm_norm_g, attn_sinks=attn_sinks, attn_out_norm_g=attn_out_norm_g, mlp_norm_g=mlp_norm_g,
             final_norm_g=final_norm_g)
    hooks = _ShardedWeights(w_in[0], w_out[0], conv_w[0], w_up[0], w_down[0], me, csel)
    p = dict(w, mix_norm_g=mix_norm_g + hooks.start_token[0:1, 0:1])
    dx, small = _local_step(x[0], loss_target[0], p, hooks)
    hooks.small_start(small)
    big = {}
    after = dx
    for tag, members in [("mlp", [("w_up", w_up, m_w_up, v_w_up, 512), ("w_down", w_down, m_w_down, v_w_down, 256)]),
                         ("mixer", [("w_in", w_in, m_w_in, v_w_in, 256), ("w_out", w_out, m_w_out, v_w_out, 256)])]:
        chip_sums, from_chips = hooks.reduce_end(tag, after)
        for i, (name, wt, mt, vt, tr) in enumerate(members):
            g, d, mn, vn = _adamw_big(wt[0], mt[0], vt[0], chip_sums[i], from_chips[i], qsel, tr, f"adamw_{name}")
            big[name] = (g[None], d[None], mn[None], vn[None])
            after = g
    gsum = _small_sum(hooks.small_end(after), "small_sum")
    loss = gsum[9, 64]
    gs = _unpack_small(gsum, CONV_DIM)
    cw = CONV_DIM // N_DEV
    g_conv_shard = lax.dynamic_slice(gsum[5:9, :], (0, me * cw), (CONV_K, cw))

    def pack(s):
        return _pack_small(s["mix_norm_g"], s["conv_b"], s["ssm_norm_g"], s["attn_out_norm_g"], s["mlp_norm_g"],
                           s["final_norm_g"], s["conv_w"][0], s["dt_bias"], s["A_log"], s["D_skip"], s["attn_sinks"])

    wp = pack(dict(w, conv_w=conv_w))
    mp = pack(dict(mix_norm_g=m_mix_norm_g, conv_b=m_conv_b, ssm_norm_g=m_ssm_norm_g,
                   attn_out_norm_g=m_attn_out_norm_g, mlp_norm_g=m_mlp_norm_g, final_norm_g=m_final_norm_g,
                   conv_w=m_conv_w, dt_bias=m_dt_bias, A_log=m_A_log, D_skip=m_D_skip, attn_sinks=m_attn_sinks))
    vp = pack(dict(mix_norm_g=v_mix_norm_g, conv_b=v_conv_b, ssm_norm_g=v_ssm_norm_g,
                   attn_out_norm_g=v_attn_out_norm_g, mlp_norm_g=v_mlp_norm_g, final_norm_g=v_final_norm_g,
                   conv_w=v_conv_w, dt_bias=v_dt_bias, A_log=v_A_log, D_skip=v_D_skip, attn_sinks=v_attn_sinks))
    gp = jnp.concatenate([gsum[0:5], jnp.pad(g_conv_shard, ((0, 0), (0, D_MODEL - cw))), gsum[9:10],
                          jnp.zeros((SMALL_ROWS - 10, D_MODEL), F32)], axis=0)
    dp, mnp, vnp = _adamw_small(wp, gp, mp, vp, "adamw_small")
    grads = dict(gs, conv_w=g_conv_shard[None])
    deltas = _unpack_small(dp, cw)
    new_m = _unpack_small(mnp, cw)
    new_v = _unpack_small(vnp, cw)
    for k, name in enumerate(["w_in", "w_out", "w_up", "w_down"]):
        grads[name], deltas[name], new_m[name], new_v[name] = big[name]
    return (loss, dx[None], *[grads[n] for n in WEIGHT_ORDER], *[deltas[n] for n in WEIGHT_ORDER],
            *[new_m[n] for n in WEIGHT_ORDER], *[new_v[n] for n in WEIGHT_ORDER])
```

```python
import functools

import jax
import jax.numpy as jnp
from jax import lax
from jax.experimental import pallas as pl
from jax.experimental.pallas import tpu as pltpu

F32 = jnp.float32
BF16 = jnp.bfloat16
HI = lax.Precision.HIGHEST
MESH = pl.DeviceIdType.MESH

EPS = 1e-5
D_MODEL = 2048
D_INNER = 1024
N_HEADS = 16
HEAD_DIM = 64
N_GROUPS = 4
D_STATE = 128
CHUNK = 128
CONV_K = 4
CONV_DIM = 2048
ATTN_W = 1024
KV_W = 128
WINDOW = 128
D_FF = 8192
IN_PROJ = 4368
N_DEV = 8
NP = 4608
OFF_Z, OFF_X, OFF_B, OFF_C, OFF_Q, OFF_K, OFF_V, OFF_DT = 0, 1024, 2048, 2560, 3072, 4096, 4224, 4352
NAT_DT = 3072

ADAM_LR = 0.001
ADAM_B1 = 0.9
ADAM_B2 = 0.999
ADAM_EPS = 1e-08
ADAM_WD = 0.01
ADAM_STEP = 10

VMEM_LIMIT = 52 * 1024 * 1024
SMALL_ROWS = 16
NEG = -1e30


def _cparams(sem=None):
    return pltpu.CompilerParams(dimension_semantics=sem, vmem_limit_bytes=VMEM_LIMIT)


def _hdot(a, b):
    return jnp.dot(a, b, precision=HI, preferred_element_type=F32)


def _dot_nn(a, b):
    return lax.dot_general(a, b, (((1,), (0,)), ((), ())), preferred_element_type=F32)


def _dot_nt(a, b):
    return lax.dot_general(a, b, (((1,), (1,)), ((), ())), preferred_element_type=F32)


def _dot_tn(a, b):
    return lax.dot_general(a, b, (((0,), (0,)), ((), ())), preferred_element_type=F32)


def _softplus(v):
    return jnp.maximum(v, 0.0) + jnp.log1p(jnp.exp(-jnp.abs(v)))


def _sigmoid(v):
    return 1.0 / (1.0 + jnp.exp(-v))


def _matmul(a, b, *, mode, grid, a_spec, b_spec, out_shapes, out_specs, tile, name,
            extras=(), extra_specs=(), epilogue=None, after=None):
    nk = grid[2]
    n_ex = len(extras)
    n_out = len(out_shapes)
    dot = {"nn": _dot_nn, "nt": _dot_nt, "tn": _dot_tn}[mode]

    def finish(acc, ex_refs, out_refs):
        res = (acc,) if epilogue is None else epilogue(acc, *[e[...] for e in ex_refs])
        for o, r in zip(out_refs, res):
            o[...] = r.astype(o.dtype)

    def body(*refs):
        a_ref, b_ref = refs[0], refs[1]
        ex_refs = refs[2:2 + n_ex]
        out_refs = refs[2 + n_ex:2 + n_ex + n_out]
        part = dot(a_ref[...].astype(BF16), b_ref[...].astype(BF16))
        if nk == 1:
            finish(part, ex_refs, out_refs)
        else:
            acc_ref = refs[-1]
            k = pl.program_id(2)

            @pl.when(k == 0)
            def _():
                acc_ref[...] = part

            @pl.when(k > 0)
            def _():
                acc_ref[...] += part

            @pl.when(k == nk - 1)
            def _():
                finish(acc_ref[...], ex_refs, out_refs)

    scratch = [] if nk == 1 else [pltpu.VMEM(tile, F32)]
    tok_specs = [] if after is None else [pl.BlockSpec((8, 128), lambda i, j, k: (0, 0))]
    tok_args = [] if after is None else [after]
    n_out = len(out_shapes)

    def body_with_token(*refs):
        body(*refs[:2 + n_ex], *refs[2 + n_ex + len(tok_args):])

    outs = pl.pallas_call(
        body_with_token, grid=grid, in_specs=[a_spec, b_spec, *extra_specs, *tok_specs], out_specs=list(out_specs),
        out_shape=list(out_shapes), scratch_shapes=scratch, name=name,
        compiler_params=_cparams(("parallel", "parallel", "arbitrary")),
    )(a, b, *extras, *tok_args)
    return outs


def _mm_simple(a, b, *, mode, M, N, K, tm, tn, tk, out_dtype, name, extras=(), epilogue=None, n_out=1,
               out_dtypes=None, after=None):
    grid = (M // tm, N // tn, K // tk)
    if mode == "nn":
        a_spec = pl.BlockSpec((tm, tk), lambda i, j, k: (i, k))
        b_spec = pl.BlockSpec((tk, tn), lambda i, j, k: (k, j))
    elif mode == "nt":
        a_spec = pl.BlockSpec((tm, tk), lambda i, j, k: (i, k))
        b_spec = pl.BlockSpec((tn, tk), lambda i, j, k: (j, k))
    else:
        a_spec = pl.BlockSpec((tk, tm), lambda i, j, k: (k, i))
        b_spec = pl.BlockSpec((tk, tn), lambda i, j, k: (k, j))
    o_spec = pl.BlockSpec((tm, tn), lambda i, j, k: (i, j))
    dts = out_dtypes if out_dtypes is not None else [out_dtype] * n_out
    return _matmul(a, b, mode=mode, grid=grid, a_spec=a_spec, b_spec=b_spec,
                   out_shapes=[jax.ShapeDtypeStruct((M, N), d) for d in dts],
                   out_specs=[o_spec] * len(dts), tile=(tm, tn), name=name,
                   extras=extras, extra_specs=[o_spec] * len(extras), epilogue=epilogue, after=after)


ROW_BLOCK = 256


def _rmsnorm_fwd(x, g, name):
    T, D = x.shape

    def body(x_ref, g_ref, o_ref):
        xf = x_ref[...]
        r = lax.rsqrt(jnp.mean(xf * xf, axis=-1, keepdims=True) + EPS)
        o_ref[...] = (xf * r * g_ref[...]).astype(BF16)

    return pl.pallas_call(
        body, grid=(T // ROW_BLOCK,),
        in_specs=[pl.BlockSpec((ROW_BLOCK, D), lambda i: (i, 0)), pl.BlockSpec((1, D), lambda i: (0, 0))],
        out_specs=pl.BlockSpec((ROW_BLOCK, D), lambda i: (i, 0)),
        out_shape=jax.ShapeDtypeStruct((T, D), BF16), name=name, compiler_params=_cparams(("parallel",)),
    )(x, g)


def _rmsnorm_bwd(dh, x, g, dres, name):
    T, D = x.shape

    def body(dh_ref, x_ref, g_ref, dres_ref, dx_ref, dxb_ref, dg_ref):
        i = pl.program_id(0)
        xf = x_ref[...]
        r = lax.rsqrt(jnp.mean(xf * xf, axis=-1, keepdims=True) + EPS)
        xh = xf * r
        d = dh_ref[...]

        @pl.when(i == 0)
        def _():
            dg_ref[...] = jnp.zeros_like(dg_ref)

        dg_ref[...] += jnp.sum(d * xh, axis=0, keepdims=True)
        dxh = d * g_ref[...]
        dx = r * (dxh - xh * jnp.mean(dxh * xh, axis=-1, keepdims=True)) + dres_ref[...]
        dx_ref[...] = dx
        dxb_ref[...] = dx.astype(BF16)

    row = pl.BlockSpec((ROW_BLOCK, D), lambda i: (i, 0))
    vec = pl.BlockSpec((1, D), lambda i: (0, 0))
    return pl.pallas_call(
        body, grid=(T // ROW_BLOCK,), in_specs=[row, row, vec, row], out_specs=[row, row, vec],
        out_shape=[jax.ShapeDtypeStruct((T, D), F32), jax.ShapeDtypeStruct((T, D), BF16),
                   jax.ShapeDtypeStruct((1, D), F32)],
        name=name, compiler_params=_cparams(("arbitrary",)),
    )(dh, x, g, dres)


def _final_loss(x3, tgt, g, name):
    T, D = x3.shape

    def body(x_ref, t_ref, g_ref, loss_ref, dg_ref, dx_ref, dxb_ref):
        i = pl.program_id(0)
        xf = x_ref[...]
        r = lax.rsqrt(jnp.mean(xf * xf, axis=-1, keepdims=True) + EPS)
        xh = xf * r
        gg = g_ref[...]
        err = xh * gg - t_ref[...]

        @pl.when(i == 0)
        def _():
            dg_ref[...] = jnp.zeros_like(dg_ref)
            loss_ref[...] = jnp.zeros_like(loss_ref)

        part = jnp.sum(jnp.sum(err * err, axis=-1, keepdims=True), axis=0, keepdims=True) * (0.5 / D)
        loss_ref[...] += jnp.broadcast_to(part, loss_ref.shape)
        dout = err * (1.0 / D)
        dg_ref[...] += jnp.sum(dout * xh, axis=0, keepdims=True)
        dxh = dout * gg
        dx = r * (dxh - xh * jnp.mean(dxh * xh, axis=-1, keepdims=True))
        dx_ref[...] = dx
        dxb_ref[...] = dx.astype(BF16)

    row = pl.BlockSpec((ROW_BLOCK, D), lambda i: (i, 0))
    vec = pl.BlockSpec((1, D), lambda i: (0, 0))
    return pl.pallas_call(
        body, grid=(T // ROW_BLOCK,), in_specs=[row, row, vec],
        out_specs=[pl.BlockSpec((1, 128), lambda i: (0, 0)), vec, row, row],
        out_shape=[jax.ShapeDtypeStruct((1, 128), F32), jax.ShapeDtypeStruct((1, D), F32),
                   jax.ShapeDtypeStruct((T, D), F32), jax.ShapeDtypeStruct((T, D), BF16)],
        name=name, compiler_params=_cparams(("arbitrary",)),
    )(x3, tgt, g)


CONV_BLOCK = 256


def _conv_apply(u, w, b):
    row = lax.broadcasted_iota(jnp.int32, u.shape, 0)
    acc = b + w[CONV_K - 1:CONV_K, :] * u
    shifted = []
    for j in range(1, CONV_K):
        uj = jnp.where(row >= j, pltpu.roll(u, j, axis=0), 0.0)
        shifted.append(uj)
        acc = acc + w[CONV_K - 1 - j:CONV_K - j, :] * uj
    return acc, shifted


def _conv_fwd(proj, conv_w, conv_b, name):
    T = proj.shape[0]
    cb0 = OFF_X // CONV_BLOCK

    def body(u_ref, w_ref, b_ref, o_ref):
        c, _ = _conv_apply(u_ref[...], w_ref[...], b_ref[...])
        o_ref[...] = c * _sigmoid(c)

    return pl.pallas_call(
        body, grid=(CONV_DIM // CONV_BLOCK,),
        in_specs=[pl.BlockSpec((T, CONV_BLOCK), lambda j: (0, cb0 + j)),
                  pl.BlockSpec((CONV_K, CONV_BLOCK), lambda j: (0, j)),
                  pl.BlockSpec((1, CONV_BLOCK), lambda j: (0, j))],
        out_specs=pl.BlockSpec((T, CONV_BLOCK), lambda j: (0, j)),
        out_shape=jax.ShapeDtypeStruct((T, CONV_DIM), F32), name=name, compiler_params=_cparams(("parallel",)),
    )(proj, conv_w, conv_b)


def _conv_bwd(proj, dact, conv_w, conv_b, name):
    T = proj.shape[0]
    cb0 = OFF_X // CONV_BLOCK

    def body(u_ref, d_ref, w_ref, b_ref, du_ref, dw_ref, db_ref):
        u = u_ref[...]
        w = w_ref[...]
        c, shifted = _conv_apply(u, w, b_ref[...])
        sg = _sigmoid(c)
        dc = d_ref[...] * sg * (1.0 + c * (1.0 - sg))
        row = lax.broadcasted_iota(jnp.int32, u.shape, 0)
        du = w[CONV_K - 1:CONV_K, :] * dc
        dw_ref[CONV_K - 1:CONV_K, :] = jnp.sum(dc * u, axis=0, keepdims=True)
        for j in range(1, CONV_K):
            dcj = jnp.where(row < T - j, pltpu.roll(dc, T - j, axis=0), 0.0)
            du = du + w[CONV_K - 1 - j:CONV_K - j, :] * dcj
            dw_ref[CONV_K - 1 - j:CONV_K - j, :] = jnp.sum(dc * shifted[j - 1], axis=0, keepdims=True)
        db_ref[...] = jnp.sum(dc, axis=0, keepdims=True)
        du_ref[...] = du.astype(BF16)

    return pl.pallas_call(
        body, grid=(CONV_DIM // CONV_BLOCK,),
        in_specs=[pl.BlockSpec((T, CONV_BLOCK), lambda j: (0, cb0 + j)),
                  pl.BlockSpec((T, CONV_BLOCK), lambda j: (0, j)),
                  pl.BlockSpec((CONV_K, CONV_BLOCK), lambda j: (0, j)),
                  pl.BlockSpec((1, CONV_BLOCK), lambda j: (0, j))],
        out_specs=[pl.BlockSpec((T, CONV_BLOCK), lambda j: (0, j)),
                   pl.BlockSpec((CONV_K, CONV_BLOCK), lambda j: (0, j)),
                   pl.BlockSpec((1, CONV_BLOCK), lambda j: (0, j))],
        out_shape=[jax.ShapeDtypeStruct((T, CONV_DIM), BF16), jax.ShapeDtypeStruct((CONV_K, CONV_DIM), F32),
                   jax.ShapeDtypeStruct((1, CONV_DIM), F32)],
        name=name, compiler_params=_cparams(("parallel",)),
    )(proj, dact, conv_w, conv_b)


GROUP_W = D_INNER // N_GROUPS
HEADS_PER_GROUP = N_HEADS // N_GROUPS


def _expand_mat():
    h = lax.broadcasted_iota(jnp.int32, (N_HEADS, D_INNER), 0)
    j = lax.broadcasted_iota(jnp.int32, (N_HEADS, D_INNER), 1)
    return (j // HEAD_DIM == h).astype(F32)


def _reduce_mat(g):
    j = lax.broadcasted_iota(jnp.int32, (GROUP_W, N_HEADS), 0)
    h = lax.broadcasted_iota(jnp.int32, (GROUP_W, N_HEADS), 1)
    return (g * HEADS_PER_GROUP + j // HEAD_DIM == h).astype(F32)


def _col16(v, h):
    lane = lax.broadcasted_iota(jnp.int32, v.shape, 1)
    return jnp.sum(jnp.where(lane == h, v, 0.0), axis=1, keepdims=True)


def _ssd_pre(dt_raw, dtT_raw, dtb, dtbT, alog, alogT):
    Q = CHUNK
    xdt = dt_raw + dtb
    dt = _softplus(xdt)
    dtT = _softplus(dtT_raw + dtbT)
    A = -jnp.exp(alog)
    AT = -jnp.exp(alogT)
    row = lax.broadcasted_iota(jnp.int32, (Q, Q), 0)
    col = lax.broadcasted_iota(jnp.int32, (Q, Q), 1)
    tril = (row >= col).astype(F32)
    triu = (row <= col).astype(F32)
    cs = _hdot(tril, dt * A)
    csT = _hdot(dtT * AT, triu)
    return xdt, dt, A, cs, csT, row >= col, triu


def _decay_matrix(cs, csT, h, causal):
    seg = _col16(cs, h) - csT[h:h + 1, :]
    return jnp.where(causal, jnp.exp(jnp.minimum(seg, 0.0)), 0.0)


def _ssd_in_specs(nc, rev):
    def cidx(c):
        return (nc - 1 - c) if rev else c

    return [
        pl.BlockSpec((CHUNK, D_INNER), lambda c: (cidx(c), 0)),
        pl.BlockSpec((CHUNK, 512), lambda c: (cidx(c), 2)),
        pl.BlockSpec((CHUNK, 512), lambda c: (cidx(c), 3)),
        pl.BlockSpec((CHUNK, D_INNER), lambda c: (cidx(c), 0)),
        pl.BlockSpec((CHUNK, 128), lambda c: (cidx(c), OFF_DT // 128)),
        pl.BlockSpec((N_HEADS, CHUNK), lambda c: (0, cidx(c))),
        pl.BlockSpec((1, N_HEADS), lambda c: (0, 0)),
        pl.BlockSpec((N_HEADS, 1), lambda c: (0, 0)),
        pl.BlockSpec((1, N_HEADS), lambda c: (0, 0)),
        pl.BlockSpec((N_HEADS, 1), lambda c: (0, 0)),
        pl.BlockSpec((1, D_INNER), lambda c: (0, 0)),
        pl.BlockSpec((1, D_INNER), lambda c: (0, 0)),
    ]


def _ssd_fwd(xbc, proj, dtT, dtb, dtbT, alog, alogT, dfull, ng, name):
    T = xbc.shape[0]
    nc = T // CHUNK
    Q = CHUNK

    def body(xs_ref, B_ref, C_ref, z_ref, dt_ref, dtT_ref, dtb_ref, dtbT_ref, al_ref, alT_ref, df_ref, ng_ref,
             y_ref, ypre_ref, hs_ref, h_scr):
        c = pl.program_id(0)

        @pl.when(c == 0)
        def _():
            h_scr[...] = jnp.zeros_like(h_scr)

        _, dt, _, cs, csT, causal, _ = _ssd_pre(dt_ref[:, :N_HEADS], dtT_ref[...], dtb_ref[...], dtbT_ref[...],
                                                al_ref[...], alT_ref[...])
        ex = _expand_mat()
        dt_full = _hdot(dt, ex)
        cs_full = _hdot(cs, ex)
        cs_last = cs_full[Q - 1:Q, :]
        xs = xs_ref[...]
        xd = xs * dt_full
        e_full = jnp.exp(cs_full)
        dec_full = jnp.exp(cs_last - cs_full)
        cd_full = jnp.exp(cs_last)
        lane_head = lax.broadcasted_iota(jnp.int32, (1, GROUP_W), 1) // HEAD_DIM
        for g in range(N_GROUPS):
            sl = slice(g * GROUP_W, (g + 1) * GROUP_W)
            Bg = B_ref[:, g * D_STATE:(g + 1) * D_STATE].astype(BF16)
            Cg = C_ref[:, g * D_STATE:(g + 1) * D_STATE].astype(BF16)
            CB = _dot_nt(Cg, Bg)
            hg = h_scr[g]
            yoff = _dot_nn(Cg, hg.astype(BF16)) * e_full[:, sl]
            xd_g = xd[:, sl]
            S = _dot_tn(Bg, (xd_g * dec_full[:, sl]).astype(BF16))
            xd_b = xd_g.astype(BF16)
            ydiag = jnp.zeros((Q, GROUP_W), F32)
            for r in range(HEADS_PER_GROUP):
                Lm = _decay_matrix(cs, csT, g * HEADS_PER_GROUP + r, causal)
                Gm = (CB * Lm).astype(BF16)
                ydiag = ydiag + _dot_nn(Gm, jnp.where(lane_head == r, xd_b, jnp.zeros_like(xd_b)))
            hs_ref[0, g] = hg
            h_scr[g] = hg * cd_full[:, sl] + S
            ypre = ydiag + yoff + xs[:, sl] * df_ref[:, sl]
            ypre_ref[:, sl] = ypre
            zg = z_ref[:, sl]
            yz = ypre * zg * _sigmoid(zg)
            rn = lax.rsqrt(jnp.mean(yz * yz, axis=-1, keepdims=True) + EPS)
            y_ref[:, sl] = (yz * rn * ng_ref[:, sl]).astype(BF16)

    return pl.pallas_call(
        body, grid=(nc,), in_specs=_ssd_in_specs(nc, False),
        out_specs=[pl.BlockSpec((CHUNK, D_INNER), lambda c: (c, 0)),
                   pl.BlockSpec((CHUNK, D_INNER), lambda c: (c, 0)),
                   pl.BlockSpec((1, N_GROUPS, D_STATE, GROUP_W), lambda c: (c, 0, 0, 0))],
        out_shape=[jax.ShapeDtypeStruct((T, D_INNER), BF16), jax.ShapeDtypeStruct((T, D_INNER), F32),
                   jax.ShapeDtypeStruct((nc, N_GROUPS, D_STATE, GROUP_W), F32)],
        scratch_shapes=[pltpu.VMEM((N_GROUPS, D_STATE, GROUP_W), F32)],
        name=name, compiler_params=_cparams(("arbitrary",)),
    )(xbc, xbc, xbc, proj, proj, dtT, dtb, dtbT, alog, alogT, dfull, ng)


def _ssd_bwd(xbc, proj, dtT, dtb, dtbT, alog, alogT, dfull, ng, ypre, hs, dy, name):
    T = xbc.shape[0]
    nc = T // CHUNK
    Q = CHUNK

    def body(xs_ref, B_ref, C_ref, z_ref, dt_ref, dtT_ref, dtb_ref, dtbT_ref, al_ref, alT_ref, df_ref, ng_ref,
             ypre_ref, hs_ref, dy_ref,
             dz_ref, dxbc_ref, ddt_ref, ddtb_ref, dal_ref, dD_ref, dng_ref, dh_scr):
        step = pl.program_id(0)

        @pl.when(step == 0)
        def _():
            dh_scr[...] = jnp.zeros_like(dh_scr)
            ddtb_ref[...] = jnp.zeros_like(ddtb_ref)
            dal_ref[...] = jnp.zeros_like(dal_ref)
            dD_ref[...] = jnp.zeros_like(dD_ref)
            dng_ref[...] = jnp.zeros_like(dng_ref)

        xdt, dt, A, cs, csT, causal, triu = _ssd_pre(dt_ref[:, :N_HEADS], dtT_ref[...], dtb_ref[...],
                                                    dtbT_ref[...], al_ref[...], alT_ref[...])
        ex = _expand_mat()
        dt_full = _hdot(dt, ex)
        cs_full = _hdot(cs, ex)
        cs_last = cs_full[Q - 1:Q, :]
        xs = xs_ref[...]
        xd = xs * dt_full
        e_full = jnp.exp(cs_full)
        dec_full = jnp.exp(cs_last - cs_full)
        cd_full = jnp.exp(cs_last)
        lane_head = lax.broadcasted_iota(jnp.int32, (1, GROUP_W), 1) // HEAD_DIM
        is_last = lax.broadcasted_iota(jnp.int32, (Q, 1), 0) == Q - 1
        dcs16 = jnp.zeros((Q, N_HEADS), F32)
        ddtx16 = jnp.zeros((Q, N_HEADS), F32)
        dD16 = jnp.zeros((8, N_HEADS), F32)
        lane16 = lax.broadcasted_iota(jnp.int32, (1, N_HEADS), 1)
        sub16 = lax.broadcasted_iota(jnp.int32, (N_HEADS, 1), 0)
        col_sums = jnp.zeros((N_HEADS, Q), F32)
        for g in range(N_GROUPS):
            sl = slice(g * GROUP_W, (g + 1) * GROUP_W)
            red = _reduce_mat(g)
            ypre_g = ypre_ref[:, sl]
            zg = z_ref[:, sl]
            sg = _sigmoid(zg)
            silu = zg * sg
            yz = ypre_g * silu
            rn = lax.rsqrt(jnp.mean(yz * yz, axis=-1, keepdims=True) + EPS)
            yh = yz * rn
            dy_g = dy_ref[:, sl]
            dng_ref[:, sl] += jnp.sum(dy_g * yh, axis=0, keepdims=True)
            dyh = dy_g * ng_ref[:, sl]
            dyz = rn * (dyh - yh * jnp.mean(dyh * yh, axis=-1, keepdims=True))
            dY = dyz * silu
            dz_ref[:, sl] = (dyz * ypre_g * sg * (1.0 + zg * (1.0 - sg))).astype(BF16)
            xs_g = xs[:, sl]
            xd_g = xd[:, sl]
            dec_g = dec_full[:, sl]
            cd_g = cd_full[:, sl]
            d_g = df_ref[:, sl]
            Bg = B_ref[:, g * D_STATE:(g + 1) * D_STATE].astype(BF16)
            Cg = C_ref[:, g * D_STATE:(g + 1) * D_STATE].astype(BF16)
            CB = _dot_nt(Cg, Bg)
            hg = hs_ref[0, g]
            hgb = hg.astype(BF16)
            yoff = _dot_nn(Cg, hgb) * e_full[:, sl]
            dhn = dh_scr[g]
            dhnb = dhn.astype(BF16)
            dYE = (dY * e_full[:, sl]).astype(BF16)
            dC = _dot_nt(dYE, hgb)
            dh_direct = _dot_tn(Cg, dYE)
            dXdd = _dot_nn(Bg, dhnb)
            dB = _dot_nt((xd_g * dec_g).astype(BF16), dhnb)
            dcd = jnp.sum(dhn * hg, axis=0, keepdims=True)
            dh_scr[g] = dh_direct + cd_g * dhn
            dYb = dY.astype(BF16)
            xd_b = xd_g.astype(BF16)
            dCB = jnp.zeros((Q, Q), F32)
            dXd = dXdd * dec_g
            for r in range(HEADS_PER_GROUP):
                h = g * HEADS_PER_GROUP + r
                Lm = _decay_matrix(cs, csT, h, causal)
                Gf = CB * Lm
                dYr = jnp.where(lane_head == r, dYb, jnp.zeros_like(dYb))
                dG = _dot_nt(dYr, xd_b)
                dCB = dCB + dG * Lm
                dXd = dXd + _dot_tn(Gf.astype(BF16), dYr)
                Mm = dG * Gf
                dcs16 = dcs16 + jnp.where(lane16 == h, jnp.sum(Mm, axis=1, keepdims=True), 0.0)
                col_sums = col_sums + jnp.where(sub16 == h, jnp.sum(Mm, axis=0, keepdims=True), 0.0)
            dCBb = dCB.astype(BF16)
            dC = dC + _dot_nn(dCBb, Bg)
            dB = dB + _dot_tn(dCBb, Cg)
            w_state = dXdd * dec_g * xd_g
            t_last = jnp.sum(w_state, axis=0, keepdims=True) + dcd * cd_g
            dcs_g = dY * yoff - w_state + jnp.where(is_last, t_last, 0.0)
            dcs16 = dcs16 + _hdot(dcs_g, red)
            ddtx16 = ddtx16 + _hdot(dXd * xs_g, red)
            dD16 = dD16 + _hdot(jnp.broadcast_to(jnp.sum(dY * xs_g, axis=0, keepdims=True), (8, GROUP_W)), red)
            dxbc_ref[:, sl] = dXd * dt_full[:, sl] + dY * d_g
            dxbc_ref[:, D_INNER + g * D_STATE:D_INNER + (g + 1) * D_STATE] = dB
            dxbc_ref[:, D_INNER + 512 + g * D_STATE:D_INNER + 512 + (g + 1) * D_STATE] = dC
        eye = (lax.broadcasted_iota(jnp.int32, (N_HEADS, N_HEADS), 0)
               == lax.broadcasted_iota(jnp.int32, (N_HEADS, N_HEADS), 1)).astype(F32)
        dcs16 = dcs16 - lax.dot_general(col_sums, eye, (((0,), (0,)), ((), ())), precision=HI,
                                        preferred_element_type=F32)
        da = _hdot(triu, dcs16)
        ddt = da * A + ddtx16
        ddt_raw = ddt * _sigmoid(xdt)
        pr = lax.broadcasted_iota(jnp.int32, (N_HEADS, 128), 0)
        pc = lax.broadcasted_iota(jnp.int32, (N_HEADS, 128), 1)
        ddt_ref[...] = _hdot(ddt_raw, (pr == pc).astype(F32))
        ddtb_ref[...] += jnp.sum(ddt_raw, axis=0, keepdims=True)
        dal_ref[...] += jnp.sum(da * dt, axis=0, keepdims=True) * A
        dD_ref[...] += dD16[0:1, :]

    def rc(c):
        return nc - 1 - c

    in_specs = _ssd_in_specs(nc, True) + [
        pl.BlockSpec((CHUNK, D_INNER), lambda c: (rc(c), 0)),
        pl.BlockSpec((1, N_GROUPS, D_STATE, GROUP_W), lambda c: (rc(c), 0, 0, 0)),
        pl.BlockSpec((CHUNK, D_INNER), lambda c: (rc(c), 0)),
    ]
    small = pl.BlockSpec((1, N_HEADS), lambda c: (0, 0))
    return pl.pallas_call(
        body, grid=(nc,), in_specs=in_specs,
        out_specs=[pl.BlockSpec((CHUNK, D_INNER), lambda c: (rc(c), 0)),
                   pl.BlockSpec((CHUNK, CONV_DIM), lambda c: (rc(c), 0)),
                   pl.BlockSpec((CHUNK, 128), lambda c: (rc(c), 0)),
                   small, small, small,
                   pl.BlockSpec((1, D_INNER), lambda c: (0, 0))],
        out_shape=[jax.ShapeDtypeStruct((T, D_INNER), BF16), jax.ShapeDtypeStruct((T, CONV_DIM), F32),
                   jax.ShapeDtypeStruct((T, 128), F32),
                   jax.ShapeDtypeStruct((1, N_HEADS), F32), jax.ShapeDtypeStruct((1, N_HEADS), F32),
                   jax.ShapeDtypeStruct((1, N_HEADS), F32), jax.ShapeDtypeStruct((1, D_INNER), F32)],
        scratch_shapes=[pltpu.VMEM((N_GROUPS, D_STATE, GROUP_W), F32)],
        name=name, compiler_params=_cparams(("arbitrary",)),
    )(xbc, xbc, xbc, proj, proj, dtT, dtb, dtbT, alog, alogT, dfull, ng, ypre, hs, dy)


N_PAIRS = ATTN_W // 128
PAIRS_PER_KV = N_PAIRS // 2
ATTN_SCALE = HEAD_DIM ** -0.5


def _kv_variants(kk):
    lo = lax.broadcasted_iota(jnp.int32, kk.shape, 1) < HEAD_DIM
    zero = jnp.zeros_like(kk)
    k00 = jnp.where(lo, kk, zero)
    k11 = jnp.where(lo, zero, kk)
    k01 = pltpu.roll(k00, HEAD_DIM, axis=1)
    k10 = pltpu.roll(k11, HEAD_DIM, axis=1)
    return [[k00.astype(BF16), k01.astype(BF16)], [k10.astype(BF16), k11.astype(BF16)]]


def _attn_valid(n):
    i = lax.broadcasted_iota(jnp.int32, (WINDOW, 2 * WINDOW), 0)
    j = lax.broadcasted_iota(jnp.int32, (WINDOW, 2 * WINDOW), 1)
    return (j > i) & (j <= i + WINDOW) & (n * WINDOW + j >= WINDOW)


def _attn_probs(qp, kvar, valid, sk):
    s = _dot_nt(qp, kvar) * ATTN_SCALE
    s = jnp.where(valid, s, NEG)
    m = jnp.maximum(jnp.max(s, axis=1, keepdims=True), sk)
    pe = jnp.exp(s - m)
    es = jnp.exp(sk - m)
    den = jnp.sum(pe, axis=1, keepdims=True) + es
    inv = 1.0 / den
    return pe * inv, es * inv


def _sink(sinks, r):
    lane = lax.broadcasted_iota(jnp.int32, sinks.shape, 1)
    return jnp.sum(jnp.where(lane == r, sinks, 0.0), axis=1, keepdims=True)


def _attn_fwd(proj, kpad, vpad, sinks, og, name):
    T = proj.shape[0]
    nb = T // WINDOW

    def body(q_ref, k_ref, v_ref, s_ref, og_ref, y_ref, o_ref):
        n = pl.program_id(0)
        start = pl.multiple_of(n * WINDOW, WINDOW)
        kv = _kv_variants(k_ref[pl.ds(start, 2 * WINDOW), :])
        vv = _kv_variants(v_ref[pl.ds(start, 2 * WINDOW), :])
        valid = _attn_valid(n)
        sinks_v = s_ref[...]
        ssq = jnp.zeros((WINDOW, 1), F32)
        for p in range(N_PAIRS):
            j = p // PAIRS_PER_KV
            qp = q_ref[:, p * 128:(p + 1) * 128].astype(BF16)
            o_pair = jnp.zeros((WINDOW, 128), F32)
            for par in range(2):
                pn, _ = _attn_probs(qp, kv[j][par], valid, _sink(sinks_v, 2 * p + par))
                o_pair = o_pair + _dot_nn(pn.astype(BF16), vv[j][par])
            o_ref[:, p * 128:(p + 1) * 128] = o_pair
            ssq = ssq + jnp.sum(o_pair * o_pair, axis=1, keepdims=True)
        rn = lax.rsqrt(ssq * (1.0 / ATTN_W) + EPS)
        y_ref[...] = (o_ref[...] * rn * og_ref[...]).astype(BF16)

    full_kv = pl.BlockSpec((T + WINDOW, KV_W), lambda n: (0, 0))
    return pl.pallas_call(
        body, grid=(nb,),
        in_specs=[pl.BlockSpec((WINDOW, ATTN_W), lambda n: (n, OFF_Q // ATTN_W)), full_kv, full_kv,
                  pl.BlockSpec((1, N_HEADS), lambda n: (0, 0)), pl.BlockSpec((1, ATTN_W), lambda n: (0, 0))],
        out_specs=[pl.BlockSpec((WINDOW, ATTN_W), lambda n: (n, 0)), pl.BlockSpec((WINDOW, ATTN_W), lambda n: (n, 0))],
        out_shape=[jax.ShapeDtypeStruct((T, ATTN_W), BF16), jax.ShapeDtypeStruct((T, ATTN_W), F32)],
        name=name, compiler_params=_cparams(("parallel",)),
    )(proj, kpad, vpad, sinks, og)


def _attn_bwd(proj, kpad, vpad, sinks, og, o, dy, name):
    T = proj.shape[0]
    nb = T // WINDOW

    def body(q_ref, k_ref, v_ref, s_ref, og_ref, o_ref, dy_ref, dq_ref, dk_ref, dv_ref, ds_ref, dog_ref):
        n = pl.program_id(0)

        @pl.when(n == 0)
        def _():
            dk_ref[...] = jnp.zeros_like(dk_ref)
            dv_ref[...] = jnp.zeros_like(dv_ref)
            ds_ref[...] = jnp.zeros_like(ds_ref)
            dog_ref[...] = jnp.zeros_like(dog_ref)

        start = pl.multiple_of(n * WINDOW, WINDOW)
        kv = _kv_variants(k_ref[pl.ds(start, 2 * WINDOW), :])
        vv = _kv_variants(v_ref[pl.ds(start, 2 * WINDOW), :])
        valid = _attn_valid(n)
        sinks_v = s_ref[...]
        of = o_ref[...]
        rn = lax.rsqrt(jnp.mean(of * of, axis=-1, keepdims=True) + EPS)
        oh = of * rn
        dyf = dy_ref[...]
        dog_ref[...] += jnp.sum(dyf * oh, axis=0, keepdims=True)
        doh = dyf * og_ref[...]
        do = rn * (doh - oh * jnp.mean(doh * oh, axis=-1, keepdims=True))
        lane = lax.broadcasted_iota(jnp.int32, (1, 128), 1)
        lane16 = lax.broadcasted_iota(jnp.int32, (1, N_HEADS), 1)
        dk_acc = [[jnp.zeros((2 * WINDOW, 128), F32) for _ in range(2)] for _ in range(2)]
        dv_acc = [[jnp.zeros((2 * WINDOW, 128), F32) for _ in range(2)] for _ in range(2)]
        dsink = jnp.zeros((1, N_HEADS), F32)
        for p in range(N_PAIRS):
            j = p // PAIRS_PER_KV
            qp = q_ref[:, p * 128:(p + 1) * 128].astype(BF16)
            do_p = do[:, p * 128:(p + 1) * 128]
            o_p = of[:, p * 128:(p + 1) * 128]
            do_b = do_p.astype(BF16)
            prod = do_p * o_p
            dq_pair = jnp.zeros((WINDOW, 128), F32)
            for par in range(2):
                r = 2 * p + par
                half = (lane < HEAD_DIM) if par == 0 else (lane >= HEAD_DIM)
                pn, ps = _attn_probs(qp, kv[j][par], valid, _sink(sinks_v, r))
                delta = jnp.sum(jnp.where(half, prod, 0.0), axis=1, keepdims=True)
                dP = _dot_nt(do_b, vv[j][par])
                dS = pn * (dP - delta)
                dsink = dsink + jnp.where(lane16 == r, -jnp.sum(ps * delta, axis=0, keepdims=True), 0.0)
                dSb = (dS * ATTN_SCALE).astype(BF16)
                dq_pair = dq_pair + _dot_nn(dSb, kv[j][par])
                dk_acc[j][par] = dk_acc[j][par] + _dot_tn(dSb, jnp.where(half, qp, jnp.zeros_like(qp)))
                dv_acc[j][par] = dv_acc[j][par] + _dot_tn(pn.astype(BF16), jnp.where(half, do_b, jnp.zeros_like(do_b)))
            dq_ref[:, p * 128:(p + 1) * 128] = dq_pair.astype(BF16)
        dkk = (dk_acc[0][0] + pltpu.roll(dk_acc[0][1], HEAD_DIM, axis=1)
               + dk_acc[1][1] + pltpu.roll(dk_acc[1][0], HEAD_DIM, axis=1))
        dvv = (dv_acc[0][0] + pltpu.roll(dv_acc[0][1], HEAD_DIM, axis=1)
               + dv_acc[1][1] + pltpu.roll(dv_acc[1][0], HEAD_DIM, axis=1))
        dk_ref[pl.ds(start, 2 * WINDOW), :] += dkk
        dv_ref[pl.ds(start, 2 * WINDOW), :] += dvv
        ds_ref[...] += dsink

    full_kv = pl.BlockSpec((T + WINDOW, KV_W), lambda n: (0, 0))
    blk = pl.BlockSpec((WINDOW, ATTN_W), lambda n: (n, 0))
    return pl.pallas_call(
        body, grid=(nb,),
        in_specs=[pl.BlockSpec((WINDOW, ATTN_W), lambda n: (n, OFF_Q // ATTN_W)), full_kv, full_kv,
                  pl.BlockSpec((1, N_HEADS), lambda n: (0, 0)), pl.BlockSpec((1, ATTN_W), lambda n: (0, 0)),
                  blk, pl.BlockSpec((WINDOW, ATTN_W), lambda n: (n, 1))],
        out_specs=[blk, full_kv, full_kv, pl.BlockSpec((1, N_HEADS), lambda n: (0, 0)),
                   pl.BlockSpec((1, ATTN_W), lambda n: (0, 0))],
        out_shape=[jax.ShapeDtypeStruct((T, ATTN_W), BF16), jax.ShapeDtypeStruct((T + WINDOW, KV_W), F32),
                   jax.ShapeDtypeStruct((T + WINDOW, KV_W), F32), jax.ShapeDtypeStruct((1, N_HEADS), F32),
                   jax.ShapeDtypeStruct((1, ATTN_W), F32)],
        name=name, compiler_params=_cparams(("arbitrary",)),
    )(proj, kpad, vpad, sinks, og, o, dy)


ANY = pl.BlockSpec(memory_space=pl.ANY)


def _coords():
    return lax.axis_index("x"), lax.axis_index("y"), lax.axis_index("c")


def _all_gather(arrs, name):
    n = len(arrs)

    def body(*refs):
        ins, outs = refs[:n], refs[n:2 * n]
        send_sems, recv_sems, local_sems = refs[2 * n:]
        x, y, c = _coords()
        me = 4 * x + 2 * y + c
        sibling = (x, y, 1 - c)
        chips = [(1 - x, y), (x, 1 - y), (1 - x, 1 - y)]

        def copy(a, k, block, to, src=None):
            dst = outs[a].at[block]
            return pltpu.make_async_remote_copy(
                src_ref=dst if src is None else src, dst_ref=dst, send_sem=send_sems.at[a, k],
                recv_sem=recv_sems.at[a, k], device_id=to, device_id_type=MESH)

        mine = [pltpu.make_async_copy(ins[a], outs[a].at[me], local_sems.at[a]) for a in range(n)]
        for cp in mine:
            cp.start()
        first = []
        for a in range(n):
            first.append(copy(a, 0, me, sibling, src=ins[a]))
            for j, chip in enumerate(chips):
                first.append(copy(a, 1 + j, me, (*chip, c), src=ins[a]))
        for cp in first:
            cp.start()
        passed = []
        for j, (px, py) in enumerate(chips):
            blk = 4 * px + 2 * py + c
            for a in range(n):
                copy(a, 1 + j, blk, sibling).wait_recv()
                fwd = copy(a, 4 + j, blk, sibling)
                fwd.start()
                passed.append(fwd)
        for a in range(n):
            copy(a, 0, 4 * x + 2 * y + (1 - c), sibling).wait_recv()
            for j, (px, py) in enumerate(chips):
                copy(a, 4 + j, 4 * px + 2 * py + (1 - c), sibling).wait_recv()
        for cp in first + passed:
            cp.wait_send()
        for cp in mine:
            cp.wait()

    return pl.pallas_call(
        body, in_specs=[ANY] * n, out_specs=[ANY] * n,
        out_shape=[jax.ShapeDtypeStruct((N_DEV,) + a.shape, a.dtype) for a in arrs],
        scratch_shapes=[pltpu.SemaphoreType.DMA((n, 7)), pltpu.SemaphoreType.DMA((n, 7)),
                        pltpu.SemaphoreType.DMA((n,))],
        name=name,
    )(*arrs)


def _exchange_pair(arrs, name):
    n = len(arrs)

    def body(*refs):
        ins, outs = refs[:n], refs[n:2 * n]
        send_sems, recv_sems = refs[2 * n:]
        x, y, c = _coords()
        cps = []
        for a in range(n):
            for q in range(4):
                cps.append(pltpu.make_async_remote_copy(
                    src_ref=ins[a].at[2 * q + (1 - c)], dst_ref=outs[a].at[q], send_sem=send_sems.at[a, q],
                    recv_sem=recv_sems.at[a, q], device_id=(x, y, 1 - c), device_id_type=MESH))
        for cp in cps:
            cp.start()
        for cp in cps:
            cp.wait()

    return pl.pallas_call(
        body, in_specs=[ANY] * n, out_specs=[ANY] * n,
        out_shape=[jax.ShapeDtypeStruct((4,) + a.shape[1:], a.dtype) for a in arrs],
        scratch_shapes=[pltpu.SemaphoreType.DMA((n, 4)), pltpu.SemaphoreType.DMA((n, 4))],
        name=name,
    )(*arrs)


def _exchange_chips(arrs, name):
    n = len(arrs)

    def body(*refs):
        ins, outs = refs[:n], refs[n:2 * n]
        send_sems, recv_sems = refs[2 * n:]
        x, y, c = _coords()
        chips = [(1 - x, y), (x, 1 - y), (1 - x, 1 - y)]
        cps = []
        for a in range(n):
            for k, (tx, ty) in enumerate(chips):
                cps.append(pltpu.make_async_remote_copy(
                    src_ref=ins[a].at[2 * tx + ty], dst_ref=outs[a].at[k], send_sem=send_sems.at[a, k],
                    recv_sem=recv_sems.at[a, k], device_id=(tx, ty, c), device_id_type=MESH))
        for cp in cps:
            cp.start()
        for cp in cps:
            cp.wait()

    return pl.pallas_call(
        body, in_specs=[ANY] * n, out_specs=[ANY] * n,
        out_shape=[jax.ShapeDtypeStruct((3,) + a.shape[1:], a.dtype) for a in arrs],
        scratch_shapes=[pltpu.SemaphoreType.DMA((n, 3)), pltpu.SemaphoreType.DMA((n, 3))],
        name=name,
    )(*arrs)


HBM = pl.BlockSpec(memory_space=pltpu.HBM)
SEM = pl.BlockSpec(memory_space=pltpu.SEMAPHORE)
EFFECT = pltpu.SideEffectType.DATAFLOW_SIDE_EFFECTING


def _in_hbm(a):
    return pltpu.with_memory_space_constraint(a, pltpu.HBM)


def _remote_start(srcs, lands, plan, n_copies, name, after=None):
    n = len(srcs)
    n_after = 0 if after is None else 1

    def body(*refs):
        src_refs, land_refs = refs[:n], refs[n:2 * n]
        send_sems, recv_sems = refs[2 * n + n_after], refs[2 * n + n_after + 1]
        token = refs[-1]
        x, y, c = _coords()
        for i, (sv, dv, dev) in enumerate(plan(src_refs, land_refs, x, y, c)):
            pltpu.make_async_remote_copy(src_ref=sv, dst_ref=dv, send_sem=send_sems.at[i], recv_sem=recv_sems.at[i],
                                         device_id=dev, device_id_type=MESH).start()
        token[...] = jnp.zeros_like(token)

    bufs = list(srcs) + list(lands)
    outs = pl.pallas_call(
        body, name=name,
        out_shape=(pltpu.SemaphoreType.DMA((n_copies,)), pltpu.SemaphoreType.DMA((n_copies,)),
                   *[pltpu.HBM(b.shape, b.dtype) for b in bufs], jax.ShapeDtypeStruct((8, 128), F32)),
        in_specs=[HBM] * (2 * n) + [ANY] * n_after,
        out_specs=(SEM, SEM, *[HBM] * (2 * n), pl.BlockSpec(memory_space=pltpu.VMEM)),
        input_output_aliases={i: 2 + i for i in range(2 * n)},
        compiler_params=pltpu.CompilerParams(has_side_effects=EFFECT),
    )(*[_in_hbm(b) for b in bufs], *([] if after is None else [after]))
    return outs[0], outs[1], list(outs[2:2 + n]), list(outs[2 + n:2 + 2 * n]), outs[-1]


def _remote_wait(started, after, plan, name):
    send_sems, recv_sems, srcs, lands, _ = started
    n = len(srcs)

    def body(*refs):
        src_refs, land_refs = refs[:n], refs[n:2 * n]
        send_sems, recv_sems = refs[2 * n], refs[2 * n + 1]
        x, y, c = _coords()
        for i, (sv, dv, dev) in enumerate(plan(src_refs, land_refs, x, y, c)):
            cp = pltpu.make_async_remote_copy(src_ref=sv, dst_ref=dv, send_sem=send_sems.at[i],
                                              recv_sem=recv_sems.at[i], device_id=dev, device_id_type=MESH)
            cp.wait_send()
            cp.wait_recv()

    bufs = list(srcs) + list(lands)
    outs = pl.pallas_call(
        body, name=name, out_shape=tuple(pltpu.HBM(b.shape, b.dtype) for b in bufs),
        in_specs=[HBM] * (2 * n) + [SEM, SEM, ANY], out_specs=tuple([HBM] * (2 * n)),
        input_output_aliases={i: i for i in range(2 * n)},
        compiler_params=pltpu.CompilerParams(has_side_effects=EFFECT),
    )(*bufs, send_sems, recv_sems, after)
    return list(outs[:n]), list(outs[n:])


def _gather_plan(src_refs, land_refs, x, y, c):
    me = 4 * x + 2 * y + c
    plan = []
    for s, l in zip(src_refs, land_refs):
        for dev in [(x, y, 1 - c), (1 - x, y, c), (x, 1 - y, c), (1 - x, 1 - y, c)]:
            plan.append((s, l.at[me], dev))
    return plan


def _pair_plan(src_refs, land_refs, x, y, c):
    plan = []
    for s, l in zip(src_refs, land_refs):
        for q in range(4):
            plan.append((s.at[2 * q + (1 - c)], l.at[q], (x, y, 1 - c)))
    return plan


def _chips_plan(src_refs, land_refs, x, y, c):
    plan = []
    for s, l in zip(src_refs, land_refs):
        for k, (tx, ty) in enumerate([(1 - x, y), (x, 1 - y), (1 - x, 1 - y)]):
            plan.append((s.at[2 * tx + ty], l.at[k], (tx, ty, c)))
    return plan


def _everyone_plan(src_refs, land_refs, x, y, c):
    me = 4 * x + 2 * y + c
    plan = []
    for s, l in zip(src_refs, land_refs):
        for fx, fy, fc in [(0, 0, 1), (1, 0, 0), (1, 0, 1), (0, 1, 0), (0, 1, 1), (1, 1, 0), (1, 1, 1)]:
            dev = ((1 - x) if fx else x, (1 - y) if fy else y, (1 - c) if fc else c)
            plan.append((s, l.at[me], dev))
    return plan


def _gather_finish(gathered, name):
    n = len(gathered)

    def body(*refs):
        outs = refs[n:2 * n]
        send_sems, recv_sems = refs[2 * n:]
        x, y, c = _coords()
        cps = []
        for a in range(n):
            for j, (px, py) in enumerate([(1 - x, y), (x, 1 - y), (1 - x, 1 - y)]):
                blk = outs[a].at[4 * px + 2 * py + c]
                got = outs[a].at[4 * px + 2 * py + (1 - c)]
                cps.append((pltpu.make_async_remote_copy(
                    src_ref=blk, dst_ref=blk, send_sem=send_sems.at[a, j], recv_sem=recv_sems.at[a, j],
                    device_id=(x, y, 1 - c), device_id_type=MESH), pltpu.make_async_remote_copy(
                    src_ref=got, dst_ref=got, send_sem=send_sems.at[a, j], recv_sem=recv_sems.at[a, j],
                    device_id=(x, y, 1 - c), device_id_type=MESH)))
        for cp, _ in cps:
            cp.start()
        for cp, arrival in cps:
            cp.wait_send()
            arrival.wait_recv()

    return pl.pallas_call(
        body, in_specs=[ANY] * n, out_specs=[ANY] * n,
        out_shape=[jax.ShapeDtypeStruct(g.shape, g.dtype) for g in gathered],
        input_output_aliases={a: a for a in range(n)},
        scratch_shapes=[pltpu.SemaphoreType.DMA((n, 3)), pltpu.SemaphoreType.DMA((n, 3))],
        name=name,
    )(*gathered)


def _pair_add(g8, r1, csel, tr, name):
    _, R, C = r1.shape
    g4 = g8.reshape(4, 2, R, C)

    def body(c_ref, g_ref, r_ref, o_ref):
        o_ref[...] = (g_ref[...].astype(F32) + r_ref[...].astype(F32)).astype(BF16)

    return pl.pallas_call(
        body,
        grid_spec=pltpu.PrefetchScalarGridSpec(
            num_scalar_prefetch=1, grid=(4, R // tr),
            in_specs=[pl.BlockSpec((None, None, tr, C), lambda q, i, cs: (q, cs[0], i, 0)),
                      pl.BlockSpec((None, tr, C), lambda q, i, cs: (q, i, 0))],
            out_specs=pl.BlockSpec((None, tr, C), lambda q, i, cs: (q, i, 0))),
        out_shape=jax.ShapeDtypeStruct((4, R, C), BF16), name=name,
        compiler_params=_cparams(("parallel", "parallel")),
    )(csel, g4, r1)


def _adamw_math(w, g, m, v):
    m = ADAM_B1 * m + (1.0 - ADAM_B1) * g
    v = ADAM_B2 * v + (1.0 - ADAM_B2) * (g * g)
    m_hat = m / (1.0 - ADAM_B1 ** ADAM_STEP)
    v_hat = v / (1.0 - ADAM_B2 ** ADAM_STEP)
    delta = -ADAM_LR * (m_hat / (jnp.sqrt(v_hat) + ADAM_EPS) + ADAM_WD * w)
    return delta, m, v


def _adamw_big(w, m, v, p4, r3, qsel, tr, name):
    R, C = w.shape

    def body(q_ref, w_ref, m_ref, v_ref, p_ref, r_ref, g_out, d_out, m_out, v_out):
        g = p_ref[...].astype(F32) + r_ref[0].astype(F32) + r_ref[1].astype(F32) + r_ref[2].astype(F32)
        d, mn, vn = _adamw_math(w_ref[...], g, m_ref[...], v_ref[...])
        g_out[...] = g
        d_out[...] = d
        m_out[...] = mn
        v_out[...] = vn

    blk = pl.BlockSpec((tr, C), lambda i, qs: (i, 0))
    return pl.pallas_call(
        body,
        grid_spec=pltpu.PrefetchScalarGridSpec(
            num_scalar_prefetch=1, grid=(R // tr,),
            in_specs=[blk, blk, blk, pl.BlockSpec((None, tr, C), lambda i, qs: (qs[0], i, 0)),
                      pl.BlockSpec((3, tr, C), lambda i, qs: (0, i, 0))],
            out_specs=[blk, blk, blk, blk]),
        out_shape=[jax.ShapeDtypeStruct((R, C), F32)] * 4, name=name,
        compiler_params=_cparams(("parallel",)),
    )(qsel, w, m, v, p4, r3)


def _small_sum(parts, name):
    def body(p_ref, o_ref):
        acc = p_ref[0]
        for d in range(1, N_DEV):
            acc = acc + p_ref[d]
        o_ref[...] = acc

    return pl.pallas_call(
        body, out_shape=jax.ShapeDtypeStruct(parts.shape[1:], F32), name=name,
        compiler_params=_cparams(),
    )(parts)


def _adamw_small(w, g, m, v, name):
    def body(w_ref, g_ref, m_ref, v_ref, d_out, m_out, v_out):
        d, mn, vn = _adamw_math(w_ref[...], g_ref[...], m_ref[...], v_ref[...])
        d_out[...] = d
        m_out[...] = mn
        v_out[...] = vn

    return pl.pallas_call(
        body, out_shape=[jax.ShapeDtypeStruct(w.shape, F32)] * 3, name=name, compiler_params=_cparams(),
    )(w, g, m, v)


def _row(*pieces):
    r = jnp.concatenate([p.reshape(1, -1) for p in pieces], axis=1)
    return jnp.pad(r, ((0, 0), (0, D_MODEL - r.shape[1])))


def _pack_small(mix, convb, ssmg, attng, mlpg, fing, convw, dtb, alog, dsk, sinks, extra=None):
    last = [dtb, alog, dsk, sinks] + ([extra] if extra is not None else [])
    rows = [_row(mix), _row(convb), _row(ssmg, attng), _row(mlpg), _row(fing),
            jnp.pad(convw, ((0, 0), (0, D_MODEL - convw.shape[1]))), _row(*last)]
    packed = jnp.concatenate(rows, axis=0)
    return jnp.pad(packed, ((0, SMALL_ROWS - packed.shape[0]), (0, 0)))


def _unpack_small(p, conv_n):
    return dict(
        mix_norm_g=p[0:1, :], conv_b=p[1:2, :], ssm_norm_g=p[2:3, :D_INNER], attn_out_norm_g=p[2:3, D_INNER:],
        mlp_norm_g=p[3:4, :], final_norm_g=p[4, :], conv_w=p[5:9, :conv_n][None],
        dt_bias=p[9:10, 0:16], A_log=p[9:10, 16:32], D_skip=p[9:10, 32:48], attn_sinks=p[9:10, 48:64])


SMALL_NAMES = ["mix_norm_g", "conv_w", "conv_b", "dt_bias", "A_log", "D_skip", "ssm_norm_g", "attn_sinks",
               "attn_out_norm_g", "mlp_norm_g", "final_norm_g"]
WEIGHT_ORDER = ["mix_norm_g", "w_in", "conv_w", "conv_b", "dt_bias", "A_log", "D_skip", "ssm_norm_g", "attn_sinks",
                "attn_out_norm_g", "w_out", "mlp_norm_g", "w_up", "w_down", "final_norm_g"]


def _to_my_columns(w_nat):
    pad = jnp.zeros((w_nat.shape[0], NP - IN_PROJ), w_nat.dtype)
    return jnp.concatenate([w_nat[:, :NAT_DT], w_nat[:, NAT_DT + N_HEADS:], w_nat[:, NAT_DT:NAT_DT + N_HEADS], pad],
                           axis=1)


def _to_natural_columns(w_my):
    return jnp.concatenate([w_my[:, :NAT_DT], w_my[:, OFF_DT:OFF_DT + N_HEADS], w_my[:, NAT_DT:OFF_DT]], axis=1)


class _FixedWeights:
    def __init__(self, w_in_p, w_out_f, w_up_s, w_down_f, conv_w_f):
        self.w = (w_in_p, w_out_f, w_up_s, w_down_f, conv_w_f)
        self.grads = {}

    def mixer_weights(self, after):
        return self.w[0], self.w[4], None

    def out_weight(self, after):
        return self.w[1]

    def up_weight(self, after):
        return self.w[2]

    def down_weight(self, after):
        return self.w[3]

    def mlp_grads(self, g_up, g_down):
        self.grads.update(w_up=g_up, w_down=g_down)
        return None

    def mlp_grads_sent(self, after):
        return None

    def out_grad(self, g_out):
        self.grads.update(w_out=g_out)
        return None

    def in_grad(self, g_in):
        self.grads.update(w_in=g_in)
        return None


def _local_step(x, tgt, p, hooks):
    T = x.shape[0]
    D = D_MODEL
    h1 = _rmsnorm_fwd(x, p["mix_norm_g"], "norm_mix")
    w_in_p, conv_w_f, token = hooks.mixer_weights(h1)
    (proj,) = _mm_simple(h1, w_in_p, mode="nn", M=T, N=NP, K=D, tm=min(T, 1024), tn=768, tk=D, out_dtype=F32,
                         name="in_proj", after=token)
    xbc = _conv_fwd(proj, conv_w_f, p["conv_b"], "conv_fwd")
    dtT = proj[:, OFF_DT:OFF_DT + N_HEADS].T
    dtbT = p["dt_bias"].T
    alogT = p["A_log"].T
    dfull = jnp.repeat(p["D_skip"], HEAD_DIM, axis=1)
    y_ssm, ypre, hs = _ssd_fwd(xbc, proj, dtT, p["dt_bias"], dtbT, p["A_log"], alogT, dfull, p["ssm_norm_g"],
                               "ssd_fwd")
    kpad = jnp.pad(proj[:, OFF_K:OFF_K + KV_W], ((WINDOW, 0), (0, 0)))
    vpad = jnp.pad(proj[:, OFF_V:OFF_V + KV_W], ((WINDOW, 0), (0, 0)))
    y_att, o_att = _attn_fwd(proj, kpad, vpad, p["attn_sinks"], p["attn_out_norm_g"], "attn_fwd")
    ycat = jnp.concatenate([y_ssm, y_att], axis=1)
    w_out_f = hooks.out_weight(ycat)
    tm = min(T, 1024)
    (x2,) = _mm_simple(ycat, w_out_f, mode="nn", M=T, N=D, K=D, tm=tm, tn=1024, tk=D, out_dtype=F32, name="out_proj",
                       extras=(x,), epilogue=lambda acc, res: (acc + res,))
    h2 = _rmsnorm_fwd(x2, p["mlp_norm_g"], "norm_mlp")
    w_up_s = hooks.up_weight(h2)
    grid = (T // tm, N_DEV, 1)
    u, act = _matmul(
        h2, w_up_s, mode="nn", grid=grid,
        a_spec=pl.BlockSpec((tm, D), lambda i, j, k: (i, 0)),
        b_spec=pl.BlockSpec((None, D, 1024), lambda i, j, k: (j, 0, 0)),
        out_shapes=[jax.ShapeDtypeStruct((T, D_FF), F32), jax.ShapeDtypeStruct((T, D_FF), BF16)],
        out_specs=[pl.BlockSpec((tm, 1024), lambda i, j, k: (i, j))] * 2, tile=(tm, 1024), name="mlp_up",
        epilogue=lambda acc: (acc, jnp.square(jnp.maximum(acc, 0.0))))
    w_down_f = hooks.down_weight(act)
    (x3,) = _mm_simple(act, w_down_f, mode="nn", M=T, N=D, K=D_FF, tm=tm, tn=1024, tk=1024, out_dtype=F32,
                       name="mlp_down", extras=(x2,), epilogue=lambda acc, res: (acc + res,))
    loss_part, d_fin, dx3, dx3b = _final_loss(x3, tgt, p["final_norm_g"].reshape(1, D), "loss_head")
    (g_down,) = _mm_simple(act, dx3b, mode="tn", M=D_FF, N=D, K=T, tm=1024, tn=1024, tk=T, out_dtype=BF16,
                           name="grad_w_down")
    (du,) = _mm_simple(dx3b, w_down_f, mode="nt", M=T, N=D_FF, K=D, tm=tm, tn=1024, tk=D, out_dtype=BF16,
                       name="mlp_down_bwd", extras=(u,),
                       epilogue=lambda acc, uu: (acc * (2.0 * jnp.maximum(uu, 0.0)),))
    (g_up,) = _matmul(
        h2, du, mode="tn", grid=(D // 1024, N_DEV, 1),
        a_spec=pl.BlockSpec((T, 1024), lambda i, j, k: (0, i)),
        b_spec=pl.BlockSpec((T, 1024), lambda i, j, k: (0, j)),
        out_shapes=[jax.ShapeDtypeStruct((N_DEV, D, 1024), BF16)],
        out_specs=[pl.BlockSpec((None, 1024, 1024), lambda i, j, k: (j, i, 0))], tile=(1024, 1024), name="grad_w_up")
    token = hooks.mlp_grads(g_up, g_down)
    (dh2,) = _matmul(
        du, w_up_s, mode="nt", grid=(T // tm, D // 1024, N_DEV),
        a_spec=pl.BlockSpec((tm, 1024), lambda i, j, k: (i, k)),
        b_spec=pl.BlockSpec((None, 1024, 1024), lambda i, j, k: (k, j, 0)),
        out_shapes=[jax.ShapeDtypeStruct((T, D), F32)],
        out_specs=[pl.BlockSpec((tm, 1024), lambda i, j, k: (i, j))], tile=(tm, 1024), name="mlp_up_bwd",
        after=token)
    token = hooks.mlp_grads_sent(dh2)
    dx2, dx2b, d_mlp = _rmsnorm_bwd(dh2, x2, p["mlp_norm_g"], dx3, "norm_mlp_bwd")
    (g_out,) = _mm_simple(ycat, dx2b, mode="tn", M=D, N=D, K=T, tm=1024, tn=1024, tk=T, out_dtype=BF16,
                          name="grad_w_out", after=token)
    token = hooks.out_grad(g_out)
    (dy,) = _mm_simple(dx2b, w_out_f, mode="nt", M=T, N=D, K=D, tm=tm, tn=1024, tk=D, out_dtype=F32,
                       name="out_proj_bwd", after=token)
    dz, dxbc_act, ddt, d_dtb, d_alog, d_dskip, d_ssmg = _ssd_bwd(
        xbc, proj, dtT, p["dt_bias"], dtbT, p["A_log"], alogT, dfull, p["ssm_norm_g"], ypre, hs, dy, "ssd_bwd")
    dq, dkpad, dvpad, d_sinks, d_attng = _attn_bwd(proj, kpad, vpad, p["attn_sinks"], p["attn_out_norm_g"], o_att, dy,
                                                   "attn_bwd")
    dxbc, d_convw, d_convb = _conv_bwd(proj, dxbc_act, conv_w_f, p["conv_b"], "conv_bwd")
    dproj = jnp.concatenate(
        [dz, dxbc, dq, dkpad[WINDOW:].astype(BF16), dvpad[WINDOW:].astype(BF16), ddt.astype(BF16),
         jnp.zeros((T, NP - OFF_DT - 128), BF16)], axis=1)
    (g_in,) = _mm_simple(h1, dproj, mode="tn", M=D, N=NP, K=T, tm=1024, tn=768, tk=T, out_dtype=BF16,
                         name="grad_w_in")
    token = hooks.in_grad(g_in)
    (dh1,) = _mm_simple(dproj, w_in_p, mode="nt", M=T, N=D, K=NP, tm=tm, tn=1024, tk=768, out_dtype=F32,
                        name="in_proj_bwd", after=token)
    dx, _, d_mix = _rmsnorm_bwd(dh1, x, p["mix_norm_g"], dx2, "norm_mix_bwd")
    small = _pack_small(d_mix, d_convb, d_ssmg, d_attng, d_mlp, d_fin, d_convw, d_dtb, d_alog, d_dskip, d_sinks,
                        extra=loss_part[:, 0:1])
    return dx, small


def _landing(own, me):
    zone = lax.empty((N_DEV,) + own.shape, own.dtype)
    return lax.dynamic_update_slice(zone, own[None], (me,) + (0,) * own.ndim)


def _gather_end(started, after, plan, name):
    _, lands = _remote_wait(started, after, plan, name + "_wait")
    return _gather_finish(lands, name + "_finish")


class _ShardedWeights:
    def __init__(self, w_in, w_out, conv_w, w_up, w_down, me, csel):
        self.me, self.csel = me, csel
        shards = [w_in.astype(BF16), conv_w]
        self.st_mixer = _remote_start(shards, [_landing(s, me) for s in shards], _gather_plan, 4 * len(shards),
                                      "gather_start_mixer")
        self.start_token = self.st_mixer[4]
        self.later = {"out": w_out.astype(BF16), "up": w_up.astype(BF16), "down": w_down.astype(BF16)}
        self.later_lands = {k: _landing(v, me) for k, v in self.later.items()}
        self.st_later = {}
        self.reduces = {}
        self.pair_mlp = None

    def mixer_weights(self, after):
        g_in, g_conv = _gather_end(self.st_mixer, after, _gather_plan, "gather_mixer")
        for k in ["out", "up", "down"]:
            self.st_later[k] = _remote_start([self.later[k]], [self.later_lands[k]], _gather_plan, 4,
                                             f"gather_start_{k}", after=g_conv)
        per = IN_PROJ // N_DEV
        k, off = NAT_DT // per, NAT_DT % per
        pieces = [g_in[i] for i in range(k)] + [g_in[k][:, :off], g_in[k][:, off + N_HEADS:]]
        pieces += [g_in[i] for i in range(k + 1, N_DEV)]
        pieces += [g_in[k][:, off:off + N_HEADS], jnp.zeros((D_MODEL, NP - IN_PROJ), BF16)]
        w_in_p = jnp.concatenate(pieces, axis=1)
        conv_w_f = jnp.concatenate([g_conv[i] for i in range(N_DEV)], axis=1)
        started = self.st_later["out"][4] + self.st_later["up"][4] + self.st_later["down"][4]
        return w_in_p, conv_w_f, started

    def out_weight(self, after):
        return _gather_end(self.st_later["out"], after, _gather_plan, "gather_out")[0].reshape(D_MODEL, D_MODEL)

    def up_weight(self, after):
        return _gather_end(self.st_later["up"], after, _gather_plan, "gather_up")[0]

    def down_weight(self, after):
        return _gather_end(self.st_later["down"], after, _gather_plan, "gather_down")[0].reshape(D_FF, D_MODEL)

    def _chips_start(self, slabs, from_sibling, rows, tag):
        sums = [_pair_add(s, r, self.csel, tr, f"pair_add_{tag}_{i}")
                for i, (s, r, tr) in enumerate(zip(slabs, from_sibling, rows))]
        lands = [lax.empty((3,) + s.shape[1:], s.dtype) for s in sums]
        self.reduces[tag] = _remote_start(sums, lands, _chips_plan, 3 * len(sums), f"reduce_start_{tag}")
        return self.reduces[tag][4]

    def mlp_grads(self, g_up, g_down):
        slabs = [g_up, g_down.reshape(N_DEV, D_FF // N_DEV, D_MODEL)]
        lands = [lax.empty((4,) + s.shape[1:], s.dtype) for s in slabs]
        self.pair_mlp = _remote_start(slabs, lands, _pair_plan, 8, "reduce_pair_start_mlp")
        return self.pair_mlp[4]

    def mlp_grads_sent(self, after):
        slabs, from_sibling = _remote_wait(self.pair_mlp, after, _pair_plan, "reduce_pair_wait_mlp")
        return self._chips_start(slabs, from_sibling, [512, 256], "mlp")

    def out_grad(self, g_out):
        slabs = [g_out.reshape(N_DEV, D_MODEL // N_DEV, D_MODEL)]
        return self._chips_start(slabs, _exchange_pair(slabs, "reduce_pair_out"), [256], "out")

    def in_grad(self, g_in):
        per = IN_PROJ // N_DEV
        k, off = NAT_DT // per, NAT_DT % per

        def slab(i):
            if i < k:
                return g_in[:, i * per:(i + 1) * per]
            if i == k:
                return jnp.concatenate([g_in[:, k * per:NAT_DT], g_in[:, OFF_DT:OFF_DT + N_HEADS],
                                        g_in[:, NAT_DT:NAT_DT + per - off - N_HEADS]], axis=1)
            return g_in[:, i * per - N_HEADS:(i + 1) * per - N_HEADS]

        slabs = [jnp.stack([slab(i) for i in range(N_DEV)])]
        return self._chips_start(slabs, _exchange_pair(slabs, "reduce_pair_in"), [256], "in")

    def small_start(self, small):
        self.st_small = _remote_start([small], [_landing(small, self.me)], _everyone_plan, N_DEV - 1, "gather_start_small")

    def small_end(self, after):
        return _remote_wait(self.st_small, after, _everyone_plan, "gather_small_wait")[1][0]

    def reduce_end(self, tag, after):
        return _remote_wait(self.reduces[tag], after, _chips_plan, f"reduce_wait_{tag}")


def kernel(x, mix_norm_g, w_in, conv_w, conv_b, dt_bias, A_log, D_skip, ssm_norm_g, attn_sinks, attn_out_norm_g, w_out, mlp_norm_g, w_up, w_down, final_norm_g, loss_target, m_mix_norm_g, m_w_in, m_conv_w, m_conv_b, m_dt_bias, m_A_log, m_D_skip, m_ssm_norm_g, m_attn_sinks, m_attn_out_norm_g, m_w_out, m_mlp_norm_g, m_w_up, m_w_down, m_final_norm_g, v_mix_norm_g, v_w_in, v_conv_w, v_conv_b, v_dt_bias, v_A_log, v_D_skip, v_ssm_norm_g, v_attn_sinks, v_attn_out_norm_g, v_w_out, v_mlp_norm_g, v_w_up, v_w_down, v_final_norm_g):
    xi, yi, ci = _coords()
    me = 4 * xi + 2 * yi + ci
    csel = jnp.reshape(ci, (1,)).astype(jnp.int32)
    qsel = jnp.reshape(2 * xi + yi, (1,)).astype(jnp.int32)
    w = dict(mix_norm_g=mix_norm_g, conv_b=conv_b, dt_bias=dt_bias, A_log=A_log, D_skip=D_skip,
             ssm_norm_g=ssm_norm_g, attn_sinks=attn_sinks, attn_out_norm_g=attn_out_norm_g, mlp_norm_g=mlp_norm_g,
             final_norm_g=final_norm_g)
    hooks = _ShardedWeights(w_in[0], w_out[0], conv_w[0], w_up[0], w_down[0], me, csel)
    p = dict(w, mix_norm_g=mix_norm_g + hooks.start_token[0:1, 0:1])
    dx, small = _local_step(x[0], loss_target[0], p, hooks)
    hooks.small_start(small)
    big = {}
    after = dx
    for tag, members in [("mlp", [("w_up", w_up, m_w_up, v_w_up, 512), ("w_down", w_down, m_w_down, v_w_down, 256)]),
                         ("out", [("w_out", w_out, m_w_out, v_w_out, 256)]),
                         ("in", [("w_in", w_in, m_w_in, v_w_in, 256)])]:
        chip_sums, from_chips = hooks.reduce_end(tag, after)
        for i, (name, wt, mt, vt, tr) in enumerate(members):
            g, d, mn, vn = _adamw_big(wt[0], mt[0], vt[0], chip_sums[i], from_chips[i], qsel, tr, f"adamw_{name}")
            big[name] = (g[None], d[None], mn[None], vn[None])
            after = g
    gsum = _small_sum(hooks.small_end(after), "small_sum")
    loss = gsum[9, 64]
    gs = _unpack_small(gsum, CONV_DIM)
    cw = CONV_DIM // N_DEV
    g_conv_shard = lax.dynamic_slice(gsum[5:9, :], (0, me * cw), (CONV_K, cw))

    def pack(s):
        return _pack_small(s["mix_norm_g"], s["conv_b"], s["ssm_norm_g"], s["attn_out_norm_g"], s["mlp_norm_g"],
                           s["final_norm_g"], s["conv_w"][0], s["dt_bias"], s["A_log"], s["D_skip"], s["attn_sinks"])

    wp = pack(dict(w, conv_w=conv_w))
    mp = pack(dict(mix_norm_g=m_mix_norm_g, conv_b=m_conv_b, ssm_norm_g=m_ssm_norm_g,
                   attn_out_norm_g=m_attn_out_norm_g, mlp_norm_g=m_mlp_norm_g, final_norm_g=m_final_norm_g,
                   conv_w=m_conv_w, dt_bias=m_dt_bias, A_log=m_A_log, D_skip=m_D_skip, attn_sinks=m_attn_sinks))
    vp = pack(dict(mix_norm_g=v_mix_norm_g, conv_b=v_conv_b, ssm_norm_g=v_ssm_norm_g,
                   attn_out_norm_g=v_attn_out_norm_g, mlp_norm_g=v_mlp_norm_g, final_norm_g=v_final_norm_g,
                   conv_w=v_conv_w, dt_bias=v_dt_bias, A_log=v_A_log, D_skip=v_D_skip, attn_sinks=v_attn_sinks))
    gp = jnp.concatenate([gsum[0:5], jnp.pad(g_conv_shard, ((0, 0), (0, D_MODEL - cw))), gsum[9:10],
                          jnp.zeros((SMALL_ROWS - 10, D_MODEL), F32)], axis=0)
    dp, mnp, vnp = _adamw_small(wp, gp, mp, vp, "adamw_small")
    grads = dict(gs, conv_w=g_conv_shard[None])
    deltas = _unpack_small(dp, cw)
    new_m = _unpack_small(mnp, cw)
    new_v = _unpack_small(vnp, cw)
    for k, name in enumerate(["w_in", "w_out", "w_up", "w_down"]):
        grads[name], deltas[name], new_m[name], new_v[name] = big[name]
    return (loss, dx[None], *[grads[n] for n in WEIGHT_ORDER], *[deltas[n] for n in WEIGHT_ORDER],
            *[new_m[n] for n in WEIGHT_ORDER], *[new_v[n] for n in WEIGHT_ORDER])
```

```python
import functools

import jax
import jax.numpy as jnp
from jax import lax
from jax.experimental import pallas as pl
from jax.experimental.pallas import tpu as pltpu

F32 = jnp.float32
BF16 = jnp.bfloat16
HI = lax.Precision.HIGHEST
MESH = pl.DeviceIdType.MESH

EPS = 1e-5
D_MODEL = 2048
D_INNER = 1024
N_HEADS = 16
HEAD_DIM = 64
N_GROUPS = 4
D_STATE = 128
CHUNK = 128
CONV_K = 4
CONV_DIM = 2048
ATTN_W = 1024
KV_W = 128
WINDOW = 128
D_FF = 8192
IN_PROJ = 4368
N_DEV = 8
NP = 4608
OFF_Z, OFF_X, OFF_B, OFF_C, OFF_Q, OFF_K, OFF_V, OFF_DT = 0, 1024, 2048, 2560, 3072, 4096, 4224, 4352
NAT_DT = 3072

ADAM_LR = 0.001
ADAM_B1 = 0.9
ADAM_B2 = 0.999
ADAM_EPS = 1e-08
ADAM_WD = 0.01
ADAM_STEP = 10

VMEM_LIMIT = 52 * 1024 * 1024
SMALL_ROWS = 16
NEG = -1e30


def _cparams(sem=None):
    return pltpu.CompilerParams(dimension_semantics=sem, vmem_limit_bytes=VMEM_LIMIT)


def _hdot(a, b):
    return jnp.dot(a, b, precision=HI, preferred_element_type=F32)


def _dot_nn(a, b):
    return lax.dot_general(a, b, (((1,), (0,)), ((), ())), preferred_element_type=F32)


def _dot_nt(a, b):
    return lax.dot_general(a, b, (((1,), (1,)), ((), ())), preferred_element_type=F32)


def _dot_tn(a, b):
    return lax.dot_general(a, b, (((0,), (0,)), ((), ())), preferred_element_type=F32)


def _softplus(v):
    return jnp.maximum(v, 0.0) + jnp.log1p(jnp.exp(-jnp.abs(v)))


def _sigmoid(v):
    return 1.0 / (1.0 + jnp.exp(-v))


def _matmul(a, b, *, mode, grid, a_spec, b_spec, out_shapes, out_specs, tile, name,
            extras=(), extra_specs=(), epilogue=None, after=None):
    nk = grid[2]
    n_ex = len(extras)
    n_out = len(out_shapes)
    dot = {"nn": _dot_nn, "nt": _dot_nt, "tn": _dot_tn}[mode]

    def finish(acc, ex_refs, out_refs):
        res = (acc,) if epilogue is None else epilogue(acc, *[e[...] for e in ex_refs])
        for o, r in zip(out_refs, res):
            o[...] = r.astype(o.dtype)

    def body(*refs):
        a_ref, b_ref = refs[0], refs[1]
        ex_refs = refs[2:2 + n_ex]
        out_refs = refs[2 + n_ex:2 + n_ex + n_out]
        part = dot(a_ref[...].astype(BF16), b_ref[...].astype(BF16))
        if nk == 1:
            finish(part, ex_refs, out_refs)
        else:
            acc_ref = refs[-1]
            k = pl.program_id(2)

            @pl.when(k == 0)
            def _():
                acc_ref[...] = part

            @pl.when(k > 0)
            def _():
                acc_ref[...] += part

            @pl.when(k == nk - 1)
            def _():
                finish(acc_ref[...], ex_refs, out_refs)

    scratch = [] if nk == 1 else [pltpu.VMEM(tile, F32)]
    tok_specs = [] if after is None else [pl.BlockSpec((8, 128), lambda i, j, k: (0, 0))]
    tok_args = [] if after is None else [after]
    n_out = len(out_shapes)

    def body_with_token(*refs):
        body(*refs[:2 + n_ex], *refs[2 + n_ex + len(tok_args):])

    outs = pl.pallas_call(
        body_with_token, grid=grid, in_specs=[a_spec, b_spec, *extra_specs, *tok_specs], out_specs=list(out_specs),
        out_shape=list(out_shapes), scratch_shapes=scratch, name=name,
        compiler_params=_cparams(("parallel", "parallel", "arbitrary")),
    )(a, b, *extras, *tok_args)
    return outs


def _mm_simple(a, b, *, mode, M, N, K, tm, tn, tk, out_dtype, name, extras=(), epilogue=None, n_out=1,
               out_dtypes=None, after=None):
    grid = (M // tm, N // tn, K // tk)
    if mode == "nn":
        a_spec = pl.BlockSpec((tm, tk), lambda i, j, k: (i, k))
        b_spec = pl.BlockSpec((tk, tn), lambda i, j, k: (k, j))
    elif mode == "nt":
        a_spec = pl.BlockSpec((tm, tk), lambda i, j, k: (i, k))
        b_spec = pl.BlockSpec((tn, tk), lambda i, j, k: (j, k))
    else:
        a_spec = pl.BlockSpec((tk, tm), lambda i, j, k: (k, i))
        b_spec = pl.BlockSpec((tk, tn), lambda i, j, k: (k, j))
    o_spec = pl.BlockSpec((tm, tn), lambda i, j, k: (i, j))
    dts = out_dtypes if out_dtypes is not None else [out_dtype] * n_out
    return _matmul(a, b, mode=mode, grid=grid, a_spec=a_spec, b_spec=b_spec,
                   out_shapes=[jax.ShapeDtypeStruct((M, N), d) for d in dts],
                   out_specs=[o_spec] * len(dts), tile=(tm, tn), name=name,
                   extras=extras, extra_specs=[o_spec] * len(extras), epilogue=epilogue, after=after)


ROW_BLOCK = 256


def _rmsnorm_fwd(x, g, name):
    T, D = x.shape

    def body(x_ref, g_ref, o_ref):
        xf = x_ref[...]
        r = lax.rsqrt(jnp.mean(xf * xf, axis=-1, keepdims=True) + EPS)
        o_ref[...] = (xf * r * g_ref[...]).astype(BF16)

    return pl.pallas_call(
        body, grid=(T // ROW_BLOCK,),
        in_specs=[pl.BlockSpec((ROW_BLOCK, D), lambda i: (i, 0)), pl.BlockSpec((1, D), lambda i: (0, 0))],
        out_specs=pl.BlockSpec((ROW_BLOCK, D), lambda i: (i, 0)),
        out_shape=jax.ShapeDtypeStruct((T, D), BF16), name=name, compiler_params=_cparams(("parallel",)),
    )(x, g)


def _rmsnorm_bwd(dh, x, g, dres, name):
    T, D = x.shape

    def body(dh_ref, x_ref, g_ref, dres_ref, dx_ref, dxb_ref, dg_ref):
        i = pl.program_id(0)
        xf = x_ref[...]
        r = lax.rsqrt(jnp.mean(xf * xf, axis=-1, keepdims=True) + EPS)
        xh = xf * r
        d = dh_ref[...]

        @pl.when(i == 0)
        def _():
            dg_ref[...] = jnp.zeros_like(dg_ref)

        dg_ref[...] += jnp.sum(d * xh, axis=0, keepdims=True)
        dxh = d * g_ref[...]
        dx = r * (dxh - xh * jnp.mean(dxh * xh, axis=-1, keepdims=True)) + dres_ref[...]
        dx_ref[...] = dx
        dxb_ref[...] = dx.astype(BF16)

    row = pl.BlockSpec((ROW_BLOCK, D), lambda i: (i, 0))
    vec = pl.BlockSpec((1, D), lambda i: (0, 0))
    return pl.pallas_call(
        body, grid=(T // ROW_BLOCK,), in_specs=[row, row, vec, row], out_specs=[row, row, vec],
        out_shape=[jax.ShapeDtypeStruct((T, D), F32), jax.ShapeDtypeStruct((T, D), BF16),
                   jax.ShapeDtypeStruct((1, D), F32)],
        name=name, compiler_params=_cparams(("arbitrary",)),
    )(dh, x, g, dres)


def _final_loss(x3, tgt, g, name):
    T, D = x3.shape

    def body(x_ref, t_ref, g_ref, loss_ref, dg_ref, dx_ref, dxb_ref):
        i = pl.program_id(0)
        xf = x_ref[...]
        r = lax.rsqrt(jnp.mean(xf * xf, axis=-1, keepdims=True) + EPS)
        xh = xf * r
        gg = g_ref[...]
        err = xh * gg - t_ref[...]

        @pl.when(i == 0)
        def _():
            dg_ref[...] = jnp.zeros_like(dg_ref)
            loss_ref[...] = jnp.zeros_like(loss_ref)

        part = jnp.sum(jnp.sum(err * err, axis=-1, keepdims=True), axis=0, keepdims=True) * (0.5 / D)
        loss_ref[...] += jnp.broadcast_to(part, loss_ref.shape)
        dout = err * (1.0 / D)
        dg_ref[...] += jnp.sum(dout * xh, axis=0, keepdims=True)
        dxh = dout * gg
        dx = r * (dxh - xh * jnp.mean(dxh * xh, axis=-1, keepdims=True))
        dx_ref[...] = dx
        dxb_ref[...] = dx.astype(BF16)

    row = pl.BlockSpec((ROW_BLOCK, D), lambda i: (i, 0))
    vec = pl.BlockSpec((1, D), lambda i: (0, 0))
    return pl.pallas_call(
        body, grid=(T // ROW_BLOCK,), in_specs=[row, row, vec],
        out_specs=[pl.BlockSpec((1, 128), lambda i: (0, 0)), vec, row, row],
        out_shape=[jax.ShapeDtypeStruct((1, 128), F32), jax.ShapeDtypeStruct((1, D), F32),
                   jax.ShapeDtypeStruct((T, D), F32), jax.ShapeDtypeStruct((T, D), BF16)],
        name=name, compiler_params=_cparams(("arbitrary",)),
    )(x3, tgt, g)


CONV_BLOCK = 256


def _conv_apply(u, w, b):
    row = lax.broadcasted_iota(jnp.int32, u.shape, 0)
    acc = b + w[CONV_K - 1:CONV_K, :] * u
    shifted = []
    for j in range(1, CONV_K):
        uj = jnp.where(row >= j, pltpu.roll(u, j, axis=0), 0.0)
        shifted.append(uj)
        acc = acc + w[CONV_K - 1 - j:CONV_K - j, :] * uj
    return acc, shifted


def _conv_fwd(proj, conv_w, conv_b, name):
    T = proj.shape[0]
    cb0 = OFF_X // CONV_BLOCK

    def body(u_ref, w_ref, b_ref, o_ref):
        c, _ = _conv_apply(u_ref[...], w_ref[...], b_ref[...])
        o_ref[...] = c * _sigmoid(c)

    return pl.pallas_call(
        body, grid=(CONV_DIM // CONV_BLOCK,),
        in_specs=[pl.BlockSpec((T, CONV_BLOCK), lambda j: (0, cb0 + j)),
                  pl.BlockSpec((CONV_K, CONV_BLOCK), lambda j: (0, j)),
                  pl.BlockSpec((1, CONV_BLOCK), lambda j: (0, j))],
        out_specs=pl.BlockSpec((T, CONV_BLOCK), lambda j: (0, j)),
        out_shape=jax.ShapeDtypeStruct((T, CONV_DIM), F32), name=name, compiler_params=_cparams(("parallel",)),
    )(proj, conv_w, conv_b)


def _conv_bwd(proj, dact, conv_w, conv_b, name):
    T = proj.shape[0]
    cb0 = OFF_X // CONV_BLOCK

    def body(u_ref, d_ref, w_ref, b_ref, du_ref, dw_ref, db_ref):
        u = u_ref[...]
        w = w_ref[...]
        c, shifted = _conv_apply(u, w, b_ref[...])
        sg = _sigmoid(c)
        dc = d_ref[...] * sg * (1.0 + c * (1.0 - sg))
        row = lax.broadcasted_iota(jnp.int32, u.shape, 0)
        du = w[CONV_K - 1:CONV_K, :] * dc
        dw_ref[CONV_K - 1:CONV_K, :] = jnp.sum(dc * u, axis=0, keepdims=True)
        for j in range(1, CONV_K):
            dcj = jnp.where(row < T - j, pltpu.roll(dc, T - j, axis=0), 0.0)
            du = du + w[CONV_K - 1 - j:CONV_K - j, :] * dcj
            dw_ref[CONV_K - 1 - j:CONV_K - j, :] = jnp.sum(dc * shifted[j - 1], axis=0, keepdims=True)
        db_ref[...] = jnp.sum(dc, axis=0, keepdims=True)
        du_ref[...] = du.astype(BF16)

    return pl.pallas_call(
        body, grid=(CONV_DIM // CONV_BLOCK,),
        in_specs=[pl.BlockSpec((T, CONV_BLOCK), lambda j: (0, cb0 + j)),
                  pl.BlockSpec((T, CONV_BLOCK), lambda j: (0, j)),
                  pl.BlockSpec((CONV_K, CONV_BLOCK), lambda j: (0, j)),
                  pl.BlockSpec((1, CONV_BLOCK), lambda j: (0, j))],
        out_specs=[pl.BlockSpec((T, CONV_BLOCK), lambda j: (0, j)),
                   pl.BlockSpec((CONV_K, CONV_BLOCK), lambda j: (0, j)),
                   pl.BlockSpec((1, CONV_BLOCK), lambda j: (0, j))],
        out_shape=[jax.ShapeDtypeStruct((T, CONV_DIM), BF16), jax.ShapeDtypeStruct((CONV_K, CONV_DIM), F32),
                   jax.ShapeDtypeStruct((1, CONV_DIM), F32)],
        name=name, compiler_params=_cparams(("parallel",)),
    )(proj, dact, conv_w, conv_b)


GROUP_W = D_INNER // N_GROUPS
HEADS_PER_GROUP = N_HEADS // N_GROUPS


def _expand_mat():
    h = lax.broadcasted_iota(jnp.int32, (N_HEADS, D_INNER), 0)
    j = lax.broadcasted_iota(jnp.int32, (N_HEADS, D_INNER), 1)
    return (j // HEAD_DIM == h).astype(F32)


def _reduce_mat(g):
    j = lax.broadcasted_iota(jnp.int32, (GROUP_W, N_HEADS), 0)
    h = lax.broadcasted_iota(jnp.int32, (GROUP_W, N_HEADS), 1)
    return (g * HEADS_PER_GROUP + j // HEAD_DIM == h).astype(F32)


def _col16(v, h):
    lane = lax.broadcasted_iota(jnp.int32, v.shape, 1)
    return jnp.sum(jnp.where(lane == h, v, 0.0), axis=1, keepdims=True)


def _ssd_pre(dt_raw, dtT_raw, dtb, dtbT, alog, alogT):
    Q = CHUNK
    xdt = dt_raw + dtb
    dt = _softplus(xdt)
    dtT = _softplus(dtT_raw + dtbT)
    A = -jnp.exp(alog)
    AT = -jnp.exp(alogT)
    row = lax.broadcasted_iota(jnp.int32, (Q, Q), 0)
    col = lax.broadcasted_iota(jnp.int32, (Q, Q), 1)
    tril = (row >= col).astype(F32)
    triu = (row <= col).astype(F32)
    cs = _hdot(tril, dt * A)
    csT = _hdot(dtT * AT, triu)
    return xdt, dt, A, cs, csT, row >= col, triu


def _decay_matrix(cs, csT, h, causal):
    seg = _col16(cs, h) - csT[h:h + 1, :]
    return jnp.where(causal, jnp.exp(jnp.minimum(seg, 0.0)), 0.0)


def _ssd_in_specs(nc, rev):
    def cidx(c):
        return (nc - 1 - c) if rev else c

    return [
        pl.BlockSpec((CHUNK, D_INNER), lambda c: (cidx(c), 0)),
        pl.BlockSpec((CHUNK, 512), lambda c: (cidx(c), 2)),
        pl.BlockSpec((CHUNK, 512), lambda c: (cidx(c), 3)),
        pl.BlockSpec((CHUNK, D_INNER), lambda c: (cidx(c), 0)),
        pl.BlockSpec((CHUNK, 128), lambda c: (cidx(c), OFF_DT // 128)),
        pl.BlockSpec((N_HEADS, CHUNK), lambda c: (0, cidx(c))),
        pl.BlockSpec((1, N_HEADS), lambda c: (0, 0)),
        pl.BlockSpec((N_HEADS, 1), lambda c: (0, 0)),
        pl.BlockSpec((1, N_HEADS), lambda c: (0, 0)),
        pl.BlockSpec((N_HEADS, 1), lambda c: (0, 0)),
        pl.BlockSpec((1, D_INNER), lambda c: (0, 0)),
        pl.BlockSpec((1, D_INNER), lambda c: (0, 0)),
    ]


def _ssd_fwd(xbc, proj, dtT, dtb, dtbT, alog, alogT, dfull, ng, name):
    T = xbc.shape[0]
    nc = T // CHUNK
    Q = CHUNK

    def body(xs_ref, B_ref, C_ref, z_ref, dt_ref, dtT_ref, dtb_ref, dtbT_ref, al_ref, alT_ref, df_ref, ng_ref,
             y_ref, ypre_ref, hs_ref, h_scr):
        c = pl.program_id(0)

        @pl.when(c == 0)
        def _():
            h_scr[...] = jnp.zeros_like(h_scr)

        _, dt, _, cs, csT, causal, _ = _ssd_pre(dt_ref[:, :N_HEADS], dtT_ref[...], dtb_ref[...], dtbT_ref[...],
                                                al_ref[...], alT_ref[...])
        ex = _expand_mat()
        dt_full = _hdot(dt, ex)
        cs_full = _hdot(cs, ex)
        cs_last = cs_full[Q - 1:Q, :]
        xs = xs_ref[...]
        xd = xs * dt_full
        e_full = jnp.exp(cs_full)
        dec_full = jnp.exp(cs_last - cs_full)
        cd_full = jnp.exp(cs_last)
        lane_head = lax.broadcasted_iota(jnp.int32, (1, GROUP_W), 1) // HEAD_DIM
        for g in range(N_GROUPS):
            sl = slice(g * GROUP_W, (g + 1) * GROUP_W)
            Bg = B_ref[:, g * D_STATE:(g + 1) * D_STATE].astype(BF16)
            Cg = C_ref[:, g * D_STATE:(g + 1) * D_STATE].astype(BF16)
            CB = _dot_nt(Cg, Bg)
            hg = h_scr[g]
            yoff = _dot_nn(Cg, hg.astype(BF16)) * e_full[:, sl]
            xd_g = xd[:, sl]
            S = _dot_tn(Bg, (xd_g * dec_full[:, sl]).astype(BF16))
            xd_b = xd_g.astype(BF16)
            ydiag = jnp.zeros((Q, GROUP_W), F32)
            for r in range(HEADS_PER_GROUP):
                Lm = _decay_matrix(cs, csT, g * HEADS_PER_GROUP + r, causal)
                Gm = (CB * Lm).astype(BF16)
                ydiag = ydiag + _dot_nn(Gm, jnp.where(lane_head == r, xd_b, jnp.zeros_like(xd_b)))
            hs_ref[0, g] = hg
            h_scr[g] = hg * cd_full[:, sl] + S
            ypre = ydiag + yoff + xs[:, sl] * df_ref[:, sl]
            ypre_ref[:, sl] = ypre
            zg = z_ref[:, sl]
            yz = ypre * zg * _sigmoid(zg)
            rn = lax.rsqrt(jnp.mean(yz * yz, axis=-1, keepdims=True) + EPS)
            y_ref[:, sl] = (yz * rn * ng_ref[:, sl]).astype(BF16)

    return pl.pallas_call(
        body, grid=(nc,), in_specs=_ssd_in_specs(nc, False),
        out_specs=[pl.BlockSpec((CHUNK, D_INNER), lambda c: (c, 0)),
                   pl.BlockSpec((CHUNK, D_INNER), lambda c: (c, 0)),
                   pl.BlockSpec((1, N_GROUPS, D_STATE, GROUP_W), lambda c: (c, 0, 0, 0))],
        out_shape=[jax.ShapeDtypeStruct((T, D_INNER), BF16), jax.ShapeDtypeStruct((T, D_INNER), F32),
                   jax.ShapeDtypeStruct((nc, N_GROUPS, D_STATE, GROUP_W), F32)],
        scratch_shapes=[pltpu.VMEM((N_GROUPS, D_STATE, GROUP_W), F32)],
        name=name, compiler_params=_cparams(("arbitrary",)),
    )(xbc, xbc, xbc, proj, proj, dtT, dtb, dtbT, alog, alogT, dfull, ng)


def _ssd_bwd(xbc, proj, dtT, dtb, dtbT, alog, alogT, dfull, ng, ypre, hs, dy, name):
    T = xbc.shape[0]
    nc = T // CHUNK
    Q = CHUNK

    def body(xs_ref, B_ref, C_ref, z_ref, dt_ref, dtT_ref, dtb_ref, dtbT_ref, al_ref, alT_ref, df_ref, ng_ref,
             ypre_ref, hs_ref, dy_ref,
             dz_ref, dxbc_ref, ddt_ref, ddtb_ref, dal_ref, dD_ref, dng_ref, dh_scr):
        step = pl.program_id(0)

        @pl.when(step == 0)
        def _():
            dh_scr[...] = jnp.zeros_like(dh_scr)
            ddtb_ref[...] = jnp.zeros_like(ddtb_ref)
            dal_ref[...] = jnp.zeros_like(dal_ref)
            dD_ref[...] = jnp.zeros_like(dD_ref)
            dng_ref[...] = jnp.zeros_like(dng_ref)

        xdt, dt, A, cs, csT, causal, triu = _ssd_pre(dt_ref[:, :N_HEADS], dtT_ref[...], dtb_ref[...],
                                                    dtbT_ref[...], al_ref[...], alT_ref[...])
        ex = _expand_mat()
        dt_full = _hdot(dt, ex)
        cs_full = _hdot(cs, ex)
        cs_last = cs_full[Q - 1:Q, :]
        xs = xs_ref[...]
        xd = xs * dt_full
        e_full = jnp.exp(cs_full)
        dec_full = jnp.exp(cs_last - cs_full)
        cd_full = jnp.exp(cs_last)
        lane_head = lax.broadcasted_iota(jnp.int32, (1, GROUP_W), 1) // HEAD_DIM
        is_last = lax.broadcasted_iota(jnp.int32, (Q, 1), 0) == Q - 1
        dcs16 = jnp.zeros((Q, N_HEADS), F32)
        ddtx16 = jnp.zeros((Q, N_HEADS), F32)
        dD16 = jnp.zeros((8, N_HEADS), F32)
        lane16 = lax.broadcasted_iota(jnp.int32, (1, N_HEADS), 1)
        sub16 = lax.broadcasted_iota(jnp.int32, (N_HEADS, 1), 0)
        col_sums = jnp.zeros((N_HEADS, Q), F32)
        for g in range(N_GROUPS):
            sl = slice(g * GROUP_W, (g + 1) * GROUP_W)
            red = _reduce_mat(g)
            ypre_g = ypre_ref[:, sl]
            zg = z_ref[:, sl]
            sg = _sigmoid(zg)
            silu = zg * sg
            yz = ypre_g * silu
            rn = lax.rsqrt(jnp.mean(yz * yz, axis=-1, keepdims=True) + EPS)
            yh = yz * rn
            dy_g = dy_ref[:, sl]
            dng_ref[:, sl] += jnp.sum(dy_g * yh, axis=0, keepdims=True)
            dyh = dy_g * ng_ref[:, sl]
            dyz = rn * (dyh - yh * jnp.mean(dyh * yh, axis=-1, keepdims=True))
            dY = dyz * silu
            dz_ref[:, sl] = (dyz * ypre_g * sg * (1.0 + zg * (1.0 - sg))).astype(BF16)
            xs_g = xs[:, sl]
            xd_g = xd[:, sl]
            dec_g = dec_full[:, sl]
            cd_g = cd_full[:, sl]
            d_g = df_ref[:, sl]
            Bg = B_ref[:, g * D_STATE:(g + 1) * D_STATE].astype(BF16)
            Cg = C_ref[:, g * D_STATE:(g + 1) * D_STATE].astype(BF16)
            CB = _dot_nt(Cg, Bg)
            hg = hs_ref[0, g]
            hgb = hg.astype(BF16)
            yoff = _dot_nn(Cg, hgb) * e_full[:, sl]
            dhn = dh_scr[g]
            dhnb = dhn.astype(BF16)
            dYE = (dY * e_full[:, sl]).astype(BF16)
            dC = _dot_nt(dYE, hgb)
            dh_direct = _dot_tn(Cg, dYE)
            dXdd = _dot_nn(Bg, dhnb)
            dB = _dot_nt((xd_g * dec_g).astype(BF16), dhnb)
            dcd = jnp.sum(dhn * hg, axis=0, keepdims=True)
            dh_scr[g] = dh_direct + cd_g * dhn
            dYb = dY.astype(BF16)
            xd_b = xd_g.astype(BF16)
            dCB = jnp.zeros((Q, Q), F32)
            dXd = dXdd * dec_g
            for r in range(HEADS_PER_GROUP):
                h = g * HEADS_PER_GROUP + r
                Lm = _decay_matrix(cs, csT, h, causal)
                Gf = CB * Lm
                dYr = jnp.where(lane_head == r, dYb, jnp.zeros_like(dYb))
                dG = _dot_nt(dYr, xd_b)
                dCB = dCB + dG * Lm
                dXd = dXd + _dot_tn(Gf.astype(BF16), dYr)
                Mm = dG * Gf
                dcs16 = dcs16 + jnp.where(lane16 == h, jnp.sum(Mm, axis=1, keepdims=True), 0.0)
                col_sums = col_sums + jnp.where(sub16 == h, jnp.sum(Mm, axis=0, keepdims=True), 0.0)
            dCBb = dCB.astype(BF16)
            dC = dC + _dot_nn(dCBb, Bg)
            dB = dB + _dot_tn(dCBb, Cg)
            w_state = dXdd * dec_g * xd_g
            t_last = jnp.sum(w_state, axis=0, keepdims=True) + dcd * cd_g
            dcs_g = dY * yoff - w_state + jnp.where(is_last, t_last, 0.0)
            dcs16 = dcs16 + _hdot(dcs_g, red)
            ddtx16 = ddtx16 + _hdot(dXd * xs_g, red)
            dD16 = dD16 + _hdot(jnp.broadcast_to(jnp.sum(dY * xs_g, axis=0, keepdims=True), (8, GROUP_W)), red)
            dxbc_ref[:, sl] = dXd * dt_full[:, sl] + dY * d_g
            dxbc_ref[:, D_INNER + g * D_STATE:D_INNER + (g + 1) * D_STATE] = dB
            dxbc_ref[:, D_INNER + 512 + g * D_STATE:D_INNER + 512 + (g + 1) * D_STATE] = dC
        eye = (lax.broadcasted_iota(jnp.int32, (N_HEADS, N_HEADS), 0)
               == lax.broadcasted_iota(jnp.int32, (N_HEADS, N_HEADS), 1)).astype(F32)
        dcs16 = dcs16 - lax.dot_general(col_sums, eye, (((0,), (0,)), ((), ())), precision=HI,
                                        preferred_element_type=F32)
        da = _hdot(triu, dcs16)
        ddt = da * A + ddtx16
        ddt_raw = ddt * _sigmoid(xdt)
        pr = lax.broadcasted_iota(jnp.int32, (N_HEADS, 128), 0)
        pc = lax.broadcasted_iota(jnp.int32, (N_HEADS, 128), 1)
        ddt_ref[...] = _hdot(ddt_raw, (pr == pc).astype(F32))
        ddtb_ref[...] += jnp.sum(ddt_raw, axis=0, keepdims=True)
        dal_ref[...] += jnp.sum(da * dt, axis=0, keepdims=True) * A
        dD_ref[...] += dD16[0:1, :]

    def rc(c):
        return nc - 1 - c

    in_specs = _ssd_in_specs(nc, True) + [
        pl.BlockSpec((CHUNK, D_INNER), lambda c: (rc(c), 0)),
        pl.BlockSpec((1, N_GROUPS, D_STATE, GROUP_W), lambda c: (rc(c), 0, 0, 0)),
        pl.BlockSpec((CHUNK, D_INNER), lambda c: (rc(c), 0)),
    ]
    small = pl.BlockSpec((1, N_HEADS), lambda c: (0, 0))
    return pl.pallas_call(
        body, grid=(nc,), in_specs=in_specs,
        out_specs=[pl.BlockSpec((CHUNK, D_INNER), lambda c: (rc(c), 0)),
                   pl.BlockSpec((CHUNK, CONV_DIM), lambda c: (rc(c), 0)),
                   pl.BlockSpec((CHUNK, 128), lambda c: (rc(c), 0)),
                   small, small, small,
                   pl.BlockSpec((1, D_INNER), lambda c: (0, 0))],
        out_shape=[jax.ShapeDtypeStruct((T, D_INNER), BF16), jax.ShapeDtypeStruct((T, CONV_DIM), F32),
                   jax.ShapeDtypeStruct((T, 128), F32),
                   jax.ShapeDtypeStruct((1, N_HEADS), F32), jax.ShapeDtypeStruct((1, N_HEADS), F32),
                   jax.ShapeDtypeStruct((1, N_HEADS), F32), jax.ShapeDtypeStruct((1, D_INNER), F32)],
        scratch_shapes=[pltpu.VMEM((N_GROUPS, D_STATE, GROUP_W), F32)],
        name=name, compiler_params=_cparams(("arbitrary",)),
    )(xbc, xbc, xbc, proj, proj, dtT, dtb, dtbT, alog, alogT, dfull, ng, ypre, hs, dy)


N_PAIRS = ATTN_W // 128
PAIRS_PER_KV = N_PAIRS // 2
ATTN_SCALE = HEAD_DIM ** -0.5


def _kv_variants(kk):
    lo = lax.broadcasted_iota(jnp.int32, kk.shape, 1) < HEAD_DIM
    zero = jnp.zeros_like(kk)
    k00 = jnp.where(lo, kk, zero)
    k11 = jnp.where(lo, zero, kk)
    k01 = pltpu.roll(k00, HEAD_DIM, axis=1)
    k10 = pltpu.roll(k11, HEAD_DIM, axis=1)
    return [[k00.astype(BF16), k01.astype(BF16)], [k10.astype(BF16), k11.astype(BF16)]]


def _attn_valid(n):
    i = lax.broadcasted_iota(jnp.int32, (WINDOW, 2 * WINDOW), 0)
    j = lax.broadcasted_iota(jnp.int32, (WINDOW, 2 * WINDOW), 1)
    return (j > i) & (j <= i + WINDOW) & (n * WINDOW + j >= WINDOW)


def _attn_probs(qp, kvar, valid, sk):
    s = _dot_nt(qp, kvar) * ATTN_SCALE
    s = jnp.where(valid, s, NEG)
    m = jnp.maximum(jnp.max(s, axis=1, keepdims=True), sk)
    pe = jnp.exp(s - m)
    es = jnp.exp(sk - m)
    den = jnp.sum(pe, axis=1, keepdims=True) + es
    inv = 1.0 / den
    return pe * inv, es * inv


def _sink(sinks, r):
    lane = lax.broadcasted_iota(jnp.int32, sinks.shape, 1)
    return jnp.sum(jnp.where(lane == r, sinks, 0.0), axis=1, keepdims=True)


def _attn_fwd(proj, kpad, vpad, sinks, og, name):
    T = proj.shape[0]
    nb = T // WINDOW

    def body(q_ref, k_ref, v_ref, s_ref, og_ref, y_ref, o_ref):
        n = pl.program_id(0)
        start = pl.multiple_of(n * WINDOW, WINDOW)
        kv = _kv_variants(k_ref[pl.ds(start, 2 * WINDOW), :])
        vv = _kv_variants(v_ref[pl.ds(start, 2 * WINDOW), :])
        valid = _attn_valid(n)
        sinks_v = s_ref[...]
        ssq = jnp.zeros((WINDOW, 1), F32)
        for p in range(N_PAIRS):
            j = p // PAIRS_PER_KV
            qp = q_ref[:, p * 128:(p + 1) * 128].astype(BF16)
            o_pair = jnp.zeros((WINDOW, 128), F32)
            for par in range(2):
                pn, _ = _attn_probs(qp, kv[j][par], valid, _sink(sinks_v, 2 * p + par))
                o_pair = o_pair + _dot_nn(pn.astype(BF16), vv[j][par])
            o_ref[:, p * 128:(p + 1) * 128] = o_pair
            ssq = ssq + jnp.sum(o_pair * o_pair, axis=1, keepdims=True)
        rn = lax.rsqrt(ssq * (1.0 / ATTN_W) + EPS)
        y_ref[...] = (o_ref[...] * rn * og_ref[...]).astype(BF16)

    full_kv = pl.BlockSpec((T + WINDOW, KV_W), lambda n: (0, 0))
    return pl.pallas_call(
        body, grid=(nb,),
        in_specs=[pl.BlockSpec((WINDOW, ATTN_W), lambda n: (n, OFF_Q // ATTN_W)), full_kv, full_kv,
                  pl.BlockSpec((1, N_HEADS), lambda n: (0, 0)), pl.BlockSpec((1, ATTN_W), lambda n: (0, 0))],
        out_specs=[pl.BlockSpec((WINDOW, ATTN_W), lambda n: (n, 0)), pl.BlockSpec((WINDOW, ATTN_W), lambda n: (n, 0))],
        out_shape=[jax.ShapeDtypeStruct((T, ATTN_W), BF16), jax.ShapeDtypeStruct((T, ATTN_W), F32)],
        name=name, compiler_params=_cparams(("parallel",)),
    )(proj, kpad, vpad, sinks, og)


def _attn_bwd(proj, kpad, vpad, sinks, og, o, dy, name):
    T = proj.shape[0]
    nb = T // WINDOW

    def body(q_ref, k_ref, v_ref, s_ref, og_ref, o_ref, dy_ref, dq_ref, dk_ref, dv_ref, ds_ref, dog_ref):
        n = pl.program_id(0)

        @pl.when(n == 0)
        def _():
            dk_ref[...] = jnp.zeros_like(dk_ref)
            dv_ref[...] = jnp.zeros_like(dv_ref)
            ds_ref[...] = jnp.zeros_like(ds_ref)
            dog_ref[...] = jnp.zeros_like(dog_ref)

        start = pl.multiple_of(n * WINDOW, WINDOW)
        kv = _kv_variants(k_ref[pl.ds(start, 2 * WINDOW), :])
        vv = _kv_variants(v_ref[pl.ds(start, 2 * WINDOW), :])
        valid = _attn_valid(n)
        sinks_v = s_ref[...]
        of = o_ref[...]
        rn = lax.rsqrt(jnp.mean(of * of, axis=-1, keepdims=True) + EPS)
        oh = of * rn
        dyf = dy_ref[...]
        dog_ref[...] += jnp.sum(dyf * oh, axis=0, keepdims=True)
        doh = dyf * og_ref[...]
        do = rn * (doh - oh * jnp.mean(doh * oh, axis=-1, keepdims=True))
        lane = lax.broadcasted_iota(jnp.int32, (1, 128), 1)
        lane16 = lax.broadcasted_iota(jnp.int32, (1, N_HEADS), 1)
        dk_acc = [[jnp.zeros((2 * WINDOW, 128), F32) for _ in range(2)] for _ in range(2)]
        dv_acc = [[jnp.zeros((2 * WINDOW, 128), F32) for _ in range(2)] for _ in range(2)]
        dsink = jnp.zeros((1, N_HEADS), F32)
        for p in range(N_PAIRS):
            j = p // PAIRS_PER_KV
            qp = q_ref[:, p * 128:(p + 1) * 128].astype(BF16)
            do_p = do[:, p * 128:(p + 1) * 128]
            o_p = of[:, p * 128:(p + 1) * 128]
            do_b = do_p.astype(BF16)
            prod = do_p * o_p
            dq_pair = jnp.zeros((WINDOW, 128), F32)
            for par in range(2):
                r = 2 * p + par
                half = (lane < HEAD_DIM) if par == 0 else (lane >= HEAD_DIM)
                pn, ps = _attn_probs(qp, kv[j][par], valid, _sink(sinks_v, r))
                delta = jnp.sum(jnp.where(half, prod, 0.0), axis=1, keepdims=True)
                dP = _dot_nt(do_b, vv[j][par])
                dS = pn * (dP - delta)
                dsink = dsink + jnp.where(lane16 == r, -jnp.sum(ps * delta, axis=0, keepdims=True), 0.0)
                dSb = (dS * ATTN_SCALE).astype(BF16)
                dq_pair = dq_pair + _dot_nn(dSb, kv[j][par])
                dk_acc[j][par] = dk_acc[j][par] + _dot_tn(dSb, jnp.where(half, qp, jnp.zeros_like(qp)))
                dv_acc[j][par] = dv_acc[j][par] + _dot_tn(pn.astype(BF16), jnp.where(half, do_b, jnp.zeros_like(do_b)))
            dq_ref[:, p * 128:(p + 1) * 128] = dq_pair.astype(BF16)
        dkk = (dk_acc[0][0] + pltpu.roll(dk_acc[0][1], HEAD_DIM, axis=1)
               + dk_acc[1][1] + pltpu.roll(dk_acc[1][0], HEAD_DIM, axis=1))
        dvv = (dv_acc[0][0] + pltpu.roll(dv_acc[0][1], HEAD_DIM, axis=1)
               + dv_acc[1][1] + pltpu.roll(dv_acc[1][0], HEAD_DIM, axis=1))
        dk_ref[pl.ds(start, 2 * WINDOW), :] += dkk
        dv_ref[pl.ds(start, 2 * WINDOW), :] += dvv
        ds_ref[...] += dsink

    full_kv = pl.BlockSpec((T + WINDOW, KV_W), lambda n: (0, 0))
    blk = pl.BlockSpec((WINDOW, ATTN_W), lambda n: (n, 0))
    return pl.pallas_call(
        body, grid=(nb,),
        in_specs=[pl.BlockSpec((WINDOW, ATTN_W), lambda n: (n, OFF_Q // ATTN_W)), full_kv, full_kv,
                  pl.BlockSpec((1, N_HEADS), lambda n: (0, 0)), pl.BlockSpec((1, ATTN_W), lambda n: (0, 0)),
                  blk, pl.BlockSpec((WINDOW, ATTN_W), lambda n: (n, 1))],
        out_specs=[blk, full_kv, full_kv, pl.BlockSpec((1, N_HEADS), lambda n: (0, 0)),
                   pl.BlockSpec((1, ATTN_W), lambda n: (0, 0))],
        out_shape=[jax.ShapeDtypeStruct((T, ATTN_W), BF16), jax.ShapeDtypeStruct((T + WINDOW, KV_W), F32),
                   jax.ShapeDtypeStruct((T + WINDOW, KV_W), F32), jax.ShapeDtypeStruct((1, N_HEADS), F32),
                   jax.ShapeDtypeStruct((1, ATTN_W), F32)],
        name=name, compiler_params=_cparams(("arbitrary",)),
    )(proj, kpad, vpad, sinks, og, o, dy)


ANY = pl.BlockSpec(memory_space=pl.ANY)


def _coords():
    return lax.axis_index("x"), lax.axis_index("y"), lax.axis_index("c")


def _all_gather(arrs, name):
    n = len(arrs)

    def body(*refs):
        ins, outs = refs[:n], refs[n:2 * n]
        send_sems, recv_sems, local_sems = refs[2 * n:]
        x, y, c = _coords()
        me = 4 * x + 2 * y + c
        sibling = (x, y, 1 - c)
        chips = [(1 - x, y), (x, 1 - y), (1 - x, 1 - y)]

        def copy(a, k, block, to, src=None):
            dst = outs[a].at[block]
            return pltpu.make_async_remote_copy(
                src_ref=dst if src is None else src, dst_ref=dst, send_sem=send_sems.at[a, k],
                recv_sem=recv_sems.at[a, k], device_id=to, device_id_type=MESH)

        mine = [pltpu.make_async_copy(ins[a], outs[a].at[me], local_sems.at[a]) for a in range(n)]
        for cp in mine:
            cp.start()
        first = []
        for a in range(n):
            first.append(copy(a, 0, me, sibling, src=ins[a]))
            for j, chip in enumerate(chips):
                first.append(copy(a, 1 + j, me, (*chip, c), src=ins[a]))
        for cp in first:
            cp.start()
        passed = []
        for j, (px, py) in enumerate(chips):
            blk = 4 * px + 2 * py + c
            for a in range(n):
                copy(a, 1 + j, blk, sibling).wait_recv()
                fwd = copy(a, 4 + j, blk, sibling)
                fwd.start()
                passed.append(fwd)
        for a in range(n):
            copy(a, 0, 4 * x + 2 * y + (1 - c), sibling).wait_recv()
            for j, (px, py) in enumerate(chips):
                copy(a, 4 + j, 4 * px + 2 * py + (1 - c), sibling).wait_recv()
        for cp in first + passed:
            cp.wait_send()
        for cp in mine:
            cp.wait()

    return pl.pallas_call(
        body, in_specs=[ANY] * n, out_specs=[ANY] * n,
        out_shape=[jax.ShapeDtypeStruct((N_DEV,) + a.shape, a.dtype) for a in arrs],
        scratch_shapes=[pltpu.SemaphoreType.DMA((n, 7)), pltpu.SemaphoreType.DMA((n, 7)),
                        pltpu.SemaphoreType.DMA((n,))],
        name=name,
    )(*arrs)


def _exchange_pair(arrs, name):
    n = len(arrs)

    def body(*refs):
        ins, outs = refs[:n], refs[n:2 * n]
        send_sems, recv_sems = refs[2 * n:]
        x, y, c = _coords()
        cps = []
        for a in range(n):
            for q in range(4):
                cps.append(pltpu.make_async_remote_copy(
                    src_ref=ins[a].at[2 * q + (1 - c)], dst_ref=outs[a].at[q], send_sem=send_sems.at[a, q],
                    recv_sem=recv_sems.at[a, q], device_id=(x, y, 1 - c), device_id_type=MESH))
        for cp in cps:
            cp.start()
        for cp in cps:
            cp.wait()

    return pl.pallas_call(
        body, in_specs=[ANY] * n, out_specs=[ANY] * n,
        out_shape=[jax.ShapeDtypeStruct((4,) + a.shape[1:], a.dtype) for a in arrs],
        scratch_shapes=[pltpu.SemaphoreType.DMA((n, 4)), pltpu.SemaphoreType.DMA((n, 4))],
        name=name,
    )(*arrs)


def _exchange_chips(arrs, name):
    n = len(arrs)

    def body(*refs):
        ins, outs = refs[:n], refs[n:2 * n]
        send_sems, recv_sems = refs[2 * n:]
        x, y, c = _coords()
        chips = [(1 - x, y), (x, 1 - y), (1 - x, 1 - y)]
        cps = []
        for a in range(n):
            for k, (tx, ty) in enumerate(chips):
                cps.append(pltpu.make_async_remote_copy(
                    src_ref=ins[a].at[2 * tx + ty], dst_ref=outs[a].at[k], send_sem=send_sems.at[a, k],
                    recv_sem=recv_sems.at[a, k], device_id=(tx, ty, c), device_id_type=MESH))
        for cp in cps:
            cp.start()
        for cp in cps:
            cp.wait()

    return pl.pallas_call(
        body, in_specs=[ANY] * n, out_specs=[ANY] * n,
        out_shape=[jax.ShapeDtypeStruct((3,) + a.shape[1:], a.dtype) for a in arrs],
        scratch_shapes=[pltpu.SemaphoreType.DMA((n, 3)), pltpu.SemaphoreType.DMA((n, 3))],
        name=name,
    )(*arrs)


HBM = pl.BlockSpec(memory_space=pltpu.HBM)
SEM = pl.BlockSpec(memory_space=pltpu.SEMAPHORE)
EFFECT = pltpu.SideEffectType.DATAFLOW_SIDE_EFFECTING


def _in_hbm(a):
    return pltpu.with_memory_space_constraint(a, pltpu.HBM)


def _remote_start(srcs, lands, plan, n_copies, name, after=None):
    n = len(srcs)
    n_after = 0 if after is None else 1

    def body(*refs):
        src_refs, land_refs = refs[:n], refs[n:2 * n]
        send_sems, recv_sems = refs[2 * n + n_after], refs[2 * n + n_after + 1]
        token = refs[-1]
        x, y, c = _coords()
        for i, (sv, dv, dev) in enumerate(plan(src_refs, land_refs, x, y, c)):
            pltpu.make_async_remote_copy(src_ref=sv, dst_ref=dv, send_sem=send_sems.at[i], recv_sem=recv_sems.at[i],
                                         device_id=dev, device_id_type=MESH).start()
        token[...] = jnp.zeros_like(token)

    bufs = list(srcs) + list(lands)
    outs = pl.pallas_call(
        body, name=name,
        out_shape=(pltpu.SemaphoreType.DMA((n_copies,)), pltpu.SemaphoreType.DMA((n_copies,)),
                   *[pltpu.HBM(b.shape, b.dtype) for b in bufs], jax.ShapeDtypeStruct((8, 128), F32)),
        in_specs=[HBM] * (2 * n) + [ANY] * n_after,
        out_specs=(SEM, SEM, *[HBM] * (2 * n), pl.BlockSpec(memory_space=pltpu.VMEM)),
        input_output_aliases={i: 2 + i for i in range(2 * n)},
        compiler_params=pltpu.CompilerParams(has_side_effects=EFFECT),
    )(*[_in_hbm(b) for b in bufs], *([] if after is None else [after]))
    return outs[0], outs[1], list(outs[2:2 + n]), list(outs[2 + n:2 + 2 * n]), outs[-1]


def _remote_wait(started, after, plan, name):
    send_sems, recv_sems, srcs, lands, _ = started
    n = len(srcs)

    def body(*refs):
        src_refs, land_refs = refs[:n], refs[n:2 * n]
        send_sems, recv_sems = refs[2 * n], refs[2 * n + 1]
        x, y, c = _coords()
        for i, (sv, dv, dev) in enumerate(plan(src_refs, land_refs, x, y, c)):
            cp = pltpu.make_async_remote_copy(src_ref=sv, dst_ref=dv, send_sem=send_sems.at[i],
                                              recv_sem=recv_sems.at[i], device_id=dev, device_id_type=MESH)
            cp.wait_send()
            cp.wait_recv()

    bufs = list(srcs) + list(lands)
    outs = pl.pallas_call(
        body, name=name, out_shape=tuple(pltpu.HBM(b.shape, b.dtype) for b in bufs),
        in_specs=[HBM] * (2 * n) + [SEM, SEM, ANY], out_specs=tuple([HBM] * (2 * n)),
        input_output_aliases={i: i for i in range(2 * n)},
        compiler_params=pltpu.CompilerParams(has_side_effects=EFFECT),
    )(*bufs, send_sems, recv_sems, after)
    return list(outs[:n]), list(outs[n:])


def _gather_plan(src_refs, land_refs, x, y, c):
    me = 4 * x + 2 * y + c
    plan = []
    for s, l in zip(src_refs, land_refs):
        for dev in [(x, y, 1 - c), (1 - x, y, c), (x, 1 - y, c), (1 - x, 1 - y, c)]:
            plan.append((s, l.at[me], dev))
    return plan


def _pair_plan(src_refs, land_refs, x, y, c):
    plan = []
    for s, l in zip(src_refs, land_refs):
        for q in range(4):
            plan.append((s.at[2 * q + (1 - c)], l.at[q], (x, y, 1 - c)))
    return plan


def _chips_plan(src_refs, land_refs, x, y, c):
    plan = []
    for s, l in zip(src_refs, land_refs):
        for k, (tx, ty) in enumerate([(1 - x, y), (x, 1 - y), (1 - x, 1 - y)]):
            plan.append((s.at[2 * tx + ty], l.at[k], (tx, ty, c)))
    return plan


def _everyone_plan(src_refs, land_refs, x, y, c):
    me = 4 * x + 2 * y + c
    plan = []
    for s, l in zip(src_refs, land_refs):
        for fx, fy, fc in [(0, 0, 1), (1, 0, 0), (1, 0, 1), (0, 1, 0), (0, 1, 1), (1, 1, 0), (1, 1, 1)]:
            dev = ((1 - x) if fx else x, (1 - y) if fy else y, (1 - c) if fc else c)
            plan.append((s, l.at[me], dev))
    return plan


def _gather_finish(gathered, name):
    n = len(gathered)

    def body(*refs):
        outs = refs[n:2 * n]
        send_sems, recv_sems = refs[2 * n:]
        x, y, c = _coords()
        cps = []
        for a in range(n):
            for j, (px, py) in enumerate([(1 - x, y), (x, 1 - y), (1 - x, 1 - y)]):
                blk = outs[a].at[4 * px + 2 * py + c]
                got = outs[a].at[4 * px + 2 * py + (1 - c)]
                cps.append((pltpu.make_async_remote_copy(
                    src_ref=blk, dst_ref=blk, send_sem=send_sems.at[a, j], recv_sem=recv_sems.at[a, j],
                    device_id=(x, y, 1 - c), device_id_type=MESH), pltpu.make_async_remote_copy(
                    src_ref=got, dst_ref=got, send_sem=send_sems.at[a, j], recv_sem=recv_sems.at[a, j],
                    device_id=(x, y, 1 - c), device_id_type=MESH)))
        for cp, _ in cps:
            cp.start()
        for cp, arrival in cps:
            cp.wait_send()
            arrival.wait_recv()

    return pl.pallas_call(
        body, in_specs=[ANY] * n, out_specs=[ANY] * n,
        out_shape=[jax.ShapeDtypeStruct(g.shape, g.dtype) for g in gathered],
        input_output_aliases={a: a for a in range(n)},
        scratch_shapes=[pltpu.SemaphoreType.DMA((n, 3)), pltpu.SemaphoreType.DMA((n, 3))],
        name=name,
    )(*gathered)


def _pair_add(g8, r1, csel, tr, name):
    _, R, C = r1.shape
    g4 = g8.reshape(4, 2, R, C)

    def body(c_ref, g_ref, r_ref, o_ref):
        o_ref[...] = (g_ref[...].astype(F32) + r_ref[...].astype(F32)).astype(BF16)

    return pl.pallas_call(
        body,
        grid_spec=pltpu.PrefetchScalarGridSpec(
            num_scalar_prefetch=1, grid=(4, R // tr),
            in_specs=[pl.BlockSpec((None, None, tr, C), lambda q, i, cs: (q, cs[0], i, 0)),
                      pl.BlockSpec((None, tr, C), lambda q, i, cs: (q, i, 0))],
            out_specs=pl.BlockSpec((None, tr, C), lambda q, i, cs: (q, i, 0))),
        out_shape=jax.ShapeDtypeStruct((4, R, C), BF16), name=name,
        compiler_params=_cparams(("parallel", "parallel")),
    )(csel, g4, r1)


def _adamw_math(w, g, m, v):
    m = ADAM_B1 * m + (1.0 - ADAM_B1) * g
    v = ADAM_B2 * v + (1.0 - ADAM_B2) * (g * g)
    m_hat = m / (1.0 - ADAM_B1 ** ADAM_STEP)
    v_hat = v / (1.0 - ADAM_B2 ** ADAM_STEP)
    delta = -ADAM_LR * (m_hat / (jnp.sqrt(v_hat) + ADAM_EPS) + ADAM_WD * w)
    return delta, m, v


def _adamw_big(w, m, v, p4, r3, qsel, tr, name):
    R, C = w.shape

    def body(q_ref, w_ref, m_ref, v_ref, p_ref, r_ref, g_out, d_out, m_out, v_out):
        g = p_ref[...].astype(F32) + r_ref[0].astype(F32) + r_ref[1].astype(F32) + r_ref[2].astype(F32)
        d, mn, vn = _adamw_math(w_ref[...], g, m_ref[...], v_ref[...])
        g_out[...] = g
        d_out[...] = d
        m_out[...] = mn
        v_out[...] = vn

    blk = pl.BlockSpec((tr, C), lambda i, qs: (i, 0))
    return pl.pallas_call(
        body,
        grid_spec=pltpu.PrefetchScalarGridSpec(
            num_scalar_prefetch=1, grid=(R // tr,),
            in_specs=[blk, blk, blk, pl.BlockSpec((None, tr, C), lambda i, qs: (qs[0], i, 0)),
                      pl.BlockSpec((3, tr, C), lambda i, qs: (0, i, 0))],
            out_specs=[blk, blk, blk, blk]),
        out_shape=[jax.ShapeDtypeStruct((R, C), F32)] * 4, name=name,
        compiler_params=_cparams(("parallel",)),
    )(qsel, w, m, v, p4, r3)


def _small_sum(parts, name):
    def body(p_ref, o_ref):
        acc = p_ref[0]
        for d in range(1, N_DEV):
            acc = acc + p_ref[d]
        o_ref[...] = acc

    return pl.pallas_call(
        body, out_shape=jax.ShapeDtypeStruct(parts.shape[1:], F32), name=name,
        compiler_params=_cparams(),
    )(parts)


def _adamw_small(w, g, m, v, name):
    def body(w_ref, g_ref, m_ref, v_ref, d_out, m_out, v_out):
        d, mn, vn = _adamw_math(w_ref[...], g_ref[...], m_ref[...], v_ref[...])
        d_out[...] = d
        m_out[...] = mn
        v_out[...] = vn

    return pl.pallas_call(
        body, out_shape=[jax.ShapeDtypeStruct(w.shape, F32)] * 3, name=name, compiler_params=_cparams(),
    )(w, g, m, v)


def _row(*pieces):
    r = jnp.concatenate([p.reshape(1, -1) for p in pieces], axis=1)
    return jnp.pad(r, ((0, 0), (0, D_MODEL - r.shape[1])))


def _pack_small(mix, convb, ssmg, attng, mlpg, fing, convw, dtb, alog, dsk, sinks, extra=None):
    last = [dtb, alog, dsk, sinks] + ([extra] if extra is not None else [])
    rows = [_row(mix), _row(convb), _row(ssmg, attng), _row(mlpg), _row(fing),
            jnp.pad(convw, ((0, 0), (0, D_MODEL - convw.shape[1]))), _row(*last)]
    packed = jnp.concatenate(rows, axis=0)
    return jnp.pad(packed, ((0, SMALL_ROWS - packed.shape[0]), (0, 0)))


def _unpack_small(p, conv_n):
    return dict(
        mix_norm_g=p[0:1, :], conv_b=p[1:2, :], ssm_norm_g=p[2:3, :D_INNER], attn_out_norm_g=p[2:3, D_INNER:],
        mlp_norm_g=p[3:4, :], final_norm_g=p[4, :], conv_w=p[5:9, :conv_n][None],
        dt_bias=p[9:10, 0:16], A_log=p[9:10, 16:32], D_skip=p[9:10, 32:48], attn_sinks=p[9:10, 48:64])


SMALL_NAMES = ["mix_norm_g", "conv_w", "conv_b", "dt_bias", "A_log", "D_skip", "ssm_norm_g", "attn_sinks",
               "attn_out_norm_g", "mlp_norm_g", "final_norm_g"]
WEIGHT_ORDER = ["mix_norm_g", "w_in", "conv_w", "conv_b", "dt_bias", "A_log", "D_skip", "ssm_norm_g", "attn_sinks",
                "attn_out_norm_g", "w_out", "mlp_norm_g", "w_up", "w_down", "final_norm_g"]


def _to_my_columns(w_nat):
    pad = jnp.zeros((w_nat.shape[0], NP - IN_PROJ), w_nat.dtype)
    return jnp.concatenate([w_nat[:, :NAT_DT], w_nat[:, NAT_DT + N_HEADS:], w_nat[:, NAT_DT:NAT_DT + N_HEADS], pad],
                           axis=1)


def _to_natural_columns(w_my):
    return jnp.concatenate([w_my[:, :NAT_DT], w_my[:, OFF_DT:OFF_DT + N_HEADS], w_my[:, NAT_DT:OFF_DT]], axis=1)


class _FixedWeights:
    def __init__(self, w_in_p, w_out_f, w_up_s, w_down_f, conv_w_f):
        self.w = (w_in_p, w_out_f, w_up_s, w_down_f, conv_w_f)
        self.grads = {}

    def mixer_weights(self, after):
        return self.w[0], self.w[4], None

    def out_weight(self, after):
        return self.w[1]

    def up_weight(self, after):
        return self.w[2]

    def down_weight(self, after):
        return self.w[3]

    def mlp_grads(self, g_up, g_down):
        self.grads.update(w_up=g_up, w_down=g_down)
        return None

    def mlp_grads_sent(self, after):
        return None

    def out_grad(self, g_out):
        self.grads.update(w_out=g_out)
        return None

    def in_grad(self, g_in):
        self.grads.update(w_in=g_in)
        return None


def _local_step(x, tgt, p, hooks):
    T = x.shape[0]
    D = D_MODEL
    h1 = _rmsnorm_fwd(x, p["mix_norm_g"], "norm_mix")
    w_in_p, conv_w_f, token = hooks.mixer_weights(h1)
    (proj,) = _mm_simple(h1, w_in_p, mode="nn", M=T, N=NP, K=D, tm=min(T, 1024), tn=768, tk=D, out_dtype=F32,
                         name="in_proj", after=token)
    xbc = _conv_fwd(proj, conv_w_f, p["conv_b"], "conv_fwd")
    dtT = proj[:, OFF_DT:OFF_DT + N_HEADS].T
    dtbT = p["dt_bias"].T
    alogT = p["A_log"].T
    dfull = jnp.repeat(p["D_skip"], HEAD_DIM, axis=1)
    y_ssm, ypre, hs = _ssd_fwd(xbc, proj, dtT, p["dt_bias"], dtbT, p["A_log"], alogT, dfull, p["ssm_norm_g"],
                               "ssd_fwd")
    kpad = jnp.pad(proj[:, OFF_K:OFF_K + KV_W], ((WINDOW, 0), (0, 0)))
    vpad = jnp.pad(proj[:, OFF_V:OFF_V + KV_W], ((WINDOW, 0), (0, 0)))
    y_att, o_att = _attn_fwd(proj, kpad, vpad, p["attn_sinks"], p["attn_out_norm_g"], "attn_fwd")
    ycat = jnp.concatenate([y_ssm, y_att], axis=1)
    w_out_f = hooks.out_weight(ycat)
    tm = min(T, 1024)
    (x2,) = _mm_simple(ycat, w_out_f, mode="nn", M=T, N=D, K=D, tm=tm, tn=1024, tk=D, out_dtype=F32, name="out_proj",
                       extras=(x,), epilogue=lambda acc, res: (acc + res,))
    h2 = _rmsnorm_fwd(x2, p["mlp_norm_g"], "norm_mlp")
    w_up_s = hooks.up_weight(h2)
    grid = (T // tm, N_DEV, 1)
    u, act = _matmul(
        h2, w_up_s, mode="nn", grid=grid,
        a_spec=pl.BlockSpec((tm, D), lambda i, j, k: (i, 0)),
        b_spec=pl.BlockSpec((None, D, 1024), lambda i, j, k: (j, 0, 0)),
        out_shapes=[jax.ShapeDtypeStruct((T, D_FF), F32), jax.ShapeDtypeStruct((T, D_FF), BF16)],
        out_specs=[pl.BlockSpec((tm, 1024), lambda i, j, k: (i, j))] * 2, tile=(tm, 1024), name="mlp_up",
        epilogue=lambda acc: (acc, jnp.square(jnp.maximum(acc, 0.0))))
    w_down_f = hooks.down_weight(act)
    (x3,) = _mm_simple(act, w_down_f, mode="nn", M=T, N=D, K=D_FF, tm=tm, tn=1024, tk=1024, out_dtype=F32,
                       name="mlp_down", extras=(x2,), epilogue=lambda acc, res: (acc + res,))
    loss_part, d_fin, dx3, dx3b = _final_loss(x3, tgt, p["final_norm_g"].reshape(1, D), "loss_head")
    (g_down,) = _mm_simple(act, dx3b, mode="tn", M=D_FF, N=D, K=T, tm=1024, tn=1024, tk=T, out_dtype=BF16,
                           name="grad_w_down")
    (du,) = _mm_simple(dx3b, w_down_f, mode="nt", M=T, N=D_FF, K=D, tm=tm, tn=1024, tk=D, out_dtype=BF16,
                       name="mlp_down_bwd", extras=(u,),
                       epilogue=lambda acc, uu: (acc * (2.0 * jnp.maximum(uu, 0.0)),))
    (g_up,) = _matmul(
        h2, du, mode="tn", grid=(D // 1024, N_DEV, 1),
        a_spec=pl.BlockSpec((T, 1024), lambda i, j, k: (0, i)),
        b_spec=pl.BlockSpec((T, 1024), lambda i, j, k: (0, j)),
        out_shapes=[jax.ShapeDtypeStruct((N_DEV, D, 1024), BF16)],
        out_specs=[pl.BlockSpec((None, 1024, 1024), lambda i, j, k: (j, i, 0))], tile=(1024, 1024), name="grad_w_up")
    token = hooks.mlp_grads(g_up, g_down)
    (dh2,) = _matmul(
        du, w_up_s, mode="nt", grid=(T // tm, D // 1024, N_DEV),
        a_spec=pl.BlockSpec((tm, 1024), lambda i, j, k: (i, k)),
        b_spec=pl.BlockSpec((None, 1024, 1024), lambda i, j, k: (k, j, 0)),
        out_shapes=[jax.ShapeDtypeStruct((T, D), F32)],
        out_specs=[pl.BlockSpec((tm, 1024), lambda i, j, k: (i, j))], tile=(tm, 1024), name="mlp_up_bwd",
        after=token)
    token = hooks.mlp_grads_sent(dh2)
    dx2, dx2b, d_mlp = _rmsnorm_bwd(dh2, x2, p["mlp_norm_g"], dx3, "norm_mlp_bwd")
    (g_out,) = _mm_simple(ycat, dx2b, mode="tn", M=D, N=D, K=T, tm=1024, tn=1024, tk=T, out_dtype=BF16,
                          name="grad_w_out", after=token)
    token = hooks.out_grad(g_out)
    (dy,) = _mm_simple(dx2b, w_out_f, mode="nt", M=T, N=D, K=D, tm=tm, tn=1024, tk=D, out_dtype=F32,
                       name="out_proj_bwd", after=token)
    dz, dxbc_act, ddt, d_dtb, d_alog, d_dskip, d_ssmg = _ssd_bwd(
        xbc, proj, dtT, p["dt_bias"], dtbT, p["A_log"], alogT, dfull, p["ssm_norm_g"], ypre, hs, dy, "ssd_bwd")
    dq, dkpad, dvpad, d_sinks, d_attng = _attn_bwd(proj, kpad, vpad, p["attn_sinks"], p["attn_out_norm_g"], o_att, dy,
                                                   "attn_bwd")
    dxbc, d_convw, d_convb = _conv_bwd(proj, dxbc_act, conv_w_f, p["conv_b"], "conv_bwd")
    dproj = jnp.concatenate(
        [dz, dxbc, dq, dkpad[WINDOW:].astype(BF16), dvpad[WINDOW:].astype(BF16), ddt.astype(BF16),
         jnp.zeros((T, NP - OFF_DT - 128), BF16)], axis=1)
    (g_in,) = _mm_simple(h1, dproj, mode="tn", M=D, N=NP, K=T, tm=1024, tn=768, tk=T, out_dtype=BF16,
                         name="grad_w_in")
    token = hooks.in_grad(g_in)
    (dh1,) = _mm_simple(dproj, w_in_p, mode="nt", M=T, N=D, K=NP, tm=tm, tn=1024, tk=768, out_dtype=F32,
                        name="in_proj_bwd", after=token)
    dx, _, d_mix = _rmsnorm_bwd(dh1, x, p["mix_norm_g"], dx2, "norm_mix_bwd")
    small = _pack_small(d_mix, d_convb, d_ssmg, d_attng, d_mlp, d_fin, d_convw, d_dtb, d_alog, d_dskip, d_sinks,
                        extra=loss_part[:, 0:1])
    return dx, small


def _landing(own, me):
    zone = lax.empty((N_DEV,) + own.shape, own.dtype)
    return lax.dynamic_update_slice(zone, own[None], (me,) + (0,) * own.ndim)


def _gather_end(started, after, plan, name):
    _, lands = _remote_wait(started, after, plan, name + "_wait")
    return _gather_finish(lands, name + "_finish")


class _ShardedWeights:
    def __init__(self, w_in, w_out, conv_w, w_up, w_down, me, csel):
        self.me, self.csel = me, csel
        shards = [w_in.astype(BF16), conv_w]
        self.st_mixer = _remote_start(shards, [_landing(s, me) for s in shards], _gather_plan, 4 * len(shards),
                                      "gather_start_mixer")
        self.start_token = self.st_mixer[4]
        self.later = {"out": w_out.astype(BF16), "up": w_up.astype(BF16), "down": w_down.astype(BF16)}
        self.later_lands = {k: _landing(v, me) for k, v in self.later.items()}
        self.st_later = {}
        self.reduces = {}
        self.pair_mlp = None

    def mixer_weights(self, after):
        g_in, g_conv = _gather_end(self.st_mixer, after, _gather_plan, "gather_mixer")
        order = g_conv
        for k in ["out", "up", "down"]:
            self.st_later[k] = _remote_start([self.later[k]], [self.later_lands[k]], _gather_plan, 4,
                                             f"gather_start_{k}", after=order)
            order = self.st_later[k][4]
        per = IN_PROJ // N_DEV
        k, off = NAT_DT // per, NAT_DT % per
        pieces = [g_in[i] for i in range(k)] + [g_in[k][:, :off], g_in[k][:, off + N_HEADS:]]
        pieces += [g_in[i] for i in range(k + 1, N_DEV)]
        pieces += [g_in[k][:, off:off + N_HEADS], jnp.zeros((D_MODEL, NP - IN_PROJ), BF16)]
        w_in_p = jnp.concatenate(pieces, axis=1)
        conv_w_f = jnp.concatenate([g_conv[i] for i in range(N_DEV)], axis=1)
        return w_in_p, conv_w_f, order

    def out_weight(self, after):
        return _gather_end(self.st_later["out"], after, _gather_plan, "gather_out")[0].reshape(D_MODEL, D_MODEL)

    def up_weight(self, after):
        return _gather_end(self.st_later["up"], after, _gather_plan, "gather_up")[0]

    def down_weight(self, after):
        return _gather_end(self.st_later["down"], after, _gather_plan, "gather_down")[0].reshape(D_FF, D_MODEL)

    def _chips_start(self, slabs, from_sibling, rows, tag):
        sums = [_pair_add(s, r, self.csel, tr, f"pair_add_{tag}_{i}")
                for i, (s, r, tr) in enumerate(zip(slabs, from_sibling, rows))]
        lands = [lax.empty((3,) + s.shape[1:], s.dtype) for s in sums]
        self.reduces[tag] = _remote_start(sums, lands, _chips_plan, 3 * len(sums), f"reduce_start_{tag}")
        return self.reduces[tag][4]

    def mlp_grads(self, g_up, g_down):
        slabs = [g_up, g_down.reshape(N_DEV, D_FF // N_DEV, D_MODEL)]
        lands = [lax.empty((4,) + s.shape[1:], s.dtype) for s in slabs]
        self.pair_mlp = _remote_start(slabs, lands, _pair_plan, 8, "reduce_pair_start_mlp")
        return self.pair_mlp[4]

    def mlp_grads_sent(self, after):
        slabs, from_sibling = _remote_wait(self.pair_mlp, after, _pair_plan, "reduce_pair_wait_mlp")
        return self._chips_start(slabs, from_sibling, [512, 256], "mlp")

    def out_grad(self, g_out):
        slabs = [g_out.reshape(N_DEV, D_MODEL // N_DEV, D_MODEL)]
        return self._chips_start(slabs, _exchange_pair(slabs, "reduce_pair_out"), [256], "out")

    def in_grad(self, g_in):
        per = IN_PROJ // N_DEV
        k, off = NAT_DT // per, NAT_DT % per

        def slab(i):
            if i < k:
                return g_in[:, i * per:(i + 1) * per]
            if i == k:
                return jnp.concatenate([g_in[:, k * per:NAT_DT], g_in[:, OFF_DT:OFF_DT + N_HEADS],
                                        g_in[:, NAT_DT:NAT_DT + per - off - N_HEADS]], axis=1)
            return g_in[:, i * per - N_HEADS:(i + 1) * per - N_HEADS]

        slabs = [jnp.stack([slab(i) for i in range(N_DEV)])]
        return self._chips_start(slabs, _exchange_pair(slabs, "reduce_pair_in"), [256], "in")

    def small_start(self, small):
        self.st_small = _remote_start([small], [_landing(small, self.me)], _everyone_plan, N_DEV - 1, "gather_start_small")

    def small_end(self, after):
        return _remote_wait(self.st_small, after, _everyone_plan, "gather_small_wait")[1][0]

    def reduce_end(self, tag, after):
        return _remote_wait(self.reduces[tag], after, _chips_plan, f"reduce_wait_{tag}")


def kernel(x, mix_norm_g, w_in, conv_w, conv_b, dt_bias, A_log, D_skip, ssm_norm_g, attn_sinks, attn_out_norm_g, w_out, mlp_norm_g, w_up, w_down, final_norm_g, loss_target, m_mix_norm_g, m_w_in, m_conv_w, m_conv_b, m_dt_bias, m_A_log, m_D_skip, m_ssm_norm_g, m_attn_sinks, m_attn_out_norm_g, m_w_out, m_mlp_norm_g, m_w_up, m_w_down, m_final_norm_g, v_mix_norm_g, v_w_in, v_conv_w, v_conv_b, v_dt_bias, v_A_log, v_D_skip, v_ssm_norm_g, v_attn_sinks, v_attn_out_norm_g, v_w_out, v_mlp_norm_g, v_w_up, v_w_down, v_final_norm_g):
    xi, yi, ci = _coords()
    me = 4 * xi + 2 * yi + ci
    csel = jnp.reshape(ci, (1,)).astype(jnp.int32)
    qsel = jnp.reshape(2 * xi + yi, (1,)).astype(jnp.int32)
    w = dict(mix_norm_g=mix_norm_g, conv_b=conv_b, dt_bias=dt_bias, A_log=A_log, D_skip=D_skip,
             ssm_norm_g=ssm_norm_g, attn_sinks=attn_sinks, attn_out_norm_g=attn_out_norm_g, mlp_norm_g=mlp_norm_g,
             final_norm_g=final_norm_g)
    hooks = _ShardedWeights(w_in[0], w_out[0], conv_w[0], w_up[0], w_down[0], me, csel)
    p = dict(w, mix_norm_g=mix_norm_g + hooks.start_token[0:1, 0:1])
    dx, small = _local_step(x[0], loss_target[0], p, hooks)
    hooks.small_start(small)
    big = {}
    after = dx
    for tag, members in [("mlp", [("w_up", w_up, m_w_up, v_w_up, 512), ("w_down", w_down, m_w_down, v_w_down, 256)]),
                         ("out", [("w_out", w_out, m_w_out, v_w_out, 256)]),
                         ("in", [("w_in", w_in, m_w_in, v_w_in, 256)])]:
        chip_sums, from_chips = hooks.reduce_end(tag, after)
        for i, (name, wt, mt, vt, tr) in enumerate(members):
            g, d, mn, vn = _adamw_big(wt[0], mt[0], vt[0], chip_sums[i], from_chips[i], qsel, tr, f"adamw_{name}")
            big[name] = (g[None], d[None], mn[None], vn[None])
            after = g
    gsum = _small_sum(hooks.small_end(after), "small_sum")
    loss = gsum[9, 64]
    gs = _unpack_small(gsum, CONV_DIM)
    cw = CONV_DIM // N_DEV
    g_conv_shard = lax.dynamic_slice(gsum[5:9, :], (0, me * cw), (CONV_K, cw))

    def pack(s):
        return _pack_small(s["mix_norm_g"], s["conv_b"], s["ssm_norm_g"], s["attn_out_norm_g"], s["mlp_norm_g"],
                           s["final_norm_g"], s["conv_w"][0], s["dt_bias"], s["A_log"], s["D_skip"], s["attn_sinks"])

    wp = pack(dict(w, conv_w=conv_w))
    mp = pack(dict(mix_norm_g=m_mix_norm_g, conv_b=m_conv_b, ssm_norm_g=m_ssm_norm_g,
                   attn_out_norm_g=m_attn_out_norm_g, mlp_norm_g=m_mlp_norm_g, final_norm_g=m_final_norm_g,
                   conv_w=m_conv_w, dt_bias=m_dt_bias, A_log=m_A_log, D_skip=m_D_skip, attn_sinks=m_attn_sinks))
    vp = pack(dict(mix_norm_g=v_mix_norm_g, conv_b=v_conv_b, ssm_norm_g=v_ssm_norm_g,
                   attn_out_norm_g=v_attn_out_norm_g, mlp_norm_g=v_mlp_norm_g, final_norm_g=v_final_norm_g,
                   conv_w=v_conv_w, dt_bias=v_dt_bias, A_log=v_A_log, D_skip=v_D_skip, attn_sinks=v_attn_sinks))
    gp = jnp.concatenate([gsum[0:5], jnp.pad(g_conv_shard, ((0, 0), (0, D_MODEL - cw))), gsum[9:10],
                          jnp.zeros((SMALL_ROWS - 10, D_MODEL), F32)], axis=0)
    dp, mnp, vnp = _adamw_small(wp, gp, mp, vp, "adamw_small")
    grads = dict(gs, conv_w=g_conv_shard[None])
    deltas = _unpack_small(dp, cw)
    new_m = _unpack_small(mnp, cw)
    new_v = _unpack_small(vnp, cw)
    for k, name in enumerate(["w_in", "w_out", "w_up", "w_down"]):
        grads[name], deltas[name], new_m[name], new_v[name] = big[name]
    return (loss, dx[None], *[grads[n] for n in WEIGHT_ORDER], *[deltas[n] for n in WEIGHT_ORDER],
            *[new_m[n] for n in WEIGHT_ORDER], *[new_v[n] for n in WEIGHT_ORDER])
```

```python
import functools

import jax
import jax.numpy as jnp
from jax import lax
from jax.experimental import pallas as pl
from jax.experimental.pallas import tpu as pltpu

F32 = jnp.float32
BF16 = jnp.bfloat16
HI = lax.Precision.HIGHEST
MESH = pl.DeviceIdType.MESH

EPS = 1e-5
D_MODEL = 2048
D_INNER = 1024
N_HEADS = 16
HEAD_DIM = 64
N_GROUPS = 4
D_STATE = 128
CHUNK = 128
CONV_K = 4
CONV_DIM = 2048
ATTN_W = 1024
KV_W = 128
WINDOW = 128
D_FF = 8192
IN_PROJ = 4368
N_DEV = 8
NP = 4608
OFF_Z, OFF_X, OFF_B, OFF_C, OFF_Q, OFF_K, OFF_V, OFF_DT = 0, 1024, 2048, 2560, 3072, 4096, 4224, 4352
NAT_DT = 3072

ADAM_LR = 0.001
ADAM_B1 = 0.9
ADAM_B2 = 0.999
ADAM_EPS = 1e-08
ADAM_WD = 0.01
ADAM_STEP = 10

VMEM_LIMIT = 52 * 1024 * 1024
SMALL_ROWS = 16
NEG = -1e30


def _cparams(sem=None):
    return pltpu.CompilerParams(dimension_semantics=sem, vmem_limit_bytes=VMEM_LIMIT)


def _hdot(a, b):
    return jnp.dot(a, b, precision=HI, preferred_element_type=F32)


def _dot_nn(a, b):
    return lax.dot_general(a, b, (((1,), (0,)), ((), ())), preferred_element_type=F32)


def _dot_nt(a, b):
    return lax.dot_general(a, b, (((1,), (1,)), ((), ())), preferred_element_type=F32)


def _dot_tn(a, b):
    return lax.dot_general(a, b, (((0,), (0,)), ((), ())), preferred_element_type=F32)


def _softplus(v):
    return jnp.maximum(v, 0.0) + jnp.log1p(jnp.exp(-jnp.abs(v)))


def _sigmoid(v):
    return 1.0 / (1.0 + jnp.exp(-v))


def _matmul(a, b, *, mode, grid, a_spec, b_spec, out_shapes, out_specs, tile, name,
            extras=(), extra_specs=(), epilogue=None, after=None, dot_fn=None):
    nk = grid[2]
    n_ex = len(extras)
    n_out = len(out_shapes)
    dot = dot_fn if dot_fn is not None else {"nn": _dot_nn, "nt": _dot_nt, "tn": _dot_tn}[mode]

    def finish(acc, ex_refs, out_refs):
        res = (acc,) if epilogue is None else epilogue(acc, *[e[...] for e in ex_refs])
        for o, r in zip(out_refs, res):
            o[...] = r.astype(o.dtype)

    def body(*refs):
        a_ref, b_ref = refs[0], refs[1]
        ex_refs = refs[2:2 + n_ex]
        out_refs = refs[2 + n_ex:2 + n_ex + n_out]
        part = dot(a_ref[...].astype(BF16), b_ref[...].astype(BF16))
        if nk == 1:
            finish(part, ex_refs, out_refs)
        else:
            acc_ref = refs[-1]
            k = pl.program_id(2)

            @pl.when(k == 0)
            def _():
                acc_ref[...] = part

            @pl.when(k > 0)
            def _():
                acc_ref[...] += part

            @pl.when(k == nk - 1)
            def _():
                finish(acc_ref[...], ex_refs, out_refs)

    scratch = [] if nk == 1 else [pltpu.VMEM(tile, F32)]
    tok_specs = [] if after is None else [pl.BlockSpec((8, 128), lambda i, j, k: (0, 0))]
    tok_args = [] if after is None else [after]
    n_out = len(out_shapes)

    def body_with_token(*refs):
        body(*refs[:2 + n_ex], *refs[2 + n_ex + len(tok_args):])

    outs = pl.pallas_call(
        body_with_token, grid=grid, in_specs=[a_spec, b_spec, *extra_specs, *tok_specs], out_specs=list(out_specs),
        out_shape=list(out_shapes), scratch_shapes=scratch, name=name,
        compiler_params=_cparams(("parallel", "parallel", "arbitrary")),
    )(a, b, *extras, *tok_args)
    return outs


def _mm_simple(a, b, *, mode, M, N, K, tm, tn, tk, out_dtype, name, extras=(), epilogue=None, n_out=1,
               out_dtypes=None, after=None):
    grid = (M // tm, N // tn, K // tk)
    if mode == "nn":
        a_spec = pl.BlockSpec((tm, tk), lambda i, j, k: (i, k))
        b_spec = pl.BlockSpec((tk, tn), lambda i, j, k: (k, j))
    elif mode == "nt":
        a_spec = pl.BlockSpec((tm, tk), lambda i, j, k: (i, k))
        b_spec = pl.BlockSpec((tn, tk), lambda i, j, k: (j, k))
    else:
        a_spec = pl.BlockSpec((tk, tm), lambda i, j, k: (k, i))
        b_spec = pl.BlockSpec((tk, tn), lambda i, j, k: (k, j))
    o_spec = pl.BlockSpec((tm, tn), lambda i, j, k: (i, j))
    dts = out_dtypes if out_dtypes is not None else [out_dtype] * n_out
    return _matmul(a, b, mode=mode, grid=grid, a_spec=a_spec, b_spec=b_spec,
                   out_shapes=[jax.ShapeDtypeStruct((M, N), d) for d in dts],
                   out_specs=[o_spec] * len(dts), tile=(tm, tn), name=name,
                   extras=extras, extra_specs=[o_spec] * len(extras), epilogue=epilogue, after=after)


ROW_BLOCK = 256


def _rmsnorm_fwd(x, g, name):
    T, D = x.shape

    def body(x_ref, g_ref, o_ref):
        xf = x_ref[...]
        r = lax.rsqrt(jnp.mean(xf * xf, axis=-1, keepdims=True) + EPS)
        o_ref[...] = (xf * r * g_ref[...]).astype(BF16)

    return pl.pallas_call(
        body, grid=(T // ROW_BLOCK,),
        in_specs=[pl.BlockSpec((ROW_BLOCK, D), lambda i: (i, 0)), pl.BlockSpec((1, D), lambda i: (0, 0))],
        out_specs=pl.BlockSpec((ROW_BLOCK, D), lambda i: (i, 0)),
        out_shape=jax.ShapeDtypeStruct((T, D), BF16), name=name, compiler_params=_cparams(("parallel",)),
    )(x, g)


def _rmsnorm_bwd(dh, x, g, dres, name):
    T, D = x.shape

    def body(dh_ref, x_ref, g_ref, dres_ref, dx_ref, dxb_ref, dg_ref):
        i = pl.program_id(0)
        xf = x_ref[...]
        r = lax.rsqrt(jnp.mean(xf * xf, axis=-1, keepdims=True) + EPS)
        xh = xf * r
        d = dh_ref[...]

        @pl.when(i == 0)
        def _():
            dg_ref[...] = jnp.zeros_like(dg_ref)

        dg_ref[...] += jnp.sum(d * xh, axis=0, keepdims=True)
        dxh = d * g_ref[...]
        dx = r * (dxh - xh * jnp.mean(dxh * xh, axis=-1, keepdims=True)) + dres_ref[...]
        dx_ref[...] = dx
        dxb_ref[...] = dx.astype(BF16)

    row = pl.BlockSpec((ROW_BLOCK, D), lambda i: (i, 0))
    vec = pl.BlockSpec((1, D), lambda i: (0, 0))
    return pl.pallas_call(
        body, grid=(T // ROW_BLOCK,), in_specs=[row, row, vec, row], out_specs=[row, row, vec],
        out_shape=[jax.ShapeDtypeStruct((T, D), F32), jax.ShapeDtypeStruct((T, D), BF16),
                   jax.ShapeDtypeStruct((1, D), F32)],
        name=name, compiler_params=_cparams(("arbitrary",)),
    )(dh, x, g, dres)


def _final_loss(x3, tgt, g, name):
    T, D = x3.shape

    def body(x_ref, t_ref, g_ref, loss_ref, dg_ref, dx_ref, dxb_ref):
        i = pl.program_id(0)
        xf = x_ref[...]
        r = lax.rsqrt(jnp.mean(xf * xf, axis=-1, keepdims=True) + EPS)
        xh = xf * r
        gg = g_ref[...]
        err = xh * gg - t_ref[...]

        @pl.when(i == 0)
        def _():
            dg_ref[...] = jnp.zeros_like(dg_ref)
            loss_ref[...] = jnp.zeros_like(loss_ref)

        part = jnp.sum(jnp.sum(err * err, axis=-1, keepdims=True), axis=0, keepdims=True) * (0.5 / D)
        loss_ref[...] += jnp.broadcast_to(part, loss_ref.shape)
        dout = err * (1.0 / D)
        dg_ref[...] += jnp.sum(dout * xh, axis=0, keepdims=True)
        dxh = dout * gg
        dx = r * (dxh - xh * jnp.mean(dxh * xh, axis=-1, keepdims=True))
        dx_ref[...] = dx
        dxb_ref[...] = dx.astype(BF16)

    row = pl.BlockSpec((ROW_BLOCK, D), lambda i: (i, 0))
    vec = pl.BlockSpec((1, D), lambda i: (0, 0))
    return pl.pallas_call(
        body, grid=(T // ROW_BLOCK,), in_specs=[row, row, vec],
        out_specs=[pl.BlockSpec((1, 128), lambda i: (0, 0)), vec, row, row],
        out_shape=[jax.ShapeDtypeStruct((1, 128), F32), jax.ShapeDtypeStruct((1, D), F32),
                   jax.ShapeDtypeStruct((T, D), F32), jax.ShapeDtypeStruct((T, D), BF16)],
        name=name, compiler_params=_cparams(("arbitrary",)),
    )(x3, tgt, g)


CONV_BLOCK = 256


def _conv_apply(u, w, b):
    row = lax.broadcasted_iota(jnp.int32, u.shape, 0)
    acc = b + w[CONV_K - 1:CONV_K, :] * u
    shifted = []
    for j in range(1, CONV_K):
        uj = jnp.where(row >= j, pltpu.roll(u, j, axis=0), 0.0)
        shifted.append(uj)
        acc = acc + w[CONV_K - 1 - j:CONV_K - j, :] * uj
    return acc, shifted


def _conv_fwd(proj, conv_w, conv_b, name):
    T = proj.shape[0]
    cb0 = OFF_X // CONV_BLOCK

    def body(u_ref, w_ref, b_ref, o_ref):
        c, _ = _conv_apply(u_ref[...], w_ref[...], b_ref[...])
        o_ref[...] = c * _sigmoid(c)

    return pl.pallas_call(
        body, grid=(CONV_DIM // CONV_BLOCK,),
        in_specs=[pl.BlockSpec((T, CONV_BLOCK), lambda j: (0, cb0 + j)),
                  pl.BlockSpec((CONV_K, CONV_BLOCK), lambda j: (0, j)),
                  pl.BlockSpec((1, CONV_BLOCK), lambda j: (0, j))],
        out_specs=pl.BlockSpec((T, CONV_BLOCK), lambda j: (0, j)),
        out_shape=jax.ShapeDtypeStruct((T, CONV_DIM), F32), name=name, compiler_params=_cparams(("parallel",)),
    )(proj, conv_w, conv_b)


def _conv_bwd(proj, dact, conv_w, conv_b, name):
    T = proj.shape[0]
    cb0 = OFF_X // CONV_BLOCK

    def body(u_ref, d_ref, w_ref, b_ref, du_ref, dw_ref, db_ref):
        u = u_ref[...]
        w = w_ref[...]
        c, shifted = _conv_apply(u, w, b_ref[...])
        sg = _sigmoid(c)
        dc = d_ref[...] * sg * (1.0 + c * (1.0 - sg))
        row = lax.broadcasted_iota(jnp.int32, u.shape, 0)
        du = w[CONV_K - 1:CONV_K, :] * dc
        dw_ref[CONV_K - 1:CONV_K, :] = jnp.sum(dc * u, axis=0, keepdims=True)
        for j in range(1, CONV_K):
            dcj = jnp.where(row < T - j, pltpu.roll(dc, T - j, axis=0), 0.0)
            du = du + w[CONV_K - 1 - j:CONV_K - j, :] * dcj
            dw_ref[CONV_K - 1 - j:CONV_K - j, :] = jnp.sum(dc * shifted[j - 1], axis=0, keepdims=True)
        db_ref[...] = jnp.sum(dc, axis=0, keepdims=True)
        du_ref[...] = du.astype(BF16)

    return pl.pallas_call(
        body, grid=(CONV_DIM // CONV_BLOCK,),
        in_specs=[pl.BlockSpec((T, CONV_BLOCK), lambda j: (0, cb0 + j)),
                  pl.BlockSpec((T, CONV_BLOCK), lambda j: (0, j)),
                  pl.BlockSpec((CONV_K, CONV_BLOCK), lambda j: (0, j)),
                  pl.BlockSpec((1, CONV_BLOCK), lambda j: (0, j))],
        out_specs=[pl.BlockSpec((T, CONV_BLOCK), lambda j: (0, j)),
                   pl.BlockSpec((CONV_K, CONV_BLOCK), lambda j: (0, j)),
                   pl.BlockSpec((1, CONV_BLOCK), lambda j: (0, j))],
        out_shape=[jax.ShapeDtypeStruct((T, CONV_DIM), BF16), jax.ShapeDtypeStruct((CONV_K, CONV_DIM), F32),
                   jax.ShapeDtypeStruct((1, CONV_DIM), F32)],
        name=name, compiler_params=_cparams(("parallel",)),
    )(proj, dact, conv_w, conv_b)


GROUP_W = D_INNER // N_GROUPS
HEADS_PER_GROUP = N_HEADS // N_GROUPS


def _expand_mat():
    h = lax.broadcasted_iota(jnp.int32, (N_HEADS, D_INNER), 0)
    j = lax.broadcasted_iota(jnp.int32, (N_HEADS, D_INNER), 1)
    return (j // HEAD_DIM == h).astype(F32)


def _reduce_mat(g):
    j = lax.broadcasted_iota(jnp.int32, (GROUP_W, N_HEADS), 0)
    h = lax.broadcasted_iota(jnp.int32, (GROUP_W, N_HEADS), 1)
    return (g * HEADS_PER_GROUP + j // HEAD_DIM == h).astype(F32)


def _col16(v, h):
    lane = lax.broadcasted_iota(jnp.int32, v.shape, 1)
    return jnp.sum(jnp.where(lane == h, v, 0.0), axis=1, keepdims=True)


def _ssd_pre(dt_raw, dtT_raw, dtb, dtbT, alog, alogT):
    Q = CHUNK
    xdt = dt_raw + dtb
    dt = _softplus(xdt)
    dtT = _softplus(dtT_raw + dtbT)
    A = -jnp.exp(alog)
    AT = -jnp.exp(alogT)
    row = lax.broadcasted_iota(jnp.int32, (Q, Q), 0)
    col = lax.broadcasted_iota(jnp.int32, (Q, Q), 1)
    tril = (row >= col).astype(F32)
    triu = (row <= col).astype(F32)
    cs = _hdot(tril, dt * A)
    csT = _hdot(dtT * AT, triu)
    return xdt, dt, A, cs, csT, row >= col, triu


def _decay_matrix(cs, csT, h, causal):
    seg = _col16(cs, h) - csT[h:h + 1, :]
    return jnp.where(causal, jnp.exp(jnp.minimum(seg, 0.0)), 0.0)


def _ssd_in_specs(nc, rev):
    def cidx(c):
        return (nc - 1 - c) if rev else c

    return [
        pl.BlockSpec((CHUNK, D_INNER), lambda c: (cidx(c), 0)),
        pl.BlockSpec((CHUNK, 512), lambda c: (cidx(c), 2)),
        pl.BlockSpec((CHUNK, 512), lambda c: (cidx(c), 3)),
        pl.BlockSpec((CHUNK, D_INNER), lambda c: (cidx(c), 0)),
        pl.BlockSpec((CHUNK, 128), lambda c: (cidx(c), OFF_DT // 128)),
        pl.BlockSpec((N_HEADS, CHUNK), lambda c: (0, cidx(c))),
        pl.BlockSpec((1, N_HEADS), lambda c: (0, 0)),
        pl.BlockSpec((N_HEADS, 1), lambda c: (0, 0)),
        pl.BlockSpec((1, N_HEADS), lambda c: (0, 0)),
        pl.BlockSpec((N_HEADS, 1), lambda c: (0, 0)),
        pl.BlockSpec((1, D_INNER), lambda c: (0, 0)),
        pl.BlockSpec((1, D_INNER), lambda c: (0, 0)),
    ]


def _ssd_fwd(xbc, proj, dtT, dtb, dtbT, alog, alogT, dfull, ng, name):
    T = xbc.shape[0]
    nc = T // CHUNK
    Q = CHUNK

    def body(xs_ref, B_ref, C_ref, z_ref, dt_ref, dtT_ref, dtb_ref, dtbT_ref, al_ref, alT_ref, df_ref, ng_ref,
             y_ref, ypre_ref, hs_ref, h_scr):
        c = pl.program_id(0)

        @pl.when(c == 0)
        def _():
            h_scr[...] = jnp.zeros_like(h_scr)

        _, dt, _, cs, csT, causal, _ = _ssd_pre(dt_ref[:, :N_HEADS], dtT_ref[...], dtb_ref[...], dtbT_ref[...],
                                                al_ref[...], alT_ref[...])
        ex = _expand_mat()
        dt_full = _hdot(dt, ex)
        cs_full = _hdot(cs, ex)
        cs_last = cs_full[Q - 1:Q, :]
        xs = xs_ref[...]
        xd = xs * dt_full
        e_full = jnp.exp(cs_full)
        dec_full = jnp.exp(cs_last - cs_full)
        cd_full = jnp.exp(cs_last)
        lane_head = lax.broadcasted_iota(jnp.int32, (1, GROUP_W), 1) // HEAD_DIM
        for g in range(N_GROUPS):
            sl = slice(g * GROUP_W, (g + 1) * GROUP_W)
            Bg = B_ref[:, g * D_STATE:(g + 1) * D_STATE].astype(BF16)
            Cg = C_ref[:, g * D_STATE:(g + 1) * D_STATE].astype(BF16)
            CB = _dot_nt(Cg, Bg)
            hg = h_scr[g]
            yoff = _dot_nn(Cg, hg.astype(BF16)) * e_full[:, sl]
            xd_g = xd[:, sl]
            S = _dot_tn(Bg, (xd_g * dec_full[:, sl]).astype(BF16))
            xd_b = xd_g.astype(BF16)
            ydiag = jnp.zeros((Q, GROUP_W), F32)
            for r in range(HEADS_PER_GROUP):
                Lm = _decay_matrix(cs, csT, g * HEADS_PER_GROUP + r, causal)
                Gm = (CB * Lm).astype(BF16)
                ydiag = ydiag + _dot_nn(Gm, jnp.where(lane_head == r, xd_b, jnp.zeros_like(xd_b)))
            hs_ref[0, g] = hg
            h_scr[g] = hg * cd_full[:, sl] + S
            ypre = ydiag + yoff + xs[:, sl] * df_ref[:, sl]
            ypre_ref[:, sl] = ypre
            zg = z_ref[:, sl]
            yz = ypre * zg * _sigmoid(zg)
            rn = lax.rsqrt(jnp.mean(yz * yz, axis=-1, keepdims=True) + EPS)
            y_ref[:, sl] = (yz * rn * ng_ref[:, sl]).astype(BF16)

    return pl.pallas_call(
        body, grid=(nc,), in_specs=_ssd_in_specs(nc, False),
        out_specs=[pl.BlockSpec((CHUNK, D_INNER), lambda c: (c, 0)),
                   pl.BlockSpec((CHUNK, D_INNER), lambda c: (c, 0)),
                   pl.BlockSpec((1, N_GROUPS, D_STATE, GROUP_W), lambda c: (c, 0, 0, 0))],
        out_shape=[jax.ShapeDtypeStruct((T, D_INNER), BF16), jax.ShapeDtypeStruct((T, D_INNER), F32),
                   jax.ShapeDtypeStruct((nc, N_GROUPS, D_STATE, GROUP_W), F32)],
        scratch_shapes=[pltpu.VMEM((N_GROUPS, D_STATE, GROUP_W), F32)],
        name=name, compiler_params=_cparams(("arbitrary",)),
    )(xbc, xbc, xbc, proj, proj, dtT, dtb, dtbT, alog, alogT, dfull, ng)


def _ssd_bwd(xbc, proj, dtT, dtb, dtbT, alog, alogT, dfull, ng, ypre, hs, dy, name):
    T = xbc.shape[0]
    nc = T // CHUNK
    Q = CHUNK

    def body(xs_ref, B_ref, C_ref, z_ref, dt_ref, dtT_ref, dtb_ref, dtbT_ref, al_ref, alT_ref, df_ref, ng_ref,
             ypre_ref, hs_ref, dy_ref,
             dz_ref, dxbc_ref, ddt_ref, ddtb_ref, dal_ref, dD_ref, dng_ref, dh_scr):
        step = pl.program_id(0)

        @pl.when(step == 0)
        def _():
            dh_scr[...] = jnp.zeros_like(dh_scr)
            ddtb_ref[...] = jnp.zeros_like(ddtb_ref)
            dal_ref[...] = jnp.zeros_like(dal_ref)
            dD_ref[...] = jnp.zeros_like(dD_ref)
            dng_ref[...] = jnp.zeros_like(dng_ref)

        xdt, dt, A, cs, csT, causal, triu = _ssd_pre(dt_ref[:, :N_HEADS], dtT_ref[...], dtb_ref[...],
                                                    dtbT_ref[...], al_ref[...], alT_ref[...])
        ex = _expand_mat()
        dt_full = _hdot(dt, ex)
        cs_full = _hdot(cs, ex)
        cs_last = cs_full[Q - 1:Q, :]
        xs = xs_ref[...]
        xd = xs * dt_full
        e_full = jnp.exp(cs_full)
        dec_full = jnp.exp(cs_last - cs_full)
        cd_full = jnp.exp(cs_last)
        lane_head = lax.broadcasted_iota(jnp.int32, (1, GROUP_W), 1) // HEAD_DIM
        is_last = lax.broadcasted_iota(jnp.int32, (Q, 1), 0) == Q - 1
        dcs16 = jnp.zeros((Q, N_HEADS), F32)
        ddtx16 = jnp.zeros((Q, N_HEADS), F32)
        dD16 = jnp.zeros((8, N_HEADS), F32)
        lane16 = lax.broadcasted_iota(jnp.int32, (1, N_HEADS), 1)
        sub16 = lax.broadcasted_iota(jnp.int32, (N_HEADS, 1), 0)
        col_sums = jnp.zeros((N_HEADS, Q), F32)
        for g in range(N_GROUPS):
            sl = slice(g * GROUP_W, (g + 1) * GROUP_W)
            red = _reduce_mat(g)
            ypre_g = ypre_ref[:, sl]
            zg = z_ref[:, sl]
            sg = _sigmoid(zg)
            silu = zg * sg
            yz = ypre_g * silu
            rn = lax.rsqrt(jnp.mean(yz * yz, axis=-1, keepdims=True) + EPS)
            yh = yz * rn
            dy_g = dy_ref[:, sl]
            dng_ref[:, sl] += jnp.sum(dy_g * yh, axis=0, keepdims=True)
            dyh = dy_g * ng_ref[:, sl]
            dyz = rn * (dyh - yh * jnp.mean(dyh * yh, axis=-1, keepdims=True))
            dY = dyz * silu
            dz_ref[:, sl] = (dyz * ypre_g * sg * (1.0 + zg * (1.0 - sg))).astype(BF16)
            xs_g = xs[:, sl]
            xd_g = xd[:, sl]
            dec_g = dec_full[:, sl]
            cd_g = cd_full[:, sl]
            d_g = df_ref[:, sl]
            Bg = B_ref[:, g * D_STATE:(g + 1) * D_STATE].astype(BF16)
            Cg = C_ref[:, g * D_STATE:(g + 1) * D_STATE].astype(BF16)
            CB = _dot_nt(Cg, Bg)
            hg = hs_ref[0, g]
            hgb = hg.astype(BF16)
            yoff = _dot_nn(Cg, hgb) * e_full[:, sl]
            dhn = dh_scr[g]
            dhnb = dhn.astype(BF16)
            dYE = (dY * e_full[:, sl]).astype(BF16)
            dC = _dot_nt(dYE, hgb)
            dh_direct = _dot_tn(Cg, dYE)
            dXdd = _dot_nn(Bg, dhnb)
            dB = _dot_nt((xd_g * dec_g).astype(BF16), dhnb)
            dcd = jnp.sum(dhn * hg, axis=0, keepdims=True)
            dh_scr[g] = dh_direct + cd_g * dhn
            dYb = dY.astype(BF16)
            xd_b = xd_g.astype(BF16)
            dCB = jnp.zeros((Q, Q), F32)
            dXd = dXdd * dec_g
            for r in range(HEADS_PER_GROUP):
                h = g * HEADS_PER_GROUP + r
                Lm = _decay_matrix(cs, csT, h, causal)
                Gf = CB * Lm
                dYr = jnp.where(lane_head == r, dYb, jnp.zeros_like(dYb))
                dG = _dot_nt(dYr, xd_b)
                dCB = dCB + dG * Lm
                dXd = dXd + _dot_tn(Gf.astype(BF16), dYr)
                Mm = dG * Gf
                dcs16 = dcs16 + jnp.where(lane16 == h, jnp.sum(Mm, axis=1, keepdims=True), 0.0)
                col_sums = col_sums + jnp.where(sub16 == h, jnp.sum(Mm, axis=0, keepdims=True), 0.0)
            dCBb = dCB.astype(BF16)
            dC = dC + _dot_nn(dCBb, Bg)
            dB = dB + _dot_tn(dCBb, Cg)
            w_state = dXdd * dec_g * xd_g
            t_last = jnp.sum(w_state, axis=0, keepdims=True) + dcd * cd_g
            dcs_g = dY * yoff - w_state + jnp.where(is_last, t_last, 0.0)
            dcs16 = dcs16 + _hdot(dcs_g, red)
            ddtx16 = ddtx16 + _hdot(dXd * xs_g, red)
            dD16 = dD16 + _hdot(jnp.broadcast_to(jnp.sum(dY * xs_g, axis=0, keepdims=True), (8, GROUP_W)), red)
            dxbc_ref[:, sl] = dXd * dt_full[:, sl] + dY * d_g
            dxbc_ref[:, D_INNER + g * D_STATE:D_INNER + (g + 1) * D_STATE] = dB
            dxbc_ref[:, D_INNER + 512 + g * D_STATE:D_INNER + 512 + (g + 1) * D_STATE] = dC
        eye = (lax.broadcasted_iota(jnp.int32, (N_HEADS, N_HEADS), 0)
               == lax.broadcasted_iota(jnp.int32, (N_HEADS, N_HEADS), 1)).astype(F32)
        dcs16 = dcs16 - lax.dot_general(col_sums, eye, (((0,), (0,)), ((), ())), precision=HI,
                                        preferred_element_type=F32)
        da = _hdot(triu, dcs16)
        ddt = da * A + ddtx16
        ddt_raw = ddt * _sigmoid(xdt)
        pr = lax.broadcasted_iota(jnp.int32, (N_HEADS, 128), 0)
        pc = lax.broadcasted_iota(jnp.int32, (N_HEADS, 128), 1)
        ddt_ref[...] = _hdot(ddt_raw, (pr == pc).astype(F32))
        ddtb_ref[...] += jnp.sum(ddt_raw, axis=0, keepdims=True)
        dal_ref[...] += jnp.sum(da * dt, axis=0, keepdims=True) * A
        dD_ref[...] += dD16[0:1, :]

    def rc(c):
        return nc - 1 - c

    in_specs = _ssd_in_specs(nc, True) + [
        pl.BlockSpec((CHUNK, D_INNER), lambda c: (rc(c), 0)),
        pl.BlockSpec((1, N_GROUPS, D_STATE, GROUP_W), lambda c: (rc(c), 0, 0, 0)),
        pl.BlockSpec((CHUNK, D_INNER), lambda c: (rc(c), 0)),
    ]
    small = pl.BlockSpec((1, N_HEADS), lambda c: (0, 0))
    return pl.pallas_call(
        body, grid=(nc,), in_specs=in_specs,
        out_specs=[pl.BlockSpec((CHUNK, D_INNER), lambda c: (rc(c), 0)),
                   pl.BlockSpec((CHUNK, CONV_DIM), lambda c: (rc(c), 0)),
                   pl.BlockSpec((CHUNK, 128), lambda c: (rc(c), 0)),
                   small, small, small,
                   pl.BlockSpec((1, D_INNER), lambda c: (0, 0))],
        out_shape=[jax.ShapeDtypeStruct((T, D_INNER), BF16), jax.ShapeDtypeStruct((T, CONV_DIM), F32),
                   jax.ShapeDtypeStruct((T, 128), F32),
                   jax.ShapeDtypeStruct((1, N_HEADS), F32), jax.ShapeDtypeStruct((1, N_HEADS), F32),
                   jax.ShapeDtypeStruct((1, N_HEADS), F32), jax.ShapeDtypeStruct((1, D_INNER), F32)],
        scratch_shapes=[pltpu.VMEM((N_GROUPS, D_STATE, GROUP_W), F32)],
        name=name, compiler_params=_cparams(("arbitrary",)),
    )(xbc, xbc, xbc, proj, proj, dtT, dtb, dtbT, alog, alogT, dfull, ng, ypre, hs, dy)


N_PAIRS = ATTN_W // 128
PAIRS_PER_KV = N_PAIRS // 2
ATTN_SCALE = HEAD_DIM ** -0.5


def _kv_variants(kk):
    lo = lax.broadcasted_iota(jnp.int32, kk.shape, 1) < HEAD_DIM
    zero = jnp.zeros_like(kk)
    k00 = jnp.where(lo, kk, zero)
    k11 = jnp.where(lo, zero, kk)
    k01 = pltpu.roll(k00, HEAD_DIM, axis=1)
    k10 = pltpu.roll(k11, HEAD_DIM, axis=1)
    return [[k00.astype(BF16), k01.astype(BF16)], [k10.astype(BF16), k11.astype(BF16)]]


def _attn_valid(n):
    i = lax.broadcasted_iota(jnp.int32, (WINDOW, 2 * WINDOW), 0)
    j = lax.broadcasted_iota(jnp.int32, (WINDOW, 2 * WINDOW), 1)
    return (j > i) & (j <= i + WINDOW) & (n * WINDOW + j >= WINDOW)


def _attn_probs(qp, kvar, valid, sk):
    s = _dot_nt(qp, kvar) * ATTN_SCALE
    s = jnp.where(valid, s, NEG)
    m = jnp.maximum(jnp.max(s, axis=1, keepdims=True), sk)
    pe = jnp.exp(s - m)
    es = jnp.exp(sk - m)
    den = jnp.sum(pe, axis=1, keepdims=True) + es
    inv = 1.0 / den
    return pe * inv, es * inv


def _sink(sinks, r):
    lane = lax.broadcasted_iota(jnp.int32, sinks.shape, 1)
    return jnp.sum(jnp.where(lane == r, sinks, 0.0), axis=1, keepdims=True)


def _attn_fwd(proj, kpad, vpad, sinks, og, name):
    T = proj.shape[0]
    nb = T // WINDOW

    def body(q_ref, k_ref, v_ref, s_ref, og_ref, y_ref, o_ref):
        n = pl.program_id(0)
        start = pl.multiple_of(n * WINDOW, WINDOW)
        kv = _kv_variants(k_ref[pl.ds(start, 2 * WINDOW), :])
        vv = _kv_variants(v_ref[pl.ds(start, 2 * WINDOW), :])
        valid = _attn_valid(n)
        sinks_v = s_ref[...]
        ssq = jnp.zeros((WINDOW, 1), F32)
        for p in range(N_PAIRS):
            j = p // PAIRS_PER_KV
            qp = q_ref[:, p * 128:(p + 1) * 128].astype(BF16)
            o_pair = jnp.zeros((WINDOW, 128), F32)
            for par in range(2):
                pn, _ = _attn_probs(qp, kv[j][par], valid, _sink(sinks_v, 2 * p + par))
                o_pair = o_pair + _dot_nn(pn.astype(BF16), vv[j][par])
            o_ref[:, p * 128:(p + 1) * 128] = o_pair
            ssq = ssq + jnp.sum(o_pair * o_pair, axis=1, keepdims=True)
        rn = lax.rsqrt(ssq * (1.0 / ATTN_W) + EPS)
        y_ref[...] = (o_ref[...] * rn * og_ref[...]).astype(BF16)

    full_kv = pl.BlockSpec((T + WINDOW, KV_W), lambda n: (0, 0))
    return pl.pallas_call(
        body, grid=(nb,),
        in_specs=[pl.BlockSpec((WINDOW, ATTN_W), lambda n: (n, OFF_Q // ATTN_W)), full_kv, full_kv,
                  pl.BlockSpec((1, N_HEADS), lambda n: (0, 0)), pl.BlockSpec((1, ATTN_W), lambda n: (0, 0))],
        out_specs=[pl.BlockSpec((WINDOW, ATTN_W), lambda n: (n, 0)), pl.BlockSpec((WINDOW, ATTN_W), lambda n: (n, 0))],
        out_shape=[jax.ShapeDtypeStruct((T, ATTN_W), BF16), jax.ShapeDtypeStruct((T, ATTN_W), F32)],
        name=name, compiler_params=_cparams(("parallel",)),
    )(proj, kpad, vpad, sinks, og)


def _attn_bwd(proj, kpad, vpad, sinks, og, o, dy, name):
    T = proj.shape[0]
    nb = T // WINDOW

    def body(q_ref, k_ref, v_ref, s_ref, og_ref, o_ref, dy_ref, dq_ref, dk_ref, dv_ref, ds_ref, dog_ref):
        n = pl.program_id(0)

        @pl.when(n == 0)
        def _():
            dk_ref[...] = jnp.zeros_like(dk_ref)
            dv_ref[...] = jnp.zeros_like(dv_ref)
            ds_ref[...] = jnp.zeros_like(ds_ref)
            dog_ref[...] = jnp.zeros_like(dog_ref)

        start = pl.multiple_of(n * WINDOW, WINDOW)
        kv = _kv_variants(k_ref[pl.ds(start, 2 * WINDOW), :])
        vv = _kv_variants(v_ref[pl.ds(start, 2 * WINDOW), :])
        valid = _attn_valid(n)
        sinks_v = s_ref[...]
        of = o_ref[...]
        rn = lax.rsqrt(jnp.mean(of * of, axis=-1, keepdims=True) + EPS)
        oh = of * rn
        dyf = dy_ref[...]
        dog_ref[...] += jnp.sum(dyf * oh, axis=0, keepdims=True)
        doh = dyf * og_ref[...]
        do = rn * (doh - oh * jnp.mean(doh * oh, axis=-1, keepdims=True))
        lane = lax.broadcasted_iota(jnp.int32, (1, 128), 1)
        lane16 = lax.broadcasted_iota(jnp.int32, (1, N_HEADS), 1)
        dk_acc = [[jnp.zeros((2 * WINDOW, 128), F32) for _ in range(2)] for _ in range(2)]
        dv_acc = [[jnp.zeros((2 * WINDOW, 128), F32) for _ in range(2)] for _ in range(2)]
        dsink = jnp.zeros((1, N_HEADS), F32)
        for p in range(N_PAIRS):
            j = p // PAIRS_PER_KV
            qp = q_ref[:, p * 128:(p + 1) * 128].astype(BF16)
            do_p = do[:, p * 128:(p + 1) * 128]
            o_p = of[:, p * 128:(p + 1) * 128]
            do_b = do_p.astype(BF16)
            prod = do_p * o_p
            dq_pair = jnp.zeros((WINDOW, 128), F32)
            for par in range(2):
                r = 2 * p + par
                half = (lane < HEAD_DIM) if par == 0 else (lane >= HEAD_DIM)
                pn, ps = _attn_probs(qp, kv[j][par], valid, _sink(sinks_v, r))
                delta = jnp.sum(jnp.where(half, prod, 0.0), axis=1, keepdims=True)
                dP = _dot_nt(do_b, vv[j][par])
                dS = pn * (dP - delta)
                dsink = dsink + jnp.where(lane16 == r, -jnp.sum(ps * delta, axis=0, keepdims=True), 0.0)
                dSb = (dS * ATTN_SCALE).astype(BF16)
                dq_pair = dq_pair + _dot_nn(dSb, kv[j][par])
                dk_acc[j][par] = dk_acc[j][par] + _dot_tn(dSb, jnp.where(half, qp, jnp.zeros_like(qp)))
                dv_acc[j][par] = dv_acc[j][par] + _dot_tn(pn.astype(BF16), jnp.where(half, do_b, jnp.zeros_like(do_b)))
            dq_ref[:, p * 128:(p + 1) * 128] = dq_pair.astype(BF16)
        dkk = (dk_acc[0][0] + pltpu.roll(dk_acc[0][1], HEAD_DIM, axis=1)
               + dk_acc[1][1] + pltpu.roll(dk_acc[1][0], HEAD_DIM, axis=1))
        dvv = (dv_acc[0][0] + pltpu.roll(dv_acc[0][1], HEAD_DIM, axis=1)
               + dv_acc[1][1] + pltpu.roll(dv_acc[1][0], HEAD_DIM, axis=1))
        dk_ref[pl.ds(start, 2 * WINDOW), :] += dkk
        dv_ref[pl.ds(start, 2 * WINDOW), :] += dvv
        ds_ref[...] += dsink

    full_kv = pl.BlockSpec((T + WINDOW, KV_W), lambda n: (0, 0))
    blk = pl.BlockSpec((WINDOW, ATTN_W), lambda n: (n, 0))
    return pl.pallas_call(
        body, grid=(nb,),
        in_specs=[pl.BlockSpec((WINDOW, ATTN_W), lambda n: (n, OFF_Q // ATTN_W)), full_kv, full_kv,
                  pl.BlockSpec((1, N_HEADS), lambda n: (0, 0)), pl.BlockSpec((1, ATTN_W), lambda n: (0, 0)),
                  blk, pl.BlockSpec((WINDOW, ATTN_W), lambda n: (n, 1))],
        out_specs=[blk, full_kv, full_kv, pl.BlockSpec((1, N_HEADS), lambda n: (0, 0)),
                   pl.BlockSpec((1, ATTN_W), lambda n: (0, 0))],
        out_shape=[jax.ShapeDtypeStruct((T, ATTN_W), BF16), jax.ShapeDtypeStruct((T + WINDOW, KV_W), F32),
                   jax.ShapeDtypeStruct((T + WINDOW, KV_W), F32), jax.ShapeDtypeStruct((1, N_HEADS), F32),
                   jax.ShapeDtypeStruct((1, ATTN_W), F32)],
        name=name, compiler_params=_cparams(("arbitrary",)),
    )(proj, kpad, vpad, sinks, og, o, dy)


ANY = pl.BlockSpec(memory_space=pl.ANY)


def _coords():
    return lax.axis_index("x"), lax.axis_index("y"), lax.axis_index("c")


def _all_gather(arrs, name):
    n = len(arrs)

    def body(*refs):
        ins, outs = refs[:n], refs[n:2 * n]
        send_sems, recv_sems, local_sems = refs[2 * n:]
        x, y, c = _coords()
        me = 4 * x + 2 * y + c
        sibling = (x, y, 1 - c)
        chips = [(1 - x, y), (x, 1 - y), (1 - x, 1 - y)]

        def copy(a, k, block, to, src=None):
            dst = outs[a].at[block]
            return pltpu.make_async_remote_copy(
                src_ref=dst if src is None else src, dst_ref=dst, send_sem=send_sems.at[a, k],
                recv_sem=recv_sems.at[a, k], device_id=to, device_id_type=MESH)

        mine = [pltpu.make_async_copy(ins[a], outs[a].at[me], local_sems.at[a]) for a in range(n)]
        for cp in mine:
            cp.start()
        first = []
        for a in range(n):
            first.append(copy(a, 0, me, sibling, src=ins[a]))
            for j, chip in enumerate(chips):
                first.append(copy(a, 1 + j, me, (*chip, c), src=ins[a]))
        for cp in first:
            cp.start()
        passed = []
        for j, (px, py) in enumerate(chips):
            blk = 4 * px + 2 * py + c
            for a in range(n):
                copy(a, 1 + j, blk, sibling).wait_recv()
                fwd = copy(a, 4 + j, blk, sibling)
                fwd.start()
                passed.append(fwd)
        for a in range(n):
            copy(a, 0, 4 * x + 2 * y + (1 - c), sibling).wait_recv()
            for j, (px, py) in enumerate(chips):
                copy(a, 4 + j, 4 * px + 2 * py + (1 - c), sibling).wait_recv()
        for cp in first + passed:
            cp.wait_send()
        for cp in mine:
            cp.wait()

    return pl.pallas_call(
        body, in_specs=[ANY] * n, out_specs=[ANY] * n,
        out_shape=[jax.ShapeDtypeStruct((N_DEV,) + a.shape, a.dtype) for a in arrs],
        scratch_shapes=[pltpu.SemaphoreType.DMA((n, 7)), pltpu.SemaphoreType.DMA((n, 7)),
                        pltpu.SemaphoreType.DMA((n,))],
        name=name,
    )(*arrs)


def _exchange_pair(arrs, name):
    n = len(arrs)

    def body(*refs):
        ins, outs = refs[:n], refs[n:2 * n]
        send_sems, recv_sems = refs[2 * n:]
        x, y, c = _coords()
        cps = []
        for a in range(n):
            for q in range(4):
                cps.append(pltpu.make_async_remote_copy(
                    src_ref=ins[a].at[2 * q + (1 - c)], dst_ref=outs[a].at[q], send_sem=send_sems.at[a, q],
                    recv_sem=recv_sems.at[a, q], device_id=(x, y, 1 - c), device_id_type=MESH))
        for cp in cps:
            cp.start()
        for cp in cps:
            cp.wait()

    return pl.pallas_call(
        body, in_specs=[ANY] * n, out_specs=[ANY] * n,
        out_shape=[jax.ShapeDtypeStruct((4,) + a.shape[1:], a.dtype) for a in arrs],
        scratch_shapes=[pltpu.SemaphoreType.DMA((n, 4)), pltpu.SemaphoreType.DMA((n, 4))],
        name=name,
    )(*arrs)


def _exchange_chips(arrs, name):
    n = len(arrs)

    def body(*refs):
        ins, outs = refs[:n], refs[n:2 * n]
        send_sems, recv_sems = refs[2 * n:]
        x, y, c = _coords()
        chips = [(1 - x, y), (x, 1 - y), (1 - x, 1 - y)]
        cps = []
        for a in range(n):
            for k, (tx, ty) in enumerate(chips):
                cps.append(pltpu.make_async_remote_copy(
                    src_ref=ins[a].at[2 * tx + ty], dst_ref=outs[a].at[k], send_sem=send_sems.at[a, k],
                    recv_sem=recv_sems.at[a, k], device_id=(tx, ty, c), device_id_type=MESH))
        for cp in cps:
            cp.start()
        for cp in cps:
            cp.wait()

    return pl.pallas_call(
        body, in_specs=[ANY] * n, out_specs=[ANY] * n,
        out_shape=[jax.ShapeDtypeStruct((3,) + a.shape[1:], a.dtype) for a in arrs],
        scratch_shapes=[pltpu.SemaphoreType.DMA((n, 3)), pltpu.SemaphoreType.DMA((n, 3))],
        name=name,
    )(*arrs)


HBM = pl.BlockSpec(memory_space=pltpu.HBM)
SEM = pl.BlockSpec(memory_space=pltpu.SEMAPHORE)
EFFECT = pltpu.SideEffectType.DATAFLOW_SIDE_EFFECTING


def _in_hbm(a):
    return pltpu.with_memory_space_constraint(a, pltpu.HBM)


def _remote_start(srcs, lands, plan, n_copies, name, after=None):
    n = len(srcs)
    n_after = 0 if after is None else 1

    def body(*refs):
        src_refs, land_refs = refs[:n], refs[n:2 * n]
        send_sems, recv_sems = refs[2 * n + n_after], refs[2 * n + n_after + 1]
        token = refs[-1]
        x, y, c = _coords()
        for i, (sv, dv, dev) in enumerate(plan(src_refs, land_refs, x, y, c)):
            pltpu.make_async_remote_copy(src_ref=sv, dst_ref=dv, send_sem=send_sems.at[i], recv_sem=recv_sems.at[i],
                                         device_id=dev, device_id_type=MESH).start()
        token[...] = jnp.zeros_like(token)

    bufs = list(srcs) + list(lands)
    outs = pl.pallas_call(
        body, name=name,
        out_shape=(pltpu.SemaphoreType.DMA((n_copies,)), pltpu.SemaphoreType.DMA((n_copies,)),
                   *[pltpu.HBM(b.shape, b.dtype) for b in bufs], jax.ShapeDtypeStruct((8, 128), F32)),
        in_specs=[HBM] * (2 * n) + [ANY] * n_after,
        out_specs=(SEM, SEM, *[HBM] * (2 * n), pl.BlockSpec(memory_space=pltpu.VMEM)),
        input_output_aliases={i: 2 + i for i in range(2 * n)},
        compiler_params=pltpu.CompilerParams(has_side_effects=EFFECT),
    )(*[_in_hbm(b) for b in bufs], *([] if after is None else [after]))
    return outs[0], outs[1], list(outs[2:2 + n]), list(outs[2 + n:2 + 2 * n]), outs[-1]


def _remote_wait(started, after, plan, name):
    send_sems, recv_sems, srcs, lands, _ = started
    n = len(srcs)

    def body(*refs):
        src_refs, land_refs = refs[:n], refs[n:2 * n]
        send_sems, recv_sems = refs[2 * n], refs[2 * n + 1]
        x, y, c = _coords()
        for i, (sv, dv, dev) in enumerate(plan(src_refs, land_refs, x, y, c)):
            cp = pltpu.make_async_remote_copy(src_ref=sv, dst_ref=dv, send_sem=send_sems.at[i],
                                              recv_sem=recv_sems.at[i], device_id=dev, device_id_type=MESH)
            cp.wait_send()
            cp.wait_recv()

    bufs = list(srcs) + list(lands)
    outs = pl.pallas_call(
        body, name=name, out_shape=tuple(pltpu.HBM(b.shape, b.dtype) for b in bufs),
        in_specs=[HBM] * (2 * n) + [SEM, SEM, ANY], out_specs=tuple([HBM] * (2 * n)),
        input_output_aliases={i: i for i in range(2 * n)},
        compiler_params=pltpu.CompilerParams(has_side_effects=EFFECT),
    )(*bufs, send_sems, recv_sems, after)
    return list(outs[:n]), list(outs[n:])


def _gather_plan(src_refs, land_refs, x, y, c):
    me = 4 * x + 2 * y + c
    plan = []
    for s, l in zip(src_refs, land_refs):
        for dev in [(x, y, 1 - c), (1 - x, y, c), (x, 1 - y, c), (1 - x, 1 - y, c)]:
            plan.append((s, l.at[me], dev))
    return plan


def _pair_plan(src_refs, land_refs, x, y, c):
    plan = []
    for s, l in zip(src_refs, land_refs):
        for q in range(4):
            plan.append((s.at[2 * q + (1 - c)], l.at[q], (x, y, 1 - c)))
    return plan


def _chips_plan(src_refs, land_refs, x, y, c):
    plan = []
    for s, l in zip(src_refs, land_refs):
        for k, (tx, ty) in enumerate([(1 - x, y), (x, 1 - y), (1 - x, 1 - y)]):
            plan.append((s.at[2 * tx + ty], l.at[k], (tx, ty, c)))
    return plan


def _everyone_plan(src_refs, land_refs, x, y, c):
    me = 4 * x + 2 * y + c
    plan = []
    for s, l in zip(src_refs, land_refs):
        for fx, fy, fc in [(0, 0, 1), (1, 0, 0), (1, 0, 1), (0, 1, 0), (0, 1, 1), (1, 1, 0), (1, 1, 1)]:
            dev = ((1 - x) if fx else x, (1 - y) if fy else y, (1 - c) if fc else c)
            plan.append((s, l.at[me], dev))
    return plan


def _gather_finish(gathered, name):
    n = len(gathered)

    def body(*refs):
        outs = refs[n:2 * n]
        send_sems, recv_sems = refs[2 * n:]
        x, y, c = _coords()
        cps = []
        for a in range(n):
            for j, (px, py) in enumerate([(1 - x, y), (x, 1 - y), (1 - x, 1 - y)]):
                blk = outs[a].at[4 * px + 2 * py + c]
                got = outs[a].at[4 * px + 2 * py + (1 - c)]
                cps.append((pltpu.make_async_remote_copy(
                    src_ref=blk, dst_ref=blk, send_sem=send_sems.at[a, j], recv_sem=recv_sems.at[a, j],
                    device_id=(x, y, 1 - c), device_id_type=MESH), pltpu.make_async_remote_copy(
                    src_ref=got, dst_ref=got, send_sem=send_sems.at[a, j], recv_sem=recv_sems.at[a, j],
                    device_id=(x, y, 1 - c), device_id_type=MESH)))
        for cp, _ in cps:
            cp.start()
        for cp, arrival in cps:
            cp.wait_send()
            arrival.wait_recv()

    return pl.pallas_call(
        body, in_specs=[ANY] * n, out_specs=[ANY] * n,
        out_shape=[jax.ShapeDtypeStruct(g.shape, g.dtype) for g in gathered],
        input_output_aliases={a: a for a in range(n)},
        scratch_shapes=[pltpu.SemaphoreType.DMA((n, 3)), pltpu.SemaphoreType.DMA((n, 3))],
        name=name,
    )(*gathered)


def _pair_add(g8, r1, csel, tr, name):
    _, R, C = r1.shape
    g4 = g8.reshape(4, 2, R, C)

    def body(c_ref, g_ref, r_ref, o_ref):
        o_ref[...] = (g_ref[...].astype(F32) + r_ref[...].astype(F32)).astype(BF16)

    return pl.pallas_call(
        body,
        grid_spec=pltpu.PrefetchScalarGridSpec(
            num_scalar_prefetch=1, grid=(4, R // tr),
            in_specs=[pl.BlockSpec((None, None, tr, C), lambda q, i, cs: (q, cs[0], i, 0)),
                      pl.BlockSpec((None, tr, C), lambda q, i, cs: (q, i, 0))],
            out_specs=pl.BlockSpec((None, tr, C), lambda q, i, cs: (q, i, 0))),
        out_shape=jax.ShapeDtypeStruct((4, R, C), BF16), name=name,
        compiler_params=_cparams(("parallel", "parallel")),
    )(csel, g4, r1)


def _adamw_math(w, g, m, v):
    m = ADAM_B1 * m + (1.0 - ADAM_B1) * g
    v = ADAM_B2 * v + (1.0 - ADAM_B2) * (g * g)
    m_hat = m / (1.0 - ADAM_B1 ** ADAM_STEP)
    v_hat = v / (1.0 - ADAM_B2 ** ADAM_STEP)
    delta = -ADAM_LR * (m_hat / (jnp.sqrt(v_hat) + ADAM_EPS) + ADAM_WD * w)
    return delta, m, v


def _adamw_big(w, m, v, p4, r3, qsel, tr, name):
    R, C = w.shape

    def body(q_ref, w_ref, m_ref, v_ref, p_ref, r_ref, g_out, d_out, m_out, v_out):
        g = p_ref[...].astype(F32) + r_ref[0].astype(F32) + r_ref[1].astype(F32) + r_ref[2].astype(F32)
        d, mn, vn = _adamw_math(w_ref[...], g, m_ref[...], v_ref[...])
        g_out[...] = g
        d_out[...] = d
        m_out[...] = mn
        v_out[...] = vn

    blk = pl.BlockSpec((tr, C), lambda i, qs: (i, 0))
    return pl.pallas_call(
        body,
        grid_spec=pltpu.PrefetchScalarGridSpec(
            num_scalar_prefetch=1, grid=(R // tr,),
            in_specs=[blk, blk, blk, pl.BlockSpec((None, tr, C), lambda i, qs: (qs[0], i, 0)),
                      pl.BlockSpec((3, tr, C), lambda i, qs: (0, i, 0))],
            out_specs=[blk, blk, blk, blk]),
        out_shape=[jax.ShapeDtypeStruct((R, C), F32)] * 4, name=name,
        compiler_params=_cparams(("parallel",)),
    )(qsel, w, m, v, p4, r3)


def _small_sum(parts, name):
    def body(p_ref, o_ref):
        acc = p_ref[0]
        for d in range(1, N_DEV):
            acc = acc + p_ref[d]
        o_ref[...] = acc

    return pl.pallas_call(
        body, out_shape=jax.ShapeDtypeStruct(parts.shape[1:], F32), name=name,
        compiler_params=_cparams(),
    )(parts)


def _adamw_small(w, g, m, v, name):
    def body(w_ref, g_ref, m_ref, v_ref, d_out, m_out, v_out):
        d, mn, vn = _adamw_math(w_ref[...], g_ref[...], m_ref[...], v_ref[...])
        d_out[...] = d
        m_out[...] = mn
        v_out[...] = vn

    return pl.pallas_call(
        body, out_shape=[jax.ShapeDtypeStruct(w.shape, F32)] * 3, name=name, compiler_params=_cparams(),
    )(w, g, m, v)


def _row(*pieces):
    r = jnp.concatenate([p.reshape(1, -1) for p in pieces], axis=1)
    return jnp.pad(r, ((0, 0), (0, D_MODEL - r.shape[1])))


def _pack_small(mix, convb, ssmg, attng, mlpg, fing, convw, dtb, alog, dsk, sinks, extra=None):
    last = [dtb, alog, dsk, sinks] + ([extra] if extra is not None else [])
    rows = [_row(mix), _row(convb), _row(ssmg, attng), _row(mlpg), _row(fing),
            jnp.pad(convw, ((0, 0), (0, D_MODEL - convw.shape[1]))), _row(*last)]
    packed = jnp.concatenate(rows, axis=0)
    return jnp.pad(packed, ((0, SMALL_ROWS - packed.shape[0]), (0, 0)))


def _unpack_small(p, conv_n):
    return dict(
        mix_norm_g=p[0:1, :], conv_b=p[1:2, :], ssm_norm_g=p[2:3, :D_INNER], attn_out_norm_g=p[2:3, D_INNER:],
        mlp_norm_g=p[3:4, :], final_norm_g=p[4, :], conv_w=p[5:9, :conv_n][None],
        dt_bias=p[9:10, 0:16], A_log=p[9:10, 16:32], D_skip=p[9:10, 32:48], attn_sinks=p[9:10, 48:64])


SMALL_NAMES = ["mix_norm_g", "conv_w", "conv_b", "dt_bias", "A_log", "D_skip", "ssm_norm_g", "attn_sinks",
               "attn_out_norm_g", "mlp_norm_g", "final_norm_g"]
WEIGHT_ORDER = ["mix_norm_g", "w_in", "conv_w", "conv_b", "dt_bias", "A_log", "D_skip", "ssm_norm_g", "attn_sinks",
                "attn_out_norm_g", "w_out", "mlp_norm_g", "w_up", "w_down", "final_norm_g"]


def _to_my_columns(w_nat):
    pad = jnp.zeros((w_nat.shape[0], NP - IN_PROJ), w_nat.dtype)
    return jnp.concatenate([w_nat[:, :NAT_DT], w_nat[:, NAT_DT + N_HEADS:], w_nat[:, NAT_DT:NAT_DT + N_HEADS], pad],
                           axis=1)


def _to_natural_columns(w_my):
    return jnp.concatenate([w_my[:, :NAT_DT], w_my[:, OFF_DT:OFF_DT + N_HEADS], w_my[:, NAT_DT:OFF_DT]], axis=1)


class _FixedWeights:
    def __init__(self, w_in_p, w_out_f, w_up_s, w_down_f, conv_w_f):
        self.w = (w_in_p, w_out_f, w_up_s, w_down_f, conv_w_f)
        self.grads = {}

    def mixer_weights(self, after):
        return self.w[0], self.w[4], None

    def out_weight(self, after):
        return self.w[1]

    def up_weight(self, after):
        return self.w[2]

    def down_weight(self, after):
        return self.w[3]

    def mlp_grads(self, g_up, g_down):
        self.grads.update(w_up=g_up, w_down=g_down)
        return None

    def mlp_grads_sent(self, after):
        return None

    def out_grad(self, g_out):
        self.grads.update(w_out=g_out)
        return None

    def in_grad(self, g_in):
        self.grads.update(w_in=g_in)
        return None


def _local_step(x, tgt, p, hooks):
    T = x.shape[0]
    D = D_MODEL
    h1 = _rmsnorm_fwd(x, p["mix_norm_g"], "norm_mix")
    w_in_p, conv_w_f, token = hooks.mixer_weights(h1)
    (proj,) = _mm_simple(h1, w_in_p, mode="nn", M=T, N=NP, K=D, tm=min(T, 1024), tn=1536, tk=D, out_dtype=F32,
                         name="in_proj", after=token)
    xbc = _conv_fwd(proj, conv_w_f, p["conv_b"], "conv_fwd")
    dtT = proj[:, OFF_DT:OFF_DT + N_HEADS].T
    dtbT = p["dt_bias"].T
    alogT = p["A_log"].T
    dfull = jnp.repeat(p["D_skip"], HEAD_DIM, axis=1)
    y_ssm, ypre, hs = _ssd_fwd(xbc, proj, dtT, p["dt_bias"], dtbT, p["A_log"], alogT, dfull, p["ssm_norm_g"],
                               "ssd_fwd")
    kpad = jnp.pad(proj[:, OFF_K:OFF_K + KV_W], ((WINDOW, 0), (0, 0)))
    vpad = jnp.pad(proj[:, OFF_V:OFF_V + KV_W], ((WINDOW, 0), (0, 0)))
    y_att, o_att = _attn_fwd(proj, kpad, vpad, p["attn_sinks"], p["attn_out_norm_g"], "attn_fwd")
    ycat = jnp.concatenate([y_ssm, y_att], axis=1)
    w_out_f = hooks.out_weight(ycat)
    tm = min(T, 1024)
    (x2,) = _mm_simple(ycat, w_out_f, mode="nn", M=T, N=D, K=D, tm=tm, tn=1024, tk=D, out_dtype=F32, name="out_proj",
                       extras=(x,), epilogue=lambda acc, res: (acc + res,))
    h2 = _rmsnorm_fwd(x2, p["mlp_norm_g"], "norm_mlp")
    w_up_s = hooks.up_weight(h2)
    grid = (T // tm, N_DEV, 1)
    u, act = _matmul(
        h2, w_up_s, mode="nn", grid=grid,
        a_spec=pl.BlockSpec((tm, D), lambda i, j, k: (i, 0)),
        b_spec=pl.BlockSpec((None, D, 1024), lambda i, j, k: (j, 0, 0)),
        out_shapes=[jax.ShapeDtypeStruct((T, D_FF), F32), jax.ShapeDtypeStruct((T, D_FF), BF16)],
        out_specs=[pl.BlockSpec((tm, 1024), lambda i, j, k: (i, j))] * 2, tile=(tm, 1024), name="mlp_up",
        epilogue=lambda acc: (acc, jnp.square(jnp.maximum(acc, 0.0))))
    w_down_f = hooks.down_weight(act)
    (x3,) = _mm_simple(act, w_down_f, mode="nn", M=T, N=D, K=D_FF, tm=tm, tn=1024, tk=2048, out_dtype=F32,
                       name="mlp_down", extras=(x2,), epilogue=lambda acc, res: (acc + res,))
    loss_part, d_fin, dx3, dx3b = _final_loss(x3, tgt, p["final_norm_g"].reshape(1, D), "loss_head")
    (g_down,) = _mm_simple(act, dx3b, mode="tn", M=D_FF, N=D, K=T, tm=1024, tn=1024, tk=T, out_dtype=BF16,
                           name="grad_w_down")
    (du,) = _mm_simple(dx3b, w_down_f, mode="nt", M=T, N=D_FF, K=D, tm=tm, tn=1024, tk=D, out_dtype=BF16,
                       name="mlp_down_bwd", extras=(u,),
                       epilogue=lambda acc, uu: (acc * (2.0 * jnp.maximum(uu, 0.0)),))
    (g_up,) = _matmul(
        h2, du, mode="tn", grid=(D // 1024, N_DEV, 1),
        a_spec=pl.BlockSpec((T, 1024), lambda i, j, k: (0, i)),
        b_spec=pl.BlockSpec((T, 1024), lambda i, j, k: (0, j)),
        out_shapes=[jax.ShapeDtypeStruct((N_DEV, D, 1024), BF16)],
        out_specs=[pl.BlockSpec((None, 1024, 1024), lambda i, j, k: (j, i, 0))], tile=(1024, 1024), name="grad_w_up")
    token = hooks.mlp_grads(g_up, g_down)
    (dh2,) = _matmul(
        du, w_up_s, mode="nt", grid=(T // tm, D // 1024, N_DEV // 2),
        a_spec=pl.BlockSpec((tm, 2048), lambda i, j, k: (i, k)),
        b_spec=pl.BlockSpec((2, 1024, 1024), lambda i, j, k: (k, j, 0)),
        out_shapes=[jax.ShapeDtypeStruct((T, D), F32)],
        out_specs=[pl.BlockSpec((tm, 1024), lambda i, j, k: (i, j))], tile=(tm, 1024), name="mlp_up_bwd",
        after=token, dot_fn=lambda a, b: _dot_nt(a[:, :1024], b[0]) + _dot_nt(a[:, 1024:], b[1]))
    token = hooks.mlp_grads_sent(dh2)
    dx2, dx2b, d_mlp = _rmsnorm_bwd(dh2, x2, p["mlp_norm_g"], dx3, "norm_mlp_bwd")
    (g_out,) = _mm_simple(ycat, dx2b, mode="tn", M=D, N=D, K=T, tm=1024, tn=1024, tk=T, out_dtype=BF16,
                          name="grad_w_out", after=token)
    token = hooks.out_grad(g_out)
    (dy,) = _mm_simple(dx2b, w_out_f, mode="nt", M=T, N=D, K=D, tm=tm, tn=1024, tk=D, out_dtype=F32,
                       name="out_proj_bwd", after=token)
    dz, dxbc_act, ddt, d_dtb, d_alog, d_dskip, d_ssmg = _ssd_bwd(
        xbc, proj, dtT, p["dt_bias"], dtbT, p["A_log"], alogT, dfull, p["ssm_norm_g"], ypre, hs, dy, "ssd_bwd")
    dq, dkpad, dvpad, d_sinks, d_attng = _attn_bwd(proj, kpad, vpad, p["attn_sinks"], p["attn_out_norm_g"], o_att, dy,
                                                   "attn_bwd")
    dxbc, d_convw, d_convb = _conv_bwd(proj, dxbc_act, conv_w_f, p["conv_b"], "conv_bwd")
    dproj = jnp.concatenate(
        [dz, dxbc, dq, dkpad[WINDOW:].astype(BF16), dvpad[WINDOW:].astype(BF16), ddt.astype(BF16),
         jnp.zeros((T, NP - OFF_DT - 128), BF16)], axis=1)
    (g_in,) = _mm_simple(h1, dproj, mode="tn", M=D, N=NP, K=T, tm=1024, tn=1536, tk=T, out_dtype=BF16,
                         name="grad_w_in")
    token = hooks.in_grad(g_in)
    (dh1,) = _mm_simple(dproj, w_in_p, mode="nt", M=T, N=D, K=NP, tm=tm, tn=1024, tk=2304, out_dtype=F32,
                        name="in_proj_bwd", after=token)
    dx, _, d_mix = _rmsnorm_bwd(dh1, x, p["mix_norm_g"], dx2, "norm_mix_bwd")
    small = _pack_small(d_mix, d_convb, d_ssmg, d_attng, d_mlp, d_fin, d_convw, d_dtb, d_alog, d_dskip, d_sinks,
                        extra=loss_part[:, 0:1])
    return dx, small


def _landing(own, me):
    zone = lax.empty((N_DEV,) + own.shape, own.dtype)
    return lax.dynamic_update_slice(zone, own[None], (me,) + (0,) * own.ndim)


def _gather_end(started, after, plan, name):
    _, lands = _remote_wait(started, after, plan, name + "_wait")
    return _gather_finish(lands, name + "_finish")


class _ShardedWeights:
    def __init__(self, w_in, w_out, conv_w, w_up, w_down, me, csel):
        self.me, self.csel = me, csel
        shards = [w_in.astype(BF16), conv_w]
        self.st_mixer = _remote_start(shards, [_landing(s, me) for s in shards], _gather_plan, 4 * len(shards),
                                      "gather_start_mixer")
        self.start_token = self.st_mixer[4]
        self.later = {"out": w_out.astype(BF16), "up": w_up.astype(BF16), "down": w_down.astype(BF16)}
        self.later_lands = {k: _landing(v, me) for k, v in self.later.items()}
        self.st_later = {}
        self.reduces = {}
        self.pair_mlp = None

    def mixer_weights(self, after):
        g_in, g_conv = _gather_end(self.st_mixer, after, _gather_plan, "gather_mixer")
        order = g_conv
        for k in ["out", "up", "down"]:
            self.st_later[k] = _remote_start([self.later[k]], [self.later_lands[k]], _gather_plan, 4,
                                             f"gather_start_{k}", after=order)
            order = self.st_later[k][4]
        per = IN_PROJ // N_DEV
        k, off = NAT_DT // per, NAT_DT % per
        pieces = [g_in[i] for i in range(k)] + [g_in[k][:, :off], g_in[k][:, off + N_HEADS:]]
        pieces += [g_in[i] for i in range(k + 1, N_DEV)]
        pieces += [g_in[k][:, off:off + N_HEADS], jnp.zeros((D_MODEL, NP - IN_PROJ), BF16)]
        w_in_p = jnp.concatenate(pieces, axis=1)
        conv_w_f = jnp.concatenate([g_conv[i] for i in range(N_DEV)], axis=1)
        return w_in_p, conv_w_f, order

    def out_weight(self, after):
        return _gather_end(self.st_later["out"], after, _gather_plan, "gather_out")[0].reshape(D_MODEL, D_MODEL)

    def up_weight(self, after):
        return _gather_end(self.st_later["up"], after, _gather_plan, "gather_up")[0]

    def down_weight(self, after):
        return _gather_end(self.st_later["down"], after, _gather_plan, "gather_down")[0].reshape(D_FF, D_MODEL)

    def _chips_start(self, slabs, from_sibling, rows, tag):
        sums = [_pair_add(s, r, self.csel, tr, f"pair_add_{tag}_{i}")
                for i, (s, r, tr) in enumerate(zip(slabs, from_sibling, rows))]
        lands = [lax.empty((3,) + s.shape[1:], s.dtype) for s in sums]
        self.reduces[tag] = _remote_start(sums, lands, _chips_plan, 3 * len(sums), f"reduce_start_{tag}")
        return self.reduces[tag][4]

    def mlp_grads(self, g_up, g_down):
        slabs = [g_up, g_down.reshape(N_DEV, D_FF // N_DEV, D_MODEL)]
        lands = [lax.empty((4,) + s.shape[1:], s.dtype) for s in slabs]
        self.pair_mlp = _remote_start(slabs, lands, _pair_plan, 8, "reduce_pair_start_mlp")
        return self.pair_mlp[4]

    def mlp_grads_sent(self, after):
        slabs, from_sibling = _remote_wait(self.pair_mlp, after, _pair_plan, "reduce_pair_wait_mlp")
        return self._chips_start(slabs, from_sibling, [512, 256], "mlp")

    def out_grad(self, g_out):
        slabs = [g_out.reshape(N_DEV, D_MODEL // N_DEV, D_MODEL)]
        return self._chips_start(slabs, _exchange_pair(slabs, "reduce_pair_out"), [256], "out")

    def in_grad(self, g_in):
        per = IN_PROJ // N_DEV
        k, off = NAT_DT // per, NAT_DT % per

        def slab(i):
            if i < k:
                return g_in[:, i * per:(i + 1) * per]
            if i == k:
                return jnp.concatenate([g_in[:, k * per:NAT_DT], g_in[:, OFF_DT:OFF_DT + N_HEADS],
                                        g_in[:, NAT_DT:NAT_DT + per - off - N_HEADS]], axis=1)
            return g_in[:, i * per - N_HEADS:(i + 1) * per - N_HEADS]

        slabs = [jnp.stack([slab(i) for i in range(N_DEV)])]
        return self._chips_start(slabs, _exchange_pair(slabs, "reduce_pair_in"), [256], "in")

    def small_start(self, small):
        self.st_small = _remote_start([small], [_landing(small, self.me)], _everyone_plan, N_DEV - 1, "gather_start_small")

    def small_end(self, after):
        return _remote_wait(self.st_small, after, _everyone_plan, "gather_small_wait")[1][0]

    def reduce_end(self, tag, after):
        return _remote_wait(self.reduces[tag], after, _chips_plan, f"reduce_wait_{tag}")


def kernel(x, mix_norm_g, w_in, conv_w, conv_b, dt_bias, A_log, D_skip, ssm_norm_g, attn_sinks, attn_out_norm_g, w_out, mlp_norm_g, w_up, w_down, final_norm_g, loss_target, m_mix_norm_g, m_w_in, m_conv_w, m_conv_b, m_dt_bias, m_A_log, m_D_skip, m_ssm_norm_g, m_attn_sinks, m_attn_out_norm_g, m_w_out, m_mlp_norm_g, m_w_up, m_w_down, m_final_norm_g, v_mix_norm_g, v_w_in, v_conv_w, v_conv_b, v_dt_bias, v_A_log, v_D_skip, v_ssm_norm_g, v_attn_sinks, v_attn_out_norm_g, v_w_out, v_mlp_norm_g, v_w_up, v_w_down, v_final_norm_g):
    xi, yi, ci = _coords()
    me = 4 * xi + 2 * yi + ci
    csel = jnp.reshape(ci, (1,)).astype(jnp.int32)
    qsel = jnp.reshape(2 * xi + yi, (1,)).astype(jnp.int32)
    w = dict(mix_norm_g=mix_norm_g, conv_b=conv_b, dt_bias=dt_bias, A_log=A_log, D_skip=D_skip,
             ssm_norm_g=ssm_norm_g, attn_sinks=attn_sinks, attn_out_norm_g=attn_out_norm_g, mlp_norm_g=mlp_norm_g,
             final_norm_g=final_norm_g)
    hooks = _ShardedWeights(w_in[0], w_out[0], conv_w[0], w_up[0], w_down[0], me, csel)
    p = dict(w, mix_norm_g=mix_norm_g + hooks.start_token[0:1, 0:1])
    dx, small = _local_step(x[0], loss_target[0], p, hooks)
    hooks.small_start(small)
    big = {}
    after = dx
    for tag, members in [("mlp", [("w_up", w_up, m_w_up, v_w_up, 512), ("w_down", w_down, m_w_down, v_w_down, 256)]),
                         ("out", [("w_out", w_out, m_w_out, v_w_out, 256)]),
                         ("in", [("w_in", w_in, m_w_in, v_w_in, 256)])]:
        chip_sums, from_chips = hooks.reduce_end(tag, after)
        for i, (name, wt, mt, vt, tr) in enumerate(members):
            g, d, mn, vn = _adamw_big(wt[0], mt[0], vt[0], chip_sums[i], from_chips[i], qsel, tr, f"adamw_{name}")
            big[name] = (g[None], d[None], mn[None], vn[None])
            after = g
    gsum = _small_sum(hooks.small_end(after), "small_sum")
    loss = gsum[9, 64]
    gs = _unpack_small(gsum, CONV_DIM)
    cw = CONV_DIM // N_DEV
    g_conv_shard = lax.dynamic_slice(gsum[5:9, :], (0, me * cw), (CONV_K, cw))

    def pack(s):
        return _pack_small(s["mix_norm_g"], s["conv_b"], s["ssm_norm_g"], s["attn_out_norm_g"], s["mlp_norm_g"],
                           s["final_norm_g"], s["conv_w"][0], s["dt_bias"], s["A_log"], s["D_skip"], s["attn_sinks"])

    wp = pack(dict(w, conv_w=conv_w))
    mp = pack(dict(mix_norm_g=m_mix_norm_g, conv_b=m_conv_b, ssm_norm_g=m_ssm_norm_g,
                   attn_out_norm_g=m_attn_out_norm_g, mlp_norm_g=m_mlp_norm_g, final_norm_g=m_final_norm_g,
                   conv_w=m_conv_w, dt_bias=m_dt_bias, A_log=m_A_log, D_skip=m_D_skip, attn_sinks=m_attn_sinks))
    vp = pack(dict(mix_norm_g=v_mix_norm_g, conv_b=v_conv_b, ssm_norm_g=v_ssm_norm_g,
                   attn_out_norm_g=v_attn_out_norm_g, mlp_norm_g=v_mlp_norm_g, final_norm_g=v_final_norm_g,
                   conv_w=v_conv_w, dt_bias=v_dt_bias, A_log=v_A_log, D_skip=v_D_skip, attn_sinks=v_attn_sinks))
    gp = jnp.concatenate([gsum[0:5], jnp.pad(g_conv_shard, ((0, 0), (0, D_MODEL - cw))), gsum[9:10],
                          jnp.zeros((SMALL_ROWS - 10, D_MODEL), F32)], axis=0)
    dp, mnp, vnp = _adamw_small(wp, gp, mp, vp, "adamw_small")
    grads = dict(gs, conv_w=g_conv_shard[None])
    deltas = _unpack_small(dp, cw)
    new_m = _unpack_small(mnp, cw)
    new_v = _unpack_small(vnp, cw)
    for k, name in enumerate(["w_in", "w_out", "w_up", "w_down"]):
        grads[name], deltas[name], new_m[name], new_v[name] = big[name]
    return (loss, dx[None], *[grads[n] for n in WEIGHT_ORDER], *[deltas[n] for n in WEIGHT_ORDER],
            *[new_m[n] for n in WEIGHT_ORDER], *[new_v[n] for n in WEIGHT_ORDER])
```

```python
import functools

import jax
import jax.numpy as jnp
from jax import lax
from jax.experimental import pallas as pl
from jax.experimental.pallas import tpu as pltpu

F32 = jnp.float32
BF16 = jnp.bfloat16
HI = lax.Precision.HIGHEST
MESH = pl.DeviceIdType.MESH

EPS = 1e-5
D_MODEL = 2048
D_INNER = 1024
N_HEADS = 16
HEAD_DIM = 64
N_GROUPS = 4
D_STATE = 128
CHUNK = 128
CONV_K = 4
CONV_DIM = 2048
ATTN_W = 1024
KV_W = 128
WINDOW = 128
D_FF = 8192
IN_PROJ = 4368
N_DEV = 8
NP = 4608
OFF_Z, OFF_X, OFF_B, OFF_C, OFF_Q, OFF_K, OFF_V, OFF_DT = 0, 1024, 2048, 2560, 3072, 4096, 4224, 4352
NAT_DT = 3072

ADAM_LR = 0.001
ADAM_B1 = 0.9
ADAM_B2 = 0.999
ADAM_EPS = 1e-08
ADAM_WD = 0.01
ADAM_STEP = 10

VMEM_LIMIT = 52 * 1024 * 1024
SMALL_ROWS = 16
NEG = -1e30


def _cparams(sem=None):
    return pltpu.CompilerParams(dimension_semantics=sem, vmem_limit_bytes=VMEM_LIMIT)


def _split3(v):
    hi = v.astype(BF16)
    rest = v - hi.astype(F32)
    mid = rest.astype(BF16)
    return hi, mid, (rest - mid.astype(F32)).astype(BF16)


def _hdot(a, b, data):
    if data == "a":
        sel = b.astype(BF16)
        return sum(_dot_nn(part, sel) for part in _split3(a))
    sel = a.astype(BF16)
    return sum(_dot_nn(sel, part) for part in _split3(b))


def _dot_nn(a, b):
    return lax.dot_general(a, b, (((1,), (0,)), ((), ())), preferred_element_type=F32)


def _dot_nt(a, b):
    return lax.dot_general(a, b, (((1,), (1,)), ((), ())), preferred_element_type=F32)


def _dot_tn(a, b):
    return lax.dot_general(a, b, (((0,), (0,)), ((), ())), preferred_element_type=F32)


def _softplus(v):
    return jnp.maximum(v, 0.0) + jnp.log1p(jnp.exp(-jnp.abs(v)))


def _sigmoid(v):
    return 1.0 / (1.0 + jnp.exp(-v))


def _matmul(a, b, *, mode, grid, a_spec, b_spec, out_shapes, out_specs, tile, name,
            extras=(), extra_specs=(), epilogue=None, after=None, dot_fn=None, prefetch=None):
    nk = grid[2]
    n_ex = len(extras)
    n_out = len(out_shapes)
    dot = dot_fn if dot_fn is not None else {"nn": _dot_nn, "nt": _dot_nt, "tn": _dot_tn}[mode]

    def finish(acc, ex_refs, out_refs):
        res = (acc,) if epilogue is None else epilogue(acc, *[e[...] for e in ex_refs])
        for o, r in zip(out_refs, res):
            o[...] = r.astype(o.dtype)

    def body(*refs):
        a_ref, b_ref = refs[0], refs[1]
        ex_refs = refs[2:2 + n_ex]
        out_refs = refs[2 + n_ex:2 + n_ex + n_out]
        part = dot(a_ref[...].astype(BF16), b_ref[...].astype(BF16))
        if nk == 1:
            finish(part, ex_refs, out_refs)
        else:
            acc_ref = refs[-1]
            k = pl.program_id(2)

            @pl.when(k == 0)
            def _():
                acc_ref[...] = part

            @pl.when(k > 0)
            def _():
                acc_ref[...] += part

            @pl.when(k == nk - 1)
            def _():
                finish(acc_ref[...], ex_refs, out_refs)

    scratch = [] if nk == 1 else [pltpu.VMEM(tile, F32)]
    n_pre = 0 if prefetch is None else 1
    tok_specs = [] if after is None else [pl.BlockSpec((8, 128), lambda *_: (0, 0))]
    tok_args = [] if after is None else [after]

    def body_with_token(*refs):
        refs = refs[n_pre:]
        body(*refs[:2 + n_ex], *refs[2 + n_ex + len(tok_args):])

    in_specs = [a_spec, b_spec, *extra_specs, *tok_specs]
    params = _cparams(("parallel", "parallel", "arbitrary"))
    if prefetch is None:
        return pl.pallas_call(
            body_with_token, grid=grid, in_specs=in_specs, out_specs=list(out_specs), out_shape=list(out_shapes),
            scratch_shapes=scratch, name=name, compiler_params=params)(a, b, *extras, *tok_args)
    return pl.pallas_call(
        body_with_token,
        grid_spec=pltpu.PrefetchScalarGridSpec(num_scalar_prefetch=1, grid=grid, in_specs=in_specs,
                                               out_specs=list(out_specs), scratch_shapes=scratch),
        out_shape=list(out_shapes), name=name, compiler_params=params)(prefetch, a, b, *extras, *tok_args)


def _mm_simple(a, b, *, mode, M, N, K, tm, tn, tk, out_dtype, name, extras=(), epilogue=None, n_out=1,
               out_dtypes=None, after=None):
    grid = (M // tm, N // tn, K // tk)
    if mode == "nn":
        a_spec = pl.BlockSpec((tm, tk), lambda i, j, k: (i, k))
        b_spec = pl.BlockSpec((tk, tn), lambda i, j, k: (k, j))
    elif mode == "nt":
        a_spec = pl.BlockSpec((tm, tk), lambda i, j, k: (i, k))
        b_spec = pl.BlockSpec((tn, tk), lambda i, j, k: (j, k))
    else:
        a_spec = pl.BlockSpec((tk, tm), lambda i, j, k: (k, i))
        b_spec = pl.BlockSpec((tk, tn), lambda i, j, k: (k, j))
    o_spec = pl.BlockSpec((tm, tn), lambda i, j, k: (i, j))
    dts = out_dtypes if out_dtypes is not None else [out_dtype] * n_out
    return _matmul(a, b, mode=mode, grid=grid, a_spec=a_spec, b_spec=b_spec,
                   out_shapes=[jax.ShapeDtypeStruct((M, N), d) for d in dts],
                   out_specs=[o_spec] * len(dts), tile=(tm, tn), name=name,
                   extras=extras, extra_specs=[o_spec] * len(extras), epilogue=epilogue, after=after)


ROW_BLOCK = 256


def _rmsnorm_fwd(x, g, name):
    T, D = x.shape

    def body(x_ref, g_ref, o_ref):
        xf = x_ref[...]
        r = lax.rsqrt(jnp.mean(xf * xf, axis=-1, keepdims=True) + EPS)
        o_ref[...] = (xf * r * g_ref[...]).astype(BF16)

    return pl.pallas_call(
        body, grid=(T // ROW_BLOCK,),
        in_specs=[pl.BlockSpec((ROW_BLOCK, D), lambda i: (i, 0)), pl.BlockSpec((1, D), lambda i: (0, 0))],
        out_specs=pl.BlockSpec((ROW_BLOCK, D), lambda i: (i, 0)),
        out_shape=jax.ShapeDtypeStruct((T, D), BF16), name=name, compiler_params=_cparams(("parallel",)),
    )(x, g)


def _rmsnorm_bwd(dh, x, g, dres, name):
    T, D = x.shape

    def body(dh_ref, x_ref, g_ref, dres_ref, dx_ref, dxb_ref, dg_ref):
        i = pl.program_id(0)
        xf = x_ref[...]
        r = lax.rsqrt(jnp.mean(xf * xf, axis=-1, keepdims=True) + EPS)
        xh = xf * r
        d = dh_ref[...]

        @pl.when(i == 0)
        def _():
            dg_ref[...] = jnp.zeros_like(dg_ref)

        dg_ref[...] += jnp.sum(d * xh, axis=0, keepdims=True)
        dxh = d * g_ref[...]
        dx = r * (dxh - xh * jnp.mean(dxh * xh, axis=-1, keepdims=True)) + dres_ref[...]
        dx_ref[...] = dx
        dxb_ref[...] = dx.astype(BF16)

    row = pl.BlockSpec((ROW_BLOCK, D), lambda i: (i, 0))
    vec = pl.BlockSpec((1, D), lambda i: (0, 0))
    return pl.pallas_call(
        body, grid=(T // ROW_BLOCK,), in_specs=[row, row, vec, row], out_specs=[row, row, vec],
        out_shape=[jax.ShapeDtypeStruct((T, D), F32), jax.ShapeDtypeStruct((T, D), BF16),
                   jax.ShapeDtypeStruct((1, D), F32)],
        name=name, compiler_params=_cparams(("arbitrary",)),
    )(dh, x, g, dres)


def _final_loss(x3, tgt, g, name):
    T, D = x3.shape

    def body(x_ref, t_ref, g_ref, loss_ref, dg_ref, dx_ref, dxb_ref):
        i = pl.program_id(0)
        xf = x_ref[...]
        r = lax.rsqrt(jnp.mean(xf * xf, axis=-1, keepdims=True) + EPS)
        xh = xf * r
        gg = g_ref[...]
        err = xh * gg - t_ref[...]

        @pl.when(i == 0)
        def _():
            dg_ref[...] = jnp.zeros_like(dg_ref)
            loss_ref[...] = jnp.zeros_like(loss_ref)

        part = jnp.sum(jnp.sum(err * err, axis=-1, keepdims=True), axis=0, keepdims=True) * (0.5 / D)
        loss_ref[...] += jnp.broadcast_to(part, loss_ref.shape)
        dout = err * (1.0 / D)
        dg_ref[...] += jnp.sum(dout * xh, axis=0, keepdims=True)
        dxh = dout * gg
        dx = r * (dxh - xh * jnp.mean(dxh * xh, axis=-1, keepdims=True))
        dx_ref[...] = dx
        dxb_ref[...] = dx.astype(BF16)

    row = pl.BlockSpec((ROW_BLOCK, D), lambda i: (i, 0))
    vec = pl.BlockSpec((1, D), lambda i: (0, 0))
    return pl.pallas_call(
        body, grid=(T // ROW_BLOCK,), in_specs=[row, row, vec],
        out_specs=[pl.BlockSpec((1, 128), lambda i: (0, 0)), vec, row, row],
        out_shape=[jax.ShapeDtypeStruct((1, 128), F32), jax.ShapeDtypeStruct((1, D), F32),
                   jax.ShapeDtypeStruct((T, D), F32), jax.ShapeDtypeStruct((T, D), BF16)],
        name=name, compiler_params=_cparams(("arbitrary",)),
    )(x3, tgt, g)


CONV_BLOCK = 256


def _conv_apply(u, w, b):
    row = lax.broadcasted_iota(jnp.int32, u.shape, 0)
    acc = b + w[CONV_K - 1:CONV_K, :] * u
    shifted = []
    for j in range(1, CONV_K):
        uj = jnp.where(row >= j, pltpu.roll(u, j, axis=0), 0.0)
        shifted.append(uj)
        acc = acc + w[CONV_K - 1 - j:CONV_K - j, :] * uj
    return acc, shifted


def _conv_fwd(proj, conv_w, conv_b, name):
    T = proj.shape[0]
    cb0 = OFF_X // CONV_BLOCK

    def body(u_ref, w_ref, b_ref, o_ref):
        c, _ = _conv_apply(u_ref[...], w_ref[...], b_ref[...])
        o_ref[...] = c * _sigmoid(c)

    return pl.pallas_call(
        body, grid=(CONV_DIM // CONV_BLOCK,),
        in_specs=[pl.BlockSpec((T, CONV_BLOCK), lambda j: (0, cb0 + j)),
                  pl.BlockSpec((CONV_K, CONV_BLOCK), lambda j: (0, j)),
                  pl.BlockSpec((1, CONV_BLOCK), lambda j: (0, j))],
        out_specs=pl.BlockSpec((T, CONV_BLOCK), lambda j: (0, j)),
        out_shape=jax.ShapeDtypeStruct((T, CONV_DIM), F32), name=name, compiler_params=_cparams(("parallel",)),
    )(proj, conv_w, conv_b)


def _conv_bwd(proj, dact, conv_w, conv_b, name):
    T = proj.shape[0]
    cb0 = OFF_X // CONV_BLOCK

    def body(u_ref, d_ref, w_ref, b_ref, du_ref, dw_ref, db_ref):
        u = u_ref[...]
        w = w_ref[...]
        c, shifted = _conv_apply(u, w, b_ref[...])
        sg = _sigmoid(c)
        dc = d_ref[...] * sg * (1.0 + c * (1.0 - sg))
        row = lax.broadcasted_iota(jnp.int32, u.shape, 0)
        du = w[CONV_K - 1:CONV_K, :] * dc
        dw_ref[CONV_K - 1:CONV_K, :] = jnp.sum(dc * u, axis=0, keepdims=True)
        for j in range(1, CONV_K):
            dcj = jnp.where(row < T - j, pltpu.roll(dc, T - j, axis=0), 0.0)
            du = du + w[CONV_K - 1 - j:CONV_K - j, :] * dcj
            dw_ref[CONV_K - 1 - j:CONV_K - j, :] = jnp.sum(dc * shifted[j - 1], axis=0, keepdims=True)
        db_ref[...] = jnp.sum(dc, axis=0, keepdims=True)
        du_ref[...] = du.astype(BF16)

    return pl.pallas_call(
        body, grid=(CONV_DIM // CONV_BLOCK,),
        in_specs=[pl.BlockSpec((T, CONV_BLOCK), lambda j: (0, cb0 + j)),
                  pl.BlockSpec((T, CONV_BLOCK), lambda j: (0, j)),
                  pl.BlockSpec((CONV_K, CONV_BLOCK), lambda j: (0, j)),
                  pl.BlockSpec((1, CONV_BLOCK), lambda j: (0, j))],
        out_specs=[pl.BlockSpec((T, CONV_BLOCK), lambda j: (0, j)),
                   pl.BlockSpec((CONV_K, CONV_BLOCK), lambda j: (0, j)),
                   pl.BlockSpec((1, CONV_BLOCK), lambda j: (0, j))],
        out_shape=[jax.ShapeDtypeStruct((T, CONV_DIM), BF16), jax.ShapeDtypeStruct((CONV_K, CONV_DIM), F32),
                   jax.ShapeDtypeStruct((1, CONV_DIM), F32)],
        name=name, compiler_params=_cparams(("parallel",)),
    )(proj, dact, conv_w, conv_b)


GROUP_W = D_INNER // N_GROUPS
HEADS_PER_GROUP = N_HEADS // N_GROUPS


def _expand_mat():
    h = lax.broadcasted_iota(jnp.int32, (N_HEADS, D_INNER), 0)
    j = lax.broadcasted_iota(jnp.int32, (N_HEADS, D_INNER), 1)
    return (j // HEAD_DIM == h).astype(F32)


def _reduce_mat(g):
    j = lax.broadcasted_iota(jnp.int32, (GROUP_W, N_HEADS), 0)
    h = lax.broadcasted_iota(jnp.int32, (GROUP_W, N_HEADS), 1)
    return (g * HEADS_PER_GROUP + j // HEAD_DIM == h).astype(F32)


def _col16(v, h):
    lane = lax.broadcasted_iota(jnp.int32, v.shape, 1)
    return jnp.sum(jnp.where(lane == h, v, 0.0), axis=1, keepdims=True)


def _ssd_pre(dt_raw, dtT_raw, dtb, dtbT, alog, alogT):
    Q = CHUNK
    xdt = dt_raw + dtb
    dt = _softplus(xdt)
    dtT = _softplus(dtT_raw + dtbT)
    A = -jnp.exp(alog)
    AT = -jnp.exp(alogT)
    row = lax.broadcasted_iota(jnp.int32, (Q, Q), 0)
    col = lax.broadcasted_iota(jnp.int32, (Q, Q), 1)
    tril = (row >= col).astype(F32)
    triu = (row <= col).astype(F32)
    cs = _hdot(tril, dt * A, "b")
    csT = _hdot(dtT * AT, triu, "a")
    return xdt, dt, A, cs, csT, row >= col, triu


def _decay_matrix(cs, csT, h, causal):
    seg = _col16(cs, h) - csT[h:h + 1, :]
    return jnp.where(causal, jnp.exp(jnp.minimum(seg, 0.0)), 0.0)


def _ssd_in_specs(nc, rev):
    def cidx(c):
        return (nc - 1 - c) if rev else c

    return [
        pl.BlockSpec((CHUNK, D_INNER), lambda c: (cidx(c), 0)),
        pl.BlockSpec((CHUNK, 512), lambda c: (cidx(c), 2)),
        pl.BlockSpec((CHUNK, 512), lambda c: (cidx(c), 3)),
        pl.BlockSpec((CHUNK, D_INNER), lambda c: (cidx(c), 0)),
        pl.BlockSpec((CHUNK, 128), lambda c: (cidx(c), OFF_DT // 128)),
        pl.BlockSpec((N_HEADS, CHUNK), lambda c: (0, cidx(c))),
        pl.BlockSpec((1, N_HEADS), lambda c: (0, 0)),
        pl.BlockSpec((N_HEADS, 1), lambda c: (0, 0)),
        pl.BlockSpec((1, N_HEADS), lambda c: (0, 0)),
        pl.BlockSpec((N_HEADS, 1), lambda c: (0, 0)),
        pl.BlockSpec((1, D_INNER), lambda c: (0, 0)),
        pl.BlockSpec((1, D_INNER), lambda c: (0, 0)),
    ]


def _ssd_fwd(xbc, proj, dtT, dtb, dtbT, alog, alogT, dfull, ng, name):
    T = xbc.shape[0]
    nc = T // CHUNK
    Q = CHUNK

    def body(xs_ref, B_ref, C_ref, z_ref, dt_ref, dtT_ref, dtb_ref, dtbT_ref, al_ref, alT_ref, df_ref, ng_ref,
             y_ref, ypre_ref, hs_ref, h_scr):
        c = pl.program_id(0)

        @pl.when(c == 0)
        def _():
            h_scr[...] = jnp.zeros_like(h_scr)

        _, dt, _, cs, csT, causal, _ = _ssd_pre(dt_ref[:, :N_HEADS], dtT_ref[...], dtb_ref[...], dtbT_ref[...],
                                                al_ref[...], alT_ref[...])
        ex = _expand_mat()
        dt_full = _hdot(dt, ex, "a")
        cs_full = _hdot(cs, ex, "a")
        cs_last = cs_full[Q - 1:Q, :]
        xs = xs_ref[...]
        xd = xs * dt_full
        e_full = jnp.exp(cs_full)
        dec_full = jnp.exp(cs_last - cs_full)
        cd_full = jnp.exp(cs_last)
        lane_head = lax.broadcasted_iota(jnp.int32, (1, GROUP_W), 1) // HEAD_DIM
        for g in range(N_GROUPS):
            sl = slice(g * GROUP_W, (g + 1) * GROUP_W)
            Bg = B_ref[:, g * D_STATE:(g + 1) * D_STATE].astype(BF16)
            Cg = C_ref[:, g * D_STATE:(g + 1) * D_STATE].astype(BF16)
            CB = _dot_nt(Cg, Bg)
            hg = h_scr[g]
            yoff = _dot_nn(Cg, hg.astype(BF16)) * e_full[:, sl]
            xd_g = xd[:, sl]
            S = _dot_tn(Bg, (xd_g * dec_full[:, sl]).astype(BF16))
            xd_b = xd_g.astype(BF16)
            ydiag = jnp.zeros((Q, GROUP_W), F32)
            for r in range(HEADS_PER_GROUP):
                Lm = _decay_matrix(cs, csT, g * HEADS_PER_GROUP + r, causal)
                Gm = (CB * Lm).astype(BF16)
                ydiag = ydiag + _dot_nn(Gm, jnp.where(lane_head == r, xd_b, jnp.zeros_like(xd_b)))
            hs_ref[0, g] = hg
            h_scr[g] = hg * cd_full[:, sl] + S
            ypre = ydiag + yoff + xs[:, sl] * df_ref[:, sl]
            ypre_ref[:, sl] = ypre
            zg = z_ref[:, sl]
            yz = ypre * zg * _sigmoid(zg)
            rn = lax.rsqrt(jnp.mean(yz * yz, axis=-1, keepdims=True) + EPS)
            y_ref[:, sl] = (yz * rn * ng_ref[:, sl]).astype(BF16)

    return pl.pallas_call(
        body, grid=(nc,), in_specs=_ssd_in_specs(nc, False),
        out_specs=[pl.BlockSpec((CHUNK, D_INNER), lambda c: (c, 0)),
                   pl.BlockSpec((CHUNK, D_INNER), lambda c: (c, 0)),
                   pl.BlockSpec((1, N_GROUPS, D_STATE, GROUP_W), lambda c: (c, 0, 0, 0))],
        out_shape=[jax.ShapeDtypeStruct((T, D_INNER), BF16), jax.ShapeDtypeStruct((T, D_INNER), F32),
                   jax.ShapeDtypeStruct((nc, N_GROUPS, D_STATE, GROUP_W), F32)],
        scratch_shapes=[pltpu.VMEM((N_GROUPS, D_STATE, GROUP_W), F32)],
        name=name, compiler_params=_cparams(("arbitrary",)),
    )(xbc, xbc, xbc, proj, proj, dtT, dtb, dtbT, alog, alogT, dfull, ng)


def _ssd_bwd(xbc, proj, dtT, dtb, dtbT, alog, alogT, dfull, ng, ypre, hs, dy, name):
    T = xbc.shape[0]
    nc = T // CHUNK
    Q = CHUNK

    def body(xs_ref, B_ref, C_ref, z_ref, dt_ref, dtT_ref, dtb_ref, dtbT_ref, al_ref, alT_ref, df_ref, ng_ref,
             ypre_ref, hs_ref, dy_ref,
             dz_ref, dxbc_ref, ddt_ref, ddtb_ref, dal_ref, dD_ref, dng_ref, dh_scr):
        step = pl.program_id(0)

        @pl.when(step == 0)
        def _():
            dh_scr[...] = jnp.zeros_like(dh_scr)
            ddtb_ref[...] = jnp.zeros_like(ddtb_ref)
            dal_ref[...] = jnp.zeros_like(dal_ref)
            dD_ref[...] = jnp.zeros_like(dD_ref)
            dng_ref[...] = jnp.zeros_like(dng_ref)

        xdt, dt, A, cs, csT, causal, triu = _ssd_pre(dt_ref[:, :N_HEADS], dtT_ref[...], dtb_ref[...],
                                                    dtbT_ref[...], al_ref[...], alT_ref[...])
        ex = _expand_mat()
        dt_full = _hdot(dt, ex, "a")
        cs_full = _hdot(cs, ex, "a")
        cs_last = cs_full[Q - 1:Q, :]
        xs = xs_ref[...]
        xd = xs * dt_full
        e_full = jnp.exp(cs_full)
        dec_full = jnp.exp(cs_last - cs_full)
        cd_full = jnp.exp(cs_last)
        lane_head = lax.broadcasted_iota(jnp.int32, (1, GROUP_W), 1) // HEAD_DIM
        is_last = lax.broadcasted_iota(jnp.int32, (Q, 1), 0) == Q - 1
        dcs16 = jnp.zeros((Q, N_HEADS), F32)
        ddtx16 = jnp.zeros((Q, N_HEADS), F32)
        dD16 = jnp.zeros((8, N_HEADS), F32)
        lane16 = lax.broadcasted_iota(jnp.int32, (1, N_HEADS), 1)
        sub16 = lax.broadcasted_iota(jnp.int32, (N_HEADS, 1), 0)
        col_sums = jnp.zeros((N_HEADS, Q), F32)
        for g in range(N_GROUPS):
            sl = slice(g * GROUP_W, (g + 1) * GROUP_W)
            red = _reduce_mat(g)
            ypre_g = ypre_ref[:, sl]
            zg = z_ref[:, sl]
            sg = _sigmoid(zg)
            silu = zg * sg
            yz = ypre_g * silu
            rn = lax.rsqrt(jnp.mean(yz * yz, axis=-1, keepdims=True) + EPS)
            yh = yz * rn
            dy_g = dy_ref[:, sl]
            dng_ref[:, sl] += jnp.sum(dy_g * yh, axis=0, keepdims=True)
            dyh = dy_g * ng_ref[:, sl]
            dyz = rn * (dyh - yh * jnp.mean(dyh * yh, axis=-1, keepdims=True))
            dY = dyz * silu
            dz_ref[:, sl] = (dyz * ypre_g * sg * (1.0 + zg * (1.0 - sg))).astype(BF16)
            xs_g = xs[:, sl]
            xd_g = xd[:, sl]
            dec_g = dec_full[:, sl]
            cd_g = cd_full[:, sl]
            d_g = df_ref[:, sl]
            Bg = B_ref[:, g * D_STATE:(g + 1) * D_STATE].astype(BF16)
            Cg = C_ref[:, g * D_STATE:(g + 1) * D_STATE].astype(BF16)
            CB = _dot_nt(Cg, Bg)
            hg = hs_ref[0, g]
            hgb = hg.astype(BF16)
            yoff = _dot_nn(Cg, hgb) * e_full[:, sl]
            dhn = dh_scr[g]
            dhnb = dhn.astype(BF16)
            dYE = (dY * e_full[:, sl]).astype(BF16)
            dC = _dot_nt(dYE, hgb)
            dh_direct = _dot_tn(Cg, dYE)
            dXdd = _dot_nn(Bg, dhnb)
            dB = _dot_nt((xd_g * dec_g).astype(BF16), dhnb)
            dcd = jnp.sum(dhn * hg, axis=0, keepdims=True)
            dh_scr[g] = dh_direct + cd_g * dhn
            dYb = dY.astype(BF16)
            xd_b = xd_g.astype(BF16)
            dCB = jnp.zeros((Q, Q), F32)
            dXd = dXdd * dec_g
            for r in range(HEADS_PER_GROUP):
                h = g * HEADS_PER_GROUP + r
                Lm = _decay_matrix(cs, csT, h, causal)
                Gf = CB * Lm
                dYr = jnp.where(lane_head == r, dYb, jnp.zeros_like(dYb))
                dG = _dot_nt(dYr, xd_b)
                dCB = dCB + dG * Lm
                dXd = dXd + _dot_tn(Gf.astype(BF16), dYr)
                Mm = dG * Gf
                dcs16 = dcs16 + jnp.where(lane16 == h, jnp.sum(Mm, axis=1, keepdims=True), 0.0)
                col_sums = col_sums + jnp.where(sub16 == h, jnp.sum(Mm, axis=0, keepdims=True), 0.0)
            dCBb = dCB.astype(BF16)
            dC = dC + _dot_nn(dCBb, Bg)
            dB = dB + _dot_tn(dCBb, Cg)
            w_state = dXdd * dec_g * xd_g
            t_last = jnp.sum(w_state, axis=0, keepdims=True) + dcd * cd_g
            dcs_g = dY * yoff - w_state + jnp.where(is_last, t_last, 0.0)
            dcs16 = dcs16 + _hdot(dcs_g, red, "a")
            ddtx16 = ddtx16 + _hdot(dXd * xs_g, red, "a")
            dD16 = dD16 + _hdot(jnp.broadcast_to(jnp.sum(dY * xs_g, axis=0, keepdims=True), (8, GROUP_W)), red, "a")
            dxbc_ref[:, sl] = dXd * dt_full[:, sl] + dY * d_g
            dxbc_ref[:, D_INNER + g * D_STATE:D_INNER + (g + 1) * D_STATE] = dB
            dxbc_ref[:, D_INNER + 512 + g * D_STATE:D_INNER + 512 + (g + 1) * D_STATE] = dC
        eye = (lax.broadcasted_iota(jnp.int32, (N_HEADS, N_HEADS), 0)
               == lax.broadcasted_iota(jnp.int32, (N_HEADS, N_HEADS), 1)).astype(BF16)
        dcs16 = dcs16 - sum(_dot_tn(part, eye) for part in _split3(col_sums))
        da = _hdot(triu, dcs16, "b")
        ddt = da * A + ddtx16
        ddt_raw = ddt * _sigmoid(xdt)
        pr = lax.broadcasted_iota(jnp.int32, (N_HEADS, 128), 0)
        pc = lax.broadcasted_iota(jnp.int32, (N_HEADS, 128), 1)
        ddt_ref[...] = _hdot(ddt_raw, (pr == pc).astype(F32), "a")
        ddtb_ref[...] += jnp.sum(ddt_raw, axis=0, keepdims=True)
        dal_ref[...] += jnp.sum(da * dt, axis=0, keepdims=True) * A
        dD_ref[...] += dD16[0:1, :]

    def rc(c):
        return nc - 1 - c

    in_specs = _ssd_in_specs(nc, True) + [
        pl.BlockSpec((CHUNK, D_INNER), lambda c: (rc(c), 0)),
        pl.BlockSpec((1, N_GROUPS, D_STATE, GROUP_W), lambda c: (rc(c), 0, 0, 0)),
        pl.BlockSpec((CHUNK, D_INNER), lambda c: (rc(c), 0)),
    ]
    small = pl.BlockSpec((1, N_HEADS), lambda c: (0, 0))
    return pl.pallas_call(
        body, grid=(nc,), in_specs=in_specs,
        out_specs=[pl.BlockSpec((CHUNK, D_INNER), lambda c: (rc(c), 0)),
                   pl.BlockSpec((CHUNK, CONV_DIM), lambda c: (rc(c), 0)),
                   pl.BlockSpec((CHUNK, 128), lambda c: (rc(c), 0)),
                   small, small, small,
                   pl.BlockSpec((1, D_INNER), lambda c: (0, 0))],
        out_shape=[jax.ShapeDtypeStruct((T, D_INNER), BF16), jax.ShapeDtypeStruct((T, CONV_DIM), F32),
                   jax.ShapeDtypeStruct((T, 128), F32),
                   jax.ShapeDtypeStruct((1, N_HEADS), F32), jax.ShapeDtypeStruct((1, N_HEADS), F32),
                   jax.ShapeDtypeStruct((1, N_HEADS), F32), jax.ShapeDtypeStruct((1, D_INNER), F32)],
        scratch_shapes=[pltpu.VMEM((N_GROUPS, D_STATE, GROUP_W), F32)],
        name=name, compiler_params=_cparams(("arbitrary",)),
    )(xbc, xbc, xbc, proj, proj, dtT, dtb, dtbT, alog, alogT, dfull, ng, ypre, hs, dy)


N_PAIRS = ATTN_W // 128
PAIRS_PER_KV = N_PAIRS // 2
ATTN_SCALE = HEAD_DIM ** -0.5


def _kv_variants(kk):
    lo = lax.broadcasted_iota(jnp.int32, kk.shape, 1) < HEAD_DIM
    zero = jnp.zeros_like(kk)
    k00 = jnp.where(lo, kk, zero)
    k11 = jnp.where(lo, zero, kk)
    k01 = pltpu.roll(k00, HEAD_DIM, axis=1)
    k10 = pltpu.roll(k11, HEAD_DIM, axis=1)
    return [[k00.astype(BF16), k01.astype(BF16)], [k10.astype(BF16), k11.astype(BF16)]]


def _attn_valid(n):
    i = lax.broadcasted_iota(jnp.int32, (WINDOW, 2 * WINDOW), 0)
    j = lax.broadcasted_iota(jnp.int32, (WINDOW, 2 * WINDOW), 1)
    return (j > i) & (j <= i + WINDOW) & (n * WINDOW + j >= WINDOW)


def _attn_probs(qp, kvar, valid, sk):
    s = _dot_nt(qp, kvar) * ATTN_SCALE
    s = jnp.where(valid, s, NEG)
    m = jnp.maximum(jnp.max(s, axis=1, keepdims=True), sk)
    pe = jnp.exp(s - m)
    es = jnp.exp(sk - m)
    den = jnp.sum(pe, axis=1, keepdims=True) + es
    inv = 1.0 / den
    return pe * inv, es * inv


def _sink(sinks, r):
    lane = lax.broadcasted_iota(jnp.int32, sinks.shape, 1)
    return jnp.sum(jnp.where(lane == r, sinks, 0.0), axis=1, keepdims=True)


def _attn_fwd(proj, kpad, vpad, sinks, og, name):
    T = proj.shape[0]
    nb = T // WINDOW

    def body(q_ref, k_ref, v_ref, s_ref, og_ref, y_ref, o_ref):
        n = pl.program_id(0)
        start = pl.multiple_of(n * WINDOW, WINDOW)
        kv = _kv_variants(k_ref[pl.ds(start, 2 * WINDOW), :])
        vv = _kv_variants(v_ref[pl.ds(start, 2 * WINDOW), :])
        valid = _attn_valid(n)
        sinks_v = s_ref[...]
        ssq = jnp.zeros((WINDOW, 1), F32)
        for p in range(N_PAIRS):
            j = p // PAIRS_PER_KV
            qp = q_ref[:, p * 128:(p + 1) * 128].astype(BF16)
            o_pair = jnp.zeros((WINDOW, 128), F32)
            for par in range(2):
                pn, _ = _attn_probs(qp, kv[j][par], valid, _sink(sinks_v, 2 * p + par))
                o_pair = o_pair + _dot_nn(pn.astype(BF16), vv[j][par])
            o_ref[:, p * 128:(p + 1) * 128] = o_pair
            ssq = ssq + jnp.sum(o_pair * o_pair, axis=1, keepdims=True)
        rn = lax.rsqrt(ssq * (1.0 / ATTN_W) + EPS)
        y_ref[...] = (o_ref[...] * rn * og_ref[...]).astype(BF16)

    full_kv = pl.BlockSpec((T + WINDOW, KV_W), lambda n: (0, 0))
    return pl.pallas_call(
        body, grid=(nb,),
        in_specs=[pl.BlockSpec((WINDOW, ATTN_W), lambda n: (n, OFF_Q // ATTN_W)), full_kv, full_kv,
                  pl.BlockSpec((1, N_HEADS), lambda n: (0, 0)), pl.BlockSpec((1, ATTN_W), lambda n: (0, 0))],
        out_specs=[pl.BlockSpec((WINDOW, ATTN_W), lambda n: (n, 0)), pl.BlockSpec((WINDOW, ATTN_W), lambda n: (n, 0))],
        out_shape=[jax.ShapeDtypeStruct((T, ATTN_W), BF16), jax.ShapeDtypeStruct((T, ATTN_W), F32)],
        name=name, compiler_params=_cparams(("parallel",)),
    )(proj, kpad, vpad, sinks, og)


def _attn_bwd(proj, kpad, vpad, sinks, og, o, dy, name):
    T = proj.shape[0]
    nb = T // WINDOW

    def body(q_ref, k_ref, v_ref, s_ref, og_ref, o_ref, dy_ref, dq_ref, dk_ref, dv_ref, ds_ref, dog_ref):
        n = pl.program_id(0)

        @pl.when(n == 0)
        def _():
            dk_ref[...] = jnp.zeros_like(dk_ref)
            dv_ref[...] = jnp.zeros_like(dv_ref)
            ds_ref[...] = jnp.zeros_like(ds_ref)
            dog_ref[...] = jnp.zeros_like(dog_ref)

        start = pl.multiple_of(n * WINDOW, WINDOW)
        kv = _kv_variants(k_ref[pl.ds(start, 2 * WINDOW), :])
        vv = _kv_variants(v_ref[pl.ds(start, 2 * WINDOW), :])
        valid = _attn_valid(n)
        sinks_v = s_ref[...]
        of = o_ref[...]
        rn = lax.rsqrt(jnp.mean(of * of, axis=-1, keepdims=True) + EPS)
        oh = of * rn
        dyf = dy_ref[...]
        dog_ref[...] += jnp.sum(dyf * oh, axis=0, keepdims=True)
        doh = dyf * og_ref[...]
        do = rn * (doh - oh * jnp.mean(doh * oh, axis=-1, keepdims=True))
        lane = lax.broadcasted_iota(jnp.int32, (1, 128), 1)
        lane16 = lax.broadcasted_iota(jnp.int32, (1, N_HEADS), 1)
        dk_acc = [[jnp.zeros((2 * WINDOW, 128), F32) for _ in range(2)] for _ in range(2)]
        dv_acc = [[jnp.zeros((2 * WINDOW, 128), F32) for _ in range(2)] for _ in range(2)]
        dsink = jnp.zeros((1, N_HEADS), F32)
        for p in range(N_PAIRS):
            j = p // PAIRS_PER_KV
            qp = q_ref[:, p * 128:(p + 1) * 128].astype(BF16)
            do_p = do[:, p * 128:(p + 1) * 128]
            o_p = of[:, p * 128:(p + 1) * 128]
            do_b = do_p.astype(BF16)
            prod = do_p * o_p
            dq_pair = jnp.zeros((WINDOW, 128), F32)
            for par in range(2):
                r = 2 * p + par
                half = (lane < HEAD_DIM) if par == 0 else (lane >= HEAD_DIM)
                pn, ps = _attn_probs(qp, kv[j][par], valid, _sink(sinks_v, r))
                delta = jnp.sum(jnp.where(half, prod, 0.0), axis=1, keepdims=True)
                dP = _dot_nt(do_b, vv[j][par])
                dS = pn * (dP - delta)
                dsink = dsink + jnp.where(lane16 == r, -jnp.sum(ps * delta, axis=0, keepdims=True), 0.0)
                dSb = (dS * ATTN_SCALE).astype(BF16)
                dq_pair = dq_pair + _dot_nn(dSb, kv[j][par])
                dk_acc[j][par] = dk_acc[j][par] + _dot_tn(dSb, jnp.where(half, qp, jnp.zeros_like(qp)))
                dv_acc[j][par] = dv_acc[j][par] + _dot_tn(pn.astype(BF16), jnp.where(half, do_b, jnp.zeros_like(do_b)))
            dq_ref[:, p * 128:(p + 1) * 128] = dq_pair.astype(BF16)
        dkk = (dk_acc[0][0] + pltpu.roll(dk_acc[0][1], HEAD_DIM, axis=1)
               + dk_acc[1][1] + pltpu.roll(dk_acc[1][0], HEAD_DIM, axis=1))
        dvv = (dv_acc[0][0] + pltpu.roll(dv_acc[0][1], HEAD_DIM, axis=1)
               + dv_acc[1][1] + pltpu.roll(dv_acc[1][0], HEAD_DIM, axis=1))
        dk_ref[pl.ds(start, 2 * WINDOW), :] += dkk
        dv_ref[pl.ds(start, 2 * WINDOW), :] += dvv
        ds_ref[...] += dsink

    full_kv = pl.BlockSpec((T + WINDOW, KV_W), lambda n: (0, 0))
    blk = pl.BlockSpec((WINDOW, ATTN_W), lambda n: (n, 0))
    return pl.pallas_call(
        body, grid=(nb,),
        in_specs=[pl.BlockSpec((WINDOW, ATTN_W), lambda n: (n, OFF_Q // ATTN_W)), full_kv, full_kv,
                  pl.BlockSpec((1, N_HEADS), lambda n: (0, 0)), pl.BlockSpec((1, ATTN_W), lambda n: (0, 0)),
                  blk, pl.BlockSpec((WINDOW, ATTN_W), lambda n: (n, 1))],
        out_specs=[blk, full_kv, full_kv, pl.BlockSpec((1, N_HEADS), lambda n: (0, 0)),
                   pl.BlockSpec((1, ATTN_W), lambda n: (0, 0))],
        out_shape=[jax.ShapeDtypeStruct((T, ATTN_W), BF16), jax.ShapeDtypeStruct((T + WINDOW, KV_W), F32),
                   jax.ShapeDtypeStruct((T + WINDOW, KV_W), F32), jax.ShapeDtypeStruct((1, N_HEADS), F32),
                   jax.ShapeDtypeStruct((1, ATTN_W), F32)],
        name=name, compiler_params=_cparams(("arbitrary",)),
    )(proj, kpad, vpad, sinks, og, o, dy)


ANY = pl.BlockSpec(memory_space=pl.ANY)


def _coords():
    return lax.axis_index("x"), lax.axis_index("y"), lax.axis_index("c")


def _all_gather(arrs, name):
    n = len(arrs)

    def body(*refs):
        ins, outs = refs[:n], refs[n:2 * n]
        send_sems, recv_sems, local_sems = refs[2 * n:]
        x, y, c = _coords()
        me = 4 * x + 2 * y + c
        sibling = (x, y, 1 - c)
        chips = [(1 - x, y), (x, 1 - y), (1 - x, 1 - y)]

        def copy(a, k, block, to, src=None):
            dst = outs[a].at[block]
            return pltpu.make_async_remote_copy(
                src_ref=dst if src is None else src, dst_ref=dst, send_sem=send_sems.at[a, k],
                recv_sem=recv_sems.at[a, k], device_id=to, device_id_type=MESH)

        mine = [pltpu.make_async_copy(ins[a], outs[a].at[me], local_sems.at[a]) for a in range(n)]
        for cp in mine:
            cp.start()
        first = []
        for a in range(n):
            first.append(copy(a, 0, me, sibling, src=ins[a]))
            for j, chip in enumerate(chips):
                first.append(copy(a, 1 + j, me, (*chip, c), src=ins[a]))
        for cp in first:
            cp.start()
        passed = []
        for j, (px, py) in enumerate(chips):
            blk = 4 * px + 2 * py + c
            for a in range(n):
                copy(a, 1 + j, blk, sibling).wait_recv()
                fwd = copy(a, 4 + j, blk, sibling)
                fwd.start()
                passed.append(fwd)
        for a in range(n):
            copy(a, 0, 4 * x + 2 * y + (1 - c), sibling).wait_recv()
            for j, (px, py) in enumerate(chips):
                copy(a, 4 + j, 4 * px + 2 * py + (1 - c), sibling).wait_recv()
        for cp in first + passed:
            cp.wait_send()
        for cp in mine:
            cp.wait()

    return pl.pallas_call(
        body, in_specs=[ANY] * n, out_specs=[ANY] * n,
        out_shape=[jax.ShapeDtypeStruct((N_DEV,) + a.shape, a.dtype) for a in arrs],
        scratch_shapes=[pltpu.SemaphoreType.DMA((n, 7)), pltpu.SemaphoreType.DMA((n, 7)),
                        pltpu.SemaphoreType.DMA((n,))],
        name=name,
    )(*arrs)


def _exchange_pair(arrs, name):
    n = len(arrs)

    def body(*refs):
        ins, outs = refs[:n], refs[n:2 * n]
        send_sems, recv_sems = refs[2 * n:]
        x, y, c = _coords()
        cps = []
        for a in range(n):
            for q in range(4):
                cps.append(pltpu.make_async_remote_copy(
                    src_ref=ins[a].at[2 * q + (1 - c)], dst_ref=outs[a].at[q], send_sem=send_sems.at[a, q],
                    recv_sem=recv_sems.at[a, q], device_id=(x, y, 1 - c), device_id_type=MESH))
        for cp in cps:
            cp.start()
        for cp in cps:
            cp.wait()

    return pl.pallas_call(
        body, in_specs=[ANY] * n, out_specs=[ANY] * n,
        out_shape=[jax.ShapeDtypeStruct((4,) + a.shape[1:], a.dtype) for a in arrs],
        scratch_shapes=[pltpu.SemaphoreType.DMA((n, 4)), pltpu.SemaphoreType.DMA((n, 4))],
        name=name,
    )(*arrs)


def _exchange_chips(arrs, name):
    n = len(arrs)

    def body(*refs):
        ins, outs = refs[:n], refs[n:2 * n]
        send_sems, recv_sems = refs[2 * n:]
        x, y, c = _coords()
        chips = [(1 - x, y), (x, 1 - y), (1 - x, 1 - y)]
        cps = []
        for a in range(n):
            for k, (tx, ty) in enumerate(chips):
                cps.append(pltpu.make_async_remote_copy(
                    src_ref=ins[a].at[2 * tx + ty], dst_ref=outs[a].at[k], send_sem=send_sems.at[a, k],
                    recv_sem=recv_sems.at[a, k], device_id=(tx, ty, c), device_id_type=MESH))
        for cp in cps:
            cp.start()
        for cp in cps:
            cp.wait()

    return pl.pallas_call(
        body, in_specs=[ANY] * n, out_specs=[ANY] * n,
        out_shape=[jax.ShapeDtypeStruct((3,) + a.shape[1:], a.dtype) for a in arrs],
        scratch_shapes=[pltpu.SemaphoreType.DMA((n, 3)), pltpu.SemaphoreType.DMA((n, 3))],
        name=name,
    )(*arrs)


HBM = pl.BlockSpec(memory_space=pltpu.HBM)
SEM = pl.BlockSpec(memory_space=pltpu.SEMAPHORE)
EFFECT = pltpu.SideEffectType.DATAFLOW_SIDE_EFFECTING


def _in_hbm(a):
    return pltpu.with_memory_space_constraint(a, pltpu.HBM)


def _remote_start(srcs, lands, plan, n_copies, name, after=None):
    n = len(srcs)
    n_after = 0 if after is None else 1

    def body(*refs):
        src_refs, land_refs = refs[:n], refs[n:2 * n]
        send_sems, recv_sems = refs[2 * n + n_after], refs[2 * n + n_after + 1]
        token = refs[-1]
        x, y, c = _coords()
        for i, (sv, dv, dev) in enumerate(plan(src_refs, land_refs, x, y, c)):
            pltpu.make_async_remote_copy(src_ref=sv, dst_ref=dv, send_sem=send_sems.at[i], recv_sem=recv_sems.at[i],
                                         device_id=dev, device_id_type=MESH).start()
        token[...] = jnp.zeros_like(token)

    bufs = list(srcs) + list(lands)
    outs = pl.pallas_call(
        body, name=name,
        out_shape=(pltpu.SemaphoreType.DMA((n_copies,)), pltpu.SemaphoreType.DMA((n_copies,)),
                   *[pltpu.HBM(b.shape, b.dtype) for b in bufs], jax.ShapeDtypeStruct((8, 128), F32)),
        in_specs=[HBM] * (2 * n) + [ANY] * n_after,
        out_specs=(SEM, SEM, *[HBM] * (2 * n), pl.BlockSpec(memory_space=pltpu.VMEM)),
        input_output_aliases={i: 2 + i for i in range(2 * n)},
        compiler_params=pltpu.CompilerParams(has_side_effects=EFFECT),
    )(*[_in_hbm(b) for b in bufs], *([] if after is None else [after]))
    return outs[0], outs[1], list(outs[2:2 + n]), list(outs[2 + n:2 + 2 * n]), outs[-1]


def _remote_wait(started, after, plan, name):
    send_sems, recv_sems, srcs, lands, _ = started
    n = len(srcs)

    def body(*refs):
        src_refs, land_refs = refs[:n], refs[n:2 * n]
        send_sems, recv_sems = refs[2 * n], refs[2 * n + 1]
        x, y, c = _coords()
        for i, (sv, dv, dev) in enumerate(plan(src_refs, land_refs, x, y, c)):
            cp = pltpu.make_async_remote_copy(src_ref=sv, dst_ref=dv, send_sem=send_sems.at[i],
                                              recv_sem=recv_sems.at[i], device_id=dev, device_id_type=MESH)
            cp.wait_send()
            cp.wait_recv()

    bufs = list(srcs) + list(lands)
    outs = pl.pallas_call(
        body, name=name, out_shape=tuple(pltpu.HBM(b.shape, b.dtype) for b in bufs),
        in_specs=[HBM] * (2 * n) + [SEM, SEM, ANY], out_specs=tuple([HBM] * (2 * n)),
        input_output_aliases={i: i for i in range(2 * n)},
        compiler_params=pltpu.CompilerParams(has_side_effects=EFFECT),
    )(*bufs, send_sems, recv_sems, after)
    return list(outs[:n]), list(outs[n:])


def _gather_plan(src_refs, land_refs, x, y, c):
    me = 4 * x + 2 * y + c
    plan = []
    for s, l in zip(src_refs, land_refs):
        for dev in [(x, y, 1 - c), (1 - x, y, c), (x, 1 - y, c), (1 - x, 1 - y, c)]:
            plan.append((s, l.at[me], dev))
    return plan


def _pair_plan(src_refs, land_refs, x, y, c):
    plan = []
    for s, l in zip(src_refs, land_refs):
        for q in range(4):
            plan.append((s.at[2 * q + (1 - c)], l.at[q], (x, y, 1 - c)))
    return plan


def _pair4_plan(src_refs, land_refs, x, y, c):
    plan = []
    for s, l in zip(src_refs, land_refs):
        for q in range(4):
            plan.append((s.at[q], l.at[q], (x, y, 1 - c)))
    return plan


def _chips_plan(src_refs, land_refs, x, y, c):
    plan = []
    for s, l in zip(src_refs, land_refs):
        for k, (tx, ty) in enumerate([(1 - x, y), (x, 1 - y), (1 - x, 1 - y)]):
            plan.append((s.at[2 * tx + ty], l.at[k], (tx, ty, c)))
    return plan


def _everyone_plan(src_refs, land_refs, x, y, c):
    me = 4 * x + 2 * y + c
    plan = []
    for s, l in zip(src_refs, land_refs):
        for fx, fy, fc in [(0, 0, 1), (1, 0, 0), (1, 0, 1), (0, 1, 0), (0, 1, 1), (1, 1, 0), (1, 1, 1)]:
            dev = ((1 - x) if fx else x, (1 - y) if fy else y, (1 - c) if fc else c)
            plan.append((s, l.at[me], dev))
    return plan


def _gather_finish(gathered, name):
    n = len(gathered)

    def body(*refs):
        outs = refs[n:2 * n]
        send_sems, recv_sems = refs[2 * n:]
        x, y, c = _coords()
        cps = []
        for a in range(n):
            for j, (px, py) in enumerate([(1 - x, y), (x, 1 - y), (1 - x, 1 - y)]):
                blk = outs[a].at[4 * px + 2 * py + c]
                got = outs[a].at[4 * px + 2 * py + (1 - c)]
                cps.append((pltpu.make_async_remote_copy(
                    src_ref=blk, dst_ref=blk, send_sem=send_sems.at[a, j], recv_sem=recv_sems.at[a, j],
                    device_id=(x, y, 1 - c), device_id_type=MESH), pltpu.make_async_remote_copy(
                    src_ref=got, dst_ref=got, send_sem=send_sems.at[a, j], recv_sem=recv_sems.at[a, j],
                    device_id=(x, y, 1 - c), device_id_type=MESH)))
        for cp, _ in cps:
            cp.start()
        for cp, arrival in cps:
            cp.wait_send()
            arrival.wait_recv()

    return pl.pallas_call(
        body, in_specs=[ANY] * n, out_specs=[ANY] * n,
        out_shape=[jax.ShapeDtypeStruct(g.shape, g.dtype) for g in gathered],
        input_output_aliases={a: a for a in range(n)},
        scratch_shapes=[pltpu.SemaphoreType.DMA((n, 3)), pltpu.SemaphoreType.DMA((n, 3))],
        name=name,
    )(*gathered)


def _pair_add(g8, r1, csel, tr, name):
    _, R, C = r1.shape
    g4 = g8.reshape(4, 2, R, C)

    def body(c_ref, g_ref, r_ref, o_ref):
        o_ref[...] = (g_ref[...].astype(F32) + r_ref[...].astype(F32)).astype(BF16)

    return pl.pallas_call(
        body,
        grid_spec=pltpu.PrefetchScalarGridSpec(
            num_scalar_prefetch=1, grid=(4, R // tr),
            in_specs=[pl.BlockSpec((None, None, tr, C), lambda q, i, cs: (q, cs[0], i, 0)),
                      pl.BlockSpec((None, tr, C), lambda q, i, cs: (q, i, 0))],
            out_specs=pl.BlockSpec((None, tr, C), lambda q, i, cs: (q, i, 0))),
        out_shape=jax.ShapeDtypeStruct((4, R, C), BF16), name=name,
        compiler_params=_cparams(("parallel", "parallel")),
    )(csel, g4, r1)


def _adamw_math(w, g, m, v):
    m = ADAM_B1 * m + (1.0 - ADAM_B1) * g
    v = ADAM_B2 * v + (1.0 - ADAM_B2) * (g * g)
    m_hat = m / (1.0 - ADAM_B1 ** ADAM_STEP)
    v_hat = v / (1.0 - ADAM_B2 ** ADAM_STEP)
    delta = -ADAM_LR * (m_hat / (jnp.sqrt(v_hat) + ADAM_EPS) + ADAM_WD * w)
    return delta, m, v


def _adamw_big(w, m, v, p4, r3, qsel, tr, name):
    R, C = w.shape

    def body(q_ref, w_ref, m_ref, v_ref, p_ref, r_ref, g_out, d_out, m_out, v_out):
        g = p_ref[...].astype(F32) + r_ref[0].astype(F32) + r_ref[1].astype(F32) + r_ref[2].astype(F32)
        d, mn, vn = _adamw_math(w_ref[...], g, m_ref[...], v_ref[...])
        g_out[...] = g
        d_out[...] = d
        m_out[...] = mn
        v_out[...] = vn

    blk = pl.BlockSpec((tr, C), lambda i, qs: (i, 0))
    return pl.pallas_call(
        body,
        grid_spec=pltpu.PrefetchScalarGridSpec(
            num_scalar_prefetch=1, grid=(R // tr,),
            in_specs=[blk, blk, blk, pl.BlockSpec((None, tr, C), lambda i, qs: (qs[0], i, 0)),
                      pl.BlockSpec((3, tr, C), lambda i, qs: (0, i, 0))],
            out_specs=[blk, blk, blk, blk]),
        out_shape=[jax.ShapeDtypeStruct((R, C), F32)] * 4, name=name,
        compiler_params=_cparams(("parallel",)),
    )(qsel, w, m, v, p4, r3)


def _small_sum(parts, name):
    def body(p_ref, o_ref):
        acc = p_ref[0]
        for d in range(1, N_DEV):
            acc = acc + p_ref[d]
        o_ref[...] = acc

    return pl.pallas_call(
        body, out_shape=jax.ShapeDtypeStruct(parts.shape[1:], F32), name=name,
        compiler_params=_cparams(),
    )(parts)


def _adamw_small(w, g, m, v, name):
    def body(w_ref, g_ref, m_ref, v_ref, d_out, m_out, v_out):
        d, mn, vn = _adamw_math(w_ref[...], g_ref[...], m_ref[...], v_ref[...])
        d_out[...] = d
        m_out[...] = mn
        v_out[...] = vn

    return pl.pallas_call(
        body, out_shape=[jax.ShapeDtypeStruct(w.shape, F32)] * 3, name=name, compiler_params=_cparams(),
    )(w, g, m, v)


def _row(*pieces):
    r = jnp.concatenate([p.reshape(1, -1) for p in pieces], axis=1)
    return jnp.pad(r, ((0, 0), (0, D_MODEL - r.shape[1])))


def _pack_small(mix, convb, ssmg, attng, mlpg, fing, convw, dtb, alog, dsk, sinks, extra=None):
    last = [dtb, alog, dsk, sinks] + ([extra] if extra is not None else [])
    rows = [_row(mix), _row(convb), _row(ssmg, attng), _row(mlpg), _row(fing),
            jnp.pad(convw, ((0, 0), (0, D_MODEL - convw.shape[1]))), _row(*last)]
    packed = jnp.concatenate(rows, axis=0)
    return jnp.pad(packed, ((0, SMALL_ROWS - packed.shape[0]), (0, 0)))


def _unpack_small(p, conv_n):
    return dict(
        mix_norm_g=p[0:1, :], conv_b=p[1:2, :], ssm_norm_g=p[2:3, :D_INNER], attn_out_norm_g=p[2:3, D_INNER:],
        mlp_norm_g=p[3:4, :], final_norm_g=p[4, :], conv_w=p[5:9, :conv_n][None],
        dt_bias=p[9:10, 0:16], A_log=p[9:10, 16:32], D_skip=p[9:10, 32:48], attn_sinks=p[9:10, 48:64])


SMALL_NAMES = ["mix_norm_g", "conv_w", "conv_b", "dt_bias", "A_log", "D_skip", "ssm_norm_g", "attn_sinks",
               "attn_out_norm_g", "mlp_norm_g", "final_norm_g"]
WEIGHT_ORDER = ["mix_norm_g", "w_in", "conv_w", "conv_b", "dt_bias", "A_log", "D_skip", "ssm_norm_g", "attn_sinks",
                "attn_out_norm_g", "w_out", "mlp_norm_g", "w_up", "w_down", "final_norm_g"]


def _to_my_columns(w_nat):
    pad = jnp.zeros((w_nat.shape[0], NP - IN_PROJ), w_nat.dtype)
    return jnp.concatenate([w_nat[:, :NAT_DT], w_nat[:, NAT_DT + N_HEADS:], w_nat[:, NAT_DT:NAT_DT + N_HEADS], pad],
                           axis=1)


def _to_natural_columns(w_my):
    return jnp.concatenate([w_my[:, :NAT_DT], w_my[:, OFF_DT:OFF_DT + N_HEADS], w_my[:, NAT_DT:OFF_DT]], axis=1)


SLAB = 1024


def _grad_w_up(h2, du, name, sel=None, add=None, after=None):
    T, D = h2.shape
    if sel is None:
        pick, n_slab, pre = (lambda j, *cs: j), N_DEV, None
    else:
        pre, other = sel
        pick, n_slab = (lambda j, cs: 2 * j + ((1 - cs[0]) if other else cs[0])), 4
    o_spec = pl.BlockSpec((None, SLAB, SLAB), lambda i, j, k, *cs: (j, i, 0))
    return _matmul(
        h2, du, mode="tn", grid=(D // SLAB, n_slab, 1),
        a_spec=pl.BlockSpec((T, SLAB), lambda i, j, k, *cs: (0, i)),
        b_spec=pl.BlockSpec((T, SLAB), lambda i, j, k, *cs: (0, pick(j, *cs))),
        out_shapes=[jax.ShapeDtypeStruct((n_slab, D, SLAB), BF16)], out_specs=[o_spec], tile=(SLAB, SLAB), name=name,
        extras=() if add is None else (add,), extra_specs=() if add is None else (o_spec,),
        epilogue=None if add is None else (lambda acc, r: (acc + r.astype(F32),)), after=after, prefetch=pre)[0]


def _grad_w_down(act, dx3b, name, sel=None, add=None, after=None):
    T, D = dx3b.shape
    if sel is None:
        pick, n_slab, pre = (lambda i, *cs: i), N_DEV, None
    else:
        pre, other = sel
        pick, n_slab = (lambda i, cs: 2 * i + ((1 - cs[0]) if other else cs[0])), 4
    o_spec = pl.BlockSpec((None, SLAB, SLAB), lambda i, j, k, *cs: (i, 0, j))
    return _matmul(
        act, dx3b, mode="tn", grid=(n_slab, D // SLAB, 1),
        a_spec=pl.BlockSpec((T, SLAB), lambda i, j, k, *cs: (0, pick(i, *cs))),
        b_spec=pl.BlockSpec((T, SLAB), lambda i, j, k, *cs: (0, j)),
        out_shapes=[jax.ShapeDtypeStruct((n_slab, SLAB, D), BF16)], out_specs=[o_spec], tile=(SLAB, SLAB), name=name,
        extras=() if add is None else (add,), extra_specs=() if add is None else (o_spec,),
        epilogue=None if add is None else (lambda acc, r: (acc + r.astype(F32),)), after=after, prefetch=pre)[0]


class _FixedWeights:
    def __init__(self, w_in_p, w_out_f, w_up_s, w_down_f, conv_w_f):
        self.w = (w_in_p, w_out_f, w_up_s, w_down_f, conv_w_f)
        self.grads = {}

    def mixer_weights(self, after):
        return self.w[0], self.w[4], None

    def out_weight(self, after):
        return self.w[1]

    def up_weight(self, after):
        return self.w[2]

    def down_weight(self, after):
        return self.w[3]

    def mlp_grads(self, h2, du, act, dx3b):
        self.grads.update(w_up=_grad_w_up(h2, du, "grad_w_up"),
                          w_down=_grad_w_down(act, dx3b, "grad_w_down").reshape(D_FF, D_MODEL))
        return None

    def out_grad(self, g_out):
        self.grads.update(w_out=g_out)
        return None

    def in_grad(self, g_in):
        self.grads.update(w_in=g_in)
        return None


def _local_step(x, tgt, p, hooks):
    T = x.shape[0]
    D = D_MODEL
    h1 = _rmsnorm_fwd(x, p["mix_norm_g"], "norm_mix")
    w_in_p, conv_w_f, token = hooks.mixer_weights(h1)
    (proj,) = _mm_simple(h1, w_in_p, mode="nn", M=T, N=NP, K=D, tm=min(T, 1024), tn=1536, tk=D, out_dtype=F32,
                         name="in_proj", after=token)
    xbc = _conv_fwd(proj, conv_w_f, p["conv_b"], "conv_fwd")
    dtT = proj[:, OFF_DT:OFF_DT + N_HEADS].T
    dtbT = p["dt_bias"].T
    alogT = p["A_log"].T
    dfull = jnp.repeat(p["D_skip"], HEAD_DIM, axis=1)
    y_ssm, ypre, hs = _ssd_fwd(xbc, proj, dtT, p["dt_bias"], dtbT, p["A_log"], alogT, dfull, p["ssm_norm_g"],
                               "ssd_fwd")
    kpad = jnp.pad(proj[:, OFF_K:OFF_K + KV_W], ((WINDOW, 0), (0, 0)))
    vpad = jnp.pad(proj[:, OFF_V:OFF_V + KV_W], ((WINDOW, 0), (0, 0)))
    y_att, o_att = _attn_fwd(proj, kpad, vpad, p["attn_sinks"], p["attn_out_norm_g"], "attn_fwd")
    ycat = jnp.concatenate([y_ssm, y_att], axis=1)
    w_out_f = hooks.out_weight(ycat)
    tm = min(T, 1024)
    (x2,) = _mm_simple(ycat, w_out_f, mode="nn", M=T, N=D, K=D, tm=tm, tn=1024, tk=D, out_dtype=F32, name="out_proj",
                       extras=(x,), epilogue=lambda acc, res: (acc + res,))
    h2 = _rmsnorm_fwd(x2, p["mlp_norm_g"], "norm_mlp")
    w_up_s = hooks.up_weight(h2)
    grid = (T // tm, N_DEV, 1)
    u, act = _matmul(
        h2, w_up_s, mode="nn", grid=grid,
        a_spec=pl.BlockSpec((tm, D), lambda i, j, k: (i, 0)),
        b_spec=pl.BlockSpec((None, D, 1024), lambda i, j, k: (j, 0, 0)),
        out_shapes=[jax.ShapeDtypeStruct((T, D_FF), F32), jax.ShapeDtypeStruct((T, D_FF), BF16)],
        out_specs=[pl.BlockSpec((tm, 1024), lambda i, j, k: (i, j))] * 2, tile=(tm, 1024), name="mlp_up",
        epilogue=lambda acc: (acc, jnp.square(jnp.maximum(acc, 0.0))))
    w_down_f = hooks.down_weight(act)
    (x3,) = _mm_simple(act, w_down_f, mode="nn", M=T, N=D, K=D_FF, tm=tm, tn=1024, tk=2048, out_dtype=F32,
                       name="mlp_down", extras=(x2,), epilogue=lambda acc, res: (acc + res,))
    loss_part, d_fin, dx3, dx3b = _final_loss(x3, tgt, p["final_norm_g"].reshape(1, D), "loss_head")
    (du,) = _mm_simple(dx3b, w_down_f, mode="nt", M=T, N=D_FF, K=D, tm=tm, tn=1024, tk=D, out_dtype=BF16,
                       name="mlp_down_bwd", extras=(u,),
                       epilogue=lambda acc, uu: (acc * (2.0 * jnp.maximum(uu, 0.0)),))
    token = hooks.mlp_grads(h2, du, act, dx3b)
    (dh2,) = _matmul(
        du, w_up_s, mode="nt", grid=(T // tm, D // 1024, N_DEV // 2),
        a_spec=pl.BlockSpec((tm, 2048), lambda i, j, k: (i, k)),
        b_spec=pl.BlockSpec((2, 1024, 1024), lambda i, j, k: (k, j, 0)),
        out_shapes=[jax.ShapeDtypeStruct((T, D), F32)],
        out_specs=[pl.BlockSpec((tm, 1024), lambda i, j, k: (i, j))], tile=(tm, 1024), name="mlp_up_bwd",
        after=token, dot_fn=lambda a, b: _dot_nt(a[:, :1024], b[0]) + _dot_nt(a[:, 1024:], b[1]))
    dx2, dx2b, d_mlp = _rmsnorm_bwd(dh2, x2, p["mlp_norm_g"], dx3, "norm_mlp_bwd")
    (g_out,) = _mm_simple(ycat, dx2b, mode="tn", M=D, N=D, K=T, tm=1024, tn=1024, tk=T, out_dtype=BF16,
                          name="grad_w_out")
    token = hooks.out_grad(g_out)
    (dy,) = _mm_simple(dx2b, w_out_f, mode="nt", M=T, N=D, K=D, tm=tm, tn=1024, tk=D, out_dtype=F32,
                       name="out_proj_bwd", after=token)
    dz, dxbc_act, ddt, d_dtb, d_alog, d_dskip, d_ssmg = _ssd_bwd(
        xbc, proj, dtT, p["dt_bias"], dtbT, p["A_log"], alogT, dfull, p["ssm_norm_g"], ypre, hs, dy, "ssd_bwd")
    dq, dkpad, dvpad, d_sinks, d_attng = _attn_bwd(proj, kpad, vpad, p["attn_sinks"], p["attn_out_norm_g"], o_att, dy,
                                                   "attn_bwd")
    dxbc, d_convw, d_convb = _conv_bwd(proj, dxbc_act, conv_w_f, p["conv_b"], "conv_bwd")
    dproj = jnp.concatenate(
        [dz, dxbc, dq, dkpad[WINDOW:].astype(BF16), dvpad[WINDOW:].astype(BF16), ddt.astype(BF16),
         jnp.zeros((T, NP - OFF_DT - 128), BF16)], axis=1)
    (g_in,) = _mm_simple(h1, dproj, mode="tn", M=D, N=NP, K=T, tm=1024, tn=1536, tk=T, out_dtype=BF16,
                         name="grad_w_in")
    token = hooks.in_grad(g_in)
    (dh1,) = _mm_simple(dproj, w_in_p, mode="nt", M=T, N=D, K=NP, tm=tm, tn=1024, tk=2304, out_dtype=F32,
                        name="in_proj_bwd", after=token)
    dx, _, d_mix = _rmsnorm_bwd(dh1, x, p["mix_norm_g"], dx2, "norm_mix_bwd")
    small = _pack_small(d_mix, d_convb, d_ssmg, d_attng, d_mlp, d_fin, d_convw, d_dtb, d_alog, d_dskip, d_sinks,
                        extra=loss_part[:, 0:1])
    return dx, small


def _landing(own, me):
    zone = lax.empty((N_DEV,) + own.shape, own.dtype)
    return lax.dynamic_update_slice(zone, own[None], (me,) + (0,) * own.ndim)


def _gather_end(started, after, plan, name):
    _, lands = _remote_wait(started, after, plan, name + "_wait")
    return _gather_finish(lands, name + "_finish")


class _ShardedWeights:
    def __init__(self, w_in, w_out, conv_w, w_up, w_down, me, csel):
        self.me, self.csel = me, csel
        shards = [w_in.astype(BF16), conv_w]
        self.st_mixer = _remote_start(shards, [_landing(s, me) for s in shards], _gather_plan, 4 * len(shards),
                                      "gather_start_mixer")
        self.start_token = self.st_mixer[4]
        self.later = {"out": w_out.astype(BF16), "up": w_up.astype(BF16), "down": w_down.astype(BF16)}
        self.later_lands = {k: _landing(v, me) for k, v in self.later.items()}
        self.st_later = {}
        self.reduces = {}
        self.pair_mlp = None

    def mixer_weights(self, after):
        g_in, g_conv = _gather_end(self.st_mixer, after, _gather_plan, "gather_mixer")
        order = g_conv
        for k in ["out", "up", "down"]:
            self.st_later[k] = _remote_start([self.later[k]], [self.later_lands[k]], _gather_plan, 4,
                                             f"gather_start_{k}", after=order)
            order = self.st_later[k][4]
        per = IN_PROJ // N_DEV
        k, off = NAT_DT // per, NAT_DT % per
        pieces = [g_in[i] for i in range(k)] + [g_in[k][:, :off], g_in[k][:, off + N_HEADS:]]
        pieces += [g_in[i] for i in range(k + 1, N_DEV)]
        pieces += [g_in[k][:, off:off + N_HEADS], jnp.zeros((D_MODEL, NP - IN_PROJ), BF16)]
        w_in_p = jnp.concatenate(pieces, axis=1)
        conv_w_f = jnp.concatenate([g_conv[i] for i in range(N_DEV)], axis=1)
        return w_in_p, conv_w_f, order

    def out_weight(self, after):
        return _gather_end(self.st_later["out"], after, _gather_plan, "gather_out")[0].reshape(D_MODEL, D_MODEL)

    def up_weight(self, after):
        return _gather_end(self.st_later["up"], after, _gather_plan, "gather_up")[0]

    def down_weight(self, after):
        return _gather_end(self.st_later["down"], after, _gather_plan, "gather_down")[0].reshape(D_FF, D_MODEL)

    def _chips_start(self, slabs, from_sibling, rows, tag):
        sums = [_pair_add(s, r, self.csel, tr, f"pair_add_{tag}_{i}")
                for i, (s, r, tr) in enumerate(zip(slabs, from_sibling, rows))]
        lands = [lax.empty((3,) + s.shape[1:], s.dtype) for s in sums]
        self.reduces[tag] = _remote_start(sums, lands, _chips_plan, 3 * len(sums), f"reduce_start_{tag}")
        return self.reduces[tag][4]

    def mlp_grads(self, h2, du, act, dx3b):
        def send(part, tag, after):
            st = _remote_start([part], [lax.empty(part.shape, part.dtype)], _pair4_plan, 4,
                               f"reduce_pair_start_{tag}", after=after)
            return st

        def received(st, after, tag):
            return _remote_wait(st, after, _pair4_plan, f"reduce_pair_wait_{tag}")[1][0]

        def to_chips(sums, tag):
            self.reduces[tag] = _remote_start([sums], [lax.empty((3,) + sums.shape[1:], sums.dtype)], _chips_plan, 3,
                                              f"reduce_start_{tag}")
            return self.reduces[tag][4]

        up_send = _grad_w_up(h2, du, "grad_w_up_send", sel=(self.csel, True))
        st_up = send(up_send, "up", None)
        down_send = _grad_w_down(act, dx3b, "grad_w_down_send", sel=(self.csel, True), after=st_up[4])
        st_down = send(down_send, "down", None)
        up_sum = _grad_w_up(h2, du, "grad_w_up_keep", sel=(self.csel, False), add=received(st_up, down_send, "up"),
                            after=st_down[4])
        token = to_chips(up_sum, "up")
        down_sum = _grad_w_down(act, dx3b, "grad_w_down_keep", sel=(self.csel, False),
                                add=received(st_down, up_sum, "down"), after=token)
        return to_chips(down_sum, "down")

    def out_grad(self, g_out):
        slabs = [g_out.reshape(N_DEV, D_MODEL // N_DEV, D_MODEL)]
        return self._chips_start(slabs, _exchange_pair(slabs, "reduce_pair_out"), [256], "out")

    def in_grad(self, g_in):
        per = IN_PROJ // N_DEV
        k, off = NAT_DT // per, NAT_DT % per

        def slab(i):
            if i < k:
                return g_in[:, i * per:(i + 1) * per]
            if i == k:
                return jnp.concatenate([g_in[:, k * per:NAT_DT], g_in[:, OFF_DT:OFF_DT + N_HEADS],
                                        g_in[:, NAT_DT:NAT_DT + per - off - N_HEADS]], axis=1)
            return g_in[:, i * per - N_HEADS:(i + 1) * per - N_HEADS]

        slabs = [jnp.stack([slab(i) for i in range(N_DEV)])]
        return self._chips_start(slabs, _exchange_pair(slabs, "reduce_pair_in"), [256], "in")

    def small_start(self, small):
        self.st_small = _remote_start([small], [_landing(small, self.me)], _everyone_plan, N_DEV - 1, "gather_start_small")

    def small_end(self, after):
        return _remote_wait(self.st_small, after, _everyone_plan, "gather_small_wait")[1][0]

    def reduce_end(self, tag, after):
        return _remote_wait(self.reduces[tag], after, _chips_plan, f"reduce_wait_{tag}")


def kernel(x, mix_norm_g, w_in, conv_w, conv_b, dt_bias, A_log, D_skip, ssm_norm_g, attn_sinks, attn_out_norm_g, w_out, mlp_norm_g, w_up, w_down, final_norm_g, loss_target, m_mix_norm_g, m_w_in, m_conv_w, m_conv_b, m_dt_bias, m_A_log, m_D_skip, m_ssm_norm_g, m_attn_sinks, m_attn_out_norm_g, m_w_out, m_mlp_norm_g, m_w_up, m_w_down, m_final_norm_g, v_mix_norm_g, v_w_in, v_conv_w, v_conv_b, v_dt_bias, v_A_log, v_D_skip, v_ssm_norm_g, v_attn_sinks, v_attn_out_norm_g, v_w_out, v_mlp_norm_g, v_w_up, v_w_down, v_final_norm_g):
    xi, yi, ci = _coords()
    me = 4 * xi + 2 * yi + ci
    csel = jnp.reshape(ci, (1,)).astype(jnp.int32)
    qsel = jnp.reshape(2 * xi + yi, (1,)).astype(jnp.int32)
    w = dict(mix_norm_g=mix_norm_g, conv_b=conv_b, dt_bias=dt_bias, A_log=A_log, D_skip=D_skip,
             ssm_norm_g=ssm_norm_g, attn_sinks=attn_sinks, attn_out_norm_g=attn_out_norm_g, mlp_norm_g=mlp_norm_g,
             final_norm_g=final_norm_g)
    hooks = _ShardedWeights(w_in[0], w_out[0], conv_w[0], w_up[0], w_down[0], me, csel)
    p = dict(w, mix_norm_g=mix_norm_g + hooks.start_token[0:1, 0:1])
    dx, small = _local_step(x[0], loss_target[0], p, hooks)
    hooks.small_start(small)
    big = {}
    after = dx
    for tag, members in [("up", [("w_up", w_up, m_w_up, v_w_up, 512)]),
                         ("down", [("w_down", w_down, m_w_down, v_w_down, 256)]),
                         ("out", [("w_out", w_out, m_w_out, v_w_out, 256)]),
                         ("in", [("w_in", w_in, m_w_in, v_w_in, 256)])]:
        chip_sums, from_chips = hooks.reduce_end(tag, after)
        for i, (name, wt, mt, vt, tr) in enumerate(members):
            g, d, mn, vn = _adamw_big(wt[0], mt[0], vt[0], chip_sums[i], from_chips[i], qsel, tr, f"adamw_{name}")
            big[name] = (g[None], d[None], mn[None], vn[None])
            after = g
    gsum = _small_sum(hooks.small_end(after), "small_sum")
    loss = gsum[9, 64]
    gs = _unpack_small(gsum, CONV_DIM)
    cw = CONV_DIM // N_DEV
    g_conv_shard = lax.dynamic_slice(gsum[5:9, :], (0, me * cw), (CONV_K, cw))

    def pack(s):
        return _pack_small(s["mix_norm_g"], s["conv_b"], s["ssm_norm_g"], s["attn_out_norm_g"], s["mlp_norm_g"],
                           s["final_norm_g"], s["conv_w"][0], s["dt_bias"], s["A_log"], s["D_skip"], s["attn_sinks"])

    wp = pack(dict(w, conv_w=conv_w))
    mp = pack(dict(mix_norm_g=m_mix_norm_g, conv_b=m_conv_b, ssm_norm_g=m_ssm_norm_g,
                   attn_out_norm_g=m_attn_out_norm_g, mlp_norm_g=m_mlp_norm_g, final_norm_g=m_final_norm_g,
                   conv_w=m_conv_w, dt_bias=m_dt_bias, A_log=m_A_log, D_skip=m_D_skip, attn_sinks=m_attn_sinks))
    vp = pack(dict(mix_norm_g=v_mix_norm_g, conv_b=v_conv_b, ssm_norm_g=v_ssm_norm_g,
                   attn_out_norm_g=v_attn_out_norm_g, mlp_norm_g=v_mlp_norm_g, final_norm_g=v_final_norm_g,
                   conv_w=v_conv_w, dt_bias=v_dt_bias, A_log=v_A_log, D_skip=v_D_skip, attn_sinks=v_attn_sinks))
    gp = jnp.concatenate([gsum[0:5], jnp.pad(g_conv_shard, ((0, 0), (0, D_MODEL - cw))), gsum[9:10],
                          jnp.zeros((SMALL_ROWS - 10, D_MODEL), F32)], axis=0)
    dp, mnp, vnp = _adamw_small(wp, gp, mp, vp, "adamw_small")
    grads = dict(gs, conv_w=g_conv_shard[None])
    deltas = _unpack_small(dp, cw)
    new_m = _unpack_small(mnp, cw)
    new_v = _unpack_small(vnp, cw)
    for k, name in enumerate(["w_in", "w_out", "w_up", "w_down"]):
        grads[name], deltas[name], new_m[name], new_v[name] = big[name]
    return (loss, dx[None], *[grads[n] for n in WEIGHT_ORDER], *[deltas[n] for n in WEIGHT_ORDER],
            *[new_m[n] for n in WEIGHT_ORDER], *[new_v[n] for n in WEIGHT_ORDER])
```

```python
import functools

import jax
import jax.numpy as jnp
from jax import lax
from jax.experimental import pallas as pl
from jax.experimental.pallas import tpu as pltpu

F32 = jnp.float32
BF16 = jnp.bfloat16
HI = lax.Precision.HIGHEST
MESH = pl.DeviceIdType.MESH

EPS = 1e-5
D_MODEL = 2048
D_INNER = 1024
N_HEADS = 16
HEAD_DIM = 64
N_GROUPS = 4
D_STATE = 128
CHUNK = 128
CONV_K = 4
CONV_DIM = 2048
ATTN_W = 1024
KV_W = 128
WINDOW = 128
D_FF = 8192
IN_PROJ = 4368
N_DEV = 8
NP = 4608
OFF_Z, OFF_X, OFF_B, OFF_C, OFF_Q, OFF_K, OFF_V, OFF_DT = 0, 1024, 2048, 2560, 3072, 4096, 4224, 4352
NAT_DT = 3072

ADAM_LR = 0.001
ADAM_B1 = 0.9
ADAM_B2 = 0.999
ADAM_EPS = 1e-08
ADAM_WD = 0.01
ADAM_STEP = 10

VMEM_LIMIT = 52 * 1024 * 1024
SMALL_ROWS = 16
NEG = -1e30


def _cparams(sem=None):
    return pltpu.CompilerParams(dimension_semantics=sem, vmem_limit_bytes=VMEM_LIMIT)


def _split3(v):
    hi = v.astype(BF16)
    rest = v - hi.astype(F32)
    mid = rest.astype(BF16)
    return hi, mid, (rest - mid.astype(F32)).astype(BF16)


def _hdot(a, b, data):
    if data == "a":
        sel = b.astype(BF16)
        return sum(_dot_nn(part, sel) for part in _split3(a))
    sel = a.astype(BF16)
    return sum(_dot_nn(sel, part) for part in _split3(b))


def _dot_nn(a, b):
    return lax.dot_general(a, b, (((1,), (0,)), ((), ())), preferred_element_type=F32)


def _dot_nt(a, b):
    return lax.dot_general(a, b, (((1,), (1,)), ((), ())), preferred_element_type=F32)


def _dot_tn(a, b):
    return lax.dot_general(a, b, (((0,), (0,)), ((), ())), preferred_element_type=F32)


def _softplus(v):
    return jnp.maximum(v, 0.0) + jnp.log1p(jnp.exp(-jnp.abs(v)))


def _sigmoid(v):
    return 1.0 / (1.0 + jnp.exp(-v))


def _matmul(a, b, *, mode, grid, a_spec, b_spec, out_shapes, out_specs, tile, name,
            extras=(), extra_specs=(), epilogue=None, after=None, dot_fn=None, prefetch=None):
    nk = grid[2]
    n_ex = len(extras)
    n_out = len(out_shapes)
    dot = dot_fn if dot_fn is not None else {"nn": _dot_nn, "nt": _dot_nt, "tn": _dot_tn}[mode]

    def finish(acc, ex_refs, out_refs):
        res = (acc,) if epilogue is None else epilogue(acc, *[e[...] for e in ex_refs])
        for o, r in zip(out_refs, res):
            o[...] = r.astype(o.dtype)

    def body(*refs):
        a_ref, b_ref = refs[0], refs[1]
        ex_refs = refs[2:2 + n_ex]
        out_refs = refs[2 + n_ex:2 + n_ex + n_out]
        part = dot(a_ref[...].astype(BF16), b_ref[...].astype(BF16))
        if nk == 1:
            finish(part, ex_refs, out_refs)
        else:
            acc_ref = refs[-1]
            k = pl.program_id(2)

            @pl.when(k == 0)
            def _():
                acc_ref[...] = part

            @pl.when(k > 0)
            def _():
                acc_ref[...] += part

            @pl.when(k == nk - 1)
            def _():
                finish(acc_ref[...], ex_refs, out_refs)

    scratch = [] if nk == 1 else [pltpu.VMEM(tile, F32)]
    n_pre = 0 if prefetch is None else 1
    tok_specs = [] if after is None else [pl.BlockSpec((8, 128), lambda *_: (0, 0))]
    tok_args = [] if after is None else [after]

    def body_with_token(*refs):
        refs = refs[n_pre:]
        body(*refs[:2 + n_ex], *refs[2 + n_ex + len(tok_args):])

    in_specs = [a_spec, b_spec, *extra_specs, *tok_specs]
    params = _cparams(("parallel", "parallel", "arbitrary"))
    if prefetch is None:
        return pl.pallas_call(
            body_with_token, grid=grid, in_specs=in_specs, out_specs=list(out_specs), out_shape=list(out_shapes),
            scratch_shapes=scratch, name=name, compiler_params=params)(a, b, *extras, *tok_args)
    return pl.pallas_call(
        body_with_token,
        grid_spec=pltpu.PrefetchScalarGridSpec(num_scalar_prefetch=1, grid=grid, in_specs=in_specs,
                                               out_specs=list(out_specs), scratch_shapes=scratch),
        out_shape=list(out_shapes), name=name, compiler_params=params)(prefetch, a, b, *extras, *tok_args)


def _mm_simple(a, b, *, mode, M, N, K, tm, tn, tk, out_dtype, name, extras=(), epilogue=None, n_out=1,
               out_dtypes=None, after=None):
    grid = (M // tm, N // tn, K // tk)
    if mode == "nn":
        a_spec = pl.BlockSpec((tm, tk), lambda i, j, k: (i, k))
        b_spec = pl.BlockSpec((tk, tn), lambda i, j, k: (k, j))
    elif mode == "nt":
        a_spec = pl.BlockSpec((tm, tk), lambda i, j, k: (i, k))
        b_spec = pl.BlockSpec((tn, tk), lambda i, j, k: (j, k))
    else:
        a_spec = pl.BlockSpec((tk, tm), lambda i, j, k: (k, i))
        b_spec = pl.BlockSpec((tk, tn), lambda i, j, k: (k, j))
    o_spec = pl.BlockSpec((tm, tn), lambda i, j, k: (i, j))
    dts = out_dtypes if out_dtypes is not None else [out_dtype] * n_out
    return _matmul(a, b, mode=mode, grid=grid, a_spec=a_spec, b_spec=b_spec,
                   out_shapes=[jax.ShapeDtypeStruct((M, N), d) for d in dts],
                   out_specs=[o_spec] * len(dts), tile=(tm, tn), name=name,
                   extras=extras, extra_specs=[o_spec] * len(extras), epilogue=epilogue, after=after)


ROW_BLOCK = 256


def _rmsnorm_fwd(x, g, name):
    T, D = x.shape

    def body(x_ref, g_ref, o_ref):
        xf = x_ref[...]
        r = lax.rsqrt(jnp.mean(xf * xf, axis=-1, keepdims=True) + EPS)
        o_ref[...] = (xf * r * g_ref[...]).astype(BF16)

    return pl.pallas_call(
        body, grid=(T // ROW_BLOCK,),
        in_specs=[pl.BlockSpec((ROW_BLOCK, D), lambda i: (i, 0)), pl.BlockSpec((1, D), lambda i: (0, 0))],
        out_specs=pl.BlockSpec((ROW_BLOCK, D), lambda i: (i, 0)),
        out_shape=jax.ShapeDtypeStruct((T, D), BF16), name=name, compiler_params=_cparams(("parallel",)),
    )(x, g)


def _rmsnorm_bwd(dh, x, g, dres, name):
    T, D = x.shape

    def body(dh_ref, x_ref, g_ref, dres_ref, dx_ref, dxb_ref, dg_ref):
        i = pl.program_id(0)
        xf = x_ref[...]
        r = lax.rsqrt(jnp.mean(xf * xf, axis=-1, keepdims=True) + EPS)
        xh = xf * r
        d = dh_ref[...]

        @pl.when(i == 0)
        def _():
            dg_ref[...] = jnp.zeros_like(dg_ref)

        dg_ref[...] += jnp.sum(d * xh, axis=0, keepdims=True)
        dxh = d * g_ref[...]
        dx = r * (dxh - xh * jnp.mean(dxh * xh, axis=-1, keepdims=True)) + dres_ref[...]
        dx_ref[...] = dx
        dxb_ref[...] = dx.astype(BF16)

    row = pl.BlockSpec((ROW_BLOCK, D), lambda i: (i, 0))
    vec = pl.BlockSpec((1, D), lambda i: (0, 0))
    return pl.pallas_call(
        body, grid=(T // ROW_BLOCK,), in_specs=[row, row, vec, row], out_specs=[row, row, vec],
        out_shape=[jax.ShapeDtypeStruct((T, D), F32), jax.ShapeDtypeStruct((T, D), BF16),
                   jax.ShapeDtypeStruct((1, D), F32)],
        name=name, compiler_params=_cparams(("arbitrary",)),
    )(dh, x, g, dres)


def _final_loss(x3, tgt, g, name):
    T, D = x3.shape

    def body(x_ref, t_ref, g_ref, loss_ref, dg_ref, dx_ref, dxb_ref):
        i = pl.program_id(0)
        xf = x_ref[...]
        r = lax.rsqrt(jnp.mean(xf * xf, axis=-1, keepdims=True) + EPS)
        xh = xf * r
        gg = g_ref[...]
        err = xh * gg - t_ref[...]

        @pl.when(i == 0)
        def _():
            dg_ref[...] = jnp.zeros_like(dg_ref)
            loss_ref[...] = jnp.zeros_like(loss_ref)

        part = jnp.sum(jnp.sum(err * err, axis=-1, keepdims=True), axis=0, keepdims=True) * (0.5 / D)
        loss_ref[...] += jnp.broadcast_to(part, loss_ref.shape)
        dout = err * (1.0 / D)
        dg_ref[...] += jnp.sum(dout * xh, axis=0, keepdims=True)
        dxh = dout * gg
        dx = r * (dxh - xh * jnp.mean(dxh * xh, axis=-1, keepdims=True))
        dx_ref[...] = dx
        dxb_ref[...] = dx.astype(BF16)

    row = pl.BlockSpec((ROW_BLOCK, D), lambda i: (i, 0))
    vec = pl.BlockSpec((1, D), lambda i: (0, 0))
    return pl.pallas_call(
        body, grid=(T // ROW_BLOCK,), in_specs=[row, row, vec],
        out_specs=[pl.BlockSpec((1, 128), lambda i: (0, 0)), vec, row, row],
        out_shape=[jax.ShapeDtypeStruct((1, 128), F32), jax.ShapeDtypeStruct((1, D), F32),
                   jax.ShapeDtypeStruct((T, D), F32), jax.ShapeDtypeStruct((T, D), BF16)],
        name=name, compiler_params=_cparams(("arbitrary",)),
    )(x3, tgt, g)


CONV_BLOCK = 256


def _conv_apply(u, w, b):
    row = lax.broadcasted_iota(jnp.int32, u.shape, 0)
    acc = b + w[CONV_K - 1:CONV_K, :] * u
    shifted = []
    for j in range(1, CONV_K):
        uj = jnp.where(row >= j, pltpu.roll(u, j, axis=0), 0.0)
        shifted.append(uj)
        acc = acc + w[CONV_K - 1 - j:CONV_K - j, :] * uj
    return acc, shifted


def _conv_fwd(proj, conv_w, conv_b, name):
    T = proj.shape[0]
    cb0 = OFF_X // CONV_BLOCK

    def body(u_ref, w_ref, b_ref, o_ref):
        c, _ = _conv_apply(u_ref[...], w_ref[...], b_ref[...])
        o_ref[...] = c * _sigmoid(c)

    return pl.pallas_call(
        body, grid=(CONV_DIM // CONV_BLOCK,),
        in_specs=[pl.BlockSpec((T, CONV_BLOCK), lambda j: (0, cb0 + j)),
                  pl.BlockSpec((CONV_K, CONV_BLOCK), lambda j: (0, j)),
                  pl.BlockSpec((1, CONV_BLOCK), lambda j: (0, j))],
        out_specs=pl.BlockSpec((T, CONV_BLOCK), lambda j: (0, j)),
        out_shape=jax.ShapeDtypeStruct((T, CONV_DIM), F32), name=name, compiler_params=_cparams(("parallel",)),
    )(proj, conv_w, conv_b)


def _conv_bwd(proj, dact, conv_w, conv_b, name):
    T = proj.shape[0]
    cb0 = OFF_X // CONV_BLOCK

    def body(u_ref, d_ref, w_ref, b_ref, du_ref, dw_ref, db_ref):
        u = u_ref[...]
        w = w_ref[...]
        c, shifted = _conv_apply(u, w, b_ref[...])
        sg = _sigmoid(c)
        dc = d_ref[...] * sg * (1.0 + c * (1.0 - sg))
        row = lax.broadcasted_iota(jnp.int32, u.shape, 0)
        du = w[CONV_K - 1:CONV_K, :] * dc
        dw_ref[CONV_K - 1:CONV_K, :] = jnp.sum(dc * u, axis=0, keepdims=True)
        for j in range(1, CONV_K):
            dcj = jnp.where(row < T - j, pltpu.roll(dc, T - j, axis=0), 0.0)
            du = du + w[CONV_K - 1 - j:CONV_K - j, :] * dcj
            dw_ref[CONV_K - 1 - j:CONV_K - j, :] = jnp.sum(dc * shifted[j - 1], axis=0, keepdims=True)
        db_ref[...] = jnp.sum(dc, axis=0, keepdims=True)
        du_ref[...] = du.astype(BF16)

    return pl.pallas_call(
        body, grid=(CONV_DIM // CONV_BLOCK,),
        in_specs=[pl.BlockSpec((T, CONV_BLOCK), lambda j: (0, cb0 + j)),
                  pl.BlockSpec((T, CONV_BLOCK), lambda j: (0, j)),
                  pl.BlockSpec((CONV_K, CONV_BLOCK), lambda j: (0, j)),
                  pl.BlockSpec((1, CONV_BLOCK), lambda j: (0, j))],
        out_specs=[pl.BlockSpec((T, CONV_BLOCK), lambda j: (0, j)),
                   pl.BlockSpec((CONV_K, CONV_BLOCK), lambda j: (0, j)),
                   pl.BlockSpec((1, CONV_BLOCK), lambda j: (0, j))],
        out_shape=[jax.ShapeDtypeStruct((T, CONV_DIM), BF16), jax.ShapeDtypeStruct((CONV_K, CONV_DIM), F32),
                   jax.ShapeDtypeStruct((1, CONV_DIM), F32)],
        name=name, compiler_params=_cparams(("parallel",)),
    )(proj, dact, conv_w, conv_b)


GROUP_W = D_INNER // N_GROUPS
HEADS_PER_GROUP = N_HEADS // N_GROUPS


def _expand_mat():
    h = lax.broadcasted_iota(jnp.int32, (N_HEADS, D_INNER), 0)
    j = lax.broadcasted_iota(jnp.int32, (N_HEADS, D_INNER), 1)
    return (j // HEAD_DIM == h).astype(F32)


def _reduce_mat(g):
    j = lax.broadcasted_iota(jnp.int32, (GROUP_W, N_HEADS), 0)
    h = lax.broadcasted_iota(jnp.int32, (GROUP_W, N_HEADS), 1)
    return (g * HEADS_PER_GROUP + j // HEAD_DIM == h).astype(F32)


def _col16(v, h):
    lane = lax.broadcasted_iota(jnp.int32, v.shape, 1)
    return jnp.sum(jnp.where(lane == h, v, 0.0), axis=1, keepdims=True)


def _ssd_pre(dt_raw, dtT_raw, dtb, dtbT, alog, alogT):
    Q = CHUNK
    xdt = dt_raw + dtb
    dt = _softplus(xdt)
    dtT = _softplus(dtT_raw + dtbT)
    A = -jnp.exp(alog)
    AT = -jnp.exp(alogT)
    row = lax.broadcasted_iota(jnp.int32, (Q, Q), 0)
    col = lax.broadcasted_iota(jnp.int32, (Q, Q), 1)
    tril = (row >= col).astype(F32)
    triu = (row <= col).astype(F32)
    cs = _hdot(tril, dt * A, "b")
    csT = _hdot(dtT * AT, triu, "a")
    return xdt, dt, A, cs, csT, row >= col, triu


def _decay_matrix(cs, csT, h, causal):
    seg = _col16(cs, h) - csT[h:h + 1, :]
    return jnp.where(causal, jnp.exp(jnp.minimum(seg, 0.0)), 0.0)


def _ssd_in_specs(nc, rev):
    def cidx(c):
        return (nc - 1 - c) if rev else c

    return [
        pl.BlockSpec((CHUNK, D_INNER), lambda c: (cidx(c), 0)),
        pl.BlockSpec((CHUNK, 512), lambda c: (cidx(c), 2)),
        pl.BlockSpec((CHUNK, 512), lambda c: (cidx(c), 3)),
        pl.BlockSpec((CHUNK, D_INNER), lambda c: (cidx(c), 0)),
        pl.BlockSpec((CHUNK, 128), lambda c: (cidx(c), OFF_DT // 128)),
        pl.BlockSpec((N_HEADS, CHUNK), lambda c: (0, cidx(c))),
        pl.BlockSpec((1, N_HEADS), lambda c: (0, 0)),
        pl.BlockSpec((N_HEADS, 1), lambda c: (0, 0)),
        pl.BlockSpec((1, N_HEADS), lambda c: (0, 0)),
        pl.BlockSpec((N_HEADS, 1), lambda c: (0, 0)),
        pl.BlockSpec((1, D_INNER), lambda c: (0, 0)),
        pl.BlockSpec((1, D_INNER), lambda c: (0, 0)),
    ]


def _ssd_fwd(xbc, proj, dtT, dtb, dtbT, alog, alogT, dfull, ng, name):
    T = xbc.shape[0]
    nc = T // CHUNK
    Q = CHUNK

    def body(xs_ref, B_ref, C_ref, z_ref, dt_ref, dtT_ref, dtb_ref, dtbT_ref, al_ref, alT_ref, df_ref, ng_ref,
             y_ref, ypre_ref, hs_ref, h_scr):
        c = pl.program_id(0)

        @pl.when(c == 0)
        def _():
            h_scr[...] = jnp.zeros_like(h_scr)

        _, dt, _, cs, csT, causal, _ = _ssd_pre(dt_ref[:, :N_HEADS], dtT_ref[...], dtb_ref[...], dtbT_ref[...],
                                                al_ref[...], alT_ref[...])
        ex = _expand_mat()
        dt_full = _hdot(dt, ex, "a")
        cs_full = _hdot(cs, ex, "a")
        cs_last = cs_full[Q - 1:Q, :]
        xs = xs_ref[...]
        xd = xs * dt_full
        e_full = jnp.exp(cs_full)
        dec_full = jnp.exp(cs_last - cs_full)
        cd_full = jnp.exp(cs_last)
        lane_head = lax.broadcasted_iota(jnp.int32, (1, GROUP_W), 1) // HEAD_DIM
        for g in range(N_GROUPS):
            sl = slice(g * GROUP_W, (g + 1) * GROUP_W)
            Bg = B_ref[:, g * D_STATE:(g + 1) * D_STATE].astype(BF16)
            Cg = C_ref[:, g * D_STATE:(g + 1) * D_STATE].astype(BF16)
            CB = _dot_nt(Cg, Bg)
            hg = h_scr[g]
            yoff = _dot_nn(Cg, hg.astype(BF16)) * e_full[:, sl]
            xd_g = xd[:, sl]
            S = _dot_tn(Bg, (xd_g * dec_full[:, sl]).astype(BF16))
            xd_b = xd_g.astype(BF16)
            ydiag = jnp.zeros((Q, GROUP_W), F32)
            for r in range(HEADS_PER_GROUP):
                Lm = _decay_matrix(cs, csT, g * HEADS_PER_GROUP + r, causal)
                Gm = (CB * Lm).astype(BF16)
                ydiag = ydiag + _dot_nn(Gm, jnp.where(lane_head == r, xd_b, jnp.zeros_like(xd_b)))
            hs_ref[0, g] = hg
            h_scr[g] = hg * cd_full[:, sl] + S
            ypre = ydiag + yoff + xs[:, sl] * df_ref[:, sl]
            ypre_ref[:, sl] = ypre
            zg = z_ref[:, sl]
            yz = ypre * zg * _sigmoid(zg)
            rn = lax.rsqrt(jnp.mean(yz * yz, axis=-1, keepdims=True) + EPS)
            y_ref[:, sl] = (yz * rn * ng_ref[:, sl]).astype(BF16)

    return pl.pallas_call(
        body, grid=(nc,), in_specs=_ssd_in_specs(nc, False),
        out_specs=[pl.BlockSpec((CHUNK, D_INNER), lambda c: (c, 0)),
                   pl.BlockSpec((CHUNK, D_INNER), lambda c: (c, 0)),
                   pl.BlockSpec((1, N_GROUPS, D_STATE, GROUP_W), lambda c: (c, 0, 0, 0))],
        out_shape=[jax.ShapeDtypeStruct((T, D_INNER), BF16), jax.ShapeDtypeStruct((T, D_INNER), F32),
                   jax.ShapeDtypeStruct((nc, N_GROUPS, D_STATE, GROUP_W), F32)],
        scratch_shapes=[pltpu.VMEM((N_GROUPS, D_STATE, GROUP_W), F32)],
        name=name, compiler_params=_cparams(("arbitrary",)),
    )(xbc, xbc, xbc, proj, proj, dtT, dtb, dtbT, alog, alogT, dfull, ng)


def _ssd_bwd(xbc, proj, dtT, dtb, dtbT, alog, alogT, dfull, ng, ypre, hs, dy, name):
    T = xbc.shape[0]
    nc = T // CHUNK
    Q = CHUNK

    def body(xs_ref, B_ref, C_ref, z_ref, dt_ref, dtT_ref, dtb_ref, dtbT_ref, al_ref, alT_ref, df_ref, ng_ref,
             ypre_ref, hs_ref, dy_ref,
             dz_ref, dxbc_ref, ddt_ref, ddtb_ref, dal_ref, dD_ref, dng_ref, dh_scr):
        step = pl.program_id(0)

        @pl.when(step == 0)
        def _():
            dh_scr[...] = jnp.zeros_like(dh_scr)
            ddtb_ref[...] = jnp.zeros_like(ddtb_ref)
            dal_ref[...] = jnp.zeros_like(dal_ref)
            dD_ref[...] = jnp.zeros_like(dD_ref)
            dng_ref[...] = jnp.zeros_like(dng_ref)

        xdt, dt, A, cs, csT, causal, triu = _ssd_pre(dt_ref[:, :N_HEADS], dtT_ref[...], dtb_ref[...],
                                                    dtbT_ref[...], al_ref[...], alT_ref[...])
        ex = _expand_mat()
        dt_full = _hdot(dt, ex, "a")
        cs_full = _hdot(cs, ex, "a")
        cs_last = cs_full[Q - 1:Q, :]
        xs = xs_ref[...]
        xd = xs * dt_full
        e_full = jnp.exp(cs_full)
        dec_full = jnp.exp(cs_last - cs_full)
        cd_full = jnp.exp(cs_last)
        lane_head = lax.broadcasted_iota(jnp.int32, (1, GROUP_W), 1) // HEAD_DIM
        is_last = lax.broadcasted_iota(jnp.int32, (Q, 1), 0) == Q - 1
        dcs16 = jnp.zeros((Q, N_HEADS), F32)
        ddtx16 = jnp.zeros((Q, N_HEADS), F32)
        dD16 = jnp.zeros((8, N_HEADS), F32)
        lane16 = lax.broadcasted_iota(jnp.int32, (1, N_HEADS), 1)
        sub16 = lax.broadcasted_iota(jnp.int32, (N_HEADS, 1), 0)
        col_sums = jnp.zeros((N_HEADS, Q), F32)
        for g in range(N_GROUPS):
            sl = slice(g * GROUP_W, (g + 1) * GROUP_W)
            red = _reduce_mat(g)
            ypre_g = ypre_ref[:, sl]
            zg = z_ref[:, sl]
            sg = _sigmoid(zg)
            silu = zg * sg
            yz = ypre_g * silu
            rn = lax.rsqrt(jnp.mean(yz * yz, axis=-1, keepdims=True) + EPS)
            yh = yz * rn
            dy_g = dy_ref[:, sl]
            dng_ref[:, sl] += jnp.sum(dy_g * yh, axis=0, keepdims=True)
            dyh = dy_g * ng_ref[:, sl]
            dyz = rn * (dyh - yh * jnp.mean(dyh * yh, axis=-1, keepdims=True))
            dY = dyz * silu
            dz_ref[:, sl] = (dyz * ypre_g * sg * (1.0 + zg * (1.0 - sg))).astype(BF16)
            xs_g = xs[:, sl]
            xd_g = xd[:, sl]
            dec_g = dec_full[:, sl]
            cd_g = cd_full[:, sl]
            d_g = df_ref[:, sl]
            Bg = B_ref[:, g * D_STATE:(g + 1) * D_STATE].astype(BF16)
            Cg = C_ref[:, g * D_STATE:(g + 1) * D_STATE].astype(BF16)
            CB = _dot_nt(Cg, Bg)
            hg = hs_ref[0, g]
            hgb = hg.astype(BF16)
            yoff = _dot_nn(Cg, hgb) * e_full[:, sl]
            dhn = dh_scr[g]
            dhnb = dhn.astype(BF16)
            dYE = (dY * e_full[:, sl]).astype(BF16)
            dC = _dot_nt(dYE, hgb)
            dh_direct = _dot_tn(Cg, dYE)
            dXdd = _dot_nn(Bg, dhnb)
            dB = _dot_nt((xd_g * dec_g).astype(BF16), dhnb)
            dcd = jnp.sum(dhn * hg, axis=0, keepdims=True)
            dh_scr[g] = dh_direct + cd_g * dhn
            dYb = dY.astype(BF16)
            xd_b = xd_g.astype(BF16)
            dCB = jnp.zeros((Q, Q), F32)
            dXd = dXdd * dec_g
            for r in range(HEADS_PER_GROUP):
                h = g * HEADS_PER_GROUP + r
                Lm = _decay_matrix(cs, csT, h, causal)
                Gf = CB * Lm
                dYr = jnp.where(lane_head == r, dYb, jnp.zeros_like(dYb))
                dG = _dot_nt(dYr, xd_b)
                dCB = dCB + dG * Lm
                dXd = dXd + _dot_tn(Gf.astype(BF16), dYr)
                Mm = dG * Gf
                dcs16 = dcs16 + jnp.where(lane16 == h, jnp.sum(Mm, axis=1, keepdims=True), 0.0)
                col_sums = col_sums + jnp.where(sub16 == h, jnp.sum(Mm, axis=0, keepdims=True), 0.0)
            dCBb = dCB.astype(BF16)
            dC = dC + _dot_nn(dCBb, Bg)
            dB = dB + _dot_tn(dCBb, Cg)
            w_state = dXdd * dec_g * xd_g
            t_last = jnp.sum(w_state, axis=0, keepdims=True) + dcd * cd_g
            dcs_g = dY * yoff - w_state + jnp.where(is_last, t_last, 0.0)
            dcs16 = dcs16 + _hdot(dcs_g, red, "a")
            ddtx16 = ddtx16 + _hdot(dXd * xs_g, red, "a")
            dD16 = dD16 + _hdot(jnp.broadcast_to(jnp.sum(dY * xs_g, axis=0, keepdims=True), (8, GROUP_W)), red, "a")
            dxbc_ref[:, sl] = dXd * dt_full[:, sl] + dY * d_g
            dxbc_ref[:, D_INNER + g * D_STATE:D_INNER + (g + 1) * D_STATE] = dB
            dxbc_ref[:, D_INNER + 512 + g * D_STATE:D_INNER + 512 + (g + 1) * D_STATE] = dC
        eye = (lax.broadcasted_iota(jnp.int32, (N_HEADS, N_HEADS), 0)
               == lax.broadcasted_iota(jnp.int32, (N_HEADS, N_HEADS), 1)).astype(BF16)
        dcs16 = dcs16 - sum(_dot_tn(part, eye) for part in _split3(col_sums))
        da = _hdot(triu, dcs16, "b")
        ddt = da * A + ddtx16
        ddt_raw = ddt * _sigmoid(xdt)
        pr = lax.broadcasted_iota(jnp.int32, (N_HEADS, 128), 0)
        pc = lax.broadcasted_iota(jnp.int32, (N_HEADS, 128), 1)
        ddt_ref[...] = _hdot(ddt_raw, (pr == pc).astype(F32), "a")
        ddtb_ref[...] += jnp.sum(ddt_raw, axis=0, keepdims=True)
        dal_ref[...] += jnp.sum(da * dt, axis=0, keepdims=True) * A
        dD_ref[...] += dD16[0:1, :]

    def rc(c):
        return nc - 1 - c

    in_specs = _ssd_in_specs(nc, True) + [
        pl.BlockSpec((CHUNK, D_INNER), lambda c: (rc(c), 0)),
        pl.BlockSpec((1, N_GROUPS, D_STATE, GROUP_W), lambda c: (rc(c), 0, 0, 0)),
        pl.BlockSpec((CHUNK, D_INNER), lambda c: (rc(c), 0)),
    ]
    small = pl.BlockSpec((1, N_HEADS), lambda c: (0, 0))
    return pl.pallas_call(
        body, grid=(nc,), in_specs=in_specs,
        out_specs=[pl.BlockSpec((CHUNK, D_INNER), lambda c: (rc(c), 0)),
                   pl.BlockSpec((CHUNK, CONV_DIM), lambda c: (rc(c), 0)),
                   pl.BlockSpec((CHUNK, 128), lambda c: (rc(c), 0)),
                   small, small, small,
                   pl.BlockSpec((1, D_INNER), lambda c: (0, 0))],
        out_shape=[jax.ShapeDtypeStruct((T, D_INNER), BF16), jax.ShapeDtypeStruct((T, CONV_DIM), F32),
                   jax.ShapeDtypeStruct((T, 128), F32),
                   jax.ShapeDtypeStruct((1, N_HEADS), F32), jax.ShapeDtypeStruct((1, N_HEADS), F32),
                   jax.ShapeDtypeStruct((1, N_HEADS), F32), jax.ShapeDtypeStruct((1, D_INNER), F32)],
        scratch_shapes=[pltpu.VMEM((N_GROUPS, D_STATE, GROUP_W), F32)],
        name=name, compiler_params=_cparams(("arbitrary",)),
    )(xbc, xbc, xbc, proj, proj, dtT, dtb, dtbT, alog, alogT, dfull, ng, ypre, hs, dy)


N_PAIRS = ATTN_W // 128
PAIRS_PER_KV = N_PAIRS // 2
ATTN_SCALE = HEAD_DIM ** -0.5


def _kv_variants(kk):
    lo = lax.broadcasted_iota(jnp.int32, kk.shape, 1) < HEAD_DIM
    zero = jnp.zeros_like(kk)
    k00 = jnp.where(lo, kk, zero)
    k11 = jnp.where(lo, zero, kk)
    k01 = pltpu.roll(k00, HEAD_DIM, axis=1)
    k10 = pltpu.roll(k11, HEAD_DIM, axis=1)
    return [[k00.astype(BF16), k01.astype(BF16)], [k10.astype(BF16), k11.astype(BF16)]]


def _attn_valid(n):
    i = lax.broadcasted_iota(jnp.int32, (WINDOW, 2 * WINDOW), 0)
    j = lax.broadcasted_iota(jnp.int32, (WINDOW, 2 * WINDOW), 1)
    return (j > i) & (j <= i + WINDOW) & (n * WINDOW + j >= WINDOW)


def _attn_probs(qp, kvar, valid, sk):
    s = _dot_nt(qp, kvar) * ATTN_SCALE
    s = jnp.where(valid, s, NEG)
    m = jnp.maximum(jnp.max(s, axis=1, keepdims=True), sk)
    pe = jnp.exp(s - m)
    es = jnp.exp(sk - m)
    den = jnp.sum(pe, axis=1, keepdims=True) + es
    inv = 1.0 / den
    return pe * inv, es * inv


def _sink(sinks, r):
    lane = lax.broadcasted_iota(jnp.int32, sinks.shape, 1)
    return jnp.sum(jnp.where(lane == r, sinks, 0.0), axis=1, keepdims=True)


def _attn_fwd(proj, kpad, vpad, sinks, og, name):
    T = proj.shape[0]
    nb = T // WINDOW

    def body(q_ref, k_ref, v_ref, s_ref, og_ref, y_ref, o_ref):
        n = pl.program_id(0)
        start = pl.multiple_of(n * WINDOW, WINDOW)
        kv = _kv_variants(k_ref[pl.ds(start, 2 * WINDOW), :])
        vv = _kv_variants(v_ref[pl.ds(start, 2 * WINDOW), :])
        valid = _attn_valid(n)
        sinks_v = s_ref[...]
        ssq = jnp.zeros((WINDOW, 1), F32)
        for p in range(N_PAIRS):
            j = p // PAIRS_PER_KV
            qp = q_ref[:, p * 128:(p + 1) * 128].astype(BF16)
            o_pair = jnp.zeros((WINDOW, 128), F32)
            for par in range(2):
                pn, _ = _attn_probs(qp, kv[j][par], valid, _sink(sinks_v, 2 * p + par))
                o_pair = o_pair + _dot_nn(pn.astype(BF16), vv[j][par])
            o_ref[:, p * 128:(p + 1) * 128] = o_pair
            ssq = ssq + jnp.sum(o_pair * o_pair, axis=1, keepdims=True)
        rn = lax.rsqrt(ssq * (1.0 / ATTN_W) + EPS)
        y_ref[...] = (o_ref[...] * rn * og_ref[...]).astype(BF16)

    full_kv = pl.BlockSpec((T + WINDOW, KV_W), lambda n: (0, 0))
    return pl.pallas_call(
        body, grid=(nb,),
        in_specs=[pl.BlockSpec((WINDOW, ATTN_W), lambda n: (n, OFF_Q // ATTN_W)), full_kv, full_kv,
                  pl.BlockSpec((1, N_HEADS), lambda n: (0, 0)), pl.BlockSpec((1, ATTN_W), lambda n: (0, 0))],
        out_specs=[pl.BlockSpec((WINDOW, ATTN_W), lambda n: (n, 0)), pl.BlockSpec((WINDOW, ATTN_W), lambda n: (n, 0))],
        out_shape=[jax.ShapeDtypeStruct((T, ATTN_W), BF16), jax.ShapeDtypeStruct((T, ATTN_W), F32)],
        name=name, compiler_params=_cparams(("parallel",)),
    )(proj, kpad, vpad, sinks, og)


def _attn_bwd(proj, kpad, vpad, sinks, og, o, dy, name):
    T = proj.shape[0]
    nb = T // WINDOW

    def body(q_ref, k_ref, v_ref, s_ref, og_ref, o_ref, dy_ref, dq_ref, dk_ref, dv_ref, ds_ref, dog_ref):
        n = pl.program_id(0)

        @pl.when(n == 0)
        def _():
            dk_ref[...] = jnp.zeros_like(dk_ref)
            dv_ref[...] = jnp.zeros_like(dv_ref)
            ds_ref[...] = jnp.zeros_like(ds_ref)
            dog_ref[...] = jnp.zeros_like(dog_ref)

        start = pl.multiple_of(n * WINDOW, WINDOW)
        kv = _kv_variants(k_ref[pl.ds(start, 2 * WINDOW), :])
        vv = _kv_variants(v_ref[pl.ds(start, 2 * WINDOW), :])
        valid = _attn_valid(n)
        sinks_v = s_ref[...]
        of = o_ref[...]
        rn = lax.rsqrt(jnp.mean(of * of, axis=-1, keepdims=True) + EPS)
        oh = of * rn
        dyf = dy_ref[...]
        dog_ref[...] += jnp.sum(dyf * oh, axis=0, keepdims=True)
        doh = dyf * og_ref[...]
        do = rn * (doh - oh * jnp.mean(doh * oh, axis=-1, keepdims=True))
        lane = lax.broadcasted_iota(jnp.int32, (1, 128), 1)
        lane16 = lax.broadcasted_iota(jnp.int32, (1, N_HEADS), 1)
        dk_acc = [[jnp.zeros((2 * WINDOW, 128), F32) for _ in range(2)] for _ in range(2)]
        dv_acc = [[jnp.zeros((2 * WINDOW, 128), F32) for _ in range(2)] for _ in range(2)]
        dsink = jnp.zeros((1, N_HEADS), F32)
        for p in range(N_PAIRS):
            j = p // PAIRS_PER_KV
            qp = q_ref[:, p * 128:(p + 1) * 128].astype(BF16)
            do_p = do[:, p * 128:(p + 1) * 128]
            o_p = of[:, p * 128:(p + 1) * 128]
            do_b = do_p.astype(BF16)
            prod = do_p * o_p
            dq_pair = jnp.zeros((WINDOW, 128), F32)
            for par in range(2):
                r = 2 * p + par
                half = (lane < HEAD_DIM) if par == 0 else (lane >= HEAD_DIM)
                pn, ps = _attn_probs(qp, kv[j][par], valid, _sink(sinks_v, r))
                delta = jnp.sum(jnp.where(half, prod, 0.0), axis=1, keepdims=True)
                dP = _dot_nt(do_b, vv[j][par])
                dS = pn * (dP - delta)
                dsink = dsink + jnp.where(lane16 == r, -jnp.sum(ps * delta, axis=0, keepdims=True), 0.0)
                dSb = (dS * ATTN_SCALE).astype(BF16)
                dq_pair = dq_pair + _dot_nn(dSb, kv[j][par])
                dk_acc[j][par] = dk_acc[j][par] + _dot_tn(dSb, jnp.where(half, qp, jnp.zeros_like(qp)))
                dv_acc[j][par] = dv_acc[j][par] + _dot_tn(pn.astype(BF16), jnp.where(half, do_b, jnp.zeros_like(do_b)))
            dq_ref[:, p * 128:(p + 1) * 128] = dq_pair.astype(BF16)
        dkk = (dk_acc[0][0] + pltpu.roll(dk_acc[0][1], HEAD_DIM, axis=1)
               + dk_acc[1][1] + pltpu.roll(dk_acc[1][0], HEAD_DIM, axis=1))
        dvv = (dv_acc[0][0] + pltpu.roll(dv_acc[0][1], HEAD_DIM, axis=1)
               + dv_acc[1][1] + pltpu.roll(dv_acc[1][0], HEAD_DIM, axis=1))
        dk_ref[pl.ds(start, 2 * WINDOW), :] += dkk
        dv_ref[pl.ds(start, 2 * WINDOW), :] += dvv
        ds_ref[...] += dsink

    full_kv = pl.BlockSpec((T + WINDOW, KV_W), lambda n: (0, 0))
    blk = pl.BlockSpec((WINDOW, ATTN_W), lambda n: (n, 0))
    return pl.pallas_call(
        body, grid=(nb,),
        in_specs=[pl.BlockSpec((WINDOW, ATTN_W), lambda n: (n, OFF_Q // ATTN_W)), full_kv, full_kv,
                  pl.BlockSpec((1, N_HEADS), lambda n: (0, 0)), pl.BlockSpec((1, ATTN_W), lambda n: (0, 0)),
                  blk, pl.BlockSpec((WINDOW, ATTN_W), lambda n: (n, 1))],
        out_specs=[blk, full_kv, full_kv, pl.BlockSpec((1, N_HEADS), lambda n: (0, 0)),
                   pl.BlockSpec((1, ATTN_W), lambda n: (0, 0))],
        out_shape=[jax.ShapeDtypeStruct((T, ATTN_W), BF16), jax.ShapeDtypeStruct((T + WINDOW, KV_W), F32),
                   jax.ShapeDtypeStruct((T + WINDOW, KV_W), F32), jax.ShapeDtypeStruct((1, N_HEADS), F32),
                   jax.ShapeDtypeStruct((1, ATTN_W), F32)],
        name=name, compiler_params=_cparams(("arbitrary",)),
    )(proj, kpad, vpad, sinks, og, o, dy)


ANY = pl.BlockSpec(memory_space=pl.ANY)


def _coords():
    return lax.axis_index("x"), lax.axis_index("y"), lax.axis_index("c")


def _all_gather(arrs, name):
    n = len(arrs)

    def body(*refs):
        ins, outs = refs[:n], refs[n:2 * n]
        send_sems, recv_sems, local_sems = refs[2 * n:]
        x, y, c = _coords()
        me = 4 * x + 2 * y + c
        sibling = (x, y, 1 - c)
        chips = [(1 - x, y), (x, 1 - y), (1 - x, 1 - y)]

        def copy(a, k, block, to, src=None):
            dst = outs[a].at[block]
            return pltpu.make_async_remote_copy(
                src_ref=dst if src is None else src, dst_ref=dst, send_sem=send_sems.at[a, k],
                recv_sem=recv_sems.at[a, k], device_id=to, device_id_type=MESH)

        mine = [pltpu.make_async_copy(ins[a], outs[a].at[me], local_sems.at[a]) for a in range(n)]
        for cp in mine:
            cp.start()
        first = []
        for a in range(n):
            first.append(copy(a, 0, me, sibling, src=ins[a]))
            for j, chip in enumerate(chips):
                first.append(copy(a, 1 + j, me, (*chip, c), src=ins[a]))
        for cp in first:
            cp.start()
        passed = []
        for j, (px, py) in enumerate(chips):
            blk = 4 * px + 2 * py + c
            for a in range(n):
                copy(a, 1 + j, blk, sibling).wait_recv()
                fwd = copy(a, 4 + j, blk, sibling)
                fwd.start()
                passed.append(fwd)
        for a in range(n):
            copy(a, 0, 4 * x + 2 * y + (1 - c), sibling).wait_recv()
            for j, (px, py) in enumerate(chips):
                copy(a, 4 + j, 4 * px + 2 * py + (1 - c), sibling).wait_recv()
        for cp in first + passed:
            cp.wait_send()
        for cp in mine:
            cp.wait()

    return pl.pallas_call(
        body, in_specs=[ANY] * n, out_specs=[ANY] * n,
        out_shape=[jax.ShapeDtypeStruct((N_DEV,) + a.shape, a.dtype) for a in arrs],
        scratch_shapes=[pltpu.SemaphoreType.DMA((n, 7)), pltpu.SemaphoreType.DMA((n, 7)),
                        pltpu.SemaphoreType.DMA((n,))],
        name=name,
    )(*arrs)


def _exchange_pair(arrs, name):
    n = len(arrs)

    def body(*refs):
        ins, outs = refs[:n], refs[n:2 * n]
        send_sems, recv_sems = refs[2 * n:]
        x, y, c = _coords()
        cps = []
        for a in range(n):
            for q in range(4):
                cps.append(pltpu.make_async_remote_copy(
                    src_ref=ins[a].at[2 * q + (1 - c)], dst_ref=outs[a].at[q], send_sem=send_sems.at[a, q],
                    recv_sem=recv_sems.at[a, q], device_id=(x, y, 1 - c), device_id_type=MESH))
        for cp in cps:
            cp.start()
        for cp in cps:
            cp.wait()

    return pl.pallas_call(
        body, in_specs=[ANY] * n, out_specs=[ANY] * n,
        out_shape=[jax.ShapeDtypeStruct((4,) + a.shape[1:], a.dtype) for a in arrs],
        scratch_shapes=[pltpu.SemaphoreType.DMA((n, 4)), pltpu.SemaphoreType.DMA((n, 4))],
        name=name,
    )(*arrs)


def _exchange_chips(arrs, name):
    n = len(arrs)

    def body(*refs):
        ins, outs = refs[:n], refs[n:2 * n]
        send_sems, recv_sems = refs[2 * n:]
        x, y, c = _coords()
        chips = [(1 - x, y), (x, 1 - y), (1 - x, 1 - y)]
        cps = []
        for a in range(n):
            for k, (tx, ty) in enumerate(chips):
                cps.append(pltpu.make_async_remote_copy(
                    src_ref=ins[a].at[2 * tx + ty], dst_ref=outs[a].at[k], send_sem=send_sems.at[a, k],
                    recv_sem=recv_sems.at[a, k], device_id=(tx, ty, c), device_id_type=MESH))
        for cp in cps:
            cp.start()
        for cp in cps:
            cp.wait()

    return pl.pallas_call(
        body, in_specs=[ANY] * n, out_specs=[ANY] * n,
        out_shape=[jax.ShapeDtypeStruct((3,) + a.shape[1:], a.dtype) for a in arrs],
        scratch_shapes=[pltpu.SemaphoreType.DMA((n, 3)), pltpu.SemaphoreType.DMA((n, 3))],
        name=name,
    )(*arrs)


HBM = pl.BlockSpec(memory_space=pltpu.HBM)
SEM = pl.BlockSpec(memory_space=pltpu.SEMAPHORE)
EFFECT = pltpu.SideEffectType.DATAFLOW_SIDE_EFFECTING


def _in_hbm(a):
    return pltpu.with_memory_space_constraint(a, pltpu.HBM)


def _remote_start(srcs, lands, plan, n_copies, name, after=None):
    n = len(srcs)
    n_after = 0 if after is None else 1

    def body(*refs):
        src_refs, land_refs = refs[:n], refs[n:2 * n]
        send_sems, recv_sems = refs[2 * n + n_after], refs[2 * n + n_after + 1]
        token = refs[-1]
        x, y, c = _coords()
        for i, (sv, dv, dev) in enumerate(plan(src_refs, land_refs, x, y, c)):
            pltpu.make_async_remote_copy(src_ref=sv, dst_ref=dv, send_sem=send_sems.at[i], recv_sem=recv_sems.at[i],
                                         device_id=dev, device_id_type=MESH).start()
        token[...] = jnp.zeros_like(token)

    bufs = list(srcs) + list(lands)
    outs = pl.pallas_call(
        body, name=name,
        out_shape=(pltpu.SemaphoreType.DMA((n_copies,)), pltpu.SemaphoreType.DMA((n_copies,)),
                   *[pltpu.HBM(b.shape, b.dtype) for b in bufs], jax.ShapeDtypeStruct((8, 128), F32)),
        in_specs=[HBM] * (2 * n) + [ANY] * n_after,
        out_specs=(SEM, SEM, *[HBM] * (2 * n), pl.BlockSpec(memory_space=pltpu.VMEM)),
        input_output_aliases={i: 2 + i for i in range(2 * n)},
        compiler_params=pltpu.CompilerParams(has_side_effects=EFFECT),
    )(*[_in_hbm(b) for b in bufs], *([] if after is None else [after]))
    return outs[0], outs[1], list(outs[2:2 + n]), list(outs[2 + n:2 + 2 * n]), outs[-1]


def _remote_wait(started, after, plan, name):
    send_sems, recv_sems, srcs, lands, _ = started
    n = len(srcs)

    def body(*refs):
        src_refs, land_refs = refs[:n], refs[n:2 * n]
        send_sems, recv_sems = refs[2 * n], refs[2 * n + 1]
        x, y, c = _coords()
        for i, (sv, dv, dev) in enumerate(plan(src_refs, land_refs, x, y, c)):
            cp = pltpu.make_async_remote_copy(src_ref=sv, dst_ref=dv, send_sem=send_sems.at[i],
                                              recv_sem=recv_sems.at[i], device_id=dev, device_id_type=MESH)
            cp.wait_send()
            cp.wait_recv()

    bufs = list(srcs) + list(lands)
    outs = pl.pallas_call(
        body, name=name, out_shape=tuple(pltpu.HBM(b.shape, b.dtype) for b in bufs),
        in_specs=[HBM] * (2 * n) + [SEM, SEM, ANY], out_specs=tuple([HBM] * (2 * n)),
        input_output_aliases={i: i for i in range(2 * n)},
        compiler_params=pltpu.CompilerParams(has_side_effects=EFFECT),
    )(*bufs, send_sems, recv_sems, after)
    return list(outs[:n]), list(outs[n:])


def _gather_plan(src_refs, land_refs, x, y, c):
    me = 4 * x + 2 * y + c
    plan = []
    for s, l in zip(src_refs, land_refs):
        for dev in [(x, y, 1 - c), (1 - x, y, c), (x, 1 - y, c), (1 - x, 1 - y, c)]:
            plan.append((s, l.at[me], dev))
    return plan


def _pair_plan(src_refs, land_refs, x, y, c):
    plan = []
    for s, l in zip(src_refs, land_refs):
        for q in range(4):
            plan.append((s.at[2 * q + (1 - c)], l.at[q], (x, y, 1 - c)))
    return plan


def _pair4_plan(src_refs, land_refs, x, y, c):
    plan = []
    for s, l in zip(src_refs, land_refs):
        for q in range(4):
            plan.append((s.at[q], l.at[q], (x, y, 1 - c)))
    return plan


def _chips_plan(src_refs, land_refs, x, y, c):
    plan = []
    for s, l in zip(src_refs, land_refs):
        for k, (tx, ty) in enumerate([(1 - x, y), (x, 1 - y), (1 - x, 1 - y)]):
            plan.append((s.at[2 * tx + ty], l.at[k], (tx, ty, c)))
    return plan


def _everyone_plan(src_refs, land_refs, x, y, c):
    me = 4 * x + 2 * y + c
    plan = []
    for s, l in zip(src_refs, land_refs):
        for fx, fy, fc in [(0, 0, 1), (1, 0, 0), (1, 0, 1), (0, 1, 0), (0, 1, 1), (1, 1, 0), (1, 1, 1)]:
            dev = ((1 - x) if fx else x, (1 - y) if fy else y, (1 - c) if fc else c)
            plan.append((s, l.at[me], dev))
    return plan


def _gather_finish(gathered, name):
    n = len(gathered)

    def body(*refs):
        outs = refs[n:2 * n]
        send_sems, recv_sems = refs[2 * n:]
        x, y, c = _coords()
        cps = []
        for a in range(n):
            for j, (px, py) in enumerate([(1 - x, y), (x, 1 - y), (1 - x, 1 - y)]):
                blk = outs[a].at[4 * px + 2 * py + c]
                got = outs[a].at[4 * px + 2 * py + (1 - c)]
                cps.append((pltpu.make_async_remote_copy(
                    src_ref=blk, dst_ref=blk, send_sem=send_sems.at[a, j], recv_sem=recv_sems.at[a, j],
                    device_id=(x, y, 1 - c), device_id_type=MESH), pltpu.make_async_remote_copy(
                    src_ref=got, dst_ref=got, send_sem=send_sems.at[a, j], recv_sem=recv_sems.at[a, j],
                    device_id=(x, y, 1 - c), device_id_type=MESH)))
        for cp, _ in cps:
            cp.start()
        for cp, arrival in cps:
            cp.wait_send()
            arrival.wait_recv()

    return pl.pallas_call(
        body, in_specs=[ANY] * n, out_specs=[ANY] * n,
        out_shape=[jax.ShapeDtypeStruct(g.shape, g.dtype) for g in gathered],
        input_output_aliases={a: a for a in range(n)},
        scratch_shapes=[pltpu.SemaphoreType.DMA((n, 3)), pltpu.SemaphoreType.DMA((n, 3))],
        name=name,
    )(*gathered)


def _pair_add(g8, r1, csel, tr, name):
    _, R, C = r1.shape
    g4 = g8.reshape(4, 2, R, C)

    def body(c_ref, g_ref, r_ref, o_ref):
        o_ref[...] = (g_ref[...].astype(F32) + r_ref[...].astype(F32)).astype(BF16)

    return pl.pallas_call(
        body,
        grid_spec=pltpu.PrefetchScalarGridSpec(
            num_scalar_prefetch=1, grid=(4, R // tr),
            in_specs=[pl.BlockSpec((None, None, tr, C), lambda q, i, cs: (q, cs[0], i, 0)),
                      pl.BlockSpec((None, tr, C), lambda q, i, cs: (q, i, 0))],
            out_specs=pl.BlockSpec((None, tr, C), lambda q, i, cs: (q, i, 0))),
        out_shape=jax.ShapeDtypeStruct((4, R, C), BF16), name=name,
        compiler_params=_cparams(("parallel", "parallel")),
    )(csel, g4, r1)


def _adamw_math(w, g, m, v):
    m = ADAM_B1 * m + (1.0 - ADAM_B1) * g
    v = ADAM_B2 * v + (1.0 - ADAM_B2) * (g * g)
    m_hat = m / (1.0 - ADAM_B1 ** ADAM_STEP)
    v_hat = v / (1.0 - ADAM_B2 ** ADAM_STEP)
    delta = -ADAM_LR * (m_hat / (jnp.sqrt(v_hat) + ADAM_EPS) + ADAM_WD * w)
    return delta, m, v


def _adamw_big(w, m, v, p4, r3, qsel, tile, name):
    R, C = w.shape
    tr, tc = tile

    def body(q_ref, w_ref, m_ref, v_ref, p_ref, r_ref, g_out, d_out, m_out, v_out):
        g = p_ref[...].astype(F32) + r_ref[0].astype(F32) + r_ref[1].astype(F32) + r_ref[2].astype(F32)
        d, mn, vn = _adamw_math(w_ref[...], g, m_ref[...], v_ref[...])
        g_out[...] = g
        d_out[...] = d
        m_out[...] = mn
        v_out[...] = vn

    blk = pl.BlockSpec((tr, tc), lambda i, j, qs: (i, j))
    return pl.pallas_call(
        body,
        grid_spec=pltpu.PrefetchScalarGridSpec(
            num_scalar_prefetch=1, grid=(R // tr, C // tc),
            in_specs=[blk, blk, blk, pl.BlockSpec((None, tr, tc), lambda i, j, qs: (qs[0], i, j)),
                      pl.BlockSpec((3, tr, tc), lambda i, j, qs: (0, i, j))],
            out_specs=[blk, blk, blk, blk]),
        out_shape=[jax.ShapeDtypeStruct((R, C), F32)] * 4, name=name,
        compiler_params=_cparams(("parallel", "parallel")),
    )(qsel, w, m, v, p4, r3)


def _small_sum(parts, name):
    def body(p_ref, o_ref):
        acc = p_ref[0]
        for d in range(1, N_DEV):
            acc = acc + p_ref[d]
        o_ref[...] = acc

    return pl.pallas_call(
        body, out_shape=jax.ShapeDtypeStruct(parts.shape[1:], F32), name=name,
        compiler_params=_cparams(),
    )(parts)


def _adamw_small(w, g, m, v, name):
    def body(w_ref, g_ref, m_ref, v_ref, d_out, m_out, v_out):
        d, mn, vn = _adamw_math(w_ref[...], g_ref[...], m_ref[...], v_ref[...])
        d_out[...] = d
        m_out[...] = mn
        v_out[...] = vn

    return pl.pallas_call(
        body, out_shape=[jax.ShapeDtypeStruct(w.shape, F32)] * 3, name=name, compiler_params=_cparams(),
    )(w, g, m, v)


def _row(*pieces):
    r = jnp.concatenate([p.reshape(1, -1) for p in pieces], axis=1)
    return jnp.pad(r, ((0, 0), (0, D_MODEL - r.shape[1])))


def _pack_small(mix, convb, ssmg, attng, mlpg, fing, convw, dtb, alog, dsk, sinks, extra=None):
    last = [dtb, alog, dsk, sinks] + ([extra] if extra is not None else [])
    rows = [_row(mix), _row(convb), _row(ssmg, attng), _row(mlpg), _row(fing),
            jnp.pad(convw, ((0, 0), (0, D_MODEL - convw.shape[1]))), _row(*last)]
    packed = jnp.concatenate(rows, axis=0)
    return jnp.pad(packed, ((0, SMALL_ROWS - packed.shape[0]), (0, 0)))


def _unpack_small(p, conv_n):
    return dict(
        mix_norm_g=p[0:1, :], conv_b=p[1:2, :], ssm_norm_g=p[2:3, :D_INNER], attn_out_norm_g=p[2:3, D_INNER:],
        mlp_norm_g=p[3:4, :], final_norm_g=p[4, :], conv_w=p[5:9, :conv_n][None],
        dt_bias=p[9:10, 0:16], A_log=p[9:10, 16:32], D_skip=p[9:10, 32:48], attn_sinks=p[9:10, 48:64])


SMALL_NAMES = ["mix_norm_g", "conv_w", "conv_b", "dt_bias", "A_log", "D_skip", "ssm_norm_g", "attn_sinks",
               "attn_out_norm_g", "mlp_norm_g", "final_norm_g"]
WEIGHT_ORDER = ["mix_norm_g", "w_in", "conv_w", "conv_b", "dt_bias", "A_log", "D_skip", "ssm_norm_g", "attn_sinks",
                "attn_out_norm_g", "w_out", "mlp_norm_g", "w_up", "w_down", "final_norm_g"]


def _to_my_columns(w_nat):
    pad = jnp.zeros((w_nat.shape[0], NP - IN_PROJ), w_nat.dtype)
    return jnp.concatenate([w_nat[:, :NAT_DT], w_nat[:, NAT_DT + N_HEADS:], w_nat[:, NAT_DT:NAT_DT + N_HEADS], pad],
                           axis=1)


def _to_natural_columns(w_my):
    return jnp.concatenate([w_my[:, :NAT_DT], w_my[:, OFF_DT:OFF_DT + N_HEADS], w_my[:, NAT_DT:OFF_DT]], axis=1)


SLAB = 1024


def _grad_w_up(h2, du, name, sel=None, add=None, after=None):
    T, D = h2.shape
    if sel is None:
        pick, n_slab, pre = (lambda j, *cs: j), N_DEV, None
    else:
        pre, other = sel
        pick, n_slab = (lambda j, cs: 2 * j + ((1 - cs[0]) if other else cs[0])), 4
    o_spec = pl.BlockSpec((None, SLAB, SLAB), lambda i, j, k, *cs: (j, i, 0))
    return _matmul(
        h2, du, mode="tn", grid=(D // SLAB, n_slab, 1),
        a_spec=pl.BlockSpec((T, SLAB), lambda i, j, k, *cs: (0, i)),
        b_spec=pl.BlockSpec((T, SLAB), lambda i, j, k, *cs: (0, pick(j, *cs))),
        out_shapes=[jax.ShapeDtypeStruct((n_slab, D, SLAB), BF16)], out_specs=[o_spec], tile=(SLAB, SLAB), name=name,
        extras=() if add is None else (add,), extra_specs=() if add is None else (o_spec,),
        epilogue=None if add is None else (lambda acc, r: (acc + r.astype(F32),)), after=after, prefetch=pre)[0]


def _grad_w_down(act, dx3b, name, sel=None, add=None, after=None):
    T, D = dx3b.shape
    if sel is None:
        pick, n_slab, pre = (lambda i, *cs: i), N_DEV, None
    else:
        pre, other = sel
        pick, n_slab = (lambda i, cs: 2 * i + ((1 - cs[0]) if other else cs[0])), 4
    o_spec = pl.BlockSpec((None, SLAB, SLAB), lambda i, j, k, *cs: (i, 0, j))
    return _matmul(
        act, dx3b, mode="tn", grid=(n_slab, D // SLAB, 1),
        a_spec=pl.BlockSpec((T, SLAB), lambda i, j, k, *cs: (0, pick(i, *cs))),
        b_spec=pl.BlockSpec((T, SLAB), lambda i, j, k, *cs: (0, j)),
        out_shapes=[jax.ShapeDtypeStruct((n_slab, SLAB, D), BF16)], out_specs=[o_spec], tile=(SLAB, SLAB), name=name,
        extras=() if add is None else (add,), extra_specs=() if add is None else (o_spec,),
        epilogue=None if add is None else (lambda acc, r: (acc + r.astype(F32),)), after=after, prefetch=pre)[0]


class _FixedWeights:
    def __init__(self, w_in_p, w_out_f, w_up_s, w_down_f, conv_w_f):
        self.w = (w_in_p, w_out_f, w_up_s, w_down_f, conv_w_f)
        self.grads = {}

    def mixer_weights(self, after):
        return self.w[0], self.w[4], None

    def out_weight(self, after):
        return self.w[1]

    def up_weight(self, after):
        return self.w[2]

    def down_weight(self, after):
        return self.w[3]

    def mlp_grads(self, h2, du, act, dx3b):
        self.grads.update(w_up=_grad_w_up(h2, du, "grad_w_up"),
                          w_down=_grad_w_down(act, dx3b, "grad_w_down").reshape(D_FF, D_MODEL))
        return None

    def out_grad(self, g_out):
        self.grads.update(w_out=g_out)
        return None

    def in_grad(self, g_in):
        self.grads.update(w_in=g_in)
        return None


def _local_step(x, tgt, p, hooks):
    T = x.shape[0]
    D = D_MODEL
    h1 = _rmsnorm_fwd(x, p["mix_norm_g"], "norm_mix")
    w_in_p, conv_w_f, token = hooks.mixer_weights(h1)
    (proj,) = _mm_simple(h1, w_in_p, mode="nn", M=T, N=NP, K=D, tm=min(T, 1024), tn=1536, tk=D, out_dtype=F32,
                         name="in_proj", after=token)
    xbc = _conv_fwd(proj, conv_w_f, p["conv_b"], "conv_fwd")
    dtT = proj[:, OFF_DT:OFF_DT + N_HEADS].T
    dtbT = p["dt_bias"].T
    alogT = p["A_log"].T
    dfull = jnp.repeat(p["D_skip"], HEAD_DIM, axis=1)
    y_ssm, ypre, hs = _ssd_fwd(xbc, proj, dtT, p["dt_bias"], dtbT, p["A_log"], alogT, dfull, p["ssm_norm_g"],
                               "ssd_fwd")
    kpad = jnp.pad(proj[:, OFF_K:OFF_K + KV_W], ((WINDOW, 0), (0, 0)))
    vpad = jnp.pad(proj[:, OFF_V:OFF_V + KV_W], ((WINDOW, 0), (0, 0)))
    y_att, o_att = _attn_fwd(proj, kpad, vpad, p["attn_sinks"], p["attn_out_norm_g"], "attn_fwd")
    ycat = jnp.concatenate([y_ssm, y_att], axis=1)
    w_out_f = hooks.out_weight(ycat)
    tm = min(T, 1024)
    (x2,) = _mm_simple(ycat, w_out_f, mode="nn", M=T, N=D, K=D, tm=tm, tn=1024, tk=D, out_dtype=F32, name="out_proj",
                       extras=(x,), epilogue=lambda acc, res: (acc + res,))
    h2 = _rmsnorm_fwd(x2, p["mlp_norm_g"], "norm_mlp")
    w_up_s = hooks.up_weight(h2)
    grid = (T // tm, N_DEV, 1)
    u, act = _matmul(
        h2, w_up_s, mode="nn", grid=grid,
        a_spec=pl.BlockSpec((tm, D), lambda i, j, k: (i, 0)),
        b_spec=pl.BlockSpec((None, D, 1024), lambda i, j, k: (j, 0, 0)),
        out_shapes=[jax.ShapeDtypeStruct((T, D_FF), F32), jax.ShapeDtypeStruct((T, D_FF), BF16)],
        out_specs=[pl.BlockSpec((tm, 1024), lambda i, j, k: (i, j))] * 2, tile=(tm, 1024), name="mlp_up",
        epilogue=lambda acc: (acc, jnp.square(jnp.maximum(acc, 0.0))))
    w_down_f = hooks.down_weight(act)
    (x3,) = _mm_simple(act, w_down_f, mode="nn", M=T, N=D, K=D_FF, tm=tm, tn=1024, tk=2048, out_dtype=F32,
                       name="mlp_down", extras=(x2,), epilogue=lambda acc, res: (acc + res,))
    loss_part, d_fin, dx3, dx3b = _final_loss(x3, tgt, p["final_norm_g"].reshape(1, D), "loss_head")
    (du,) = _mm_simple(dx3b, w_down_f, mode="nt", M=T, N=D_FF, K=D, tm=tm, tn=1024, tk=D, out_dtype=BF16,
                       name="mlp_down_bwd", extras=(u,),
                       epilogue=lambda acc, uu: (acc * (2.0 * jnp.maximum(uu, 0.0)),))
    token = hooks.mlp_grads(h2, du, act, dx3b)
    (dh2,) = _matmul(
        du, w_up_s, mode="nt", grid=(T // tm, D // 1024, N_DEV // 2),
        a_spec=pl.BlockSpec((tm, 2048), lambda i, j, k: (i, k)),
        b_spec=pl.BlockSpec((2, 1024, 1024), lambda i, j, k: (k, j, 0)),
        out_shapes=[jax.ShapeDtypeStruct((T, D), F32)],
        out_specs=[pl.BlockSpec((tm, 1024), lambda i, j, k: (i, j))], tile=(tm, 1024), name="mlp_up_bwd",
        after=token, dot_fn=lambda a, b: _dot_nt(a[:, :1024], b[0]) + _dot_nt(a[:, 1024:], b[1]))
    dx2, dx2b, d_mlp = _rmsnorm_bwd(dh2, x2, p["mlp_norm_g"], dx3, "norm_mlp_bwd")
    (g_out,) = _mm_simple(ycat, dx2b, mode="tn", M=D, N=D, K=T, tm=1024, tn=1024, tk=T, out_dtype=BF16,
                          name="grad_w_out")
    token = hooks.out_grad(g_out)
    (dy,) = _mm_simple(dx2b, w_out_f, mode="nt", M=T, N=D, K=D, tm=tm, tn=1024, tk=D, out_dtype=F32,
                       name="out_proj_bwd", after=token)
    dz, dxbc_act, ddt, d_dtb, d_alog, d_dskip, d_ssmg = _ssd_bwd(
        xbc, proj, dtT, p["dt_bias"], dtbT, p["A_log"], alogT, dfull, p["ssm_norm_g"], ypre, hs, dy, "ssd_bwd")
    dq, dkpad, dvpad, d_sinks, d_attng = _attn_bwd(proj, kpad, vpad, p["attn_sinks"], p["attn_out_norm_g"], o_att, dy,
                                                   "attn_bwd")
    dxbc, d_convw, d_convb = _conv_bwd(proj, dxbc_act, conv_w_f, p["conv_b"], "conv_bwd")
    dproj = jnp.concatenate(
        [dz, dxbc, dq, dkpad[WINDOW:].astype(BF16), dvpad[WINDOW:].astype(BF16), ddt.astype(BF16),
         jnp.zeros((T, NP - OFF_DT - 128), BF16)], axis=1)
    (g_in,) = _mm_simple(dproj, h1, mode="tn", M=NP, N=D, K=T, tm=1536, tn=1024, tk=T, out_dtype=BF16,
                         name="grad_w_in")
    token = hooks.in_grad(g_in)
    (dh1,) = _mm_simple(dproj, w_in_p, mode="nt", M=T, N=D, K=NP, tm=tm, tn=1024, tk=2304, out_dtype=F32,
                        name="in_proj_bwd", after=token)
    dx, _, d_mix = _rmsnorm_bwd(dh1, x, p["mix_norm_g"], dx2, "norm_mix_bwd")
    small = _pack_small(d_mix, d_convb, d_ssmg, d_attng, d_mlp, d_fin, d_convw, d_dtb, d_alog, d_dskip, d_sinks,
                        extra=loss_part[:, 0:1])
    return dx, small


def _landing(own, me):
    zone = lax.empty((N_DEV,) + own.shape, own.dtype)
    return lax.dynamic_update_slice(zone, own[None], (me,) + (0,) * own.ndim)


def _gather_end(started, after, plan, name):
    _, lands = _remote_wait(started, after, plan, name + "_wait")
    return _gather_finish(lands, name + "_finish")


class _ShardedWeights:
    def __init__(self, w_in, w_out, conv_w, w_up, w_down, me, csel):
        self.me, self.csel = me, csel
        shards = [w_in.astype(BF16), conv_w]
        self.st_mixer = _remote_start(shards, [_landing(s, me) for s in shards], _gather_plan, 4 * len(shards),
                                      "gather_start_mixer")
        self.start_token = self.st_mixer[4]
        self.later = {"out": w_out.astype(BF16), "up": w_up.astype(BF16), "down": w_down.astype(BF16)}
        self.later_lands = {k: _landing(v, me) for k, v in self.later.items()}
        self.st_later = {}
        self.reduces = {}
        self.pair_mlp = None

    def mixer_weights(self, after):
        g_in, g_conv = _gather_end(self.st_mixer, after, _gather_plan, "gather_mixer")
        order = g_conv
        for k in ["out", "up", "down"]:
            self.st_later[k] = _remote_start([self.later[k]], [self.later_lands[k]], _gather_plan, 4,
                                             f"gather_start_{k}", after=order)
            order = self.st_later[k][4]
        per = IN_PROJ // N_DEV
        k, off = NAT_DT // per, NAT_DT % per
        pieces = [g_in[i] for i in range(k)] + [g_in[k][:, :off], g_in[k][:, off + N_HEADS:]]
        pieces += [g_in[i] for i in range(k + 1, N_DEV)]
        pieces += [g_in[k][:, off:off + N_HEADS], jnp.zeros((D_MODEL, NP - IN_PROJ), BF16)]
        w_in_p = jnp.concatenate(pieces, axis=1)
        conv_w_f = jnp.concatenate([g_conv[i] for i in range(N_DEV)], axis=1)
        return w_in_p, conv_w_f, order

    def out_weight(self, after):
        return _gather_end(self.st_later["out"], after, _gather_plan, "gather_out")[0].reshape(D_MODEL, D_MODEL)

    def up_weight(self, after):
        return _gather_end(self.st_later["up"], after, _gather_plan, "gather_up")[0]

    def down_weight(self, after):
        return _gather_end(self.st_later["down"], after, _gather_plan, "gather_down")[0].reshape(D_FF, D_MODEL)

    def _chips_start(self, slabs, from_sibling, rows, tag):
        sums = [_pair_add(s, r, self.csel, tr, f"pair_add_{tag}_{i}")
                for i, (s, r, tr) in enumerate(zip(slabs, from_sibling, rows))]
        lands = [lax.empty((3,) + s.shape[1:], s.dtype) for s in sums]
        self.reduces[tag] = _remote_start(sums, lands, _chips_plan, 3 * len(sums), f"reduce_start_{tag}")
        return self.reduces[tag][4]

    def mlp_grads(self, h2, du, act, dx3b):
        def send(part, tag, after):
            st = _remote_start([part], [lax.empty(part.shape, part.dtype)], _pair4_plan, 4,
                               f"reduce_pair_start_{tag}", after=after)
            return st

        def received(st, after, tag):
            return _remote_wait(st, after, _pair4_plan, f"reduce_pair_wait_{tag}")[1][0]

        def to_chips(sums, tag):
            self.reduces[tag] = _remote_start([sums], [lax.empty((3,) + sums.shape[1:], sums.dtype)], _chips_plan, 3,
                                              f"reduce_start_{tag}")
            return self.reduces[tag][4]

        up_send = _grad_w_up(h2, du, "grad_w_up_send", sel=(self.csel, True))
        st_up = send(up_send, "up", None)
        down_send = _grad_w_down(act, dx3b, "grad_w_down_send", sel=(self.csel, True), after=st_up[4])
        st_down = send(down_send, "down", None)
        up_sum = _grad_w_up(h2, du, "grad_w_up_keep", sel=(self.csel, False), add=received(st_up, down_send, "up"),
                            after=st_down[4])
        token = to_chips(up_sum, "up")
        down_sum = _grad_w_down(act, dx3b, "grad_w_down_keep", sel=(self.csel, False),
                                add=received(st_down, up_sum, "down"), after=token)
        return to_chips(down_sum, "down")

    def out_grad(self, g_out):
        slabs = [g_out.reshape(N_DEV, D_MODEL // N_DEV, D_MODEL)]
        return self._chips_start(slabs, _exchange_pair(slabs, "reduce_pair_out"), [256], "out")

    def in_grad(self, g_in):
        per = IN_PROJ // N_DEV
        nat = jnp.concatenate([g_in[:NAT_DT], g_in[OFF_DT:OFF_DT + N_HEADS], g_in[NAT_DT:OFF_DT]], axis=0)
        slabs = [nat.reshape(N_DEV, per, D_MODEL)]
        return self._chips_start(slabs, _exchange_pair(slabs, "reduce_pair_in"), [per], "in")

    def small_start(self, small):
        self.st_small = _remote_start([small], [_landing(small, self.me)], _everyone_plan, N_DEV - 1, "gather_start_small")

    def small_end(self, after):
        return _remote_wait(self.st_small, after, _everyone_plan, "gather_small_wait")[1][0]

    def reduce_end(self, tag, after):
        return _remote_wait(self.reduces[tag], after, _chips_plan, f"reduce_wait_{tag}")


def kernel(x, mix_norm_g, w_in, conv_w, conv_b, dt_bias, A_log, D_skip, ssm_norm_g, attn_sinks, attn_out_norm_g, w_out, mlp_norm_g, w_up, w_down, final_norm_g, loss_target, m_mix_norm_g, m_w_in, m_conv_w, m_conv_b, m_dt_bias, m_A_log, m_D_skip, m_ssm_norm_g, m_attn_sinks, m_attn_out_norm_g, m_w_out, m_mlp_norm_g, m_w_up, m_w_down, m_final_norm_g, v_mix_norm_g, v_w_in, v_conv_w, v_conv_b, v_dt_bias, v_A_log, v_D_skip, v_ssm_norm_g, v_attn_sinks, v_attn_out_norm_g, v_w_out, v_mlp_norm_g, v_w_up, v_w_down, v_final_norm_g):
    xi, yi, ci = _coords()
    me = 4 * xi + 2 * yi + ci
    csel = jnp.reshape(ci, (1,)).astype(jnp.int32)
    qsel = jnp.reshape(2 * xi + yi, (1,)).astype(jnp.int32)
    w = dict(mix_norm_g=mix_norm_g, conv_b=conv_b, dt_bias=dt_bias, A_log=A_log, D_skip=D_skip,
             ssm_norm_g=ssm_norm_g, attn_sinks=attn_sinks, attn_out_norm_g=attn_out_norm_g, mlp_norm_g=mlp_norm_g,
             final_norm_g=final_norm_g)
    hooks = _ShardedWeights(w_in[0], w_out[0], conv_w[0], w_up[0], w_down[0], me, csel)
    p = dict(w, mix_norm_g=mix_norm_g + hooks.start_token[0:1, 0:1])
    dx, small = _local_step(x[0], loss_target[0], p, hooks)
    hooks.small_start(small)
    big = {}
    after = dx
    for name, wt, mt, vt, tile, transposed in [
            ("up", w_up, m_w_up, v_w_up, (512, SLAB), False), ("down", w_down, m_w_down, v_w_down, (256, D_MODEL), False),
            ("out", w_out, m_w_out, v_w_out, (256, D_MODEL), False),
            ("in", w_in, m_w_in, v_w_in, (IN_PROJ // N_DEV, 512), True)]:
        (chip_sums,), (from_chips,) = hooks.reduce_end(name, after)
        shard = [jnp.transpose(t[0]) if transposed else t[0] for t in (wt, mt, vt)]
        res = _adamw_big(*shard, chip_sums, from_chips, qsel, tile, f"adamw_w_{name}")
        big["w_" + name] = tuple((jnp.transpose(r) if transposed else r)[None] for r in res)
        after = res[0]
    gsum = _small_sum(hooks.small_end(after), "small_sum")
    loss = gsum[9, 64]
    gs = _unpack_small(gsum, CONV_DIM)
    cw = CONV_DIM // N_DEV
    g_conv_shard = lax.dynamic_slice(gsum[5:9, :], (0, me * cw), (CONV_K, cw))

    def pack(s):
        return _pack_small(s["mix_norm_g"], s["conv_b"], s["ssm_norm_g"], s["attn_out_norm_g"], s["mlp_norm_g"],
                           s["final_norm_g"], s["conv_w"][0], s["dt_bias"], s["A_log"], s["D_skip"], s["attn_sinks"])

    wp = pack(dict(w, conv_w=conv_w))
    mp = pack(dict(mix_norm_g=m_mix_norm_g, conv_b=m_conv_b, ssm_norm_g=m_ssm_norm_g,
                   attn_out_norm_g=m_attn_out_norm_g, mlp_norm_g=m_mlp_norm_g, final_norm_g=m_final_norm_g,
                   conv_w=m_conv_w, dt_bias=m_dt_bias, A_log=m_A_log, D_skip=m_D_skip, attn_sinks=m_attn_sinks))
    vp = pack(dict(mix_norm_g=v_mix_norm_g, conv_b=v_conv_b, ssm_norm_g=v_ssm_norm_g,
                   attn_out_norm_g=v_attn_out_norm_g, mlp_norm_g=v_mlp_norm_g, final_norm_g=v_final_norm_g,
                   conv_w=v_conv_w, dt_bias=v_dt_bias, A_log=v_A_log, D_skip=v_D_skip, attn_sinks=v_attn_sinks))
    gp = jnp.concatenate([gsum[0:5], jnp.pad(g_conv_shard, ((0, 0), (0, D_MODEL - cw))), gsum[9:10],
                          jnp.zeros((SMALL_ROWS - 10, D_MODEL), F32)], axis=0)
    dp, mnp, vnp = _adamw_small(wp, gp, mp, vp, "adamw_small")
    grads = dict(gs, conv_w=g_conv_shard[None])
    deltas = _unpack_small(dp, cw)
    new_m = _unpack_small(mnp, cw)
    new_v = _unpack_small(vnp, cw)
    for k, name in enumerate(["w_in", "w_out", "w_up", "w_down"]):
        grads[name], deltas[name], new_m[name], new_v[name] = big[name]
    return (loss, dx[None], *[grads[n] for n in WEIGHT_ORDER], *[deltas[n] for n in WEIGHT_ORDER],
            *[new_m[n] for n in WEIGHT_ORDER], *[new_v[n] for n in WEIGHT_ORDER])
```

```python
import functools

import jax
import jax.numpy as jnp
from jax import lax
from jax.experimental import pallas as pl
from jax.experimental.pallas import tpu as pltpu

F32 = jnp.float32
BF16 = jnp.bfloat16
HI = lax.Precision.HIGHEST
MESH = pl.DeviceIdType.MESH

EPS = 1e-5
D_MODEL = 2048
D_INNER = 1024
N_HEADS = 16
HEAD_DIM = 64
N_GROUPS = 4
D_STATE = 128
CHUNK = 128
CONV_K = 4
CONV_DIM = 2048
ATTN_W = 1024
KV_W = 128
WINDOW = 128
D_FF = 8192
IN_PROJ = 4368
N_DEV = 8
NP = 4608
OFF_Z, OFF_X, OFF_B, OFF_C, OFF_Q, OFF_K, OFF_V, OFF_DT = 0, 1024, 2048, 2560, 3072, 4096, 4224, 4352
NAT_DT = 3072

ADAM_LR = 0.001
ADAM_B1 = 0.9
ADAM_B2 = 0.999
ADAM_EPS = 1e-08
ADAM_WD = 0.01
ADAM_STEP = 10

VMEM_LIMIT = 52 * 1024 * 1024
SMALL_ROWS = 16
NEG = -1e30


def _cparams(sem=None):
    return pltpu.CompilerParams(dimension_semantics=sem, vmem_limit_bytes=VMEM_LIMIT)


def _split3(v):
    hi = v.astype(BF16)
    rest = v - hi.astype(F32)
    mid = rest.astype(BF16)
    return hi, mid, (rest - mid.astype(F32)).astype(BF16)


def _hdot(a, b, data):
    if data == "a":
        sel = b.astype(BF16)
        return sum(_dot_nn(part, sel) for part in _split3(a))
    sel = a.astype(BF16)
    return sum(_dot_nn(sel, part) for part in _split3(b))


def _dot_nn(a, b):
    return lax.dot_general(a, b, (((1,), (0,)), ((), ())), preferred_element_type=F32)


def _dot_nt(a, b):
    return lax.dot_general(a, b, (((1,), (1,)), ((), ())), preferred_element_type=F32)


def _dot_tn(a, b):
    return lax.dot_general(a, b, (((0,), (0,)), ((), ())), preferred_element_type=F32)


def _softplus(v):
    return jnp.maximum(v, 0.0) + jnp.log1p(jnp.exp(-jnp.abs(v)))


def _sigmoid(v):
    return 1.0 / (1.0 + jnp.exp(-v))


def _matmul(a, b, *, mode, grid, a_spec, b_spec, out_shapes, out_specs, tile, name,
            extras=(), extra_specs=(), epilogue=None, after=None, dot_fn=None, prefetch=None):
    nk = grid[2]
    n_ex = len(extras)
    n_out = len(out_shapes)
    dot = dot_fn if dot_fn is not None else {"nn": _dot_nn, "nt": _dot_nt, "tn": _dot_tn}[mode]

    def finish(acc, ex_refs, out_refs):
        res = (acc,) if epilogue is None else epilogue(acc, *[e[...] for e in ex_refs])
        for o, r in zip(out_refs, res):
            o[...] = r.astype(o.dtype)

    def body(*refs):
        a_ref, b_ref = refs[0], refs[1]
        ex_refs = refs[2:2 + n_ex]
        out_refs = refs[2 + n_ex:2 + n_ex + n_out]
        part = dot(a_ref[...].astype(BF16), b_ref[...].astype(BF16))
        if nk == 1:
            finish(part, ex_refs, out_refs)
        else:
            acc_ref = refs[-1]
            k = pl.program_id(2)

            @pl.when(k == 0)
            def _():
                acc_ref[...] = part

            @pl.when(k > 0)
            def _():
                acc_ref[...] += part

            @pl.when(k == nk - 1)
            def _():
                finish(acc_ref[...], ex_refs, out_refs)

    scratch = [] if nk == 1 else [pltpu.VMEM(tile, F32)]
    n_pre = 0 if prefetch is None else 1
    tok_specs = [] if after is None else [pl.BlockSpec((8, 128), lambda *_: (0, 0))]
    tok_args = [] if after is None else [after]

    def body_with_token(*refs):
        refs = refs[n_pre:]
        body(*refs[:2 + n_ex], *refs[2 + n_ex + len(tok_args):])

    in_specs = [a_spec, b_spec, *extra_specs, *tok_specs]
    params = _cparams(("parallel", "parallel", "arbitrary"))
    if prefetch is None:
        return pl.pallas_call(
            body_with_token, grid=grid, in_specs=in_specs, out_specs=list(out_specs), out_shape=list(out_shapes),
            scratch_shapes=scratch, name=name, compiler_params=params)(a, b, *extras, *tok_args)
    return pl.pallas_call(
        body_with_token,
        grid_spec=pltpu.PrefetchScalarGridSpec(num_scalar_prefetch=1, grid=grid, in_specs=in_specs,
                                               out_specs=list(out_specs), scratch_shapes=scratch),
        out_shape=list(out_shapes), name=name, compiler_params=params)(prefetch, a, b, *extras, *tok_args)


def _mm_simple(a, b, *, mode, M, N, K, tm, tn, tk, out_dtype, name, extras=(), epilogue=None, n_out=1,
               out_dtypes=None, after=None):
    grid = (M // tm, N // tn, K // tk)
    if mode == "nn":
        a_spec = pl.BlockSpec((tm, tk), lambda i, j, k: (i, k))
        b_spec = pl.BlockSpec((tk, tn), lambda i, j, k: (k, j))
    elif mode == "nt":
        a_spec = pl.BlockSpec((tm, tk), lambda i, j, k: (i, k))
        b_spec = pl.BlockSpec((tn, tk), lambda i, j, k: (j, k))
    else:
        a_spec = pl.BlockSpec((tk, tm), lambda i, j, k: (k, i))
        b_spec = pl.BlockSpec((tk, tn), lambda i, j, k: (k, j))
    o_spec = pl.BlockSpec((tm, tn), lambda i, j, k: (i, j))
    dts = out_dtypes if out_dtypes is not None else [out_dtype] * n_out
    return _matmul(a, b, mode=mode, grid=grid, a_spec=a_spec, b_spec=b_spec,
                   out_shapes=[jax.ShapeDtypeStruct((M, N), d) for d in dts],
                   out_specs=[o_spec] * len(dts), tile=(tm, tn), name=name,
                   extras=extras, extra_specs=[o_spec] * len(extras), epilogue=epilogue, after=after)


ROW_BLOCK = 256


def _rmsnorm_fwd(x, g, name):
    T, D = x.shape

    def body(x_ref, g_ref, o_ref):
        xf = x_ref[...]
        r = lax.rsqrt(jnp.mean(xf * xf, axis=-1, keepdims=True) + EPS)
        o_ref[...] = (xf * r * g_ref[...]).astype(BF16)

    return pl.pallas_call(
        body, grid=(T // ROW_BLOCK,),
        in_specs=[pl.BlockSpec((ROW_BLOCK, D), lambda i: (i, 0)), pl.BlockSpec((1, D), lambda i: (0, 0))],
        out_specs=pl.BlockSpec((ROW_BLOCK, D), lambda i: (i, 0)),
        out_shape=jax.ShapeDtypeStruct((T, D), BF16), name=name, compiler_params=_cparams(("parallel",)),
    )(x, g)


def _rmsnorm_bwd(dh, x, g, dres, name):
    T, D = x.shape

    def body(dh_ref, x_ref, g_ref, dres_ref, dx_ref, dxb_ref, dg_ref):
        i = pl.program_id(0)
        xf = x_ref[...]
        r = lax.rsqrt(jnp.mean(xf * xf, axis=-1, keepdims=True) + EPS)
        xh = xf * r
        d = dh_ref[...]

        @pl.when(i == 0)
        def _():
            dg_ref[...] = jnp.zeros_like(dg_ref)

        dg_ref[...] += jnp.sum(d * xh, axis=0, keepdims=True)
        dxh = d * g_ref[...]
        dx = r * (dxh - xh * jnp.mean(dxh * xh, axis=-1, keepdims=True)) + dres_ref[...]
        dx_ref[...] = dx
        dxb_ref[...] = dx.astype(BF16)

    row = pl.BlockSpec((ROW_BLOCK, D), lambda i: (i, 0))
    vec = pl.BlockSpec((1, D), lambda i: (0, 0))
    return pl.pallas_call(
        body, grid=(T // ROW_BLOCK,), in_specs=[row, row, vec, row], out_specs=[row, row, vec],
        out_shape=[jax.ShapeDtypeStruct((T, D), F32), jax.ShapeDtypeStruct((T, D), BF16),
                   jax.ShapeDtypeStruct((1, D), F32)],
        name=name, compiler_params=_cparams(("arbitrary",)),
    )(dh, x, g, dres)


def _final_loss(x3, tgt, g, name):
    T, D = x3.shape

    def body(x_ref, t_ref, g_ref, loss_ref, dg_ref, dx_ref, dxb_ref):
        i = pl.program_id(0)
        xf = x_ref[...]
        r = lax.rsqrt(jnp.mean(xf * xf, axis=-1, keepdims=True) + EPS)
        xh = xf * r
        gg = g_ref[...]
        err = xh * gg - t_ref[...]

        @pl.when(i == 0)
        def _():
            dg_ref[...] = jnp.zeros_like(dg_ref)
            loss_ref[...] = jnp.zeros_like(loss_ref)

        part = jnp.sum(jnp.sum(err * err, axis=-1, keepdims=True), axis=0, keepdims=True) * (0.5 / D)
        loss_ref[...] += jnp.broadcast_to(part, loss_ref.shape)
        dout = err * (1.0 / D)
        dg_ref[...] += jnp.sum(dout * xh, axis=0, keepdims=True)
        dxh = dout * gg
        dx = r * (dxh - xh * jnp.mean(dxh * xh, axis=-1, keepdims=True))
        dx_ref[...] = dx
        dxb_ref[...] = dx.astype(BF16)

    row = pl.BlockSpec((ROW_BLOCK, D), lambda i: (i, 0))
    vec = pl.BlockSpec((1, D), lambda i: (0, 0))
    return pl.pallas_call(
        body, grid=(T // ROW_BLOCK,), in_specs=[row, row, vec],
        out_specs=[pl.BlockSpec((1, 128), lambda i: (0, 0)), vec, row, row],
        out_shape=[jax.ShapeDtypeStruct((1, 128), F32), jax.ShapeDtypeStruct((1, D), F32),
                   jax.ShapeDtypeStruct((T, D), F32), jax.ShapeDtypeStruct((T, D), BF16)],
        name=name, compiler_params=_cparams(("arbitrary",)),
    )(x3, tgt, g)


CONV_BLOCK = 256


def _conv_apply(u, w, b):
    row = lax.broadcasted_iota(jnp.int32, u.shape, 0)
    acc = b + w[CONV_K - 1:CONV_K, :] * u
    shifted = []
    for j in range(1, CONV_K):
        uj = jnp.where(row >= j, pltpu.roll(u, j, axis=0), 0.0)
        shifted.append(uj)
        acc = acc + w[CONV_K - 1 - j:CONV_K - j, :] * uj
    return acc, shifted


def _conv_fwd(proj, conv_w, conv_b, name):
    T = proj.shape[0]
    cb0 = OFF_X // CONV_BLOCK

    def body(u_ref, w_ref, b_ref, o_ref):
        c, _ = _conv_apply(u_ref[...], w_ref[...], b_ref[...])
        o_ref[...] = c * _sigmoid(c)

    return pl.pallas_call(
        body, grid=(CONV_DIM // CONV_BLOCK,),
        in_specs=[pl.BlockSpec((T, CONV_BLOCK), lambda j: (0, cb0 + j)),
                  pl.BlockSpec((CONV_K, CONV_BLOCK), lambda j: (0, j)),
                  pl.BlockSpec((1, CONV_BLOCK), lambda j: (0, j))],
        out_specs=pl.BlockSpec((T, CONV_BLOCK), lambda j: (0, j)),
        out_shape=jax.ShapeDtypeStruct((T, CONV_DIM), F32), name=name, compiler_params=_cparams(("parallel",)),
    )(proj, conv_w, conv_b)


def _conv_bwd(proj, dact, conv_w, conv_b, name):
    T = proj.shape[0]
    cb0 = OFF_X // CONV_BLOCK

    def body(u_ref, d_ref, w_ref, b_ref, du_ref, dw_ref, db_ref):
        u = u_ref[...]
        w = w_ref[...]
        c, shifted = _conv_apply(u, w, b_ref[...])
        sg = _sigmoid(c)
        dc = d_ref[...] * sg * (1.0 + c * (1.0 - sg))
        row = lax.broadcasted_iota(jnp.int32, u.shape, 0)
        du = w[CONV_K - 1:CONV_K, :] * dc
        dw_ref[CONV_K - 1:CONV_K, :] = jnp.sum(dc * u, axis=0, keepdims=True)
        for j in range(1, CONV_K):
            dcj = jnp.where(row < T - j, pltpu.roll(dc, T - j, axis=0), 0.0)
            du = du + w[CONV_K - 1 - j:CONV_K - j, :] * dcj
            dw_ref[CONV_K - 1 - j:CONV_K - j, :] = jnp.sum(dc * shifted[j - 1], axis=0, keepdims=True)
        db_ref[...] = jnp.sum(dc, axis=0, keepdims=True)
        du_ref[...] = du.astype(BF16)

    return pl.pallas_call(
        body, grid=(CONV_DIM // CONV_BLOCK,),
        in_specs=[pl.BlockSpec((T, CONV_BLOCK), lambda j: (0, cb0 + j)),
                  pl.BlockSpec((T, CONV_BLOCK), lambda j: (0, j)),
                  pl.BlockSpec((CONV_K, CONV_BLOCK), lambda j: (0, j)),
                  pl.BlockSpec((1, CONV_BLOCK), lambda j: (0, j))],
        out_specs=[pl.BlockSpec((T, CONV_BLOCK), lambda j: (0, j)),
                   pl.BlockSpec((CONV_K, CONV_BLOCK), lambda j: (0, j)),
                   pl.BlockSpec((1, CONV_BLOCK), lambda j: (0, j))],
        out_shape=[jax.ShapeDtypeStruct((T, CONV_DIM), BF16), jax.ShapeDtypeStruct((CONV_K, CONV_DIM), F32),
                   jax.ShapeDtypeStruct((1, CONV_DIM), F32)],
        name=name, compiler_params=_cparams(("parallel",)),
    )(proj, dact, conv_w, conv_b)


GROUP_W = D_INNER // N_GROUPS
HEADS_PER_GROUP = N_HEADS // N_GROUPS


def _expand_mat():
    h = lax.broadcasted_iota(jnp.int32, (N_HEADS, D_INNER), 0)
    j = lax.broadcasted_iota(jnp.int32, (N_HEADS, D_INNER), 1)
    return (j // HEAD_DIM == h).astype(F32)


def _reduce_mat(g):
    j = lax.broadcasted_iota(jnp.int32, (GROUP_W, N_HEADS), 0)
    h = lax.broadcasted_iota(jnp.int32, (GROUP_W, N_HEADS), 1)
    return (g * HEADS_PER_GROUP + j // HEAD_DIM == h).astype(F32)


def _col16(v, h):
    lane = lax.broadcasted_iota(jnp.int32, v.shape, 1)
    return jnp.sum(jnp.where(lane == h, v, 0.0), axis=1, keepdims=True)


def _ssd_pre(dt_raw, dtT_raw, dtb, dtbT, alog, alogT):
    Q = CHUNK
    xdt = dt_raw + dtb
    dt = _softplus(xdt)
    dtT = _softplus(dtT_raw + dtbT)
    A = -jnp.exp(alog)
    AT = -jnp.exp(alogT)
    row = lax.broadcasted_iota(jnp.int32, (Q, Q), 0)
    col = lax.broadcasted_iota(jnp.int32, (Q, Q), 1)
    tril = (row >= col).astype(F32)
    triu = (row <= col).astype(F32)
    cs = _hdot(tril, dt * A, "b")
    csT = _hdot(dtT * AT, triu, "a")
    return xdt, dt, A, cs, csT, row >= col, triu


def _decay_matrix(cs, csT, h, causal):
    seg = _col16(cs, h) - csT[h:h + 1, :]
    return jnp.where(causal, jnp.exp(jnp.minimum(seg, 0.0)), 0.0)


def _ssd_in_specs(nc, rev):
    def cidx(c):
        return (nc - 1 - c) if rev else c

    return [
        pl.BlockSpec((CHUNK, D_INNER), lambda c: (cidx(c), 0)),
        pl.BlockSpec((CHUNK, 512), lambda c: (cidx(c), 2)),
        pl.BlockSpec((CHUNK, 512), lambda c: (cidx(c), 3)),
        pl.BlockSpec((CHUNK, D_INNER), lambda c: (cidx(c), 0)),
        pl.BlockSpec((CHUNK, 128), lambda c: (cidx(c), OFF_DT // 128)),
        pl.BlockSpec((N_HEADS, CHUNK), lambda c: (0, cidx(c))),
        pl.BlockSpec((1, N_HEADS), lambda c: (0, 0)),
        pl.BlockSpec((N_HEADS, 1), lambda c: (0, 0)),
        pl.BlockSpec((1, N_HEADS), lambda c: (0, 0)),
        pl.BlockSpec((N_HEADS, 1), lambda c: (0, 0)),
        pl.BlockSpec((1, D_INNER), lambda c: (0, 0)),
        pl.BlockSpec((1, D_INNER), lambda c: (0, 0)),
    ]


def _ssd_fwd(xbc, proj, dtT, dtb, dtbT, alog, alogT, dfull, ng, name):
    T = xbc.shape[0]
    nc = T // CHUNK
    Q = CHUNK

    def body(xs_ref, B_ref, C_ref, z_ref, dt_ref, dtT_ref, dtb_ref, dtbT_ref, al_ref, alT_ref, df_ref, ng_ref,
             y_ref, ypre_ref, hs_ref, h_scr):
        c = pl.program_id(0)

        @pl.when(c == 0)
        def _():
            h_scr[...] = jnp.zeros_like(h_scr)

        _, dt, _, cs, csT, causal, _ = _ssd_pre(dt_ref[:, :N_HEADS], dtT_ref[...], dtb_ref[...], dtbT_ref[...],
                                                al_ref[...], alT_ref[...])
        ex = _expand_mat()
        dt_full = _hdot(dt, ex, "a")
        cs_full = _hdot(cs, ex, "a")
        cs_last = cs_full[Q - 1:Q, :]
        xs = xs_ref[...]
        xd = xs * dt_full
        e_full = jnp.exp(cs_full)
        dec_full = jnp.exp(cs_last - cs_full)
        cd_full = jnp.exp(cs_last)
        lane_head = lax.broadcasted_iota(jnp.int32, (1, GROUP_W), 1) // HEAD_DIM
        for g in range(N_GROUPS):
            sl = slice(g * GROUP_W, (g + 1) * GROUP_W)
            Bg = B_ref[:, g * D_STATE:(g + 1) * D_STATE].astype(BF16)
            Cg = C_ref[:, g * D_STATE:(g + 1) * D_STATE].astype(BF16)
            CB = _dot_nt(Cg, Bg)
            hg = h_scr[g]
            yoff = _dot_nn(Cg, hg.astype(BF16)) * e_full[:, sl]
            xd_g = xd[:, sl]
            S = _dot_tn(Bg, (xd_g * dec_full[:, sl]).astype(BF16))
            xd_b = xd_g.astype(BF16)
            ydiag = jnp.zeros((Q, GROUP_W), F32)
            for r in range(HEADS_PER_GROUP):
                Lm = _decay_matrix(cs, csT, g * HEADS_PER_GROUP + r, causal)
                Gm = (CB * Lm).astype(BF16)
                ydiag = ydiag + _dot_nn(Gm, jnp.where(lane_head == r, xd_b, jnp.zeros_like(xd_b)))
            hs_ref[0, g] = hg
            h_scr[g] = hg * cd_full[:, sl] + S
            ypre = ydiag + yoff + xs[:, sl] * df_ref[:, sl]
            ypre_ref[:, sl] = ypre
            zg = z_ref[:, sl]
            yz = ypre * zg * _sigmoid(zg)
            rn = lax.rsqrt(jnp.mean(yz * yz, axis=-1, keepdims=True) + EPS)
            y_ref[:, sl] = (yz * rn * ng_ref[:, sl]).astype(BF16)

    return pl.pallas_call(
        body, grid=(nc,), in_specs=_ssd_in_specs(nc, False),
        out_specs=[pl.BlockSpec((CHUNK, D_INNER), lambda c: (c, 0)),
                   pl.BlockSpec((CHUNK, D_INNER), lambda c: (c, 0)),
                   pl.BlockSpec((1, N_GROUPS, D_STATE, GROUP_W), lambda c: (c, 0, 0, 0))],
        out_shape=[jax.ShapeDtypeStruct((T, D_INNER + ATTN_W), BF16), jax.ShapeDtypeStruct((T, D_INNER), F32),
                   jax.ShapeDtypeStruct((nc, N_GROUPS, D_STATE, GROUP_W), F32)],
        scratch_shapes=[pltpu.VMEM((N_GROUPS, D_STATE, GROUP_W), F32)],
        name=name, compiler_params=_cparams(("arbitrary",)),
    )(xbc, xbc, xbc, proj, proj, dtT, dtb, dtbT, alog, alogT, dfull, ng)


def _ssd_bwd(xbc, proj, dtT, dtb, dtbT, alog, alogT, dfull, ng, ypre, hs, dy, name):
    T = xbc.shape[0]
    nc = T // CHUNK
    Q = CHUNK

    def body(xs_ref, B_ref, C_ref, z_ref, dt_ref, dtT_ref, dtb_ref, dtbT_ref, al_ref, alT_ref, df_ref, ng_ref,
             ypre_ref, hs_ref, dy_ref,
             dz_ref, dxbc_ref, ddt_ref, ddtb_ref, dal_ref, dD_ref, dng_ref, dh_scr):
        step = pl.program_id(0)

        @pl.when(step == 0)
        def _():
            dh_scr[...] = jnp.zeros_like(dh_scr)
            ddtb_ref[...] = jnp.zeros_like(ddtb_ref)
            dal_ref[...] = jnp.zeros_like(dal_ref)
            dD_ref[...] = jnp.zeros_like(dD_ref)
            dng_ref[...] = jnp.zeros_like(dng_ref)

        xdt, dt, A, cs, csT, causal, triu = _ssd_pre(dt_ref[:, :N_HEADS], dtT_ref[...], dtb_ref[...],
                                                    dtbT_ref[...], al_ref[...], alT_ref[...])
        ex = _expand_mat()
        dt_full = _hdot(dt, ex, "a")
        cs_full = _hdot(cs, ex, "a")
        cs_last = cs_full[Q - 1:Q, :]
        xs = xs_ref[...]
        xd = xs * dt_full
        e_full = jnp.exp(cs_full)
        dec_full = jnp.exp(cs_last - cs_full)
        cd_full = jnp.exp(cs_last)
        lane_head = lax.broadcasted_iota(jnp.int32, (1, GROUP_W), 1) // HEAD_DIM
        is_last = lax.broadcasted_iota(jnp.int32, (Q, 1), 0) == Q - 1
        dcs16 = jnp.zeros((Q, N_HEADS), F32)
        ddtx16 = jnp.zeros((Q, N_HEADS), F32)
        dD16 = jnp.zeros((8, N_HEADS), F32)
        lane16 = lax.broadcasted_iota(jnp.int32, (1, N_HEADS), 1)
        sub16 = lax.broadcasted_iota(jnp.int32, (N_HEADS, 1), 0)
        col_sums = jnp.zeros((N_HEADS, Q), F32)
        for g in range(N_GROUPS):
            sl = slice(g * GROUP_W, (g + 1) * GROUP_W)
            red = _reduce_mat(g)
            ypre_g = ypre_ref[:, sl]
            zg = z_ref[:, sl]
            sg = _sigmoid(zg)
            silu = zg * sg
            yz = ypre_g * silu
            rn = lax.rsqrt(jnp.mean(yz * yz, axis=-1, keepdims=True) + EPS)
            yh = yz * rn
            dy_g = dy_ref[:, sl]
            dng_ref[:, sl] += jnp.sum(dy_g * yh, axis=0, keepdims=True)
            dyh = dy_g * ng_ref[:, sl]
            dyz = rn * (dyh - yh * jnp.mean(dyh * yh, axis=-1, keepdims=True))
            dY = dyz * silu
            dz_ref[:, sl] = (dyz * ypre_g * sg * (1.0 + zg * (1.0 - sg))).astype(BF16)
            xs_g = xs[:, sl]
            xd_g = xd[:, sl]
            dec_g = dec_full[:, sl]
            cd_g = cd_full[:, sl]
            d_g = df_ref[:, sl]
            Bg = B_ref[:, g * D_STATE:(g + 1) * D_STATE].astype(BF16)
            Cg = C_ref[:, g * D_STATE:(g + 1) * D_STATE].astype(BF16)
            CB = _dot_nt(Cg, Bg)
            hg = hs_ref[0, g]
            hgb = hg.astype(BF16)
            yoff = _dot_nn(Cg, hgb) * e_full[:, sl]
            dhn = dh_scr[g]
            dhnb = dhn.astype(BF16)
            dYE = (dY * e_full[:, sl]).astype(BF16)
            dC = _dot_nt(dYE, hgb)
            dh_direct = _dot_tn(Cg, dYE)
            dXdd = _dot_nn(Bg, dhnb)
            dB = _dot_nt((xd_g * dec_g).astype(BF16), dhnb)
            dcd = jnp.sum(dhn * hg, axis=0, keepdims=True)
            dh_scr[g] = dh_direct + cd_g * dhn
            dYb = dY.astype(BF16)
            xd_b = xd_g.astype(BF16)
            dCB = jnp.zeros((Q, Q), F32)
            dXd = dXdd * dec_g
            for r in range(HEADS_PER_GROUP):
                h = g * HEADS_PER_GROUP + r
                Lm = _decay_matrix(cs, csT, h, causal)
                Gf = CB * Lm
                dYr = jnp.where(lane_head == r, dYb, jnp.zeros_like(dYb))
                dG = _dot_nt(dYr, xd_b)
                dCB = dCB + dG * Lm
                dXd = dXd + _dot_tn(Gf.astype(BF16), dYr)
                Mm = dG * Gf
                dcs16 = dcs16 + jnp.where(lane16 == h, jnp.sum(Mm, axis=1, keepdims=True), 0.0)
                col_sums = col_sums + jnp.where(sub16 == h, jnp.sum(Mm, axis=0, keepdims=True), 0.0)
            dCBb = dCB.astype(BF16)
            dC = dC + _dot_nn(dCBb, Bg)
            dB = dB + _dot_tn(dCBb, Cg)
            w_state = dXdd * dec_g * xd_g
            t_last = jnp.sum(w_state, axis=0, keepdims=True) + dcd * cd_g
            dcs_g = dY * yoff - w_state + jnp.where(is_last, t_last, 0.0)
            dcs16 = dcs16 + _hdot(dcs_g, red, "a")
            ddtx16 = ddtx16 + _hdot(dXd * xs_g, red, "a")
            dD16 = dD16 + _hdot(jnp.broadcast_to(jnp.sum(dY * xs_g, axis=0, keepdims=True), (8, GROUP_W)), red, "a")
            dxbc_ref[:, sl] = dXd * dt_full[:, sl] + dY * d_g
            dxbc_ref[:, D_INNER + g * D_STATE:D_INNER + (g + 1) * D_STATE] = dB
            dxbc_ref[:, D_INNER + 512 + g * D_STATE:D_INNER + 512 + (g + 1) * D_STATE] = dC
        eye = (lax.broadcasted_iota(jnp.int32, (N_HEADS, N_HEADS), 0)
               == lax.broadcasted_iota(jnp.int32, (N_HEADS, N_HEADS), 1)).astype(BF16)
        dcs16 = dcs16 - sum(_dot_tn(part, eye) for part in _split3(col_sums))
        da = _hdot(triu, dcs16, "b")
        ddt = da * A + ddtx16
        ddt_raw = ddt * _sigmoid(xdt)
        pr = lax.broadcasted_iota(jnp.int32, (N_HEADS, 128), 0)
        pc = lax.broadcasted_iota(jnp.int32, (N_HEADS, 128), 1)
        ddt_ref[...] = _hdot(ddt_raw, (pr == pc).astype(F32), "a")
        ddtb_ref[...] += jnp.sum(ddt_raw, axis=0, keepdims=True)
        dal_ref[...] += jnp.sum(da * dt, axis=0, keepdims=True) * A
        dD_ref[...] += dD16[0:1, :]

    def rc(c):
        return nc - 1 - c

    in_specs = _ssd_in_specs(nc, True) + [
        pl.BlockSpec((CHUNK, D_INNER), lambda c: (rc(c), 0)),
        pl.BlockSpec((1, N_GROUPS, D_STATE, GROUP_W), lambda c: (rc(c), 0, 0, 0)),
        pl.BlockSpec((CHUNK, D_INNER), lambda c: (rc(c), 0)),
    ]
    small = pl.BlockSpec((1, N_HEADS), lambda c: (0, 0))
    return pl.pallas_call(
        body, grid=(nc,), in_specs=in_specs,
        out_specs=[pl.BlockSpec((CHUNK, D_INNER), lambda c: (rc(c), 0)),
                   pl.BlockSpec((CHUNK, CONV_DIM), lambda c: (rc(c), 0)),
                   pl.BlockSpec((CHUNK, 128), lambda c: (rc(c), 0)),
                   small, small, small,
                   pl.BlockSpec((1, D_INNER), lambda c: (0, 0))],
        out_shape=[jax.ShapeDtypeStruct((T, D_INNER), BF16), jax.ShapeDtypeStruct((T, CONV_DIM), F32),
                   jax.ShapeDtypeStruct((T, 128), F32),
                   jax.ShapeDtypeStruct((1, N_HEADS), F32), jax.ShapeDtypeStruct((1, N_HEADS), F32),
                   jax.ShapeDtypeStruct((1, N_HEADS), F32), jax.ShapeDtypeStruct((1, D_INNER), F32)],
        scratch_shapes=[pltpu.VMEM((N_GROUPS, D_STATE, GROUP_W), F32)],
        name=name, compiler_params=_cparams(("arbitrary",)),
    )(xbc, xbc, xbc, proj, proj, dtT, dtb, dtbT, alog, alogT, dfull, ng, ypre, hs, dy)


N_PAIRS = ATTN_W // 128
PAIRS_PER_KV = N_PAIRS // 2
ATTN_SCALE = HEAD_DIM ** -0.5


def _kv_variants(kk):
    lo = lax.broadcasted_iota(jnp.int32, kk.shape, 1) < HEAD_DIM
    zero = jnp.zeros_like(kk)
    k00 = jnp.where(lo, kk, zero)
    k11 = jnp.where(lo, zero, kk)
    k01 = pltpu.roll(k00, HEAD_DIM, axis=1)
    k10 = pltpu.roll(k11, HEAD_DIM, axis=1)
    return [[k00.astype(BF16), k01.astype(BF16)], [k10.astype(BF16), k11.astype(BF16)]]


def _attn_valid(n):
    i = lax.broadcasted_iota(jnp.int32, (WINDOW, 2 * WINDOW), 0)
    j = lax.broadcasted_iota(jnp.int32, (WINDOW, 2 * WINDOW), 1)
    return (j > i) & (j <= i + WINDOW) & (n * WINDOW + j >= WINDOW)


def _attn_probs(qp, kvar, valid, sk):
    s = _dot_nt(qp, kvar) * ATTN_SCALE
    s = jnp.where(valid, s, NEG)
    m = jnp.maximum(jnp.max(s, axis=1, keepdims=True), sk)
    pe = jnp.exp(s - m)
    es = jnp.exp(sk - m)
    den = jnp.sum(pe, axis=1, keepdims=True) + es
    inv = 1.0 / den
    return pe * inv, es * inv


def _sink(sinks, r):
    lane = lax.broadcasted_iota(jnp.int32, sinks.shape, 1)
    return jnp.sum(jnp.where(lane == r, sinks, 0.0), axis=1, keepdims=True)


def _attn_fwd(proj, kpad, vpad, sinks, og, ycat, name):
    T = proj.shape[0]
    nb = T // WINDOW

    def body(q_ref, k_ref, v_ref, s_ref, og_ref, _, y_ref, o_ref):
        n = pl.program_id(0)
        start = pl.multiple_of(n * WINDOW, WINDOW)
        kv = _kv_variants(k_ref[pl.ds(start, 2 * WINDOW), :])
        vv = _kv_variants(v_ref[pl.ds(start, 2 * WINDOW), :])
        valid = _attn_valid(n)
        sinks_v = s_ref[...]
        ssq = jnp.zeros((WINDOW, 1), F32)
        for p in range(N_PAIRS):
            j = p // PAIRS_PER_KV
            qp = q_ref[:, p * 128:(p + 1) * 128].astype(BF16)
            o_pair = jnp.zeros((WINDOW, 128), F32)
            for par in range(2):
                pn, _ = _attn_probs(qp, kv[j][par], valid, _sink(sinks_v, 2 * p + par))
                o_pair = o_pair + _dot_nn(pn.astype(BF16), vv[j][par])
            o_ref[:, p * 128:(p + 1) * 128] = o_pair
            ssq = ssq + jnp.sum(o_pair * o_pair, axis=1, keepdims=True)
        rn = lax.rsqrt(ssq * (1.0 / ATTN_W) + EPS)
        y_ref[...] = (o_ref[...] * rn * og_ref[...]).astype(BF16)

    full_kv = pl.BlockSpec((T + WINDOW, KV_W), lambda n: (0, 0))
    return pl.pallas_call(
        body, grid=(nb,),
        in_specs=[pl.BlockSpec((WINDOW, ATTN_W), lambda n: (n, OFF_Q // ATTN_W)), full_kv, full_kv,
                  pl.BlockSpec((1, N_HEADS), lambda n: (0, 0)), pl.BlockSpec((1, ATTN_W), lambda n: (0, 0)), ANY],
        out_specs=[pl.BlockSpec((WINDOW, ATTN_W), lambda n: (n, 1)), pl.BlockSpec((WINDOW, ATTN_W), lambda n: (n, 0))],
        out_shape=[jax.ShapeDtypeStruct(ycat.shape, BF16), jax.ShapeDtypeStruct((T, ATTN_W), F32)],
        input_output_aliases={5: 0}, name=name, compiler_params=_cparams(("parallel",)),
    )(proj, kpad, vpad, sinks, og, ycat)


def _attn_bwd(proj, kpad, vpad, sinks, og, o, dy, name):
    T = proj.shape[0]
    nb = T // WINDOW

    def body(q_ref, k_ref, v_ref, s_ref, og_ref, o_ref, dy_ref, dq_ref, dk_ref, dv_ref, ds_ref, dog_ref):
        n = pl.program_id(0)

        @pl.when(n == 0)
        def _():
            dk_ref[...] = jnp.zeros_like(dk_ref)
            dv_ref[...] = jnp.zeros_like(dv_ref)
            ds_ref[...] = jnp.zeros_like(ds_ref)
            dog_ref[...] = jnp.zeros_like(dog_ref)

        start = pl.multiple_of(n * WINDOW, WINDOW)
        kv = _kv_variants(k_ref[pl.ds(start, 2 * WINDOW), :])
        vv = _kv_variants(v_ref[pl.ds(start, 2 * WINDOW), :])
        valid = _attn_valid(n)
        sinks_v = s_ref[...]
        of = o_ref[...]
        rn = lax.rsqrt(jnp.mean(of * of, axis=-1, keepdims=True) + EPS)
        oh = of * rn
        dyf = dy_ref[...]
        dog_ref[...] += jnp.sum(dyf * oh, axis=0, keepdims=True)
        doh = dyf * og_ref[...]
        do = rn * (doh - oh * jnp.mean(doh * oh, axis=-1, keepdims=True))
        lane = lax.broadcasted_iota(jnp.int32, (1, 128), 1)
        lane16 = lax.broadcasted_iota(jnp.int32, (1, N_HEADS), 1)
        dk_acc = [[jnp.zeros((2 * WINDOW, 128), F32) for _ in range(2)] for _ in range(2)]
        dv_acc = [[jnp.zeros((2 * WINDOW, 128), F32) for _ in range(2)] for _ in range(2)]
        dsink = jnp.zeros((1, N_HEADS), F32)
        for p in range(N_PAIRS):
            j = p // PAIRS_PER_KV
            qp = q_ref[:, p * 128:(p + 1) * 128].astype(BF16)
            do_p = do[:, p * 128:(p + 1) * 128]
            o_p = of[:, p * 128:(p + 1) * 128]
            do_b = do_p.astype(BF16)
            prod = do_p * o_p
            dq_pair = jnp.zeros((WINDOW, 128), F32)
            for par in range(2):
                r = 2 * p + par
                half = (lane < HEAD_DIM) if par == 0 else (lane >= HEAD_DIM)
                pn, ps = _attn_probs(qp, kv[j][par], valid, _sink(sinks_v, r))
                delta = jnp.sum(jnp.where(half, prod, 0.0), axis=1, keepdims=True)
                dP = _dot_nt(do_b, vv[j][par])
                dS = pn * (dP - delta)
                dsink = dsink + jnp.where(lane16 == r, -jnp.sum(ps * delta, axis=0, keepdims=True), 0.0)
                dSb = (dS * ATTN_SCALE).astype(BF16)
                dq_pair = dq_pair + _dot_nn(dSb, kv[j][par])
                dk_acc[j][par] = dk_acc[j][par] + _dot_tn(dSb, jnp.where(half, qp, jnp.zeros_like(qp)))
                dv_acc[j][par] = dv_acc[j][par] + _dot_tn(pn.astype(BF16), jnp.where(half, do_b, jnp.zeros_like(do_b)))
            dq_ref[:, p * 128:(p + 1) * 128] = dq_pair.astype(BF16)
        dkk = (dk_acc[0][0] + pltpu.roll(dk_acc[0][1], HEAD_DIM, axis=1)
               + dk_acc[1][1] + pltpu.roll(dk_acc[1][0], HEAD_DIM, axis=1))
        dvv = (dv_acc[0][0] + pltpu.roll(dv_acc[0][1], HEAD_DIM, axis=1)
               + dv_acc[1][1] + pltpu.roll(dv_acc[1][0], HEAD_DIM, axis=1))
        dk_ref[pl.ds(start, 2 * WINDOW), :] += dkk
        dv_ref[pl.ds(start, 2 * WINDOW), :] += dvv
        ds_ref[...] += dsink

    full_kv = pl.BlockSpec((T + WINDOW, KV_W), lambda n: (0, 0))
    blk = pl.BlockSpec((WINDOW, ATTN_W), lambda n: (n, 0))
    return pl.pallas_call(
        body, grid=(nb,),
        in_specs=[pl.BlockSpec((WINDOW, ATTN_W), lambda n: (n, OFF_Q // ATTN_W)), full_kv, full_kv,
                  pl.BlockSpec((1, N_HEADS), lambda n: (0, 0)), pl.BlockSpec((1, ATTN_W), lambda n: (0, 0)),
                  blk, pl.BlockSpec((WINDOW, ATTN_W), lambda n: (n, 1))],
        out_specs=[blk, full_kv, full_kv, pl.BlockSpec((1, N_HEADS), lambda n: (0, 0)),
                   pl.BlockSpec((1, ATTN_W), lambda n: (0, 0))],
        out_shape=[jax.ShapeDtypeStruct((T, ATTN_W), BF16), jax.ShapeDtypeStruct((T + WINDOW, KV_W), F32),
                   jax.ShapeDtypeStruct((T + WINDOW, KV_W), F32), jax.ShapeDtypeStruct((1, N_HEADS), F32),
                   jax.ShapeDtypeStruct((1, ATTN_W), F32)],
        name=name, compiler_params=_cparams(("arbitrary",)),
    )(proj, kpad, vpad, sinks, og, o, dy)


ANY = pl.BlockSpec(memory_space=pl.ANY)


def _coords():
    return lax.axis_index("x"), lax.axis_index("y"), lax.axis_index("c")


def _all_gather(arrs, name):
    n = len(arrs)

    def body(*refs):
        ins, outs = refs[:n], refs[n:2 * n]
        send_sems, recv_sems, local_sems = refs[2 * n:]
        x, y, c = _coords()
        me = 4 * x + 2 * y + c
        sibling = (x, y, 1 - c)
        chips = [(1 - x, y), (x, 1 - y), (1 - x, 1 - y)]

        def copy(a, k, block, to, src=None):
            dst = outs[a].at[block]
            return pltpu.make_async_remote_copy(
                src_ref=dst if src is None else src, dst_ref=dst, send_sem=send_sems.at[a, k],
                recv_sem=recv_sems.at[a, k], device_id=to, device_id_type=MESH)

        mine = [pltpu.make_async_copy(ins[a], outs[a].at[me], local_sems.at[a]) for a in range(n)]
        for cp in mine:
            cp.start()
        first = []
        for a in range(n):
            first.append(copy(a, 0, me, sibling, src=ins[a]))
            for j, chip in enumerate(chips):
                first.append(copy(a, 1 + j, me, (*chip, c), src=ins[a]))
        for cp in first:
            cp.start()
        passed = []
        for j, (px, py) in enumerate(chips):
            blk = 4 * px + 2 * py + c
            for a in range(n):
                copy(a, 1 + j, blk, sibling).wait_recv()
                fwd = copy(a, 4 + j, blk, sibling)
                fwd.start()
                passed.append(fwd)
        for a in range(n):
            copy(a, 0, 4 * x + 2 * y + (1 - c), sibling).wait_recv()
            for j, (px, py) in enumerate(chips):
                copy(a, 4 + j, 4 * px + 2 * py + (1 - c), sibling).wait_recv()
        for cp in first + passed:
            cp.wait_send()
        for cp in mine:
            cp.wait()

    return pl.pallas_call(
        body, in_specs=[ANY] * n, out_specs=[ANY] * n,
        out_shape=[jax.ShapeDtypeStruct((N_DEV,) + a.shape, a.dtype) for a in arrs],
        scratch_shapes=[pltpu.SemaphoreType.DMA((n, 7)), pltpu.SemaphoreType.DMA((n, 7)),
                        pltpu.SemaphoreType.DMA((n,))],
        name=name,
    )(*arrs)


def _exchange_pair(arrs, name):
    n = len(arrs)

    def body(*refs):
        ins, outs = refs[:n], refs[n:2 * n]
        send_sems, recv_sems = refs[2 * n:]
        x, y, c = _coords()
        cps = []
        for a in range(n):
            for q in range(4):
                cps.append(pltpu.make_async_remote_copy(
                    src_ref=ins[a].at[2 * q + (1 - c)], dst_ref=outs[a].at[q], send_sem=send_sems.at[a, q],
                    recv_sem=recv_sems.at[a, q], device_id=(x, y, 1 - c), device_id_type=MESH))
        for cp in cps:
            cp.start()
        for cp in cps:
            cp.wait()

    return pl.pallas_call(
        body, in_specs=[ANY] * n, out_specs=[ANY] * n,
        out_shape=[jax.ShapeDtypeStruct((4,) + a.shape[1:], a.dtype) for a in arrs],
        scratch_shapes=[pltpu.SemaphoreType.DMA((n, 4)), pltpu.SemaphoreType.DMA((n, 4))],
        name=name,
    )(*arrs)


def _exchange_chips(arrs, name):
    n = len(arrs)

    def body(*refs):
        ins, outs = refs[:n], refs[n:2 * n]
        send_sems, recv_sems = refs[2 * n:]
        x, y, c = _coords()
        chips = [(1 - x, y), (x, 1 - y), (1 - x, 1 - y)]
        cps = []
        for a in range(n):
            for k, (tx, ty) in enumerate(chips):
                cps.append(pltpu.make_async_remote_copy(
                    src_ref=ins[a].at[2 * tx + ty], dst_ref=outs[a].at[k], send_sem=send_sems.at[a, k],
                    recv_sem=recv_sems.at[a, k], device_id=(tx, ty, c), device_id_type=MESH))
        for cp in cps:
            cp.start()
        for cp in cps:
            cp.wait()

    return pl.pallas_call(
        body, in_specs=[ANY] * n, out_specs=[ANY] * n,
        out_shape=[jax.ShapeDtypeStruct((3,) + a.shape[1:], a.dtype) for a in arrs],
        scratch_shapes=[pltpu.SemaphoreType.DMA((n, 3)), pltpu.SemaphoreType.DMA((n, 3))],
        name=name,
    )(*arrs)


HBM = pl.BlockSpec(memory_space=pltpu.HBM)
SEM = pl.BlockSpec(memory_space=pltpu.SEMAPHORE)
EFFECT = pltpu.SideEffectType.DATAFLOW_SIDE_EFFECTING


def _in_hbm(a):
    return pltpu.with_memory_space_constraint(a, pltpu.HBM)


def _remote_start(srcs, lands, plan, n_copies, name, after=None):
    n = len(srcs)
    n_after = 0 if after is None else 1

    def body(*refs):
        src_refs, land_refs = refs[:n], refs[n:2 * n]
        send_sems, recv_sems = refs[2 * n + n_after], refs[2 * n + n_after + 1]
        token = refs[-1]
        x, y, c = _coords()
        for i, (sv, dv, dev) in enumerate(plan(src_refs, land_refs, x, y, c)):
            pltpu.make_async_remote_copy(src_ref=sv, dst_ref=dv, send_sem=send_sems.at[i], recv_sem=recv_sems.at[i],
                                         device_id=dev, device_id_type=MESH).start()
        token[...] = jnp.zeros_like(token)

    bufs = list(srcs) + list(lands)
    outs = pl.pallas_call(
        body, name=name,
        out_shape=(pltpu.SemaphoreType.DMA((n_copies,)), pltpu.SemaphoreType.DMA((n_copies,)),
                   *[pltpu.HBM(b.shape, b.dtype) for b in bufs], jax.ShapeDtypeStruct((8, 128), F32)),
        in_specs=[HBM] * (2 * n) + [ANY] * n_after,
        out_specs=(SEM, SEM, *[HBM] * (2 * n), pl.BlockSpec(memory_space=pltpu.VMEM)),
        input_output_aliases={i: 2 + i for i in range(2 * n)},
        compiler_params=pltpu.CompilerParams(has_side_effects=EFFECT),
    )(*[_in_hbm(b) for b in bufs], *([] if after is None else [after]))
    return outs[0], outs[1], list(outs[2:2 + n]), list(outs[2 + n:2 + 2 * n]), outs[-1]


def _remote_wait(started, after, plan, name):
    send_sems, recv_sems, srcs, lands, _ = started
    n = len(srcs)

    def body(*refs):
        src_refs, land_refs = refs[:n], refs[n:2 * n]
        send_sems, recv_sems = refs[2 * n], refs[2 * n + 1]
        x, y, c = _coords()
        for i, (sv, dv, dev) in enumerate(plan(src_refs, land_refs, x, y, c)):
            cp = pltpu.make_async_remote_copy(src_ref=sv, dst_ref=dv, send_sem=send_sems.at[i],
                                              recv_sem=recv_sems.at[i], device_id=dev, device_id_type=MESH)
            cp.wait_send()
            cp.wait_recv()

    bufs = list(srcs) + list(lands)
    outs = pl.pallas_call(
        body, name=name, out_shape=tuple(pltpu.HBM(b.shape, b.dtype) for b in bufs),
        in_specs=[HBM] * (2 * n) + [SEM, SEM, ANY], out_specs=tuple([HBM] * (2 * n)),
        input_output_aliases={i: i for i in range(2 * n)},
        compiler_params=pltpu.CompilerParams(has_side_effects=EFFECT),
    )(*bufs, send_sems, recv_sems, after)
    return list(outs[:n]), list(outs[n:])


def _gather_plan(src_refs, land_refs, x, y, c):
    me = 4 * x + 2 * y + c
    plan = []
    for s, l in zip(src_refs, land_refs):
        for dev in [(x, y, 1 - c), (1 - x, y, c), (x, 1 - y, c), (1 - x, 1 - y, c)]:
            plan.append((s, l.at[me], dev))
    return plan


def _pair_plan(src_refs, land_refs, x, y, c):
    plan = []
    for s, l in zip(src_refs, land_refs):
        for q in range(4):
            plan.append((s.at[2 * q + (1 - c)], l.at[q], (x, y, 1 - c)))
    return plan


def _pair4_plan(src_refs, land_refs, x, y, c):
    plan = []
    for s, l in zip(src_refs, land_refs):
        for q in range(4):
            plan.append((s.at[q], l.at[q], (x, y, 1 - c)))
    return plan


def _chips_plan(src_refs, land_refs, x, y, c):
    plan = []
    for s, l in zip(src_refs, land_refs):
        for k, (tx, ty) in enumerate([(1 - x, y), (x, 1 - y), (1 - x, 1 - y)]):
            plan.append((s.at[2 * tx + ty], l.at[k], (tx, ty, c)))
    return plan


def _everyone_plan(src_refs, land_refs, x, y, c):
    me = 4 * x + 2 * y + c
    plan = []
    for s, l in zip(src_refs, land_refs):
        for fx, fy, fc in [(0, 0, 1), (1, 0, 0), (1, 0, 1), (0, 1, 0), (0, 1, 1), (1, 1, 0), (1, 1, 1)]:
            dev = ((1 - x) if fx else x, (1 - y) if fy else y, (1 - c) if fc else c)
            plan.append((s, l.at[me], dev))
    return plan


def _gather_finish(gathered, name):
    n = len(gathered)

    def body(*refs):
        outs = refs[n:2 * n]
        send_sems, recv_sems = refs[2 * n:]
        x, y, c = _coords()
        cps = []
        for a in range(n):
            for j, (px, py) in enumerate([(1 - x, y), (x, 1 - y), (1 - x, 1 - y)]):
                blk = outs[a].at[4 * px + 2 * py + c]
                got = outs[a].at[4 * px + 2 * py + (1 - c)]
                cps.append((pltpu.make_async_remote_copy(
                    src_ref=blk, dst_ref=blk, send_sem=send_sems.at[a, j], recv_sem=recv_sems.at[a, j],
                    device_id=(x, y, 1 - c), device_id_type=MESH), pltpu.make_async_remote_copy(
                    src_ref=got, dst_ref=got, send_sem=send_sems.at[a, j], recv_sem=recv_sems.at[a, j],
                    device_id=(x, y, 1 - c), device_id_type=MESH)))
        for cp, _ in cps:
            cp.start()
        for cp, arrival in cps:
            cp.wait_send()
            arrival.wait_recv()

    return pl.pallas_call(
        body, in_specs=[ANY] * n, out_specs=[ANY] * n,
        out_shape=[jax.ShapeDtypeStruct(g.shape, g.dtype) for g in gathered],
        input_output_aliases={a: a for a in range(n)},
        scratch_shapes=[pltpu.SemaphoreType.DMA((n, 3)), pltpu.SemaphoreType.DMA((n, 3))],
        name=name,
    )(*gathered)


def _pair_add(g8, r1, csel, tr, name):
    _, R, C = r1.shape
    g4 = g8.reshape(4, 2, R, C)

    def body(c_ref, g_ref, r_ref, o_ref):
        o_ref[...] = (g_ref[...].astype(F32) + r_ref[...].astype(F32)).astype(BF16)

    return pl.pallas_call(
        body,
        grid_spec=pltpu.PrefetchScalarGridSpec(
            num_scalar_prefetch=1, grid=(4, R // tr),
            in_specs=[pl.BlockSpec((None, None, tr, C), lambda q, i, cs: (q, cs[0], i, 0)),
                      pl.BlockSpec((None, tr, C), lambda q, i, cs: (q, i, 0))],
            out_specs=pl.BlockSpec((None, tr, C), lambda q, i, cs: (q, i, 0))),
        out_shape=jax.ShapeDtypeStruct((4, R, C), BF16), name=name,
        compiler_params=_cparams(("parallel", "parallel")),
    )(csel, g4, r1)


def _adamw_math(w, g, m, v):
    m = ADAM_B1 * m + (1.0 - ADAM_B1) * g
    v = ADAM_B2 * v + (1.0 - ADAM_B2) * (g * g)
    m_hat = m / (1.0 - ADAM_B1 ** ADAM_STEP)
    v_hat = v / (1.0 - ADAM_B2 ** ADAM_STEP)
    delta = -ADAM_LR * (m_hat / (jnp.sqrt(v_hat) + ADAM_EPS) + ADAM_WD * w)
    return delta, m, v


def _adamw_big(w, m, v, p4, r3, qsel, tile, name):
    R, C = w.shape
    tr, tc = tile

    def body(q_ref, w_ref, m_ref, v_ref, p_ref, r_ref, g_out, d_out, m_out, v_out):
        g = p_ref[...].astype(F32) + r_ref[0].astype(F32) + r_ref[1].astype(F32) + r_ref[2].astype(F32)
        d, mn, vn = _adamw_math(w_ref[...], g, m_ref[...], v_ref[...])
        g_out[...] = g
        d_out[...] = d
        m_out[...] = mn
        v_out[...] = vn

    blk = pl.BlockSpec((tr, tc), lambda i, j, qs: (i, j))
    return pl.pallas_call(
        body,
        grid_spec=pltpu.PrefetchScalarGridSpec(
            num_scalar_prefetch=1, grid=(R // tr, C // tc),
            in_specs=[blk, blk, blk, pl.BlockSpec((None, tr, tc), lambda i, j, qs: (qs[0], i, j)),
                      pl.BlockSpec((3, tr, tc), lambda i, j, qs: (0, i, j))],
            out_specs=[blk, blk, blk, blk]),
        out_shape=[jax.ShapeDtypeStruct((R, C), F32)] * 4, name=name,
        compiler_params=_cparams(("parallel", "parallel")),
    )(qsel, w, m, v, p4, r3)


def _small_sum(parts, name):
    def body(p_ref, o_ref):
        acc = p_ref[0]
        for d in range(1, N_DEV):
            acc = acc + p_ref[d]
        o_ref[...] = acc

    return pl.pallas_call(
        body, out_shape=jax.ShapeDtypeStruct(parts.shape[1:], F32), name=name,
        compiler_params=_cparams(),
    )(parts)


def _adamw_small(w, g, m, v, name):
    def body(w_ref, g_ref, m_ref, v_ref, d_out, m_out, v_out):
        d, mn, vn = _adamw_math(w_ref[...], g_ref[...], m_ref[...], v_ref[...])
        d_out[...] = d
        m_out[...] = mn
        v_out[...] = vn

    return pl.pallas_call(
        body, out_shape=[jax.ShapeDtypeStruct(w.shape, F32)] * 3, name=name, compiler_params=_cparams(),
    )(w, g, m, v)


def _row(*pieces):
    r = jnp.concatenate([p.reshape(1, -1) for p in pieces], axis=1)
    return jnp.pad(r, ((0, 0), (0, D_MODEL - r.shape[1])))


def _pack_small(mix, convb, ssmg, attng, mlpg, fing, convw, dtb, alog, dsk, sinks, extra=None):
    last = [dtb, alog, dsk, sinks] + ([extra] if extra is not None else [])
    rows = [_row(mix), _row(convb), _row(ssmg, attng), _row(mlpg), _row(fing),
            jnp.pad(convw, ((0, 0), (0, D_MODEL - convw.shape[1]))), _row(*last)]
    packed = jnp.concatenate(rows, axis=0)
    return jnp.pad(packed, ((0, SMALL_ROWS - packed.shape[0]), (0, 0)))


def _unpack_small(p, conv_n):
    return dict(
        mix_norm_g=p[0:1, :], conv_b=p[1:2, :], ssm_norm_g=p[2:3, :D_INNER], attn_out_norm_g=p[2:3, D_INNER:],
        mlp_norm_g=p[3:4, :], final_norm_g=p[4, :], conv_w=p[5:9, :conv_n][None],
        dt_bias=p[9:10, 0:16], A_log=p[9:10, 16:32], D_skip=p[9:10, 32:48], attn_sinks=p[9:10, 48:64])


SMALL_NAMES = ["mix_norm_g", "conv_w", "conv_b", "dt_bias", "A_log", "D_skip", "ssm_norm_g", "attn_sinks",
               "attn_out_norm_g", "mlp_norm_g", "final_norm_g"]
WEIGHT_ORDER = ["mix_norm_g", "w_in", "conv_w", "conv_b", "dt_bias", "A_log", "D_skip", "ssm_norm_g", "attn_sinks",
                "attn_out_norm_g", "w_out", "mlp_norm_g", "w_up", "w_down", "final_norm_g"]


def _to_my_columns(w_nat):
    pad = jnp.zeros((w_nat.shape[0], NP - IN_PROJ), w_nat.dtype)
    return jnp.concatenate([w_nat[:, :NAT_DT], w_nat[:, NAT_DT + N_HEADS:], w_nat[:, NAT_DT:NAT_DT + N_HEADS], pad],
                           axis=1)


def _to_natural_columns(w_my):
    return jnp.concatenate([w_my[:, :NAT_DT], w_my[:, OFF_DT:OFF_DT + N_HEADS], w_my[:, NAT_DT:OFF_DT]], axis=1)


SLAB = 1024


def _grad_w_up(h2, du, name, sel=None, add=None, after=None):
    T, D = h2.shape
    if sel is None:
        pick, n_slab, pre = (lambda j, *cs: j), N_DEV, None
    else:
        pre, other = sel
        pick, n_slab = (lambda j, cs: 2 * j + ((1 - cs[0]) if other else cs[0])), 4
    o_spec = pl.BlockSpec((None, SLAB, SLAB), lambda i, j, k, *cs: (j, i, 0))
    return _matmul(
        h2, du, mode="tn", grid=(D // SLAB, n_slab, 1),
        a_spec=pl.BlockSpec((T, SLAB), lambda i, j, k, *cs: (0, i)),
        b_spec=pl.BlockSpec((T, SLAB), lambda i, j, k, *cs: (0, pick(j, *cs))),
        out_shapes=[jax.ShapeDtypeStruct((n_slab, D, SLAB), BF16)], out_specs=[o_spec], tile=(SLAB, SLAB), name=name,
        extras=() if add is None else (add,), extra_specs=() if add is None else (o_spec,),
        epilogue=None if add is None else (lambda acc, r: (acc + r.astype(F32),)), after=after, prefetch=pre)[0]


def _grad_w_down(act, dx3b, name, sel=None, add=None, after=None):
    T, D = dx3b.shape
    if sel is None:
        pick, n_slab, pre = (lambda i, *cs: i), N_DEV, None
    else:
        pre, other = sel
        pick, n_slab = (lambda i, cs: 2 * i + ((1 - cs[0]) if other else cs[0])), 4
    o_spec = pl.BlockSpec((None, SLAB, SLAB), lambda i, j, k, *cs: (i, 0, j))
    return _matmul(
        act, dx3b, mode="tn", grid=(n_slab, D // SLAB, 1),
        a_spec=pl.BlockSpec((T, SLAB), lambda i, j, k, *cs: (0, pick(i, *cs))),
        b_spec=pl.BlockSpec((T, SLAB), lambda i, j, k, *cs: (0, j)),
        out_shapes=[jax.ShapeDtypeStruct((n_slab, SLAB, D), BF16)], out_specs=[o_spec], tile=(SLAB, SLAB), name=name,
        extras=() if add is None else (add,), extra_specs=() if add is None else (o_spec,),
        epilogue=None if add is None else (lambda acc, r: (acc + r.astype(F32),)), after=after, prefetch=pre)[0]


class _FixedWeights:
    def __init__(self, w_in_p, w_out_f, w_up_s, w_down_f, conv_w_f):
        self.w = (w_in_p, w_out_f, w_up_s, w_down_f, conv_w_f)
        self.grads = {}

    def mixer_weights(self, after):
        return self.w[0], self.w[4], None

    def out_weight(self, after):
        return self.w[1]

    def up_weight(self, after):
        return self.w[2]

    def down_weight(self, after):
        return self.w[3]

    def mlp_grads(self, h2, du, act, dx3b):
        self.grads.update(w_up=_grad_w_up(h2, du, "grad_w_up"),
                          w_down=_grad_w_down(act, dx3b, "grad_w_down").reshape(D_FF, D_MODEL))
        return None

    def out_grad(self, g_out):
        self.grads.update(w_out=g_out)
        return None

    def in_grad(self, g_in):
        self.grads.update(w_in=g_in)
        return None


def _local_step(x, tgt, p, hooks):
    T = x.shape[0]
    D = D_MODEL
    h1 = _rmsnorm_fwd(x, p["mix_norm_g"], "norm_mix")
    w_in_p, conv_w_f, token = hooks.mixer_weights(h1)
    (proj,) = _mm_simple(h1, w_in_p, mode="nn", M=T, N=NP, K=D, tm=min(T, 1024), tn=1536, tk=D, out_dtype=F32,
                         name="in_proj", after=token)
    xbc = _conv_fwd(proj, conv_w_f, p["conv_b"], "conv_fwd")
    dtT = proj[:, OFF_DT:OFF_DT + N_HEADS].T
    dtbT = p["dt_bias"].T
    alogT = p["A_log"].T
    dfull = jnp.repeat(p["D_skip"], HEAD_DIM, axis=1)
    ycat, ypre, hs = _ssd_fwd(xbc, proj, dtT, p["dt_bias"], dtbT, p["A_log"], alogT, dfull, p["ssm_norm_g"],
                              "ssd_fwd")
    kpad = jnp.pad(proj[:, OFF_K:OFF_K + KV_W], ((WINDOW, 0), (0, 0)))
    vpad = jnp.pad(proj[:, OFF_V:OFF_V + KV_W], ((WINDOW, 0), (0, 0)))
    ycat, o_att = _attn_fwd(proj, kpad, vpad, p["attn_sinks"], p["attn_out_norm_g"], ycat, "attn_fwd")
    w_out_f = hooks.out_weight(ycat)
    tm = min(T, 1024)
    (x2,) = _mm_simple(ycat, w_out_f, mode="nn", M=T, N=D, K=D, tm=tm, tn=1024, tk=D, out_dtype=F32, name="out_proj",
                       extras=(x,), epilogue=lambda acc, res: (acc + res,))
    h2 = _rmsnorm_fwd(x2, p["mlp_norm_g"], "norm_mlp")
    w_up_s = hooks.up_weight(h2)
    grid = (T // tm, N_DEV, 1)
    u, act = _matmul(
        h2, w_up_s, mode="nn", grid=grid,
        a_spec=pl.BlockSpec((tm, D), lambda i, j, k: (i, 0)),
        b_spec=pl.BlockSpec((None, D, 1024), lambda i, j, k: (j, 0, 0)),
        out_shapes=[jax.ShapeDtypeStruct((T, D_FF), F32), jax.ShapeDtypeStruct((T, D_FF), BF16)],
        out_specs=[pl.BlockSpec((tm, 1024), lambda i, j, k: (i, j))] * 2, tile=(tm, 1024), name="mlp_up",
        epilogue=lambda acc: (acc, jnp.square(jnp.maximum(acc, 0.0))))
    w_down_f = hooks.down_weight(act)
    (x3,) = _mm_simple(act, w_down_f, mode="nn", M=T, N=D, K=D_FF, tm=tm, tn=1024, tk=2048, out_dtype=F32,
                       name="mlp_down", extras=(x2,), epilogue=lambda acc, res: (acc + res,))
    loss_part, d_fin, dx3, dx3b = _final_loss(x3, tgt, p["final_norm_g"].reshape(1, D), "loss_head")
    (du,) = _mm_simple(dx3b, w_down_f, mode="nt", M=T, N=D_FF, K=D, tm=tm, tn=1024, tk=D, out_dtype=BF16,
                       name="mlp_down_bwd", extras=(u,),
                       epilogue=lambda acc, uu: (acc * (2.0 * jnp.maximum(uu, 0.0)),))
    token = hooks.mlp_grads(h2, du, act, dx3b)
    (dh2,) = _matmul(
        du, w_up_s, mode="nt", grid=(T // tm, D // 1024, N_DEV // 2),
        a_spec=pl.BlockSpec((tm, 2048), lambda i, j, k: (i, k)),
        b_spec=pl.BlockSpec((2, 1024, 1024), lambda i, j, k: (k, j, 0)),
        out_shapes=[jax.ShapeDtypeStruct((T, D), F32)],
        out_specs=[pl.BlockSpec((tm, 1024), lambda i, j, k: (i, j))], tile=(tm, 1024), name="mlp_up_bwd",
        after=token, dot_fn=lambda a, b: _dot_nt(a[:, :1024], b[0]) + _dot_nt(a[:, 1024:], b[1]))
    dx2, dx2b, d_mlp = _rmsnorm_bwd(dh2, x2, p["mlp_norm_g"], dx3, "norm_mlp_bwd")
    (g_out,) = _mm_simple(ycat, dx2b, mode="tn", M=D, N=D, K=T, tm=1024, tn=1024, tk=T, out_dtype=BF16,
                          name="grad_w_out")
    token = hooks.out_grad(g_out)
    (dy,) = _mm_simple(dx2b, w_out_f, mode="nt", M=T, N=D, K=D, tm=tm, tn=1024, tk=D, out_dtype=F32,
                       name="out_proj_bwd", after=token)
    dz, dxbc_act, ddt, d_dtb, d_alog, d_dskip, d_ssmg = _ssd_bwd(
        xbc, proj, dtT, p["dt_bias"], dtbT, p["A_log"], alogT, dfull, p["ssm_norm_g"], ypre, hs, dy, "ssd_bwd")
    dq, dkpad, dvpad, d_sinks, d_attng = _attn_bwd(proj, kpad, vpad, p["attn_sinks"], p["attn_out_norm_g"], o_att, dy,
                                                   "attn_bwd")
    dxbc, d_convw, d_convb = _conv_bwd(proj, dxbc_act, conv_w_f, p["conv_b"], "conv_bwd")
    dproj = jnp.concatenate(
        [dz, dxbc, dq, dkpad[WINDOW:].astype(BF16), dvpad[WINDOW:].astype(BF16), ddt.astype(BF16),
         jnp.zeros((T, NP - OFF_DT - 128), BF16)], axis=1)
    (g_in,) = _mm_simple(dproj, h1, mode="tn", M=NP, N=D, K=T, tm=1536, tn=1024, tk=T, out_dtype=BF16,
                         name="grad_w_in")
    token = hooks.in_grad(g_in)
    (dh1,) = _mm_simple(dproj, w_in_p, mode="nt", M=T, N=D, K=NP, tm=tm, tn=1024, tk=2304, out_dtype=F32,
                        name="in_proj_bwd", after=token)
    dx, _, d_mix = _rmsnorm_bwd(dh1, x, p["mix_norm_g"], dx2, "norm_mix_bwd")
    small = _pack_small(d_mix, d_convb, d_ssmg, d_attng, d_mlp, d_fin, d_convw, d_dtb, d_alog, d_dskip, d_sinks,
                        extra=loss_part[:, 0:1])
    return dx, small


def _landing(own, me):
    zone = lax.empty((N_DEV,) + own.shape, own.dtype)
    return lax.dynamic_update_slice(zone, own[None], (me,) + (0,) * own.ndim)


def _gather_end(started, after, plan, name):
    _, lands = _remote_wait(started, after, plan, name + "_wait")
    return _gather_finish(lands, name + "_finish")


class _ShardedWeights:
    def __init__(self, w_in, w_out, conv_w, w_up, w_down, me, csel):
        self.me, self.csel = me, csel
        shards = [w_in.astype(BF16), conv_w]
        self.st_mixer = _remote_start(shards, [_landing(s, me) for s in shards], _gather_plan, 4 * len(shards),
                                      "gather_start_mixer")
        order = self.st_mixer[4]
        self.st_later = {}
        for k, wt in [("out", w_out), ("up", w_up), ("down", w_down)]:
            shard = wt.astype(BF16)
            self.st_later[k] = _remote_start([shard], [_landing(shard, me)], _gather_plan, 4, f"gather_start_{k}",
                                             after=order)
            order = self.st_later[k][4]
        self.start_token = order
        self.reduces = {}

    def mixer_weights(self, after):
        g_in, g_conv = _gather_end(self.st_mixer, after, _gather_plan, "gather_mixer")
        per = IN_PROJ // N_DEV
        k, off = NAT_DT // per, NAT_DT % per
        pieces = [g_in[i] for i in range(k)] + [g_in[k][:, :off], g_in[k][:, off + N_HEADS:]]
        pieces += [g_in[i] for i in range(k + 1, N_DEV)]
        pieces += [g_in[k][:, off:off + N_HEADS], jnp.zeros((D_MODEL, NP - IN_PROJ), BF16)]
        w_in_p = jnp.concatenate(pieces, axis=1)
        conv_w_f = jnp.concatenate([g_conv[i] for i in range(N_DEV)], axis=1)
        return w_in_p, conv_w_f, None

    def out_weight(self, after):
        return _gather_end(self.st_later["out"], after, _gather_plan, "gather_out")[0].reshape(D_MODEL, D_MODEL)

    def up_weight(self, after):
        return _gather_end(self.st_later["up"], after, _gather_plan, "gather_up")[0]

    def down_weight(self, after):
        return _gather_end(self.st_later["down"], after, _gather_plan, "gather_down")[0].reshape(D_FF, D_MODEL)

    def _chips_start(self, slabs, from_sibling, rows, tag):
        sums = [_pair_add(s, r, self.csel, tr, f"pair_add_{tag}_{i}")
                for i, (s, r, tr) in enumerate(zip(slabs, from_sibling, rows))]
        lands = [lax.empty((3,) + s.shape[1:], s.dtype) for s in sums]
        self.reduces[tag] = _remote_start(sums, lands, _chips_plan, 3 * len(sums), f"reduce_start_{tag}")
        return self.reduces[tag][4]

    def mlp_grads(self, h2, du, act, dx3b):
        def send(part, tag, after):
            st = _remote_start([part], [lax.empty(part.shape, part.dtype)], _pair4_plan, 4,
                               f"reduce_pair_start_{tag}", after=after)
            return st

        def received(st, after, tag):
            return _remote_wait(st, after, _pair4_plan, f"reduce_pair_wait_{tag}")[1][0]

        def to_chips(sums, tag):
            self.reduces[tag] = _remote_start([sums], [lax.empty((3,) + sums.shape[1:], sums.dtype)], _chips_plan, 3,
                                              f"reduce_start_{tag}")
            return self.reduces[tag][4]

        up_send = _grad_w_up(h2, du, "grad_w_up_send", sel=(self.csel, True))
        st_up = send(up_send, "up", None)
        down_send = _grad_w_down(act, dx3b, "grad_w_down_send", sel=(self.csel, True), after=st_up[4])
        st_down = send(down_send, "down", None)
        up_sum = _grad_w_up(h2, du, "grad_w_up_keep", sel=(self.csel, False), add=received(st_up, down_send, "up"),
                            after=st_down[4])
        token = to_chips(up_sum, "up")
        down_sum = _grad_w_down(act, dx3b, "grad_w_down_keep", sel=(self.csel, False),
                                add=received(st_down, up_sum, "down"), after=token)
        return to_chips(down_sum, "down")

    def out_grad(self, g_out):
        slabs = [g_out.reshape(N_DEV, D_MODEL // N_DEV, D_MODEL)]
        return self._chips_start(slabs, _exchange_pair(slabs, "reduce_pair_out"), [256], "out")

    def in_grad(self, g_in):
        per = IN_PROJ // N_DEV
        nat = jnp.concatenate([g_in[:NAT_DT], g_in[OFF_DT:OFF_DT + N_HEADS], g_in[NAT_DT:OFF_DT]], axis=0)
        slabs = [nat.reshape(N_DEV, per, D_MODEL)]
        return self._chips_start(slabs, _exchange_pair(slabs, "reduce_pair_in"), [per], "in")

    def small_start(self, small):
        self.st_small = _remote_start([small], [_landing(small, self.me)], _everyone_plan, N_DEV - 1, "gather_start_small")

    def small_end(self, after):
        return _remote_wait(self.st_small, after, _everyone_plan, "gather_small_wait")[1][0]

    def reduce_end(self, tag, after):
        return _remote_wait(self.reduces[tag], after, _chips_plan, f"reduce_wait_{tag}")


def kernel(x, mix_norm_g, w_in, conv_w, conv_b, dt_bias, A_log, D_skip, ssm_norm_g, attn_sinks, attn_out_norm_g, w_out, mlp_norm_g, w_up, w_down, final_norm_g, loss_target, m_mix_norm_g, m_w_in, m_conv_w, m_conv_b, m_dt_bias, m_A_log, m_D_skip, m_ssm_norm_g, m_attn_sinks, m_attn_out_norm_g, m_w_out, m_mlp_norm_g, m_w_up, m_w_down, m_final_norm_g, v_mix_norm_g, v_w_in, v_conv_w, v_conv_b, v_dt_bias, v_A_log, v_D_skip, v_ssm_norm_g, v_attn_sinks, v_attn_out_norm_g, v_w_out, v_mlp_norm_g, v_w_up, v_w_down, v_final_norm_g):
    xi, yi, ci = _coords()
    me = 4 * xi + 2 * yi + ci
    csel = jnp.reshape(ci, (1,)).astype(jnp.int32)
    qsel = jnp.reshape(2 * xi + yi, (1,)).astype(jnp.int32)
    w = dict(mix_norm_g=mix_norm_g, conv_b=conv_b, dt_bias=dt_bias, A_log=A_log, D_skip=D_skip,
             ssm_norm_g=ssm_norm_g, attn_sinks=attn_sinks, attn_out_norm_g=attn_out_norm_g, mlp_norm_g=mlp_norm_g,
             final_norm_g=final_norm_g)
    hooks = _ShardedWeights(w_in[0], w_out[0], conv_w[0], w_up[0], w_down[0], me, csel)
    p = dict(w, mix_norm_g=mix_norm_g + hooks.start_token[0:1, 0:1])
    dx, small = _local_step(x[0], loss_target[0], p, hooks)
    hooks.small_start(small)
    big = {}
    after = dx
    for name, wt, mt, vt, tile, transposed in [
            ("up", w_up, m_w_up, v_w_up, (512, SLAB), False), ("down", w_down, m_w_down, v_w_down, (256, D_MODEL), False),
            ("out", w_out, m_w_out, v_w_out, (256, D_MODEL), False),
            ("in", w_in, m_w_in, v_w_in, (IN_PROJ // N_DEV, 512), True)]:
        (chip_sums,), (from_chips,) = hooks.reduce_end(name, after)
        shard = [jnp.transpose(t[0]) if transposed else t[0] for t in (wt, mt, vt)]
        res = _adamw_big(*shard, chip_sums, from_chips, qsel, tile, f"adamw_w_{name}")
        big["w_" + name] = tuple((jnp.transpose(r) if transposed else r)[None] for r in res)
        after = res[0]
    gsum = _small_sum(hooks.small_end(after), "small_sum")
    loss = gsum[9, 64]
    gs = _unpack_small(gsum, CONV_DIM)
    cw = CONV_DIM // N_DEV
    g_conv_shard = lax.dynamic_slice(gsum[5:9, :], (0, me * cw), (CONV_K, cw))

    def pack(s):
        return _pack_small(s["mix_norm_g"], s["conv_b"], s["ssm_norm_g"], s["attn_out_norm_g"], s["mlp_norm_g"],
                           s["final_norm_g"], s["conv_w"][0], s["dt_bias"], s["A_log"], s["D_skip"], s["attn_sinks"])

    wp = pack(dict(w, conv_w=conv_w))
    mp = pack(dict(mix_norm_g=m_mix_norm_g, conv_b=m_conv_b, ssm_norm_g=m_ssm_norm_g,
                   attn_out_norm_g=m_attn_out_norm_g, mlp_norm_g=m_mlp_norm_g, final_norm_g=m_final_norm_g,
                   conv_w=m_conv_w, dt_bias=m_dt_bias, A_log=m_A_log, D_skip=m_D_skip, attn_sinks=m_attn_sinks))
    vp = pack(dict(mix_norm_g=v_mix_norm_g, conv_b=v_conv_b, ssm_norm_g=v_ssm_norm_g,
                   attn_out_norm_g=v_attn_out_norm_g, mlp_norm_g=v_mlp_norm_g, final_norm_g=v_final_norm_g,
                   conv_w=v_conv_w, dt_bias=v_dt_bias, A_log=v_A_log, D_skip=v_D_skip, attn_sinks=v_attn_sinks))
    gp = jnp.concatenate([gsum[0:5], jnp.pad(g_conv_shard, ((0, 0), (0, D_MODEL - cw))), gsum[9:10],
                          jnp.zeros((SMALL_ROWS - 10, D_MODEL), F32)], axis=0)
    dp, mnp, vnp = _adamw_small(wp, gp, mp, vp, "adamw_small")
    grads = dict(gs, conv_w=g_conv_shard[None])
    deltas = _unpack_small(dp, cw)
    new_m = _unpack_small(mnp, cw)
    new_v = _unpack_small(vnp, cw)
    for k, name in enumerate(["w_in", "w_out", "w_up", "w_down"]):
        grads[name], deltas[name], new_m[name], new_v[name] = big[name]
    return (loss, dx[None], *[grads[n] for n in WEIGHT_ORDER], *[deltas[n] for n in WEIGHT_ORDER],
            *[new_m[n] for n in WEIGHT_ORDER], *[new_v[n] for n in WEIGHT_ORDER])
```

```python
import functools

import jax
import jax.numpy as jnp
from jax import lax
from jax.experimental import pallas as pl
from jax.experimental.pallas import tpu as pltpu

F32 = jnp.float32
BF16 = jnp.bfloat16
HI = lax.Precision.HIGHEST
MESH = pl.DeviceIdType.MESH

EPS = 1e-5
D_MODEL = 2048
D_INNER = 1024
N_HEADS = 16
HEAD_DIM = 64
N_GROUPS = 4
D_STATE = 128
CHUNK = 128
CONV_K = 4
CONV_DIM = 2048
ATTN_W = 1024
KV_W = 128
WINDOW = 128
D_FF = 8192
IN_PROJ = 4368
N_DEV = 8
NP = 4608
OFF_Z, OFF_X, OFF_B, OFF_C, OFF_Q, OFF_K, OFF_V, OFF_DT = 0, 1024, 2048, 2560, 3072, 4096, 4224, 4352
NAT_DT = 3072

ADAM_LR = 0.001
ADAM_B1 = 0.9
ADAM_B2 = 0.999
ADAM_EPS = 1e-08
ADAM_WD = 0.01
ADAM_STEP = 10

VMEM_LIMIT = 52 * 1024 * 1024
SMALL_ROWS = 16
NEG = -1e30


def _cparams(sem=None):
    return pltpu.CompilerParams(dimension_semantics=sem, vmem_limit_bytes=VMEM_LIMIT)


def _split3(v):
    hi = v.astype(BF16)
    rest = v - hi.astype(F32)
    mid = rest.astype(BF16)
    return hi, mid, (rest - mid.astype(F32)).astype(BF16)


def _hdot(a, b, data):
    if data == "a":
        sel = b.astype(BF16)
        return sum(_dot_nn(part, sel) for part in _split3(a))
    sel = a.astype(BF16)
    return sum(_dot_nn(sel, part) for part in _split3(b))


def _dot_nn(a, b):
    return lax.dot_general(a, b, (((1,), (0,)), ((), ())), preferred_element_type=F32)


def _dot_nt(a, b):
    return lax.dot_general(a, b, (((1,), (1,)), ((), ())), preferred_element_type=F32)


def _dot_tn(a, b):
    return lax.dot_general(a, b, (((0,), (0,)), ((), ())), preferred_element_type=F32)


def _softplus(v):
    return jnp.maximum(v, 0.0) + jnp.log1p(jnp.exp(-jnp.abs(v)))


def _sigmoid(v):
    return 1.0 / (1.0 + jnp.exp(-v))


def _matmul(a, b, *, mode, grid, a_spec, b_spec, out_shapes, out_specs, tile, name,
            extras=(), extra_specs=(), epilogue=None, after=None, dot_fn=None, prefetch=None):
    nk = grid[2]
    n_ex = len(extras)
    n_out = len(out_shapes)
    dot = dot_fn if dot_fn is not None else {"nn": _dot_nn, "nt": _dot_nt, "tn": _dot_tn}[mode]

    def finish(acc, ex_refs, out_refs):
        res = (acc,) if epilogue is None else epilogue(acc, *[e[...] for e in ex_refs])
        for o, r in zip(out_refs, res):
            o[...] = r.astype(o.dtype)

    def body(*refs):
        a_ref, b_ref = refs[0], refs[1]
        ex_refs = refs[2:2 + n_ex]
        out_refs = refs[2 + n_ex:2 + n_ex + n_out]
        part = dot(a_ref[...].astype(BF16), b_ref[...].astype(BF16))
        if nk == 1:
            finish(part, ex_refs, out_refs)
        else:
            acc_ref = refs[-1]
            k = pl.program_id(2)

            @pl.when(k == 0)
            def _():
                acc_ref[...] = part

            @pl.when(k > 0)
            def _():
                acc_ref[...] += part

            @pl.when(k == nk - 1)
            def _():
                finish(acc_ref[...], ex_refs, out_refs)

    scratch = [] if nk == 1 else [pltpu.VMEM(tile, F32)]
    n_pre = 0 if prefetch is None else 1
    tok_specs = [] if after is None else [pl.BlockSpec((8, 128), lambda *_: (0, 0))]
    tok_args = [] if after is None else [after]

    def body_with_token(*refs):
        refs = refs[n_pre:]
        body(*refs[:2 + n_ex], *refs[2 + n_ex + len(tok_args):])

    in_specs = [a_spec, b_spec, *extra_specs, *tok_specs]
    params = _cparams(("parallel", "parallel", "arbitrary"))
    if prefetch is None:
        return pl.pallas_call(
            body_with_token, grid=grid, in_specs=in_specs, out_specs=list(out_specs), out_shape=list(out_shapes),
            scratch_shapes=scratch, name=name, compiler_params=params)(a, b, *extras, *tok_args)
    return pl.pallas_call(
        body_with_token,
        grid_spec=pltpu.PrefetchScalarGridSpec(num_scalar_prefetch=1, grid=grid, in_specs=in_specs,
                                               out_specs=list(out_specs), scratch_shapes=scratch),
        out_shape=list(out_shapes), name=name, compiler_params=params)(prefetch, a, b, *extras, *tok_args)


def _mm_simple(a, b, *, mode, M, N, K, tm, tn, tk, out_dtype, name, extras=(), epilogue=None, n_out=1,
               out_dtypes=None, after=None):
    grid = (M // tm, N // tn, K // tk)
    if mode == "nn":
        a_spec = pl.BlockSpec((tm, tk), lambda i, j, k: (i, k))
        b_spec = pl.BlockSpec((tk, tn), lambda i, j, k: (k, j))
    elif mode == "nt":
        a_spec = pl.BlockSpec((tm, tk), lambda i, j, k: (i, k))
        b_spec = pl.BlockSpec((tn, tk), lambda i, j, k: (j, k))
    else:
        a_spec = pl.BlockSpec((tk, tm), lambda i, j, k: (k, i))
        b_spec = pl.BlockSpec((tk, tn), lambda i, j, k: (k, j))
    o_spec = pl.BlockSpec((tm, tn), lambda i, j, k: (i, j))
    dts = out_dtypes if out_dtypes is not None else [out_dtype] * n_out
    return _matmul(a, b, mode=mode, grid=grid, a_spec=a_spec, b_spec=b_spec,
                   out_shapes=[jax.ShapeDtypeStruct((M, N), d) for d in dts],
                   out_specs=[o_spec] * len(dts), tile=(tm, tn), name=name,
                   extras=extras, extra_specs=[o_spec] * len(extras), epilogue=epilogue, after=after)


ROW_BLOCK = 256


def _rmsnorm_fwd(x, g, name):
    T, D = x.shape

    def body(x_ref, g_ref, o_ref):
        xf = x_ref[...]
        r = lax.rsqrt(jnp.mean(xf * xf, axis=-1, keepdims=True) + EPS)
        o_ref[...] = (xf * r * g_ref[...]).astype(BF16)

    return pl.pallas_call(
        body, grid=(T // ROW_BLOCK,),
        in_specs=[pl.BlockSpec((ROW_BLOCK, D), lambda i: (i, 0)), pl.BlockSpec((1, D), lambda i: (0, 0))],
        out_specs=pl.BlockSpec((ROW_BLOCK, D), lambda i: (i, 0)),
        out_shape=jax.ShapeDtypeStruct((T, D), BF16), name=name, compiler_params=_cparams(("parallel",)),
    )(x, g)


def _rmsnorm_bwd(dh, x, g, dres, name):
    T, D = x.shape

    def body(dh_ref, x_ref, g_ref, dres_ref, dx_ref, dxb_ref, dg_ref):
        i = pl.program_id(0)
        xf = x_ref[...]
        r = lax.rsqrt(jnp.mean(xf * xf, axis=-1, keepdims=True) + EPS)
        xh = xf * r
        d = dh_ref[...]

        @pl.when(i == 0)
        def _():
            dg_ref[...] = jnp.zeros_like(dg_ref)

        dg_ref[...] += jnp.sum(d * xh, axis=0, keepdims=True)
        dxh = d * g_ref[...]
        dx = r * (dxh - xh * jnp.mean(dxh * xh, axis=-1, keepdims=True)) + dres_ref[...]
        dx_ref[...] = dx
        dxb_ref[...] = dx.astype(BF16)

    row = pl.BlockSpec((ROW_BLOCK, D), lambda i: (i, 0))
    vec = pl.BlockSpec((1, D), lambda i: (0, 0))
    return pl.pallas_call(
        body, grid=(T // ROW_BLOCK,), in_specs=[row, row, vec, row], out_specs=[row, row, vec],
        out_shape=[jax.ShapeDtypeStruct((T, D), F32), jax.ShapeDtypeStruct((T, D), BF16),
                   jax.ShapeDtypeStruct((1, D), F32)],
        name=name, compiler_params=_cparams(("arbitrary",)),
    )(dh, x, g, dres)


def _final_loss(x3, tgt, g, name):
    T, D = x3.shape

    def body(x_ref, t_ref, g_ref, loss_ref, dg_ref, dx_ref, dxb_ref):
        i = pl.program_id(0)
        xf = x_ref[...]
        r = lax.rsqrt(jnp.mean(xf * xf, axis=-1, keepdims=True) + EPS)
        xh = xf * r
        gg = g_ref[...]
        err = xh * gg - t_ref[...]

        @pl.when(i == 0)
        def _():
            dg_ref[...] = jnp.zeros_like(dg_ref)
            loss_ref[...] = jnp.zeros_like(loss_ref)

        part = jnp.sum(jnp.sum(err * err, axis=-1, keepdims=True), axis=0, keepdims=True) * (0.5 / D)
        loss_ref[...] += jnp.broadcast_to(part, loss_ref.shape)
        dout = err * (1.0 / D)
        dg_ref[...] += jnp.sum(dout * xh, axis=0, keepdims=True)
        dxh = dout * gg
        dx = r * (dxh - xh * jnp.mean(dxh * xh, axis=-1, keepdims=True))
        dx_ref[...] = dx
        dxb_ref[...] = dx.astype(BF16)

    row = pl.BlockSpec((ROW_BLOCK, D), lambda i: (i, 0))
    vec = pl.BlockSpec((1, D), lambda i: (0, 0))
    return pl.pallas_call(
        body, grid=(T // ROW_BLOCK,), in_specs=[row, row, vec],
        out_specs=[pl.BlockSpec((1, 128), lambda i: (0, 0)), vec, row, row],
        out_shape=[jax.ShapeDtypeStruct((1, 128), F32), jax.ShapeDtypeStruct((1, D), F32),
                   jax.ShapeDtypeStruct((T, D), F32), jax.ShapeDtypeStruct((T, D), BF16)],
        name=name, compiler_params=_cparams(("arbitrary",)),
    )(x3, tgt, g)


CONV_BLOCK = 256


def _conv_apply(u, w, b):
    row = lax.broadcasted_iota(jnp.int32, u.shape, 0)
    acc = b + w[CONV_K - 1:CONV_K, :] * u
    shifted = []
    for j in range(1, CONV_K):
        uj = jnp.where(row >= j, pltpu.roll(u, j, axis=0), 0.0)
        shifted.append(uj)
        acc = acc + w[CONV_K - 1 - j:CONV_K - j, :] * uj
    return acc, shifted


def _conv_fwd(proj, conv_w, conv_b, name):
    T = proj.shape[0]
    cb0 = OFF_X // CONV_BLOCK

    def body(u_ref, w_ref, b_ref, o_ref):
        c, _ = _conv_apply(u_ref[...], w_ref[...], b_ref[...])
        o_ref[...] = c * _sigmoid(c)

    return pl.pallas_call(
        body, grid=(CONV_DIM // CONV_BLOCK,),
        in_specs=[pl.BlockSpec((T, CONV_BLOCK), lambda j: (0, cb0 + j)),
                  pl.BlockSpec((CONV_K, CONV_BLOCK), lambda j: (0, j)),
                  pl.BlockSpec((1, CONV_BLOCK), lambda j: (0, j))],
        out_specs=pl.BlockSpec((T, CONV_BLOCK), lambda j: (0, j)),
        out_shape=jax.ShapeDtypeStruct((T, CONV_DIM), F32), name=name, compiler_params=_cparams(("parallel",)),
    )(proj, conv_w, conv_b)


def _conv_bwd(proj, dact, conv_w, conv_b, dproj, name):
    T = proj.shape[0]
    cb0 = OFF_X // CONV_BLOCK

    def body(u_ref, d_ref, w_ref, b_ref, _, du_ref, dw_ref, db_ref):
        u = u_ref[...]
        w = w_ref[...]
        c, shifted = _conv_apply(u, w, b_ref[...])
        sg = _sigmoid(c)
        dc = d_ref[...] * sg * (1.0 + c * (1.0 - sg))
        row = lax.broadcasted_iota(jnp.int32, u.shape, 0)
        du = w[CONV_K - 1:CONV_K, :] * dc
        dw_ref[CONV_K - 1:CONV_K, :] = jnp.sum(dc * u, axis=0, keepdims=True)
        for j in range(1, CONV_K):
            dcj = jnp.where(row < T - j, pltpu.roll(dc, T - j, axis=0), 0.0)
            du = du + w[CONV_K - 1 - j:CONV_K - j, :] * dcj
            dw_ref[CONV_K - 1 - j:CONV_K - j, :] = jnp.sum(dc * shifted[j - 1], axis=0, keepdims=True)
        db_ref[...] = jnp.sum(dc, axis=0, keepdims=True)
        du_ref[...] = du.astype(BF16)

    return pl.pallas_call(
        body, grid=(CONV_DIM // CONV_BLOCK,),
        in_specs=[pl.BlockSpec((T, CONV_BLOCK), lambda j: (0, cb0 + j)),
                  pl.BlockSpec((T, CONV_BLOCK), lambda j: (0, j)),
                  pl.BlockSpec((CONV_K, CONV_BLOCK), lambda j: (0, j)),
                  pl.BlockSpec((1, CONV_BLOCK), lambda j: (0, j)), pl.BlockSpec(memory_space=pl.ANY)],
        out_specs=[pl.BlockSpec((T, CONV_BLOCK), lambda j: (0, cb0 + j)),
                   pl.BlockSpec((CONV_K, CONV_BLOCK), lambda j: (0, j)),
                   pl.BlockSpec((1, CONV_BLOCK), lambda j: (0, j))],
        out_shape=[jax.ShapeDtypeStruct(dproj.shape, BF16), jax.ShapeDtypeStruct((CONV_K, CONV_DIM), F32),
                   jax.ShapeDtypeStruct((1, CONV_DIM), F32)],
        input_output_aliases={4: 0}, name=name, compiler_params=_cparams(("parallel",)),
    )(proj, dact, conv_w, conv_b, dproj)


GROUP_W = D_INNER // N_GROUPS
HEADS_PER_GROUP = N_HEADS // N_GROUPS


def _expand_mat():
    h = lax.broadcasted_iota(jnp.int32, (N_HEADS, D_INNER), 0)
    j = lax.broadcasted_iota(jnp.int32, (N_HEADS, D_INNER), 1)
    return (j // HEAD_DIM == h).astype(F32)


def _reduce_mat(g):
    j = lax.broadcasted_iota(jnp.int32, (GROUP_W, N_HEADS), 0)
    h = lax.broadcasted_iota(jnp.int32, (GROUP_W, N_HEADS), 1)
    return (g * HEADS_PER_GROUP + j // HEAD_DIM == h).astype(F32)


def _col16(v, h):
    lane = lax.broadcasted_iota(jnp.int32, v.shape, 1)
    return jnp.sum(jnp.where(lane == h, v, 0.0), axis=1, keepdims=True)


def _ssd_pre(dt_raw, dtT_raw, dtb, dtbT, alog, alogT):
    Q = CHUNK
    xdt = dt_raw + dtb
    dt = _softplus(xdt)
    dtT = _softplus(dtT_raw + dtbT)
    A = -jnp.exp(alog)
    AT = -jnp.exp(alogT)
    row = lax.broadcasted_iota(jnp.int32, (Q, Q), 0)
    col = lax.broadcasted_iota(jnp.int32, (Q, Q), 1)
    tril = (row >= col).astype(F32)
    triu = (row <= col).astype(F32)
    cs = _hdot(tril, dt * A, "b")
    csT = _hdot(dtT * AT, triu, "a")
    return xdt, dt, A, cs, csT, row >= col, triu


def _decay_matrix(cs, csT, h, causal):
    seg = _col16(cs, h) - csT[h:h + 1, :]
    return jnp.where(causal, jnp.exp(jnp.minimum(seg, 0.0)), 0.0)


def _ssd_in_specs(nc, rev):
    def cidx(c):
        return (nc - 1 - c) if rev else c

    return [
        pl.BlockSpec((CHUNK, D_INNER), lambda c: (cidx(c), 0)),
        pl.BlockSpec((CHUNK, 512), lambda c: (cidx(c), 2)),
        pl.BlockSpec((CHUNK, 512), lambda c: (cidx(c), 3)),
        pl.BlockSpec((CHUNK, D_INNER), lambda c: (cidx(c), 0)),
        pl.BlockSpec((CHUNK, 128), lambda c: (cidx(c), OFF_DT // 128)),
        pl.BlockSpec((N_HEADS, CHUNK), lambda c: (0, cidx(c))),
        pl.BlockSpec((1, N_HEADS), lambda c: (0, 0)),
        pl.BlockSpec((N_HEADS, 1), lambda c: (0, 0)),
        pl.BlockSpec((1, N_HEADS), lambda c: (0, 0)),
        pl.BlockSpec((N_HEADS, 1), lambda c: (0, 0)),
        pl.BlockSpec((1, D_INNER), lambda c: (0, 0)),
        pl.BlockSpec((1, D_INNER), lambda c: (0, 0)),
    ]


def _ssd_fwd(xbc, proj, dtT, dtb, dtbT, alog, alogT, dfull, ng, name):
    T = xbc.shape[0]
    nc = T // CHUNK
    Q = CHUNK

    def body(xs_ref, B_ref, C_ref, z_ref, dt_ref, dtT_ref, dtb_ref, dtbT_ref, al_ref, alT_ref, df_ref, ng_ref,
             y_ref, ypre_ref, hs_ref, h_scr):
        c = pl.program_id(0)

        @pl.when(c == 0)
        def _():
            h_scr[...] = jnp.zeros_like(h_scr)

        _, dt, _, cs, csT, causal, _ = _ssd_pre(dt_ref[:, :N_HEADS], dtT_ref[...], dtb_ref[...], dtbT_ref[...],
                                                al_ref[...], alT_ref[...])
        ex = _expand_mat()
        dt_full = _hdot(dt, ex, "a")
        cs_full = _hdot(cs, ex, "a")
        cs_last = cs_full[Q - 1:Q, :]
        xs = xs_ref[...]
        xd = xs * dt_full
        e_full = jnp.exp(cs_full)
        dec_full = jnp.exp(cs_last - cs_full)
        cd_full = jnp.exp(cs_last)
        lane_head = lax.broadcasted_iota(jnp.int32, (1, GROUP_W), 1) // HEAD_DIM
        for g in range(N_GROUPS):
            sl = slice(g * GROUP_W, (g + 1) * GROUP_W)
            Bg = B_ref[:, g * D_STATE:(g + 1) * D_STATE].astype(BF16)
            Cg = C_ref[:, g * D_STATE:(g + 1) * D_STATE].astype(BF16)
            CB = _dot_nt(Cg, Bg)
            hg = h_scr[g]
            yoff = _dot_nn(Cg, hg.astype(BF16)) * e_full[:, sl]
            xd_g = xd[:, sl]
            S = _dot_tn(Bg, (xd_g * dec_full[:, sl]).astype(BF16))
            xd_b = xd_g.astype(BF16)
            ydiag = jnp.zeros((Q, GROUP_W), F32)
            for r in range(HEADS_PER_GROUP):
                Lm = _decay_matrix(cs, csT, g * HEADS_PER_GROUP + r, causal)
                Gm = (CB * Lm).astype(BF16)
                ydiag = ydiag + _dot_nn(Gm, jnp.where(lane_head == r, xd_b, jnp.zeros_like(xd_b)))
            hs_ref[0, g] = hg
            h_scr[g] = hg * cd_full[:, sl] + S
            ypre = ydiag + yoff + xs[:, sl] * df_ref[:, sl]
            ypre_ref[:, sl] = ypre
            zg = z_ref[:, sl]
            yz = ypre * zg * _sigmoid(zg)
            rn = lax.rsqrt(jnp.mean(yz * yz, axis=-1, keepdims=True) + EPS)
            y_ref[:, sl] = (yz * rn * ng_ref[:, sl]).astype(BF16)

    return pl.pallas_call(
        body, grid=(nc,), in_specs=_ssd_in_specs(nc, False),
        out_specs=[pl.BlockSpec((CHUNK, D_INNER), lambda c: (c, 0)),
                   pl.BlockSpec((CHUNK, D_INNER), lambda c: (c, 0)),
                   pl.BlockSpec((1, N_GROUPS, D_STATE, GROUP_W), lambda c: (c, 0, 0, 0))],
        out_shape=[jax.ShapeDtypeStruct((T, D_INNER + ATTN_W), BF16), jax.ShapeDtypeStruct((T, D_INNER), F32),
                   jax.ShapeDtypeStruct((nc, N_GROUPS, D_STATE, GROUP_W), F32)],
        scratch_shapes=[pltpu.VMEM((N_GROUPS, D_STATE, GROUP_W), F32)],
        name=name, compiler_params=_cparams(("arbitrary",)),
    )(xbc, xbc, xbc, proj, proj, dtT, dtb, dtbT, alog, alogT, dfull, ng)


def _ssd_bwd(xbc, proj, dtT, dtb, dtbT, alog, alogT, dfull, ng, ypre, hs, dy, name):
    T = xbc.shape[0]
    nc = T // CHUNK
    Q = CHUNK

    def body(xs_ref, B_ref, C_ref, z_ref, dt_ref, dtT_ref, dtb_ref, dtbT_ref, al_ref, alT_ref, df_ref, ng_ref,
             ypre_ref, hs_ref, dy_ref,
             dz_ref, dxbc_ref, ddtb_ref, dal_ref, dD_ref, dng_ref, dh_scr):
        step = pl.program_id(0)

        @pl.when(step == 0)
        def _():
            dh_scr[...] = jnp.zeros_like(dh_scr)
            ddtb_ref[...] = jnp.zeros_like(ddtb_ref)
            dal_ref[...] = jnp.zeros_like(dal_ref)
            dD_ref[...] = jnp.zeros_like(dD_ref)
            dng_ref[...] = jnp.zeros_like(dng_ref)

        xdt, dt, A, cs, csT, causal, triu = _ssd_pre(dt_ref[:, :N_HEADS], dtT_ref[...], dtb_ref[...],
                                                    dtbT_ref[...], al_ref[...], alT_ref[...])
        ex = _expand_mat()
        dt_full = _hdot(dt, ex, "a")
        cs_full = _hdot(cs, ex, "a")
        cs_last = cs_full[Q - 1:Q, :]
        xs = xs_ref[...]
        xd = xs * dt_full
        e_full = jnp.exp(cs_full)
        dec_full = jnp.exp(cs_last - cs_full)
        cd_full = jnp.exp(cs_last)
        lane_head = lax.broadcasted_iota(jnp.int32, (1, GROUP_W), 1) // HEAD_DIM
        is_last = lax.broadcasted_iota(jnp.int32, (Q, 1), 0) == Q - 1
        dcs16 = jnp.zeros((Q, N_HEADS), F32)
        ddtx16 = jnp.zeros((Q, N_HEADS), F32)
        dD16 = jnp.zeros((8, N_HEADS), F32)
        lane16 = lax.broadcasted_iota(jnp.int32, (1, N_HEADS), 1)
        sub16 = lax.broadcasted_iota(jnp.int32, (N_HEADS, 1), 0)
        col_sums = jnp.zeros((N_HEADS, Q), F32)
        for g in range(N_GROUPS):
            sl = slice(g * GROUP_W, (g + 1) * GROUP_W)
            red = _reduce_mat(g)
            ypre_g = ypre_ref[:, sl]
            zg = z_ref[:, sl]
            sg = _sigmoid(zg)
            silu = zg * sg
            yz = ypre_g * silu
            rn = lax.rsqrt(jnp.mean(yz * yz, axis=-1, keepdims=True) + EPS)
            yh = yz * rn
            dy_g = dy_ref[:, sl]
            dng_ref[:, sl] += jnp.sum(dy_g * yh, axis=0, keepdims=True)
            dyh = dy_g * ng_ref[:, sl]
            dyz = rn * (dyh - yh * jnp.mean(dyh * yh, axis=-1, keepdims=True))
            dY = dyz * silu
            dz_ref[:, sl] = (dyz * ypre_g * sg * (1.0 + zg * (1.0 - sg))).astype(BF16)
            xs_g = xs[:, sl]
            xd_g = xd[:, sl]
            dec_g = dec_full[:, sl]
            cd_g = cd_full[:, sl]
            d_g = df_ref[:, sl]
            Bg = B_ref[:, g * D_STATE:(g + 1) * D_STATE].astype(BF16)
            Cg = C_ref[:, g * D_STATE:(g + 1) * D_STATE].astype(BF16)
            CB = _dot_nt(Cg, Bg)
            hg = hs_ref[0, g]
            hgb = hg.astype(BF16)
            yoff = _dot_nn(Cg, hgb) * e_full[:, sl]
            dhn = dh_scr[g]
            dhnb = dhn.astype(BF16)
            dYE = (dY * e_full[:, sl]).astype(BF16)
            dC = _dot_nt(dYE, hgb)
            dh_direct = _dot_tn(Cg, dYE)
            dXdd = _dot_nn(Bg, dhnb)
            dB = _dot_nt((xd_g * dec_g).astype(BF16), dhnb)
            dcd = jnp.sum(dhn * hg, axis=0, keepdims=True)
            dh_scr[g] = dh_direct + cd_g * dhn
            dYb = dY.astype(BF16)
            xd_b = xd_g.astype(BF16)
            dCB = jnp.zeros((Q, Q), F32)
            dXd = dXdd * dec_g
            for r in range(HEADS_PER_GROUP):
                h = g * HEADS_PER_GROUP + r
                Lm = _decay_matrix(cs, csT, h, causal)
                Gf = CB * Lm
                dYr = jnp.where(lane_head == r, dYb, jnp.zeros_like(dYb))
                dG = _dot_nt(dYr, xd_b)
                dCB = dCB + dG * Lm
                dXd = dXd + _dot_tn(Gf.astype(BF16), dYr)
                Mm = dG * Gf
                dcs16 = dcs16 + jnp.where(lane16 == h, jnp.sum(Mm, axis=1, keepdims=True), 0.0)
                col_sums = col_sums + jnp.where(sub16 == h, jnp.sum(Mm, axis=0, keepdims=True), 0.0)
            dCBb = dCB.astype(BF16)
            dC = dC + _dot_nn(dCBb, Bg)
            dB = dB + _dot_tn(dCBb, Cg)
            w_state = dXdd * dec_g * xd_g
            t_last = jnp.sum(w_state, axis=0, keepdims=True) + dcd * cd_g
            dcs_g = dY * yoff - w_state + jnp.where(is_last, t_last, 0.0)
            dcs16 = dcs16 + _hdot(dcs_g, red, "a")
            ddtx16 = ddtx16 + _hdot(dXd * xs_g, red, "a")
            dD16 = dD16 + _hdot(jnp.broadcast_to(jnp.sum(dY * xs_g, axis=0, keepdims=True), (8, GROUP_W)), red, "a")
            dxbc_ref[:, sl] = dXd * dt_full[:, sl] + dY * d_g
            dxbc_ref[:, D_INNER + g * D_STATE:D_INNER + (g + 1) * D_STATE] = dB
            dxbc_ref[:, D_INNER + 512 + g * D_STATE:D_INNER + 512 + (g + 1) * D_STATE] = dC
        eye = (lax.broadcasted_iota(jnp.int32, (N_HEADS, N_HEADS), 0)
               == lax.broadcasted_iota(jnp.int32, (N_HEADS, N_HEADS), 1)).astype(BF16)
        dcs16 = dcs16 - sum(_dot_tn(part, eye) for part in _split3(col_sums))
        da = _hdot(triu, dcs16, "b")
        ddt = da * A + ddtx16
        ddt_raw = ddt * _sigmoid(xdt)
        pr = lax.broadcasted_iota(jnp.int32, (N_HEADS, 128), 0)
        pc = lax.broadcasted_iota(jnp.int32, (N_HEADS, 128), 1)
        dz_ref[:, D_INNER:OFF_DT] = jnp.zeros((Q, OFF_DT - D_INNER), BF16)
        dz_ref[:, OFF_DT:OFF_DT + 128] = _hdot(ddt_raw, (pr == pc).astype(F32), "a").astype(BF16)
        dz_ref[:, OFF_DT + 128:] = jnp.zeros((Q, NP - OFF_DT - 128), BF16)
        ddtb_ref[...] += jnp.sum(ddt_raw, axis=0, keepdims=True)
        dal_ref[...] += jnp.sum(da * dt, axis=0, keepdims=True) * A
        dD_ref[...] += dD16[0:1, :]

    def rc(c):
        return nc - 1 - c

    in_specs = _ssd_in_specs(nc, True) + [
        pl.BlockSpec((CHUNK, D_INNER), lambda c: (rc(c), 0)),
        pl.BlockSpec((1, N_GROUPS, D_STATE, GROUP_W), lambda c: (rc(c), 0, 0, 0)),
        pl.BlockSpec((CHUNK, D_INNER), lambda c: (rc(c), 0)),
    ]
    small = pl.BlockSpec((1, N_HEADS), lambda c: (0, 0))
    return pl.pallas_call(
        body, grid=(nc,), in_specs=in_specs,
        out_specs=[pl.BlockSpec((CHUNK, NP), lambda c: (rc(c), 0)),
                   pl.BlockSpec((CHUNK, CONV_DIM), lambda c: (rc(c), 0)),
                   small, small, small,
                   pl.BlockSpec((1, D_INNER), lambda c: (0, 0))],
        out_shape=[jax.ShapeDtypeStruct((T, NP), BF16), jax.ShapeDtypeStruct((T, CONV_DIM), F32),
                   jax.ShapeDtypeStruct((1, N_HEADS), F32), jax.ShapeDtypeStruct((1, N_HEADS), F32),
                   jax.ShapeDtypeStruct((1, N_HEADS), F32), jax.ShapeDtypeStruct((1, D_INNER), F32)],
        scratch_shapes=[pltpu.VMEM((N_GROUPS, D_STATE, GROUP_W), F32)],
        name=name, compiler_params=_cparams(("arbitrary",)),
    )(xbc, xbc, xbc, proj, proj, dtT, dtb, dtbT, alog, alogT, dfull, ng, ypre, hs, dy)


N_PAIRS = ATTN_W // 128
PAIRS_PER_KV = N_PAIRS // 2
ATTN_SCALE = HEAD_DIM ** -0.5


def _kv_variants(kk):
    lo = lax.broadcasted_iota(jnp.int32, kk.shape, 1) < HEAD_DIM
    zero = jnp.zeros_like(kk)
    k00 = jnp.where(lo, kk, zero)
    k11 = jnp.where(lo, zero, kk)
    k01 = pltpu.roll(k00, HEAD_DIM, axis=1)
    k10 = pltpu.roll(k11, HEAD_DIM, axis=1)
    return [[k00.astype(BF16), k01.astype(BF16)], [k10.astype(BF16), k11.astype(BF16)]]


def _attn_valid(n):
    i = lax.broadcasted_iota(jnp.int32, (WINDOW, 2 * WINDOW), 0)
    j = lax.broadcasted_iota(jnp.int32, (WINDOW, 2 * WINDOW), 1)
    return (j > i) & (j <= i + WINDOW) & (n * WINDOW + j >= WINDOW)


def _attn_probs(qp, kvar, valid, sk):
    s = _dot_nt(qp, kvar) * ATTN_SCALE
    s = jnp.where(valid, s, NEG)
    m = jnp.maximum(jnp.max(s, axis=1, keepdims=True), sk)
    pe = jnp.exp(s - m)
    es = jnp.exp(sk - m)
    den = jnp.sum(pe, axis=1, keepdims=True) + es
    inv = 1.0 / den
    return pe * inv, es * inv


def _sink(sinks, r):
    lane = lax.broadcasted_iota(jnp.int32, sinks.shape, 1)
    return jnp.sum(jnp.where(lane == r, sinks, 0.0), axis=1, keepdims=True)


def _kv_specs():
    return [pl.BlockSpec((WINDOW, KV_W), lambda n: (jnp.maximum(n - 1, 0), OFF_K // KV_W)),
            pl.BlockSpec((WINDOW, KV_W), lambda n: (n, OFF_K // KV_W)),
            pl.BlockSpec((WINDOW, KV_W), lambda n: (jnp.maximum(n - 1, 0), OFF_V // KV_W)),
            pl.BlockSpec((WINDOW, KV_W), lambda n: (n, OFF_V // KV_W))]


def _attn_fwd(proj, sinks, og, ycat, name):
    T = proj.shape[0]
    nb = T // WINDOW

    def body(q_ref, kp_ref, kc_ref, vp_ref, vc_ref, s_ref, og_ref, _, y_ref, o_ref):
        n = pl.program_id(0)
        kv = _kv_variants(jnp.concatenate([kp_ref[...], kc_ref[...]], axis=0))
        vv = _kv_variants(jnp.concatenate([vp_ref[...], vc_ref[...]], axis=0))
        valid = _attn_valid(n)
        sinks_v = s_ref[...]
        ssq = jnp.zeros((WINDOW, 1), F32)
        for p in range(N_PAIRS):
            j = p // PAIRS_PER_KV
            qp = q_ref[:, p * 128:(p + 1) * 128].astype(BF16)
            o_pair = jnp.zeros((WINDOW, 128), F32)
            for par in range(2):
                pn, _ = _attn_probs(qp, kv[j][par], valid, _sink(sinks_v, 2 * p + par))
                o_pair = o_pair + _dot_nn(pn.astype(BF16), vv[j][par])
            o_ref[:, p * 128:(p + 1) * 128] = o_pair
            ssq = ssq + jnp.sum(o_pair * o_pair, axis=1, keepdims=True)
        rn = lax.rsqrt(ssq * (1.0 / ATTN_W) + EPS)
        y_ref[...] = (o_ref[...] * rn * og_ref[...]).astype(BF16)

    return pl.pallas_call(
        body, grid=(nb,),
        in_specs=[pl.BlockSpec((WINDOW, ATTN_W), lambda n: (n, OFF_Q // ATTN_W)), *_kv_specs(),
                  pl.BlockSpec((1, N_HEADS), lambda n: (0, 0)), pl.BlockSpec((1, ATTN_W), lambda n: (0, 0)), ANY],
        out_specs=[pl.BlockSpec((WINDOW, ATTN_W), lambda n: (n, 1)), pl.BlockSpec((WINDOW, ATTN_W), lambda n: (n, 0))],
        out_shape=[jax.ShapeDtypeStruct(ycat.shape, BF16), jax.ShapeDtypeStruct((T, ATTN_W), F32)],
        input_output_aliases={7: 0}, name=name, compiler_params=_cparams(("parallel",)),
    )(proj, proj, proj, proj, proj, sinks, og, ycat)


def _attn_bwd(proj, sinks, og, o, dy, dproj, name):
    T = proj.shape[0]
    nb = T // WINDOW

    def body(q_ref, kp_ref, kc_ref, vp_ref, vc_ref, s_ref, og_ref, o_ref, dy_ref, _,
             dq_ref, dk_ref, dv_ref, ds_ref, dog_ref):
        n = pl.program_id(0)

        @pl.when(n == 0)
        def _():
            dk_ref[...] = jnp.zeros_like(dk_ref)
            dv_ref[...] = jnp.zeros_like(dv_ref)
            ds_ref[...] = jnp.zeros_like(ds_ref)
            dog_ref[...] = jnp.zeros_like(dog_ref)

        kv = _kv_variants(jnp.concatenate([kp_ref[...], kc_ref[...]], axis=0))
        vv = _kv_variants(jnp.concatenate([vp_ref[...], vc_ref[...]], axis=0))
        valid = _attn_valid(n)
        sinks_v = s_ref[...]
        of = o_ref[...]
        rn = lax.rsqrt(jnp.mean(of * of, axis=-1, keepdims=True) + EPS)
        oh = of * rn
        dyf = dy_ref[...]
        dog_ref[...] += jnp.sum(dyf * oh, axis=0, keepdims=True)
        doh = dyf * og_ref[...]
        do = rn * (doh - oh * jnp.mean(doh * oh, axis=-1, keepdims=True))
        lane = lax.broadcasted_iota(jnp.int32, (1, 128), 1)
        lane16 = lax.broadcasted_iota(jnp.int32, (1, N_HEADS), 1)
        dk_acc = [[jnp.zeros((2 * WINDOW, 128), F32) for _ in range(2)] for _ in range(2)]
        dv_acc = [[jnp.zeros((2 * WINDOW, 128), F32) for _ in range(2)] for _ in range(2)]
        dsink = jnp.zeros((1, N_HEADS), F32)
        for p in range(N_PAIRS):
            j = p // PAIRS_PER_KV
            qp = q_ref[:, p * 128:(p + 1) * 128].astype(BF16)
            do_p = do[:, p * 128:(p + 1) * 128]
            o_p = of[:, p * 128:(p + 1) * 128]
            do_b = do_p.astype(BF16)
            prod = do_p * o_p
            dq_pair = jnp.zeros((WINDOW, 128), F32)
            for par in range(2):
                r = 2 * p + par
                half = (lane < HEAD_DIM) if par == 0 else (lane >= HEAD_DIM)
                pn, ps = _attn_probs(qp, kv[j][par], valid, _sink(sinks_v, r))
                delta = jnp.sum(jnp.where(half, prod, 0.0), axis=1, keepdims=True)
                dP = _dot_nt(do_b, vv[j][par])
                dS = pn * (dP - delta)
                dsink = dsink + jnp.where(lane16 == r, -jnp.sum(ps * delta, axis=0, keepdims=True), 0.0)
                dSb = (dS * ATTN_SCALE).astype(BF16)
                dq_pair = dq_pair + _dot_nn(dSb, kv[j][par])
                dk_acc[j][par] = dk_acc[j][par] + _dot_tn(dSb, jnp.where(half, qp, jnp.zeros_like(qp)))
                dv_acc[j][par] = dv_acc[j][par] + _dot_tn(pn.astype(BF16), jnp.where(half, do_b, jnp.zeros_like(do_b)))
            dq_ref[:, p * 128:(p + 1) * 128] = dq_pair.astype(BF16)
        dkk = (dk_acc[0][0] + pltpu.roll(dk_acc[0][1], HEAD_DIM, axis=1)
               + dk_acc[1][1] + pltpu.roll(dk_acc[1][0], HEAD_DIM, axis=1))
        dvv = (dv_acc[0][0] + pltpu.roll(dv_acc[0][1], HEAD_DIM, axis=1)
               + dv_acc[1][1] + pltpu.roll(dv_acc[1][0], HEAD_DIM, axis=1))
        prev = pl.multiple_of(jnp.maximum(n - 1, 0) * WINDOW, WINDOW)
        own = pl.multiple_of(n * WINDOW, WINDOW)
        dk_ref[pl.ds(prev, WINDOW), :] += dkk[:WINDOW]
        dk_ref[pl.ds(own, WINDOW), :] += dkk[WINDOW:]
        dv_ref[pl.ds(prev, WINDOW), :] += dvv[:WINDOW]
        dv_ref[pl.ds(own, WINDOW), :] += dvv[WINDOW:]
        ds_ref[...] += dsink

    full_kv = pl.BlockSpec((T, KV_W), lambda n: (0, 0))
    blk = pl.BlockSpec((WINDOW, ATTN_W), lambda n: (n, 0))
    return pl.pallas_call(
        body, grid=(nb,),
        in_specs=[pl.BlockSpec((WINDOW, ATTN_W), lambda n: (n, OFF_Q // ATTN_W)), *_kv_specs(),
                  pl.BlockSpec((1, N_HEADS), lambda n: (0, 0)), pl.BlockSpec((1, ATTN_W), lambda n: (0, 0)),
                  blk, pl.BlockSpec((WINDOW, ATTN_W), lambda n: (n, 1)), ANY],
        out_specs=[pl.BlockSpec((WINDOW, ATTN_W), lambda n: (n, OFF_Q // ATTN_W)), full_kv, full_kv,
                   pl.BlockSpec((1, N_HEADS), lambda n: (0, 0)), pl.BlockSpec((1, ATTN_W), lambda n: (0, 0))],
        out_shape=[jax.ShapeDtypeStruct(dproj.shape, BF16), jax.ShapeDtypeStruct((T, KV_W), F32),
                   jax.ShapeDtypeStruct((T, KV_W), F32), jax.ShapeDtypeStruct((1, N_HEADS), F32),
                   jax.ShapeDtypeStruct((1, ATTN_W), F32)],
        input_output_aliases={9: 0}, name=name, compiler_params=_cparams(("arbitrary",)),
    )(proj, proj, proj, proj, proj, sinks, og, o, dy, dproj)


ANY = pl.BlockSpec(memory_space=pl.ANY)


def _coords():
    return lax.axis_index("x"), lax.axis_index("y"), lax.axis_index("c")


def _all_gather(arrs, name):
    n = len(arrs)

    def body(*refs):
        ins, outs = refs[:n], refs[n:2 * n]
        send_sems, recv_sems, local_sems = refs[2 * n:]
        x, y, c = _coords()
        me = 4 * x + 2 * y + c
        sibling = (x, y, 1 - c)
        chips = [(1 - x, y), (x, 1 - y), (1 - x, 1 - y)]

        def copy(a, k, block, to, src=None):
            dst = outs[a].at[block]
            return pltpu.make_async_remote_copy(
                src_ref=dst if src is None else src, dst_ref=dst, send_sem=send_sems.at[a, k],
                recv_sem=recv_sems.at[a, k], device_id=to, device_id_type=MESH)

        mine = [pltpu.make_async_copy(ins[a], outs[a].at[me], local_sems.at[a]) for a in range(n)]
        for cp in mine:
            cp.start()
        first = []
        for a in range(n):
            first.append(copy(a, 0, me, sibling, src=ins[a]))
            for j, chip in enumerate(chips):
                first.append(copy(a, 1 + j, me, (*chip, c), src=ins[a]))
        for cp in first:
            cp.start()
        passed = []
        for j, (px, py) in enumerate(chips):
            blk = 4 * px + 2 * py + c
            for a in range(n):
                copy(a, 1 + j, blk, sibling).wait_recv()
                fwd = copy(a, 4 + j, blk, sibling)
                fwd.start()
                passed.append(fwd)
        for a in range(n):
            copy(a, 0, 4 * x + 2 * y + (1 - c), sibling).wait_recv()
            for j, (px, py) in enumerate(chips):
                copy(a, 4 + j, 4 * px + 2 * py + (1 - c), sibling).wait_recv()
        for cp in first + passed:
            cp.wait_send()
        for cp in mine:
            cp.wait()

    return pl.pallas_call(
        body, in_specs=[ANY] * n, out_specs=[ANY] * n,
        out_shape=[jax.ShapeDtypeStruct((N_DEV,) + a.shape, a.dtype) for a in arrs],
        scratch_shapes=[pltpu.SemaphoreType.DMA((n, 7)), pltpu.SemaphoreType.DMA((n, 7)),
                        pltpu.SemaphoreType.DMA((n,))],
        name=name,
    )(*arrs)


def _exchange_pair(arrs, name):
    n = len(arrs)

    def body(*refs):
        ins, outs = refs[:n], refs[n:2 * n]
        send_sems, recv_sems = refs[2 * n:]
        x, y, c = _coords()
        cps = []
        for a in range(n):
            for q in range(4):
                cps.append(pltpu.make_async_remote_copy(
                    src_ref=ins[a].at[2 * q + (1 - c)], dst_ref=outs[a].at[q], send_sem=send_sems.at[a, q],
                    recv_sem=recv_sems.at[a, q], device_id=(x, y, 1 - c), device_id_type=MESH))
        for cp in cps:
            cp.start()
        for cp in cps:
            cp.wait()

    return pl.pallas_call(
        body, in_specs=[ANY] * n, out_specs=[ANY] * n,
        out_shape=[jax.ShapeDtypeStruct((4,) + a.shape[1:], a.dtype) for a in arrs],
        scratch_shapes=[pltpu.SemaphoreType.DMA((n, 4)), pltpu.SemaphoreType.DMA((n, 4))],
        name=name,
    )(*arrs)


def _exchange_chips(arrs, name):
    n = len(arrs)

    def body(*refs):
        ins, outs = refs[:n], refs[n:2 * n]
        send_sems, recv_sems = refs[2 * n:]
        x, y, c = _coords()
        chips = [(1 - x, y), (x, 1 - y), (1 - x, 1 - y)]
        cps = []
        for a in range(n):
            for k, (tx, ty) in enumerate(chips):
                cps.append(pltpu.make_async_remote_copy(
                    src_ref=ins[a].at[2 * tx + ty], dst_ref=outs[a].at[k], send_sem=send_sems.at[a, k],
                    recv_sem=recv_sems.at[a, k], device_id=(tx, ty, c), device_id_type=MESH))
        for cp in cps:
            cp.start()
        for cp in cps:
            cp.wait()

    return pl.pallas_call(
        body, in_specs=[ANY] * n, out_specs=[ANY] * n,
        out_shape=[jax.ShapeDtypeStruct((3,) + a.shape[1:], a.dtype) for a in arrs],
        scratch_shapes=[pltpu.SemaphoreType.DMA((n, 3)), pltpu.SemaphoreType.DMA((n, 3))],
        name=name,
    )(*arrs)


HBM = pl.BlockSpec(memory_space=pltpu.HBM)
SEM = pl.BlockSpec(memory_space=pltpu.SEMAPHORE)
EFFECT = pltpu.SideEffectType.DATAFLOW_SIDE_EFFECTING


def _in_hbm(a):
    return pltpu.with_memory_space_constraint(a, pltpu.HBM)


def _remote_start(srcs, lands, plan, n_copies, name, after=None):
    n = len(srcs)
    n_after = 0 if after is None else 1

    def body(*refs):
        src_refs, land_refs = refs[:n], refs[n:2 * n]
        send_sems, recv_sems = refs[2 * n + n_after], refs[2 * n + n_after + 1]
        token = refs[-1]
        x, y, c = _coords()
        for i, (sv, dv, dev) in enumerate(plan(src_refs, land_refs, x, y, c)):
            pltpu.make_async_remote_copy(src_ref=sv, dst_ref=dv, send_sem=send_sems.at[i], recv_sem=recv_sems.at[i],
                                         device_id=dev, device_id_type=MESH).start()
        token[...] = jnp.zeros_like(token)

    bufs = list(srcs) + list(lands)
    outs = pl.pallas_call(
        body, name=name,
        out_shape=(pltpu.SemaphoreType.DMA((n_copies,)), pltpu.SemaphoreType.DMA((n_copies,)),
                   *[pltpu.HBM(b.shape, b.dtype) for b in bufs], jax.ShapeDtypeStruct((8, 128), F32)),
        in_specs=[HBM] * (2 * n) + [ANY] * n_after,
        out_specs=(SEM, SEM, *[HBM] * (2 * n), pl.BlockSpec(memory_space=pltpu.VMEM)),
        input_output_aliases={i: 2 + i for i in range(2 * n)},
        compiler_params=pltpu.CompilerParams(has_side_effects=EFFECT),
    )(*[_in_hbm(b) for b in bufs], *([] if after is None else [after]))
    return outs[0], outs[1], list(outs[2:2 + n]), list(outs[2 + n:2 + 2 * n]), outs[-1]


def _remote_wait(started, after, plan, name):
    send_sems, recv_sems, srcs, lands, _ = started
    n = len(srcs)

    def body(*refs):
        src_refs, land_refs = refs[:n], refs[n:2 * n]
        send_sems, recv_sems = refs[2 * n], refs[2 * n + 1]
        x, y, c = _coords()
        for i, (sv, dv, dev) in enumerate(plan(src_refs, land_refs, x, y, c)):
            cp = pltpu.make_async_remote_copy(src_ref=sv, dst_ref=dv, send_sem=send_sems.at[i],
                                              recv_sem=recv_sems.at[i], device_id=dev, device_id_type=MESH)
            cp.wait_send()
            cp.wait_recv()

    bufs = list(srcs) + list(lands)
    outs = pl.pallas_call(
        body, name=name, out_shape=tuple(pltpu.HBM(b.shape, b.dtype) for b in bufs),
        in_specs=[HBM] * (2 * n) + [SEM, SEM, ANY], out_specs=tuple([HBM] * (2 * n)),
        input_output_aliases={i: i for i in range(2 * n)},
        compiler_params=pltpu.CompilerParams(has_side_effects=EFFECT),
    )(*bufs, send_sems, recv_sems, after)
    return list(outs[:n]), list(outs[n:])


def _gather_plan(src_refs, land_refs, x, y, c):
    me = 4 * x + 2 * y + c
    plan = []
    for s, l in zip(src_refs, land_refs):
        for dev in [(x, y, 1 - c), (1 - x, y, c), (x, 1 - y, c), (1 - x, 1 - y, c)]:
            plan.append((s, l.at[me], dev))
    return plan


def _pair_plan(src_refs, land_refs, x, y, c):
    plan = []
    for s, l in zip(src_refs, land_refs):
        for q in range(4):
            plan.append((s.at[2 * q + (1 - c)], l.at[q], (x, y, 1 - c)))
    return plan


def _pair4_plan(src_refs, land_refs, x, y, c):
    plan = []
    for s, l in zip(src_refs, land_refs):
        for q in range(4):
            plan.append((s.at[q], l.at[q], (x, y, 1 - c)))
    return plan


def _chips_plan(src_refs, land_refs, x, y, c):
    plan = []
    for s, l in zip(src_refs, land_refs):
        for k, (tx, ty) in enumerate([(1 - x, y), (x, 1 - y), (1 - x, 1 - y)]):
            plan.append((s.at[2 * tx + ty], l.at[k], (tx, ty, c)))
    return plan


def _everyone_plan(src_refs, land_refs, x, y, c):
    me = 4 * x + 2 * y + c
    plan = []
    for s, l in zip(src_refs, land_refs):
        for fx, fy, fc in [(0, 0, 1), (1, 0, 0), (1, 0, 1), (0, 1, 0), (0, 1, 1), (1, 1, 0), (1, 1, 1)]:
            dev = ((1 - x) if fx else x, (1 - y) if fy else y, (1 - c) if fc else c)
            plan.append((s, l.at[me], dev))
    return plan


def _gather_finish(gathered, name):
    n = len(gathered)

    def body(*refs):
        outs = refs[n:2 * n]
        send_sems, recv_sems = refs[2 * n:]
        x, y, c = _coords()
        cps = []
        for a in range(n):
            for j, (px, py) in enumerate([(1 - x, y), (x, 1 - y), (1 - x, 1 - y)]):
                blk = outs[a].at[4 * px + 2 * py + c]
                got = outs[a].at[4 * px + 2 * py + (1 - c)]
                cps.append((pltpu.make_async_remote_copy(
                    src_ref=blk, dst_ref=blk, send_sem=send_sems.at[a, j], recv_sem=recv_sems.at[a, j],
                    device_id=(x, y, 1 - c), device_id_type=MESH), pltpu.make_async_remote_copy(
                    src_ref=got, dst_ref=got, send_sem=send_sems.at[a, j], recv_sem=recv_sems.at[a, j],
                    device_id=(x, y, 1 - c), device_id_type=MESH)))
        for cp, _ in cps:
            cp.start()
        for cp, arrival in cps:
            cp.wait_send()
            arrival.wait_recv()

    return pl.pallas_call(
        body, in_specs=[ANY] * n, out_specs=[ANY] * n,
        out_shape=[jax.ShapeDtypeStruct(g.shape, g.dtype) for g in gathered],
        input_output_aliases={a: a for a in range(n)},
        scratch_shapes=[pltpu.SemaphoreType.DMA((n, 3)), pltpu.SemaphoreType.DMA((n, 3))],
        name=name,
    )(*gathered)


def _pair_add(g8, r1, csel, tr, name):
    _, R, C = r1.shape
    g4 = g8.reshape(4, 2, R, C)

    def body(c_ref, g_ref, r_ref, o_ref):
        o_ref[...] = (g_ref[...].astype(F32) + r_ref[...].astype(F32)).astype(BF16)

    return pl.pallas_call(
        body,
        grid_spec=pltpu.PrefetchScalarGridSpec(
            num_scalar_prefetch=1, grid=(4, R // tr),
            in_specs=[pl.BlockSpec((None, None, tr, C), lambda q, i, cs: (q, cs[0], i, 0)),
                      pl.BlockSpec((None, tr, C), lambda q, i, cs: (q, i, 0))],
            out_specs=pl.BlockSpec((None, tr, C), lambda q, i, cs: (q, i, 0))),
        out_shape=jax.ShapeDtypeStruct((4, R, C), BF16), name=name,
        compiler_params=_cparams(("parallel", "parallel")),
    )(csel, g4, r1)


def _adamw_math(w, g, m, v):
    m = ADAM_B1 * m + (1.0 - ADAM_B1) * g
    v = ADAM_B2 * v + (1.0 - ADAM_B2) * (g * g)
    m_hat = m / (1.0 - ADAM_B1 ** ADAM_STEP)
    v_hat = v / (1.0 - ADAM_B2 ** ADAM_STEP)
    delta = -ADAM_LR * (m_hat / (jnp.sqrt(v_hat) + ADAM_EPS) + ADAM_WD * w)
    return delta, m, v


def _adamw_big(w, m, v, p4, r3, qsel, tile, name):
    R, C = w.shape
    tr, tc = tile

    def body(q_ref, w_ref, m_ref, v_ref, p_ref, r_ref, g_out, d_out, m_out, v_out):
        g = p_ref[...].astype(F32) + r_ref[0].astype(F32) + r_ref[1].astype(F32) + r_ref[2].astype(F32)
        d, mn, vn = _adamw_math(w_ref[...], g, m_ref[...], v_ref[...])
        g_out[...] = g
        d_out[...] = d
        m_out[...] = mn
        v_out[...] = vn

    blk = pl.BlockSpec((tr, tc), lambda i, j, qs: (i, j))
    return pl.pallas_call(
        body,
        grid_spec=pltpu.PrefetchScalarGridSpec(
            num_scalar_prefetch=1, grid=(R // tr, C // tc),
            in_specs=[blk, blk, blk, pl.BlockSpec((None, tr, tc), lambda i, j, qs: (qs[0], i, j)),
                      pl.BlockSpec((3, tr, tc), lambda i, j, qs: (0, i, j))],
            out_specs=[blk, blk, blk, blk]),
        out_shape=[jax.ShapeDtypeStruct((R, C), F32)] * 4, name=name,
        compiler_params=_cparams(("parallel", "parallel")),
    )(qsel, w, m, v, p4, r3)


def _small_sum(parts, name):
    def body(p_ref, o_ref):
        acc = p_ref[0]
        for d in range(1, N_DEV):
            acc = acc + p_ref[d]
        o_ref[...] = acc

    return pl.pallas_call(
        body, out_shape=jax.ShapeDtypeStruct(parts.shape[1:], F32), name=name,
        compiler_params=_cparams(),
    )(parts)


def _adamw_small(w, g, m, v, name):
    def body(w_ref, g_ref, m_ref, v_ref, d_out, m_out, v_out):
        d, mn, vn = _adamw_math(w_ref[...], g_ref[...], m_ref[...], v_ref[...])
        d_out[...] = d
        m_out[...] = mn
        v_out[...] = vn

    return pl.pallas_call(
        body, out_shape=[jax.ShapeDtypeStruct(w.shape, F32)] * 3, name=name, compiler_params=_cparams(),
    )(w, g, m, v)


def _row(*pieces):
    r = jnp.concatenate([p.reshape(1, -1) for p in pieces], axis=1)
    return jnp.pad(r, ((0, 0), (0, D_MODEL - r.shape[1])))


def _pack_small(mix, convb, ssmg, attng, mlpg, fing, convw, dtb, alog, dsk, sinks, extra=None):
    last = [dtb, alog, dsk, sinks] + ([extra] if extra is not None else [])
    rows = [_row(mix), _row(convb), _row(ssmg, attng), _row(mlpg), _row(fing),
            jnp.pad(convw, ((0, 0), (0, D_MODEL - convw.shape[1]))), _row(*last)]
    packed = jnp.concatenate(rows, axis=0)
    return jnp.pad(packed, ((0, SMALL_ROWS - packed.shape[0]), (0, 0)))


def _unpack_small(p, conv_n):
    return dict(
        mix_norm_g=p[0:1, :], conv_b=p[1:2, :], ssm_norm_g=p[2:3, :D_INNER], attn_out_norm_g=p[2:3, D_INNER:],
        mlp_norm_g=p[3:4, :], final_norm_g=p[4, :], conv_w=p[5:9, :conv_n][None],
        dt_bias=p[9:10, 0:16], A_log=p[9:10, 16:32], D_skip=p[9:10, 32:48], attn_sinks=p[9:10, 48:64])


SMALL_NAMES = ["mix_norm_g", "conv_w", "conv_b", "dt_bias", "A_log", "D_skip", "ssm_norm_g", "attn_sinks",
               "attn_out_norm_g", "mlp_norm_g", "final_norm_g"]
WEIGHT_ORDER = ["mix_norm_g", "w_in", "conv_w", "conv_b", "dt_bias", "A_log", "D_skip", "ssm_norm_g", "attn_sinks",
                "attn_out_norm_g", "w_out", "mlp_norm_g", "w_up", "w_down", "final_norm_g"]


def _to_my_columns(w_nat):
    pad = jnp.zeros((w_nat.shape[0], NP - IN_PROJ), w_nat.dtype)
    return jnp.concatenate([w_nat[:, :NAT_DT], w_nat[:, NAT_DT + N_HEADS:], w_nat[:, NAT_DT:NAT_DT + N_HEADS], pad],
                           axis=1)


def _to_natural_columns(w_my):
    return jnp.concatenate([w_my[:, :NAT_DT], w_my[:, OFF_DT:OFF_DT + N_HEADS], w_my[:, NAT_DT:OFF_DT]], axis=1)


SLAB = 1024


def _grad_w_up(h2, du, name, sel=None, add=None, after=None):
    T, D = h2.shape
    if sel is None:
        pick, n_slab, pre = (lambda j, *cs: j), N_DEV, None
    else:
        pre, other = sel
        pick, n_slab = (lambda j, cs: 2 * j + ((1 - cs[0]) if other else cs[0])), 4
    o_spec = pl.BlockSpec((None, SLAB, SLAB), lambda i, j, k, *cs: (j, i, 0))
    return _matmul(
        h2, du, mode="tn", grid=(D // SLAB, n_slab, 1),
        a_spec=pl.BlockSpec((T, SLAB), lambda i, j, k, *cs: (0, i)),
        b_spec=pl.BlockSpec((T, SLAB), lambda i, j, k, *cs: (0, pick(j, *cs))),
        out_shapes=[jax.ShapeDtypeStruct((n_slab, D, SLAB), BF16)], out_specs=[o_spec], tile=(SLAB, SLAB), name=name,
        extras=() if add is None else (add,), extra_specs=() if add is None else (o_spec,),
        epilogue=None if add is None else (lambda acc, r: (acc + r.astype(F32),)), after=after, prefetch=pre)[0]


def _grad_w_down(act, dx3b, name, sel=None, add=None, after=None):
    T, D = dx3b.shape
    if sel is None:
        pick, n_slab, pre = (lambda i, *cs: i), N_DEV, None
    else:
        pre, other = sel
        pick, n_slab = (lambda i, cs: 2 * i + ((1 - cs[0]) if other else cs[0])), 4
    o_spec = pl.BlockSpec((None, SLAB, SLAB), lambda i, j, k, *cs: (i, 0, j))
    return _matmul(
        act, dx3b, mode="tn", grid=(n_slab, D // SLAB, 1),
        a_spec=pl.BlockSpec((T, SLAB), lambda i, j, k, *cs: (0, pick(i, *cs))),
        b_spec=pl.BlockSpec((T, SLAB), lambda i, j, k, *cs: (0, j)),
        out_shapes=[jax.ShapeDtypeStruct((n_slab, SLAB, D), BF16)], out_specs=[o_spec], tile=(SLAB, SLAB), name=name,
        extras=() if add is None else (add,), extra_specs=() if add is None else (o_spec,),
        epilogue=None if add is None else (lambda acc, r: (acc + r.astype(F32),)), after=after, prefetch=pre)[0]


class _FixedWeights:
    def __init__(self, w_in_p, w_out_f, w_up_s, w_down_f, conv_w_f):
        self.w = (w_in_p, w_out_f, w_up_s, w_down_f, conv_w_f)
        self.grads = {}

    def mixer_weights(self, after):
        return self.w[0], self.w[4], None

    def out_weight(self, after):
        return self.w[1]

    def up_weight(self, after):
        return self.w[2]

    def down_weight(self, after):
        return self.w[3]

    def mlp_grads(self, h2, du, act, dx3b):
        self.grads.update(w_up=_grad_w_up(h2, du, "grad_w_up"),
                          w_down=_grad_w_down(act, dx3b, "grad_w_down").reshape(D_FF, D_MODEL))
        return None

    def out_grad(self, g_out):
        self.grads.update(w_out=g_out)
        return None

    def in_grad(self, g_in):
        self.grads.update(w_in=g_in)
        return None


def _local_step(x, tgt, p, hooks):
    T = x.shape[0]
    D = D_MODEL
    h1 = _rmsnorm_fwd(x, p["mix_norm_g"], "norm_mix")
    w_in_p, conv_w_f, token = hooks.mixer_weights(h1)
    (proj,) = _mm_simple(h1, w_in_p, mode="nn", M=T, N=NP, K=D, tm=min(T, 1024), tn=1536, tk=D, out_dtype=F32,
                         name="in_proj", after=token)
    xbc = _conv_fwd(proj, conv_w_f, p["conv_b"], "conv_fwd")
    dtT = proj[:, OFF_DT:OFF_DT + N_HEADS].T
    dtbT = p["dt_bias"].T
    alogT = p["A_log"].T
    dfull = jnp.repeat(p["D_skip"], HEAD_DIM, axis=1)
    ycat, ypre, hs = _ssd_fwd(xbc, proj, dtT, p["dt_bias"], dtbT, p["A_log"], alogT, dfull, p["ssm_norm_g"],
                              "ssd_fwd")
    ycat, o_att = _attn_fwd(proj, p["attn_sinks"], p["attn_out_norm_g"], ycat, "attn_fwd")
    w_out_f = hooks.out_weight(ycat)
    tm = min(T, 1024)
    (x2,) = _mm_simple(ycat, w_out_f, mode="nn", M=T, N=D, K=D, tm=tm, tn=1024, tk=D, out_dtype=F32, name="out_proj",
                       extras=(x,), epilogue=lambda acc, res: (acc + res,))
    h2 = _rmsnorm_fwd(x2, p["mlp_norm_g"], "norm_mlp")
    w_up_s = hooks.up_weight(h2)
    grid = (T // tm, N_DEV, 1)
    u, act = _matmul(
        h2, w_up_s, mode="nn", grid=grid,
        a_spec=pl.BlockSpec((tm, D), lambda i, j, k: (i, 0)),
        b_spec=pl.BlockSpec((None, D, 1024), lambda i, j, k: (j, 0, 0)),
        out_shapes=[jax.ShapeDtypeStruct((T, D_FF), F32), jax.ShapeDtypeStruct((T, D_FF), BF16)],
        out_specs=[pl.BlockSpec((tm, 1024), lambda i, j, k: (i, j))] * 2, tile=(tm, 1024), name="mlp_up",
        epilogue=lambda acc: (acc, jnp.square(jnp.maximum(acc, 0.0))))
    w_down_f = hooks.down_weight(act)
    (x3,) = _mm_simple(act, w_down_f, mode="nn", M=T, N=D, K=D_FF, tm=tm, tn=1024, tk=2048, out_dtype=F32,
                       name="mlp_down", extras=(x2,), epilogue=lambda acc, res: (acc + res,))
    loss_part, d_fin, dx3, dx3b = _final_loss(x3, tgt, p["final_norm_g"].reshape(1, D), "loss_head")
    (du,) = _mm_simple(dx3b, w_down_f, mode="nt", M=T, N=D_FF, K=D, tm=tm, tn=1024, tk=D, out_dtype=BF16,
                       name="mlp_down_bwd", extras=(u,),
                       epilogue=lambda acc, uu: (acc * (2.0 * jnp.maximum(uu, 0.0)),))
    token = hooks.mlp_grads(h2, du, act, dx3b)
    (dh2,) = _matmul(
        du, w_up_s, mode="nt", grid=(T // tm, D // 1024, N_DEV // 2),
        a_spec=pl.BlockSpec((tm, 2048), lambda i, j, k: (i, k)),
        b_spec=pl.BlockSpec((2, 1024, 1024), lambda i, j, k: (k, j, 0)),
        out_shapes=[jax.ShapeDtypeStruct((T, D), F32)],
        out_specs=[pl.BlockSpec((tm, 1024), lambda i, j, k: (i, j))], tile=(tm, 1024), name="mlp_up_bwd",
        after=token, dot_fn=lambda a, b: _dot_nt(a[:, :1024], b[0]) + _dot_nt(a[:, 1024:], b[1]))
    dx2, dx2b, d_mlp = _rmsnorm_bwd(dh2, x2, p["mlp_norm_g"], dx3, "norm_mlp_bwd")
    (g_out,) = _mm_simple(ycat, dx2b, mode="tn", M=D, N=D, K=T, tm=1024, tn=1024, tk=T, out_dtype=BF16,
                          name="grad_w_out")
    token = hooks.out_grad(g_out)
    (dy,) = _mm_simple(dx2b, w_out_f, mode="nt", M=T, N=D, K=D, tm=tm, tn=1024, tk=D, out_dtype=F32,
                       name="out_proj_bwd", after=token)
    dproj, dxbc_act, d_dtb, d_alog, d_dskip, d_ssmg = _ssd_bwd(
        xbc, proj, dtT, p["dt_bias"], dtbT, p["A_log"], alogT, dfull, p["ssm_norm_g"], ypre, hs, dy, "ssd_bwd")
    dproj, d_convw, d_convb = _conv_bwd(proj, dxbc_act, conv_w_f, p["conv_b"], dproj, "conv_bwd")
    dproj, dk, dv, d_sinks, d_attng = _attn_bwd(proj, p["attn_sinks"], p["attn_out_norm_g"], o_att, dy, dproj,
                                                "attn_bwd")
    dproj = lax.dynamic_update_slice(dproj, jnp.concatenate([dk, dv], axis=1).astype(BF16), (0, OFF_K))
    (g_in,) = _mm_simple(dproj, h1, mode="tn", M=NP, N=D, K=T, tm=1536, tn=1024, tk=T, out_dtype=BF16,
                         name="grad_w_in")
    token = hooks.in_grad(g_in)
    (dh1,) = _mm_simple(dproj, w_in_p, mode="nt", M=T, N=D, K=NP, tm=tm, tn=1024, tk=2304, out_dtype=F32,
                        name="in_proj_bwd", after=token)
    dx, _, d_mix = _rmsnorm_bwd(dh1, x, p["mix_norm_g"], dx2, "norm_mix_bwd")
    small = _pack_small(d_mix, d_convb, d_ssmg, d_attng, d_mlp, d_fin, d_convw, d_dtb, d_alog, d_dskip, d_sinks,
                        extra=loss_part[:, 0:1])
    return dx, small


def _landing(own, me):
    zone = lax.empty((N_DEV,) + own.shape, own.dtype)
    return lax.dynamic_update_slice(zone, own[None], (me,) + (0,) * own.ndim)


def _gather_end(started, after, plan, name):
    _, lands = _remote_wait(started, after, plan, name + "_wait")
    return _gather_finish(lands, name + "_finish")


class _ShardedWeights:
    def __init__(self, w_in, w_out, conv_w, w_up, w_down, me, csel):
        self.me, self.csel = me, csel
        shards = [w_in.astype(BF16), conv_w]
        self.st_mixer = _remote_start(shards, [_landing(s, me) for s in shards], _gather_plan, 4 * len(shards),
                                      "gather_start_mixer")
        order = self.st_mixer[4]
        self.st_later = {}
        for k, wt in [("out", w_out), ("up", w_up), ("down", w_down)]:
            shard = (wt + order[0, 0]).astype(BF16)
            self.st_later[k] = _remote_start([shard], [_landing(shard, me)], _gather_plan, 4, f"gather_start_{k}",
                                             after=order)
            order = self.st_later[k][4]
        self.start_token = order
        self.reduces = {}

    def mixer_weights(self, after):
        g_in, g_conv = _gather_end(self.st_mixer, after, _gather_plan, "gather_mixer")
        per = IN_PROJ // N_DEV
        k, off = NAT_DT // per, NAT_DT % per
        pieces = [g_in[i] for i in range(k)] + [g_in[k][:, :off], g_in[k][:, off + N_HEADS:]]
        pieces += [g_in[i] for i in range(k + 1, N_DEV)]
        pieces += [g_in[k][:, off:off + N_HEADS], jnp.zeros((D_MODEL, NP - IN_PROJ), BF16)]
        w_in_p = jnp.concatenate(pieces, axis=1)
        conv_w_f = jnp.concatenate([g_conv[i] for i in range(N_DEV)], axis=1)
        return w_in_p, conv_w_f, None

    def out_weight(self, after):
        return _gather_end(self.st_later["out"], after, _gather_plan, "gather_out")[0].reshape(D_MODEL, D_MODEL)

    def up_weight(self, after):
        return _gather_end(self.st_later["up"], after, _gather_plan, "gather_up")[0]

    def down_weight(self, after):
        return _gather_end(self.st_later["down"], after, _gather_plan, "gather_down")[0].reshape(D_FF, D_MODEL)

    def _chips_start(self, slabs, from_sibling, rows, tag):
        sums = [_pair_add(s, r, self.csel, tr, f"pair_add_{tag}_{i}")
                for i, (s, r, tr) in enumerate(zip(slabs, from_sibling, rows))]
        lands = [lax.empty((3,) + s.shape[1:], s.dtype) for s in sums]
        self.reduces[tag] = _remote_start(sums, lands, _chips_plan, 3 * len(sums), f"reduce_start_{tag}")
        return self.reduces[tag][4]

    def mlp_grads(self, h2, du, act, dx3b):
        def send(part, tag, after):
            st = _remote_start([part], [lax.empty(part.shape, part.dtype)], _pair4_plan, 4,
                               f"reduce_pair_start_{tag}", after=after)
            return st

        def received(st, after, tag):
            return _remote_wait(st, after, _pair4_plan, f"reduce_pair_wait_{tag}")[1][0]

        def to_chips(sums, tag):
            self.reduces[tag] = _remote_start([sums], [lax.empty((3,) + sums.shape[1:], sums.dtype)], _chips_plan, 3,
                                              f"reduce_start_{tag}")
            return self.reduces[tag][4]

        up_send = _grad_w_up(h2, du, "grad_w_up_send", sel=(self.csel, True))
        st_up = send(up_send, "up", None)
        down_send = _grad_w_down(act, dx3b, "grad_w_down_send", sel=(self.csel, True), after=st_up[4])
        st_down = send(down_send, "down", None)
        up_sum = _grad_w_up(h2, du, "grad_w_up_keep", sel=(self.csel, False), add=received(st_up, down_send, "up"),
                            after=st_down[4])
        token = to_chips(up_sum, "up")
        down_sum = _grad_w_down(act, dx3b, "grad_w_down_keep", sel=(self.csel, False),
                                add=received(st_down, up_sum, "down"), after=token)
        return to_chips(down_sum, "down")

    def out_grad(self, g_out):
        slabs = [g_out.reshape(N_DEV, D_MODEL // N_DEV, D_MODEL)]
        return self._chips_start(slabs, _exchange_pair(slabs, "reduce_pair_out"), [256], "out")

    def in_grad(self, g_in):
        per = IN_PROJ // N_DEV
        nat = jnp.concatenate([g_in[:NAT_DT], g_in[OFF_DT:OFF_DT + N_HEADS], g_in[NAT_DT:OFF_DT]], axis=0)
        slabs = [nat.reshape(N_DEV, per, D_MODEL)]
        return self._chips_start(slabs, _exchange_pair(slabs, "reduce_pair_in"), [per], "in")

    def small_start(self, small):
        self.st_small = _remote_start([small], [_landing(small, self.me)], _everyone_plan, N_DEV - 1, "gather_start_small")

    def small_end(self, after):
        return _remote_wait(self.st_small, after, _everyone_plan, "gather_small_wait")[1][0]

    def reduce_end(self, tag, after):
        return _remote_wait(self.reduces[tag], after, _chips_plan, f"reduce_wait_{tag}")


def kernel(x, mix_norm_g, w_in, conv_w, conv_b, dt_bias, A_log, D_skip, ssm_norm_g, attn_sinks, attn_out_norm_g, w_out, mlp_norm_g, w_up, w_down, final_norm_g, loss_target, m_mix_norm_g, m_w_in, m_conv_w, m_conv_b, m_dt_bias, m_A_log, m_D_skip, m_ssm_norm_g, m_attn_sinks, m_attn_out_norm_g, m_w_out, m_mlp_norm_g, m_w_up, m_w_down, m_final_norm_g, v_mix_norm_g, v_w_in, v_conv_w, v_conv_b, v_dt_bias, v_A_log, v_D_skip, v_ssm_norm_g, v_attn_sinks, v_attn_out_norm_g, v_w_out, v_mlp_norm_g, v_w_up, v_w_down, v_final_norm_g):
    xi, yi, ci = _coords()
    me = 4 * xi + 2 * yi + ci
    csel = jnp.reshape(ci, (1,)).astype(jnp.int32)
    qsel = jnp.reshape(2 * xi + yi, (1,)).astype(jnp.int32)
    w = dict(mix_norm_g=mix_norm_g, conv_b=conv_b, dt_bias=dt_bias, A_log=A_log, D_skip=D_skip,
             ssm_norm_g=ssm_norm_g, attn_sinks=attn_sinks, attn_out_norm_g=attn_out_norm_g, mlp_norm_g=mlp_norm_g,
             final_norm_g=final_norm_g)
    hooks = _ShardedWeights(w_in[0], w_out[0], conv_w[0], w_up[0], w_down[0], me, csel)
    p = dict(w, mix_norm_g=mix_norm_g + hooks.start_token[0:1, 0:1])
    dx, small = _local_step(x[0], loss_target[0], p, hooks)
    hooks.small_start(small)
    big = {}
    after = dx
    for name, wt, mt, vt, tile, transposed in [
            ("up", w_up, m_w_up, v_w_up, (512, SLAB), False), ("down", w_down, m_w_down, v_w_down, (256, D_MODEL), False),
            ("out", w_out, m_w_out, v_w_out, (256, D_MODEL), False),
            ("in", w_in, m_w_in, v_w_in, (IN_PROJ // N_DEV, 512), True)]:
        (chip_sums,), (from_chips,) = hooks.reduce_end(name, after)
        shard = [jnp.transpose(t[0]) if transposed else t[0] for t in (wt, mt, vt)]
        res = _adamw_big(*shard, chip_sums, from_chips, qsel, tile, f"adamw_w_{name}")
        big["w_" + name] = tuple((jnp.transpose(r) if transposed else r)[None] for r in res)
        after = res[0]
    gsum = _small_sum(hooks.small_end(after), "small_sum")
    loss = gsum[9, 64]
    gs = _unpack_small(gsum, CONV_DIM)
    cw = CONV_DIM // N_DEV
    g_conv_shard = lax.dynamic_slice(gsum[5:9, :], (0, me * cw), (CONV_K, cw))

    def pack(s):
        return _pack_small(s["mix_norm_g"], s["conv_b"], s["ssm_norm_g"], s["attn_out_norm_g"], s["mlp_norm_g"],
                           s["final_norm_g"], s["conv_w"][0], s["dt_bias"], s["A_log"], s["D_skip"], s["attn_sinks"])

    wp = pack(dict(w, conv_w=conv_w))
    mp = pack(dict(mix_norm_g=m_mix_norm_g, conv_b=m_conv_b, ssm_norm_g=m_ssm_norm_g,
                   attn_out_norm_g=m_attn_out_norm_g, mlp_norm_g=m_mlp_norm_g, final_norm_g=m_final_norm_g,
                   conv_w=m_conv_w, dt_bias=m_dt_bias, A_log=m_A_log, D_skip=m_D_skip, attn_sinks=m_attn_sinks))
    vp = pack(dict(mix_norm_g=v_mix_norm_g, conv_b=v_conv_b, ssm_norm_g=v_ssm_norm_g,
                   attn_out_norm_g=v_attn_out_norm_g, mlp_norm_g=v_mlp_norm_g, final_norm_g=v_final_norm_g,
                   conv_w=v_conv_w, dt_bias=v_dt_bias, A_log=v_A_log, D_skip=v_D_skip, attn_sinks=v_attn_sinks))
    gp = jnp.concatenate([gsum[0:5], jnp.pad(g_conv_shard, ((0, 0), (0, D_MODEL - cw))), gsum[9:10],
                          jnp.zeros((SMALL_ROWS - 10, D_MODEL), F32)], axis=0)
    dp, mnp, vnp = _adamw_small(wp, gp, mp, vp, "adamw_small")
    grads = dict(gs, conv_w=g_conv_shard[None])
    deltas = _unpack_small(dp, cw)
    new_m = _unpack_small(mnp, cw)
    new_v = _unpack_small(vnp, cw)
    for k, name in enumerate(["w_in", "w_out", "w_up", "w_down"]):
        grads[name], deltas[name], new_m[name], new_v[name] = big[name]
    return (loss, dx[None], *[grads[n] for n in WEIGHT_ORDER], *[deltas[n] for n in WEIGHT_ORDER],
            *[new_m[n] for n in WEIGHT_ORDER], *[new_v[n] for n in WEIGHT_ORDER])
```

```python
import functools

import jax
import jax.numpy as jnp
from jax import lax
from jax.experimental import pallas as pl
from jax.experimental.pallas import tpu as pltpu

F32 = jnp.float32
BF16 = jnp.bfloat16
HI = lax.Precision.HIGHEST
MESH = pl.DeviceIdType.MESH

EPS = 1e-5
D_MODEL = 2048
D_INNER = 1024
N_HEADS = 16
HEAD_DIM = 64
N_GROUPS = 4
D_STATE = 128
CHUNK = 128
CONV_K = 4
CONV_DIM = 2048
ATTN_W = 1024
KV_W = 128
WINDOW = 128
D_FF = 8192
IN_PROJ = 4368
N_DEV = 8
NP = 4608
OFF_Z, OFF_X, OFF_B, OFF_C, OFF_Q, OFF_K, OFF_V, OFF_DT = 0, 1024, 2048, 2560, 3072, 4096, 4224, 4352
NAT_DT = 3072

ADAM_LR = 0.001
ADAM_B1 = 0.9
ADAM_B2 = 0.999
ADAM_EPS = 1e-08
ADAM_WD = 0.01
ADAM_STEP = 10

VMEM_LIMIT = 52 * 1024 * 1024
SMALL_ROWS = 16
NEG = -1e30


def _cparams(sem=None):
    return pltpu.CompilerParams(dimension_semantics=sem, vmem_limit_bytes=VMEM_LIMIT)


def _split3(v):
    hi = v.astype(BF16)
    rest = v - hi.astype(F32)
    mid = rest.astype(BF16)
    return hi, mid, (rest - mid.astype(F32)).astype(BF16)


def _hdot(a, b, data):
    if data == "a":
        sel = b.astype(BF16)
        return sum(_dot_nn(part, sel) for part in _split3(a))
    sel = a.astype(BF16)
    return sum(_dot_nn(sel, part) for part in _split3(b))


def _dot_nn(a, b):
    return lax.dot_general(a, b, (((1,), (0,)), ((), ())), preferred_element_type=F32)


def _dot_nt(a, b):
    return lax.dot_general(a, b, (((1,), (1,)), ((), ())), preferred_element_type=F32)


def _dot_tn(a, b):
    return lax.dot_general(a, b, (((0,), (0,)), ((), ())), preferred_element_type=F32)


def _softplus(v):
    return jnp.maximum(v, 0.0) + jnp.log1p(jnp.exp(-jnp.abs(v)))


def _sigmoid(v):
    return 1.0 / (1.0 + jnp.exp(-v))


def _matmul(a, b, *, mode, grid, a_spec, b_spec, out_shapes, out_specs, tile, name,
            extras=(), extra_specs=(), epilogue=None, after=None, dot_fn=None, prefetch=None):
    nk = grid[2]
    n_ex = len(extras)
    n_out = len(out_shapes)
    dot = dot_fn if dot_fn is not None else {"nn": _dot_nn, "nt": _dot_nt, "tn": _dot_tn}[mode]

    def finish(acc, ex_refs, out_refs):
        res = (acc,) if epilogue is None else epilogue(acc, *[e[...] for e in ex_refs])
        for o, r in zip(out_refs, res):
            o[...] = r.astype(o.dtype)

    def body(*refs):
        a_ref, b_ref = refs[0], refs[1]
        ex_refs = refs[2:2 + n_ex]
        out_refs = refs[2 + n_ex:2 + n_ex + n_out]
        part = dot(a_ref[...].astype(BF16), b_ref[...].astype(BF16))
        if nk == 1:
            finish(part, ex_refs, out_refs)
        else:
            acc_ref = refs[-1]
            k = pl.program_id(2)

            @pl.when(k == 0)
            def _():
                acc_ref[...] = part

            @pl.when(k > 0)
            def _():
                acc_ref[...] += part

            @pl.when(k == nk - 1)
            def _():
                finish(acc_ref[...], ex_refs, out_refs)

    scratch = [] if nk == 1 else [pltpu.VMEM(tile, F32)]
    n_pre = 0 if prefetch is None else 1
    tok_specs = [] if after is None else [pl.BlockSpec((8, 128), lambda *_: (0, 0))]
    tok_args = [] if after is None else [after]

    def body_with_token(*refs):
        refs = refs[n_pre:]
        body(*refs[:2 + n_ex], *refs[2 + n_ex + len(tok_args):])

    in_specs = [a_spec, b_spec, *extra_specs, *tok_specs]
    params = _cparams(("parallel", "parallel", "arbitrary"))
    if prefetch is None:
        return pl.pallas_call(
            body_with_token, grid=grid, in_specs=in_specs, out_specs=list(out_specs), out_shape=list(out_shapes),
            scratch_shapes=scratch, name=name, compiler_params=params)(a, b, *extras, *tok_args)
    return pl.pallas_call(
        body_with_token,
        grid_spec=pltpu.PrefetchScalarGridSpec(num_scalar_prefetch=1, grid=grid, in_specs=in_specs,
                                               out_specs=list(out_specs), scratch_shapes=scratch),
        out_shape=list(out_shapes), name=name, compiler_params=params)(prefetch, a, b, *extras, *tok_args)


def _mm_simple(a, b, *, mode, M, N, K, tm, tn, tk, out_dtype, name, extras=(), epilogue=None, n_out=1,
               out_dtypes=None, after=None):
    grid = (M // tm, N // tn, K // tk)
    if mode == "nn":
        a_spec = pl.BlockSpec((tm, tk), lambda i, j, k: (i, k))
        b_spec = pl.BlockSpec((tk, tn), lambda i, j, k: (k, j))
    elif mode == "nt":
        a_spec = pl.BlockSpec((tm, tk), lambda i, j, k: (i, k))
        b_spec = pl.BlockSpec((tn, tk), lambda i, j, k: (j, k))
    else:
        a_spec = pl.BlockSpec((tk, tm), lambda i, j, k: (k, i))
        b_spec = pl.BlockSpec((tk, tn), lambda i, j, k: (k, j))
    o_spec = pl.BlockSpec((tm, tn), lambda i, j, k: (i, j))
    dts = out_dtypes if out_dtypes is not None else [out_dtype] * n_out
    return _matmul(a, b, mode=mode, grid=grid, a_spec=a_spec, b_spec=b_spec,
                   out_shapes=[jax.ShapeDtypeStruct((M, N), d) for d in dts],
                   out_specs=[o_spec] * len(dts), tile=(tm, tn), name=name,
                   extras=extras, extra_specs=[o_spec] * len(extras), epilogue=epilogue, after=after)


ROW_BLOCK = 256


def _rmsnorm_fwd(x, g, name):
    T, D = x.shape

    def body(x_ref, g_ref, o_ref):
        xf = x_ref[...]
        r = lax.rsqrt(jnp.mean(xf * xf, axis=-1, keepdims=True) + EPS)
        o_ref[...] = (xf * r * g_ref[...]).astype(BF16)

    return pl.pallas_call(
        body, grid=(T // ROW_BLOCK,),
        in_specs=[pl.BlockSpec((ROW_BLOCK, D), lambda i: (i, 0)), pl.BlockSpec((1, D), lambda i: (0, 0))],
        out_specs=pl.BlockSpec((ROW_BLOCK, D), lambda i: (i, 0)),
        out_shape=jax.ShapeDtypeStruct((T, D), BF16), name=name, compiler_params=_cparams(("parallel",)),
    )(x, g)


def _rmsnorm_bwd(dh, x, g, dres, name):
    T, D = x.shape

    def body(dh_ref, x_ref, g_ref, dres_ref, dx_ref, dxb_ref, dg_ref):
        i = pl.program_id(0)
        xf = x_ref[...]
        r = lax.rsqrt(jnp.mean(xf * xf, axis=-1, keepdims=True) + EPS)
        xh = xf * r
        d = dh_ref[...]

        @pl.when(i == 0)
        def _():
            dg_ref[...] = jnp.zeros_like(dg_ref)

        dg_ref[...] += jnp.sum(d * xh, axis=0, keepdims=True)
        dxh = d * g_ref[...]
        dx = r * (dxh - xh * jnp.mean(dxh * xh, axis=-1, keepdims=True)) + dres_ref[...]
        dx_ref[...] = dx
        dxb_ref[...] = dx.astype(BF16)

    row = pl.BlockSpec((ROW_BLOCK, D), lambda i: (i, 0))
    vec = pl.BlockSpec((1, D), lambda i: (0, 0))
    return pl.pallas_call(
        body, grid=(T // ROW_BLOCK,), in_specs=[row, row, vec, row], out_specs=[row, row, vec],
        out_shape=[jax.ShapeDtypeStruct((T, D), F32), jax.ShapeDtypeStruct((T, D), BF16),
                   jax.ShapeDtypeStruct((1, D), F32)],
        name=name, compiler_params=_cparams(("arbitrary",)),
    )(dh, x, g, dres)


def _final_loss(x3, tgt, g, name):
    T, D = x3.shape

    def body(x_ref, t_ref, g_ref, loss_ref, dg_ref, dx_ref, dxb_ref):
        i = pl.program_id(0)
        xf = x_ref[...]
        r = lax.rsqrt(jnp.mean(xf * xf, axis=-1, keepdims=True) + EPS)
        xh = xf * r
        gg = g_ref[...]
        err = xh * gg - t_ref[...]

        @pl.when(i == 0)
        def _():
            dg_ref[...] = jnp.zeros_like(dg_ref)
            loss_ref[...] = jnp.zeros_like(loss_ref)

        part = jnp.sum(jnp.sum(err * err, axis=-1, keepdims=True), axis=0, keepdims=True) * (0.5 / D)
        loss_ref[...] += jnp.broadcast_to(part, loss_ref.shape)
        dout = err * (1.0 / D)
        dg_ref[...] += jnp.sum(dout * xh, axis=0, keepdims=True)
        dxh = dout * gg
        dx = r * (dxh - xh * jnp.mean(dxh * xh, axis=-1, keepdims=True))
        dx_ref[...] = dx
        dxb_ref[...] = dx.astype(BF16)

    row = pl.BlockSpec((ROW_BLOCK, D), lambda i: (i, 0))
    vec = pl.BlockSpec((1, D), lambda i: (0, 0))
    return pl.pallas_call(
        body, grid=(T // ROW_BLOCK,), in_specs=[row, row, vec],
        out_specs=[pl.BlockSpec((1, 128), lambda i: (0, 0)), vec, row, row],
        out_shape=[jax.ShapeDtypeStruct((1, 128), F32), jax.ShapeDtypeStruct((1, D), F32),
                   jax.ShapeDtypeStruct((T, D), F32), jax.ShapeDtypeStruct((T, D), BF16)],
        name=name, compiler_params=_cparams(("arbitrary",)),
    )(x3, tgt, g)


CONV_BLOCK = 256


def _conv_apply(u, w, b):
    row = lax.broadcasted_iota(jnp.int32, u.shape, 0)
    acc = b + w[CONV_K - 1:CONV_K, :] * u
    shifted = []
    for j in range(1, CONV_K):
        uj = jnp.where(row >= j, pltpu.roll(u, j, axis=0), 0.0)
        shifted.append(uj)
        acc = acc + w[CONV_K - 1 - j:CONV_K - j, :] * uj
    return acc, shifted


def _conv_fwd(proj, conv_w, conv_b, name):
    T = proj.shape[0]
    cb0 = OFF_X // CONV_BLOCK

    def body(u_ref, w_ref, b_ref, o_ref):
        c, _ = _conv_apply(u_ref[...], w_ref[...], b_ref[...])
        o_ref[...] = c * _sigmoid(c)

    return pl.pallas_call(
        body, grid=(CONV_DIM // CONV_BLOCK,),
        in_specs=[pl.BlockSpec((T, CONV_BLOCK), lambda j: (0, cb0 + j)),
                  pl.BlockSpec((CONV_K, CONV_BLOCK), lambda j: (0, j)),
                  pl.BlockSpec((1, CONV_BLOCK), lambda j: (0, j))],
        out_specs=pl.BlockSpec((T, CONV_BLOCK), lambda j: (0, j)),
        out_shape=jax.ShapeDtypeStruct((T, CONV_DIM), F32), name=name, compiler_params=_cparams(("parallel",)),
    )(proj, conv_w, conv_b)


def _conv_bwd(proj, dact, conv_w, conv_b, dproj, name):
    T = proj.shape[0]
    cb0 = OFF_X // CONV_BLOCK

    def body(u_ref, d_ref, w_ref, b_ref, _, du_ref, dw_ref, db_ref):
        u = u_ref[...]
        w = w_ref[...]
        c, shifted = _conv_apply(u, w, b_ref[...])
        sg = _sigmoid(c)
        dc = d_ref[...] * sg * (1.0 + c * (1.0 - sg))
        row = lax.broadcasted_iota(jnp.int32, u.shape, 0)
        du = w[CONV_K - 1:CONV_K, :] * dc
        dw_ref[CONV_K - 1:CONV_K, :] = jnp.sum(dc * u, axis=0, keepdims=True)
        for j in range(1, CONV_K):
            dcj = jnp.where(row < T - j, pltpu.roll(dc, T - j, axis=0), 0.0)
            du = du + w[CONV_K - 1 - j:CONV_K - j, :] * dcj
            dw_ref[CONV_K - 1 - j:CONV_K - j, :] = jnp.sum(dc * shifted[j - 1], axis=0, keepdims=True)
        db_ref[...] = jnp.sum(dc, axis=0, keepdims=True)
        du_ref[...] = du.astype(BF16)

    return pl.pallas_call(
        body, grid=(CONV_DIM // CONV_BLOCK,),
        in_specs=[pl.BlockSpec((T, CONV_BLOCK), lambda j: (0, cb0 + j)),
                  pl.BlockSpec((T, CONV_BLOCK), lambda j: (0, j)),
                  pl.BlockSpec((CONV_K, CONV_BLOCK), lambda j: (0, j)),
                  pl.BlockSpec((1, CONV_BLOCK), lambda j: (0, j)), pl.BlockSpec(memory_space=pl.ANY)],
        out_specs=[pl.BlockSpec((T, CONV_BLOCK), lambda j: (0, cb0 + j)),
                   pl.BlockSpec((CONV_K, CONV_BLOCK), lambda j: (0, j)),
                   pl.BlockSpec((1, CONV_BLOCK), lambda j: (0, j))],
        out_shape=[jax.ShapeDtypeStruct(dproj.shape, BF16), jax.ShapeDtypeStruct((CONV_K, CONV_DIM), F32),
                   jax.ShapeDtypeStruct((1, CONV_DIM), F32)],
        input_output_aliases={4: 0}, name=name, compiler_params=_cparams(("parallel",)),
    )(proj, dact, conv_w, conv_b, dproj)


GROUP_W = D_INNER // N_GROUPS
HEADS_PER_GROUP = N_HEADS // N_GROUPS


def _expand_mat():
    h = lax.broadcasted_iota(jnp.int32, (N_HEADS, D_INNER), 0)
    j = lax.broadcasted_iota(jnp.int32, (N_HEADS, D_INNER), 1)
    return (j // HEAD_DIM == h).astype(F32)


def _reduce_mat(g):
    j = lax.broadcasted_iota(jnp.int32, (GROUP_W, N_HEADS), 0)
    h = lax.broadcasted_iota(jnp.int32, (GROUP_W, N_HEADS), 1)
    return (g * HEADS_PER_GROUP + j // HEAD_DIM == h).astype(F32)


def _col16(v, h):
    lane = lax.broadcasted_iota(jnp.int32, v.shape, 1)
    return jnp.sum(jnp.where(lane == h, v, 0.0), axis=1, keepdims=True)


def _ssd_pre(dt_raw, dtT_raw, dtb, dtbT, alog, alogT):
    Q = CHUNK
    xdt = dt_raw + dtb
    dt = _softplus(xdt)
    dtT = _softplus(dtT_raw + dtbT)
    A = -jnp.exp(alog)
    AT = -jnp.exp(alogT)
    row = lax.broadcasted_iota(jnp.int32, (Q, Q), 0)
    col = lax.broadcasted_iota(jnp.int32, (Q, Q), 1)
    tril = (row >= col).astype(F32)
    triu = (row <= col).astype(F32)
    cs = _hdot(tril, dt * A, "b")
    csT = _hdot(dtT * AT, triu, "a")
    return xdt, dt, A, cs, csT, row >= col, triu


def _decay_matrix(cs, csT, h, causal):
    seg = _col16(cs, h) - csT[h:h + 1, :]
    return jnp.where(causal, jnp.exp(jnp.minimum(seg, 0.0)), 0.0)


def _ssd_in_specs(nc, rev):
    def cidx(c):
        return (nc - 1 - c) if rev else c

    return [
        pl.BlockSpec((CHUNK, D_INNER), lambda c: (cidx(c), 0)),
        pl.BlockSpec((CHUNK, 512), lambda c: (cidx(c), 2)),
        pl.BlockSpec((CHUNK, 512), lambda c: (cidx(c), 3)),
        pl.BlockSpec((CHUNK, D_INNER), lambda c: (cidx(c), 0)),
        pl.BlockSpec((CHUNK, 128), lambda c: (cidx(c), OFF_DT // 128)),
        pl.BlockSpec((N_HEADS, CHUNK), lambda c: (0, cidx(c))),
        pl.BlockSpec((1, N_HEADS), lambda c: (0, 0)),
        pl.BlockSpec((N_HEADS, 1), lambda c: (0, 0)),
        pl.BlockSpec((1, N_HEADS), lambda c: (0, 0)),
        pl.BlockSpec((N_HEADS, 1), lambda c: (0, 0)),
        pl.BlockSpec((1, D_INNER), lambda c: (0, 0)),
        pl.BlockSpec((1, D_INNER), lambda c: (0, 0)),
    ]


def _ssd_fwd(xbc, proj, dtT, dtb, dtbT, alog, alogT, dfull, ng, name):
    T = xbc.shape[0]
    nc = T // CHUNK
    Q = CHUNK

    def body(xs_ref, B_ref, C_ref, z_ref, dt_ref, dtT_ref, dtb_ref, dtbT_ref, al_ref, alT_ref, df_ref, ng_ref,
             y_ref, ypre_ref, hs_ref, h_scr):
        c = pl.program_id(0)

        @pl.when(c == 0)
        def _():
            h_scr[...] = jnp.zeros_like(h_scr)

        _, dt, _, cs, csT, causal, _ = _ssd_pre(dt_ref[:, :N_HEADS], dtT_ref[...], dtb_ref[...], dtbT_ref[...],
                                                al_ref[...], alT_ref[...])
        ex = _expand_mat()
        dt_full = _hdot(dt, ex, "a")
        cs_full = _hdot(cs, ex, "a")
        cs_last = cs_full[Q - 1:Q, :]
        xs = xs_ref[...]
        xd = xs * dt_full
        e_full = jnp.exp(cs_full)
        dec_full = jnp.exp(cs_last - cs_full)
        cd_full = jnp.exp(cs_last)
        lane_head = lax.broadcasted_iota(jnp.int32, (1, GROUP_W), 1) // HEAD_DIM
        for g in range(N_GROUPS):
            sl = slice(g * GROUP_W, (g + 1) * GROUP_W)
            Bg = B_ref[:, g * D_STATE:(g + 1) * D_STATE].astype(BF16)
            Cg = C_ref[:, g * D_STATE:(g + 1) * D_STATE].astype(BF16)
            CB = _dot_nt(Cg, Bg)
            hg = h_scr[g]
            yoff = _dot_nn(Cg, hg.astype(BF16)) * e_full[:, sl]
            xd_g = xd[:, sl]
            S = _dot_tn(Bg, (xd_g * dec_full[:, sl]).astype(BF16))
            xd_b = xd_g.astype(BF16)
            ydiag = jnp.zeros((Q, GROUP_W), F32)
            for r in range(HEADS_PER_GROUP):
                Lm = _decay_matrix(cs, csT, g * HEADS_PER_GROUP + r, causal)
                Gm = (CB * Lm).astype(BF16)
                ydiag = ydiag + _dot_nn(Gm, jnp.where(lane_head == r, xd_b, jnp.zeros_like(xd_b)))
            hs_ref[0, g] = hg
            h_scr[g] = hg * cd_full[:, sl] + S
            ypre = ydiag + yoff + xs[:, sl] * df_ref[:, sl]
            ypre_ref[:, sl] = ypre
            zg = z_ref[:, sl]
            yz = ypre * zg * _sigmoid(zg)
            rn = lax.rsqrt(jnp.mean(yz * yz, axis=-1, keepdims=True) + EPS)
            y_ref[:, sl] = (yz * rn * ng_ref[:, sl]).astype(BF16)

    return pl.pallas_call(
        body, grid=(nc,), in_specs=_ssd_in_specs(nc, False),
        out_specs=[pl.BlockSpec((CHUNK, D_INNER), lambda c: (c, 0)),
                   pl.BlockSpec((CHUNK, D_INNER), lambda c: (c, 0)),
                   pl.BlockSpec((1, N_GROUPS, D_STATE, GROUP_W), lambda c: (c, 0, 0, 0))],
        out_shape=[jax.ShapeDtypeStruct((T, D_INNER + ATTN_W), BF16), jax.ShapeDtypeStruct((T, D_INNER), F32),
                   jax.ShapeDtypeStruct((nc, N_GROUPS, D_STATE, GROUP_W), F32)],
        scratch_shapes=[pltpu.VMEM((N_GROUPS, D_STATE, GROUP_W), F32)],
        name=name, compiler_params=_cparams(("arbitrary",)),
    )(xbc, xbc, xbc, proj, proj, dtT, dtb, dtbT, alog, alogT, dfull, ng)


def _ssd_bwd(xbc, proj, dtT, dtb, dtbT, alog, alogT, dfull, ng, ypre, hs, dy, name):
    T = xbc.shape[0]
    nc = T // CHUNK
    Q = CHUNK

    def body(xs_ref, B_ref, C_ref, z_ref, dt_ref, dtT_ref, dtb_ref, dtbT_ref, al_ref, alT_ref, df_ref, ng_ref,
             ypre_ref, hs_ref, dy_ref,
             dz_ref, dxbc_ref, ddtb_ref, dal_ref, dD_ref, dng_ref, dh_scr):
        step = pl.program_id(0)

        @pl.when(step == 0)
        def _():
            dh_scr[...] = jnp.zeros_like(dh_scr)
            ddtb_ref[...] = jnp.zeros_like(ddtb_ref)
            dal_ref[...] = jnp.zeros_like(dal_ref)
            dD_ref[...] = jnp.zeros_like(dD_ref)
            dng_ref[...] = jnp.zeros_like(dng_ref)

        xdt, dt, A, cs, csT, causal, triu = _ssd_pre(dt_ref[:, :N_HEADS], dtT_ref[...], dtb_ref[...],
                                                    dtbT_ref[...], al_ref[...], alT_ref[...])
        ex = _expand_mat()
        dt_full = _hdot(dt, ex, "a")
        cs_full = _hdot(cs, ex, "a")
        cs_last = cs_full[Q - 1:Q, :]
        xs = xs_ref[...]
        xd = xs * dt_full
        e_full = jnp.exp(cs_full)
        dec_full = jnp.exp(cs_last - cs_full)
        cd_full = jnp.exp(cs_last)
        lane_head = lax.broadcasted_iota(jnp.int32, (1, GROUP_W), 1) // HEAD_DIM
        is_last = lax.broadcasted_iota(jnp.int32, (Q, 1), 0) == Q - 1
        dcs16 = jnp.zeros((Q, N_HEADS), F32)
        ddtx16 = jnp.zeros((Q, N_HEADS), F32)
        dD16 = jnp.zeros((8, N_HEADS), F32)
        lane16 = lax.broadcasted_iota(jnp.int32, (1, N_HEADS), 1)
        sub16 = lax.broadcasted_iota(jnp.int32, (N_HEADS, 1), 0)
        col_sums = jnp.zeros((N_HEADS, Q), F32)
        for g in range(N_GROUPS):
            sl = slice(g * GROUP_W, (g + 1) * GROUP_W)
            red = _reduce_mat(g)
            ypre_g = ypre_ref[:, sl]
            zg = z_ref[:, sl]
            sg = _sigmoid(zg)
            silu = zg * sg
            yz = ypre_g * silu
            rn = lax.rsqrt(jnp.mean(yz * yz, axis=-1, keepdims=True) + EPS)
            yh = yz * rn
            dy_g = dy_ref[:, sl]
            dng_ref[:, sl] += jnp.sum(dy_g * yh, axis=0, keepdims=True)
            dyh = dy_g * ng_ref[:, sl]
            dyz = rn * (dyh - yh * jnp.mean(dyh * yh, axis=-1, keepdims=True))
            dY = dyz * silu
            dz_ref[:, sl] = (dyz * ypre_g * sg * (1.0 + zg * (1.0 - sg))).astype(BF16)
            xs_g = xs[:, sl]
            xd_g = xd[:, sl]
            dec_g = dec_full[:, sl]
            cd_g = cd_full[:, sl]
            d_g = df_ref[:, sl]
            Bg = B_ref[:, g * D_STATE:(g + 1) * D_STATE].astype(BF16)
            Cg = C_ref[:, g * D_STATE:(g + 1) * D_STATE].astype(BF16)
            CB = _dot_nt(Cg, Bg)
            hg = hs_ref[0, g]
            hgb = hg.astype(BF16)
            yoff = _dot_nn(Cg, hgb) * e_full[:, sl]
            dhn = dh_scr[g]
            dhnb = dhn.astype(BF16)
            dYE = (dY * e_full[:, sl]).astype(BF16)
            dC = _dot_nt(dYE, hgb)
            dh_direct = _dot_tn(Cg, dYE)
            dXdd = _dot_nn(Bg, dhnb)
            dB = _dot_nt((xd_g * dec_g).astype(BF16), dhnb)
            dcd = jnp.sum(dhn * hg, axis=0, keepdims=True)
            dh_scr[g] = dh_direct + cd_g * dhn
            dYb = dY.astype(BF16)
            xd_b = xd_g.astype(BF16)
            dCB = jnp.zeros((Q, Q), F32)
            dXd = dXdd * dec_g
            for r in range(HEADS_PER_GROUP):
                h = g * HEADS_PER_GROUP + r
                Lm = _decay_matrix(cs, csT, h, causal)
                Gf = CB * Lm
                dYr = jnp.where(lane_head == r, dYb, jnp.zeros_like(dYb))
                dG = _dot_nt(dYr, xd_b)
                dCB = dCB + dG * Lm
                dXd = dXd + _dot_tn(Gf.astype(BF16), dYr)
                Mm = dG * Gf
                dcs16 = dcs16 + jnp.where(lane16 == h, jnp.sum(Mm, axis=1, keepdims=True), 0.0)
                col_sums = col_sums + jnp.where(sub16 == h, jnp.sum(Mm, axis=0, keepdims=True), 0.0)
            dCBb = dCB.astype(BF16)
            dC = dC + _dot_nn(dCBb, Bg)
            dB = dB + _dot_tn(dCBb, Cg)
            w_state = dXdd * dec_g * xd_g
            t_last = jnp.sum(w_state, axis=0, keepdims=True) + dcd * cd_g
            dcs_g = dY * yoff - w_state + jnp.where(is_last, t_last, 0.0)
            dcs16 = dcs16 + _hdot(dcs_g, red, "a")
            ddtx16 = ddtx16 + _hdot(dXd * xs_g, red, "a")
            dD16 = dD16 + _hdot(jnp.broadcast_to(jnp.sum(dY * xs_g, axis=0, keepdims=True), (8, GROUP_W)), red, "a")
            dxbc_ref[:, sl] = dXd * dt_full[:, sl] + dY * d_g
            dxbc_ref[:, D_INNER + g * D_STATE:D_INNER + (g + 1) * D_STATE] = dB
            dxbc_ref[:, D_INNER + 512 + g * D_STATE:D_INNER + 512 + (g + 1) * D_STATE] = dC
        eye = (lax.broadcasted_iota(jnp.int32, (N_HEADS, N_HEADS), 0)
               == lax.broadcasted_iota(jnp.int32, (N_HEADS, N_HEADS), 1)).astype(BF16)
        dcs16 = dcs16 - sum(_dot_tn(part, eye) for part in _split3(col_sums))
        da = _hdot(triu, dcs16, "b")
        ddt = da * A + ddtx16
        ddt_raw = ddt * _sigmoid(xdt)
        pr = lax.broadcasted_iota(jnp.int32, (N_HEADS, 128), 0)
        pc = lax.broadcasted_iota(jnp.int32, (N_HEADS, 128), 1)
        dz_ref[:, D_INNER:OFF_DT] = jnp.zeros((Q, OFF_DT - D_INNER), BF16)
        dz_ref[:, OFF_DT:OFF_DT + 128] = _hdot(ddt_raw, (pr == pc).astype(F32), "a").astype(BF16)
        dz_ref[:, OFF_DT + 128:] = jnp.zeros((Q, NP - OFF_DT - 128), BF16)
        ddtb_ref[...] += jnp.sum(ddt_raw, axis=0, keepdims=True)
        dal_ref[...] += jnp.sum(da * dt, axis=0, keepdims=True) * A
        dD_ref[...] += dD16[0:1, :]

    def rc(c):
        return nc - 1 - c

    in_specs = _ssd_in_specs(nc, True) + [
        pl.BlockSpec((CHUNK, D_INNER), lambda c: (rc(c), 0)),
        pl.BlockSpec((1, N_GROUPS, D_STATE, GROUP_W), lambda c: (rc(c), 0, 0, 0)),
        pl.BlockSpec((CHUNK, D_INNER), lambda c: (rc(c), 0)),
    ]
    small = pl.BlockSpec((1, N_HEADS), lambda c: (0, 0))
    return pl.pallas_call(
        body, grid=(nc,), in_specs=in_specs,
        out_specs=[pl.BlockSpec((CHUNK, NP), lambda c: (rc(c), 0)),
                   pl.BlockSpec((CHUNK, CONV_DIM), lambda c: (rc(c), 0)),
                   small, small, small,
                   pl.BlockSpec((1, D_INNER), lambda c: (0, 0))],
        out_shape=[jax.ShapeDtypeStruct((T, NP), BF16), jax.ShapeDtypeStruct((T, CONV_DIM), F32),
                   jax.ShapeDtypeStruct((1, N_HEADS), F32), jax.ShapeDtypeStruct((1, N_HEADS), F32),
                   jax.ShapeDtypeStruct((1, N_HEADS), F32), jax.ShapeDtypeStruct((1, D_INNER), F32)],
        scratch_shapes=[pltpu.VMEM((N_GROUPS, D_STATE, GROUP_W), F32)],
        name=name, compiler_params=_cparams(("arbitrary",)),
    )(xbc, xbc, xbc, proj, proj, dtT, dtb, dtbT, alog, alogT, dfull, ng, ypre, hs, dy)


N_PAIRS = ATTN_W // 128
PAIRS_PER_KV = N_PAIRS // 2
ATTN_SCALE = HEAD_DIM ** -0.5


def _kv_variants(kk):
    lo = lax.broadcasted_iota(jnp.int32, kk.shape, 1) < HEAD_DIM
    zero = jnp.zeros_like(kk)
    k00 = jnp.where(lo, kk, zero)
    k11 = jnp.where(lo, zero, kk)
    k01 = pltpu.roll(k00, HEAD_DIM, axis=1)
    k10 = pltpu.roll(k11, HEAD_DIM, axis=1)
    return [[k00.astype(BF16), k01.astype(BF16)], [k10.astype(BF16), k11.astype(BF16)]]


def _attn_valid(n):
    i = lax.broadcasted_iota(jnp.int32, (WINDOW, 2 * WINDOW), 0)
    j = lax.broadcasted_iota(jnp.int32, (WINDOW, 2 * WINDOW), 1)
    return (j > i) & (j <= i + WINDOW) & (n * WINDOW + j >= WINDOW)


def _attn_probs(qp, kvar, valid, sk):
    s = _dot_nt(qp, kvar) * ATTN_SCALE
    s = jnp.where(valid, s, NEG)
    m = jnp.maximum(jnp.max(s, axis=1, keepdims=True), sk)
    pe = jnp.exp(s - m)
    es = jnp.exp(sk - m)
    den = jnp.sum(pe, axis=1, keepdims=True) + es
    inv = 1.0 / den
    return pe * inv, es * inv


def _sink(sinks, r):
    lane = lax.broadcasted_iota(jnp.int32, sinks.shape, 1)
    return jnp.sum(jnp.where(lane == r, sinks, 0.0), axis=1, keepdims=True)


def _kv_specs():
    return [pl.BlockSpec((WINDOW, KV_W), lambda n: (jnp.maximum(n - 1, 0), OFF_K // KV_W)),
            pl.BlockSpec((WINDOW, KV_W), lambda n: (n, OFF_K // KV_W)),
            pl.BlockSpec((WINDOW, KV_W), lambda n: (jnp.maximum(n - 1, 0), OFF_V // KV_W)),
            pl.BlockSpec((WINDOW, KV_W), lambda n: (n, OFF_V // KV_W))]


def _attn_fwd(proj, sinks, og, ycat, name):
    T = proj.shape[0]
    nb = T // WINDOW

    def body(q_ref, kp_ref, kc_ref, vp_ref, vc_ref, s_ref, og_ref, _, y_ref, o_ref):
        n = pl.program_id(0)
        kv = _kv_variants(jnp.concatenate([kp_ref[...], kc_ref[...]], axis=0))
        vv = _kv_variants(jnp.concatenate([vp_ref[...], vc_ref[...]], axis=0))
        valid = _attn_valid(n)
        sinks_v = s_ref[...]
        ssq = jnp.zeros((WINDOW, 1), F32)
        for p in range(N_PAIRS):
            j = p // PAIRS_PER_KV
            qp = q_ref[:, p * 128:(p + 1) * 128].astype(BF16)
            o_pair = jnp.zeros((WINDOW, 128), F32)
            for par in range(2):
                pn, _ = _attn_probs(qp, kv[j][par], valid, _sink(sinks_v, 2 * p + par))
                o_pair = o_pair + _dot_nn(pn.astype(BF16), vv[j][par])
            o_ref[:, p * 128:(p + 1) * 128] = o_pair
            ssq = ssq + jnp.sum(o_pair * o_pair, axis=1, keepdims=True)
        rn = lax.rsqrt(ssq * (1.0 / ATTN_W) + EPS)
        y_ref[...] = (o_ref[...] * rn * og_ref[...]).astype(BF16)

    return pl.pallas_call(
        body, grid=(nb,),
        in_specs=[pl.BlockSpec((WINDOW, ATTN_W), lambda n: (n, OFF_Q // ATTN_W)), *_kv_specs(),
                  pl.BlockSpec((1, N_HEADS), lambda n: (0, 0)), pl.BlockSpec((1, ATTN_W), lambda n: (0, 0)), ANY],
        out_specs=[pl.BlockSpec((WINDOW, ATTN_W), lambda n: (n, 1)), pl.BlockSpec((WINDOW, ATTN_W), lambda n: (n, 0))],
        out_shape=[jax.ShapeDtypeStruct(ycat.shape, BF16), jax.ShapeDtypeStruct((T, ATTN_W), F32)],
        input_output_aliases={7: 0}, name=name, compiler_params=_cparams(("parallel",)),
    )(proj, proj, proj, proj, proj, sinks, og, ycat)


def _attn_bwd(proj, sinks, og, o, dy, dproj, name):
    T = proj.shape[0]
    nb = T // WINDOW

    def body(q_ref, kp_ref, kc_ref, vp_ref, vc_ref, s_ref, og_ref, o_ref, dy_ref, _,
             dq_ref, dk_ref, dv_ref, ds_ref, dog_ref):
        n = pl.program_id(0)

        @pl.when(n == 0)
        def _():
            dk_ref[...] = jnp.zeros_like(dk_ref)
            dv_ref[...] = jnp.zeros_like(dv_ref)
            ds_ref[...] = jnp.zeros_like(ds_ref)
            dog_ref[...] = jnp.zeros_like(dog_ref)

        kv = _kv_variants(jnp.concatenate([kp_ref[...], kc_ref[...]], axis=0))
        vv = _kv_variants(jnp.concatenate([vp_ref[...], vc_ref[...]], axis=0))
        valid = _attn_valid(n)
        sinks_v = s_ref[...]
        of = o_ref[...]
        rn = lax.rsqrt(jnp.mean(of * of, axis=-1, keepdims=True) + EPS)
        oh = of * rn
        dyf = dy_ref[...]
        dog_ref[...] += jnp.sum(dyf * oh, axis=0, keepdims=True)
        doh = dyf * og_ref[...]
        do = rn * (doh - oh * jnp.mean(doh * oh, axis=-1, keepdims=True))
        lane = lax.broadcasted_iota(jnp.int32, (1, 128), 1)
        lane16 = lax.broadcasted_iota(jnp.int32, (1, N_HEADS), 1)
        dk_acc = [[jnp.zeros((2 * WINDOW, 128), F32) for _ in range(2)] for _ in range(2)]
        dv_acc = [[jnp.zeros((2 * WINDOW, 128), F32) for _ in range(2)] for _ in range(2)]
        dsink = jnp.zeros((1, N_HEADS), F32)
        for p in range(N_PAIRS):
            j = p // PAIRS_PER_KV
            qp = q_ref[:, p * 128:(p + 1) * 128].astype(BF16)
            do_p = do[:, p * 128:(p + 1) * 128]
            o_p = of[:, p * 128:(p + 1) * 128]
            do_b = do_p.astype(BF16)
            prod = do_p * o_p
            dq_pair = jnp.zeros((WINDOW, 128), F32)
            for par in range(2):
                r = 2 * p + par
                half = (lane < HEAD_DIM) if par == 0 else (lane >= HEAD_DIM)
                pn, ps = _attn_probs(qp, kv[j][par], valid, _sink(sinks_v, r))
                delta = jnp.sum(jnp.where(half, prod, 0.0), axis=1, keepdims=True)
                dP = _dot_nt(do_b, vv[j][par])
                dS = pn * (dP - delta)
                dsink = dsink + jnp.where(lane16 == r, -jnp.sum(ps * delta, axis=0, keepdims=True), 0.0)
                dSb = (dS * ATTN_SCALE).astype(BF16)
                dq_pair = dq_pair + _dot_nn(dSb, kv[j][par])
                dk_acc[j][par] = dk_acc[j][par] + _dot_tn(dSb, jnp.where(half, qp, jnp.zeros_like(qp)))
                dv_acc[j][par] = dv_acc[j][par] + _dot_tn(pn.astype(BF16), jnp.where(half, do_b, jnp.zeros_like(do_b)))
            dq_ref[:, p * 128:(p + 1) * 128] = dq_pair.astype(BF16)
        dkk = (dk_acc[0][0] + pltpu.roll(dk_acc[0][1], HEAD_DIM, axis=1)
               + dk_acc[1][1] + pltpu.roll(dk_acc[1][0], HEAD_DIM, axis=1))
        dvv = (dv_acc[0][0] + pltpu.roll(dv_acc[0][1], HEAD_DIM, axis=1)
               + dv_acc[1][1] + pltpu.roll(dv_acc[1][0], HEAD_DIM, axis=1))
        prev = pl.multiple_of(jnp.maximum(n - 1, 0) * WINDOW, WINDOW)
        own = pl.multiple_of(n * WINDOW, WINDOW)
        dk_ref[pl.ds(prev, WINDOW), :] += dkk[:WINDOW]
        dk_ref[pl.ds(own, WINDOW), :] += dkk[WINDOW:]
        dv_ref[pl.ds(prev, WINDOW), :] += dvv[:WINDOW]
        dv_ref[pl.ds(own, WINDOW), :] += dvv[WINDOW:]
        ds_ref[...] += dsink

    full_kv = pl.BlockSpec((T, KV_W), lambda n: (0, 0))
    blk = pl.BlockSpec((WINDOW, ATTN_W), lambda n: (n, 0))
    return pl.pallas_call(
        body, grid=(nb,),
        in_specs=[pl.BlockSpec((WINDOW, ATTN_W), lambda n: (n, OFF_Q // ATTN_W)), *_kv_specs(),
                  pl.BlockSpec((1, N_HEADS), lambda n: (0, 0)), pl.BlockSpec((1, ATTN_W), lambda n: (0, 0)),
                  blk, pl.BlockSpec((WINDOW, ATTN_W), lambda n: (n, 1)), ANY],
        out_specs=[pl.BlockSpec((WINDOW, ATTN_W), lambda n: (n, OFF_Q // ATTN_W)), full_kv, full_kv,
                   pl.BlockSpec((1, N_HEADS), lambda n: (0, 0)), pl.BlockSpec((1, ATTN_W), lambda n: (0, 0))],
        out_shape=[jax.ShapeDtypeStruct(dproj.shape, BF16), jax.ShapeDtypeStruct((T, KV_W), F32),
                   jax.ShapeDtypeStruct((T, KV_W), F32), jax.ShapeDtypeStruct((1, N_HEADS), F32),
                   jax.ShapeDtypeStruct((1, ATTN_W), F32)],
        input_output_aliases={9: 0}, name=name, compiler_params=_cparams(("arbitrary",)),
    )(proj, proj, proj, proj, proj, sinks, og, o, dy, dproj)


ANY = pl.BlockSpec(memory_space=pl.ANY)


def _coords():
    return lax.axis_index("x"), lax.axis_index("y"), lax.axis_index("c")


def _all_gather(arrs, name):
    n = len(arrs)

    def body(*refs):
        ins, outs = refs[:n], refs[n:2 * n]
        send_sems, recv_sems, local_sems = refs[2 * n:]
        x, y, c = _coords()
        me = 4 * x + 2 * y + c
        sibling = (x, y, 1 - c)
        chips = [(1 - x, y), (x, 1 - y), (1 - x, 1 - y)]

        def copy(a, k, block, to, src=None):
            dst = outs[a].at[block]
            return pltpu.make_async_remote_copy(
                src_ref=dst if src is None else src, dst_ref=dst, send_sem=send_sems.at[a, k],
                recv_sem=recv_sems.at[a, k], device_id=to, device_id_type=MESH)

        mine = [pltpu.make_async_copy(ins[a], outs[a].at[me], local_sems.at[a]) for a in range(n)]
        for cp in mine:
            cp.start()
        first = []
        for a in range(n):
            first.append(copy(a, 0, me, sibling, src=ins[a]))
            for j, chip in enumerate(chips):
                first.append(copy(a, 1 + j, me, (*chip, c), src=ins[a]))
        for cp in first:
            cp.start()
        passed = []
        for j, (px, py) in enumerate(chips):
            blk = 4 * px + 2 * py + c
            for a in range(n):
                copy(a, 1 + j, blk, sibling).wait_recv()
                fwd = copy(a, 4 + j, blk, sibling)
                fwd.start()
                passed.append(fwd)
        for a in range(n):
            copy(a, 0, 4 * x + 2 * y + (1 - c), sibling).wait_recv()
            for j, (px, py) in enumerate(chips):
                copy(a, 4 + j, 4 * px + 2 * py + (1 - c), sibling).wait_recv()
        for cp in first + passed:
            cp.wait_send()
        for cp in mine:
            cp.wait()

    return pl.pallas_call(
        body, in_specs=[ANY] * n, out_specs=[ANY] * n,
        out_shape=[jax.ShapeDtypeStruct((N_DEV,) + a.shape, a.dtype) for a in arrs],
        scratch_shapes=[pltpu.SemaphoreType.DMA((n, 7)), pltpu.SemaphoreType.DMA((n, 7)),
                        pltpu.SemaphoreType.DMA((n,))],
        name=name,
    )(*arrs)


def _exchange_pair(arrs, name):
    n = len(arrs)

    def body(*refs):
        ins, outs = refs[:n], refs[n:2 * n]
        send_sems, recv_sems = refs[2 * n:]
        x, y, c = _coords()
        cps = []
        for a in range(n):
            for q in range(4):
                cps.append(pltpu.make_async_remote_copy(
                    src_ref=ins[a].at[2 * q + (1 - c)], dst_ref=outs[a].at[q], send_sem=send_sems.at[a, q],
                    recv_sem=recv_sems.at[a, q], device_id=(x, y, 1 - c), device_id_type=MESH))
        for cp in cps:
            cp.start()
        for cp in cps:
            cp.wait()

    return pl.pallas_call(
        body, in_specs=[ANY] * n, out_specs=[ANY] * n,
        out_shape=[jax.ShapeDtypeStruct((4,) + a.shape[1:], a.dtype) for a in arrs],
        scratch_shapes=[pltpu.SemaphoreType.DMA((n, 4)), pltpu.SemaphoreType.DMA((n, 4))],
        name=name,
    )(*arrs)


def _exchange_chips(arrs, name):
    n = len(arrs)

    def body(*refs):
        ins, outs = refs[:n], refs[n:2 * n]
        send_sems, recv_sems = refs[2 * n:]
        x, y, c = _coords()
        chips = [(1 - x, y), (x, 1 - y), (1 - x, 1 - y)]
        cps = []
        for a in range(n):
            for k, (tx, ty) in enumerate(chips):
                cps.append(pltpu.make_async_remote_copy(
                    src_ref=ins[a].at[2 * tx + ty], dst_ref=outs[a].at[k], send_sem=send_sems.at[a, k],
                    recv_sem=recv_sems.at[a, k], device_id=(tx, ty, c), device_id_type=MESH))
        for cp in cps:
            cp.start()
        for cp in cps:
            cp.wait()

    return pl.pallas_call(
        body, in_specs=[ANY] * n, out_specs=[ANY] * n,
        out_shape=[jax.ShapeDtypeStruct((3,) + a.shape[1:], a.dtype) for a in arrs],
        scratch_shapes=[pltpu.SemaphoreType.DMA((n, 3)), pltpu.SemaphoreType.DMA((n, 3))],
        name=name,
    )(*arrs)


HBM = pl.BlockSpec(memory_space=pltpu.HBM)
SEM = pl.BlockSpec(memory_space=pltpu.SEMAPHORE)
EFFECT = pltpu.SideEffectType.DATAFLOW_SIDE_EFFECTING


def _in_hbm(a):
    return pltpu.with_memory_space_constraint(a, pltpu.HBM)


def _remote_start(srcs, lands, plan, n_copies, name, after=None):
    n = len(srcs)
    n_after = 0 if after is None else 1

    def body(*refs):
        src_refs, land_refs = refs[:n], refs[n:2 * n]
        send_sems, recv_sems = refs[2 * n + n_after], refs[2 * n + n_after + 1]
        token = refs[-1]
        x, y, c = _coords()
        for i, (sv, dv, dev) in enumerate(plan(src_refs, land_refs, x, y, c)):
            pltpu.make_async_remote_copy(src_ref=sv, dst_ref=dv, send_sem=send_sems.at[i], recv_sem=recv_sems.at[i],
                                         device_id=dev, device_id_type=MESH).start()
        token[...] = jnp.zeros_like(token)

    bufs = list(srcs) + list(lands)
    outs = pl.pallas_call(
        body, name=name,
        out_shape=(pltpu.SemaphoreType.DMA((n_copies,)), pltpu.SemaphoreType.DMA((n_copies,)),
                   *[pltpu.HBM(b.shape, b.dtype) for b in bufs], jax.ShapeDtypeStruct((8, 128), F32)),
        in_specs=[HBM] * (2 * n) + [ANY] * n_after,
        out_specs=(SEM, SEM, *[HBM] * (2 * n), pl.BlockSpec(memory_space=pltpu.VMEM)),
        input_output_aliases={i: 2 + i for i in range(2 * n)},
        compiler_params=pltpu.CompilerParams(has_side_effects=EFFECT),
    )(*[_in_hbm(b) for b in bufs], *([] if after is None else [after]))
    return outs[0], outs[1], list(outs[2:2 + n]), list(outs[2 + n:2 + 2 * n]), outs[-1]


def _remote_wait(started, after, plan, name):
    send_sems, recv_sems, srcs, lands, _ = started
    n = len(srcs)

    def body(*refs):
        src_refs, land_refs = refs[:n], refs[n:2 * n]
        send_sems, recv_sems = refs[2 * n], refs[2 * n + 1]
        x, y, c = _coords()
        for i, (sv, dv, dev) in enumerate(plan(src_refs, land_refs, x, y, c)):
            cp = pltpu.make_async_remote_copy(src_ref=sv, dst_ref=dv, send_sem=send_sems.at[i],
                                              recv_sem=recv_sems.at[i], device_id=dev, device_id_type=MESH)
            cp.wait_send()
            cp.wait_recv()

    bufs = list(srcs) + list(lands)
    outs = pl.pallas_call(
        body, name=name, out_shape=tuple(pltpu.HBM(b.shape, b.dtype) for b in bufs),
        in_specs=[HBM] * (2 * n) + [SEM, SEM, ANY], out_specs=tuple([HBM] * (2 * n)),
        input_output_aliases={i: i for i in range(2 * n)},
        compiler_params=pltpu.CompilerParams(has_side_effects=EFFECT),
    )(*bufs, send_sems, recv_sems, after)
    return list(outs[:n]), list(outs[n:])


def _gather_plan(src_refs, land_refs, x, y, c):
    me = 4 * x + 2 * y + c
    plan = []
    for s, l in zip(src_refs, land_refs):
        for dev in [(x, y, 1 - c), (1 - x, y, c), (x, 1 - y, c), (1 - x, 1 - y, c)]:
            plan.append((s, l.at[me], dev))
    return plan


def _pair_plan(src_refs, land_refs, x, y, c):
    plan = []
    for s, l in zip(src_refs, land_refs):
        for q in range(4):
            plan.append((s.at[2 * q + (1 - c)], l.at[q], (x, y, 1 - c)))
    return plan


def _pair4_plan(src_refs, land_refs, x, y, c):
    plan = []
    for s, l in zip(src_refs, land_refs):
        for q in range(4):
            plan.append((s.at[q], l.at[q], (x, y, 1 - c)))
    return plan


def _chips_plan(src_refs, land_refs, x, y, c):
    plan = []
    for s, l in zip(src_refs, land_refs):
        for k, (tx, ty) in enumerate([(1 - x, y), (x, 1 - y), (1 - x, 1 - y)]):
            plan.append((s.at[2 * tx + ty], l.at[k], (tx, ty, c)))
    return plan


def _everyone_plan(src_refs, land_refs, x, y, c):
    me = 4 * x + 2 * y + c
    plan = []
    for s, l in zip(src_refs, land_refs):
        for fx, fy, fc in [(0, 0, 1), (1, 0, 0), (1, 0, 1), (0, 1, 0), (0, 1, 1), (1, 1, 0), (1, 1, 1)]:
            dev = ((1 - x) if fx else x, (1 - y) if fy else y, (1 - c) if fc else c)
            plan.append((s, l.at[me], dev))
    return plan


def _gather_finish(gathered, name):
    n = len(gathered)

    def body(*refs):
        outs = refs[n:2 * n]
        send_sems, recv_sems = refs[2 * n:]
        x, y, c = _coords()
        cps = []
        for a in range(n):
            for j, (px, py) in enumerate([(1 - x, y), (x, 1 - y), (1 - x, 1 - y)]):
                blk = outs[a].at[4 * px + 2 * py + c]
                got = outs[a].at[4 * px + 2 * py + (1 - c)]
                cps.append((pltpu.make_async_remote_copy(
                    src_ref=blk, dst_ref=blk, send_sem=send_sems.at[a, j], recv_sem=recv_sems.at[a, j],
                    device_id=(x, y, 1 - c), device_id_type=MESH), pltpu.make_async_remote_copy(
                    src_ref=got, dst_ref=got, send_sem=send_sems.at[a, j], recv_sem=recv_sems.at[a, j],
                    device_id=(x, y, 1 - c), device_id_type=MESH)))
        for cp, _ in cps:
            cp.start()
        for cp, arrival in cps:
            cp.wait_send()
            arrival.wait_recv()

    return pl.pallas_call(
        body, in_specs=[ANY] * n, out_specs=[ANY] * n,
        out_shape=[jax.ShapeDtypeStruct(g.shape, g.dtype) for g in gathered],
        input_output_aliases={a: a for a in range(n)},
        scratch_shapes=[pltpu.SemaphoreType.DMA((n, 3)), pltpu.SemaphoreType.DMA((n, 3))],
        name=name,
    )(*gathered)


def _pair_add(g8, r1, csel, tr, name):
    _, R, C = r1.shape
    g4 = g8.reshape(4, 2, R, C)

    def body(c_ref, g_ref, r_ref, o_ref):
        o_ref[...] = (g_ref[...].astype(F32) + r_ref[...].astype(F32)).astype(BF16)

    return pl.pallas_call(
        body,
        grid_spec=pltpu.PrefetchScalarGridSpec(
            num_scalar_prefetch=1, grid=(4, R // tr),
            in_specs=[pl.BlockSpec((None, None, tr, C), lambda q, i, cs: (q, cs[0], i, 0)),
                      pl.BlockSpec((None, tr, C), lambda q, i, cs: (q, i, 0))],
            out_specs=pl.BlockSpec((None, tr, C), lambda q, i, cs: (q, i, 0))),
        out_shape=jax.ShapeDtypeStruct((4, R, C), BF16), name=name,
        compiler_params=_cparams(("parallel", "parallel")),
    )(csel, g4, r1)


def _adamw_math(w, g, m, v):
    m = ADAM_B1 * m + (1.0 - ADAM_B1) * g
    v = ADAM_B2 * v + (1.0 - ADAM_B2) * (g * g)
    m_hat = m / (1.0 - ADAM_B1 ** ADAM_STEP)
    v_hat = v / (1.0 - ADAM_B2 ** ADAM_STEP)
    delta = -ADAM_LR * (m_hat / (jnp.sqrt(v_hat) + ADAM_EPS) + ADAM_WD * w)
    return delta, m, v


def _adamw_big(w, m, v, p4, r3, qsel, tile, name):
    R, C = w.shape
    tr, tc = tile

    def body(q_ref, w_ref, m_ref, v_ref, p_ref, r_ref, g_out, d_out, m_out, v_out):
        g = p_ref[...].astype(F32) + r_ref[0].astype(F32) + r_ref[1].astype(F32) + r_ref[2].astype(F32)
        d, mn, vn = _adamw_math(w_ref[...], g, m_ref[...], v_ref[...])
        g_out[...] = g
        d_out[...] = d
        m_out[...] = mn
        v_out[...] = vn

    blk = pl.BlockSpec((tr, tc), lambda i, j, qs: (i, j))
    return pl.pallas_call(
        body,
        grid_spec=pltpu.PrefetchScalarGridSpec(
            num_scalar_prefetch=1, grid=(R // tr, C // tc),
            in_specs=[blk, blk, blk, pl.BlockSpec((None, tr, tc), lambda i, j, qs: (qs[0], i, j)),
                      pl.BlockSpec((3, tr, tc), lambda i, j, qs: (0, i, j))],
            out_specs=[blk, blk, blk, blk]),
        out_shape=[jax.ShapeDtypeStruct((R, C), F32)] * 4, name=name,
        compiler_params=_cparams(("parallel", "parallel")),
    )(qsel, w, m, v, p4, r3)


def _sum_partials(p4, r3, qsel, tc, name):
    _, R, C = p4.shape

    def body(q_ref, p_ref, r_ref, o_ref):
        o_ref[...] = p_ref[...].astype(F32) + r_ref[0].astype(F32) + r_ref[1].astype(F32) + r_ref[2].astype(F32)

    return pl.pallas_call(
        body,
        grid_spec=pltpu.PrefetchScalarGridSpec(
            num_scalar_prefetch=1, grid=(C // tc,),
            in_specs=[pl.BlockSpec((None, R, tc), lambda j, qs: (qs[0], 0, j)),
                      pl.BlockSpec((3, R, tc), lambda j, qs: (0, 0, j))],
            out_specs=pl.BlockSpec((R, tc), lambda j, qs: (0, j))),
        out_shape=jax.ShapeDtypeStruct((R, C), F32), name=name, compiler_params=_cparams(("parallel",)),
    )(qsel, p4, r3)


def _adamw_tiled(w, g, m, v, tc, name):
    R, C = w.shape

    def body(w_ref, g_ref, m_ref, v_ref, d_out, m_out, v_out):
        d, mn, vn = _adamw_math(w_ref[...], g_ref[...], m_ref[...], v_ref[...])
        d_out[...] = d
        m_out[...] = mn
        v_out[...] = vn

    blk = pl.BlockSpec((R, tc), lambda j: (0, j))
    return pl.pallas_call(
        body, grid=(C // tc,), in_specs=[blk] * 4, out_specs=[blk] * 3,
        out_shape=[jax.ShapeDtypeStruct((R, C), F32)] * 3, name=name, compiler_params=_cparams(("parallel",)),
    )(w, g, m, v)


def _small_sum(parts, name):
    def body(p_ref, o_ref):
        acc = p_ref[0]
        for d in range(1, N_DEV):
            acc = acc + p_ref[d]
        o_ref[...] = acc

    return pl.pallas_call(
        body, out_shape=jax.ShapeDtypeStruct(parts.shape[1:], F32), name=name,
        compiler_params=_cparams(),
    )(parts)


def _adamw_small(w, g, m, v, name):
    def body(w_ref, g_ref, m_ref, v_ref, d_out, m_out, v_out):
        d, mn, vn = _adamw_math(w_ref[...], g_ref[...], m_ref[...], v_ref[...])
        d_out[...] = d
        m_out[...] = mn
        v_out[...] = vn

    return pl.pallas_call(
        body, out_shape=[jax.ShapeDtypeStruct(w.shape, F32)] * 3, name=name, compiler_params=_cparams(),
    )(w, g, m, v)


def _row(*pieces):
    r = jnp.concatenate([p.reshape(1, -1) for p in pieces], axis=1)
    return jnp.pad(r, ((0, 0), (0, D_MODEL - r.shape[1])))


def _pack_small(mix, convb, ssmg, attng, mlpg, fing, convw, dtb, alog, dsk, sinks, extra=None):
    last = [dtb, alog, dsk, sinks] + ([extra] if extra is not None else [])
    rows = [_row(mix), _row(convb), _row(ssmg, attng), _row(mlpg), _row(fing),
            jnp.pad(convw, ((0, 0), (0, D_MODEL - convw.shape[1]))), _row(*last)]
    packed = jnp.concatenate(rows, axis=0)
    return jnp.pad(packed, ((0, SMALL_ROWS - packed.shape[0]), (0, 0)))


def _unpack_small(p, conv_n):
    return dict(
        mix_norm_g=p[0:1, :], conv_b=p[1:2, :], ssm_norm_g=p[2:3, :D_INNER], attn_out_norm_g=p[2:3, D_INNER:],
        mlp_norm_g=p[3:4, :], final_norm_g=p[4, :], conv_w=p[5:9, :conv_n][None],
        dt_bias=p[9:10, 0:16], A_log=p[9:10, 16:32], D_skip=p[9:10, 32:48], attn_sinks=p[9:10, 48:64])


SMALL_NAMES = ["mix_norm_g", "conv_w", "conv_b", "dt_bias", "A_log", "D_skip", "ssm_norm_g", "attn_sinks",
               "attn_out_norm_g", "mlp_norm_g", "final_norm_g"]
WEIGHT_ORDER = ["mix_norm_g", "w_in", "conv_w", "conv_b", "dt_bias", "A_log", "D_skip", "ssm_norm_g", "attn_sinks",
                "attn_out_norm_g", "w_out", "mlp_norm_g", "w_up", "w_down", "final_norm_g"]


def _to_my_columns(w_nat):
    pad = jnp.zeros((w_nat.shape[0], NP - IN_PROJ), w_nat.dtype)
    return jnp.concatenate([w_nat[:, :NAT_DT], w_nat[:, NAT_DT + N_HEADS:], w_nat[:, NAT_DT:NAT_DT + N_HEADS], pad],
                           axis=1)


PER = IN_PROJ // N_DEV
SUPER_STEP = 544
SUPER = 576


def _natural_rows(g, lo, hi):
    segments = [(0, NAT_DT, 0), (NAT_DT, NAT_DT + N_HEADS, OFF_DT - NAT_DT), (NAT_DT + N_HEADS, IN_PROJ, -N_HEADS),
                (IN_PROJ, NP, 0)]
    pieces = [g[max(lo, a) + shift:min(hi, b) + shift] for a, b, shift in segments if max(lo, a) < min(hi, b)]
    return pieces[0] if len(pieces) == 1 else jnp.concatenate(pieces, axis=0)


def _w_in_from_super_slabs(sup):
    seam = SUPER - SUPER_STEP
    units = []
    for i in range(N_DEV):
        base = SUPER_STEP * i
        units.append((base, base + seam, sup[i, :seam] if i == 0 else sup[i - 1, SUPER_STEP:] + sup[i, :seam]))
        units.append((base + seam, base + SUPER_STEP, sup[i, seam:SUPER_STEP]))
    units.append((SUPER_STEP * N_DEV, SUPER_STEP * N_DEV + seam, sup[N_DEV - 1, SUPER_STEP:]))

    def natural(lo, hi):
        return [rows[max(lo, a) - a:min(hi, b) - a] for a, b, rows in units if max(lo, a) < min(hi, b)]

    pieces = natural(0, NAT_DT) + natural(NAT_DT + N_HEADS, IN_PROJ) + natural(NAT_DT, NAT_DT + N_HEADS)
    return jnp.concatenate(pieces + [jnp.zeros((NP - IN_PROJ, D_MODEL), sup.dtype)], axis=0)


def _to_natural_columns(w_my):
    return jnp.concatenate([w_my[:, :NAT_DT], w_my[:, OFF_DT:OFF_DT + N_HEADS], w_my[:, NAT_DT:OFF_DT]], axis=1)


SLAB = 1024


def _grad_w_up(h2, du, name, sel=None, add=None, after=None):
    T, D = h2.shape
    if sel is None:
        pick, n_slab, pre = (lambda j, *cs: j), N_DEV, None
    else:
        pre, other = sel
        pick, n_slab = (lambda j, cs: 2 * j + ((1 - cs[0]) if other else cs[0])), 4
    o_spec = pl.BlockSpec((None, SLAB, SLAB), lambda i, j, k, *cs: (j, i, 0))
    return _matmul(
        h2, du, mode="tn", grid=(D // SLAB, n_slab, 1),
        a_spec=pl.BlockSpec((T, SLAB), lambda i, j, k, *cs: (0, i)),
        b_spec=pl.BlockSpec((T, SLAB), lambda i, j, k, *cs: (0, pick(j, *cs))),
        out_shapes=[jax.ShapeDtypeStruct((n_slab, D, SLAB), BF16)], out_specs=[o_spec], tile=(SLAB, SLAB), name=name,
        extras=() if add is None else (add,), extra_specs=() if add is None else (o_spec,),
        epilogue=None if add is None else (lambda acc, r: (acc + r.astype(F32),)), after=after, prefetch=pre)[0]


def _grad_w_down(act, dx3b, name, sel=None, add=None, after=None):
    T, D = dx3b.shape
    if sel is None:
        pick, n_slab, pre = (lambda i, *cs: i), N_DEV, None
    else:
        pre, other = sel
        pick, n_slab = (lambda i, cs: 2 * i + ((1 - cs[0]) if other else cs[0])), 4
    o_spec = pl.BlockSpec((None, SLAB, SLAB), lambda i, j, k, *cs: (i, 0, j))
    return _matmul(
        act, dx3b, mode="tn", grid=(n_slab, D // SLAB, 1),
        a_spec=pl.BlockSpec((T, SLAB), lambda i, j, k, *cs: (0, pick(i, *cs))),
        b_spec=pl.BlockSpec((T, SLAB), lambda i, j, k, *cs: (0, j)),
        out_shapes=[jax.ShapeDtypeStruct((n_slab, SLAB, D), BF16)], out_specs=[o_spec], tile=(SLAB, SLAB), name=name,
        extras=() if add is None else (add,), extra_specs=() if add is None else (o_spec,),
        epilogue=None if add is None else (lambda acc, r: (acc + r.astype(F32),)), after=after, prefetch=pre)[0]


class _FixedWeights:
    def __init__(self, w_in_p, w_out_f, w_up_s, w_down_f, conv_w_f):
        self.w = (w_in_p, w_out_f, w_up_s, w_down_f, conv_w_f)
        self.grads = {}

    def mixer_weights(self, after):
        return self.w[0], self.w[4], None

    def out_weight(self, after):
        return self.w[1]

    def up_weight(self, after):
        return self.w[2]

    def down_weight(self, after):
        return self.w[3]

    def mlp_grads(self, h2, du, act, dx3b):
        self.grads.update(w_up=_grad_w_up(h2, du, "grad_w_up"),
                          w_down=_grad_w_down(act, dx3b, "grad_w_down").reshape(D_FF, D_MODEL))
        return None

    def out_grad(self, g_out):
        self.grads.update(w_out=g_out)
        return None

    def in_grad(self, g_in):
        self.grads.update(w_in=g_in)
        return None


def _local_step(x, tgt, p, hooks):
    T = x.shape[0]
    D = D_MODEL
    h1 = _rmsnorm_fwd(x, p["mix_norm_g"], "norm_mix")
    w_in_t, conv_w_f, token = hooks.mixer_weights(h1)
    (proj,) = _mm_simple(h1, w_in_t, mode="nt", M=T, N=NP, K=D, tm=min(T, 1024), tn=1536, tk=D, out_dtype=F32,
                         name="in_proj", after=token)
    xbc = _conv_fwd(proj, conv_w_f, p["conv_b"], "conv_fwd")
    dtT = proj[:, OFF_DT:OFF_DT + N_HEADS].T
    dtbT = p["dt_bias"].T
    alogT = p["A_log"].T
    dfull = jnp.repeat(p["D_skip"], HEAD_DIM, axis=1)
    ycat, ypre, hs = _ssd_fwd(xbc, proj, dtT, p["dt_bias"], dtbT, p["A_log"], alogT, dfull, p["ssm_norm_g"],
                              "ssd_fwd")
    ycat, o_att = _attn_fwd(proj, p["attn_sinks"], p["attn_out_norm_g"], ycat, "attn_fwd")
    w_out_f = hooks.out_weight(ycat)
    tm = min(T, 1024)
    (x2,) = _mm_simple(ycat, w_out_f, mode="nn", M=T, N=D, K=D, tm=tm, tn=1024, tk=D, out_dtype=F32, name="out_proj",
                       extras=(x,), epilogue=lambda acc, res: (acc + res,))
    h2 = _rmsnorm_fwd(x2, p["mlp_norm_g"], "norm_mlp")
    w_up_s = hooks.up_weight(h2)
    grid = (T // tm, N_DEV, 1)
    u, act = _matmul(
        h2, w_up_s, mode="nn", grid=grid,
        a_spec=pl.BlockSpec((tm, D), lambda i, j, k: (i, 0)),
        b_spec=pl.BlockSpec((None, D, 1024), lambda i, j, k: (j, 0, 0)),
        out_shapes=[jax.ShapeDtypeStruct((T, D_FF), F32), jax.ShapeDtypeStruct((T, D_FF), BF16)],
        out_specs=[pl.BlockSpec((tm, 1024), lambda i, j, k: (i, j))] * 2, tile=(tm, 1024), name="mlp_up",
        epilogue=lambda acc: (acc, jnp.square(jnp.maximum(acc, 0.0))))
    w_down_f = hooks.down_weight(act)
    (x3,) = _mm_simple(act, w_down_f, mode="nn", M=T, N=D, K=D_FF, tm=tm, tn=1024, tk=2048, out_dtype=F32,
                       name="mlp_down", extras=(x2,), epilogue=lambda acc, res: (acc + res,))
    loss_part, d_fin, dx3, dx3b = _final_loss(x3, tgt, p["final_norm_g"].reshape(1, D), "loss_head")
    (du,) = _mm_simple(dx3b, w_down_f, mode="nt", M=T, N=D_FF, K=D, tm=tm, tn=1024, tk=D, out_dtype=BF16,
                       name="mlp_down_bwd", extras=(u,),
                       epilogue=lambda acc, uu: (acc * (2.0 * jnp.maximum(uu, 0.0)),))
    token = hooks.mlp_grads(h2, du, act, dx3b)
    (dh2,) = _matmul(
        du, w_up_s, mode="nt", grid=(T // tm, D // 1024, N_DEV // 2),
        a_spec=pl.BlockSpec((tm, 2048), lambda i, j, k: (i, k)),
        b_spec=pl.BlockSpec((2, 1024, 1024), lambda i, j, k: (k, j, 0)),
        out_shapes=[jax.ShapeDtypeStruct((T, D), F32)],
        out_specs=[pl.BlockSpec((tm, 1024), lambda i, j, k: (i, j))], tile=(tm, 1024), name="mlp_up_bwd",
        after=token, dot_fn=lambda a, b: _dot_nt(a[:, :1024], b[0]) + _dot_nt(a[:, 1024:], b[1]))
    dx2, dx2b, d_mlp = _rmsnorm_bwd(dh2, x2, p["mlp_norm_g"], dx3, "norm_mlp_bwd")
    (g_out,) = _mm_simple(ycat, dx2b, mode="tn", M=D, N=D, K=T, tm=1024, tn=1024, tk=T, out_dtype=BF16,
                          name="grad_w_out")
    token = hooks.out_grad(g_out)
    (dy,) = _mm_simple(dx2b, w_out_f, mode="nt", M=T, N=D, K=D, tm=tm, tn=1024, tk=D, out_dtype=F32,
                       name="out_proj_bwd", after=token)
    dproj, dxbc_act, d_dtb, d_alog, d_dskip, d_ssmg = _ssd_bwd(
        xbc, proj, dtT, p["dt_bias"], dtbT, p["A_log"], alogT, dfull, p["ssm_norm_g"], ypre, hs, dy, "ssd_bwd")
    dproj, d_convw, d_convb = _conv_bwd(proj, dxbc_act, conv_w_f, p["conv_b"], dproj, "conv_bwd")
    dproj, dk, dv, d_sinks, d_attng = _attn_bwd(proj, p["attn_sinks"], p["attn_out_norm_g"], o_att, dy, dproj,
                                                "attn_bwd")
    dproj = lax.dynamic_update_slice(dproj, jnp.concatenate([dk, dv], axis=1).astype(BF16), (0, OFF_K))
    (g_in,) = _mm_simple(dproj, h1, mode="tn", M=NP, N=D, K=T, tm=1536, tn=1024, tk=T, out_dtype=BF16,
                         name="grad_w_in")
    token = hooks.in_grad(g_in)
    (dh1,) = _mm_simple(dproj, w_in_t, mode="nn", M=T, N=D, K=NP, tm=tm, tn=1024, tk=2304, out_dtype=F32,
                        name="in_proj_bwd", after=token)
    dx, _, d_mix = _rmsnorm_bwd(dh1, x, p["mix_norm_g"], dx2, "norm_mix_bwd")
    small = _pack_small(d_mix, d_convb, d_ssmg, d_attng, d_mlp, d_fin, d_convw, d_dtb, d_alog, d_dskip, d_sinks,
                        extra=loss_part[:, 0:1])
    return dx, small


def _landing(own, me):
    zone = lax.empty((N_DEV,) + own.shape, own.dtype)
    return lax.dynamic_update_slice(zone, own[None], (me,) + (0,) * own.ndim)


def _gather_end(started, after, plan, name):
    _, lands = _remote_wait(started, after, plan, name + "_wait")
    return _gather_finish(lands, name + "_finish")


class _ShardedWeights:
    def __init__(self, w_in, w_out, conv_w, w_up, w_down, me, csel):
        self.me, self.csel = me, csel
        own_rows = jnp.transpose(w_in).astype(BF16)
        shards = [lax.dynamic_update_slice(jnp.zeros((SUPER, D_MODEL), BF16), own_rows, (2 * me, 0)), conv_w]
        self.st_mixer = _remote_start(shards, [_landing(s, me) for s in shards], _gather_plan, 4 * len(shards),
                                      "gather_start_mixer")
        order = self.st_mixer[4]
        self.st_later = {}
        for k, wt in [("out", w_out), ("up", w_up), ("down", w_down)]:
            shard = (wt + order[0, 0]).astype(BF16)
            self.st_later[k] = _remote_start([shard], [_landing(shard, me)], _gather_plan, 4, f"gather_start_{k}",
                                             after=order)
            order = self.st_later[k][4]
        self.start_token = order
        self.reduces = {}

    def mixer_weights(self, after):
        g_in, g_conv = _gather_end(self.st_mixer, after, _gather_plan, "gather_mixer")
        conv_w_f = jnp.concatenate([g_conv[i] for i in range(N_DEV)], axis=1)
        return _w_in_from_super_slabs(g_in), conv_w_f, None

    def out_weight(self, after):
        return _gather_end(self.st_later["out"], after, _gather_plan, "gather_out")[0].reshape(D_MODEL, D_MODEL)

    def up_weight(self, after):
        return _gather_end(self.st_later["up"], after, _gather_plan, "gather_up")[0]

    def down_weight(self, after):
        return _gather_end(self.st_later["down"], after, _gather_plan, "gather_down")[0].reshape(D_FF, D_MODEL)

    def _chips_start(self, slabs, from_sibling, rows, tag):
        sums = [_pair_add(s, r, self.csel, tr, f"pair_add_{tag}_{i}")
                for i, (s, r, tr) in enumerate(zip(slabs, from_sibling, rows))]
        lands = [lax.empty((3,) + s.shape[1:], s.dtype) for s in sums]
        self.reduces[tag] = _remote_start(sums, lands, _chips_plan, 3 * len(sums), f"reduce_start_{tag}")
        return self.reduces[tag][4]

    def mlp_grads(self, h2, du, act, dx3b):
        def send(part, tag, after):
            st = _remote_start([part], [lax.empty(part.shape, part.dtype)], _pair4_plan, 4,
                               f"reduce_pair_start_{tag}", after=after)
            return st

        def received(st, after, tag):
            return _remote_wait(st, after, _pair4_plan, f"reduce_pair_wait_{tag}")[1][0]

        def to_chips(sums, tag):
            self.reduces[tag] = _remote_start([sums], [lax.empty((3,) + sums.shape[1:], sums.dtype)], _chips_plan, 3,
                                              f"reduce_start_{tag}")
            return self.reduces[tag][4]

        up_send = _grad_w_up(h2, du, "grad_w_up_send", sel=(self.csel, True))
        st_up = send(up_send, "up", None)
        down_send = _grad_w_down(act, dx3b, "grad_w_down_send", sel=(self.csel, True), after=st_up[4])
        st_down = send(down_send, "down", None)
        up_sum = _grad_w_up(h2, du, "grad_w_up_keep", sel=(self.csel, False), add=received(st_up, down_send, "up"),
                            after=st_down[4])
        token = to_chips(up_sum, "up")
        down_sum = _grad_w_down(act, dx3b, "grad_w_down_keep", sel=(self.csel, False),
                                add=received(st_down, up_sum, "down"), after=token)
        return to_chips(down_sum, "down")

    def out_grad(self, g_out):
        slabs = [g_out.reshape(N_DEV, D_MODEL // N_DEV, D_MODEL)]
        return self._chips_start(slabs, _exchange_pair(slabs, "reduce_pair_out"), [256], "out")

    def in_grad(self, g_in):
        slabs = [jnp.stack([_natural_rows(g_in, SUPER_STEP * j, SUPER_STEP * j + SUPER) for j in range(N_DEV)])]
        return self._chips_start(slabs, _exchange_pair(slabs, "reduce_pair_in"), [SUPER], "in")

    def small_start(self, small):
        self.st_small = _remote_start([small], [_landing(small, self.me)], _everyone_plan, N_DEV - 1, "gather_start_small")

    def small_end(self, after):
        return _remote_wait(self.st_small, after, _everyone_plan, "gather_small_wait")[1][0]

    def reduce_end(self, tag, after):
        return _remote_wait(self.reduces[tag], after, _chips_plan, f"reduce_wait_{tag}")


def kernel(x, mix_norm_g, w_in, conv_w, conv_b, dt_bias, A_log, D_skip, ssm_norm_g, attn_sinks, attn_out_norm_g, w_out, mlp_norm_g, w_up, w_down, final_norm_g, loss_target, m_mix_norm_g, m_w_in, m_conv_w, m_conv_b, m_dt_bias, m_A_log, m_D_skip, m_ssm_norm_g, m_attn_sinks, m_attn_out_norm_g, m_w_out, m_mlp_norm_g, m_w_up, m_w_down, m_final_norm_g, v_mix_norm_g, v_w_in, v_conv_w, v_conv_b, v_dt_bias, v_A_log, v_D_skip, v_ssm_norm_g, v_attn_sinks, v_attn_out_norm_g, v_w_out, v_mlp_norm_g, v_w_up, v_w_down, v_final_norm_g):
    xi, yi, ci = _coords()
    me = 4 * xi + 2 * yi + ci
    csel = jnp.reshape(ci, (1,)).astype(jnp.int32)
    qsel = jnp.reshape(2 * xi + yi, (1,)).astype(jnp.int32)
    w = dict(mix_norm_g=mix_norm_g, conv_b=conv_b, dt_bias=dt_bias, A_log=A_log, D_skip=D_skip,
             ssm_norm_g=ssm_norm_g, attn_sinks=attn_sinks, attn_out_norm_g=attn_out_norm_g, mlp_norm_g=mlp_norm_g,
             final_norm_g=final_norm_g)
    hooks = _ShardedWeights(w_in[0], w_out[0], conv_w[0], w_up[0], w_down[0], me, csel)
    p = dict(w, mix_norm_g=mix_norm_g + hooks.start_token[0:1, 0:1])
    dx, small = _local_step(x[0], loss_target[0], p, hooks)
    hooks.small_start(small)
    big = {}
    after = dx
    for name, wt, mt, vt, tile in [
            ("up", w_up, m_w_up, v_w_up, (512, SLAB)), ("down", w_down, m_w_down, v_w_down, (256, D_MODEL)),
            ("out", w_out, m_w_out, v_w_out, (256, D_MODEL))]:
        (chip_sums,), (from_chips,) = hooks.reduce_end(name, after)
        res = _adamw_big(wt[0], mt[0], vt[0], chip_sums, from_chips, qsel, tile, f"adamw_w_{name}")
        big["w_" + name] = tuple(r[None] for r in res)
        after = res[0]
    (chip_sums,), (from_chips,) = hooks.reduce_end("in", after)
    g_super = _sum_partials(chip_sums, from_chips, qsel, 512, "grad_w_in_sum")
    g_in = lax.dynamic_slice(g_super, (2 * me, 0), (PER, D_MODEL))
    res = _adamw_tiled(jnp.transpose(w_in[0]), g_in, jnp.transpose(m_w_in[0]), jnp.transpose(v_w_in[0]), 512,
                       "adamw_w_in")
    big["w_in"] = tuple(jnp.transpose(r)[None] for r in (g_in, *res))
    after = res[0]
    gsum = _small_sum(hooks.small_end(after), "small_sum")
    loss = gsum[9, 64]
    gs = _unpack_small(gsum, CONV_DIM)
    cw = CONV_DIM // N_DEV
    g_conv_shard = lax.dynamic_slice(gsum[5:9, :], (0, me * cw), (CONV_K, cw))

    def pack(s):
        return _pack_small(s["mix_norm_g"], s["conv_b"], s["ssm_norm_g"], s["attn_out_norm_g"], s["mlp_norm_g"],
                           s["final_norm_g"], s["conv_w"][0], s["dt_bias"], s["A_log"], s["D_skip"], s["attn_sinks"])

    wp = pack(dict(w, conv_w=conv_w))
    mp = pack(dict(mix_norm_g=m_mix_norm_g, conv_b=m_conv_b, ssm_norm_g=m_ssm_norm_g,
                   attn_out_norm_g=m_attn_out_norm_g, mlp_norm_g=m_mlp_norm_g, final_norm_g=m_final_norm_g,
                   conv_w=m_conv_w, dt_bias=m_dt_bias, A_log=m_A_log, D_skip=m_D_skip, attn_sinks=m_attn_sinks))
    vp = pack(dict(mix_norm_g=v_mix_norm_g, conv_b=v_conv_b, ssm_norm_g=v_ssm_norm_g,
                   attn_out_norm_g=v_attn_out_norm_g, mlp_norm_g=v_mlp_norm_g, final_norm_g=v_final_norm_g,
                   conv_w=v_conv_w, dt_bias=v_dt_bias, A_log=v_A_log, D_skip=v_D_skip, attn_sinks=v_attn_sinks))
    gp = jnp.concatenate([gsum[0:5], jnp.pad(g_conv_shard, ((0, 0), (0, D_MODEL - cw))), gsum[9:10],
                          jnp.zeros((SMALL_ROWS - 10, D_MODEL), F32)], axis=0)
    dp, mnp, vnp = _adamw_small(wp, gp, mp, vp, "adamw_small")
    grads = dict(gs, conv_w=g_conv_shard[None])
    deltas = _unpack_small(dp, cw)
    new_m = _unpack_small(mnp, cw)
    new_v = _unpack_small(vnp, cw)
    for k, name in enumerate(["w_in", "w_out", "w_up", "w_down"]):
        grads[name], deltas[name], new_m[name], new_v[name] = big[name]
    return (loss, dx[None], *[grads[n] for n in WEIGHT_ORDER], *[deltas[n] for n in WEIGHT_ORDER],
            *[new_m[n] for n in WEIGHT_ORDER], *[new_v[n] for n in WEIGHT_ORDER])
```

```python
import functools

import jax
import jax.numpy as jnp
from jax import lax
from jax.experimental import pallas as pl
from jax.experimental.pallas import tpu as pltpu

F32 = jnp.float32
BF16 = jnp.bfloat16
HI = lax.Precision.HIGHEST
MESH = pl.DeviceIdType.MESH

EPS = 1e-5
D_MODEL = 2048
D_INNER = 1024
N_HEADS = 16
HEAD_DIM = 64
N_GROUPS = 4
D_STATE = 128
CHUNK = 128
CONV_K = 4
CONV_DIM = 2048
ATTN_W = 1024
KV_W = 128
WINDOW = 128
D_FF = 8192
IN_PROJ = 4368
N_DEV = 8
NP = 4608
OFF_Z, OFF_X, OFF_B, OFF_C, OFF_Q, OFF_K, OFF_V, OFF_DT = 0, 1024, 2048, 2560, 3072, 4096, 4224, 4352
NAT_DT = 3072

ADAM_LR = 0.001
ADAM_B1 = 0.9
ADAM_B2 = 0.999
ADAM_EPS = 1e-08
ADAM_WD = 0.01
ADAM_STEP = 10

VMEM_LIMIT = 52 * 1024 * 1024
SMALL_ROWS = 16
NEG = -1e30


def _cparams(sem=None):
    return pltpu.CompilerParams(dimension_semantics=sem, vmem_limit_bytes=VMEM_LIMIT)


def _split3(v):
    hi = v.astype(BF16)
    rest = v - hi.astype(F32)
    mid = rest.astype(BF16)
    return hi, mid, (rest - mid.astype(F32)).astype(BF16)


def _hdot(a, b, data):
    if data == "a":
        sel = b.astype(BF16)
        return sum(_dot_nn(part, sel) for part in _split3(a))
    sel = a.astype(BF16)
    return sum(_dot_nn(sel, part) for part in _split3(b))


def _dot_nn(a, b):
    return lax.dot_general(a, b, (((1,), (0,)), ((), ())), preferred_element_type=F32)


def _dot_nt(a, b):
    return lax.dot_general(a, b, (((1,), (1,)), ((), ())), preferred_element_type=F32)


def _dot_tn(a, b):
    return lax.dot_general(a, b, (((0,), (0,)), ((), ())), preferred_element_type=F32)


def _softplus(v):
    return jnp.maximum(v, 0.0) + jnp.log1p(jnp.exp(-jnp.abs(v)))


def _sigmoid(v):
    return 1.0 / (1.0 + jnp.exp(-v))


def _matmul(a, b, *, mode, grid, a_spec, b_spec, out_shapes, out_specs, tile, name,
            extras=(), extra_specs=(), epilogue=None, after=None, dot_fn=None, prefetch=None):
    nk = grid[2]
    n_ex = len(extras)
    n_out = len(out_shapes)
    dot = dot_fn if dot_fn is not None else {"nn": _dot_nn, "nt": _dot_nt, "tn": _dot_tn}[mode]

    def finish(acc, ex_refs, out_refs):
        res = (acc,) if epilogue is None else epilogue(acc, *[e[...] for e in ex_refs])
        for o, r in zip(out_refs, res):
            o[...] = r.astype(o.dtype)

    def body(*refs):
        a_ref, b_ref = refs[0], refs[1]
        ex_refs = refs[2:2 + n_ex]
        out_refs = refs[2 + n_ex:2 + n_ex + n_out]
        part = dot(a_ref[...].astype(BF16), b_ref[...].astype(BF16))
        if nk == 1:
            finish(part, ex_refs, out_refs)
        else:
            acc_ref = refs[-1]
            k = pl.program_id(2)

            @pl.when(k == 0)
            def _():
                acc_ref[...] = part

            @pl.when(k > 0)
            def _():
                acc_ref[...] += part

            @pl.when(k == nk - 1)
            def _():
                finish(acc_ref[...], ex_refs, out_refs)

    scratch = [] if nk == 1 else [pltpu.VMEM(tile, F32)]
    n_pre = 0 if prefetch is None else 1
    tok_specs = [] if after is None else [pl.BlockSpec((8, 128), lambda *_: (0, 0))]
    tok_args = [] if after is None else [after]

    def body_with_token(*refs):
        refs = refs[n_pre:]
        body(*refs[:2 + n_ex], *refs[2 + n_ex + len(tok_args):])

    in_specs = [a_spec, b_spec, *extra_specs, *tok_specs]
    params = _cparams(("parallel", "parallel", "arbitrary"))
    if prefetch is None:
        return pl.pallas_call(
            body_with_token, grid=grid, in_specs=in_specs, out_specs=list(out_specs), out_shape=list(out_shapes),
            scratch_shapes=scratch, name=name, compiler_params=params)(a, b, *extras, *tok_args)
    return pl.pallas_call(
        body_with_token,
        grid_spec=pltpu.PrefetchScalarGridSpec(num_scalar_prefetch=1, grid=grid, in_specs=in_specs,
                                               out_specs=list(out_specs), scratch_shapes=scratch),
        out_shape=list(out_shapes), name=name, compiler_params=params)(prefetch, a, b, *extras, *tok_args)


def _mm_simple(a, b, *, mode, M, N, K, tm, tn, tk, out_dtype, name, extras=(), epilogue=None, n_out=1,
               out_dtypes=None, after=None):
    grid = (M // tm, N // tn, K // tk)
    if mode == "nn":
        a_spec = pl.BlockSpec((tm, tk), lambda i, j, k: (i, k))
        b_spec = pl.BlockSpec((tk, tn), lambda i, j, k: (k, j))
    elif mode == "nt":
        a_spec = pl.BlockSpec((tm, tk), lambda i, j, k: (i, k))
        b_spec = pl.BlockSpec((tn, tk), lambda i, j, k: (j, k))
    else:
        a_spec = pl.BlockSpec((tk, tm), lambda i, j, k: (k, i))
        b_spec = pl.BlockSpec((tk, tn), lambda i, j, k: (k, j))
    o_spec = pl.BlockSpec((tm, tn), lambda i, j, k: (i, j))
    dts = out_dtypes if out_dtypes is not None else [out_dtype] * n_out
    return _matmul(a, b, mode=mode, grid=grid, a_spec=a_spec, b_spec=b_spec,
                   out_shapes=[jax.ShapeDtypeStruct((M, N), d) for d in dts],
                   out_specs=[o_spec] * len(dts), tile=(tm, tn), name=name,
                   extras=extras, extra_specs=[o_spec] * len(extras), epilogue=epilogue, after=after)


ROW_BLOCK = 256


def _rmsnorm_fwd(x, g, name):
    T, D = x.shape

    def body(x_ref, g_ref, o_ref):
        xf = x_ref[...]
        r = lax.rsqrt(jnp.mean(xf * xf, axis=-1, keepdims=True) + EPS)
        o_ref[...] = (xf * r * g_ref[...]).astype(BF16)

    return pl.pallas_call(
        body, grid=(T // ROW_BLOCK,),
        in_specs=[pl.BlockSpec((ROW_BLOCK, D), lambda i: (i, 0)), pl.BlockSpec((1, D), lambda i: (0, 0))],
        out_specs=pl.BlockSpec((ROW_BLOCK, D), lambda i: (i, 0)),
        out_shape=jax.ShapeDtypeStruct((T, D), BF16), name=name, compiler_params=_cparams(("parallel",)),
    )(x, g)


def _rmsnorm_bwd(dh, x, g, dres, name):
    T, D = x.shape

    def body(dh_ref, x_ref, g_ref, dres_ref, dx_ref, dxb_ref, dg_ref):
        i = pl.program_id(0)
        xf = x_ref[...]
        r = lax.rsqrt(jnp.mean(xf * xf, axis=-1, keepdims=True) + EPS)
        xh = xf * r
        d = dh_ref[...]

        @pl.when(i == 0)
        def _():
            dg_ref[...] = jnp.zeros_like(dg_ref)

        dg_ref[...] += jnp.sum(d * xh, axis=0, keepdims=True)
        dxh = d * g_ref[...]
        dx = r * (dxh - xh * jnp.mean(dxh * xh, axis=-1, keepdims=True)) + dres_ref[...]
        dx_ref[...] = dx
        dxb_ref[...] = dx.astype(BF16)

    row = pl.BlockSpec((ROW_BLOCK, D), lambda i: (i, 0))
    vec = pl.BlockSpec((1, D), lambda i: (0, 0))
    return pl.pallas_call(
        body, grid=(T // ROW_BLOCK,), in_specs=[row, row, vec, row], out_specs=[row, row, vec],
        out_shape=[jax.ShapeDtypeStruct((T, D), F32), jax.ShapeDtypeStruct((T, D), BF16),
                   jax.ShapeDtypeStruct((1, D), F32)],
        name=name, compiler_params=_cparams(("arbitrary",)),
    )(dh, x, g, dres)


def _final_loss(x3, tgt, g, name):
    T, D = x3.shape

    def body(x_ref, t_ref, g_ref, loss_ref, dg_ref, dx_ref, dxb_ref):
        i = pl.program_id(0)
        xf = x_ref[...]
        r = lax.rsqrt(jnp.mean(xf * xf, axis=-1, keepdims=True) + EPS)
        xh = xf * r
        gg = g_ref[...]
        err = xh * gg - t_ref[...]

        @pl.when(i == 0)
        def _():
            dg_ref[...] = jnp.zeros_like(dg_ref)
            loss_ref[...] = jnp.zeros_like(loss_ref)

        part = jnp.sum(jnp.sum(err * err, axis=-1, keepdims=True), axis=0, keepdims=True) * (0.5 / D)
        loss_ref[...] += jnp.broadcast_to(part, loss_ref.shape)
        dout = err * (1.0 / D)
        dg_ref[...] += jnp.sum(dout * xh, axis=0, keepdims=True)
        dxh = dout * gg
        dx = r * (dxh - xh * jnp.mean(dxh * xh, axis=-1, keepdims=True))
        dx_ref[...] = dx
        dxb_ref[...] = dx.astype(BF16)

    row = pl.BlockSpec((ROW_BLOCK, D), lambda i: (i, 0))
    vec = pl.BlockSpec((1, D), lambda i: (0, 0))
    return pl.pallas_call(
        body, grid=(T // ROW_BLOCK,), in_specs=[row, row, vec],
        out_specs=[pl.BlockSpec((1, 128), lambda i: (0, 0)), vec, row, row],
        out_shape=[jax.ShapeDtypeStruct((1, 128), F32), jax.ShapeDtypeStruct((1, D), F32),
                   jax.ShapeDtypeStruct((T, D), F32), jax.ShapeDtypeStruct((T, D), BF16)],
        name=name, compiler_params=_cparams(("arbitrary",)),
    )(x3, tgt, g)


CONV_BLOCK = 256


def _conv_apply(u, w, b):
    row = lax.broadcasted_iota(jnp.int32, u.shape, 0)
    acc = b + w[CONV_K - 1:CONV_K, :] * u
    shifted = []
    for j in range(1, CONV_K):
        uj = jnp.where(row >= j, pltpu.roll(u, j, axis=0), 0.0)
        shifted.append(uj)
        acc = acc + w[CONV_K - 1 - j:CONV_K - j, :] * uj
    return acc, shifted


def _conv_fwd(proj, conv_w, conv_b, name):
    T = proj.shape[0]
    cb0 = OFF_X // CONV_BLOCK

    def body(u_ref, w_ref, b_ref, o_ref):
        c, _ = _conv_apply(u_ref[...], w_ref[...], b_ref[...])
        o_ref[...] = c * _sigmoid(c)

    return pl.pallas_call(
        body, grid=(CONV_DIM // CONV_BLOCK,),
        in_specs=[pl.BlockSpec((T, CONV_BLOCK), lambda j: (0, cb0 + j)),
                  pl.BlockSpec((CONV_K, CONV_BLOCK), lambda j: (0, j)),
                  pl.BlockSpec((1, CONV_BLOCK), lambda j: (0, j))],
        out_specs=pl.BlockSpec((T, CONV_BLOCK), lambda j: (0, j)),
        out_shape=jax.ShapeDtypeStruct((T, CONV_DIM), F32), name=name, compiler_params=_cparams(("parallel",)),
    )(proj, conv_w, conv_b)


def _conv_bwd(proj, dact, conv_w, conv_b, dproj, name):
    T = proj.shape[0]
    cb0 = OFF_X // CONV_BLOCK

    def body(u_ref, d_ref, w_ref, b_ref, _, du_ref, dw_ref, db_ref):
        u = u_ref[...]
        w = w_ref[...]
        c, shifted = _conv_apply(u, w, b_ref[...])
        sg = _sigmoid(c)
        dc = d_ref[...] * sg * (1.0 + c * (1.0 - sg))
        row = lax.broadcasted_iota(jnp.int32, u.shape, 0)
        du = w[CONV_K - 1:CONV_K, :] * dc
        dw_ref[CONV_K - 1:CONV_K, :] = jnp.sum(dc * u, axis=0, keepdims=True)
        for j in range(1, CONV_K):
            dcj = jnp.where(row < T - j, pltpu.roll(dc, T - j, axis=0), 0.0)
            du = du + w[CONV_K - 1 - j:CONV_K - j, :] * dcj
            dw_ref[CONV_K - 1 - j:CONV_K - j, :] = jnp.sum(dc * shifted[j - 1], axis=0, keepdims=True)
        db_ref[...] = jnp.sum(dc, axis=0, keepdims=True)
        du_ref[...] = du.astype(BF16)

    return pl.pallas_call(
        body, grid=(CONV_DIM // CONV_BLOCK,),
        in_specs=[pl.BlockSpec((T, CONV_BLOCK), lambda j: (0, cb0 + j)),
                  pl.BlockSpec((T, CONV_BLOCK), lambda j: (0, j)),
                  pl.BlockSpec((CONV_K, CONV_BLOCK), lambda j: (0, j)),
                  pl.BlockSpec((1, CONV_BLOCK), lambda j: (0, j)), pl.BlockSpec(memory_space=pl.ANY)],
        out_specs=[pl.BlockSpec((T, CONV_BLOCK), lambda j: (0, cb0 + j)),
                   pl.BlockSpec((CONV_K, CONV_BLOCK), lambda j: (0, j)),
                   pl.BlockSpec((1, CONV_BLOCK), lambda j: (0, j))],
        out_shape=[jax.ShapeDtypeStruct(dproj.shape, BF16), jax.ShapeDtypeStruct((CONV_K, CONV_DIM), F32),
                   jax.ShapeDtypeStruct((1, CONV_DIM), F32)],
        input_output_aliases={4: 0}, name=name, compiler_params=_cparams(("parallel",)),
    )(proj, dact, conv_w, conv_b, dproj)


GROUP_W = D_INNER // N_GROUPS
HEADS_PER_GROUP = N_HEADS // N_GROUPS


def _expand_mat():
    h = lax.broadcasted_iota(jnp.int32, (N_HEADS, D_INNER), 0)
    j = lax.broadcasted_iota(jnp.int32, (N_HEADS, D_INNER), 1)
    return (j // HEAD_DIM == h).astype(F32)


def _reduce_mat(g):
    j = lax.broadcasted_iota(jnp.int32, (GROUP_W, N_HEADS), 0)
    h = lax.broadcasted_iota(jnp.int32, (GROUP_W, N_HEADS), 1)
    return (g * HEADS_PER_GROUP + j // HEAD_DIM == h).astype(F32)


def _col16(v, h):
    lane = lax.broadcasted_iota(jnp.int32, v.shape, 1)
    return jnp.sum(jnp.where(lane == h, v, 0.0), axis=1, keepdims=True)


def _ssd_pre(dt_raw, dtT_raw, dtb, dtbT, alog, alogT):
    Q = CHUNK
    xdt = dt_raw + dtb
    dt = _softplus(xdt)
    dtT = _softplus(dtT_raw + dtbT)
    A = -jnp.exp(alog)
    AT = -jnp.exp(alogT)
    row = lax.broadcasted_iota(jnp.int32, (Q, Q), 0)
    col = lax.broadcasted_iota(jnp.int32, (Q, Q), 1)
    tril = (row >= col).astype(F32)
    triu = (row <= col).astype(F32)
    cs = _hdot(tril, dt * A, "b")
    csT = _hdot(dtT * AT, triu, "a")
    return xdt, dt, A, cs, csT, row >= col, triu


def _decay_matrix(cs, csT, h, causal):
    seg = _col16(cs, h) - csT[h:h + 1, :]
    return jnp.where(causal, jnp.exp(jnp.minimum(seg, 0.0)), 0.0)


def _ssd_in_specs(nc, rev):
    def cidx(c):
        return (nc - 1 - c) if rev else c

    return [
        pl.BlockSpec((CHUNK, D_INNER), lambda c: (cidx(c), 0)),
        pl.BlockSpec((CHUNK, 512), lambda c: (cidx(c), 2)),
        pl.BlockSpec((CHUNK, 512), lambda c: (cidx(c), 3)),
        pl.BlockSpec((CHUNK, D_INNER), lambda c: (cidx(c), 0)),
        pl.BlockSpec((CHUNK, 128), lambda c: (cidx(c), OFF_DT // 128)),
        pl.BlockSpec((N_HEADS, CHUNK), lambda c: (0, cidx(c))),
        pl.BlockSpec((1, N_HEADS), lambda c: (0, 0)),
        pl.BlockSpec((N_HEADS, 1), lambda c: (0, 0)),
        pl.BlockSpec((1, N_HEADS), lambda c: (0, 0)),
        pl.BlockSpec((N_HEADS, 1), lambda c: (0, 0)),
        pl.BlockSpec((1, D_INNER), lambda c: (0, 0)),
        pl.BlockSpec((1, D_INNER), lambda c: (0, 0)),
    ]


def _ssd_fwd(xbc, proj, dtT, dtb, dtbT, alog, alogT, dfull, ng, name):
    T = xbc.shape[0]
    nc = T // CHUNK
    Q = CHUNK

    def body(xs_ref, B_ref, C_ref, z_ref, dt_ref, dtT_ref, dtb_ref, dtbT_ref, al_ref, alT_ref, df_ref, ng_ref,
             y_ref, ypre_ref, hs_ref, h_scr):
        c = pl.program_id(0)

        @pl.when(c == 0)
        def _():
            h_scr[...] = jnp.zeros_like(h_scr)

        _, dt, _, cs, csT, causal, _ = _ssd_pre(dt_ref[:, :N_HEADS], dtT_ref[...], dtb_ref[...], dtbT_ref[...],
                                                al_ref[...], alT_ref[...])
        ex = _expand_mat()
        dt_full = _hdot(dt, ex, "a")
        cs_full = _hdot(cs, ex, "a")
        cs_last = cs_full[Q - 1:Q, :]
        xs = xs_ref[...]
        xd = xs * dt_full
        e_full = jnp.exp(cs_full)
        dec_full = jnp.exp(cs_last - cs_full)
        cd_full = jnp.exp(cs_last)
        lane_head = lax.broadcasted_iota(jnp.int32, (1, GROUP_W), 1) // HEAD_DIM
        for g in range(N_GROUPS):
            sl = slice(g * GROUP_W, (g + 1) * GROUP_W)
            Bg = B_ref[:, g * D_STATE:(g + 1) * D_STATE].astype(BF16)
            Cg = C_ref[:, g * D_STATE:(g + 1) * D_STATE].astype(BF16)
            CB = _dot_nt(Cg, Bg)
            hg = h_scr[g]
            yoff = _dot_nn(Cg, hg.astype(BF16)) * e_full[:, sl]
            xd_g = xd[:, sl]
            S = _dot_tn(Bg, (xd_g * dec_full[:, sl]).astype(BF16))
            xd_b = xd_g.astype(BF16)
            ydiag = jnp.zeros((Q, GROUP_W), F32)
            for r in range(HEADS_PER_GROUP):
                Lm = _decay_matrix(cs, csT, g * HEADS_PER_GROUP + r, causal)
                Gm = (CB * Lm).astype(BF16)
                ydiag = ydiag + _dot_nn(Gm, jnp.where(lane_head == r, xd_b, jnp.zeros_like(xd_b)))
            hs_ref[0, g] = hg
            h_scr[g] = hg * cd_full[:, sl] + S
            ypre = ydiag + yoff + xs[:, sl] * df_ref[:, sl]
            ypre_ref[:, sl] = ypre
            zg = z_ref[:, sl]
            yz = ypre * zg * _sigmoid(zg)
            rn = lax.rsqrt(jnp.mean(yz * yz, axis=-1, keepdims=True) + EPS)
            y_ref[:, sl] = (yz * rn * ng_ref[:, sl]).astype(BF16)

    return pl.pallas_call(
        body, grid=(nc,), in_specs=_ssd_in_specs(nc, False),
        out_specs=[pl.BlockSpec((CHUNK, D_INNER), lambda c: (c, 0)),
                   pl.BlockSpec((CHUNK, D_INNER), lambda c: (c, 0)),
                   pl.BlockSpec((1, N_GROUPS, D_STATE, GROUP_W), lambda c: (c, 0, 0, 0))],
        out_shape=[jax.ShapeDtypeStruct((T, D_INNER + ATTN_W), BF16), jax.ShapeDtypeStruct((T, D_INNER), F32),
                   jax.ShapeDtypeStruct((nc, N_GROUPS, D_STATE, GROUP_W), F32)],
        scratch_shapes=[pltpu.VMEM((N_GROUPS, D_STATE, GROUP_W), F32)],
        name=name, compiler_params=_cparams(("arbitrary",)),
    )(xbc, xbc, xbc, proj, proj, dtT, dtb, dtbT, alog, alogT, dfull, ng)


def _ssd_bwd(xbc, proj, dtT, dtb, dtbT, alog, alogT, dfull, ng, ypre, hs, dy, name):
    T = xbc.shape[0]
    nc = T // CHUNK
    Q = CHUNK

    def body(xs_ref, B_ref, C_ref, z_ref, dt_ref, dtT_ref, dtb_ref, dtbT_ref, al_ref, alT_ref, df_ref, ng_ref,
             ypre_ref, hs_ref, dy_ref,
             dz_ref, dxbc_ref, ddtb_ref, dal_ref, dD_ref, dng_ref, dh_scr):
        step = pl.program_id(0)

        @pl.when(step == 0)
        def _():
            dh_scr[...] = jnp.zeros_like(dh_scr)
            ddtb_ref[...] = jnp.zeros_like(ddtb_ref)
            dal_ref[...] = jnp.zeros_like(dal_ref)
            dD_ref[...] = jnp.zeros_like(dD_ref)
            dng_ref[...] = jnp.zeros_like(dng_ref)

        xdt, dt, A, cs, csT, causal, triu = _ssd_pre(dt_ref[:, :N_HEADS], dtT_ref[...], dtb_ref[...],
                                                    dtbT_ref[...], al_ref[...], alT_ref[...])
        ex = _expand_mat()
        dt_full = _hdot(dt, ex, "a")
        cs_full = _hdot(cs, ex, "a")
        cs_last = cs_full[Q - 1:Q, :]
        xs = xs_ref[...]
        xd = xs * dt_full
        e_full = jnp.exp(cs_full)
        dec_full = jnp.exp(cs_last - cs_full)
        cd_full = jnp.exp(cs_last)
        lane_head = lax.broadcasted_iota(jnp.int32, (1, GROUP_W), 1) // HEAD_DIM
        is_last = lax.broadcasted_iota(jnp.int32, (Q, 1), 0) == Q - 1
        dcs16 = jnp.zeros((Q, N_HEADS), F32)
        ddtx16 = jnp.zeros((Q, N_HEADS), F32)
        dD16 = jnp.zeros((8, N_HEADS), F32)
        lane16 = lax.broadcasted_iota(jnp.int32, (1, N_HEADS), 1)
        sub16 = lax.broadcasted_iota(jnp.int32, (N_HEADS, 1), 0)
        col_sums = jnp.zeros((N_HEADS, Q), F32)
        for g in range(N_GROUPS):
            sl = slice(g * GROUP_W, (g + 1) * GROUP_W)
            red = _reduce_mat(g)
            ypre_g = ypre_ref[:, sl]
            zg = z_ref[:, sl]
            sg = _sigmoid(zg)
            silu = zg * sg
            yz = ypre_g * silu
            rn = lax.rsqrt(jnp.mean(yz * yz, axis=-1, keepdims=True) + EPS)
            yh = yz * rn
            dy_g = dy_ref[:, sl]
            dng_ref[:, sl] += jnp.sum(dy_g * yh, axis=0, keepdims=True)
            dyh = dy_g * ng_ref[:, sl]
            dyz = rn * (dyh - yh * jnp.mean(dyh * yh, axis=-1, keepdims=True))
            dY = dyz * silu
            dz_ref[:, sl] = (dyz * ypre_g * sg * (1.0 + zg * (1.0 - sg))).astype(BF16)
            xs_g = xs[:, sl]
            xd_g = xd[:, sl]
            dec_g = dec_full[:, sl]
            cd_g = cd_full[:, sl]
            d_g = df_ref[:, sl]
            Bg = B_ref[:, g * D_STATE:(g + 1) * D_STATE].astype(BF16)
            Cg = C_ref[:, g * D_STATE:(g + 1) * D_STATE].astype(BF16)
            CB = _dot_nt(Cg, Bg)
            hg = hs_ref[0, g]
            hgb = hg.astype(BF16)
            yoff = _dot_nn(Cg, hgb) * e_full[:, sl]
            dhn = dh_scr[g]
            dhnb = dhn.astype(BF16)
            dYE = (dY * e_full[:, sl]).astype(BF16)
            dC = _dot_nt(dYE, hgb)
            dh_direct = _dot_tn(Cg, dYE)
            dXdd = _dot_nn(Bg, dhnb)
            dB = _dot_nt((xd_g * dec_g).astype(BF16), dhnb)
            dcd = jnp.sum(dhn * hg, axis=0, keepdims=True)
            dh_scr[g] = dh_direct + cd_g * dhn
            dYb = dY.astype(BF16)
            xd_b = xd_g.astype(BF16)
            dCB = jnp.zeros((Q, Q), F32)
            dXd = dXdd * dec_g
            for r in range(HEADS_PER_GROUP):
                h = g * HEADS_PER_GROUP + r
                Lm = _decay_matrix(cs, csT, h, causal)
                Gf = CB * Lm
                dYr = jnp.where(lane_head == r, dYb, jnp.zeros_like(dYb))
                dG = _dot_nt(dYr, xd_b)
                dCB = dCB + dG * Lm
                dXd = dXd + _dot_tn(Gf.astype(BF16), dYr)
                Mm = dG * Gf
                dcs16 = dcs16 + jnp.where(lane16 == h, jnp.sum(Mm, axis=1, keepdims=True), 0.0)
                col_sums = col_sums + jnp.where(sub16 == h, jnp.sum(Mm, axis=0, keepdims=True), 0.0)
            dCBb = dCB.astype(BF16)
            dC = dC + _dot_nn(dCBb, Bg)
            dB = dB + _dot_tn(dCBb, Cg)
            w_state = dXdd * dec_g * xd_g
            t_last = jnp.sum(w_state, axis=0, keepdims=True) + dcd * cd_g
            dcs_g = dY * yoff - w_state + jnp.where(is_last, t_last, 0.0)
            dcs16 = dcs16 + _hdot(dcs_g, red, "a")
            ddtx16 = ddtx16 + _hdot(dXd * xs_g, red, "a")
            dD16 = dD16 + _hdot(jnp.broadcast_to(jnp.sum(dY * xs_g, axis=0, keepdims=True), (8, GROUP_W)), red, "a")
            dxbc_ref[:, sl] = dXd * dt_full[:, sl] + dY * d_g
            dxbc_ref[:, D_INNER + g * D_STATE:D_INNER + (g + 1) * D_STATE] = dB
            dxbc_ref[:, D_INNER + 512 + g * D_STATE:D_INNER + 512 + (g + 1) * D_STATE] = dC
        eye = (lax.broadcasted_iota(jnp.int32, (N_HEADS, N_HEADS), 0)
               == lax.broadcasted_iota(jnp.int32, (N_HEADS, N_HEADS), 1)).astype(BF16)
        dcs16 = dcs16 - sum(_dot_tn(part, eye) for part in _split3(col_sums))
        da = _hdot(triu, dcs16, "b")
        ddt = da * A + ddtx16
        ddt_raw = ddt * _sigmoid(xdt)
        pr = lax.broadcasted_iota(jnp.int32, (N_HEADS, 128), 0)
        pc = lax.broadcasted_iota(jnp.int32, (N_HEADS, 128), 1)
        dz_ref[:, D_INNER:OFF_DT] = jnp.zeros((Q, OFF_DT - D_INNER), BF16)
        dz_ref[:, OFF_DT:OFF_DT + 128] = _hdot(ddt_raw, (pr == pc).astype(F32), "a").astype(BF16)
        dz_ref[:, OFF_DT + 128:] = jnp.zeros((Q, NP - OFF_DT - 128), BF16)
        ddtb_ref[...] += jnp.sum(ddt_raw, axis=0, keepdims=True)
        dal_ref[...] += jnp.sum(da * dt, axis=0, keepdims=True) * A
        dD_ref[...] += dD16[0:1, :]

    def rc(c):
        return nc - 1 - c

    in_specs = _ssd_in_specs(nc, True) + [
        pl.BlockSpec((CHUNK, D_INNER), lambda c: (rc(c), 0)),
        pl.BlockSpec((1, N_GROUPS, D_STATE, GROUP_W), lambda c: (rc(c), 0, 0, 0)),
        pl.BlockSpec((CHUNK, D_INNER), lambda c: (rc(c), 0)),
    ]
    small = pl.BlockSpec((1, N_HEADS), lambda c: (0, 0))
    return pl.pallas_call(
        body, grid=(nc,), in_specs=in_specs,
        out_specs=[pl.BlockSpec((CHUNK, NP), lambda c: (rc(c), 0)),
                   pl.BlockSpec((CHUNK, CONV_DIM), lambda c: (rc(c), 0)),
                   small, small, small,
                   pl.BlockSpec((1, D_INNER), lambda c: (0, 0))],
        out_shape=[jax.ShapeDtypeStruct((T, NP), BF16), jax.ShapeDtypeStruct((T, CONV_DIM), F32),
                   jax.ShapeDtypeStruct((1, N_HEADS), F32), jax.ShapeDtypeStruct((1, N_HEADS), F32),
                   jax.ShapeDtypeStruct((1, N_HEADS), F32), jax.ShapeDtypeStruct((1, D_INNER), F32)],
        scratch_shapes=[pltpu.VMEM((N_GROUPS, D_STATE, GROUP_W), F32)],
        name=name, compiler_params=_cparams(("arbitrary",)),
    )(xbc, xbc, xbc, proj, proj, dtT, dtb, dtbT, alog, alogT, dfull, ng, ypre, hs, dy)


N_PAIRS = ATTN_W // 128
PAIRS_PER_KV = N_PAIRS // 2
ATTN_SCALE = HEAD_DIM ** -0.5


def _kv_variants(kk):
    lo = lax.broadcasted_iota(jnp.int32, kk.shape, 1) < HEAD_DIM
    zero = jnp.zeros_like(kk)
    k00 = jnp.where(lo, kk, zero)
    k11 = jnp.where(lo, zero, kk)
    k01 = pltpu.roll(k00, HEAD_DIM, axis=1)
    k10 = pltpu.roll(k11, HEAD_DIM, axis=1)
    return [[k00.astype(BF16), k01.astype(BF16)], [k10.astype(BF16), k11.astype(BF16)]]


def _attn_valid(n):
    i = lax.broadcasted_iota(jnp.int32, (WINDOW, 2 * WINDOW), 0)
    j = lax.broadcasted_iota(jnp.int32, (WINDOW, 2 * WINDOW), 1)
    return (j > i) & (j <= i + WINDOW) & (n * WINDOW + j >= WINDOW)


def _attn_probs(qp, kvar, valid, sk):
    s = _dot_nt(qp, kvar) * ATTN_SCALE
    s = jnp.where(valid, s, NEG)
    m = jnp.maximum(jnp.max(s, axis=1, keepdims=True), sk)
    pe = jnp.exp(s - m)
    es = jnp.exp(sk - m)
    den = jnp.sum(pe, axis=1, keepdims=True) + es
    inv = 1.0 / den
    return pe * inv, es * inv


def _sink(sinks, r):
    lane = lax.broadcasted_iota(jnp.int32, sinks.shape, 1)
    return jnp.sum(jnp.where(lane == r, sinks, 0.0), axis=1, keepdims=True)


def _kv_specs():
    return [pl.BlockSpec((WINDOW, KV_W), lambda n: (jnp.maximum(n - 1, 0), OFF_K // KV_W)),
            pl.BlockSpec((WINDOW, KV_W), lambda n: (n, OFF_K // KV_W)),
            pl.BlockSpec((WINDOW, KV_W), lambda n: (jnp.maximum(n - 1, 0), OFF_V // KV_W)),
            pl.BlockSpec((WINDOW, KV_W), lambda n: (n, OFF_V // KV_W))]


def _attn_fwd(proj, sinks, og, ycat, name):
    T = proj.shape[0]
    nb = T // WINDOW

    def body(q_ref, kp_ref, kc_ref, vp_ref, vc_ref, s_ref, og_ref, _, y_ref, o_ref):
        n = pl.program_id(0)
        kv = _kv_variants(jnp.concatenate([kp_ref[...], kc_ref[...]], axis=0))
        vv = _kv_variants(jnp.concatenate([vp_ref[...], vc_ref[...]], axis=0))
        valid = _attn_valid(n)
        sinks_v = s_ref[...]
        ssq = jnp.zeros((WINDOW, 1), F32)
        for p in range(N_PAIRS):
            j = p // PAIRS_PER_KV
            qp = q_ref[:, p * 128:(p + 1) * 128].astype(BF16)
            o_pair = jnp.zeros((WINDOW, 128), F32)
            for par in range(2):
                pn, _ = _attn_probs(qp, kv[j][par], valid, _sink(sinks_v, 2 * p + par))
                o_pair = o_pair + _dot_nn(pn.astype(BF16), vv[j][par])
            o_ref[:, p * 128:(p + 1) * 128] = o_pair
            ssq = ssq + jnp.sum(o_pair * o_pair, axis=1, keepdims=True)
        rn = lax.rsqrt(ssq * (1.0 / ATTN_W) + EPS)
        y_ref[...] = (o_ref[...] * rn * og_ref[...]).astype(BF16)

    return pl.pallas_call(
        body, grid=(nb,),
        in_specs=[pl.BlockSpec((WINDOW, ATTN_W), lambda n: (n, OFF_Q // ATTN_W)), *_kv_specs(),
                  pl.BlockSpec((1, N_HEADS), lambda n: (0, 0)), pl.BlockSpec((1, ATTN_W), lambda n: (0, 0)), ANY],
        out_specs=[pl.BlockSpec((WINDOW, ATTN_W), lambda n: (n, 1)), pl.BlockSpec((WINDOW, ATTN_W), lambda n: (n, 0))],
        out_shape=[jax.ShapeDtypeStruct(ycat.shape, BF16), jax.ShapeDtypeStruct((T, ATTN_W), F32)],
        input_output_aliases={7: 0}, name=name, compiler_params=_cparams(("parallel",)),
    )(proj, proj, proj, proj, proj, sinks, og, ycat)


def _attn_bwd(proj, sinks, og, o, dy, dproj, name):
    T = proj.shape[0]
    nb = T // WINDOW

    def body(q_ref, kp_ref, kc_ref, vp_ref, vc_ref, s_ref, og_ref, o_ref, dy_ref, _,
             dq_ref, dk_ref, dv_ref, ds_ref, dog_ref):
        n = pl.program_id(0)

        @pl.when(n == 0)
        def _():
            dk_ref[...] = jnp.zeros_like(dk_ref)
            dv_ref[...] = jnp.zeros_like(dv_ref)
            ds_ref[...] = jnp.zeros_like(ds_ref)
            dog_ref[...] = jnp.zeros_like(dog_ref)

        kv = _kv_variants(jnp.concatenate([kp_ref[...], kc_ref[...]], axis=0))
        vv = _kv_variants(jnp.concatenate([vp_ref[...], vc_ref[...]], axis=0))
        valid = _attn_valid(n)
        sinks_v = s_ref[...]
        of = o_ref[...]
        rn = lax.rsqrt(jnp.mean(of * of, axis=-1, keepdims=True) + EPS)
        oh = of * rn
        dyf = dy_ref[...]
        dog_ref[...] += jnp.sum(dyf * oh, axis=0, keepdims=True)
        doh = dyf * og_ref[...]
        do = rn * (doh - oh * jnp.mean(doh * oh, axis=-1, keepdims=True))
        lane = lax.broadcasted_iota(jnp.int32, (1, 128), 1)
        lane16 = lax.broadcasted_iota(jnp.int32, (1, N_HEADS), 1)
        dk_acc = [[jnp.zeros((2 * WINDOW, 128), F32) for _ in range(2)] for _ in range(2)]
        dv_acc = [[jnp.zeros((2 * WINDOW, 128), F32) for _ in range(2)] for _ in range(2)]
        dsink = jnp.zeros((1, N_HEADS), F32)
        for p in range(N_PAIRS):
            j = p // PAIRS_PER_KV
            qp = q_ref[:, p * 128:(p + 1) * 128].astype(BF16)
            do_p = do[:, p * 128:(p + 1) * 128]
            o_p = of[:, p * 128:(p + 1) * 128]
            do_b = do_p.astype(BF16)
            prod = do_p * o_p
            dq_pair = jnp.zeros((WINDOW, 128), F32)
            for par in range(2):
                r = 2 * p + par
                half = (lane < HEAD_DIM) if par == 0 else (lane >= HEAD_DIM)
                pn, ps = _attn_probs(qp, kv[j][par], valid, _sink(sinks_v, r))
                delta = jnp.sum(jnp.where(half, prod, 0.0), axis=1, keepdims=True)
                dP = _dot_nt(do_b, vv[j][par])
                dS = pn * (dP - delta)
                dsink = dsink + jnp.where(lane16 == r, -jnp.sum(ps * delta, axis=0, keepdims=True), 0.0)
                dSb = (dS * ATTN_SCALE).astype(BF16)
                dq_pair = dq_pair + _dot_nn(dSb, kv[j][par])
                dk_acc[j][par] = dk_acc[j][par] + _dot_tn(dSb, jnp.where(half, qp, jnp.zeros_like(qp)))
                dv_acc[j][par] = dv_acc[j][par] + _dot_tn(pn.astype(BF16), jnp.where(half, do_b, jnp.zeros_like(do_b)))
            dq_ref[:, p * 128:(p + 1) * 128] = dq_pair.astype(BF16)
        dkk = (dk_acc[0][0] + pltpu.roll(dk_acc[0][1], HEAD_DIM, axis=1)
               + dk_acc[1][1] + pltpu.roll(dk_acc[1][0], HEAD_DIM, axis=1))
        dvv = (dv_acc[0][0] + pltpu.roll(dv_acc[0][1], HEAD_DIM, axis=1)
               + dv_acc[1][1] + pltpu.roll(dv_acc[1][0], HEAD_DIM, axis=1))
        prev = pl.multiple_of(jnp.maximum(n - 1, 0) * WINDOW, WINDOW)
        own = pl.multiple_of(n * WINDOW, WINDOW)
        dk_ref[pl.ds(prev, WINDOW), :] += dkk[:WINDOW]
        dk_ref[pl.ds(own, WINDOW), :] += dkk[WINDOW:]
        dv_ref[pl.ds(prev, WINDOW), :] += dvv[:WINDOW]
        dv_ref[pl.ds(own, WINDOW), :] += dvv[WINDOW:]
        ds_ref[...] += dsink

    full_kv = pl.BlockSpec((T, KV_W), lambda n: (0, 0))
    blk = pl.BlockSpec((WINDOW, ATTN_W), lambda n: (n, 0))
    return pl.pallas_call(
        body, grid=(nb,),
        in_specs=[pl.BlockSpec((WINDOW, ATTN_W), lambda n: (n, OFF_Q // ATTN_W)), *_kv_specs(),
                  pl.BlockSpec((1, N_HEADS), lambda n: (0, 0)), pl.BlockSpec((1, ATTN_W), lambda n: (0, 0)),
                  blk, pl.BlockSpec((WINDOW, ATTN_W), lambda n: (n, 1)), ANY],
        out_specs=[pl.BlockSpec((WINDOW, ATTN_W), lambda n: (n, OFF_Q // ATTN_W)), full_kv, full_kv,
                   pl.BlockSpec((1, N_HEADS), lambda n: (0, 0)), pl.BlockSpec((1, ATTN_W), lambda n: (0, 0))],
        out_shape=[jax.ShapeDtypeStruct(dproj.shape, BF16), jax.ShapeDtypeStruct((T, KV_W), F32),
                   jax.ShapeDtypeStruct((T, KV_W), F32), jax.ShapeDtypeStruct((1, N_HEADS), F32),
                   jax.ShapeDtypeStruct((1, ATTN_W), F32)],
        input_output_aliases={9: 0}, name=name, compiler_params=_cparams(("arbitrary",)),
    )(proj, proj, proj, proj, proj, sinks, og, o, dy, dproj)


ANY = pl.BlockSpec(memory_space=pl.ANY)


def _coords():
    return lax.axis_index("x"), lax.axis_index("y"), lax.axis_index("c")


def _all_gather(arrs, name):
    n = len(arrs)

    def body(*refs):
        ins, outs = refs[:n], refs[n:2 * n]
        send_sems, recv_sems, local_sems = refs[2 * n:]
        x, y, c = _coords()
        me = 4 * x + 2 * y + c
        sibling = (x, y, 1 - c)
        chips = [(1 - x, y), (x, 1 - y), (1 - x, 1 - y)]

        def copy(a, k, block, to, src=None):
            dst = outs[a].at[block]
            return pltpu.make_async_remote_copy(
                src_ref=dst if src is None else src, dst_ref=dst, send_sem=send_sems.at[a, k],
                recv_sem=recv_sems.at[a, k], device_id=to, device_id_type=MESH)

        mine = [pltpu.make_async_copy(ins[a], outs[a].at[me], local_sems.at[a]) for a in range(n)]
        for cp in mine:
            cp.start()
        first = []
        for a in range(n):
            first.append(copy(a, 0, me, sibling, src=ins[a]))
            for j, chip in enumerate(chips):
                first.append(copy(a, 1 + j, me, (*chip, c), src=ins[a]))
        for cp in first:
            cp.start()
        passed = []
        for j, (px, py) in enumerate(chips):
            blk = 4 * px + 2 * py + c
            for a in range(n):
                copy(a, 1 + j, blk, sibling).wait_recv()
                fwd = copy(a, 4 + j, blk, sibling)
                fwd.start()
                passed.append(fwd)
        for a in range(n):
            copy(a, 0, 4 * x + 2 * y + (1 - c), sibling).wait_recv()
            for j, (px, py) in enumerate(chips):
                copy(a, 4 + j, 4 * px + 2 * py + (1 - c), sibling).wait_recv()
        for cp in first + passed:
            cp.wait_send()
        for cp in mine:
            cp.wait()

    return pl.pallas_call(
        body, in_specs=[ANY] * n, out_specs=[ANY] * n,
        out_shape=[jax.ShapeDtypeStruct((N_DEV,) + a.shape, a.dtype) for a in arrs],
        scratch_shapes=[pltpu.SemaphoreType.DMA((n, 7)), pltpu.SemaphoreType.DMA((n, 7)),
                        pltpu.SemaphoreType.DMA((n,))],
        name=name,
    )(*arrs)


def _exchange_pair(arrs, name):
    n = len(arrs)

    def body(*refs):
        ins, outs = refs[:n], refs[n:2 * n]
        send_sems, recv_sems = refs[2 * n:]
        x, y, c = _coords()
        cps = []
        for a in range(n):
            for q in range(4):
                cps.append(pltpu.make_async_remote_copy(
                    src_ref=ins[a].at[2 * q + (1 - c)], dst_ref=outs[a].at[q], send_sem=send_sems.at[a, q],
                    recv_sem=recv_sems.at[a, q], device_id=(x, y, 1 - c), device_id_type=MESH))
        for cp in cps:
            cp.start()
        for cp in cps:
            cp.wait()

    return pl.pallas_call(
        body, in_specs=[ANY] * n, out_specs=[ANY] * n,
        out_shape=[jax.ShapeDtypeStruct((4,) + a.shape[1:], a.dtype) for a in arrs],
        scratch_shapes=[pltpu.SemaphoreType.DMA((n, 4)), pltpu.SemaphoreType.DMA((n, 4))],
        name=name,
    )(*arrs)


def _exchange_chips(arrs, name):
    n = len(arrs)

    def body(*refs):
        ins, outs = refs[:n], refs[n:2 * n]
        send_sems, recv_sems = refs[2 * n:]
        x, y, c = _coords()
        chips = [(1 - x, y), (x, 1 - y), (1 - x, 1 - y)]
        cps = []
        for a in range(n):
            for k, (tx, ty) in enumerate(chips):
                cps.append(pltpu.make_async_remote_copy(
                    src_ref=ins[a].at[2 * tx + ty], dst_ref=outs[a].at[k], send_sem=send_sems.at[a, k],
                    recv_sem=recv_sems.at[a, k], device_id=(tx, ty, c), device_id_type=MESH))
        for cp in cps:
            cp.start()
        for cp in cps:
            cp.wait()

    return pl.pallas_call(
        body, in_specs=[ANY] * n, out_specs=[ANY] * n,
        out_shape=[jax.ShapeDtypeStruct((3,) + a.shape[1:], a.dtype) for a in arrs],
        scratch_shapes=[pltpu.SemaphoreType.DMA((n, 3)), pltpu.SemaphoreType.DMA((n, 3))],
        name=name,
    )(*arrs)


HBM = pl.BlockSpec(memory_space=pltpu.HBM)
SEM = pl.BlockSpec(memory_space=pltpu.SEMAPHORE)
EFFECT = pltpu.SideEffectType.DATAFLOW_SIDE_EFFECTING


def _in_hbm(a):
    return pltpu.with_memory_space_constraint(a, pltpu.HBM)


def _remote_start(srcs, lands, plan, n_copies, name, after=None):
    n = len(srcs)
    n_after = 0 if after is None else 1

    def body(*refs):
        src_refs, land_refs = refs[:n], refs[n:2 * n]
        send_sems, recv_sems = refs[2 * n + n_after], refs[2 * n + n_after + 1]
        token = refs[-1]
        x, y, c = _coords()
        for i, (sv, dv, dev) in enumerate(plan(src_refs, land_refs, x, y, c)):
            pltpu.make_async_remote_copy(src_ref=sv, dst_ref=dv, send_sem=send_sems.at[i], recv_sem=recv_sems.at[i],
                                         device_id=dev, device_id_type=MESH).start()
        token[...] = jnp.zeros_like(token)

    bufs = list(srcs) + list(lands)
    outs = pl.pallas_call(
        body, name=name,
        out_shape=(pltpu.SemaphoreType.DMA((n_copies,)), pltpu.SemaphoreType.DMA((n_copies,)),
                   *[pltpu.HBM(b.shape, b.dtype) for b in bufs], jax.ShapeDtypeStruct((8, 128), F32)),
        in_specs=[HBM] * (2 * n) + [ANY] * n_after,
        out_specs=(SEM, SEM, *[HBM] * (2 * n), pl.BlockSpec(memory_space=pltpu.VMEM)),
        input_output_aliases={i: 2 + i for i in range(2 * n)},
        compiler_params=pltpu.CompilerParams(has_side_effects=EFFECT),
    )(*[_in_hbm(b) for b in bufs], *([] if after is None else [after]))
    return outs[0], outs[1], list(outs[2:2 + n]), list(outs[2 + n:2 + 2 * n]), outs[-1]


def _remote_wait(started, after, plan, name):
    send_sems, recv_sems, srcs, lands, _ = started
    n = len(srcs)

    def body(*refs):
        src_refs, land_refs = refs[:n], refs[n:2 * n]
        send_sems, recv_sems = refs[2 * n], refs[2 * n + 1]
        x, y, c = _coords()
        for i, (sv, dv, dev) in enumerate(plan(src_refs, land_refs, x, y, c)):
            cp = pltpu.make_async_remote_copy(src_ref=sv, dst_ref=dv, send_sem=send_sems.at[i],
                                              recv_sem=recv_sems.at[i], device_id=dev, device_id_type=MESH)
            cp.wait_send()
            cp.wait_recv()

    bufs = list(srcs) + list(lands)
    outs = pl.pallas_call(
        body, name=name, out_shape=tuple(pltpu.HBM(b.shape, b.dtype) for b in bufs),
        in_specs=[HBM] * (2 * n) + [SEM, SEM, ANY], out_specs=tuple([HBM] * (2 * n)),
        input_output_aliases={i: i for i in range(2 * n)},
        compiler_params=pltpu.CompilerParams(has_side_effects=EFFECT),
    )(*bufs, send_sems, recv_sems, after)
    return list(outs[:n]), list(outs[n:])


def _gather_plan(src_refs, land_refs, x, y, c):
    me = 4 * x + 2 * y + c
    plan = []
    for s, l in zip(src_refs, land_refs):
        for dev in [(x, y, 1 - c), (1 - x, y, c), (x, 1 - y, c), (1 - x, 1 - y, c)]:
            plan.append((s, l.at[me], dev))
    return plan


def _pair_plan(src_refs, land_refs, x, y, c):
    plan = []
    for s, l in zip(src_refs, land_refs):
        for q in range(4):
            plan.append((s.at[2 * q + (1 - c)], l.at[q], (x, y, 1 - c)))
    return plan


def _pair4_plan(src_refs, land_refs, x, y, c):
    plan = []
    for s, l in zip(src_refs, land_refs):
        for q in range(4):
            plan.append((s.at[q], l.at[q], (x, y, 1 - c)))
    return plan


def _chips_plan(src_refs, land_refs, x, y, c):
    plan = []
    for s, l in zip(src_refs, land_refs):
        for k, (tx, ty) in enumerate([(1 - x, y), (x, 1 - y), (1 - x, 1 - y)]):
            plan.append((s.at[2 * tx + ty], l.at[k], (tx, ty, c)))
    return plan


def _everyone_plan(src_refs, land_refs, x, y, c):
    me = 4 * x + 2 * y + c
    plan = []
    for s, l in zip(src_refs, land_refs):
        for fx, fy, fc in [(0, 0, 1), (1, 0, 0), (1, 0, 1), (0, 1, 0), (0, 1, 1), (1, 1, 0), (1, 1, 1)]:
            dev = ((1 - x) if fx else x, (1 - y) if fy else y, (1 - c) if fc else c)
            plan.append((s, l.at[me], dev))
    return plan


def _gather_finish(gathered, name):
    n = len(gathered)

    def body(*refs):
        outs = refs[n:2 * n]
        send_sems, recv_sems = refs[2 * n:]
        x, y, c = _coords()
        cps = []
        for a in range(n):
            for j, (px, py) in enumerate([(1 - x, y), (x, 1 - y), (1 - x, 1 - y)]):
                blk = outs[a].at[4 * px + 2 * py + c]
                got = outs[a].at[4 * px + 2 * py + (1 - c)]
                cps.append((pltpu.make_async_remote_copy(
                    src_ref=blk, dst_ref=blk, send_sem=send_sems.at[a, j], recv_sem=recv_sems.at[a, j],
                    device_id=(x, y, 1 - c), device_id_type=MESH), pltpu.make_async_remote_copy(
                    src_ref=got, dst_ref=got, send_sem=send_sems.at[a, j], recv_sem=recv_sems.at[a, j],
                    device_id=(x, y, 1 - c), device_id_type=MESH)))
        for cp, _ in cps:
            cp.start()
        for cp, arrival in cps:
            cp.wait_send()
            arrival.wait_recv()

    return pl.pallas_call(
        body, in_specs=[ANY] * n, out_specs=[ANY] * n,
        out_shape=[jax.ShapeDtypeStruct(g.shape, g.dtype) for g in gathered],
        input_output_aliases={a: a for a in range(n)},
        scratch_shapes=[pltpu.SemaphoreType.DMA((n, 3)), pltpu.SemaphoreType.DMA((n, 3))],
        name=name,
    )(*gathered)


def _pair_add(g8, r1, csel, tr, name):
    _, R, C = r1.shape
    g4 = g8.reshape(4, 2, R, C)

    def body(c_ref, g_ref, r_ref, o_ref):
        o_ref[...] = (g_ref[...].astype(F32) + r_ref[...].astype(F32)).astype(BF16)

    return pl.pallas_call(
        body,
        grid_spec=pltpu.PrefetchScalarGridSpec(
            num_scalar_prefetch=1, grid=(4, R // tr),
            in_specs=[pl.BlockSpec((None, None, tr, C), lambda q, i, cs: (q, cs[0], i, 0)),
                      pl.BlockSpec((None, tr, C), lambda q, i, cs: (q, i, 0))],
            out_specs=pl.BlockSpec((None, tr, C), lambda q, i, cs: (q, i, 0))),
        out_shape=jax.ShapeDtypeStruct((4, R, C), BF16), name=name,
        compiler_params=_cparams(("parallel", "parallel")),
    )(csel, g4, r1)


def _adamw_math(w, g, m, v):
    m = ADAM_B1 * m + (1.0 - ADAM_B1) * g
    v = ADAM_B2 * v + (1.0 - ADAM_B2) * (g * g)
    m_hat = m / (1.0 - ADAM_B1 ** ADAM_STEP)
    v_hat = v / (1.0 - ADAM_B2 ** ADAM_STEP)
    delta = -ADAM_LR * (m_hat / (jnp.sqrt(v_hat) + ADAM_EPS) + ADAM_WD * w)
    return delta, m, v


def _adamw_big(w, m, v, p4, r3, qsel, tile, name):
    R, C = w.shape
    tr, tc = tile

    def body(q_ref, w_ref, m_ref, v_ref, p_ref, r_ref, g_out, d_out, m_out, v_out):
        g = p_ref[...].astype(F32) + r_ref[0].astype(F32) + r_ref[1].astype(F32) + r_ref[2].astype(F32)
        d, mn, vn = _adamw_math(w_ref[...], g, m_ref[...], v_ref[...])
        g_out[...] = g
        d_out[...] = d
        m_out[...] = mn
        v_out[...] = vn

    blk = pl.BlockSpec((tr, tc), lambda i, j, qs: (i, j))
    return pl.pallas_call(
        body,
        grid_spec=pltpu.PrefetchScalarGridSpec(
            num_scalar_prefetch=1, grid=(R // tr, C // tc),
            in_specs=[blk, blk, blk, pl.BlockSpec((None, tr, tc), lambda i, j, qs: (qs[0], i, j)),
                      pl.BlockSpec((3, tr, tc), lambda i, j, qs: (0, i, j))],
            out_specs=[blk, blk, blk, blk]),
        out_shape=[jax.ShapeDtypeStruct((R, C), F32)] * 4, name=name,
        compiler_params=_cparams(("parallel", "parallel")),
    )(qsel, w, m, v, p4, r3)


def _sum_partials(p4, r3, qsel, tc, name):
    _, R, C = p4.shape

    def body(q_ref, p_ref, r_ref, o_ref):
        o_ref[...] = p_ref[...].astype(F32) + r_ref[0].astype(F32) + r_ref[1].astype(F32) + r_ref[2].astype(F32)

    return pl.pallas_call(
        body,
        grid_spec=pltpu.PrefetchScalarGridSpec(
            num_scalar_prefetch=1, grid=(C // tc,),
            in_specs=[pl.BlockSpec((None, R, tc), lambda j, qs: (qs[0], 0, j)),
                      pl.BlockSpec((3, R, tc), lambda j, qs: (0, 0, j))],
            out_specs=pl.BlockSpec((R, tc), lambda j, qs: (0, j))),
        out_shape=jax.ShapeDtypeStruct((R, C), F32), name=name, compiler_params=_cparams(("parallel",)),
    )(qsel, p4, r3)


def _adamw_rows(w, m, v, g, tr, name):
    R, C = w.shape

    def body(w_ref, m_ref, v_ref, g_ref, g_out, d_out, m_out, v_out):
        g = g_ref[...]
        d, mn, vn = _adamw_math(w_ref[...], g, m_ref[...], v_ref[...])
        g_out[...] = g
        d_out[...] = d
        m_out[...] = mn
        v_out[...] = vn

    blk = pl.BlockSpec((tr, C), lambda i: (i, 0))
    return pl.pallas_call(
        body, grid=(R // tr,), in_specs=[blk] * 4, out_specs=[blk] * 4,
        out_shape=[jax.ShapeDtypeStruct((R, C), F32)] * 4, name=name, compiler_params=_cparams(("parallel",)),
    )(w, m, v, g)


def _small_sum(parts, name):
    def body(p_ref, o_ref):
        acc = p_ref[0]
        for d in range(1, N_DEV):
            acc = acc + p_ref[d]
        o_ref[...] = acc

    return pl.pallas_call(
        body, out_shape=jax.ShapeDtypeStruct(parts.shape[1:], F32), name=name,
        compiler_params=_cparams(),
    )(parts)


def _adamw_small(w, g, m, v, name):
    def body(w_ref, g_ref, m_ref, v_ref, d_out, m_out, v_out):
        d, mn, vn = _adamw_math(w_ref[...], g_ref[...], m_ref[...], v_ref[...])
        d_out[...] = d
        m_out[...] = mn
        v_out[...] = vn

    return pl.pallas_call(
        body, out_shape=[jax.ShapeDtypeStruct(w.shape, F32)] * 3, name=name, compiler_params=_cparams(),
    )(w, g, m, v)


def _row(*pieces):
    r = jnp.concatenate([p.reshape(1, -1) for p in pieces], axis=1)
    return jnp.pad(r, ((0, 0), (0, D_MODEL - r.shape[1])))


def _pack_small(mix, convb, ssmg, attng, mlpg, fing, convw, dtb, alog, dsk, sinks, extra=None):
    last = [dtb, alog, dsk, sinks] + ([extra] if extra is not None else [])
    rows = [_row(mix), _row(convb), _row(ssmg, attng), _row(mlpg), _row(fing),
            jnp.pad(convw, ((0, 0), (0, D_MODEL - convw.shape[1]))), _row(*last)]
    packed = jnp.concatenate(rows, axis=0)
    return jnp.pad(packed, ((0, SMALL_ROWS - packed.shape[0]), (0, 0)))


def _unpack_small(p, conv_n):
    return dict(
        mix_norm_g=p[0:1, :], conv_b=p[1:2, :], ssm_norm_g=p[2:3, :D_INNER], attn_out_norm_g=p[2:3, D_INNER:],
        mlp_norm_g=p[3:4, :], final_norm_g=p[4, :], conv_w=p[5:9, :conv_n][None],
        dt_bias=p[9:10, 0:16], A_log=p[9:10, 16:32], D_skip=p[9:10, 32:48], attn_sinks=p[9:10, 48:64])


SMALL_NAMES = ["mix_norm_g", "conv_w", "conv_b", "dt_bias", "A_log", "D_skip", "ssm_norm_g", "attn_sinks",
               "attn_out_norm_g", "mlp_norm_g", "final_norm_g"]
WEIGHT_ORDER = ["mix_norm_g", "w_in", "conv_w", "conv_b", "dt_bias", "A_log", "D_skip", "ssm_norm_g", "attn_sinks",
                "attn_out_norm_g", "w_out", "mlp_norm_g", "w_up", "w_down", "final_norm_g"]


def _to_my_columns(w_nat):
    pad = jnp.zeros((w_nat.shape[0], NP - IN_PROJ), w_nat.dtype)
    return jnp.concatenate([w_nat[:, :NAT_DT], w_nat[:, NAT_DT + N_HEADS:], w_nat[:, NAT_DT:NAT_DT + N_HEADS], pad],
                           axis=1)


PER = IN_PROJ // N_DEV
SUPER_STEP = 544
SUPER = 576


def _natural_rows(g, lo, hi):
    segments = [(0, NAT_DT, 0), (NAT_DT, NAT_DT + N_HEADS, OFF_DT - NAT_DT), (NAT_DT + N_HEADS, IN_PROJ, -N_HEADS),
                (IN_PROJ, NP, 0)]
    pieces = [g[max(lo, a) + shift:min(hi, b) + shift] for a, b, shift in segments if max(lo, a) < min(hi, b)]
    return pieces[0] if len(pieces) == 1 else jnp.concatenate(pieces, axis=0)


def _w_in_from_super_slabs(sup):
    seam = SUPER - SUPER_STEP
    units = []
    for i in range(N_DEV):
        base = SUPER_STEP * i
        units.append((base, base + seam, sup[i, :seam] if i == 0 else sup[i - 1, SUPER_STEP:] + sup[i, :seam]))
        units.append((base + seam, base + SUPER_STEP, sup[i, seam:SUPER_STEP]))
    units.append((SUPER_STEP * N_DEV, SUPER_STEP * N_DEV + seam, sup[N_DEV - 1, SUPER_STEP:]))

    def natural(lo, hi):
        return [rows[max(lo, a) - a:min(hi, b) - a] for a, b, rows in units if max(lo, a) < min(hi, b)]

    pieces = natural(0, NAT_DT) + natural(NAT_DT + N_HEADS, IN_PROJ) + natural(NAT_DT, NAT_DT + N_HEADS)
    return jnp.concatenate(pieces + [jnp.zeros((NP - IN_PROJ, D_MODEL), sup.dtype)], axis=0)


def _to_natural_columns(w_my):
    return jnp.concatenate([w_my[:, :NAT_DT], w_my[:, OFF_DT:OFF_DT + N_HEADS], w_my[:, NAT_DT:OFF_DT]], axis=1)


SLAB = 1024


def _grad_w_up(h2, du, name, sel=None, add=None, after=None):
    T, D = h2.shape
    if sel is None:
        pick, n_slab, pre = (lambda j, *cs: j), N_DEV, None
    else:
        pre, other = sel
        pick, n_slab = (lambda j, cs: 2 * j + ((1 - cs[0]) if other else cs[0])), 4
    o_spec = pl.BlockSpec((None, SLAB, SLAB), lambda i, j, k, *cs: (j, i, 0))
    return _matmul(
        h2, du, mode="tn", grid=(D // SLAB, n_slab, 1),
        a_spec=pl.BlockSpec((T, SLAB), lambda i, j, k, *cs: (0, i)),
        b_spec=pl.BlockSpec((T, SLAB), lambda i, j, k, *cs: (0, pick(j, *cs))),
        out_shapes=[jax.ShapeDtypeStruct((n_slab, D, SLAB), BF16)], out_specs=[o_spec], tile=(SLAB, SLAB), name=name,
        extras=() if add is None else (add,), extra_specs=() if add is None else (o_spec,),
        epilogue=None if add is None else (lambda acc, r: (acc + r.astype(F32),)), after=after, prefetch=pre)[0]


def _grad_w_down(act, dx3b, name, sel=None, add=None, after=None):
    T, D = dx3b.shape
    if sel is None:
        pick, n_slab, pre = (lambda i, *cs: i), N_DEV, None
    else:
        pre, other = sel
        pick, n_slab = (lambda i, cs: 2 * i + ((1 - cs[0]) if other else cs[0])), 4
    o_spec = pl.BlockSpec((None, SLAB, SLAB), lambda i, j, k, *cs: (i, 0, j))
    return _matmul(
        act, dx3b, mode="tn", grid=(n_slab, D // SLAB, 1),
        a_spec=pl.BlockSpec((T, SLAB), lambda i, j, k, *cs: (0, pick(i, *cs))),
        b_spec=pl.BlockSpec((T, SLAB), lambda i, j, k, *cs: (0, j)),
        out_shapes=[jax.ShapeDtypeStruct((n_slab, SLAB, D), BF16)], out_specs=[o_spec], tile=(SLAB, SLAB), name=name,
        extras=() if add is None else (add,), extra_specs=() if add is None else (o_spec,),
        epilogue=None if add is None else (lambda acc, r: (acc + r.astype(F32),)), after=after, prefetch=pre)[0]


class _FixedWeights:
    def __init__(self, w_in_p, w_out_f, w_up_s, w_down_f, conv_w_f):
        self.w = (w_in_p, w_out_f, w_up_s, w_down_f, conv_w_f)
        self.grads = {}

    def mixer_weights(self, after):
        return self.w[0], self.w[4], None

    def out_weight(self, after):
        return self.w[1]

    def up_weight(self, after):
        return self.w[2]

    def down_weight(self, after):
        return self.w[3]

    def mlp_grads(self, h2, du, act, dx3b):
        self.grads.update(w_up=_grad_w_up(h2, du, "grad_w_up"),
                          w_down=_grad_w_down(act, dx3b, "grad_w_down").reshape(D_FF, D_MODEL))
        return None

    def out_grad(self, g_out):
        self.grads.update(w_out=g_out)
        return None

    def in_grad(self, g_in):
        self.grads.update(w_in=g_in)
        return None


def _local_step(x, tgt, p, hooks):
    T = x.shape[0]
    D = D_MODEL
    h1 = _rmsnorm_fwd(x, p["mix_norm_g"], "norm_mix")
    w_in_t, conv_w_f, token = hooks.mixer_weights(h1)
    (proj,) = _mm_simple(h1, w_in_t, mode="nt", M=T, N=NP, K=D, tm=min(T, 1024), tn=1536, tk=D, out_dtype=F32,
                         name="in_proj", after=token)
    xbc = _conv_fwd(proj, conv_w_f, p["conv_b"], "conv_fwd")
    dtT = proj[:, OFF_DT:OFF_DT + N_HEADS].T
    dtbT = p["dt_bias"].T
    alogT = p["A_log"].T
    dfull = jnp.repeat(p["D_skip"], HEAD_DIM, axis=1)
    ycat, ypre, hs = _ssd_fwd(xbc, proj, dtT, p["dt_bias"], dtbT, p["A_log"], alogT, dfull, p["ssm_norm_g"],
                              "ssd_fwd")
    ycat, o_att = _attn_fwd(proj, p["attn_sinks"], p["attn_out_norm_g"], ycat, "attn_fwd")
    w_out_f = hooks.out_weight(ycat)
    tm = min(T, 1024)
    (x2,) = _mm_simple(ycat, w_out_f, mode="nn", M=T, N=D, K=D, tm=tm, tn=1024, tk=D, out_dtype=F32, name="out_proj",
                       extras=(x,), epilogue=lambda acc, res: (acc + res,))
    h2 = _rmsnorm_fwd(x2, p["mlp_norm_g"], "norm_mlp")
    w_up_s = hooks.up_weight(h2)
    grid = (T // tm, N_DEV, 1)
    u, act = _matmul(
        h2, w_up_s, mode="nn", grid=grid,
        a_spec=pl.BlockSpec((tm, D), lambda i, j, k: (i, 0)),
        b_spec=pl.BlockSpec((None, D, 1024), lambda i, j, k: (j, 0, 0)),
        out_shapes=[jax.ShapeDtypeStruct((T, D_FF), F32), jax.ShapeDtypeStruct((T, D_FF), BF16)],
        out_specs=[pl.BlockSpec((tm, 1024), lambda i, j, k: (i, j))] * 2, tile=(tm, 1024), name="mlp_up",
        epilogue=lambda acc: (acc, jnp.square(jnp.maximum(acc, 0.0))))
    w_down_f = hooks.down_weight(act)
    (x3,) = _mm_simple(act, w_down_f, mode="nn", M=T, N=D, K=D_FF, tm=tm, tn=1024, tk=2048, out_dtype=F32,
                       name="mlp_down", extras=(x2,), epilogue=lambda acc, res: (acc + res,))
    loss_part, d_fin, dx3, dx3b = _final_loss(x3, tgt, p["final_norm_g"].reshape(1, D), "loss_head")
    (du,) = _mm_simple(dx3b, w_down_f, mode="nt", M=T, N=D_FF, K=D, tm=tm, tn=1024, tk=D, out_dtype=BF16,
                       name="mlp_down_bwd", extras=(u,),
                       epilogue=lambda acc, uu: (acc * (2.0 * jnp.maximum(uu, 0.0)),))
    token = hooks.mlp_grads(h2, du, act, dx3b)
    (dh2,) = _matmul(
        du, w_up_s, mode="nt", grid=(T // tm, D // 1024, N_DEV // 2),
        a_spec=pl.BlockSpec((tm, 2048), lambda i, j, k: (i, k)),
        b_spec=pl.BlockSpec((2, 1024, 1024), lambda i, j, k: (k, j, 0)),
        out_shapes=[jax.ShapeDtypeStruct((T, D), F32)],
        out_specs=[pl.BlockSpec((tm, 1024), lambda i, j, k: (i, j))], tile=(tm, 1024), name="mlp_up_bwd",
        after=token, dot_fn=lambda a, b: _dot_nt(a[:, :1024], b[0]) + _dot_nt(a[:, 1024:], b[1]))
    dx2, dx2b, d_mlp = _rmsnorm_bwd(dh2, x2, p["mlp_norm_g"], dx3, "norm_mlp_bwd")
    (g_out,) = _mm_simple(ycat, dx2b, mode="tn", M=D, N=D, K=T, tm=1024, tn=1024, tk=T, out_dtype=BF16,
                          name="grad_w_out")
    token = hooks.out_grad(g_out)
    (dy,) = _mm_simple(dx2b, w_out_f, mode="nt", M=T, N=D, K=D, tm=tm, tn=1024, tk=D, out_dtype=F32,
                       name="out_proj_bwd", after=token)
    dproj, dxbc_act, d_dtb, d_alog, d_dskip, d_ssmg = _ssd_bwd(
        xbc, proj, dtT, p["dt_bias"], dtbT, p["A_log"], alogT, dfull, p["ssm_norm_g"], ypre, hs, dy, "ssd_bwd")
    dproj, d_convw, d_convb = _conv_bwd(proj, dxbc_act, conv_w_f, p["conv_b"], dproj, "conv_bwd")
    dproj, dk, dv, d_sinks, d_attng = _attn_bwd(proj, p["attn_sinks"], p["attn_out_norm_g"], o_att, dy, dproj,
                                                "attn_bwd")
    dproj = lax.dynamic_update_slice(dproj, jnp.concatenate([dk, dv], axis=1).astype(BF16), (0, OFF_K))
    (g_in,) = _mm_simple(dproj, h1, mode="tn", M=NP, N=D, K=T, tm=1536, tn=1024, tk=T, out_dtype=BF16,
                         name="grad_w_in")
    token = hooks.in_grad(g_in)
    (dh1,) = _mm_simple(dproj, w_in_t, mode="nn", M=T, N=D, K=NP, tm=tm, tn=1024, tk=2304, out_dtype=F32,
                        name="in_proj_bwd", after=token)
    dx, _, d_mix = _rmsnorm_bwd(dh1, x, p["mix_norm_g"], dx2, "norm_mix_bwd")
    small = _pack_small(d_mix, d_convb, d_ssmg, d_attng, d_mlp, d_fin, d_convw, d_dtb, d_alog, d_dskip, d_sinks,
                        extra=loss_part[:, 0:1])
    return dx, small


def _landing(own, me):
    zone = lax.empty((N_DEV,) + own.shape, own.dtype)
    return lax.dynamic_update_slice(zone, own[None], (me,) + (0,) * own.ndim)


def _gather_end(started, after, plan, name):
    _, lands = _remote_wait(started, after, plan, name + "_wait")
    return _gather_finish(lands, name + "_finish")


class _ShardedWeights:
    def __init__(self, w_in, w_out, conv_w, w_up, w_down, me, csel):
        self.me, self.csel = me, csel
        own_rows = jnp.transpose(w_in).astype(BF16)
        shards = [lax.dynamic_update_slice(jnp.zeros((SUPER, D_MODEL), BF16), own_rows, (2 * me, 0)), conv_w]
        self.st_mixer = _remote_start(shards, [_landing(s, me) for s in shards], _gather_plan, 4 * len(shards),
                                      "gather_start_mixer")
        order = self.st_mixer[4]
        self.st_later = {}
        for k, wt in [("out", w_out), ("up", w_up), ("down", w_down)]:
            shard = (wt + order[0, 0]).astype(BF16)
            self.st_later[k] = _remote_start([shard], [_landing(shard, me)], _gather_plan, 4, f"gather_start_{k}",
                                             after=order)
            order = self.st_later[k][4]
        self.start_token = order
        self.reduces = {}

    def mixer_weights(self, after):
        g_in, g_conv = _gather_end(self.st_mixer, after, _gather_plan, "gather_mixer")
        conv_w_f = jnp.concatenate([g_conv[i] for i in range(N_DEV)], axis=1)
        return _w_in_from_super_slabs(g_in), conv_w_f, None

    def out_weight(self, after):
        return _gather_end(self.st_later["out"], after, _gather_plan, "gather_out")[0].reshape(D_MODEL, D_MODEL)

    def up_weight(self, after):
        return _gather_end(self.st_later["up"], after, _gather_plan, "gather_up")[0]

    def down_weight(self, after):
        return _gather_end(self.st_later["down"], after, _gather_plan, "gather_down")[0].reshape(D_FF, D_MODEL)

    def _chips_start(self, slabs, from_sibling, rows, tag):
        sums = [_pair_add(s, r, self.csel, tr, f"pair_add_{tag}_{i}")
                for i, (s, r, tr) in enumerate(zip(slabs, from_sibling, rows))]
        lands = [lax.empty((3,) + s.shape[1:], s.dtype) for s in sums]
        self.reduces[tag] = _remote_start(sums, lands, _chips_plan, 3 * len(sums), f"reduce_start_{tag}")
        return self.reduces[tag][4]

    def mlp_grads(self, h2, du, act, dx3b):
        def send(part, tag, after):
            st = _remote_start([part], [lax.empty(part.shape, part.dtype)], _pair4_plan, 4,
                               f"reduce_pair_start_{tag}", after=after)
            return st

        def received(st, after, tag):
            return _remote_wait(st, after, _pair4_plan, f"reduce_pair_wait_{tag}")[1][0]

        def to_chips(sums, tag):
            self.reduces[tag] = _remote_start([sums], [lax.empty((3,) + sums.shape[1:], sums.dtype)], _chips_plan, 3,
                                              f"reduce_start_{tag}")
            return self.reduces[tag][4]

        up_send = _grad_w_up(h2, du, "grad_w_up_send", sel=(self.csel, True))
        st_up = send(up_send, "up", None)
        down_send = _grad_w_down(act, dx3b, "grad_w_down_send", sel=(self.csel, True), after=st_up[4])
        st_down = send(down_send, "down", None)
        up_sum = _grad_w_up(h2, du, "grad_w_up_keep", sel=(self.csel, False), add=received(st_up, down_send, "up"),
                            after=st_down[4])
        token = to_chips(up_sum, "up")
        down_sum = _grad_w_down(act, dx3b, "grad_w_down_keep", sel=(self.csel, False),
                                add=received(st_down, up_sum, "down"), after=token)
        return to_chips(down_sum, "down")

    def out_grad(self, g_out):
        slabs = [g_out.reshape(N_DEV, D_MODEL // N_DEV, D_MODEL)]
        return self._chips_start(slabs, _exchange_pair(slabs, "reduce_pair_out"), [256], "out")

    def in_grad(self, g_in):
        slabs = [jnp.stack([_natural_rows(g_in, SUPER_STEP * j, SUPER_STEP * j + SUPER) for j in range(N_DEV)])]
        return self._chips_start(slabs, _exchange_pair(slabs, "reduce_pair_in"), [SUPER], "in")

    def small_start(self, small):
        self.st_small = _remote_start([small], [_landing(small, self.me)], _everyone_plan, N_DEV - 1, "gather_start_small")

    def small_end(self, after):
        return _remote_wait(self.st_small, after, _everyone_plan, "gather_small_wait")[1][0]

    def reduce_end(self, tag, after):
        return _remote_wait(self.reduces[tag], after, _chips_plan, f"reduce_wait_{tag}")


def kernel(x, mix_norm_g, w_in, conv_w, conv_b, dt_bias, A_log, D_skip, ssm_norm_g, attn_sinks, attn_out_norm_g, w_out, mlp_norm_g, w_up, w_down, final_norm_g, loss_target, m_mix_norm_g, m_w_in, m_conv_w, m_conv_b, m_dt_bias, m_A_log, m_D_skip, m_ssm_norm_g, m_attn_sinks, m_attn_out_norm_g, m_w_out, m_mlp_norm_g, m_w_up, m_w_down, m_final_norm_g, v_mix_norm_g, v_w_in, v_conv_w, v_conv_b, v_dt_bias, v_A_log, v_D_skip, v_ssm_norm_g, v_attn_sinks, v_attn_out_norm_g, v_w_out, v_mlp_norm_g, v_w_up, v_w_down, v_final_norm_g):
    xi, yi, ci = _coords()
    me = 4 * xi + 2 * yi + ci
    csel = jnp.reshape(ci, (1,)).astype(jnp.int32)
    qsel = jnp.reshape(2 * xi + yi, (1,)).astype(jnp.int32)
    w = dict(mix_norm_g=mix_norm_g, conv_b=conv_b, dt_bias=dt_bias, A_log=A_log, D_skip=D_skip,
             ssm_norm_g=ssm_norm_g, attn_sinks=attn_sinks, attn_out_norm_g=attn_out_norm_g, mlp_norm_g=mlp_norm_g,
             final_norm_g=final_norm_g)
    hooks = _ShardedWeights(w_in[0], w_out[0], conv_w[0], w_up[0], w_down[0], me, csel)
    p = dict(w, mix_norm_g=mix_norm_g + hooks.start_token[0:1, 0:1])
    dx, small = _local_step(x[0], loss_target[0], p, hooks)
    hooks.small_start(small)
    big = {}
    after = dx
    for name, wt, mt, vt, tile in [
            ("up", w_up, m_w_up, v_w_up, (512, SLAB)), ("down", w_down, m_w_down, v_w_down, (256, D_MODEL)),
            ("out", w_out, m_w_out, v_w_out, (256, D_MODEL))]:
        (chip_sums,), (from_chips,) = hooks.reduce_end(name, after)
        res = _adamw_big(wt[0], mt[0], vt[0], chip_sums, from_chips, qsel, tile, f"adamw_w_{name}")
        big["w_" + name] = tuple(r[None] for r in res)
        after = res[0]
    (chip_sums,), (from_chips,) = hooks.reduce_end("in", after)
    g_super = _sum_partials(chip_sums, from_chips, qsel, 512, "grad_w_in_sum")
    flat = (PER * D_MODEL // 128, 128)
    g_in = lax.dynamic_slice(g_super, (2 * me, 0), (PER, D_MODEL)).reshape(flat)
    res = _adamw_rows(*[jnp.transpose(t[0]).reshape(flat) for t in (w_in, m_w_in, v_w_in)], g_in, 672, "adamw_w_in")
    big["w_in"] = tuple(jnp.transpose(r.reshape(PER, D_MODEL))[None] for r in res)
    after = res[0]
    gsum = _small_sum(hooks.small_end(after), "small_sum")
    loss = gsum[9, 64]
    gs = _unpack_small(gsum, CONV_DIM)
    cw = CONV_DIM // N_DEV
    g_conv_shard = lax.dynamic_slice(gsum[5:9, :], (0, me * cw), (CONV_K, cw))

    def pack(s):
        return _pack_small(s["mix_norm_g"], s["conv_b"], s["ssm_norm_g"], s["attn_out_norm_g"], s["mlp_norm_g"],
                           s["final_norm_g"], s["conv_w"][0], s["dt_bias"], s["A_log"], s["D_skip"], s["attn_sinks"])

    wp = pack(dict(w, conv_w=conv_w))
    mp = pack(dict(mix_norm_g=m_mix_norm_g, conv_b=m_conv_b, ssm_norm_g=m_ssm_norm_g,
                   attn_out_norm_g=m_attn_out_norm_g, mlp_norm_g=m_mlp_norm_g, final_norm_g=m_final_norm_g,
                   conv_w=m_conv_w, dt_bias=m_dt_bias, A_log=m_A_log, D_skip=m_D_skip, attn_sinks=m_attn_sinks))
    vp = pack(dict(mix_norm_g=v_mix_norm_g, conv_b=v_conv_b, ssm_norm_g=v_ssm_norm_g,
                   attn_out_norm_g=v_attn_out_norm_g, mlp_norm_g=v_mlp_norm_g, final_norm_g=v_final_norm_g,
                   conv_w=v_conv_w, dt_bias=v_dt_bias, A_log=v_A_log, D_skip=v_D_skip, attn_sinks=v_attn_sinks))
    gp = jnp.concatenate([gsum[0:5], jnp.pad(g_conv_shard, ((0, 0), (0, D_MODEL - cw))), gsum[9:10],
                          jnp.zeros((SMALL_ROWS - 10, D_MODEL), F32)], axis=0)
    dp, mnp, vnp = _adamw_small(wp, gp, mp, vp, "adamw_small")
    grads = dict(gs, conv_w=g_conv_shard[None])
    deltas = _unpack_small(dp, cw)
    new_m = _unpack_small(mnp, cw)
    new_v = _unpack_small(vnp, cw)
    for k, name in enumerate(["w_in", "w_out", "w_up", "w_down"]):
        grads[name], deltas[name], new_m[name], new_v[name] = big[name]
    return (loss, dx[None], *[grads[n] for n in WEIGHT_ORDER], *[deltas[n] for n in WEIGHT_ORDER],
            *[new_m[n] for n in WEIGHT_ORDER], *[new_v[n] for n in WEIGHT_ORDER])
```

```python
import functools

import jax
import jax.numpy as jnp
from jax import lax
from jax.experimental import pallas as pl
from jax.experimental.pallas import tpu as pltpu

F32 = jnp.float32
BF16 = jnp.bfloat16
HI = lax.Precision.HIGHEST
MESH = pl.DeviceIdType.MESH

EPS = 1e-5
D_MODEL = 2048
D_INNER = 1024
N_HEADS = 16
HEAD_DIM = 64
N_GROUPS = 4
D_STATE = 128
CHUNK = 128
CONV_K = 4
CONV_DIM = 2048
ATTN_W = 1024
KV_W = 128
WINDOW = 128
D_FF = 8192
IN_PROJ = 4368
N_DEV = 8
NP = 4608
OFF_Z, OFF_X, OFF_B, OFF_C, OFF_Q, OFF_K, OFF_V, OFF_DT = 0, 1024, 2048, 2560, 3072, 4096, 4224, 4352
NAT_DT = 3072

ADAM_LR = 0.001
ADAM_B1 = 0.9
ADAM_B2 = 0.999
ADAM_EPS = 1e-08
ADAM_WD = 0.01
ADAM_STEP = 10

VMEM_LIMIT = 52 * 1024 * 1024
SMALL_ROWS = 16
NEG = -1e30


def _cparams(sem=None):
    return pltpu.CompilerParams(dimension_semantics=sem, vmem_limit_bytes=VMEM_LIMIT)


def _split3(v):
    hi = v.astype(BF16)
    rest = v - hi.astype(F32)
    mid = rest.astype(BF16)
    return hi, mid, (rest - mid.astype(F32)).astype(BF16)


def _hdot(a, b, data):
    if data == "a":
        sel = b.astype(BF16)
        return sum(_dot_nn(part, sel) for part in _split3(a))
    sel = a.astype(BF16)
    return sum(_dot_nn(sel, part) for part in _split3(b))


def _dot_nn(a, b):
    return lax.dot_general(a, b, (((1,), (0,)), ((), ())), preferred_element_type=F32)


def _dot_nt(a, b):
    return lax.dot_general(a, b, (((1,), (1,)), ((), ())), preferred_element_type=F32)


def _dot_tn(a, b):
    return lax.dot_general(a, b, (((0,), (0,)), ((), ())), preferred_element_type=F32)


def _softplus(v):
    return jnp.maximum(v, 0.0) + jnp.log1p(jnp.exp(-jnp.abs(v)))


def _sigmoid(v):
    return 1.0 / (1.0 + jnp.exp(-v))


def _matmul(a, b, *, mode, grid, a_spec, b_spec, out_shapes, out_specs, tile, name,
            extras=(), extra_specs=(), epilogue=None, after=None, dot_fn=None, prefetch=None):
    nk = grid[2]
    n_ex = len(extras)
    n_out = len(out_shapes)
    dot = dot_fn if dot_fn is not None else {"nn": _dot_nn, "nt": _dot_nt, "tn": _dot_tn}[mode]

    def finish(acc, ex_refs, out_refs):
        res = (acc,) if epilogue is None else epilogue(acc, *[e[...] for e in ex_refs])
        for o, r in zip(out_refs, res):
            o[...] = r.astype(o.dtype)

    def body(*refs):
        a_ref, b_ref = refs[0], refs[1]
        ex_refs = refs[2:2 + n_ex]
        out_refs = refs[2 + n_ex:2 + n_ex + n_out]
        part = dot(a_ref[...].astype(BF16), b_ref[...].astype(BF16))
        if nk == 1:
            finish(part, ex_refs, out_refs)
        else:
            acc_ref = refs[-1]
            k = pl.program_id(2)

            @pl.when(k == 0)
            def _():
                acc_ref[...] = part

            @pl.when(k > 0)
            def _():
                acc_ref[...] += part

            @pl.when(k == nk - 1)
            def _():
                finish(acc_ref[...], ex_refs, out_refs)

    scratch = [] if nk == 1 else [pltpu.VMEM(tile, F32)]
    n_pre = 0 if prefetch is None else 1
    tok_specs = [] if after is None else [pl.BlockSpec((8, 128), lambda *_: (0, 0))]
    tok_args = [] if after is None else [after]

    def body_with_token(*refs):
        refs = refs[n_pre:]
        body(*refs[:2 + n_ex], *refs[2 + n_ex + len(tok_args):])

    in_specs = [a_spec, b_spec, *extra_specs, *tok_specs]
    params = _cparams(("parallel", "parallel", "arbitrary"))
    if prefetch is None:
        return pl.pallas_call(
            body_with_token, grid=grid, in_specs=in_specs, out_specs=list(out_specs), out_shape=list(out_shapes),
            scratch_shapes=scratch, name=name, compiler_params=params)(a, b, *extras, *tok_args)
    return pl.pallas_call(
        body_with_token,
        grid_spec=pltpu.PrefetchScalarGridSpec(num_scalar_prefetch=1, grid=grid, in_specs=in_specs,
                                               out_specs=list(out_specs), scratch_shapes=scratch),
        out_shape=list(out_shapes), name=name, compiler_params=params)(prefetch, a, b, *extras, *tok_args)


def _mm_simple(a, b, *, mode, M, N, K, tm, tn, tk, out_dtype, name, extras=(), epilogue=None, n_out=1,
               out_dtypes=None, after=None):
    grid = (M // tm, N // tn, K // tk)
    if mode == "nn":
        a_spec = pl.BlockSpec((tm, tk), lambda i, j, k: (i, k))
        b_spec = pl.BlockSpec((tk, tn), lambda i, j, k: (k, j))
    elif mode == "nt":
        a_spec = pl.BlockSpec((tm, tk), lambda i, j, k: (i, k))
        b_spec = pl.BlockSpec((tn, tk), lambda i, j, k: (j, k))
    else:
        a_spec = pl.BlockSpec((tk, tm), lambda i, j, k: (k, i))
        b_spec = pl.BlockSpec((tk, tn), lambda i, j, k: (k, j))
    o_spec = pl.BlockSpec((tm, tn), lambda i, j, k: (i, j))
    dts = out_dtypes if out_dtypes is not None else [out_dtype] * n_out
    return _matmul(a, b, mode=mode, grid=grid, a_spec=a_spec, b_spec=b_spec,
                   out_shapes=[jax.ShapeDtypeStruct((M, N), d) for d in dts],
                   out_specs=[o_spec] * len(dts), tile=(tm, tn), name=name,
                   extras=extras, extra_specs=[o_spec] * len(extras), epilogue=epilogue, after=after)


ROW_BLOCK = 256


def _rmsnorm_fwd(x, g, name):
    T, D = x.shape

    def body(x_ref, g_ref, o_ref):
        xf = x_ref[...]
        r = lax.rsqrt(jnp.mean(xf * xf, axis=-1, keepdims=True) + EPS)
        o_ref[...] = (xf * r * g_ref[...]).astype(BF16)

    return pl.pallas_call(
        body, grid=(T // ROW_BLOCK,),
        in_specs=[pl.BlockSpec((ROW_BLOCK, D), lambda i: (i, 0)), pl.BlockSpec((1, D), lambda i: (0, 0))],
        out_specs=pl.BlockSpec((ROW_BLOCK, D), lambda i: (i, 0)),
        out_shape=jax.ShapeDtypeStruct((T, D), BF16), name=name, compiler_params=_cparams(("parallel",)),
    )(x, g)


def _rmsnorm_bwd(dh, x, g, dres, name):
    T, D = x.shape

    def body(dh_ref, x_ref, g_ref, dres_ref, dx_ref, dxb_ref, dg_ref):
        i = pl.program_id(0)
        xf = x_ref[...]
        r = lax.rsqrt(jnp.mean(xf * xf, axis=-1, keepdims=True) + EPS)
        xh = xf * r
        d = dh_ref[...]

        @pl.when(i == 0)
        def _():
            dg_ref[...] = jnp.zeros_like(dg_ref)

        dg_ref[...] += jnp.sum(d * xh, axis=0, keepdims=True)
        dxh = d * g_ref[...]
        dx = r * (dxh - xh * jnp.mean(dxh * xh, axis=-1, keepdims=True)) + dres_ref[...]
        dx_ref[...] = dx
        dxb_ref[...] = dx.astype(BF16)

    row = pl.BlockSpec((ROW_BLOCK, D), lambda i: (i, 0))
    vec = pl.BlockSpec((1, D), lambda i: (0, 0))
    return pl.pallas_call(
        body, grid=(T // ROW_BLOCK,), in_specs=[row, row, vec, row], out_specs=[row, row, vec],
        out_shape=[jax.ShapeDtypeStruct((T, D), F32), jax.ShapeDtypeStruct((T, D), BF16),
                   jax.ShapeDtypeStruct((1, D), F32)],
        name=name, compiler_params=_cparams(("arbitrary",)),
    )(dh, x, g, dres)


def _final_loss(x3, tgt, g, name):
    T, D = x3.shape

    def body(x_ref, t_ref, g_ref, loss_ref, dg_ref, dx_ref, dxb_ref):
        i = pl.program_id(0)
        xf = x_ref[...]
        r = lax.rsqrt(jnp.mean(xf * xf, axis=-1, keepdims=True) + EPS)
        xh = xf * r
        gg = g_ref[...]
        err = xh * gg - t_ref[...]

        @pl.when(i == 0)
        def _():
            dg_ref[...] = jnp.zeros_like(dg_ref)
            loss_ref[...] = jnp.zeros_like(loss_ref)

        part = jnp.sum(jnp.sum(err * err, axis=-1, keepdims=True), axis=0, keepdims=True) * (0.5 / D)
        loss_ref[...] += jnp.broadcast_to(part, loss_ref.shape)
        dout = err * (1.0 / D)
        dg_ref[...] += jnp.sum(dout * xh, axis=0, keepdims=True)
        dxh = dout * gg
        dx = r * (dxh - xh * jnp.mean(dxh * xh, axis=-1, keepdims=True))
        dx_ref[...] = dx
        dxb_ref[...] = dx.astype(BF16)

    row = pl.BlockSpec((ROW_BLOCK, D), lambda i: (i, 0))
    vec = pl.BlockSpec((1, D), lambda i: (0, 0))
    return pl.pallas_call(
        body, grid=(T // ROW_BLOCK,), in_specs=[row, row, vec],
        out_specs=[pl.BlockSpec((1, 128), lambda i: (0, 0)), vec, row, row],
        out_shape=[jax.ShapeDtypeStruct((1, 128), F32), jax.ShapeDtypeStruct((1, D), F32),
                   jax.ShapeDtypeStruct((T, D), F32), jax.ShapeDtypeStruct((T, D), BF16)],
        name=name, compiler_params=_cparams(("arbitrary",)),
    )(x3, tgt, g)


CONV_BLOCK = 256


def _conv_apply(u, w, b):
    row = lax.broadcasted_iota(jnp.int32, u.shape, 0)
    acc = b + w[CONV_K - 1:CONV_K, :] * u
    shifted = []
    for j in range(1, CONV_K):
        uj = jnp.where(row >= j, pltpu.roll(u, j, axis=0), 0.0)
        shifted.append(uj)
        acc = acc + w[CONV_K - 1 - j:CONV_K - j, :] * uj
    return acc, shifted


def _conv_fwd(proj, conv_w, conv_b, name):
    T = proj.shape[0]
    cb0 = OFF_X // CONV_BLOCK

    def body(u_ref, w_ref, b_ref, o_ref):
        c, _ = _conv_apply(u_ref[...], w_ref[...], b_ref[...])
        o_ref[...] = c * _sigmoid(c)

    return pl.pallas_call(
        body, grid=(CONV_DIM // CONV_BLOCK,),
        in_specs=[pl.BlockSpec((T, CONV_BLOCK), lambda j: (0, cb0 + j)),
                  pl.BlockSpec((CONV_K, CONV_BLOCK), lambda j: (0, j)),
                  pl.BlockSpec((1, CONV_BLOCK), lambda j: (0, j))],
        out_specs=pl.BlockSpec((T, CONV_BLOCK), lambda j: (0, j)),
        out_shape=jax.ShapeDtypeStruct((T, CONV_DIM), F32), name=name, compiler_params=_cparams(("parallel",)),
    )(proj, conv_w, conv_b)


def _conv_bwd(proj, dact, conv_w, conv_b, dproj, name):
    T = proj.shape[0]
    cb0 = OFF_X // CONV_BLOCK

    def body(u_ref, d_ref, w_ref, b_ref, _, du_ref, dw_ref, db_ref):
        u = u_ref[...]
        w = w_ref[...]
        c, shifted = _conv_apply(u, w, b_ref[...])
        sg = _sigmoid(c)
        dc = d_ref[...] * sg * (1.0 + c * (1.0 - sg))
        row = lax.broadcasted_iota(jnp.int32, u.shape, 0)
        du = w[CONV_K - 1:CONV_K, :] * dc
        dw_ref[CONV_K - 1:CONV_K, :] = jnp.sum(dc * u, axis=0, keepdims=True)
        for j in range(1, CONV_K):
            dcj = jnp.where(row < T - j, pltpu.roll(dc, T - j, axis=0), 0.0)
            du = du + w[CONV_K - 1 - j:CONV_K - j, :] * dcj
            dw_ref[CONV_K - 1 - j:CONV_K - j, :] = jnp.sum(dc * shifted[j - 1], axis=0, keepdims=True)
        db_ref[...] = jnp.sum(dc, axis=0, keepdims=True)
        du_ref[...] = du.astype(BF16)

    return pl.pallas_call(
        body, grid=(CONV_DIM // CONV_BLOCK,),
        in_specs=[pl.BlockSpec((T, CONV_BLOCK), lambda j: (0, cb0 + j)),
                  pl.BlockSpec((T, CONV_BLOCK), lambda j: (0, j)),
                  pl.BlockSpec((CONV_K, CONV_BLOCK), lambda j: (0, j)),
                  pl.BlockSpec((1, CONV_BLOCK), lambda j: (0, j)), pl.BlockSpec(memory_space=pl.ANY)],
        out_specs=[pl.BlockSpec((T, CONV_BLOCK), lambda j: (0, cb0 + j)),
                   pl.BlockSpec((CONV_K, CONV_BLOCK), lambda j: (0, j)),
                   pl.BlockSpec((1, CONV_BLOCK), lambda j: (0, j))],
        out_shape=[jax.ShapeDtypeStruct(dproj.shape, BF16), jax.ShapeDtypeStruct((CONV_K, CONV_DIM), F32),
                   jax.ShapeDtypeStruct((1, CONV_DIM), F32)],
        input_output_aliases={4: 0}, name=name, compiler_params=_cparams(("parallel",)),
    )(proj, dact, conv_w, conv_b, dproj)


GROUP_W = D_INNER // N_GROUPS
HEADS_PER_GROUP = N_HEADS // N_GROUPS


def _expand_mat():
    h = lax.broadcasted_iota(jnp.int32, (N_HEADS, D_INNER), 0)
    j = lax.broadcasted_iota(jnp.int32, (N_HEADS, D_INNER), 1)
    return (j // HEAD_DIM == h).astype(F32)


def _reduce_mat(g):
    j = lax.broadcasted_iota(jnp.int32, (GROUP_W, N_HEADS), 0)
    h = lax.broadcasted_iota(jnp.int32, (GROUP_W, N_HEADS), 1)
    return (g * HEADS_PER_GROUP + j // HEAD_DIM == h).astype(F32)


def _col16(v, h):
    lane = lax.broadcasted_iota(jnp.int32, v.shape, 1)
    return jnp.sum(jnp.where(lane == h, v, 0.0), axis=1, keepdims=True)


def _ssd_pre(dt_raw, dtT_raw, dtb, dtbT, alog, alogT):
    Q = CHUNK
    xdt = dt_raw + dtb
    dt = _softplus(xdt)
    dtT = _softplus(dtT_raw + dtbT)
    A = -jnp.exp(alog)
    AT = -jnp.exp(alogT)
    row = lax.broadcasted_iota(jnp.int32, (Q, Q), 0)
    col = lax.broadcasted_iota(jnp.int32, (Q, Q), 1)
    tril = (row >= col).astype(F32)
    triu = (row <= col).astype(F32)
    cs = _hdot(tril, dt * A, "b")
    csT = _hdot(dtT * AT, triu, "a")
    return xdt, dt, A, cs, csT, row >= col, triu


def _decay_matrix(cs, csT, h, causal):
    seg = _col16(cs, h) - csT[h:h + 1, :]
    return jnp.where(causal, jnp.exp(jnp.minimum(seg, 0.0)), 0.0)


def _ssd_in_specs(nc, rev):
    def cidx(c):
        return (nc - 1 - c) if rev else c

    return [
        pl.BlockSpec((CHUNK, D_INNER), lambda c: (cidx(c), 0)),
        pl.BlockSpec((CHUNK, 512), lambda c: (cidx(c), 2)),
        pl.BlockSpec((CHUNK, 512), lambda c: (cidx(c), 3)),
        pl.BlockSpec((CHUNK, D_INNER), lambda c: (cidx(c), 0)),
        pl.BlockSpec((CHUNK, 128), lambda c: (cidx(c), OFF_DT // 128)),
        pl.BlockSpec((N_HEADS, CHUNK), lambda c: (0, cidx(c))),
        pl.BlockSpec((1, N_HEADS), lambda c: (0, 0)),
        pl.BlockSpec((N_HEADS, 1), lambda c: (0, 0)),
        pl.BlockSpec((1, N_HEADS), lambda c: (0, 0)),
        pl.BlockSpec((N_HEADS, 1), lambda c: (0, 0)),
        pl.BlockSpec((1, D_INNER), lambda c: (0, 0)),
        pl.BlockSpec((1, D_INNER), lambda c: (0, 0)),
    ]


def _ssd_fwd(xbc, proj, dtT, dtb, dtbT, alog, alogT, dfull, ng, name):
    T = xbc.shape[0]
    nc = T // CHUNK
    Q = CHUNK

    def body(xs_ref, B_ref, C_ref, z_ref, dt_ref, dtT_ref, dtb_ref, dtbT_ref, al_ref, alT_ref, df_ref, ng_ref,
             y_ref, ypre_ref, hs_ref, h_scr):
        c = pl.program_id(0)

        @pl.when(c == 0)
        def _():
            h_scr[...] = jnp.zeros_like(h_scr)

        _, dt, _, cs, csT, causal, _ = _ssd_pre(dt_ref[:, :N_HEADS], dtT_ref[...], dtb_ref[...], dtbT_ref[...],
                                                al_ref[...], alT_ref[...])
        ex = _expand_mat()
        dt_full = _hdot(dt, ex, "a")
        cs_full = _hdot(cs, ex, "a")
        cs_last = cs_full[Q - 1:Q, :]
        xs = xs_ref[...]
        xd = xs * dt_full
        e_full = jnp.exp(cs_full)
        dec_full = jnp.exp(cs_last - cs_full)
        cd_full = jnp.exp(cs_last)
        lane_head = lax.broadcasted_iota(jnp.int32, (1, GROUP_W), 1) // HEAD_DIM
        for g in range(N_GROUPS):
            sl = slice(g * GROUP_W, (g + 1) * GROUP_W)
            Bg = B_ref[:, g * D_STATE:(g + 1) * D_STATE].astype(BF16)
            Cg = C_ref[:, g * D_STATE:(g + 1) * D_STATE].astype(BF16)
            CB = _dot_nt(Cg, Bg)
            hg = h_scr[g]
            yoff = _dot_nn(Cg, hg.astype(BF16)) * e_full[:, sl]
            xd_g = xd[:, sl]
            S = _dot_tn(Bg, (xd_g * dec_full[:, sl]).astype(BF16))
            xd_b = xd_g.astype(BF16)
            ydiag = jnp.zeros((Q, GROUP_W), F32)
            for r in range(HEADS_PER_GROUP):
                Lm = _decay_matrix(cs, csT, g * HEADS_PER_GROUP + r, causal)
                Gm = (CB * Lm).astype(BF16)
                ydiag = ydiag + _dot_nn(Gm, jnp.where(lane_head == r, xd_b, jnp.zeros_like(xd_b)))
            hs_ref[0, g] = hg
            h_scr[g] = hg * cd_full[:, sl] + S
            ypre = ydiag + yoff + xs[:, sl] * df_ref[:, sl]
            ypre_ref[:, sl] = ypre
            zg = z_ref[:, sl]
            yz = ypre * zg * _sigmoid(zg)
            rn = lax.rsqrt(jnp.mean(yz * yz, axis=-1, keepdims=True) + EPS)
            y_ref[:, sl] = (yz * rn * ng_ref[:, sl]).astype(BF16)

    return pl.pallas_call(
        body, grid=(nc,), in_specs=_ssd_in_specs(nc, False),
        out_specs=[pl.BlockSpec((CHUNK, D_INNER), lambda c: (c, 0)),
                   pl.BlockSpec((CHUNK, D_INNER), lambda c: (c, 0)),
                   pl.BlockSpec((1, N_GROUPS, D_STATE, GROUP_W), lambda c: (c, 0, 0, 0))],
        out_shape=[jax.ShapeDtypeStruct((T, D_INNER + ATTN_W), BF16), jax.ShapeDtypeStruct((T, D_INNER), F32),
                   jax.ShapeDtypeStruct((nc, N_GROUPS, D_STATE, GROUP_W), F32)],
        scratch_shapes=[pltpu.VMEM((N_GROUPS, D_STATE, GROUP_W), F32)],
        name=name, compiler_params=_cparams(("arbitrary",)),
    )(xbc, xbc, xbc, proj, proj, dtT, dtb, dtbT, alog, alogT, dfull, ng)


def _ssd_bwd(xbc, proj, dtT, dtb, dtbT, alog, alogT, dfull, ng, ypre, hs, dy, name):
    T = xbc.shape[0]
    nc = T // CHUNK
    Q = CHUNK

    def body(xs_ref, B_ref, C_ref, z_ref, dt_ref, dtT_ref, dtb_ref, dtbT_ref, al_ref, alT_ref, df_ref, ng_ref,
             ypre_ref, hs_ref, dy_ref,
             dz_ref, dxbc_ref, ddtb_ref, dal_ref, dD_ref, dng_ref, dh_scr):
        step = pl.program_id(0)

        @pl.when(step == 0)
        def _():
            dh_scr[...] = jnp.zeros_like(dh_scr)
            ddtb_ref[...] = jnp.zeros_like(ddtb_ref)
            dal_ref[...] = jnp.zeros_like(dal_ref)
            dD_ref[...] = jnp.zeros_like(dD_ref)
            dng_ref[...] = jnp.zeros_like(dng_ref)

        xdt, dt, A, cs, csT, causal, triu = _ssd_pre(dt_ref[:, :N_HEADS], dtT_ref[...], dtb_ref[...],
                                                    dtbT_ref[...], al_ref[...], alT_ref[...])
        ex = _expand_mat()
        dt_full = _hdot(dt, ex, "a")
        cs_full = _hdot(cs, ex, "a")
        cs_last = cs_full[Q - 1:Q, :]
        xs = xs_ref[...]
        xd = xs * dt_full
        e_full = jnp.exp(cs_full)
        dec_full = jnp.exp(cs_last - cs_full)
        cd_full = jnp.exp(cs_last)
        lane_head = lax.broadcasted_iota(jnp.int32, (1, GROUP_W), 1) // HEAD_DIM
        is_last = lax.broadcasted_iota(jnp.int32, (Q, 1), 0) == Q - 1
        dcs16 = jnp.zeros((Q, N_HEADS), F32)
        ddtx16 = jnp.zeros((Q, N_HEADS), F32)
        dD16 = jnp.zeros((8, N_HEADS), F32)
        lane16 = lax.broadcasted_iota(jnp.int32, (1, N_HEADS), 1)
        sub16 = lax.broadcasted_iota(jnp.int32, (N_HEADS, 1), 0)
        col_sums = jnp.zeros((N_HEADS, Q), F32)
        for g in range(N_GROUPS):
            sl = slice(g * GROUP_W, (g + 1) * GROUP_W)
            red = _reduce_mat(g)
            ypre_g = ypre_ref[:, sl]
            zg = z_ref[:, sl]
            sg = _sigmoid(zg)
            silu = zg * sg
            yz = ypre_g * silu
            rn = lax.rsqrt(jnp.mean(yz * yz, axis=-1, keepdims=True) + EPS)
            yh = yz * rn
            dy_g = dy_ref[:, sl]
            dng_ref[:, sl] += jnp.sum(dy_g * yh, axis=0, keepdims=True)
            dyh = dy_g * ng_ref[:, sl]
            dyz = rn * (dyh - yh * jnp.mean(dyh * yh, axis=-1, keepdims=True))
            dY = dyz * silu
            dz_ref[:, sl] = (dyz * ypre_g * sg * (1.0 + zg * (1.0 - sg))).astype(BF16)
            xs_g = xs[:, sl]
            xd_g = xd[:, sl]
            dec_g = dec_full[:, sl]
            cd_g = cd_full[:, sl]
            d_g = df_ref[:, sl]
            Bg = B_ref[:, g * D_STATE:(g + 1) * D_STATE].astype(BF16)
            Cg = C_ref[:, g * D_STATE:(g + 1) * D_STATE].astype(BF16)
            CB = _dot_nt(Cg, Bg)
            hg = hs_ref[0, g]
            hgb = hg.astype(BF16)
            yoff = _dot_nn(Cg, hgb) * e_full[:, sl]
            dhn = dh_scr[g]
            dhnb = dhn.astype(BF16)
            dYE = (dY * e_full[:, sl]).astype(BF16)
            dC = _dot_nt(dYE, hgb)
            dh_direct = _dot_tn(Cg, dYE)
            dXdd = _dot_nn(Bg, dhnb)
            dB = _dot_nt((xd_g * dec_g).astype(BF16), dhnb)
            dcd = jnp.sum(dhn * hg, axis=0, keepdims=True)
            dh_scr[g] = dh_direct + cd_g * dhn
            dYb = dY.astype(BF16)
            xd_b = xd_g.astype(BF16)
            dCB = jnp.zeros((Q, Q), F32)
            dXd = dXdd * dec_g
            for r in range(HEADS_PER_GROUP):
                h = g * HEADS_PER_GROUP + r
                Lm = _decay_matrix(cs, csT, h, causal)
                Gf = CB * Lm
                dYr = jnp.where(lane_head == r, dYb, jnp.zeros_like(dYb))
                dG = _dot_nt(dYr, xd_b)
                dCB = dCB + dG * Lm
                dXd = dXd + _dot_tn(Gf.astype(BF16), dYr)
                Mm = dG * Gf
                dcs16 = dcs16 + jnp.where(lane16 == h, jnp.sum(Mm, axis=1, keepdims=True), 0.0)
                col_sums = col_sums + jnp.where(sub16 == h, jnp.sum(Mm, axis=0, keepdims=True), 0.0)
            dCBb = dCB.astype(BF16)
            dC = dC + _dot_nn(dCBb, Bg)
            dB = dB + _dot_tn(dCBb, Cg)
            w_state = dXdd * dec_g * xd_g
            t_last = jnp.sum(w_state, axis=0, keepdims=True) + dcd * cd_g
            dcs_g = dY * yoff - w_state + jnp.where(is_last, t_last, 0.0)
            dcs16 = dcs16 + _hdot(dcs_g, red, "a")
            ddtx16 = ddtx16 + _hdot(dXd * xs_g, red, "a")
            dD16 = dD16 + _hdot(jnp.broadcast_to(jnp.sum(dY * xs_g, axis=0, keepdims=True), (8, GROUP_W)), red, "a")
            dxbc_ref[:, sl] = dXd * dt_full[:, sl] + dY * d_g
            dxbc_ref[:, D_INNER + g * D_STATE:D_INNER + (g + 1) * D_STATE] = dB
            dxbc_ref[:, D_INNER + 512 + g * D_STATE:D_INNER + 512 + (g + 1) * D_STATE] = dC
        eye = (lax.broadcasted_iota(jnp.int32, (N_HEADS, N_HEADS), 0)
               == lax.broadcasted_iota(jnp.int32, (N_HEADS, N_HEADS), 1)).astype(BF16)
        dcs16 = dcs16 - sum(_dot_tn(part, eye) for part in _split3(col_sums))
        da = _hdot(triu, dcs16, "b")
        ddt = da * A + ddtx16
        ddt_raw = ddt * _sigmoid(xdt)
        pr = lax.broadcasted_iota(jnp.int32, (N_HEADS, 128), 0)
        pc = lax.broadcasted_iota(jnp.int32, (N_HEADS, 128), 1)
        dz_ref[:, D_INNER:OFF_DT] = jnp.zeros((Q, OFF_DT - D_INNER), BF16)
        dz_ref[:, OFF_DT:OFF_DT + 128] = _hdot(ddt_raw, (pr == pc).astype(F32), "a").astype(BF16)
        dz_ref[:, OFF_DT + 128:] = jnp.zeros((Q, NP - OFF_DT - 128), BF16)
        ddtb_ref[...] += jnp.sum(ddt_raw, axis=0, keepdims=True)
        dal_ref[...] += jnp.sum(da * dt, axis=0, keepdims=True) * A
        dD_ref[...] += dD16[0:1, :]

    def rc(c):
        return nc - 1 - c

    in_specs = _ssd_in_specs(nc, True) + [
        pl.BlockSpec((CHUNK, D_INNER), lambda c: (rc(c), 0)),
        pl.BlockSpec((1, N_GROUPS, D_STATE, GROUP_W), lambda c: (rc(c), 0, 0, 0)),
        pl.BlockSpec((CHUNK, D_INNER), lambda c: (rc(c), 0)),
    ]
    small = pl.BlockSpec((1, N_HEADS), lambda c: (0, 0))
    return pl.pallas_call(
        body, grid=(nc,), in_specs=in_specs,
        out_specs=[pl.BlockSpec((CHUNK, NP), lambda c: (rc(c), 0)),
                   pl.BlockSpec((CHUNK, CONV_DIM), lambda c: (rc(c), 0)),
                   small, small, small,
                   pl.BlockSpec((1, D_INNER), lambda c: (0, 0))],
        out_shape=[jax.ShapeDtypeStruct((T, NP), BF16), jax.ShapeDtypeStruct((T, CONV_DIM), F32),
                   jax.ShapeDtypeStruct((1, N_HEADS), F32), jax.ShapeDtypeStruct((1, N_HEADS), F32),
                   jax.ShapeDtypeStruct((1, N_HEADS), F32), jax.ShapeDtypeStruct((1, D_INNER), F32)],
        scratch_shapes=[pltpu.VMEM((N_GROUPS, D_STATE, GROUP_W), F32)],
        name=name, compiler_params=_cparams(("arbitrary",)),
    )(xbc, xbc, xbc, proj, proj, dtT, dtb, dtbT, alog, alogT, dfull, ng, ypre, hs, dy)


N_PAIRS = ATTN_W // 128
PAIRS_PER_KV = N_PAIRS // 2
ATTN_SCALE = HEAD_DIM ** -0.5


def _kv_variants(kk):
    lo = lax.broadcasted_iota(jnp.int32, kk.shape, 1) < HEAD_DIM
    zero = jnp.zeros_like(kk)
    k00 = jnp.where(lo, kk, zero)
    k11 = jnp.where(lo, zero, kk)
    k01 = pltpu.roll(k00, HEAD_DIM, axis=1)
    k10 = pltpu.roll(k11, HEAD_DIM, axis=1)
    return [[k00.astype(BF16), k01.astype(BF16)], [k10.astype(BF16), k11.astype(BF16)]]


def _attn_valid(n):
    i = lax.broadcasted_iota(jnp.int32, (WINDOW, 2 * WINDOW), 0)
    j = lax.broadcasted_iota(jnp.int32, (WINDOW, 2 * WINDOW), 1)
    return (j > i) & (j <= i + WINDOW) & (n * WINDOW + j >= WINDOW)


def _attn_probs(qp, kvar, valid, sk):
    s = _dot_nt(qp, kvar) * ATTN_SCALE
    s = jnp.where(valid, s, NEG)
    m = jnp.maximum(jnp.max(s, axis=1, keepdims=True), sk)
    pe = jnp.exp(s - m)
    es = jnp.exp(sk - m)
    den = jnp.sum(pe, axis=1, keepdims=True) + es
    inv = 1.0 / den
    return pe * inv, es * inv


def _sink(sinks, r):
    lane = lax.broadcasted_iota(jnp.int32, sinks.shape, 1)
    return jnp.sum(jnp.where(lane == r, sinks, 0.0), axis=1, keepdims=True)


def _kv_specs():
    return [pl.BlockSpec((WINDOW, KV_W), lambda n: (jnp.maximum(n - 1, 0), OFF_K // KV_W)),
            pl.BlockSpec((WINDOW, KV_W), lambda n: (n, OFF_K // KV_W)),
            pl.BlockSpec((WINDOW, KV_W), lambda n: (jnp.maximum(n - 1, 0), OFF_V // KV_W)),
            pl.BlockSpec((WINDOW, KV_W), lambda n: (n, OFF_V // KV_W))]


def _attn_fwd(proj, sinks, og, ycat, name):
    T = proj.shape[0]
    nb = T // WINDOW

    def body(q_ref, kp_ref, kc_ref, vp_ref, vc_ref, s_ref, og_ref, _, y_ref, o_ref):
        n = pl.program_id(0)
        kv = _kv_variants(jnp.concatenate([kp_ref[...], kc_ref[...]], axis=0))
        vv = _kv_variants(jnp.concatenate([vp_ref[...], vc_ref[...]], axis=0))
        valid = _attn_valid(n)
        sinks_v = s_ref[...]
        ssq = jnp.zeros((WINDOW, 1), F32)
        for p in range(N_PAIRS):
            j = p // PAIRS_PER_KV
            qp = q_ref[:, p * 128:(p + 1) * 128].astype(BF16)
            o_pair = jnp.zeros((WINDOW, 128), F32)
            for par in range(2):
                pn, _ = _attn_probs(qp, kv[j][par], valid, _sink(sinks_v, 2 * p + par))
                o_pair = o_pair + _dot_nn(pn.astype(BF16), vv[j][par])
            o_ref[:, p * 128:(p + 1) * 128] = o_pair
            ssq = ssq + jnp.sum(o_pair * o_pair, axis=1, keepdims=True)
        rn = lax.rsqrt(ssq * (1.0 / ATTN_W) + EPS)
        y_ref[...] = (o_ref[...] * rn * og_ref[...]).astype(BF16)

    return pl.pallas_call(
        body, grid=(nb,),
        in_specs=[pl.BlockSpec((WINDOW, ATTN_W), lambda n: (n, OFF_Q // ATTN_W)), *_kv_specs(),
                  pl.BlockSpec((1, N_HEADS), lambda n: (0, 0)), pl.BlockSpec((1, ATTN_W), lambda n: (0, 0)), ANY],
        out_specs=[pl.BlockSpec((WINDOW, ATTN_W), lambda n: (n, 1)), pl.BlockSpec((WINDOW, ATTN_W), lambda n: (n, 0))],
        out_shape=[jax.ShapeDtypeStruct(ycat.shape, BF16), jax.ShapeDtypeStruct((T, ATTN_W), F32)],
        input_output_aliases={7: 0}, name=name, compiler_params=_cparams(("parallel",)),
    )(proj, proj, proj, proj, proj, sinks, og, ycat)


def _attn_bwd(proj, sinks, og, o, dy, dproj, name):
    T = proj.shape[0]
    nb = T // WINDOW

    def body(q_ref, kp_ref, kc_ref, vp_ref, vc_ref, s_ref, og_ref, o_ref, dy_ref, _,
             dq_ref, dk_ref, dv_ref, ds_ref, dog_ref):
        n = pl.program_id(0)

        @pl.when(n == 0)
        def _():
            dk_ref[...] = jnp.zeros_like(dk_ref)
            dv_ref[...] = jnp.zeros_like(dv_ref)
            ds_ref[...] = jnp.zeros_like(ds_ref)
            dog_ref[...] = jnp.zeros_like(dog_ref)

        kv = _kv_variants(jnp.concatenate([kp_ref[...], kc_ref[...]], axis=0))
        vv = _kv_variants(jnp.concatenate([vp_ref[...], vc_ref[...]], axis=0))
        valid = _attn_valid(n)
        sinks_v = s_ref[...]
        of = o_ref[...]
        rn = lax.rsqrt(jnp.mean(of * of, axis=-1, keepdims=True) + EPS)
        oh = of * rn
        dyf = dy_ref[...]
        dog_ref[...] += jnp.sum(dyf * oh, axis=0, keepdims=True)
        doh = dyf * og_ref[...]
        do = rn * (doh - oh * jnp.mean(doh * oh, axis=-1, keepdims=True))
        lane = lax.broadcasted_iota(jnp.int32, (1, 128), 1)
        lane16 = lax.broadcasted_iota(jnp.int32, (1, N_HEADS), 1)
        dk_acc = [[jnp.zeros((2 * WINDOW, 128), F32) for _ in range(2)] for _ in range(2)]
        dv_acc = [[jnp.zeros((2 * WINDOW, 128), F32) for _ in range(2)] for _ in range(2)]
        dsink = jnp.zeros((1, N_HEADS), F32)
        for p in range(N_PAIRS):
            j = p // PAIRS_PER_KV
            qp = q_ref[:, p * 128:(p + 1) * 128].astype(BF16)
            do_p = do[:, p * 128:(p + 1) * 128]
            o_p = of[:, p * 128:(p + 1) * 128]
            do_b = do_p.astype(BF16)
            prod = do_p * o_p
            dq_pair = jnp.zeros((WINDOW, 128), F32)
            for par in range(2):
                r = 2 * p + par
                half = (lane < HEAD_DIM) if par == 0 else (lane >= HEAD_DIM)
                pn, ps = _attn_probs(qp, kv[j][par], valid, _sink(sinks_v, r))
                delta = jnp.sum(jnp.where(half, prod, 0.0), axis=1, keepdims=True)
                dP = _dot_nt(do_b, vv[j][par])
                dS = pn * (dP - delta)
                dsink = dsink + jnp.where(lane16 == r, -jnp.sum(ps * delta, axis=0, keepdims=True), 0.0)
                dSb = (dS * ATTN_SCALE).astype(BF16)
                dq_pair = dq_pair + _dot_nn(dSb, kv[j][par])
                dk_acc[j][par] = dk_acc[j][par] + _dot_tn(dSb, jnp.where(half, qp, jnp.zeros_like(qp)))
                dv_acc[j][par] = dv_acc[j][par] + _dot_tn(pn.astype(BF16), jnp.where(half, do_b, jnp.zeros_like(do_b)))
            dq_ref[:, p * 128:(p + 1) * 128] = dq_pair.astype(BF16)
        dkk = (dk_acc[0][0] + pltpu.roll(dk_acc[0][1], HEAD_DIM, axis=1)
               + dk_acc[1][1] + pltpu.roll(dk_acc[1][0], HEAD_DIM, axis=1))
        dvv = (dv_acc[0][0] + pltpu.roll(dv_acc[0][1], HEAD_DIM, axis=1)
               + dv_acc[1][1] + pltpu.roll(dv_acc[1][0], HEAD_DIM, axis=1))
        prev = pl.multiple_of(jnp.maximum(n - 1, 0) * WINDOW, WINDOW)
        own = pl.multiple_of(n * WINDOW, WINDOW)
        dk_ref[pl.ds(prev, WINDOW), :] += dkk[:WINDOW]
        dk_ref[pl.ds(own, WINDOW), :] += dkk[WINDOW:]
        dv_ref[pl.ds(prev, WINDOW), :] += dvv[:WINDOW]
        dv_ref[pl.ds(own, WINDOW), :] += dvv[WINDOW:]
        ds_ref[...] += dsink

    full_kv = pl.BlockSpec((T, KV_W), lambda n: (0, 0))
    blk = pl.BlockSpec((WINDOW, ATTN_W), lambda n: (n, 0))
    return pl.pallas_call(
        body, grid=(nb,),
        in_specs=[pl.BlockSpec((WINDOW, ATTN_W), lambda n: (n, OFF_Q // ATTN_W)), *_kv_specs(),
                  pl.BlockSpec((1, N_HEADS), lambda n: (0, 0)), pl.BlockSpec((1, ATTN_W), lambda n: (0, 0)),
                  blk, pl.BlockSpec((WINDOW, ATTN_W), lambda n: (n, 1)), ANY],
        out_specs=[pl.BlockSpec((WINDOW, ATTN_W), lambda n: (n, OFF_Q // ATTN_W)), full_kv, full_kv,
                   pl.BlockSpec((1, N_HEADS), lambda n: (0, 0)), pl.BlockSpec((1, ATTN_W), lambda n: (0, 0))],
        out_shape=[jax.ShapeDtypeStruct(dproj.shape, BF16), jax.ShapeDtypeStruct((T, KV_W), F32),
                   jax.ShapeDtypeStruct((T, KV_W), F32), jax.ShapeDtypeStruct((1, N_HEADS), F32),
                   jax.ShapeDtypeStruct((1, ATTN_W), F32)],
        input_output_aliases={9: 0}, name=name, compiler_params=_cparams(("arbitrary",)),
    )(proj, proj, proj, proj, proj, sinks, og, o, dy, dproj)


ANY = pl.BlockSpec(memory_space=pl.ANY)


def _coords():
    return lax.axis_index("x"), lax.axis_index("y"), lax.axis_index("c")


def _all_gather(arrs, name):
    n = len(arrs)

    def body(*refs):
        ins, outs = refs[:n], refs[n:2 * n]
        send_sems, recv_sems, local_sems = refs[2 * n:]
        x, y, c = _coords()
        me = 4 * x + 2 * y + c
        sibling = (x, y, 1 - c)
        chips = [(1 - x, y), (x, 1 - y), (1 - x, 1 - y)]

        def copy(a, k, block, to, src=None):
            dst = outs[a].at[block]
            return pltpu.make_async_remote_copy(
                src_ref=dst if src is None else src, dst_ref=dst, send_sem=send_sems.at[a, k],
                recv_sem=recv_sems.at[a, k], device_id=to, device_id_type=MESH)

        mine = [pltpu.make_async_copy(ins[a], outs[a].at[me], local_sems.at[a]) for a in range(n)]
        for cp in mine:
            cp.start()
        first = []
        for a in range(n):
            first.append(copy(a, 0, me, sibling, src=ins[a]))
            for j, chip in enumerate(chips):
                first.append(copy(a, 1 + j, me, (*chip, c), src=ins[a]))
        for cp in first:
            cp.start()
        passed = []
        for j, (px, py) in enumerate(chips):
            blk = 4 * px + 2 * py + c
            for a in range(n):
                copy(a, 1 + j, blk, sibling).wait_recv()
                fwd = copy(a, 4 + j, blk, sibling)
                fwd.start()
                passed.append(fwd)
        for a in range(n):
            copy(a, 0, 4 * x + 2 * y + (1 - c), sibling).wait_recv()
            for j, (px, py) in enumerate(chips):
                copy(a, 4 + j, 4 * px + 2 * py + (1 - c), sibling).wait_recv()
        for cp in first + passed:
            cp.wait_send()
        for cp in mine:
            cp.wait()

    return pl.pallas_call(
        body, in_specs=[ANY] * n, out_specs=[ANY] * n,
        out_shape=[jax.ShapeDtypeStruct((N_DEV,) + a.shape, a.dtype) for a in arrs],
        scratch_shapes=[pltpu.SemaphoreType.DMA((n, 7)), pltpu.SemaphoreType.DMA((n, 7)),
                        pltpu.SemaphoreType.DMA((n,))],
        name=name,
    )(*arrs)


def _exchange_pair(arrs, name):
    n = len(arrs)

    def body(*refs):
        ins, outs = refs[:n], refs[n:2 * n]
        send_sems, recv_sems = refs[2 * n:]
        x, y, c = _coords()
        cps = []
        for a in range(n):
            for q in range(4):
                cps.append(pltpu.make_async_remote_copy(
                    src_ref=ins[a].at[2 * q + (1 - c)], dst_ref=outs[a].at[q], send_sem=send_sems.at[a, q],
                    recv_sem=recv_sems.at[a, q], device_id=(x, y, 1 - c), device_id_type=MESH))
        for cp in cps:
            cp.start()
        for cp in cps:
            cp.wait()

    return pl.pallas_call(
        body, in_specs=[ANY] * n, out_specs=[ANY] * n,
        out_shape=[jax.ShapeDtypeStruct((4,) + a.shape[1:], a.dtype) for a in arrs],
        scratch_shapes=[pltpu.SemaphoreType.DMA((n, 4)), pltpu.SemaphoreType.DMA((n, 4))],
        name=name,
    )(*arrs)


def _exchange_chips(arrs, name):
    n = len(arrs)

    def body(*refs):
        ins, outs = refs[:n], refs[n:2 * n]
        send_sems, recv_sems = refs[2 * n:]
        x, y, c = _coords()
        chips = [(1 - x, y), (x, 1 - y), (1 - x, 1 - y)]
        cps = []
        for a in range(n):
            for k, (tx, ty) in enumerate(chips):
                cps.append(pltpu.make_async_remote_copy(
                    src_ref=ins[a].at[2 * tx + ty], dst_ref=outs[a].at[k], send_sem=send_sems.at[a, k],
                    recv_sem=recv_sems.at[a, k], device_id=(tx, ty, c), device_id_type=MESH))
        for cp in cps:
            cp.start()
        for cp in cps:
            cp.wait()

    return pl.pallas_call(
        body, in_specs=[ANY] * n, out_specs=[ANY] * n,
        out_shape=[jax.ShapeDtypeStruct((3,) + a.shape[1:], a.dtype) for a in arrs],
        scratch_shapes=[pltpu.SemaphoreType.DMA((n, 3)), pltpu.SemaphoreType.DMA((n, 3))],
        name=name,
    )(*arrs)


HBM = pl.BlockSpec(memory_space=pltpu.HBM)
SEM = pl.BlockSpec(memory_space=pltpu.SEMAPHORE)
EFFECT = pltpu.SideEffectType.DATAFLOW_SIDE_EFFECTING


def _in_hbm(a):
    return pltpu.with_memory_space_constraint(a, pltpu.HBM)


def _remote_start(srcs, lands, plan, n_copies, name, after=None):
    ns, nb = len(srcs), len(srcs) + len(lands)
    n_after = 0 if after is None else 1

    def body(*refs):
        src_refs, land_refs = refs[:ns], refs[ns:nb]
        send_sems, recv_sems = refs[nb + n_after], refs[nb + n_after + 1]
        token = refs[-1]
        x, y, c = _coords()
        for i, (sv, dv, dev) in enumerate(plan(src_refs, land_refs, x, y, c)):
            pltpu.make_async_remote_copy(src_ref=sv, dst_ref=dv, send_sem=send_sems.at[i], recv_sem=recv_sems.at[i],
                                         device_id=dev, device_id_type=MESH).start()
        token[...] = jnp.zeros_like(token)

    bufs = list(srcs) + list(lands)
    outs = pl.pallas_call(
        body, name=name,
        out_shape=(pltpu.SemaphoreType.DMA((n_copies,)), pltpu.SemaphoreType.DMA((n_copies,)),
                   *[pltpu.HBM(b.shape, b.dtype) for b in bufs], jax.ShapeDtypeStruct((8, 128), F32)),
        in_specs=[HBM] * nb + [ANY] * n_after,
        out_specs=(SEM, SEM, *[HBM] * nb, pl.BlockSpec(memory_space=pltpu.VMEM)),
        input_output_aliases={i: 2 + i for i in range(nb)},
        compiler_params=pltpu.CompilerParams(has_side_effects=EFFECT),
    )(*[_in_hbm(b) for b in bufs], *([] if after is None else [after]))
    return outs[0], outs[1], list(outs[2:2 + ns]), list(outs[2 + ns:2 + nb]), outs[-1]


def _remote_wait(started, after, plan, name):
    send_sems, recv_sems, srcs, lands, _ = started
    ns, nb = len(srcs), len(srcs) + len(lands)

    def body(*refs):
        src_refs, land_refs = refs[:ns], refs[ns:nb]
        send_sems, recv_sems = refs[nb], refs[nb + 1]
        x, y, c = _coords()
        for i, (sv, dv, dev) in enumerate(plan(src_refs, land_refs, x, y, c)):
            cp = pltpu.make_async_remote_copy(src_ref=sv, dst_ref=dv, send_sem=send_sems.at[i],
                                              recv_sem=recv_sems.at[i], device_id=dev, device_id_type=MESH)
            cp.wait_send()
            cp.wait_recv()

    bufs = list(srcs) + list(lands)
    outs = pl.pallas_call(
        body, name=name, out_shape=tuple(pltpu.HBM(b.shape, b.dtype) for b in bufs),
        in_specs=[HBM] * nb + [SEM, SEM, ANY], out_specs=tuple([HBM] * nb),
        input_output_aliases={i: i for i in range(nb)},
        compiler_params=pltpu.CompilerParams(has_side_effects=EFFECT),
    )(*bufs, send_sems, recv_sems, after)
    return list(outs[:ns]), list(outs[ns:])


def _gather_plan(src_refs, land_refs, x, y, c):
    me = 4 * x + 2 * y + c
    plan = []
    for s, l in zip(src_refs, land_refs):
        for dev in [(x, y, 1 - c), (1 - x, y, c), (x, 1 - y, c), (1 - x, 1 - y, c)]:
            plan.append((s, l.at[me], dev))
    return plan


def _forward_plan(src_refs, land_refs, x, y, c):
    plan = []
    for l in land_refs:
        for px, py in [(1 - x, y), (x, 1 - y), (1 - x, 1 - y)]:
            blk = l.at[4 * px + 2 * py + c]
            plan.append((blk, blk, (x, y, 1 - c)))
    return plan


def _pair_plan(src_refs, land_refs, x, y, c):
    plan = []
    for s, l in zip(src_refs, land_refs):
        for q in range(4):
            plan.append((s.at[2 * q + (1 - c)], l.at[q], (x, y, 1 - c)))
    return plan


def _pair4_plan(src_refs, land_refs, x, y, c):
    plan = []
    for s, l in zip(src_refs, land_refs):
        for q in range(4):
            plan.append((s.at[q], l.at[q], (x, y, 1 - c)))
    return plan


def _chips_plan(src_refs, land_refs, x, y, c):
    plan = []
    for s, l in zip(src_refs, land_refs):
        for k, (tx, ty) in enumerate([(1 - x, y), (x, 1 - y), (1 - x, 1 - y)]):
            plan.append((s.at[2 * tx + ty], l.at[k], (tx, ty, c)))
    return plan


def _everyone_plan(src_refs, land_refs, x, y, c):
    me = 4 * x + 2 * y + c
    plan = []
    for s, l in zip(src_refs, land_refs):
        for fx, fy, fc in [(0, 0, 1), (1, 0, 0), (1, 0, 1), (0, 1, 0), (0, 1, 1), (1, 1, 0), (1, 1, 1)]:
            dev = ((1 - x) if fx else x, (1 - y) if fy else y, (1 - c) if fc else c)
            plan.append((s, l.at[me], dev))
    return plan


def _gather_finish(gathered, name):
    n = len(gathered)

    def body(*refs):
        outs = refs[n:2 * n]
        send_sems, recv_sems = refs[2 * n:]
        x, y, c = _coords()
        cps = []
        for a in range(n):
            for j, (px, py) in enumerate([(1 - x, y), (x, 1 - y), (1 - x, 1 - y)]):
                blk = outs[a].at[4 * px + 2 * py + c]
                got = outs[a].at[4 * px + 2 * py + (1 - c)]
                cps.append((pltpu.make_async_remote_copy(
                    src_ref=blk, dst_ref=blk, send_sem=send_sems.at[a, j], recv_sem=recv_sems.at[a, j],
                    device_id=(x, y, 1 - c), device_id_type=MESH), pltpu.make_async_remote_copy(
                    src_ref=got, dst_ref=got, send_sem=send_sems.at[a, j], recv_sem=recv_sems.at[a, j],
                    device_id=(x, y, 1 - c), device_id_type=MESH)))
        for cp, _ in cps:
            cp.start()
        for cp, arrival in cps:
            cp.wait_send()
            arrival.wait_recv()

    return pl.pallas_call(
        body, in_specs=[ANY] * n, out_specs=[ANY] * n,
        out_shape=[jax.ShapeDtypeStruct(g.shape, g.dtype) for g in gathered],
        input_output_aliases={a: a for a in range(n)},
        scratch_shapes=[pltpu.SemaphoreType.DMA((n, 3)), pltpu.SemaphoreType.DMA((n, 3))],
        name=name,
    )(*gathered)


def _pair_add(g8, r1, csel, tr, name):
    _, R, C = r1.shape
    g4 = g8.reshape(4, 2, R, C)

    def body(c_ref, g_ref, r_ref, o_ref):
        o_ref[...] = (g_ref[...].astype(F32) + r_ref[...].astype(F32)).astype(BF16)

    return pl.pallas_call(
        body,
        grid_spec=pltpu.PrefetchScalarGridSpec(
            num_scalar_prefetch=1, grid=(4, R // tr),
            in_specs=[pl.BlockSpec((None, None, tr, C), lambda q, i, cs: (q, cs[0], i, 0)),
                      pl.BlockSpec((None, tr, C), lambda q, i, cs: (q, i, 0))],
            out_specs=pl.BlockSpec((None, tr, C), lambda q, i, cs: (q, i, 0))),
        out_shape=jax.ShapeDtypeStruct((4, R, C), BF16), name=name,
        compiler_params=_cparams(("parallel", "parallel")),
    )(csel, g4, r1)


def _adamw_math(w, g, m, v):
    m = ADAM_B1 * m + (1.0 - ADAM_B1) * g
    v = ADAM_B2 * v + (1.0 - ADAM_B2) * (g * g)
    m_hat = m / (1.0 - ADAM_B1 ** ADAM_STEP)
    v_hat = v / (1.0 - ADAM_B2 ** ADAM_STEP)
    delta = -ADAM_LR * (m_hat / (jnp.sqrt(v_hat) + ADAM_EPS) + ADAM_WD * w)
    return delta, m, v


def _adamw_big(w, m, v, p4, r3, qsel, tile, name):
    R, C = w.shape
    tr, tc = tile

    def body(q_ref, w_ref, m_ref, v_ref, p_ref, r_ref, g_out, d_out, m_out, v_out):
        g = p_ref[...].astype(F32) + r_ref[0].astype(F32) + r_ref[1].astype(F32) + r_ref[2].astype(F32)
        d, mn, vn = _adamw_math(w_ref[...], g, m_ref[...], v_ref[...])
        g_out[...] = g
        d_out[...] = d
        m_out[...] = mn
        v_out[...] = vn

    blk = pl.BlockSpec((tr, tc), lambda i, j, qs: (i, j))
    return pl.pallas_call(
        body,
        grid_spec=pltpu.PrefetchScalarGridSpec(
            num_scalar_prefetch=1, grid=(R // tr, C // tc),
            in_specs=[blk, blk, blk, pl.BlockSpec((None, tr, tc), lambda i, j, qs: (qs[0], i, j)),
                      pl.BlockSpec((3, tr, tc), lambda i, j, qs: (0, i, j))],
            out_specs=[blk, blk, blk, blk]),
        out_shape=[jax.ShapeDtypeStruct((R, C), F32)] * 4, name=name,
        compiler_params=_cparams(("parallel", "parallel")),
    )(qsel, w, m, v, p4, r3)


def _sum_partials(p4, r3, qsel, tc, name):
    _, R, C = p4.shape

    def body(q_ref, p_ref, r_ref, o_ref):
        o_ref[...] = p_ref[...].astype(F32) + r_ref[0].astype(F32) + r_ref[1].astype(F32) + r_ref[2].astype(F32)

    return pl.pallas_call(
        body,
        grid_spec=pltpu.PrefetchScalarGridSpec(
            num_scalar_prefetch=1, grid=(C // tc,),
            in_specs=[pl.BlockSpec((None, R, tc), lambda j, qs: (qs[0], 0, j)),
                      pl.BlockSpec((3, R, tc), lambda j, qs: (0, 0, j))],
            out_specs=pl.BlockSpec((R, tc), lambda j, qs: (0, j))),
        out_shape=jax.ShapeDtypeStruct((R, C), F32), name=name, compiler_params=_cparams(("parallel",)),
    )(qsel, p4, r3)


def _adamw_tiled(w, g, m, v, tc, name):
    R, C = w.shape

    def body(w_ref, g_ref, m_ref, v_ref, d_out, m_out, v_out):
        d, mn, vn = _adamw_math(w_ref[...], g_ref[...], m_ref[...], v_ref[...])
        d_out[...] = d
        m_out[...] = mn
        v_out[...] = vn

    blk = pl.BlockSpec((R, tc), lambda j: (0, j))
    return pl.pallas_call(
        body, grid=(C // tc,), in_specs=[blk] * 4, out_specs=[blk] * 3,
        out_shape=[jax.ShapeDtypeStruct((R, C), F32)] * 3, name=name, compiler_params=_cparams(("parallel",)),
    )(w, g, m, v)


def _small_sum(parts, name):
    def body(p_ref, o_ref):
        acc = p_ref[0]
        for d in range(1, N_DEV):
            acc = acc + p_ref[d]
        o_ref[...] = acc

    return pl.pallas_call(
        body, out_shape=jax.ShapeDtypeStruct(parts.shape[1:], F32), name=name,
        compiler_params=_cparams(),
    )(parts)


def _adamw_small(w, g, m, v, name):
    def body(w_ref, g_ref, m_ref, v_ref, d_out, m_out, v_out):
        d, mn, vn = _adamw_math(w_ref[...], g_ref[...], m_ref[...], v_ref[...])
        d_out[...] = d
        m_out[...] = mn
        v_out[...] = vn

    return pl.pallas_call(
        body, out_shape=[jax.ShapeDtypeStruct(w.shape, F32)] * 3, name=name, compiler_params=_cparams(),
    )(w, g, m, v)


def _row(*pieces):
    r = jnp.concatenate([p.reshape(1, -1) for p in pieces], axis=1)
    return jnp.pad(r, ((0, 0), (0, D_MODEL - r.shape[1])))


def _pack_small(mix, convb, ssmg, attng, mlpg, fing, convw, dtb, alog, dsk, sinks, extra=None):
    last = [dtb, alog, dsk, sinks] + ([extra] if extra is not None else [])
    rows = [_row(mix), _row(convb), _row(ssmg, attng), _row(mlpg), _row(fing),
            jnp.pad(convw, ((0, 0), (0, D_MODEL - convw.shape[1]))), _row(*last)]
    packed = jnp.concatenate(rows, axis=0)
    return jnp.pad(packed, ((0, SMALL_ROWS - packed.shape[0]), (0, 0)))


def _unpack_small(p, conv_n):
    return dict(
        mix_norm_g=p[0:1, :], conv_b=p[1:2, :], ssm_norm_g=p[2:3, :D_INNER], attn_out_norm_g=p[2:3, D_INNER:],
        mlp_norm_g=p[3:4, :], final_norm_g=p[4, :], conv_w=p[5:9, :conv_n][None],
        dt_bias=p[9:10, 0:16], A_log=p[9:10, 16:32], D_skip=p[9:10, 32:48], attn_sinks=p[9:10, 48:64])


SMALL_NAMES = ["mix_norm_g", "conv_w", "conv_b", "dt_bias", "A_log", "D_skip", "ssm_norm_g", "attn_sinks",
               "attn_out_norm_g", "mlp_norm_g", "final_norm_g"]
WEIGHT_ORDER = ["mix_norm_g", "w_in", "conv_w", "conv_b", "dt_bias", "A_log", "D_skip", "ssm_norm_g", "attn_sinks",
                "attn_out_norm_g", "w_out", "mlp_norm_g", "w_up", "w_down", "final_norm_g"]


def _to_my_columns(w_nat):
    pad = jnp.zeros((w_nat.shape[0], NP - IN_PROJ), w_nat.dtype)
    return jnp.concatenate([w_nat[:, :NAT_DT], w_nat[:, NAT_DT + N_HEADS:], w_nat[:, NAT_DT:NAT_DT + N_HEADS], pad],
                           axis=1)


PER = IN_PROJ // N_DEV
SUPER_STEP = 544
SUPER = 576


def _natural_rows(g, lo, hi):
    segments = [(0, NAT_DT, 0), (NAT_DT, NAT_DT + N_HEADS, OFF_DT - NAT_DT), (NAT_DT + N_HEADS, IN_PROJ, -N_HEADS),
                (IN_PROJ, NP, 0)]
    pieces = [g[max(lo, a) + shift:min(hi, b) + shift] for a, b, shift in segments if max(lo, a) < min(hi, b)]
    return pieces[0] if len(pieces) == 1 else jnp.concatenate(pieces, axis=0)


def _w_in_from_super_slabs(sup):
    seam = SUPER - SUPER_STEP
    units = []
    for i in range(N_DEV):
        base = SUPER_STEP * i
        units.append((base, base + seam, sup[i, :seam] if i == 0 else sup[i - 1, SUPER_STEP:] + sup[i, :seam]))
        units.append((base + seam, base + SUPER_STEP, sup[i, seam:SUPER_STEP]))
    units.append((SUPER_STEP * N_DEV, SUPER_STEP * N_DEV + seam, sup[N_DEV - 1, SUPER_STEP:]))

    def natural(lo, hi):
        return [rows[max(lo, a) - a:min(hi, b) - a] for a, b, rows in units if max(lo, a) < min(hi, b)]

    pieces = natural(0, NAT_DT) + natural(NAT_DT + N_HEADS, IN_PROJ) + natural(NAT_DT, NAT_DT + N_HEADS)
    return jnp.concatenate(pieces + [jnp.zeros((NP - IN_PROJ, D_MODEL), sup.dtype)], axis=0)


def _to_natural_columns(w_my):
    return jnp.concatenate([w_my[:, :NAT_DT], w_my[:, OFF_DT:OFF_DT + N_HEADS], w_my[:, NAT_DT:OFF_DT]], axis=1)


SLAB = 1024


def _grad_w_up(h2, du, name, sel=None, add=None, after=None):
    T, D = h2.shape
    if sel is None:
        pick, n_slab, pre = (lambda j, *cs: j), N_DEV, None
    else:
        pre, other = sel
        pick, n_slab = (lambda j, cs: 2 * j + ((1 - cs[0]) if other else cs[0])), 4
    o_spec = pl.BlockSpec((None, SLAB, SLAB), lambda i, j, k, *cs: (j, i, 0))
    return _matmul(
        h2, du, mode="tn", grid=(D // SLAB, n_slab, 1),
        a_spec=pl.BlockSpec((T, SLAB), lambda i, j, k, *cs: (0, i)),
        b_spec=pl.BlockSpec((T, SLAB), lambda i, j, k, *cs: (0, pick(j, *cs))),
        out_shapes=[jax.ShapeDtypeStruct((n_slab, D, SLAB), BF16)], out_specs=[o_spec], tile=(SLAB, SLAB), name=name,
        extras=() if add is None else (add,), extra_specs=() if add is None else (o_spec,),
        epilogue=None if add is None else (lambda acc, r: (acc + r.astype(F32),)), after=after, prefetch=pre)[0]


def _grad_w_down(act, dx3b, name, sel=None, add=None, after=None):
    T, D = dx3b.shape
    if sel is None:
        pick, n_slab, pre = (lambda i, *cs: i), N_DEV, None
    else:
        pre, other = sel
        pick, n_slab = (lambda i, cs: 2 * i + ((1 - cs[0]) if other else cs[0])), 4
    o_spec = pl.BlockSpec((None, SLAB, SLAB), lambda i, j, k, *cs: (i, 0, j))
    return _matmul(
        act, dx3b, mode="tn", grid=(n_slab, D // SLAB, 1),
        a_spec=pl.BlockSpec((T, SLAB), lambda i, j, k, *cs: (0, pick(i, *cs))),
        b_spec=pl.BlockSpec((T, SLAB), lambda i, j, k, *cs: (0, j)),
        out_shapes=[jax.ShapeDtypeStruct((n_slab, SLAB, D), BF16)], out_specs=[o_spec], tile=(SLAB, SLAB), name=name,
        extras=() if add is None else (add,), extra_specs=() if add is None else (o_spec,),
        epilogue=None if add is None else (lambda acc, r: (acc + r.astype(F32),)), after=after, prefetch=pre)[0]


class _FixedWeights:
    def __init__(self, w_in_p, w_out_f, w_up_s, w_down_f, conv_w_f):
        self.w = (w_in_p, w_out_f, w_up_s, w_down_f, conv_w_f)
        self.grads = {}

    def mixer_weights(self, after):
        return self.w[0], self.w[4], None

    def prefetch(self, k, after):
        return None

    def out_weight(self, after):
        return self.w[1]

    def up_weight(self, after):
        return self.w[2]

    def down_weight(self, after):
        return self.w[3]

    def mlp_grads(self, h2, du, act, dx3b):
        self.grads.update(w_up=_grad_w_up(h2, du, "grad_w_up"),
                          w_down=_grad_w_down(act, dx3b, "grad_w_down").reshape(D_FF, D_MODEL))
        return None

    def out_grad(self, g_out):
        self.grads.update(w_out=g_out)
        return None

    def in_grad(self, g_in):
        self.grads.update(w_in=g_in)
        return None


def _local_step(x, tgt, p, hooks):
    T = x.shape[0]
    D = D_MODEL
    h1 = _rmsnorm_fwd(x, p["mix_norm_g"], "norm_mix")
    w_in_t, conv_w_f, token = hooks.mixer_weights(h1)
    (proj,) = _mm_simple(h1, w_in_t, mode="nt", M=T, N=NP, K=D, tm=min(T, 1024), tn=1536, tk=D, out_dtype=F32,
                         name="in_proj", after=token)
    xbc = _conv_fwd(proj, conv_w_f, p["conv_b"], "conv_fwd")
    dtT = proj[:, OFF_DT:OFF_DT + N_HEADS].T
    dtbT = p["dt_bias"].T
    alogT = p["A_log"].T
    dfull = jnp.repeat(p["D_skip"], HEAD_DIM, axis=1)
    token = hooks.prefetch("out", xbc)
    ssm_g = p["ssm_norm_g"] if token is None else p["ssm_norm_g"] + token[0:1, 0:1]
    ycat, ypre, hs = _ssd_fwd(xbc, proj, dtT, p["dt_bias"], dtbT, p["A_log"], alogT, dfull, ssm_g, "ssd_fwd")
    ycat, o_att = _attn_fwd(proj, p["attn_sinks"], p["attn_out_norm_g"], ycat, "attn_fwd")
    token = hooks.prefetch("up", ycat)
    w_out_f = hooks.out_weight(ycat if token is None else token)
    tm = min(T, 1024)
    (x2,) = _mm_simple(ycat, w_out_f, mode="nn", M=T, N=D, K=D, tm=tm, tn=1024, tk=D, out_dtype=F32, name="out_proj",
                       extras=(x,), epilogue=lambda acc, res: (acc + res,))
    h2 = _rmsnorm_fwd(x2, p["mlp_norm_g"], "norm_mlp")
    w_up_s = hooks.up_weight(h2)
    grid = (T // tm, N_DEV, 1)
    u, act = _matmul(
        h2, w_up_s, mode="nn", grid=grid,
        a_spec=pl.BlockSpec((tm, D), lambda i, j, k: (i, 0)),
        b_spec=pl.BlockSpec((None, D, 1024), lambda i, j, k: (j, 0, 0)),
        out_shapes=[jax.ShapeDtypeStruct((T, D_FF), F32), jax.ShapeDtypeStruct((T, D_FF), BF16)],
        out_specs=[pl.BlockSpec((tm, 1024), lambda i, j, k: (i, j))] * 2, tile=(tm, 1024), name="mlp_up",
        epilogue=lambda acc: (acc, jnp.square(jnp.maximum(acc, 0.0))))
    w_down_f = hooks.down_weight(act)
    (x3,) = _mm_simple(act, w_down_f, mode="nn", M=T, N=D, K=D_FF, tm=tm, tn=1024, tk=2048, out_dtype=F32,
                       name="mlp_down", extras=(x2,), epilogue=lambda acc, res: (acc + res,))
    loss_part, d_fin, dx3, dx3b = _final_loss(x3, tgt, p["final_norm_g"].reshape(1, D), "loss_head")
    (du,) = _mm_simple(dx3b, w_down_f, mode="nt", M=T, N=D_FF, K=D, tm=tm, tn=1024, tk=D, out_dtype=BF16,
                       name="mlp_down_bwd", extras=(u,),
                       epilogue=lambda acc, uu: (acc * (2.0 * jnp.maximum(uu, 0.0)),))
    token = hooks.mlp_grads(h2, du, act, dx3b)
    (dh2,) = _matmul(
        du, w_up_s, mode="nt", grid=(T // tm, D // 1024, N_DEV // 2),
        a_spec=pl.BlockSpec((tm, 2048), lambda i, j, k: (i, k)),
        b_spec=pl.BlockSpec((2, 1024, 1024), lambda i, j, k: (k, j, 0)),
        out_shapes=[jax.ShapeDtypeStruct((T, D), F32)],
        out_specs=[pl.BlockSpec((tm, 1024), lambda i, j, k: (i, j))], tile=(tm, 1024), name="mlp_up_bwd",
        after=token, dot_fn=lambda a, b: _dot_nt(a[:, :1024], b[0]) + _dot_nt(a[:, 1024:], b[1]))
    dx2, dx2b, d_mlp = _rmsnorm_bwd(dh2, x2, p["mlp_norm_g"], dx3, "norm_mlp_bwd")
    (g_out,) = _mm_simple(ycat, dx2b, mode="tn", M=D, N=D, K=T, tm=1024, tn=1024, tk=T, out_dtype=BF16,
                          name="grad_w_out")
    token = hooks.out_grad(g_out)
    (dy,) = _mm_simple(dx2b, w_out_f, mode="nt", M=T, N=D, K=D, tm=tm, tn=1024, tk=D, out_dtype=F32,
                       name="out_proj_bwd", after=token)
    dproj, dxbc_act, d_dtb, d_alog, d_dskip, d_ssmg = _ssd_bwd(
        xbc, proj, dtT, p["dt_bias"], dtbT, p["A_log"], alogT, dfull, p["ssm_norm_g"], ypre, hs, dy, "ssd_bwd")
    dproj, d_convw, d_convb = _conv_bwd(proj, dxbc_act, conv_w_f, p["conv_b"], dproj, "conv_bwd")
    dproj, dk, dv, d_sinks, d_attng = _attn_bwd(proj, p["attn_sinks"], p["attn_out_norm_g"], o_att, dy, dproj,
                                                "attn_bwd")
    dproj = lax.dynamic_update_slice(dproj, jnp.concatenate([dk, dv], axis=1).astype(BF16), (0, OFF_K))
    (g_in,) = _mm_simple(dproj, h1, mode="tn", M=NP, N=D, K=T, tm=1536, tn=1024, tk=T, out_dtype=BF16,
                         name="grad_w_in")
    token = hooks.in_grad(g_in)
    (dh1,) = _mm_simple(dproj, w_in_t, mode="nn", M=T, N=D, K=NP, tm=tm, tn=1024, tk=2304, out_dtype=F32,
                        name="in_proj_bwd", after=token)
    dx, _, d_mix = _rmsnorm_bwd(dh1, x, p["mix_norm_g"], dx2, "norm_mix_bwd")
    small = _pack_small(d_mix, d_convb, d_ssmg, d_attng, d_mlp, d_fin, d_convw, d_dtb, d_alog, d_dskip, d_sinks,
                        extra=loss_part[:, 0:1])
    return dx, small


def _landing(own, me):
    zone = lax.empty((N_DEV,) + own.shape, own.dtype)
    return lax.dynamic_update_slice(zone, own[None], (me,) + (0,) * own.ndim)


def _gather_end(started, after, plan, name):
    _, lands = _remote_wait(started, after, plan, name + "_wait")
    return _gather_finish(lands, name + "_finish")


class _ShardedWeights:
    def __init__(self, w_in, w_out, conv_w, w_up, w_down, me, csel):
        self.me, self.csel = me, csel
        own_rows = lax.dynamic_update_slice(jnp.zeros((SUPER, D_MODEL), F32), jnp.transpose(w_in), (2 * me, 0))
        shards = [own_rows.astype(BF16), conv_w]
        self.st_mixer = _remote_start(shards, [_landing(s, me) for s in shards], _gather_plan, 4 * len(shards),
                                      "gather_start_mixer")
        order = self.st_mixer[4]
        self.st_later = {}
        for k, wt in [("out", w_out), ("up", w_up), ("down", w_down)]:
            shard = (wt + order[0, 0]).astype(BF16)
            self.st_later[k] = _remote_start([shard], [_landing(shard, me)], _gather_plan, 4, f"gather_start_{k}",
                                             after=order)
            order = self.st_later[k][4]
        self.start_token = order
        self.forwards = {}
        self.reduces = {}

    def mixer_weights(self, after):
        g_in, g_conv = _gather_end(self.st_mixer, after, _gather_plan, "gather_mixer")
        conv_w_f = jnp.concatenate([g_conv[i] for i in range(N_DEV)], axis=1)
        return _w_in_from_super_slabs(g_in), conv_w_f, None

    def prefetch(self, k, after):
        _, lands = _remote_wait(self.st_later[k], after, _gather_plan, f"gather_{k}_wait")
        self.forwards[k] = _remote_start([], lands, _forward_plan, 3, f"gather_{k}_forward")
        return self.forwards[k][4]

    def _prefetched(self, k, after):
        return _remote_wait(self.forwards[k], after, _forward_plan, f"gather_{k}_forward_wait")[1][0]

    def out_weight(self, after):
        return self._prefetched("out", after).reshape(D_MODEL, D_MODEL)

    def up_weight(self, after):
        return self._prefetched("up", after)

    def down_weight(self, after):
        return _gather_end(self.st_later["down"], after, _gather_plan, "gather_down")[0].reshape(D_FF, D_MODEL)

    def _chips_start(self, slabs, from_sibling, rows, tag):
        sums = [_pair_add(s, r, self.csel, tr, f"pair_add_{tag}_{i}")
                for i, (s, r, tr) in enumerate(zip(slabs, from_sibling, rows))]
        lands = [lax.empty((3,) + s.shape[1:], s.dtype) for s in sums]
        self.reduces[tag] = _remote_start(sums, lands, _chips_plan, 3 * len(sums), f"reduce_start_{tag}")
        return self.reduces[tag][4]

    def mlp_grads(self, h2, du, act, dx3b):
        def send(part, tag, after):
            st = _remote_start([part], [lax.empty(part.shape, part.dtype)], _pair4_plan, 4,
                               f"reduce_pair_start_{tag}", after=after)
            return st

        def received(st, after, tag):
            return _remote_wait(st, after, _pair4_plan, f"reduce_pair_wait_{tag}")[1][0]

        def to_chips(sums, tag):
            self.reduces[tag] = _remote_start([sums], [lax.empty((3,) + sums.shape[1:], sums.dtype)], _chips_plan, 3,
                                              f"reduce_start_{tag}")
            return self.reduces[tag][4]

        up_send = _grad_w_up(h2, du, "grad_w_up_send", sel=(self.csel, True))
        st_up = send(up_send, "up", None)
        down_send = _grad_w_down(act, dx3b, "grad_w_down_send", sel=(self.csel, True), after=st_up[4])
        st_down = send(down_send, "down", None)
        up_sum = _grad_w_up(h2, du, "grad_w_up_keep", sel=(self.csel, False), add=received(st_up, down_send, "up"),
                            after=st_down[4])
        token = to_chips(up_sum, "up")
        down_sum = _grad_w_down(act, dx3b, "grad_w_down_keep", sel=(self.csel, False),
                                add=received(st_down, up_sum, "down"), after=token)
        return to_chips(down_sum, "down")

    def out_grad(self, g_out):
        slabs = [g_out.reshape(N_DEV, D_MODEL // N_DEV, D_MODEL)]
        return self._chips_start(slabs, _exchange_pair(slabs, "reduce_pair_out"), [256], "out")

    def in_grad(self, g_in):
        slabs = [jnp.stack([_natural_rows(g_in, SUPER_STEP * j, SUPER_STEP * j + SUPER) for j in range(N_DEV)])]
        return self._chips_start(slabs, _exchange_pair(slabs, "reduce_pair_in"), [SUPER], "in")

    def small_start(self, small):
        self.st_small = _remote_start([small], [_landing(small, self.me)], _everyone_plan, N_DEV - 1, "gather_start_small")

    def small_end(self, after):
        return _remote_wait(self.st_small, after, _everyone_plan, "gather_small_wait")[1][0]

    def reduce_end(self, tag, after):
        return _remote_wait(self.reduces[tag], after, _chips_plan, f"reduce_wait_{tag}")


def kernel(x, mix_norm_g, w_in, conv_w, conv_b, dt_bias, A_log, D_skip, ssm_norm_g, attn_sinks, attn_out_norm_g, w_out, mlp_norm_g, w_up, w_down, final_norm_g, loss_target, m_mix_norm_g, m_w_in, m_conv_w, m_conv_b, m_dt_bias, m_A_log, m_D_skip, m_ssm_norm_g, m_attn_sinks, m_attn_out_norm_g, m_w_out, m_mlp_norm_g, m_w_up, m_w_down, m_final_norm_g, v_mix_norm_g, v_w_in, v_conv_w, v_conv_b, v_dt_bias, v_A_log, v_D_skip, v_ssm_norm_g, v_attn_sinks, v_attn_out_norm_g, v_w_out, v_mlp_norm_g, v_w_up, v_w_down, v_final_norm_g):
    xi, yi, ci = _coords()
    me = 4 * xi + 2 * yi + ci
    csel = jnp.reshape(ci, (1,)).astype(jnp.int32)
    qsel = jnp.reshape(2 * xi + yi, (1,)).astype(jnp.int32)
    w = dict(mix_norm_g=mix_norm_g, conv_b=conv_b, dt_bias=dt_bias, A_log=A_log, D_skip=D_skip,
             ssm_norm_g=ssm_norm_g, attn_sinks=attn_sinks, attn_out_norm_g=attn_out_norm_g, mlp_norm_g=mlp_norm_g,
             final_norm_g=final_norm_g)
    hooks = _ShardedWeights(w_in[0], w_out[0], conv_w[0], w_up[0], w_down[0], me, csel)
    p = dict(w, mix_norm_g=mix_norm_g + hooks.start_token[0:1, 0:1])
    dx, small = _local_step(x[0], loss_target[0], p, hooks)
    hooks.small_start(small)
    big = {}
    after = dx
    for name, wt, mt, vt, tile in [
            ("up", w_up, m_w_up, v_w_up, (512, SLAB)), ("down", w_down, m_w_down, v_w_down, (256, D_MODEL)),
            ("out", w_out, m_w_out, v_w_out, (256, D_MODEL))]:
        (chip_sums,), (from_chips,) = hooks.reduce_end(name, after)
        res = _adamw_big(wt[0], mt[0], vt[0], chip_sums, from_chips, qsel, tile, f"adamw_w_{name}")
        big["w_" + name] = tuple(r[None] for r in res)
        after = res[0]
    (chip_sums,), (from_chips,) = hooks.reduce_end("in", after)
    g_super = _sum_partials(chip_sums, from_chips, qsel, 512, "grad_w_in_sum")
    g_in = lax.dynamic_slice(g_super, (2 * me, 0), (PER, D_MODEL))
    res = _adamw_tiled(jnp.transpose(w_in[0]), g_in, jnp.transpose(m_w_in[0]), jnp.transpose(v_w_in[0]), 512,
                       "adamw_w_in")
    big["w_in"] = tuple(jnp.transpose(r)[None] for r in (g_in, *res))
    after = res[0]
    gsum = _small_sum(hooks.small_end(after), "small_sum")
    loss = gsum[9, 64]
    gs = _unpack_small(gsum, CONV_DIM)
    cw = CONV_DIM // N_DEV
    g_conv_shard = lax.dynamic_slice(gsum[5:9, :], (0, me * cw), (CONV_K, cw))

    def pack(s):
        return _pack_small(s["mix_norm_g"], s["conv_b"], s["ssm_norm_g"], s["attn_out_norm_g"], s["mlp_norm_g"],
                           s["final_norm_g"], s["conv_w"][0], s["dt_bias"], s["A_log"], s["D_skip"], s["attn_sinks"])

    wp = pack(dict(w, conv_w=conv_w))
    mp = pack(dict(mix_norm_g=m_mix_norm_g, conv_b=m_conv_b, ssm_norm_g=m_ssm_norm_g,
                   attn_out_norm_g=m_attn_out_norm_g, mlp_norm_g=m_mlp_norm_g, final_norm_g=m_final_norm_g,
                   conv_w=m_conv_w, dt_bias=m_dt_bias, A_log=m_A_log, D_skip=m_D_skip, attn_sinks=m_attn_sinks))
    vp = pack(dict(mix_norm_g=v_mix_norm_g, conv_b=v_conv_b, ssm_norm_g=v_ssm_norm_g,
                   attn_out_norm_g=v_attn_out_norm_g, mlp_norm_g=v_mlp_norm_g, final_norm_g=v_final_norm_g,
                   conv_w=v_conv_w, dt_bias=v_dt_bias, A_log=v_A_log, D_skip=v_D_skip, attn_sinks=v_attn_sinks))
    gp = jnp.concatenate([gsum[0:5], jnp.pad(g_conv_shard, ((0, 0), (0, D_MODEL - cw))), gsum[9:10],
                          jnp.zeros((SMALL_ROWS - 10, D_MODEL), F32)], axis=0)
    dp, mnp, vnp = _adamw_small(wp, gp, mp, vp, "adamw_small")
    grads = dict(gs, conv_w=g_conv_shard[None])
    deltas = _unpack_small(dp, cw)
    new_m = _unpack_small(mnp, cw)
    new_v = _unpack_small(vnp, cw)
    for k, name in enumerate(["w_in", "w_out", "w_up", "w_down"]):
        grads[name], deltas[name], new_m[name], new_v[name] = big[name]
    return (loss, dx[None], *[grads[n] for n in WEIGHT_ORDER], *[deltas[n] for n in WEIGHT_ORDER],
            *[new_m[n] for n in WEIGHT_ORDER], *[new_v[n] for n in WEIGHT_ORDER])
```

```python
import functools

import jax
import jax.numpy as jnp
from jax import lax
from jax.experimental import pallas as pl
from jax.experimental.pallas import tpu as pltpu

F32 = jnp.float32
BF16 = jnp.bfloat16
HI = lax.Precision.HIGHEST
MESH = pl.DeviceIdType.MESH

EPS = 1e-5
D_MODEL = 2048
D_INNER = 1024
N_HEADS = 16
HEAD_DIM = 64
N_GROUPS = 4
D_STATE = 128
CHUNK = 128
CONV_K = 4
CONV_DIM = 2048
ATTN_W = 1024
KV_W = 128
WINDOW = 128
D_FF = 8192
IN_PROJ = 4368
N_DEV = 8
NP = 4608
OFF_Z, OFF_X, OFF_B, OFF_C, OFF_Q, OFF_K, OFF_V, OFF_DT = 0, 1024, 2048, 2560, 3072, 4096, 4224, 4352
NAT_DT = 3072

ADAM_LR = 0.001
ADAM_B1 = 0.9
ADAM_B2 = 0.999
ADAM_EPS = 1e-08
ADAM_WD = 0.01
ADAM_STEP = 10

VMEM_LIMIT = 52 * 1024 * 1024
SMALL_ROWS = 16
NEG = -1e30


def _cparams(sem=None):
    return pltpu.CompilerParams(dimension_semantics=sem, vmem_limit_bytes=VMEM_LIMIT)


def _split3(v):
    hi = v.astype(BF16)
    rest = v - hi.astype(F32)
    mid = rest.astype(BF16)
    return hi, mid, (rest - mid.astype(F32)).astype(BF16)


def _hdot(a, b, data):
    if data == "a":
        sel = b.astype(BF16)
        return sum(_dot_nn(part, sel) for part in _split3(a))
    sel = a.astype(BF16)
    return sum(_dot_nn(sel, part) for part in _split3(b))


def _dot_nn(a, b):
    return lax.dot_general(a, b, (((1,), (0,)), ((), ())), preferred_element_type=F32)


def _dot_nt(a, b):
    return lax.dot_general(a, b, (((1,), (1,)), ((), ())), preferred_element_type=F32)


def _dot_tn(a, b):
    return lax.dot_general(a, b, (((0,), (0,)), ((), ())), preferred_element_type=F32)


def _softplus(v):
    return jnp.maximum(v, 0.0) + jnp.log1p(jnp.exp(-jnp.abs(v)))


def _sigmoid(v):
    return 1.0 / (1.0 + jnp.exp(-v))


def _matmul(a, b, *, mode, grid, a_spec, b_spec, out_shapes, out_specs, tile, name,
            extras=(), extra_specs=(), epilogue=None, after=None, dot_fn=None, prefetch=None):
    nk = grid[2]
    n_ex = len(extras)
    n_out = len(out_shapes)
    dot = dot_fn if dot_fn is not None else {"nn": _dot_nn, "nt": _dot_nt, "tn": _dot_tn}[mode]

    def finish(acc, ex_refs, out_refs):
        res = (acc,) if epilogue is None else epilogue(acc, *[e[...] for e in ex_refs])
        for o, r in zip(out_refs, res):
            o[...] = r.astype(o.dtype)

    def body(*refs):
        a_ref, b_ref = refs[0], refs[1]
        ex_refs = refs[2:2 + n_ex]
        out_refs = refs[2 + n_ex:2 + n_ex + n_out]
        part = dot(a_ref[...].astype(BF16), b_ref[...].astype(BF16))
        if nk == 1:
            finish(part, ex_refs, out_refs)
        else:
            acc_ref = refs[-1]
            k = pl.program_id(2)

            @pl.when(k == 0)
            def _():
                acc_ref[...] = part

            @pl.when(k > 0)
            def _():
                acc_ref[...] += part

            @pl.when(k == nk - 1)
            def _():
                finish(acc_ref[...], ex_refs, out_refs)

    scratch = [] if nk == 1 else [pltpu.VMEM(tile, F32)]
    n_pre = 0 if prefetch is None else 1
    tok_specs = [] if after is None else [pl.BlockSpec((8, 128), lambda *_: (0, 0))]
    tok_args = [] if after is None else [after]

    def body_with_token(*refs):
        refs = refs[n_pre:]
        body(*refs[:2 + n_ex], *refs[2 + n_ex + len(tok_args):])

    in_specs = [a_spec, b_spec, *extra_specs, *tok_specs]
    params = _cparams(("parallel", "parallel", "arbitrary"))
    if prefetch is None:
        return pl.pallas_call(
            body_with_token, grid=grid, in_specs=in_specs, out_specs=list(out_specs), out_shape=list(out_shapes),
            scratch_shapes=scratch, name=name, compiler_params=params)(a, b, *extras, *tok_args)
    return pl.pallas_call(
        body_with_token,
        grid_spec=pltpu.PrefetchScalarGridSpec(num_scalar_prefetch=1, grid=grid, in_specs=in_specs,
                                               out_specs=list(out_specs), scratch_shapes=scratch),
        out_shape=list(out_shapes), name=name, compiler_params=params)(prefetch, a, b, *extras, *tok_args)


def _mm_simple(a, b, *, mode, M, N, K, tm, tn, tk, out_dtype, name, extras=(), epilogue=None, n_out=1,
               out_dtypes=None, after=None):
    grid = (M // tm, N // tn, K // tk)
    if mode == "nn":
        a_spec = pl.BlockSpec((tm, tk), lambda i, j, k: (i, k))
        b_spec = pl.BlockSpec((tk, tn), lambda i, j, k: (k, j))
    elif mode == "nt":
        a_spec = pl.BlockSpec((tm, tk), lambda i, j, k: (i, k))
        b_spec = pl.BlockSpec((tn, tk), lambda i, j, k: (j, k))
    else:
        a_spec = pl.BlockSpec((tk, tm), lambda i, j, k: (k, i))
        b_spec = pl.BlockSpec((tk, tn), lambda i, j, k: (k, j))
    o_spec = pl.BlockSpec((tm, tn), lambda i, j, k: (i, j))
    dts = out_dtypes if out_dtypes is not None else [out_dtype] * n_out
    return _matmul(a, b, mode=mode, grid=grid, a_spec=a_spec, b_spec=b_spec,
                   out_shapes=[jax.ShapeDtypeStruct((M, N), d) for d in dts],
                   out_specs=[o_spec] * len(dts), tile=(tm, tn), name=name,
                   extras=extras, extra_specs=[o_spec] * len(extras), epilogue=epilogue, after=after)


ROW_BLOCK = 256


def _rmsnorm_fwd(x, g, name):
    T, D = x.shape

    def body(x_ref, g_ref, o_ref):
        xf = x_ref[...]
        r = lax.rsqrt(jnp.mean(xf * xf, axis=-1, keepdims=True) + EPS)
        o_ref[...] = (xf * r * g_ref[...]).astype(BF16)

    return pl.pallas_call(
        body, grid=(T // ROW_BLOCK,),
        in_specs=[pl.BlockSpec((ROW_BLOCK, D), lambda i: (i, 0)), pl.BlockSpec((1, D), lambda i: (0, 0))],
        out_specs=pl.BlockSpec((ROW_BLOCK, D), lambda i: (i, 0)),
        out_shape=jax.ShapeDtypeStruct((T, D), BF16), name=name, compiler_params=_cparams(("parallel",)),
    )(x, g)


def _rmsnorm_bwd(dh, x, g, dres, name):
    T, D = x.shape

    def body(dh_ref, x_ref, g_ref, dres_ref, dx_ref, dxb_ref, dg_ref):
        i = pl.program_id(0)
        xf = x_ref[...]
        r = lax.rsqrt(jnp.mean(xf * xf, axis=-1, keepdims=True) + EPS)
        xh = xf * r
        d = dh_ref[...]

        @pl.when(i == 0)
        def _():
            dg_ref[...] = jnp.zeros_like(dg_ref)

        dg_ref[...] += jnp.sum(d * xh, axis=0, keepdims=True)
        dxh = d * g_ref[...]
        dx = r * (dxh - xh * jnp.mean(dxh * xh, axis=-1, keepdims=True)) + dres_ref[...]
        dx_ref[...] = dx
        dxb_ref[...] = dx.astype(BF16)

    row = pl.BlockSpec((ROW_BLOCK, D), lambda i: (i, 0))
    vec = pl.BlockSpec((1, D), lambda i: (0, 0))
    return pl.pallas_call(
        body, grid=(T // ROW_BLOCK,), in_specs=[row, row, vec, row], out_specs=[row, row, vec],
        out_shape=[jax.ShapeDtypeStruct((T, D), F32), jax.ShapeDtypeStruct((T, D), BF16),
                   jax.ShapeDtypeStruct((1, D), F32)],
        name=name, compiler_params=_cparams(("arbitrary",)),
    )(dh, x, g, dres)


def _final_loss(x3, tgt, g, name):
    T, D = x3.shape

    def body(x_ref, t_ref, g_ref, loss_ref, dg_ref, dx_ref, dxb_ref):
        i = pl.program_id(0)
        xf = x_ref[...]
        r = lax.rsqrt(jnp.mean(xf * xf, axis=-1, keepdims=True) + EPS)
        xh = xf * r
        gg = g_ref[...]
        err = xh * gg - t_ref[...]

        @pl.when(i == 0)
        def _():
            dg_ref[...] = jnp.zeros_like(dg_ref)
            loss_ref[...] = jnp.zeros_like(loss_ref)

        part = jnp.sum(jnp.sum(err * err, axis=-1, keepdims=True), axis=0, keepdims=True) * (0.5 / D)
        loss_ref[...] += jnp.broadcast_to(part, loss_ref.shape)
        dout = err * (1.0 / D)
        dg_ref[...] += jnp.sum(dout * xh, axis=0, keepdims=True)
        dxh = dout * gg
        dx = r * (dxh - xh * jnp.mean(dxh * xh, axis=-1, keepdims=True))
        dx_ref[...] = dx
        dxb_ref[...] = dx.astype(BF16)

    row = pl.BlockSpec((ROW_BLOCK, D), lambda i: (i, 0))
    vec = pl.BlockSpec((1, D), lambda i: (0, 0))
    return pl.pallas_call(
        body, grid=(T // ROW_BLOCK,), in_specs=[row, row, vec],
        out_specs=[pl.BlockSpec((1, 128), lambda i: (0, 0)), vec, row, row],
        out_shape=[jax.ShapeDtypeStruct((1, 128), F32), jax.ShapeDtypeStruct((1, D), F32),
                   jax.ShapeDtypeStruct((T, D), F32), jax.ShapeDtypeStruct((T, D), BF16)],
        name=name, compiler_params=_cparams(("arbitrary",)),
    )(x3, tgt, g)


CONV_BLOCK = 256


def _conv_apply(u, w, b):
    row = lax.broadcasted_iota(jnp.int32, u.shape, 0)
    acc = b + w[CONV_K - 1:CONV_K, :] * u
    shifted = []
    for j in range(1, CONV_K):
        uj = jnp.where(row >= j, pltpu.roll(u, j, axis=0), 0.0)
        shifted.append(uj)
        acc = acc + w[CONV_K - 1 - j:CONV_K - j, :] * uj
    return acc, shifted


def _conv_fwd(proj, conv_w, conv_b, name):
    T = proj.shape[0]
    cb0 = OFF_X // CONV_BLOCK

    def body(u_ref, w_ref, b_ref, o_ref):
        c, _ = _conv_apply(u_ref[...], w_ref[...], b_ref[...])
        o_ref[...] = c * _sigmoid(c)

    return pl.pallas_call(
        body, grid=(CONV_DIM // CONV_BLOCK,),
        in_specs=[pl.BlockSpec((T, CONV_BLOCK), lambda j: (0, cb0 + j)),
                  pl.BlockSpec((CONV_K, CONV_BLOCK), lambda j: (0, j)),
                  pl.BlockSpec((1, CONV_BLOCK), lambda j: (0, j))],
        out_specs=pl.BlockSpec((T, CONV_BLOCK), lambda j: (0, j)),
        out_shape=jax.ShapeDtypeStruct((T, CONV_DIM), F32), name=name, compiler_params=_cparams(("parallel",)),
    )(proj, conv_w, conv_b)


def _conv_bwd(proj, dact, conv_w, conv_b, dproj, name):
    T = proj.shape[0]
    cb0 = OFF_X // CONV_BLOCK

    def body(u_ref, d_ref, w_ref, b_ref, _, du_ref, dw_ref, db_ref):
        u = u_ref[...]
        w = w_ref[...]
        c, shifted = _conv_apply(u, w, b_ref[...])
        sg = _sigmoid(c)
        dc = d_ref[...] * sg * (1.0 + c * (1.0 - sg))
        row = lax.broadcasted_iota(jnp.int32, u.shape, 0)
        du = w[CONV_K - 1:CONV_K, :] * dc
        dw_ref[CONV_K - 1:CONV_K, :] = jnp.sum(dc * u, axis=0, keepdims=True)
        for j in range(1, CONV_K):
            dcj = jnp.where(row < T - j, pltpu.roll(dc, T - j, axis=0), 0.0)
            du = du + w[CONV_K - 1 - j:CONV_K - j, :] * dcj
            dw_ref[CONV_K - 1 - j:CONV_K - j, :] = jnp.sum(dc * shifted[j - 1], axis=0, keepdims=True)
        db_ref[...] = jnp.sum(dc, axis=0, keepdims=True)
        du_ref[...] = du.astype(BF16)

    return pl.pallas_call(
        body, grid=(CONV_DIM // CONV_BLOCK,),
        in_specs=[pl.BlockSpec((T, CONV_BLOCK), lambda j: (0, cb0 + j)),
                  pl.BlockSpec((T, CONV_BLOCK), lambda j: (0, j)),
                  pl.BlockSpec((CONV_K, CONV_BLOCK), lambda j: (0, j)),
                  pl.BlockSpec((1, CONV_BLOCK), lambda j: (0, j)), pl.BlockSpec(memory_space=pl.ANY)],
        out_specs=[pl.BlockSpec((T, CONV_BLOCK), lambda j: (0, cb0 + j)),
                   pl.BlockSpec((CONV_K, CONV_BLOCK), lambda j: (0, j)),
                   pl.BlockSpec((1, CONV_BLOCK), lambda j: (0, j))],
        out_shape=[jax.ShapeDtypeStruct(dproj.shape, BF16), jax.ShapeDtypeStruct((CONV_K, CONV_DIM), F32),
                   jax.ShapeDtypeStruct((1, CONV_DIM), F32)],
        input_output_aliases={4: 0}, name=name, compiler_params=_cparams(("parallel",)),
    )(proj, dact, conv_w, conv_b, dproj)


GROUP_W = D_INNER // N_GROUPS
HEADS_PER_GROUP = N_HEADS // N_GROUPS


def _expand_mat():
    h = lax.broadcasted_iota(jnp.int32, (N_HEADS, D_INNER), 0)
    j = lax.broadcasted_iota(jnp.int32, (N_HEADS, D_INNER), 1)
    return (j // HEAD_DIM == h).astype(F32)


def _reduce_mat(g):
    j = lax.broadcasted_iota(jnp.int32, (GROUP_W, N_HEADS), 0)
    h = lax.broadcasted_iota(jnp.int32, (GROUP_W, N_HEADS), 1)
    return (g * HEADS_PER_GROUP + j // HEAD_DIM == h).astype(F32)


def _col16(v, h):
    lane = lax.broadcasted_iota(jnp.int32, v.shape, 1)
    return jnp.sum(jnp.where(lane == h, v, 0.0), axis=1, keepdims=True)


def _ssd_pre(dt_raw, dtT_raw, dtb, dtbT, alog, alogT):
    Q = CHUNK
    xdt = dt_raw + dtb
    dt = _softplus(xdt)
    dtT = _softplus(dtT_raw + dtbT)
    A = -jnp.exp(alog)
    AT = -jnp.exp(alogT)
    row = lax.broadcasted_iota(jnp.int32, (Q, Q), 0)
    col = lax.broadcasted_iota(jnp.int32, (Q, Q), 1)
    tril = (row >= col).astype(F32)
    triu = (row <= col).astype(F32)
    cs = _hdot(tril, dt * A, "b")
    csT = _hdot(dtT * AT, triu, "a")
    return xdt, dt, A, cs, csT, row >= col, triu


def _decay_matrix(cs, csT, h, causal):
    seg = _col16(cs, h) - csT[h:h + 1, :]
    return jnp.where(causal, jnp.exp(jnp.minimum(seg, 0.0)), 0.0)


def _ssd_in_specs(nc, rev):
    def cidx(c):
        return (nc - 1 - c) if rev else c

    return [
        pl.BlockSpec((CHUNK, D_INNER), lambda c: (cidx(c), 0)),
        pl.BlockSpec((CHUNK, 512), lambda c: (cidx(c), 2)),
        pl.BlockSpec((CHUNK, 512), lambda c: (cidx(c), 3)),
        pl.BlockSpec((CHUNK, D_INNER), lambda c: (cidx(c), 0)),
        pl.BlockSpec((CHUNK, 128), lambda c: (cidx(c), OFF_DT // 128)),
        pl.BlockSpec((N_HEADS, CHUNK), lambda c: (0, cidx(c))),
        pl.BlockSpec((1, N_HEADS), lambda c: (0, 0)),
        pl.BlockSpec((N_HEADS, 1), lambda c: (0, 0)),
        pl.BlockSpec((1, N_HEADS), lambda c: (0, 0)),
        pl.BlockSpec((N_HEADS, 1), lambda c: (0, 0)),
        pl.BlockSpec((1, D_INNER), lambda c: (0, 0)),
        pl.BlockSpec((1, D_INNER), lambda c: (0, 0)),
    ]


def _ssd_fwd(xbc, proj, dtT, dtb, dtbT, alog, alogT, dfull, ng, name):
    T = xbc.shape[0]
    nc = T // CHUNK
    Q = CHUNK

    def body(xs_ref, B_ref, C_ref, z_ref, dt_ref, dtT_ref, dtb_ref, dtbT_ref, al_ref, alT_ref, df_ref, ng_ref,
             y_ref, ypre_ref, hs_ref, h_scr):
        c = pl.program_id(0)

        @pl.when(c == 0)
        def _():
            h_scr[...] = jnp.zeros_like(h_scr)

        _, dt, _, cs, csT, causal, _ = _ssd_pre(dt_ref[:, :N_HEADS], dtT_ref[...], dtb_ref[...], dtbT_ref[...],
                                                al_ref[...], alT_ref[...])
        ex = _expand_mat()
        dt_full = _hdot(dt, ex, "a")
        cs_full = _hdot(cs, ex, "a")
        cs_last = cs_full[Q - 1:Q, :]
        xs = xs_ref[...]
        xd = xs * dt_full
        e_full = jnp.exp(cs_full)
        dec_full = jnp.exp(cs_last - cs_full)
        cd_full = jnp.exp(cs_last)
        lane_head = lax.broadcasted_iota(jnp.int32, (1, GROUP_W), 1) // HEAD_DIM
        for g in range(N_GROUPS):
            sl = slice(g * GROUP_W, (g + 1) * GROUP_W)
            Bg = B_ref[:, g * D_STATE:(g + 1) * D_STATE].astype(BF16)
            Cg = C_ref[:, g * D_STATE:(g + 1) * D_STATE].astype(BF16)
            CB = _dot_nt(Cg, Bg)
            hg = h_scr[g]
            yoff = _dot_nn(Cg, hg.astype(BF16)) * e_full[:, sl]
            xd_g = xd[:, sl]
            S = _dot_tn(Bg, (xd_g * dec_full[:, sl]).astype(BF16))
            xd_b = xd_g.astype(BF16)
            ydiag = jnp.zeros((Q, GROUP_W), F32)
            for r in range(HEADS_PER_GROUP):
                Lm = _decay_matrix(cs, csT, g * HEADS_PER_GROUP + r, causal)
                Gm = (CB * Lm).astype(BF16)
                ydiag = ydiag + _dot_nn(Gm, jnp.where(lane_head == r, xd_b, jnp.zeros_like(xd_b)))
            hs_ref[0, g] = hg
            h_scr[g] = hg * cd_full[:, sl] + S
            ypre = ydiag + yoff + xs[:, sl] * df_ref[:, sl]
            ypre_ref[:, sl] = ypre
            zg = z_ref[:, sl]
            yz = ypre * zg * _sigmoid(zg)
            rn = lax.rsqrt(jnp.mean(yz * yz, axis=-1, keepdims=True) + EPS)
            y_ref[:, sl] = (yz * rn * ng_ref[:, sl]).astype(BF16)

    return pl.pallas_call(
        body, grid=(nc,), in_specs=_ssd_in_specs(nc, False),
        out_specs=[pl.BlockSpec((CHUNK, D_INNER), lambda c: (c, 0)),
                   pl.BlockSpec((CHUNK, D_INNER), lambda c: (c, 0)),
                   pl.BlockSpec((1, N_GROUPS, D_STATE, GROUP_W), lambda c: (c, 0, 0, 0))],
        out_shape=[jax.ShapeDtypeStruct((T, D_INNER + ATTN_W), BF16), jax.ShapeDtypeStruct((T, D_INNER), F32),
                   jax.ShapeDtypeStruct((nc, N_GROUPS, D_STATE, GROUP_W), F32)],
        scratch_shapes=[pltpu.VMEM((N_GROUPS, D_STATE, GROUP_W), F32)],
        name=name, compiler_params=_cparams(("arbitrary",)),
    )(xbc, xbc, xbc, proj, proj, dtT, dtb, dtbT, alog, alogT, dfull, ng)


def _ssd_bwd(xbc, proj, dtT, dtb, dtbT, alog, alogT, dfull, ng, ypre, hs, dy, name):
    T = xbc.shape[0]
    nc = T // CHUNK
    Q = CHUNK

    def body(xs_ref, B_ref, C_ref, z_ref, dt_ref, dtT_ref, dtb_ref, dtbT_ref, al_ref, alT_ref, df_ref, ng_ref,
             ypre_ref, hs_ref, dy_ref,
             dz_ref, dxbc_ref, ddtb_ref, dal_ref, dD_ref, dng_ref, dh_scr):
        step = pl.program_id(0)

        @pl.when(step == 0)
        def _():
            dh_scr[...] = jnp.zeros_like(dh_scr)
            ddtb_ref[...] = jnp.zeros_like(ddtb_ref)
            dal_ref[...] = jnp.zeros_like(dal_ref)
            dD_ref[...] = jnp.zeros_like(dD_ref)
            dng_ref[...] = jnp.zeros_like(dng_ref)

        xdt, dt, A, cs, csT, causal, triu = _ssd_pre(dt_ref[:, :N_HEADS], dtT_ref[...], dtb_ref[...],
                                                    dtbT_ref[...], al_ref[...], alT_ref[...])
        ex = _expand_mat()
        dt_full = _hdot(dt, ex, "a")
        cs_full = _hdot(cs, ex, "a")
        cs_last = cs_full[Q - 1:Q, :]
        xs = xs_ref[...]
        xd = xs * dt_full
        e_full = jnp.exp(cs_full)
        dec_full = jnp.exp(cs_last - cs_full)
        cd_full = jnp.exp(cs_last)
        lane_head = lax.broadcasted_iota(jnp.int32, (1, GROUP_W), 1) // HEAD_DIM
        is_last = lax.broadcasted_iota(jnp.int32, (Q, 1), 0) == Q - 1
        dcs16 = jnp.zeros((Q, N_HEADS), F32)
        ddtx16 = jnp.zeros((Q, N_HEADS), F32)
        dD16 = jnp.zeros((8, N_HEADS), F32)
        lane16 = lax.broadcasted_iota(jnp.int32, (1, N_HEADS), 1)
        sub16 = lax.broadcasted_iota(jnp.int32, (N_HEADS, 1), 0)
        col_sums = jnp.zeros((N_HEADS, Q), F32)
        for g in range(N_GROUPS):
            sl = slice(g * GROUP_W, (g + 1) * GROUP_W)
            red = _reduce_mat(g)
            ypre_g = ypre_ref[:, sl]
            zg = z_ref[:, sl]
            sg = _sigmoid(zg)
            silu = zg * sg
            yz = ypre_g * silu
            rn = lax.rsqrt(jnp.mean(yz * yz, axis=-1, keepdims=True) + EPS)
            yh = yz * rn
            dy_g = dy_ref[:, sl]
            dng_ref[:, sl] += jnp.sum(dy_g * yh, axis=0, keepdims=True)
            dyh = dy_g * ng_ref[:, sl]
            dyz = rn * (dyh - yh * jnp.mean(dyh * yh, axis=-1, keepdims=True))
            dY = dyz * silu
            dz_ref[:, sl] = (dyz * ypre_g * sg * (1.0 + zg * (1.0 - sg))).astype(BF16)
            xs_g = xs[:, sl]
            xd_g = xd[:, sl]
            dec_g = dec_full[:, sl]
            cd_g = cd_full[:, sl]
            d_g = df_ref[:, sl]
            Bg = B_ref[:, g * D_STATE:(g + 1) * D_STATE].astype(BF16)
            Cg = C_ref[:, g * D_STATE:(g + 1) * D_STATE].astype(BF16)
            CB = _dot_nt(Cg, Bg)
            hg = hs_ref[0, g]
            hgb = hg.astype(BF16)
            yoff = _dot_nn(Cg, hgb) * e_full[:, sl]
            dhn = dh_scr[g]
            dhnb = dhn.astype(BF16)
            dYE = (dY * e_full[:, sl]).astype(BF16)
            dC = _dot_nt(dYE, hgb)
            dh_direct = _dot_tn(Cg, dYE)
            dXdd = _dot_nn(Bg, dhnb)
            dB = _dot_nt((xd_g * dec_g).astype(BF16), dhnb)
            dcd = jnp.sum(dhn * hg, axis=0, keepdims=True)
            dh_scr[g] = dh_direct + cd_g * dhn
            dYb = dY.astype(BF16)
            xd_b = xd_g.astype(BF16)
            dCB = jnp.zeros((Q, Q), F32)
            dXd = dXdd * dec_g
            for r in range(HEADS_PER_GROUP):
                h = g * HEADS_PER_GROUP + r
                Lm = _decay_matrix(cs, csT, h, causal)
                Gf = CB * Lm
                dYr = jnp.where(lane_head == r, dYb, jnp.zeros_like(dYb))
                dG = _dot_nt(dYr, xd_b)
                dCB = dCB + dG * Lm
                dXd = dXd + _dot_tn(Gf.astype(BF16), dYr)
                Mm = dG * Gf
                dcs16 = dcs16 + jnp.where(lane16 == h, jnp.sum(Mm, axis=1, keepdims=True), 0.0)
                col_sums = col_sums + jnp.where(sub16 == h, jnp.sum(Mm, axis=0, keepdims=True), 0.0)
            dCBb = dCB.astype(BF16)
            dC = dC + _dot_nn(dCBb, Bg)
            dB = dB + _dot_tn(dCBb, Cg)
            w_state = dXdd * dec_g * xd_g
            t_last = jnp.sum(w_state, axis=0, keepdims=True) + dcd * cd_g
            dcs_g = dY * yoff - w_state + jnp.where(is_last, t_last, 0.0)
            dcs16 = dcs16 + _hdot(dcs_g, red, "a")
            ddtx16 = ddtx16 + _hdot(dXd * xs_g, red, "a")
            dD16 = dD16 + _hdot(jnp.broadcast_to(jnp.sum(dY * xs_g, axis=0, keepdims=True), (8, GROUP_W)), red, "a")
            dxbc_ref[:, sl] = dXd * dt_full[:, sl] + dY * d_g
            dxbc_ref[:, D_INNER + g * D_STATE:D_INNER + (g + 1) * D_STATE] = dB
            dxbc_ref[:, D_INNER + 512 + g * D_STATE:D_INNER + 512 + (g + 1) * D_STATE] = dC
        eye = (lax.broadcasted_iota(jnp.int32, (N_HEADS, N_HEADS), 0)
               == lax.broadcasted_iota(jnp.int32, (N_HEADS, N_HEADS), 1)).astype(BF16)
        dcs16 = dcs16 - sum(_dot_tn(part, eye) for part in _split3(col_sums))
        da = _hdot(triu, dcs16, "b")
        ddt = da * A + ddtx16
        ddt_raw = ddt * _sigmoid(xdt)
        pr = lax.broadcasted_iota(jnp.int32, (N_HEADS, 128), 0)
        pc = lax.broadcasted_iota(jnp.int32, (N_HEADS, 128), 1)
        dz_ref[:, D_INNER:OFF_DT] = jnp.zeros((Q, OFF_DT - D_INNER), BF16)
        dz_ref[:, OFF_DT:OFF_DT + 128] = _hdot(ddt_raw, (pr == pc).astype(F32), "a").astype(BF16)
        dz_ref[:, OFF_DT + 128:] = jnp.zeros((Q, NP - OFF_DT - 128), BF16)
        ddtb_ref[...] += jnp.sum(ddt_raw, axis=0, keepdims=True)
        dal_ref[...] += jnp.sum(da * dt, axis=0, keepdims=True) * A
        dD_ref[...] += dD16[0:1, :]

    def rc(c):
        return nc - 1 - c

    in_specs = _ssd_in_specs(nc, True) + [
        pl.BlockSpec((CHUNK, D_INNER), lambda c: (rc(c), 0)),
        pl.BlockSpec((1, N_GROUPS, D_STATE, GROUP_W), lambda c: (rc(c), 0, 0, 0)),
        pl.BlockSpec((CHUNK, D_INNER), lambda c: (rc(c), 0)),
    ]
    small = pl.BlockSpec((1, N_HEADS), lambda c: (0, 0))
    return pl.pallas_call(
        body, grid=(nc,), in_specs=in_specs,
        out_specs=[pl.BlockSpec((CHUNK, NP), lambda c: (rc(c), 0)),
                   pl.BlockSpec((CHUNK, CONV_DIM), lambda c: (rc(c), 0)),
                   small, small, small,
                   pl.BlockSpec((1, D_INNER), lambda c: (0, 0))],
        out_shape=[jax.ShapeDtypeStruct((T, NP), BF16), jax.ShapeDtypeStruct((T, CONV_DIM), F32),
                   jax.ShapeDtypeStruct((1, N_HEADS), F32), jax.ShapeDtypeStruct((1, N_HEADS), F32),
                   jax.ShapeDtypeStruct((1, N_HEADS), F32), jax.ShapeDtypeStruct((1, D_INNER), F32)],
        scratch_shapes=[pltpu.VMEM((N_GROUPS, D_STATE, GROUP_W), F32)],
        name=name, compiler_params=_cparams(("arbitrary",)),
    )(xbc, xbc, xbc, proj, proj, dtT, dtb, dtbT, alog, alogT, dfull, ng, ypre, hs, dy)


N_PAIRS = ATTN_W // 128
PAIRS_PER_KV = N_PAIRS // 2
ATTN_SCALE = HEAD_DIM ** -0.5


def _kv_variants(kk):
    lo = lax.broadcasted_iota(jnp.int32, kk.shape, 1) < HEAD_DIM
    zero = jnp.zeros_like(kk)
    k00 = jnp.where(lo, kk, zero)
    k11 = jnp.where(lo, zero, kk)
    k01 = pltpu.roll(k00, HEAD_DIM, axis=1)
    k10 = pltpu.roll(k11, HEAD_DIM, axis=1)
    return [[k00.astype(BF16), k01.astype(BF16)], [k10.astype(BF16), k11.astype(BF16)]]


LOG2E = 1.4426950408889634


def _own_block():
    i = lax.broadcasted_iota(jnp.int32, (WINDOW, WINDOW), 0)
    j = lax.broadcasted_iota(jnp.int32, (WINDOW, WINDOW), 1)
    return j <= i


def _fold(own, a):
    return jnp.where(own, a[:, WINDOW:], a[:, :WINDOW])


def _attn_probs(qp, kvar, own, prev_bias, sk):
    s = _dot_nt(qp, kvar)
    sb = jnp.where(own, s[:, WINDOW:], s[:, :WINDOW] + prev_bias) * (ATTN_SCALE * LOG2E)
    sk2 = sk * LOG2E
    m = jnp.maximum(jnp.max(sb, axis=1, keepdims=True), sk2)
    pe = jnp.exp2(sb - m)
    es = jnp.exp2(sk2 - m)
    den = jnp.sum(pe, axis=1, keepdims=True) + es
    inv = 1.0 / den
    return pe * inv, es * inv


def _unfold(own, a):
    zero = jnp.zeros_like(a)
    return jnp.where(own, zero, a), jnp.where(own, a, zero)


def _sink(sinks, r):
    lane = lax.broadcasted_iota(jnp.int32, sinks.shape, 1)
    return jnp.sum(jnp.where(lane == r, sinks, 0.0), axis=1, keepdims=True)


def _kv_specs():
    return [pl.BlockSpec((WINDOW, KV_W), lambda n: (jnp.maximum(n - 1, 0), OFF_K // KV_W)),
            pl.BlockSpec((WINDOW, KV_W), lambda n: (n, OFF_K // KV_W)),
            pl.BlockSpec((WINDOW, KV_W), lambda n: (jnp.maximum(n - 1, 0), OFF_V // KV_W)),
            pl.BlockSpec((WINDOW, KV_W), lambda n: (n, OFF_V // KV_W))]


def _attn_fwd(proj, sinks, og, ycat, name):
    T = proj.shape[0]
    nb = T // WINDOW

    def body(q_ref, kp_ref, kc_ref, vp_ref, vc_ref, s_ref, og_ref, _, y_ref, o_ref):
        n = pl.program_id(0)
        kv = _kv_variants(jnp.concatenate([kp_ref[...], kc_ref[...]], axis=0))
        vv = _kv_variants(jnp.concatenate([vp_ref[...], vc_ref[...]], axis=0))
        own = _own_block()
        prev_bias = jnp.where(n > 0, 0.0, NEG)
        sinks_v = s_ref[...]
        ssq = jnp.zeros((WINDOW, 1), F32)
        for p in range(N_PAIRS):
            j = p // PAIRS_PER_KV
            qp = q_ref[:, p * 128:(p + 1) * 128].astype(BF16)
            o_pair = jnp.zeros((WINDOW, 128), F32)
            for par in range(2):
                pn, _ = _attn_probs(qp, kv[j][par], own, prev_bias, _sink(sinks_v, 2 * p + par))
                p_prev, p_own = _unfold(own, pn.astype(BF16))
                o_pair = o_pair + _dot_nn(p_prev, vv[j][par][:WINDOW]) + _dot_nn(p_own, vv[j][par][WINDOW:])
            o_ref[:, p * 128:(p + 1) * 128] = o_pair
            ssq = ssq + jnp.sum(o_pair * o_pair, axis=1, keepdims=True)
        rn = lax.rsqrt(ssq * (1.0 / ATTN_W) + EPS)
        y_ref[...] = (o_ref[...] * rn * og_ref[...]).astype(BF16)

    return pl.pallas_call(
        body, grid=(nb,),
        in_specs=[pl.BlockSpec((WINDOW, ATTN_W), lambda n: (n, OFF_Q // ATTN_W)), *_kv_specs(),
                  pl.BlockSpec((1, N_HEADS), lambda n: (0, 0)), pl.BlockSpec((1, ATTN_W), lambda n: (0, 0)), ANY],
        out_specs=[pl.BlockSpec((WINDOW, ATTN_W), lambda n: (n, 1)), pl.BlockSpec((WINDOW, ATTN_W), lambda n: (n, 0))],
        out_shape=[jax.ShapeDtypeStruct(ycat.shape, BF16), jax.ShapeDtypeStruct((T, ATTN_W), F32)],
        input_output_aliases={7: 0}, name=name, compiler_params=_cparams(("parallel",)),
    )(proj, proj, proj, proj, proj, sinks, og, ycat)


def _attn_bwd(proj, sinks, og, o, dy, dproj, name):
    T = proj.shape[0]
    nb = T // WINDOW

    def body(q_ref, kp_ref, kc_ref, vp_ref, vc_ref, s_ref, og_ref, o_ref, dy_ref, _,
             dq_ref, dk_ref, dv_ref, ds_ref, dog_ref):
        n = pl.program_id(0)

        @pl.when(n == 0)
        def _():
            dk_ref[...] = jnp.zeros_like(dk_ref)
            dv_ref[...] = jnp.zeros_like(dv_ref)
            ds_ref[...] = jnp.zeros_like(ds_ref)
            dog_ref[...] = jnp.zeros_like(dog_ref)

        kv = _kv_variants(jnp.concatenate([kp_ref[...], kc_ref[...]], axis=0))
        vv = _kv_variants(jnp.concatenate([vp_ref[...], vc_ref[...]], axis=0))
        own = _own_block()
        prev_bias = jnp.where(n > 0, 0.0, NEG)
        sinks_v = s_ref[...]
        of = o_ref[...]
        rn = lax.rsqrt(jnp.mean(of * of, axis=-1, keepdims=True) + EPS)
        oh = of * rn
        dyf = dy_ref[...]
        dog_ref[...] += jnp.sum(dyf * oh, axis=0, keepdims=True)
        doh = dyf * og_ref[...]
        do = rn * (doh - oh * jnp.mean(doh * oh, axis=-1, keepdims=True))
        lane = lax.broadcasted_iota(jnp.int32, (1, 128), 1)
        lane16 = lax.broadcasted_iota(jnp.int32, (1, N_HEADS), 1)
        sub = lax.broadcasted_iota(jnp.int32, (128, 1), 0)
        dk_acc = [[[jnp.zeros((128, WINDOW), F32) for _ in range(2)] for _ in range(2)] for _ in range(2)]
        dv_acc = [[[jnp.zeros((128, WINDOW), F32) for _ in range(2)] for _ in range(2)] for _ in range(2)]
        dsink = jnp.zeros((1, N_HEADS), F32)
        for p in range(N_PAIRS):
            j = p // PAIRS_PER_KV
            q_f = q_ref[:, p * 128:(p + 1) * 128]
            qp = q_f.astype(BF16)
            q_t = q_f.T.astype(BF16)
            do_p = do[:, p * 128:(p + 1) * 128]
            o_p = of[:, p * 128:(p + 1) * 128]
            do_b = do_p.astype(BF16)
            do_t = do_p.T.astype(BF16)
            prod = do_p * o_p
            dq_pair = jnp.zeros((WINDOW, 128), F32)
            for par in range(2):
                r = 2 * p + par
                half = (lane < HEAD_DIM) if par == 0 else (lane >= HEAD_DIM)
                rows_half = (sub < HEAD_DIM) if par == 0 else (sub >= HEAD_DIM)
                pn, ps = _attn_probs(qp, kv[j][par], own, prev_bias, _sink(sinks_v, r))
                delta = jnp.sum(jnp.where(half, prod, 0.0), axis=1, keepdims=True)
                dP = _fold(own, _dot_nt(do_b, vv[j][par]))
                dS = pn * (dP - delta)
                dsink = dsink + jnp.where(lane16 == r, -jnp.sum(ps * delta, axis=0, keepdims=True), 0.0)
                dS_parts = _unfold(own, (dS * ATTN_SCALE).astype(BF16))
                p_parts = _unfold(own, pn.astype(BF16))
                q_th = jnp.where(rows_half, q_t, jnp.zeros_like(q_t))
                do_th = jnp.where(rows_half, do_t, jnp.zeros_like(do_t))
                for blk in range(2):
                    dq_pair = dq_pair + _dot_nn(dS_parts[blk], kv[j][par][blk * WINDOW:(blk + 1) * WINDOW])
                    dk_acc[j][par][blk] = dk_acc[j][par][blk] + _dot_nn(q_th, dS_parts[blk])
                    dv_acc[j][par][blk] = dv_acc[j][par][blk] + _dot_nn(do_th, p_parts[blk])
            dq_ref[:, p * 128:(p + 1) * 128] = dq_pair.astype(BF16)
        rows = [pl.multiple_of(jnp.maximum(n - 1, 0) * WINDOW, WINDOW), pl.multiple_of(n * WINDOW, WINDOW)]
        for acc, ref in [(dk_acc, dk_ref), (dv_acc, dv_ref)]:
            for blk in range(2):
                both_t = (acc[0][0][blk] + pltpu.roll(acc[0][1][blk], HEAD_DIM, axis=0)
                          + acc[1][1][blk] + pltpu.roll(acc[1][0][blk], HEAD_DIM, axis=0))
                ref[pl.ds(rows[blk], WINDOW), :] += both_t.T
        ds_ref[...] += dsink

    full_kv = pl.BlockSpec((T, KV_W), lambda n: (0, 0))
    blk = pl.BlockSpec((WINDOW, ATTN_W), lambda n: (n, 0))
    return pl.pallas_call(
        body, grid=(nb,),
        in_specs=[pl.BlockSpec((WINDOW, ATTN_W), lambda n: (n, OFF_Q // ATTN_W)), *_kv_specs(),
                  pl.BlockSpec((1, N_HEADS), lambda n: (0, 0)), pl.BlockSpec((1, ATTN_W), lambda n: (0, 0)),
                  blk, pl.BlockSpec((WINDOW, ATTN_W), lambda n: (n, 1)), ANY],
        out_specs=[pl.BlockSpec((WINDOW, ATTN_W), lambda n: (n, OFF_Q // ATTN_W)), full_kv, full_kv,
                   pl.BlockSpec((1, N_HEADS), lambda n: (0, 0)), pl.BlockSpec((1, ATTN_W), lambda n: (0, 0))],
        out_shape=[jax.ShapeDtypeStruct(dproj.shape, BF16), jax.ShapeDtypeStruct((T, KV_W), F32),
                   jax.ShapeDtypeStruct((T, KV_W), F32), jax.ShapeDtypeStruct((1, N_HEADS), F32),
                   jax.ShapeDtypeStruct((1, ATTN_W), F32)],
        input_output_aliases={9: 0}, name=name, compiler_params=_cparams(("arbitrary",)),
    )(proj, proj, proj, proj, proj, sinks, og, o, dy, dproj)


ANY = pl.BlockSpec(memory_space=pl.ANY)


def _coords():
    return lax.axis_index("x"), lax.axis_index("y"), lax.axis_index("c")


def _all_gather(arrs, name):
    n = len(arrs)

    def body(*refs):
        ins, outs = refs[:n], refs[n:2 * n]
        send_sems, recv_sems, local_sems = refs[2 * n:]
        x, y, c = _coords()
        me = 4 * x + 2 * y + c
        sibling = (x, y, 1 - c)
        chips = [(1 - x, y), (x, 1 - y), (1 - x, 1 - y)]

        def copy(a, k, block, to, src=None):
            dst = outs[a].at[block]
            return pltpu.make_async_remote_copy(
                src_ref=dst if src is None else src, dst_ref=dst, send_sem=send_sems.at[a, k],
                recv_sem=recv_sems.at[a, k], device_id=to, device_id_type=MESH)

        mine = [pltpu.make_async_copy(ins[a], outs[a].at[me], local_sems.at[a]) for a in range(n)]
        for cp in mine:
            cp.start()
        first = []
        for a in range(n):
            first.append(copy(a, 0, me, sibling, src=ins[a]))
            for j, chip in enumerate(chips):
                first.append(copy(a, 1 + j, me, (*chip, c), src=ins[a]))
        for cp in first:
            cp.start()
        passed = []
        for j, (px, py) in enumerate(chips):
            blk = 4 * px + 2 * py + c
            for a in range(n):
                copy(a, 1 + j, blk, sibling).wait_recv()
                fwd = copy(a, 4 + j, blk, sibling)
                fwd.start()
                passed.append(fwd)
        for a in range(n):
            copy(a, 0, 4 * x + 2 * y + (1 - c), sibling).wait_recv()
            for j, (px, py) in enumerate(chips):
                copy(a, 4 + j, 4 * px + 2 * py + (1 - c), sibling).wait_recv()
        for cp in first + passed:
            cp.wait_send()
        for cp in mine:
            cp.wait()

    return pl.pallas_call(
        body, in_specs=[ANY] * n, out_specs=[ANY] * n,
        out_shape=[jax.ShapeDtypeStruct((N_DEV,) + a.shape, a.dtype) for a in arrs],
        scratch_shapes=[pltpu.SemaphoreType.DMA((n, 7)), pltpu.SemaphoreType.DMA((n, 7)),
                        pltpu.SemaphoreType.DMA((n,))],
        name=name,
    )(*arrs)


def _exchange_pair(arrs, name):
    n = len(arrs)

    def body(*refs):
        ins, outs = refs[:n], refs[n:2 * n]
        send_sems, recv_sems = refs[2 * n:]
        x, y, c = _coords()
        cps = []
        for a in range(n):
            for q in range(4):
                cps.append(pltpu.make_async_remote_copy(
                    src_ref=ins[a].at[2 * q + (1 - c)], dst_ref=outs[a].at[q], send_sem=send_sems.at[a, q],
                    recv_sem=recv_sems.at[a, q], device_id=(x, y, 1 - c), device_id_type=MESH))
        for cp in cps:
            cp.start()
        for cp in cps:
            cp.wait()

    return pl.pallas_call(
        body, in_specs=[ANY] * n, out_specs=[ANY] * n,
        out_shape=[jax.ShapeDtypeStruct((4,) + a.shape[1:], a.dtype) for a in arrs],
        scratch_shapes=[pltpu.SemaphoreType.DMA((n, 4)), pltpu.SemaphoreType.DMA((n, 4))],
        name=name,
    )(*arrs)


def _exchange_chips(arrs, name):
    n = len(arrs)

    def body(*refs):
        ins, outs = refs[:n], refs[n:2 * n]
        send_sems, recv_sems = refs[2 * n:]
        x, y, c = _coords()
        chips = [(1 - x, y), (x, 1 - y), (1 - x, 1 - y)]
        cps = []
        for a in range(n):
            for k, (tx, ty) in enumerate(chips):
                cps.append(pltpu.make_async_remote_copy(
                    src_ref=ins[a].at[2 * tx + ty], dst_ref=outs[a].at[k], send_sem=send_sems.at[a, k],
                    recv_sem=recv_sems.at[a, k], device_id=(tx, ty, c), device_id_type=MESH))
        for cp in cps:
            cp.start()
        for cp in cps:
            cp.wait()

    return pl.pallas_call(
        body, in_specs=[ANY] * n, out_specs=[ANY] * n,
        out_shape=[jax.ShapeDtypeStruct((3,) + a.shape[1:], a.dtype) for a in arrs],
        scratch_shapes=[pltpu.SemaphoreType.DMA((n, 3)), pltpu.SemaphoreType.DMA((n, 3))],
        name=name,
    )(*arrs)


HBM = pl.BlockSpec(memory_space=pltpu.HBM)
SEM = pl.BlockSpec(memory_space=pltpu.SEMAPHORE)
EFFECT = pltpu.SideEffectType.DATAFLOW_SIDE_EFFECTING


def _in_hbm(a):
    return pltpu.with_memory_space_constraint(a, pltpu.HBM)


def _remote_start(srcs, lands, plan, n_copies, name, after=None):
    ns, nb = len(srcs), len(srcs) + len(lands)
    n_after = 0 if after is None else 1

    def body(*refs):
        src_refs, land_refs = refs[:ns], refs[ns:nb]
        send_sems, recv_sems = refs[nb + n_after], refs[nb + n_after + 1]
        token = refs[-1]
        x, y, c = _coords()
        for i, (sv, dv, dev) in enumerate(plan(src_refs, land_refs, x, y, c)):
            pltpu.make_async_remote_copy(src_ref=sv, dst_ref=dv, send_sem=send_sems.at[i], recv_sem=recv_sems.at[i],
                                         device_id=dev, device_id_type=MESH).start()
        token[...] = jnp.zeros_like(token)

    bufs = list(srcs) + list(lands)
    outs = pl.pallas_call(
        body, name=name,
        out_shape=(pltpu.SemaphoreType.DMA((n_copies,)), pltpu.SemaphoreType.DMA((n_copies,)),
                   *[pltpu.HBM(b.shape, b.dtype) for b in bufs], jax.ShapeDtypeStruct((8, 128), F32)),
        in_specs=[HBM] * nb + [ANY] * n_after,
        out_specs=(SEM, SEM, *[HBM] * nb, pl.BlockSpec(memory_space=pltpu.VMEM)),
        input_output_aliases={i: 2 + i for i in range(nb)},
        compiler_params=pltpu.CompilerParams(has_side_effects=EFFECT),
    )(*[_in_hbm(b) for b in bufs], *([] if after is None else [after]))
    return outs[0], outs[1], list(outs[2:2 + ns]), list(outs[2 + ns:2 + nb]), outs[-1]


def _remote_wait(started, after, plan, name):
    send_sems, recv_sems, srcs, lands, _ = started
    ns, nb = len(srcs), len(srcs) + len(lands)

    def body(*refs):
        src_refs, land_refs = refs[:ns], refs[ns:nb]
        send_sems, recv_sems = refs[nb], refs[nb + 1]
        x, y, c = _coords()
        for i, (sv, dv, dev) in enumerate(plan(src_refs, land_refs, x, y, c)):
            cp = pltpu.make_async_remote_copy(src_ref=sv, dst_ref=dv, send_sem=send_sems.at[i],
                                              recv_sem=recv_sems.at[i], device_id=dev, device_id_type=MESH)
            cp.wait_send()
            cp.wait_recv()

    bufs = list(srcs) + list(lands)
    outs = pl.pallas_call(
        body, name=name, out_shape=tuple(pltpu.HBM(b.shape, b.dtype) for b in bufs),
        in_specs=[HBM] * nb + [SEM, SEM, ANY], out_specs=tuple([HBM] * nb),
        input_output_aliases={i: i for i in range(nb)},
        compiler_params=pltpu.CompilerParams(has_side_effects=EFFECT),
    )(*bufs, send_sems, recv_sems, after)
    return list(outs[:ns]), list(outs[ns:])


def _gather_plan(src_refs, land_refs, x, y, c):
    me = 4 * x + 2 * y + c
    plan = []
    for s, l in zip(src_refs, land_refs):
        for dev in [(x, y, 1 - c), (1 - x, y, c), (x, 1 - y, c), (1 - x, 1 - y, c)]:
            plan.append((s, l.at[me], dev))
    return plan


def _forward_plan(src_refs, land_refs, x, y, c):
    plan = []
    for l in land_refs:
        for px, py in [(1 - x, y), (x, 1 - y), (1 - x, 1 - y)]:
            blk = l.at[4 * px + 2 * py + c]
            plan.append((blk, blk, (x, y, 1 - c)))
    return plan


def _pair_plan(src_refs, land_refs, x, y, c):
    plan = []
    for s, l in zip(src_refs, land_refs):
        for q in range(4):
            plan.append((s.at[2 * q + (1 - c)], l.at[q], (x, y, 1 - c)))
    return plan


def _pair4_plan(src_refs, land_refs, x, y, c):
    plan = []
    for s, l in zip(src_refs, land_refs):
        for q in range(4):
            plan.append((s.at[q], l.at[q], (x, y, 1 - c)))
    return plan


def _chips_plan(src_refs, land_refs, x, y, c):
    plan = []
    for s, l in zip(src_refs, land_refs):
        for k, (tx, ty) in enumerate([(1 - x, y), (x, 1 - y), (1 - x, 1 - y)]):
            plan.append((s.at[2 * tx + ty], l.at[k], (tx, ty, c)))
    return plan


def _everyone_plan(src_refs, land_refs, x, y, c):
    me = 4 * x + 2 * y + c
    plan = []
    for s, l in zip(src_refs, land_refs):
        for fx, fy, fc in [(0, 0, 1), (1, 0, 0), (1, 0, 1), (0, 1, 0), (0, 1, 1), (1, 1, 0), (1, 1, 1)]:
            dev = ((1 - x) if fx else x, (1 - y) if fy else y, (1 - c) if fc else c)
            plan.append((s, l.at[me], dev))
    return plan


def _gather_finish(gathered, name):
    n = len(gathered)

    def body(*refs):
        outs = refs[n:2 * n]
        send_sems, recv_sems = refs[2 * n:]
        x, y, c = _coords()
        cps = []
        for a in range(n):
            for j, (px, py) in enumerate([(1 - x, y), (x, 1 - y), (1 - x, 1 - y)]):
                blk = outs[a].at[4 * px + 2 * py + c]
                got = outs[a].at[4 * px + 2 * py + (1 - c)]
                cps.append((pltpu.make_async_remote_copy(
                    src_ref=blk, dst_ref=blk, send_sem=send_sems.at[a, j], recv_sem=recv_sems.at[a, j],
                    device_id=(x, y, 1 - c), device_id_type=MESH), pltpu.make_async_remote_copy(
                    src_ref=got, dst_ref=got, send_sem=send_sems.at[a, j], recv_sem=recv_sems.at[a, j],
                    device_id=(x, y, 1 - c), device_id_type=MESH)))
        for cp, _ in cps:
            cp.start()
        for cp, arrival in cps:
            cp.wait_send()
            arrival.wait_recv()

    return pl.pallas_call(
        body, in_specs=[ANY] * n, out_specs=[ANY] * n,
        out_shape=[jax.ShapeDtypeStruct(g.shape, g.dtype) for g in gathered],
        input_output_aliases={a: a for a in range(n)},
        scratch_shapes=[pltpu.SemaphoreType.DMA((n, 3)), pltpu.SemaphoreType.DMA((n, 3))],
        name=name,
    )(*gathered)


def _pair_add(g8, r1, csel, tr, name):
    _, R, C = r1.shape
    g4 = g8.reshape(4, 2, R, C)

    def body(c_ref, g_ref, r_ref, o_ref):
        o_ref[...] = (g_ref[...].astype(F32) + r_ref[...].astype(F32)).astype(BF16)

    return pl.pallas_call(
        body,
        grid_spec=pltpu.PrefetchScalarGridSpec(
            num_scalar_prefetch=1, grid=(4, R // tr),
            in_specs=[pl.BlockSpec((None, None, tr, C), lambda q, i, cs: (q, cs[0], i, 0)),
                      pl.BlockSpec((None, tr, C), lambda q, i, cs: (q, i, 0))],
            out_specs=pl.BlockSpec((None, tr, C), lambda q, i, cs: (q, i, 0))),
        out_shape=jax.ShapeDtypeStruct((4, R, C), BF16), name=name,
        compiler_params=_cparams(("parallel", "parallel")),
    )(csel, g4, r1)


def _adamw_math(w, g, m, v):
    m = ADAM_B1 * m + (1.0 - ADAM_B1) * g
    v = ADAM_B2 * v + (1.0 - ADAM_B2) * (g * g)
    m_hat = m / (1.0 - ADAM_B1 ** ADAM_STEP)
    v_hat = v / (1.0 - ADAM_B2 ** ADAM_STEP)
    delta = -ADAM_LR * (m_hat / (jnp.sqrt(v_hat) + ADAM_EPS) + ADAM_WD * w)
    return delta, m, v


def _adamw_big(w, m, v, p4, r3, qsel, tile, name):
    R, C = w.shape
    tr, tc = tile

    def body(q_ref, w_ref, m_ref, v_ref, p_ref, r_ref, g_out, d_out, m_out, v_out):
        g = p_ref[...].astype(F32) + r_ref[0].astype(F32) + r_ref[1].astype(F32) + r_ref[2].astype(F32)
        d, mn, vn = _adamw_math(w_ref[...], g, m_ref[...], v_ref[...])
        g_out[...] = g
        d_out[...] = d
        m_out[...] = mn
        v_out[...] = vn

    blk = pl.BlockSpec((tr, tc), lambda i, j, qs: (i, j))
    return pl.pallas_call(
        body,
        grid_spec=pltpu.PrefetchScalarGridSpec(
            num_scalar_prefetch=1, grid=(R // tr, C // tc),
            in_specs=[blk, blk, blk, pl.BlockSpec((None, tr, tc), lambda i, j, qs: (qs[0], i, j)),
                      pl.BlockSpec((3, tr, tc), lambda i, j, qs: (0, i, j))],
            out_specs=[blk, blk, blk, blk]),
        out_shape=[jax.ShapeDtypeStruct((R, C), F32)] * 4, name=name,
        compiler_params=_cparams(("parallel", "parallel")),
    )(qsel, w, m, v, p4, r3)


def _sum_partials(p4, r3, qsel, tc, name):
    _, R, C = p4.shape

    def body(q_ref, p_ref, r_ref, o_ref):
        o_ref[...] = p_ref[...].astype(F32) + r_ref[0].astype(F32) + r_ref[1].astype(F32) + r_ref[2].astype(F32)

    return pl.pallas_call(
        body,
        grid_spec=pltpu.PrefetchScalarGridSpec(
            num_scalar_prefetch=1, grid=(C // tc,),
            in_specs=[pl.BlockSpec((None, R, tc), lambda j, qs: (qs[0], 0, j)),
                      pl.BlockSpec((3, R, tc), lambda j, qs: (0, 0, j))],
            out_specs=pl.BlockSpec((R, tc), lambda j, qs: (0, j))),
        out_shape=jax.ShapeDtypeStruct((R, C), F32), name=name, compiler_params=_cparams(("parallel",)),
    )(qsel, p4, r3)


def _adamw_tiled(w, g, m, v, tc, name):
    R, C = w.shape

    def body(w_ref, g_ref, m_ref, v_ref, d_out, m_out, v_out):
        d, mn, vn = _adamw_math(w_ref[...], g_ref[...], m_ref[...], v_ref[...])
        d_out[...] = d
        m_out[...] = mn
        v_out[...] = vn

    blk = pl.BlockSpec((R, tc), lambda j: (0, j))
    return pl.pallas_call(
        body, grid=(C // tc,), in_specs=[blk] * 4, out_specs=[blk] * 3,
        out_shape=[jax.ShapeDtypeStruct((R, C), F32)] * 3, name=name, compiler_params=_cparams(("parallel",)),
    )(w, g, m, v)


def _small_sum(parts, name):
    def body(p_ref, o_ref):
        acc = p_ref[0]
        for d in range(1, N_DEV):
            acc = acc + p_ref[d]
        o_ref[...] = acc

    return pl.pallas_call(
        body, out_shape=jax.ShapeDtypeStruct(parts.shape[1:], F32), name=name,
        compiler_params=_cparams(),
    )(parts)


def _adamw_small(w, g, m, v, name):
    def body(w_ref, g_ref, m_ref, v_ref, d_out, m_out, v_out):
        d, mn, vn = _adamw_math(w_ref[...], g_ref[...], m_ref[...], v_ref[...])
        d_out[...] = d
        m_out[...] = mn
        v_out[...] = vn

    return pl.pallas_call(
        body, out_shape=[jax.ShapeDtypeStruct(w.shape, F32)] * 3, name=name, compiler_params=_cparams(),
    )(w, g, m, v)


def _row(*pieces):
    r = jnp.concatenate([p.reshape(1, -1) for p in pieces], axis=1)
    return jnp.pad(r, ((0, 0), (0, D_MODEL - r.shape[1])))


def _pack_small(mix, convb, ssmg, attng, mlpg, fing, convw, dtb, alog, dsk, sinks, extra=None):
    last = [dtb, alog, dsk, sinks] + ([extra] if extra is not None else [])
    rows = [_row(mix), _row(convb), _row(ssmg, attng), _row(mlpg), _row(fing),
            jnp.pad(convw, ((0, 0), (0, D_MODEL - convw.shape[1]))), _row(*last)]
    packed = jnp.concatenate(rows, axis=0)
    return jnp.pad(packed, ((0, SMALL_ROWS - packed.shape[0]), (0, 0)))


def _unpack_small(p, conv_n):
    return dict(
        mix_norm_g=p[0:1, :], conv_b=p[1:2, :], ssm_norm_g=p[2:3, :D_INNER], attn_out_norm_g=p[2:3, D_INNER:],
        mlp_norm_g=p[3:4, :], final_norm_g=p[4, :], conv_w=p[5:9, :conv_n][None],
        dt_bias=p[9:10, 0:16], A_log=p[9:10, 16:32], D_skip=p[9:10, 32:48], attn_sinks=p[9:10, 48:64])


SMALL_NAMES = ["mix_norm_g", "conv_w", "conv_b", "dt_bias", "A_log", "D_skip", "ssm_norm_g", "attn_sinks",
               "attn_out_norm_g", "mlp_norm_g", "final_norm_g"]
WEIGHT_ORDER = ["mix_norm_g", "w_in", "conv_w", "conv_b", "dt_bias", "A_log", "D_skip", "ssm_norm_g", "attn_sinks",
                "attn_out_norm_g", "w_out", "mlp_norm_g", "w_up", "w_down", "final_norm_g"]


def _to_my_columns(w_nat):
    pad = jnp.zeros((w_nat.shape[0], NP - IN_PROJ), w_nat.dtype)
    return jnp.concatenate([w_nat[:, :NAT_DT], w_nat[:, NAT_DT + N_HEADS:], w_nat[:, NAT_DT:NAT_DT + N_HEADS], pad],
                           axis=1)


PER = IN_PROJ // N_DEV
SUPER_STEP = 544
SUPER = 576


def _natural_rows(g, lo, hi):
    segments = [(0, NAT_DT, 0), (NAT_DT, NAT_DT + N_HEADS, OFF_DT - NAT_DT), (NAT_DT + N_HEADS, IN_PROJ, -N_HEADS),
                (IN_PROJ, NP, 0)]
    pieces = [g[max(lo, a) + shift:min(hi, b) + shift] for a, b, shift in segments if max(lo, a) < min(hi, b)]
    return pieces[0] if len(pieces) == 1 else jnp.concatenate(pieces, axis=0)


def _w_in_from_super_slabs(sup):
    seam = SUPER - SUPER_STEP
    units = []
    for i in range(N_DEV):
        base = SUPER_STEP * i
        units.append((base, base + seam, sup[i, :seam] if i == 0 else sup[i - 1, SUPER_STEP:] + sup[i, :seam]))
        units.append((base + seam, base + SUPER_STEP, sup[i, seam:SUPER_STEP]))
    units.append((SUPER_STEP * N_DEV, SUPER_STEP * N_DEV + seam, sup[N_DEV - 1, SUPER_STEP:]))

    def natural(lo, hi):
        return [rows[max(lo, a) - a:min(hi, b) - a] for a, b, rows in units if max(lo, a) < min(hi, b)]

    pieces = natural(0, NAT_DT) + natural(NAT_DT + N_HEADS, IN_PROJ) + natural(NAT_DT, NAT_DT + N_HEADS)
    return jnp.concatenate(pieces + [jnp.zeros((NP - IN_PROJ, D_MODEL), sup.dtype)], axis=0)


def _to_natural_columns(w_my):
    return jnp.concatenate([w_my[:, :NAT_DT], w_my[:, OFF_DT:OFF_DT + N_HEADS], w_my[:, NAT_DT:OFF_DT]], axis=1)


SLAB = 1024


def _grad_w_up(h2, du, name, sel=None, add=None, after=None):
    T, D = h2.shape
    if sel is None:
        pick, n_slab, pre = (lambda j, *cs: j), N_DEV, None
    else:
        pre, other = sel
        pick, n_slab = (lambda j, cs: 2 * j + ((1 - cs[0]) if other else cs[0])), 4
    o_spec = pl.BlockSpec((None, SLAB, SLAB), lambda i, j, k, *cs: (j, i, 0))
    return _matmul(
        h2, du, mode="tn", grid=(D // SLAB, n_slab, 1),
        a_spec=pl.BlockSpec((T, SLAB), lambda i, j, k, *cs: (0, i)),
        b_spec=pl.BlockSpec((T, SLAB), lambda i, j, k, *cs: (0, pick(j, *cs))),
        out_shapes=[jax.ShapeDtypeStruct((n_slab, D, SLAB), BF16)], out_specs=[o_spec], tile=(SLAB, SLAB), name=name,
        extras=() if add is None else (add,), extra_specs=() if add is None else (o_spec,),
        epilogue=None if add is None else (lambda acc, r: (acc + r.astype(F32),)), after=after, prefetch=pre)[0]


def _grad_w_down(act, dx3b, name, sel=None, add=None, after=None):
    T, D = dx3b.shape
    if sel is None:
        pick, n_slab, pre = (lambda i, *cs: i), N_DEV, None
    else:
        pre, other = sel
        pick, n_slab = (lambda i, cs: 2 * i + ((1 - cs[0]) if other else cs[0])), 4
    o_spec = pl.BlockSpec((None, SLAB, SLAB), lambda i, j, k, *cs: (i, 0, j))
    return _matmul(
        act, dx3b, mode="tn", grid=(n_slab, D // SLAB, 1),
        a_spec=pl.BlockSpec((T, SLAB), lambda i, j, k, *cs: (0, pick(i, *cs))),
        b_spec=pl.BlockSpec((T, SLAB), lambda i, j, k, *cs: (0, j)),
        out_shapes=[jax.ShapeDtypeStruct((n_slab, SLAB, D), BF16)], out_specs=[o_spec], tile=(SLAB, SLAB), name=name,
        extras=() if add is None else (add,), extra_specs=() if add is None else (o_spec,),
        epilogue=None if add is None else (lambda acc, r: (acc + r.astype(F32),)), after=after, prefetch=pre)[0]


class _FixedWeights:
    def __init__(self, w_in_p, w_out_f, w_up_s, w_down_f, conv_w_f):
        self.w = (w_in_p, w_out_f, w_up_s, w_down_f, conv_w_f)
        self.grads = {}

    def mixer_weights(self, after):
        return self.w[0], self.w[4], None

    def prefetch(self, k, after):
        return None

    def out_weight(self, after):
        return self.w[1]

    def up_weight(self, after):
        return self.w[2]

    def down_weight(self, after):
        return self.w[3]

    def mlp_grads(self, h2, du, act, dx3b):
        self.grads.update(w_up=_grad_w_up(h2, du, "grad_w_up"),
                          w_down=_grad_w_down(act, dx3b, "grad_w_down").reshape(D_FF, D_MODEL))
        return None

    def out_grad(self, g_out):
        self.grads.update(w_out=g_out)
        return None

    def in_grad(self, g_in):
        self.grads.update(w_in=g_in)
        return None


def _local_step(x, tgt, p, hooks):
    T = x.shape[0]
    D = D_MODEL
    h1 = _rmsnorm_fwd(x, p["mix_norm_g"], "norm_mix")
    w_in_t, conv_w_f, token = hooks.mixer_weights(h1)
    (proj,) = _mm_simple(h1, w_in_t, mode="nt", M=T, N=NP, K=D, tm=min(T, 1024), tn=1536, tk=D, out_dtype=F32,
                         name="in_proj", after=token)
    xbc = _conv_fwd(proj, conv_w_f, p["conv_b"], "conv_fwd")
    dtT = proj[:, OFF_DT:OFF_DT + N_HEADS].T
    dtbT = p["dt_bias"].T
    alogT = p["A_log"].T
    dfull = jnp.repeat(p["D_skip"], HEAD_DIM, axis=1)
    token = hooks.prefetch("out", xbc)
    ssm_g = p["ssm_norm_g"] if token is None else p["ssm_norm_g"] + token[0:1, 0:1]
    ycat, ypre, hs = _ssd_fwd(xbc, proj, dtT, p["dt_bias"], dtbT, p["A_log"], alogT, dfull, ssm_g, "ssd_fwd")
    ycat, o_att = _attn_fwd(proj, p["attn_sinks"], p["attn_out_norm_g"], ycat, "attn_fwd")
    token = hooks.prefetch("up", ycat)
    w_out_f = hooks.out_weight(ycat if token is None else token)
    tm = min(T, 1024)
    (x2,) = _mm_simple(ycat, w_out_f, mode="nn", M=T, N=D, K=D, tm=tm, tn=1024, tk=D, out_dtype=F32, name="out_proj",
                       extras=(x,), epilogue=lambda acc, res: (acc + res,))
    h2 = _rmsnorm_fwd(x2, p["mlp_norm_g"], "norm_mlp")
    w_up_s = hooks.up_weight(h2)
    grid = (T // tm, N_DEV, 1)
    u, act = _matmul(
        h2, w_up_s, mode="nn", grid=grid,
        a_spec=pl.BlockSpec((tm, D), lambda i, j, k: (i, 0)),
        b_spec=pl.BlockSpec((None, D, 1024), lambda i, j, k: (j, 0, 0)),
        out_shapes=[jax.ShapeDtypeStruct((T, D_FF), F32), jax.ShapeDtypeStruct((T, D_FF), BF16)],
        out_specs=[pl.BlockSpec((tm, 1024), lambda i, j, k: (i, j))] * 2, tile=(tm, 1024), name="mlp_up",
        epilogue=lambda acc: (acc, jnp.square(jnp.maximum(acc, 0.0))))
    w_down_f = hooks.down_weight(act)
    (x3,) = _mm_simple(act, w_down_f, mode="nn", M=T, N=D, K=D_FF, tm=tm, tn=1024, tk=2048, out_dtype=F32,
                       name="mlp_down", extras=(x2,), epilogue=lambda acc, res: (acc + res,))
    loss_part, d_fin, dx3, dx3b = _final_loss(x3, tgt, p["final_norm_g"].reshape(1, D), "loss_head")
    (du,) = _mm_simple(dx3b, w_down_f, mode="nt", M=T, N=D_FF, K=D, tm=tm, tn=1024, tk=D, out_dtype=BF16,
                       name="mlp_down_bwd", extras=(u,),
                       epilogue=lambda acc, uu: (acc * (2.0 * jnp.maximum(uu, 0.0)),))
    token = hooks.mlp_grads(h2, du, act, dx3b)
    (dh2,) = _matmul(
        du, w_up_s, mode="nt", grid=(T // tm, D // 1024, N_DEV // 2),
        a_spec=pl.BlockSpec((tm, 2048), lambda i, j, k: (i, k)),
        b_spec=pl.BlockSpec((2, 1024, 1024), lambda i, j, k: (k, j, 0)),
        out_shapes=[jax.ShapeDtypeStruct((T, D), F32)],
        out_specs=[pl.BlockSpec((tm, 1024), lambda i, j, k: (i, j))], tile=(tm, 1024), name="mlp_up_bwd",
        after=token, dot_fn=lambda a, b: _dot_nt(a[:, :1024], b[0]) + _dot_nt(a[:, 1024:], b[1]))
    dx2, dx2b, d_mlp = _rmsnorm_bwd(dh2, x2, p["mlp_norm_g"], dx3, "norm_mlp_bwd")
    (g_out,) = _mm_simple(ycat, dx2b, mode="tn", M=D, N=D, K=T, tm=1024, tn=1024, tk=T, out_dtype=BF16,
                          name="grad_w_out")
    token = hooks.out_grad(g_out)
    (dy,) = _mm_simple(dx2b, w_out_f, mode="nt", M=T, N=D, K=D, tm=tm, tn=1024, tk=D, out_dtype=F32,
                       name="out_proj_bwd", after=token)
    dproj, dxbc_act, d_dtb, d_alog, d_dskip, d_ssmg = _ssd_bwd(
        xbc, proj, dtT, p["dt_bias"], dtbT, p["A_log"], alogT, dfull, p["ssm_norm_g"], ypre, hs, dy, "ssd_bwd")
    dproj, d_convw, d_convb = _conv_bwd(proj, dxbc_act, conv_w_f, p["conv_b"], dproj, "conv_bwd")
    dproj, dk, dv, d_sinks, d_attng = _attn_bwd(proj, p["attn_sinks"], p["attn_out_norm_g"], o_att, dy, dproj,
                                                "attn_bwd")
    dproj = lax.dynamic_update_slice(dproj, jnp.concatenate([dk, dv], axis=1).astype(BF16), (0, OFF_K))
    (g_in,) = _mm_simple(dproj, h1, mode="tn", M=NP, N=D, K=T, tm=1536, tn=1024, tk=T, out_dtype=BF16,
                         name="grad_w_in")
    token = hooks.in_grad(g_in)
    (dh1,) = _mm_simple(dproj, w_in_t, mode="nn", M=T, N=D, K=NP, tm=tm, tn=1024, tk=2304, out_dtype=F32,
                        name="in_proj_bwd", after=token)
    dx, _, d_mix = _rmsnorm_bwd(dh1, x, p["mix_norm_g"], dx2, "norm_mix_bwd")
    small = _pack_small(d_mix, d_convb, d_ssmg, d_attng, d_mlp, d_fin, d_convw, d_dtb, d_alog, d_dskip, d_sinks,
                        extra=loss_part[:, 0:1])
    return dx, small


def _landing(own, me):
    zone = lax.empty((N_DEV,) + own.shape, own.dtype)
    return lax.dynamic_update_slice(zone, own[None], (me,) + (0,) * own.ndim)


def _gather_end(started, after, plan, name):
    _, lands = _remote_wait(started, after, plan, name + "_wait")
    return _gather_finish(lands, name + "_finish")


class _ShardedWeights:
    def __init__(self, w_in, w_out, conv_w, w_up, w_down, me, csel):
        self.me, self.csel = me, csel
        own_rows = lax.dynamic_update_slice(jnp.zeros((SUPER, D_MODEL), F32), jnp.transpose(w_in), (2 * me, 0))
        shards = [own_rows.astype(BF16), conv_w]
        self.st_mixer = _remote_start(shards, [_landing(s, me) for s in shards], _gather_plan, 4 * len(shards),
                                      "gather_start_mixer")
        order = self.st_mixer[4]
        self.st_later = {}
        for k, wt in [("out", w_out), ("up", w_up), ("down", w_down)]:
            shard = (wt + order[0, 0]).astype(BF16)
            self.st_later[k] = _remote_start([shard], [_landing(shard, me)], _gather_plan, 4, f"gather_start_{k}",
                                             after=order)
            order = self.st_later[k][4]
        self.start_token = order
        self.forwards = {}
        self.reduces = {}

    def mixer_weights(self, after):
        g_in, g_conv = _gather_end(self.st_mixer, after, _gather_plan, "gather_mixer")
        conv_w_f = jnp.concatenate([g_conv[i] for i in range(N_DEV)], axis=1)
        return _w_in_from_super_slabs(g_in), conv_w_f, None

    def prefetch(self, k, after):
        _, lands = _remote_wait(self.st_later[k], after, _gather_plan, f"gather_{k}_wait")
        self.forwards[k] = _remote_start([], lands, _forward_plan, 3, f"gather_{k}_forward")
        return self.forwards[k][4]

    def _prefetched(self, k, after):
        return _remote_wait(self.forwards[k], after, _forward_plan, f"gather_{k}_forward_wait")[1][0]

    def out_weight(self, after):
        return self._prefetched("out", after).reshape(D_MODEL, D_MODEL)

    def up_weight(self, after):
        return self._prefetched("up", after)

    def down_weight(self, after):
        return _gather_end(self.st_later["down"], after, _gather_plan, "gather_down")[0].reshape(D_FF, D_MODEL)

    def _chips_start(self, slabs, from_sibling, rows, tag):
        sums = [_pair_add(s, r, self.csel, tr, f"pair_add_{tag}_{i}")
                for i, (s, r, tr) in enumerate(zip(slabs, from_sibling, rows))]
        lands = [lax.empty((3,) + s.shape[1:], s.dtype) for s in sums]
        self.reduces[tag] = _remote_start(sums, lands, _chips_plan, 3 * len(sums), f"reduce_start_{tag}")
        return self.reduces[tag][4]

    def mlp_grads(self, h2, du, act, dx3b):
        def send(part, tag, after):
            st = _remote_start([part], [lax.empty(part.shape, part.dtype)], _pair4_plan, 4,
                               f"reduce_pair_start_{tag}", after=after)
            return st

        def received(st, after, tag):
            return _remote_wait(st, after, _pair4_plan, f"reduce_pair_wait_{tag}")[1][0]

        def to_chips(sums, tag):
            self.reduces[tag] = _remote_start([sums], [lax.empty((3,) + sums.shape[1:], sums.dtype)], _chips_plan, 3,
                                              f"reduce_start_{tag}")
            return self.reduces[tag][4]

        up_send = _grad_w_up(h2, du, "grad_w_up_send", sel=(self.csel, True))
        st_up = send(up_send, "up", None)
        down_send = _grad_w_down(act, dx3b, "grad_w_down_send", sel=(self.csel, True), after=st_up[4])
        st_down = send(down_send, "down", None)
        up_sum = _grad_w_up(h2, du, "grad_w_up_keep", sel=(self.csel, False), add=received(st_up, down_send, "up"),
                            after=st_down[4])
        token = to_chips(up_sum, "up")
        down_sum = _grad_w_down(act, dx3b, "grad_w_down_keep", sel=(self.csel, False),
                                add=received(st_down, up_sum, "down"), after=token)
        return to_chips(down_sum, "down")

    def out_grad(self, g_out):
        slabs = [g_out.reshape(N_DEV, D_MODEL // N_DEV, D_MODEL)]
        return self._chips_start(slabs, _exchange_pair(slabs, "reduce_pair_out"), [256], "out")

    def in_grad(self, g_in):
        slabs = [jnp.stack([_natural_rows(g_in, SUPER_STEP * j, SUPER_STEP * j + SUPER) for j in range(N_DEV)])]
        return self._chips_start(slabs, _exchange_pair(slabs, "reduce_pair_in"), [SUPER], "in")

    def small_start(self, small):
        self.st_small = _remote_start([small], [_landing(small, self.me)], _everyone_plan, N_DEV - 1, "gather_start_small")

    def small_end(self, after):
        return _remote_wait(self.st_small, after, _everyone_plan, "gather_small_wait")[1][0]

    def reduce_end(self, tag, after):
        return _remote_wait(self.reduces[tag], after, _chips_plan, f"reduce_wait_{tag}")


def kernel(x, mix_norm_g, w_in, conv_w, conv_b, dt_bias, A_log, D_skip, ssm_norm_g, attn_sinks, attn_out_norm_g, w_out, mlp_norm_g, w_up, w_down, final_norm_g, loss_target, m_mix_norm_g, m_w_in, m_conv_w, m_conv_b, m_dt_bias, m_A_log, m_D_skip, m_ssm_norm_g, m_attn_sinks, m_attn_out_norm_g, m_w_out, m_mlp_norm_g, m_w_up, m_w_down, m_final_norm_g, v_mix_norm_g, v_w_in, v_conv_w, v_conv_b, v_dt_bias, v_A_log, v_D_skip, v_ssm_norm_g, v_attn_sinks, v_attn_out_norm_g, v_w_out, v_mlp_norm_g, v_w_up, v_w_down, v_final_norm_g):
    xi, yi, ci = _coords()
    me = 4 * xi + 2 * yi + ci
    csel = jnp.reshape(ci, (1,)).astype(jnp.int32)
    qsel = jnp.reshape(2 * xi + yi, (1,)).astype(jnp.int32)
    w = dict(mix_norm_g=mix_norm_g, conv_b=conv_b, dt_bias=dt_bias, A_log=A_log, D_skip=D_skip,
             ssm_norm_g=ssm_norm_g, attn_sinks=attn_sinks, attn_out_norm_g=attn_out_norm_g, mlp_norm_g=mlp_norm_g,
             final_norm_g=final_norm_g)
    hooks = _ShardedWeights(w_in[0], w_out[0], conv_w[0], w_up[0], w_down[0], me, csel)
    p = dict(w, mix_norm_g=mix_norm_g + hooks.start_token[0:1, 0:1])
    dx, small = _local_step(x[0], loss_target[0], p, hooks)
    hooks.small_start(small)
    big = {}
    after = dx
    for name, wt, mt, vt, tile in [
            ("up", w_up, m_w_up, v_w_up, (512, SLAB)), ("down", w_down, m_w_down, v_w_down, (256, D_MODEL)),
            ("out", w_out, m_w_out, v_w_out, (256, D_MODEL))]:
        (chip_sums,), (from_chips,) = hooks.reduce_end(name, after)
        res = _adamw_big(wt[0], mt[0], vt[0], chip_sums, from_chips, qsel, tile, f"adamw_w_{name}")
        big["w_" + name] = tuple(r[None] for r in res)
        after = res[0]
    (chip_sums,), (from_chips,) = hooks.reduce_end("in", after)
    g_super = _sum_partials(chip_sums, from_chips, qsel, 512, "grad_w_in_sum")
    g_in = lax.dynamic_slice(g_super, (2 * me, 0), (PER, D_MODEL))
    res = _adamw_tiled(jnp.transpose(w_in[0]), g_in, jnp.transpose(m_w_in[0]), jnp.transpose(v_w_in[0]), 512,
                       "adamw_w_in")
    big["w_in"] = tuple(jnp.transpose(r)[None] for r in (g_in, *res))
    after = res[0]
    gsum = _small_sum(hooks.small_end(after), "small_sum")
    loss = gsum[9, 64]
    gs = _unpack_small(gsum, CONV_DIM)
    cw = CONV_DIM // N_DEV
    g_conv_shard = lax.dynamic_slice(gsum[5:9, :], (0, me * cw), (CONV_K, cw))

    def pack(s):
        return _pack_small(s["mix_norm_g"], s["conv_b"], s["ssm_norm_g"], s["attn_out_norm_g"], s["mlp_norm_g"],
                           s["final_norm_g"], s["conv_w"][0], s["dt_bias"], s["A_log"], s["D_skip"], s["attn_sinks"])

    wp = pack(dict(w, conv_w=conv_w))
    mp = pack(dict(mix_norm_g=m_mix_norm_g, conv_b=m_conv_b, ssm_norm_g=m_ssm_norm_g,
                   attn_out_norm_g=m_attn_out_norm_g, mlp_norm_g=m_mlp_norm_g, final_norm_g=m_final_norm_g,
                   conv_w=m_conv_w, dt_bias=m_dt_bias, A_log=m_A_log, D_skip=m_D_skip, attn_sinks=m_attn_sinks))
    vp = pack(dict(mix_norm_g=v_mix_norm_g, conv_b=v_conv_b, ssm_norm_g=v_ssm_norm_g,
                   attn_out_norm_g=v_attn_out_norm_g, mlp_norm_g=v_mlp_norm_g, final_norm_g=v_final_norm_g,
                   conv_w=v_conv_w, dt_bias=v_dt_bias, A_log=v_A_log, D_skip=v_D_skip, attn_sinks=v_attn_sinks))
    gp = jnp.concatenate([gsum[0:5], jnp.pad(g_conv_shard, ((0, 0), (0, D_MODEL - cw))), gsum[9:10],
                          jnp.zeros((SMALL_ROWS - 10, D_MODEL), F32)], axis=0)
    dp, mnp, vnp = _adamw_small(wp, gp, mp, vp, "adamw_small")
    grads = dict(gs, conv_w=g_conv_shard[None])
    deltas = _unpack_small(dp, cw)
    new_m = _unpack_small(mnp, cw)
    new_v = _unpack_small(vnp, cw)
    for k, name in enumerate(["w_in", "w_out", "w_up", "w_down"]):
        grads[name], deltas[name], new_m[name], new_v[name] = big[name]
    return (loss, dx[None], *[grads[n] for n in WEIGHT_ORDER], *[deltas[n] for n in WEIGHT_ORDER],
            *[new_m[n] for n in WEIGHT_ORDER], *[new_v[n] for n in WEIGHT_ORDER])
```

```python
import functools

import jax
import jax.numpy as jnp
from jax import lax
from jax.experimental import pallas as pl
from jax.experimental.pallas import tpu as pltpu
from jax.experimental.pallas import tpu_sc as plsc

F32 = jnp.float32
BF16 = jnp.bfloat16
HI = lax.Precision.HIGHEST
MESH = pl.DeviceIdType.MESH

EPS = 1e-5
D_MODEL = 2048
D_INNER = 1024
N_HEADS = 16
HEAD_DIM = 64
N_GROUPS = 4
D_STATE = 128
CHUNK = 128
CONV_K = 4
CONV_DIM = 2048
ATTN_W = 1024
KV_W = 128
WINDOW = 128
D_FF = 8192
IN_PROJ = 4368
N_DEV = 8
NP = 4608
OFF_Z, OFF_X, OFF_B, OFF_C, OFF_Q, OFF_K, OFF_V, OFF_DT = 0, 1024, 2048, 2560, 3072, 4096, 4224, 4352
NAT_DT = 3072

ADAM_LR = 0.001
ADAM_B1 = 0.9
ADAM_B2 = 0.999
ADAM_EPS = 1e-08
ADAM_WD = 0.01
ADAM_STEP = 10

VMEM_LIMIT = 52 * 1024 * 1024
SMALL_ROWS = 16
NEG = -1e30


def _cparams(sem=None):
    return pltpu.CompilerParams(dimension_semantics=sem, vmem_limit_bytes=VMEM_LIMIT)


def _split3(v):
    hi = v.astype(BF16)
    rest = v - hi.astype(F32)
    mid = rest.astype(BF16)
    return hi, mid, (rest - mid.astype(F32)).astype(BF16)


def _hdot(a, b, data):
    if data == "a":
        sel = b.astype(BF16)
        return sum(_dot_nn(part, sel) for part in _split3(a))
    sel = a.astype(BF16)
    return sum(_dot_nn(sel, part) for part in _split3(b))


def _dot_nn(a, b):
    return lax.dot_general(a, b, (((1,), (0,)), ((), ())), preferred_element_type=F32)


def _dot_nt(a, b):
    return lax.dot_general(a, b, (((1,), (1,)), ((), ())), preferred_element_type=F32)


def _dot_tn(a, b):
    return lax.dot_general(a, b, (((0,), (0,)), ((), ())), preferred_element_type=F32)


def _softplus(v):
    return jnp.maximum(v, 0.0) + jnp.log1p(jnp.exp(-jnp.abs(v)))


def _sigmoid(v):
    return 1.0 / (1.0 + jnp.exp(-v))


def _matmul(a, b, *, mode, grid, a_spec, b_spec, out_shapes, out_specs, tile, name,
            extras=(), extra_specs=(), epilogue=None, after=None, dot_fn=None, prefetch=None):
    nk = grid[2]
    n_ex = len(extras)
    n_out = len(out_shapes)
    dot = dot_fn if dot_fn is not None else {"nn": _dot_nn, "nt": _dot_nt, "tn": _dot_tn}[mode]

    def finish(acc, ex_refs, out_refs):
        res = (acc,) if epilogue is None else epilogue(acc, *[e[...] for e in ex_refs])
        for o, r in zip(out_refs, res):
            o[...] = r.astype(o.dtype)

    def body(*refs):
        a_ref, b_ref = refs[0], refs[1]
        ex_refs = refs[2:2 + n_ex]
        out_refs = refs[2 + n_ex:2 + n_ex + n_out]
        part = dot(a_ref[...].astype(BF16), b_ref[...].astype(BF16))
        if nk == 1:
            finish(part, ex_refs, out_refs)
        else:
            acc_ref = refs[-1]
            k = pl.program_id(2)

            @pl.when(k == 0)
            def _():
                acc_ref[...] = part

            @pl.when(k > 0)
            def _():
                acc_ref[...] += part

            @pl.when(k == nk - 1)
            def _():
                finish(acc_ref[...], ex_refs, out_refs)

    scratch = [] if nk == 1 else [pltpu.VMEM(tile, F32)]
    n_pre = 0 if prefetch is None else 1
    tok_specs = [] if after is None else [pl.BlockSpec((8, 128), lambda *_: (0, 0))]
    tok_args = [] if after is None else [after]

    def body_with_token(*refs):
        refs = refs[n_pre:]
        body(*refs[:2 + n_ex], *refs[2 + n_ex + len(tok_args):])

    in_specs = [a_spec, b_spec, *extra_specs, *tok_specs]
    params = _cparams(("parallel", "parallel", "arbitrary"))
    if prefetch is None:
        return pl.pallas_call(
            body_with_token, grid=grid, in_specs=in_specs, out_specs=list(out_specs), out_shape=list(out_shapes),
            scratch_shapes=scratch, name=name, compiler_params=params)(a, b, *extras, *tok_args)
    return pl.pallas_call(
        body_with_token,
        grid_spec=pltpu.PrefetchScalarGridSpec(num_scalar_prefetch=1, grid=grid, in_specs=in_specs,
                                               out_specs=list(out_specs), scratch_shapes=scratch),
        out_shape=list(out_shapes), name=name, compiler_params=params)(prefetch, a, b, *extras, *tok_args)


def _mm_simple(a, b, *, mode, M, N, K, tm, tn, tk, out_dtype, name, extras=(), epilogue=None, n_out=1,
               out_dtypes=None, after=None):
    grid = (M // tm, N // tn, K // tk)
    if mode == "nn":
        a_spec = pl.BlockSpec((tm, tk), lambda i, j, k: (i, k))
        b_spec = pl.BlockSpec((tk, tn), lambda i, j, k: (k, j))
    elif mode == "nt":
        a_spec = pl.BlockSpec((tm, tk), lambda i, j, k: (i, k))
        b_spec = pl.BlockSpec((tn, tk), lambda i, j, k: (j, k))
    else:
        a_spec = pl.BlockSpec((tk, tm), lambda i, j, k: (k, i))
        b_spec = pl.BlockSpec((tk, tn), lambda i, j, k: (k, j))
    o_spec = pl.BlockSpec((tm, tn), lambda i, j, k: (i, j))
    dts = out_dtypes if out_dtypes is not None else [out_dtype] * n_out
    return _matmul(a, b, mode=mode, grid=grid, a_spec=a_spec, b_spec=b_spec,
                   out_shapes=[jax.ShapeDtypeStruct((M, N), d) for d in dts],
                   out_specs=[o_spec] * len(dts), tile=(tm, tn), name=name,
                   extras=extras, extra_specs=[o_spec] * len(extras), epilogue=epilogue, after=after)


ROW_BLOCK = 256


def _rmsnorm_fwd(x, g, name):
    T, D = x.shape

    def body(x_ref, g_ref, o_ref):
        xf = x_ref[...]
        r = lax.rsqrt(jnp.mean(xf * xf, axis=-1, keepdims=True) + EPS)
        o_ref[...] = (xf * r * g_ref[...]).astype(BF16)

    return pl.pallas_call(
        body, grid=(T // ROW_BLOCK,),
        in_specs=[pl.BlockSpec((ROW_BLOCK, D), lambda i: (i, 0)), pl.BlockSpec((1, D), lambda i: (0, 0))],
        out_specs=pl.BlockSpec((ROW_BLOCK, D), lambda i: (i, 0)),
        out_shape=jax.ShapeDtypeStruct((T, D), BF16), name=name, compiler_params=_cparams(("parallel",)),
    )(x, g)


def _rmsnorm_bwd(dh, x, g, dres, name):
    T, D = x.shape

    def body(dh_ref, x_ref, g_ref, dres_ref, dx_ref, dxb_ref, dg_ref):
        i = pl.program_id(0)
        xf = x_ref[...]
        r = lax.rsqrt(jnp.mean(xf * xf, axis=-1, keepdims=True) + EPS)
        xh = xf * r
        d = dh_ref[...]

        @pl.when(i == 0)
        def _():
            dg_ref[...] = jnp.zeros_like(dg_ref)

        dg_ref[...] += jnp.sum(d * xh, axis=0, keepdims=True)
        dxh = d * g_ref[...]
        dx = r * (dxh - xh * jnp.mean(dxh * xh, axis=-1, keepdims=True)) + dres_ref[...]
        dx_ref[...] = dx
        dxb_ref[...] = dx.astype(BF16)

    row = pl.BlockSpec((ROW_BLOCK, D), lambda i: (i, 0))
    vec = pl.BlockSpec((1, D), lambda i: (0, 0))
    return pl.pallas_call(
        body, grid=(T // ROW_BLOCK,), in_specs=[row, row, vec, row], out_specs=[row, row, vec],
        out_shape=[jax.ShapeDtypeStruct((T, D), F32), jax.ShapeDtypeStruct((T, D), BF16),
                   jax.ShapeDtypeStruct((1, D), F32)],
        name=name, compiler_params=_cparams(("arbitrary",)),
    )(dh, x, g, dres)


def _final_loss(x3, tgt, g, name):
    T, D = x3.shape

    def body(x_ref, t_ref, g_ref, loss_ref, dg_ref, dx_ref, dxb_ref):
        i = pl.program_id(0)
        xf = x_ref[...]
        r = lax.rsqrt(jnp.mean(xf * xf, axis=-1, keepdims=True) + EPS)
        xh = xf * r
        gg = g_ref[...]
        err = xh * gg - t_ref[...]

        @pl.when(i == 0)
        def _():
            dg_ref[...] = jnp.zeros_like(dg_ref)
            loss_ref[...] = jnp.zeros_like(loss_ref)

        part = jnp.sum(jnp.sum(err * err, axis=-1, keepdims=True), axis=0, keepdims=True) * (0.5 / D)
        loss_ref[...] += jnp.broadcast_to(part, loss_ref.shape)
        dout = err * (1.0 / D)
        dg_ref[...] += jnp.sum(dout * xh, axis=0, keepdims=True)
        dxh = dout * gg
        dx = r * (dxh - xh * jnp.mean(dxh * xh, axis=-1, keepdims=True))
        dx_ref[...] = dx
        dxb_ref[...] = dx.astype(BF16)

    row = pl.BlockSpec((ROW_BLOCK, D), lambda i: (i, 0))
    vec = pl.BlockSpec((1, D), lambda i: (0, 0))
    return pl.pallas_call(
        body, grid=(T // ROW_BLOCK,), in_specs=[row, row, vec],
        out_specs=[pl.BlockSpec((1, 128), lambda i: (0, 0)), vec, row, row],
        out_shape=[jax.ShapeDtypeStruct((1, 128), F32), jax.ShapeDtypeStruct((1, D), F32),
                   jax.ShapeDtypeStruct((T, D), F32), jax.ShapeDtypeStruct((T, D), BF16)],
        name=name, compiler_params=_cparams(("arbitrary",)),
    )(x3, tgt, g)


CONV_BLOCK = 256


def _conv_apply(u, w, b):
    row = lax.broadcasted_iota(jnp.int32, u.shape, 0)
    acc = b + w[CONV_K - 1:CONV_K, :] * u
    shifted = []
    for j in range(1, CONV_K):
        uj = jnp.where(row >= j, pltpu.roll(u, j, axis=0), 0.0)
        shifted.append(uj)
        acc = acc + w[CONV_K - 1 - j:CONV_K - j, :] * uj
    return acc, shifted


def _conv_fwd(proj, conv_w, conv_b, name):
    T = proj.shape[0]
    cb0 = OFF_X // CONV_BLOCK

    def body(u_ref, w_ref, b_ref, o_ref):
        c, _ = _conv_apply(u_ref[...], w_ref[...], b_ref[...])
        o_ref[...] = c * _sigmoid(c)

    return pl.pallas_call(
        body, grid=(CONV_DIM // CONV_BLOCK,),
        in_specs=[pl.BlockSpec((T, CONV_BLOCK), lambda j: (0, cb0 + j)),
                  pl.BlockSpec((CONV_K, CONV_BLOCK), lambda j: (0, j)),
                  pl.BlockSpec((1, CONV_BLOCK), lambda j: (0, j))],
        out_specs=pl.BlockSpec((T, CONV_BLOCK), lambda j: (0, j)),
        out_shape=jax.ShapeDtypeStruct((T, CONV_DIM), F32), name=name, compiler_params=_cparams(("parallel",)),
    )(proj, conv_w, conv_b)


def _conv_bwd(proj, dact, conv_w, conv_b, dproj, name):
    T = proj.shape[0]
    cb0 = OFF_X // CONV_BLOCK

    def body(u_ref, d_ref, w_ref, b_ref, _, du_ref, dw_ref, db_ref):
        u = u_ref[...]
        w = w_ref[...]
        c, shifted = _conv_apply(u, w, b_ref[...])
        sg = _sigmoid(c)
        dc = d_ref[...] * sg * (1.0 + c * (1.0 - sg))
        row = lax.broadcasted_iota(jnp.int32, u.shape, 0)
        du = w[CONV_K - 1:CONV_K, :] * dc
        dw_ref[CONV_K - 1:CONV_K, :] = jnp.sum(dc * u, axis=0, keepdims=True)
        for j in range(1, CONV_K):
            dcj = jnp.where(row < T - j, pltpu.roll(dc, T - j, axis=0), 0.0)
            du = du + w[CONV_K - 1 - j:CONV_K - j, :] * dcj
            dw_ref[CONV_K - 1 - j:CONV_K - j, :] = jnp.sum(dc * shifted[j - 1], axis=0, keepdims=True)
        db_ref[...] = jnp.sum(dc, axis=0, keepdims=True)
        du_ref[...] = du.astype(BF16)

    return pl.pallas_call(
        body, grid=(CONV_DIM // CONV_BLOCK,),
        in_specs=[pl.BlockSpec((T, CONV_BLOCK), lambda j: (0, cb0 + j)),
                  pl.BlockSpec((T, CONV_BLOCK), lambda j: (0, j)),
                  pl.BlockSpec((CONV_K, CONV_BLOCK), lambda j: (0, j)),
                  pl.BlockSpec((1, CONV_BLOCK), lambda j: (0, j)), pl.BlockSpec(memory_space=pl.ANY)],
        out_specs=[pl.BlockSpec((T, CONV_BLOCK), lambda j: (0, cb0 + j)),
                   pl.BlockSpec((CONV_K, CONV_BLOCK), lambda j: (0, j)),
                   pl.BlockSpec((1, CONV_BLOCK), lambda j: (0, j))],
        out_shape=[jax.ShapeDtypeStruct(dproj.shape, BF16), jax.ShapeDtypeStruct((CONV_K, CONV_DIM), F32),
                   jax.ShapeDtypeStruct((1, CONV_DIM), F32)],
        input_output_aliases={4: 0}, name=name, compiler_params=_cparams(("parallel",)),
    )(proj, dact, conv_w, conv_b, dproj)


GROUP_W = D_INNER // N_GROUPS
HEADS_PER_GROUP = N_HEADS // N_GROUPS


def _expand_mat():
    h = lax.broadcasted_iota(jnp.int32, (N_HEADS, D_INNER), 0)
    j = lax.broadcasted_iota(jnp.int32, (N_HEADS, D_INNER), 1)
    return (j // HEAD_DIM == h).astype(F32)


def _reduce_mat(g):
    j = lax.broadcasted_iota(jnp.int32, (GROUP_W, N_HEADS), 0)
    h = lax.broadcasted_iota(jnp.int32, (GROUP_W, N_HEADS), 1)
    return (g * HEADS_PER_GROUP + j // HEAD_DIM == h).astype(F32)


def _col16(v, h):
    lane = lax.broadcasted_iota(jnp.int32, v.shape, 1)
    return jnp.sum(jnp.where(lane == h, v, 0.0), axis=1, keepdims=True)


def _ssd_pre(dt_raw, dtT_raw, dtb, dtbT, alog, alogT):
    Q = CHUNK
    xdt = dt_raw + dtb
    dt = _softplus(xdt)
    dtT = _softplus(dtT_raw + dtbT)
    A = -jnp.exp(alog)
    AT = -jnp.exp(alogT)
    row = lax.broadcasted_iota(jnp.int32, (Q, Q), 0)
    col = lax.broadcasted_iota(jnp.int32, (Q, Q), 1)
    tril = (row >= col).astype(F32)
    triu = (row <= col).astype(F32)
    cs = _hdot(tril, dt * A, "b")
    csT = _hdot(dtT * AT, triu, "a")
    return xdt, dt, A, cs, csT, row >= col, triu


def _decay_matrix(cs, csT, h, causal):
    seg = _col16(cs, h) - csT[h:h + 1, :]
    return jnp.where(causal, jnp.exp(jnp.minimum(seg, 0.0)), 0.0)


def _ssd_in_specs(nc, rev):
    def cidx(c):
        return (nc - 1 - c) if rev else c

    return [
        pl.BlockSpec((CHUNK, D_INNER), lambda c: (cidx(c), 0)),
        pl.BlockSpec((CHUNK, 512), lambda c: (cidx(c), 2)),
        pl.BlockSpec((CHUNK, 512), lambda c: (cidx(c), 3)),
        pl.BlockSpec((CHUNK, D_INNER), lambda c: (cidx(c), 0)),
        pl.BlockSpec((CHUNK, 128), lambda c: (cidx(c), OFF_DT // 128)),
        pl.BlockSpec((N_HEADS, CHUNK), lambda c: (0, cidx(c))),
        pl.BlockSpec((1, N_HEADS), lambda c: (0, 0)),
        pl.BlockSpec((N_HEADS, 1), lambda c: (0, 0)),
        pl.BlockSpec((1, N_HEADS), lambda c: (0, 0)),
        pl.BlockSpec((N_HEADS, 1), lambda c: (0, 0)),
        pl.BlockSpec((1, D_INNER), lambda c: (0, 0)),
        pl.BlockSpec((1, D_INNER), lambda c: (0, 0)),
    ]


def _ssd_fwd(xbc, proj, dtT, dtb, dtbT, alog, alogT, dfull, ng, name):
    T = xbc.shape[0]
    nc = T // CHUNK
    Q = CHUNK

    def body(xs_ref, B_ref, C_ref, z_ref, dt_ref, dtT_ref, dtb_ref, dtbT_ref, al_ref, alT_ref, df_ref, ng_ref,
             y_ref, ypre_ref, hs_ref, h_scr):
        c = pl.program_id(0)

        @pl.when(c == 0)
        def _():
            h_scr[...] = jnp.zeros_like(h_scr)

        _, dt, _, cs, csT, causal, _ = _ssd_pre(dt_ref[:, :N_HEADS], dtT_ref[...], dtb_ref[...], dtbT_ref[...],
                                                al_ref[...], alT_ref[...])
        ex = _expand_mat()
        dt_full = _hdot(dt, ex, "a")
        cs_full = _hdot(cs, ex, "a")
        cs_last = cs_full[Q - 1:Q, :]
        xs = xs_ref[...]
        xd = xs * dt_full
        e_full = jnp.exp(cs_full)
        dec_full = jnp.exp(cs_last - cs_full)
        cd_full = jnp.exp(cs_last)
        lane_head = lax.broadcasted_iota(jnp.int32, (1, GROUP_W), 1) // HEAD_DIM
        for g in range(N_GROUPS):
            sl = slice(g * GROUP_W, (g + 1) * GROUP_W)
            Bg = B_ref[:, g * D_STATE:(g + 1) * D_STATE].astype(BF16)
            Cg = C_ref[:, g * D_STATE:(g + 1) * D_STATE].astype(BF16)
            CB = _dot_nt(Cg, Bg)
            hg = h_scr[g]
            yoff = _dot_nn(Cg, hg.astype(BF16)) * e_full[:, sl]
            xd_g = xd[:, sl]
            S = _dot_tn(Bg, (xd_g * dec_full[:, sl]).astype(BF16))
            xd_b = xd_g.astype(BF16)
            ydiag = jnp.zeros((Q, GROUP_W), F32)
            for r in range(HEADS_PER_GROUP):
                Lm = _decay_matrix(cs, csT, g * HEADS_PER_GROUP + r, causal)
                Gm = (CB * Lm).astype(BF16)
                ydiag = ydiag + _dot_nn(Gm, jnp.where(lane_head == r, xd_b, jnp.zeros_like(xd_b)))
            hs_ref[0, g] = hg
            h_scr[g] = hg * cd_full[:, sl] + S
            ypre = ydiag + yoff + xs[:, sl] * df_ref[:, sl]
            ypre_ref[:, sl] = ypre
            zg = z_ref[:, sl]
            yz = ypre * zg * _sigmoid(zg)
            rn = lax.rsqrt(jnp.mean(yz * yz, axis=-1, keepdims=True) + EPS)
            y_ref[:, sl] = (yz * rn * ng_ref[:, sl]).astype(BF16)

    return pl.pallas_call(
        body, grid=(nc,), in_specs=_ssd_in_specs(nc, False),
        out_specs=[pl.BlockSpec((CHUNK, D_INNER), lambda c: (c, 0)),
                   pl.BlockSpec((CHUNK, D_INNER), lambda c: (c, 0)),
                   pl.BlockSpec((1, N_GROUPS, D_STATE, GROUP_W), lambda c: (c, 0, 0, 0))],
        out_shape=[jax.ShapeDtypeStruct((T, D_INNER + ATTN_W), BF16), jax.ShapeDtypeStruct((T, D_INNER), F32),
                   jax.ShapeDtypeStruct((nc, N_GROUPS, D_STATE, GROUP_W), F32)],
        scratch_shapes=[pltpu.VMEM((N_GROUPS, D_STATE, GROUP_W), F32)],
        name=name, compiler_params=_cparams(("arbitrary",)),
    )(xbc, xbc, xbc, proj, proj, dtT, dtb, dtbT, alog, alogT, dfull, ng)


def _ssd_bwd(xbc, proj, dtT, dtb, dtbT, alog, alogT, dfull, ng, ypre, hs, dy, name):
    T = xbc.shape[0]
    nc = T // CHUNK
    Q = CHUNK

    def body(xs_ref, B_ref, C_ref, z_ref, dt_ref, dtT_ref, dtb_ref, dtbT_ref, al_ref, alT_ref, df_ref, ng_ref,
             ypre_ref, hs_ref, dy_ref,
             dz_ref, dxbc_ref, ddtb_ref, dal_ref, dD_ref, dng_ref, dh_scr):
        step = pl.program_id(0)

        @pl.when(step == 0)
        def _():
            dh_scr[...] = jnp.zeros_like(dh_scr)
            ddtb_ref[...] = jnp.zeros_like(ddtb_ref)
            dal_ref[...] = jnp.zeros_like(dal_ref)
            dD_ref[...] = jnp.zeros_like(dD_ref)
            dng_ref[...] = jnp.zeros_like(dng_ref)

        xdt, dt, A, cs, csT, causal, triu = _ssd_pre(dt_ref[:, :N_HEADS], dtT_ref[...], dtb_ref[...],
                                                    dtbT_ref[...], al_ref[...], alT_ref[...])
        ex = _expand_mat()
        dt_full = _hdot(dt, ex, "a")
        cs_full = _hdot(cs, ex, "a")
        cs_last = cs_full[Q - 1:Q, :]
        xs = xs_ref[...]
        xd = xs * dt_full
        e_full = jnp.exp(cs_full)
        dec_full = jnp.exp(cs_last - cs_full)
        cd_full = jnp.exp(cs_last)
        lane_head = lax.broadcasted_iota(jnp.int32, (1, GROUP_W), 1) // HEAD_DIM
        is_last = lax.broadcasted_iota(jnp.int32, (Q, 1), 0) == Q - 1
        dcs16 = jnp.zeros((Q, N_HEADS), F32)
        ddtx16 = jnp.zeros((Q, N_HEADS), F32)
        dD16 = jnp.zeros((8, N_HEADS), F32)
        lane16 = lax.broadcasted_iota(jnp.int32, (1, N_HEADS), 1)
        sub16 = lax.broadcasted_iota(jnp.int32, (N_HEADS, 1), 0)
        col_sums = jnp.zeros((N_HEADS, Q), F32)
        for g in range(N_GROUPS):
            sl = slice(g * GROUP_W, (g + 1) * GROUP_W)
            red = _reduce_mat(g)
            ypre_g = ypre_ref[:, sl]
            zg = z_ref[:, sl]
            sg = _sigmoid(zg)
            silu = zg * sg
            yz = ypre_g * silu
            rn = lax.rsqrt(jnp.mean(yz * yz, axis=-1, keepdims=True) + EPS)
            yh = yz * rn
            dy_g = dy_ref[:, sl]
            dng_ref[:, sl] += jnp.sum(dy_g * yh, axis=0, keepdims=True)
            dyh = dy_g * ng_ref[:, sl]
            dyz = rn * (dyh - yh * jnp.mean(dyh * yh, axis=-1, keepdims=True))
            dY = dyz * silu
            dz_ref[:, sl] = (dyz * ypre_g * sg * (1.0 + zg * (1.0 - sg))).astype(BF16)
            xs_g = xs[:, sl]
            xd_g = xd[:, sl]
            dec_g = dec_full[:, sl]
            cd_g = cd_full[:, sl]
            d_g = df_ref[:, sl]
            Bg = B_ref[:, g * D_STATE:(g + 1) * D_STATE].astype(BF16)
            Cg = C_ref[:, g * D_STATE:(g + 1) * D_STATE].astype(BF16)
            CB = _dot_nt(Cg, Bg)
            hg = hs_ref[0, g]
            hgb = hg.astype(BF16)
            yoff = _dot_nn(Cg, hgb) * e_full[:, sl]
            dhn = dh_scr[g]
            dhnb = dhn.astype(BF16)
            dYE = (dY * e_full[:, sl]).astype(BF16)
            dC = _dot_nt(dYE, hgb)
            dh_direct = _dot_tn(Cg, dYE)
            dXdd = _dot_nn(Bg, dhnb)
            dB = _dot_nt((xd_g * dec_g).astype(BF16), dhnb)
            dcd = jnp.sum(dhn * hg, axis=0, keepdims=True)
            dh_scr[g] = dh_direct + cd_g * dhn
            dYb = dY.astype(BF16)
            xd_b = xd_g.astype(BF16)
            dCB = jnp.zeros((Q, Q), F32)
            dXd = dXdd * dec_g
            for r in range(HEADS_PER_GROUP):
                h = g * HEADS_PER_GROUP + r
                Lm = _decay_matrix(cs, csT, h, causal)
                Gf = CB * Lm
                dYr = jnp.where(lane_head == r, dYb, jnp.zeros_like(dYb))
                dG = _dot_nt(dYr, xd_b)
                dCB = dCB + dG * Lm
                dXd = dXd + _dot_tn(Gf.astype(BF16), dYr)
                Mm = dG * Gf
                dcs16 = dcs16 + jnp.where(lane16 == h, jnp.sum(Mm, axis=1, keepdims=True), 0.0)
                col_sums = col_sums + jnp.where(sub16 == h, jnp.sum(Mm, axis=0, keepdims=True), 0.0)
            dCBb = dCB.astype(BF16)
            dC = dC + _dot_nn(dCBb, Bg)
            dB = dB + _dot_tn(dCBb, Cg)
            w_state = dXdd * dec_g * xd_g
            t_last = jnp.sum(w_state, axis=0, keepdims=True) + dcd * cd_g
            dcs_g = dY * yoff - w_state + jnp.where(is_last, t_last, 0.0)
            dcs16 = dcs16 + _hdot(dcs_g, red, "a")
            ddtx16 = ddtx16 + _hdot(dXd * xs_g, red, "a")
            dD16 = dD16 + _hdot(jnp.broadcast_to(jnp.sum(dY * xs_g, axis=0, keepdims=True), (8, GROUP_W)), red, "a")
            dxbc_ref[:, sl] = dXd * dt_full[:, sl] + dY * d_g
            dxbc_ref[:, D_INNER + g * D_STATE:D_INNER + (g + 1) * D_STATE] = dB
            dxbc_ref[:, D_INNER + 512 + g * D_STATE:D_INNER + 512 + (g + 1) * D_STATE] = dC
        eye = (lax.broadcasted_iota(jnp.int32, (N_HEADS, N_HEADS), 0)
               == lax.broadcasted_iota(jnp.int32, (N_HEADS, N_HEADS), 1)).astype(BF16)
        dcs16 = dcs16 - sum(_dot_tn(part, eye) for part in _split3(col_sums))
        da = _hdot(triu, dcs16, "b")
        ddt = da * A + ddtx16
        ddt_raw = ddt * _sigmoid(xdt)
        pr = lax.broadcasted_iota(jnp.int32, (N_HEADS, 128), 0)
        pc = lax.broadcasted_iota(jnp.int32, (N_HEADS, 128), 1)
        dz_ref[:, D_INNER:OFF_DT] = jnp.zeros((Q, OFF_DT - D_INNER), BF16)
        dz_ref[:, OFF_DT:OFF_DT + 128] = _hdot(ddt_raw, (pr == pc).astype(F32), "a").astype(BF16)
        dz_ref[:, OFF_DT + 128:] = jnp.zeros((Q, NP - OFF_DT - 128), BF16)
        ddtb_ref[...] += jnp.sum(ddt_raw, axis=0, keepdims=True)
        dal_ref[...] += jnp.sum(da * dt, axis=0, keepdims=True) * A
        dD_ref[...] += dD16[0:1, :]

    def rc(c):
        return nc - 1 - c

    in_specs = _ssd_in_specs(nc, True) + [
        pl.BlockSpec((CHUNK, D_INNER), lambda c: (rc(c), 0)),
        pl.BlockSpec((1, N_GROUPS, D_STATE, GROUP_W), lambda c: (rc(c), 0, 0, 0)),
        pl.BlockSpec((CHUNK, D_INNER), lambda c: (rc(c), 0)),
    ]
    small = pl.BlockSpec((1, N_HEADS), lambda c: (0, 0))
    return pl.pallas_call(
        body, grid=(nc,), in_specs=in_specs,
        out_specs=[pl.BlockSpec((CHUNK, NP), lambda c: (rc(c), 0)),
                   pl.BlockSpec((CHUNK, CONV_DIM), lambda c: (rc(c), 0)),
                   small, small, small,
                   pl.BlockSpec((1, D_INNER), lambda c: (0, 0))],
        out_shape=[jax.ShapeDtypeStruct((T, NP), BF16), jax.ShapeDtypeStruct((T, CONV_DIM), F32),
                   jax.ShapeDtypeStruct((1, N_HEADS), F32), jax.ShapeDtypeStruct((1, N_HEADS), F32),
                   jax.ShapeDtypeStruct((1, N_HEADS), F32), jax.ShapeDtypeStruct((1, D_INNER), F32)],
        scratch_shapes=[pltpu.VMEM((N_GROUPS, D_STATE, GROUP_W), F32)],
        name=name, compiler_params=_cparams(("arbitrary",)),
    )(xbc, xbc, xbc, proj, proj, dtT, dtb, dtbT, alog, alogT, dfull, ng, ypre, hs, dy)


N_PAIRS = ATTN_W // 128
PAIRS_PER_KV = N_PAIRS // 2
ATTN_SCALE = HEAD_DIM ** -0.5


def _kv_variants(kk):
    lo = lax.broadcasted_iota(jnp.int32, kk.shape, 1) < HEAD_DIM
    zero = jnp.zeros_like(kk)
    k00 = jnp.where(lo, kk, zero)
    k11 = jnp.where(lo, zero, kk)
    k01 = pltpu.roll(k00, HEAD_DIM, axis=1)
    k10 = pltpu.roll(k11, HEAD_DIM, axis=1)
    return [[k00.astype(BF16), k01.astype(BF16)], [k10.astype(BF16), k11.astype(BF16)]]


LOG2E = 1.4426950408889634


def _own_block():
    i = lax.broadcasted_iota(jnp.int32, (WINDOW, WINDOW), 0)
    j = lax.broadcasted_iota(jnp.int32, (WINDOW, WINDOW), 1)
    return j <= i


def _fold(own, a):
    return jnp.where(own, a[:, WINDOW:], a[:, :WINDOW])


def _attn_probs(qp, kvar, own, prev_bias, sk):
    s = _dot_nt(qp, kvar)
    sb = jnp.where(own, s[:, WINDOW:], s[:, :WINDOW] + prev_bias) * (ATTN_SCALE * LOG2E)
    sk2 = sk * LOG2E
    m = jnp.maximum(jnp.max(sb, axis=1, keepdims=True), sk2)
    pe = jnp.exp2(sb - m)
    es = jnp.exp2(sk2 - m)
    den = jnp.sum(pe, axis=1, keepdims=True) + es
    inv = 1.0 / den
    return pe * inv, es * inv


def _unfold(own, a):
    zero = jnp.zeros_like(a)
    return jnp.where(own, zero, a), jnp.where(own, a, zero)


def _sink(sinks, r):
    lane = lax.broadcasted_iota(jnp.int32, sinks.shape, 1)
    return jnp.sum(jnp.where(lane == r, sinks, 0.0), axis=1, keepdims=True)


def _kv_specs():
    return [pl.BlockSpec((WINDOW, KV_W), lambda n: (jnp.maximum(n - 1, 0), OFF_K // KV_W)),
            pl.BlockSpec((WINDOW, KV_W), lambda n: (n, OFF_K // KV_W)),
            pl.BlockSpec((WINDOW, KV_W), lambda n: (jnp.maximum(n - 1, 0), OFF_V // KV_W)),
            pl.BlockSpec((WINDOW, KV_W), lambda n: (n, OFF_V // KV_W))]


def _attn_fwd(proj, sinks, og, ycat, name):
    T = proj.shape[0]
    nb = T // WINDOW

    def body(q_ref, kp_ref, kc_ref, vp_ref, vc_ref, s_ref, og_ref, _, y_ref, o_ref):
        n = pl.program_id(0)
        kv = _kv_variants(jnp.concatenate([kp_ref[...], kc_ref[...]], axis=0))
        vv = _kv_variants(jnp.concatenate([vp_ref[...], vc_ref[...]], axis=0))
        own = _own_block()
        prev_bias = jnp.where(n > 0, 0.0, NEG)
        sinks_v = s_ref[...]
        ssq = jnp.zeros((WINDOW, 1), F32)
        for p in range(N_PAIRS):
            j = p // PAIRS_PER_KV
            qp = q_ref[:, p * 128:(p + 1) * 128].astype(BF16)
            o_pair = jnp.zeros((WINDOW, 128), F32)
            for par in range(2):
                pn, _ = _attn_probs(qp, kv[j][par], own, prev_bias, _sink(sinks_v, 2 * p + par))
                p_prev, p_own = _unfold(own, pn.astype(BF16))
                o_pair = o_pair + _dot_nn(p_prev, vv[j][par][:WINDOW]) + _dot_nn(p_own, vv[j][par][WINDOW:])
            o_ref[:, p * 128:(p + 1) * 128] = o_pair
            ssq = ssq + jnp.sum(o_pair * o_pair, axis=1, keepdims=True)
        rn = lax.rsqrt(ssq * (1.0 / ATTN_W) + EPS)
        y_ref[...] = (o_ref[...] * rn * og_ref[...]).astype(BF16)

    return pl.pallas_call(
        body, grid=(nb,),
        in_specs=[pl.BlockSpec((WINDOW, ATTN_W), lambda n: (n, OFF_Q // ATTN_W)), *_kv_specs(),
                  pl.BlockSpec((1, N_HEADS), lambda n: (0, 0)), pl.BlockSpec((1, ATTN_W), lambda n: (0, 0)), ANY],
        out_specs=[pl.BlockSpec((WINDOW, ATTN_W), lambda n: (n, 1)), pl.BlockSpec((WINDOW, ATTN_W), lambda n: (n, 0))],
        out_shape=[jax.ShapeDtypeStruct(ycat.shape, BF16), jax.ShapeDtypeStruct((T, ATTN_W), F32)],
        input_output_aliases={7: 0}, name=name, compiler_params=_cparams(("parallel",)),
    )(proj, proj, proj, proj, proj, sinks, og, ycat)


def _attn_bwd(proj, sinks, og, o, dy, dproj, name):
    T = proj.shape[0]
    nb = T // WINDOW

    def body(q_ref, kp_ref, kc_ref, vp_ref, vc_ref, s_ref, og_ref, o_ref, dy_ref, _,
             dq_ref, dk_ref, dv_ref, ds_ref, dog_ref):
        n = pl.program_id(0)

        @pl.when(n == 0)
        def _():
            dk_ref[...] = jnp.zeros_like(dk_ref)
            dv_ref[...] = jnp.zeros_like(dv_ref)
            ds_ref[...] = jnp.zeros_like(ds_ref)
            dog_ref[...] = jnp.zeros_like(dog_ref)

        kv = _kv_variants(jnp.concatenate([kp_ref[...], kc_ref[...]], axis=0))
        vv = _kv_variants(jnp.concatenate([vp_ref[...], vc_ref[...]], axis=0))
        own = _own_block()
        prev_bias = jnp.where(n > 0, 0.0, NEG)
        sinks_v = s_ref[...]
        of = o_ref[...]
        rn = lax.rsqrt(jnp.mean(of * of, axis=-1, keepdims=True) + EPS)
        oh = of * rn
        dyf = dy_ref[...]
        dog_ref[...] += jnp.sum(dyf * oh, axis=0, keepdims=True)
        doh = dyf * og_ref[...]
        do = rn * (doh - oh * jnp.mean(doh * oh, axis=-1, keepdims=True))
        lane = lax.broadcasted_iota(jnp.int32, (1, 128), 1)
        lane16 = lax.broadcasted_iota(jnp.int32, (1, N_HEADS), 1)
        sub = lax.broadcasted_iota(jnp.int32, (128, 1), 0)
        dk_acc = [[[jnp.zeros((128, WINDOW), F32) for _ in range(2)] for _ in range(2)] for _ in range(2)]
        dv_acc = [[[jnp.zeros((128, WINDOW), F32) for _ in range(2)] for _ in range(2)] for _ in range(2)]
        dsink = jnp.zeros((1, N_HEADS), F32)
        for p in range(N_PAIRS):
            j = p // PAIRS_PER_KV
            q_f = q_ref[:, p * 128:(p + 1) * 128]
            qp = q_f.astype(BF16)
            q_t = q_f.T.astype(BF16)
            do_p = do[:, p * 128:(p + 1) * 128]
            o_p = of[:, p * 128:(p + 1) * 128]
            do_b = do_p.astype(BF16)
            do_t = do_p.T.astype(BF16)
            prod = do_p * o_p
            dq_pair = jnp.zeros((WINDOW, 128), F32)
            for par in range(2):
                r = 2 * p + par
                half = (lane < HEAD_DIM) if par == 0 else (lane >= HEAD_DIM)
                rows_half = (sub < HEAD_DIM) if par == 0 else (sub >= HEAD_DIM)
                pn, ps = _attn_probs(qp, kv[j][par], own, prev_bias, _sink(sinks_v, r))
                delta = jnp.sum(jnp.where(half, prod, 0.0), axis=1, keepdims=True)
                dP = _fold(own, _dot_nt(do_b, vv[j][par]))
                dS = pn * (dP - delta)
                dsink = dsink + jnp.where(lane16 == r, -jnp.sum(ps * delta, axis=0, keepdims=True), 0.0)
                dS_parts = _unfold(own, (dS * ATTN_SCALE).astype(BF16))
                p_parts = _unfold(own, pn.astype(BF16))
                q_th = jnp.where(rows_half, q_t, jnp.zeros_like(q_t))
                do_th = jnp.where(rows_half, do_t, jnp.zeros_like(do_t))
                for blk in range(2):
                    dq_pair = dq_pair + _dot_nn(dS_parts[blk], kv[j][par][blk * WINDOW:(blk + 1) * WINDOW])
                    dk_acc[j][par][blk] = dk_acc[j][par][blk] + _dot_nn(q_th, dS_parts[blk])
                    dv_acc[j][par][blk] = dv_acc[j][par][blk] + _dot_nn(do_th, p_parts[blk])
            dq_ref[:, p * 128:(p + 1) * 128] = dq_pair.astype(BF16)
        rows = [pl.multiple_of(jnp.maximum(n - 1, 0) * WINDOW, WINDOW), pl.multiple_of(n * WINDOW, WINDOW)]
        for acc, ref in [(dk_acc, dk_ref), (dv_acc, dv_ref)]:
            for blk in range(2):
                both_t = (acc[0][0][blk] + pltpu.roll(acc[0][1][blk], HEAD_DIM, axis=0)
                          + acc[1][1][blk] + pltpu.roll(acc[1][0][blk], HEAD_DIM, axis=0))
                ref[pl.ds(rows[blk], WINDOW), :] += both_t.T
        ds_ref[...] += dsink

    full_kv = pl.BlockSpec((T, KV_W), lambda n: (0, 0))
    blk = pl.BlockSpec((WINDOW, ATTN_W), lambda n: (n, 0))
    return pl.pallas_call(
        body, grid=(nb,),
        in_specs=[pl.BlockSpec((WINDOW, ATTN_W), lambda n: (n, OFF_Q // ATTN_W)), *_kv_specs(),
                  pl.BlockSpec((1, N_HEADS), lambda n: (0, 0)), pl.BlockSpec((1, ATTN_W), lambda n: (0, 0)),
                  blk, pl.BlockSpec((WINDOW, ATTN_W), lambda n: (n, 1)), ANY],
        out_specs=[pl.BlockSpec((WINDOW, ATTN_W), lambda n: (n, OFF_Q // ATTN_W)), full_kv, full_kv,
                   pl.BlockSpec((1, N_HEADS), lambda n: (0, 0)), pl.BlockSpec((1, ATTN_W), lambda n: (0, 0))],
        out_shape=[jax.ShapeDtypeStruct(dproj.shape, BF16), jax.ShapeDtypeStruct((T, KV_W), F32),
                   jax.ShapeDtypeStruct((T, KV_W), F32), jax.ShapeDtypeStruct((1, N_HEADS), F32),
                   jax.ShapeDtypeStruct((1, ATTN_W), F32)],
        input_output_aliases={9: 0}, name=name, compiler_params=_cparams(("arbitrary",)),
    )(proj, proj, proj, proj, proj, sinks, og, o, dy, dproj)


ANY = pl.BlockSpec(memory_space=pl.ANY)


def _coords():
    return lax.axis_index("x"), lax.axis_index("y"), lax.axis_index("c")


def _all_gather(arrs, name):
    n = len(arrs)

    def body(*refs):
        ins, outs = refs[:n], refs[n:2 * n]
        send_sems, recv_sems, local_sems = refs[2 * n:]
        x, y, c = _coords()
        me = 4 * x + 2 * y + c
        sibling = (x, y, 1 - c)
        chips = [(1 - x, y), (x, 1 - y), (1 - x, 1 - y)]

        def copy(a, k, block, to, src=None):
            dst = outs[a].at[block]
            return pltpu.make_async_remote_copy(
                src_ref=dst if src is None else src, dst_ref=dst, send_sem=send_sems.at[a, k],
                recv_sem=recv_sems.at[a, k], device_id=to, device_id_type=MESH)

        mine = [pltpu.make_async_copy(ins[a], outs[a].at[me], local_sems.at[a]) for a in range(n)]
        for cp in mine:
            cp.start()
        first = []
        for a in range(n):
            first.append(copy(a, 0, me, sibling, src=ins[a]))
            for j, chip in enumerate(chips):
                first.append(copy(a, 1 + j, me, (*chip, c), src=ins[a]))
        for cp in first:
            cp.start()
        passed = []
        for j, (px, py) in enumerate(chips):
            blk = 4 * px + 2 * py + c
            for a in range(n):
                copy(a, 1 + j, blk, sibling).wait_recv()
                fwd = copy(a, 4 + j, blk, sibling)
                fwd.start()
                passed.append(fwd)
        for a in range(n):
            copy(a, 0, 4 * x + 2 * y + (1 - c), sibling).wait_recv()
            for j, (px, py) in enumerate(chips):
                copy(a, 4 + j, 4 * px + 2 * py + (1 - c), sibling).wait_recv()
        for cp in first + passed:
            cp.wait_send()
        for cp in mine:
            cp.wait()

    return pl.pallas_call(
        body, in_specs=[ANY] * n, out_specs=[ANY] * n,
        out_shape=[jax.ShapeDtypeStruct((N_DEV,) + a.shape, a.dtype) for a in arrs],
        scratch_shapes=[pltpu.SemaphoreType.DMA((n, 7)), pltpu.SemaphoreType.DMA((n, 7)),
                        pltpu.SemaphoreType.DMA((n,))],
        name=name,
    )(*arrs)


def _exchange_pair(arrs, name):
    n = len(arrs)

    def body(*refs):
        ins, outs = refs[:n], refs[n:2 * n]
        send_sems, recv_sems = refs[2 * n:]
        x, y, c = _coords()
        cps = []
        for a in range(n):
            for q in range(4):
                cps.append(pltpu.make_async_remote_copy(
                    src_ref=ins[a].at[2 * q + (1 - c)], dst_ref=outs[a].at[q], send_sem=send_sems.at[a, q],
                    recv_sem=recv_sems.at[a, q], device_id=(x, y, 1 - c), device_id_type=MESH))
        for cp in cps:
            cp.start()
        for cp in cps:
            cp.wait()

    return pl.pallas_call(
        body, in_specs=[ANY] * n, out_specs=[ANY] * n,
        out_shape=[jax.ShapeDtypeStruct((4,) + a.shape[1:], a.dtype) for a in arrs],
        scratch_shapes=[pltpu.SemaphoreType.DMA((n, 4)), pltpu.SemaphoreType.DMA((n, 4))],
        name=name,
    )(*arrs)


def _exchange_chips(arrs, name):
    n = len(arrs)

    def body(*refs):
        ins, outs = refs[:n], refs[n:2 * n]
        send_sems, recv_sems = refs[2 * n:]
        x, y, c = _coords()
        chips = [(1 - x, y), (x, 1 - y), (1 - x, 1 - y)]
        cps = []
        for a in range(n):
            for k, (tx, ty) in enumerate(chips):
                cps.append(pltpu.make_async_remote_copy(
                    src_ref=ins[a].at[2 * tx + ty], dst_ref=outs[a].at[k], send_sem=send_sems.at[a, k],
                    recv_sem=recv_sems.at[a, k], device_id=(tx, ty, c), device_id_type=MESH))
        for cp in cps:
            cp.start()
        for cp in cps:
            cp.wait()

    return pl.pallas_call(
        body, in_specs=[ANY] * n, out_specs=[ANY] * n,
        out_shape=[jax.ShapeDtypeStruct((3,) + a.shape[1:], a.dtype) for a in arrs],
        scratch_shapes=[pltpu.SemaphoreType.DMA((n, 3)), pltpu.SemaphoreType.DMA((n, 3))],
        name=name,
    )(*arrs)


HBM = pl.BlockSpec(memory_space=pltpu.HBM)
SEM = pl.BlockSpec(memory_space=pltpu.SEMAPHORE)
EFFECT = pltpu.SideEffectType.DATAFLOW_SIDE_EFFECTING


def _in_hbm(a):
    return pltpu.with_memory_space_constraint(a, pltpu.HBM)


def _remote_start(srcs, lands, plan, n_copies, name, after=None):
    ns, nb = len(srcs), len(srcs) + len(lands)
    n_after = 0 if after is None else 1

    def body(*refs):
        src_refs, land_refs = refs[:ns], refs[ns:nb]
        send_sems, recv_sems = refs[nb + n_after], refs[nb + n_after + 1]
        token = refs[-1]
        x, y, c = _coords()
        for i, (sv, dv, dev) in enumerate(plan(src_refs, land_refs, x, y, c)):
            pltpu.make_async_remote_copy(src_ref=sv, dst_ref=dv, send_sem=send_sems.at[i], recv_sem=recv_sems.at[i],
                                         device_id=dev, device_id_type=MESH).start()
        token[...] = jnp.zeros_like(token)

    bufs = list(srcs) + list(lands)
    outs = pl.pallas_call(
        body, name=name,
        out_shape=(pltpu.SemaphoreType.DMA((n_copies,)), pltpu.SemaphoreType.DMA((n_copies,)),
                   *[pltpu.HBM(b.shape, b.dtype) for b in bufs], jax.ShapeDtypeStruct((8, 128), F32)),
        in_specs=[HBM] * nb + [ANY] * n_after,
        out_specs=(SEM, SEM, *[HBM] * nb, pl.BlockSpec(memory_space=pltpu.VMEM)),
        input_output_aliases={i: 2 + i for i in range(nb)},
        compiler_params=pltpu.CompilerParams(has_side_effects=EFFECT),
    )(*[_in_hbm(b) for b in bufs], *([] if after is None else [after]))
    return outs[0], outs[1], list(outs[2:2 + ns]), list(outs[2 + ns:2 + nb]), outs[-1]


def _remote_wait(started, after, plan, name):
    send_sems, recv_sems, srcs, lands, _ = started
    ns, nb = len(srcs), len(srcs) + len(lands)

    def body(*refs):
        src_refs, land_refs = refs[:ns], refs[ns:nb]
        send_sems, recv_sems = refs[nb], refs[nb + 1]
        x, y, c = _coords()
        for i, (sv, dv, dev) in enumerate(plan(src_refs, land_refs, x, y, c)):
            cp = pltpu.make_async_remote_copy(src_ref=sv, dst_ref=dv, send_sem=send_sems.at[i],
                                              recv_sem=recv_sems.at[i], device_id=dev, device_id_type=MESH)
            cp.wait_send()
            cp.wait_recv()

    bufs = list(srcs) + list(lands)
    outs = pl.pallas_call(
        body, name=name, out_shape=tuple(pltpu.HBM(b.shape, b.dtype) for b in bufs),
        in_specs=[HBM] * nb + [SEM, SEM, ANY], out_specs=tuple([HBM] * nb),
        input_output_aliases={i: i for i in range(nb)},
        compiler_params=pltpu.CompilerParams(has_side_effects=EFFECT),
    )(*bufs, send_sems, recv_sems, after)
    return list(outs[:ns]), list(outs[ns:])


def _gather_plan(src_refs, land_refs, x, y, c):
    me = 4 * x + 2 * y + c
    plan = []
    for s, l in zip(src_refs, land_refs):
        for dev in [(x, y, 1 - c), (1 - x, y, c), (x, 1 - y, c), (1 - x, 1 - y, c)]:
            plan.append((s, l.at[me], dev))
    return plan


def _forward_plan(src_refs, land_refs, x, y, c):
    plan = []
    for l in land_refs:
        for px, py in [(1 - x, y), (x, 1 - y), (1 - x, 1 - y)]:
            blk = l.at[4 * px + 2 * py + c]
            plan.append((blk, blk, (x, y, 1 - c)))
    return plan


def _pair_plan(src_refs, land_refs, x, y, c):
    plan = []
    for s, l in zip(src_refs, land_refs):
        for q in range(4):
            plan.append((s.at[2 * q + (1 - c)], l.at[q], (x, y, 1 - c)))
    return plan


def _pair4_plan(src_refs, land_refs, x, y, c):
    plan = []
    for s, l in zip(src_refs, land_refs):
        for q in range(4):
            plan.append((s.at[q], l.at[q], (x, y, 1 - c)))
    return plan


def _chips_plan(src_refs, land_refs, x, y, c):
    plan = []
    for s, l in zip(src_refs, land_refs):
        for k, (tx, ty) in enumerate([(1 - x, y), (x, 1 - y), (1 - x, 1 - y)]):
            plan.append((s.at[2 * tx + ty], l.at[k], (tx, ty, c)))
    return plan


def _everyone_plan(src_refs, land_refs, x, y, c):
    me = 4 * x + 2 * y + c
    plan = []
    for s, l in zip(src_refs, land_refs):
        for fx, fy, fc in [(0, 0, 1), (1, 0, 0), (1, 0, 1), (0, 1, 0), (0, 1, 1), (1, 1, 0), (1, 1, 1)]:
            dev = ((1 - x) if fx else x, (1 - y) if fy else y, (1 - c) if fc else c)
            plan.append((s, l.at[me], dev))
    return plan


def _gather_finish(gathered, name):
    n = len(gathered)

    def body(*refs):
        outs = refs[n:2 * n]
        send_sems, recv_sems = refs[2 * n:]
        x, y, c = _coords()
        cps = []
        for a in range(n):
            for j, (px, py) in enumerate([(1 - x, y), (x, 1 - y), (1 - x, 1 - y)]):
                blk = outs[a].at[4 * px + 2 * py + c]
                got = outs[a].at[4 * px + 2 * py + (1 - c)]
                cps.append((pltpu.make_async_remote_copy(
                    src_ref=blk, dst_ref=blk, send_sem=send_sems.at[a, j], recv_sem=recv_sems.at[a, j],
                    device_id=(x, y, 1 - c), device_id_type=MESH), pltpu.make_async_remote_copy(
                    src_ref=got, dst_ref=got, send_sem=send_sems.at[a, j], recv_sem=recv_sems.at[a, j],
                    device_id=(x, y, 1 - c), device_id_type=MESH)))
        for cp, _ in cps:
            cp.start()
        for cp, arrival in cps:
            cp.wait_send()
            arrival.wait_recv()

    return pl.pallas_call(
        body, in_specs=[ANY] * n, out_specs=[ANY] * n,
        out_shape=[jax.ShapeDtypeStruct(g.shape, g.dtype) for g in gathered],
        input_output_aliases={a: a for a in range(n)},
        scratch_shapes=[pltpu.SemaphoreType.DMA((n, 3)), pltpu.SemaphoreType.DMA((n, 3))],
        name=name,
    )(*gathered)


def _pair_add(g8, r1, csel, tr, name):
    _, R, C = r1.shape
    g4 = g8.reshape(4, 2, R, C)

    def body(c_ref, g_ref, r_ref, o_ref):
        o_ref[...] = (g_ref[...].astype(F32) + r_ref[...].astype(F32)).astype(BF16)

    return pl.pallas_call(
        body,
        grid_spec=pltpu.PrefetchScalarGridSpec(
            num_scalar_prefetch=1, grid=(4, R // tr),
            in_specs=[pl.BlockSpec((None, None, tr, C), lambda q, i, cs: (q, cs[0], i, 0)),
                      pl.BlockSpec((None, tr, C), lambda q, i, cs: (q, i, 0))],
            out_specs=pl.BlockSpec((None, tr, C), lambda q, i, cs: (q, i, 0))),
        out_shape=jax.ShapeDtypeStruct((4, R, C), BF16), name=name,
        compiler_params=_cparams(("parallel", "parallel")),
    )(csel, g4, r1)


def _adamw_math(w, g, m, v):
    m = ADAM_B1 * m + (1.0 - ADAM_B1) * g
    v = ADAM_B2 * v + (1.0 - ADAM_B2) * (g * g)
    m_hat = m / (1.0 - ADAM_B1 ** ADAM_STEP)
    v_hat = v / (1.0 - ADAM_B2 ** ADAM_STEP)
    delta = -ADAM_LR * (m_hat / (jnp.sqrt(v_hat) + ADAM_EPS) + ADAM_WD * w)
    return delta, m, v


def _adamw_big(w, m, v, p4, r3, qsel, tile, name):
    R, C = w.shape
    tr, tc = tile

    def body(q_ref, w_ref, m_ref, v_ref, p_ref, r_ref, g_out, d_out, m_out, v_out):
        g = p_ref[...].astype(F32) + r_ref[0].astype(F32) + r_ref[1].astype(F32) + r_ref[2].astype(F32)
        d, mn, vn = _adamw_math(w_ref[...], g, m_ref[...], v_ref[...])
        g_out[...] = g
        d_out[...] = d
        m_out[...] = mn
        v_out[...] = vn

    blk = pl.BlockSpec((tr, tc), lambda i, j, qs: (i, j))
    return pl.pallas_call(
        body,
        grid_spec=pltpu.PrefetchScalarGridSpec(
            num_scalar_prefetch=1, grid=(R // tr, C // tc),
            in_specs=[blk, blk, blk, pl.BlockSpec((None, tr, tc), lambda i, j, qs: (qs[0], i, j)),
                      pl.BlockSpec((3, tr, tc), lambda i, j, qs: (0, i, j))],
            out_specs=[blk, blk, blk, blk]),
        out_shape=[jax.ShapeDtypeStruct((R, C), F32)] * 4, name=name,
        compiler_params=_cparams(("parallel", "parallel")),
    )(qsel, w, m, v, p4, r3)


def _sum_partials(p4, r3, qsel, tc, name):
    _, R, C = p4.shape

    def body(q_ref, p_ref, r_ref, o_ref):
        o_ref[...] = p_ref[...].astype(F32) + r_ref[0].astype(F32) + r_ref[1].astype(F32) + r_ref[2].astype(F32)

    return pl.pallas_call(
        body,
        grid_spec=pltpu.PrefetchScalarGridSpec(
            num_scalar_prefetch=1, grid=(C // tc,),
            in_specs=[pl.BlockSpec((None, R, tc), lambda j, qs: (qs[0], 0, j)),
                      pl.BlockSpec((3, R, tc), lambda j, qs: (0, 0, j))],
            out_specs=pl.BlockSpec((R, tc), lambda j, qs: (0, j))),
        out_shape=jax.ShapeDtypeStruct((R, C), F32), name=name, compiler_params=_cparams(("parallel",)),
    )(qsel, p4, r3)


def _adamw_tiled(w, g, m, v, tc, name):
    R, C = w.shape

    def body(w_ref, g_ref, m_ref, v_ref, d_out, m_out, v_out):
        d, mn, vn = _adamw_math(w_ref[...], g_ref[...], m_ref[...], v_ref[...])
        d_out[...] = d
        m_out[...] = mn
        v_out[...] = vn

    blk = pl.BlockSpec((R, tc), lambda j: (0, j))
    return pl.pallas_call(
        body, grid=(C // tc,), in_specs=[blk] * 4, out_specs=[blk] * 3,
        out_shape=[jax.ShapeDtypeStruct((R, C), F32)] * 3, name=name, compiler_params=_cparams(("parallel",)),
    )(w, g, m, v)


def _small_sum(parts, name):
    def body(p_ref, o_ref):
        acc = p_ref[0]
        for d in range(1, N_DEV):
            acc = acc + p_ref[d]
        o_ref[...] = acc

    return pl.pallas_call(
        body, out_shape=jax.ShapeDtypeStruct(parts.shape[1:], F32), name=name,
        compiler_params=_cparams(),
    )(parts)


def _adamw_small(w, g, m, v, name):
    def body(w_ref, g_ref, m_ref, v_ref, d_out, m_out, v_out):
        d, mn, vn = _adamw_math(w_ref[...], g_ref[...], m_ref[...], v_ref[...])
        d_out[...] = d
        m_out[...] = mn
        v_out[...] = vn

    return pl.pallas_call(
        body, out_shape=[jax.ShapeDtypeStruct(w.shape, F32)] * 3, name=name, compiler_params=_cparams(),
    )(w, g, m, v)


def _row(*pieces):
    r = jnp.concatenate([p.reshape(1, -1) for p in pieces], axis=1)
    return jnp.pad(r, ((0, 0), (0, D_MODEL - r.shape[1])))


def _pack_small(mix, convb, ssmg, attng, mlpg, fing, convw, dtb, alog, dsk, sinks, extra=None):
    last = [dtb, alog, dsk, sinks] + ([extra] if extra is not None else [])
    rows = [_row(mix), _row(convb), _row(ssmg, attng), _row(mlpg), _row(fing),
            jnp.pad(convw, ((0, 0), (0, D_MODEL - convw.shape[1]))), _row(*last)]
    packed = jnp.concatenate(rows, axis=0)
    return jnp.pad(packed, ((0, SMALL_ROWS - packed.shape[0]), (0, 0)))


def _unpack_small(p, conv_n):
    return dict(
        mix_norm_g=p[0:1, :], conv_b=p[1:2, :], ssm_norm_g=p[2:3, :D_INNER], attn_out_norm_g=p[2:3, D_INNER:],
        mlp_norm_g=p[3:4, :], final_norm_g=p[4, :], conv_w=p[5:9, :conv_n][None],
        dt_bias=p[9:10, 0:16], A_log=p[9:10, 16:32], D_skip=p[9:10, 32:48], attn_sinks=p[9:10, 48:64])


SMALL_NAMES = ["mix_norm_g", "conv_w", "conv_b", "dt_bias", "A_log", "D_skip", "ssm_norm_g", "attn_sinks",
               "attn_out_norm_g", "mlp_norm_g", "final_norm_g"]
WEIGHT_ORDER = ["mix_norm_g", "w_in", "conv_w", "conv_b", "dt_bias", "A_log", "D_skip", "ssm_norm_g", "attn_sinks",
                "attn_out_norm_g", "w_out", "mlp_norm_g", "w_up", "w_down", "final_norm_g"]


def _to_my_columns(w_nat):
    pad = jnp.zeros((w_nat.shape[0], NP - IN_PROJ), w_nat.dtype)
    return jnp.concatenate([w_nat[:, :NAT_DT], w_nat[:, NAT_DT + N_HEADS:], w_nat[:, NAT_DT:NAT_DT + N_HEADS], pad],
                           axis=1)


PER = IN_PROJ // N_DEV
SUPER_STEP = 544
SUPER = 576


def _natural_rows(g, lo, hi):
    segments = [(0, NAT_DT, 0), (NAT_DT, NAT_DT + N_HEADS, OFF_DT - NAT_DT), (NAT_DT + N_HEADS, IN_PROJ, -N_HEADS),
                (IN_PROJ, NP, 0)]
    pieces = [g[max(lo, a) + shift:min(hi, b) + shift] for a, b, shift in segments if max(lo, a) < min(hi, b)]
    return pieces[0] if len(pieces) == 1 else jnp.concatenate(pieces, axis=0)


def _w_in_from_super_slabs(sup):
    seam = SUPER - SUPER_STEP
    units = []
    for i in range(N_DEV):
        base = SUPER_STEP * i
        units.append((base, base + seam, sup[i, :seam] if i == 0 else sup[i - 1, SUPER_STEP:] + sup[i, :seam]))
        units.append((base + seam, base + SUPER_STEP, sup[i, seam:SUPER_STEP]))
    units.append((SUPER_STEP * N_DEV, SUPER_STEP * N_DEV + seam, sup[N_DEV - 1, SUPER_STEP:]))

    def natural(lo, hi):
        return [rows[max(lo, a) - a:min(hi, b) - a] for a, b, rows in units if max(lo, a) < min(hi, b)]

    pieces = natural(0, NAT_DT) + natural(NAT_DT + N_HEADS, IN_PROJ) + natural(NAT_DT, NAT_DT + N_HEADS)
    return jnp.concatenate(pieces + [jnp.zeros((NP - IN_PROJ, D_MODEL), sup.dtype)], axis=0)


def _to_natural_columns(w_my):
    return jnp.concatenate([w_my[:, :NAT_DT], w_my[:, OFF_DT:OFF_DT + N_HEADS], w_my[:, NAT_DT:OFF_DT]], axis=1)


SLAB = 1024


def _grad_w_up(h2, du, name, sel=None, add=None, after=None):
    T, D = h2.shape
    if sel is None:
        pick, n_slab, pre = (lambda j, *cs: j), N_DEV, None
    else:
        pre, other = sel
        pick, n_slab = (lambda j, cs: 2 * j + ((1 - cs[0]) if other else cs[0])), 4
    o_spec = pl.BlockSpec((None, SLAB, SLAB), lambda i, j, k, *cs: (j, i, 0))
    return _matmul(
        h2, du, mode="tn", grid=(D // SLAB, n_slab, 1),
        a_spec=pl.BlockSpec((T, SLAB), lambda i, j, k, *cs: (0, i)),
        b_spec=pl.BlockSpec((T, SLAB), lambda i, j, k, *cs: (0, pick(j, *cs))),
        out_shapes=[jax.ShapeDtypeStruct((n_slab, D, SLAB), BF16)], out_specs=[o_spec], tile=(SLAB, SLAB), name=name,
        extras=() if add is None else (add,), extra_specs=() if add is None else (o_spec,),
        epilogue=None if add is None else (lambda acc, r: (acc + r.astype(F32),)), after=after, prefetch=pre)[0]


def _grad_w_down(act, dx3b, name, sel=None, add=None, after=None):
    T, D = dx3b.shape
    if sel is None:
        pick, n_slab, pre = (lambda i, *cs: i), N_DEV, None
    else:
        pre, other = sel
        pick, n_slab = (lambda i, cs: 2 * i + ((1 - cs[0]) if other else cs[0])), 4
    o_spec = pl.BlockSpec((None, SLAB, SLAB), lambda i, j, k, *cs: (i, 0, j))
    return _matmul(
        act, dx3b, mode="tn", grid=(n_slab, D // SLAB, 1),
        a_spec=pl.BlockSpec((T, SLAB), lambda i, j, k, *cs: (0, pick(i, *cs))),
        b_spec=pl.BlockSpec((T, SLAB), lambda i, j, k, *cs: (0, j)),
        out_shapes=[jax.ShapeDtypeStruct((n_slab, SLAB, D), BF16)], out_specs=[o_spec], tile=(SLAB, SLAB), name=name,
        extras=() if add is None else (add,), extra_specs=() if add is None else (o_spec,),
        epilogue=None if add is None else (lambda acc, r: (acc + r.astype(F32),)), after=after, prefetch=pre)[0]


class _FixedWeights:
    def __init__(self, w_in_p, w_out_f, w_up_s, w_down_f, conv_w_f):
        self.w = (w_in_p, w_out_f, w_up_s, w_down_f, conv_w_f)
        self.grads = {}

    def mixer_weights(self, after):
        return self.w[0], self.w[4], None

    def prefetch(self, k, after):
        return None

    def out_weight(self, after):
        return self.w[1]

    def up_weight(self, after):
        return self.w[2]

    def down_weight(self, after):
        return self.w[3]

    def mlp_grads(self, h2, du, act, dx3b):
        self.grads.update(w_up=_grad_w_up(h2, du, "grad_w_up"),
                          w_down=_grad_w_down(act, dx3b, "grad_w_down").reshape(D_FF, D_MODEL))
        return None

    def out_grad(self, g_out):
        self.grads.update(w_out=g_out)
        return None

    def in_grad(self, g_in):
        self.grads.update(w_in=g_in)
        return None


def _local_step(x, tgt, p, hooks):
    T = x.shape[0]
    D = D_MODEL
    h1 = _rmsnorm_fwd(x, p["mix_norm_g"], "norm_mix")
    w_in_t, conv_w_f, token = hooks.mixer_weights(h1)
    (proj,) = _mm_simple(h1, w_in_t, mode="nt", M=T, N=NP, K=D, tm=min(T, 1024), tn=1536, tk=D, out_dtype=F32,
                         name="in_proj", after=token)
    xbc = _conv_fwd(proj, conv_w_f, p["conv_b"], "conv_fwd")
    dtT = proj[:, OFF_DT:OFF_DT + N_HEADS].T
    dtbT = p["dt_bias"].T
    alogT = p["A_log"].T
    dfull = jnp.repeat(p["D_skip"], HEAD_DIM, axis=1)
    token = hooks.prefetch("out", xbc)
    ssm_g = p["ssm_norm_g"] if token is None else p["ssm_norm_g"] + token[0:1, 0:1]
    ycat, ypre, hs = _ssd_fwd(xbc, proj, dtT, p["dt_bias"], dtbT, p["A_log"], alogT, dfull, ssm_g, "ssd_fwd")
    ycat, o_att = _attn_fwd(proj, p["attn_sinks"], p["attn_out_norm_g"], ycat, "attn_fwd")
    token = hooks.prefetch("up", ycat)
    w_out_f = hooks.out_weight(ycat if token is None else token)
    tm = min(T, 1024)
    (x2,) = _mm_simple(ycat, w_out_f, mode="nn", M=T, N=D, K=D, tm=tm, tn=1024, tk=D, out_dtype=F32, name="out_proj",
                       extras=(x,), epilogue=lambda acc, res: (acc + res,))
    h2 = _rmsnorm_fwd(x2, p["mlp_norm_g"], "norm_mlp")
    w_up_s = hooks.up_weight(h2)
    grid = (T // tm, N_DEV, 1)
    u, act = _matmul(
        h2, w_up_s, mode="nn", grid=grid,
        a_spec=pl.BlockSpec((tm, D), lambda i, j, k: (i, 0)),
        b_spec=pl.BlockSpec((None, D, 1024), lambda i, j, k: (j, 0, 0)),
        out_shapes=[jax.ShapeDtypeStruct((T, D_FF), F32), jax.ShapeDtypeStruct((T, D_FF), BF16)],
        out_specs=[pl.BlockSpec((tm, 1024), lambda i, j, k: (i, j))] * 2, tile=(tm, 1024), name="mlp_up",
        epilogue=lambda acc: (acc, jnp.square(jnp.maximum(acc, 0.0))))
    w_down_f = hooks.down_weight(act)
    (x3,) = _mm_simple(act, w_down_f, mode="nn", M=T, N=D, K=D_FF, tm=tm, tn=1024, tk=2048, out_dtype=F32,
                       name="mlp_down", extras=(x2,), epilogue=lambda acc, res: (acc + res,))
    loss_part, d_fin, dx3, dx3b = _final_loss(x3, tgt, p["final_norm_g"].reshape(1, D), "loss_head")
    (du,) = _mm_simple(dx3b, w_down_f, mode="nt", M=T, N=D_FF, K=D, tm=tm, tn=1024, tk=D, out_dtype=BF16,
                       name="mlp_down_bwd", extras=(u,),
                       epilogue=lambda acc, uu: (acc * (2.0 * jnp.maximum(uu, 0.0)),))
    token = hooks.mlp_grads(h2, du, act, dx3b)
    (dh2,) = _matmul(
        du, w_up_s, mode="nt", grid=(T // tm, D // 1024, N_DEV // 2),
        a_spec=pl.BlockSpec((tm, 2048), lambda i, j, k: (i, k)),
        b_spec=pl.BlockSpec((2, 1024, 1024), lambda i, j, k: (k, j, 0)),
        out_shapes=[jax.ShapeDtypeStruct((T, D), F32)],
        out_specs=[pl.BlockSpec((tm, 1024), lambda i, j, k: (i, j))], tile=(tm, 1024), name="mlp_up_bwd",
        after=token, dot_fn=lambda a, b: _dot_nt(a[:, :1024], b[0]) + _dot_nt(a[:, 1024:], b[1]))
    dx2, dx2b, d_mlp = _rmsnorm_bwd(dh2, x2, p["mlp_norm_g"], dx3, "norm_mlp_bwd")
    (g_out,) = _mm_simple(ycat, dx2b, mode="tn", M=D, N=D, K=T, tm=1024, tn=1024, tk=T, out_dtype=BF16,
                          name="grad_w_out")
    token = hooks.out_grad(g_out)
    (dy,) = _mm_simple(dx2b, w_out_f, mode="nt", M=T, N=D, K=D, tm=tm, tn=1024, tk=D, out_dtype=F32,
                       name="out_proj_bwd", after=token)
    dproj, dxbc_act, d_dtb, d_alog, d_dskip, d_ssmg = _ssd_bwd(
        xbc, proj, dtT, p["dt_bias"], dtbT, p["A_log"], alogT, dfull, p["ssm_norm_g"], ypre, hs, dy, "ssd_bwd")
    dproj, d_convw, d_convb = _conv_bwd(proj, dxbc_act, conv_w_f, p["conv_b"], dproj, "conv_bwd")
    dproj, dk, dv, d_sinks, d_attng = _attn_bwd(proj, p["attn_sinks"], p["attn_out_norm_g"], o_att, dy, dproj,
                                                "attn_bwd")
    dproj = lax.dynamic_update_slice(dproj, jnp.concatenate([dk, dv], axis=1).astype(BF16), (0, OFF_K))
    (g_in,) = _mm_simple(dproj, h1, mode="tn", M=NP, N=D, K=T, tm=1536, tn=1024, tk=T, out_dtype=BF16,
                         name="grad_w_in")
    token = hooks.in_grad(g_in)
    (dh1,) = _mm_simple(dproj, w_in_t, mode="nn", M=T, N=D, K=NP, tm=tm, tn=1024, tk=2304, out_dtype=F32,
                        name="in_proj_bwd", after=token)
    dx, _, d_mix = _rmsnorm_bwd(dh1, x, p["mix_norm_g"], dx2, "norm_mix_bwd")
    small = _pack_small(d_mix, d_convb, d_ssmg, d_attng, d_mlp, d_fin, d_convw, d_dtb, d_alog, d_dskip, d_sinks,
                        extra=loss_part[:, 0:1])
    return dx, small


def _landing(own, me):
    zone = lax.empty((N_DEV,) + own.shape, own.dtype)
    return lax.dynamic_update_slice(zone, own[None], (me,) + (0,) * own.ndim)


def _sequencer_gather(own, me, collective_id, name):
    zone_ref = jax.new_ref(_landing(own, me), memory_space=pltpu.MemorySpace.HBM)
    own_ref = jax.new_ref(own, memory_space=pltpu.MemorySpace.HBM)

    @pl.kernel(mesh=plsc.ScalarSubcoreMesh(axis_name="sequencer", num_cores=1), name=name,
               scratch_types=(pltpu.SemaphoreType.DMA((7,)), pltpu.SemaphoreType.DMA((7,))),
               compiler_params=pltpu.CompilerParams(collective_id=collective_id))
    def launch(send_sems, recv_sems):
        x, y, c = _coords()
        sibling = (x, y, 1 - c)
        chips = [(1 - x, y), (x, 1 - y), (1 - x, 1 - y)]
        peers = [sibling] + [(px, py, c) for px, py in chips]
        barrier = pltpu.get_barrier_semaphore()
        for peer in peers:
            pl.semaphore_signal(barrier, inc=1, device_id=peer, device_id_type=MESH)
        pl.semaphore_wait(barrier, len(peers))

        def copy(k, src, dst, to):
            return pltpu.make_async_remote_copy(src_ref=src, dst_ref=dst, send_sem=send_sems.at[k],
                                                recv_sem=recv_sems.at[k], device_id=to, device_id_type=MESH)

        mine = zone_ref.at[4 * x + 2 * y + c]
        first = [copy(k, own_ref, mine, peer) for k, peer in enumerate(peers)]
        for cp in first:
            cp.start()
        passed = []
        for j, (px, py) in enumerate(chips):
            first[1 + j].wait_recv()
            blk = zone_ref.at[4 * px + 2 * py + c]
            passed.append(copy(4 + j, blk, blk, sibling))
            passed[-1].start()
        first[0].wait_recv()
        for cp in passed:
            cp.wait_recv()
        for cp in first + passed:
            cp.wait_send()

    launch()
    return zone_ref


def _gather_end(started, after, plan, name):
    _, lands = _remote_wait(started, after, plan, name + "_wait")
    return _gather_finish(lands, name + "_finish")


class _ShardedWeights:
    def __init__(self, w_in, w_out, conv_w, w_up, w_down, me, csel):
        self.me, self.csel = me, csel
        own_rows = lax.dynamic_update_slice(jnp.zeros((SUPER, D_MODEL), F32), jnp.transpose(w_in), (2 * me, 0))
        shards = [own_rows.astype(BF16), conv_w]
        self.st_mixer = _remote_start(shards, [_landing(s, me) for s in shards], _gather_plan, 4 * len(shards),
                                      "gather_start_mixer")
        order = self.st_mixer[4]
        self.st_later = {}
        for k, wt in [("out", w_out), ("up", w_up)]:
            shard = (wt + order[0, 0]).astype(BF16)
            self.st_later[k] = _remote_start([shard], [_landing(shard, me)], _gather_plan, 4, f"gather_start_{k}",
                                             after=order)
            order = self.st_later[k][4]
        self.down_ref = _sequencer_gather((w_down + order[0, 0]).astype(BF16), me, 7, "gather_down_sequencer")
        self.start_token = order
        self.forwards = {}
        self.reduces = {}

    def mixer_weights(self, after):
        g_in, g_conv = _gather_end(self.st_mixer, after, _gather_plan, "gather_mixer")
        conv_w_f = jnp.concatenate([g_conv[i] for i in range(N_DEV)], axis=1)
        return _w_in_from_super_slabs(g_in), conv_w_f, None

    def prefetch(self, k, after):
        _, lands = _remote_wait(self.st_later[k], after, _gather_plan, f"gather_{k}_wait")
        self.forwards[k] = _remote_start([], lands, _forward_plan, 3, f"gather_{k}_forward")
        return self.forwards[k][4]

    def _prefetched(self, k, after):
        return _remote_wait(self.forwards[k], after, _forward_plan, f"gather_{k}_forward_wait")[1][0]

    def out_weight(self, after):
        return self._prefetched("out", after).reshape(D_MODEL, D_MODEL)

    def up_weight(self, after):
        return self._prefetched("up", after)

    def down_weight(self, after):
        return self.down_ref[...].reshape(D_FF, D_MODEL)

    def _chips_start(self, slabs, from_sibling, rows, tag):
        sums = [_pair_add(s, r, self.csel, tr, f"pair_add_{tag}_{i}")
                for i, (s, r, tr) in enumerate(zip(slabs, from_sibling, rows))]
        lands = [lax.empty((3,) + s.shape[1:], s.dtype) for s in sums]
        self.reduces[tag] = _remote_start(sums, lands, _chips_plan, 3 * len(sums), f"reduce_start_{tag}")
        return self.reduces[tag][4]

    def mlp_grads(self, h2, du, act, dx3b):
        def send(part, tag, after):
            st = _remote_start([part], [lax.empty(part.shape, part.dtype)], _pair4_plan, 4,
                               f"reduce_pair_start_{tag}", after=after)
            return st

        def received(st, after, tag):
            return _remote_wait(st, after, _pair4_plan, f"reduce_pair_wait_{tag}")[1][0]

        def to_chips(sums, tag):
            self.reduces[tag] = _remote_start([sums], [lax.empty((3,) + sums.shape[1:], sums.dtype)], _chips_plan, 3,
                                              f"reduce_start_{tag}")
            return self.reduces[tag][4]

        up_send = _grad_w_up(h2, du, "grad_w_up_send", sel=(self.csel, True))
        st_up = send(up_send, "up", None)
        down_send = _grad_w_down(act, dx3b, "grad_w_down_send", sel=(self.csel, True), after=st_up[4])
        st_down = send(down_send, "down", None)
        up_sum = _grad_w_up(h2, du, "grad_w_up_keep", sel=(self.csel, False), add=received(st_up, down_send, "up"),
                            after=st_down[4])
        token = to_chips(up_sum, "up")
        down_sum = _grad_w_down(act, dx3b, "grad_w_down_keep", sel=(self.csel, False),
                                add=received(st_down, up_sum, "down"), after=token)
        return to_chips(down_sum, "down")

    def out_grad(self, g_out):
        slabs = [g_out.reshape(N_DEV, D_MODEL // N_DEV, D_MODEL)]
        return self._chips_start(slabs, _exchange_pair(slabs, "reduce_pair_out"), [256], "out")

    def in_grad(self, g_in):
        slabs = [jnp.stack([_natural_rows(g_in, SUPER_STEP * j, SUPER_STEP * j + SUPER) for j in range(N_DEV)])]
        return self._chips_start(slabs, _exchange_pair(slabs, "reduce_pair_in"), [SUPER], "in")

    def small_start(self, small):
        self.st_small = _remote_start([small], [_landing(small, self.me)], _everyone_plan, N_DEV - 1, "gather_start_small")

    def small_end(self, after):
        return _remote_wait(self.st_small, after, _everyone_plan, "gather_small_wait")[1][0]

    def reduce_end(self, tag, after):
        return _remote_wait(self.reduces[tag], after, _chips_plan, f"reduce_wait_{tag}")


def kernel(x, mix_norm_g, w_in, conv_w, conv_b, dt_bias, A_log, D_skip, ssm_norm_g, attn_sinks, attn_out_norm_g, w_out, mlp_norm_g, w_up, w_down, final_norm_g, loss_target, m_mix_norm_g, m_w_in, m_conv_w, m_conv_b, m_dt_bias, m_A_log, m_D_skip, m_ssm_norm_g, m_attn_sinks, m_attn_out_norm_g, m_w_out, m_mlp_norm_g, m_w_up, m_w_down, m_final_norm_g, v_mix_norm_g, v_w_in, v_conv_w, v_conv_b, v_dt_bias, v_A_log, v_D_skip, v_ssm_norm_g, v_attn_sinks, v_attn_out_norm_g, v_w_out, v_mlp_norm_g, v_w_up, v_w_down, v_final_norm_g):
    xi, yi, ci = _coords()
    me = 4 * xi + 2 * yi + ci
    csel = jnp.reshape(ci, (1,)).astype(jnp.int32)
    qsel = jnp.reshape(2 * xi + yi, (1,)).astype(jnp.int32)
    w = dict(mix_norm_g=mix_norm_g, conv_b=conv_b, dt_bias=dt_bias, A_log=A_log, D_skip=D_skip,
             ssm_norm_g=ssm_norm_g, attn_sinks=attn_sinks, attn_out_norm_g=attn_out_norm_g, mlp_norm_g=mlp_norm_g,
             final_norm_g=final_norm_g)
    hooks = _ShardedWeights(w_in[0], w_out[0], conv_w[0], w_up[0], w_down[0], me, csel)
    p = dict(w, mix_norm_g=mix_norm_g + hooks.start_token[0:1, 0:1])
    dx, small = _local_step(x[0], loss_target[0], p, hooks)
    hooks.small_start(small)
    big = {}
    after = dx
    for name, wt, mt, vt, tile in [
            ("up", w_up, m_w_up, v_w_up, (512, SLAB)), ("down", w_down, m_w_down, v_w_down, (256, D_MODEL)),
            ("out", w_out, m_w_out, v_w_out, (256, D_MODEL))]:
        (chip_sums,), (from_chips,) = hooks.reduce_end(name, after)
        res = _adamw_big(wt[0], mt[0], vt[0], chip_sums, from_chips, qsel, tile, f"adamw_w_{name}")
        big["w_" + name] = tuple(r[None] for r in res)
        after = res[0]
    (chip_sums,), (from_chips,) = hooks.reduce_end("in", after)
    g_super = _sum_partials(chip_sums, from_chips, qsel, 512, "grad_w_in_sum")
    g_in = lax.dynamic_slice(g_super, (2 * me, 0), (PER, D_MODEL))
    res = _adamw_tiled(jnp.transpose(w_in[0]), g_in, jnp.transpose(m_w_in[0]), jnp.transpose(v_w_in[0]), 512,
                       "adamw_w_in")
    big["w_in"] = tuple(jnp.transpose(r)[None] for r in (g_in, *res))
    after = res[0]
    gsum = _small_sum(hooks.small_end(after), "small_sum")
    loss = gsum[9, 64]
    gs = _unpack_small(gsum, CONV_DIM)
    cw = CONV_DIM // N_DEV
    g_conv_shard = lax.dynamic_slice(gsum[5:9, :], (0, me * cw), (CONV_K, cw))

    def pack(s):
        return _pack_small(s["mix_norm_g"], s["conv_b"], s["ssm_norm_g"], s["attn_out_norm_g"], s["mlp_norm_g"],
                           s["final_norm_g"], s["conv_w"][0], s["dt_bias"], s["A_log"], s["D_skip"], s["attn_sinks"])

    wp = pack(dict(w, conv_w=conv_w))
    mp = pack(dict(mix_norm_g=m_mix_norm_g, conv_b=m_conv_b, ssm_norm_g=m_ssm_norm_g,
                   attn_out_norm_g=m_attn_out_norm_g, mlp_norm_g=m_mlp_norm_g, final_norm_g=m_final_norm_g,
                   conv_w=m_conv_w, dt_bias=m_dt_bias, A_log=m_A_log, D_skip=m_D_skip, attn_sinks=m_attn_sinks))
    vp = pack(dict(mix_norm_g=v_mix_norm_g, conv_b=v_conv_b, ssm_norm_g=v_ssm_norm_g,
                   attn_out_norm_g=v_attn_out_norm_g, mlp_norm_g=v_mlp_norm_g, final_norm_g=v_final_norm_g,
                   conv_w=v_conv_w, dt_bias=v_dt_bias, A_log=v_A_log, D_skip=v_D_skip, attn_sinks=v_attn_sinks))
    gp = jnp.concatenate([gsum[0:5], jnp.pad(g_conv_shard, ((0, 0), (0, D_MODEL - cw))), gsum[9:10],
                          jnp.zeros((SMALL_ROWS - 10, D_MODEL), F32)], axis=0)
    dp, mnp, vnp = _adamw_small(wp, gp, mp, vp, "adamw_small")
    grads = dict(gs, conv_w=g_conv_shard[None])
    deltas = _unpack_small(dp, cw)
    new_m = _unpack_small(mnp, cw)
    new_v = _unpack_small(vnp, cw)
    for k, name in enumerate(["w_in", "w_out", "w_up", "w_down"]):
        grads[name], deltas[name], new_m[name], new_v[name] = big[name]
    return (loss, dx[None], *[grads[n] for n in WEIGHT_ORDER], *[deltas[n] for n in WEIGHT_ORDER],
            *[new_m[n] for n in WEIGHT_ORDER], *[new_v[n] for n in WEIGHT_ORDER])
```

```python
import functools

import jax
import jax.numpy as jnp
from jax import lax
from jax.experimental import pallas as pl
from jax.experimental.pallas import tpu as pltpu
from jax.experimental.pallas import tpu_sc as plsc

F32 = jnp.float32
BF16 = jnp.bfloat16
HI = lax.Precision.HIGHEST
MESH = pl.DeviceIdType.MESH

EPS = 1e-5
D_MODEL = 2048
D_INNER = 1024
N_HEADS = 16
HEAD_DIM = 64
N_GROUPS = 4
D_STATE = 128
CHUNK = 128
CONV_K = 4
CONV_DIM = 2048
ATTN_W = 1024
KV_W = 128
WINDOW = 128
D_FF = 8192
IN_PROJ = 4368
N_DEV = 8
NP = 4608
OFF_Z, OFF_X, OFF_B, OFF_C, OFF_Q, OFF_K, OFF_V, OFF_DT = 0, 1024, 2048, 2560, 3072, 4096, 4224, 4352
NAT_DT = 3072

ADAM_LR = 0.001
ADAM_B1 = 0.9
ADAM_B2 = 0.999
ADAM_EPS = 1e-08
ADAM_WD = 0.01
ADAM_STEP = 10

VMEM_LIMIT = 52 * 1024 * 1024
SMALL_ROWS = 16
NEG = -1e30


def _cparams(sem=None):
    return pltpu.CompilerParams(dimension_semantics=sem, vmem_limit_bytes=VMEM_LIMIT)


def _split3(v):
    hi = v.astype(BF16)
    rest = v - hi.astype(F32)
    mid = rest.astype(BF16)
    return hi, mid, (rest - mid.astype(F32)).astype(BF16)


def _hdot(a, b, data):
    if data == "a":
        sel = b.astype(BF16)
        return sum(_dot_nn(part, sel) for part in _split3(a))
    sel = a.astype(BF16)
    return sum(_dot_nn(sel, part) for part in _split3(b))


def _dot_nn(a, b):
    return lax.dot_general(a, b, (((1,), (0,)), ((), ())), preferred_element_type=F32)


def _dot_nt(a, b):
    return lax.dot_general(a, b, (((1,), (1,)), ((), ())), preferred_element_type=F32)


def _dot_tn(a, b):
    return lax.dot_general(a, b, (((0,), (0,)), ((), ())), preferred_element_type=F32)


def _softplus(v):
    return jnp.maximum(v, 0.0) + jnp.log1p(jnp.exp(-jnp.abs(v)))


def _sigmoid(v):
    return 1.0 / (1.0 + jnp.exp(-v))


def _matmul(a, b, *, mode, grid, a_spec, b_spec, out_shapes, out_specs, tile, name,
            extras=(), extra_specs=(), epilogue=None, after=None, dot_fn=None, prefetch=None):
    nk = grid[2]
    n_ex = len(extras)
    n_out = len(out_shapes)
    dot = dot_fn if dot_fn is not None else {"nn": _dot_nn, "nt": _dot_nt, "tn": _dot_tn}[mode]

    def finish(acc, ex_refs, out_refs):
        res = (acc,) if epilogue is None else epilogue(acc, *[e[...] for e in ex_refs])
        for o, r in zip(out_refs, res):
            o[...] = r.astype(o.dtype)

    def body(*refs):
        a_ref, b_ref = refs[0], refs[1]
        ex_refs = refs[2:2 + n_ex]
        out_refs = refs[2 + n_ex:2 + n_ex + n_out]
        part = dot(a_ref[...].astype(BF16), b_ref[...].astype(BF16))
        if nk == 1:
            finish(part, ex_refs, out_refs)
        else:
            acc_ref = refs[-1]
            k = pl.program_id(2)

            @pl.when(k == 0)
            def _():
                acc_ref[...] = part

            @pl.when(k > 0)
            def _():
                acc_ref[...] += part

            @pl.when(k == nk - 1)
            def _():
                finish(acc_ref[...], ex_refs, out_refs)

    scratch = [] if nk == 1 else [pltpu.VMEM(tile, F32)]
    n_pre = 0 if prefetch is None else 1
    tok_specs = [] if after is None else [pl.BlockSpec((8, 128), lambda *_: (0, 0))]
    tok_args = [] if after is None else [after]

    def body_with_token(*refs):
        refs = refs[n_pre:]
        body(*refs[:2 + n_ex], *refs[2 + n_ex + len(tok_args):])

    in_specs = [a_spec, b_spec, *extra_specs, *tok_specs]
    params = _cparams(("parallel", "parallel", "arbitrary"))
    if prefetch is None:
        return pl.pallas_call(
            body_with_token, grid=grid, in_specs=in_specs, out_specs=list(out_specs), out_shape=list(out_shapes),
            scratch_shapes=scratch, name=name, compiler_params=params)(a, b, *extras, *tok_args)
    return pl.pallas_call(
        body_with_token,
        grid_spec=pltpu.PrefetchScalarGridSpec(num_scalar_prefetch=1, grid=grid, in_specs=in_specs,
                                               out_specs=list(out_specs), scratch_shapes=scratch),
        out_shape=list(out_shapes), name=name, compiler_params=params)(prefetch, a, b, *extras, *tok_args)


def _mm_simple(a, b, *, mode, M, N, K, tm, tn, tk, out_dtype, name, extras=(), epilogue=None, n_out=1,
               out_dtypes=None, after=None):
    grid = (M // tm, N // tn, K // tk)
    if mode == "nn":
        a_spec = pl.BlockSpec((tm, tk), lambda i, j, k: (i, k))
        b_spec = pl.BlockSpec((tk, tn), lambda i, j, k: (k, j))
    elif mode == "nt":
        a_spec = pl.BlockSpec((tm, tk), lambda i, j, k: (i, k))
        b_spec = pl.BlockSpec((tn, tk), lambda i, j, k: (j, k))
    else:
        a_spec = pl.BlockSpec((tk, tm), lambda i, j, k: (k, i))
        b_spec = pl.BlockSpec((tk, tn), lambda i, j, k: (k, j))
    o_spec = pl.BlockSpec((tm, tn), lambda i, j, k: (i, j))
    dts = out_dtypes if out_dtypes is not None else [out_dtype] * n_out
    return _matmul(a, b, mode=mode, grid=grid, a_spec=a_spec, b_spec=b_spec,
                   out_shapes=[jax.ShapeDtypeStruct((M, N), d) for d in dts],
                   out_specs=[o_spec] * len(dts), tile=(tm, tn), name=name,
                   extras=extras, extra_specs=[o_spec] * len(extras), epilogue=epilogue, after=after)


ROW_BLOCK = 256


def _rmsnorm_fwd(x, g, name):
    T, D = x.shape

    def body(x_ref, g_ref, o_ref):
        xf = x_ref[...]
        r = lax.rsqrt(jnp.mean(xf * xf, axis=-1, keepdims=True) + EPS)
        o_ref[...] = (xf * r * g_ref[...]).astype(BF16)

    return pl.pallas_call(
        body, grid=(T // ROW_BLOCK,),
        in_specs=[pl.BlockSpec((ROW_BLOCK, D), lambda i: (i, 0)), pl.BlockSpec((1, D), lambda i: (0, 0))],
        out_specs=pl.BlockSpec((ROW_BLOCK, D), lambda i: (i, 0)),
        out_shape=jax.ShapeDtypeStruct((T, D), BF16), name=name, compiler_params=_cparams(("parallel",)),
    )(x, g)


def _rmsnorm_bwd(dh, x, g, dres, name):
    T, D = x.shape

    def body(dh_ref, x_ref, g_ref, dres_ref, dx_ref, dxb_ref, dg_ref):
        i = pl.program_id(0)
        xf = x_ref[...]
        r = lax.rsqrt(jnp.mean(xf * xf, axis=-1, keepdims=True) + EPS)
        xh = xf * r
        d = dh_ref[...]

        @pl.when(i == 0)
        def _():
            dg_ref[...] = jnp.zeros_like(dg_ref)

        dg_ref[...] += jnp.sum(d * xh, axis=0, keepdims=True)
        dxh = d * g_ref[...]
        dx = r * (dxh - xh * jnp.mean(dxh * xh, axis=-1, keepdims=True)) + dres_ref[...]
        dx_ref[...] = dx
        dxb_ref[...] = dx.astype(BF16)

    row = pl.BlockSpec((ROW_BLOCK, D), lambda i: (i, 0))
    vec = pl.BlockSpec((1, D), lambda i: (0, 0))
    return pl.pallas_call(
        body, grid=(T // ROW_BLOCK,), in_specs=[row, row, vec, row], out_specs=[row, row, vec],
        out_shape=[jax.ShapeDtypeStruct((T, D), F32), jax.ShapeDtypeStruct((T, D), BF16),
                   jax.ShapeDtypeStruct((1, D), F32)],
        name=name, compiler_params=_cparams(("arbitrary",)),
    )(dh, x, g, dres)


def _final_loss(x3, tgt, g, name):
    T, D = x3.shape

    def body(x_ref, t_ref, g_ref, loss_ref, dg_ref, dx_ref, dxb_ref):
        i = pl.program_id(0)
        xf = x_ref[...]
        r = lax.rsqrt(jnp.mean(xf * xf, axis=-1, keepdims=True) + EPS)
        xh = xf * r
        gg = g_ref[...]
        err = xh * gg - t_ref[...]

        @pl.when(i == 0)
        def _():
            dg_ref[...] = jnp.zeros_like(dg_ref)
            loss_ref[...] = jnp.zeros_like(loss_ref)

        part = jnp.sum(jnp.sum(err * err, axis=-1, keepdims=True), axis=0, keepdims=True) * (0.5 / D)
        loss_ref[...] += jnp.broadcast_to(part, loss_ref.shape)
        dout = err * (1.0 / D)
        dg_ref[...] += jnp.sum(dout * xh, axis=0, keepdims=True)
        dxh = dout * gg
        dx = r * (dxh - xh * jnp.mean(dxh * xh, axis=-1, keepdims=True))
        dx_ref[...] = dx
        dxb_ref[...] = dx.astype(BF16)

    row = pl.BlockSpec((ROW_BLOCK, D), lambda i: (i, 0))
    vec = pl.BlockSpec((1, D), lambda i: (0, 0))
    return pl.pallas_call(
        body, grid=(T // ROW_BLOCK,), in_specs=[row, row, vec],
        out_specs=[pl.BlockSpec((1, 128), lambda i: (0, 0)), vec, row, row],
        out_shape=[jax.ShapeDtypeStruct((1, 128), F32), jax.ShapeDtypeStruct((1, D), F32),
                   jax.ShapeDtypeStruct((T, D), F32), jax.ShapeDtypeStruct((T, D), BF16)],
        name=name, compiler_params=_cparams(("arbitrary",)),
    )(x3, tgt, g)


CONV_BLOCK = 256


def _conv_apply(u, w, b):
    row = lax.broadcasted_iota(jnp.int32, u.shape, 0)
    acc = b + w[CONV_K - 1:CONV_K, :] * u
    shifted = []
    for j in range(1, CONV_K):
        uj = jnp.where(row >= j, pltpu.roll(u, j, axis=0), 0.0)
        shifted.append(uj)
        acc = acc + w[CONV_K - 1 - j:CONV_K - j, :] * uj
    return acc, shifted


def _conv_fwd(proj, conv_w, conv_b, name):
    T = proj.shape[0]
    cb0 = OFF_X // CONV_BLOCK

    def body(u_ref, w_ref, b_ref, o_ref):
        c, _ = _conv_apply(u_ref[...], w_ref[...], b_ref[...])
        o_ref[...] = c * _sigmoid(c)

    return pl.pallas_call(
        body, grid=(CONV_DIM // CONV_BLOCK,),
        in_specs=[pl.BlockSpec((T, CONV_BLOCK), lambda j: (0, cb0 + j)),
                  pl.BlockSpec((CONV_K, CONV_BLOCK), lambda j: (0, j)),
                  pl.BlockSpec((1, CONV_BLOCK), lambda j: (0, j))],
        out_specs=pl.BlockSpec((T, CONV_BLOCK), lambda j: (0, j)),
        out_shape=jax.ShapeDtypeStruct((T, CONV_DIM), F32), name=name, compiler_params=_cparams(("parallel",)),
    )(proj, conv_w, conv_b)


def _conv_bwd(proj, dact, conv_w, conv_b, dproj, name):
    T = proj.shape[0]
    cb0 = OFF_X // CONV_BLOCK

    def body(u_ref, d_ref, w_ref, b_ref, _, du_ref, dw_ref, db_ref):
        u = u_ref[...]
        w = w_ref[...]
        c, shifted = _conv_apply(u, w, b_ref[...])
        sg = _sigmoid(c)
        dc = d_ref[...] * sg * (1.0 + c * (1.0 - sg))
        row = lax.broadcasted_iota(jnp.int32, u.shape, 0)
        du = w[CONV_K - 1:CONV_K, :] * dc
        dw_ref[CONV_K - 1:CONV_K, :] = jnp.sum(dc * u, axis=0, keepdims=True)
        for j in range(1, CONV_K):
            dcj = jnp.where(row < T - j, pltpu.roll(dc, T - j, axis=0), 0.0)
            du = du + w[CONV_K - 1 - j:CONV_K - j, :] * dcj
            dw_ref[CONV_K - 1 - j:CONV_K - j, :] = jnp.sum(dc * shifted[j - 1], axis=0, keepdims=True)
        db_ref[...] = jnp.sum(dc, axis=0, keepdims=True)
        du_ref[...] = du.astype(BF16)

    return pl.pallas_call(
        body, grid=(CONV_DIM // CONV_BLOCK,),
        in_specs=[pl.BlockSpec((T, CONV_BLOCK), lambda j: (0, cb0 + j)),
                  pl.BlockSpec((T, CONV_BLOCK), lambda j: (0, j)),
                  pl.BlockSpec((CONV_K, CONV_BLOCK), lambda j: (0, j)),
                  pl.BlockSpec((1, CONV_BLOCK), lambda j: (0, j)), pl.BlockSpec(memory_space=pl.ANY)],
        out_specs=[pl.BlockSpec((T, CONV_BLOCK), lambda j: (0, cb0 + j)),
                   pl.BlockSpec((CONV_K, CONV_BLOCK), lambda j: (0, j)),
                   pl.BlockSpec((1, CONV_BLOCK), lambda j: (0, j))],
        out_shape=[jax.ShapeDtypeStruct(dproj.shape, BF16), jax.ShapeDtypeStruct((CONV_K, CONV_DIM), F32),
                   jax.ShapeDtypeStruct((1, CONV_DIM), F32)],
        input_output_aliases={4: 0}, name=name, compiler_params=_cparams(("parallel",)),
    )(proj, dact, conv_w, conv_b, dproj)


GROUP_W = D_INNER // N_GROUPS
HEADS_PER_GROUP = N_HEADS // N_GROUPS


def _expand_mat():
    h = lax.broadcasted_iota(jnp.int32, (N_HEADS, D_INNER), 0)
    j = lax.broadcasted_iota(jnp.int32, (N_HEADS, D_INNER), 1)
    return (j // HEAD_DIM == h).astype(F32)


def _reduce_mat(g):
    j = lax.broadcasted_iota(jnp.int32, (GROUP_W, N_HEADS), 0)
    h = lax.broadcasted_iota(jnp.int32, (GROUP_W, N_HEADS), 1)
    return (g * HEADS_PER_GROUP + j // HEAD_DIM == h).astype(F32)


def _col16(v, h):
    lane = lax.broadcasted_iota(jnp.int32, v.shape, 1)
    return jnp.sum(jnp.where(lane == h, v, 0.0), axis=1, keepdims=True)


def _ssd_pre(dt_raw, dtT_raw, dtb, dtbT, alog, alogT):
    Q = CHUNK
    xdt = dt_raw + dtb
    dt = _softplus(xdt)
    dtT = _softplus(dtT_raw + dtbT)
    A = -jnp.exp(alog)
    AT = -jnp.exp(alogT)
    row = lax.broadcasted_iota(jnp.int32, (Q, Q), 0)
    col = lax.broadcasted_iota(jnp.int32, (Q, Q), 1)
    tril = (row >= col).astype(F32)
    triu = (row <= col).astype(F32)
    cs = _hdot(tril, dt * A, "b")
    csT = _hdot(dtT * AT, triu, "a")
    return xdt, dt, A, cs, csT, row >= col, triu


def _decay_matrix(cs, csT, h, causal):
    seg = _col16(cs, h) - csT[h:h + 1, :]
    return jnp.where(causal, jnp.exp(jnp.minimum(seg, 0.0)), 0.0)


def _ssd_in_specs(nc, rev):
    def cidx(c):
        return (nc - 1 - c) if rev else c

    return [
        pl.BlockSpec((CHUNK, D_INNER), lambda c: (cidx(c), 0)),
        pl.BlockSpec((CHUNK, 512), lambda c: (cidx(c), 2)),
        pl.BlockSpec((CHUNK, 512), lambda c: (cidx(c), 3)),
        pl.BlockSpec((CHUNK, D_INNER), lambda c: (cidx(c), 0)),
        pl.BlockSpec((CHUNK, 128), lambda c: (cidx(c), OFF_DT // 128)),
        pl.BlockSpec((N_HEADS, CHUNK), lambda c: (0, cidx(c))),
        pl.BlockSpec((1, N_HEADS), lambda c: (0, 0)),
        pl.BlockSpec((N_HEADS, 1), lambda c: (0, 0)),
        pl.BlockSpec((1, N_HEADS), lambda c: (0, 0)),
        pl.BlockSpec((N_HEADS, 1), lambda c: (0, 0)),
        pl.BlockSpec((1, D_INNER), lambda c: (0, 0)),
        pl.BlockSpec((1, D_INNER), lambda c: (0, 0)),
    ]


def _ssd_fwd(xbc, proj, dtT, dtb, dtbT, alog, alogT, dfull, ng, name):
    T = xbc.shape[0]
    nc = T // CHUNK
    Q = CHUNK

    def body(xs_ref, B_ref, C_ref, z_ref, dt_ref, dtT_ref, dtb_ref, dtbT_ref, al_ref, alT_ref, df_ref, ng_ref,
             y_ref, ypre_ref, hs_ref, h_scr):
        c = pl.program_id(0)

        @pl.when(c == 0)
        def _():
            h_scr[...] = jnp.zeros_like(h_scr)

        _, dt, _, cs, csT, causal, _ = _ssd_pre(dt_ref[:, :N_HEADS], dtT_ref[...], dtb_ref[...], dtbT_ref[...],
                                                al_ref[...], alT_ref[...])
        ex = _expand_mat()
        dt_full = _hdot(dt, ex, "a")
        cs_full = _hdot(cs, ex, "a")
        cs_last = cs_full[Q - 1:Q, :]
        xs = xs_ref[...]
        xd = xs * dt_full
        e_full = jnp.exp(cs_full)
        dec_full = jnp.exp(cs_last - cs_full)
        cd_full = jnp.exp(cs_last)
        lane_head = lax.broadcasted_iota(jnp.int32, (1, GROUP_W), 1) // HEAD_DIM
        for g in range(N_GROUPS):
            sl = slice(g * GROUP_W, (g + 1) * GROUP_W)
            Bg = B_ref[:, g * D_STATE:(g + 1) * D_STATE].astype(BF16)
            Cg = C_ref[:, g * D_STATE:(g + 1) * D_STATE].astype(BF16)
            CB = _dot_nt(Cg, Bg)
            hg = h_scr[g]
            yoff = _dot_nn(Cg, hg.astype(BF16)) * e_full[:, sl]
            xd_g = xd[:, sl]
            S = _dot_tn(Bg, (xd_g * dec_full[:, sl]).astype(BF16))
            xd_b = xd_g.astype(BF16)
            ydiag = jnp.zeros((Q, GROUP_W), F32)
            for r in range(HEADS_PER_GROUP):
                Lm = _decay_matrix(cs, csT, g * HEADS_PER_GROUP + r, causal)
                Gm = (CB * Lm).astype(BF16)
                ydiag = ydiag + _dot_nn(Gm, jnp.where(lane_head == r, xd_b, jnp.zeros_like(xd_b)))
            hs_ref[0, g] = hg
            h_scr[g] = hg * cd_full[:, sl] + S
            ypre = ydiag + yoff + xs[:, sl] * df_ref[:, sl]
            ypre_ref[:, sl] = ypre
            zg = z_ref[:, sl]
            yz = ypre * zg * _sigmoid(zg)
            rn = lax.rsqrt(jnp.mean(yz * yz, axis=-1, keepdims=True) + EPS)
            y_ref[:, sl] = (yz * rn * ng_ref[:, sl]).astype(BF16)

    return pl.pallas_call(
        body, grid=(nc,), in_specs=_ssd_in_specs(nc, False),
        out_specs=[pl.BlockSpec((CHUNK, D_INNER), lambda c: (c, 0)),
                   pl.BlockSpec((CHUNK, D_INNER), lambda c: (c, 0)),
                   pl.BlockSpec((1, N_GROUPS, D_STATE, GROUP_W), lambda c: (c, 0, 0, 0))],
        out_shape=[jax.ShapeDtypeStruct((T, D_INNER + ATTN_W), BF16), jax.ShapeDtypeStruct((T, D_INNER), F32),
                   jax.ShapeDtypeStruct((nc, N_GROUPS, D_STATE, GROUP_W), F32)],
        scratch_shapes=[pltpu.VMEM((N_GROUPS, D_STATE, GROUP_W), F32)],
        name=name, compiler_params=_cparams(("arbitrary",)),
    )(xbc, xbc, xbc, proj, proj, dtT, dtb, dtbT, alog, alogT, dfull, ng)


def _ssd_bwd(xbc, proj, dtT, dtb, dtbT, alog, alogT, dfull, ng, ypre, hs, dy, name):
    T = xbc.shape[0]
    nc = T // CHUNK
    Q = CHUNK

    def body(xs_ref, B_ref, C_ref, z_ref, dt_ref, dtT_ref, dtb_ref, dtbT_ref, al_ref, alT_ref, df_ref, ng_ref,
             ypre_ref, hs_ref, dy_ref,
             dz_ref, dxbc_ref, ddtb_ref, dal_ref, dD_ref, dng_ref, dh_scr):
        step = pl.program_id(0)

        @pl.when(step == 0)
        def _():
            dh_scr[...] = jnp.zeros_like(dh_scr)
            ddtb_ref[...] = jnp.zeros_like(ddtb_ref)
            dal_ref[...] = jnp.zeros_like(dal_ref)
            dD_ref[...] = jnp.zeros_like(dD_ref)
            dng_ref[...] = jnp.zeros_like(dng_ref)

        xdt, dt, A, cs, csT, causal, triu = _ssd_pre(dt_ref[:, :N_HEADS], dtT_ref[...], dtb_ref[...],
                                                    dtbT_ref[...], al_ref[...], alT_ref[...])
        ex = _expand_mat()
        dt_full = _hdot(dt, ex, "a")
        cs_full = _hdot(cs, ex, "a")
        cs_last = cs_full[Q - 1:Q, :]
        xs = xs_ref[...]
        xd = xs * dt_full
        e_full = jnp.exp(cs_full)
        dec_full = jnp.exp(cs_last - cs_full)
        cd_full = jnp.exp(cs_last)
        lane_head = lax.broadcasted_iota(jnp.int32, (1, GROUP_W), 1) // HEAD_DIM
        is_last = lax.broadcasted_iota(jnp.int32, (Q, 1), 0) == Q - 1
        dcs16 = jnp.zeros((Q, N_HEADS), F32)
        ddtx16 = jnp.zeros((Q, N_HEADS), F32)
        dD16 = jnp.zeros((8, N_HEADS), F32)
        lane16 = lax.broadcasted_iota(jnp.int32, (1, N_HEADS), 1)
        sub16 = lax.broadcasted_iota(jnp.int32, (N_HEADS, 1), 0)
        col_sums = jnp.zeros((N_HEADS, Q), F32)
        for g in range(N_GROUPS):
            sl = slice(g * GROUP_W, (g + 1) * GROUP_W)
            red = _reduce_mat(g)
            ypre_g = ypre_ref[:, sl]
            zg = z_ref[:, sl]
            sg = _sigmoid(zg)
            silu = zg * sg
            yz = ypre_g * silu
            rn = lax.rsqrt(jnp.mean(yz * yz, axis=-1, keepdims=True) + EPS)
            yh = yz * rn
            dy_g = dy_ref[:, sl]
            dng_ref[:, sl] += jnp.sum(dy_g * yh, axis=0, keepdims=True)
            dyh = dy_g * ng_ref[:, sl]
            dyz = rn * (dyh - yh * jnp.mean(dyh * yh, axis=-1, keepdims=True))
            dY = dyz * silu
            dz_ref[:, sl] = (dyz * ypre_g * sg * (1.0 + zg * (1.0 - sg))).astype(BF16)
            xs_g = xs[:, sl]
            xd_g = xd[:, sl]
            dec_g = dec_full[:, sl]
            cd_g = cd_full[:, sl]
            d_g = df_ref[:, sl]
            Bg = B_ref[:, g * D_STATE:(g + 1) * D_STATE].astype(BF16)
            Cg = C_ref[:, g * D_STATE:(g + 1) * D_STATE].astype(BF16)
            CB = _dot_nt(Cg, Bg)
            hg = hs_ref[0, g]
            hgb = hg.astype(BF16)
            yoff = _dot_nn(Cg, hgb) * e_full[:, sl]
            dhn = dh_scr[g]
            dhnb = dhn.astype(BF16)
            dYE = (dY * e_full[:, sl]).astype(BF16)
            dC = _dot_nt(dYE, hgb)
            dh_direct = _dot_tn(Cg, dYE)
            dXdd = _dot_nn(Bg, dhnb)
            dB = _dot_nt((xd_g * dec_g).astype(BF16), dhnb)
            dcd = jnp.sum(dhn * hg, axis=0, keepdims=True)
            dh_scr[g] = dh_direct + cd_g * dhn
            dYb = dY.astype(BF16)
            xd_b = xd_g.astype(BF16)
            dCB = jnp.zeros((Q, Q), F32)
            dXd = dXdd * dec_g
            for r in range(HEADS_PER_GROUP):
                h = g * HEADS_PER_GROUP + r
                Lm = _decay_matrix(cs, csT, h, causal)
                Gf = CB * Lm
                dYr = jnp.where(lane_head == r, dYb, jnp.zeros_like(dYb))
                dG = _dot_nt(dYr, xd_b)
                dCB = dCB + dG * Lm
                dXd = dXd + _dot_tn(Gf.astype(BF16), dYr)
                Mm = dG * Gf
                dcs16 = dcs16 + jnp.where(lane16 == h, jnp.sum(Mm, axis=1, keepdims=True), 0.0)
                col_sums = col_sums + jnp.where(sub16 == h, jnp.sum(Mm, axis=0, keepdims=True), 0.0)
            dCBb = dCB.astype(BF16)
            dC = dC + _dot_nn(dCBb, Bg)
            dB = dB + _dot_tn(dCBb, Cg)
            w_state = dXdd * dec_g * xd_g
            t_last = jnp.sum(w_state, axis=0, keepdims=True) + dcd * cd_g
            dcs_g = dY * yoff - w_state + jnp.where(is_last, t_last, 0.0)
            dcs16 = dcs16 + _hdot(dcs_g, red, "a")
            ddtx16 = ddtx16 + _hdot(dXd * xs_g, red, "a")
            dD16 = dD16 + _hdot(jnp.broadcast_to(jnp.sum(dY * xs_g, axis=0, keepdims=True), (8, GROUP_W)), red, "a")
            dxbc_ref[:, sl] = dXd * dt_full[:, sl] + dY * d_g
            dxbc_ref[:, D_INNER + g * D_STATE:D_INNER + (g + 1) * D_STATE] = dB
            dxbc_ref[:, D_INNER + 512 + g * D_STATE:D_INNER + 512 + (g + 1) * D_STATE] = dC
        eye = (lax.broadcasted_iota(jnp.int32, (N_HEADS, N_HEADS), 0)
               == lax.broadcasted_iota(jnp.int32, (N_HEADS, N_HEADS), 1)).astype(BF16)
        dcs16 = dcs16 - sum(_dot_tn(part, eye) for part in _split3(col_sums))
        da = _hdot(triu, dcs16, "b")
        ddt = da * A + ddtx16
        ddt_raw = ddt * _sigmoid(xdt)
        pr = lax.broadcasted_iota(jnp.int32, (N_HEADS, 128), 0)
        pc = lax.broadcasted_iota(jnp.int32, (N_HEADS, 128), 1)
        dz_ref[:, D_INNER:OFF_DT] = jnp.zeros((Q, OFF_DT - D_INNER), BF16)
        dz_ref[:, OFF_DT:OFF_DT + 128] = _hdot(ddt_raw, (pr == pc).astype(F32), "a").astype(BF16)
        dz_ref[:, OFF_DT + 128:] = jnp.zeros((Q, NP - OFF_DT - 128), BF16)
        ddtb_ref[...] += jnp.sum(ddt_raw, axis=0, keepdims=True)
        dal_ref[...] += jnp.sum(da * dt, axis=0, keepdims=True) * A
        dD_ref[...] += dD16[0:1, :]

    def rc(c):
        return nc - 1 - c

    in_specs = _ssd_in_specs(nc, True) + [
        pl.BlockSpec((CHUNK, D_INNER), lambda c: (rc(c), 0)),
        pl.BlockSpec((1, N_GROUPS, D_STATE, GROUP_W), lambda c: (rc(c), 0, 0, 0)),
        pl.BlockSpec((CHUNK, D_INNER), lambda c: (rc(c), 0)),
    ]
    small = pl.BlockSpec((1, N_HEADS), lambda c: (0, 0))
    return pl.pallas_call(
        body, grid=(nc,), in_specs=in_specs,
        out_specs=[pl.BlockSpec((CHUNK, NP), lambda c: (rc(c), 0)),
                   pl.BlockSpec((CHUNK, CONV_DIM), lambda c: (rc(c), 0)),
                   small, small, small,
                   pl.BlockSpec((1, D_INNER), lambda c: (0, 0))],
        out_shape=[jax.ShapeDtypeStruct((T, NP), BF16), jax.ShapeDtypeStruct((T, CONV_DIM), F32),
                   jax.ShapeDtypeStruct((1, N_HEADS), F32), jax.ShapeDtypeStruct((1, N_HEADS), F32),
                   jax.ShapeDtypeStruct((1, N_HEADS), F32), jax.ShapeDtypeStruct((1, D_INNER), F32)],
        scratch_shapes=[pltpu.VMEM((N_GROUPS, D_STATE, GROUP_W), F32)],
        name=name, compiler_params=_cparams(("arbitrary",)),
    )(xbc, xbc, xbc, proj, proj, dtT, dtb, dtbT, alog, alogT, dfull, ng, ypre, hs, dy)


N_PAIRS = ATTN_W // 128
PAIRS_PER_KV = N_PAIRS // 2
ATTN_SCALE = HEAD_DIM ** -0.5


def _kv_variants(kk):
    lo = lax.broadcasted_iota(jnp.int32, kk.shape, 1) < HEAD_DIM
    zero = jnp.zeros_like(kk)
    k00 = jnp.where(lo, kk, zero)
    k11 = jnp.where(lo, zero, kk)
    k01 = pltpu.roll(k00, HEAD_DIM, axis=1)
    k10 = pltpu.roll(k11, HEAD_DIM, axis=1)
    return [[k00.astype(BF16), k01.astype(BF16)], [k10.astype(BF16), k11.astype(BF16)]]


LOG2E = 1.4426950408889634


def _own_block():
    i = lax.broadcasted_iota(jnp.int32, (WINDOW, WINDOW), 0)
    j = lax.broadcasted_iota(jnp.int32, (WINDOW, WINDOW), 1)
    return j <= i


def _fold(own, a):
    return jnp.where(own, a[:, WINDOW:], a[:, :WINDOW])


def _attn_probs(qp, kvar, own, prev_bias, sk):
    s = _dot_nt(qp, kvar)
    sb = jnp.where(own, s[:, WINDOW:], s[:, :WINDOW] + prev_bias) * (ATTN_SCALE * LOG2E)
    sk2 = sk * LOG2E
    m = jnp.maximum(jnp.max(sb, axis=1, keepdims=True), sk2)
    pe = jnp.exp2(sb - m)
    es = jnp.exp2(sk2 - m)
    den = jnp.sum(pe, axis=1, keepdims=True) + es
    inv = 1.0 / den
    return pe * inv, es * inv


def _unfold(own, a):
    zero = jnp.zeros_like(a)
    return jnp.where(own, zero, a), jnp.where(own, a, zero)


def _sink(sinks, r):
    lane = lax.broadcasted_iota(jnp.int32, sinks.shape, 1)
    return jnp.sum(jnp.where(lane == r, sinks, 0.0), axis=1, keepdims=True)


def _kv_specs():
    return [pl.BlockSpec((WINDOW, KV_W), lambda n: (jnp.maximum(n - 1, 0), OFF_K // KV_W)),
            pl.BlockSpec((WINDOW, KV_W), lambda n: (n, OFF_K // KV_W)),
            pl.BlockSpec((WINDOW, KV_W), lambda n: (jnp.maximum(n - 1, 0), OFF_V // KV_W)),
            pl.BlockSpec((WINDOW, KV_W), lambda n: (n, OFF_V // KV_W))]


def _attn_fwd(proj, sinks, og, ycat, name):
    T = proj.shape[0]
    nb = T // WINDOW

    def body(q_ref, kp_ref, kc_ref, vp_ref, vc_ref, s_ref, og_ref, _, y_ref, o_ref):
        n = pl.program_id(0)
        kv = _kv_variants(jnp.concatenate([kp_ref[...], kc_ref[...]], axis=0))
        vv = _kv_variants(jnp.concatenate([vp_ref[...], vc_ref[...]], axis=0))
        own = _own_block()
        prev_bias = jnp.where(n > 0, 0.0, NEG)
        sinks_v = s_ref[...]
        ssq = jnp.zeros((WINDOW, 1), F32)
        for p in range(N_PAIRS):
            j = p // PAIRS_PER_KV
            qp = q_ref[:, p * 128:(p + 1) * 128].astype(BF16)
            o_pair = jnp.zeros((WINDOW, 128), F32)
            for par in range(2):
                pn, _ = _attn_probs(qp, kv[j][par], own, prev_bias, _sink(sinks_v, 2 * p + par))
                p_prev, p_own = _unfold(own, pn.astype(BF16))
                o_pair = o_pair + _dot_nn(p_prev, vv[j][par][:WINDOW]) + _dot_nn(p_own, vv[j][par][WINDOW:])
            o_ref[:, p * 128:(p + 1) * 128] = o_pair
            ssq = ssq + jnp.sum(o_pair * o_pair, axis=1, keepdims=True)
        rn = lax.rsqrt(ssq * (1.0 / ATTN_W) + EPS)
        y_ref[...] = (o_ref[...] * rn * og_ref[...]).astype(BF16)

    return pl.pallas_call(
        body, grid=(nb,),
        in_specs=[pl.BlockSpec((WINDOW, ATTN_W), lambda n: (n, OFF_Q // ATTN_W)), *_kv_specs(),
                  pl.BlockSpec((1, N_HEADS), lambda n: (0, 0)), pl.BlockSpec((1, ATTN_W), lambda n: (0, 0)), ANY],
        out_specs=[pl.BlockSpec((WINDOW, ATTN_W), lambda n: (n, 1)), pl.BlockSpec((WINDOW, ATTN_W), lambda n: (n, 0))],
        out_shape=[jax.ShapeDtypeStruct(ycat.shape, BF16), jax.ShapeDtypeStruct((T, ATTN_W), F32)],
        input_output_aliases={7: 0}, name=name, compiler_params=_cparams(("parallel",)),
    )(proj, proj, proj, proj, proj, sinks, og, ycat)


def _attn_bwd(proj, sinks, og, o, dy, dproj, name):
    T = proj.shape[0]
    nb = T // WINDOW

    def body(q_ref, kp_ref, kc_ref, vp_ref, vc_ref, s_ref, og_ref, o_ref, dy_ref, _,
             dq_ref, dk_ref, dv_ref, ds_ref, dog_ref):
        n = pl.program_id(0)

        @pl.when(n == 0)
        def _():
            dk_ref[...] = jnp.zeros_like(dk_ref)
            dv_ref[...] = jnp.zeros_like(dv_ref)
            ds_ref[...] = jnp.zeros_like(ds_ref)
            dog_ref[...] = jnp.zeros_like(dog_ref)

        kv = _kv_variants(jnp.concatenate([kp_ref[...], kc_ref[...]], axis=0))
        vv = _kv_variants(jnp.concatenate([vp_ref[...], vc_ref[...]], axis=0))
        own = _own_block()
        prev_bias = jnp.where(n > 0, 0.0, NEG)
        sinks_v = s_ref[...]
        of = o_ref[...]
        rn = lax.rsqrt(jnp.mean(of * of, axis=-1, keepdims=True) + EPS)
        oh = of * rn
        dyf = dy_ref[...]
        dog_ref[...] += jnp.sum(dyf * oh, axis=0, keepdims=True)
        doh = dyf * og_ref[...]
        do = rn * (doh - oh * jnp.mean(doh * oh, axis=-1, keepdims=True))
        lane = lax.broadcasted_iota(jnp.int32, (1, 128), 1)
        lane16 = lax.broadcasted_iota(jnp.int32, (1, N_HEADS), 1)
        sub = lax.broadcasted_iota(jnp.int32, (128, 1), 0)
        dk_acc = [[[jnp.zeros((128, WINDOW), F32) for _ in range(2)] for _ in range(2)] for _ in range(2)]
        dv_acc = [[[jnp.zeros((128, WINDOW), F32) for _ in range(2)] for _ in range(2)] for _ in range(2)]
        dsink = jnp.zeros((1, N_HEADS), F32)
        for p in range(N_PAIRS):
            j = p // PAIRS_PER_KV
            q_f = q_ref[:, p * 128:(p + 1) * 128]
            qp = q_f.astype(BF16)
            q_t = q_f.T.astype(BF16)
            do_p = do[:, p * 128:(p + 1) * 128]
            o_p = of[:, p * 128:(p + 1) * 128]
            do_b = do_p.astype(BF16)
            do_t = do_p.T.astype(BF16)
            prod = do_p * o_p
            dq_pair = jnp.zeros((WINDOW, 128), F32)
            for par in range(2):
                r = 2 * p + par
                half = (lane < HEAD_DIM) if par == 0 else (lane >= HEAD_DIM)
                rows_half = (sub < HEAD_DIM) if par == 0 else (sub >= HEAD_DIM)
                pn, ps = _attn_probs(qp, kv[j][par], own, prev_bias, _sink(sinks_v, r))
                delta = jnp.sum(jnp.where(half, prod, 0.0), axis=1, keepdims=True)
                dP = _fold(own, _dot_nt(do_b, vv[j][par]))
                dS = pn * (dP - delta)
                dsink = dsink + jnp.where(lane16 == r, -jnp.sum(ps * delta, axis=0, keepdims=True), 0.0)
                dS_parts = _unfold(own, (dS * ATTN_SCALE).astype(BF16))
                p_parts = _unfold(own, pn.astype(BF16))
                q_th = jnp.where(rows_half, q_t, jnp.zeros_like(q_t))
                do_th = jnp.where(rows_half, do_t, jnp.zeros_like(do_t))
                for blk in range(2):
                    dq_pair = dq_pair + _dot_nn(dS_parts[blk], kv[j][par][blk * WINDOW:(blk + 1) * WINDOW])
                    dk_acc[j][par][blk] = dk_acc[j][par][blk] + _dot_nn(q_th, dS_parts[blk])
                    dv_acc[j][par][blk] = dv_acc[j][par][blk] + _dot_nn(do_th, p_parts[blk])
            dq_ref[:, p * 128:(p + 1) * 128] = dq_pair.astype(BF16)
        rows = [pl.multiple_of(jnp.maximum(n - 1, 0) * WINDOW, WINDOW), pl.multiple_of(n * WINDOW, WINDOW)]
        for acc, ref in [(dk_acc, dk_ref), (dv_acc, dv_ref)]:
            for blk in range(2):
                both_t = (acc[0][0][blk] + pltpu.roll(acc[0][1][blk], HEAD_DIM, axis=0)
                          + acc[1][1][blk] + pltpu.roll(acc[1][0][blk], HEAD_DIM, axis=0))
                ref[pl.ds(rows[blk], WINDOW), :] += both_t.T
        ds_ref[...] += dsink

    full_kv = pl.BlockSpec((T, KV_W), lambda n: (0, 0))
    blk = pl.BlockSpec((WINDOW, ATTN_W), lambda n: (n, 0))
    return pl.pallas_call(
        body, grid=(nb,),
        in_specs=[pl.BlockSpec((WINDOW, ATTN_W), lambda n: (n, OFF_Q // ATTN_W)), *_kv_specs(),
                  pl.BlockSpec((1, N_HEADS), lambda n: (0, 0)), pl.BlockSpec((1, ATTN_W), lambda n: (0, 0)),
                  blk, pl.BlockSpec((WINDOW, ATTN_W), lambda n: (n, 1)), ANY],
        out_specs=[pl.BlockSpec((WINDOW, ATTN_W), lambda n: (n, OFF_Q // ATTN_W)), full_kv, full_kv,
                   pl.BlockSpec((1, N_HEADS), lambda n: (0, 0)), pl.BlockSpec((1, ATTN_W), lambda n: (0, 0))],
        out_shape=[jax.ShapeDtypeStruct(dproj.shape, BF16), jax.ShapeDtypeStruct((T, KV_W), F32),
                   jax.ShapeDtypeStruct((T, KV_W), F32), jax.ShapeDtypeStruct((1, N_HEADS), F32),
                   jax.ShapeDtypeStruct((1, ATTN_W), F32)],
        input_output_aliases={9: 0}, name=name, compiler_params=_cparams(("arbitrary",)),
    )(proj, proj, proj, proj, proj, sinks, og, o, dy, dproj)


ANY = pl.BlockSpec(memory_space=pl.ANY)


def _coords():
    return lax.axis_index("x"), lax.axis_index("y"), lax.axis_index("c")


def _all_gather(arrs, name):
    n = len(arrs)

    def body(*refs):
        ins, outs = refs[:n], refs[n:2 * n]
        send_sems, recv_sems, local_sems = refs[2 * n:]
        x, y, c = _coords()
        me = 4 * x + 2 * y + c
        sibling = (x, y, 1 - c)
        chips = [(1 - x, y), (x, 1 - y), (1 - x, 1 - y)]

        def copy(a, k, block, to, src=None):
            dst = outs[a].at[block]
            return pltpu.make_async_remote_copy(
                src_ref=dst if src is None else src, dst_ref=dst, send_sem=send_sems.at[a, k],
                recv_sem=recv_sems.at[a, k], device_id=to, device_id_type=MESH)

        mine = [pltpu.make_async_copy(ins[a], outs[a].at[me], local_sems.at[a]) for a in range(n)]
        for cp in mine:
            cp.start()
        first = []
        for a in range(n):
            first.append(copy(a, 0, me, sibling, src=ins[a]))
            for j, chip in enumerate(chips):
                first.append(copy(a, 1 + j, me, (*chip, c), src=ins[a]))
        for cp in first:
            cp.start()
        passed = []
        for j, (px, py) in enumerate(chips):
            blk = 4 * px + 2 * py + c
            for a in range(n):
                copy(a, 1 + j, blk, sibling).wait_recv()
                fwd = copy(a, 4 + j, blk, sibling)
                fwd.start()
                passed.append(fwd)
        for a in range(n):
            copy(a, 0, 4 * x + 2 * y + (1 - c), sibling).wait_recv()
            for j, (px, py) in enumerate(chips):
                copy(a, 4 + j, 4 * px + 2 * py + (1 - c), sibling).wait_recv()
        for cp in first + passed:
            cp.wait_send()
        for cp in mine:
            cp.wait()

    return pl.pallas_call(
        body, in_specs=[ANY] * n, out_specs=[ANY] * n,
        out_shape=[jax.ShapeDtypeStruct((N_DEV,) + a.shape, a.dtype) for a in arrs],
        scratch_shapes=[pltpu.SemaphoreType.DMA((n, 7)), pltpu.SemaphoreType.DMA((n, 7)),
                        pltpu.SemaphoreType.DMA((n,))],
        name=name,
    )(*arrs)


def _exchange_pair(arrs, name):
    n = len(arrs)

    def body(*refs):
        ins, outs = refs[:n], refs[n:2 * n]
        send_sems, recv_sems = refs[2 * n:]
        x, y, c = _coords()
        cps = []
        for a in range(n):
            for q in range(4):
                cps.append(pltpu.make_async_remote_copy(
                    src_ref=ins[a].at[2 * q + (1 - c)], dst_ref=outs[a].at[q], send_sem=send_sems.at[a, q],
                    recv_sem=recv_sems.at[a, q], device_id=(x, y, 1 - c), device_id_type=MESH))
        for cp in cps:
            cp.start()
        for cp in cps:
            cp.wait()

    return pl.pallas_call(
        body, in_specs=[ANY] * n, out_specs=[ANY] * n,
        out_shape=[jax.ShapeDtypeStruct((4,) + a.shape[1:], a.dtype) for a in arrs],
        scratch_shapes=[pltpu.SemaphoreType.DMA((n, 4)), pltpu.SemaphoreType.DMA((n, 4))],
        name=name,
    )(*arrs)


def _exchange_chips(arrs, name):
    n = len(arrs)

    def body(*refs):
        ins, outs = refs[:n], refs[n:2 * n]
        send_sems, recv_sems = refs[2 * n:]
        x, y, c = _coords()
        chips = [(1 - x, y), (x, 1 - y), (1 - x, 1 - y)]
        cps = []
        for a in range(n):
            for k, (tx, ty) in enumerate(chips):
                cps.append(pltpu.make_async_remote_copy(
                    src_ref=ins[a].at[2 * tx + ty], dst_ref=outs[a].at[k], send_sem=send_sems.at[a, k],
                    recv_sem=recv_sems.at[a, k], device_id=(tx, ty, c), device_id_type=MESH))
        for cp in cps:
            cp.start()
        for cp in cps:
            cp.wait()

    return pl.pallas_call(
        body, in_specs=[ANY] * n, out_specs=[ANY] * n,
        out_shape=[jax.ShapeDtypeStruct((3,) + a.shape[1:], a.dtype) for a in arrs],
        scratch_shapes=[pltpu.SemaphoreType.DMA((n, 3)), pltpu.SemaphoreType.DMA((n, 3))],
        name=name,
    )(*arrs)


HBM = pl.BlockSpec(memory_space=pltpu.HBM)
SEM = pl.BlockSpec(memory_space=pltpu.SEMAPHORE)
EFFECT = pltpu.SideEffectType.DATAFLOW_SIDE_EFFECTING


def _in_hbm(a):
    return pltpu.with_memory_space_constraint(a, pltpu.HBM)


def _remote_start(srcs, lands, plan, n_copies, name, after=None):
    ns, nb = len(srcs), len(srcs) + len(lands)
    n_after = 0 if after is None else 1

    def body(*refs):
        src_refs, land_refs = refs[:ns], refs[ns:nb]
        send_sems, recv_sems = refs[nb + n_after], refs[nb + n_after + 1]
        token = refs[-1]
        x, y, c = _coords()
        for i, (sv, dv, dev) in enumerate(plan(src_refs, land_refs, x, y, c)):
            pltpu.make_async_remote_copy(src_ref=sv, dst_ref=dv, send_sem=send_sems.at[i], recv_sem=recv_sems.at[i],
                                         device_id=dev, device_id_type=MESH).start()
        token[...] = jnp.zeros_like(token)

    bufs = list(srcs) + list(lands)
    outs = pl.pallas_call(
        body, name=name,
        out_shape=(pltpu.SemaphoreType.DMA((n_copies,)), pltpu.SemaphoreType.DMA((n_copies,)),
                   *[pltpu.HBM(b.shape, b.dtype) for b in bufs], jax.ShapeDtypeStruct((8, 128), F32)),
        in_specs=[HBM] * nb + [ANY] * n_after,
        out_specs=(SEM, SEM, *[HBM] * nb, pl.BlockSpec(memory_space=pltpu.VMEM)),
        input_output_aliases={i: 2 + i for i in range(nb)},
        compiler_params=pltpu.CompilerParams(has_side_effects=EFFECT),
    )(*[_in_hbm(b) for b in bufs], *([] if after is None else [after]))
    return outs[0], outs[1], list(outs[2:2 + ns]), list(outs[2 + ns:2 + nb]), outs[-1]


def _remote_wait(started, after, plan, name):
    send_sems, recv_sems, srcs, lands, _ = started
    ns, nb = len(srcs), len(srcs) + len(lands)

    def body(*refs):
        src_refs, land_refs = refs[:ns], refs[ns:nb]
        send_sems, recv_sems = refs[nb], refs[nb + 1]
        x, y, c = _coords()
        for i, (sv, dv, dev) in enumerate(plan(src_refs, land_refs, x, y, c)):
            cp = pltpu.make_async_remote_copy(src_ref=sv, dst_ref=dv, send_sem=send_sems.at[i],
                                              recv_sem=recv_sems.at[i], device_id=dev, device_id_type=MESH)
            cp.wait_send()
            cp.wait_recv()

    bufs = list(srcs) + list(lands)
    outs = pl.pallas_call(
        body, name=name, out_shape=tuple(pltpu.HBM(b.shape, b.dtype) for b in bufs),
        in_specs=[HBM] * nb + [SEM, SEM, ANY], out_specs=tuple([HBM] * nb),
        input_output_aliases={i: i for i in range(nb)},
        compiler_params=pltpu.CompilerParams(has_side_effects=EFFECT),
    )(*bufs, send_sems, recv_sems, after)
    return list(outs[:ns]), list(outs[ns:])


def _gather_plan(src_refs, land_refs, x, y, c):
    me = 4 * x + 2 * y + c
    plan = []
    for s, l in zip(src_refs, land_refs):
        for dev in [(x, y, 1 - c), (1 - x, y, c), (x, 1 - y, c), (1 - x, 1 - y, c)]:
            plan.append((s, l.at[me], dev))
    return plan


def _forward_plan(src_refs, land_refs, x, y, c):
    plan = []
    for l in land_refs:
        for px, py in [(1 - x, y), (x, 1 - y), (1 - x, 1 - y)]:
            blk = l.at[4 * px + 2 * py + c]
            plan.append((blk, blk, (x, y, 1 - c)))
    return plan


def _pair_plan(src_refs, land_refs, x, y, c):
    plan = []
    for s, l in zip(src_refs, land_refs):
        for q in range(4):
            plan.append((s.at[2 * q + (1 - c)], l.at[q], (x, y, 1 - c)))
    return plan


def _pair4_plan(src_refs, land_refs, x, y, c):
    plan = []
    for s, l in zip(src_refs, land_refs):
        for q in range(4):
            plan.append((s.at[q], l.at[q], (x, y, 1 - c)))
    return plan


def _chips_plan(src_refs, land_refs, x, y, c):
    plan = []
    for s, l in zip(src_refs, land_refs):
        for k, (tx, ty) in enumerate([(1 - x, y), (x, 1 - y), (1 - x, 1 - y)]):
            plan.append((s.at[2 * tx + ty], l.at[k], (tx, ty, c)))
    return plan


def _everyone_plan(src_refs, land_refs, x, y, c):
    me = 4 * x + 2 * y + c
    plan = []
    for s, l in zip(src_refs, land_refs):
        for fx, fy, fc in [(0, 0, 1), (1, 0, 0), (1, 0, 1), (0, 1, 0), (0, 1, 1), (1, 1, 0), (1, 1, 1)]:
            dev = ((1 - x) if fx else x, (1 - y) if fy else y, (1 - c) if fc else c)
            plan.append((s, l.at[me], dev))
    return plan


def _gather_finish(gathered, name):
    n = len(gathered)

    def body(*refs):
        outs = refs[n:2 * n]
        send_sems, recv_sems = refs[2 * n:]
        x, y, c = _coords()
        cps = []
        for a in range(n):
            for j, (px, py) in enumerate([(1 - x, y), (x, 1 - y), (1 - x, 1 - y)]):
                blk = outs[a].at[4 * px + 2 * py + c]
                got = outs[a].at[4 * px + 2 * py + (1 - c)]
                cps.append((pltpu.make_async_remote_copy(
                    src_ref=blk, dst_ref=blk, send_sem=send_sems.at[a, j], recv_sem=recv_sems.at[a, j],
                    device_id=(x, y, 1 - c), device_id_type=MESH), pltpu.make_async_remote_copy(
                    src_ref=got, dst_ref=got, send_sem=send_sems.at[a, j], recv_sem=recv_sems.at[a, j],
                    device_id=(x, y, 1 - c), device_id_type=MESH)))
        for cp, _ in cps:
            cp.start()
        for cp, arrival in cps:
            cp.wait_send()
            arrival.wait_recv()

    return pl.pallas_call(
        body, in_specs=[ANY] * n, out_specs=[ANY] * n,
        out_shape=[jax.ShapeDtypeStruct(g.shape, g.dtype) for g in gathered],
        input_output_aliases={a: a for a in range(n)},
        scratch_shapes=[pltpu.SemaphoreType.DMA((n, 3)), pltpu.SemaphoreType.DMA((n, 3))],
        name=name,
    )(*gathered)


def _pair_add(g8, r1, csel, tr, name):
    _, R, C = r1.shape
    g4 = g8.reshape(4, 2, R, C)

    def body(c_ref, g_ref, r_ref, o_ref):
        o_ref[...] = (g_ref[...].astype(F32) + r_ref[...].astype(F32)).astype(BF16)

    return pl.pallas_call(
        body,
        grid_spec=pltpu.PrefetchScalarGridSpec(
            num_scalar_prefetch=1, grid=(4, R // tr),
            in_specs=[pl.BlockSpec((None, None, tr, C), lambda q, i, cs: (q, cs[0], i, 0)),
                      pl.BlockSpec((None, tr, C), lambda q, i, cs: (q, i, 0))],
            out_specs=pl.BlockSpec((None, tr, C), lambda q, i, cs: (q, i, 0))),
        out_shape=jax.ShapeDtypeStruct((4, R, C), BF16), name=name,
        compiler_params=_cparams(("parallel", "parallel")),
    )(csel, g4, r1)


def _adamw_math(w, g, m, v):
    m = ADAM_B1 * m + (1.0 - ADAM_B1) * g
    v = ADAM_B2 * v + (1.0 - ADAM_B2) * (g * g)
    m_hat = m / (1.0 - ADAM_B1 ** ADAM_STEP)
    v_hat = v / (1.0 - ADAM_B2 ** ADAM_STEP)
    delta = -ADAM_LR * (m_hat / (jnp.sqrt(v_hat) + ADAM_EPS) + ADAM_WD * w)
    return delta, m, v


def _adamw_big(w, m, v, p4, r3, qsel, tile, name):
    R, C = w.shape
    tr, tc = tile

    def body(q_ref, w_ref, m_ref, v_ref, p_ref, r_ref, g_out, d_out, m_out, v_out):
        g = p_ref[...].astype(F32) + r_ref[0].astype(F32) + r_ref[1].astype(F32) + r_ref[2].astype(F32)
        d, mn, vn = _adamw_math(w_ref[...], g, m_ref[...], v_ref[...])
        g_out[...] = g
        d_out[...] = d
        m_out[...] = mn
        v_out[...] = vn

    blk = pl.BlockSpec((tr, tc), lambda i, j, qs: (i, j))
    return pl.pallas_call(
        body,
        grid_spec=pltpu.PrefetchScalarGridSpec(
            num_scalar_prefetch=1, grid=(R // tr, C // tc),
            in_specs=[blk, blk, blk, pl.BlockSpec((None, tr, tc), lambda i, j, qs: (qs[0], i, j)),
                      pl.BlockSpec((3, tr, tc), lambda i, j, qs: (0, i, j))],
            out_specs=[blk, blk, blk, blk]),
        out_shape=[jax.ShapeDtypeStruct((R, C), F32)] * 4, name=name,
        compiler_params=_cparams(("parallel", "parallel")),
    )(qsel, w, m, v, p4, r3)


def _sum_partials(p4, r3, qsel, tc, name):
    _, R, C = p4.shape

    def body(q_ref, p_ref, r_ref, o_ref):
        o_ref[...] = p_ref[...].astype(F32) + r_ref[0].astype(F32) + r_ref[1].astype(F32) + r_ref[2].astype(F32)

    return pl.pallas_call(
        body,
        grid_spec=pltpu.PrefetchScalarGridSpec(
            num_scalar_prefetch=1, grid=(C // tc,),
            in_specs=[pl.BlockSpec((None, R, tc), lambda j, qs: (qs[0], 0, j)),
                      pl.BlockSpec((3, R, tc), lambda j, qs: (0, 0, j))],
            out_specs=pl.BlockSpec((R, tc), lambda j, qs: (0, j))),
        out_shape=jax.ShapeDtypeStruct((R, C), F32), name=name, compiler_params=_cparams(("parallel",)),
    )(qsel, p4, r3)


def _adamw_tiled(w, g, m, v, tc, name):
    R, C = w.shape

    def body(w_ref, g_ref, m_ref, v_ref, d_out, m_out, v_out):
        d, mn, vn = _adamw_math(w_ref[...], g_ref[...], m_ref[...], v_ref[...])
        d_out[...] = d
        m_out[...] = mn
        v_out[...] = vn

    blk = pl.BlockSpec((R, tc), lambda j: (0, j))
    return pl.pallas_call(
        body, grid=(C // tc,), in_specs=[blk] * 4, out_specs=[blk] * 3,
        out_shape=[jax.ShapeDtypeStruct((R, C), F32)] * 3, name=name, compiler_params=_cparams(("parallel",)),
    )(w, g, m, v)


def _small_sum(parts, name):
    def body(p_ref, o_ref):
        acc = p_ref[0]
        for d in range(1, N_DEV):
            acc = acc + p_ref[d]
        o_ref[...] = acc

    return pl.pallas_call(
        body, out_shape=jax.ShapeDtypeStruct(parts.shape[1:], F32), name=name,
        compiler_params=_cparams(),
    )(parts)


def _adamw_small(w, g, m, v, name):
    def body(w_ref, g_ref, m_ref, v_ref, d_out, m_out, v_out):
        d, mn, vn = _adamw_math(w_ref[...], g_ref[...], m_ref[...], v_ref[...])
        d_out[...] = d
        m_out[...] = mn
        v_out[...] = vn

    return pl.pallas_call(
        body, out_shape=[jax.ShapeDtypeStruct(w.shape, F32)] * 3, name=name, compiler_params=_cparams(),
    )(w, g, m, v)


def _row(*pieces):
    r = jnp.concatenate([p.reshape(1, -1) for p in pieces], axis=1)
    return jnp.pad(r, ((0, 0), (0, D_MODEL - r.shape[1])))


def _pack_small(mix, convb, ssmg, attng, mlpg, fing, convw, dtb, alog, dsk, sinks, extra=None):
    last = [dtb, alog, dsk, sinks] + ([extra] if extra is not None else [])
    rows = [_row(mix), _row(convb), _row(ssmg, attng), _row(mlpg), _row(fing),
            jnp.pad(convw, ((0, 0), (0, D_MODEL - convw.shape[1]))), _row(*last)]
    packed = jnp.concatenate(rows, axis=0)
    return jnp.pad(packed, ((0, SMALL_ROWS - packed.shape[0]), (0, 0)))


def _unpack_small(p, conv_n):
    return dict(
        mix_norm_g=p[0:1, :], conv_b=p[1:2, :], ssm_norm_g=p[2:3, :D_INNER], attn_out_norm_g=p[2:3, D_INNER:],
        mlp_norm_g=p[3:4, :], final_norm_g=p[4, :], conv_w=p[5:9, :conv_n][None],
        dt_bias=p[9:10, 0:16], A_log=p[9:10, 16:32], D_skip=p[9:10, 32:48], attn_sinks=p[9:10, 48:64])


SMALL_NAMES = ["mix_norm_g", "conv_w", "conv_b", "dt_bias", "A_log", "D_skip", "ssm_norm_g", "attn_sinks",
               "attn_out_norm_g", "mlp_norm_g", "final_norm_g"]
WEIGHT_ORDER = ["mix_norm_g", "w_in", "conv_w", "conv_b", "dt_bias", "A_log", "D_skip", "ssm_norm_g", "attn_sinks",
                "attn_out_norm_g", "w_out", "mlp_norm_g", "w_up", "w_down", "final_norm_g"]


def _to_my_columns(w_nat):
    pad = jnp.zeros((w_nat.shape[0], NP - IN_PROJ), w_nat.dtype)
    return jnp.concatenate([w_nat[:, :NAT_DT], w_nat[:, NAT_DT + N_HEADS:], w_nat[:, NAT_DT:NAT_DT + N_HEADS], pad],
                           axis=1)


PER = IN_PROJ // N_DEV
SUPER_STEP = 544
SUPER = 576


def _natural_rows(g, lo, hi):
    segments = [(0, NAT_DT, 0), (NAT_DT, NAT_DT + N_HEADS, OFF_DT - NAT_DT), (NAT_DT + N_HEADS, IN_PROJ, -N_HEADS),
                (IN_PROJ, NP, 0)]
    pieces = [g[max(lo, a) + shift:min(hi, b) + shift] for a, b, shift in segments if max(lo, a) < min(hi, b)]
    return pieces[0] if len(pieces) == 1 else jnp.concatenate(pieces, axis=0)


def _w_in_from_super_slabs(sup):
    seam = SUPER - SUPER_STEP
    units = []
    for i in range(N_DEV):
        base = SUPER_STEP * i
        units.append((base, base + seam, sup[i, :seam] if i == 0 else sup[i - 1, SUPER_STEP:] + sup[i, :seam]))
        units.append((base + seam, base + SUPER_STEP, sup[i, seam:SUPER_STEP]))
    units.append((SUPER_STEP * N_DEV, SUPER_STEP * N_DEV + seam, sup[N_DEV - 1, SUPER_STEP:]))

    def natural(lo, hi):
        return [rows[max(lo, a) - a:min(hi, b) - a] for a, b, rows in units if max(lo, a) < min(hi, b)]

    pieces = natural(0, NAT_DT) + natural(NAT_DT + N_HEADS, IN_PROJ) + natural(NAT_DT, NAT_DT + N_HEADS)
    return jnp.concatenate(pieces + [jnp.zeros((NP - IN_PROJ, D_MODEL), sup.dtype)], axis=0)


def _to_natural_columns(w_my):
    return jnp.concatenate([w_my[:, :NAT_DT], w_my[:, OFF_DT:OFF_DT + N_HEADS], w_my[:, NAT_DT:OFF_DT]], axis=1)


SLAB = 1024


def _grad_w_up(h2, du, name, sel=None, add=None, after=None):
    T, D = h2.shape
    if sel is None:
        pick, n_slab, pre = (lambda j, *cs: j), N_DEV, None
    else:
        pre, other = sel
        pick, n_slab = (lambda j, cs: 2 * j + ((1 - cs[0]) if other else cs[0])), 4
    o_spec = pl.BlockSpec((None, SLAB, SLAB), lambda i, j, k, *cs: (j, i, 0))
    return _matmul(
        h2, du, mode="tn", grid=(D // SLAB, n_slab, 1),
        a_spec=pl.BlockSpec((T, SLAB), lambda i, j, k, *cs: (0, i)),
        b_spec=pl.BlockSpec((T, SLAB), lambda i, j, k, *cs: (0, pick(j, *cs))),
        out_shapes=[jax.ShapeDtypeStruct((n_slab, D, SLAB), BF16)], out_specs=[o_spec], tile=(SLAB, SLAB), name=name,
        extras=() if add is None else (add,), extra_specs=() if add is None else (o_spec,),
        epilogue=None if add is None else (lambda acc, r: (acc + r.astype(F32),)), after=after, prefetch=pre)[0]


def _grad_w_down(act, dx3b, name, sel=None, add=None, after=None):
    T, D = dx3b.shape
    if sel is None:
        pick, n_slab, pre = (lambda i, *cs: i), N_DEV, None
    else:
        pre, other = sel
        pick, n_slab = (lambda i, cs: 2 * i + ((1 - cs[0]) if other else cs[0])), 4
    o_spec = pl.BlockSpec((None, SLAB, SLAB), lambda i, j, k, *cs: (i, 0, j))
    return _matmul(
        act, dx3b, mode="tn", grid=(n_slab, D // SLAB, 1),
        a_spec=pl.BlockSpec((T, SLAB), lambda i, j, k, *cs: (0, pick(i, *cs))),
        b_spec=pl.BlockSpec((T, SLAB), lambda i, j, k, *cs: (0, j)),
        out_shapes=[jax.ShapeDtypeStruct((n_slab, SLAB, D), BF16)], out_specs=[o_spec], tile=(SLAB, SLAB), name=name,
        extras=() if add is None else (add,), extra_specs=() if add is None else (o_spec,),
        epilogue=None if add is None else (lambda acc, r: (acc + r.astype(F32),)), after=after, prefetch=pre)[0]


class _FixedWeights:
    def __init__(self, w_in_p, w_out_f, w_up_s, w_down_f, conv_w_f):
        self.w = (w_in_p, w_out_f, w_up_s, w_down_f, conv_w_f)
        self.grads = {}

    def mixer_weights(self, after):
        return self.w[0], self.w[4], None

    def prefetch(self, k, after):
        return None

    def out_weight(self, after):
        return self.w[1]

    def up_weight(self, after):
        return self.w[2]

    def down_weight(self, after):
        return self.w[3]

    def mlp_grads(self, h2, du, act, dx3b):
        self.grads.update(w_up=_grad_w_up(h2, du, "grad_w_up"),
                          w_down=_grad_w_down(act, dx3b, "grad_w_down").reshape(D_FF, D_MODEL))
        return None

    def out_grad(self, g_out):
        self.grads.update(w_out=g_out)
        return None

    def in_grad(self, g_in):
        self.grads.update(w_in=g_in)
        return None


def _local_step(x, tgt, p, hooks):
    T = x.shape[0]
    D = D_MODEL
    h1 = _rmsnorm_fwd(x, p["mix_norm_g"], "norm_mix")
    w_in_t, conv_w_f, token = hooks.mixer_weights(h1)
    (proj,) = _mm_simple(h1, w_in_t, mode="nt", M=T, N=NP, K=D, tm=min(T, 1024), tn=1536, tk=D, out_dtype=F32,
                         name="in_proj", after=token)
    xbc = _conv_fwd(proj, conv_w_f, p["conv_b"], "conv_fwd")
    dtT = proj[:, OFF_DT:OFF_DT + N_HEADS].T
    dtbT = p["dt_bias"].T
    alogT = p["A_log"].T
    dfull = jnp.repeat(p["D_skip"], HEAD_DIM, axis=1)
    token = hooks.prefetch("out", xbc)
    ssm_g = p["ssm_norm_g"] if token is None else p["ssm_norm_g"] + token[0:1, 0:1]
    ycat, ypre, hs = _ssd_fwd(xbc, proj, dtT, p["dt_bias"], dtbT, p["A_log"], alogT, dfull, ssm_g, "ssd_fwd")
    ycat, o_att = _attn_fwd(proj, p["attn_sinks"], p["attn_out_norm_g"], ycat, "attn_fwd")
    token = hooks.prefetch("up", ycat)
    w_out_f = hooks.out_weight(ycat if token is None else token)
    tm = min(T, 1024)
    (x2,) = _mm_simple(ycat, w_out_f, mode="nn", M=T, N=D, K=D, tm=tm, tn=1024, tk=D, out_dtype=F32, name="out_proj",
                       extras=(x,), epilogue=lambda acc, res: (acc + res,))
    h2 = _rmsnorm_fwd(x2, p["mlp_norm_g"], "norm_mlp")
    w_up_s = hooks.up_weight(h2)
    grid = (T // tm, N_DEV, 1)
    u, act = _matmul(
        h2, w_up_s, mode="nn", grid=grid,
        a_spec=pl.BlockSpec((tm, D), lambda i, j, k: (i, 0)),
        b_spec=pl.BlockSpec((None, D, 1024), lambda i, j, k: (j, 0, 0)),
        out_shapes=[jax.ShapeDtypeStruct((T, D_FF), F32), jax.ShapeDtypeStruct((T, D_FF), BF16)],
        out_specs=[pl.BlockSpec((tm, 1024), lambda i, j, k: (i, j))] * 2, tile=(tm, 1024), name="mlp_up",
        epilogue=lambda acc: (acc, jnp.square(jnp.maximum(acc, 0.0))))
    w_down_f = hooks.down_weight(act)
    (x3,) = _mm_simple(act, w_down_f, mode="nn", M=T, N=D, K=D_FF, tm=tm, tn=1024, tk=2048, out_dtype=F32,
                       name="mlp_down", extras=(x2,), epilogue=lambda acc, res: (acc + res,))
    loss_part, d_fin, dx3, dx3b = _final_loss(x3, tgt, p["final_norm_g"].reshape(1, D), "loss_head")
    (du,) = _mm_simple(dx3b, w_down_f, mode="nt", M=T, N=D_FF, K=D, tm=tm, tn=1024, tk=D, out_dtype=BF16,
                       name="mlp_down_bwd", extras=(u,),
                       epilogue=lambda acc, uu: (acc * (2.0 * jnp.maximum(uu, 0.0)),))
    token = hooks.mlp_grads(h2, du, act, dx3b)
    (dh2,) = _matmul(
        du, w_up_s, mode="nt", grid=(T // tm, D // 1024, N_DEV // 2),
        a_spec=pl.BlockSpec((tm, 2048), lambda i, j, k: (i, k)),
        b_spec=pl.BlockSpec((2, 1024, 1024), lambda i, j, k: (k, j, 0)),
        out_shapes=[jax.ShapeDtypeStruct((T, D), F32)],
        out_specs=[pl.BlockSpec((tm, 1024), lambda i, j, k: (i, j))], tile=(tm, 1024), name="mlp_up_bwd",
        after=token, dot_fn=lambda a, b: _dot_nt(a[:, :1024], b[0]) + _dot_nt(a[:, 1024:], b[1]))
    dx2, dx2b, d_mlp = _rmsnorm_bwd(dh2, x2, p["mlp_norm_g"], dx3, "norm_mlp_bwd")
    (g_out,) = _mm_simple(ycat, dx2b, mode="tn", M=D, N=D, K=T, tm=1024, tn=1024, tk=T, out_dtype=BF16,
                          name="grad_w_out")
    token = hooks.out_grad(g_out)
    (dy,) = _mm_simple(dx2b, w_out_f, mode="nt", M=T, N=D, K=D, tm=tm, tn=1024, tk=D, out_dtype=F32,
                       name="out_proj_bwd", after=token)
    dproj, dxbc_act, d_dtb, d_alog, d_dskip, d_ssmg = _ssd_bwd(
        xbc, proj, dtT, p["dt_bias"], dtbT, p["A_log"], alogT, dfull, p["ssm_norm_g"], ypre, hs, dy, "ssd_bwd")
    dproj, d_convw, d_convb = _conv_bwd(proj, dxbc_act, conv_w_f, p["conv_b"], dproj, "conv_bwd")
    dproj, dk, dv, d_sinks, d_attng = _attn_bwd(proj, p["attn_sinks"], p["attn_out_norm_g"], o_att, dy, dproj,
                                                "attn_bwd")
    dproj = lax.dynamic_update_slice(dproj, jnp.concatenate([dk, dv], axis=1).astype(BF16), (0, OFF_K))
    (g_in,) = _mm_simple(dproj, h1, mode="tn", M=NP, N=D, K=T, tm=1536, tn=1024, tk=T, out_dtype=BF16,
                         name="grad_w_in")
    token = hooks.in_grad(g_in)
    (dh1,) = _mm_simple(dproj, w_in_t, mode="nn", M=T, N=D, K=NP, tm=tm, tn=1024, tk=2304, out_dtype=F32,
                        name="in_proj_bwd", after=token)
    dx, _, d_mix = _rmsnorm_bwd(dh1, x, p["mix_norm_g"], dx2, "norm_mix_bwd")
    small = _pack_small(d_mix, d_convb, d_ssmg, d_attng, d_mlp, d_fin, d_convw, d_dtb, d_alog, d_dskip, d_sinks,
                        extra=loss_part[:, 0:1])
    return dx, small


def _landing(own, me):
    zone = lax.empty((N_DEV,) + own.shape, own.dtype)
    return lax.dynamic_update_slice(zone, own[None], (me,) + (0,) * own.ndim)


def _sequencer_gather(owns, split, me, collective_id, name):
    n = len(owns)
    zone_refs = [jax.new_ref(_landing(o, me), memory_space=pltpu.MemorySpace.HBM) for o in owns]
    own_refs = [jax.new_ref(o, memory_space=pltpu.MemorySpace.HBM) for o in owns]
    N_COPIES = 9

    @pl.kernel(mesh=plsc.ScalarSubcoreMesh(axis_name="sequencer", num_cores=1), name=name,
               scratch_types=(pltpu.SemaphoreType.DMA((n, N_COPIES)), pltpu.SemaphoreType.DMA((n, N_COPIES))),
               compiler_params=pltpu.CompilerParams(collective_id=collective_id))
    def launch(send_sems, recv_sems):
        x, y, c = _coords()
        sibling, xn, yn, diag = (x, y, 1 - c), (1 - x, y, c), (x, 1 - y, c), (1 - x, 1 - y, c)
        barrier = pltpu.get_barrier_semaphore()
        for peer in [sibling, xn, yn, diag]:
            pl.semaphore_signal(barrier, inc=1, device_id=peer, device_id_type=MESH)
        pl.semaphore_wait(barrier, 4)

        def block(a, dev, half=None):
            ref = zone_refs[a].at[4 * dev[0] + 2 * dev[1] + dev[2]]
            if half is None:
                return ref
            rows = owns[a].shape[0] // 2
            return ref.at[pl.ds(half * rows, rows)]

        def copy(a, k, src, dst, to):
            return pltpu.make_async_remote_copy(src_ref=src, dst_ref=dst, send_sem=send_sems.at[a, k],
                                                recv_sem=recv_sems.at[a, k], device_id=to, device_id_type=MESH)

        me_dev = (x, y, c)
        sent = []
        first = {}
        for a in range(n):
            for k, peer in enumerate([sibling, xn, yn] + ([] if split[a] else [diag])):
                first[a, k] = copy(a, k, own_refs[a], block(a, me_dev), peer)
                first[a, k].start()
                sent.append(first[a, k])
        from_sibling = []
        for a in range(n):
            first[a, 1].wait_recv()
            sent.append(copy(a, 4, block(a, xn), block(a, xn), sibling))
            if split[a]:
                sent.append(copy(a, 6, block(a, xn, 0), block(a, xn, 0), yn))
            first[a, 2].wait_recv()
            sent.append(copy(a, 5, block(a, yn), block(a, yn), sibling))
            if split[a]:
                sent.append(copy(a, 7, block(a, yn, 1), block(a, yn, 1), xn))
            for cp in sent[-(4 if split[a] else 2):]:
                cp.start()
        for a in range(n):
            if split[a]:
                copy(a, 6, block(a, diag, 0), block(a, diag, 0), yn).wait_recv()
                sent.append(copy(a, 8, block(a, diag, 0), block(a, diag, 0), sibling))
                sent[-1].start()
                copy(a, 7, block(a, diag, 1), block(a, diag, 1), xn).wait_recv()
                sent.append(copy(a, 3, block(a, diag, 1), block(a, diag, 1), sibling))
                sent[-1].start()
            else:
                first[a, 3].wait_recv()
                sent.append(copy(a, 8, block(a, diag), block(a, diag), sibling))
                sent[-1].start()
        for a in range(n):
            first[a, 0].wait_recv()
            copy(a, 4, block(a, xn), block(a, xn), sibling).wait_recv()
            copy(a, 5, block(a, yn), block(a, yn), sibling).wait_recv()
            if split[a]:
                copy(a, 8, block(a, diag, 0), block(a, diag, 0), sibling).wait_recv()
                copy(a, 3, block(a, diag, 1), block(a, diag, 1), sibling).wait_recv()
            else:
                copy(a, 8, block(a, diag), block(a, diag), sibling).wait_recv()
        for cp in sent:
            cp.wait_send()

    launch()
    return zone_refs


def _gather_end(started, after, plan, name):
    _, lands = _remote_wait(started, after, plan, name + "_wait")
    return _gather_finish(lands, name + "_finish")


class _ShardedWeights:
    def __init__(self, w_in, w_out, conv_w, w_up, w_down, me, csel):
        self.me, self.csel = me, csel
        own_rows = lax.dynamic_update_slice(jnp.zeros((SUPER, D_MODEL), F32), jnp.transpose(w_in), (2 * me, 0))
        self.in_ref, self.conv_ref = _sequencer_gather([own_rows.astype(BF16), conv_w], [True, False], me, 7,
                                                       "gather_w_in_sequencer")
        (self.out_ref,) = _sequencer_gather([w_out.astype(BF16)], [True], me, 8, "gather_w_out_sequencer")
        (self.up_ref,) = _sequencer_gather([w_up.astype(BF16)], [True], me, 9, "gather_w_up_sequencer")
        (self.down_ref,) = _sequencer_gather([w_down.astype(BF16)], [True], me, 10, "gather_w_down_sequencer")
        self.reduces = {}

    def mixer_weights(self, after):
        g_conv = self.conv_ref[...]
        conv_w_f = jnp.concatenate([g_conv[i] for i in range(N_DEV)], axis=1)
        return _w_in_from_super_slabs(self.in_ref[...]), conv_w_f, None

    def prefetch(self, k, after):
        return None

    def out_weight(self, after):
        return self.out_ref[...].reshape(D_MODEL, D_MODEL)

    def up_weight(self, after):
        return self.up_ref[...]

    def down_weight(self, after):
        return self.down_ref[...].reshape(D_FF, D_MODEL)

    def _chips_start(self, slabs, from_sibling, rows, tag):
        sums = [_pair_add(s, r, self.csel, tr, f"pair_add_{tag}_{i}")
                for i, (s, r, tr) in enumerate(zip(slabs, from_sibling, rows))]
        lands = [lax.empty((3,) + s.shape[1:], s.dtype) for s in sums]
        self.reduces[tag] = _remote_start(sums, lands, _chips_plan, 3 * len(sums), f"reduce_start_{tag}")
        return self.reduces[tag][4]

    def mlp_grads(self, h2, du, act, dx3b):
        def send(part, tag, after):
            st = _remote_start([part], [lax.empty(part.shape, part.dtype)], _pair4_plan, 4,
                               f"reduce_pair_start_{tag}", after=after)
            return st

        def received(st, after, tag):
            return _remote_wait(st, after, _pair4_plan, f"reduce_pair_wait_{tag}")[1][0]

        def to_chips(sums, tag):
            self.reduces[tag] = _remote_start([sums], [lax.empty((3,) + sums.shape[1:], sums.dtype)], _chips_plan, 3,
                                              f"reduce_start_{tag}")
            return self.reduces[tag][4]

        up_send = _grad_w_up(h2, du, "grad_w_up_send", sel=(self.csel, True))
        st_up = send(up_send, "up", None)
        down_send = _grad_w_down(act, dx3b, "grad_w_down_send", sel=(self.csel, True), after=st_up[4])
        st_down = send(down_send, "down", None)
        up_sum = _grad_w_up(h2, du, "grad_w_up_keep", sel=(self.csel, False), add=received(st_up, down_send, "up"),
                            after=st_down[4])
        token = to_chips(up_sum, "up")
        down_sum = _grad_w_down(act, dx3b, "grad_w_down_keep", sel=(self.csel, False),
                                add=received(st_down, up_sum, "down"), after=token)
        return to_chips(down_sum, "down")

    def out_grad(self, g_out):
        slabs = [g_out.reshape(N_DEV, D_MODEL // N_DEV, D_MODEL)]
        return self._chips_start(slabs, _exchange_pair(slabs, "reduce_pair_out"), [256], "out")

    def in_grad(self, g_in):
        slabs = [jnp.stack([_natural_rows(g_in, SUPER_STEP * j, SUPER_STEP * j + SUPER) for j in range(N_DEV)])]
        return self._chips_start(slabs, _exchange_pair(slabs, "reduce_pair_in"), [SUPER], "in")

    def small_start(self, small):
        self.st_small = _remote_start([small], [_landing(small, self.me)], _everyone_plan, N_DEV - 1, "gather_start_small")

    def small_end(self, after):
        return _remote_wait(self.st_small, after, _everyone_plan, "gather_small_wait")[1][0]

    def reduce_end(self, tag, after):
        return _remote_wait(self.reduces[tag], after, _chips_plan, f"reduce_wait_{tag}")


def kernel(x, mix_norm_g, w_in, conv_w, conv_b, dt_bias, A_log, D_skip, ssm_norm_g, attn_sinks, attn_out_norm_g, w_out, mlp_norm_g, w_up, w_down, final_norm_g, loss_target, m_mix_norm_g, m_w_in, m_conv_w, m_conv_b, m_dt_bias, m_A_log, m_D_skip, m_ssm_norm_g, m_attn_sinks, m_attn_out_norm_g, m_w_out, m_mlp_norm_g, m_w_up, m_w_down, m_final_norm_g, v_mix_norm_g, v_w_in, v_conv_w, v_conv_b, v_dt_bias, v_A_log, v_D_skip, v_ssm_norm_g, v_attn_sinks, v_attn_out_norm_g, v_w_out, v_mlp_norm_g, v_w_up, v_w_down, v_final_norm_g):
    xi, yi, ci = _coords()
    me = 4 * xi + 2 * yi + ci
    csel = jnp.reshape(ci, (1,)).astype(jnp.int32)
    qsel = jnp.reshape(2 * xi + yi, (1,)).astype(jnp.int32)
    w = dict(mix_norm_g=mix_norm_g, conv_b=conv_b, dt_bias=dt_bias, A_log=A_log, D_skip=D_skip,
             ssm_norm_g=ssm_norm_g, attn_sinks=attn_sinks, attn_out_norm_g=attn_out_norm_g, mlp_norm_g=mlp_norm_g,
             final_norm_g=final_norm_g)
    hooks = _ShardedWeights(w_in[0], w_out[0], conv_w[0], w_up[0], w_down[0], me, csel)
    p = dict(w)
    dx, small = _local_step(x[0], loss_target[0], p, hooks)
    hooks.small_start(small)
    big = {}
    after = dx
    for name, wt, mt, vt, tile in [
            ("up", w_up, m_w_up, v_w_up, (512, SLAB)), ("down", w_down, m_w_down, v_w_down, (256, D_MODEL)),
            ("out", w_out, m_w_out, v_w_out, (256, D_MODEL))]:
        (chip_sums,), (from_chips,) = hooks.reduce_end(name, after)
        res = _adamw_big(wt[0], mt[0], vt[0], chip_sums, from_chips, qsel, tile, f"adamw_w_{name}")
        big["w_" + name] = tuple(r[None] for r in res)
        after = res[0]
    (chip_sums,), (from_chips,) = hooks.reduce_end("in", after)
    g_super = _sum_partials(chip_sums, from_chips, qsel, 512, "grad_w_in_sum")
    g_in = lax.dynamic_slice(g_super, (2 * me, 0), (PER, D_MODEL))
    res = _adamw_tiled(jnp.transpose(w_in[0]), g_in, jnp.transpose(m_w_in[0]), jnp.transpose(v_w_in[0]), 512,
                       "adamw_w_in")
    big["w_in"] = tuple(jnp.transpose(r)[None] for r in (g_in, *res))
    after = res[0]
    gsum = _small_sum(hooks.small_end(after), "small_sum")
    loss = gsum[9, 64]
    gs = _unpack_small(gsum, CONV_DIM)
    cw = CONV_DIM // N_DEV
    g_conv_shard = lax.dynamic_slice(gsum[5:9, :], (0, me * cw), (CONV_K, cw))

    def pack(s):
        return _pack_small(s["mix_norm_g"], s["conv_b"], s["ssm_norm_g"], s["attn_out_norm_g"], s["mlp_norm_g"],
                           s["final_norm_g"], s["conv_w"][0], s["dt_bias"], s["A_log"], s["D_skip"], s["attn_sinks"])

    wp = pack(dict(w, conv_w=conv_w))
    mp = pack(dict(mix_norm_g=m_mix_norm_g, conv_b=m_conv_b, ssm_norm_g=m_ssm_norm_g,
                   attn_out_norm_g=m_attn_out_norm_g, mlp_norm_g=m_mlp_norm_g, final_norm_g=m_final_norm_g,
                   conv_w=m_conv_w, dt_bias=m_dt_bias, A_log=m_A_log, D_skip=m_D_skip, attn_sinks=m_attn_sinks))
    vp = pack(dict(mix_norm_g=v_mix_norm_g, conv_b=v_conv_b, ssm_norm_g=v_ssm_norm_g,
                   attn_out_norm_g=v_attn_out_norm_g, mlp_norm_g=v_mlp_norm_g, final_norm_g=v_final_norm_g,
                   conv_w=v_conv_w, dt_bias=v_dt_bias, A_log=v_A_log, D_skip=v_D_skip, attn_sinks=v_attn_sinks))
    gp = jnp.concatenate([gsum[0:5], jnp.pad(g_conv_shard, ((0, 0), (0, D_MODEL - cw))), gsum[9:10],
                          jnp.zeros((SMALL_ROWS - 10, D_MODEL), F32)], axis=0)
    dp, mnp, vnp = _adamw_small(wp, gp, mp, vp, "adamw_small")
    grads = dict(gs, conv_w=g_conv_shard[None])
    deltas = _unpack_small(dp, cw)
    new_m = _unpack_small(mnp, cw)
    new_v = _unpack_small(vnp, cw)
    for k, name in enumerate(["w_in", "w_out", "w_up", "w_down"]):
        grads[name], deltas[name], new_m[name], new_v[name] = big[name]
    return (loss, dx[None], *[grads[n] for n in WEIGHT_ORDER], *[deltas[n] for n in WEIGHT_ORDER],
            *[new_m[n] for n in WEIGHT_ORDER], *[new_v[n] for n in WEIGHT_ORDER])
```

```python
import functools

import jax
import jax.numpy as jnp
from jax import lax
from jax.experimental import pallas as pl
from jax.experimental.pallas import tpu as pltpu
from jax.experimental.pallas import tpu_sc as plsc

F32 = jnp.float32
BF16 = jnp.bfloat16
HI = lax.Precision.HIGHEST
MESH = pl.DeviceIdType.MESH

EPS = 1e-5
D_MODEL = 2048
D_INNER = 1024
N_HEADS = 16
HEAD_DIM = 64
N_GROUPS = 4
D_STATE = 128
CHUNK = 128
CONV_K = 4
CONV_DIM = 2048
ATTN_W = 1024
KV_W = 128
WINDOW = 128
D_FF = 8192
IN_PROJ = 4368
N_DEV = 8
NP = 4608
OFF_Z, OFF_X, OFF_B, OFF_C, OFF_Q, OFF_K, OFF_V, OFF_DT = 0, 1024, 2048, 2560, 3072, 4096, 4224, 4352
NAT_DT = 3072

ADAM_LR = 0.001
ADAM_B1 = 0.9
ADAM_B2 = 0.999
ADAM_EPS = 1e-08
ADAM_WD = 0.01
ADAM_STEP = 10

VMEM_LIMIT = 52 * 1024 * 1024
SMALL_ROWS = 16
NEG = -1e30


def _cparams(sem=None):
    return pltpu.CompilerParams(dimension_semantics=sem, vmem_limit_bytes=VMEM_LIMIT)


def _split3(v):
    hi = v.astype(BF16)
    rest = v - hi.astype(F32)
    mid = rest.astype(BF16)
    return hi, mid, (rest - mid.astype(F32)).astype(BF16)


def _hdot(a, b, data):
    if data == "a":
        sel = b.astype(BF16)
        return sum(_dot_nn(part, sel) for part in _split3(a))
    sel = a.astype(BF16)
    return sum(_dot_nn(sel, part) for part in _split3(b))


def _dot_nn(a, b):
    return lax.dot_general(a, b, (((1,), (0,)), ((), ())), preferred_element_type=F32)


def _dot_nt(a, b):
    return lax.dot_general(a, b, (((1,), (1,)), ((), ())), preferred_element_type=F32)


def _dot_tn(a, b):
    return lax.dot_general(a, b, (((0,), (0,)), ((), ())), preferred_element_type=F32)


def _softplus(v):
    return jnp.maximum(v, 0.0) + jnp.log1p(jnp.exp(-jnp.abs(v)))


def _sigmoid(v):
    return 1.0 / (1.0 + jnp.exp(-v))


def _matmul(a, b, *, mode, grid, a_spec, b_spec, out_shapes, out_specs, tile, name,
            extras=(), extra_specs=(), epilogue=None, after=None, dot_fn=None, prefetch=None):
    nk = grid[2]
    n_ex = len(extras)
    n_out = len(out_shapes)
    dot = dot_fn if dot_fn is not None else {"nn": _dot_nn, "nt": _dot_nt, "tn": _dot_tn}[mode]

    def finish(acc, ex_refs, out_refs):
        res = (acc,) if epilogue is None else epilogue(acc, *[e[...] for e in ex_refs])
        for o, r in zip(out_refs, res):
            o[...] = r.astype(o.dtype)

    def body(*refs):
        a_ref, b_ref = refs[0], refs[1]
        ex_refs = refs[2:2 + n_ex]
        out_refs = refs[2 + n_ex:2 + n_ex + n_out]
        part = dot(a_ref[...].astype(BF16), b_ref[...].astype(BF16))
        if nk == 1:
            finish(part, ex_refs, out_refs)
        else:
            acc_ref = refs[-1]
            k = pl.program_id(2)

            @pl.when(k == 0)
            def _():
                acc_ref[...] = part

            @pl.when(k > 0)
            def _():
                acc_ref[...] += part

            @pl.when(k == nk - 1)
            def _():
                finish(acc_ref[...], ex_refs, out_refs)

    scratch = [] if nk == 1 else [pltpu.VMEM(tile, F32)]
    n_pre = 0 if prefetch is None else 1
    tok_specs = [] if after is None else [pl.BlockSpec((8, 128), lambda *_: (0, 0))]
    tok_args = [] if after is None else [after]

    def body_with_token(*refs):
        refs = refs[n_pre:]
        body(*refs[:2 + n_ex], *refs[2 + n_ex + len(tok_args):])

    in_specs = [a_spec, b_spec, *extra_specs, *tok_specs]
    params = _cparams(("parallel", "parallel", "arbitrary"))
    if prefetch is None:
        return pl.pallas_call(
            body_with_token, grid=grid, in_specs=in_specs, out_specs=list(out_specs), out_shape=list(out_shapes),
            scratch_shapes=scratch, name=name, compiler_params=params)(a, b, *extras, *tok_args)
    return pl.pallas_call(
        body_with_token,
        grid_spec=pltpu.PrefetchScalarGridSpec(num_scalar_prefetch=1, grid=grid, in_specs=in_specs,
                                               out_specs=list(out_specs), scratch_shapes=scratch),
        out_shape=list(out_shapes), name=name, compiler_params=params)(prefetch, a, b, *extras, *tok_args)


def _mm_simple(a, b, *, mode, M, N, K, tm, tn, tk, out_dtype, name, extras=(), epilogue=None, n_out=1,
               out_dtypes=None, after=None):
    grid = (M // tm, N // tn, K // tk)
    if mode == "nn":
        a_spec = pl.BlockSpec((tm, tk), lambda i, j, k: (i, k))
        b_spec = pl.BlockSpec((tk, tn), lambda i, j, k: (k, j))
    elif mode == "nt":
        a_spec = pl.BlockSpec((tm, tk), lambda i, j, k: (i, k))
        b_spec = pl.BlockSpec((tn, tk), lambda i, j, k: (j, k))
    else:
        a_spec = pl.BlockSpec((tk, tm), lambda i, j, k: (k, i))
        b_spec = pl.BlockSpec((tk, tn), lambda i, j, k: (k, j))
    o_spec = pl.BlockSpec((tm, tn), lambda i, j, k: (i, j))
    dts = out_dtypes if out_dtypes is not None else [out_dtype] * n_out
    return _matmul(a, b, mode=mode, grid=grid, a_spec=a_spec, b_spec=b_spec,
                   out_shapes=[jax.ShapeDtypeStruct((M, N), d) for d in dts],
                   out_specs=[o_spec] * len(dts), tile=(tm, tn), name=name,
                   extras=extras, extra_specs=[o_spec] * len(extras), epilogue=epilogue, after=after)


ROW_BLOCK = 256


def _rmsnorm_fwd(x, g, name):
    T, D = x.shape

    def body(x_ref, g_ref, o_ref):
        xf = x_ref[...]
        r = lax.rsqrt(jnp.mean(xf * xf, axis=-1, keepdims=True) + EPS)
        o_ref[...] = (xf * r * g_ref[...]).astype(BF16)

    return pl.pallas_call(
        body, grid=(T // ROW_BLOCK,),
        in_specs=[pl.BlockSpec((ROW_BLOCK, D), lambda i: (i, 0)), pl.BlockSpec((1, D), lambda i: (0, 0))],
        out_specs=pl.BlockSpec((ROW_BLOCK, D), lambda i: (i, 0)),
        out_shape=jax.ShapeDtypeStruct((T, D), BF16), name=name, compiler_params=_cparams(("parallel",)),
    )(x, g)


def _rmsnorm_bwd(dh, x, g, dres, name):
    T, D = x.shape

    def body(dh_ref, x_ref, g_ref, dres_ref, dx_ref, dxb_ref, dg_ref):
        i = pl.program_id(0)
        xf = x_ref[...]
        r = lax.rsqrt(jnp.mean(xf * xf, axis=-1, keepdims=True) + EPS)
        xh = xf * r
        d = dh_ref[...]

        @pl.when(i == 0)
        def _():
            dg_ref[...] = jnp.zeros_like(dg_ref)

        dg_ref[...] += jnp.sum(d * xh, axis=0, keepdims=True)
        dxh = d * g_ref[...]
        dx = r * (dxh - xh * jnp.mean(dxh * xh, axis=-1, keepdims=True)) + dres_ref[...]
        dx_ref[...] = dx
        dxb_ref[...] = dx.astype(BF16)

    row = pl.BlockSpec((ROW_BLOCK, D), lambda i: (i, 0))
    vec = pl.BlockSpec((1, D), lambda i: (0, 0))
    return pl.pallas_call(
        body, grid=(T // ROW_BLOCK,), in_specs=[row, row, vec, row], out_specs=[row, row, vec],
        out_shape=[jax.ShapeDtypeStruct((T, D), F32), jax.ShapeDtypeStruct((T, D), BF16),
                   jax.ShapeDtypeStruct((1, D), F32)],
        name=name, compiler_params=_cparams(("arbitrary",)),
    )(dh, x, g, dres)


def _final_loss(x3, tgt, g, name):
    T, D = x3.shape

    def body(x_ref, t_ref, g_ref, loss_ref, dg_ref, dx_ref, dxb_ref):
        i = pl.program_id(0)
        xf = x_ref[...]
        r = lax.rsqrt(jnp.mean(xf * xf, axis=-1, keepdims=True) + EPS)
        xh = xf * r
        gg = g_ref[...]
        err = xh * gg - t_ref[...]

        @pl.when(i == 0)
        def _():
            dg_ref[...] = jnp.zeros_like(dg_ref)
            loss_ref[...] = jnp.zeros_like(loss_ref)

        part = jnp.sum(jnp.sum(err * err, axis=-1, keepdims=True), axis=0, keepdims=True) * (0.5 / D)
        loss_ref[...] += jnp.broadcast_to(part, loss_ref.shape)
        dout = err * (1.0 / D)
        dg_ref[...] += jnp.sum(dout * xh, axis=0, keepdims=True)
        dxh = dout * gg
        dx = r * (dxh - xh * jnp.mean(dxh * xh, axis=-1, keepdims=True))
        dx_ref[...] = dx
        dxb_ref[...] = dx.astype(BF16)

    row = pl.BlockSpec((ROW_BLOCK, D), lambda i: (i, 0))
    vec = pl.BlockSpec((1, D), lambda i: (0, 0))
    return pl.pallas_call(
        body, grid=(T // ROW_BLOCK,), in_specs=[row, row, vec],
        out_specs=[pl.BlockSpec((1, 128), lambda i: (0, 0)), vec, row, row],
        out_shape=[jax.ShapeDtypeStruct((1, 128), F32), jax.ShapeDtypeStruct((1, D), F32),
                   jax.ShapeDtypeStruct((T, D), F32), jax.ShapeDtypeStruct((T, D), BF16)],
        name=name, compiler_params=_cparams(("arbitrary",)),
    )(x3, tgt, g)


CONV_BLOCK = 256


def _conv_apply(u, w, b):
    row = lax.broadcasted_iota(jnp.int32, u.shape, 0)
    acc = b + w[CONV_K - 1:CONV_K, :] * u
    shifted = []
    for j in range(1, CONV_K):
        uj = jnp.where(row >= j, pltpu.roll(u, j, axis=0), 0.0)
        shifted.append(uj)
        acc = acc + w[CONV_K - 1 - j:CONV_K - j, :] * uj
    return acc, shifted


def _conv_fwd(proj, conv_w, conv_b, name):
    T = proj.shape[0]
    cb0 = OFF_X // CONV_BLOCK

    def body(u_ref, w_ref, b_ref, o_ref):
        c, _ = _conv_apply(u_ref[...], w_ref[...], b_ref[...])
        o_ref[...] = c * _sigmoid(c)

    return pl.pallas_call(
        body, grid=(CONV_DIM // CONV_BLOCK,),
        in_specs=[pl.BlockSpec((T, CONV_BLOCK), lambda j: (0, cb0 + j)),
                  pl.BlockSpec((CONV_K, CONV_BLOCK), lambda j: (0, j)),
                  pl.BlockSpec((1, CONV_BLOCK), lambda j: (0, j))],
        out_specs=pl.BlockSpec((T, CONV_BLOCK), lambda j: (0, j)),
        out_shape=jax.ShapeDtypeStruct((T, CONV_DIM), F32), name=name, compiler_params=_cparams(("parallel",)),
    )(proj, conv_w, conv_b)


def _conv_bwd(proj, dact, conv_w, conv_b, dproj, name):
    T = proj.shape[0]
    cb0 = OFF_X // CONV_BLOCK

    def body(u_ref, d_ref, w_ref, b_ref, _, du_ref, dw_ref, db_ref):
        u = u_ref[...]
        w = w_ref[...]
        c, shifted = _conv_apply(u, w, b_ref[...])
        sg = _sigmoid(c)
        dc = d_ref[...] * sg * (1.0 + c * (1.0 - sg))
        row = lax.broadcasted_iota(jnp.int32, u.shape, 0)
        du = w[CONV_K - 1:CONV_K, :] * dc
        dw_ref[CONV_K - 1:CONV_K, :] = jnp.sum(dc * u, axis=0, keepdims=True)
        for j in range(1, CONV_K):
            dcj = jnp.where(row < T - j, pltpu.roll(dc, T - j, axis=0), 0.0)
            du = du + w[CONV_K - 1 - j:CONV_K - j, :] * dcj
            dw_ref[CONV_K - 1 - j:CONV_K - j, :] = jnp.sum(dc * shifted[j - 1], axis=0, keepdims=True)
        db_ref[...] = jnp.sum(dc, axis=0, keepdims=True)
        du_ref[...] = du.astype(BF16)

    return pl.pallas_call(
        body, grid=(CONV_DIM // CONV_BLOCK,),
        in_specs=[pl.BlockSpec((T, CONV_BLOCK), lambda j: (0, cb0 + j)),
                  pl.BlockSpec((T, CONV_BLOCK), lambda j: (0, j)),
                  pl.BlockSpec((CONV_K, CONV_BLOCK), lambda j: (0, j)),
                  pl.BlockSpec((1, CONV_BLOCK), lambda j: (0, j)), pl.BlockSpec(memory_space=pl.ANY)],
        out_specs=[pl.BlockSpec((T, CONV_BLOCK), lambda j: (0, cb0 + j)),
                   pl.BlockSpec((CONV_K, CONV_BLOCK), lambda j: (0, j)),
                   pl.BlockSpec((1, CONV_BLOCK), lambda j: (0, j))],
        out_shape=[jax.ShapeDtypeStruct(dproj.shape, BF16), jax.ShapeDtypeStruct((CONV_K, CONV_DIM), F32),
                   jax.ShapeDtypeStruct((1, CONV_DIM), F32)],
        input_output_aliases={4: 0}, name=name, compiler_params=_cparams(("parallel",)),
    )(proj, dact, conv_w, conv_b, dproj)


GROUP_W = D_INNER // N_GROUPS
HEADS_PER_GROUP = N_HEADS // N_GROUPS


def _expand_mat():
    h = lax.broadcasted_iota(jnp.int32, (N_HEADS, D_INNER), 0)
    j = lax.broadcasted_iota(jnp.int32, (N_HEADS, D_INNER), 1)
    return (j // HEAD_DIM == h).astype(F32)


def _reduce_mat(g):
    j = lax.broadcasted_iota(jnp.int32, (GROUP_W, N_HEADS), 0)
    h = lax.broadcasted_iota(jnp.int32, (GROUP_W, N_HEADS), 1)
    return (g * HEADS_PER_GROUP + j // HEAD_DIM == h).astype(F32)


def _col16(v, h):
    lane = lax.broadcasted_iota(jnp.int32, v.shape, 1)
    return jnp.sum(jnp.where(lane == h, v, 0.0), axis=1, keepdims=True)


def _ssd_pre(dt_raw, dtT_raw, dtb, dtbT, alog, alogT):
    Q = CHUNK
    xdt = dt_raw + dtb
    dt = _softplus(xdt)
    dtT = _softplus(dtT_raw + dtbT)
    A = -jnp.exp(alog)
    AT = -jnp.exp(alogT)
    row = lax.broadcasted_iota(jnp.int32, (Q, Q), 0)
    col = lax.broadcasted_iota(jnp.int32, (Q, Q), 1)
    tril = (row >= col).astype(F32)
    triu = (row <= col).astype(F32)
    cs = _hdot(tril, dt * A, "b")
    csT = _hdot(dtT * AT, triu, "a")
    return xdt, dt, A, cs, csT, row >= col, triu


def _decay_matrix(cs, csT, h, causal):
    seg = _col16(cs, h) - csT[h:h + 1, :]
    return jnp.where(causal, jnp.exp(jnp.minimum(seg, 0.0)), 0.0)


def _ssd_in_specs(nc, rev):
    def cidx(c):
        return (nc - 1 - c) if rev else c

    return [
        pl.BlockSpec((CHUNK, D_INNER), lambda c: (cidx(c), 0)),
        pl.BlockSpec((CHUNK, 512), lambda c: (cidx(c), 2)),
        pl.BlockSpec((CHUNK, 512), lambda c: (cidx(c), 3)),
        pl.BlockSpec((CHUNK, D_INNER), lambda c: (cidx(c), 0)),
        pl.BlockSpec((CHUNK, 128), lambda c: (cidx(c), OFF_DT // 128)),
        pl.BlockSpec((N_HEADS, CHUNK), lambda c: (0, cidx(c))),
        pl.BlockSpec((1, N_HEADS), lambda c: (0, 0)),
        pl.BlockSpec((N_HEADS, 1), lambda c: (0, 0)),
        pl.BlockSpec((1, N_HEADS), lambda c: (0, 0)),
        pl.BlockSpec((N_HEADS, 1), lambda c: (0, 0)),
        pl.BlockSpec((1, D_INNER), lambda c: (0, 0)),
        pl.BlockSpec((1, D_INNER), lambda c: (0, 0)),
    ]


def _ssd_fwd(xbc, proj, dtT, dtb, dtbT, alog, alogT, dfull, ng, name):
    T = xbc.shape[0]
    nc = T // CHUNK
    Q = CHUNK

    def body(xs_ref, B_ref, C_ref, z_ref, dt_ref, dtT_ref, dtb_ref, dtbT_ref, al_ref, alT_ref, df_ref, ng_ref,
             y_ref, ypre_ref, hs_ref, h_scr):
        c = pl.program_id(0)

        @pl.when(c == 0)
        def _():
            h_scr[...] = jnp.zeros_like(h_scr)

        _, dt, _, cs, csT, causal, _ = _ssd_pre(dt_ref[:, :N_HEADS], dtT_ref[...], dtb_ref[...], dtbT_ref[...],
                                                al_ref[...], alT_ref[...])
        ex = _expand_mat()
        dt_full = _hdot(dt, ex, "a")
        cs_full = _hdot(cs, ex, "a")
        cs_last = cs_full[Q - 1:Q, :]
        xs = xs_ref[...]
        xd = xs * dt_full
        e_full = jnp.exp(cs_full)
        dec_full = jnp.exp(cs_last - cs_full)
        cd_full = jnp.exp(cs_last)
        lane_head = lax.broadcasted_iota(jnp.int32, (1, GROUP_W), 1) // HEAD_DIM
        for g in range(N_GROUPS):
            sl = slice(g * GROUP_W, (g + 1) * GROUP_W)
            Bg = B_ref[:, g * D_STATE:(g + 1) * D_STATE].astype(BF16)
            Cg = C_ref[:, g * D_STATE:(g + 1) * D_STATE].astype(BF16)
            CB = _dot_nt(Cg, Bg)
            hg = h_scr[g]
            yoff = _dot_nn(Cg, hg.astype(BF16)) * e_full[:, sl]
            xd_g = xd[:, sl]
            S = _dot_tn(Bg, (xd_g * dec_full[:, sl]).astype(BF16))
            xd_b = xd_g.astype(BF16)
            ydiag = jnp.zeros((Q, GROUP_W), F32)
            for r in range(HEADS_PER_GROUP):
                Lm = _decay_matrix(cs, csT, g * HEADS_PER_GROUP + r, causal)
                Gm = (CB * Lm).astype(BF16)
                ydiag = ydiag + _dot_nn(Gm, jnp.where(lane_head == r, xd_b, jnp.zeros_like(xd_b)))
            hs_ref[0, g] = hg
            h_scr[g] = hg * cd_full[:, sl] + S
            ypre = ydiag + yoff + xs[:, sl] * df_ref[:, sl]
            ypre_ref[:, sl] = ypre
            zg = z_ref[:, sl]
            yz = ypre * zg * _sigmoid(zg)
            rn = lax.rsqrt(jnp.mean(yz * yz, axis=-1, keepdims=True) + EPS)
            y_ref[:, sl] = (yz * rn * ng_ref[:, sl]).astype(BF16)

    return pl.pallas_call(
        body, grid=(nc,), in_specs=_ssd_in_specs(nc, False),
        out_specs=[pl.BlockSpec((CHUNK, D_INNER), lambda c: (c, 0)),
                   pl.BlockSpec((CHUNK, D_INNER), lambda c: (c, 0)),
                   pl.BlockSpec((1, N_GROUPS, D_STATE, GROUP_W), lambda c: (c, 0, 0, 0))],
        out_shape=[jax.ShapeDtypeStruct((T, D_INNER + ATTN_W), BF16), jax.ShapeDtypeStruct((T, D_INNER), F32),
                   jax.ShapeDtypeStruct((nc, N_GROUPS, D_STATE, GROUP_W), F32)],
        scratch_shapes=[pltpu.VMEM((N_GROUPS, D_STATE, GROUP_W), F32)],
        name=name, compiler_params=_cparams(("arbitrary",)),
    )(xbc, xbc, xbc, proj, proj, dtT, dtb, dtbT, alog, alogT, dfull, ng)


def _ssd_bwd(xbc, proj, dtT, dtb, dtbT, alog, alogT, dfull, ng, ypre, hs, dy, name):
    T = xbc.shape[0]
    nc = T // CHUNK
    Q = CHUNK

    def body(xs_ref, B_ref, C_ref, z_ref, dt_ref, dtT_ref, dtb_ref, dtbT_ref, al_ref, alT_ref, df_ref, ng_ref,
             ypre_ref, hs_ref, dy_ref,
             dz_ref, dxbc_ref, ddtb_ref, dal_ref, dD_ref, dng_ref, dh_scr):
        step = pl.program_id(0)

        @pl.when(step == 0)
        def _():
            dh_scr[...] = jnp.zeros_like(dh_scr)
            ddtb_ref[...] = jnp.zeros_like(ddtb_ref)
            dal_ref[...] = jnp.zeros_like(dal_ref)
            dD_ref[...] = jnp.zeros_like(dD_ref)
            dng_ref[...] = jnp.zeros_like(dng_ref)

        xdt, dt, A, cs, csT, causal, triu = _ssd_pre(dt_ref[:, :N_HEADS], dtT_ref[...], dtb_ref[...],
                                                    dtbT_ref[...], al_ref[...], alT_ref[...])
        ex = _expand_mat()
        dt_full = _hdot(dt, ex, "a")
        cs_full = _hdot(cs, ex, "a")
        cs_last = cs_full[Q - 1:Q, :]
        xs = xs_ref[...]
        xd = xs * dt_full
        e_full = jnp.exp(cs_full)
        dec_full = jnp.exp(cs_last - cs_full)
        cd_full = jnp.exp(cs_last)
        lane_head = lax.broadcasted_iota(jnp.int32, (1, GROUP_W), 1) // HEAD_DIM
        is_last = lax.broadcasted_iota(jnp.int32, (Q, 1), 0) == Q - 1
        dcs16 = jnp.zeros((Q, N_HEADS), F32)
        ddtx16 = jnp.zeros((Q, N_HEADS), F32)
        dD16 = jnp.zeros((8, N_HEADS), F32)
        lane16 = lax.broadcasted_iota(jnp.int32, (1, N_HEADS), 1)
        sub16 = lax.broadcasted_iota(jnp.int32, (N_HEADS, 1), 0)
        col_sums = jnp.zeros((N_HEADS, Q), F32)
        for g in range(N_GROUPS):
            sl = slice(g * GROUP_W, (g + 1) * GROUP_W)
            red = _reduce_mat(g)
            ypre_g = ypre_ref[:, sl]
            zg = z_ref[:, sl]
            sg = _sigmoid(zg)
            silu = zg * sg
            yz = ypre_g * silu
            rn = lax.rsqrt(jnp.mean(yz * yz, axis=-1, keepdims=True) + EPS)
            yh = yz * rn
            dy_g = dy_ref[:, sl]
            dng_ref[:, sl] += jnp.sum(dy_g * yh, axis=0, keepdims=True)
            dyh = dy_g * ng_ref[:, sl]
            dyz = rn * (dyh - yh * jnp.mean(dyh * yh, axis=-1, keepdims=True))
            dY = dyz * silu
            dz_ref[:, sl] = (dyz * ypre_g * sg * (1.0 + zg * (1.0 - sg))).astype(BF16)
            xs_g = xs[:, sl]
            xd_g = xd[:, sl]
            dec_g = dec_full[:, sl]
            cd_g = cd_full[:, sl]
            d_g = df_ref[:, sl]
            Bg = B_ref[:, g * D_STATE:(g + 1) * D_STATE].astype(BF16)
            Cg = C_ref[:, g * D_STATE:(g + 1) * D_STATE].astype(BF16)
            CB = _dot_nt(Cg, Bg)
            hg = hs_ref[0, g]
            hgb = hg.astype(BF16)
            yoff = _dot_nn(Cg, hgb) * e_full[:, sl]
            dhn = dh_scr[g]
            dhnb = dhn.astype(BF16)
            dYE = (dY * e_full[:, sl]).astype(BF16)
            dC = _dot_nt(dYE, hgb)
            dh_direct = _dot_tn(Cg, dYE)
            dXdd = _dot_nn(Bg, dhnb)
            dB = _dot_nt((xd_g * dec_g).astype(BF16), dhnb)
            dcd = jnp.sum(dhn * hg, axis=0, keepdims=True)
            dh_scr[g] = dh_direct + cd_g * dhn
            dYb = dY.astype(BF16)
            xd_b = xd_g.astype(BF16)
            dCB = jnp.zeros((Q, Q), F32)
            dXd = dXdd * dec_g
            for r in range(HEADS_PER_GROUP):
                h = g * HEADS_PER_GROUP + r
                Lm = _decay_matrix(cs, csT, h, causal)
                Gf = CB * Lm
                dYr = jnp.where(lane_head == r, dYb, jnp.zeros_like(dYb))
                dG = _dot_nt(dYr, xd_b)
                dCB = dCB + dG * Lm
                dXd = dXd + _dot_tn(Gf.astype(BF16), dYr)
                Mm = dG * Gf
                dcs16 = dcs16 + jnp.where(lane16 == h, jnp.sum(Mm, axis=1, keepdims=True), 0.0)
                col_sums = col_sums + jnp.where(sub16 == h, jnp.sum(Mm, axis=0, keepdims=True), 0.0)
            dCBb = dCB.astype(BF16)
            dC = dC + _dot_nn(dCBb, Bg)
            dB = dB + _dot_tn(dCBb, Cg)
            w_state = dXdd * dec_g * xd_g
            t_last = jnp.sum(w_state, axis=0, keepdims=True) + dcd * cd_g
            dcs_g = dY * yoff - w_state + jnp.where(is_last, t_last, 0.0)
            dcs16 = dcs16 + _hdot(dcs_g, red, "a")
            ddtx16 = ddtx16 + _hdot(dXd * xs_g, red, "a")
            dD16 = dD16 + _hdot(jnp.broadcast_to(jnp.sum(dY * xs_g, axis=0, keepdims=True), (8, GROUP_W)), red, "a")
            dxbc_ref[:, sl] = dXd * dt_full[:, sl] + dY * d_g
            dxbc_ref[:, D_INNER + g * D_STATE:D_INNER + (g + 1) * D_STATE] = dB
            dxbc_ref[:, D_INNER + 512 + g * D_STATE:D_INNER + 512 + (g + 1) * D_STATE] = dC
        eye = (lax.broadcasted_iota(jnp.int32, (N_HEADS, N_HEADS), 0)
               == lax.broadcasted_iota(jnp.int32, (N_HEADS, N_HEADS), 1)).astype(BF16)
        dcs16 = dcs16 - sum(_dot_tn(part, eye) for part in _split3(col_sums))
        da = _hdot(triu, dcs16, "b")
        ddt = da * A + ddtx16
        ddt_raw = ddt * _sigmoid(xdt)
        pr = lax.broadcasted_iota(jnp.int32, (N_HEADS, 128), 0)
        pc = lax.broadcasted_iota(jnp.int32, (N_HEADS, 128), 1)
        dz_ref[:, D_INNER:OFF_DT] = jnp.zeros((Q, OFF_DT - D_INNER), BF16)
        dz_ref[:, OFF_DT:OFF_DT + 128] = _hdot(ddt_raw, (pr == pc).astype(F32), "a").astype(BF16)
        dz_ref[:, OFF_DT + 128:] = jnp.zeros((Q, NP - OFF_DT - 128), BF16)
        ddtb_ref[...] += jnp.sum(ddt_raw, axis=0, keepdims=True)
        dal_ref[...] += jnp.sum(da * dt, axis=0, keepdims=True) * A
        dD_ref[...] += dD16[0:1, :]

    def rc(c):
        return nc - 1 - c

    in_specs = _ssd_in_specs(nc, True) + [
        pl.BlockSpec((CHUNK, D_INNER), lambda c: (rc(c), 0)),
        pl.BlockSpec((1, N_GROUPS, D_STATE, GROUP_W), lambda c: (rc(c), 0, 0, 0)),
        pl.BlockSpec((CHUNK, D_INNER), lambda c: (rc(c), 0)),
    ]
    small = pl.BlockSpec((1, N_HEADS), lambda c: (0, 0))
    return pl.pallas_call(
        body, grid=(nc,), in_specs=in_specs,
        out_specs=[pl.BlockSpec((CHUNK, NP), lambda c: (rc(c), 0)),
                   pl.BlockSpec((CHUNK, CONV_DIM), lambda c: (rc(c), 0)),
                   small, small, small,
                   pl.BlockSpec((1, D_INNER), lambda c: (0, 0))],
        out_shape=[jax.ShapeDtypeStruct((T, NP), BF16), jax.ShapeDtypeStruct((T, CONV_DIM), F32),
                   jax.ShapeDtypeStruct((1, N_HEADS), F32), jax.ShapeDtypeStruct((1, N_HEADS), F32),
                   jax.ShapeDtypeStruct((1, N_HEADS), F32), jax.ShapeDtypeStruct((1, D_INNER), F32)],
        scratch_shapes=[pltpu.VMEM((N_GROUPS, D_STATE, GROUP_W), F32)],
        name=name, compiler_params=_cparams(("arbitrary",)),
    )(xbc, xbc, xbc, proj, proj, dtT, dtb, dtbT, alog, alogT, dfull, ng, ypre, hs, dy)


N_PAIRS = ATTN_W // 128
PAIRS_PER_KV = N_PAIRS // 2
ATTN_SCALE = HEAD_DIM ** -0.5


def _kv_variants(kk):
    lo = lax.broadcasted_iota(jnp.int32, kk.shape, 1) < HEAD_DIM
    zero = jnp.zeros_like(kk)
    k00 = jnp.where(lo, kk, zero)
    k11 = jnp.where(lo, zero, kk)
    k01 = pltpu.roll(k00, HEAD_DIM, axis=1)
    k10 = pltpu.roll(k11, HEAD_DIM, axis=1)
    return [[k00.astype(BF16), k01.astype(BF16)], [k10.astype(BF16), k11.astype(BF16)]]


LOG2E = 1.4426950408889634


def _own_block():
    i = lax.broadcasted_iota(jnp.int32, (WINDOW, WINDOW), 0)
    j = lax.broadcasted_iota(jnp.int32, (WINDOW, WINDOW), 1)
    return j <= i


def _fold(own, a):
    return jnp.where(own, a[:, WINDOW:], a[:, :WINDOW])


def _attn_probs(qp, kvar, own, prev_bias, sk):
    s = _dot_nt(qp, kvar)
    sb = jnp.where(own, s[:, WINDOW:], s[:, :WINDOW] + prev_bias) * (ATTN_SCALE * LOG2E)
    sk2 = sk * LOG2E
    m = jnp.maximum(jnp.max(sb, axis=1, keepdims=True), sk2)
    pe = jnp.exp2(sb - m)
    es = jnp.exp2(sk2 - m)
    den = jnp.sum(pe, axis=1, keepdims=True) + es
    inv = 1.0 / den
    return pe * inv, es * inv


def _unfold(own, a):
    zero = jnp.zeros_like(a)
    return jnp.where(own, zero, a), jnp.where(own, a, zero)


def _sink(sinks, r):
    lane = lax.broadcasted_iota(jnp.int32, sinks.shape, 1)
    return jnp.sum(jnp.where(lane == r, sinks, 0.0), axis=1, keepdims=True)


def _kv_specs():
    return [pl.BlockSpec((WINDOW, KV_W), lambda n: (jnp.maximum(n - 1, 0), OFF_K // KV_W)),
            pl.BlockSpec((WINDOW, KV_W), lambda n: (n, OFF_K // KV_W)),
            pl.BlockSpec((WINDOW, KV_W), lambda n: (jnp.maximum(n - 1, 0), OFF_V // KV_W)),
            pl.BlockSpec((WINDOW, KV_W), lambda n: (n, OFF_V // KV_W))]


def _attn_fwd(proj, sinks, og, ycat, name):
    T = proj.shape[0]
    nb = T // WINDOW

    def body(q_ref, kp_ref, kc_ref, vp_ref, vc_ref, s_ref, og_ref, _, y_ref, o_ref):
        n = pl.program_id(0)
        kv = _kv_variants(jnp.concatenate([kp_ref[...], kc_ref[...]], axis=0))
        vv = _kv_variants(jnp.concatenate([vp_ref[...], vc_ref[...]], axis=0))
        own = _own_block()
        prev_bias = jnp.where(n > 0, 0.0, NEG)
        sinks_v = s_ref[...]
        ssq = jnp.zeros((WINDOW, 1), F32)
        for p in range(N_PAIRS):
            j = p // PAIRS_PER_KV
            qp = q_ref[:, p * 128:(p + 1) * 128].astype(BF16)
            o_pair = jnp.zeros((WINDOW, 128), F32)
            for par in range(2):
                pn, _ = _attn_probs(qp, kv[j][par], own, prev_bias, _sink(sinks_v, 2 * p + par))
                p_prev, p_own = _unfold(own, pn.astype(BF16))
                o_pair = o_pair + _dot_nn(p_prev, vv[j][par][:WINDOW]) + _dot_nn(p_own, vv[j][par][WINDOW:])
            o_ref[:, p * 128:(p + 1) * 128] = o_pair
            ssq = ssq + jnp.sum(o_pair * o_pair, axis=1, keepdims=True)
        rn = lax.rsqrt(ssq * (1.0 / ATTN_W) + EPS)
        y_ref[...] = (o_ref[...] * rn * og_ref[...]).astype(BF16)

    return pl.pallas_call(
        body, grid=(nb,),
        in_specs=[pl.BlockSpec((WINDOW, ATTN_W), lambda n: (n, OFF_Q // ATTN_W)), *_kv_specs(),
                  pl.BlockSpec((1, N_HEADS), lambda n: (0, 0)), pl.BlockSpec((1, ATTN_W), lambda n: (0, 0)), ANY],
        out_specs=[pl.BlockSpec((WINDOW, ATTN_W), lambda n: (n, 1)), pl.BlockSpec((WINDOW, ATTN_W), lambda n: (n, 0))],
        out_shape=[jax.ShapeDtypeStruct(ycat.shape, BF16), jax.ShapeDtypeStruct((T, ATTN_W), F32)],
        input_output_aliases={7: 0}, name=name, compiler_params=_cparams(("parallel",)),
    )(proj, proj, proj, proj, proj, sinks, og, ycat)


def _attn_bwd(proj, sinks, og, o, dy, dproj, name):
    T = proj.shape[0]
    nb = T // WINDOW

    def body(q_ref, kp_ref, kc_ref, vp_ref, vc_ref, s_ref, og_ref, o_ref, dy_ref, _,
             dq_ref, dk_ref, dv_ref, ds_ref, dog_ref):
        n = pl.program_id(0)

        @pl.when(n == 0)
        def _():
            dk_ref[...] = jnp.zeros_like(dk_ref)
            dv_ref[...] = jnp.zeros_like(dv_ref)
            ds_ref[...] = jnp.zeros_like(ds_ref)
            dog_ref[...] = jnp.zeros_like(dog_ref)

        kv = _kv_variants(jnp.concatenate([kp_ref[...], kc_ref[...]], axis=0))
        vv = _kv_variants(jnp.concatenate([vp_ref[...], vc_ref[...]], axis=0))
        own = _own_block()
        prev_bias = jnp.where(n > 0, 0.0, NEG)
        sinks_v = s_ref[...]
        of = o_ref[...]
        rn = lax.rsqrt(jnp.mean(of * of, axis=-1, keepdims=True) + EPS)
        oh = of * rn
        dyf = dy_ref[...]
        dog_ref[...] += jnp.sum(dyf * oh, axis=0, keepdims=True)
        doh = dyf * og_ref[...]
        do = rn * (doh - oh * jnp.mean(doh * oh, axis=-1, keepdims=True))
        lane = lax.broadcasted_iota(jnp.int32, (1, 128), 1)
        lane16 = lax.broadcasted_iota(jnp.int32, (1, N_HEADS), 1)
        sub = lax.broadcasted_iota(jnp.int32, (128, 1), 0)
        dk_acc = [[[jnp.zeros((128, WINDOW), F32) for _ in range(2)] for _ in range(2)] for _ in range(2)]
        dv_acc = [[[jnp.zeros((128, WINDOW), F32) for _ in range(2)] for _ in range(2)] for _ in range(2)]
        dsink = jnp.zeros((1, N_HEADS), F32)
        for p in range(N_PAIRS):
            j = p // PAIRS_PER_KV
            q_f = q_ref[:, p * 128:(p + 1) * 128]
            qp = q_f.astype(BF16)
            q_t = q_f.T.astype(BF16)
            do_p = do[:, p * 128:(p + 1) * 128]
            o_p = of[:, p * 128:(p + 1) * 128]
            do_b = do_p.astype(BF16)
            do_t = do_p.T.astype(BF16)
            prod = do_p * o_p
            dq_pair = jnp.zeros((WINDOW, 128), F32)
            for par in range(2):
                r = 2 * p + par
                half = (lane < HEAD_DIM) if par == 0 else (lane >= HEAD_DIM)
                rows_half = (sub < HEAD_DIM) if par == 0 else (sub >= HEAD_DIM)
                pn, ps = _attn_probs(qp, kv[j][par], own, prev_bias, _sink(sinks_v, r))
                delta = jnp.sum(jnp.where(half, prod, 0.0), axis=1, keepdims=True)
                dP = _fold(own, _dot_nt(do_b, vv[j][par]))
                dS = pn * (dP - delta)
                dsink = dsink + jnp.where(lane16 == r, -jnp.sum(ps * delta, axis=0, keepdims=True), 0.0)
                dS_parts = _unfold(own, (dS * ATTN_SCALE).astype(BF16))
                p_parts = _unfold(own, pn.astype(BF16))
                q_th = jnp.where(rows_half, q_t, jnp.zeros_like(q_t))
                do_th = jnp.where(rows_half, do_t, jnp.zeros_like(do_t))
                for blk in range(2):
                    dq_pair = dq_pair + _dot_nn(dS_parts[blk], kv[j][par][blk * WINDOW:(blk + 1) * WINDOW])
                    dk_acc[j][par][blk] = dk_acc[j][par][blk] + _dot_nn(q_th, dS_parts[blk])
                    dv_acc[j][par][blk] = dv_acc[j][par][blk] + _dot_nn(do_th, p_parts[blk])
            dq_ref[:, p * 128:(p + 1) * 128] = dq_pair.astype(BF16)
        rows = [pl.multiple_of(jnp.maximum(n - 1, 0) * WINDOW, WINDOW), pl.multiple_of(n * WINDOW, WINDOW)]
        for acc, ref in [(dk_acc, dk_ref), (dv_acc, dv_ref)]:
            for blk in range(2):
                both_t = (acc[0][0][blk] + pltpu.roll(acc[0][1][blk], HEAD_DIM, axis=0)
                          + acc[1][1][blk] + pltpu.roll(acc[1][0][blk], HEAD_DIM, axis=0))
                ref[pl.ds(rows[blk], WINDOW), :] += both_t.T
        ds_ref[...] += dsink

    full_kv = pl.BlockSpec((T, KV_W), lambda n: (0, 0))
    blk = pl.BlockSpec((WINDOW, ATTN_W), lambda n: (n, 0))
    return pl.pallas_call(
        body, grid=(nb,),
        in_specs=[pl.BlockSpec((WINDOW, ATTN_W), lambda n: (n, OFF_Q // ATTN_W)), *_kv_specs(),
                  pl.BlockSpec((1, N_HEADS), lambda n: (0, 0)), pl.BlockSpec((1, ATTN_W), lambda n: (0, 0)),
                  blk, pl.BlockSpec((WINDOW, ATTN_W), lambda n: (n, 1)), ANY],
        out_specs=[pl.BlockSpec((WINDOW, ATTN_W), lambda n: (n, OFF_Q // ATTN_W)), full_kv, full_kv,
                   pl.BlockSpec((1, N_HEADS), lambda n: (0, 0)), pl.BlockSpec((1, ATTN_W), lambda n: (0, 0))],
        out_shape=[jax.ShapeDtypeStruct(dproj.shape, BF16), jax.ShapeDtypeStruct((T, KV_W), F32),
                   jax.ShapeDtypeStruct((T, KV_W), F32), jax.ShapeDtypeStruct((1, N_HEADS), F32),
                   jax.ShapeDtypeStruct((1, ATTN_W), F32)],
        input_output_aliases={9: 0}, name=name, compiler_params=_cparams(("arbitrary",)),
    )(proj, proj, proj, proj, proj, sinks, og, o, dy, dproj)


ANY = pl.BlockSpec(memory_space=pl.ANY)


def _coords():
    return lax.axis_index("x"), lax.axis_index("y"), lax.axis_index("c")


def _all_gather(arrs, name):
    n = len(arrs)

    def body(*refs):
        ins, outs = refs[:n], refs[n:2 * n]
        send_sems, recv_sems, local_sems = refs[2 * n:]
        x, y, c = _coords()
        me = 4 * x + 2 * y + c
        sibling = (x, y, 1 - c)
        chips = [(1 - x, y), (x, 1 - y), (1 - x, 1 - y)]

        def copy(a, k, block, to, src=None):
            dst = outs[a].at[block]
            return pltpu.make_async_remote_copy(
                src_ref=dst if src is None else src, dst_ref=dst, send_sem=send_sems.at[a, k],
                recv_sem=recv_sems.at[a, k], device_id=to, device_id_type=MESH)

        mine = [pltpu.make_async_copy(ins[a], outs[a].at[me], local_sems.at[a]) for a in range(n)]
        for cp in mine:
            cp.start()
        first = []
        for a in range(n):
            first.append(copy(a, 0, me, sibling, src=ins[a]))
            for j, chip in enumerate(chips):
                first.append(copy(a, 1 + j, me, (*chip, c), src=ins[a]))
        for cp in first:
            cp.start()
        passed = []
        for j, (px, py) in enumerate(chips):
            blk = 4 * px + 2 * py + c
            for a in range(n):
                copy(a, 1 + j, blk, sibling).wait_recv()
                fwd = copy(a, 4 + j, blk, sibling)
                fwd.start()
                passed.append(fwd)
        for a in range(n):
            copy(a, 0, 4 * x + 2 * y + (1 - c), sibling).wait_recv()
            for j, (px, py) in enumerate(chips):
                copy(a, 4 + j, 4 * px + 2 * py + (1 - c), sibling).wait_recv()
        for cp in first + passed:
            cp.wait_send()
        for cp in mine:
            cp.wait()

    return pl.pallas_call(
        body, in_specs=[ANY] * n, out_specs=[ANY] * n,
        out_shape=[jax.ShapeDtypeStruct((N_DEV,) + a.shape, a.dtype) for a in arrs],
        scratch_shapes=[pltpu.SemaphoreType.DMA((n, 7)), pltpu.SemaphoreType.DMA((n, 7)),
                        pltpu.SemaphoreType.DMA((n,))],
        name=name,
    )(*arrs)


def _exchange_pair(arrs, name):
    n = len(arrs)

    def body(*refs):
        ins, outs = refs[:n], refs[n:2 * n]
        send_sems, recv_sems = refs[2 * n:]
        x, y, c = _coords()
        cps = []
        for a in range(n):
            for q in range(4):
                cps.append(pltpu.make_async_remote_copy(
                    src_ref=ins[a].at[2 * q + (1 - c)], dst_ref=outs[a].at[q], send_sem=send_sems.at[a, q],
                    recv_sem=recv_sems.at[a, q], device_id=(x, y, 1 - c), device_id_type=MESH))
        for cp in cps:
            cp.start()
        for cp in cps:
            cp.wait()

    return pl.pallas_call(
        body, in_specs=[ANY] * n, out_specs=[ANY] * n,
        out_shape=[jax.ShapeDtypeStruct((4,) + a.shape[1:], a.dtype) for a in arrs],
        scratch_shapes=[pltpu.SemaphoreType.DMA((n, 4)), pltpu.SemaphoreType.DMA((n, 4))],
        name=name,
    )(*arrs)


def _exchange_chips(arrs, name):
    n = len(arrs)

    def body(*refs):
        ins, outs = refs[:n], refs[n:2 * n]
        send_sems, recv_sems = refs[2 * n:]
        x, y, c = _coords()
        chips = [(1 - x, y), (x, 1 - y), (1 - x, 1 - y)]
        cps = []
        for a in range(n):
            for k, (tx, ty) in enumerate(chips):
                cps.append(pltpu.make_async_remote_copy(
                    src_ref=ins[a].at[2 * tx + ty], dst_ref=outs[a].at[k], send_sem=send_sems.at[a, k],
                    recv_sem=recv_sems.at[a, k], device_id=(tx, ty, c), device_id_type=MESH))
        for cp in cps:
            cp.start()
        for cp in cps:
            cp.wait()

    return pl.pallas_call(
        body, in_specs=[ANY] * n, out_specs=[ANY] * n,
        out_shape=[jax.ShapeDtypeStruct((3,) + a.shape[1:], a.dtype) for a in arrs],
        scratch_shapes=[pltpu.SemaphoreType.DMA((n, 3)), pltpu.SemaphoreType.DMA((n, 3))],
        name=name,
    )(*arrs)


HBM = pl.BlockSpec(memory_space=pltpu.HBM)
SEM = pl.BlockSpec(memory_space=pltpu.SEMAPHORE)
EFFECT = pltpu.SideEffectType.DATAFLOW_SIDE_EFFECTING


def _in_hbm(a):
    return pltpu.with_memory_space_constraint(a, pltpu.HBM)


def _remote_start(srcs, lands, plan, n_copies, name, after=None):
    ns, nb = len(srcs), len(srcs) + len(lands)
    n_after = 0 if after is None else 1

    def body(*refs):
        src_refs, land_refs = refs[:ns], refs[ns:nb]
        send_sems, recv_sems = refs[nb + n_after], refs[nb + n_after + 1]
        token = refs[-1]
        x, y, c = _coords()
        for i, (sv, dv, dev) in enumerate(plan(src_refs, land_refs, x, y, c)):
            pltpu.make_async_remote_copy(src_ref=sv, dst_ref=dv, send_sem=send_sems.at[i], recv_sem=recv_sems.at[i],
                                         device_id=dev, device_id_type=MESH).start()
        token[...] = jnp.zeros_like(token)

    bufs = list(srcs) + list(lands)
    outs = pl.pallas_call(
        body, name=name,
        out_shape=(pltpu.SemaphoreType.DMA((n_copies,)), pltpu.SemaphoreType.DMA((n_copies,)),
                   *[pltpu.HBM(b.shape, b.dtype) for b in bufs], jax.ShapeDtypeStruct((8, 128), F32)),
        in_specs=[HBM] * nb + [ANY] * n_after,
        out_specs=(SEM, SEM, *[HBM] * nb, pl.BlockSpec(memory_space=pltpu.VMEM)),
        input_output_aliases={i: 2 + i for i in range(nb)},
        compiler_params=pltpu.CompilerParams(has_side_effects=EFFECT),
    )(*[_in_hbm(b) for b in bufs], *([] if after is None else [after]))
    return outs[0], outs[1], list(outs[2:2 + ns]), list(outs[2 + ns:2 + nb]), outs[-1]


def _remote_wait(started, after, plan, name):
    send_sems, recv_sems, srcs, lands, _ = started
    ns, nb = len(srcs), len(srcs) + len(lands)

    def body(*refs):
        src_refs, land_refs = refs[:ns], refs[ns:nb]
        send_sems, recv_sems = refs[nb], refs[nb + 1]
        x, y, c = _coords()
        for i, (sv, dv, dev) in enumerate(plan(src_refs, land_refs, x, y, c)):
            cp = pltpu.make_async_remote_copy(src_ref=sv, dst_ref=dv, send_sem=send_sems.at[i],
                                              recv_sem=recv_sems.at[i], device_id=dev, device_id_type=MESH)
            cp.wait_send()
            cp.wait_recv()

    bufs = list(srcs) + list(lands)
    outs = pl.pallas_call(
        body, name=name, out_shape=tuple(pltpu.HBM(b.shape, b.dtype) for b in bufs),
        in_specs=[HBM] * nb + [SEM, SEM, ANY], out_specs=tuple([HBM] * nb),
        input_output_aliases={i: i for i in range(nb)},
        compiler_params=pltpu.CompilerParams(has_side_effects=EFFECT),
    )(*bufs, send_sems, recv_sems, after)
    return list(outs[:ns]), list(outs[ns:])


def _gather_plan(src_refs, land_refs, x, y, c):
    me = 4 * x + 2 * y + c
    plan = []
    for s, l in zip(src_refs, land_refs):
        for dev in [(x, y, 1 - c), (1 - x, y, c), (x, 1 - y, c), (1 - x, 1 - y, c)]:
            plan.append((s, l.at[me], dev))
    return plan


def _forward_plan(src_refs, land_refs, x, y, c):
    plan = []
    for l in land_refs:
        for px, py in [(1 - x, y), (x, 1 - y), (1 - x, 1 - y)]:
            blk = l.at[4 * px + 2 * py + c]
            plan.append((blk, blk, (x, y, 1 - c)))
    return plan


def _pair_plan(src_refs, land_refs, x, y, c):
    plan = []
    for s, l in zip(src_refs, land_refs):
        for q in range(4):
            plan.append((s.at[2 * q + (1 - c)], l.at[q], (x, y, 1 - c)))
    return plan


def _pair4_plan(src_refs, land_refs, x, y, c):
    plan = []
    for s, l in zip(src_refs, land_refs):
        for q in range(4):
            plan.append((s.at[q], l.at[q], (x, y, 1 - c)))
    return plan


def _chips_plan(src_refs, land_refs, x, y, c):
    plan = []
    for s, l in zip(src_refs, land_refs):
        for k, (tx, ty) in enumerate([(1 - x, y), (x, 1 - y), (1 - x, 1 - y)]):
            plan.append((s.at[2 * tx + ty], l.at[k], (tx, ty, c)))
    return plan


def _everyone_plan(src_refs, land_refs, x, y, c):
    me = 4 * x + 2 * y + c
    plan = []
    for s, l in zip(src_refs, land_refs):
        for fx, fy, fc in [(0, 0, 1), (1, 0, 0), (1, 0, 1), (0, 1, 0), (0, 1, 1), (1, 1, 0), (1, 1, 1)]:
            dev = ((1 - x) if fx else x, (1 - y) if fy else y, (1 - c) if fc else c)
            plan.append((s, l.at[me], dev))
    return plan


def _gather_finish(gathered, name):
    n = len(gathered)

    def body(*refs):
        outs = refs[n:2 * n]
        send_sems, recv_sems = refs[2 * n:]
        x, y, c = _coords()
        cps = []
        for a in range(n):
            for j, (px, py) in enumerate([(1 - x, y), (x, 1 - y), (1 - x, 1 - y)]):
                blk = outs[a].at[4 * px + 2 * py + c]
                got = outs[a].at[4 * px + 2 * py + (1 - c)]
                cps.append((pltpu.make_async_remote_copy(
                    src_ref=blk, dst_ref=blk, send_sem=send_sems.at[a, j], recv_sem=recv_sems.at[a, j],
                    device_id=(x, y, 1 - c), device_id_type=MESH), pltpu.make_async_remote_copy(
                    src_ref=got, dst_ref=got, send_sem=send_sems.at[a, j], recv_sem=recv_sems.at[a, j],
                    device_id=(x, y, 1 - c), device_id_type=MESH)))
        for cp, _ in cps:
            cp.start()
        for cp, arrival in cps:
            cp.wait_send()
            arrival.wait_recv()

    return pl.pallas_call(
        body, in_specs=[ANY] * n, out_specs=[ANY] * n,
        out_shape=[jax.ShapeDtypeStruct(g.shape, g.dtype) for g in gathered],
        input_output_aliases={a: a for a in range(n)},
        scratch_shapes=[pltpu.SemaphoreType.DMA((n, 3)), pltpu.SemaphoreType.DMA((n, 3))],
        name=name,
    )(*gathered)


def _pair_add(g8, r1, csel, tr, name):
    _, R, C = r1.shape
    g4 = g8.reshape(4, 2, R, C)

    def body(c_ref, g_ref, r_ref, o_ref):
        o_ref[...] = (g_ref[...].astype(F32) + r_ref[...].astype(F32)).astype(BF16)

    return pl.pallas_call(
        body,
        grid_spec=pltpu.PrefetchScalarGridSpec(
            num_scalar_prefetch=1, grid=(4, R // tr),
            in_specs=[pl.BlockSpec((None, None, tr, C), lambda q, i, cs: (q, cs[0], i, 0)),
                      pl.BlockSpec((None, tr, C), lambda q, i, cs: (q, i, 0))],
            out_specs=pl.BlockSpec((None, tr, C), lambda q, i, cs: (q, i, 0))),
        out_shape=jax.ShapeDtypeStruct((4, R, C), BF16), name=name,
        compiler_params=_cparams(("parallel", "parallel")),
    )(csel, g4, r1)


def _adamw_math(w, g, m, v):
    m = ADAM_B1 * m + (1.0 - ADAM_B1) * g
    v = ADAM_B2 * v + (1.0 - ADAM_B2) * (g * g)
    m_hat = m / (1.0 - ADAM_B1 ** ADAM_STEP)
    v_hat = v / (1.0 - ADAM_B2 ** ADAM_STEP)
    delta = -ADAM_LR * (m_hat / (jnp.sqrt(v_hat) + ADAM_EPS) + ADAM_WD * w)
    return delta, m, v


def _adamw_big(w, m, v, p4, r3, qsel, tile, name):
    R, C = w.shape
    tr, tc = tile

    def body(q_ref, w_ref, m_ref, v_ref, p_ref, r_ref, g_out, d_out, m_out, v_out):
        g = p_ref[...].astype(F32) + r_ref[0].astype(F32) + r_ref[1].astype(F32) + r_ref[2].astype(F32)
        d, mn, vn = _adamw_math(w_ref[...], g, m_ref[...], v_ref[...])
        g_out[...] = g
        d_out[...] = d
        m_out[...] = mn
        v_out[...] = vn

    blk = pl.BlockSpec((tr, tc), lambda i, j, qs: (i, j))
    return pl.pallas_call(
        body,
        grid_spec=pltpu.PrefetchScalarGridSpec(
            num_scalar_prefetch=1, grid=(R // tr, C // tc),
            in_specs=[blk, blk, blk, pl.BlockSpec((None, tr, tc), lambda i, j, qs: (qs[0], i, j)),
                      pl.BlockSpec((3, tr, tc), lambda i, j, qs: (0, i, j))],
            out_specs=[blk, blk, blk, blk]),
        out_shape=[jax.ShapeDtypeStruct((R, C), F32)] * 4, name=name,
        compiler_params=_cparams(("parallel", "parallel")),
    )(qsel, w, m, v, p4, r3)


def _sum_partials(p4, r3, qsel, tc, name):
    _, R, C = p4.shape

    def body(q_ref, p_ref, r_ref, o_ref):
        o_ref[...] = p_ref[...].astype(F32) + r_ref[0].astype(F32) + r_ref[1].astype(F32) + r_ref[2].astype(F32)

    return pl.pallas_call(
        body,
        grid_spec=pltpu.PrefetchScalarGridSpec(
            num_scalar_prefetch=1, grid=(C // tc,),
            in_specs=[pl.BlockSpec((None, R, tc), lambda j, qs: (qs[0], 0, j)),
                      pl.BlockSpec((3, R, tc), lambda j, qs: (0, 0, j))],
            out_specs=pl.BlockSpec((R, tc), lambda j, qs: (0, j))),
        out_shape=jax.ShapeDtypeStruct((R, C), F32), name=name, compiler_params=_cparams(("parallel",)),
    )(qsel, p4, r3)


def _adamw_tiled(w, g, m, v, tc, name):
    R, C = w.shape

    def body(w_ref, g_ref, m_ref, v_ref, d_out, m_out, v_out):
        d, mn, vn = _adamw_math(w_ref[...], g_ref[...], m_ref[...], v_ref[...])
        d_out[...] = d
        m_out[...] = mn
        v_out[...] = vn

    blk = pl.BlockSpec((R, tc), lambda j: (0, j))
    return pl.pallas_call(
        body, grid=(C // tc,), in_specs=[blk] * 4, out_specs=[blk] * 3,
        out_shape=[jax.ShapeDtypeStruct((R, C), F32)] * 3, name=name, compiler_params=_cparams(("parallel",)),
    )(w, g, m, v)


def _small_sum(parts, name):
    def body(p_ref, o_ref):
        acc = p_ref[0]
        for d in range(1, N_DEV):
            acc = acc + p_ref[d]
        o_ref[...] = acc

    return pl.pallas_call(
        body, out_shape=jax.ShapeDtypeStruct(parts.shape[1:], F32), name=name,
        compiler_params=_cparams(),
    )(parts)


def _adamw_small(w, g, m, v, name):
    def body(w_ref, g_ref, m_ref, v_ref, d_out, m_out, v_out):
        d, mn, vn = _adamw_math(w_ref[...], g_ref[...], m_ref[...], v_ref[...])
        d_out[...] = d
        m_out[...] = mn
        v_out[...] = vn

    return pl.pallas_call(
        body, out_shape=[jax.ShapeDtypeStruct(w.shape, F32)] * 3, name=name, compiler_params=_cparams(),
    )(w, g, m, v)


def _row(*pieces):
    r = jnp.concatenate([p.reshape(1, -1) for p in pieces], axis=1)
    return jnp.pad(r, ((0, 0), (0, D_MODEL - r.shape[1])))


def _pack_small(mix, convb, ssmg, attng, mlpg, fing, convw, dtb, alog, dsk, sinks, extra=None):
    last = [dtb, alog, dsk, sinks] + ([extra] if extra is not None else [])
    rows = [_row(mix), _row(convb), _row(ssmg, attng), _row(mlpg), _row(fing),
            jnp.pad(convw, ((0, 0), (0, D_MODEL - convw.shape[1]))), _row(*last)]
    packed = jnp.concatenate(rows, axis=0)
    return jnp.pad(packed, ((0, SMALL_ROWS - packed.shape[0]), (0, 0)))


def _unpack_small(p, conv_n):
    return dict(
        mix_norm_g=p[0:1, :], conv_b=p[1:2, :], ssm_norm_g=p[2:3, :D_INNER], attn_out_norm_g=p[2:3, D_INNER:],
        mlp_norm_g=p[3:4, :], final_norm_g=p[4, :], conv_w=p[5:9, :conv_n][None],
        dt_bias=p[9:10, 0:16], A_log=p[9:10, 16:32], D_skip=p[9:10, 32:48], attn_sinks=p[9:10, 48:64])


SMALL_NAMES = ["mix_norm_g", "conv_w", "conv_b", "dt_bias", "A_log", "D_skip", "ssm_norm_g", "attn_sinks",
               "attn_out_norm_g", "mlp_norm_g", "final_norm_g"]
WEIGHT_ORDER = ["mix_norm_g", "w_in", "conv_w", "conv_b", "dt_bias", "A_log", "D_skip", "ssm_norm_g", "attn_sinks",
                "attn_out_norm_g", "w_out", "mlp_norm_g", "w_up", "w_down", "final_norm_g"]


def _to_my_columns(w_nat):
    pad = jnp.zeros((w_nat.shape[0], NP - IN_PROJ), w_nat.dtype)
    return jnp.concatenate([w_nat[:, :NAT_DT], w_nat[:, NAT_DT + N_HEADS:], w_nat[:, NAT_DT:NAT_DT + N_HEADS], pad],
                           axis=1)


PER = IN_PROJ // N_DEV
SUPER_STEP = 544
SUPER = 576


def _natural_rows(g, lo, hi):
    segments = [(0, NAT_DT, 0), (NAT_DT, NAT_DT + N_HEADS, OFF_DT - NAT_DT), (NAT_DT + N_HEADS, IN_PROJ, -N_HEADS),
                (IN_PROJ, NP, 0)]
    pieces = [g[max(lo, a) + shift:min(hi, b) + shift] for a, b, shift in segments if max(lo, a) < min(hi, b)]
    return pieces[0] if len(pieces) == 1 else jnp.concatenate(pieces, axis=0)


def _w_in_from_super_slabs(sup):
    seam = SUPER - SUPER_STEP
    units = []
    for i in range(N_DEV):
        base = SUPER_STEP * i
        units.append((base, base + seam, sup[i, :seam] if i == 0 else sup[i - 1, SUPER_STEP:] + sup[i, :seam]))
        units.append((base + seam, base + SUPER_STEP, sup[i, seam:SUPER_STEP]))
    units.append((SUPER_STEP * N_DEV, SUPER_STEP * N_DEV + seam, sup[N_DEV - 1, SUPER_STEP:]))

    def natural(lo, hi):
        return [rows[max(lo, a) - a:min(hi, b) - a] for a, b, rows in units if max(lo, a) < min(hi, b)]

    pieces = natural(0, NAT_DT) + natural(NAT_DT + N_HEADS, IN_PROJ) + natural(NAT_DT, NAT_DT + N_HEADS)
    return jnp.concatenate(pieces + [jnp.zeros((NP - IN_PROJ, D_MODEL), sup.dtype)], axis=0)


def _to_natural_columns(w_my):
    return jnp.concatenate([w_my[:, :NAT_DT], w_my[:, OFF_DT:OFF_DT + N_HEADS], w_my[:, NAT_DT:OFF_DT]], axis=1)


SLAB = 1024


def _grad_w_up(h2, du, name, sel=None, add=None, after=None):
    T, D = h2.shape
    if sel is None:
        pick, n_slab, pre = (lambda j, *cs: j), N_DEV, None
    else:
        pre, other = sel
        pick, n_slab = (lambda j, cs: 2 * j + ((1 - cs[0]) if other else cs[0])), 4
    o_spec = pl.BlockSpec((None, SLAB, SLAB), lambda i, j, k, *cs: (j, i, 0))
    return _matmul(
        h2, du, mode="tn", grid=(D // SLAB, n_slab, 1),
        a_spec=pl.BlockSpec((T, SLAB), lambda i, j, k, *cs: (0, i)),
        b_spec=pl.BlockSpec((T, SLAB), lambda i, j, k, *cs: (0, pick(j, *cs))),
        out_shapes=[jax.ShapeDtypeStruct((n_slab, D, SLAB), BF16)], out_specs=[o_spec], tile=(SLAB, SLAB), name=name,
        extras=() if add is None else (add,), extra_specs=() if add is None else (o_spec,),
        epilogue=None if add is None else (lambda acc, r: (acc + r.astype(F32),)), after=after, prefetch=pre)[0]


def _grad_w_down(act, dx3b, name, sel=None, add=None, after=None):
    T, D = dx3b.shape
    if sel is None:
        pick, n_slab, pre = (lambda i, *cs: i), N_DEV, None
    else:
        pre, other = sel
        pick, n_slab = (lambda i, cs: 2 * i + ((1 - cs[0]) if other else cs[0])), 4
    o_spec = pl.BlockSpec((None, SLAB, SLAB), lambda i, j, k, *cs: (i, 0, j))
    return _matmul(
        act, dx3b, mode="tn", grid=(n_slab, D // SLAB, 1),
        a_spec=pl.BlockSpec((T, SLAB), lambda i, j, k, *cs: (0, pick(i, *cs))),
        b_spec=pl.BlockSpec((T, SLAB), lambda i, j, k, *cs: (0, j)),
        out_shapes=[jax.ShapeDtypeStruct((n_slab, SLAB, D), BF16)], out_specs=[o_spec], tile=(SLAB, SLAB), name=name,
        extras=() if add is None else (add,), extra_specs=() if add is None else (o_spec,),
        epilogue=None if add is None else (lambda acc, r: (acc + r.astype(F32),)), after=after, prefetch=pre)[0]


class _FixedWeights:
    def __init__(self, w_in_p, w_out_f, w_up_s, w_down_f, conv_w_f):
        self.w = (w_in_p, w_out_f, w_up_s, w_down_f, conv_w_f)
        self.grads = {}

    def mixer_weights(self, after):
        return self.w[0], self.w[4], None

    def prefetch(self, k, after):
        return None

    def out_weight(self, after):
        return self.w[1]

    def up_weight(self, after):
        return self.w[2]

    def down_weight(self, after):
        return self.w[3]

    def mlp_grads(self, h2, du, act, dx3b):
        self.grads.update(w_up=_grad_w_up(h2, du, "grad_w_up"),
                          w_down=_grad_w_down(act, dx3b, "grad_w_down").reshape(D_FF, D_MODEL))
        return None

    def out_grad(self, g_out):
        self.grads.update(w_out=g_out)
        return None

    def in_grad(self, g_in):
        self.grads.update(w_in=g_in)
        return None


def _local_step(x, tgt, p, hooks):
    T = x.shape[0]
    D = D_MODEL
    h1 = _rmsnorm_fwd(x, p["mix_norm_g"], "norm_mix")
    w_in_t, conv_w_f, token = hooks.mixer_weights(h1)
    (proj,) = _mm_simple(h1, w_in_t, mode="nt", M=T, N=NP, K=D, tm=min(T, 1024), tn=1536, tk=D, out_dtype=F32,
                         name="in_proj", after=token)
    xbc = _conv_fwd(proj, conv_w_f, p["conv_b"], "conv_fwd")
    dtT = proj[:, OFF_DT:OFF_DT + N_HEADS].T
    dtbT = p["dt_bias"].T
    alogT = p["A_log"].T
    dfull = jnp.repeat(p["D_skip"], HEAD_DIM, axis=1)
    token = hooks.prefetch("out", xbc)
    ssm_g = p["ssm_norm_g"] if token is None else p["ssm_norm_g"] + token[0:1, 0:1]
    ycat, ypre, hs = _ssd_fwd(xbc, proj, dtT, p["dt_bias"], dtbT, p["A_log"], alogT, dfull, ssm_g, "ssd_fwd")
    ycat, o_att = _attn_fwd(proj, p["attn_sinks"], p["attn_out_norm_g"], ycat, "attn_fwd")
    token = hooks.prefetch("up", ycat)
    w_out_f = hooks.out_weight(ycat if token is None else token)
    tm = min(T, 1024)
    (x2,) = _mm_simple(ycat, w_out_f, mode="nn", M=T, N=D, K=D, tm=tm, tn=1024, tk=D, out_dtype=F32, name="out_proj",
                       extras=(x,), epilogue=lambda acc, res: (acc + res,))
    h2 = _rmsnorm_fwd(x2, p["mlp_norm_g"], "norm_mlp")
    w_up_s = hooks.up_weight(h2)
    grid = (T // tm, N_DEV, 1)
    u, act = _matmul(
        h2, w_up_s, mode="nn", grid=grid,
        a_spec=pl.BlockSpec((tm, D), lambda i, j, k: (i, 0)),
        b_spec=pl.BlockSpec((None, D, 1024), lambda i, j, k: (j, 0, 0)),
        out_shapes=[jax.ShapeDtypeStruct((T, D_FF), F32), jax.ShapeDtypeStruct((T, D_FF), BF16)],
        out_specs=[pl.BlockSpec((tm, 1024), lambda i, j, k: (i, j))] * 2, tile=(tm, 1024), name="mlp_up",
        epilogue=lambda acc: (acc, jnp.square(jnp.maximum(acc, 0.0))))
    w_down_f = hooks.down_weight(act)
    (x3,) = _mm_simple(act, w_down_f, mode="nn", M=T, N=D, K=D_FF, tm=tm, tn=1024, tk=2048, out_dtype=F32,
                       name="mlp_down", extras=(x2,), epilogue=lambda acc, res: (acc + res,))
    loss_part, d_fin, dx3, dx3b = _final_loss(x3, tgt, p["final_norm_g"].reshape(1, D), "loss_head")
    (du,) = _mm_simple(dx3b, w_down_f, mode="nt", M=T, N=D_FF, K=D, tm=tm, tn=1024, tk=D, out_dtype=BF16,
                       name="mlp_down_bwd", extras=(u,),
                       epilogue=lambda acc, uu: (acc * (2.0 * jnp.maximum(uu, 0.0)),))
    token = hooks.mlp_grads(h2, du, act, dx3b)
    (dh2,) = _matmul(
        du, w_up_s, mode="nt", grid=(T // tm, D // 1024, N_DEV // 2),
        a_spec=pl.BlockSpec((tm, 2048), lambda i, j, k: (i, k)),
        b_spec=pl.BlockSpec((2, 1024, 1024), lambda i, j, k: (k, j, 0)),
        out_shapes=[jax.ShapeDtypeStruct((T, D), F32)],
        out_specs=[pl.BlockSpec((tm, 1024), lambda i, j, k: (i, j))], tile=(tm, 1024), name="mlp_up_bwd",
        after=token, dot_fn=lambda a, b: _dot_nt(a[:, :1024], b[0]) + _dot_nt(a[:, 1024:], b[1]))
    dx2, dx2b, d_mlp = _rmsnorm_bwd(dh2, x2, p["mlp_norm_g"], dx3, "norm_mlp_bwd")
    (g_out,) = _mm_simple(ycat, dx2b, mode="tn", M=D, N=D, K=T, tm=1024, tn=1024, tk=T, out_dtype=BF16,
                          name="grad_w_out")
    token = hooks.out_grad(g_out)
    (dy,) = _mm_simple(dx2b, w_out_f, mode="nt", M=T, N=D, K=D, tm=tm, tn=1024, tk=D, out_dtype=F32,
                       name="out_proj_bwd", after=token)
    dproj, dxbc_act, d_dtb, d_alog, d_dskip, d_ssmg = _ssd_bwd(
        xbc, proj, dtT, p["dt_bias"], dtbT, p["A_log"], alogT, dfull, p["ssm_norm_g"], ypre, hs, dy, "ssd_bwd")
    dproj, d_convw, d_convb = _conv_bwd(proj, dxbc_act, conv_w_f, p["conv_b"], dproj, "conv_bwd")
    dproj, dk, dv, d_sinks, d_attng = _attn_bwd(proj, p["attn_sinks"], p["attn_out_norm_g"], o_att, dy, dproj,
                                                "attn_bwd")
    dproj = lax.dynamic_update_slice(dproj, jnp.concatenate([dk, dv], axis=1).astype(BF16), (0, OFF_K))
    (g_in,) = _mm_simple(dproj, h1, mode="tn", M=NP, N=D, K=T, tm=1536, tn=1024, tk=T, out_dtype=BF16,
                         name="grad_w_in")
    token = hooks.in_grad(g_in)
    (dh1,) = _mm_simple(dproj, w_in_t, mode="nn", M=T, N=D, K=NP, tm=tm, tn=1024, tk=2304, out_dtype=F32,
                        name="in_proj_bwd", after=token)
    dx, _, d_mix = _rmsnorm_bwd(dh1, x, p["mix_norm_g"], dx2, "norm_mix_bwd")
    small = _pack_small(d_mix, d_convb, d_ssmg, d_attng, d_mlp, d_fin, d_convw, d_dtb, d_alog, d_dskip, d_sinks,
                        extra=loss_part[:, 0:1])
    return dx, small


def _landing(own, me):
    zone = lax.empty((N_DEV,) + own.shape, own.dtype)
    return lax.dynamic_update_slice(zone, own[None], (me,) + (0,) * own.ndim)


def _sequencer_gather(owns, split, me, collective_id, name):
    n = len(owns)
    zone_refs = [jax.new_ref(_landing(o, me), memory_space=pltpu.MemorySpace.HBM) for o in owns]
    own_refs = [jax.new_ref(o, memory_space=pltpu.MemorySpace.HBM) for o in owns]
    N_COPIES = 9

    @pl.kernel(mesh=plsc.ScalarSubcoreMesh(axis_name="sequencer", num_cores=1), name=name,
               scratch_types=(pltpu.SemaphoreType.DMA((n, N_COPIES)), pltpu.SemaphoreType.DMA((n, N_COPIES))),
               compiler_params=pltpu.CompilerParams(collective_id=collective_id))
    def launch(send_sems, recv_sems):
        x, y, c = _coords()
        sibling, xn, yn, diag = (x, y, 1 - c), (1 - x, y, c), (x, 1 - y, c), (1 - x, 1 - y, c)
        barrier = pltpu.get_barrier_semaphore()
        for peer in [sibling, xn, yn, diag]:
            pl.semaphore_signal(barrier, inc=1, device_id=peer, device_id_type=MESH)
        pl.semaphore_wait(barrier, 4)

        def block(a, dev, half=None):
            ref = zone_refs[a].at[4 * dev[0] + 2 * dev[1] + dev[2]]
            if half is None:
                return ref
            rows = owns[a].shape[0] // 2
            return ref.at[pl.ds(half * rows, rows)]

        def copy(a, k, src, dst, to):
            return pltpu.make_async_remote_copy(src_ref=src, dst_ref=dst, send_sem=send_sems.at[a, k],
                                                recv_sem=recv_sems.at[a, k], device_id=to, device_id_type=MESH)

        me_dev = (x, y, c)
        sent = []
        first = {}
        for a in range(n):
            for k, peer in enumerate([sibling, xn, yn] + ([] if split[a] else [diag])):
                first[a, k] = copy(a, k, own_refs[a], block(a, me_dev), peer)
                first[a, k].start()
                sent.append(first[a, k])
        from_sibling = []
        for a in range(n):
            first[a, 1].wait_recv()
            sent.append(copy(a, 4, block(a, xn), block(a, xn), sibling))
            if split[a]:
                sent.append(copy(a, 6, block(a, xn, 0), block(a, xn, 0), yn))
            first[a, 2].wait_recv()
            sent.append(copy(a, 5, block(a, yn), block(a, yn), sibling))
            if split[a]:
                sent.append(copy(a, 7, block(a, yn, 1), block(a, yn, 1), xn))
            for cp in sent[-(4 if split[a] else 2):]:
                cp.start()
        for a in range(n):
            if split[a]:
                copy(a, 6, block(a, diag, 0), block(a, diag, 0), yn).wait_recv()
                sent.append(copy(a, 8, block(a, diag, 0), block(a, diag, 0), sibling))
                sent[-1].start()
                copy(a, 7, block(a, diag, 1), block(a, diag, 1), xn).wait_recv()
                sent.append(copy(a, 3, block(a, diag, 1), block(a, diag, 1), sibling))
                sent[-1].start()
            else:
                first[a, 3].wait_recv()
                sent.append(copy(a, 8, block(a, diag), block(a, diag), sibling))
                sent[-1].start()
        for a in range(n):
            first[a, 0].wait_recv()
            copy(a, 4, block(a, xn), block(a, xn), sibling).wait_recv()
            copy(a, 5, block(a, yn), block(a, yn), sibling).wait_recv()
            if split[a]:
                copy(a, 8, block(a, diag, 0), block(a, diag, 0), sibling).wait_recv()
                copy(a, 3, block(a, diag, 1), block(a, diag, 1), sibling).wait_recv()
            else:
                copy(a, 8, block(a, diag), block(a, diag), sibling).wait_recv()
        for cp in sent:
            cp.wait_send()

    launch()
    return zone_refs


def _sequencer_exchange(srcs, lands, plan, peers, n_copies, collective_id, name):
    src_refs = [jax.new_ref(s, memory_space=pltpu.MemorySpace.HBM) for s in srcs]
    land_refs = [jax.empty_ref(l, memory_space=pltpu.MemorySpace.HBM) if isinstance(l, jax.ShapeDtypeStruct)
                 else jax.new_ref(l, memory_space=pltpu.MemorySpace.HBM) for l in lands]

    @pl.kernel(mesh=plsc.ScalarSubcoreMesh(axis_name="sequencer", num_cores=1), name=name,
               scratch_types=(pltpu.SemaphoreType.DMA((n_copies,)), pltpu.SemaphoreType.DMA((n_copies,))),
               compiler_params=pltpu.CompilerParams(collective_id=collective_id))
    def launch(send_sems, recv_sems):
        x, y, c = _coords()
        who = peers(x, y, c)
        barrier = pltpu.get_barrier_semaphore()
        for peer in who:
            pl.semaphore_signal(barrier, inc=1, device_id=peer, device_id_type=MESH)
        pl.semaphore_wait(barrier, len(who))
        cps = [pltpu.make_async_remote_copy(src_ref=sv, dst_ref=dv, send_sem=send_sems.at[i], recv_sem=recv_sems.at[i],
                                            device_id=dev, device_id_type=MESH)
               for i, (sv, dv, dev) in enumerate(plan(src_refs, land_refs, x, y, c))]
        for cp in cps:
            cp.start()
        for cp in cps:
            cp.wait()

    launch()
    return src_refs, land_refs


def _sibling_only(x, y, c):
    return [(x, y, 1 - c)]


def _other_chips(x, y, c):
    return [(1 - x, y, c), (x, 1 - y, c), (1 - x, 1 - y, c)]


def _everyone_else(x, y, c):
    return [((1 - x) if fx else x, (1 - y) if fy else y, (1 - c) if fc else c)
            for fx, fy, fc in [(0, 0, 1), (1, 0, 0), (1, 0, 1), (0, 1, 0), (0, 1, 1), (1, 1, 0), (1, 1, 1)]]


def _gather_end(started, after, plan, name):
    _, lands = _remote_wait(started, after, plan, name + "_wait")
    return _gather_finish(lands, name + "_finish")


class _ShardedWeights:
    def __init__(self, w_in, w_out, conv_w, w_up, w_down, me, csel):
        self.me, self.csel = me, csel
        own_rows = lax.dynamic_update_slice(jnp.zeros((SUPER, D_MODEL), F32), jnp.transpose(w_in), (2 * me, 0))
        self.in_ref, self.conv_ref = _sequencer_gather([own_rows.astype(BF16), conv_w], [True, False], me, 7,
                                                       "gather_w_in_sequencer")
        (self.out_ref,) = _sequencer_gather([w_out.astype(BF16)], [True], me, 8, "gather_w_out_sequencer")
        (self.up_ref,) = _sequencer_gather([w_up.astype(BF16)], [True], me, 9, "gather_w_up_sequencer")
        (self.down_ref,) = _sequencer_gather([w_down.astype(BF16)], [True], me, 10, "gather_w_down_sequencer")
        self.reduces = {}

    def mixer_weights(self, after):
        g_conv = self.conv_ref[...]
        conv_w_f = jnp.concatenate([g_conv[i] for i in range(N_DEV)], axis=1)
        return _w_in_from_super_slabs(self.in_ref[...]), conv_w_f, None

    def prefetch(self, k, after):
        return None

    def out_weight(self, after):
        return self.out_ref[...].reshape(D_MODEL, D_MODEL)

    def up_weight(self, after):
        return self.up_ref[...]

    def down_weight(self, after):
        return self.down_ref[...].reshape(D_FF, D_MODEL)

    CHIPS_ID = {"up": 13, "down": 14, "out": 15, "in": 16}

    def _to_chips(self, sums, tag):
        land = jax.ShapeDtypeStruct((3,) + sums.shape[1:], sums.dtype)
        (sums_ref,), (got,) = _sequencer_exchange([sums], [land], _chips_plan, _other_chips, 3, self.CHIPS_ID[tag],
                                                  f"reduce_chips_{tag}_sequencer")
        self.reduces[tag] = (sums_ref, got)

    def _chips_start(self, slabs, from_sibling, rows, tag):
        (s,), (r,), (tr,) = slabs, from_sibling, rows
        self._to_chips(_pair_add(s, r, self.csel, tr, f"pair_add_{tag}"), tag)
        return None

    def mlp_grads(self, h2, du, act, dx3b):
        def to_sibling(part, tag, cid):
            _, (got,) = _sequencer_exchange([part], [jax.ShapeDtypeStruct(part.shape, part.dtype)], _pair4_plan,
                                            _sibling_only, 4, cid, f"reduce_pair_{tag}_sequencer")
            return got

        up_send = _grad_w_up(h2, du, "grad_w_up_send", sel=(self.csel, True))
        from_up = to_sibling(up_send, "up", 11)
        down_send = _grad_w_down(act, dx3b, "grad_w_down_send", sel=(self.csel, True))
        from_down = to_sibling(down_send, "down", 12)
        up_sum = _grad_w_up(h2, du, "grad_w_up_keep", sel=(self.csel, False), add=from_up[...])
        self._to_chips(up_sum, "up")
        down_sum = _grad_w_down(act, dx3b, "grad_w_down_keep", sel=(self.csel, False), add=from_down[...])
        self._to_chips(down_sum, "down")
        return None

    def out_grad(self, g_out):
        slabs = [g_out.reshape(N_DEV, D_MODEL // N_DEV, D_MODEL)]
        return self._chips_start(slabs, _exchange_pair(slabs, "reduce_pair_out"), [256], "out")

    def in_grad(self, g_in):
        slabs = [jnp.stack([_natural_rows(g_in, SUPER_STEP * j, SUPER_STEP * j + SUPER) for j in range(N_DEV)])]
        return self._chips_start(slabs, _exchange_pair(slabs, "reduce_pair_in"), [SUPER], "in")

    def small_start(self, small):
        _, (self.small_ref,) = _sequencer_exchange([small], [_landing(small, self.me)], _everyone_plan,
                                                   _everyone_else, N_DEV - 1, 17, "gather_small_sequencer")

    def small_end(self, after):
        return self.small_ref[...]

    def reduce_end(self, tag, after):
        sums_ref, got = self.reduces[tag]
        return [sums_ref[...]], [got[...]]


def kernel(x, mix_norm_g, w_in, conv_w, conv_b, dt_bias, A_log, D_skip, ssm_norm_g, attn_sinks, attn_out_norm_g, w_out, mlp_norm_g, w_up, w_down, final_norm_g, loss_target, m_mix_norm_g, m_w_in, m_conv_w, m_conv_b, m_dt_bias, m_A_log, m_D_skip, m_ssm_norm_g, m_attn_sinks, m_attn_out_norm_g, m_w_out, m_mlp_norm_g, m_w_up, m_w_down, m_final_norm_g, v_mix_norm_g, v_w_in, v_conv_w, v_conv_b, v_dt_bias, v_A_log, v_D_skip, v_ssm_norm_g, v_attn_sinks, v_attn_out_norm_g, v_w_out, v_mlp_norm_g, v_w_up, v_w_down, v_final_norm_g):
    xi, yi, ci = _coords()
    me = 4 * xi + 2 * yi + ci
    csel = jnp.reshape(ci, (1,)).astype(jnp.int32)
    qsel = jnp.reshape(2 * xi + yi, (1,)).astype(jnp.int32)
    w = dict(mix_norm_g=mix_norm_g, conv_b=conv_b, dt_bias=dt_bias, A_log=A_log, D_skip=D_skip,
             ssm_norm_g=ssm_norm_g, attn_sinks=attn_sinks, attn_out_norm_g=attn_out_norm_g, mlp_norm_g=mlp_norm_g,
             final_norm_g=final_norm_g)
    hooks = _ShardedWeights(w_in[0], w_out[0], conv_w[0], w_up[0], w_down[0], me, csel)
    p = dict(w)
    dx, small = _local_step(x[0], loss_target[0], p, hooks)
    hooks.small_start(small)
    big = {}
    after = dx
    for name, wt, mt, vt, tile in [
            ("up", w_up, m_w_up, v_w_up, (512, SLAB)), ("down", w_down, m_w_down, v_w_down, (256, D_MODEL)),
            ("out", w_out, m_w_out, v_w_out, (256, D_MODEL))]:
        (chip_sums,), (from_chips,) = hooks.reduce_end(name, after)
        res = _adamw_big(wt[0], mt[0], vt[0], chip_sums, from_chips, qsel, tile, f"adamw_w_{name}")
        big["w_" + name] = tuple(r[None] for r in res)
        after = res[0]
    (chip_sums,), (from_chips,) = hooks.reduce_end("in", after)
    g_super = _sum_partials(chip_sums, from_chips, qsel, 512, "grad_w_in_sum")
    g_in = lax.dynamic_slice(g_super, (2 * me, 0), (PER, D_MODEL))
    res = _adamw_tiled(jnp.transpose(w_in[0]), g_in, jnp.transpose(m_w_in[0]), jnp.transpose(v_w_in[0]), 512,
                       "adamw_w_in")
    big["w_in"] = tuple(jnp.transpose(r)[None] for r in (g_in, *res))
    after = res[0]
    gsum = _small_sum(hooks.small_end(after), "small_sum")
    loss = gsum[9, 64]
    gs = _unpack_small(gsum, CONV_DIM)
    cw = CONV_DIM // N_DEV
    g_conv_shard = lax.dynamic_slice(gsum[5:9, :], (0, me * cw), (CONV_K, cw))

    def pack(s):
        return _pack_small(s["mix_norm_g"], s["conv_b"], s["ssm_norm_g"], s["attn_out_norm_g"], s["mlp_norm_g"],
                           s["final_norm_g"], s["conv_w"][0], s["dt_bias"], s["A_log"], s["D_skip"], s["attn_sinks"])

    wp = pack(dict(w, conv_w=conv_w))
    mp = pack(dict(mix_norm_g=m_mix_norm_g, conv_b=m_conv_b, ssm_norm_g=m_ssm_norm_g,
                   attn_out_norm_g=m_attn_out_norm_g, mlp_norm_g=m_mlp_norm_g, final_norm_g=m_final_norm_g,
                   conv_w=m_conv_w, dt_bias=m_dt_bias, A_log=m_A_log, D_skip=m_D_skip, attn_sinks=m_attn_sinks))
    vp = pack(dict(mix_norm_g=v_mix_norm_g, conv_b=v_conv_b, ssm_norm_g=v_ssm_norm_g,
                   attn_out_norm_g=v_attn_out_norm_g, mlp_norm_g=v_mlp_norm_g, final_norm_g=v_final_norm_g,
                   conv_w=v_conv_w, dt_bias=v_dt_bias, A_log=v_A_log, D_skip=v_D_skip, attn_sinks=v_attn_sinks))
    gp = jnp.concatenate([gsum[0:5], jnp.pad(g_conv_shard, ((0, 0), (0, D_MODEL - cw))), gsum[9:10],
                          jnp.zeros((SMALL_ROWS - 10, D_MODEL), F32)], axis=0)
    dp, mnp, vnp = _adamw_small(wp, gp, mp, vp, "adamw_small")
    grads = dict(gs, conv_w=g_conv_shard[None])
    deltas = _unpack_small(dp, cw)
    new_m = _unpack_small(mnp, cw)
    new_v = _unpack_small(vnp, cw)
    for k, name in enumerate(["w_in", "w_out", "w_up", "w_down"]):
        grads[name], deltas[name], new_m[name], new_v[name] = big[name]
    return (loss, dx[None], *[grads[n] for n in WEIGHT_ORDER], *[deltas[n] for n in WEIGHT_ORDER],
            *[new_m[n] for n in WEIGHT_ORDER], *[new_v[n] for n in WEIGHT_ORDER])
```

```python
import functools

import jax
import jax.numpy as jnp
from jax import lax
from jax.experimental import pallas as pl
from jax.experimental.pallas import tpu as pltpu
from jax.experimental.pallas import tpu_sc as plsc

F32 = jnp.float32
BF16 = jnp.bfloat16
HI = lax.Precision.HIGHEST
MESH = pl.DeviceIdType.MESH

EPS = 1e-5
D_MODEL = 2048
D_INNER = 1024
N_HEADS = 16
HEAD_DIM = 64
N_GROUPS = 4
D_STATE = 128
CHUNK = 128
CONV_K = 4
CONV_DIM = 2048
ATTN_W = 1024
KV_W = 128
WINDOW = 128
D_FF = 8192
IN_PROJ = 4368
N_DEV = 8
NP = 4608
OFF_Z, OFF_X, OFF_B, OFF_C, OFF_Q, OFF_K, OFF_V, OFF_DT = 0, 1024, 2048, 2560, 3072, 4096, 4224, 4352
NAT_DT = 3072

ADAM_LR = 0.001
ADAM_B1 = 0.9
ADAM_B2 = 0.999
ADAM_EPS = 1e-08
ADAM_WD = 0.01
ADAM_STEP = 10

VMEM_LIMIT = 52 * 1024 * 1024
SMALL_ROWS = 16
NEG = -1e30


def _cparams(sem=None):
    return pltpu.CompilerParams(dimension_semantics=sem, vmem_limit_bytes=VMEM_LIMIT)


def _split3(v):
    hi = v.astype(BF16)
    rest = v - hi.astype(F32)
    mid = rest.astype(BF16)
    return hi, mid, (rest - mid.astype(F32)).astype(BF16)


def _hdot(a, b, data):
    if data == "a":
        sel = b.astype(BF16)
        return sum(_dot_nn(part, sel) for part in _split3(a))
    sel = a.astype(BF16)
    return sum(_dot_nn(sel, part) for part in _split3(b))


def _dot_nn(a, b):
    return lax.dot_general(a, b, (((1,), (0,)), ((), ())), preferred_element_type=F32)


def _dot_nt(a, b):
    return lax.dot_general(a, b, (((1,), (1,)), ((), ())), preferred_element_type=F32)


def _dot_tn(a, b):
    return lax.dot_general(a, b, (((0,), (0,)), ((), ())), preferred_element_type=F32)


def _softplus(v):
    return jnp.maximum(v, 0.0) + jnp.log1p(jnp.exp(-jnp.abs(v)))


def _sigmoid(v):
    return 1.0 / (1.0 + jnp.exp(-v))


def _matmul(a, b, *, mode, grid, a_spec, b_spec, out_shapes, out_specs, tile, name,
            extras=(), extra_specs=(), epilogue=None, after=None, dot_fn=None, prefetch=None):
    nk = grid[2]
    n_ex = len(extras)
    n_out = len(out_shapes)
    dot = dot_fn if dot_fn is not None else {"nn": _dot_nn, "nt": _dot_nt, "tn": _dot_tn}[mode]

    def finish(acc, ex_refs, out_refs):
        res = (acc,) if epilogue is None else epilogue(acc, *[e[...] for e in ex_refs])
        for o, r in zip(out_refs, res):
            o[...] = r.astype(o.dtype)

    def body(*refs):
        a_ref, b_ref = refs[0], refs[1]
        ex_refs = refs[2:2 + n_ex]
        out_refs = refs[2 + n_ex:2 + n_ex + n_out]
        part = dot(a_ref[...].astype(BF16), b_ref[...].astype(BF16))
        if nk == 1:
            finish(part, ex_refs, out_refs)
        else:
            acc_ref = refs[-1]
            k = pl.program_id(2)

            @pl.when(k == 0)
            def _():
                acc_ref[...] = part

            @pl.when(k > 0)
            def _():
                acc_ref[...] += part

            @pl.when(k == nk - 1)
            def _():
                finish(acc_ref[...], ex_refs, out_refs)

    scratch = [] if nk == 1 else [pltpu.VMEM(tile, F32)]
    n_pre = 0 if prefetch is None else 1
    tok_specs = [] if after is None else [pl.BlockSpec((8, 128), lambda *_: (0, 0))]
    tok_args = [] if after is None else [after]

    def body_with_token(*refs):
        refs = refs[n_pre:]
        body(*refs[:2 + n_ex], *refs[2 + n_ex + len(tok_args):])

    in_specs = [a_spec, b_spec, *extra_specs, *tok_specs]
    params = _cparams(("parallel", "parallel", "arbitrary"))
    if prefetch is None:
        return pl.pallas_call(
            body_with_token, grid=grid, in_specs=in_specs, out_specs=list(out_specs), out_shape=list(out_shapes),
            scratch_shapes=scratch, name=name, compiler_params=params)(a, b, *extras, *tok_args)
    return pl.pallas_call(
        body_with_token,
        grid_spec=pltpu.PrefetchScalarGridSpec(num_scalar_prefetch=1, grid=grid, in_specs=in_specs,
                                               out_specs=list(out_specs), scratch_shapes=scratch),
        out_shape=list(out_shapes), name=name, compiler_params=params)(prefetch, a, b, *extras, *tok_args)


def _mm_simple(a, b, *, mode, M, N, K, tm, tn, tk, out_dtype, name, extras=(), epilogue=None, n_out=1,
               out_dtypes=None, after=None):
    grid = (M // tm, N // tn, K // tk)
    if mode == "nn":
        a_spec = pl.BlockSpec((tm, tk), lambda i, j, k: (i, k))
        b_spec = pl.BlockSpec((tk, tn), lambda i, j, k: (k, j))
    elif mode == "nt":
        a_spec = pl.BlockSpec((tm, tk), lambda i, j, k: (i, k))
        b_spec = pl.BlockSpec((tn, tk), lambda i, j, k: (j, k))
    else:
        a_spec = pl.BlockSpec((tk, tm), lambda i, j, k: (k, i))
        b_spec = pl.BlockSpec((tk, tn), lambda i, j, k: (k, j))
    o_spec = pl.BlockSpec((tm, tn), lambda i, j, k: (i, j))
    dts = out_dtypes if out_dtypes is not None else [out_dtype] * n_out
    return _matmul(a, b, mode=mode, grid=grid, a_spec=a_spec, b_spec=b_spec,
                   out_shapes=[jax.ShapeDtypeStruct((M, N), d) for d in dts],
                   out_specs=[o_spec] * len(dts), tile=(tm, tn), name=name,
                   extras=extras, extra_specs=[o_spec] * len(extras), epilogue=epilogue, after=after)


ROW_BLOCK = 256


def _rmsnorm_fwd(x, g, name):
    T, D = x.shape

    def body(x_ref, g_ref, o_ref):
        xf = x_ref[...]
        r = lax.rsqrt(jnp.mean(xf * xf, axis=-1, keepdims=True) + EPS)
        o_ref[...] = (xf * r * g_ref[...]).astype(BF16)

    return pl.pallas_call(
        body, grid=(T // ROW_BLOCK,),
        in_specs=[pl.BlockSpec((ROW_BLOCK, D), lambda i: (i, 0)), pl.BlockSpec((1, D), lambda i: (0, 0))],
        out_specs=pl.BlockSpec((ROW_BLOCK, D), lambda i: (i, 0)),
        out_shape=jax.ShapeDtypeStruct((T, D), BF16), name=name, compiler_params=_cparams(("parallel",)),
    )(x, g)


def _rmsnorm_bwd(dh, x, g, dres, name):
    T, D = x.shape

    def body(dh_ref, x_ref, g_ref, dres_ref, dx_ref, dxb_ref, dg_ref):
        i = pl.program_id(0)
        xf = x_ref[...]
        r = lax.rsqrt(jnp.mean(xf * xf, axis=-1, keepdims=True) + EPS)
        xh = xf * r
        d = dh_ref[...]

        @pl.when(i == 0)
        def _():
            dg_ref[...] = jnp.zeros_like(dg_ref)

        dg_ref[...] += jnp.sum(d * xh, axis=0, keepdims=True)
        dxh = d * g_ref[...]
        dx = r * (dxh - xh * jnp.mean(dxh * xh, axis=-1, keepdims=True)) + dres_ref[...]
        dx_ref[...] = dx
        dxb_ref[...] = dx.astype(BF16)

    row = pl.BlockSpec((ROW_BLOCK, D), lambda i: (i, 0))
    vec = pl.BlockSpec((1, D), lambda i: (0, 0))
    return pl.pallas_call(
        body, grid=(T // ROW_BLOCK,), in_specs=[row, row, vec, row], out_specs=[row, row, vec],
        out_shape=[jax.ShapeDtypeStruct((T, D), F32), jax.ShapeDtypeStruct((T, D), BF16),
                   jax.ShapeDtypeStruct((1, D), F32)],
        name=name, compiler_params=_cparams(("arbitrary",)),
    )(dh, x, g, dres)


def _final_loss(x3, tgt, g, name):
    T, D = x3.shape

    def body(x_ref, t_ref, g_ref, loss_ref, dg_ref, dx_ref, dxb_ref):
        i = pl.program_id(0)
        xf = x_ref[...]
        r = lax.rsqrt(jnp.mean(xf * xf, axis=-1, keepdims=True) + EPS)
        xh = xf * r
        gg = g_ref[...]
        err = xh * gg - t_ref[...]

        @pl.when(i == 0)
        def _():
            dg_ref[...] = jnp.zeros_like(dg_ref)
            loss_ref[...] = jnp.zeros_like(loss_ref)

        part = jnp.sum(jnp.sum(err * err, axis=-1, keepdims=True), axis=0, keepdims=True) * (0.5 / D)
        loss_ref[...] += jnp.broadcast_to(part, loss_ref.shape)
        dout = err * (1.0 / D)
        dg_ref[...] += jnp.sum(dout * xh, axis=0, keepdims=True)
        dxh = dout * gg
        dx = r * (dxh - xh * jnp.mean(dxh * xh, axis=-1, keepdims=True))
        dx_ref[...] = dx
        dxb_ref[...] = dx.astype(BF16)

    row = pl.BlockSpec((ROW_BLOCK, D), lambda i: (i, 0))
    vec = pl.BlockSpec((1, D), lambda i: (0, 0))
    return pl.pallas_call(
        body, grid=(T // ROW_BLOCK,), in_specs=[row, row, vec],
        out_specs=[pl.BlockSpec((1, 128), lambda i: (0, 0)), vec, row, row],
        out_shape=[jax.ShapeDtypeStruct((1, 128), F32), jax.ShapeDtypeStruct((1, D), F32),
                   jax.ShapeDtypeStruct((T, D), F32), jax.ShapeDtypeStruct((T, D), BF16)],
        name=name, compiler_params=_cparams(("arbitrary",)),
    )(x3, tgt, g)


CONV_BLOCK = 256


def _conv_apply(u, w, b):
    row = lax.broadcasted_iota(jnp.int32, u.shape, 0)
    acc = b + w[CONV_K - 1:CONV_K, :] * u
    shifted = []
    for j in range(1, CONV_K):
        uj = jnp.where(row >= j, pltpu.roll(u, j, axis=0), 0.0)
        shifted.append(uj)
        acc = acc + w[CONV_K - 1 - j:CONV_K - j, :] * uj
    return acc, shifted


def _conv_fwd(proj, conv_w, conv_b, name):
    T = proj.shape[0]
    cb0 = OFF_X // CONV_BLOCK

    def body(u_ref, w_ref, b_ref, o_ref):
        c, _ = _conv_apply(u_ref[...], w_ref[...], b_ref[...])
        o_ref[...] = c * _sigmoid(c)

    return pl.pallas_call(
        body, grid=(CONV_DIM // CONV_BLOCK,),
        in_specs=[pl.BlockSpec((T, CONV_BLOCK), lambda j: (0, cb0 + j)),
                  pl.BlockSpec((CONV_K, CONV_BLOCK), lambda j: (0, j)),
                  pl.BlockSpec((1, CONV_BLOCK), lambda j: (0, j))],
        out_specs=pl.BlockSpec((T, CONV_BLOCK), lambda j: (0, j)),
        out_shape=jax.ShapeDtypeStruct((T, CONV_DIM), F32), name=name, compiler_params=_cparams(("parallel",)),
    )(proj, conv_w, conv_b)


def _conv_bwd(proj, dact, conv_w, conv_b, dproj, name):
    T = proj.shape[0]
    cb0 = OFF_X // CONV_BLOCK

    def body(u_ref, d_ref, w_ref, b_ref, _, du_ref, dw_ref, db_ref):
        u = u_ref[...]
        w = w_ref[...]
        c, shifted = _conv_apply(u, w, b_ref[...])
        sg = _sigmoid(c)
        dc = d_ref[...] * sg * (1.0 + c * (1.0 - sg))
        row = lax.broadcasted_iota(jnp.int32, u.shape, 0)
        du = w[CONV_K - 1:CONV_K, :] * dc
        dw_ref[CONV_K - 1:CONV_K, :] = jnp.sum(dc * u, axis=0, keepdims=True)
        for j in range(1, CONV_K):
            dcj = jnp.where(row < T - j, pltpu.roll(dc, T - j, axis=0), 0.0)
            du = du + w[CONV_K - 1 - j:CONV_K - j, :] * dcj
            dw_ref[CONV_K - 1 - j:CONV_K - j, :] = jnp.sum(dc * shifted[j - 1], axis=0, keepdims=True)
        db_ref[...] = jnp.sum(dc, axis=0, keepdims=True)
        du_ref[...] = du.astype(BF16)

    return pl.pallas_call(
        body, grid=(CONV_DIM // CONV_BLOCK,),
        in_specs=[pl.BlockSpec((T, CONV_BLOCK), lambda j: (0, cb0 + j)),
                  pl.BlockSpec((T, CONV_BLOCK), lambda j: (0, j)),
                  pl.BlockSpec((CONV_K, CONV_BLOCK), lambda j: (0, j)),
                  pl.BlockSpec((1, CONV_BLOCK), lambda j: (0, j)), pl.BlockSpec(memory_space=pl.ANY)],
        out_specs=[pl.BlockSpec((T, CONV_BLOCK), lambda j: (0, cb0 + j)),
                   pl.BlockSpec((CONV_K, CONV_BLOCK), lambda j: (0, j)),
                   pl.BlockSpec((1, CONV_BLOCK), lambda j: (0, j))],
        out_shape=[jax.ShapeDtypeStruct(dproj.shape, BF16), jax.ShapeDtypeStruct((CONV_K, CONV_DIM), F32),
                   jax.ShapeDtypeStruct((1, CONV_DIM), F32)],
        input_output_aliases={4: 0}, name=name, compiler_params=_cparams(("parallel",)),
    )(proj, dact, conv_w, conv_b, dproj)


GROUP_W = D_INNER // N_GROUPS
HEADS_PER_GROUP = N_HEADS // N_GROUPS


def _expand_mat():
    h = lax.broadcasted_iota(jnp.int32, (N_HEADS, D_INNER), 0)
    j = lax.broadcasted_iota(jnp.int32, (N_HEADS, D_INNER), 1)
    return (j // HEAD_DIM == h).astype(F32)


def _reduce_mat(g):
    j = lax.broadcasted_iota(jnp.int32, (GROUP_W, N_HEADS), 0)
    h = lax.broadcasted_iota(jnp.int32, (GROUP_W, N_HEADS), 1)
    return (g * HEADS_PER_GROUP + j // HEAD_DIM == h).astype(F32)


def _col16(v, h):
    lane = lax.broadcasted_iota(jnp.int32, v.shape, 1)
    return jnp.sum(jnp.where(lane == h, v, 0.0), axis=1, keepdims=True)


def _ssd_pre(dt_raw, dtT_raw, dtb, dtbT, alog, alogT):
    Q = CHUNK
    xdt = dt_raw + dtb
    dt = _softplus(xdt)
    dtT = _softplus(dtT_raw + dtbT)
    A = -jnp.exp(alog)
    AT = -jnp.exp(alogT)
    row = lax.broadcasted_iota(jnp.int32, (Q, Q), 0)
    col = lax.broadcasted_iota(jnp.int32, (Q, Q), 1)
    tril = (row >= col).astype(F32)
    triu = (row <= col).astype(F32)
    cs = _hdot(tril, dt * A, "b")
    csT = _hdot(dtT * AT, triu, "a")
    return xdt, dt, A, cs, csT, row >= col, triu


def _decay_matrix(cs, csT, h, causal):
    seg = _col16(cs, h) - csT[h:h + 1, :]
    return jnp.where(causal, jnp.exp(jnp.minimum(seg, 0.0)), 0.0)


def _ssd_in_specs(nc, rev):
    def cidx(c):
        return (nc - 1 - c) if rev else c

    return [
        pl.BlockSpec((CHUNK, D_INNER), lambda c: (cidx(c), 0)),
        pl.BlockSpec((CHUNK, 512), lambda c: (cidx(c), 2)),
        pl.BlockSpec((CHUNK, 512), lambda c: (cidx(c), 3)),
        pl.BlockSpec((CHUNK, D_INNER), lambda c: (cidx(c), 0)),
        pl.BlockSpec((CHUNK, 128), lambda c: (cidx(c), OFF_DT // 128)),
        pl.BlockSpec((N_HEADS, CHUNK), lambda c: (0, cidx(c))),
        pl.BlockSpec((1, N_HEADS), lambda c: (0, 0)),
        pl.BlockSpec((N_HEADS, 1), lambda c: (0, 0)),
        pl.BlockSpec((1, N_HEADS), lambda c: (0, 0)),
        pl.BlockSpec((N_HEADS, 1), lambda c: (0, 0)),
        pl.BlockSpec((1, D_INNER), lambda c: (0, 0)),
        pl.BlockSpec((1, D_INNER), lambda c: (0, 0)),
    ]


def _ssd_fwd(xbc, proj, dtT, dtb, dtbT, alog, alogT, dfull, ng, name):
    T = xbc.shape[0]
    nc = T // CHUNK
    Q = CHUNK

    def body(xs_ref, B_ref, C_ref, z_ref, dt_ref, dtT_ref, dtb_ref, dtbT_ref, al_ref, alT_ref, df_ref, ng_ref,
             y_ref, ypre_ref, hs_ref, h_scr):
        c = pl.program_id(0)

        @pl.when(c == 0)
        def _():
            h_scr[...] = jnp.zeros_like(h_scr)

        _, dt, _, cs, csT, causal, _ = _ssd_pre(dt_ref[:, :N_HEADS], dtT_ref[...], dtb_ref[...], dtbT_ref[...],
                                                al_ref[...], alT_ref[...])
        ex = _expand_mat()
        dt_full = _hdot(dt, ex, "a")
        cs_full = _hdot(cs, ex, "a")
        cs_last = cs_full[Q - 1:Q, :]
        xs = xs_ref[...]
        xd = xs * dt_full
        e_full = jnp.exp(cs_full)
        dec_full = jnp.exp(cs_last - cs_full)
        cd_full = jnp.exp(cs_last)
        lane_head = lax.broadcasted_iota(jnp.int32, (1, GROUP_W), 1) // HEAD_DIM
        for g in range(N_GROUPS):
            sl = slice(g * GROUP_W, (g + 1) * GROUP_W)
            Bg = B_ref[:, g * D_STATE:(g + 1) * D_STATE].astype(BF16)
            Cg = C_ref[:, g * D_STATE:(g + 1) * D_STATE].astype(BF16)
            CB = _dot_nt(Cg, Bg)
            hg = h_scr[g]
            yoff = _dot_nn(Cg, hg.astype(BF16)) * e_full[:, sl]
            xd_g = xd[:, sl]
            S = _dot_tn(Bg, (xd_g * dec_full[:, sl]).astype(BF16))
            xd_b = xd_g.astype(BF16)
            ydiag = jnp.zeros((Q, GROUP_W), F32)
            for r in range(HEADS_PER_GROUP):
                Lm = _decay_matrix(cs, csT, g * HEADS_PER_GROUP + r, causal)
                Gm = (CB * Lm).astype(BF16)
                ydiag = ydiag + _dot_nn(Gm, jnp.where(lane_head == r, xd_b, jnp.zeros_like(xd_b)))
            hs_ref[0, g] = hg
            h_scr[g] = hg * cd_full[:, sl] + S
            ypre = ydiag + yoff + xs[:, sl] * df_ref[:, sl]
            ypre_ref[:, sl] = ypre
            zg = z_ref[:, sl]
            yz = ypre * zg * _sigmoid(zg)
            rn = lax.rsqrt(jnp.mean(yz * yz, axis=-1, keepdims=True) + EPS)
            y_ref[:, sl] = (yz * rn * ng_ref[:, sl]).astype(BF16)

    return pl.pallas_call(
        body, grid=(nc,), in_specs=_ssd_in_specs(nc, False),
        out_specs=[pl.BlockSpec((CHUNK, D_INNER), lambda c: (c, 0)),
                   pl.BlockSpec((CHUNK, D_INNER), lambda c: (c, 0)),
                   pl.BlockSpec((1, N_GROUPS, D_STATE, GROUP_W), lambda c: (c, 0, 0, 0))],
        out_shape=[jax.ShapeDtypeStruct((T, D_INNER + ATTN_W), BF16), jax.ShapeDtypeStruct((T, D_INNER), F32),
                   jax.ShapeDtypeStruct((nc, N_GROUPS, D_STATE, GROUP_W), F32)],
        scratch_shapes=[pltpu.VMEM((N_GROUPS, D_STATE, GROUP_W), F32)],
        name=name, compiler_params=_cparams(("arbitrary",)),
    )(xbc, xbc, xbc, proj, proj, dtT, dtb, dtbT, alog, alogT, dfull, ng)


def _ssd_bwd(xbc, proj, dtT, dtb, dtbT, alog, alogT, dfull, ng, ypre, hs, dy, name):
    T = xbc.shape[0]
    nc = T // CHUNK
    Q = CHUNK

    def body(xs_ref, B_ref, C_ref, z_ref, dt_ref, dtT_ref, dtb_ref, dtbT_ref, al_ref, alT_ref, df_ref, ng_ref,
             ypre_ref, hs_ref, dy_ref,
             dz_ref, dxbc_ref, ddtb_ref, dal_ref, dD_ref, dng_ref, dh_scr):
        step = pl.program_id(0)

        @pl.when(step == 0)
        def _():
            dh_scr[...] = jnp.zeros_like(dh_scr)
            ddtb_ref[...] = jnp.zeros_like(ddtb_ref)
            dal_ref[...] = jnp.zeros_like(dal_ref)
            dD_ref[...] = jnp.zeros_like(dD_ref)
            dng_ref[...] = jnp.zeros_like(dng_ref)

        xdt, dt, A, cs, csT, causal, triu = _ssd_pre(dt_ref[:, :N_HEADS], dtT_ref[...], dtb_ref[...],
                                                    dtbT_ref[...], al_ref[...], alT_ref[...])
        ex = _expand_mat()
        dt_full = _hdot(dt, ex, "a")
        cs_full = _hdot(cs, ex, "a")
        cs_last = cs_full[Q - 1:Q, :]
        xs = xs_ref[...]
        xd = xs * dt_full
        e_full = jnp.exp(cs_full)
        dec_full = jnp.exp(cs_last - cs_full)
        cd_full = jnp.exp(cs_last)
        lane_head = lax.broadcasted_iota(jnp.int32, (1, GROUP_W), 1) // HEAD_DIM
        is_last = lax.broadcasted_iota(jnp.int32, (Q, 1), 0) == Q - 1
        dcs16 = jnp.zeros((Q, N_HEADS), F32)
        ddtx16 = jnp.zeros((Q, N_HEADS), F32)
        dD16 = jnp.zeros((8, N_HEADS), F32)
        lane16 = lax.broadcasted_iota(jnp.int32, (1, N_HEADS), 1)
        sub16 = lax.broadcasted_iota(jnp.int32, (N_HEADS, 1), 0)
        col_sums = jnp.zeros((N_HEADS, Q), F32)
        for g in range(N_GROUPS):
            sl = slice(g * GROUP_W, (g + 1) * GROUP_W)
            red = _reduce_mat(g)
            ypre_g = ypre_ref[:, sl]
            zg = z_ref[:, sl]
            sg = _sigmoid(zg)
            silu = zg * sg
            yz = ypre_g * silu
            rn = lax.rsqrt(jnp.mean(yz * yz, axis=-1, keepdims=True) + EPS)
            yh = yz * rn
            dy_g = dy_ref[:, sl]
            dng_ref[:, sl] += jnp.sum(dy_g * yh, axis=0, keepdims=True)
            dyh = dy_g * ng_ref[:, sl]
            dyz = rn * (dyh - yh * jnp.mean(dyh * yh, axis=-1, keepdims=True))
            dY = dyz * silu
            dz_ref[:, sl] = (dyz * ypre_g * sg * (1.0 + zg * (1.0 - sg))).astype(BF16)
            xs_g = xs[:, sl]
            xd_g = xd[:, sl]
            dec_g = dec_full[:, sl]
            cd_g = cd_full[:, sl]
            d_g = df_ref[:, sl]
            Bg = B_ref[:, g * D_STATE:(g + 1) * D_STATE].astype(BF16)
            Cg = C_ref[:, g * D_STATE:(g + 1) * D_STATE].astype(BF16)
            CB = _dot_nt(Cg, Bg)
            hg = hs_ref[0, g]
            hgb = hg.astype(BF16)
            yoff = _dot_nn(Cg, hgb) * e_full[:, sl]
            dhn = dh_scr[g]
            dhnb = dhn.astype(BF16)
            dYE = (dY * e_full[:, sl]).astype(BF16)
            dC = _dot_nt(dYE, hgb)
            dh_direct = _dot_tn(Cg, dYE)
            dXdd = _dot_nn(Bg, dhnb)
            dB = _dot_nt((xd_g * dec_g).astype(BF16), dhnb)
            dcd = jnp.sum(dhn * hg, axis=0, keepdims=True)
            dh_scr[g] = dh_direct + cd_g * dhn
            dYb = dY.astype(BF16)
            xd_b = xd_g.astype(BF16)
            dCB = jnp.zeros((Q, Q), F32)
            dXd = dXdd * dec_g
            for r in range(HEADS_PER_GROUP):
                h = g * HEADS_PER_GROUP + r
                Lm = _decay_matrix(cs, csT, h, causal)
                Gf = CB * Lm
                dYr = jnp.where(lane_head == r, dYb, jnp.zeros_like(dYb))
                dG = _dot_nt(dYr, xd_b)
                dCB = dCB + dG * Lm
                dXd = dXd + _dot_tn(Gf.astype(BF16), dYr)
                Mm = dG * Gf
                dcs16 = dcs16 + jnp.where(lane16 == h, jnp.sum(Mm, axis=1, keepdims=True), 0.0)
                col_sums = col_sums + jnp.where(sub16 == h, jnp.sum(Mm, axis=0, keepdims=True), 0.0)
            dCBb = dCB.astype(BF16)
            dC = dC + _dot_nn(dCBb, Bg)
            dB = dB + _dot_tn(dCBb, Cg)
            w_state = dXdd * dec_g * xd_g
            t_last = jnp.sum(w_state, axis=0, keepdims=True) + dcd * cd_g
            dcs_g = dY * yoff - w_state + jnp.where(is_last, t_last, 0.0)
            dcs16 = dcs16 + _hdot(dcs_g, red, "a")
            ddtx16 = ddtx16 + _hdot(dXd * xs_g, red, "a")
            dD16 = dD16 + _hdot(jnp.broadcast_to(jnp.sum(dY * xs_g, axis=0, keepdims=True), (8, GROUP_W)), red, "a")
            dxbc_ref[:, sl] = dXd * dt_full[:, sl] + dY * d_g
            dxbc_ref[:, D_INNER + g * D_STATE:D_INNER + (g + 1) * D_STATE] = dB
            dxbc_ref[:, D_INNER + 512 + g * D_STATE:D_INNER + 512 + (g + 1) * D_STATE] = dC
        eye = (lax.broadcasted_iota(jnp.int32, (N_HEADS, N_HEADS), 0)
               == lax.broadcasted_iota(jnp.int32, (N_HEADS, N_HEADS), 1)).astype(BF16)
        dcs16 = dcs16 - sum(_dot_tn(part, eye) for part in _split3(col_sums))
        da = _hdot(triu, dcs16, "b")
        ddt = da * A + ddtx16
        ddt_raw = ddt * _sigmoid(xdt)
        pr = lax.broadcasted_iota(jnp.int32, (N_HEADS, 128), 0)
        pc = lax.broadcasted_iota(jnp.int32, (N_HEADS, 128), 1)
        dz_ref[:, D_INNER:OFF_DT] = jnp.zeros((Q, OFF_DT - D_INNER), BF16)
        dz_ref[:, OFF_DT:OFF_DT + 128] = _hdot(ddt_raw, (pr == pc).astype(F32), "a").astype(BF16)
        dz_ref[:, OFF_DT + 128:] = jnp.zeros((Q, NP - OFF_DT - 128), BF16)
        ddtb_ref[...] += jnp.sum(ddt_raw, axis=0, keepdims=True)
        dal_ref[...] += jnp.sum(da * dt, axis=0, keepdims=True) * A
        dD_ref[...] += dD16[0:1, :]

    def rc(c):
        return nc - 1 - c

    in_specs = _ssd_in_specs(nc, True) + [
        pl.BlockSpec((CHUNK, D_INNER), lambda c: (rc(c), 0)),
        pl.BlockSpec((1, N_GROUPS, D_STATE, GROUP_W), lambda c: (rc(c), 0, 0, 0)),
        pl.BlockSpec((CHUNK, D_INNER), lambda c: (rc(c), 0)),
    ]
    small = pl.BlockSpec((1, N_HEADS), lambda c: (0, 0))
    return pl.pallas_call(
        body, grid=(nc,), in_specs=in_specs,
        out_specs=[pl.BlockSpec((CHUNK, NP), lambda c: (rc(c), 0)),
                   pl.BlockSpec((CHUNK, CONV_DIM), lambda c: (rc(c), 0)),
                   small, small, small,
                   pl.BlockSpec((1, D_INNER), lambda c: (0, 0))],
        out_shape=[jax.ShapeDtypeStruct((T, NP), BF16), jax.ShapeDtypeStruct((T, CONV_DIM), F32),
                   jax.ShapeDtypeStruct((1, N_HEADS), F32), jax.ShapeDtypeStruct((1, N_HEADS), F32),
                   jax.ShapeDtypeStruct((1, N_HEADS), F32), jax.ShapeDtypeStruct((1, D_INNER), F32)],
        scratch_shapes=[pltpu.VMEM((N_GROUPS, D_STATE, GROUP_W), F32)],
        name=name, compiler_params=_cparams(("arbitrary",)),
    )(xbc, xbc, xbc, proj, proj, dtT, dtb, dtbT, alog, alogT, dfull, ng, ypre, hs, dy)


N_PAIRS = ATTN_W // 128
PAIRS_PER_KV = N_PAIRS // 2
ATTN_SCALE = HEAD_DIM ** -0.5


def _kv_variants(kk):
    lo = lax.broadcasted_iota(jnp.int32, kk.shape, 1) < HEAD_DIM
    zero = jnp.zeros_like(kk)
    k00 = jnp.where(lo, kk, zero)
    k11 = jnp.where(lo, zero, kk)
    k01 = pltpu.roll(k00, HEAD_DIM, axis=1)
    k10 = pltpu.roll(k11, HEAD_DIM, axis=1)
    return [[k00.astype(BF16), k01.astype(BF16)], [k10.astype(BF16), k11.astype(BF16)]]


LOG2E = 1.4426950408889634


def _own_block():
    i = lax.broadcasted_iota(jnp.int32, (WINDOW, WINDOW), 0)
    j = lax.broadcasted_iota(jnp.int32, (WINDOW, WINDOW), 1)
    return j <= i


def _fold(own, a):
    return jnp.where(own, a[:, WINDOW:], a[:, :WINDOW])


def _attn_probs(qp, kvar, own, prev_bias, sk):
    s = _dot_nt(qp, kvar)
    sb = jnp.where(own, s[:, WINDOW:], s[:, :WINDOW] + prev_bias) * (ATTN_SCALE * LOG2E)
    sk2 = sk * LOG2E
    m = jnp.maximum(jnp.max(sb, axis=1, keepdims=True), sk2)
    pe = jnp.exp2(sb - m)
    es = jnp.exp2(sk2 - m)
    den = jnp.sum(pe, axis=1, keepdims=True) + es
    inv = 1.0 / den
    return pe * inv, es * inv


def _unfold(own, a):
    zero = jnp.zeros_like(a)
    return jnp.where(own, zero, a), jnp.where(own, a, zero)


def _sink(sinks, r):
    lane = lax.broadcasted_iota(jnp.int32, sinks.shape, 1)
    return jnp.sum(jnp.where(lane == r, sinks, 0.0), axis=1, keepdims=True)


def _kv_specs():
    return [pl.BlockSpec((WINDOW, KV_W), lambda n: (jnp.maximum(n - 1, 0), OFF_K // KV_W)),
            pl.BlockSpec((WINDOW, KV_W), lambda n: (n, OFF_K // KV_W)),
            pl.BlockSpec((WINDOW, KV_W), lambda n: (jnp.maximum(n - 1, 0), OFF_V // KV_W)),
            pl.BlockSpec((WINDOW, KV_W), lambda n: (n, OFF_V // KV_W))]


def _attn_fwd(proj, sinks, og, ycat, name):
    T = proj.shape[0]
    nb = T // WINDOW

    def body(q_ref, kp_ref, kc_ref, vp_ref, vc_ref, s_ref, og_ref, _, y_ref, o_ref):
        n = pl.program_id(0)
        kv = _kv_variants(jnp.concatenate([kp_ref[...], kc_ref[...]], axis=0))
        vv = _kv_variants(jnp.concatenate([vp_ref[...], vc_ref[...]], axis=0))
        own = _own_block()
        prev_bias = jnp.where(n > 0, 0.0, NEG)
        sinks_v = s_ref[...]
        ssq = jnp.zeros((WINDOW, 1), F32)
        for p in range(N_PAIRS):
            j = p // PAIRS_PER_KV
            qp = q_ref[:, p * 128:(p + 1) * 128].astype(BF16)
            o_pair = jnp.zeros((WINDOW, 128), F32)
            for par in range(2):
                pn, _ = _attn_probs(qp, kv[j][par], own, prev_bias, _sink(sinks_v, 2 * p + par))
                p_prev, p_own = _unfold(own, pn.astype(BF16))
                o_pair = o_pair + _dot_nn(p_prev, vv[j][par][:WINDOW]) + _dot_nn(p_own, vv[j][par][WINDOW:])
            o_ref[:, p * 128:(p + 1) * 128] = o_pair
            ssq = ssq + jnp.sum(o_pair * o_pair, axis=1, keepdims=True)
        rn = lax.rsqrt(ssq * (1.0 / ATTN_W) + EPS)
        y_ref[...] = (o_ref[...] * rn * og_ref[...]).astype(BF16)

    return pl.pallas_call(
        body, grid=(nb,),
        in_specs=[pl.BlockSpec((WINDOW, ATTN_W), lambda n: (n, OFF_Q // ATTN_W)), *_kv_specs(),
                  pl.BlockSpec((1, N_HEADS), lambda n: (0, 0)), pl.BlockSpec((1, ATTN_W), lambda n: (0, 0)), ANY],
        out_specs=[pl.BlockSpec((WINDOW, ATTN_W), lambda n: (n, 1)), pl.BlockSpec((WINDOW, ATTN_W), lambda n: (n, 0))],
        out_shape=[jax.ShapeDtypeStruct(ycat.shape, BF16), jax.ShapeDtypeStruct((T, ATTN_W), F32)],
        input_output_aliases={7: 0}, name=name, compiler_params=_cparams(("parallel",)),
    )(proj, proj, proj, proj, proj, sinks, og, ycat)


def _attn_bwd(proj, sinks, og, o, dy, dproj, name):
    T = proj.shape[0]
    nb = T // WINDOW

    def body(q_ref, kp_ref, kc_ref, vp_ref, vc_ref, s_ref, og_ref, o_ref, dy_ref, _,
             dq_ref, dk_ref, dv_ref, ds_ref, dog_ref):
        n = pl.program_id(0)

        @pl.when(n == 0)
        def _():
            dk_ref[...] = jnp.zeros_like(dk_ref)
            dv_ref[...] = jnp.zeros_like(dv_ref)
            ds_ref[...] = jnp.zeros_like(ds_ref)
            dog_ref[...] = jnp.zeros_like(dog_ref)

        kv = _kv_variants(jnp.concatenate([kp_ref[...], kc_ref[...]], axis=0))
        vv = _kv_variants(jnp.concatenate([vp_ref[...], vc_ref[...]], axis=0))
        own = _own_block()
        prev_bias = jnp.where(n > 0, 0.0, NEG)
        sinks_v = s_ref[...]
        of = o_ref[...]
        rn = lax.rsqrt(jnp.mean(of * of, axis=-1, keepdims=True) + EPS)
        oh = of * rn
        dyf = dy_ref[...]
        dog_ref[...] += jnp.sum(dyf * oh, axis=0, keepdims=True)
        doh = dyf * og_ref[...]
        do = rn * (doh - oh * jnp.mean(doh * oh, axis=-1, keepdims=True))
        lane = lax.broadcasted_iota(jnp.int32, (1, 128), 1)
        lane16 = lax.broadcasted_iota(jnp.int32, (1, N_HEADS), 1)
        sub = lax.broadcasted_iota(jnp.int32, (128, 1), 0)
        dk_acc = [[[jnp.zeros((128, WINDOW), F32) for _ in range(2)] for _ in range(2)] for _ in range(2)]
        dv_acc = [[[jnp.zeros((128, WINDOW), F32) for _ in range(2)] for _ in range(2)] for _ in range(2)]
        dsink = jnp.zeros((1, N_HEADS), F32)
        for p in range(N_PAIRS):
            j = p // PAIRS_PER_KV
            q_f = q_ref[:, p * 128:(p + 1) * 128]
            qp = q_f.astype(BF16)
            q_t = q_f.T.astype(BF16)
            do_p = do[:, p * 128:(p + 1) * 128]
            o_p = of[:, p * 128:(p + 1) * 128]
            do_b = do_p.astype(BF16)
            do_t = do_p.T.astype(BF16)
            prod = do_p * o_p
            dq_pair = jnp.zeros((WINDOW, 128), F32)
            for par in range(2):
                r = 2 * p + par
                half = (lane < HEAD_DIM) if par == 0 else (lane >= HEAD_DIM)
                rows_half = (sub < HEAD_DIM) if par == 0 else (sub >= HEAD_DIM)
                pn, ps = _attn_probs(qp, kv[j][par], own, prev_bias, _sink(sinks_v, r))
                delta = jnp.sum(jnp.where(half, prod, 0.0), axis=1, keepdims=True)
                dP = _fold(own, _dot_nt(do_b, vv[j][par]))
                dS = pn * (dP - delta)
                dsink = dsink + jnp.where(lane16 == r, -jnp.sum(ps * delta, axis=0, keepdims=True), 0.0)
                dS_parts = _unfold(own, (dS * ATTN_SCALE).astype(BF16))
                p_parts = _unfold(own, pn.astype(BF16))
                q_th = jnp.where(rows_half, q_t, jnp.zeros_like(q_t))
                do_th = jnp.where(rows_half, do_t, jnp.zeros_like(do_t))
                for blk in range(2):
                    dq_pair = dq_pair + _dot_nn(dS_parts[blk], kv[j][par][blk * WINDOW:(blk + 1) * WINDOW])
                    dk_acc[j][par][blk] = dk_acc[j][par][blk] + _dot_nn(q_th, dS_parts[blk])
                    dv_acc[j][par][blk] = dv_acc[j][par][blk] + _dot_nn(do_th, p_parts[blk])
            dq_ref[:, p * 128:(p + 1) * 128] = dq_pair.astype(BF16)
        rows = [pl.multiple_of(jnp.maximum(n - 1, 0) * WINDOW, WINDOW), pl.multiple_of(n * WINDOW, WINDOW)]
        for acc, ref in [(dk_acc, dk_ref), (dv_acc, dv_ref)]:
            for blk in range(2):
                both_t = (acc[0][0][blk] + pltpu.roll(acc[0][1][blk], HEAD_DIM, axis=0)
                          + acc[1][1][blk] + pltpu.roll(acc[1][0][blk], HEAD_DIM, axis=0))
                ref[pl.ds(rows[blk], WINDOW), :] += both_t.T
        ds_ref[...] += dsink

    full_kv = pl.BlockSpec((T, KV_W), lambda n: (0, 0))
    blk = pl.BlockSpec((WINDOW, ATTN_W), lambda n: (n, 0))
    return pl.pallas_call(
        body, grid=(nb,),
        in_specs=[pl.BlockSpec((WINDOW, ATTN_W), lambda n: (n, OFF_Q // ATTN_W)), *_kv_specs(),
                  pl.BlockSpec((1, N_HEADS), lambda n: (0, 0)), pl.BlockSpec((1, ATTN_W), lambda n: (0, 0)),
                  blk, pl.BlockSpec((WINDOW, ATTN_W), lambda n: (n, 1)), ANY],
        out_specs=[pl.BlockSpec((WINDOW, ATTN_W), lambda n: (n, OFF_Q // ATTN_W)), full_kv, full_kv,
                   pl.BlockSpec((1, N_HEADS), lambda n: (0, 0)), pl.BlockSpec((1, ATTN_W), lambda n: (0, 0))],
        out_shape=[jax.ShapeDtypeStruct(dproj.shape, BF16), jax.ShapeDtypeStruct((T, KV_W), F32),
                   jax.ShapeDtypeStruct((T, KV_W), F32), jax.ShapeDtypeStruct((1, N_HEADS), F32),
                   jax.ShapeDtypeStruct((1, ATTN_W), F32)],
        input_output_aliases={9: 0}, name=name, compiler_params=_cparams(("arbitrary",)),
    )(proj, proj, proj, proj, proj, sinks, og, o, dy, dproj)


ANY = pl.BlockSpec(memory_space=pl.ANY)


def _coords():
    return lax.axis_index("x"), lax.axis_index("y"), lax.axis_index("c")


def _all_gather(arrs, name):
    n = len(arrs)

    def body(*refs):
        ins, outs = refs[:n], refs[n:2 * n]
        send_sems, recv_sems, local_sems = refs[2 * n:]
        x, y, c = _coords()
        me = 4 * x + 2 * y + c
        sibling = (x, y, 1 - c)
        chips = [(1 - x, y), (x, 1 - y), (1 - x, 1 - y)]

        def copy(a, k, block, to, src=None):
            dst = outs[a].at[block]
            return pltpu.make_async_remote_copy(
                src_ref=dst if src is None else src, dst_ref=dst, send_sem=send_sems.at[a, k],
                recv_sem=recv_sems.at[a, k], device_id=to, device_id_type=MESH)

        mine = [pltpu.make_async_copy(ins[a], outs[a].at[me], local_sems.at[a]) for a in range(n)]
        for cp in mine:
            cp.start()
        first = []
        for a in range(n):
            first.append(copy(a, 0, me, sibling, src=ins[a]))
            for j, chip in enumerate(chips):
                first.append(copy(a, 1 + j, me, (*chip, c), src=ins[a]))
        for cp in first:
            cp.start()
        passed = []
        for j, (px, py) in enumerate(chips):
            blk = 4 * px + 2 * py + c
            for a in range(n):
                copy(a, 1 + j, blk, sibling).wait_recv()
                fwd = copy(a, 4 + j, blk, sibling)
                fwd.start()
                passed.append(fwd)
        for a in range(n):
            copy(a, 0, 4 * x + 2 * y + (1 - c), sibling).wait_recv()
            for j, (px, py) in enumerate(chips):
                copy(a, 4 + j, 4 * px + 2 * py + (1 - c), sibling).wait_recv()
        for cp in first + passed:
            cp.wait_send()
        for cp in mine:
            cp.wait()

    return pl.pallas_call(
        body, in_specs=[ANY] * n, out_specs=[ANY] * n,
        out_shape=[jax.ShapeDtypeStruct((N_DEV,) + a.shape, a.dtype) for a in arrs],
        scratch_shapes=[pltpu.SemaphoreType.DMA((n, 7)), pltpu.SemaphoreType.DMA((n, 7)),
                        pltpu.SemaphoreType.DMA((n,))],
        name=name,
    )(*arrs)


def _exchange_pair(arrs, name):
    n = len(arrs)

    def body(*refs):
        ins, outs = refs[:n], refs[n:2 * n]
        send_sems, recv_sems = refs[2 * n:]
        x, y, c = _coords()
        cps = []
        for a in range(n):
            for q in range(4):
                cps.append(pltpu.make_async_remote_copy(
                    src_ref=ins[a].at[2 * q + (1 - c)], dst_ref=outs[a].at[q], send_sem=send_sems.at[a, q],
                    recv_sem=recv_sems.at[a, q], device_id=(x, y, 1 - c), device_id_type=MESH))
        for cp in cps:
            cp.start()
        for cp in cps:
            cp.wait()

    return pl.pallas_call(
        body, in_specs=[ANY] * n, out_specs=[ANY] * n,
        out_shape=[jax.ShapeDtypeStruct((4,) + a.shape[1:], a.dtype) for a in arrs],
        scratch_shapes=[pltpu.SemaphoreType.DMA((n, 4)), pltpu.SemaphoreType.DMA((n, 4))],
        name=name,
    )(*arrs)


def _exchange_chips(arrs, name):
    n = len(arrs)

    def body(*refs):
        ins, outs = refs[:n], refs[n:2 * n]
        send_sems, recv_sems = refs[2 * n:]
        x, y, c = _coords()
        chips = [(1 - x, y), (x, 1 - y), (1 - x, 1 - y)]
        cps = []
        for a in range(n):
            for k, (tx, ty) in enumerate(chips):
                cps.append(pltpu.make_async_remote_copy(
                    src_ref=ins[a].at[2 * tx + ty], dst_ref=outs[a].at[k], send_sem=send_sems.at[a, k],
                    recv_sem=recv_sems.at[a, k], device_id=(tx, ty, c), device_id_type=MESH))
        for cp in cps:
            cp.start()
        for cp in cps:
            cp.wait()

    return pl.pallas_call(
        body, in_specs=[ANY] * n, out_specs=[ANY] * n,
        out_shape=[jax.ShapeDtypeStruct((3,) + a.shape[1:], a.dtype) for a in arrs],
        scratch_shapes=[pltpu.SemaphoreType.DMA((n, 3)), pltpu.SemaphoreType.DMA((n, 3))],
        name=name,
    )(*arrs)


HBM = pl.BlockSpec(memory_space=pltpu.HBM)
SEM = pl.BlockSpec(memory_space=pltpu.SEMAPHORE)
EFFECT = pltpu.SideEffectType.DATAFLOW_SIDE_EFFECTING


def _in_hbm(a):
    return pltpu.with_memory_space_constraint(a, pltpu.HBM)


def _remote_start(srcs, lands, plan, n_copies, name, after=None):
    ns, nb = len(srcs), len(srcs) + len(lands)
    n_after = 0 if after is None else 1

    def body(*refs):
        src_refs, land_refs = refs[:ns], refs[ns:nb]
        send_sems, recv_sems = refs[nb + n_after], refs[nb + n_after + 1]
        token = refs[-1]
        x, y, c = _coords()
        for i, (sv, dv, dev) in enumerate(plan(src_refs, land_refs, x, y, c)):
            pltpu.make_async_remote_copy(src_ref=sv, dst_ref=dv, send_sem=send_sems.at[i], recv_sem=recv_sems.at[i],
                                         device_id=dev, device_id_type=MESH).start()
        token[...] = jnp.zeros_like(token)

    bufs = list(srcs) + list(lands)
    outs = pl.pallas_call(
        body, name=name,
        out_shape=(pltpu.SemaphoreType.DMA((n_copies,)), pltpu.SemaphoreType.DMA((n_copies,)),
                   *[pltpu.HBM(b.shape, b.dtype) for b in bufs], jax.ShapeDtypeStruct((8, 128), F32)),
        in_specs=[HBM] * nb + [ANY] * n_after,
        out_specs=(SEM, SEM, *[HBM] * nb, pl.BlockSpec(memory_space=pltpu.VMEM)),
        input_output_aliases={i: 2 + i for i in range(nb)},
        compiler_params=pltpu.CompilerParams(has_side_effects=EFFECT),
    )(*[_in_hbm(b) for b in bufs], *([] if after is None else [after]))
    return outs[0], outs[1], list(outs[2:2 + ns]), list(outs[2 + ns:2 + nb]), outs[-1]


def _remote_wait(started, after, plan, name):
    send_sems, recv_sems, srcs, lands, _ = started
    ns, nb = len(srcs), len(srcs) + len(lands)

    def body(*refs):
        src_refs, land_refs = refs[:ns], refs[ns:nb]
        send_sems, recv_sems = refs[nb], refs[nb + 1]
        x, y, c = _coords()
        for i, (sv, dv, dev) in enumerate(plan(src_refs, land_refs, x, y, c)):
            cp = pltpu.make_async_remote_copy(src_ref=sv, dst_ref=dv, send_sem=send_sems.at[i],
                                              recv_sem=recv_sems.at[i], device_id=dev, device_id_type=MESH)
            cp.wait_send()
            cp.wait_recv()

    bufs = list(srcs) + list(lands)
    outs = pl.pallas_call(
        body, name=name, out_shape=tuple(pltpu.HBM(b.shape, b.dtype) for b in bufs),
        in_specs=[HBM] * nb + [SEM, SEM, ANY], out_specs=tuple([HBM] * nb),
        input_output_aliases={i: i for i in range(nb)},
        compiler_params=pltpu.CompilerParams(has_side_effects=EFFECT),
    )(*bufs, send_sems, recv_sems, after)
    return list(outs[:ns]), list(outs[ns:])


def _gather_plan(src_refs, land_refs, x, y, c):
    me = 4 * x + 2 * y + c
    plan = []
    for s, l in zip(src_refs, land_refs):
        for dev in [(x, y, 1 - c), (1 - x, y, c), (x, 1 - y, c), (1 - x, 1 - y, c)]:
            plan.append((s, l.at[me], dev))
    return plan


def _forward_plan(src_refs, land_refs, x, y, c):
    plan = []
    for l in land_refs:
        for px, py in [(1 - x, y), (x, 1 - y), (1 - x, 1 - y)]:
            blk = l.at[4 * px + 2 * py + c]
            plan.append((blk, blk, (x, y, 1 - c)))
    return plan


def _pair_plan(src_refs, land_refs, x, y, c):
    plan = []
    for s, l in zip(src_refs, land_refs):
        for q in range(4):
            plan.append((s.at[2 * q + (1 - c)], l.at[q], (x, y, 1 - c)))
    return plan


def _pair4_plan(src_refs, land_refs, x, y, c):
    plan = []
    for s, l in zip(src_refs, land_refs):
        for q in range(4):
            plan.append((s.at[q], l.at[q], (x, y, 1 - c)))
    return plan


def _chips_plan(src_refs, land_refs, x, y, c):
    plan = []
    for s, l in zip(src_refs, land_refs):
        for k, (tx, ty) in enumerate([(1 - x, y), (x, 1 - y), (1 - x, 1 - y)]):
            plan.append((s.at[2 * tx + ty], l.at[k], (tx, ty, c)))
    return plan


def _everyone_plan(src_refs, land_refs, x, y, c):
    me = 4 * x + 2 * y + c
    plan = []
    for s, l in zip(src_refs, land_refs):
        for fx, fy, fc in [(0, 0, 1), (1, 0, 0), (1, 0, 1), (0, 1, 0), (0, 1, 1), (1, 1, 0), (1, 1, 1)]:
            dev = ((1 - x) if fx else x, (1 - y) if fy else y, (1 - c) if fc else c)
            plan.append((s, l.at[me], dev))
    return plan


def _gather_finish(gathered, name):
    n = len(gathered)

    def body(*refs):
        outs = refs[n:2 * n]
        send_sems, recv_sems = refs[2 * n:]
        x, y, c = _coords()
        cps = []
        for a in range(n):
            for j, (px, py) in enumerate([(1 - x, y), (x, 1 - y), (1 - x, 1 - y)]):
                blk = outs[a].at[4 * px + 2 * py + c]
                got = outs[a].at[4 * px + 2 * py + (1 - c)]
                cps.append((pltpu.make_async_remote_copy(
                    src_ref=blk, dst_ref=blk, send_sem=send_sems.at[a, j], recv_sem=recv_sems.at[a, j],
                    device_id=(x, y, 1 - c), device_id_type=MESH), pltpu.make_async_remote_copy(
                    src_ref=got, dst_ref=got, send_sem=send_sems.at[a, j], recv_sem=recv_sems.at[a, j],
                    device_id=(x, y, 1 - c), device_id_type=MESH)))
        for cp, _ in cps:
            cp.start()
        for cp, arrival in cps:
            cp.wait_send()
            arrival.wait_recv()

    return pl.pallas_call(
        body, in_specs=[ANY] * n, out_specs=[ANY] * n,
        out_shape=[jax.ShapeDtypeStruct(g.shape, g.dtype) for g in gathered],
        input_output_aliases={a: a for a in range(n)},
        scratch_shapes=[pltpu.SemaphoreType.DMA((n, 3)), pltpu.SemaphoreType.DMA((n, 3))],
        name=name,
    )(*gathered)


def _pair_add(g8, r1, csel, tr, name):
    _, R, C = r1.shape
    g4 = g8.reshape(4, 2, R, C)

    def body(c_ref, g_ref, r_ref, o_ref):
        o_ref[...] = (g_ref[...].astype(F32) + r_ref[...].astype(F32)).astype(BF16)

    return pl.pallas_call(
        body,
        grid_spec=pltpu.PrefetchScalarGridSpec(
            num_scalar_prefetch=1, grid=(4, R // tr),
            in_specs=[pl.BlockSpec((None, None, tr, C), lambda q, i, cs: (q, cs[0], i, 0)),
                      pl.BlockSpec((None, tr, C), lambda q, i, cs: (q, i, 0))],
            out_specs=pl.BlockSpec((None, tr, C), lambda q, i, cs: (q, i, 0))),
        out_shape=jax.ShapeDtypeStruct((4, R, C), BF16), name=name,
        compiler_params=_cparams(("parallel", "parallel")),
    )(csel, g4, r1)


def _adamw_math(w, g, m, v):
    m = ADAM_B1 * m + (1.0 - ADAM_B1) * g
    v = ADAM_B2 * v + (1.0 - ADAM_B2) * (g * g)
    m_hat = m / (1.0 - ADAM_B1 ** ADAM_STEP)
    v_hat = v / (1.0 - ADAM_B2 ** ADAM_STEP)
    delta = -ADAM_LR * (m_hat / (jnp.sqrt(v_hat) + ADAM_EPS) + ADAM_WD * w)
    return delta, m, v


def _adamw_big(w, m, v, p4, r3, qsel, tile, name):
    R, C = w.shape
    tr, tc = tile

    def body(q_ref, w_ref, m_ref, v_ref, p_ref, r_ref, g_out, d_out, m_out, v_out):
        g = p_ref[...].astype(F32) + r_ref[0].astype(F32) + r_ref[1].astype(F32) + r_ref[2].astype(F32)
        d, mn, vn = _adamw_math(w_ref[...], g, m_ref[...], v_ref[...])
        g_out[...] = g
        d_out[...] = d
        m_out[...] = mn
        v_out[...] = vn

    blk = pl.BlockSpec((tr, tc), lambda i, j, qs: (i, j))
    return pl.pallas_call(
        body,
        grid_spec=pltpu.PrefetchScalarGridSpec(
            num_scalar_prefetch=1, grid=(R // tr, C // tc),
            in_specs=[blk, blk, blk, pl.BlockSpec((None, tr, tc), lambda i, j, qs: (qs[0], i, j)),
                      pl.BlockSpec((3, tr, tc), lambda i, j, qs: (0, i, j))],
            out_specs=[blk, blk, blk, blk]),
        out_shape=[jax.ShapeDtypeStruct((R, C), F32)] * 4, name=name,
        compiler_params=_cparams(("parallel", "parallel")),
    )(qsel, w, m, v, p4, r3)


def _sum_partials(p4, r3, qsel, tc, name):
    _, R, C = p4.shape

    def body(q_ref, p_ref, r_ref, o_ref):
        o_ref[...] = p_ref[...].astype(F32) + r_ref[0].astype(F32) + r_ref[1].astype(F32) + r_ref[2].astype(F32)

    return pl.pallas_call(
        body,
        grid_spec=pltpu.PrefetchScalarGridSpec(
            num_scalar_prefetch=1, grid=(C // tc,),
            in_specs=[pl.BlockSpec((None, R, tc), lambda j, qs: (qs[0], 0, j)),
                      pl.BlockSpec((3, R, tc), lambda j, qs: (0, 0, j))],
            out_specs=pl.BlockSpec((R, tc), lambda j, qs: (0, j))),
        out_shape=jax.ShapeDtypeStruct((R, C), F32), name=name, compiler_params=_cparams(("parallel",)),
    )(qsel, p4, r3)


def _adamw_tiled(w, g, m, v, tc, name):
    R, C = w.shape

    def body(w_ref, g_ref, m_ref, v_ref, d_out, m_out, v_out):
        d, mn, vn = _adamw_math(w_ref[...], g_ref[...], m_ref[...], v_ref[...])
        d_out[...] = d
        m_out[...] = mn
        v_out[...] = vn

    blk = pl.BlockSpec((R, tc), lambda j: (0, j))
    return pl.pallas_call(
        body, grid=(C // tc,), in_specs=[blk] * 4, out_specs=[blk] * 3,
        out_shape=[jax.ShapeDtypeStruct((R, C), F32)] * 3, name=name, compiler_params=_cparams(("parallel",)),
    )(w, g, m, v)


def _small_sum(parts, name):
    def body(p_ref, o_ref):
        acc = p_ref[0]
        for d in range(1, N_DEV):
            acc = acc + p_ref[d]
        o_ref[...] = acc

    return pl.pallas_call(
        body, out_shape=jax.ShapeDtypeStruct(parts.shape[1:], F32), name=name,
        compiler_params=_cparams(),
    )(parts)


def _adamw_small(w, g, m, v, name):
    def body(w_ref, g_ref, m_ref, v_ref, d_out, m_out, v_out):
        d, mn, vn = _adamw_math(w_ref[...], g_ref[...], m_ref[...], v_ref[...])
        d_out[...] = d
        m_out[...] = mn
        v_out[...] = vn

    return pl.pallas_call(
        body, out_shape=[jax.ShapeDtypeStruct(w.shape, F32)] * 3, name=name, compiler_params=_cparams(),
    )(w, g, m, v)


def _row(*pieces):
    r = jnp.concatenate([p.reshape(1, -1) for p in pieces], axis=1)
    return jnp.pad(r, ((0, 0), (0, D_MODEL - r.shape[1])))


def _pack_small(mix, convb, ssmg, attng, mlpg, fing, convw, dtb, alog, dsk, sinks, extra=None):
    last = [dtb, alog, dsk, sinks] + ([extra] if extra is not None else [])
    rows = [_row(mix), _row(convb), _row(ssmg, attng), _row(mlpg), _row(fing),
            jnp.pad(convw, ((0, 0), (0, D_MODEL - convw.shape[1]))), _row(*last)]
    packed = jnp.concatenate(rows, axis=0)
    return jnp.pad(packed, ((0, SMALL_ROWS - packed.shape[0]), (0, 0)))


def _unpack_small(p, conv_n):
    return dict(
        mix_norm_g=p[0:1, :], conv_b=p[1:2, :], ssm_norm_g=p[2:3, :D_INNER], attn_out_norm_g=p[2:3, D_INNER:],
        mlp_norm_g=p[3:4, :], final_norm_g=p[4, :], conv_w=p[5:9, :conv_n][None],
        dt_bias=p[9:10, 0:16], A_log=p[9:10, 16:32], D_skip=p[9:10, 32:48], attn_sinks=p[9:10, 48:64])


SMALL_NAMES = ["mix_norm_g", "conv_w", "conv_b", "dt_bias", "A_log", "D_skip", "ssm_norm_g", "attn_sinks",
               "attn_out_norm_g", "mlp_norm_g", "final_norm_g"]
WEIGHT_ORDER = ["mix_norm_g", "w_in", "conv_w", "conv_b", "dt_bias", "A_log", "D_skip", "ssm_norm_g", "attn_sinks",
                "attn_out_norm_g", "w_out", "mlp_norm_g", "w_up", "w_down", "final_norm_g"]


def _to_my_columns(w_nat):
    pad = jnp.zeros((w_nat.shape[0], NP - IN_PROJ), w_nat.dtype)
    return jnp.concatenate([w_nat[:, :NAT_DT], w_nat[:, NAT_DT + N_HEADS:], w_nat[:, NAT_DT:NAT_DT + N_HEADS], pad],
                           axis=1)


PER = IN_PROJ // N_DEV
SUPER_STEP = 544
SUPER = 576


def _natural_rows(g, lo, hi):
    segments = [(0, NAT_DT, 0), (NAT_DT, NAT_DT + N_HEADS, OFF_DT - NAT_DT), (NAT_DT + N_HEADS, IN_PROJ, -N_HEADS),
                (IN_PROJ, NP, 0)]
    pieces = [g[max(lo, a) + shift:min(hi, b) + shift] for a, b, shift in segments if max(lo, a) < min(hi, b)]
    return pieces[0] if len(pieces) == 1 else jnp.concatenate(pieces, axis=0)


def _w_in_from_super_slabs(sup):
    seam = SUPER - SUPER_STEP
    units = []
    for i in range(N_DEV):
        base = SUPER_STEP * i
        units.append((base, base + seam, sup[i, :seam] if i == 0 else sup[i - 1, SUPER_STEP:] + sup[i, :seam]))
        units.append((base + seam, base + SUPER_STEP, sup[i, seam:SUPER_STEP]))
    units.append((SUPER_STEP * N_DEV, SUPER_STEP * N_DEV + seam, sup[N_DEV - 1, SUPER_STEP:]))

    def natural(lo, hi):
        return [rows[max(lo, a) - a:min(hi, b) - a] for a, b, rows in units if max(lo, a) < min(hi, b)]

    pieces = natural(0, NAT_DT) + natural(NAT_DT + N_HEADS, IN_PROJ) + natural(NAT_DT, NAT_DT + N_HEADS)
    return jnp.concatenate(pieces + [jnp.zeros((NP - IN_PROJ, D_MODEL), sup.dtype)], axis=0)


def _to_natural_columns(w_my):
    return jnp.concatenate([w_my[:, :NAT_DT], w_my[:, OFF_DT:OFF_DT + N_HEADS], w_my[:, NAT_DT:OFF_DT]], axis=1)


SLAB = 1024


def _grad_w_up(h2, du, name, sel=None, add=None, after=None):
    T, D = h2.shape
    if sel is None:
        pick, n_slab, pre = (lambda j, *cs: j), N_DEV, None
    else:
        pre, other = sel
        pick, n_slab = (lambda j, cs: 2 * j + ((1 - cs[0]) if other else cs[0])), 4
    o_spec = pl.BlockSpec((None, SLAB, SLAB), lambda i, j, k, *cs: (j, i, 0))
    return _matmul(
        h2, du, mode="tn", grid=(D // SLAB, n_slab, 1),
        a_spec=pl.BlockSpec((T, SLAB), lambda i, j, k, *cs: (0, i)),
        b_spec=pl.BlockSpec((T, SLAB), lambda i, j, k, *cs: (0, pick(j, *cs))),
        out_shapes=[jax.ShapeDtypeStruct((n_slab, D, SLAB), BF16)], out_specs=[o_spec], tile=(SLAB, SLAB), name=name,
        extras=() if add is None else (add,), extra_specs=() if add is None else (o_spec,),
        epilogue=None if add is None else (lambda acc, r: (acc + r.astype(F32),)), after=after, prefetch=pre)[0]


def _grad_w_down(act, dx3b, name, sel=None, add=None, after=None):
    T, D = dx3b.shape
    if sel is None:
        pick, n_slab, pre = (lambda i, *cs: i), N_DEV, None
    else:
        pre, other = sel
        pick, n_slab = (lambda i, cs: 2 * i + ((1 - cs[0]) if other else cs[0])), 4
    o_spec = pl.BlockSpec((None, SLAB, SLAB), lambda i, j, k, *cs: (i, 0, j))
    return _matmul(
        act, dx3b, mode="tn", grid=(n_slab, D // SLAB, 1),
        a_spec=pl.BlockSpec((T, SLAB), lambda i, j, k, *cs: (0, pick(i, *cs))),
        b_spec=pl.BlockSpec((T, SLAB), lambda i, j, k, *cs: (0, j)),
        out_shapes=[jax.ShapeDtypeStruct((n_slab, SLAB, D), BF16)], out_specs=[o_spec], tile=(SLAB, SLAB), name=name,
        extras=() if add is None else (add,), extra_specs=() if add is None else (o_spec,),
        epilogue=None if add is None else (lambda acc, r: (acc + r.astype(F32),)), after=after, prefetch=pre)[0]


class _FixedWeights:
    def __init__(self, w_in_p, w_out_f, w_up_s, w_down_f, conv_w_f):
        self.w = (w_in_p, w_out_f, w_up_s, w_down_f, conv_w_f)
        self.grads = {}

    def mixer_weights(self, after):
        return self.w[0], self.w[4], None

    def prefetch(self, k, after):
        return None

    def out_weight(self, after):
        return self.w[1]

    def up_weight(self, after):
        return self.w[2]

    def down_weight(self, after):
        return self.w[3]

    def mlp_grads(self, h2, du, act, dx3b):
        self.grads.update(w_up=_grad_w_up(h2, du, "grad_w_up"),
                          w_down=_grad_w_down(act, dx3b, "grad_w_down").reshape(D_FF, D_MODEL))
        return None

    def out_grad(self, g_out):
        self.grads.update(w_out=g_out)
        return None

    def in_grad(self, g_in):
        self.grads.update(w_in=g_in)
        return None


def _local_step(x, tgt, p, hooks):
    T = x.shape[0]
    D = D_MODEL
    h1 = _rmsnorm_fwd(x, p["mix_norm_g"], "norm_mix")
    w_in_t, conv_w_f, token = hooks.mixer_weights(h1)
    (proj,) = _mm_simple(h1, w_in_t, mode="nt", M=T, N=NP, K=D, tm=min(T, 1024), tn=1536, tk=D, out_dtype=F32,
                         name="in_proj", after=token)
    xbc = _conv_fwd(proj, conv_w_f, p["conv_b"], "conv_fwd")
    dtT = proj[:, OFF_DT:OFF_DT + N_HEADS].T
    dtbT = p["dt_bias"].T
    alogT = p["A_log"].T
    dfull = jnp.repeat(p["D_skip"], HEAD_DIM, axis=1)
    token = hooks.prefetch("out", xbc)
    ssm_g = p["ssm_norm_g"] if token is None else p["ssm_norm_g"] + token[0:1, 0:1]
    ycat, ypre, hs = _ssd_fwd(xbc, proj, dtT, p["dt_bias"], dtbT, p["A_log"], alogT, dfull, ssm_g, "ssd_fwd")
    ycat, o_att = _attn_fwd(proj, p["attn_sinks"], p["attn_out_norm_g"], ycat, "attn_fwd")
    token = hooks.prefetch("up", ycat)
    w_out_f = hooks.out_weight(ycat if token is None else token)
    tm = min(T, 1024)
    (x2,) = _mm_simple(ycat, w_out_f, mode="nn", M=T, N=D, K=D, tm=tm, tn=1024, tk=D, out_dtype=F32, name="out_proj",
                       extras=(x,), epilogue=lambda acc, res: (acc + res,))
    h2 = _rmsnorm_fwd(x2, p["mlp_norm_g"], "norm_mlp")
    w_up_s = hooks.up_weight(h2)
    grid = (T // tm, N_DEV, 1)
    u, act = _matmul(
        h2, w_up_s, mode="nn", grid=grid,
        a_spec=pl.BlockSpec((tm, D), lambda i, j, k: (i, 0)),
        b_spec=pl.BlockSpec((None, D, 1024), lambda i, j, k: (j, 0, 0)),
        out_shapes=[jax.ShapeDtypeStruct((T, D_FF), F32), jax.ShapeDtypeStruct((T, D_FF), BF16)],
        out_specs=[pl.BlockSpec((tm, 1024), lambda i, j, k: (i, j))] * 2, tile=(tm, 1024), name="mlp_up",
        epilogue=lambda acc: (acc, jnp.square(jnp.maximum(acc, 0.0))))
    w_down_f = hooks.down_weight(act)
    (x3,) = _mm_simple(act, w_down_f, mode="nn", M=T, N=D, K=D_FF, tm=tm, tn=1024, tk=2048, out_dtype=F32,
                       name="mlp_down", extras=(x2,), epilogue=lambda acc, res: (acc + res,))
    loss_part, d_fin, dx3, dx3b = _final_loss(x3, tgt, p["final_norm_g"].reshape(1, D), "loss_head")
    (du,) = _mm_simple(dx3b, w_down_f, mode="nt", M=T, N=D_FF, K=D, tm=tm, tn=1024, tk=D, out_dtype=BF16,
                       name="mlp_down_bwd", extras=(u,),
                       epilogue=lambda acc, uu: (acc * (2.0 * jnp.maximum(uu, 0.0)),))
    token = hooks.mlp_grads(h2, du, act, dx3b)
    (dh2,) = _matmul(
        du, w_up_s, mode="nt", grid=(T // tm, D // 1024, N_DEV // 2),
        a_spec=pl.BlockSpec((tm, 2048), lambda i, j, k: (i, k)),
        b_spec=pl.BlockSpec((2, 1024, 1024), lambda i, j, k: (k, j, 0)),
        out_shapes=[jax.ShapeDtypeStruct((T, D), F32)],
        out_specs=[pl.BlockSpec((tm, 1024), lambda i, j, k: (i, j))], tile=(tm, 1024), name="mlp_up_bwd",
        after=token, dot_fn=lambda a, b: _dot_nt(a[:, :1024], b[0]) + _dot_nt(a[:, 1024:], b[1]))
    dx2, dx2b, d_mlp = _rmsnorm_bwd(dh2, x2, p["mlp_norm_g"], dx3, "norm_mlp_bwd")
    (g_out,) = _mm_simple(ycat, dx2b, mode="tn", M=D, N=D, K=T, tm=1024, tn=1024, tk=T, out_dtype=BF16,
                          name="grad_w_out")
    token = hooks.out_grad(g_out)
    (dy,) = _mm_simple(dx2b, w_out_f, mode="nt", M=T, N=D, K=D, tm=tm, tn=1024, tk=D, out_dtype=F32,
                       name="out_proj_bwd", after=token)
    dproj, dxbc_act, d_dtb, d_alog, d_dskip, d_ssmg = _ssd_bwd(
        xbc, proj, dtT, p["dt_bias"], dtbT, p["A_log"], alogT, dfull, p["ssm_norm_g"], ypre, hs, dy, "ssd_bwd")
    dproj, d_convw, d_convb = _conv_bwd(proj, dxbc_act, conv_w_f, p["conv_b"], dproj, "conv_bwd")
    dproj, dk, dv, d_sinks, d_attng = _attn_bwd(proj, p["attn_sinks"], p["attn_out_norm_g"], o_att, dy, dproj,
                                                "attn_bwd")
    dproj = lax.dynamic_update_slice(dproj, jnp.concatenate([dk, dv], axis=1).astype(BF16), (0, OFF_K))
    (g_in,) = _mm_simple(dproj, h1, mode="tn", M=NP, N=D, K=T, tm=1536, tn=1024, tk=T, out_dtype=BF16,
                         name="grad_w_in")
    token = hooks.in_grad(g_in)
    (dh1,) = _mm_simple(dproj, w_in_t, mode="nn", M=T, N=D, K=NP, tm=tm, tn=1024, tk=2304, out_dtype=F32,
                        name="in_proj_bwd", after=token)
    dx, _, d_mix = _rmsnorm_bwd(dh1, x, p["mix_norm_g"], dx2, "norm_mix_bwd")
    small = _pack_small(d_mix, d_convb, d_ssmg, d_attng, d_mlp, d_fin, d_convw, d_dtb, d_alog, d_dskip, d_sinks,
                        extra=loss_part[:, 0:1])
    return dx, small


def _landing(own, me):
    zone = lax.empty((N_DEV,) + own.shape, own.dtype)
    return lax.dynamic_update_slice(zone, own[None], (me,) + (0,) * own.ndim)


def _sequencer_gather(owns, split, me, collective_id, name):
    n = len(owns)
    zone_refs = [jax.new_ref(_landing(o, me), memory_space=pltpu.MemorySpace.HBM) for o in owns]
    own_refs = [jax.new_ref(o, memory_space=pltpu.MemorySpace.HBM) for o in owns]
    N_COPIES = 9

    @pl.kernel(mesh=plsc.ScalarSubcoreMesh(axis_name="sequencer", num_cores=1), name=name,
               scratch_types=(pltpu.SemaphoreType.DMA((n, N_COPIES)), pltpu.SemaphoreType.DMA((n, N_COPIES))),
               compiler_params=pltpu.CompilerParams(collective_id=collective_id))
    def launch(send_sems, recv_sems):
        x, y, c = _coords()
        sibling, xn, yn, diag = (x, y, 1 - c), (1 - x, y, c), (x, 1 - y, c), (1 - x, 1 - y, c)
        barrier = pltpu.get_barrier_semaphore()
        for peer in [sibling, xn, yn, diag]:
            pl.semaphore_signal(barrier, inc=1, device_id=peer, device_id_type=MESH)
        pl.semaphore_wait(barrier, 4)

        def block(a, dev, half=None):
            ref = zone_refs[a].at[4 * dev[0] + 2 * dev[1] + dev[2]]
            if half is None:
                return ref
            rows = owns[a].shape[0] // 2
            return ref.at[pl.ds(half * rows, rows)]

        def copy(a, k, src, dst, to):
            return pltpu.make_async_remote_copy(src_ref=src, dst_ref=dst, send_sem=send_sems.at[a, k],
                                                recv_sem=recv_sems.at[a, k], device_id=to, device_id_type=MESH)

        me_dev = (x, y, c)
        sent = []
        first = {}
        for a in range(n):
            for k, peer in enumerate([sibling, xn, yn] + ([] if split[a] else [diag])):
                first[a, k] = copy(a, k, own_refs[a], block(a, me_dev), peer)
                first[a, k].start()
                sent.append(first[a, k])
        from_sibling = []
        for a in range(n):
            first[a, 1].wait_recv()
            sent.append(copy(a, 4, block(a, xn), block(a, xn), sibling))
            if split[a]:
                sent.append(copy(a, 6, block(a, xn, 0), block(a, xn, 0), yn))
            first[a, 2].wait_recv()
            sent.append(copy(a, 5, block(a, yn), block(a, yn), sibling))
            if split[a]:
                sent.append(copy(a, 7, block(a, yn, 1), block(a, yn, 1), xn))
            for cp in sent[-(4 if split[a] else 2):]:
                cp.start()
        for a in range(n):
            if split[a]:
                copy(a, 6, block(a, diag, 0), block(a, diag, 0), yn).wait_recv()
                sent.append(copy(a, 8, block(a, diag, 0), block(a, diag, 0), sibling))
                sent[-1].start()
                copy(a, 7, block(a, diag, 1), block(a, diag, 1), xn).wait_recv()
                sent.append(copy(a, 3, block(a, diag, 1), block(a, diag, 1), sibling))
                sent[-1].start()
            else:
                first[a, 3].wait_recv()
                sent.append(copy(a, 8, block(a, diag), block(a, diag), sibling))
                sent[-1].start()
        for a in range(n):
            first[a, 0].wait_recv()
            copy(a, 4, block(a, xn), block(a, xn), sibling).wait_recv()
            copy(a, 5, block(a, yn), block(a, yn), sibling).wait_recv()
            if split[a]:
                copy(a, 8, block(a, diag, 0), block(a, diag, 0), sibling).wait_recv()
                copy(a, 3, block(a, diag, 1), block(a, diag, 1), sibling).wait_recv()
            else:
                copy(a, 8, block(a, diag), block(a, diag), sibling).wait_recv()
        for cp in sent:
            cp.wait_send()

    launch()
    return zone_refs


def _sequencer_exchange(srcs, lands, plan, peers, n_copies, collective_id, name):
    src_refs = [jax.new_ref(s, memory_space=pltpu.MemorySpace.HBM) for s in srcs]
    land_refs = [jax.empty_ref(l, memory_space=pltpu.MemorySpace.HBM) if isinstance(l, jax.ShapeDtypeStruct)
                 else jax.new_ref(l, memory_space=pltpu.MemorySpace.HBM) for l in lands]

    @pl.kernel(mesh=plsc.ScalarSubcoreMesh(axis_name="sequencer", num_cores=1), name=name,
               scratch_types=(pltpu.SemaphoreType.DMA((n_copies,)), pltpu.SemaphoreType.DMA((n_copies,))),
               compiler_params=pltpu.CompilerParams(collective_id=collective_id))
    def launch(send_sems, recv_sems):
        x, y, c = _coords()
        who = peers(x, y, c)
        barrier = pltpu.get_barrier_semaphore()
        for peer in who:
            pl.semaphore_signal(barrier, inc=1, device_id=peer, device_id_type=MESH)
        pl.semaphore_wait(barrier, len(who))
        cps = [pltpu.make_async_remote_copy(src_ref=sv, dst_ref=dv, send_sem=send_sems.at[i], recv_sem=recv_sems.at[i],
                                            device_id=dev, device_id_type=MESH)
               for i, (sv, dv, dev) in enumerate(plan(src_refs, land_refs, x, y, c))]
        for cp in cps:
            cp.start()
        for cp in cps:
            cp.wait()

    launch()
    return src_refs, land_refs


def _sibling_only(x, y, c):
    return [(x, y, 1 - c)]


def _other_chips(x, y, c):
    return [(1 - x, y, c), (x, 1 - y, c), (1 - x, 1 - y, c)]


def _everyone_else(x, y, c):
    return [((1 - x) if fx else x, (1 - y) if fy else y, (1 - c) if fc else c)
            for fx, fy, fc in [(0, 0, 1), (1, 0, 0), (1, 0, 1), (0, 1, 0), (0, 1, 1), (1, 1, 0), (1, 1, 1)]]


def _gather_end(started, after, plan, name):
    _, lands = _remote_wait(started, after, plan, name + "_wait")
    return _gather_finish(lands, name + "_finish")


class _ShardedWeights:
    def __init__(self, w_in, w_out, conv_w, w_up, w_down, me, csel):
        self.me, self.csel = me, csel
        own_rows = lax.dynamic_update_slice(jnp.zeros((SUPER, D_MODEL), F32), jnp.transpose(w_in), (2 * me, 0))
        self.in_ref, self.conv_ref = _sequencer_gather([own_rows.astype(BF16), conv_w], [True, False], me, 7,
                                                       "gather_w_in_sequencer")
        (self.out_ref,) = _sequencer_gather([w_out.astype(BF16)], [True], me, 8, "gather_w_out_sequencer")
        (self.up_ref,) = _sequencer_gather([w_up.astype(BF16)], [True], me, 9, "gather_w_up_sequencer")
        (self.down_ref,) = _sequencer_gather([w_down.astype(BF16)], [True], me, 10, "gather_w_down_sequencer")
        self.reduces = {}

    def mixer_weights(self, after):
        g_conv = self.conv_ref[...]
        conv_w_f = jnp.concatenate([g_conv[i] for i in range(N_DEV)], axis=1)
        return _w_in_from_super_slabs(self.in_ref[...]), conv_w_f, None

    def prefetch(self, k, after):
        return None

    def out_weight(self, after):
        return self.out_ref[...].reshape(D_MODEL, D_MODEL)

    def up_weight(self, after):
        return self.up_ref[...]

    def down_weight(self, after):
        return self.down_ref[...].reshape(D_FF, D_MODEL)

    CHIPS_ID = {"up": 13, "down": 14, "out": 15, "in": 16}

    def _to_chips(self, sums, tag):
        land = jax.ShapeDtypeStruct((3,) + sums.shape[1:], sums.dtype)
        token = sums[0, :8, :128].astype(F32)
        (sums_ref,), (got,) = _sequencer_exchange([sums], [land], _chips_plan, _other_chips, 3, self.CHIPS_ID[tag],
                                                  f"reduce_chips_{tag}_sequencer")
        self.reduces[tag] = (sums_ref, got)
        return token

    def _chips_start(self, slabs, from_sibling, rows, tag):
        (s,), (r,), (tr,) = slabs, from_sibling, rows
        return self._to_chips(_pair_add(s, r, self.csel, tr, f"pair_add_{tag}"), tag)

    def mlp_grads(self, h2, du, act, dx3b):
        def to_sibling(part, tag, cid):
            _, (got,) = _sequencer_exchange([part], [jax.ShapeDtypeStruct(part.shape, part.dtype)], _pair4_plan,
                                            _sibling_only, 4, cid, f"reduce_pair_{tag}_sequencer")
            return got

        up_send = _grad_w_up(h2, du, "grad_w_up_send", sel=(self.csel, True))
        from_up = to_sibling(up_send, "up", 11)
        down_send = _grad_w_down(act, dx3b, "grad_w_down_send", sel=(self.csel, True))
        from_down = to_sibling(down_send, "down", 12)
        up_sum = _grad_w_up(h2, du, "grad_w_up_keep", sel=(self.csel, False), add=from_up[...])
        token = self._to_chips(up_sum, "up")
        down_sum = _grad_w_down(act, dx3b, "grad_w_down_keep", sel=(self.csel, False), add=from_down[...],
                                after=token)
        return self._to_chips(down_sum, "down")

    def out_grad(self, g_out):
        slabs = [g_out.reshape(N_DEV, D_MODEL // N_DEV, D_MODEL)]
        return self._chips_start(slabs, _exchange_pair(slabs, "reduce_pair_out"), [256], "out")

    def in_grad(self, g_in):
        slabs = [jnp.stack([_natural_rows(g_in, SUPER_STEP * j, SUPER_STEP * j + SUPER) for j in range(N_DEV)])]
        return self._chips_start(slabs, _exchange_pair(slabs, "reduce_pair_in"), [SUPER], "in")

    def small_start(self, small):
        _, (self.small_ref,) = _sequencer_exchange([small], [_landing(small, self.me)], _everyone_plan,
                                                   _everyone_else, N_DEV - 1, 17, "gather_small_sequencer")

    def small_end(self, after):
        return self.small_ref[...]

    def reduce_end(self, tag, after):
        sums_ref, got = self.reduces[tag]
        return [sums_ref[...]], [got[...]]


def kernel(x, mix_norm_g, w_in, conv_w, conv_b, dt_bias, A_log, D_skip, ssm_norm_g, attn_sinks, attn_out_norm_g, w_out, mlp_norm_g, w_up, w_down, final_norm_g, loss_target, m_mix_norm_g, m_w_in, m_conv_w, m_conv_b, m_dt_bias, m_A_log, m_D_skip, m_ssm_norm_g, m_attn_sinks, m_attn_out_norm_g, m_w_out, m_mlp_norm_g, m_w_up, m_w_down, m_final_norm_g, v_mix_norm_g, v_w_in, v_conv_w, v_conv_b, v_dt_bias, v_A_log, v_D_skip, v_ssm_norm_g, v_attn_sinks, v_attn_out_norm_g, v_w_out, v_mlp_norm_g, v_w_up, v_w_down, v_final_norm_g):
    xi, yi, ci = _coords()
    me = 4 * xi + 2 * yi + ci
    csel = jnp.reshape(ci, (1,)).astype(jnp.int32)
    qsel = jnp.reshape(2 * xi + yi, (1,)).astype(jnp.int32)
    w = dict(mix_norm_g=mix_norm_g, conv_b=conv_b, dt_bias=dt_bias, A_log=A_log, D_skip=D_skip,
             ssm_norm_g=ssm_norm_g, attn_sinks=attn_sinks, attn_out_norm_g=attn_out_norm_g, mlp_norm_g=mlp_norm_g,
             final_norm_g=final_norm_g)
    hooks = _ShardedWeights(w_in[0], w_out[0], conv_w[0], w_up[0], w_down[0], me, csel)
    p = dict(w)
    dx, small = _local_step(x[0], loss_target[0], p, hooks)
    hooks.small_start(small)
    big = {}
    after = dx
    for name, wt, mt, vt, tile in [
            ("up", w_up, m_w_up, v_w_up, (512, SLAB)), ("down", w_down, m_w_down, v_w_down, (256, D_MODEL)),
            ("out", w_out, m_w_out, v_w_out, (256, D_MODEL))]:
        (chip_sums,), (from_chips,) = hooks.reduce_end(name, after)
        res = _adamw_big(wt[0], mt[0], vt[0], chip_sums, from_chips, qsel, tile, f"adamw_w_{name}")
        big["w_" + name] = tuple(r[None] for r in res)
        after = res[0]
    (chip_sums,), (from_chips,) = hooks.reduce_end("in", after)
    g_super = _sum_partials(chip_sums, from_chips, qsel, 512, "grad_w_in_sum")
    g_in = lax.dynamic_slice(g_super, (2 * me, 0), (PER, D_MODEL))
    res = _adamw_tiled(jnp.transpose(w_in[0]), g_in, jnp.transpose(m_w_in[0]), jnp.transpose(v_w_in[0]), 512,
                       "adamw_w_in")
    big["w_in"] = tuple(jnp.transpose(r)[None] for r in (g_in, *res))
    after = res[0]
    gsum = _small_sum(hooks.small_end(after), "small_sum")
    loss = gsum[9, 64]
    gs = _unpack_small(gsum, CONV_DIM)
    cw = CONV_DIM // N_DEV
    g_conv_shard = lax.dynamic_slice(gsum[5:9, :], (0, me * cw), (CONV_K, cw))

    def pack(s):
        return _pack_small(s["mix_norm_g"], s["conv_b"], s["ssm_norm_g"], s["attn_out_norm_g"], s["mlp_norm_g"],
                           s["final_norm_g"], s["conv_w"][0], s["dt_bias"], s["A_log"], s["D_skip"], s["attn_sinks"])

    wp = pack(dict(w, conv_w=conv_w))
    mp = pack(dict(mix_norm_g=m_mix_norm_g, conv_b=m_conv_b, ssm_norm_g=m_ssm_norm_g,
                   attn_out_norm_g=m_attn_out_norm_g, mlp_norm_g=m_mlp_norm_g, final_norm_g=m_final_norm_g,
                   conv_w=m_conv_w, dt_bias=m_dt_bias, A_log=m_A_log, D_skip=m_D_skip, attn_sinks=m_attn_sinks))
    vp = pack(dict(mix_norm_g=v_mix_norm_g, conv_b=v_conv_b, ssm_norm_g=v_ssm_norm_g,
                   attn_out_norm_g=v_attn_out_norm_g, mlp_norm_g=v_mlp_norm_g, final_norm_g=v_final_norm_g,
                   conv_w=v_conv_w, dt_bias=v_dt_bias, A_log=v_A_log, D_skip=v_D_skip, attn_sinks=v_attn_sinks))
    gp = jnp.concatenate([gsum[0:5], jnp.pad(g_conv_shard, ((0, 0), (0, D_MODEL - cw))), gsum[9:10],
                          jnp.zeros((SMALL_ROWS - 10, D_MODEL), F32)], axis=0)
    dp, mnp, vnp = _adamw_small(wp, gp, mp, vp, "adamw_small")
    grads = dict(gs, conv_w=g_conv_shard[None])
    deltas = _unpack_small(dp, cw)
    new_m = _unpack_small(mnp, cw)
    new_v = _unpack_small(vnp, cw)
    for k, name in enumerate(["w_in", "w_out", "w_up", "w_down"]):
        grads[name], deltas[name], new_m[name], new_v[name] = big[name]
    return (loss, dx[None], *[grads[n] for n in WEIGHT_ORDER], *[deltas[n] for n in WEIGHT_ORDER],
            *[new_m[n] for n in WEIGHT_ORDER], *[new_v[n] for n in WEIGHT_ORDER])
```

```python
import functools

import jax
import jax.numpy as jnp
from jax import lax
from jax.experimental import pallas as pl
from jax.experimental.pallas import tpu as pltpu
from jax.experimental.pallas import tpu_sc as plsc

F32 = jnp.float32
BF16 = jnp.bfloat16
HI = lax.Precision.HIGHEST
MESH = pl.DeviceIdType.MESH

EPS = 1e-5
D_MODEL = 2048
D_INNER = 1024
N_HEADS = 16
HEAD_DIM = 64
N_GROUPS = 4
D_STATE = 128
CHUNK = 128
CONV_K = 4
CONV_DIM = 2048
ATTN_W = 1024
KV_W = 128
WINDOW = 128
D_FF = 8192
IN_PROJ = 4368
N_DEV = 8
NP = 4608
OFF_Z, OFF_X, OFF_B, OFF_C, OFF_Q, OFF_K, OFF_V, OFF_DT = 0, 1024, 2048, 2560, 3072, 4096, 4224, 4352
NAT_DT = 3072

ADAM_LR = 0.001
ADAM_B1 = 0.9
ADAM_B2 = 0.999
ADAM_EPS = 1e-08
ADAM_WD = 0.01
ADAM_STEP = 10

VMEM_LIMIT = 52 * 1024 * 1024
SMALL_ROWS = 16
NEG = -1e30


def _cparams(sem=None):
    return pltpu.CompilerParams(dimension_semantics=sem, vmem_limit_bytes=VMEM_LIMIT)


def _split3(v):
    hi = v.astype(BF16)
    rest = v - hi.astype(F32)
    mid = rest.astype(BF16)
    return hi, mid, (rest - mid.astype(F32)).astype(BF16)


def _hdot(a, b, data):
    if data == "a":
        sel = b.astype(BF16)
        return sum(_dot_nn(part, sel) for part in _split3(a))
    sel = a.astype(BF16)
    return sum(_dot_nn(sel, part) for part in _split3(b))


def _dot_nn(a, b):
    return lax.dot_general(a, b, (((1,), (0,)), ((), ())), preferred_element_type=F32)


def _dot_nt(a, b):
    return lax.dot_general(a, b, (((1,), (1,)), ((), ())), preferred_element_type=F32)


def _dot_tn(a, b):
    return lax.dot_general(a, b, (((0,), (0,)), ((), ())), preferred_element_type=F32)


def _softplus(v):
    return jnp.maximum(v, 0.0) + jnp.log1p(jnp.exp(-jnp.abs(v)))


def _sigmoid(v):
    return 1.0 / (1.0 + jnp.exp(-v))


def _matmul(a, b, *, mode, grid, a_spec, b_spec, out_shapes, out_specs, tile, name,
            extras=(), extra_specs=(), epilogue=None, after=None, dot_fn=None, prefetch=None):
    nk = grid[2]
    n_ex = len(extras)
    n_out = len(out_shapes)
    dot = dot_fn if dot_fn is not None else {"nn": _dot_nn, "nt": _dot_nt, "tn": _dot_tn}[mode]

    def finish(acc, ex_refs, out_refs):
        res = (acc,) if epilogue is None else epilogue(acc, *[e[...] for e in ex_refs])
        for o, r in zip(out_refs, res):
            o[...] = r.astype(o.dtype)

    def body(*refs):
        a_ref, b_ref = refs[0], refs[1]
        ex_refs = refs[2:2 + n_ex]
        out_refs = refs[2 + n_ex:2 + n_ex + n_out]
        part = dot(a_ref[...].astype(BF16), b_ref[...].astype(BF16))
        if nk == 1:
            finish(part, ex_refs, out_refs)
        else:
            acc_ref = refs[-1]
            k = pl.program_id(2)

            @pl.when(k == 0)
            def _():
                acc_ref[...] = part

            @pl.when(k > 0)
            def _():
                acc_ref[...] += part

            @pl.when(k == nk - 1)
            def _():
                finish(acc_ref[...], ex_refs, out_refs)

    scratch = [] if nk == 1 else [pltpu.VMEM(tile, F32)]
    n_pre = 0 if prefetch is None else 1
    tok_specs = [] if after is None else [pl.BlockSpec((8, 128), lambda *_: (0, 0))]
    tok_args = [] if after is None else [after]

    def body_with_token(*refs):
        refs = refs[n_pre:]
        body(*refs[:2 + n_ex], *refs[2 + n_ex + len(tok_args):])

    in_specs = [a_spec, b_spec, *extra_specs, *tok_specs]
    params = _cparams(("parallel", "parallel", "arbitrary"))
    if prefetch is None:
        return pl.pallas_call(
            body_with_token, grid=grid, in_specs=in_specs, out_specs=list(out_specs), out_shape=list(out_shapes),
            scratch_shapes=scratch, name=name, compiler_params=params)(a, b, *extras, *tok_args)
    return pl.pallas_call(
        body_with_token,
        grid_spec=pltpu.PrefetchScalarGridSpec(num_scalar_prefetch=1, grid=grid, in_specs=in_specs,
                                               out_specs=list(out_specs), scratch_shapes=scratch),
        out_shape=list(out_shapes), name=name, compiler_params=params)(prefetch, a, b, *extras, *tok_args)


def _mm_simple(a, b, *, mode, M, N, K, tm, tn, tk, out_dtype, name, extras=(), epilogue=None, n_out=1,
               out_dtypes=None, after=None):
    grid = (M // tm, N // tn, K // tk)
    if mode == "nn":
        a_spec = pl.BlockSpec((tm, tk), lambda i, j, k: (i, k))
        b_spec = pl.BlockSpec((tk, tn), lambda i, j, k: (k, j))
    elif mode == "nt":
        a_spec = pl.BlockSpec((tm, tk), lambda i, j, k: (i, k))
        b_spec = pl.BlockSpec((tn, tk), lambda i, j, k: (j, k))
    else:
        a_spec = pl.BlockSpec((tk, tm), lambda i, j, k: (k, i))
        b_spec = pl.BlockSpec((tk, tn), lambda i, j, k: (k, j))
    o_spec = pl.BlockSpec((tm, tn), lambda i, j, k: (i, j))
    dts = out_dtypes if out_dtypes is not None else [out_dtype] * n_out
    return _matmul(a, b, mode=mode, grid=grid, a_spec=a_spec, b_spec=b_spec,
                   out_shapes=[jax.ShapeDtypeStruct((M, N), d) for d in dts],
                   out_specs=[o_spec] * len(dts), tile=(tm, tn), name=name,
                   extras=extras, extra_specs=[o_spec] * len(extras), epilogue=epilogue, after=after)


ROW_BLOCK = 256


def _rmsnorm_fwd(x, g, name):
    T, D = x.shape

    def body(x_ref, g_ref, o_ref):
        xf = x_ref[...]
        r = lax.rsqrt(jnp.mean(xf * xf, axis=-1, keepdims=True) + EPS)
        o_ref[...] = (xf * r * g_ref[...]).astype(BF16)

    return pl.pallas_call(
        body, grid=(T // ROW_BLOCK,),
        in_specs=[pl.BlockSpec((ROW_BLOCK, D), lambda i: (i, 0)), pl.BlockSpec((1, D), lambda i: (0, 0))],
        out_specs=pl.BlockSpec((ROW_BLOCK, D), lambda i: (i, 0)),
        out_shape=jax.ShapeDtypeStruct((T, D), BF16), name=name, compiler_params=_cparams(("parallel",)),
    )(x, g)


def _rmsnorm_bwd(dh, x, g, dres, name):
    T, D = x.shape

    def body(dh_ref, x_ref, g_ref, dres_ref, dx_ref, dxb_ref, dg_ref):
        i = pl.program_id(0)
        xf = x_ref[...]
        r = lax.rsqrt(jnp.mean(xf * xf, axis=-1, keepdims=True) + EPS)
        xh = xf * r
        d = dh_ref[...]

        @pl.when(i == 0)
        def _():
            dg_ref[...] = jnp.zeros_like(dg_ref)

        dg_ref[...] += jnp.sum(d * xh, axis=0, keepdims=True)
        dxh = d * g_ref[...]
        dx = r * (dxh - xh * jnp.mean(dxh * xh, axis=-1, keepdims=True)) + dres_ref[...]
        dx_ref[...] = dx
        dxb_ref[...] = dx.astype(BF16)

    row = pl.BlockSpec((ROW_BLOCK, D), lambda i: (i, 0))
    vec = pl.BlockSpec((1, D), lambda i: (0, 0))
    return pl.pallas_call(
        body, grid=(T // ROW_BLOCK,), in_specs=[row, row, vec, row], out_specs=[row, row, vec],
        out_shape=[jax.ShapeDtypeStruct((T, D), F32), jax.ShapeDtypeStruct((T, D), BF16),
                   jax.ShapeDtypeStruct((1, D), F32)],
        name=name, compiler_params=_cparams(("arbitrary",)),
    )(dh, x, g, dres)


def _final_loss(x3, tgt, g, name):
    T, D = x3.shape

    def body(x_ref, t_ref, g_ref, loss_ref, dg_ref, dx_ref, dxb_ref):
        i = pl.program_id(0)
        xf = x_ref[...]
        r = lax.rsqrt(jnp.mean(xf * xf, axis=-1, keepdims=True) + EPS)
        xh = xf * r
        gg = g_ref[...]
        err = xh * gg - t_ref[...]

        @pl.when(i == 0)
        def _():
            dg_ref[...] = jnp.zeros_like(dg_ref)
            loss_ref[...] = jnp.zeros_like(loss_ref)

        part = jnp.sum(jnp.sum(err * err, axis=-1, keepdims=True), axis=0, keepdims=True) * (0.5 / D)
        loss_ref[...] += jnp.broadcast_to(part, loss_ref.shape)
        dout = err * (1.0 / D)
        dg_ref[...] += jnp.sum(dout * xh, axis=0, keepdims=True)
        dxh = dout * gg
        dx = r * (dxh - xh * jnp.mean(dxh * xh, axis=-1, keepdims=True))
        dx_ref[...] = dx
        dxb_ref[...] = dx.astype(BF16)

    row = pl.BlockSpec((ROW_BLOCK, D), lambda i: (i, 0))
    vec = pl.BlockSpec((1, D), lambda i: (0, 0))
    return pl.pallas_call(
        body, grid=(T // ROW_BLOCK,), in_specs=[row, row, vec],
        out_specs=[pl.BlockSpec((1, 128), lambda i: (0, 0)), vec, row, row],
        out_shape=[jax.ShapeDtypeStruct((1, 128), F32), jax.ShapeDtypeStruct((1, D), F32),
                   jax.ShapeDtypeStruct((T, D), F32), jax.ShapeDtypeStruct((T, D), BF16)],
        name=name, compiler_params=_cparams(("arbitrary",)),
    )(x3, tgt, g)


CONV_BLOCK = 256


def _conv_apply(u, w, b):
    row = lax.broadcasted_iota(jnp.int32, u.shape, 0)
    acc = b + w[CONV_K - 1:CONV_K, :] * u
    shifted = []
    for j in range(1, CONV_K):
        uj = jnp.where(row >= j, pltpu.roll(u, j, axis=0), 0.0)
        shifted.append(uj)
        acc = acc + w[CONV_K - 1 - j:CONV_K - j, :] * uj
    return acc, shifted


def _conv_fwd(proj, conv_w, conv_b, name):
    T = proj.shape[0]
    cb0 = OFF_X // CONV_BLOCK

    def body(u_ref, w_ref, b_ref, o_ref):
        c, _ = _conv_apply(u_ref[...], w_ref[...], b_ref[...])
        o_ref[...] = c * _sigmoid(c)

    return pl.pallas_call(
        body, grid=(CONV_DIM // CONV_BLOCK,),
        in_specs=[pl.BlockSpec((T, CONV_BLOCK), lambda j: (0, cb0 + j)),
                  pl.BlockSpec((CONV_K, CONV_BLOCK), lambda j: (0, j)),
                  pl.BlockSpec((1, CONV_BLOCK), lambda j: (0, j))],
        out_specs=pl.BlockSpec((T, CONV_BLOCK), lambda j: (0, j)),
        out_shape=jax.ShapeDtypeStruct((T, CONV_DIM), F32), name=name, compiler_params=_cparams(("parallel",)),
    )(proj, conv_w, conv_b)


def _conv_bwd(proj, dact, conv_w, conv_b, dproj, name):
    T = proj.shape[0]
    cb0 = OFF_X // CONV_BLOCK

    def body(u_ref, d_ref, w_ref, b_ref, _, du_ref, dw_ref, db_ref):
        u = u_ref[...]
        w = w_ref[...]
        c, shifted = _conv_apply(u, w, b_ref[...])
        sg = _sigmoid(c)
        dc = d_ref[...] * sg * (1.0 + c * (1.0 - sg))
        row = lax.broadcasted_iota(jnp.int32, u.shape, 0)
        du = w[CONV_K - 1:CONV_K, :] * dc
        dw_ref[CONV_K - 1:CONV_K, :] = jnp.sum(dc * u, axis=0, keepdims=True)
        for j in range(1, CONV_K):
            dcj = jnp.where(row < T - j, pltpu.roll(dc, T - j, axis=0), 0.0)
            du = du + w[CONV_K - 1 - j:CONV_K - j, :] * dcj
            dw_ref[CONV_K - 1 - j:CONV_K - j, :] = jnp.sum(dc * shifted[j - 1], axis=0, keepdims=True)
        db_ref[...] = jnp.sum(dc, axis=0, keepdims=True)
        du_ref[...] = du.astype(BF16)

    return pl.pallas_call(
        body, grid=(CONV_DIM // CONV_BLOCK,),
        in_specs=[pl.BlockSpec((T, CONV_BLOCK), lambda j: (0, cb0 + j)),
                  pl.BlockSpec((T, CONV_BLOCK), lambda j: (0, j)),
                  pl.BlockSpec((CONV_K, CONV_BLOCK), lambda j: (0, j)),
                  pl.BlockSpec((1, CONV_BLOCK), lambda j: (0, j)), pl.BlockSpec(memory_space=pl.ANY)],
        out_specs=[pl.BlockSpec((T, CONV_BLOCK), lambda j: (0, cb0 + j)),
                   pl.BlockSpec((CONV_K, CONV_BLOCK), lambda j: (0, j)),
                   pl.BlockSpec((1, CONV_BLOCK), lambda j: (0, j))],
        out_shape=[jax.ShapeDtypeStruct(dproj.shape, BF16), jax.ShapeDtypeStruct((CONV_K, CONV_DIM), F32),
                   jax.ShapeDtypeStruct((1, CONV_DIM), F32)],
        input_output_aliases={4: 0}, name=name, compiler_params=_cparams(("parallel",)),
    )(proj, dact, conv_w, conv_b, dproj)


GROUP_W = D_INNER // N_GROUPS
HEADS_PER_GROUP = N_HEADS // N_GROUPS


def _expand_mat():
    h = lax.broadcasted_iota(jnp.int32, (N_HEADS, D_INNER), 0)
    j = lax.broadcasted_iota(jnp.int32, (N_HEADS, D_INNER), 1)
    return (j // HEAD_DIM == h).astype(F32)


def _reduce_mat(g):
    j = lax.broadcasted_iota(jnp.int32, (GROUP_W, N_HEADS), 0)
    h = lax.broadcasted_iota(jnp.int32, (GROUP_W, N_HEADS), 1)
    return (g * HEADS_PER_GROUP + j // HEAD_DIM == h).astype(F32)


def _col16(v, h):
    lane = lax.broadcasted_iota(jnp.int32, v.shape, 1)
    return jnp.sum(jnp.where(lane == h, v, 0.0), axis=1, keepdims=True)


def _ssd_pre(dt_raw, dtT_raw, dtb, dtbT, alog, alogT):
    Q = CHUNK
    xdt = dt_raw + dtb
    dt = _softplus(xdt)
    dtT = _softplus(dtT_raw + dtbT)
    A = -jnp.exp(alog)
    AT = -jnp.exp(alogT)
    row = lax.broadcasted_iota(jnp.int32, (Q, Q), 0)
    col = lax.broadcasted_iota(jnp.int32, (Q, Q), 1)
    tril = (row >= col).astype(F32)
    triu = (row <= col).astype(F32)
    cs = _hdot(tril, dt * A, "b")
    csT = _hdot(dtT * AT, triu, "a")
    return xdt, dt, A, cs, csT, row >= col, triu


def _decay_matrix(cs, csT, h, causal):
    seg = _col16(cs, h) - csT[h:h + 1, :]
    return jnp.where(causal, jnp.exp(jnp.minimum(seg, 0.0)), 0.0)


def _ssd_in_specs(nc, rev):
    def cidx(c):
        return (nc - 1 - c) if rev else c

    return [
        pl.BlockSpec((CHUNK, D_INNER), lambda c: (cidx(c), 0)),
        pl.BlockSpec((CHUNK, 512), lambda c: (cidx(c), 2)),
        pl.BlockSpec((CHUNK, 512), lambda c: (cidx(c), 3)),
        pl.BlockSpec((CHUNK, D_INNER), lambda c: (cidx(c), 0)),
        pl.BlockSpec((CHUNK, 128), lambda c: (cidx(c), OFF_DT // 128)),
        pl.BlockSpec((N_HEADS, CHUNK), lambda c: (0, cidx(c))),
        pl.BlockSpec((1, N_HEADS), lambda c: (0, 0)),
        pl.BlockSpec((N_HEADS, 1), lambda c: (0, 0)),
        pl.BlockSpec((1, N_HEADS), lambda c: (0, 0)),
        pl.BlockSpec((N_HEADS, 1), lambda c: (0, 0)),
        pl.BlockSpec((1, D_INNER), lambda c: (0, 0)),
        pl.BlockSpec((1, D_INNER), lambda c: (0, 0)),
    ]


def _ssd_fwd(xbc, proj, dtT, dtb, dtbT, alog, alogT, dfull, ng, name):
    T = xbc.shape[0]
    nc = T // CHUNK
    Q = CHUNK

    def body(xs_ref, B_ref, C_ref, z_ref, dt_ref, dtT_ref, dtb_ref, dtbT_ref, al_ref, alT_ref, df_ref, ng_ref,
             y_ref, ypre_ref, hs_ref, h_scr):
        c = pl.program_id(0)

        @pl.when(c == 0)
        def _():
            h_scr[...] = jnp.zeros_like(h_scr)

        _, dt, _, cs, csT, causal, _ = _ssd_pre(dt_ref[:, :N_HEADS], dtT_ref[...], dtb_ref[...], dtbT_ref[...],
                                                al_ref[...], alT_ref[...])
        ex = _expand_mat()
        dt_full = _hdot(dt, ex, "a")
        cs_full = _hdot(cs, ex, "a")
        cs_last = cs_full[Q - 1:Q, :]
        xs = xs_ref[...]
        xd = xs * dt_full
        e_full = jnp.exp(cs_full)
        dec_full = jnp.exp(cs_last - cs_full)
        cd_full = jnp.exp(cs_last)
        lane_head = lax.broadcasted_iota(jnp.int32, (1, GROUP_W), 1) // HEAD_DIM
        for g in range(N_GROUPS):
            sl = slice(g * GROUP_W, (g + 1) * GROUP_W)
            Bg = B_ref[:, g * D_STATE:(g + 1) * D_STATE].astype(BF16)
            Cg = C_ref[:, g * D_STATE:(g + 1) * D_STATE].astype(BF16)
            CB = _dot_nt(Cg, Bg)
            hg = h_scr[g]
            yoff = _dot_nn(Cg, hg.astype(BF16)) * e_full[:, sl]
            xd_g = xd[:, sl]
            S = _dot_tn(Bg, (xd_g * dec_full[:, sl]).astype(BF16))
            xd_b = xd_g.astype(BF16)
            ydiag = jnp.zeros((Q, GROUP_W), F32)
            for r in range(HEADS_PER_GROUP):
                Lm = _decay_matrix(cs, csT, g * HEADS_PER_GROUP + r, causal)
                Gm = (CB * Lm).astype(BF16)
                ydiag = ydiag + _dot_nn(Gm, jnp.where(lane_head == r, xd_b, jnp.zeros_like(xd_b)))
            hs_ref[0, g] = hg
            h_scr[g] = hg * cd_full[:, sl] + S
            ypre = ydiag + yoff + xs[:, sl] * df_ref[:, sl]
            ypre_ref[:, sl] = ypre
            zg = z_ref[:, sl]
            yz = ypre * zg * _sigmoid(zg)
            rn = lax.rsqrt(jnp.mean(yz * yz, axis=-1, keepdims=True) + EPS)
            y_ref[:, sl] = (yz * rn * ng_ref[:, sl]).astype(BF16)

    return pl.pallas_call(
        body, grid=(nc,), in_specs=_ssd_in_specs(nc, False),
        out_specs=[pl.BlockSpec((CHUNK, D_INNER), lambda c: (c, 0)),
                   pl.BlockSpec((CHUNK, D_INNER), lambda c: (c, 0)),
                   pl.BlockSpec((1, N_GROUPS, D_STATE, GROUP_W), lambda c: (c, 0, 0, 0))],
        out_shape=[jax.ShapeDtypeStruct((T, D_INNER + ATTN_W), BF16), jax.ShapeDtypeStruct((T, D_INNER), F32),
                   jax.ShapeDtypeStruct((nc, N_GROUPS, D_STATE, GROUP_W), F32)],
        scratch_shapes=[pltpu.VMEM((N_GROUPS, D_STATE, GROUP_W), F32)],
        name=name, compiler_params=_cparams(("arbitrary",)),
    )(xbc, xbc, xbc, proj, proj, dtT, dtb, dtbT, alog, alogT, dfull, ng)


def _ssd_bwd(xbc, proj, dtT, dtb, dtbT, alog, alogT, dfull, ng, ypre, hs, dy, name):
    T = xbc.shape[0]
    nc = T // CHUNK
    Q = CHUNK

    def body(xs_ref, B_ref, C_ref, z_ref, dt_ref, dtT_ref, dtb_ref, dtbT_ref, al_ref, alT_ref, df_ref, ng_ref,
             ypre_ref, hs_ref, dy_ref,
             dz_ref, dxbc_ref, ddtb_ref, dal_ref, dD_ref, dng_ref, dh_scr):
        step = pl.program_id(0)

        @pl.when(step == 0)
        def _():
            dh_scr[...] = jnp.zeros_like(dh_scr)
            ddtb_ref[...] = jnp.zeros_like(ddtb_ref)
            dal_ref[...] = jnp.zeros_like(dal_ref)
            dD_ref[...] = jnp.zeros_like(dD_ref)
            dng_ref[...] = jnp.zeros_like(dng_ref)

        xdt, dt, A, cs, csT, causal, triu = _ssd_pre(dt_ref[:, :N_HEADS], dtT_ref[...], dtb_ref[...],
                                                    dtbT_ref[...], al_ref[...], alT_ref[...])
        ex = _expand_mat()
        dt_full = _hdot(dt, ex, "a")
        cs_full = _hdot(cs, ex, "a")
        cs_last = cs_full[Q - 1:Q, :]
        xs = xs_ref[...]
        xd = xs * dt_full
        e_full = jnp.exp(cs_full)
        dec_full = jnp.exp(cs_last - cs_full)
        cd_full = jnp.exp(cs_last)
        lane_head = lax.broadcasted_iota(jnp.int32, (1, GROUP_W), 1) // HEAD_DIM
        is_last = lax.broadcasted_iota(jnp.int32, (Q, 1), 0) == Q - 1
        dcs16 = jnp.zeros((Q, N_HEADS), F32)
        ddtx16 = jnp.zeros((Q, N_HEADS), F32)
        dD16 = jnp.zeros((8, N_HEADS), F32)
        lane16 = lax.broadcasted_iota(jnp.int32, (1, N_HEADS), 1)
        sub16 = lax.broadcasted_iota(jnp.int32, (N_HEADS, 1), 0)
        col_sums = jnp.zeros((N_HEADS, Q), F32)
        for g in range(N_GROUPS):
            sl = slice(g * GROUP_W, (g + 1) * GROUP_W)
            red = _reduce_mat(g)
            ypre_g = ypre_ref[:, sl]
            zg = z_ref[:, sl]
            sg = _sigmoid(zg)
            silu = zg * sg
            yz = ypre_g * silu
            rn = lax.rsqrt(jnp.mean(yz * yz, axis=-1, keepdims=True) + EPS)
            yh = yz * rn
            dy_g = dy_ref[:, sl]
            dng_ref[:, sl] += jnp.sum(dy_g * yh, axis=0, keepdims=True)
            dyh = dy_g * ng_ref[:, sl]
            dyz = rn * (dyh - yh * jnp.mean(dyh * yh, axis=-1, keepdims=True))
            dY = dyz * silu
            dz_ref[:, sl] = (dyz * ypre_g * sg * (1.0 + zg * (1.0 - sg))).astype(BF16)
            xs_g = xs[:, sl]
            xd_g = xd[:, sl]
            dec_g = dec_full[:, sl]
            cd_g = cd_full[:, sl]
            d_g = df_ref[:, sl]
            Bg = B_ref[:, g * D_STATE:(g + 1) * D_STATE].astype(BF16)
            Cg = C_ref[:, g * D_STATE:(g + 1) * D_STATE].astype(BF16)
            CB = _dot_nt(Cg, Bg)
            hg = hs_ref[0, g]
            hgb = hg.astype(BF16)
            yoff = _dot_nn(Cg, hgb) * e_full[:, sl]
            dhn = dh_scr[g]
            dhnb = dhn.astype(BF16)
            dYE = (dY * e_full[:, sl]).astype(BF16)
            dC = _dot_nt(dYE, hgb)
            dh_direct = _dot_tn(Cg, dYE)
            dXdd = _dot_nn(Bg, dhnb)
            dB = _dot_nt((xd_g * dec_g).astype(BF16), dhnb)
            dcd = jnp.sum(dhn * hg, axis=0, keepdims=True)
            dh_scr[g] = dh_direct + cd_g * dhn
            dYb = dY.astype(BF16)
            xd_b = xd_g.astype(BF16)
            dCB = jnp.zeros((Q, Q), F32)
            dXd = dXdd * dec_g
            for r in range(HEADS_PER_GROUP):
                h = g * HEADS_PER_GROUP + r
                Lm = _decay_matrix(cs, csT, h, causal)
                Gf = CB * Lm
                dYr = jnp.where(lane_head == r, dYb, jnp.zeros_like(dYb))
                dG = _dot_nt(dYr, xd_b)
                dCB = dCB + dG * Lm
                dXd = dXd + _dot_tn(Gf.astype(BF16), dYr)
                Mm = dG * Gf
                dcs16 = dcs16 + jnp.where(lane16 == h, jnp.sum(Mm, axis=1, keepdims=True), 0.0)
                col_sums = col_sums + jnp.where(sub16 == h, jnp.sum(Mm, axis=0, keepdims=True), 0.0)
            dCBb = dCB.astype(BF16)
            dC = dC + _dot_nn(dCBb, Bg)
            dB = dB + _dot_tn(dCBb, Cg)
            w_state = dXdd * dec_g * xd_g
            t_last = jnp.sum(w_state, axis=0, keepdims=True) + dcd * cd_g
            dcs_g = dY * yoff - w_state + jnp.where(is_last, t_last, 0.0)
            dcs16 = dcs16 + _hdot(dcs_g, red, "a")
            ddtx16 = ddtx16 + _hdot(dXd * xs_g, red, "a")
            dD16 = dD16 + _hdot(jnp.broadcast_to(jnp.sum(dY * xs_g, axis=0, keepdims=True), (8, GROUP_W)), red, "a")
            dxbc_ref[:, sl] = dXd * dt_full[:, sl] + dY * d_g
            dxbc_ref[:, D_INNER + g * D_STATE:D_INNER + (g + 1) * D_STATE] = dB
            dxbc_ref[:, D_INNER + 512 + g * D_STATE:D_INNER + 512 + (g + 1) * D_STATE] = dC
        eye = (lax.broadcasted_iota(jnp.int32, (N_HEADS, N_HEADS), 0)
               == lax.broadcasted_iota(jnp.int32, (N_HEADS, N_HEADS), 1)).astype(BF16)
        dcs16 = dcs16 - sum(_dot_tn(part, eye) for part in _split3(col_sums))
        da = _hdot(triu, dcs16, "b")
        ddt = da * A + ddtx16
        ddt_raw = ddt * _sigmoid(xdt)
        pr = lax.broadcasted_iota(jnp.int32, (N_HEADS, 128), 0)
        pc = lax.broadcasted_iota(jnp.int32, (N_HEADS, 128), 1)
        dz_ref[:, D_INNER:OFF_DT] = jnp.zeros((Q, OFF_DT - D_INNER), BF16)
        dz_ref[:, OFF_DT:OFF_DT + 128] = _hdot(ddt_raw, (pr == pc).astype(F32), "a").astype(BF16)
        dz_ref[:, OFF_DT + 128:] = jnp.zeros((Q, NP - OFF_DT - 128), BF16)
        ddtb_ref[...] += jnp.sum(ddt_raw, axis=0, keepdims=True)
        dal_ref[...] += jnp.sum(da * dt, axis=0, keepdims=True) * A
        dD_ref[...] += dD16[0:1, :]

    def rc(c):
        return nc - 1 - c

    in_specs = _ssd_in_specs(nc, True) + [
        pl.BlockSpec((CHUNK, D_INNER), lambda c: (rc(c), 0)),
        pl.BlockSpec((1, N_GROUPS, D_STATE, GROUP_W), lambda c: (rc(c), 0, 0, 0)),
        pl.BlockSpec((CHUNK, D_INNER), lambda c: (rc(c), 0)),
    ]
    small = pl.BlockSpec((1, N_HEADS), lambda c: (0, 0))
    return pl.pallas_call(
        body, grid=(nc,), in_specs=in_specs,
        out_specs=[pl.BlockSpec((CHUNK, NP), lambda c: (rc(c), 0)),
                   pl.BlockSpec((CHUNK, CONV_DIM), lambda c: (rc(c), 0)),
                   small, small, small,
                   pl.BlockSpec((1, D_INNER), lambda c: (0, 0))],
        out_shape=[jax.ShapeDtypeStruct((T, NP), BF16), jax.ShapeDtypeStruct((T, CONV_DIM), F32),
                   jax.ShapeDtypeStruct((1, N_HEADS), F32), jax.ShapeDtypeStruct((1, N_HEADS), F32),
                   jax.ShapeDtypeStruct((1, N_HEADS), F32), jax.ShapeDtypeStruct((1, D_INNER), F32)],
        scratch_shapes=[pltpu.VMEM((N_GROUPS, D_STATE, GROUP_W), F32)],
        name=name, compiler_params=_cparams(("arbitrary",)),
    )(xbc, xbc, xbc, proj, proj, dtT, dtb, dtbT, alog, alogT, dfull, ng, ypre, hs, dy)


N_PAIRS = ATTN_W // 128
PAIRS_PER_KV = N_PAIRS // 2
ATTN_SCALE = HEAD_DIM ** -0.5


def _kv_variants(kk):
    lo = lax.broadcasted_iota(jnp.int32, kk.shape, 1) < HEAD_DIM
    zero = jnp.zeros_like(kk)
    k00 = jnp.where(lo, kk, zero)
    k11 = jnp.where(lo, zero, kk)
    k01 = pltpu.roll(k00, HEAD_DIM, axis=1)
    k10 = pltpu.roll(k11, HEAD_DIM, axis=1)
    return [[k00.astype(BF16), k01.astype(BF16)], [k10.astype(BF16), k11.astype(BF16)]]


LOG2E = 1.4426950408889634


def _own_block():
    i = lax.broadcasted_iota(jnp.int32, (WINDOW, WINDOW), 0)
    j = lax.broadcasted_iota(jnp.int32, (WINDOW, WINDOW), 1)
    return j <= i


def _fold(own, a):
    return jnp.where(own, a[:, WINDOW:], a[:, :WINDOW])


def _attn_probs(qp, kvar, own, prev_bias, sk):
    s = _dot_nt(qp, kvar)
    sb = jnp.where(own, s[:, WINDOW:], s[:, :WINDOW] + prev_bias) * (ATTN_SCALE * LOG2E)
    sk2 = sk * LOG2E
    m = jnp.maximum(jnp.max(sb, axis=1, keepdims=True), sk2)
    pe = jnp.exp2(sb - m)
    es = jnp.exp2(sk2 - m)
    den = jnp.sum(pe, axis=1, keepdims=True) + es
    inv = 1.0 / den
    return pe * inv, es * inv


def _unfold(own, a):
    zero = jnp.zeros_like(a)
    return jnp.where(own, zero, a), jnp.where(own, a, zero)


def _sink(sinks, r):
    lane = lax.broadcasted_iota(jnp.int32, sinks.shape, 1)
    return jnp.sum(jnp.where(lane == r, sinks, 0.0), axis=1, keepdims=True)


def _kv_specs():
    return [pl.BlockSpec((WINDOW, KV_W), lambda n: (jnp.maximum(n - 1, 0), OFF_K // KV_W)),
            pl.BlockSpec((WINDOW, KV_W), lambda n: (n, OFF_K // KV_W)),
            pl.BlockSpec((WINDOW, KV_W), lambda n: (jnp.maximum(n - 1, 0), OFF_V // KV_W)),
            pl.BlockSpec((WINDOW, KV_W), lambda n: (n, OFF_V // KV_W))]


def _attn_fwd(proj, sinks, og, ycat, name):
    T = proj.shape[0]
    nb = T // WINDOW

    def body(q_ref, kp_ref, kc_ref, vp_ref, vc_ref, s_ref, og_ref, _, y_ref, o_ref):
        n = pl.program_id(0)
        kv = _kv_variants(jnp.concatenate([kp_ref[...], kc_ref[...]], axis=0))
        vv = _kv_variants(jnp.concatenate([vp_ref[...], vc_ref[...]], axis=0))
        own = _own_block()
        prev_bias = jnp.where(n > 0, 0.0, NEG)
        sinks_v = s_ref[...]
        ssq = jnp.zeros((WINDOW, 1), F32)
        for p in range(N_PAIRS):
            j = p // PAIRS_PER_KV
            qp = q_ref[:, p * 128:(p + 1) * 128].astype(BF16)
            o_pair = jnp.zeros((WINDOW, 128), F32)
            for par in range(2):
                pn, _ = _attn_probs(qp, kv[j][par], own, prev_bias, _sink(sinks_v, 2 * p + par))
                p_prev, p_own = _unfold(own, pn.astype(BF16))
                o_pair = o_pair + _dot_nn(p_prev, vv[j][par][:WINDOW]) + _dot_nn(p_own, vv[j][par][WINDOW:])
            o_ref[:, p * 128:(p + 1) * 128] = o_pair
            ssq = ssq + jnp.sum(o_pair * o_pair, axis=1, keepdims=True)
        rn = lax.rsqrt(ssq * (1.0 / ATTN_W) + EPS)
        y_ref[...] = (o_ref[...] * rn * og_ref[...]).astype(BF16)

    return pl.pallas_call(
        body, grid=(nb,),
        in_specs=[pl.BlockSpec((WINDOW, ATTN_W), lambda n: (n, OFF_Q // ATTN_W)), *_kv_specs(),
                  pl.BlockSpec((1, N_HEADS), lambda n: (0, 0)), pl.BlockSpec((1, ATTN_W), lambda n: (0, 0)), ANY],
        out_specs=[pl.BlockSpec((WINDOW, ATTN_W), lambda n: (n, 1)), pl.BlockSpec((WINDOW, ATTN_W), lambda n: (n, 0))],
        out_shape=[jax.ShapeDtypeStruct(ycat.shape, BF16), jax.ShapeDtypeStruct((T, ATTN_W), F32)],
        input_output_aliases={7: 0}, name=name, compiler_params=_cparams(("parallel",)),
    )(proj, proj, proj, proj, proj, sinks, og, ycat)


def _attn_bwd(proj, sinks, og, o, dy, dproj, name):
    T = proj.shape[0]
    nb = T // WINDOW

    def body(q_ref, kp_ref, kc_ref, vp_ref, vc_ref, s_ref, og_ref, o_ref, dy_ref, _,
             dq_ref, dk_ref, dv_ref, ds_ref, dog_ref):
        n = pl.program_id(0)

        @pl.when(n == 0)
        def _():
            dk_ref[...] = jnp.zeros_like(dk_ref)
            dv_ref[...] = jnp.zeros_like(dv_ref)
            ds_ref[...] = jnp.zeros_like(ds_ref)
            dog_ref[...] = jnp.zeros_like(dog_ref)

        kv = _kv_variants(jnp.concatenate([kp_ref[...], kc_ref[...]], axis=0))
        vv = _kv_variants(jnp.concatenate([vp_ref[...], vc_ref[...]], axis=0))
        own = _own_block()
        prev_bias = jnp.where(n > 0, 0.0, NEG)
        sinks_v = s_ref[...]
        of = o_ref[...]
        rn = lax.rsqrt(jnp.mean(of * of, axis=-1, keepdims=True) + EPS)
        oh = of * rn
        dyf = dy_ref[...]
        dog_ref[...] += jnp.sum(dyf * oh, axis=0, keepdims=True)
        doh = dyf * og_ref[...]
        do = rn * (doh - oh * jnp.mean(doh * oh, axis=-1, keepdims=True))
        lane = lax.broadcasted_iota(jnp.int32, (1, 128), 1)
        lane16 = lax.broadcasted_iota(jnp.int32, (1, N_HEADS), 1)
        sub = lax.broadcasted_iota(jnp.int32, (128, 1), 0)
        dk_acc = [[[jnp.zeros((128, WINDOW), F32) for _ in range(2)] for _ in range(2)] for _ in range(2)]
        dv_acc = [[[jnp.zeros((128, WINDOW), F32) for _ in range(2)] for _ in range(2)] for _ in range(2)]
        dsink = jnp.zeros((1, N_HEADS), F32)
        for p in range(N_PAIRS):
            j = p // PAIRS_PER_KV
            q_f = q_ref[:, p * 128:(p + 1) * 128]
            qp = q_f.astype(BF16)
            q_t = q_f.T.astype(BF16)
            do_p = do[:, p * 128:(p + 1) * 128]
            o_p = of[:, p * 128:(p + 1) * 128]
            do_b = do_p.astype(BF16)
            do_t = do_p.T.astype(BF16)
            prod = do_p * o_p
            dq_pair = jnp.zeros((WINDOW, 128), F32)
            for par in range(2):
                r = 2 * p + par
                half = (lane < HEAD_DIM) if par == 0 else (lane >= HEAD_DIM)
                rows_half = (sub < HEAD_DIM) if par == 0 else (sub >= HEAD_DIM)
                pn, ps = _attn_probs(qp, kv[j][par], own, prev_bias, _sink(sinks_v, r))
                delta = jnp.sum(jnp.where(half, prod, 0.0), axis=1, keepdims=True)
                dP = _fold(own, _dot_nt(do_b, vv[j][par]))
                dS = pn * (dP - delta)
                dsink = dsink + jnp.where(lane16 == r, -jnp.sum(ps * delta, axis=0, keepdims=True), 0.0)
                dS_parts = _unfold(own, (dS * ATTN_SCALE).astype(BF16))
                p_parts = _unfold(own, pn.astype(BF16))
                q_th = jnp.where(rows_half, q_t, jnp.zeros_like(q_t))
                do_th = jnp.where(rows_half, do_t, jnp.zeros_like(do_t))
                for blk in range(2):
                    dq_pair = dq_pair + _dot_nn(dS_parts[blk], kv[j][par][blk * WINDOW:(blk + 1) * WINDOW])
                    dk_acc[j][par][blk] = dk_acc[j][par][blk] + _dot_nn(q_th, dS_parts[blk])
                    dv_acc[j][par][blk] = dv_acc[j][par][blk] + _dot_nn(do_th, p_parts[blk])
            dq_ref[:, p * 128:(p + 1) * 128] = dq_pair.astype(BF16)
        rows = [pl.multiple_of(jnp.maximum(n - 1, 0) * WINDOW, WINDOW), pl.multiple_of(n * WINDOW, WINDOW)]
        for acc, ref in [(dk_acc, dk_ref), (dv_acc, dv_ref)]:
            for blk in range(2):
                both_t = (acc[0][0][blk] + pltpu.roll(acc[0][1][blk], HEAD_DIM, axis=0)
                          + acc[1][1][blk] + pltpu.roll(acc[1][0][blk], HEAD_DIM, axis=0))
                ref[pl.ds(rows[blk], WINDOW), :] += both_t.T
        ds_ref[...] += dsink

    full_kv = pl.BlockSpec((T, KV_W), lambda n: (0, 0))
    blk = pl.BlockSpec((WINDOW, ATTN_W), lambda n: (n, 0))
    return pl.pallas_call(
        body, grid=(nb,),
        in_specs=[pl.BlockSpec((WINDOW, ATTN_W), lambda n: (n, OFF_Q // ATTN_W)), *_kv_specs(),
                  pl.BlockSpec((1, N_HEADS), lambda n: (0, 0)), pl.BlockSpec((1, ATTN_W), lambda n: (0, 0)),
                  blk, pl.BlockSpec((WINDOW, ATTN_W), lambda n: (n, 1)), ANY],
        out_specs=[pl.BlockSpec((WINDOW, ATTN_W), lambda n: (n, OFF_Q // ATTN_W)), full_kv, full_kv,
                   pl.BlockSpec((1, N_HEADS), lambda n: (0, 0)), pl.BlockSpec((1, ATTN_W), lambda n: (0, 0))],
        out_shape=[jax.ShapeDtypeStruct(dproj.shape, BF16), jax.ShapeDtypeStruct((T, KV_W), F32),
                   jax.ShapeDtypeStruct((T, KV_W), F32), jax.ShapeDtypeStruct((1, N_HEADS), F32),
                   jax.ShapeDtypeStruct((1, ATTN_W), F32)],
        input_output_aliases={9: 0}, name=name, compiler_params=_cparams(("arbitrary",)),
    )(proj, proj, proj, proj, proj, sinks, og, o, dy, dproj)


ANY = pl.BlockSpec(memory_space=pl.ANY)


def _coords():
    return lax.axis_index("x"), lax.axis_index("y"), lax.axis_index("c")


def _all_gather(arrs, name):
    n = len(arrs)

    def body(*refs):
        ins, outs = refs[:n], refs[n:2 * n]
        send_sems, recv_sems, local_sems = refs[2 * n:]
        x, y, c = _coords()
        me = 4 * x + 2 * y + c
        sibling = (x, y, 1 - c)
        chips = [(1 - x, y), (x, 1 - y), (1 - x, 1 - y)]

        def copy(a, k, block, to, src=None):
            dst = outs[a].at[block]
            return pltpu.make_async_remote_copy(
                src_ref=dst if src is None else src, dst_ref=dst, send_sem=send_sems.at[a, k],
                recv_sem=recv_sems.at[a, k], device_id=to, device_id_type=MESH)

        mine = [pltpu.make_async_copy(ins[a], outs[a].at[me], local_sems.at[a]) for a in range(n)]
        for cp in mine:
            cp.start()
        first = []
        for a in range(n):
            first.append(copy(a, 0, me, sibling, src=ins[a]))
            for j, chip in enumerate(chips):
                first.append(copy(a, 1 + j, me, (*chip, c), src=ins[a]))
        for cp in first:
            cp.start()
        passed = []
        for j, (px, py) in enumerate(chips):
            blk = 4 * px + 2 * py + c
            for a in range(n):
                copy(a, 1 + j, blk, sibling).wait_recv()
                fwd = copy(a, 4 + j, blk, sibling)
                fwd.start()
                passed.append(fwd)
        for a in range(n):
            copy(a, 0, 4 * x + 2 * y + (1 - c), sibling).wait_recv()
            for j, (px, py) in enumerate(chips):
                copy(a, 4 + j, 4 * px + 2 * py + (1 - c), sibling).wait_recv()
        for cp in first + passed:
            cp.wait_send()
        for cp in mine:
            cp.wait()

    return pl.pallas_call(
        body, in_specs=[ANY] * n, out_specs=[ANY] * n,
        out_shape=[jax.ShapeDtypeStruct((N_DEV,) + a.shape, a.dtype) for a in arrs],
        scratch_shapes=[pltpu.SemaphoreType.DMA((n, 7)), pltpu.SemaphoreType.DMA((n, 7)),
                        pltpu.SemaphoreType.DMA((n,))],
        name=name,
    )(*arrs)


def _exchange_pair(arrs, name):
    n = len(arrs)

    def body(*refs):
        ins, outs = refs[:n], refs[n:2 * n]
        send_sems, recv_sems = refs[2 * n:]
        x, y, c = _coords()
        cps = []
        for a in range(n):
            for q in range(4):
                cps.append(pltpu.make_async_remote_copy(
                    src_ref=ins[a].at[2 * q + (1 - c)], dst_ref=outs[a].at[q], send_sem=send_sems.at[a, q],
                    recv_sem=recv_sems.at[a, q], device_id=(x, y, 1 - c), device_id_type=MESH))
        for cp in cps:
            cp.start()
        for cp in cps:
            cp.wait()

    return pl.pallas_call(
        body, in_specs=[ANY] * n, out_specs=[ANY] * n,
        out_shape=[jax.ShapeDtypeStruct((4,) + a.shape[1:], a.dtype) for a in arrs],
        scratch_shapes=[pltpu.SemaphoreType.DMA((n, 4)), pltpu.SemaphoreType.DMA((n, 4))],
        name=name,
    )(*arrs)


def _exchange_chips(arrs, name):
    n = len(arrs)

    def body(*refs):
        ins, outs = refs[:n], refs[n:2 * n]
        send_sems, recv_sems = refs[2 * n:]
        x, y, c = _coords()
        chips = [(1 - x, y), (x, 1 - y), (1 - x, 1 - y)]
        cps = []
        for a in range(n):
            for k, (tx, ty) in enumerate(chips):
                cps.append(pltpu.make_async_remote_copy(
                    src_ref=ins[a].at[2 * tx + ty], dst_ref=outs[a].at[k], send_sem=send_sems.at[a, k],
                    recv_sem=recv_sems.at[a, k], device_id=(tx, ty, c), device_id_type=MESH))
        for cp in cps:
            cp.start()
        for cp in cps:
            cp.wait()

    return pl.pallas_call(
        body, in_specs=[ANY] * n, out_specs=[ANY] * n,
        out_shape=[jax.ShapeDtypeStruct((3,) + a.shape[1:], a.dtype) for a in arrs],
        scratch_shapes=[pltpu.SemaphoreType.DMA((n, 3)), pltpu.SemaphoreType.DMA((n, 3))],
        name=name,
    )(*arrs)


HBM = pl.BlockSpec(memory_space=pltpu.HBM)
SEM = pl.BlockSpec(memory_space=pltpu.SEMAPHORE)
EFFECT = pltpu.SideEffectType.DATAFLOW_SIDE_EFFECTING


def _in_hbm(a):
    return pltpu.with_memory_space_constraint(a, pltpu.HBM)


def _remote_start(srcs, lands, plan, n_copies, name, after=None):
    ns, nb = len(srcs), len(srcs) + len(lands)
    n_after = 0 if after is None else 1

    def body(*refs):
        src_refs, land_refs = refs[:ns], refs[ns:nb]
        send_sems, recv_sems = refs[nb + n_after], refs[nb + n_after + 1]
        token = refs[-1]
        x, y, c = _coords()
        for i, (sv, dv, dev) in enumerate(plan(src_refs, land_refs, x, y, c)):
            pltpu.make_async_remote_copy(src_ref=sv, dst_ref=dv, send_sem=send_sems.at[i], recv_sem=recv_sems.at[i],
                                         device_id=dev, device_id_type=MESH).start()
        token[...] = jnp.zeros_like(token)

    bufs = list(srcs) + list(lands)
    outs = pl.pallas_call(
        body, name=name,
        out_shape=(pltpu.SemaphoreType.DMA((n_copies,)), pltpu.SemaphoreType.DMA((n_copies,)),
                   *[pltpu.HBM(b.shape, b.dtype) for b in bufs], jax.ShapeDtypeStruct((8, 128), F32)),
        in_specs=[HBM] * nb + [ANY] * n_after,
        out_specs=(SEM, SEM, *[HBM] * nb, pl.BlockSpec(memory_space=pltpu.VMEM)),
        input_output_aliases={i: 2 + i for i in range(nb)},
        compiler_params=pltpu.CompilerParams(has_side_effects=EFFECT),
    )(*[_in_hbm(b) for b in bufs], *([] if after is None else [after]))
    return outs[0], outs[1], list(outs[2:2 + ns]), list(outs[2 + ns:2 + nb]), outs[-1]


def _remote_wait(started, after, plan, name):
    send_sems, recv_sems, srcs, lands, _ = started
    ns, nb = len(srcs), len(srcs) + len(lands)

    def body(*refs):
        src_refs, land_refs = refs[:ns], refs[ns:nb]
        send_sems, recv_sems = refs[nb], refs[nb + 1]
        x, y, c = _coords()
        for i, (sv, dv, dev) in enumerate(plan(src_refs, land_refs, x, y, c)):
            cp = pltpu.make_async_remote_copy(src_ref=sv, dst_ref=dv, send_sem=send_sems.at[i],
                                              recv_sem=recv_sems.at[i], device_id=dev, device_id_type=MESH)
            cp.wait_send()
            cp.wait_recv()

    bufs = list(srcs) + list(lands)
    outs = pl.pallas_call(
        body, name=name, out_shape=tuple(pltpu.HBM(b.shape, b.dtype) for b in bufs),
        in_specs=[HBM] * nb + [SEM, SEM, ANY], out_specs=tuple([HBM] * nb),
        input_output_aliases={i: i for i in range(nb)},
        compiler_params=pltpu.CompilerParams(has_side_effects=EFFECT),
    )(*bufs, send_sems, recv_sems, after)
    return list(outs[:ns]), list(outs[ns:])


def _gather_plan(src_refs, land_refs, x, y, c):
    me = 4 * x + 2 * y + c
    plan = []
    for s, l in zip(src_refs, land_refs):
        for dev in [(x, y, 1 - c), (1 - x, y, c), (x, 1 - y, c), (1 - x, 1 - y, c)]:
            plan.append((s, l.at[me], dev))
    return plan


def _forward_plan(src_refs, land_refs, x, y, c):
    plan = []
    for l in land_refs:
        for px, py in [(1 - x, y), (x, 1 - y), (1 - x, 1 - y)]:
            blk = l.at[4 * px + 2 * py + c]
            plan.append((blk, blk, (x, y, 1 - c)))
    return plan


def _pair_plan(src_refs, land_refs, x, y, c):
    plan = []
    for s, l in zip(src_refs, land_refs):
        for q in range(4):
            plan.append((s.at[2 * q + (1 - c)], l.at[q], (x, y, 1 - c)))
    return plan


def _pair4_plan(src_refs, land_refs, x, y, c):
    plan = []
    for s, l in zip(src_refs, land_refs):
        for q in range(4):
            plan.append((s.at[q], l.at[q], (x, y, 1 - c)))
    return plan


def _chips_plan(src_refs, land_refs, x, y, c):
    plan = []
    for s, l in zip(src_refs, land_refs):
        for k, (tx, ty) in enumerate([(1 - x, y), (x, 1 - y), (1 - x, 1 - y)]):
            plan.append((s.at[2 * tx + ty], l.at[k], (tx, ty, c)))
    return plan


def _everyone_plan(src_refs, land_refs, x, y, c):
    me = 4 * x + 2 * y + c
    plan = []
    for s, l in zip(src_refs, land_refs):
        for fx, fy, fc in [(0, 0, 1), (1, 0, 0), (1, 0, 1), (0, 1, 0), (0, 1, 1), (1, 1, 0), (1, 1, 1)]:
            dev = ((1 - x) if fx else x, (1 - y) if fy else y, (1 - c) if fc else c)
            plan.append((s, l.at[me], dev))
    return plan


def _gather_finish(gathered, name):
    n = len(gathered)

    def body(*refs):
        outs = refs[n:2 * n]
        send_sems, recv_sems = refs[2 * n:]
        x, y, c = _coords()
        cps = []
        for a in range(n):
            for j, (px, py) in enumerate([(1 - x, y), (x, 1 - y), (1 - x, 1 - y)]):
                blk = outs[a].at[4 * px + 2 * py + c]
                got = outs[a].at[4 * px + 2 * py + (1 - c)]
                cps.append((pltpu.make_async_remote_copy(
                    src_ref=blk, dst_ref=blk, send_sem=send_sems.at[a, j], recv_sem=recv_sems.at[a, j],
                    device_id=(x, y, 1 - c), device_id_type=MESH), pltpu.make_async_remote_copy(
                    src_ref=got, dst_ref=got, send_sem=send_sems.at[a, j], recv_sem=recv_sems.at[a, j],
                    device_id=(x, y, 1 - c), device_id_type=MESH)))
        for cp, _ in cps:
            cp.start()
        for cp, arrival in cps:
            cp.wait_send()
            arrival.wait_recv()

    return pl.pallas_call(
        body, in_specs=[ANY] * n, out_specs=[ANY] * n,
        out_shape=[jax.ShapeDtypeStruct(g.shape, g.dtype) for g in gathered],
        input_output_aliases={a: a for a in range(n)},
        scratch_shapes=[pltpu.SemaphoreType.DMA((n, 3)), pltpu.SemaphoreType.DMA((n, 3))],
        name=name,
    )(*gathered)


def _pair_add(g8, r1, csel, tr, name):
    _, R, C = r1.shape
    g4 = g8.reshape(4, 2, R, C)

    def body(c_ref, g_ref, r_ref, o_ref):
        o_ref[...] = (g_ref[...].astype(F32) + r_ref[...].astype(F32)).astype(BF16)

    return pl.pallas_call(
        body,
        grid_spec=pltpu.PrefetchScalarGridSpec(
            num_scalar_prefetch=1, grid=(4, R // tr),
            in_specs=[pl.BlockSpec((None, None, tr, C), lambda q, i, cs: (q, cs[0], i, 0)),
                      pl.BlockSpec((None, tr, C), lambda q, i, cs: (q, i, 0))],
            out_specs=pl.BlockSpec((None, tr, C), lambda q, i, cs: (q, i, 0))),
        out_shape=jax.ShapeDtypeStruct((4, R, C), BF16), name=name,
        compiler_params=_cparams(("parallel", "parallel")),
    )(csel, g4, r1)


def _adamw_math(w, g, m, v):
    m = ADAM_B1 * m + (1.0 - ADAM_B1) * g
    v = ADAM_B2 * v + (1.0 - ADAM_B2) * (g * g)
    m_hat = m / (1.0 - ADAM_B1 ** ADAM_STEP)
    v_hat = v / (1.0 - ADAM_B2 ** ADAM_STEP)
    delta = -ADAM_LR * (m_hat / (jnp.sqrt(v_hat) + ADAM_EPS) + ADAM_WD * w)
    return delta, m, v


def _adamw_big(w, m, v, p4, r3, qsel, tile, name):
    R, C = w.shape
    tr, tc = tile

    def body(q_ref, w_ref, m_ref, v_ref, p_ref, r_ref, g_out, d_out, m_out, v_out):
        g = p_ref[...].astype(F32) + r_ref[0].astype(F32) + r_ref[1].astype(F32) + r_ref[2].astype(F32)
        d, mn, vn = _adamw_math(w_ref[...], g, m_ref[...], v_ref[...])
        g_out[...] = g
        d_out[...] = d
        m_out[...] = mn
        v_out[...] = vn

    blk = pl.BlockSpec((tr, tc), lambda i, j, qs: (i, j))
    return pl.pallas_call(
        body,
        grid_spec=pltpu.PrefetchScalarGridSpec(
            num_scalar_prefetch=1, grid=(R // tr, C // tc),
            in_specs=[blk, blk, blk, pl.BlockSpec((None, tr, tc), lambda i, j, qs: (qs[0], i, j)),
                      pl.BlockSpec((3, tr, tc), lambda i, j, qs: (0, i, j))],
            out_specs=[blk, blk, blk, blk]),
        out_shape=[jax.ShapeDtypeStruct((R, C), F32)] * 4, name=name,
        compiler_params=_cparams(("parallel", "parallel")),
    )(qsel, w, m, v, p4, r3)


def _sum_partials(p4, r3, qsel, tc, name):
    _, R, C = p4.shape

    def body(q_ref, p_ref, r_ref, o_ref):
        o_ref[...] = p_ref[...].astype(F32) + r_ref[0].astype(F32) + r_ref[1].astype(F32) + r_ref[2].astype(F32)

    return pl.pallas_call(
        body,
        grid_spec=pltpu.PrefetchScalarGridSpec(
            num_scalar_prefetch=1, grid=(C // tc,),
            in_specs=[pl.BlockSpec((None, R, tc), lambda j, qs: (qs[0], 0, j)),
                      pl.BlockSpec((3, R, tc), lambda j, qs: (0, 0, j))],
            out_specs=pl.BlockSpec((R, tc), lambda j, qs: (0, j))),
        out_shape=jax.ShapeDtypeStruct((R, C), F32), name=name, compiler_params=_cparams(("parallel",)),
    )(qsel, p4, r3)


def _adamw_tiled(w, g, m, v, tc, name):
    R, C = w.shape

    def body(w_ref, g_ref, m_ref, v_ref, d_out, m_out, v_out):
        d, mn, vn = _adamw_math(w_ref[...], g_ref[...], m_ref[...], v_ref[...])
        d_out[...] = d
        m_out[...] = mn
        v_out[...] = vn

    blk = pl.BlockSpec((R, tc), lambda j: (0, j))
    return pl.pallas_call(
        body, grid=(C // tc,), in_specs=[blk] * 4, out_specs=[blk] * 3,
        out_shape=[jax.ShapeDtypeStruct((R, C), F32)] * 3, name=name, compiler_params=_cparams(("parallel",)),
    )(w, g, m, v)


def _small_sum(parts, name):
    def body(p_ref, o_ref):
        acc = p_ref[0]
        for d in range(1, N_DEV):
            acc = acc + p_ref[d]
        o_ref[...] = acc

    return pl.pallas_call(
        body, out_shape=jax.ShapeDtypeStruct(parts.shape[1:], F32), name=name,
        compiler_params=_cparams(),
    )(parts)


def _adamw_small(w, g, m, v, name):
    def body(w_ref, g_ref, m_ref, v_ref, d_out, m_out, v_out):
        d, mn, vn = _adamw_math(w_ref[...], g_ref[...], m_ref[...], v_ref[...])
        d_out[...] = d
        m_out[...] = mn
        v_out[...] = vn

    return pl.pallas_call(
        body, out_shape=[jax.ShapeDtypeStruct(w.shape, F32)] * 3, name=name, compiler_params=_cparams(),
    )(w, g, m, v)


def _row(*pieces):
    r = jnp.concatenate([p.reshape(1, -1) for p in pieces], axis=1)
    return jnp.pad(r, ((0, 0), (0, D_MODEL - r.shape[1])))


def _pack_small(mix, convb, ssmg, attng, mlpg, fing, convw, dtb, alog, dsk, sinks, extra=None):
    last = [dtb, alog, dsk, sinks] + ([extra] if extra is not None else [])
    rows = [_row(mix), _row(convb), _row(ssmg, attng), _row(mlpg), _row(fing),
            jnp.pad(convw, ((0, 0), (0, D_MODEL - convw.shape[1]))), _row(*last)]
    packed = jnp.concatenate(rows, axis=0)
    return jnp.pad(packed, ((0, SMALL_ROWS - packed.shape[0]), (0, 0)))


def _unpack_small(p, conv_n):
    return dict(
        mix_norm_g=p[0:1, :], conv_b=p[1:2, :], ssm_norm_g=p[2:3, :D_INNER], attn_out_norm_g=p[2:3, D_INNER:],
        mlp_norm_g=p[3:4, :], final_norm_g=p[4, :], conv_w=p[5:9, :conv_n][None],
        dt_bias=p[9:10, 0:16], A_log=p[9:10, 16:32], D_skip=p[9:10, 32:48], attn_sinks=p[9:10, 48:64])


SMALL_NAMES = ["mix_norm_g", "conv_w", "conv_b", "dt_bias", "A_log", "D_skip", "ssm_norm_g", "attn_sinks",
               "attn_out_norm_g", "mlp_norm_g", "final_norm_g"]
WEIGHT_ORDER = ["mix_norm_g", "w_in", "conv_w", "conv_b", "dt_bias", "A_log", "D_skip", "ssm_norm_g", "attn_sinks",
                "attn_out_norm_g", "w_out", "mlp_norm_g", "w_up", "w_down", "final_norm_g"]


def _to_my_columns(w_nat):
    pad = jnp.zeros((w_nat.shape[0], NP - IN_PROJ), w_nat.dtype)
    return jnp.concatenate([w_nat[:, :NAT_DT], w_nat[:, NAT_DT + N_HEADS:], w_nat[:, NAT_DT:NAT_DT + N_HEADS], pad],
                           axis=1)


PER = IN_PROJ // N_DEV
SUPER_STEP = 544
SUPER = 576


def _natural_rows(g, lo, hi):
    segments = [(0, NAT_DT, 0), (NAT_DT, NAT_DT + N_HEADS, OFF_DT - NAT_DT), (NAT_DT + N_HEADS, IN_PROJ, -N_HEADS),
                (IN_PROJ, NP, 0)]
    pieces = [g[max(lo, a) + shift:min(hi, b) + shift] for a, b, shift in segments if max(lo, a) < min(hi, b)]
    return pieces[0] if len(pieces) == 1 else jnp.concatenate(pieces, axis=0)


def _w_in_from_super_slabs(sup):
    seam = SUPER - SUPER_STEP
    units = []
    for i in range(N_DEV):
        base = SUPER_STEP * i
        units.append((base, base + seam, sup[i, :seam] if i == 0 else sup[i - 1, SUPER_STEP:] + sup[i, :seam]))
        units.append((base + seam, base + SUPER_STEP, sup[i, seam:SUPER_STEP]))
    units.append((SUPER_STEP * N_DEV, SUPER_STEP * N_DEV + seam, sup[N_DEV - 1, SUPER_STEP:]))

    def natural(lo, hi):
        return [rows[max(lo, a) - a:min(hi, b) - a] for a, b, rows in units if max(lo, a) < min(hi, b)]

    pieces = natural(0, NAT_DT) + natural(NAT_DT + N_HEADS, IN_PROJ) + natural(NAT_DT, NAT_DT + N_HEADS)
    return jnp.concatenate(pieces + [jnp.zeros((NP - IN_PROJ, D_MODEL), sup.dtype)], axis=0)


def _to_natural_columns(w_my):
    return jnp.concatenate([w_my[:, :NAT_DT], w_my[:, OFF_DT:OFF_DT + N_HEADS], w_my[:, NAT_DT:OFF_DT]], axis=1)


SLAB = 1024


def _grad_w_up(h2, du, name, sel=None, add=None, after=None):
    T, D = h2.shape
    if sel is None:
        pick, n_slab, pre = (lambda j, *cs: j), N_DEV, None
    else:
        pre, other = sel
        pick, n_slab = (lambda j, cs: 2 * j + ((1 - cs[0]) if other else cs[0])), 4
    o_spec = pl.BlockSpec((None, SLAB, SLAB), lambda i, j, k, *cs: (j, i, 0))
    return _matmul(
        h2, du, mode="tn", grid=(D // SLAB, n_slab, 1),
        a_spec=pl.BlockSpec((T, SLAB), lambda i, j, k, *cs: (0, i)),
        b_spec=pl.BlockSpec((T, SLAB), lambda i, j, k, *cs: (0, pick(j, *cs))),
        out_shapes=[jax.ShapeDtypeStruct((n_slab, D, SLAB), BF16)], out_specs=[o_spec], tile=(SLAB, SLAB), name=name,
        extras=() if add is None else (add,), extra_specs=() if add is None else (o_spec,),
        epilogue=None if add is None else (lambda acc, r: (acc + r.astype(F32),)), after=after, prefetch=pre)[0]


def _grad_w_down(act, dx3b, name, sel=None, add=None, after=None):
    T, D = dx3b.shape
    if sel is None:
        pick, n_slab, pre = (lambda i, *cs: i), N_DEV, None
    else:
        pre, other = sel
        pick, n_slab = (lambda i, cs: 2 * i + ((1 - cs[0]) if other else cs[0])), 4
    o_spec = pl.BlockSpec((None, SLAB, SLAB), lambda i, j, k, *cs: (i, 0, j))
    return _matmul(
        act, dx3b, mode="tn", grid=(n_slab, D // SLAB, 1),
        a_spec=pl.BlockSpec((T, SLAB), lambda i, j, k, *cs: (0, pick(i, *cs))),
        b_spec=pl.BlockSpec((T, SLAB), lambda i, j, k, *cs: (0, j)),
        out_shapes=[jax.ShapeDtypeStruct((n_slab, SLAB, D), BF16)], out_specs=[o_spec], tile=(SLAB, SLAB), name=name,
        extras=() if add is None else (add,), extra_specs=() if add is None else (o_spec,),
        epilogue=None if add is None else (lambda acc, r: (acc + r.astype(F32),)), after=after, prefetch=pre)[0]


class _FixedWeights:
    def __init__(self, w_in_p, w_out_f, w_up_s, w_down_f, conv_w_f):
        self.w = (w_in_p, w_out_f, w_up_s, w_down_f, conv_w_f)
        self.grads = {}

    def mixer_weights(self, after):
        return self.w[0], self.w[4], None

    def prefetch(self, k, after):
        return None

    def out_weight(self, after):
        return self.w[1]

    def up_weight(self, after):
        return self.w[2]

    def down_weight(self, after):
        return self.w[3]

    def mlp_grads(self, h2, du, act, dx3b):
        self.grads.update(w_up=_grad_w_up(h2, du, "grad_w_up"),
                          w_down=_grad_w_down(act, dx3b, "grad_w_down").reshape(D_FF, D_MODEL))
        return None

    def out_grad(self, g_out):
        self.grads.update(w_out=g_out)
        return None

    def in_grad(self, g_in):
        self.grads.update(w_in=g_in)
        return None


def _local_step(x, tgt, p, hooks):
    T = x.shape[0]
    D = D_MODEL
    h1 = _rmsnorm_fwd(x, p["mix_norm_g"], "norm_mix")
    w_in_t, conv_w_f, token = hooks.mixer_weights(h1)
    (proj,) = _mm_simple(h1, w_in_t, mode="nt", M=T, N=NP, K=D, tm=min(T, 1024), tn=1536, tk=D, out_dtype=F32,
                         name="in_proj", after=token)
    xbc = _conv_fwd(proj, conv_w_f, p["conv_b"], "conv_fwd")
    dtT = proj[:, OFF_DT:OFF_DT + N_HEADS].T
    dtbT = p["dt_bias"].T
    alogT = p["A_log"].T
    dfull = jnp.repeat(p["D_skip"], HEAD_DIM, axis=1)
    token = hooks.prefetch("out", xbc)
    ssm_g = p["ssm_norm_g"] if token is None else p["ssm_norm_g"] + token[0:1, 0:1]
    ycat, ypre, hs = _ssd_fwd(xbc, proj, dtT, p["dt_bias"], dtbT, p["A_log"], alogT, dfull, ssm_g, "ssd_fwd")
    ycat, o_att = _attn_fwd(proj, p["attn_sinks"], p["attn_out_norm_g"], ycat, "attn_fwd")
    token = hooks.prefetch("up", ycat)
    w_out_f = hooks.out_weight(ycat if token is None else token)
    tm = min(T, 1024)
    (x2,) = _mm_simple(ycat, w_out_f, mode="nn", M=T, N=D, K=D, tm=tm, tn=1024, tk=D, out_dtype=F32, name="out_proj",
                       extras=(x,), epilogue=lambda acc, res: (acc + res,))
    h2 = _rmsnorm_fwd(x2, p["mlp_norm_g"], "norm_mlp")
    w_up_s = hooks.up_weight(h2)
    grid = (T // tm, N_DEV, 1)
    u, act = _matmul(
        h2, w_up_s, mode="nn", grid=grid,
        a_spec=pl.BlockSpec((tm, D), lambda i, j, k: (i, 0)),
        b_spec=pl.BlockSpec((None, D, 1024), lambda i, j, k: (j, 0, 0)),
        out_shapes=[jax.ShapeDtypeStruct((T, D_FF), F32), jax.ShapeDtypeStruct((T, D_FF), BF16)],
        out_specs=[pl.BlockSpec((tm, 1024), lambda i, j, k: (i, j))] * 2, tile=(tm, 1024), name="mlp_up",
        epilogue=lambda acc: (acc, jnp.square(jnp.maximum(acc, 0.0))))
    w_down_f = hooks.down_weight(act)
    (x3,) = _mm_simple(act, w_down_f, mode="nn", M=T, N=D, K=D_FF, tm=tm, tn=1024, tk=2048, out_dtype=F32,
                       name="mlp_down", extras=(x2,), epilogue=lambda acc, res: (acc + res,))
    loss_part, d_fin, dx3, dx3b = _final_loss(x3, tgt, p["final_norm_g"].reshape(1, D), "loss_head")
    (du,) = _mm_simple(dx3b, w_down_f, mode="nt", M=T, N=D_FF, K=D, tm=tm, tn=1024, tk=D, out_dtype=BF16,
                       name="mlp_down_bwd", extras=(u,),
                       epilogue=lambda acc, uu: (acc * (2.0 * jnp.maximum(uu, 0.0)),))
    token = hooks.mlp_grads(h2, du, act, dx3b)
    (dh2,) = _matmul(
        du, w_up_s, mode="nt", grid=(T // tm, D // 1024, N_DEV // 2),
        a_spec=pl.BlockSpec((tm, 2048), lambda i, j, k: (i, k)),
        b_spec=pl.BlockSpec((2, 1024, 1024), lambda i, j, k: (k, j, 0)),
        out_shapes=[jax.ShapeDtypeStruct((T, D), F32)],
        out_specs=[pl.BlockSpec((tm, 1024), lambda i, j, k: (i, j))], tile=(tm, 1024), name="mlp_up_bwd",
        after=token, dot_fn=lambda a, b: _dot_nt(a[:, :1024], b[0]) + _dot_nt(a[:, 1024:], b[1]))
    dx2, dx2b, d_mlp = _rmsnorm_bwd(dh2, x2, p["mlp_norm_g"], dx3, "norm_mlp_bwd")
    (g_out,) = _mm_simple(ycat, dx2b, mode="tn", M=D, N=D, K=T, tm=1024, tn=1024, tk=T, out_dtype=BF16,
                          name="grad_w_out")
    token = hooks.out_grad(g_out)
    (dy,) = _mm_simple(dx2b, w_out_f, mode="nt", M=T, N=D, K=D, tm=tm, tn=1024, tk=D, out_dtype=F32,
                       name="out_proj_bwd", after=token)
    dproj, dxbc_act, d_dtb, d_alog, d_dskip, d_ssmg = _ssd_bwd(
        xbc, proj, dtT, p["dt_bias"], dtbT, p["A_log"], alogT, dfull, p["ssm_norm_g"], ypre, hs, dy, "ssd_bwd")
    dproj, d_convw, d_convb = _conv_bwd(proj, dxbc_act, conv_w_f, p["conv_b"], dproj, "conv_bwd")
    dproj, dk, dv, d_sinks, d_attng = _attn_bwd(proj, p["attn_sinks"], p["attn_out_norm_g"], o_att, dy, dproj,
                                                "attn_bwd")
    dproj = lax.dynamic_update_slice(dproj, jnp.concatenate([dk, dv], axis=1).astype(BF16), (0, OFF_K))
    (g_in,) = _mm_simple(dproj, h1, mode="tn", M=NP, N=D, K=T, tm=1536, tn=1024, tk=T, out_dtype=BF16,
                         name="grad_w_in")
    token = hooks.in_grad(g_in)
    (dh1,) = _mm_simple(dproj, w_in_t, mode="nn", M=T, N=D, K=NP, tm=tm, tn=1024, tk=2304, out_dtype=F32,
                        name="in_proj_bwd", after=token)
    dx, _, d_mix = _rmsnorm_bwd(dh1, x, p["mix_norm_g"], dx2, "norm_mix_bwd")
    small = _pack_small(d_mix, d_convb, d_ssmg, d_attng, d_mlp, d_fin, d_convw, d_dtb, d_alog, d_dskip, d_sinks,
                        extra=loss_part[:, 0:1])
    return dx, small


def _rows_rotated(v, shift, name):
    R, C = v.shape
    tc = 512

    def body(s_ref, v_ref, o_ref):
        o_ref[...] = pltpu.roll(v_ref[...], s_ref[0], axis=0).astype(BF16)

    return pl.pallas_call(
        body,
        grid_spec=pltpu.PrefetchScalarGridSpec(
            num_scalar_prefetch=1, grid=(C // tc,), in_specs=[pl.BlockSpec((R, tc), lambda j, s: (0, j))],
            out_specs=pl.BlockSpec((R, tc), lambda j, s: (0, j))),
        out_shape=jax.ShapeDtypeStruct((R, C), BF16), name=name, compiler_params=_cparams(("parallel",)),
    )(shift, v)


def _landing(own, me):
    zone = lax.empty((N_DEV,) + own.shape, own.dtype)
    return lax.dynamic_update_slice(zone, own[None], (me,) + (0,) * own.ndim)


def _sequencer_gather(owns, split, me, collective_id, name):
    n = len(owns)
    zone_refs = [jax.new_ref(_landing(o, me), memory_space=pltpu.MemorySpace.HBM) for o in owns]
    own_refs = [jax.new_ref(o, memory_space=pltpu.MemorySpace.HBM) for o in owns]
    N_COPIES = 9

    @pl.kernel(mesh=plsc.ScalarSubcoreMesh(axis_name="sequencer", num_cores=1), name=name,
               scratch_types=(pltpu.SemaphoreType.DMA((n, N_COPIES)), pltpu.SemaphoreType.DMA((n, N_COPIES))),
               compiler_params=pltpu.CompilerParams(collective_id=collective_id))
    def launch(send_sems, recv_sems):
        x, y, c = _coords()
        sibling, xn, yn, diag = (x, y, 1 - c), (1 - x, y, c), (x, 1 - y, c), (1 - x, 1 - y, c)
        barrier = pltpu.get_barrier_semaphore()
        for peer in [sibling, xn, yn, diag]:
            pl.semaphore_signal(barrier, inc=1, device_id=peer, device_id_type=MESH)
        pl.semaphore_wait(barrier, 4)

        def block(a, dev, half=None):
            ref = zone_refs[a].at[4 * dev[0] + 2 * dev[1] + dev[2]]
            if half is None:
                return ref
            rows = owns[a].shape[0] // 2
            return ref.at[pl.ds(half * rows, rows)]

        def copy(a, k, src, dst, to):
            return pltpu.make_async_remote_copy(src_ref=src, dst_ref=dst, send_sem=send_sems.at[a, k],
                                                recv_sem=recv_sems.at[a, k], device_id=to, device_id_type=MESH)

        me_dev = (x, y, c)
        sent = []
        first = {}
        for a in range(n):
            for k, peer in enumerate([sibling, xn, yn] + ([] if split[a] else [diag])):
                first[a, k] = copy(a, k, own_refs[a], block(a, me_dev), peer)
                first[a, k].start()
                sent.append(first[a, k])
        from_sibling = []
        for a in range(n):
            first[a, 1].wait_recv()
            sent.append(copy(a, 4, block(a, xn), block(a, xn), sibling))
            if split[a]:
                sent.append(copy(a, 6, block(a, xn, 0), block(a, xn, 0), yn))
            first[a, 2].wait_recv()
            sent.append(copy(a, 5, block(a, yn), block(a, yn), sibling))
            if split[a]:
                sent.append(copy(a, 7, block(a, yn, 1), block(a, yn, 1), xn))
            for cp in sent[-(4 if split[a] else 2):]:
                cp.start()
        for a in range(n):
            if split[a]:
                copy(a, 6, block(a, diag, 0), block(a, diag, 0), yn).wait_recv()
                sent.append(copy(a, 8, block(a, diag, 0), block(a, diag, 0), sibling))
                sent[-1].start()
                copy(a, 7, block(a, diag, 1), block(a, diag, 1), xn).wait_recv()
                sent.append(copy(a, 3, block(a, diag, 1), block(a, diag, 1), sibling))
                sent[-1].start()
            else:
                first[a, 3].wait_recv()
                sent.append(copy(a, 8, block(a, diag), block(a, diag), sibling))
                sent[-1].start()
        for a in range(n):
            first[a, 0].wait_recv()
            copy(a, 4, block(a, xn), block(a, xn), sibling).wait_recv()
            copy(a, 5, block(a, yn), block(a, yn), sibling).wait_recv()
            if split[a]:
                copy(a, 8, block(a, diag, 0), block(a, diag, 0), sibling).wait_recv()
                copy(a, 3, block(a, diag, 1), block(a, diag, 1), sibling).wait_recv()
            else:
                copy(a, 8, block(a, diag), block(a, diag), sibling).wait_recv()
        for cp in sent:
            cp.wait_send()

    launch()
    return zone_refs


def _gather_end(started, after, plan, name):
    _, lands = _remote_wait(started, after, plan, name + "_wait")
    return _gather_finish(lands, name + "_finish")


class _ShardedWeights:
    def __init__(self, w_in, w_out, conv_w, w_up, w_down, me, csel):
        self.me, self.csel = me, csel
        padded = jnp.pad(jnp.transpose(w_in), ((0, SUPER - PER), (0, 0)))
        own_rows = _rows_rotated(padded, jnp.reshape(2 * me, (1,)).astype(jnp.int32), "w_in_super_slab")
        self.in_ref, self.conv_ref = _sequencer_gather([own_rows, conv_w], [True, False], me, 7,
                                                       "gather_w_in_sequencer")
        (self.out_ref,) = _sequencer_gather([w_out.astype(BF16)], [True], me, 8, "gather_w_out_sequencer")
        (self.up_ref,) = _sequencer_gather([w_up.astype(BF16)], [True], me, 9, "gather_w_up_sequencer")
        (self.down_ref,) = _sequencer_gather([w_down.astype(BF16)], [True], me, 10, "gather_w_down_sequencer")
        self.reduces = {}

    def mixer_weights(self, after):
        g_conv = self.conv_ref[...]
        conv_w_f = jnp.concatenate([g_conv[i] for i in range(N_DEV)], axis=1)
        return _w_in_from_super_slabs(self.in_ref[...]), conv_w_f, None

    def prefetch(self, k, after):
        return None

    def out_weight(self, after):
        return self.out_ref[...].reshape(D_MODEL, D_MODEL)

    def up_weight(self, after):
        return self.up_ref[...]

    def down_weight(self, after):
        return self.down_ref[...].reshape(D_FF, D_MODEL)

    def _chips_start(self, slabs, from_sibling, rows, tag):
        sums = [_pair_add(s, r, self.csel, tr, f"pair_add_{tag}_{i}")
                for i, (s, r, tr) in enumerate(zip(slabs, from_sibling, rows))]
        lands = [lax.empty((3,) + s.shape[1:], s.dtype) for s in sums]
        self.reduces[tag] = _remote_start(sums, lands, _chips_plan, 3 * len(sums), f"reduce_start_{tag}")
        return self.reduces[tag][4]

    def mlp_grads(self, h2, du, act, dx3b):
        def send(part, tag, after):
            st = _remote_start([part], [lax.empty(part.shape, part.dtype)], _pair4_plan, 4,
                               f"reduce_pair_start_{tag}", after=after)
            return st

        def received(st, after, tag):
            return _remote_wait(st, after, _pair4_plan, f"reduce_pair_wait_{tag}")[1][0]

        def to_chips(sums, tag):
            self.reduces[tag] = _remote_start([sums], [lax.empty((3,) + sums.shape[1:], sums.dtype)], _chips_plan, 3,
                                              f"reduce_start_{tag}")
            return self.reduces[tag][4]

        up_send = _grad_w_up(h2, du, "grad_w_up_send", sel=(self.csel, True))
        st_up = send(up_send, "up", None)
        down_send = _grad_w_down(act, dx3b, "grad_w_down_send", sel=(self.csel, True), after=st_up[4])
        st_down = send(down_send, "down", None)
        up_sum = _grad_w_up(h2, du, "grad_w_up_keep", sel=(self.csel, False), add=received(st_up, down_send, "up"),
                            after=st_down[4])
        token = to_chips(up_sum, "up")
        down_sum = _grad_w_down(act, dx3b, "grad_w_down_keep", sel=(self.csel, False),
                                add=received(st_down, up_sum, "down"), after=token)
        return to_chips(down_sum, "down")

    def out_grad(self, g_out):
        slabs = [g_out.reshape(N_DEV, D_MODEL // N_DEV, D_MODEL)]
        return self._chips_start(slabs, _exchange_pair(slabs, "reduce_pair_out"), [256], "out")

    def in_grad(self, g_in):
        slabs = [jnp.stack([_natural_rows(g_in, SUPER_STEP * j, SUPER_STEP * j + SUPER) for j in range(N_DEV)])]
        return self._chips_start(slabs, _exchange_pair(slabs, "reduce_pair_in"), [SUPER], "in")

    def small_start(self, small):
        self.st_small = _remote_start([small], [_landing(small, self.me)], _everyone_plan, N_DEV - 1, "gather_start_small")

    def small_end(self, after):
        return _remote_wait(self.st_small, after, _everyone_plan, "gather_small_wait")[1][0]

    def reduce_end(self, tag, after):
        return _remote_wait(self.reduces[tag], after, _chips_plan, f"reduce_wait_{tag}")


def kernel(x, mix_norm_g, w_in, conv_w, conv_b, dt_bias, A_log, D_skip, ssm_norm_g, attn_sinks, attn_out_norm_g, w_out, mlp_norm_g, w_up, w_down, final_norm_g, loss_target, m_mix_norm_g, m_w_in, m_conv_w, m_conv_b, m_dt_bias, m_A_log, m_D_skip, m_ssm_norm_g, m_attn_sinks, m_attn_out_norm_g, m_w_out, m_mlp_norm_g, m_w_up, m_w_down, m_final_norm_g, v_mix_norm_g, v_w_in, v_conv_w, v_conv_b, v_dt_bias, v_A_log, v_D_skip, v_ssm_norm_g, v_attn_sinks, v_attn_out_norm_g, v_w_out, v_mlp_norm_g, v_w_up, v_w_down, v_final_norm_g):
    xi, yi, ci = _coords()
    me = 4 * xi + 2 * yi + ci
    csel = jnp.reshape(ci, (1,)).astype(jnp.int32)
    qsel = jnp.reshape(2 * xi + yi, (1,)).astype(jnp.int32)
    w = dict(mix_norm_g=mix_norm_g, conv_b=conv_b, dt_bias=dt_bias, A_log=A_log, D_skip=D_skip,
             ssm_norm_g=ssm_norm_g, attn_sinks=attn_sinks, attn_out_norm_g=attn_out_norm_g, mlp_norm_g=mlp_norm_g,
             final_norm_g=final_norm_g)
    hooks = _ShardedWeights(w_in[0], w_out[0], conv_w[0], w_up[0], w_down[0], me, csel)
    p = dict(w)
    dx, small = _local_step(x[0], loss_target[0], p, hooks)
    hooks.small_start(small)
    big = {}
    after = dx
    for name, wt, mt, vt, tile in [
            ("up", w_up, m_w_up, v_w_up, (512, SLAB)), ("down", w_down, m_w_down, v_w_down, (256, D_MODEL)),
            ("out", w_out, m_w_out, v_w_out, (256, D_MODEL))]:
        (chip_sums,), (from_chips,) = hooks.reduce_end(name, after)
        res = _adamw_big(wt[0], mt[0], vt[0], chip_sums, from_chips, qsel, tile, f"adamw_w_{name}")
        big["w_" + name] = tuple(r[None] for r in res)
        after = res[0]
    (chip_sums,), (from_chips,) = hooks.reduce_end("in", after)
    g_super = _sum_partials(chip_sums, from_chips, qsel, 512, "grad_w_in_sum")
    g_in = lax.dynamic_slice(g_super, (2 * me, 0), (PER, D_MODEL))
    res = _adamw_tiled(jnp.transpose(w_in[0]), g_in, jnp.transpose(m_w_in[0]), jnp.transpose(v_w_in[0]), 512,
                       "adamw_w_in")
    big["w_in"] = tuple(jnp.transpose(r)[None] for r in (g_in, *res))
    after = res[0]
    gsum = _small_sum(hooks.small_end(after), "small_sum")
    loss = gsum[9, 64]
    gs = _unpack_small(gsum, CONV_DIM)
    cw = CONV_DIM // N_DEV
    g_conv_shard = lax.dynamic_slice(gsum[5:9, :], (0, me * cw), (CONV_K, cw))

    def pack(s):
        return _pack_small(s["mix_norm_g"], s["conv_b"], s["ssm_norm_g"], s["attn_out_norm_g"], s["mlp_norm_g"],
                           s["final_norm_g"], s["conv_w"][0], s["dt_bias"], s["A_log"], s["D_skip"], s["attn_sinks"])

    wp = pack(dict(w, conv_w=conv_w))
    mp = pack(dict(mix_norm_g=m_mix_norm_g, conv_b=m_conv_b, ssm_norm_g=m_ssm_norm_g,
                   attn_out_norm_g=m_attn_out_norm_g, mlp_norm_g=m_mlp_norm_g, final_norm_g=m_final_norm_g,
                   conv_w=m_conv_w, dt_bias=m_dt_bias, A_log=m_A_log, D_skip=m_D_skip, attn_sinks=m_attn_sinks))
    vp = pack(dict(mix_norm_g=v_mix_norm_g, conv_b=v_conv_b, ssm_norm_g=v_ssm_norm_g,
                   attn_out_norm_g=v_attn_out_norm_g, mlp_norm_g=v_mlp_norm_g, final_norm_g=v_final_norm_g,
                   conv_w=v_conv_w, dt_bias=v_dt_bias, A_log=v_A_log, D_skip=v_D_skip, attn_sinks=v_attn_sinks))
    gp = jnp.concatenate([gsum[0:5], jnp.pad(g_conv_shard, ((0, 0), (0, D_MODEL - cw))), gsum[9:10],
                          jnp.zeros((SMALL_ROWS - 10, D_MODEL), F32)], axis=0)
    dp, mnp, vnp = _adamw_small(wp, gp, mp, vp, "adamw_small")
    grads = dict(gs, conv_w=g_conv_shard[None])
    deltas = _unpack_small(dp, cw)
    new_m = _unpack_small(mnp, cw)
    new_v = _unpack_small(vnp, cw)
    for k, name in enumerate(["w_in", "w_out", "w_up", "w_down"]):
        grads[name], deltas[name], new_m[name], new_v[name] = big[name]
    return (loss, dx[None], *[grads[n] for n in WEIGHT_ORDER], *[deltas[n] for n in WEIGHT_ORDER],
            *[new_m[n] for n in WEIGHT_ORDER], *[new_v[n] for n in WEIGHT_ORDER])
```

```python
import functools

import jax
import jax.numpy as jnp
from jax import lax
from jax.experimental import pallas as pl
from jax.experimental.pallas import tpu as pltpu
from jax.experimental.pallas import tpu_sc as plsc

F32 = jnp.float32
BF16 = jnp.bfloat16
HI = lax.Precision.HIGHEST
MESH = pl.DeviceIdType.MESH

EPS = 1e-5
D_MODEL = 2048
D_INNER = 1024
N_HEADS = 16
HEAD_DIM = 64
N_GROUPS = 4
D_STATE = 128
CHUNK = 128
CONV_K = 4
CONV_DIM = 2048
ATTN_W = 1024
KV_W = 128
WINDOW = 128
D_FF = 8192
IN_PROJ = 4368
N_DEV = 8
NP = 4608
OFF_Z, OFF_X, OFF_B, OFF_C, OFF_Q, OFF_K, OFF_V, OFF_DT = 0, 1024, 2048, 2560, 3072, 4096, 4224, 4352
NAT_DT = 3072

ADAM_LR = 0.001
ADAM_B1 = 0.9
ADAM_B2 = 0.999
ADAM_EPS = 1e-08
ADAM_WD = 0.01
ADAM_STEP = 10

VMEM_LIMIT = 52 * 1024 * 1024
SMALL_ROWS = 16
NEG = -1e30


def _cparams(sem=None):
    return pltpu.CompilerParams(dimension_semantics=sem, vmem_limit_bytes=VMEM_LIMIT)


def _split3(v):
    hi = v.astype(BF16)
    rest = v - hi.astype(F32)
    mid = rest.astype(BF16)
    return hi, mid, (rest - mid.astype(F32)).astype(BF16)


def _hdot(a, b, data):
    if data == "a":
        sel = b.astype(BF16)
        return sum(_dot_nn(part, sel) for part in _split3(a))
    sel = a.astype(BF16)
    return sum(_dot_nn(sel, part) for part in _split3(b))


def _dot_nn(a, b):
    return lax.dot_general(a, b, (((1,), (0,)), ((), ())), preferred_element_type=F32)


def _dot_nt(a, b):
    return lax.dot_general(a, b, (((1,), (1,)), ((), ())), preferred_element_type=F32)


def _dot_tn(a, b):
    return lax.dot_general(a, b, (((0,), (0,)), ((), ())), preferred_element_type=F32)


def _softplus(v):
    return jnp.maximum(v, 0.0) + jnp.log1p(jnp.exp(-jnp.abs(v)))


def _sigmoid(v):
    return 1.0 / (1.0 + jnp.exp(-v))


def _matmul(a, b, *, mode, grid, a_spec, b_spec, out_shapes, out_specs, tile, name,
            extras=(), extra_specs=(), epilogue=None, after=None, dot_fn=None, prefetch=None):
    nk = grid[2]
    n_ex = len(extras)
    n_out = len(out_shapes)
    dot = dot_fn if dot_fn is not None else {"nn": _dot_nn, "nt": _dot_nt, "tn": _dot_tn}[mode]

    def finish(acc, ex_refs, out_refs):
        res = (acc,) if epilogue is None else epilogue(acc, *[e[...] for e in ex_refs])
        for o, r in zip(out_refs, res):
            o[...] = r.astype(o.dtype)

    def body(*refs):
        a_ref, b_ref = refs[0], refs[1]
        ex_refs = refs[2:2 + n_ex]
        out_refs = refs[2 + n_ex:2 + n_ex + n_out]
        part = dot(a_ref[...].astype(BF16), b_ref[...].astype(BF16))
        if nk == 1:
            finish(part, ex_refs, out_refs)
        else:
            acc_ref = refs[-1]
            k = pl.program_id(2)

            @pl.when(k == 0)
            def _():
                acc_ref[...] = part

            @pl.when(k > 0)
            def _():
                acc_ref[...] += part

            @pl.when(k == nk - 1)
            def _():
                finish(acc_ref[...], ex_refs, out_refs)

    scratch = [] if nk == 1 else [pltpu.VMEM(tile, F32)]
    n_pre = 0 if prefetch is None else 1
    tok_specs = [] if after is None else [pl.BlockSpec((8, 128), lambda *_: (0, 0))]
    tok_args = [] if after is None else [after]

    def body_with_token(*refs):
        refs = refs[n_pre:]
        body(*refs[:2 + n_ex], *refs[2 + n_ex + len(tok_args):])

    in_specs = [a_spec, b_spec, *extra_specs, *tok_specs]
    params = _cparams(("parallel", "parallel", "arbitrary"))
    if prefetch is None:
        return pl.pallas_call(
            body_with_token, grid=grid, in_specs=in_specs, out_specs=list(out_specs), out_shape=list(out_shapes),
            scratch_shapes=scratch, name=name, compiler_params=params)(a, b, *extras, *tok_args)
    return pl.pallas_call(
        body_with_token,
        grid_spec=pltpu.PrefetchScalarGridSpec(num_scalar_prefetch=1, grid=grid, in_specs=in_specs,
                                               out_specs=list(out_specs), scratch_shapes=scratch),
        out_shape=list(out_shapes), name=name, compiler_params=params)(prefetch, a, b, *extras, *tok_args)


def _mm_simple(a, b, *, mode, M, N, K, tm, tn, tk, out_dtype, name, extras=(), epilogue=None, n_out=1,
               out_dtypes=None, after=None):
    grid = (M // tm, N // tn, K // tk)
    if mode == "nn":
        a_spec = pl.BlockSpec((tm, tk), lambda i, j, k: (i, k))
        b_spec = pl.BlockSpec((tk, tn), lambda i, j, k: (k, j))
    elif mode == "nt":
        a_spec = pl.BlockSpec((tm, tk), lambda i, j, k: (i, k))
        b_spec = pl.BlockSpec((tn, tk), lambda i, j, k: (j, k))
    else:
        a_spec = pl.BlockSpec((tk, tm), lambda i, j, k: (k, i))
        b_spec = pl.BlockSpec((tk, tn), lambda i, j, k: (k, j))
    o_spec = pl.BlockSpec((tm, tn), lambda i, j, k: (i, j))
    dts = out_dtypes if out_dtypes is not None else [out_dtype] * n_out
    return _matmul(a, b, mode=mode, grid=grid, a_spec=a_spec, b_spec=b_spec,
                   out_shapes=[jax.ShapeDtypeStruct((M, N), d) for d in dts],
                   out_specs=[o_spec] * len(dts), tile=(tm, tn), name=name,
                   extras=extras, extra_specs=[o_spec] * len(extras), epilogue=epilogue, after=after)


ROW_BLOCK = 256


def _rmsnorm_fwd(x, g, name):
    T, D = x.shape

    def body(x_ref, g_ref, o_ref):
        xf = x_ref[...]
        r = lax.rsqrt(jnp.mean(xf * xf, axis=-1, keepdims=True) + EPS)
        o_ref[...] = (xf * r * g_ref[...]).astype(BF16)

    return pl.pallas_call(
        body, grid=(T // ROW_BLOCK,),
        in_specs=[pl.BlockSpec((ROW_BLOCK, D), lambda i: (i, 0)), pl.BlockSpec((1, D), lambda i: (0, 0))],
        out_specs=pl.BlockSpec((ROW_BLOCK, D), lambda i: (i, 0)),
        out_shape=jax.ShapeDtypeStruct((T, D), BF16), name=name, compiler_params=_cparams(("parallel",)),
    )(x, g)


def _rmsnorm_bwd(dh, x, g, dres, name):
    T, D = x.shape

    def body(dh_ref, x_ref, g_ref, dres_ref, dx_ref, dxb_ref, dg_ref):
        i = pl.program_id(0)
        xf = x_ref[...]
        r = lax.rsqrt(jnp.mean(xf * xf, axis=-1, keepdims=True) + EPS)
        xh = xf * r
        d = dh_ref[...]

        @pl.when(i == 0)
        def _():
            dg_ref[...] = jnp.zeros_like(dg_ref)

        dg_ref[...] += jnp.sum(d * xh, axis=0, keepdims=True)
        dxh = d * g_ref[...]
        dx = r * (dxh - xh * jnp.mean(dxh * xh, axis=-1, keepdims=True)) + dres_ref[...]
        dx_ref[...] = dx
        dxb_ref[...] = dx.astype(BF16)

    row = pl.BlockSpec((ROW_BLOCK, D), lambda i: (i, 0))
    vec = pl.BlockSpec((1, D), lambda i: (0, 0))
    return pl.pallas_call(
        body, grid=(T // ROW_BLOCK,), in_specs=[row, row, vec, row], out_specs=[row, row, vec],
        out_shape=[jax.ShapeDtypeStruct((T, D), F32), jax.ShapeDtypeStruct((T, D), BF16),
                   jax.ShapeDtypeStruct((1, D), F32)],
        name=name, compiler_params=_cparams(("arbitrary",)),
    )(dh, x, g, dres)


def _final_loss(x3, tgt, g, name):
    T, D = x3.shape

    def body(x_ref, t_ref, g_ref, loss_ref, dg_ref, dx_ref, dxb_ref):
        i = pl.program_id(0)
        xf = x_ref[...]
        r = lax.rsqrt(jnp.mean(xf * xf, axis=-1, keepdims=True) + EPS)
        xh = xf * r
        gg = g_ref[...]
        err = xh * gg - t_ref[...]

        @pl.when(i == 0)
        def _():
            dg_ref[...] = jnp.zeros_like(dg_ref)
            loss_ref[...] = jnp.zeros_like(loss_ref)

        part = jnp.sum(jnp.sum(err * err, axis=-1, keepdims=True), axis=0, keepdims=True) * (0.5 / D)
        loss_ref[...] += jnp.broadcast_to(part, loss_ref.shape)
        dout = err * (1.0 / D)
        dg_ref[...] += jnp.sum(dout * xh, axis=0, keepdims=True)
        dxh = dout * gg
        dx = r * (dxh - xh * jnp.mean(dxh * xh, axis=-1, keepdims=True))
        dx_ref[...] = dx
        dxb_ref[...] = dx.astype(BF16)

    row = pl.BlockSpec((ROW_BLOCK, D), lambda i: (i, 0))
    vec = pl.BlockSpec((1, D), lambda i: (0, 0))
    return pl.pallas_call(
        body, grid=(T // ROW_BLOCK,), in_specs=[row, row, vec],
        out_specs=[pl.BlockSpec((1, 128), lambda i: (0, 0)), vec, row, row],
        out_shape=[jax.ShapeDtypeStruct((1, 128), F32), jax.ShapeDtypeStruct((1, D), F32),
                   jax.ShapeDtypeStruct((T, D), F32), jax.ShapeDtypeStruct((T, D), BF16)],
        name=name, compiler_params=_cparams(("arbitrary",)),
    )(x3, tgt, g)


CONV_BLOCK = 256


def _conv_apply(u, w, b):
    row = lax.broadcasted_iota(jnp.int32, u.shape, 0)
    acc = b + w[CONV_K - 1:CONV_K, :] * u
    shifted = []
    for j in range(1, CONV_K):
        uj = jnp.where(row >= j, pltpu.roll(u, j, axis=0), 0.0)
        shifted.append(uj)
        acc = acc + w[CONV_K - 1 - j:CONV_K - j, :] * uj
    return acc, shifted


def _conv_fwd(proj, conv_w, conv_b, name):
    T = proj.shape[0]
    cb0 = OFF_X // CONV_BLOCK

    def body(u_ref, w_ref, b_ref, o_ref):
        c, _ = _conv_apply(u_ref[...], w_ref[...], b_ref[...])
        o_ref[...] = c * _sigmoid(c)

    return pl.pallas_call(
        body, grid=(CONV_DIM // CONV_BLOCK,),
        in_specs=[pl.BlockSpec((T, CONV_BLOCK), lambda j: (0, cb0 + j)),
                  pl.BlockSpec((CONV_K, CONV_BLOCK), lambda j: (0, j)),
                  pl.BlockSpec((1, CONV_BLOCK), lambda j: (0, j))],
        out_specs=pl.BlockSpec((T, CONV_BLOCK), lambda j: (0, j)),
        out_shape=jax.ShapeDtypeStruct((T, CONV_DIM), F32), name=name, compiler_params=_cparams(("parallel",)),
    )(proj, conv_w, conv_b)


def _conv_bwd(proj, dact, conv_w, conv_b, dproj, name):
    T = proj.shape[0]
    cb0 = OFF_X // CONV_BLOCK

    def body(u_ref, d_ref, w_ref, b_ref, _, du_ref, dw_ref, db_ref):
        u = u_ref[...]
        w = w_ref[...]
        c, shifted = _conv_apply(u, w, b_ref[...])
        sg = _sigmoid(c)
        dc = d_ref[...] * sg * (1.0 + c * (1.0 - sg))
        row = lax.broadcasted_iota(jnp.int32, u.shape, 0)
        du = w[CONV_K - 1:CONV_K, :] * dc
        dw_ref[CONV_K - 1:CONV_K, :] = jnp.sum(dc * u, axis=0, keepdims=True)
        for j in range(1, CONV_K):
            dcj = jnp.where(row < T - j, pltpu.roll(dc, T - j, axis=0), 0.0)
            du = du + w[CONV_K - 1 - j:CONV_K - j, :] * dcj
            dw_ref[CONV_K - 1 - j:CONV_K - j, :] = jnp.sum(dc * shifted[j - 1], axis=0, keepdims=True)
        db_ref[...] = jnp.sum(dc, axis=0, keepdims=True)
        du_ref[...] = du.astype(BF16)

    return pl.pallas_call(
        body, grid=(CONV_DIM // CONV_BLOCK,),
        in_specs=[pl.BlockSpec((T, CONV_BLOCK), lambda j: (0, cb0 + j)),
                  pl.BlockSpec((T, CONV_BLOCK), lambda j: (0, j)),
                  pl.BlockSpec((CONV_K, CONV_BLOCK), lambda j: (0, j)),
                  pl.BlockSpec((1, CONV_BLOCK), lambda j: (0, j)), pl.BlockSpec(memory_space=pl.ANY)],
        out_specs=[pl.BlockSpec((T, CONV_BLOCK), lambda j: (0, cb0 + j)),
                   pl.BlockSpec((CONV_K, CONV_BLOCK), lambda j: (0, j)),
                   pl.BlockSpec((1, CONV_BLOCK), lambda j: (0, j))],
        out_shape=[jax.ShapeDtypeStruct(dproj.shape, BF16), jax.ShapeDtypeStruct((CONV_K, CONV_DIM), F32),
                   jax.ShapeDtypeStruct((1, CONV_DIM), F32)],
        input_output_aliases={4: 0}, name=name, compiler_params=_cparams(("parallel",)),
    )(proj, dact, conv_w, conv_b, dproj)


GROUP_W = D_INNER // N_GROUPS
HEADS_PER_GROUP = N_HEADS // N_GROUPS


def _expand_mat():
    h = lax.broadcasted_iota(jnp.int32, (N_HEADS, D_INNER), 0)
    j = lax.broadcasted_iota(jnp.int32, (N_HEADS, D_INNER), 1)
    return (j // HEAD_DIM == h).astype(F32)


def _reduce_mat(g):
    j = lax.broadcasted_iota(jnp.int32, (GROUP_W, N_HEADS), 0)
    h = lax.broadcasted_iota(jnp.int32, (GROUP_W, N_HEADS), 1)
    return (g * HEADS_PER_GROUP + j // HEAD_DIM == h).astype(F32)


def _col16(v, h):
    lane = lax.broadcasted_iota(jnp.int32, v.shape, 1)
    return jnp.sum(jnp.where(lane == h, v, 0.0), axis=1, keepdims=True)


def _ssd_pre(dt_raw, dtT_raw, dtb, dtbT, alog, alogT):
    Q = CHUNK
    xdt = dt_raw + dtb
    dt = _softplus(xdt)
    dtT = _softplus(dtT_raw + dtbT)
    A = -jnp.exp(alog)
    AT = -jnp.exp(alogT)
    row = lax.broadcasted_iota(jnp.int32, (Q, Q), 0)
    col = lax.broadcasted_iota(jnp.int32, (Q, Q), 1)
    tril = (row >= col).astype(F32)
    triu = (row <= col).astype(F32)
    cs = _hdot(tril, dt * A, "b")
    csT = _hdot(dtT * AT, triu, "a")
    return xdt, dt, A, cs, csT, row >= col, triu


def _decay_matrix(cs, csT, h, causal):
    seg = _col16(cs, h) - csT[h:h + 1, :]
    return jnp.where(causal, jnp.exp(jnp.minimum(seg, 0.0)), 0.0)


def _ssd_in_specs(nc, rev):
    def cidx(c):
        return (nc - 1 - c) if rev else c

    return [
        pl.BlockSpec((CHUNK, D_INNER), lambda c: (cidx(c), 0)),
        pl.BlockSpec((CHUNK, 512), lambda c: (cidx(c), 2)),
        pl.BlockSpec((CHUNK, 512), lambda c: (cidx(c), 3)),
        pl.BlockSpec((CHUNK, D_INNER), lambda c: (cidx(c), 0)),
        pl.BlockSpec((CHUNK, 128), lambda c: (cidx(c), OFF_DT // 128)),
        pl.BlockSpec((N_HEADS, CHUNK), lambda c: (0, cidx(c))),
        pl.BlockSpec((1, N_HEADS), lambda c: (0, 0)),
        pl.BlockSpec((N_HEADS, 1), lambda c: (0, 0)),
        pl.BlockSpec((1, N_HEADS), lambda c: (0, 0)),
        pl.BlockSpec((N_HEADS, 1), lambda c: (0, 0)),
        pl.BlockSpec((1, D_INNER), lambda c: (0, 0)),
        pl.BlockSpec((1, D_INNER), lambda c: (0, 0)),
    ]


def _ssd_fwd(xbc, proj, dtT, dtb, dtbT, alog, alogT, dfull, ng, name):
    T = xbc.shape[0]
    nc = T // CHUNK
    Q = CHUNK

    def body(xs_ref, B_ref, C_ref, z_ref, dt_ref, dtT_ref, dtb_ref, dtbT_ref, al_ref, alT_ref, df_ref, ng_ref,
             y_ref, ypre_ref, hs_ref, h_scr):
        c = pl.program_id(0)

        @pl.when(c == 0)
        def _():
            h_scr[...] = jnp.zeros_like(h_scr)

        _, dt, _, cs, csT, causal, _ = _ssd_pre(dt_ref[:, :N_HEADS], dtT_ref[...], dtb_ref[...], dtbT_ref[...],
                                                al_ref[...], alT_ref[...])
        ex = _expand_mat()
        dt_full = _hdot(dt, ex, "a")
        cs_full = _hdot(cs, ex, "a")
        cs_last = cs_full[Q - 1:Q, :]
        xs = xs_ref[...]
        xd = xs * dt_full
        e_full = jnp.exp(cs_full)
        dec_full = jnp.exp(cs_last - cs_full)
        cd_full = jnp.exp(cs_last)
        lane_head = lax.broadcasted_iota(jnp.int32, (1, GROUP_W), 1) // HEAD_DIM
        for g in range(N_GROUPS):
            sl = slice(g * GROUP_W, (g + 1) * GROUP_W)
            Bg = B_ref[:, g * D_STATE:(g + 1) * D_STATE].astype(BF16)
            Cg = C_ref[:, g * D_STATE:(g + 1) * D_STATE].astype(BF16)
            CB = _dot_nt(Cg, Bg)
            hg = h_scr[g]
            yoff = _dot_nn(Cg, hg.astype(BF16)) * e_full[:, sl]
            xd_g = xd[:, sl]
            S = _dot_tn(Bg, (xd_g * dec_full[:, sl]).astype(BF16))
            xd_b = xd_g.astype(BF16)
            ydiag = jnp.zeros((Q, GROUP_W), F32)
            for r in range(HEADS_PER_GROUP):
                Lm = _decay_matrix(cs, csT, g * HEADS_PER_GROUP + r, causal)
                Gm = (CB * Lm).astype(BF16)
                ydiag = ydiag + _dot_nn(Gm, jnp.where(lane_head == r, xd_b, jnp.zeros_like(xd_b)))
            hs_ref[0, g] = hg
            h_scr[g] = hg * cd_full[:, sl] + S
            ypre = ydiag + yoff + xs[:, sl] * df_ref[:, sl]
            ypre_ref[:, sl] = ypre
            zg = z_ref[:, sl]
            yz = ypre * zg * _sigmoid(zg)
            rn = lax.rsqrt(jnp.mean(yz * yz, axis=-1, keepdims=True) + EPS)
            y_ref[:, sl] = (yz * rn * ng_ref[:, sl]).astype(BF16)

    return pl.pallas_call(
        body, grid=(nc,), in_specs=_ssd_in_specs(nc, False),
        out_specs=[pl.BlockSpec((CHUNK, D_INNER), lambda c: (c, 0)),
                   pl.BlockSpec((CHUNK, D_INNER), lambda c: (c, 0)),
                   pl.BlockSpec((1, N_GROUPS, D_STATE, GROUP_W), lambda c: (c, 0, 0, 0))],
        out_shape=[jax.ShapeDtypeStruct((T, D_INNER + ATTN_W), BF16), jax.ShapeDtypeStruct((T, D_INNER), F32),
                   jax.ShapeDtypeStruct((nc, N_GROUPS, D_STATE, GROUP_W), F32)],
        scratch_shapes=[pltpu.VMEM((N_GROUPS, D_STATE, GROUP_W), F32)],
        name=name, compiler_params=_cparams(("arbitrary",)),
    )(xbc, xbc, xbc, proj, proj, dtT, dtb, dtbT, alog, alogT, dfull, ng)


def _ssd_bwd(xbc, proj, dtT, dtb, dtbT, alog, alogT, dfull, ng, ypre, hs, dy, name):
    T = xbc.shape[0]
    nc = T // CHUNK
    Q = CHUNK

    def body(xs_ref, B_ref, C_ref, z_ref, dt_ref, dtT_ref, dtb_ref, dtbT_ref, al_ref, alT_ref, df_ref, ng_ref,
             ypre_ref, hs_ref, dy_ref,
             dz_ref, dxbc_ref, ddtb_ref, dal_ref, dD_ref, dng_ref, dh_scr):
        step = pl.program_id(0)

        @pl.when(step == 0)
        def _():
            dh_scr[...] = jnp.zeros_like(dh_scr)
            ddtb_ref[...] = jnp.zeros_like(ddtb_ref)
            dal_ref[...] = jnp.zeros_like(dal_ref)
            dD_ref[...] = jnp.zeros_like(dD_ref)
            dng_ref[...] = jnp.zeros_like(dng_ref)

        xdt, dt, A, cs, csT, causal, triu = _ssd_pre(dt_ref[:, :N_HEADS], dtT_ref[...], dtb_ref[...],
                                                    dtbT_ref[...], al_ref[...], alT_ref[...])
        ex = _expand_mat()
        dt_full = _hdot(dt, ex, "a")
        cs_full = _hdot(cs, ex, "a")
        cs_last = cs_full[Q - 1:Q, :]
        xs = xs_ref[...]
        xd = xs * dt_full
        e_full = jnp.exp(cs_full)
        dec_full = jnp.exp(cs_last - cs_full)
        cd_full = jnp.exp(cs_last)
        lane_head = lax.broadcasted_iota(jnp.int32, (1, GROUP_W), 1) // HEAD_DIM
        is_last = lax.broadcasted_iota(jnp.int32, (Q, 1), 0) == Q - 1
        dcs16 = jnp.zeros((Q, N_HEADS), F32)
        ddtx16 = jnp.zeros((Q, N_HEADS), F32)
        dD16 = jnp.zeros((8, N_HEADS), F32)
        lane16 = lax.broadcasted_iota(jnp.int32, (1, N_HEADS), 1)
        sub16 = lax.broadcasted_iota(jnp.int32, (N_HEADS, 1), 0)
        col_sums = jnp.zeros((N_HEADS, Q), F32)
        for g in range(N_GROUPS):
            sl = slice(g * GROUP_W, (g + 1) * GROUP_W)
            red = _reduce_mat(g)
            ypre_g = ypre_ref[:, sl]
            zg = z_ref[:, sl]
            sg = _sigmoid(zg)
            silu = zg * sg
            yz = ypre_g * silu
            rn = lax.rsqrt(jnp.mean(yz * yz, axis=-1, keepdims=True) + EPS)
            yh = yz * rn
            dy_g = dy_ref[:, sl]
            dng_ref[:, sl] += jnp.sum(dy_g * yh, axis=0, keepdims=True)
            dyh = dy_g * ng_ref[:, sl]
            dyz = rn * (dyh - yh * jnp.mean(dyh * yh, axis=-1, keepdims=True))
            dY = dyz * silu
            dz_ref[:, sl] = (dyz * ypre_g * sg * (1.0 + zg * (1.0 - sg))).astype(BF16)
            xs_g = xs[:, sl]
            xd_g = xd[:, sl]
            dec_g = dec_full[:, sl]
            cd_g = cd_full[:, sl]
            d_g = df_ref[:, sl]
            Bg = B_ref[:, g * D_STATE:(g + 1) * D_STATE].astype(BF16)
            Cg = C_ref[:, g * D_STATE:(g + 1) * D_STATE].astype(BF16)
            CB = _dot_nt(Cg, Bg)
            hg = hs_ref[0, g]
            hgb = hg.astype(BF16)
            yoff = _dot_nn(Cg, hgb) * e_full[:, sl]
            dhn = dh_scr[g]
            dhnb = dhn.astype(BF16)
            dYE = (dY * e_full[:, sl]).astype(BF16)
            dC = _dot_nt(dYE, hgb)
            dh_direct = _dot_tn(Cg, dYE)
            dXdd = _dot_nn(Bg, dhnb)
            dB = _dot_nt((xd_g * dec_g).astype(BF16), dhnb)
            dcd = jnp.sum(dhn * hg, axis=0, keepdims=True)
            dh_scr[g] = dh_direct + cd_g * dhn
            dYb = dY.astype(BF16)
            xd_b = xd_g.astype(BF16)
            dCB = jnp.zeros((Q, Q), F32)
            dXd = dXdd * dec_g
            for r in range(HEADS_PER_GROUP):
                h = g * HEADS_PER_GROUP + r
                Lm = _decay_matrix(cs, csT, h, causal)
                Gf = CB * Lm
                dYr = jnp.where(lane_head == r, dYb, jnp.zeros_like(dYb))
                dG = _dot_nt(dYr, xd_b)
                dCB = dCB + dG * Lm
                dXd = dXd + _dot_tn(Gf.astype(BF16), dYr)
                Mm = dG * Gf
                dcs16 = dcs16 + jnp.where(lane16 == h, jnp.sum(Mm, axis=1, keepdims=True), 0.0)
                col_sums = col_sums + jnp.where(sub16 == h, jnp.sum(Mm, axis=0, keepdims=True), 0.0)
            dCBb = dCB.astype(BF16)
            dC = dC + _dot_nn(dCBb, Bg)
            dB = dB + _dot_tn(dCBb, Cg)
            w_state = dXdd * dec_g * xd_g
            t_last = jnp.sum(w_state, axis=0, keepdims=True) + dcd * cd_g
            dcs_g = dY * yoff - w_state + jnp.where(is_last, t_last, 0.0)
            dcs16 = dcs16 + _hdot(dcs_g, red, "a")
            ddtx16 = ddtx16 + _hdot(dXd * xs_g, red, "a")
            dD16 = dD16 + _hdot(jnp.broadcast_to(jnp.sum(dY * xs_g, axis=0, keepdims=True), (8, GROUP_W)), red, "a")
            dxbc_ref[:, sl] = dXd * dt_full[:, sl] + dY * d_g
            dxbc_ref[:, D_INNER + g * D_STATE:D_INNER + (g + 1) * D_STATE] = dB
            dxbc_ref[:, D_INNER + 512 + g * D_STATE:D_INNER + 512 + (g + 1) * D_STATE] = dC
        eye = (lax.broadcasted_iota(jnp.int32, (N_HEADS, N_HEADS), 0)
               == lax.broadcasted_iota(jnp.int32, (N_HEADS, N_HEADS), 1)).astype(BF16)
        dcs16 = dcs16 - sum(_dot_tn(part, eye) for part in _split3(col_sums))
        da = _hdot(triu, dcs16, "b")
        ddt = da * A + ddtx16
        ddt_raw = ddt * _sigmoid(xdt)
        pr = lax.broadcasted_iota(jnp.int32, (N_HEADS, 128), 0)
        pc = lax.broadcasted_iota(jnp.int32, (N_HEADS, 128), 1)
        dz_ref[:, D_INNER:OFF_DT] = jnp.zeros((Q, OFF_DT - D_INNER), BF16)
        dz_ref[:, OFF_DT:OFF_DT + 128] = _hdot(ddt_raw, (pr == pc).astype(F32), "a").astype(BF16)
        dz_ref[:, OFF_DT + 128:] = jnp.zeros((Q, NP - OFF_DT - 128), BF16)
        ddtb_ref[...] += jnp.sum(ddt_raw, axis=0, keepdims=True)
        dal_ref[...] += jnp.sum(da * dt, axis=0, keepdims=True) * A
        dD_ref[...] += dD16[0:1, :]

    def rc(c):
        return nc - 1 - c

    in_specs = _ssd_in_specs(nc, True) + [
        pl.BlockSpec((CHUNK, D_INNER), lambda c: (rc(c), 0)),
        pl.BlockSpec((1, N_GROUPS, D_STATE, GROUP_W), lambda c: (rc(c), 0, 0, 0)),
        pl.BlockSpec((CHUNK, D_INNER), lambda c: (rc(c), 0)),
    ]
    small = pl.BlockSpec((1, N_HEADS), lambda c: (0, 0))
    return pl.pallas_call(
        body, grid=(nc,), in_specs=in_specs,
        out_specs=[pl.BlockSpec((CHUNK, NP), lambda c: (rc(c), 0)),
                   pl.BlockSpec((CHUNK, CONV_DIM), lambda c: (rc(c), 0)),
                   small, small, small,
                   pl.BlockSpec((1, D_INNER), lambda c: (0, 0))],
        out_shape=[jax.ShapeDtypeStruct((T, NP), BF16), jax.ShapeDtypeStruct((T, CONV_DIM), F32),
                   jax.ShapeDtypeStruct((1, N_HEADS), F32), jax.ShapeDtypeStruct((1, N_HEADS), F32),
                   jax.ShapeDtypeStruct((1, N_HEADS), F32), jax.ShapeDtypeStruct((1, D_INNER), F32)],
        scratch_shapes=[pltpu.VMEM((N_GROUPS, D_STATE, GROUP_W), F32)],
        name=name, compiler_params=_cparams(("arbitrary",)),
    )(xbc, xbc, xbc, proj, proj, dtT, dtb, dtbT, alog, alogT, dfull, ng, ypre, hs, dy)


N_PAIRS = ATTN_W // 128
PAIRS_PER_KV = N_PAIRS // 2
ATTN_SCALE = HEAD_DIM ** -0.5


def _kv_variants(kk):
    lo = lax.broadcasted_iota(jnp.int32, kk.shape, 1) < HEAD_DIM
    zero = jnp.zeros_like(kk)
    k00 = jnp.where(lo, kk, zero)
    k11 = jnp.where(lo, zero, kk)
    k01 = pltpu.roll(k00, HEAD_DIM, axis=1)
    k10 = pltpu.roll(k11, HEAD_DIM, axis=1)
    return [[k00.astype(BF16), k01.astype(BF16)], [k10.astype(BF16), k11.astype(BF16)]]


LOG2E = 1.4426950408889634


def _own_block():
    i = lax.broadcasted_iota(jnp.int32, (WINDOW, WINDOW), 0)
    j = lax.broadcasted_iota(jnp.int32, (WINDOW, WINDOW), 1)
    return j <= i


def _fold(own, a):
    return jnp.where(own, a[:, WINDOW:], a[:, :WINDOW])


def _attn_probs(qp, kvar, own, prev_bias, sk):
    s = _dot_nt(qp, kvar)
    sb = jnp.where(own, s[:, WINDOW:], s[:, :WINDOW] + prev_bias) * (ATTN_SCALE * LOG2E)
    sk2 = sk * LOG2E
    m = jnp.maximum(jnp.max(sb, axis=1, keepdims=True), sk2)
    pe = jnp.exp2(sb - m)
    es = jnp.exp2(sk2 - m)
    den = jnp.sum(pe, axis=1, keepdims=True) + es
    inv = 1.0 / den
    return pe * inv, es * inv


def _unfold(own, a):
    zero = jnp.zeros_like(a)
    return jnp.where(own, zero, a), jnp.where(own, a, zero)


def _sink(sinks, r):
    lane = lax.broadcasted_iota(jnp.int32, sinks.shape, 1)
    return jnp.sum(jnp.where(lane == r, sinks, 0.0), axis=1, keepdims=True)


def _kv_specs():
    return [pl.BlockSpec((WINDOW, KV_W), lambda n: (jnp.maximum(n - 1, 0), OFF_K // KV_W)),
            pl.BlockSpec((WINDOW, KV_W), lambda n: (n, OFF_K // KV_W)),
            pl.BlockSpec((WINDOW, KV_W), lambda n: (jnp.maximum(n - 1, 0), OFF_V // KV_W)),
            pl.BlockSpec((WINDOW, KV_W), lambda n: (n, OFF_V // KV_W))]


def _attn_fwd(proj, sinks, og, ycat, name):
    T = proj.shape[0]
    nb = T // WINDOW

    def body(q_ref, kp_ref, kc_ref, vp_ref, vc_ref, s_ref, og_ref, _, y_ref, o_ref):
        n = pl.program_id(0)
        kv = _kv_variants(jnp.concatenate([kp_ref[...], kc_ref[...]], axis=0))
        vv = _kv_variants(jnp.concatenate([vp_ref[...], vc_ref[...]], axis=0))
        own = _own_block()
        prev_bias = jnp.where(n > 0, 0.0, NEG)
        sinks_v = s_ref[...]
        ssq = jnp.zeros((WINDOW, 1), F32)
        for p in range(N_PAIRS):
            j = p // PAIRS_PER_KV
            qp = q_ref[:, p * 128:(p + 1) * 128].astype(BF16)
            o_pair = jnp.zeros((WINDOW, 128), F32)
            for par in range(2):
                pn, _ = _attn_probs(qp, kv[j][par], own, prev_bias, _sink(sinks_v, 2 * p + par))
                p_prev, p_own = _unfold(own, pn.astype(BF16))
                o_pair = o_pair + _dot_nn(p_prev, vv[j][par][:WINDOW]) + _dot_nn(p_own, vv[j][par][WINDOW:])
            o_ref[:, p * 128:(p + 1) * 128] = o_pair
            ssq = ssq + jnp.sum(o_pair * o_pair, axis=1, keepdims=True)
        rn = lax.rsqrt(ssq * (1.0 / ATTN_W) + EPS)
        y_ref[...] = (o_ref[...] * rn * og_ref[...]).astype(BF16)

    return pl.pallas_call(
        body, grid=(nb,),
        in_specs=[pl.BlockSpec((WINDOW, ATTN_W), lambda n: (n, OFF_Q // ATTN_W)), *_kv_specs(),
                  pl.BlockSpec((1, N_HEADS), lambda n: (0, 0)), pl.BlockSpec((1, ATTN_W), lambda n: (0, 0)), ANY],
        out_specs=[pl.BlockSpec((WINDOW, ATTN_W), lambda n: (n, 1)), pl.BlockSpec((WINDOW, ATTN_W), lambda n: (n, 0))],
        out_shape=[jax.ShapeDtypeStruct(ycat.shape, BF16), jax.ShapeDtypeStruct((T, ATTN_W), F32)],
        input_output_aliases={7: 0}, name=name, compiler_params=_cparams(("parallel",)),
    )(proj, proj, proj, proj, proj, sinks, og, ycat)


def _attn_bwd(proj, sinks, og, o, dy, dproj, name):
    T = proj.shape[0]
    nb = T // WINDOW

    def body(q_ref, kp_ref, kc_ref, vp_ref, vc_ref, s_ref, og_ref, o_ref, dy_ref, _,
             dq_ref, dk_ref, dv_ref, ds_ref, dog_ref, qt_scr, dot_scr, ds_scr, p_scr):
        n = pl.program_id(0)

        @pl.when(n == 0)
        def _():
            dk_ref[...] = jnp.zeros_like(dk_ref)
            dv_ref[...] = jnp.zeros_like(dv_ref)
            ds_ref[...] = jnp.zeros_like(ds_ref)
            dog_ref[...] = jnp.zeros_like(dog_ref)

        kv = _kv_variants(jnp.concatenate([kp_ref[...], kc_ref[...]], axis=0))
        vv = _kv_variants(jnp.concatenate([vp_ref[...], vc_ref[...]], axis=0))
        own = _own_block()
        prev_bias = jnp.where(n > 0, 0.0, NEG)
        sinks_v = s_ref[...]
        of = o_ref[...]
        rn = lax.rsqrt(jnp.mean(of * of, axis=-1, keepdims=True) + EPS)
        oh = of * rn
        dyf = dy_ref[...]
        dog_ref[...] += jnp.sum(dyf * oh, axis=0, keepdims=True)
        doh = dyf * og_ref[...]
        do = rn * (doh - oh * jnp.mean(doh * oh, axis=-1, keepdims=True))
        lane = lax.broadcasted_iota(jnp.int32, (1, 128), 1)
        lane16 = lax.broadcasted_iota(jnp.int32, (1, N_HEADS), 1)
        dsink = jnp.zeros((1, N_HEADS), F32)
        for p in range(N_PAIRS):
            j = p // PAIRS_PER_KV
            q_f = q_ref[:, p * 128:(p + 1) * 128]
            qp = q_f.astype(BF16)
            q_t = q_f.T.astype(BF16)
            do_p = do[:, p * 128:(p + 1) * 128]
            o_p = of[:, p * 128:(p + 1) * 128]
            do_b = do_p.astype(BF16)
            do_t = do_p.T.astype(BF16)
            prod = do_p * o_p
            dq_pair = jnp.zeros((WINDOW, 128), F32)
            for par in range(2):
                r = 2 * p + par
                half = (lane < HEAD_DIM) if par == 0 else (lane >= HEAD_DIM)
                pn, ps = _attn_probs(qp, kv[j][par], own, prev_bias, _sink(sinks_v, r))
                delta = jnp.sum(jnp.where(half, prod, 0.0), axis=1, keepdims=True)
                dP = _fold(own, _dot_nt(do_b, vv[j][par]))
                dS = pn * (dP - delta)
                dsink = dsink + jnp.where(lane16 == r, -jnp.sum(ps * delta, axis=0, keepdims=True), 0.0)
                dS_parts = _unfold(own, (dS * ATTN_SCALE).astype(BF16))
                p_parts = _unfold(own, pn.astype(BF16))
                at = ((p % PAIRS_PER_KV) * 2 + par) * WINDOW
                qt_scr[j, :, at:at + WINDOW] = q_t[par * HEAD_DIM:(par + 1) * HEAD_DIM]
                dot_scr[j, :, at:at + WINDOW] = do_t[par * HEAD_DIM:(par + 1) * HEAD_DIM]
                for blk in range(2):
                    dq_pair = dq_pair + _dot_nn(dS_parts[blk], kv[j][par][blk * WINDOW:(blk + 1) * WINDOW])
                    ds_scr[j, blk, at:at + WINDOW, :] = dS_parts[blk]
                    p_scr[j, blk, at:at + WINDOW, :] = p_parts[blk]
            dq_ref[:, p * 128:(p + 1) * 128] = dq_pair.astype(BF16)
        rows = [pl.multiple_of(jnp.maximum(n - 1, 0) * WINDOW, WINDOW), pl.multiple_of(n * WINDOW, WINDOW)]
        for lhs, rhs, ref in [(qt_scr, ds_scr, dk_ref), (dot_scr, p_scr, dv_ref)]:
            for blk in range(2):
                both_t = jnp.concatenate([_dot_nn(lhs[j], rhs[j, blk]) for j in range(2)], axis=0)
                ref[pl.ds(rows[blk], WINDOW), :] += both_t.T
        ds_ref[...] += dsink

    full_kv = pl.BlockSpec((T, KV_W), lambda n: (0, 0))
    blk = pl.BlockSpec((WINDOW, ATTN_W), lambda n: (n, 0))
    return pl.pallas_call(
        body, grid=(nb,),
        in_specs=[pl.BlockSpec((WINDOW, ATTN_W), lambda n: (n, OFF_Q // ATTN_W)), *_kv_specs(),
                  pl.BlockSpec((1, N_HEADS), lambda n: (0, 0)), pl.BlockSpec((1, ATTN_W), lambda n: (0, 0)),
                  blk, pl.BlockSpec((WINDOW, ATTN_W), lambda n: (n, 1)), ANY],
        out_specs=[pl.BlockSpec((WINDOW, ATTN_W), lambda n: (n, OFF_Q // ATTN_W)), full_kv, full_kv,
                   pl.BlockSpec((1, N_HEADS), lambda n: (0, 0)), pl.BlockSpec((1, ATTN_W), lambda n: (0, 0))],
        out_shape=[jax.ShapeDtypeStruct(dproj.shape, BF16), jax.ShapeDtypeStruct((T, KV_W), F32),
                   jax.ShapeDtypeStruct((T, KV_W), F32), jax.ShapeDtypeStruct((1, N_HEADS), F32),
                   jax.ShapeDtypeStruct((1, ATTN_W), F32)],
        scratch_shapes=[pltpu.VMEM((2, HEAD_DIM, 8 * WINDOW), BF16), pltpu.VMEM((2, HEAD_DIM, 8 * WINDOW), BF16),
                        pltpu.VMEM((2, 2, 8 * WINDOW, WINDOW), BF16), pltpu.VMEM((2, 2, 8 * WINDOW, WINDOW), BF16)],
        input_output_aliases={9: 0}, name=name, compiler_params=_cparams(("arbitrary",)),
    )(proj, proj, proj, proj, proj, sinks, og, o, dy, dproj)


ANY = pl.BlockSpec(memory_space=pl.ANY)


def _coords():
    return lax.axis_index("x"), lax.axis_index("y"), lax.axis_index("c")


def _all_gather(arrs, name):
    n = len(arrs)

    def body(*refs):
        ins, outs = refs[:n], refs[n:2 * n]
        send_sems, recv_sems, local_sems = refs[2 * n:]
        x, y, c = _coords()
        me = 4 * x + 2 * y + c
        sibling = (x, y, 1 - c)
        chips = [(1 - x, y), (x, 1 - y), (1 - x, 1 - y)]

        def copy(a, k, block, to, src=None):
            dst = outs[a].at[block]
            return pltpu.make_async_remote_copy(
                src_ref=dst if src is None else src, dst_ref=dst, send_sem=send_sems.at[a, k],
                recv_sem=recv_sems.at[a, k], device_id=to, device_id_type=MESH)

        mine = [pltpu.make_async_copy(ins[a], outs[a].at[me], local_sems.at[a]) for a in range(n)]
        for cp in mine:
            cp.start()
        first = []
        for a in range(n):
            first.append(copy(a, 0, me, sibling, src=ins[a]))
            for j, chip in enumerate(chips):
                first.append(copy(a, 1 + j, me, (*chip, c), src=ins[a]))
        for cp in first:
            cp.start()
        passed = []
        for j, (px, py) in enumerate(chips):
            blk = 4 * px + 2 * py + c
            for a in range(n):
                copy(a, 1 + j, blk, sibling).wait_recv()
                fwd = copy(a, 4 + j, blk, sibling)
                fwd.start()
                passed.append(fwd)
        for a in range(n):
            copy(a, 0, 4 * x + 2 * y + (1 - c), sibling).wait_recv()
            for j, (px, py) in enumerate(chips):
                copy(a, 4 + j, 4 * px + 2 * py + (1 - c), sibling).wait_recv()
        for cp in first + passed:
            cp.wait_send()
        for cp in mine:
            cp.wait()

    return pl.pallas_call(
        body, in_specs=[ANY] * n, out_specs=[ANY] * n,
        out_shape=[jax.ShapeDtypeStruct((N_DEV,) + a.shape, a.dtype) for a in arrs],
        scratch_shapes=[pltpu.SemaphoreType.DMA((n, 7)), pltpu.SemaphoreType.DMA((n, 7)),
                        pltpu.SemaphoreType.DMA((n,))],
        name=name,
    )(*arrs)


def _exchange_pair(arrs, name):
    n = len(arrs)

    def body(*refs):
        ins, outs = refs[:n], refs[n:2 * n]
        send_sems, recv_sems = refs[2 * n:]
        x, y, c = _coords()
        cps = []
        for a in range(n):
            for q in range(4):
                cps.append(pltpu.make_async_remote_copy(
                    src_ref=ins[a].at[2 * q + (1 - c)], dst_ref=outs[a].at[q], send_sem=send_sems.at[a, q],
                    recv_sem=recv_sems.at[a, q], device_id=(x, y, 1 - c), device_id_type=MESH))
        for cp in cps:
            cp.start()
        for cp in cps:
            cp.wait()

    return pl.pallas_call(
        body, in_specs=[ANY] * n, out_specs=[ANY] * n,
        out_shape=[jax.ShapeDtypeStruct((4,) + a.shape[1:], a.dtype) for a in arrs],
        scratch_shapes=[pltpu.SemaphoreType.DMA((n, 4)), pltpu.SemaphoreType.DMA((n, 4))],
        name=name,
    )(*arrs)


def _exchange_chips(arrs, name):
    n = len(arrs)

    def body(*refs):
        ins, outs = refs[:n], refs[n:2 * n]
        send_sems, recv_sems = refs[2 * n:]
        x, y, c = _coords()
        chips = [(1 - x, y), (x, 1 - y), (1 - x, 1 - y)]
        cps = []
        for a in range(n):
            for k, (tx, ty) in enumerate(chips):
                cps.append(pltpu.make_async_remote_copy(
                    src_ref=ins[a].at[2 * tx + ty], dst_ref=outs[a].at[k], send_sem=send_sems.at[a, k],
                    recv_sem=recv_sems.at[a, k], device_id=(tx, ty, c), device_id_type=MESH))
        for cp in cps:
            cp.start()
        for cp in cps:
            cp.wait()

    return pl.pallas_call(
        body, in_specs=[ANY] * n, out_specs=[ANY] * n,
        out_shape=[jax.ShapeDtypeStruct((3,) + a.shape[1:], a.dtype) for a in arrs],
        scratch_shapes=[pltpu.SemaphoreType.DMA((n, 3)), pltpu.SemaphoreType.DMA((n, 3))],
        name=name,
    )(*arrs)


HBM = pl.BlockSpec(memory_space=pltpu.HBM)
SEM = pl.BlockSpec(memory_space=pltpu.SEMAPHORE)
EFFECT = pltpu.SideEffectType.DATAFLOW_SIDE_EFFECTING


def _in_hbm(a):
    return pltpu.with_memory_space_constraint(a, pltpu.HBM)


def _remote_start(srcs, lands, plan, n_copies, name, after=None):
    ns, nb = len(srcs), len(srcs) + len(lands)
    n_after = 0 if after is None else 1

    def body(*refs):
        src_refs, land_refs = refs[:ns], refs[ns:nb]
        send_sems, recv_sems = refs[nb + n_after], refs[nb + n_after + 1]
        token = refs[-1]
        x, y, c = _coords()
        for i, (sv, dv, dev) in enumerate(plan(src_refs, land_refs, x, y, c)):
            pltpu.make_async_remote_copy(src_ref=sv, dst_ref=dv, send_sem=send_sems.at[i], recv_sem=recv_sems.at[i],
                                         device_id=dev, device_id_type=MESH).start()
        token[...] = jnp.zeros_like(token)

    bufs = list(srcs) + list(lands)
    outs = pl.pallas_call(
        body, name=name,
        out_shape=(pltpu.SemaphoreType.DMA((n_copies,)), pltpu.SemaphoreType.DMA((n_copies,)),
                   *[pltpu.HBM(b.shape, b.dtype) for b in bufs], jax.ShapeDtypeStruct((8, 128), F32)),
        in_specs=[HBM] * nb + [ANY] * n_after,
        out_specs=(SEM, SEM, *[HBM] * nb, pl.BlockSpec(memory_space=pltpu.VMEM)),
        input_output_aliases={i: 2 + i for i in range(nb)},
        compiler_params=pltpu.CompilerParams(has_side_effects=EFFECT),
    )(*[_in_hbm(b) for b in bufs], *([] if after is None else [after]))
    return outs[0], outs[1], list(outs[2:2 + ns]), list(outs[2 + ns:2 + nb]), outs[-1]


def _remote_wait(started, after, plan, name):
    send_sems, recv_sems, srcs, lands, _ = started
    ns, nb = len(srcs), len(srcs) + len(lands)

    def body(*refs):
        src_refs, land_refs = refs[:ns], refs[ns:nb]
        send_sems, recv_sems = refs[nb], refs[nb + 1]
        x, y, c = _coords()
        for i, (sv, dv, dev) in enumerate(plan(src_refs, land_refs, x, y, c)):
            cp = pltpu.make_async_remote_copy(src_ref=sv, dst_ref=dv, send_sem=send_sems.at[i],
                                              recv_sem=recv_sems.at[i], device_id=dev, device_id_type=MESH)
            cp.wait_send()
            cp.wait_recv()

    bufs = list(srcs) + list(lands)
    outs = pl.pallas_call(
        body, name=name, out_shape=tuple(pltpu.HBM(b.shape, b.dtype) for b in bufs),
        in_specs=[HBM] * nb + [SEM, SEM, ANY], out_specs=tuple([HBM] * nb),
        input_output_aliases={i: i for i in range(nb)},
        compiler_params=pltpu.CompilerParams(has_side_effects=EFFECT),
    )(*bufs, send_sems, recv_sems, after)
    return list(outs[:ns]), list(outs[ns:])


def _gather_plan(src_refs, land_refs, x, y, c):
    me = 4 * x + 2 * y + c
    plan = []
    for s, l in zip(src_refs, land_refs):
        for dev in [(x, y, 1 - c), (1 - x, y, c), (x, 1 - y, c), (1 - x, 1 - y, c)]:
            plan.append((s, l.at[me], dev))
    return plan


def _forward_plan(src_refs, land_refs, x, y, c):
    plan = []
    for l in land_refs:
        for px, py in [(1 - x, y), (x, 1 - y), (1 - x, 1 - y)]:
            blk = l.at[4 * px + 2 * py + c]
            plan.append((blk, blk, (x, y, 1 - c)))
    return plan


def _pair_plan(src_refs, land_refs, x, y, c):
    plan = []
    for s, l in zip(src_refs, land_refs):
        for q in range(4):
            plan.append((s.at[2 * q + (1 - c)], l.at[q], (x, y, 1 - c)))
    return plan


def _pair4_plan(src_refs, land_refs, x, y, c):
    plan = []
    for s, l in zip(src_refs, land_refs):
        for q in range(4):
            plan.append((s.at[q], l.at[q], (x, y, 1 - c)))
    return plan


def _chips_plan(src_refs, land_refs, x, y, c):
    plan = []
    for s, l in zip(src_refs, land_refs):
        for k, (tx, ty) in enumerate([(1 - x, y), (x, 1 - y), (1 - x, 1 - y)]):
            plan.append((s.at[2 * tx + ty], l.at[k], (tx, ty, c)))
    return plan


def _everyone_plan(src_refs, land_refs, x, y, c):
    me = 4 * x + 2 * y + c
    plan = []
    for s, l in zip(src_refs, land_refs):
        for fx, fy, fc in [(0, 0, 1), (1, 0, 0), (1, 0, 1), (0, 1, 0), (0, 1, 1), (1, 1, 0), (1, 1, 1)]:
            dev = ((1 - x) if fx else x, (1 - y) if fy else y, (1 - c) if fc else c)
            plan.append((s, l.at[me], dev))
    return plan


def _gather_finish(gathered, name):
    n = len(gathered)

    def body(*refs):
        outs = refs[n:2 * n]
        send_sems, recv_sems = refs[2 * n:]
        x, y, c = _coords()
        cps = []
        for a in range(n):
            for j, (px, py) in enumerate([(1 - x, y), (x, 1 - y), (1 - x, 1 - y)]):
                blk = outs[a].at[4 * px + 2 * py + c]
                got = outs[a].at[4 * px + 2 * py + (1 - c)]
                cps.append((pltpu.make_async_remote_copy(
                    src_ref=blk, dst_ref=blk, send_sem=send_sems.at[a, j], recv_sem=recv_sems.at[a, j],
                    device_id=(x, y, 1 - c), device_id_type=MESH), pltpu.make_async_remote_copy(
                    src_ref=got, dst_ref=got, send_sem=send_sems.at[a, j], recv_sem=recv_sems.at[a, j],
                    device_id=(x, y, 1 - c), device_id_type=MESH)))
        for cp, _ in cps:
            cp.start()
        for cp, arrival in cps:
            cp.wait_send()
            arrival.wait_recv()

    return pl.pallas_call(
        body, in_specs=[ANY] * n, out_specs=[ANY] * n,
        out_shape=[jax.ShapeDtypeStruct(g.shape, g.dtype) for g in gathered],
        input_output_aliases={a: a for a in range(n)},
        scratch_shapes=[pltpu.SemaphoreType.DMA((n, 3)), pltpu.SemaphoreType.DMA((n, 3))],
        name=name,
    )(*gathered)


def _pair_add(g8, r1, csel, tr, name):
    _, R, C = r1.shape
    g4 = g8.reshape(4, 2, R, C)

    def body(c_ref, g_ref, r_ref, o_ref):
        o_ref[...] = (g_ref[...].astype(F32) + r_ref[...].astype(F32)).astype(BF16)

    return pl.pallas_call(
        body,
        grid_spec=pltpu.PrefetchScalarGridSpec(
            num_scalar_prefetch=1, grid=(4, R // tr),
            in_specs=[pl.BlockSpec((None, None, tr, C), lambda q, i, cs: (q, cs[0], i, 0)),
                      pl.BlockSpec((None, tr, C), lambda q, i, cs: (q, i, 0))],
            out_specs=pl.BlockSpec((None, tr, C), lambda q, i, cs: (q, i, 0))),
        out_shape=jax.ShapeDtypeStruct((4, R, C), BF16), name=name,
        compiler_params=_cparams(("parallel", "parallel")),
    )(csel, g4, r1)


def _adamw_math(w, g, m, v):
    m = ADAM_B1 * m + (1.0 - ADAM_B1) * g
    v = ADAM_B2 * v + (1.0 - ADAM_B2) * (g * g)
    m_hat = m / (1.0 - ADAM_B1 ** ADAM_STEP)
    v_hat = v / (1.0 - ADAM_B2 ** ADAM_STEP)
    delta = -ADAM_LR * (m_hat / (jnp.sqrt(v_hat) + ADAM_EPS) + ADAM_WD * w)
    return delta, m, v


def _adamw_big(w, m, v, p4, r3, qsel, tile, name):
    R, C = w.shape
    tr, tc = tile

    def body(q_ref, w_ref, m_ref, v_ref, p_ref, r_ref, g_out, d_out, m_out, v_out):
        g = p_ref[...].astype(F32) + r_ref[0].astype(F32) + r_ref[1].astype(F32) + r_ref[2].astype(F32)
        d, mn, vn = _adamw_math(w_ref[...], g, m_ref[...], v_ref[...])
        g_out[...] = g
        d_out[...] = d
        m_out[...] = mn
        v_out[...] = vn

    blk = pl.BlockSpec((tr, tc), lambda i, j, qs: (i, j))
    return pl.pallas_call(
        body,
        grid_spec=pltpu.PrefetchScalarGridSpec(
            num_scalar_prefetch=1, grid=(R // tr, C // tc),
            in_specs=[blk, blk, blk, pl.BlockSpec((None, tr, tc), lambda i, j, qs: (qs[0], i, j)),
                      pl.BlockSpec((3, tr, tc), lambda i, j, qs: (0, i, j))],
            out_specs=[blk, blk, blk, blk]),
        out_shape=[jax.ShapeDtypeStruct((R, C), F32)] * 4, name=name,
        compiler_params=_cparams(("parallel", "parallel")),
    )(qsel, w, m, v, p4, r3)


def _sum_partials(p4, r3, qsel, tc, name):
    _, R, C = p4.shape

    def body(q_ref, p_ref, r_ref, o_ref):
        o_ref[...] = p_ref[...].astype(F32) + r_ref[0].astype(F32) + r_ref[1].astype(F32) + r_ref[2].astype(F32)

    return pl.pallas_call(
        body,
        grid_spec=pltpu.PrefetchScalarGridSpec(
            num_scalar_prefetch=1, grid=(C // tc,),
            in_specs=[pl.BlockSpec((None, R, tc), lambda j, qs: (qs[0], 0, j)),
                      pl.BlockSpec((3, R, tc), lambda j, qs: (0, 0, j))],
            out_specs=pl.BlockSpec((R, tc), lambda j, qs: (0, j))),
        out_shape=jax.ShapeDtypeStruct((R, C), F32), name=name, compiler_params=_cparams(("parallel",)),
    )(qsel, p4, r3)


def _adamw_tiled(w, g, m, v, tc, name):
    R, C = w.shape

    def body(w_ref, g_ref, m_ref, v_ref, d_out, m_out, v_out):
        d, mn, vn = _adamw_math(w_ref[...], g_ref[...], m_ref[...], v_ref[...])
        d_out[...] = d
        m_out[...] = mn
        v_out[...] = vn

    blk = pl.BlockSpec((R, tc), lambda j: (0, j))
    return pl.pallas_call(
        body, grid=(C // tc,), in_specs=[blk] * 4, out_specs=[blk] * 3,
        out_shape=[jax.ShapeDtypeStruct((R, C), F32)] * 3, name=name, compiler_params=_cparams(("parallel",)),
    )(w, g, m, v)


def _small_sum(parts, name):
    def body(p_ref, o_ref):
        acc = p_ref[0]
        for d in range(1, N_DEV):
            acc = acc + p_ref[d]
        o_ref[...] = acc

    return pl.pallas_call(
        body, out_shape=jax.ShapeDtypeStruct(parts.shape[1:], F32), name=name,
        compiler_params=_cparams(),
    )(parts)


def _adamw_small(w, g, m, v, name):
    def body(w_ref, g_ref, m_ref, v_ref, d_out, m_out, v_out):
        d, mn, vn = _adamw_math(w_ref[...], g_ref[...], m_ref[...], v_ref[...])
        d_out[...] = d
        m_out[...] = mn
        v_out[...] = vn

    return pl.pallas_call(
        body, out_shape=[jax.ShapeDtypeStruct(w.shape, F32)] * 3, name=name, compiler_params=_cparams(),
    )(w, g, m, v)


def _row(*pieces):
    r = jnp.concatenate([p.reshape(1, -1) for p in pieces], axis=1)
    return jnp.pad(r, ((0, 0), (0, D_MODEL - r.shape[1])))


def _pack_small(mix, convb, ssmg, attng, mlpg, fing, convw, dtb, alog, dsk, sinks, extra=None):
    last = [dtb, alog, dsk, sinks] + ([extra] if extra is not None else [])
    rows = [_row(mix), _row(convb), _row(ssmg, attng), _row(mlpg), _row(fing),
            jnp.pad(convw, ((0, 0), (0, D_MODEL - convw.shape[1]))), _row(*last)]
    packed = jnp.concatenate(rows, axis=0)
    return jnp.pad(packed, ((0, SMALL_ROWS - packed.shape[0]), (0, 0)))


def _unpack_small(p, conv_n):
    return dict(
        mix_norm_g=p[0:1, :], conv_b=p[1:2, :], ssm_norm_g=p[2:3, :D_INNER], attn_out_norm_g=p[2:3, D_INNER:],
        mlp_norm_g=p[3:4, :], final_norm_g=p[4, :], conv_w=p[5:9, :conv_n][None],
        dt_bias=p[9:10, 0:16], A_log=p[9:10, 16:32], D_skip=p[9:10, 32:48], attn_sinks=p[9:10, 48:64])


SMALL_NAMES = ["mix_norm_g", "conv_w", "conv_b", "dt_bias", "A_log", "D_skip", "ssm_norm_g", "attn_sinks",
               "attn_out_norm_g", "mlp_norm_g", "final_norm_g"]
WEIGHT_ORDER = ["mix_norm_g", "w_in", "conv_w", "conv_b", "dt_bias", "A_log", "D_skip", "ssm_norm_g", "attn_sinks",
                "attn_out_norm_g", "w_out", "mlp_norm_g", "w_up", "w_down", "final_norm_g"]


def _to_my_columns(w_nat):
    pad = jnp.zeros((w_nat.shape[0], NP - IN_PROJ), w_nat.dtype)
    return jnp.concatenate([w_nat[:, :NAT_DT], w_nat[:, NAT_DT + N_HEADS:], w_nat[:, NAT_DT:NAT_DT + N_HEADS], pad],
                           axis=1)


PER = IN_PROJ // N_DEV
SUPER_STEP = 544
SUPER = 576


def _natural_rows(g, lo, hi):
    segments = [(0, NAT_DT, 0), (NAT_DT, NAT_DT + N_HEADS, OFF_DT - NAT_DT), (NAT_DT + N_HEADS, IN_PROJ, -N_HEADS),
                (IN_PROJ, NP, 0)]
    pieces = [g[max(lo, a) + shift:min(hi, b) + shift] for a, b, shift in segments if max(lo, a) < min(hi, b)]
    return pieces[0] if len(pieces) == 1 else jnp.concatenate(pieces, axis=0)


def _w_in_from_super_slabs(sup):
    seam = SUPER - SUPER_STEP
    units = []
    for i in range(N_DEV):
        base = SUPER_STEP * i
        units.append((base, base + seam, sup[i, :seam] if i == 0 else sup[i - 1, SUPER_STEP:] + sup[i, :seam]))
        units.append((base + seam, base + SUPER_STEP, sup[i, seam:SUPER_STEP]))
    units.append((SUPER_STEP * N_DEV, SUPER_STEP * N_DEV + seam, sup[N_DEV - 1, SUPER_STEP:]))

    def natural(lo, hi):
        return [rows[max(lo, a) - a:min(hi, b) - a] for a, b, rows in units if max(lo, a) < min(hi, b)]

    pieces = natural(0, NAT_DT) + natural(NAT_DT + N_HEADS, IN_PROJ) + natural(NAT_DT, NAT_DT + N_HEADS)
    return jnp.concatenate(pieces + [jnp.zeros((NP - IN_PROJ, D_MODEL), sup.dtype)], axis=0)


def _to_natural_columns(w_my):
    return jnp.concatenate([w_my[:, :NAT_DT], w_my[:, OFF_DT:OFF_DT + N_HEADS], w_my[:, NAT_DT:OFF_DT]], axis=1)


SLAB = 1024


def _grad_w_up(h2, du, name, sel=None, add=None, after=None):
    T, D = h2.shape
    if sel is None:
        pick, n_slab, pre = (lambda j, *cs: j), N_DEV, None
    else:
        pre, other = sel
        pick, n_slab = (lambda j, cs: 2 * j + ((1 - cs[0]) if other else cs[0])), 4
    o_spec = pl.BlockSpec((None, SLAB, SLAB), lambda i, j, k, *cs: (j, i, 0))
    return _matmul(
        h2, du, mode="tn", grid=(D // SLAB, n_slab, 1),
        a_spec=pl.BlockSpec((T, SLAB), lambda i, j, k, *cs: (0, i)),
        b_spec=pl.BlockSpec((T, SLAB), lambda i, j, k, *cs: (0, pick(j, *cs))),
        out_shapes=[jax.ShapeDtypeStruct((n_slab, D, SLAB), BF16)], out_specs=[o_spec], tile=(SLAB, SLAB), name=name,
        extras=() if add is None else (add,), extra_specs=() if add is None else (o_spec,),
        epilogue=None if add is None else (lambda acc, r: (acc + r.astype(F32),)), after=after, prefetch=pre)[0]


def _grad_w_down(act, dx3b, name, sel=None, add=None, after=None):
    T, D = dx3b.shape
    if sel is None:
        pick, n_slab, pre = (lambda i, *cs: i), N_DEV, None
    else:
        pre, other = sel
        pick, n_slab = (lambda i, cs: 2 * i + ((1 - cs[0]) if other else cs[0])), 4
    o_spec = pl.BlockSpec((None, SLAB, SLAB), lambda i, j, k, *cs: (i, 0, j))
    return _matmul(
        act, dx3b, mode="tn", grid=(n_slab, D // SLAB, 1),
        a_spec=pl.BlockSpec((T, SLAB), lambda i, j, k, *cs: (0, pick(i, *cs))),
        b_spec=pl.BlockSpec((T, SLAB), lambda i, j, k, *cs: (0, j)),
        out_shapes=[jax.ShapeDtypeStruct((n_slab, SLAB, D), BF16)], out_specs=[o_spec], tile=(SLAB, SLAB), name=name,
        extras=() if add is None else (add,), extra_specs=() if add is None else (o_spec,),
        epilogue=None if add is None else (lambda acc, r: (acc + r.astype(F32),)), after=after, prefetch=pre)[0]


class _FixedWeights:
    def __init__(self, w_in_p, w_out_f, w_up_s, w_down_f, conv_w_f):
        self.w = (w_in_p, w_out_f, w_up_s, w_down_f, conv_w_f)
        self.grads = {}

    def mixer_weights(self, after):
        return self.w[0], self.w[4], None

    def prefetch(self, k, after):
        return None

    def out_weight(self, after):
        return self.w[1]

    def up_weight(self, after):
        return self.w[2]

    def down_weight(self, after):
        return self.w[3]

    def mlp_grads(self, h2, du, act, dx3b):
        self.grads.update(w_up=_grad_w_up(h2, du, "grad_w_up"),
                          w_down=_grad_w_down(act, dx3b, "grad_w_down").reshape(D_FF, D_MODEL))
        return None

    def out_grad(self, g_out):
        self.grads.update(w_out=g_out)
        return None

    def in_grad(self, g_in):
        self.grads.update(w_in=g_in)
        return None


def _local_step(x, tgt, p, hooks):
    T = x.shape[0]
    D = D_MODEL
    h1 = _rmsnorm_fwd(x, p["mix_norm_g"], "norm_mix")
    w_in_t, conv_w_f, token = hooks.mixer_weights(h1)
    (proj,) = _mm_simple(h1, w_in_t, mode="nt", M=T, N=NP, K=D, tm=min(T, 1024), tn=1536, tk=D, out_dtype=F32,
                         name="in_proj", after=token)
    xbc = _conv_fwd(proj, conv_w_f, p["conv_b"], "conv_fwd")
    dtT = proj[:, OFF_DT:OFF_DT + N_HEADS].T
    dtbT = p["dt_bias"].T
    alogT = p["A_log"].T
    dfull = jnp.repeat(p["D_skip"], HEAD_DIM, axis=1)
    token = hooks.prefetch("out", xbc)
    ssm_g = p["ssm_norm_g"] if token is None else p["ssm_norm_g"] + token[0:1, 0:1]
    ycat, ypre, hs = _ssd_fwd(xbc, proj, dtT, p["dt_bias"], dtbT, p["A_log"], alogT, dfull, ssm_g, "ssd_fwd")
    ycat, o_att = _attn_fwd(proj, p["attn_sinks"], p["attn_out_norm_g"], ycat, "attn_fwd")
    token = hooks.prefetch("up", ycat)
    w_out_f = hooks.out_weight(ycat if token is None else token)
    tm = min(T, 1024)
    (x2,) = _mm_simple(ycat, w_out_f, mode="nn", M=T, N=D, K=D, tm=tm, tn=1024, tk=D, out_dtype=F32, name="out_proj",
                       extras=(x,), epilogue=lambda acc, res: (acc + res,))
    h2 = _rmsnorm_fwd(x2, p["mlp_norm_g"], "norm_mlp")
    w_up_s = hooks.up_weight(h2)
    grid = (T // tm, N_DEV, 1)
    u, act = _matmul(
        h2, w_up_s, mode="nn", grid=grid,
        a_spec=pl.BlockSpec((tm, D), lambda i, j, k: (i, 0)),
        b_spec=pl.BlockSpec((None, D, 1024), lambda i, j, k: (j, 0, 0)),
        out_shapes=[jax.ShapeDtypeStruct((T, D_FF), F32), jax.ShapeDtypeStruct((T, D_FF), BF16)],
        out_specs=[pl.BlockSpec((tm, 1024), lambda i, j, k: (i, j))] * 2, tile=(tm, 1024), name="mlp_up",
        epilogue=lambda acc: (acc, jnp.square(jnp.maximum(acc, 0.0))))
    w_down_f = hooks.down_weight(act)
    (x3,) = _mm_simple(act, w_down_f, mode="nn", M=T, N=D, K=D_FF, tm=tm, tn=1024, tk=2048, out_dtype=F32,
                       name="mlp_down", extras=(x2,), epilogue=lambda acc, res: (acc + res,))
    loss_part, d_fin, dx3, dx3b = _final_loss(x3, tgt, p["final_norm_g"].reshape(1, D), "loss_head")
    (du,) = _mm_simple(dx3b, w_down_f, mode="nt", M=T, N=D_FF, K=D, tm=tm, tn=1024, tk=D, out_dtype=BF16,
                       name="mlp_down_bwd", extras=(u,),
                       epilogue=lambda acc, uu: (acc * (2.0 * jnp.maximum(uu, 0.0)),))
    token = hooks.mlp_grads(h2, du, act, dx3b)
    (dh2,) = _matmul(
        du, w_up_s, mode="nt", grid=(T // tm, D // 1024, N_DEV // 2),
        a_spec=pl.BlockSpec((tm, 2048), lambda i, j, k: (i, k)),
        b_spec=pl.BlockSpec((2, 1024, 1024), lambda i, j, k: (k, j, 0)),
        out_shapes=[jax.ShapeDtypeStruct((T, D), F32)],
        out_specs=[pl.BlockSpec((tm, 1024), lambda i, j, k: (i, j))], tile=(tm, 1024), name="mlp_up_bwd",
        after=token, dot_fn=lambda a, b: _dot_nt(a[:, :1024], b[0]) + _dot_nt(a[:, 1024:], b[1]))
    dx2, dx2b, d_mlp = _rmsnorm_bwd(dh2, x2, p["mlp_norm_g"], dx3, "norm_mlp_bwd")
    (g_out,) = _mm_simple(ycat, dx2b, mode="tn", M=D, N=D, K=T, tm=1024, tn=1024, tk=T, out_dtype=BF16,
                          name="grad_w_out")
    token = hooks.out_grad(g_out)
    (dy,) = _mm_simple(dx2b, w_out_f, mode="nt", M=T, N=D, K=D, tm=tm, tn=1024, tk=D, out_dtype=F32,
                       name="out_proj_bwd", after=token)
    dproj, dxbc_act, d_dtb, d_alog, d_dskip, d_ssmg = _ssd_bwd(
        xbc, proj, dtT, p["dt_bias"], dtbT, p["A_log"], alogT, dfull, p["ssm_norm_g"], ypre, hs, dy, "ssd_bwd")
    dproj, d_convw, d_convb = _conv_bwd(proj, dxbc_act, conv_w_f, p["conv_b"], dproj, "conv_bwd")
    dproj, dk, dv, d_sinks, d_attng = _attn_bwd(proj, p["attn_sinks"], p["attn_out_norm_g"], o_att, dy, dproj,
                                                "attn_bwd")
    dproj = lax.dynamic_update_slice(dproj, jnp.concatenate([dk, dv], axis=1).astype(BF16), (0, OFF_K))
    (g_in,) = _mm_simple(dproj, h1, mode="tn", M=NP, N=D, K=T, tm=1536, tn=1024, tk=T, out_dtype=BF16,
                         name="grad_w_in")
    token = hooks.in_grad(g_in)
    (dh1,) = _mm_simple(dproj, w_in_t, mode="nn", M=T, N=D, K=NP, tm=tm, tn=1024, tk=2304, out_dtype=F32,
                        name="in_proj_bwd", after=token)
    dx, _, d_mix = _rmsnorm_bwd(dh1, x, p["mix_norm_g"], dx2, "norm_mix_bwd")
    small = _pack_small(d_mix, d_convb, d_ssmg, d_attng, d_mlp, d_fin, d_convw, d_dtb, d_alog, d_dskip, d_sinks,
                        extra=loss_part[:, 0:1])
    return dx, small


def _rows_rotated(v, shift, name):
    R, C = v.shape
    tc = 512

    def body(s_ref, v_ref, o_ref):
        o_ref[...] = pltpu.roll(v_ref[...], s_ref[0], axis=0).astype(BF16)

    return pl.pallas_call(
        body,
        grid_spec=pltpu.PrefetchScalarGridSpec(
            num_scalar_prefetch=1, grid=(C // tc,), in_specs=[pl.BlockSpec((R, tc), lambda j, s: (0, j))],
            out_specs=pl.BlockSpec((R, tc), lambda j, s: (0, j))),
        out_shape=jax.ShapeDtypeStruct((R, C), BF16), name=name, compiler_params=_cparams(("parallel",)),
    )(shift, v)


def _landing(own, me):
    zone = lax.empty((N_DEV,) + own.shape, own.dtype)
    return lax.dynamic_update_slice(zone, own[None], (me,) + (0,) * own.ndim)


def _sequencer_gather(owns, split, me, collective_id, name):
    n = len(owns)
    zone_refs = [jax.new_ref(_landing(o, me), memory_space=pltpu.MemorySpace.HBM) for o in owns]
    own_refs = [jax.new_ref(o, memory_space=pltpu.MemorySpace.HBM) for o in owns]
    N_COPIES = 9

    @pl.kernel(mesh=plsc.ScalarSubcoreMesh(axis_name="sequencer", num_cores=1), name=name,
               scratch_types=(pltpu.SemaphoreType.DMA((n, N_COPIES)), pltpu.SemaphoreType.DMA((n, N_COPIES))),
               compiler_params=pltpu.CompilerParams(collective_id=collective_id))
    def launch(send_sems, recv_sems):
        x, y, c = _coords()
        sibling, xn, yn, diag = (x, y, 1 - c), (1 - x, y, c), (x, 1 - y, c), (1 - x, 1 - y, c)
        barrier = pltpu.get_barrier_semaphore()
        for peer in [sibling, xn, yn, diag]:
            pl.semaphore_signal(barrier, inc=1, device_id=peer, device_id_type=MESH)
        pl.semaphore_wait(barrier, 4)

        def block(a, dev, half=None):
            ref = zone_refs[a].at[4 * dev[0] + 2 * dev[1] + dev[2]]
            if half is None:
                return ref
            rows = owns[a].shape[0] // 2
            return ref.at[pl.ds(half * rows, rows)]

        def copy(a, k, src, dst, to):
            return pltpu.make_async_remote_copy(src_ref=src, dst_ref=dst, send_sem=send_sems.at[a, k],
                                                recv_sem=recv_sems.at[a, k], device_id=to, device_id_type=MESH)

        me_dev = (x, y, c)
        sent = []
        first = {}
        for a in range(n):
            for k, peer in enumerate([sibling, xn, yn] + ([] if split[a] else [diag])):
                first[a, k] = copy(a, k, own_refs[a], block(a, me_dev), peer)
                first[a, k].start()
                sent.append(first[a, k])
        from_sibling = []
        for a in range(n):
            first[a, 1].wait_recv()
            sent.append(copy(a, 4, block(a, xn), block(a, xn), sibling))
            if split[a]:
                sent.append(copy(a, 6, block(a, xn, 0), block(a, xn, 0), yn))
            first[a, 2].wait_recv()
            sent.append(copy(a, 5, block(a, yn), block(a, yn), sibling))
            if split[a]:
                sent.append(copy(a, 7, block(a, yn, 1), block(a, yn, 1), xn))
            for cp in sent[-(4 if split[a] else 2):]:
                cp.start()
        for a in range(n):
            if split[a]:
                copy(a, 6, block(a, diag, 0), block(a, diag, 0), yn).wait_recv()
                sent.append(copy(a, 8, block(a, diag, 0), block(a, diag, 0), sibling))
                sent[-1].start()
                copy(a, 7, block(a, diag, 1), block(a, diag, 1), xn).wait_recv()
                sent.append(copy(a, 3, block(a, diag, 1), block(a, diag, 1), sibling))
                sent[-1].start()
            else:
                first[a, 3].wait_recv()
                sent.append(copy(a, 8, block(a, diag), block(a, diag), sibling))
                sent[-1].start()
        for a in range(n):
            first[a, 0].wait_recv()
            copy(a, 4, block(a, xn), block(a, xn), sibling).wait_recv()
            copy(a, 5, block(a, yn), block(a, yn), sibling).wait_recv()
            if split[a]:
                copy(a, 8, block(a, diag, 0), block(a, diag, 0), sibling).wait_recv()
                copy(a, 3, block(a, diag, 1), block(a, diag, 1), sibling).wait_recv()
            else:
                copy(a, 8, block(a, diag), block(a, diag), sibling).wait_recv()
        for cp in sent:
            cp.wait_send()

    launch()
    return zone_refs


def _gather_end(started, after, plan, name):
    _, lands = _remote_wait(started, after, plan, name + "_wait")
    return _gather_finish(lands, name + "_finish")


class _ShardedWeights:
    def __init__(self, w_in, w_out, conv_w, w_up, w_down, me, csel):
        self.me, self.csel = me, csel
        padded = jnp.pad(jnp.transpose(w_in), ((0, SUPER - PER), (0, 0)))
        own_rows = _rows_rotated(padded, jnp.reshape(2 * me, (1,)).astype(jnp.int32), "w_in_super_slab")
        self.in_ref, self.conv_ref = _sequencer_gather([own_rows, conv_w], [True, False], me, 7,
                                                       "gather_w_in_sequencer")
        (self.out_ref,) = _sequencer_gather([w_out.astype(BF16)], [True], me, 8, "gather_w_out_sequencer")
        (self.up_ref,) = _sequencer_gather([w_up.astype(BF16)], [True], me, 9, "gather_w_up_sequencer")
        (self.down_ref,) = _sequencer_gather([w_down.astype(BF16)], [True], me, 10, "gather_w_down_sequencer")
        self.reduces = {}

    def mixer_weights(self, after):
        g_conv = self.conv_ref[...]
        conv_w_f = jnp.concatenate([g_conv[i] for i in range(N_DEV)], axis=1)
        return _w_in_from_super_slabs(self.in_ref[...]), conv_w_f, None

    def prefetch(self, k, after):
        return None

    def out_weight(self, after):
        return self.out_ref[...].reshape(D_MODEL, D_MODEL)

    def up_weight(self, after):
        return self.up_ref[...]

    def down_weight(self, after):
        return self.down_ref[...].reshape(D_FF, D_MODEL)

    def _chips_start(self, slabs, from_sibling, rows, tag):
        sums = [_pair_add(s, r, self.csel, tr, f"pair_add_{tag}_{i}")
                for i, (s, r, tr) in enumerate(zip(slabs, from_sibling, rows))]
        lands = [lax.empty((3,) + s.shape[1:], s.dtype) for s in sums]
        self.reduces[tag] = _remote_start(sums, lands, _chips_plan, 3 * len(sums), f"reduce_start_{tag}")
        return self.reduces[tag][4]

    def mlp_grads(self, h2, du, act, dx3b):
        def send(part, tag, after):
            st = _remote_start([part], [lax.empty(part.shape, part.dtype)], _pair4_plan, 4,
                               f"reduce_pair_start_{tag}", after=after)
            return st

        def received(st, after, tag):
            return _remote_wait(st, after, _pair4_plan, f"reduce_pair_wait_{tag}")[1][0]

        def to_chips(sums, tag):
            self.reduces[tag] = _remote_start([sums], [lax.empty((3,) + sums.shape[1:], sums.dtype)], _chips_plan, 3,
                                              f"reduce_start_{tag}")
            return self.reduces[tag][4]

        up_send = _grad_w_up(h2, du, "grad_w_up_send", sel=(self.csel, True))
        st_up = send(up_send, "up", None)
        down_send = _grad_w_down(act, dx3b, "grad_w_down_send", sel=(self.csel, True), after=st_up[4])
        st_down = send(down_send, "down", None)
        up_sum = _grad_w_up(h2, du, "grad_w_up_keep", sel=(self.csel, False), add=received(st_up, down_send, "up"),
                            after=st_down[4])
        token = to_chips(up_sum, "up")
        down_sum = _grad_w_down(act, dx3b, "grad_w_down_keep", sel=(self.csel, False),
                                add=received(st_down, up_sum, "down"), after=token)
        return to_chips(down_sum, "down")

    def out_grad(self, g_out):
        slabs = [g_out.reshape(N_DEV, D_MODEL // N_DEV, D_MODEL)]
        return self._chips_start(slabs, _exchange_pair(slabs, "reduce_pair_out"), [256], "out")

    def in_grad(self, g_in):
        slabs = [jnp.stack([_natural_rows(g_in, SUPER_STEP * j, SUPER_STEP * j + SUPER) for j in range(N_DEV)])]
        return self._chips_start(slabs, _exchange_pair(slabs, "reduce_pair_in"), [SUPER], "in")

    def small_start(self, small):
        self.st_small = _remote_start([small], [_landing(small, self.me)], _everyone_plan, N_DEV - 1, "gather_start_small")

    def small_end(self, after):
        return _remote_wait(self.st_small, after, _everyone_plan, "gather_small_wait")[1][0]

    def reduce_end(self, tag, after):
        return _remote_wait(self.reduces[tag], after, _chips_plan, f"reduce_wait_{tag}")


def kernel(x, mix_norm_g, w_in, conv_w, conv_b, dt_bias, A_log, D_skip, ssm_norm_g, attn_sinks, attn_out_norm_g, w_out, mlp_norm_g, w_up, w_down, final_norm_g, loss_target, m_mix_norm_g, m_w_in, m_conv_w, m_conv_b, m_dt_bias, m_A_log, m_D_skip, m_ssm_norm_g, m_attn_sinks, m_attn_out_norm_g, m_w_out, m_mlp_norm_g, m_w_up, m_w_down, m_final_norm_g, v_mix_norm_g, v_w_in, v_conv_w, v_conv_b, v_dt_bias, v_A_log, v_D_skip, v_ssm_norm_g, v_attn_sinks, v_attn_out_norm_g, v_w_out, v_mlp_norm_g, v_w_up, v_w_down, v_final_norm_g):
    xi, yi, ci = _coords()
    me = 4 * xi + 2 * yi + ci
    csel = jnp.reshape(ci, (1,)).astype(jnp.int32)
    qsel = jnp.reshape(2 * xi + yi, (1,)).astype(jnp.int32)
    w = dict(mix_norm_g=mix_norm_g, conv_b=conv_b, dt_bias=dt_bias, A_log=A_log, D_skip=D_skip,
             ssm_norm_g=ssm_norm_g, attn_sinks=attn_sinks, attn_out_norm_g=attn_out_norm_g, mlp_norm_g=mlp_norm_g,
             final_norm_g=final_norm_g)
    hooks = _ShardedWeights(w_in[0], w_out[0], conv_w[0], w_up[0], w_down[0], me, csel)
    p = dict(w)
    dx, small = _local_step(x[0], loss_target[0], p, hooks)
    hooks.small_start(small)
    big = {}
    after = dx
    for name, wt, mt, vt, tile in [
            ("up", w_up, m_w_up, v_w_up, (512, SLAB)), ("down", w_down, m_w_down, v_w_down, (256, D_MODEL)),
            ("out", w_out, m_w_out, v_w_out, (256, D_MODEL))]:
        (chip_sums,), (from_chips,) = hooks.reduce_end(name, after)
        res = _adamw_big(wt[0], mt[0], vt[0], chip_sums, from_chips, qsel, tile, f"adamw_w_{name}")
        big["w_" + name] = tuple(r[None] for r in res)
        after = res[0]
    (chip_sums,), (from_chips,) = hooks.reduce_end("in", after)
    g_super = _sum_partials(chip_sums, from_chips, qsel, 512, "grad_w_in_sum")
    g_in = lax.dynamic_slice(g_super, (2 * me, 0), (PER, D_MODEL))
    res = _adamw_tiled(jnp.transpose(w_in[0]), g_in, jnp.transpose(m_w_in[0]), jnp.transpose(v_w_in[0]), 512,
                       "adamw_w_in")
    big["w_in"] = tuple(jnp.transpose(r)[None] for r in (g_in, *res))
    after = res[0]
    gsum = _small_sum(hooks.small_end(after), "small_sum")
    loss = gsum[9, 64]
    gs = _unpack_small(gsum, CONV_DIM)
    cw = CONV_DIM // N_DEV
    g_conv_shard = lax.dynamic_slice(gsum[5:9, :], (0, me * cw), (CONV_K, cw))

    def pack(s):
        return _pack_small(s["mix_norm_g"], s["conv_b"], s["ssm_norm_g"], s["attn_out_norm_g"], s["mlp_norm_g"],
                           s["final_norm_g"], s["conv_w"][0], s["dt_bias"], s["A_log"], s["D_skip"], s["attn_sinks"])

    wp = pack(dict(w, conv_w=conv_w))
    mp = pack(dict(mix_norm_g=m_mix_norm_g, conv_b=m_conv_b, ssm_norm_g=m_ssm_norm_g,
                   attn_out_norm_g=m_attn_out_norm_g, mlp_norm_g=m_mlp_norm_g, final_norm_g=m_final_norm_g,
                   conv_w=m_conv_w, dt_bias=m_dt_bias, A_log=m_A_log, D_skip=m_D_skip, attn_sinks=m_attn_sinks))
    vp = pack(dict(mix_norm_g=v_mix_norm_g, conv_b=v_conv_b, ssm_norm_g=v_ssm_norm_g,
                   attn_out_norm_g=v_attn_out_norm_g, mlp_norm_g=v_mlp_norm_g, final_norm_g=v_final_norm_g,
                   conv_w=v_conv_w, dt_bias=v_dt_bias, A_log=v_A_log, D_skip=v_D_skip, attn_sinks=v_attn_sinks))
    gp = jnp.concatenate([gsum[0:5], jnp.pad(g_conv_shard, ((0, 0), (0, D_MODEL - cw))), gsum[9:10],
                          jnp.zeros((SMALL_ROWS - 10, D_MODEL), F32)], axis=0)
    dp, mnp, vnp = _adamw_small(wp, gp, mp, vp, "adamw_small")
    grads = dict(gs, conv_w=g_conv_shard[None])
    deltas = _unpack_small(dp, cw)
    new_m = _unpack_small(mnp, cw)
    new_v = _unpack_small(vnp, cw)
    for k, name in enumerate(["w_in", "w_out", "w_up", "w_down"]):
        grads[name], deltas[name], new_m[name], new_v[name] = big[name]
    return (loss, dx[None], *[grads[n] for n in WEIGHT_ORDER], *[deltas[n] for n in WEIGHT_ORDER],
            *[new_m[n] for n in WEIGHT_ORDER], *[new_v[n] for n in WEIGHT_ORDER])
```

```python
import functools

import jax
import jax.numpy as jnp
from jax import lax
from jax.experimental import pallas as pl
from jax.experimental.pallas import tpu as pltpu
from jax.experimental.pallas import tpu_sc as plsc

F32 = jnp.float32
BF16 = jnp.bfloat16
HI = lax.Precision.HIGHEST
MESH = pl.DeviceIdType.MESH

EPS = 1e-5
D_MODEL = 2048
D_INNER = 1024
N_HEADS = 16
HEAD_DIM = 64
N_GROUPS = 4
D_STATE = 128
CHUNK = 128
CONV_K = 4
CONV_DIM = 2048
ATTN_W = 1024
KV_W = 128
WINDOW = 128
D_FF = 8192
IN_PROJ = 4368
N_DEV = 8
NP = 4608
OFF_Z, OFF_X, OFF_B, OFF_C, OFF_Q, OFF_K, OFF_V, OFF_DT = 0, 1024, 2048, 2560, 3072, 4096, 4224, 4352
NAT_DT = 3072

ADAM_LR = 0.001
ADAM_B1 = 0.9
ADAM_B2 = 0.999
ADAM_EPS = 1e-08
ADAM_WD = 0.01
ADAM_STEP = 10

VMEM_LIMIT = 52 * 1024 * 1024
SMALL_ROWS = 16
NEG = -1e30


def _cparams(sem=None):
    return pltpu.CompilerParams(dimension_semantics=sem, vmem_limit_bytes=VMEM_LIMIT)


def _split3(v):
    hi = v.astype(BF16)
    rest = v - hi.astype(F32)
    mid = rest.astype(BF16)
    return hi, mid, (rest - mid.astype(F32)).astype(BF16)


def _hdot(a, b, data):
    if data == "a":
        sel = b.astype(BF16)
        return sum(_dot_nn(part, sel) for part in _split3(a))
    sel = a.astype(BF16)
    return sum(_dot_nn(sel, part) for part in _split3(b))


def _dot_nn(a, b):
    return lax.dot_general(a, b, (((1,), (0,)), ((), ())), preferred_element_type=F32)


def _dot_nt(a, b):
    return lax.dot_general(a, b, (((1,), (1,)), ((), ())), preferred_element_type=F32)


def _dot_tn(a, b):
    return lax.dot_general(a, b, (((0,), (0,)), ((), ())), preferred_element_type=F32)


def _softplus(v):
    return jnp.maximum(v, 0.0) + jnp.log1p(jnp.exp(-jnp.abs(v)))


def _sigmoid(v):
    return 1.0 / (1.0 + jnp.exp(-v))


def _matmul(a, b, *, mode, grid, a_spec, b_spec, out_shapes, out_specs, tile, name,
            extras=(), extra_specs=(), epilogue=None, after=None, dot_fn=None, prefetch=None):
    nk = grid[2]
    n_ex = len(extras)
    n_out = len(out_shapes)
    dot = dot_fn if dot_fn is not None else {"nn": _dot_nn, "nt": _dot_nt, "tn": _dot_tn}[mode]

    def finish(acc, ex_refs, out_refs):
        res = (acc,) if epilogue is None else epilogue(acc, *[e[...] for e in ex_refs])
        for o, r in zip(out_refs, res):
            o[...] = r.astype(o.dtype)

    def body(*refs):
        a_ref, b_ref = refs[0], refs[1]
        ex_refs = refs[2:2 + n_ex]
        out_refs = refs[2 + n_ex:2 + n_ex + n_out]
        part = dot(a_ref[...].astype(BF16), b_ref[...].astype(BF16))
        if nk == 1:
            finish(part, ex_refs, out_refs)
        else:
            acc_ref = refs[-1]
            k = pl.program_id(2)

            @pl.when(k == 0)
            def _():
                acc_ref[...] = part

            @pl.when(k > 0)
            def _():
                acc_ref[...] += part

            @pl.when(k == nk - 1)
            def _():
                finish(acc_ref[...], ex_refs, out_refs)

    scratch = [] if nk == 1 else [pltpu.VMEM(tile, F32)]
    n_pre = 0 if prefetch is None else 1
    tok_specs = [] if after is None else [pl.BlockSpec((8, 128), lambda *_: (0, 0))]
    tok_args = [] if after is None else [after]

    def body_with_token(*refs):
        refs = refs[n_pre:]
        body(*refs[:2 + n_ex], *refs[2 + n_ex + len(tok_args):])

    in_specs = [a_spec, b_spec, *extra_specs, *tok_specs]
    params = _cparams(("parallel", "parallel", "arbitrary"))
    if prefetch is None:
        return pl.pallas_call(
            body_with_token, grid=grid, in_specs=in_specs, out_specs=list(out_specs), out_shape=list(out_shapes),
            scratch_shapes=scratch, name=name, compiler_params=params)(a, b, *extras, *tok_args)
    return pl.pallas_call(
        body_with_token,
        grid_spec=pltpu.PrefetchScalarGridSpec(num_scalar_prefetch=1, grid=grid, in_specs=in_specs,
                                               out_specs=list(out_specs), scratch_shapes=scratch),
        out_shape=list(out_shapes), name=name, compiler_params=params)(prefetch, a, b, *extras, *tok_args)


def _mm_simple(a, b, *, mode, M, N, K, tm, tn, tk, out_dtype, name, extras=(), epilogue=None, n_out=1,
               out_dtypes=None, after=None):
    grid = (M // tm, N // tn, K // tk)
    if mode == "nn":
        a_spec = pl.BlockSpec((tm, tk), lambda i, j, k: (i, k))
        b_spec = pl.BlockSpec((tk, tn), lambda i, j, k: (k, j))
    elif mode == "nt":
        a_spec = pl.BlockSpec((tm, tk), lambda i, j, k: (i, k))
        b_spec = pl.BlockSpec((tn, tk), lambda i, j, k: (j, k))
    else:
        a_spec = pl.BlockSpec((tk, tm), lambda i, j, k: (k, i))
        b_spec = pl.BlockSpec((tk, tn), lambda i, j, k: (k, j))
    o_spec = pl.BlockSpec((tm, tn), lambda i, j, k: (i, j))
    dts = out_dtypes if out_dtypes is not None else [out_dtype] * n_out
    return _matmul(a, b, mode=mode, grid=grid, a_spec=a_spec, b_spec=b_spec,
                   out_shapes=[jax.ShapeDtypeStruct((M, N), d) for d in dts],
                   out_specs=[o_spec] * len(dts), tile=(tm, tn), name=name,
                   extras=extras, extra_specs=[o_spec] * len(extras), epilogue=epilogue, after=after)


ROW_BLOCK = 256


def _rmsnorm_fwd(x, g, name):
    T, D = x.shape

    def body(x_ref, g_ref, o_ref):
        xf = x_ref[...]
        r = lax.rsqrt(jnp.mean(xf * xf, axis=-1, keepdims=True) + EPS)
        o_ref[...] = (xf * r * g_ref[...]).astype(BF16)

    return pl.pallas_call(
        body, grid=(T // ROW_BLOCK,),
        in_specs=[pl.BlockSpec((ROW_BLOCK, D), lambda i: (i, 0)), pl.BlockSpec((1, D), lambda i: (0, 0))],
        out_specs=pl.BlockSpec((ROW_BLOCK, D), lambda i: (i, 0)),
        out_shape=jax.ShapeDtypeStruct((T, D), BF16), name=name, compiler_params=_cparams(("parallel",)),
    )(x, g)


def _rmsnorm_bwd(dh, x, g, dres, name):
    T, D = x.shape

    def body(dh_ref, x_ref, g_ref, dres_ref, dx_ref, dxb_ref, dg_ref):
        i = pl.program_id(0)
        xf = x_ref[...]
        r = lax.rsqrt(jnp.mean(xf * xf, axis=-1, keepdims=True) + EPS)
        xh = xf * r
        d = dh_ref[...]

        @pl.when(i == 0)
        def _():
            dg_ref[...] = jnp.zeros_like(dg_ref)

        dg_ref[...] += jnp.sum(d * xh, axis=0, keepdims=True)
        dxh = d * g_ref[...]
        dx = r * (dxh - xh * jnp.mean(dxh * xh, axis=-1, keepdims=True)) + dres_ref[...]
        dx_ref[...] = dx
        dxb_ref[...] = dx.astype(BF16)

    row = pl.BlockSpec((ROW_BLOCK, D), lambda i: (i, 0))
    vec = pl.BlockSpec((1, D), lambda i: (0, 0))
    return pl.pallas_call(
        body, grid=(T // ROW_BLOCK,), in_specs=[row, row, vec, row], out_specs=[row, row, vec],
        out_shape=[jax.ShapeDtypeStruct((T, D), F32), jax.ShapeDtypeStruct((T, D), BF16),
                   jax.ShapeDtypeStruct((1, D), F32)],
        name=name, compiler_params=_cparams(("arbitrary",)),
    )(dh, x, g, dres)


def _final_loss(x3, tgt, g, name):
    T, D = x3.shape

    def body(x_ref, t_ref, g_ref, loss_ref, dg_ref, dx_ref, dxb_ref):
        i = pl.program_id(0)
        xf = x_ref[...]
        r = lax.rsqrt(jnp.mean(xf * xf, axis=-1, keepdims=True) + EPS)
        xh = xf * r
        gg = g_ref[...]
        err = xh * gg - t_ref[...]

        @pl.when(i == 0)
        def _():
            dg_ref[...] = jnp.zeros_like(dg_ref)
            loss_ref[...] = jnp.zeros_like(loss_ref)

        part = jnp.sum(jnp.sum(err * err, axis=-1, keepdims=True), axis=0, keepdims=True) * (0.5 / D)
        loss_ref[...] += jnp.broadcast_to(part, loss_ref.shape)
        dout = err * (1.0 / D)
        dg_ref[...] += jnp.sum(dout * xh, axis=0, keepdims=True)
        dxh = dout * gg
        dx = r * (dxh - xh * jnp.mean(dxh * xh, axis=-1, keepdims=True))
        dx_ref[...] = dx
        dxb_ref[...] = dx.astype(BF16)

    row = pl.BlockSpec((ROW_BLOCK, D), lambda i: (i, 0))
    vec = pl.BlockSpec((1, D), lambda i: (0, 0))
    return pl.pallas_call(
        body, grid=(T // ROW_BLOCK,), in_specs=[row, row, vec],
        out_specs=[pl.BlockSpec((1, 128), lambda i: (0, 0)), vec, row, row],
        out_shape=[jax.ShapeDtypeStruct((1, 128), F32), jax.ShapeDtypeStruct((1, D), F32),
                   jax.ShapeDtypeStruct((T, D), F32), jax.ShapeDtypeStruct((T, D), BF16)],
        name=name, compiler_params=_cparams(("arbitrary",)),
    )(x3, tgt, g)


CONV_BLOCK = 256


def _conv_apply(u, w, b):
    row = lax.broadcasted_iota(jnp.int32, u.shape, 0)
    acc = b + w[CONV_K - 1:CONV_K, :] * u
    shifted = []
    for j in range(1, CONV_K):
        uj = jnp.where(row >= j, pltpu.roll(u, j, axis=0), 0.0)
        shifted.append(uj)
        acc = acc + w[CONV_K - 1 - j:CONV_K - j, :] * uj
    return acc, shifted


def _conv_fwd(proj, conv_w, conv_b, name):
    T = proj.shape[0]
    cb0 = OFF_X // CONV_BLOCK

    def body(u_ref, w_ref, b_ref, o_ref):
        c, _ = _conv_apply(u_ref[...], w_ref[...], b_ref[...])
        o_ref[...] = c * _sigmoid(c)

    return pl.pallas_call(
        body, grid=(CONV_DIM // CONV_BLOCK,),
        in_specs=[pl.BlockSpec((T, CONV_BLOCK), lambda j: (0, cb0 + j)),
                  pl.BlockSpec((CONV_K, CONV_BLOCK), lambda j: (0, j)),
                  pl.BlockSpec((1, CONV_BLOCK), lambda j: (0, j))],
        out_specs=pl.BlockSpec((T, CONV_BLOCK), lambda j: (0, j)),
        out_shape=jax.ShapeDtypeStruct((T, CONV_DIM), F32), name=name, compiler_params=_cparams(("parallel",)),
    )(proj, conv_w, conv_b)


def _conv_bwd(proj, dact, conv_w, conv_b, dproj, name):
    T = proj.shape[0]
    cb0 = OFF_X // CONV_BLOCK

    def body(u_ref, d_ref, w_ref, b_ref, _, du_ref, dw_ref, db_ref):
        u = u_ref[...]
        w = w_ref[...]
        c, shifted = _conv_apply(u, w, b_ref[...])
        sg = _sigmoid(c)
        dc = d_ref[...] * sg * (1.0 + c * (1.0 - sg))
        row = lax.broadcasted_iota(jnp.int32, u.shape, 0)
        du = w[CONV_K - 1:CONV_K, :] * dc
        dw_ref[CONV_K - 1:CONV_K, :] = jnp.sum(dc * u, axis=0, keepdims=True)
        for j in range(1, CONV_K):
            dcj = jnp.where(row < T - j, pltpu.roll(dc, T - j, axis=0), 0.0)
            du = du + w[CONV_K - 1 - j:CONV_K - j, :] * dcj
            dw_ref[CONV_K - 1 - j:CONV_K - j, :] = jnp.sum(dc * shifted[j - 1], axis=0, keepdims=True)
        db_ref[...] = jnp.sum(dc, axis=0, keepdims=True)
        du_ref[...] = du.astype(BF16)

    return pl.pallas_call(
        body, grid=(CONV_DIM // CONV_BLOCK,),
        in_specs=[pl.BlockSpec((T, CONV_BLOCK), lambda j: (0, cb0 + j)),
                  pl.BlockSpec((T, CONV_BLOCK), lambda j: (0, j)),
                  pl.BlockSpec((CONV_K, CONV_BLOCK), lambda j: (0, j)),
                  pl.BlockSpec((1, CONV_BLOCK), lambda j: (0, j)), pl.BlockSpec(memory_space=pl.ANY)],
        out_specs=[pl.BlockSpec((T, CONV_BLOCK), lambda j: (0, cb0 + j)),
                   pl.BlockSpec((CONV_K, CONV_BLOCK), lambda j: (0, j)),
                   pl.BlockSpec((1, CONV_BLOCK), lambda j: (0, j))],
        out_shape=[jax.ShapeDtypeStruct(dproj.shape, BF16), jax.ShapeDtypeStruct((CONV_K, CONV_DIM), F32),
                   jax.ShapeDtypeStruct((1, CONV_DIM), F32)],
        input_output_aliases={4: 0}, name=name, compiler_params=_cparams(("parallel",)),
    )(proj, dact, conv_w, conv_b, dproj)


GROUP_W = D_INNER // N_GROUPS
HEADS_PER_GROUP = N_HEADS // N_GROUPS


def _expand_mat():
    h = lax.broadcasted_iota(jnp.int32, (N_HEADS, D_INNER), 0)
    j = lax.broadcasted_iota(jnp.int32, (N_HEADS, D_INNER), 1)
    return (j // HEAD_DIM == h).astype(F32)


def _reduce_mat(g):
    j = lax.broadcasted_iota(jnp.int32, (GROUP_W, N_HEADS), 0)
    h = lax.broadcasted_iota(jnp.int32, (GROUP_W, N_HEADS), 1)
    return (g * HEADS_PER_GROUP + j // HEAD_DIM == h).astype(F32)


def _col16(v, h):
    lane = lax.broadcasted_iota(jnp.int32, v.shape, 1)
    return jnp.sum(jnp.where(lane == h, v, 0.0), axis=1, keepdims=True)


def _ssd_pre(dt_raw, dtT_raw, dtb, dtbT, alog, alogT):
    Q = CHUNK
    xdt = dt_raw + dtb
    dt = _softplus(xdt)
    dtT = _softplus(dtT_raw + dtbT)
    A = -jnp.exp(alog)
    AT = -jnp.exp(alogT)
    row = lax.broadcasted_iota(jnp.int32, (Q, Q), 0)
    col = lax.broadcasted_iota(jnp.int32, (Q, Q), 1)
    tril = (row >= col).astype(F32)
    triu = (row <= col).astype(F32)
    cs = _hdot(tril, dt * A, "b")
    csT = _hdot(dtT * AT, triu, "a")
    return xdt, dt, A, cs, csT, row >= col, triu


def _decay_matrix(cs, csT, h, causal):
    seg = _col16(cs, h) - csT[h:h + 1, :]
    return jnp.where(causal, jnp.exp(jnp.minimum(seg, 0.0)), 0.0)


def _ssd_in_specs(nc, rev):
    def cidx(c):
        return (nc - 1 - c) if rev else c

    return [
        pl.BlockSpec((CHUNK, D_INNER), lambda c: (cidx(c), 0)),
        pl.BlockSpec((CHUNK, 512), lambda c: (cidx(c), 2)),
        pl.BlockSpec((CHUNK, 512), lambda c: (cidx(c), 3)),
        pl.BlockSpec((CHUNK, D_INNER), lambda c: (cidx(c), 0)),
        pl.BlockSpec((CHUNK, 128), lambda c: (cidx(c), OFF_DT // 128)),
        pl.BlockSpec((N_HEADS, CHUNK), lambda c: (0, cidx(c))),
        pl.BlockSpec((1, N_HEADS), lambda c: (0, 0)),
        pl.BlockSpec((N_HEADS, 1), lambda c: (0, 0)),
        pl.BlockSpec((1, N_HEADS), lambda c: (0, 0)),
        pl.BlockSpec((N_HEADS, 1), lambda c: (0, 0)),
        pl.BlockSpec((1, D_INNER), lambda c: (0, 0)),
        pl.BlockSpec((1, D_INNER), lambda c: (0, 0)),
    ]


def _ssd_fwd(xbc, proj, dtT, dtb, dtbT, alog, alogT, dfull, ng, name):
    T = xbc.shape[0]
    nc = T // CHUNK
    Q = CHUNK

    def body(xs_ref, B_ref, C_ref, z_ref, dt_ref, dtT_ref, dtb_ref, dtbT_ref, al_ref, alT_ref, df_ref, ng_ref,
             y_ref, ypre_ref, hs_ref, h_scr):
        c = pl.program_id(0)

        @pl.when(c == 0)
        def _():
            h_scr[...] = jnp.zeros_like(h_scr)

        _, dt, _, cs, csT, causal, _ = _ssd_pre(dt_ref[:, :N_HEADS], dtT_ref[...], dtb_ref[...], dtbT_ref[...],
                                                al_ref[...], alT_ref[...])
        ex = _expand_mat()
        dt_full = _hdot(dt, ex, "a")
        cs_full = _hdot(cs, ex, "a")
        cs_last = cs_full[Q - 1:Q, :]
        xs = xs_ref[...]
        xd = xs * dt_full
        e_full = jnp.exp(cs_full)
        dec_full = jnp.exp(cs_last - cs_full)
        cd_full = jnp.exp(cs_last)
        lane_head = lax.broadcasted_iota(jnp.int32, (1, GROUP_W), 1) // HEAD_DIM
        for g in range(N_GROUPS):
            sl = slice(g * GROUP_W, (g + 1) * GROUP_W)
            Bg = B_ref[:, g * D_STATE:(g + 1) * D_STATE].astype(BF16)
            Cg = C_ref[:, g * D_STATE:(g + 1) * D_STATE].astype(BF16)
            CB = _dot_nt(Cg, Bg)
            hg = h_scr[g]
            yoff = _dot_nn(Cg, hg.astype(BF16)) * e_full[:, sl]
            xd_g = xd[:, sl]
            S = _dot_tn(Bg, (xd_g * dec_full[:, sl]).astype(BF16))
            xd_b = xd_g.astype(BF16)
            ydiag = jnp.zeros((Q, GROUP_W), F32)
            for r in range(HEADS_PER_GROUP):
                Lm = _decay_matrix(cs, csT, g * HEADS_PER_GROUP + r, causal)
                Gm = (CB * Lm).astype(BF16)
                ydiag = ydiag + _dot_nn(Gm, jnp.where(lane_head == r, xd_b, jnp.zeros_like(xd_b)))
            hs_ref[0, g] = hg
            h_scr[g] = hg * cd_full[:, sl] + S
            ypre = ydiag + yoff + xs[:, sl] * df_ref[:, sl]
            ypre_ref[:, sl] = ypre
            zg = z_ref[:, sl]
            yz = ypre * zg * _sigmoid(zg)
            rn = lax.rsqrt(jnp.mean(yz * yz, axis=-1, keepdims=True) + EPS)
            y_ref[:, sl] = (yz * rn * ng_ref[:, sl]).astype(BF16)

    return pl.pallas_call(
        body, grid=(nc,), in_specs=_ssd_in_specs(nc, False),
        out_specs=[pl.BlockSpec((CHUNK, D_INNER), lambda c: (c, 0)),
                   pl.BlockSpec((CHUNK, D_INNER), lambda c: (c, 0)),
                   pl.BlockSpec((1, N_GROUPS, D_STATE, GROUP_W), lambda c: (c, 0, 0, 0))],
        out_shape=[jax.ShapeDtypeStruct((T, D_INNER + ATTN_W), BF16), jax.ShapeDtypeStruct((T, D_INNER), F32),
                   jax.ShapeDtypeStruct((nc, N_GROUPS, D_STATE, GROUP_W), F32)],
        scratch_shapes=[pltpu.VMEM((N_GROUPS, D_STATE, GROUP_W), F32)],
        name=name, compiler_params=_cparams(("arbitrary",)),
    )(xbc, xbc, xbc, proj, proj, dtT, dtb, dtbT, alog, alogT, dfull, ng)


def _ssd_bwd(xbc, proj, dtT, dtb, dtbT, alog, alogT, dfull, ng, ypre, hs, dy, name):
    T = xbc.shape[0]
    nc = T // CHUNK
    Q = CHUNK

    def body(xs_ref, B_ref, C_ref, z_ref, dt_ref, dtT_ref, dtb_ref, dtbT_ref, al_ref, alT_ref, df_ref, ng_ref,
             ypre_ref, hs_ref, dy_ref,
             dz_ref, dxbc_ref, ddtb_ref, dal_ref, dD_ref, dng_ref, dh_scr):
        step = pl.program_id(0)

        @pl.when(step == 0)
        def _():
            dh_scr[...] = jnp.zeros_like(dh_scr)
            ddtb_ref[...] = jnp.zeros_like(ddtb_ref)
            dal_ref[...] = jnp.zeros_like(dal_ref)
            dD_ref[...] = jnp.zeros_like(dD_ref)
            dng_ref[...] = jnp.zeros_like(dng_ref)

        xdt, dt, A, cs, csT, causal, triu = _ssd_pre(dt_ref[:, :N_HEADS], dtT_ref[...], dtb_ref[...],
                                                    dtbT_ref[...], al_ref[...], alT_ref[...])
        ex = _expand_mat()
        dt_full = _hdot(dt, ex, "a")
        cs_full = _hdot(cs, ex, "a")
        cs_last = cs_full[Q - 1:Q, :]
        xs = xs_ref[...]
        xd = xs * dt_full
        e_full = jnp.exp(cs_full)
        dec_full = jnp.exp(cs_last - cs_full)
        cd_full = jnp.exp(cs_last)
        lane_head = lax.broadcasted_iota(jnp.int32, (1, GROUP_W), 1) // HEAD_DIM
        is_last = lax.broadcasted_iota(jnp.int32, (Q, 1), 0) == Q - 1
        dcs16 = jnp.zeros((Q, N_HEADS), F32)
        ddtx16 = jnp.zeros((Q, N_HEADS), F32)
        dD16 = jnp.zeros((8, N_HEADS), F32)
        lane16 = lax.broadcasted_iota(jnp.int32, (1, N_HEADS), 1)
        sub16 = lax.broadcasted_iota(jnp.int32, (N_HEADS, 1), 0)
        col_sums = jnp.zeros((N_HEADS, Q), F32)
        for g in range(N_GROUPS):
            sl = slice(g * GROUP_W, (g + 1) * GROUP_W)
            red = _reduce_mat(g)
            ypre_g = ypre_ref[:, sl]
            zg = z_ref[:, sl]
            sg = _sigmoid(zg)
            silu = zg * sg
            yz = ypre_g * silu
            rn = lax.rsqrt(jnp.mean(yz * yz, axis=-1, keepdims=True) + EPS)
            yh = yz * rn
            dy_g = dy_ref[:, sl]
            dng_ref[:, sl] += jnp.sum(dy_g * yh, axis=0, keepdims=True)
            dyh = dy_g * ng_ref[:, sl]
            dyz = rn * (dyh - yh * jnp.mean(dyh * yh, axis=-1, keepdims=True))
            dY = dyz * silu
            dz_ref[:, sl] = (dyz * ypre_g * sg * (1.0 + zg * (1.0 - sg))).astype(BF16)
            xs_g = xs[:, sl]
            xd_g = xd[:, sl]
            dec_g = dec_full[:, sl]
            cd_g = cd_full[:, sl]
            d_g = df_ref[:, sl]
            Bg = B_ref[:, g * D_STATE:(g + 1) * D_STATE].astype(BF16)
            Cg = C_ref[:, g * D_STATE:(g + 1) * D_STATE].astype(BF16)
            CB = _dot_nt(Cg, Bg)
            hg = hs_ref[0, g]
            hgb = hg.astype(BF16)
            yoff = _dot_nn(Cg, hgb) * e_full[:, sl]
            dhn = dh_scr[g]
            dhnb = dhn.astype(BF16)
            dYE = (dY * e_full[:, sl]).astype(BF16)
            dC = _dot_nt(dYE, hgb)
            dh_direct = _dot_tn(Cg, dYE)
            dXdd = _dot_nn(Bg, dhnb)
            dB = _dot_nt((xd_g * dec_g).astype(BF16), dhnb)
            dcd = jnp.sum(dhn * hg, axis=0, keepdims=True)
            dh_scr[g] = dh_direct + cd_g * dhn
            dYb = dY.astype(BF16)
            xd_b = xd_g.astype(BF16)
            dCB = jnp.zeros((Q, Q), F32)
            dXd = dXdd * dec_g
            for r in range(HEADS_PER_GROUP):
                h = g * HEADS_PER_GROUP + r
                Lm = _decay_matrix(cs, csT, h, causal)
                Gf = CB * Lm
                dYr = jnp.where(lane_head == r, dYb, jnp.zeros_like(dYb))
                dG = _dot_nt(dYr, xd_b)
                dCB = dCB + dG * Lm
                dXd = dXd + _dot_tn(Gf.astype(BF16), dYr)
                Mm = dG * Gf
                dcs16 = dcs16 + jnp.where(lane16 == h, jnp.sum(Mm, axis=1, keepdims=True), 0.0)
                col_sums = col_sums + jnp.where(sub16 == h, jnp.sum(Mm, axis=0, keepdims=True), 0.0)
            dCBb = dCB.astype(BF16)
            dC = dC + _dot_nn(dCBb, Bg)
            dB = dB + _dot_tn(dCBb, Cg)
            w_state = dXdd * dec_g * xd_g
            t_last = jnp.sum(w_state, axis=0, keepdims=True) + dcd * cd_g
            dcs_g = dY * yoff - w_state + jnp.where(is_last, t_last, 0.0)
            dcs16 = dcs16 + _hdot(dcs_g, red, "a")
            ddtx16 = ddtx16 + _hdot(dXd * xs_g, red, "a")
            dD16 = dD16 + _hdot(jnp.broadcast_to(jnp.sum(dY * xs_g, axis=0, keepdims=True), (8, GROUP_W)), red, "a")
            dxbc_ref[:, sl] = dXd * dt_full[:, sl] + dY * d_g
            dxbc_ref[:, D_INNER + g * D_STATE:D_INNER + (g + 1) * D_STATE] = dB
            dxbc_ref[:, D_INNER + 512 + g * D_STATE:D_INNER + 512 + (g + 1) * D_STATE] = dC
        eye = (lax.broadcasted_iota(jnp.int32, (N_HEADS, N_HEADS), 0)
               == lax.broadcasted_iota(jnp.int32, (N_HEADS, N_HEADS), 1)).astype(BF16)
        dcs16 = dcs16 - sum(_dot_tn(part, eye) for part in _split3(col_sums))
        da = _hdot(triu, dcs16, "b")
        ddt = da * A + ddtx16
        ddt_raw = ddt * _sigmoid(xdt)
        pr = lax.broadcasted_iota(jnp.int32, (N_HEADS, 128), 0)
        pc = lax.broadcasted_iota(jnp.int32, (N_HEADS, 128), 1)
        dz_ref[:, D_INNER:OFF_DT] = jnp.zeros((Q, OFF_DT - D_INNER), BF16)
        dz_ref[:, OFF_DT:OFF_DT + 128] = _hdot(ddt_raw, (pr == pc).astype(F32), "a").astype(BF16)
        dz_ref[:, OFF_DT + 128:] = jnp.zeros((Q, NP - OFF_DT - 128), BF16)
        ddtb_ref[...] += jnp.sum(ddt_raw, axis=0, keepdims=True)
        dal_ref[...] += jnp.sum(da * dt, axis=0, keepdims=True) * A
        dD_ref[...] += dD16[0:1, :]

    def rc(c):
        return nc - 1 - c

    in_specs = _ssd_in_specs(nc, True) + [
        pl.BlockSpec((CHUNK, D_INNER), lambda c: (rc(c), 0)),
        pl.BlockSpec((1, N_GROUPS, D_STATE, GROUP_W), lambda c: (rc(c), 0, 0, 0)),
        pl.BlockSpec((CHUNK, D_INNER), lambda c: (rc(c), 0)),
    ]
    small = pl.BlockSpec((1, N_HEADS), lambda c: (0, 0))
    return pl.pallas_call(
        body, grid=(nc,), in_specs=in_specs,
        out_specs=[pl.BlockSpec((CHUNK, NP), lambda c: (rc(c), 0)),
                   pl.BlockSpec((CHUNK, CONV_DIM), lambda c: (rc(c), 0)),
                   small, small, small,
                   pl.BlockSpec((1, D_INNER), lambda c: (0, 0))],
        out_shape=[jax.ShapeDtypeStruct((T, NP), BF16), jax.ShapeDtypeStruct((T, CONV_DIM), F32),
                   jax.ShapeDtypeStruct((1, N_HEADS), F32), jax.ShapeDtypeStruct((1, N_HEADS), F32),
                   jax.ShapeDtypeStruct((1, N_HEADS), F32), jax.ShapeDtypeStruct((1, D_INNER), F32)],
        scratch_shapes=[pltpu.VMEM((N_GROUPS, D_STATE, GROUP_W), F32)],
        name=name, compiler_params=_cparams(("arbitrary",)),
    )(xbc, xbc, xbc, proj, proj, dtT, dtb, dtbT, alog, alogT, dfull, ng, ypre, hs, dy)


N_PAIRS = ATTN_W // 128
PAIRS_PER_KV = N_PAIRS // 2
ATTN_SCALE = HEAD_DIM ** -0.5


def _kv_variants(kk):
    lo = lax.broadcasted_iota(jnp.int32, kk.shape, 1) < HEAD_DIM
    zero = jnp.zeros_like(kk)
    k00 = jnp.where(lo, kk, zero)
    k11 = jnp.where(lo, zero, kk)
    k01 = pltpu.roll(k00, HEAD_DIM, axis=1)
    k10 = pltpu.roll(k11, HEAD_DIM, axis=1)
    return [[k00.astype(BF16), k01.astype(BF16)], [k10.astype(BF16), k11.astype(BF16)]]


LOG2E = 1.4426950408889634


def _own_block():
    i = lax.broadcasted_iota(jnp.int32, (WINDOW, WINDOW), 0)
    j = lax.broadcasted_iota(jnp.int32, (WINDOW, WINDOW), 1)
    return j <= i


def _fold(own, a):
    return jnp.where(own, a[:, WINDOW:], a[:, :WINDOW])


def _attn_probs(qp, kvar, own, prev_bias, sk):
    s = _dot_nt(qp, kvar)
    sb = jnp.where(own, s[:, WINDOW:], s[:, :WINDOW] + prev_bias) * (ATTN_SCALE * LOG2E)
    sk2 = sk * LOG2E
    m = jnp.maximum(jnp.max(sb, axis=1, keepdims=True), sk2)
    pe = jnp.exp2(sb - m)
    es = jnp.exp2(sk2 - m)
    den = jnp.sum(pe, axis=1, keepdims=True) + es
    inv = 1.0 / den
    return pe * inv, es * inv


def _unfold(own, a):
    zero = jnp.zeros_like(a)
    return jnp.where(own, zero, a), jnp.where(own, a, zero)


def _sink(sinks, r):
    lane = lax.broadcasted_iota(jnp.int32, sinks.shape, 1)
    return jnp.sum(jnp.where(lane == r, sinks, 0.0), axis=1, keepdims=True)


def _kv_specs():
    return [pl.BlockSpec((WINDOW, KV_W), lambda n: (jnp.maximum(n - 1, 0), OFF_K // KV_W)),
            pl.BlockSpec((WINDOW, KV_W), lambda n: (n, OFF_K // KV_W)),
            pl.BlockSpec((WINDOW, KV_W), lambda n: (jnp.maximum(n - 1, 0), OFF_V // KV_W)),
            pl.BlockSpec((WINDOW, KV_W), lambda n: (n, OFF_V // KV_W))]


def _attn_fwd(proj, sinks, og, ycat, name):
    T = proj.shape[0]
    nb = T // WINDOW

    def body(q_ref, kp_ref, kc_ref, vp_ref, vc_ref, s_ref, og_ref, _, y_ref, o_ref):
        n = pl.program_id(0)
        kv = _kv_variants(jnp.concatenate([kp_ref[...], kc_ref[...]], axis=0))
        vv = _kv_variants(jnp.concatenate([vp_ref[...], vc_ref[...]], axis=0))
        own = _own_block()
        prev_bias = jnp.where(n > 0, 0.0, NEG)
        sinks_v = s_ref[...]
        ssq = jnp.zeros((WINDOW, 1), F32)
        for p in range(N_PAIRS):
            j = p // PAIRS_PER_KV
            qp = q_ref[:, p * 128:(p + 1) * 128].astype(BF16)
            o_pair = jnp.zeros((WINDOW, 128), F32)
            for par in range(2):
                pn, _ = _attn_probs(qp, kv[j][par], own, prev_bias, _sink(sinks_v, 2 * p + par))
                p_prev, p_own = _unfold(own, pn.astype(BF16))
                o_pair = o_pair + _dot_nn(p_prev, vv[j][par][:WINDOW]) + _dot_nn(p_own, vv[j][par][WINDOW:])
            o_ref[:, p * 128:(p + 1) * 128] = o_pair
            ssq = ssq + jnp.sum(o_pair * o_pair, axis=1, keepdims=True)
        rn = lax.rsqrt(ssq * (1.0 / ATTN_W) + EPS)
        y_ref[...] = (o_ref[...] * rn * og_ref[...]).astype(BF16)

    return pl.pallas_call(
        body, grid=(nb,),
        in_specs=[pl.BlockSpec((WINDOW, ATTN_W), lambda n: (n, OFF_Q // ATTN_W)), *_kv_specs(),
                  pl.BlockSpec((1, N_HEADS), lambda n: (0, 0)), pl.BlockSpec((1, ATTN_W), lambda n: (0, 0)), ANY],
        out_specs=[pl.BlockSpec((WINDOW, ATTN_W), lambda n: (n, 1)), pl.BlockSpec((WINDOW, ATTN_W), lambda n: (n, 0))],
        out_shape=[jax.ShapeDtypeStruct(ycat.shape, BF16), jax.ShapeDtypeStruct((T, ATTN_W), F32)],
        input_output_aliases={7: 0}, name=name, compiler_params=_cparams(("parallel",)),
    )(proj, proj, proj, proj, proj, sinks, og, ycat)


def _attn_bwd(proj, sinks, og, o, dy, dproj, name):
    T = proj.shape[0]
    nb = T // WINDOW

    def body(q_ref, kp_ref, kc_ref, vp_ref, vc_ref, s_ref, og_ref, o_ref, dy_ref, _,
             dq_ref, dk_ref, dv_ref, ds_ref, dog_ref, qt_scr, dot_scr, ds_scr, p_scr):
        n = pl.program_id(0)

        @pl.when(n == 0)
        def _():
            dk_ref[...] = jnp.zeros_like(dk_ref)
            dv_ref[...] = jnp.zeros_like(dv_ref)
            ds_ref[...] = jnp.zeros_like(ds_ref)
            dog_ref[...] = jnp.zeros_like(dog_ref)

        kv = _kv_variants(jnp.concatenate([kp_ref[...], kc_ref[...]], axis=0))
        vv = _kv_variants(jnp.concatenate([vp_ref[...], vc_ref[...]], axis=0))
        own = _own_block()
        prev_bias = jnp.where(n > 0, 0.0, NEG)
        sinks_v = s_ref[...]
        of = o_ref[...]
        rn = lax.rsqrt(jnp.mean(of * of, axis=-1, keepdims=True) + EPS)
        oh = of * rn
        dyf = dy_ref[...]
        dog_ref[...] += jnp.sum(dyf * oh, axis=0, keepdims=True)
        doh = dyf * og_ref[...]
        do = rn * (doh - oh * jnp.mean(doh * oh, axis=-1, keepdims=True))
        lane = lax.broadcasted_iota(jnp.int32, (1, 128), 1)
        lane16 = lax.broadcasted_iota(jnp.int32, (1, N_HEADS), 1)
        dsink = jnp.zeros((1, N_HEADS), F32)
        for p in range(N_PAIRS):
            j = p // PAIRS_PER_KV
            q_f = q_ref[:, p * 128:(p + 1) * 128]
            qp = q_f.astype(BF16)
            q_t = q_f.T.astype(BF16)
            do_p = do[:, p * 128:(p + 1) * 128]
            o_p = of[:, p * 128:(p + 1) * 128]
            do_b = do_p.astype(BF16)
            do_t = do_p.T.astype(BF16)
            prod = do_p * o_p
            dq_pair = jnp.zeros((WINDOW, 128), F32)
            for par in range(2):
                r = 2 * p + par
                half = (lane < HEAD_DIM) if par == 0 else (lane >= HEAD_DIM)
                pn, ps = _attn_probs(qp, kv[j][par], own, prev_bias, _sink(sinks_v, r))
                delta = jnp.sum(jnp.where(half, prod, 0.0), axis=1, keepdims=True)
                dP = _fold(own, _dot_nt(do_b, vv[j][par]))
                dS = pn * (dP - delta)
                dsink = dsink + jnp.where(lane16 == r, -jnp.sum(ps * delta, axis=0, keepdims=True), 0.0)
                dS_parts = _unfold(own, (dS * ATTN_SCALE).astype(BF16))
                p_parts = _unfold(own, pn.astype(BF16))
                at = ((p % PAIRS_PER_KV) * 2 + par) * WINDOW
                qt_scr[j, :, at:at + WINDOW] = q_t[par * HEAD_DIM:(par + 1) * HEAD_DIM]
                dot_scr[j, :, at:at + WINDOW] = do_t[par * HEAD_DIM:(par + 1) * HEAD_DIM]
                for blk in range(2):
                    dq_pair = dq_pair + _dot_nn(dS_parts[blk], kv[j][par][blk * WINDOW:(blk + 1) * WINDOW])
                    ds_scr[j, blk, at:at + WINDOW, :] = dS_parts[blk]
                    p_scr[j, blk, at:at + WINDOW, :] = p_parts[blk]
            dq_ref[:, p * 128:(p + 1) * 128] = dq_pair.astype(BF16)
        rows = [pl.multiple_of(jnp.maximum(n - 1, 0) * WINDOW, WINDOW), pl.multiple_of(n * WINDOW, WINDOW)]
        for lhs, rhs, ref in [(qt_scr, ds_scr, dk_ref), (dot_scr, p_scr, dv_ref)]:
            for blk in range(2):
                both_t = jnp.concatenate([_dot_nn(lhs[j], rhs[j, blk]) for j in range(2)], axis=0)
                ref[pl.ds(rows[blk], WINDOW), :] += both_t.T
        ds_ref[...] += dsink

    full_kv = pl.BlockSpec((T, KV_W), lambda n: (0, 0))
    blk = pl.BlockSpec((WINDOW, ATTN_W), lambda n: (n, 0))
    return pl.pallas_call(
        body, grid=(nb,),
        in_specs=[pl.BlockSpec((WINDOW, ATTN_W), lambda n: (n, OFF_Q // ATTN_W)), *_kv_specs(),
                  pl.BlockSpec((1, N_HEADS), lambda n: (0, 0)), pl.BlockSpec((1, ATTN_W), lambda n: (0, 0)),
                  blk, pl.BlockSpec((WINDOW, ATTN_W), lambda n: (n, 1)), ANY],
        out_specs=[pl.BlockSpec((WINDOW, ATTN_W), lambda n: (n, OFF_Q // ATTN_W)), full_kv, full_kv,
                   pl.BlockSpec((1, N_HEADS), lambda n: (0, 0)), pl.BlockSpec((1, ATTN_W), lambda n: (0, 0))],
        out_shape=[jax.ShapeDtypeStruct(dproj.shape, BF16), jax.ShapeDtypeStruct((T, KV_W), F32),
                   jax.ShapeDtypeStruct((T, KV_W), F32), jax.ShapeDtypeStruct((1, N_HEADS), F32),
                   jax.ShapeDtypeStruct((1, ATTN_W), F32)],
        scratch_shapes=[pltpu.VMEM((2, HEAD_DIM, 8 * WINDOW), BF16), pltpu.VMEM((2, HEAD_DIM, 8 * WINDOW), BF16),
                        pltpu.VMEM((2, 2, 8 * WINDOW, WINDOW), BF16), pltpu.VMEM((2, 2, 8 * WINDOW, WINDOW), BF16)],
        input_output_aliases={9: 0}, name=name, compiler_params=_cparams(("arbitrary",)),
    )(proj, proj, proj, proj, proj, sinks, og, o, dy, dproj)


ANY = pl.BlockSpec(memory_space=pl.ANY)


def _coords():
    return lax.axis_index("x"), lax.axis_index("y"), lax.axis_index("c")


def _all_gather(arrs, name):
    n = len(arrs)

    def body(*refs):
        ins, outs = refs[:n], refs[n:2 * n]
        send_sems, recv_sems, local_sems = refs[2 * n:]
        x, y, c = _coords()
        me = 4 * x + 2 * y + c
        sibling = (x, y, 1 - c)
        chips = [(1 - x, y), (x, 1 - y), (1 - x, 1 - y)]

        def copy(a, k, block, to, src=None):
            dst = outs[a].at[block]
            return pltpu.make_async_remote_copy(
                src_ref=dst if src is None else src, dst_ref=dst, send_sem=send_sems.at[a, k],
                recv_sem=recv_sems.at[a, k], device_id=to, device_id_type=MESH)

        mine = [pltpu.make_async_copy(ins[a], outs[a].at[me], local_sems.at[a]) for a in range(n)]
        for cp in mine:
            cp.start()
        first = []
        for a in range(n):
            first.append(copy(a, 0, me, sibling, src=ins[a]))
            for j, chip in enumerate(chips):
                first.append(copy(a, 1 + j, me, (*chip, c), src=ins[a]))
        for cp in first:
            cp.start()
        passed = []
        for j, (px, py) in enumerate(chips):
            blk = 4 * px + 2 * py + c
            for a in range(n):
                copy(a, 1 + j, blk, sibling).wait_recv()
                fwd = copy(a, 4 + j, blk, sibling)
                fwd.start()
                passed.append(fwd)
        for a in range(n):
            copy(a, 0, 4 * x + 2 * y + (1 - c), sibling).wait_recv()
            for j, (px, py) in enumerate(chips):
                copy(a, 4 + j, 4 * px + 2 * py + (1 - c), sibling).wait_recv()
        for cp in first + passed:
            cp.wait_send()
        for cp in mine:
            cp.wait()

    return pl.pallas_call(
        body, in_specs=[ANY] * n, out_specs=[ANY] * n,
        out_shape=[jax.ShapeDtypeStruct((N_DEV,) + a.shape, a.dtype) for a in arrs],
        scratch_shapes=[pltpu.SemaphoreType.DMA((n, 7)), pltpu.SemaphoreType.DMA((n, 7)),
                        pltpu.SemaphoreType.DMA((n,))],
        name=name,
    )(*arrs)


def _exchange_pair(arrs, name):
    n = len(arrs)

    def body(*refs):
        ins, outs = refs[:n], refs[n:2 * n]
        send_sems, recv_sems = refs[2 * n:]
        x, y, c = _coords()
        cps = []
        for a in range(n):
            for q in range(4):
                cps.append(pltpu.make_async_remote_copy(
                    src_ref=ins[a].at[2 * q + (1 - c)], dst_ref=outs[a].at[q], send_sem=send_sems.at[a, q],
                    recv_sem=recv_sems.at[a, q], device_id=(x, y, 1 - c), device_id_type=MESH))
        for cp in cps:
            cp.start()
        for cp in cps:
            cp.wait()

    return pl.pallas_call(
        body, in_specs=[ANY] * n, out_specs=[ANY] * n,
        out_shape=[jax.ShapeDtypeStruct((4,) + a.shape[1:], a.dtype) for a in arrs],
        scratch_shapes=[pltpu.SemaphoreType.DMA((n, 4)), pltpu.SemaphoreType.DMA((n, 4))],
        name=name,
    )(*arrs)


def _exchange_chips(arrs, name):
    n = len(arrs)

    def body(*refs):
        ins, outs = refs[:n], refs[n:2 * n]
        send_sems, recv_sems = refs[2 * n:]
        x, y, c = _coords()
        chips = [(1 - x, y), (x, 1 - y), (1 - x, 1 - y)]
        cps = []
        for a in range(n):
            for k, (tx, ty) in enumerate(chips):
                cps.append(pltpu.make_async_remote_copy(
                    src_ref=ins[a].at[2 * tx + ty], dst_ref=outs[a].at[k], send_sem=send_sems.at[a, k],
                    recv_sem=recv_sems.at[a, k], device_id=(tx, ty, c), device_id_type=MESH))
        for cp in cps:
            cp.start()
        for cp in cps:
            cp.wait()

    return pl.pallas_call(
        body, in_specs=[ANY] * n, out_specs=[ANY] * n,
        out_shape=[jax.ShapeDtypeStruct((3,) + a.shape[1:], a.dtype) for a in arrs],
        scratch_shapes=[pltpu.SemaphoreType.DMA((n, 3)), pltpu.SemaphoreType.DMA((n, 3))],
        name=name,
    )(*arrs)


HBM = pl.BlockSpec(memory_space=pltpu.HBM)
SEM = pl.BlockSpec(memory_space=pltpu.SEMAPHORE)
EFFECT = pltpu.SideEffectType.DATAFLOW_SIDE_EFFECTING


def _in_hbm(a):
    return pltpu.with_memory_space_constraint(a, pltpu.HBM)


def _remote_start(srcs, lands, plan, n_copies, name, after=None):
    ns, nb = len(srcs), len(srcs) + len(lands)
    n_after = 0 if after is None else 1

    def body(*refs):
        src_refs, land_refs = refs[:ns], refs[ns:nb]
        send_sems, recv_sems = refs[nb + n_after], refs[nb + n_after + 1]
        token = refs[-1]
        x, y, c = _coords()
        for i, (sv, dv, dev) in enumerate(plan(src_refs, land_refs, x, y, c)):
            pltpu.make_async_remote_copy(src_ref=sv, dst_ref=dv, send_sem=send_sems.at[i], recv_sem=recv_sems.at[i],
                                         device_id=dev, device_id_type=MESH).start()
        token[...] = jnp.zeros_like(token)

    bufs = list(srcs) + list(lands)
    outs = pl.pallas_call(
        body, name=name,
        out_shape=(pltpu.SemaphoreType.DMA((n_copies,)), pltpu.SemaphoreType.DMA((n_copies,)),
                   *[pltpu.HBM(b.shape, b.dtype) for b in bufs], jax.ShapeDtypeStruct((8, 128), F32)),
        in_specs=[HBM] * nb + [ANY] * n_after,
        out_specs=(SEM, SEM, *[HBM] * nb, pl.BlockSpec(memory_space=pltpu.VMEM)),
        input_output_aliases={i: 2 + i for i in range(nb)},
        compiler_params=pltpu.CompilerParams(has_side_effects=EFFECT),
    )(*[_in_hbm(b) for b in bufs], *([] if after is None else [after]))
    return outs[0], outs[1], list(outs[2:2 + ns]), list(outs[2 + ns:2 + nb]), outs[-1]


def _remote_wait(started, after, plan, name):
    send_sems, recv_sems, srcs, lands, _ = started
    ns, nb = len(srcs), len(srcs) + len(lands)

    def body(*refs):
        src_refs, land_refs = refs[:ns], refs[ns:nb]
        send_sems, recv_sems = refs[nb], refs[nb + 1]
        x, y, c = _coords()
        for i, (sv, dv, dev) in enumerate(plan(src_refs, land_refs, x, y, c)):
            cp = pltpu.make_async_remote_copy(src_ref=sv, dst_ref=dv, send_sem=send_sems.at[i],
                                              recv_sem=recv_sems.at[i], device_id=dev, device_id_type=MESH)
            cp.wait_send()
            cp.wait_recv()

    bufs = list(srcs) + list(lands)
    outs = pl.pallas_call(
        body, name=name, out_shape=tuple(pltpu.HBM(b.shape, b.dtype) for b in bufs),
        in_specs=[HBM] * nb + [SEM, SEM, ANY], out_specs=tuple([HBM] * nb),
        input_output_aliases={i: i for i in range(nb)},
        compiler_params=pltpu.CompilerParams(has_side_effects=EFFECT),
    )(*bufs, send_sems, recv_sems, after)
    return list(outs[:ns]), list(outs[ns:])


def _gather_plan(src_refs, land_refs, x, y, c):
    me = 4 * x + 2 * y + c
    plan = []
    for s, l in zip(src_refs, land_refs):
        for dev in [(x, y, 1 - c), (1 - x, y, c), (x, 1 - y, c), (1 - x, 1 - y, c)]:
            plan.append((s, l.at[me], dev))
    return plan


def _forward_plan(src_refs, land_refs, x, y, c):
    plan = []
    for l in land_refs:
        for px, py in [(1 - x, y), (x, 1 - y), (1 - x, 1 - y)]:
            blk = l.at[4 * px + 2 * py + c]
            plan.append((blk, blk, (x, y, 1 - c)))
    return plan


def _pair_plan(src_refs, land_refs, x, y, c):
    plan = []
    for s, l in zip(src_refs, land_refs):
        for q in range(4):
            plan.append((s.at[2 * q + (1 - c)], l.at[q], (x, y, 1 - c)))
    return plan


def _pair4_plan(src_refs, land_refs, x, y, c):
    plan = []
    for s, l in zip(src_refs, land_refs):
        for q in range(4):
            plan.append((s.at[q], l.at[q], (x, y, 1 - c)))
    return plan


def _chips_plan(src_refs, land_refs, x, y, c):
    plan = []
    for s, l in zip(src_refs, land_refs):
        for k, (tx, ty) in enumerate([(1 - x, y), (x, 1 - y), (1 - x, 1 - y)]):
            plan.append((s.at[2 * tx + ty], l.at[k], (tx, ty, c)))
    return plan


def _everyone_plan(src_refs, land_refs, x, y, c):
    me = 4 * x + 2 * y + c
    plan = []
    for s, l in zip(src_refs, land_refs):
        for fx, fy, fc in [(0, 0, 1), (1, 0, 0), (1, 0, 1), (0, 1, 0), (0, 1, 1), (1, 1, 0), (1, 1, 1)]:
            dev = ((1 - x) if fx else x, (1 - y) if fy else y, (1 - c) if fc else c)
            plan.append((s, l.at[me], dev))
    return plan


def _gather_finish(gathered, name):
    n = len(gathered)

    def body(*refs):
        outs = refs[n:2 * n]
        send_sems, recv_sems = refs[2 * n:]
        x, y, c = _coords()
        cps = []
        for a in range(n):
            for j, (px, py) in enumerate([(1 - x, y), (x, 1 - y), (1 - x, 1 - y)]):
                blk = outs[a].at[4 * px + 2 * py + c]
                got = outs[a].at[4 * px + 2 * py + (1 - c)]
                cps.append((pltpu.make_async_remote_copy(
                    src_ref=blk, dst_ref=blk, send_sem=send_sems.at[a, j], recv_sem=recv_sems.at[a, j],
                    device_id=(x, y, 1 - c), device_id_type=MESH), pltpu.make_async_remote_copy(
                    src_ref=got, dst_ref=got, send_sem=send_sems.at[a, j], recv_sem=recv_sems.at[a, j],
                    device_id=(x, y, 1 - c), device_id_type=MESH)))
        for cp, _ in cps:
            cp.start()
        for cp, arrival in cps:
            cp.wait_send()
            arrival.wait_recv()

    return pl.pallas_call(
        body, in_specs=[ANY] * n, out_specs=[ANY] * n,
        out_shape=[jax.ShapeDtypeStruct(g.shape, g.dtype) for g in gathered],
        input_output_aliases={a: a for a in range(n)},
        scratch_shapes=[pltpu.SemaphoreType.DMA((n, 3)), pltpu.SemaphoreType.DMA((n, 3))],
        name=name,
    )(*gathered)


def _pair_add(g8, r1, csel, tr, name):
    _, R, C = r1.shape
    g4 = g8.reshape(4, 2, R, C)

    def body(c_ref, g_ref, r_ref, o_ref):
        o_ref[...] = (g_ref[...].astype(F32) + r_ref[...].astype(F32)).astype(BF16)

    return pl.pallas_call(
        body,
        grid_spec=pltpu.PrefetchScalarGridSpec(
            num_scalar_prefetch=1, grid=(4, R // tr),
            in_specs=[pl.BlockSpec((None, None, tr, C), lambda q, i, cs: (q, cs[0], i, 0)),
                      pl.BlockSpec((None, tr, C), lambda q, i, cs: (q, i, 0))],
            out_specs=pl.BlockSpec((None, tr, C), lambda q, i, cs: (q, i, 0))),
        out_shape=jax.ShapeDtypeStruct((4, R, C), BF16), name=name,
        compiler_params=_cparams(("parallel", "parallel")),
    )(csel, g4, r1)


def _adamw_math(w, g, m, v):
    m = ADAM_B1 * m + (1.0 - ADAM_B1) * g
    v = ADAM_B2 * v + (1.0 - ADAM_B2) * (g * g)
    m_hat = m / (1.0 - ADAM_B1 ** ADAM_STEP)
    v_hat = v / (1.0 - ADAM_B2 ** ADAM_STEP)
    delta = -ADAM_LR * (m_hat / (jnp.sqrt(v_hat) + ADAM_EPS) + ADAM_WD * w)
    return delta, m, v


def _adamw_big(w, m, v, p4, r3, qsel, tile, name):
    R, C = w.shape
    tr, tc = tile

    def body(q_ref, w_ref, m_ref, v_ref, p_ref, r_ref, g_out, d_out, m_out, v_out):
        g = p_ref[...].astype(F32) + r_ref[0].astype(F32) + r_ref[1].astype(F32) + r_ref[2].astype(F32)
        d, mn, vn = _adamw_math(w_ref[...], g, m_ref[...], v_ref[...])
        g_out[...] = g
        d_out[...] = d
        m_out[...] = mn
        v_out[...] = vn

    blk = pl.BlockSpec((tr, tc), lambda i, j, qs: (i, j))
    return pl.pallas_call(
        body,
        grid_spec=pltpu.PrefetchScalarGridSpec(
            num_scalar_prefetch=1, grid=(R // tr, C // tc),
            in_specs=[blk, blk, blk, pl.BlockSpec((None, tr, tc), lambda i, j, qs: (qs[0], i, j)),
                      pl.BlockSpec((3, tr, tc), lambda i, j, qs: (0, i, j))],
            out_specs=[blk, blk, blk, blk]),
        out_shape=[jax.ShapeDtypeStruct((R, C), F32)] * 4, name=name,
        compiler_params=_cparams(("parallel", "parallel")),
    )(qsel, w, m, v, p4, r3)


def _sum_partials(p4, r3, qsel, tc, name):
    _, R, C = p4.shape

    def body(q_ref, p_ref, r_ref, o_ref):
        o_ref[...] = p_ref[...].astype(F32) + r_ref[0].astype(F32) + r_ref[1].astype(F32) + r_ref[2].astype(F32)

    return pl.pallas_call(
        body,
        grid_spec=pltpu.PrefetchScalarGridSpec(
            num_scalar_prefetch=1, grid=(C // tc,),
            in_specs=[pl.BlockSpec((None, R, tc), lambda j, qs: (qs[0], 0, j)),
                      pl.BlockSpec((3, R, tc), lambda j, qs: (0, 0, j))],
            out_specs=pl.BlockSpec((R, tc), lambda j, qs: (0, j))),
        out_shape=jax.ShapeDtypeStruct((R, C), F32), name=name, compiler_params=_cparams(("parallel",)),
    )(qsel, p4, r3)


def _adamw_tiled(w, g, m, v, tc, name):
    R, C = w.shape

    def body(w_ref, g_ref, m_ref, v_ref, d_out, m_out, v_out):
        d, mn, vn = _adamw_math(w_ref[...], g_ref[...], m_ref[...], v_ref[...])
        d_out[...] = d
        m_out[...] = mn
        v_out[...] = vn

    blk = pl.BlockSpec((R, tc), lambda j: (0, j))
    return pl.pallas_call(
        body, grid=(C // tc,), in_specs=[blk] * 4, out_specs=[blk] * 3,
        out_shape=[jax.ShapeDtypeStruct((R, C), F32)] * 3, name=name, compiler_params=_cparams(("parallel",)),
    )(w, g, m, v)


def _small_sum(parts, name):
    def body(p_ref, o_ref):
        acc = p_ref[0]
        for d in range(1, N_DEV):
            acc = acc + p_ref[d]
        o_ref[...] = acc

    return pl.pallas_call(
        body, out_shape=jax.ShapeDtypeStruct(parts.shape[1:], F32), name=name,
        compiler_params=_cparams(),
    )(parts)


def _adamw_small(w, g, m, v, name):
    def body(w_ref, g_ref, m_ref, v_ref, d_out, m_out, v_out):
        d, mn, vn = _adamw_math(w_ref[...], g_ref[...], m_ref[...], v_ref[...])
        d_out[...] = d
        m_out[...] = mn
        v_out[...] = vn

    return pl.pallas_call(
        body, out_shape=[jax.ShapeDtypeStruct(w.shape, F32)] * 3, name=name, compiler_params=_cparams(),
    )(w, g, m, v)


def _row(*pieces):
    r = jnp.concatenate([p.reshape(1, -1) for p in pieces], axis=1)
    return jnp.pad(r, ((0, 0), (0, D_MODEL - r.shape[1])))


def _pack_small(mix, convb, ssmg, attng, mlpg, fing, convw, dtb, alog, dsk, sinks, extra=None):
    last = [dtb, alog, dsk, sinks] + ([extra] if extra is not None else [])
    rows = [_row(mix), _row(convb), _row(ssmg, attng), _row(mlpg), _row(fing),
            jnp.pad(convw, ((0, 0), (0, D_MODEL - convw.shape[1]))), _row(*last)]
    packed = jnp.concatenate(rows, axis=0)
    return jnp.pad(packed, ((0, SMALL_ROWS - packed.shape[0]), (0, 0)))


def _unpack_small(p, conv_n):
    return dict(
        mix_norm_g=p[0:1, :], conv_b=p[1:2, :], ssm_norm_g=p[2:3, :D_INNER], attn_out_norm_g=p[2:3, D_INNER:],
        mlp_norm_g=p[3:4, :], final_norm_g=p[4, :], conv_w=p[5:9, :conv_n][None],
        dt_bias=p[9:10, 0:16], A_log=p[9:10, 16:32], D_skip=p[9:10, 32:48], attn_sinks=p[9:10, 48:64])


SMALL_NAMES = ["mix_norm_g", "conv_w", "conv_b", "dt_bias", "A_log", "D_skip", "ssm_norm_g", "attn_sinks",
               "attn_out_norm_g", "mlp_norm_g", "final_norm_g"]
WEIGHT_ORDER = ["mix_norm_g", "w_in", "conv_w", "conv_b", "dt_bias", "A_log", "D_skip", "ssm_norm_g", "attn_sinks",
                "attn_out_norm_g", "w_out", "mlp_norm_g", "w_up", "w_down", "final_norm_g"]


def _to_my_columns(w_nat):
    pad = jnp.zeros((w_nat.shape[0], NP - IN_PROJ), w_nat.dtype)
    return jnp.concatenate([w_nat[:, :NAT_DT], w_nat[:, NAT_DT + N_HEADS:], w_nat[:, NAT_DT:NAT_DT + N_HEADS], pad],
                           axis=1)


PER = IN_PROJ // N_DEV
SUPER_STEP = 544
SUPER = 576


def _natural_rows(g, lo, hi):
    segments = [(0, NAT_DT, 0), (NAT_DT, NAT_DT + N_HEADS, OFF_DT - NAT_DT), (NAT_DT + N_HEADS, IN_PROJ, -N_HEADS),
                (IN_PROJ, NP, 0)]
    pieces = [g[max(lo, a) + shift:min(hi, b) + shift] for a, b, shift in segments if max(lo, a) < min(hi, b)]
    return pieces[0] if len(pieces) == 1 else jnp.concatenate(pieces, axis=0)


def _w_in_from_super_slabs(sup):
    seam = SUPER - SUPER_STEP
    units = []
    for i in range(N_DEV):
        base = SUPER_STEP * i
        units.append((base, base + seam, sup[i, :seam] if i == 0 else sup[i - 1, SUPER_STEP:] + sup[i, :seam]))
        units.append((base + seam, base + SUPER_STEP, sup[i, seam:SUPER_STEP]))
    units.append((SUPER_STEP * N_DEV, SUPER_STEP * N_DEV + seam, sup[N_DEV - 1, SUPER_STEP:]))

    def natural(lo, hi):
        return [rows[max(lo, a) - a:min(hi, b) - a] for a, b, rows in units if max(lo, a) < min(hi, b)]

    pieces = natural(0, NAT_DT) + natural(NAT_DT + N_HEADS, IN_PROJ) + natural(NAT_DT, NAT_DT + N_HEADS)
    return jnp.concatenate(pieces + [jnp.zeros((NP - IN_PROJ, D_MODEL), sup.dtype)], axis=0)


def _to_natural_columns(w_my):
    return jnp.concatenate([w_my[:, :NAT_DT], w_my[:, OFF_DT:OFF_DT + N_HEADS], w_my[:, NAT_DT:OFF_DT]], axis=1)


SLAB = 1024


def _grad_w_up(h2, du, name, sel=None, add=None, after=None):
    T, D = h2.shape
    if sel is None:
        pick, n_slab, pre = (lambda j, *cs: j), N_DEV, None
    else:
        pre, other = sel
        pick, n_slab = (lambda j, cs: 2 * j + ((1 - cs[0]) if other else cs[0])), 4
    o_spec = pl.BlockSpec((None, SLAB, SLAB), lambda i, j, k, *cs: (j, i, 0))
    return _matmul(
        h2, du, mode="tn", grid=(D // SLAB, n_slab, 1),
        a_spec=pl.BlockSpec((T, SLAB), lambda i, j, k, *cs: (0, i)),
        b_spec=pl.BlockSpec((T, SLAB), lambda i, j, k, *cs: (0, pick(j, *cs))),
        out_shapes=[jax.ShapeDtypeStruct((n_slab, D, SLAB), BF16)], out_specs=[o_spec], tile=(SLAB, SLAB), name=name,
        extras=() if add is None else (add,), extra_specs=() if add is None else (o_spec,),
        epilogue=None if add is None else (lambda acc, r: (acc + r.astype(F32),)), after=after, prefetch=pre)[0]


def _grad_w_down(act, dx3b, name, sel=None, add=None, after=None):
    T, D = dx3b.shape
    if sel is None:
        pick, n_slab, pre = (lambda i, *cs: i), N_DEV, None
    else:
        pre, other = sel
        pick, n_slab = (lambda i, cs: 2 * i + ((1 - cs[0]) if other else cs[0])), 4
    o_spec = pl.BlockSpec((None, SLAB, SLAB), lambda i, j, k, *cs: (i, 0, j))
    return _matmul(
        act, dx3b, mode="tn", grid=(n_slab, D // SLAB, 1),
        a_spec=pl.BlockSpec((T, SLAB), lambda i, j, k, *cs: (0, pick(i, *cs))),
        b_spec=pl.BlockSpec((T, SLAB), lambda i, j, k, *cs: (0, j)),
        out_shapes=[jax.ShapeDtypeStruct((n_slab, SLAB, D), BF16)], out_specs=[o_spec], tile=(SLAB, SLAB), name=name,
        extras=() if add is None else (add,), extra_specs=() if add is None else (o_spec,),
        epilogue=None if add is None else (lambda acc, r: (acc + r.astype(F32),)), after=after, prefetch=pre)[0]


class _FixedWeights:
    def __init__(self, w_in_p, w_out_f, w_up_s, w_down_f, conv_w_f):
        self.w = (w_in_p, w_out_f, w_up_s, w_down_f, conv_w_f)
        self.grads = {}

    def mixer_weights(self, after):
        return self.w[0], self.w[4], None

    def prefetch(self, k, after):
        return None

    def out_weight(self, after):
        return self.w[1]

    def up_weight(self, after):
        return self.w[2]

    def down_weight(self, after):
        return self.w[3]

    def mlp_grads(self, h2, du, act, dx3b):
        self.grads.update(w_up=_grad_w_up(h2, du, "grad_w_up"),
                          w_down=_grad_w_down(act, dx3b, "grad_w_down").reshape(D_FF, D_MODEL))
        return None

    def grad_sent(self, tag, after):
        return None

    def out_grad(self, g_out):
        self.grads.update(w_out=g_out)
        return None

    def in_grad(self, g_in):
        self.grads.update(w_in=g_in)
        return None


def _local_step(x, tgt, p, hooks):
    T = x.shape[0]
    D = D_MODEL
    h1 = _rmsnorm_fwd(x, p["mix_norm_g"], "norm_mix")
    w_in_t, conv_w_f, token = hooks.mixer_weights(h1)
    (proj,) = _mm_simple(h1, w_in_t, mode="nt", M=T, N=NP, K=D, tm=min(T, 1024), tn=1536, tk=D, out_dtype=F32,
                         name="in_proj", after=token)
    xbc = _conv_fwd(proj, conv_w_f, p["conv_b"], "conv_fwd")
    dtT = proj[:, OFF_DT:OFF_DT + N_HEADS].T
    dtbT = p["dt_bias"].T
    alogT = p["A_log"].T
    dfull = jnp.repeat(p["D_skip"], HEAD_DIM, axis=1)
    token = hooks.prefetch("out", xbc)
    ssm_g = p["ssm_norm_g"] if token is None else p["ssm_norm_g"] + token[0:1, 0:1]
    ycat, ypre, hs = _ssd_fwd(xbc, proj, dtT, p["dt_bias"], dtbT, p["A_log"], alogT, dfull, ssm_g, "ssd_fwd")
    ycat, o_att = _attn_fwd(proj, p["attn_sinks"], p["attn_out_norm_g"], ycat, "attn_fwd")
    token = hooks.prefetch("up", ycat)
    w_out_f = hooks.out_weight(ycat if token is None else token)
    tm = min(T, 1024)
    (x2,) = _mm_simple(ycat, w_out_f, mode="nn", M=T, N=D, K=D, tm=tm, tn=1024, tk=D, out_dtype=F32, name="out_proj",
                       extras=(x,), epilogue=lambda acc, res: (acc + res,))
    h2 = _rmsnorm_fwd(x2, p["mlp_norm_g"], "norm_mlp")
    w_up_s = hooks.up_weight(h2)
    grid = (T // tm, N_DEV, 1)
    u, act = _matmul(
        h2, w_up_s, mode="nn", grid=grid,
        a_spec=pl.BlockSpec((tm, D), lambda i, j, k: (i, 0)),
        b_spec=pl.BlockSpec((None, D, 1024), lambda i, j, k: (j, 0, 0)),
        out_shapes=[jax.ShapeDtypeStruct((T, D_FF), F32), jax.ShapeDtypeStruct((T, D_FF), BF16)],
        out_specs=[pl.BlockSpec((tm, 1024), lambda i, j, k: (i, j))] * 2, tile=(tm, 1024), name="mlp_up",
        epilogue=lambda acc: (acc, jnp.square(jnp.maximum(acc, 0.0))))
    w_down_f = hooks.down_weight(act)
    (x3,) = _mm_simple(act, w_down_f, mode="nn", M=T, N=D, K=D_FF, tm=tm, tn=1024, tk=2048, out_dtype=F32,
                       name="mlp_down", extras=(x2,), epilogue=lambda acc, res: (acc + res,))
    loss_part, d_fin, dx3, dx3b = _final_loss(x3, tgt, p["final_norm_g"].reshape(1, D), "loss_head")
    (du,) = _mm_simple(dx3b, w_down_f, mode="nt", M=T, N=D_FF, K=D, tm=tm, tn=1024, tk=D, out_dtype=BF16,
                       name="mlp_down_bwd", extras=(u,),
                       epilogue=lambda acc, uu: (acc * (2.0 * jnp.maximum(uu, 0.0)),))
    token = hooks.mlp_grads(h2, du, act, dx3b)
    (dh2,) = _matmul(
        du, w_up_s, mode="nt", grid=(T // tm, D // 1024, N_DEV // 2),
        a_spec=pl.BlockSpec((tm, 2048), lambda i, j, k: (i, k)),
        b_spec=pl.BlockSpec((2, 1024, 1024), lambda i, j, k: (k, j, 0)),
        out_shapes=[jax.ShapeDtypeStruct((T, D), F32)],
        out_specs=[pl.BlockSpec((tm, 1024), lambda i, j, k: (i, j))], tile=(tm, 1024), name="mlp_up_bwd",
        after=token, dot_fn=lambda a, b: _dot_nt(a[:, :1024], b[0]) + _dot_nt(a[:, 1024:], b[1]))
    dx2, dx2b, d_mlp = _rmsnorm_bwd(dh2, x2, p["mlp_norm_g"], dx3, "norm_mlp_bwd")
    (g_out,) = _mm_simple(ycat, dx2b, mode="tn", M=D, N=D, K=T, tm=1024, tn=1024, tk=T, out_dtype=BF16,
                          name="grad_w_out")
    token = hooks.out_grad(g_out)
    (dy,) = _mm_simple(dx2b, w_out_f, mode="nt", M=T, N=D, K=D, tm=tm, tn=1024, tk=D, out_dtype=F32,
                       name="out_proj_bwd", after=token)
    token = hooks.grad_sent("out", dy)
    ssm_g = p["ssm_norm_g"] if token is None else p["ssm_norm_g"] + token[0:1, 0:1]
    dproj, dxbc_act, d_dtb, d_alog, d_dskip, d_ssmg = _ssd_bwd(
        xbc, proj, dtT, p["dt_bias"], dtbT, p["A_log"], alogT, dfull, ssm_g, ypre, hs, dy, "ssd_bwd")
    dproj, d_convw, d_convb = _conv_bwd(proj, dxbc_act, conv_w_f, p["conv_b"], dproj, "conv_bwd")
    dproj, dk, dv, d_sinks, d_attng = _attn_bwd(proj, p["attn_sinks"], p["attn_out_norm_g"], o_att, dy, dproj,
                                                "attn_bwd")
    dproj = lax.dynamic_update_slice(dproj, jnp.concatenate([dk, dv], axis=1).astype(BF16), (0, OFF_K))
    (g_in,) = _mm_simple(dproj, h1, mode="tn", M=NP, N=D, K=T, tm=1536, tn=1024, tk=T, out_dtype=BF16,
                         name="grad_w_in")
    token = hooks.in_grad(g_in)
    (dh1,) = _mm_simple(dproj, w_in_t, mode="nn", M=T, N=D, K=NP, tm=tm, tn=1024, tk=2304, out_dtype=F32,
                        name="in_proj_bwd", after=token)
    token = hooks.grad_sent("in", dh1)
    mix_g = p["mix_norm_g"] if token is None else p["mix_norm_g"] + token[0:1, 0:1]
    dx, _, d_mix = _rmsnorm_bwd(dh1, x, mix_g, dx2, "norm_mix_bwd")
    small = _pack_small(d_mix, d_convb, d_ssmg, d_attng, d_mlp, d_fin, d_convw, d_dtb, d_alog, d_dskip, d_sinks,
                        extra=loss_part[:, 0:1])
    return dx, small


def _rows_rotated(v, shift, name):
    R, C = v.shape
    tc = 512

    def body(s_ref, v_ref, o_ref):
        o_ref[...] = pltpu.roll(v_ref[...], s_ref[0], axis=0).astype(BF16)

    return pl.pallas_call(
        body,
        grid_spec=pltpu.PrefetchScalarGridSpec(
            num_scalar_prefetch=1, grid=(C // tc,), in_specs=[pl.BlockSpec((R, tc), lambda j, s: (0, j))],
            out_specs=pl.BlockSpec((R, tc), lambda j, s: (0, j))),
        out_shape=jax.ShapeDtypeStruct((R, C), BF16), name=name, compiler_params=_cparams(("parallel",)),
    )(shift, v)


def _landing(own, me):
    zone = lax.empty((N_DEV,) + own.shape, own.dtype)
    return lax.dynamic_update_slice(zone, own[None], (me,) + (0,) * own.ndim)


def _sequencer_gather(owns, split, me, collective_id, name):
    n = len(owns)
    zone_refs = [jax.new_ref(_landing(o, me), memory_space=pltpu.MemorySpace.HBM) for o in owns]
    own_refs = [jax.new_ref(o, memory_space=pltpu.MemorySpace.HBM) for o in owns]
    N_COPIES = 9

    @pl.kernel(mesh=plsc.ScalarSubcoreMesh(axis_name="sequencer", num_cores=1), name=name,
               scratch_types=(pltpu.SemaphoreType.DMA((n, N_COPIES)), pltpu.SemaphoreType.DMA((n, N_COPIES))),
               compiler_params=pltpu.CompilerParams(collective_id=collective_id))
    def launch(send_sems, recv_sems):
        x, y, c = _coords()
        sibling, xn, yn, diag = (x, y, 1 - c), (1 - x, y, c), (x, 1 - y, c), (1 - x, 1 - y, c)
        barrier = pltpu.get_barrier_semaphore()
        for peer in [sibling, xn, yn, diag]:
            pl.semaphore_signal(barrier, inc=1, device_id=peer, device_id_type=MESH)
        pl.semaphore_wait(barrier, 4)

        def block(a, dev, half=None):
            ref = zone_refs[a].at[4 * dev[0] + 2 * dev[1] + dev[2]]
            if half is None:
                return ref
            rows = owns[a].shape[0] // 2
            return ref.at[pl.ds(half * rows, rows)]

        def copy(a, k, src, dst, to):
            return pltpu.make_async_remote_copy(src_ref=src, dst_ref=dst, send_sem=send_sems.at[a, k],
                                                recv_sem=recv_sems.at[a, k], device_id=to, device_id_type=MESH)

        me_dev = (x, y, c)
        sent = []
        first = {}
        for a in range(n):
            for k, peer in enumerate([sibling, xn, yn] + ([] if split[a] else [diag])):
                first[a, k] = copy(a, k, own_refs[a], block(a, me_dev), peer)
                first[a, k].start()
                sent.append(first[a, k])
        from_sibling = []
        for a in range(n):
            first[a, 1].wait_recv()
            sent.append(copy(a, 4, block(a, xn), block(a, xn), sibling))
            if split[a]:
                sent.append(copy(a, 6, block(a, xn, 0), block(a, xn, 0), yn))
            first[a, 2].wait_recv()
            sent.append(copy(a, 5, block(a, yn), block(a, yn), sibling))
            if split[a]:
                sent.append(copy(a, 7, block(a, yn, 1), block(a, yn, 1), xn))
            for cp in sent[-(4 if split[a] else 2):]:
                cp.start()
        for a in range(n):
            if split[a]:
                copy(a, 6, block(a, diag, 0), block(a, diag, 0), yn).wait_recv()
                sent.append(copy(a, 8, block(a, diag, 0), block(a, diag, 0), sibling))
                sent[-1].start()
                copy(a, 7, block(a, diag, 1), block(a, diag, 1), xn).wait_recv()
                sent.append(copy(a, 3, block(a, diag, 1), block(a, diag, 1), sibling))
                sent[-1].start()
            else:
                first[a, 3].wait_recv()
                sent.append(copy(a, 8, block(a, diag), block(a, diag), sibling))
                sent[-1].start()
        for a in range(n):
            first[a, 0].wait_recv()
            copy(a, 4, block(a, xn), block(a, xn), sibling).wait_recv()
            copy(a, 5, block(a, yn), block(a, yn), sibling).wait_recv()
            if split[a]:
                copy(a, 8, block(a, diag, 0), block(a, diag, 0), sibling).wait_recv()
                copy(a, 3, block(a, diag, 1), block(a, diag, 1), sibling).wait_recv()
            else:
                copy(a, 8, block(a, diag), block(a, diag), sibling).wait_recv()
        for cp in sent:
            cp.wait_send()

    launch()
    return zone_refs


def _gather_end(started, after, plan, name):
    _, lands = _remote_wait(started, after, plan, name + "_wait")
    return _gather_finish(lands, name + "_finish")


class _ShardedWeights:
    def __init__(self, w_in, w_out, conv_w, w_up, w_down, me, csel):
        self.me, self.csel = me, csel
        padded = jnp.pad(jnp.transpose(w_in), ((0, SUPER - PER), (0, 0)))
        own_rows = _rows_rotated(padded, jnp.reshape(2 * me, (1,)).astype(jnp.int32), "w_in_super_slab")
        self.in_ref, self.conv_ref = _sequencer_gather([own_rows, conv_w], [True, False], me, 7,
                                                       "gather_w_in_sequencer")
        (self.out_ref,) = _sequencer_gather([w_out.astype(BF16)], [True], me, 8, "gather_w_out_sequencer")
        (self.up_ref,) = _sequencer_gather([w_up.astype(BF16)], [True], me, 9, "gather_w_up_sequencer")
        (self.down_ref,) = _sequencer_gather([w_down.astype(BF16)], [True], me, 10, "gather_w_down_sequencer")
        self.reduces = {}
        self.pairs = {}

    def mixer_weights(self, after):
        g_conv = self.conv_ref[...]
        conv_w_f = jnp.concatenate([g_conv[i] for i in range(N_DEV)], axis=1)
        return _w_in_from_super_slabs(self.in_ref[...]), conv_w_f, None

    def prefetch(self, k, after):
        return None

    def out_weight(self, after):
        return self.out_ref[...].reshape(D_MODEL, D_MODEL)

    def up_weight(self, after):
        return self.up_ref[...]

    def down_weight(self, after):
        return self.down_ref[...].reshape(D_FF, D_MODEL)

    def _chips_start(self, slabs, from_sibling, rows, tag):
        sums = [_pair_add(s, r, self.csel, tr, f"pair_add_{tag}_{i}")
                for i, (s, r, tr) in enumerate(zip(slabs, from_sibling, rows))]
        lands = [lax.empty((3,) + s.shape[1:], s.dtype) for s in sums]
        self.reduces[tag] = _remote_start(sums, lands, _chips_plan, 3 * len(sums), f"reduce_start_{tag}")
        return self.reduces[tag][4]

    def mlp_grads(self, h2, du, act, dx3b):
        def send(part, tag, after):
            st = _remote_start([part], [lax.empty(part.shape, part.dtype)], _pair4_plan, 4,
                               f"reduce_pair_start_{tag}", after=after)
            return st

        def received(st, after, tag):
            return _remote_wait(st, after, _pair4_plan, f"reduce_pair_wait_{tag}")[1][0]

        def to_chips(sums, tag):
            self.reduces[tag] = _remote_start([sums], [lax.empty((3,) + sums.shape[1:], sums.dtype)], _chips_plan, 3,
                                              f"reduce_start_{tag}")
            return self.reduces[tag][4]

        up_send = _grad_w_up(h2, du, "grad_w_up_send", sel=(self.csel, True))
        st_up = send(up_send, "up", None)
        down_send = _grad_w_down(act, dx3b, "grad_w_down_send", sel=(self.csel, True), after=st_up[4])
        st_down = send(down_send, "down", None)
        up_sum = _grad_w_up(h2, du, "grad_w_up_keep", sel=(self.csel, False), add=received(st_up, down_send, "up"),
                            after=st_down[4])
        token = to_chips(up_sum, "up")
        down_sum = _grad_w_down(act, dx3b, "grad_w_down_keep", sel=(self.csel, False),
                                add=received(st_down, up_sum, "down"), after=token)
        return to_chips(down_sum, "down")

    def _pair_start(self, slabs, tag):
        land = lax.empty((4,) + slabs.shape[1:], slabs.dtype)
        self.pairs[tag] = _remote_start([slabs], [land], _pair_plan, 4, f"reduce_pair_start_{tag}")
        return self.pairs[tag][4]

    def grad_sent(self, tag, after):
        slabs, from_sibling = _remote_wait(self.pairs[tag], after, _pair_plan, f"reduce_pair_wait_{tag}")
        return self._chips_start(slabs, from_sibling, [slabs[0].shape[1]], tag)

    def out_grad(self, g_out):
        return self._pair_start(g_out.reshape(N_DEV, D_MODEL // N_DEV, D_MODEL), "out")

    def in_grad(self, g_in):
        return self._pair_start(
            jnp.stack([_natural_rows(g_in, SUPER_STEP * j, SUPER_STEP * j + SUPER) for j in range(N_DEV)]), "in")

    def small_start(self, small):
        self.st_small = _remote_start([small], [_landing(small, self.me)], _everyone_plan, N_DEV - 1, "gather_start_small")

    def small_end(self, after):
        return _remote_wait(self.st_small, after, _everyone_plan, "gather_small_wait")[1][0]

    def reduce_end(self, tag, after):
        return _remote_wait(self.reduces[tag], after, _chips_plan, f"reduce_wait_{tag}")


def kernel(x, mix_norm_g, w_in, conv_w, conv_b, dt_bias, A_log, D_skip, ssm_norm_g, attn_sinks, attn_out_norm_g, w_out, mlp_norm_g, w_up, w_down, final_norm_g, loss_target, m_mix_norm_g, m_w_in, m_conv_w, m_conv_b, m_dt_bias, m_A_log, m_D_skip, m_ssm_norm_g, m_attn_sinks, m_attn_out_norm_g, m_w_out, m_mlp_norm_g, m_w_up, m_w_down, m_final_norm_g, v_mix_norm_g, v_w_in, v_conv_w, v_conv_b, v_dt_bias, v_A_log, v_D_skip, v_ssm_norm_g, v_attn_sinks, v_attn_out_norm_g, v_w_out, v_mlp_norm_g, v_w_up, v_w_down, v_final_norm_g):
    xi, yi, ci = _coords()
    me = 4 * xi + 2 * yi + ci
    csel = jnp.reshape(ci, (1,)).astype(jnp.int32)
    qsel = jnp.reshape(2 * xi + yi, (1,)).astype(jnp.int32)
    w = dict(mix_norm_g=mix_norm_g, conv_b=conv_b, dt_bias=dt_bias, A_log=A_log, D_skip=D_skip,
             ssm_norm_g=ssm_norm_g, attn_sinks=attn_sinks, attn_out_norm_g=attn_out_norm_g, mlp_norm_g=mlp_norm_g,
             final_norm_g=final_norm_g)
    hooks = _ShardedWeights(w_in[0], w_out[0], conv_w[0], w_up[0], w_down[0], me, csel)
    p = dict(w)
    dx, small = _local_step(x[0], loss_target[0], p, hooks)
    hooks.small_start(small)
    big = {}
    after = dx
    for name, wt, mt, vt, tile in [
            ("up", w_up, m_w_up, v_w_up, (512, SLAB)), ("down", w_down, m_w_down, v_w_down, (256, D_MODEL)),
            ("out", w_out, m_w_out, v_w_out, (256, D_MODEL))]:
        (chip_sums,), (from_chips,) = hooks.reduce_end(name, after)
        res = _adamw_big(wt[0], mt[0], vt[0], chip_sums, from_chips, qsel, tile, f"adamw_w_{name}")
        big["w_" + name] = tuple(r[None] for r in res)
        after = res[0]
    (chip_sums,), (from_chips,) = hooks.reduce_end("in", after)
    g_super = _sum_partials(chip_sums, from_chips, qsel, 512, "grad_w_in_sum")
    g_in = lax.dynamic_slice(g_super, (2 * me, 0), (PER, D_MODEL))
    res = _adamw_tiled(jnp.transpose(w_in[0]), g_in, jnp.transpose(m_w_in[0]), jnp.transpose(v_w_in[0]), 512,
                       "adamw_w_in")
    big["w_in"] = tuple(jnp.transpose(r)[None] for r in (g_in, *res))
    after = res[0]
    gsum = _small_sum(hooks.small_end(after), "small_sum")
    loss = gsum[9, 64]
    gs = _unpack_small(gsum, CONV_DIM)
    cw = CONV_DIM // N_DEV
    g_conv_shard = lax.dynamic_slice(gsum[5:9, :], (0, me * cw), (CONV_K, cw))

    def pack(s):
        return _pack_small(s["mix_norm_g"], s["conv_b"], s["ssm_norm_g"], s["attn_out_norm_g"], s["mlp_norm_g"],
                           s["final_norm_g"], s["conv_w"][0], s["dt_bias"], s["A_log"], s["D_skip"], s["attn_sinks"])

    wp = pack(dict(w, conv_w=conv_w))
    mp = pack(dict(mix_norm_g=m_mix_norm_g, conv_b=m_conv_b, ssm_norm_g=m_ssm_norm_g,
                   attn_out_norm_g=m_attn_out_norm_g, mlp_norm_g=m_mlp_norm_g, final_norm_g=m_final_norm_g,
                   conv_w=m_conv_w, dt_bias=m_dt_bias, A_log=m_A_log, D_skip=m_D_skip, attn_sinks=m_attn_sinks))
    vp = pack(dict(mix_norm_g=v_mix_norm_g, conv_b=v_conv_b, ssm_norm_g=v_ssm_norm_g,
                   attn_out_norm_g=v_attn_out_norm_g, mlp_norm_g=v_mlp_norm_g, final_norm_g=v_final_norm_g,
                   conv_w=v_conv_w, dt_bias=v_dt_bias, A_log=v_A_log, D_skip=v_D_skip, attn_sinks=v_attn_sinks))
    gp = jnp.concatenate([gsum[0:5], jnp.pad(g_conv_shard, ((0, 0), (0, D_MODEL - cw))), gsum[9:10],
                          jnp.zeros((SMALL_ROWS - 10, D_MODEL), F32)], axis=0)
    dp, mnp, vnp = _adamw_small(wp, gp, mp, vp, "adamw_small")
    grads = dict(gs, conv_w=g_conv_shard[None])
    deltas = _unpack_small(dp, cw)
    new_m = _unpack_small(mnp, cw)
    new_v = _unpack_small(vnp, cw)
    for k, name in enumerate(["w_in", "w_out", "w_up", "w_down"]):
        grads[name], deltas[name], new_m[name], new_v[name] = big[name]
    return (loss, dx[None], *[grads[n] for n in WEIGHT_ORDER], *[deltas[n] for n in WEIGHT_ORDER],
            *[new_m[n] for n in WEIGHT_ORDER], *[new_v[n] for n in WEIGHT_ORDER])
```

```python
import jax
import jax.numpy as jnp
from jax import lax
from jax.experimental import pallas as pl
from jax.experimental.pallas import tpu as pltpu
from jax.experimental.pallas import tpu_sc as plsc

F32 = jnp.float32
BF16 = jnp.bfloat16
MESH = pl.DeviceIdType.MESH

EPS = 1e-5
D_MODEL = 2048
D_INNER = 1024
N_HEADS = 16
HEAD_DIM = 64
N_GROUPS = 4
D_STATE = 128
CHUNK = 128
CONV_K = 4
CONV_DIM = 2048
ATTN_W = 1024
KV_W = 128
WINDOW = 128
D_FF = 8192
IN_PROJ = 4368
N_DEV = 8
NP = 4608
OFF_Z, OFF_X, OFF_B, OFF_C, OFF_Q, OFF_K, OFF_V, OFF_DT = 0, 1024, 2048, 2560, 3072, 4096, 4224, 4352
NAT_DT = 3072

ADAM_LR = 0.001
ADAM_B1 = 0.9
ADAM_B2 = 0.999
ADAM_EPS = 1e-08
ADAM_WD = 0.01
ADAM_STEP = 10

VMEM_LIMIT = 52 * 1024 * 1024
SMALL_ROWS = 16
NEG = -1e30


def _cparams(sem=None):
    return pltpu.CompilerParams(dimension_semantics=sem, vmem_limit_bytes=VMEM_LIMIT)


def _split3(v):
    hi = v.astype(BF16)
    rest = v - hi.astype(F32)
    mid = rest.astype(BF16)
    return hi, mid, (rest - mid.astype(F32)).astype(BF16)


def _hdot(a, b, data):
    if data == "a":
        sel = b.astype(BF16)
        return sum(_dot_nn(part, sel) for part in _split3(a))
    sel = a.astype(BF16)
    return sum(_dot_nn(sel, part) for part in _split3(b))


def _dot_nn(a, b):
    return lax.dot_general(a, b, (((1,), (0,)), ((), ())), preferred_element_type=F32)


def _dot_nt(a, b):
    return lax.dot_general(a, b, (((1,), (1,)), ((), ())), preferred_element_type=F32)


def _dot_tn(a, b):
    return lax.dot_general(a, b, (((0,), (0,)), ((), ())), preferred_element_type=F32)


def _softplus(v):
    return jnp.maximum(v, 0.0) + jnp.log1p(jnp.exp(-jnp.abs(v)))


def _sigmoid(v):
    return 1.0 / (1.0 + jnp.exp(-v))


def _matmul(a, b, *, mode, grid, a_spec, b_spec, out_shapes, out_specs, tile, name,
            extras=(), extra_specs=(), epilogue=None, after=None, dot_fn=None, prefetch=None):
    nk = grid[2]
    n_ex = len(extras)
    n_out = len(out_shapes)
    bs, b_specs = (b, b_spec) if isinstance(b, tuple) else ((b,), (b_spec,))
    n_in = 1 + len(bs)
    dot = dot_fn if dot_fn is not None else {"nn": _dot_nn, "nt": _dot_nt, "tn": _dot_tn}[mode]

    def finish(acc, ex_refs, out_refs):
        res = (acc,) if epilogue is None else epilogue(acc, *[e[...] for e in ex_refs])
        for o, r in zip(out_refs, res):
            o[...] = r.astype(o.dtype)

    def body(*refs):
        ex_refs = refs[n_in:n_in + n_ex]
        out_refs = refs[n_in + n_ex:n_in + n_ex + n_out]
        part = dot(*[r[...].astype(BF16) for r in refs[:n_in]])
        if nk == 1:
            finish(part, ex_refs, out_refs)
        else:
            acc_ref = refs[-1]
            k = pl.program_id(2)

            @pl.when(k == 0)
            def _():
                acc_ref[...] = part

            @pl.when(k > 0)
            def _():
                acc_ref[...] += part

            @pl.when(k == nk - 1)
            def _():
                finish(acc_ref[...], ex_refs, out_refs)

    scratch = [] if nk == 1 else [pltpu.VMEM(tile, F32)]
    n_pre = 0 if prefetch is None else 1
    tok_specs = [] if after is None else [pl.BlockSpec((8, 128), lambda *_: (0, 0))]
    tok_args = [] if after is None else [after]

    def body_with_token(*refs):
        refs = refs[n_pre:]
        body(*refs[:n_in + n_ex], *refs[n_in + n_ex + len(tok_args):])

    in_specs = [a_spec, *b_specs, *extra_specs, *tok_specs]
    params = _cparams(("parallel", "parallel", "arbitrary"))
    if prefetch is None:
        return pl.pallas_call(
            body_with_token, grid=grid, in_specs=in_specs, out_specs=list(out_specs), out_shape=list(out_shapes),
            scratch_shapes=scratch, name=name, compiler_params=params)(a, *bs, *extras, *tok_args)
    return pl.pallas_call(
        body_with_token,
        grid_spec=pltpu.PrefetchScalarGridSpec(num_scalar_prefetch=1, grid=grid, in_specs=in_specs,
                                               out_specs=list(out_specs), scratch_shapes=scratch),
        out_shape=list(out_shapes), name=name, compiler_params=params)(prefetch, a, *bs, *extras, *tok_args)


def _mm_simple(a, b, *, mode, M, N, K, tm, tn, tk, out_dtype, name, extras=(), epilogue=None, n_out=1,
               out_dtypes=None, after=None):
    grid = (M // tm, N // tn, K // tk)
    if mode == "nn":
        a_spec = pl.BlockSpec((tm, tk), lambda i, j, k: (i, k))
        b_spec = pl.BlockSpec((tk, tn), lambda i, j, k: (k, j))
    elif mode == "nt":
        a_spec = pl.BlockSpec((tm, tk), lambda i, j, k: (i, k))
        b_spec = pl.BlockSpec((tn, tk), lambda i, j, k: (j, k))
    else:
        a_spec = pl.BlockSpec((tk, tm), lambda i, j, k: (k, i))
        b_spec = pl.BlockSpec((tk, tn), lambda i, j, k: (k, j))
    o_spec = pl.BlockSpec((tm, tn), lambda i, j, k: (i, j))
    dts = out_dtypes if out_dtypes is not None else [out_dtype] * n_out
    return _matmul(a, b, mode=mode, grid=grid, a_spec=a_spec, b_spec=b_spec,
                   out_shapes=[jax.ShapeDtypeStruct((M, N), d) for d in dts],
                   out_specs=[o_spec] * len(dts), tile=(tm, tn), name=name,
                   extras=extras, extra_specs=[o_spec] * len(extras), epilogue=epilogue, after=after)


ROW_BLOCK = 256


def _rmsnorm_fwd(x, g, name):
    T, D = x.shape

    def body(x_ref, g_ref, o_ref):
        xf = x_ref[...]
        r = lax.rsqrt(jnp.mean(xf * xf, axis=-1, keepdims=True) + EPS)
        o_ref[...] = (xf * r * g_ref[...]).astype(BF16)

    return pl.pallas_call(
        body, grid=(T // ROW_BLOCK,),
        in_specs=[pl.BlockSpec((ROW_BLOCK, D), lambda i: (i, 0)), pl.BlockSpec((1, D), lambda i: (0, 0))],
        out_specs=pl.BlockSpec((ROW_BLOCK, D), lambda i: (i, 0)),
        out_shape=jax.ShapeDtypeStruct((T, D), BF16), name=name, compiler_params=_cparams(("parallel",)),
    )(x, g)


def _rmsnorm_bwd(dh, x, g, dres, name):
    T, D = x.shape

    def body(dh_ref, x_ref, g_ref, dres_ref, dx_ref, dxb_ref, dg_ref):
        i = pl.program_id(0)
        xf = x_ref[...]
        r = lax.rsqrt(jnp.mean(xf * xf, axis=-1, keepdims=True) + EPS)
        xh = xf * r
        d = dh_ref[...]

        @pl.when(i == 0)
        def _():
            dg_ref[...] = jnp.zeros_like(dg_ref)

        dg_ref[...] += jnp.sum(d * xh, axis=0, keepdims=True)
        dxh = d * g_ref[...]
        dx = r * (dxh - xh * jnp.mean(dxh * xh, axis=-1, keepdims=True)) + dres_ref[...]
        dx_ref[...] = dx
        dxb_ref[...] = dx.astype(BF16)

    row = pl.BlockSpec((ROW_BLOCK, D), lambda i: (i, 0))
    vec = pl.BlockSpec((1, D), lambda i: (0, 0))
    return pl.pallas_call(
        body, grid=(T // ROW_BLOCK,), in_specs=[row, row, vec, row], out_specs=[row, row, vec],
        out_shape=[jax.ShapeDtypeStruct((T, D), F32), jax.ShapeDtypeStruct((T, D), BF16),
                   jax.ShapeDtypeStruct((1, D), F32)],
        name=name, compiler_params=_cparams(("arbitrary",)),
    )(dh, x, g, dres)


def _final_loss(x3_halves, tgt, g, name):
    T, D = tgt.shape

    def body(xa_ref, xb_ref, t_ref, g_ref, loss_ref, dg_ref, dx_ref, dxb_ref):
        i = pl.program_id(0)
        xf = jnp.concatenate([xa_ref[...], xb_ref[...]], axis=1)
        r = lax.rsqrt(jnp.mean(xf * xf, axis=-1, keepdims=True) + EPS)
        xh = xf * r
        gg = g_ref[...]
        err = xh * gg - t_ref[...]

        @pl.when(i == 0)
        def _():
            dg_ref[...] = jnp.zeros_like(dg_ref)
            loss_ref[...] = jnp.zeros_like(loss_ref)

        part = jnp.sum(jnp.sum(err * err, axis=-1, keepdims=True), axis=0, keepdims=True) * (0.5 / D)
        loss_ref[...] += jnp.broadcast_to(part, loss_ref.shape)
        dout = err * (1.0 / D)
        dg_ref[...] += jnp.sum(dout * xh, axis=0, keepdims=True)
        dxh = dout * gg
        dx = r * (dxh - xh * jnp.mean(dxh * xh, axis=-1, keepdims=True))
        dx_ref[...] = dx
        dxb_ref[...] = dx.astype(BF16)

    row = pl.BlockSpec((ROW_BLOCK, D), lambda i: (i, 0))
    vec = pl.BlockSpec((1, D), lambda i: (0, 0))
    return pl.pallas_call(
        body, grid=(T // ROW_BLOCK,),
        in_specs=[pl.BlockSpec((ROW_BLOCK, D // 2), lambda i: (i, 0))] * 2 + [row, vec],
        out_specs=[pl.BlockSpec((1, 128), lambda i: (0, 0)), vec, row, row],
        out_shape=[jax.ShapeDtypeStruct((1, 128), F32), jax.ShapeDtypeStruct((1, D), F32),
                   jax.ShapeDtypeStruct((T, D), F32), jax.ShapeDtypeStruct((T, D), BF16)],
        name=name, compiler_params=_cparams(("arbitrary",)),
    )(*x3_halves, tgt, g)


CONV_BLOCK = 256


def _conv_apply(u, w, b):
    row = lax.broadcasted_iota(jnp.int32, u.shape, 0)
    acc = b + w[CONV_K - 1:CONV_K, :] * u
    shifted = []
    for j in range(1, CONV_K):
        uj = jnp.where(row >= j, pltpu.roll(u, j, axis=0), 0.0)
        shifted.append(uj)
        acc = acc + w[CONV_K - 1 - j:CONV_K - j, :] * uj
    return acc, shifted


def _conv_fwd(proj, conv_w, conv_b, name):
    T = proj.shape[0]
    cb0 = OFF_X // CONV_BLOCK

    def body(u_ref, w_ref, b_ref, o_ref):
        c, _ = _conv_apply(u_ref[...], w_ref[...], b_ref[...])
        o_ref[...] = c * _sigmoid(c)

    return pl.pallas_call(
        body, grid=(CONV_DIM // CONV_BLOCK,),
        in_specs=[pl.BlockSpec((T, CONV_BLOCK), lambda j: (0, cb0 + j)),
                  pl.BlockSpec((CONV_K, CONV_BLOCK), lambda j: (0, j)),
                  pl.BlockSpec((1, CONV_BLOCK), lambda j: (0, j))],
        out_specs=pl.BlockSpec((T, CONV_BLOCK), lambda j: (0, j)),
        out_shape=jax.ShapeDtypeStruct((T, CONV_DIM), F32), name=name, compiler_params=_cparams(("parallel",)),
    )(proj, conv_w, conv_b)


def _conv_bwd(proj, dact, conv_w, conv_b, dproj, name):
    T = proj.shape[0]
    cb0 = OFF_X // CONV_BLOCK

    def body(u_ref, d_ref, w_ref, b_ref, _, du_ref, dw_ref, db_ref):
        u = u_ref[...]
        w = w_ref[...]
        c, shifted = _conv_apply(u, w, b_ref[...])
        sg = _sigmoid(c)
        dc = d_ref[...] * sg * (1.0 + c * (1.0 - sg))
        row = lax.broadcasted_iota(jnp.int32, u.shape, 0)
        du = w[CONV_K - 1:CONV_K, :] * dc
        dw_ref[CONV_K - 1:CONV_K, :] = jnp.sum(dc * u, axis=0, keepdims=True)
        for j in range(1, CONV_K):
            dcj = jnp.where(row < T - j, pltpu.roll(dc, T - j, axis=0), 0.0)
            du = du + w[CONV_K - 1 - j:CONV_K - j, :] * dcj
            dw_ref[CONV_K - 1 - j:CONV_K - j, :] = jnp.sum(dc * shifted[j - 1], axis=0, keepdims=True)
        db_ref[...] = jnp.sum(dc, axis=0, keepdims=True)
        du_ref[...] = du.astype(BF16)

    return pl.pallas_call(
        body, grid=(CONV_DIM // CONV_BLOCK,),
        in_specs=[pl.BlockSpec((T, CONV_BLOCK), lambda j: (0, cb0 + j)),
                  pl.BlockSpec((T, CONV_BLOCK), lambda j: (0, j)),
                  pl.BlockSpec((CONV_K, CONV_BLOCK), lambda j: (0, j)),
                  pl.BlockSpec((1, CONV_BLOCK), lambda j: (0, j)), pl.BlockSpec(memory_space=pl.ANY)],
        out_specs=[pl.BlockSpec((T, CONV_BLOCK), lambda j: (0, cb0 + j)),
                   pl.BlockSpec((CONV_K, CONV_BLOCK), lambda j: (0, j)),
                   pl.BlockSpec((1, CONV_BLOCK), lambda j: (0, j))],
        out_shape=[jax.ShapeDtypeStruct(dproj.shape, BF16), jax.ShapeDtypeStruct((CONV_K, CONV_DIM), F32),
                   jax.ShapeDtypeStruct((1, CONV_DIM), F32)],
        input_output_aliases={4: 0}, name=name, compiler_params=_cparams(("parallel",)),
    )(proj, dact, conv_w, conv_b, dproj)


GROUP_W = D_INNER // N_GROUPS
HEADS_PER_GROUP = N_HEADS // N_GROUPS


def _expand_mat():
    h = lax.broadcasted_iota(jnp.int32, (N_HEADS, D_INNER), 0)
    j = lax.broadcasted_iota(jnp.int32, (N_HEADS, D_INNER), 1)
    return (j // HEAD_DIM == h).astype(F32)


def _reduce_mat(g):
    j = lax.broadcasted_iota(jnp.int32, (GROUP_W, N_HEADS), 0)
    h = lax.broadcasted_iota(jnp.int32, (GROUP_W, N_HEADS), 1)
    return (g * HEADS_PER_GROUP + j // HEAD_DIM == h).astype(F32)


def _col16(v, h):
    lane = lax.broadcasted_iota(jnp.int32, v.shape, 1)
    return jnp.sum(jnp.where(lane == h, v, 0.0), axis=1, keepdims=True)


def _ssd_pre(dt_raw, dtT_raw, dtb, dtbT, alog, alogT):
    Q = CHUNK
    xdt = dt_raw + dtb
    dt = _softplus(xdt)
    dtT = _softplus(dtT_raw + dtbT)
    A = -jnp.exp(alog)
    AT = -jnp.exp(alogT)
    row = lax.broadcasted_iota(jnp.int32, (Q, Q), 0)
    col = lax.broadcasted_iota(jnp.int32, (Q, Q), 1)
    tril = (row >= col).astype(F32)
    triu = (row <= col).astype(F32)
    cs = _hdot(tril, dt * A, "b")
    csT = _hdot(dtT * AT, triu, "a")
    return xdt, dt, A, cs, csT, row >= col, triu


def _decay_matrix(cs, csT, h, causal):
    seg = _col16(cs, h) - csT[h:h + 1, :]
    return jnp.where(causal, jnp.exp(jnp.minimum(seg, 0.0)), 0.0)


def _ssd_in_specs(nc, rev):
    def cidx(c):
        return (nc - 1 - c) if rev else c

    return [
        pl.BlockSpec((CHUNK, D_INNER), lambda c: (cidx(c), 0)),
        pl.BlockSpec((CHUNK, 512), lambda c: (cidx(c), 2)),
        pl.BlockSpec((CHUNK, 512), lambda c: (cidx(c), 3)),
        pl.BlockSpec((CHUNK, D_INNER), lambda c: (cidx(c), 0)),
        pl.BlockSpec((CHUNK, 128), lambda c: (cidx(c), OFF_DT // 128)),
        pl.BlockSpec((N_HEADS, CHUNK), lambda c: (0, cidx(c))),
        pl.BlockSpec((1, N_HEADS), lambda c: (0, 0)),
        pl.BlockSpec((N_HEADS, 1), lambda c: (0, 0)),
        pl.BlockSpec((1, N_HEADS), lambda c: (0, 0)),
        pl.BlockSpec((N_HEADS, 1), lambda c: (0, 0)),
        pl.BlockSpec((1, D_INNER), lambda c: (0, 0)),
        pl.BlockSpec((1, D_INNER), lambda c: (0, 0)),
    ]


def _ssd_fwd(xbc, proj, dtT, dtb, dtbT, alog, alogT, dfull, ng, name):
    T = xbc.shape[0]
    nc = T // CHUNK
    Q = CHUNK

    def body(xs_ref, B_ref, C_ref, z_ref, dt_ref, dtT_ref, dtb_ref, dtbT_ref, al_ref, alT_ref, df_ref, ng_ref,
             y_ref, ypre_ref, hs_ref, h_scr):
        c = pl.program_id(0)

        @pl.when(c == 0)
        def _():
            h_scr[...] = jnp.zeros_like(h_scr)

        _, dt, _, cs, csT, causal, _ = _ssd_pre(dt_ref[:, :N_HEADS], dtT_ref[...], dtb_ref[...], dtbT_ref[...],
                                                al_ref[...], alT_ref[...])
        ex = _expand_mat()
        dt_full = _hdot(dt, ex, "a")
        cs_full = _hdot(cs, ex, "a")
        cs_last = cs_full[Q - 1:Q, :]
        xs = xs_ref[...]
        xd = xs * dt_full
        e_full = jnp.exp(cs_full)
        dec_full = jnp.exp(cs_last - cs_full)
        cd_full = jnp.exp(cs_last)
        lane_head = lax.broadcasted_iota(jnp.int32, (1, GROUP_W), 1) // HEAD_DIM
        for g in range(N_GROUPS):
            sl = slice(g * GROUP_W, (g + 1) * GROUP_W)
            Bg = B_ref[:, g * D_STATE:(g + 1) * D_STATE].astype(BF16)
            Cg = C_ref[:, g * D_STATE:(g + 1) * D_STATE].astype(BF16)
            CB = _dot_nt(Cg, Bg)
            hg = h_scr[g]
            yoff = _dot_nn(Cg, hg.astype(BF16)) * e_full[:, sl]
            xd_g = xd[:, sl]
            S = _dot_tn(Bg, (xd_g * dec_full[:, sl]).astype(BF16))
            xd_b = xd_g.astype(BF16)
            ydiag = jnp.zeros((Q, GROUP_W), F32)
            for r in range(HEADS_PER_GROUP):
                Lm = _decay_matrix(cs, csT, g * HEADS_PER_GROUP + r, causal)
                Gm = (CB * Lm).astype(BF16)
                ydiag = ydiag + _dot_nn(Gm, jnp.where(lane_head == r, xd_b, jnp.zeros_like(xd_b)))
            hs_ref[0, g] = hg
            h_scr[g] = hg * cd_full[:, sl] + S
            ypre = ydiag + yoff + xs[:, sl] * df_ref[:, sl]
            ypre_ref[:, sl] = ypre
            zg = z_ref[:, sl]
            yz = ypre * zg * _sigmoid(zg)
            rn = lax.rsqrt(jnp.mean(yz * yz, axis=-1, keepdims=True) + EPS)
            y_ref[:, sl] = (yz * rn * ng_ref[:, sl]).astype(BF16)

    return pl.pallas_call(
        body, grid=(nc,), in_specs=_ssd_in_specs(nc, False),
        out_specs=[pl.BlockSpec((CHUNK, D_INNER), lambda c: (c, 0)),
                   pl.BlockSpec((CHUNK, D_INNER), lambda c: (c, 0)),
                   pl.BlockSpec((1, N_GROUPS, D_STATE, GROUP_W), lambda c: (c, 0, 0, 0))],
        out_shape=[jax.ShapeDtypeStruct((T, D_INNER + ATTN_W), BF16), jax.ShapeDtypeStruct((T, D_INNER), F32),
                   jax.ShapeDtypeStruct((nc, N_GROUPS, D_STATE, GROUP_W), F32)],
        scratch_shapes=[pltpu.VMEM((N_GROUPS, D_STATE, GROUP_W), F32)],
        name=name, compiler_params=_cparams(("arbitrary",)),
    )(xbc, xbc, xbc, proj, proj, dtT, dtb, dtbT, alog, alogT, dfull, ng)


def _ssd_bwd(xbc, proj, dtT, dtb, dtbT, alog, alogT, dfull, ng, ypre, hs, dy, name):
    T = xbc.shape[0]
    nc = T // CHUNK
    Q = CHUNK

    def body(xs_ref, B_ref, C_ref, z_ref, dt_ref, dtT_ref, dtb_ref, dtbT_ref, al_ref, alT_ref, df_ref, ng_ref,
             ypre_ref, hs_ref, dy_ref,
             dz_ref, dxbc_ref, ddtb_ref, dal_ref, dD_ref, dng_ref, dh_scr):
        step = pl.program_id(0)

        @pl.when(step == 0)
        def _():
            dh_scr[...] = jnp.zeros_like(dh_scr)
            ddtb_ref[...] = jnp.zeros_like(ddtb_ref)
            dal_ref[...] = jnp.zeros_like(dal_ref)
            dD_ref[...] = jnp.zeros_like(dD_ref)
            dng_ref[...] = jnp.zeros_like(dng_ref)

        xdt, dt, A, cs, csT, causal, triu = _ssd_pre(dt_ref[:, :N_HEADS], dtT_ref[...], dtb_ref[...],
                                                    dtbT_ref[...], al_ref[...], alT_ref[...])
        ex = _expand_mat()
        dt_full = _hdot(dt, ex, "a")
        cs_full = _hdot(cs, ex, "a")
        cs_last = cs_full[Q - 1:Q, :]
        xs = xs_ref[...]
        xd = xs * dt_full
        e_full = jnp.exp(cs_full)
        dec_full = jnp.exp(cs_last - cs_full)
        cd_full = jnp.exp(cs_last)
        lane_head = lax.broadcasted_iota(jnp.int32, (1, GROUP_W), 1) // HEAD_DIM
        is_last = lax.broadcasted_iota(jnp.int32, (Q, 1), 0) == Q - 1
        dcs16 = jnp.zeros((Q, N_HEADS), F32)
        ddtx16 = jnp.zeros((Q, N_HEADS), F32)
        dD16 = jnp.zeros((8, N_HEADS), F32)
        lane16 = lax.broadcasted_iota(jnp.int32, (1, N_HEADS), 1)
        sub16 = lax.broadcasted_iota(jnp.int32, (N_HEADS, 1), 0)
        col_sums = jnp.zeros((N_HEADS, Q), F32)
        for g in range(N_GROUPS):
            sl = slice(g * GROUP_W, (g + 1) * GROUP_W)
            red = _reduce_mat(g)
            ypre_g = ypre_ref[:, sl]
            zg = z_ref[:, sl]
            sg = _sigmoid(zg)
            silu = zg * sg
            yz = ypre_g * silu
            rn = lax.rsqrt(jnp.mean(yz * yz, axis=-1, keepdims=True) + EPS)
            yh = yz * rn
            dy_g = dy_ref[:, sl]
            dng_ref[:, sl] += jnp.sum(dy_g * yh, axis=0, keepdims=True)
            dyh = dy_g * ng_ref[:, sl]
            dyz = rn * (dyh - yh * jnp.mean(dyh * yh, axis=-1, keepdims=True))
            dY = dyz * silu
            dz_ref[:, sl] = (dyz * ypre_g * sg * (1.0 + zg * (1.0 - sg))).astype(BF16)
            xs_g = xs[:, sl]
            xd_g = xd[:, sl]
            dec_g = dec_full[:, sl]
            cd_g = cd_full[:, sl]
            d_g = df_ref[:, sl]
            Bg = B_ref[:, g * D_STATE:(g + 1) * D_STATE].astype(BF16)
            Cg = C_ref[:, g * D_STATE:(g + 1) * D_STATE].astype(BF16)
            CB = _dot_nt(Cg, Bg)
            hg = hs_ref[0, g]
            hgb = hg.astype(BF16)
            yoff = _dot_nn(Cg, hgb) * e_full[:, sl]
            dhn = dh_scr[g]
            dhnb = dhn.astype(BF16)
            dYE = (dY * e_full[:, sl]).astype(BF16)
            dC = _dot_nt(dYE, hgb)
            dh_direct = _dot_tn(Cg, dYE)
            dXdd = _dot_nn(Bg, dhnb)
            dB = _dot_nt((xd_g * dec_g).astype(BF16), dhnb)
            dcd = jnp.sum(dhn * hg, axis=0, keepdims=True)
            dh_scr[g] = dh_direct + cd_g * dhn
            dYb = dY.astype(BF16)
            xd_b = xd_g.astype(BF16)
            dCB = jnp.zeros((Q, Q), F32)
            dXd = dXdd * dec_g
            for r in range(HEADS_PER_GROUP):
                h = g * HEADS_PER_GROUP + r
                Lm = _decay_matrix(cs, csT, h, causal)
                Gf = CB * Lm
                dYr = jnp.where(lane_head == r, dYb, jnp.zeros_like(dYb))
                dG = _dot_nt(dYr, xd_b)
                dCB = dCB + dG * Lm
                dXd = dXd + _dot_tn(Gf.astype(BF16), dYr)
                Mm = dG * Gf
                dcs16 = dcs16 + jnp.where(lane16 == h, jnp.sum(Mm, axis=1, keepdims=True), 0.0)
                col_sums = col_sums + jnp.where(sub16 == h, jnp.sum(Mm, axis=0, keepdims=True), 0.0)
            dCBb = dCB.astype(BF16)
            dC = dC + _dot_nn(dCBb, Bg)
            dB = dB + _dot_tn(dCBb, Cg)
            w_state = dXdd * dec_g * xd_g
            t_last = jnp.sum(w_state, axis=0, keepdims=True) + dcd * cd_g
            dcs_g = dY * yoff - w_state + jnp.where(is_last, t_last, 0.0)
            dcs16 = dcs16 + _hdot(dcs_g, red, "a")
            ddtx16 = ddtx16 + _hdot(dXd * xs_g, red, "a")
            dD16 = dD16 + _hdot(jnp.broadcast_to(jnp.sum(dY * xs_g, axis=0, keepdims=True), (8, GROUP_W)), red, "a")
            dxbc_ref[:, sl] = dXd * dt_full[:, sl] + dY * d_g
            dxbc_ref[:, D_INNER + g * D_STATE:D_INNER + (g + 1) * D_STATE] = dB
            dxbc_ref[:, D_INNER + 512 + g * D_STATE:D_INNER + 512 + (g + 1) * D_STATE] = dC
        eye = (lax.broadcasted_iota(jnp.int32, (N_HEADS, N_HEADS), 0)
               == lax.broadcasted_iota(jnp.int32, (N_HEADS, N_HEADS), 1)).astype(BF16)
        dcs16 = dcs16 - sum(_dot_tn(part, eye) for part in _split3(col_sums))
        da = _hdot(triu, dcs16, "b")
        ddt = da * A + ddtx16
        ddt_raw = ddt * _sigmoid(xdt)
        pr = lax.broadcasted_iota(jnp.int32, (N_HEADS, 128), 0)
        pc = lax.broadcasted_iota(jnp.int32, (N_HEADS, 128), 1)
        dz_ref[:, D_INNER:OFF_DT] = jnp.zeros((Q, OFF_DT - D_INNER), BF16)
        dz_ref[:, OFF_DT:OFF_DT + 128] = _hdot(ddt_raw, (pr == pc).astype(F32), "a").astype(BF16)
        dz_ref[:, OFF_DT + 128:] = jnp.zeros((Q, NP - OFF_DT - 128), BF16)
        ddtb_ref[...] += jnp.sum(ddt_raw, axis=0, keepdims=True)
        dal_ref[...] += jnp.sum(da * dt, axis=0, keepdims=True) * A
        dD_ref[...] += dD16[0:1, :]

    def rc(c):
        return nc - 1 - c

    in_specs = _ssd_in_specs(nc, True) + [
        pl.BlockSpec((CHUNK, D_INNER), lambda c: (rc(c), 0)),
        pl.BlockSpec((1, N_GROUPS, D_STATE, GROUP_W), lambda c: (rc(c), 0, 0, 0)),
        pl.BlockSpec((CHUNK, D_INNER), lambda c: (rc(c), 0)),
    ]
    small = pl.BlockSpec((1, N_HEADS), lambda c: (0, 0))
    return pl.pallas_call(
        body, grid=(nc,), in_specs=in_specs,
        out_specs=[pl.BlockSpec((CHUNK, NP), lambda c: (rc(c), 0)),
                   pl.BlockSpec((CHUNK, CONV_DIM), lambda c: (rc(c), 0)),
                   small, small, small,
                   pl.BlockSpec((1, D_INNER), lambda c: (0, 0))],
        out_shape=[jax.ShapeDtypeStruct((T, NP), BF16), jax.ShapeDtypeStruct((T, CONV_DIM), F32),
                   jax.ShapeDtypeStruct((1, N_HEADS), F32), jax.ShapeDtypeStruct((1, N_HEADS), F32),
                   jax.ShapeDtypeStruct((1, N_HEADS), F32), jax.ShapeDtypeStruct((1, D_INNER), F32)],
        scratch_shapes=[pltpu.VMEM((N_GROUPS, D_STATE, GROUP_W), F32)],
        name=name, compiler_params=_cparams(("arbitrary",)),
    )(xbc, xbc, xbc, proj, proj, dtT, dtb, dtbT, alog, alogT, dfull, ng, ypre, hs, dy)


N_PAIRS = ATTN_W // 128
PAIRS_PER_KV = N_PAIRS // 2
ATTN_SCALE = HEAD_DIM ** -0.5


def _kv_variants(kk):
    lo = lax.broadcasted_iota(jnp.int32, kk.shape, 1) < HEAD_DIM
    zero = jnp.zeros_like(kk)
    k00 = jnp.where(lo, kk, zero)
    k11 = jnp.where(lo, zero, kk)
    k01 = pltpu.roll(k00, HEAD_DIM, axis=1)
    k10 = pltpu.roll(k11, HEAD_DIM, axis=1)
    return [[k00.astype(BF16), k01.astype(BF16)], [k10.astype(BF16), k11.astype(BF16)]]


LOG2E = 1.4426950408889634


def _own_block():
    i = lax.broadcasted_iota(jnp.int32, (WINDOW, WINDOW), 0)
    j = lax.broadcasted_iota(jnp.int32, (WINDOW, WINDOW), 1)
    return j <= i


def _fold(own, a):
    return jnp.where(own, a[:, WINDOW:], a[:, :WINDOW])


def _attn_probs(qp, kvar, own, prev_bias, sk):
    s = _dot_nt(qp, kvar)
    sb = jnp.where(own, s[:, WINDOW:], s[:, :WINDOW] + prev_bias) * (ATTN_SCALE * LOG2E)
    sk2 = sk * LOG2E
    m = jnp.maximum(jnp.max(sb, axis=1, keepdims=True), sk2)
    pe = jnp.exp2(sb - m)
    es = jnp.exp2(sk2 - m)
    den = jnp.sum(pe, axis=1, keepdims=True) + es
    inv = 1.0 / den
    return pe * inv, es * inv


def _unfold(own, a):
    zero = jnp.zeros_like(a)
    return jnp.where(own, zero, a), jnp.where(own, a, zero)


def _sink(sinks, r):
    lane = lax.broadcasted_iota(jnp.int32, sinks.shape, 1)
    return jnp.sum(jnp.where(lane == r, sinks, 0.0), axis=1, keepdims=True)


def _kv_specs():
    return [pl.BlockSpec((WINDOW, KV_W), lambda n: (jnp.maximum(n - 1, 0), OFF_K // KV_W)),
            pl.BlockSpec((WINDOW, KV_W), lambda n: (n, OFF_K // KV_W)),
            pl.BlockSpec((WINDOW, KV_W), lambda n: (jnp.maximum(n - 1, 0), OFF_V // KV_W)),
            pl.BlockSpec((WINDOW, KV_W), lambda n: (n, OFF_V // KV_W))]


def _attn_fwd(proj, sinks, og, ycat, name):
    T = proj.shape[0]
    nb = T // WINDOW

    def body(q_ref, kp_ref, kc_ref, vp_ref, vc_ref, s_ref, og_ref, _, y_ref, o_ref):
        n = pl.program_id(0)
        kv = _kv_variants(jnp.concatenate([kp_ref[...], kc_ref[...]], axis=0))
        vv = _kv_variants(jnp.concatenate([vp_ref[...], vc_ref[...]], axis=0))
        own = _own_block()
        prev_bias = jnp.where(n > 0, 0.0, NEG)
        sinks_v = s_ref[...]
        ssq = jnp.zeros((WINDOW, 1), F32)
        for p in range(N_PAIRS):
            j = p // PAIRS_PER_KV
            qp = q_ref[:, p * 128:(p + 1) * 128].astype(BF16)
            o_pair = jnp.zeros((WINDOW, 128), F32)
            for par in range(2):
                pn, _ = _attn_probs(qp, kv[j][par], own, prev_bias, _sink(sinks_v, 2 * p + par))
                p_prev, p_own = _unfold(own, pn.astype(BF16))
                o_pair = o_pair + _dot_nn(p_prev, vv[j][par][:WINDOW]) + _dot_nn(p_own, vv[j][par][WINDOW:])
            o_ref[:, p * 128:(p + 1) * 128] = o_pair
            ssq = ssq + jnp.sum(o_pair * o_pair, axis=1, keepdims=True)
        rn = lax.rsqrt(ssq * (1.0 / ATTN_W) + EPS)
        y_ref[...] = (o_ref[...] * rn * og_ref[...]).astype(BF16)

    return pl.pallas_call(
        body, grid=(nb,),
        in_specs=[pl.BlockSpec((WINDOW, ATTN_W), lambda n: (n, OFF_Q // ATTN_W)), *_kv_specs(),
                  pl.BlockSpec((1, N_HEADS), lambda n: (0, 0)), pl.BlockSpec((1, ATTN_W), lambda n: (0, 0)), ANY],
        out_specs=[pl.BlockSpec((WINDOW, ATTN_W), lambda n: (n, 1)), pl.BlockSpec((WINDOW, ATTN_W), lambda n: (n, 0))],
        out_shape=[jax.ShapeDtypeStruct(ycat.shape, BF16), jax.ShapeDtypeStruct((T, ATTN_W), F32)],
        input_output_aliases={7: 0}, name=name, compiler_params=_cparams(("parallel",)),
    )(proj, proj, proj, proj, proj, sinks, og, ycat)


def _attn_bwd(proj, sinks, og, o, dy, dproj, name):
    T = proj.shape[0]
    nb = T // WINDOW

    def body(q_ref, kp_ref, kc_ref, vp_ref, vc_ref, s_ref, og_ref, o_ref, dy_ref, _,
             dq_ref, dk_ref, dv_ref, ds_ref, dog_ref, qt_scr, dot_scr, ds_scr, p_scr):
        n = pl.program_id(0)

        @pl.when(n == 0)
        def _():
            dk_ref[...] = jnp.zeros_like(dk_ref)
            dv_ref[...] = jnp.zeros_like(dv_ref)
            ds_ref[...] = jnp.zeros_like(ds_ref)
            dog_ref[...] = jnp.zeros_like(dog_ref)

        kv = _kv_variants(jnp.concatenate([kp_ref[...], kc_ref[...]], axis=0))
        vv = _kv_variants(jnp.concatenate([vp_ref[...], vc_ref[...]], axis=0))
        own = _own_block()
        prev_bias = jnp.where(n > 0, 0.0, NEG)
        sinks_v = s_ref[...]
        of = o_ref[...]
        rn = lax.rsqrt(jnp.mean(of * of, axis=-1, keepdims=True) + EPS)
        oh = of * rn
        dyf = dy_ref[...]
        dog_ref[...] += jnp.sum(dyf * oh, axis=0, keepdims=True)
        doh = dyf * og_ref[...]
        do = rn * (doh - oh * jnp.mean(doh * oh, axis=-1, keepdims=True))
        lane = lax.broadcasted_iota(jnp.int32, (1, 128), 1)
        lane16 = lax.broadcasted_iota(jnp.int32, (1, N_HEADS), 1)
        dsink = jnp.zeros((1, N_HEADS), F32)
        for p in range(N_PAIRS):
            j = p // PAIRS_PER_KV
            q_f = q_ref[:, p * 128:(p + 1) * 128]
            qp = q_f.astype(BF16)
            q_t = q_f.T.astype(BF16)
            do_p = do[:, p * 128:(p + 1) * 128]
            o_p = of[:, p * 128:(p + 1) * 128]
            do_b = do_p.astype(BF16)
            do_t = do_p.T.astype(BF16)
            prod = do_p * o_p
            dq_pair = jnp.zeros((WINDOW, 128), F32)
            for par in range(2):
                r = 2 * p + par
                half = (lane < HEAD_DIM) if par == 0 else (lane >= HEAD_DIM)
                pn, ps = _attn_probs(qp, kv[j][par], own, prev_bias, _sink(sinks_v, r))
                delta = jnp.sum(jnp.where(half, prod, 0.0), axis=1, keepdims=True)
                dP = _fold(own, _dot_nt(do_b, vv[j][par]))
                dS = pn * (dP - delta)
                dsink = dsink + jnp.where(lane16 == r, -jnp.sum(ps * delta, axis=0, keepdims=True), 0.0)
                dS_parts = _unfold(own, (dS * ATTN_SCALE).astype(BF16))
                p_parts = _unfold(own, pn.astype(BF16))
                at = ((p % PAIRS_PER_KV) * 2 + par) * WINDOW
                qt_scr[j, :, at:at + WINDOW] = q_t[par * HEAD_DIM:(par + 1) * HEAD_DIM]
                dot_scr[j, :, at:at + WINDOW] = do_t[par * HEAD_DIM:(par + 1) * HEAD_DIM]
                for blk in range(2):
                    dq_pair = dq_pair + _dot_nn(dS_parts[blk], kv[j][par][blk * WINDOW:(blk + 1) * WINDOW])
                    ds_scr[j, blk, at:at + WINDOW, :] = dS_parts[blk]
                    p_scr[j, blk, at:at + WINDOW, :] = p_parts[blk]
            dq_ref[:, p * 128:(p + 1) * 128] = dq_pair.astype(BF16)
        rows = [pl.multiple_of(jnp.maximum(n - 1, 0) * WINDOW, WINDOW), pl.multiple_of(n * WINDOW, WINDOW)]
        for lhs, rhs, ref in [(qt_scr, ds_scr, dk_ref), (dot_scr, p_scr, dv_ref)]:
            for blk in range(2):
                both_t = jnp.concatenate([_dot_nn(lhs[j], rhs[j, blk]) for j in range(2)], axis=0)
                ref[pl.ds(rows[blk], WINDOW), :] += both_t.T
        ds_ref[...] += dsink

    full_kv = pl.BlockSpec((T, KV_W), lambda n: (0, 0))
    blk = pl.BlockSpec((WINDOW, ATTN_W), lambda n: (n, 0))
    return pl.pallas_call(
        body, grid=(nb,),
        in_specs=[pl.BlockSpec((WINDOW, ATTN_W), lambda n: (n, OFF_Q // ATTN_W)), *_kv_specs(),
                  pl.BlockSpec((1, N_HEADS), lambda n: (0, 0)), pl.BlockSpec((1, ATTN_W), lambda n: (0, 0)),
                  blk, pl.BlockSpec((WINDOW, ATTN_W), lambda n: (n, 1)), ANY],
        out_specs=[pl.BlockSpec((WINDOW, ATTN_W), lambda n: (n, OFF_Q // ATTN_W)), full_kv, full_kv,
                   pl.BlockSpec((1, N_HEADS), lambda n: (0, 0)), pl.BlockSpec((1, ATTN_W), lambda n: (0, 0))],
        out_shape=[jax.ShapeDtypeStruct(dproj.shape, BF16), jax.ShapeDtypeStruct((T, KV_W), F32),
                   jax.ShapeDtypeStruct((T, KV_W), F32), jax.ShapeDtypeStruct((1, N_HEADS), F32),
                   jax.ShapeDtypeStruct((1, ATTN_W), F32)],
        scratch_shapes=[pltpu.VMEM((2, HEAD_DIM, 8 * WINDOW), BF16), pltpu.VMEM((2, HEAD_DIM, 8 * WINDOW), BF16),
                        pltpu.VMEM((2, 2, 8 * WINDOW, WINDOW), BF16), pltpu.VMEM((2, 2, 8 * WINDOW, WINDOW), BF16)],
        input_output_aliases={9: 0}, name=name, compiler_params=_cparams(("arbitrary",)),
    )(proj, proj, proj, proj, proj, sinks, og, o, dy, dproj)


ANY = pl.BlockSpec(memory_space=pl.ANY)


def _coords():
    return lax.axis_index("x"), lax.axis_index("y"), lax.axis_index("c")


HBM = pl.BlockSpec(memory_space=pltpu.HBM)
SEM = pl.BlockSpec(memory_space=pltpu.SEMAPHORE)
EFFECT = pltpu.SideEffectType.DATAFLOW_SIDE_EFFECTING


def _in_hbm(a):
    return pltpu.with_memory_space_constraint(a, pltpu.HBM)


def _remote_start(srcs, lands, plan, n_copies, name, after=None):
    ns, nb = len(srcs), len(srcs) + len(lands)
    n_after = 0 if after is None else 1

    def body(*refs):
        src_refs, land_refs = refs[:ns], refs[ns:nb]
        send_sems, recv_sems = refs[nb + n_after], refs[nb + n_after + 1]
        token = refs[-1]
        x, y, c = _coords()
        for i, (sv, dv, dev) in enumerate(plan(src_refs, land_refs, x, y, c)):
            pltpu.make_async_remote_copy(src_ref=sv, dst_ref=dv, send_sem=send_sems.at[i], recv_sem=recv_sems.at[i],
                                         device_id=dev, device_id_type=MESH).start()
        token[...] = jnp.zeros_like(token)

    bufs = list(srcs) + list(lands)
    outs = pl.pallas_call(
        body, name=name,
        out_shape=(pltpu.SemaphoreType.DMA((n_copies,)), pltpu.SemaphoreType.DMA((n_copies,)),
                   *[pltpu.HBM(b.shape, b.dtype) for b in bufs], jax.ShapeDtypeStruct((8, 128), F32)),
        in_specs=[HBM] * nb + [ANY] * n_after,
        out_specs=(SEM, SEM, *[HBM] * nb, pl.BlockSpec(memory_space=pltpu.VMEM)),
        input_output_aliases={i: 2 + i for i in range(nb)},
        compiler_params=pltpu.CompilerParams(has_side_effects=EFFECT),
    )(*[_in_hbm(b) for b in bufs], *([] if after is None else [after]))
    return outs[0], outs[1], list(outs[2:2 + ns]), list(outs[2 + ns:2 + nb]), outs[-1]


def _remote_wait(started, after, plan, name):
    send_sems, recv_sems, srcs, lands, _ = started
    ns, nb = len(srcs), len(srcs) + len(lands)

    def body(*refs):
        src_refs, land_refs = refs[:ns], refs[ns:nb]
        send_sems, recv_sems = refs[nb], refs[nb + 1]
        x, y, c = _coords()
        for i, (sv, dv, dev) in enumerate(plan(src_refs, land_refs, x, y, c)):
            cp = pltpu.make_async_remote_copy(src_ref=sv, dst_ref=dv, send_sem=send_sems.at[i],
                                              recv_sem=recv_sems.at[i], device_id=dev, device_id_type=MESH)
            cp.wait_send()
            cp.wait_recv()

    bufs = list(srcs) + list(lands)
    outs = pl.pallas_call(
        body, name=name, out_shape=tuple(pltpu.HBM(b.shape, b.dtype) for b in bufs),
        in_specs=[HBM] * nb + [SEM, SEM, ANY], out_specs=tuple([HBM] * nb),
        input_output_aliases={i: i for i in range(nb)},
        compiler_params=pltpu.CompilerParams(has_side_effects=EFFECT),
    )(*bufs, send_sems, recv_sems, after)
    return list(outs[:ns]), list(outs[ns:])


def _pair_plan(src_refs, land_refs, x, y, c):
    plan = []
    for s, l in zip(src_refs, land_refs):
        for q in range(4):
            plan.append((s.at[2 * q + (1 - c)], l.at[q], (x, y, 1 - c)))
    return plan


def _pair4_plan(src_refs, land_refs, x, y, c):
    plan = []
    for s, l in zip(src_refs, land_refs):
        for q in range(4):
            plan.append((s.at[q], l.at[q], (x, y, 1 - c)))
    return plan


def _chips_plan(src_refs, land_refs, x, y, c):
    plan = []
    for s, l in zip(src_refs, land_refs):
        for k, (tx, ty) in enumerate([(1 - x, y), (x, 1 - y), (1 - x, 1 - y)]):
            plan.append((s.at[2 * tx + ty], l.at[k], (tx, ty, c)))
    return plan


def _everyone_plan(src_refs, land_refs, x, y, c):
    me = 4 * x + 2 * y + c
    plan = []
    for s, l in zip(src_refs, land_refs):
        for fx, fy, fc in [(0, 0, 1), (1, 0, 0), (1, 0, 1), (0, 1, 0), (0, 1, 1), (1, 1, 0), (1, 1, 1)]:
            dev = ((1 - x) if fx else x, (1 - y) if fy else y, (1 - c) if fc else c)
            plan.append((s, l.at[me], dev))
    return plan


def _pair_add(g8, r1, csel, tr, name):
    _, R, C = r1.shape
    g4 = g8.reshape(4, 2, R, C)

    def body(c_ref, g_ref, r_ref, o_ref):
        o_ref[...] = (g_ref[...].astype(F32) + r_ref[...].astype(F32)).astype(BF16)

    return pl.pallas_call(
        body,
        grid_spec=pltpu.PrefetchScalarGridSpec(
            num_scalar_prefetch=1, grid=(4, R // tr),
            in_specs=[pl.BlockSpec((None, None, tr, C), lambda q, i, cs: (q, cs[0], i, 0)),
                      pl.BlockSpec((None, tr, C), lambda q, i, cs: (q, i, 0))],
            out_specs=pl.BlockSpec((None, tr, C), lambda q, i, cs: (q, i, 0))),
        out_shape=jax.ShapeDtypeStruct((4, R, C), BF16), name=name,
        compiler_params=_cparams(("parallel", "parallel")),
    )(csel, g4, r1)


def _adamw_math(w, g, m, v):
    m = ADAM_B1 * m + (1.0 - ADAM_B1) * g
    v = ADAM_B2 * v + (1.0 - ADAM_B2) * (g * g)
    m_hat = m / (1.0 - ADAM_B1 ** ADAM_STEP)
    v_hat = v / (1.0 - ADAM_B2 ** ADAM_STEP)
    delta = -ADAM_LR * (m_hat / (jnp.sqrt(v_hat) + ADAM_EPS) + ADAM_WD * w)
    return delta, m, v


def _adamw_big(w, m, v, p4, r3, qsel, tile, name):
    R, C = w.shape
    tr, tc = tile

    def body(q_ref, w_ref, m_ref, v_ref, p_ref, r_ref, g_out, d_out, m_out, v_out):
        g = p_ref[...].astype(F32) + r_ref[0].astype(F32) + r_ref[1].astype(F32) + r_ref[2].astype(F32)
        d, mn, vn = _adamw_math(w_ref[...], g, m_ref[...], v_ref[...])
        g_out[...] = g
        d_out[...] = d
        m_out[...] = mn
        v_out[...] = vn

    blk = pl.BlockSpec((tr, tc), lambda i, j, qs: (i, j))
    return pl.pallas_call(
        body,
        grid_spec=pltpu.PrefetchScalarGridSpec(
            num_scalar_prefetch=1, grid=(R // tr, C // tc),
            in_specs=[blk, blk, blk, pl.BlockSpec((None, tr, tc), lambda i, j, qs: (qs[0], i, j)),
                      pl.BlockSpec((3, tr, tc), lambda i, j, qs: (0, i, j))],
            out_specs=[blk, blk, blk, blk]),
        out_shape=[jax.ShapeDtypeStruct((R, C), F32)] * 4, name=name,
        compiler_params=_cparams(("parallel", "parallel")),
    )(qsel, w, m, v, p4, r3)


def _sum_partials(p4, r3, qsel, tc, name):
    _, R, C = p4.shape

    def body(q_ref, p_ref, r_ref, o_ref):
        o_ref[...] = p_ref[...].astype(F32) + r_ref[0].astype(F32) + r_ref[1].astype(F32) + r_ref[2].astype(F32)

    return pl.pallas_call(
        body,
        grid_spec=pltpu.PrefetchScalarGridSpec(
            num_scalar_prefetch=1, grid=(C // tc,),
            in_specs=[pl.BlockSpec((None, R, tc), lambda j, qs: (qs[0], 0, j)),
                      pl.BlockSpec((3, R, tc), lambda j, qs: (0, 0, j))],
            out_specs=pl.BlockSpec((R, tc), lambda j, qs: (0, j))),
        out_shape=jax.ShapeDtypeStruct((R, C), F32), name=name, compiler_params=_cparams(("parallel",)),
    )(qsel, p4, r3)


def _adamw_tiled(w, g, m, v, tc, name):
    R, C = w.shape

    def body(w_ref, g_ref, m_ref, v_ref, d_out, m_out, v_out):
        d, mn, vn = _adamw_math(w_ref[...], g_ref[...], m_ref[...], v_ref[...])
        d_out[...] = d
        m_out[...] = mn
        v_out[...] = vn

    blk = pl.BlockSpec((R, tc), lambda j: (0, j))
    return pl.pallas_call(
        body, grid=(C // tc,), in_specs=[blk] * 4, out_specs=[blk] * 3,
        out_shape=[jax.ShapeDtypeStruct((R, C), F32)] * 3, name=name, compiler_params=_cparams(("parallel",)),
    )(w, g, m, v)


def _small_sum(parts, name):
    def body(p_ref, o_ref):
        acc = p_ref[0]
        for d in range(1, N_DEV):
            acc = acc + p_ref[d]
        o_ref[...] = acc

    return pl.pallas_call(
        body, out_shape=jax.ShapeDtypeStruct(parts.shape[1:], F32), name=name,
        compiler_params=_cparams(),
    )(parts)


def _adamw_small(w, g, m, v, name):
    def body(w_ref, g_ref, m_ref, v_ref, d_out, m_out, v_out):
        d, mn, vn = _adamw_math(w_ref[...], g_ref[...], m_ref[...], v_ref[...])
        d_out[...] = d
        m_out[...] = mn
        v_out[...] = vn

    return pl.pallas_call(
        body, out_shape=[jax.ShapeDtypeStruct(w.shape, F32)] * 3, name=name, compiler_params=_cparams(),
    )(w, g, m, v)


def _row(*pieces):
    r = jnp.concatenate([p.reshape(1, -1) for p in pieces], axis=1)
    return jnp.pad(r, ((0, 0), (0, D_MODEL - r.shape[1])))


def _pack_small(mix, convb, ssmg, attng, mlpg, fing, convw, dtb, alog, dsk, sinks, extra=None):
    last = [dtb, alog, dsk, sinks] + ([extra] if extra is not None else [])
    rows = [_row(mix), _row(convb), _row(ssmg, attng), _row(mlpg), _row(fing),
            jnp.pad(convw, ((0, 0), (0, D_MODEL - convw.shape[1]))), _row(*last)]
    packed = jnp.concatenate(rows, axis=0)
    return jnp.pad(packed, ((0, SMALL_ROWS - packed.shape[0]), (0, 0)))


def _unpack_small(p, conv_n):
    return dict(
        mix_norm_g=p[0:1, :], conv_b=p[1:2, :], ssm_norm_g=p[2:3, :D_INNER], attn_out_norm_g=p[2:3, D_INNER:],
        mlp_norm_g=p[3:4, :], final_norm_g=p[4, :], conv_w=p[5:9, :conv_n][None],
        dt_bias=p[9:10, 0:16], A_log=p[9:10, 16:32], D_skip=p[9:10, 32:48], attn_sinks=p[9:10, 48:64])


WEIGHT_ORDER = ["mix_norm_g", "w_in", "conv_w", "conv_b", "dt_bias", "A_log", "D_skip", "ssm_norm_g", "attn_sinks",
                "attn_out_norm_g", "w_out", "mlp_norm_g", "w_up", "w_down", "final_norm_g"]


def _to_my_columns(w_nat):
    pad = jnp.zeros((w_nat.shape[0], NP - IN_PROJ), w_nat.dtype)
    return jnp.concatenate([w_nat[:, :NAT_DT], w_nat[:, NAT_DT + N_HEADS:], w_nat[:, NAT_DT:NAT_DT + N_HEADS], pad],
                           axis=1)


PER = IN_PROJ // N_DEV
SUPER_STEP = 544
SUPER = 576


def _natural_rows(g, lo, hi):
    segments = [(0, NAT_DT, 0), (NAT_DT, NAT_DT + N_HEADS, OFF_DT - NAT_DT), (NAT_DT + N_HEADS, IN_PROJ, -N_HEADS),
                (IN_PROJ, NP, 0)]
    pieces = [g[max(lo, a) + shift:min(hi, b) + shift] for a, b, shift in segments if max(lo, a) < min(hi, b)]
    return pieces[0] if len(pieces) == 1 else jnp.concatenate(pieces, axis=0)


def _w_in_from_super_slabs(sup):
    seam = SUPER - SUPER_STEP
    units = []
    for i in range(N_DEV):
        base = SUPER_STEP * i
        units.append((base, base + seam, sup[i, :seam] if i == 0 else sup[i - 1, SUPER_STEP:] + sup[i, :seam]))
        units.append((base + seam, base + SUPER_STEP, sup[i, seam:SUPER_STEP]))
    units.append((SUPER_STEP * N_DEV, SUPER_STEP * N_DEV + seam, sup[N_DEV - 1, SUPER_STEP:]))

    def natural(lo, hi):
        return [rows[max(lo, a) - a:min(hi, b) - a] for a, b, rows in units if max(lo, a) < min(hi, b)]

    pieces = natural(0, NAT_DT) + natural(NAT_DT + N_HEADS, IN_PROJ) + natural(NAT_DT, NAT_DT + N_HEADS)
    return jnp.concatenate(pieces + [jnp.zeros((NP - IN_PROJ, D_MODEL), sup.dtype)], axis=0)


def _to_natural_columns(w_my):
    return jnp.concatenate([w_my[:, :NAT_DT], w_my[:, OFF_DT:OFF_DT + N_HEADS], w_my[:, NAT_DT:OFF_DT]], axis=1)


SLAB = 1024


def _grad_w_up(h2, du, name, sel=None, add=None, after=None):
    T, D = h2.shape
    if sel is None:
        pick, n_slab, pre = (lambda j, *cs: j), N_DEV, None
    else:
        pre, other = sel
        pick, n_slab = (lambda j, cs: 2 * j + ((1 - cs[0]) if other else cs[0])), 4
    o_spec = pl.BlockSpec((None, SLAB, SLAB), lambda i, j, k, *cs: (j, i, 0))
    return _matmul(
        h2, du, mode="tn", grid=(D // SLAB, n_slab, 1),
        a_spec=pl.BlockSpec((T, SLAB), lambda i, j, k, *cs: (0, i)),
        b_spec=pl.BlockSpec((T, SLAB), lambda i, j, k, *cs: (0, pick(j, *cs))),
        out_shapes=[jax.ShapeDtypeStruct((n_slab, D, SLAB), BF16)], out_specs=[o_spec], tile=(SLAB, SLAB), name=name,
        extras=() if add is None else (add,), extra_specs=() if add is None else (o_spec,),
        epilogue=None if add is None else (lambda acc, r: (acc + r.astype(F32),)), after=after, prefetch=pre)[0]


def _grad_w_down(act, dx3b, name, sel=None, add=None, after=None):
    T, D = dx3b.shape
    if sel is None:
        pick, n_slab, pre = (lambda i, *cs: i), N_DEV, None
    else:
        pre, other = sel
        pick, n_slab = (lambda i, cs: 2 * i + ((1 - cs[0]) if other else cs[0])), 4
    o_spec = pl.BlockSpec((None, SLAB, SLAB), lambda i, j, k, *cs: (i, 0, j))
    return _matmul(
        act, dx3b, mode="tn", grid=(n_slab, D // SLAB, 1),
        a_spec=pl.BlockSpec((T, SLAB), lambda i, j, k, *cs: (0, pick(i, *cs))),
        b_spec=pl.BlockSpec((T, SLAB), lambda i, j, k, *cs: (0, j)),
        out_shapes=[jax.ShapeDtypeStruct((n_slab, SLAB, D), BF16)], out_specs=[o_spec], tile=(SLAB, SLAB), name=name,
        extras=() if add is None else (add,), extra_specs=() if add is None else (o_spec,),
        epilogue=None if add is None else (lambda acc, r: (acc + r.astype(F32),)), after=after, prefetch=pre)[0]


class _FixedWeights:
    def __init__(self, w_in_p, w_out_f, w_up_s, w_down_f, conv_w_f):
        self.w = (w_in_p, w_out_f, w_up_s, w_down_f, conv_w_f)
        self.grads = {}

    def mixer_weights(self, after):
        return self.w[0], self.w[4], None

    def out_weight(self, after):
        return self.w[1]

    def up_weight(self, after):
        return self.w[2]

    def down_weight(self, h, after):
        return self.w[3][:, h * (D_MODEL // 2):(h + 1) * (D_MODEL // 2)]

    def mlp_grads(self, h2, du, act, dx3b):
        self.grads.update(w_up=_grad_w_up(h2, du, "grad_w_up"),
                          w_down=_grad_w_down(act, dx3b, "grad_w_down").reshape(D_FF, D_MODEL))
        return None

    def grad_sent(self, tag, after):
        return None

    def out_grad(self, g_out):
        self.grads.update(w_out=g_out)
        return None

    def in_grad(self, g_in):
        self.grads.update(w_in=g_in)
        return None


def _local_step(x, tgt, p, hooks):
    T = x.shape[0]
    D = D_MODEL
    h1 = _rmsnorm_fwd(x, p["mix_norm_g"], "norm_mix")
    w_in_t, conv_w_f, token = hooks.mixer_weights(h1)
    (proj,) = _mm_simple(h1, w_in_t, mode="nt", M=T, N=NP, K=D, tm=min(T, 1024), tn=1536, tk=D, out_dtype=F32,
                         name="in_proj", after=token)
    xbc = _conv_fwd(proj, conv_w_f, p["conv_b"], "conv_fwd")
    dtT = proj[:, OFF_DT:OFF_DT + N_HEADS].T
    dtbT = p["dt_bias"].T
    alogT = p["A_log"].T
    dfull = jnp.repeat(p["D_skip"], HEAD_DIM, axis=1)
    ycat, ypre, hs = _ssd_fwd(xbc, proj, dtT, p["dt_bias"], dtbT, p["A_log"], alogT, dfull, p["ssm_norm_g"],
                              "ssd_fwd")
    ycat, o_att = _attn_fwd(proj, p["attn_sinks"], p["attn_out_norm_g"], ycat, "attn_fwd")
    w_out_f = hooks.out_weight(ycat)
    tm = min(T, 1024)
    (x2,) = _mm_simple(ycat, w_out_f, mode="nn", M=T, N=D, K=D, tm=tm, tn=1024, tk=D, out_dtype=F32, name="out_proj",
                       extras=(x,), epilogue=lambda acc, res: (acc + res,))
    h2 = _rmsnorm_fwd(x2, p["mlp_norm_g"], "norm_mlp")
    w_up_s = hooks.up_weight(h2)
    grid = (T // tm, N_DEV, 1)
    u, act = _matmul(
        h2, w_up_s, mode="nn", grid=grid,
        a_spec=pl.BlockSpec((tm, D), lambda i, j, k: (i, 0)),
        b_spec=pl.BlockSpec((None, D, 1024), lambda i, j, k: (j, 0, 0)),
        out_shapes=[jax.ShapeDtypeStruct((T, D_FF), F32), jax.ShapeDtypeStruct((T, D_FF), BF16)],
        out_specs=[pl.BlockSpec((tm, 1024), lambda i, j, k: (i, j))] * 2, tile=(tm, 1024), name="mlp_up",
        epilogue=lambda acc: (acc, jnp.square(jnp.maximum(acc, 0.0))))
    half = D // 2
    w_down_halves, x3_halves = [], []
    for h in range(2):
        w_down_halves.append(hooks.down_weight(h, act if h == 0 else x3_halves[0]))
        x3_halves.append(_matmul(
            act, w_down_halves[h], mode="nn", grid=(T // tm, 1, D_FF // 2048),
            a_spec=pl.BlockSpec((tm, 2048), lambda i, j, k: (i, k)),
            b_spec=pl.BlockSpec((2048, half), lambda i, j, k: (k, 0)),
            out_shapes=[jax.ShapeDtypeStruct((T, half), F32)],
            out_specs=[pl.BlockSpec((tm, half), lambda i, j, k: (i, 0))], tile=(tm, half), name=f"mlp_down_{h}",
            extras=(x2,), extra_specs=[pl.BlockSpec((tm, half), lambda i, j, k, h=h: (i, h))],
            epilogue=lambda acc, res: (acc + res,))[0])
    loss_part, d_fin, dx3, dx3b = _final_loss(x3_halves, tgt, p["final_norm_g"].reshape(1, D), "loss_head")
    (du,) = _matmul(
        dx3b, tuple(w_down_halves), mode="nt", grid=(T // tm, D_FF // 1024, 1),
        a_spec=pl.BlockSpec((tm, D), lambda i, j, k: (i, 0)),
        b_spec=(pl.BlockSpec((1024, half), lambda i, j, k: (j, 0)),) * 2,
        out_shapes=[jax.ShapeDtypeStruct((T, D_FF), BF16)],
        out_specs=[pl.BlockSpec((tm, 1024), lambda i, j, k: (i, j))], tile=(tm, 1024), name="mlp_down_bwd",
        extras=(u,), extra_specs=[pl.BlockSpec((tm, 1024), lambda i, j, k: (i, j))],
        epilogue=lambda acc, uu: (acc * (2.0 * jnp.maximum(uu, 0.0)),),
        dot_fn=lambda a, b0, b1: _dot_nt(a[:, :half], b0) + _dot_nt(a[:, half:], b1))
    token = hooks.mlp_grads(h2, du, act, dx3b)
    (dh2,) = _matmul(
        du, w_up_s, mode="nt", grid=(T // tm, D // 1024, N_DEV // 2),
        a_spec=pl.BlockSpec((tm, 2048), lambda i, j, k: (i, k)),
        b_spec=pl.BlockSpec((2, 1024, 1024), lambda i, j, k: (k, j, 0)),
        out_shapes=[jax.ShapeDtypeStruct((T, D), F32)],
        out_specs=[pl.BlockSpec((tm, 1024), lambda i, j, k: (i, j))], tile=(tm, 1024), name="mlp_up_bwd",
        after=token, dot_fn=lambda a, b: _dot_nt(a[:, :1024], b[0]) + _dot_nt(a[:, 1024:], b[1]))
    dx2, dx2b, d_mlp = _rmsnorm_bwd(dh2, x2, p["mlp_norm_g"], dx3, "norm_mlp_bwd")
    (g_out,) = _mm_simple(ycat, dx2b, mode="tn", M=D, N=D, K=T, tm=1024, tn=1024, tk=T, out_dtype=BF16,
                          name="grad_w_out")
    token = hooks.out_grad(g_out)
    (dy,) = _mm_simple(dx2b, w_out_f, mode="nt", M=T, N=D, K=D, tm=tm, tn=1024, tk=D, out_dtype=F32,
                       name="out_proj_bwd", after=token)
    token = hooks.grad_sent("out", dy)
    ssm_g = p["ssm_norm_g"] if token is None else p["ssm_norm_g"] + token[0:1, 0:1]
    dproj, dxbc_act, d_dtb, d_alog, d_dskip, d_ssmg = _ssd_bwd(
        xbc, proj, dtT, p["dt_bias"], dtbT, p["A_log"], alogT, dfull, ssm_g, ypre, hs, dy, "ssd_bwd")
    dproj, d_convw, d_convb = _conv_bwd(proj, dxbc_act, conv_w_f, p["conv_b"], dproj, "conv_bwd")
    dproj, dk, dv, d_sinks, d_attng = _attn_bwd(proj, p["attn_sinks"], p["attn_out_norm_g"], o_att, dy, dproj,
                                                "attn_bwd")
    dproj = lax.dynamic_update_slice(dproj, jnp.concatenate([dk, dv], axis=1).astype(BF16), (0, OFF_K))
    (g_in,) = _mm_simple(dproj, h1, mode="tn", M=NP, N=D, K=T, tm=1536, tn=1024, tk=T, out_dtype=BF16,
                         name="grad_w_in")
    token = hooks.in_grad(g_in)
    (dh1,) = _mm_simple(dproj, w_in_t, mode="nn", M=T, N=D, K=NP, tm=tm, tn=1024, tk=2304, out_dtype=F32,
                        name="in_proj_bwd", after=token)
    token = hooks.grad_sent("in", dh1)
    mix_g = p["mix_norm_g"] if token is None else p["mix_norm_g"] + token[0:1, 0:1]
    dx, _, d_mix = _rmsnorm_bwd(dh1, x, mix_g, dx2, "norm_mix_bwd")
    small = _pack_small(d_mix, d_convb, d_ssmg, d_attng, d_mlp, d_fin, d_convw, d_dtb, d_alog, d_dskip, d_sinks,
                        extra=loss_part[:, 0:1])
    return dx, small


def _rows_rotated(v, shift, name):
    R, C = v.shape
    tc = 512

    def body(s_ref, v_ref, o_ref):
        o_ref[...] = pltpu.roll(v_ref[...], s_ref[0], axis=0).astype(BF16)

    return pl.pallas_call(
        body,
        grid_spec=pltpu.PrefetchScalarGridSpec(
            num_scalar_prefetch=1, grid=(C // tc,), in_specs=[pl.BlockSpec((R, tc), lambda j, s: (0, j))],
            out_specs=pl.BlockSpec((R, tc), lambda j, s: (0, j))),
        out_shape=jax.ShapeDtypeStruct((R, C), BF16), name=name, compiler_params=_cparams(("parallel",)),
    )(shift, v)


def _landing(own, me):
    zone = lax.empty((N_DEV,) + own.shape, own.dtype)
    return lax.dynamic_update_slice(zone, own[None], (me,) + (0,) * own.ndim)


def _sequencer_gather(owns, split, me, collective_id, name):
    n = len(owns)
    zone_refs = [jax.new_ref(_landing(o, me), memory_space=pltpu.MemorySpace.HBM) for o in owns]
    own_refs = [jax.new_ref(o, memory_space=pltpu.MemorySpace.HBM) for o in owns]
    N_COPIES = 9

    @pl.kernel(mesh=plsc.ScalarSubcoreMesh(axis_name="sequencer", num_cores=1), name=name,
               scratch_types=(pltpu.SemaphoreType.DMA((n, N_COPIES)), pltpu.SemaphoreType.DMA((n, N_COPIES))),
               compiler_params=pltpu.CompilerParams(collective_id=collective_id))
    def launch(send_sems, recv_sems):
        x, y, c = _coords()
        sibling, xn, yn, diag = (x, y, 1 - c), (1 - x, y, c), (x, 1 - y, c), (1 - x, 1 - y, c)
        barrier = pltpu.get_barrier_semaphore()
        for peer in [sibling, xn, yn, diag]:
            pl.semaphore_signal(barrier, inc=1, device_id=peer, device_id_type=MESH)
        pl.semaphore_wait(barrier, 4)

        def block(a, dev, half=None):
            ref = zone_refs[a].at[4 * dev[0] + 2 * dev[1] + dev[2]]
            if half is None:
                return ref
            rows = owns[a].shape[0] // 2
            return ref.at[pl.ds(half * rows, rows)]

        def copy(a, k, src, dst, to):
            return pltpu.make_async_remote_copy(src_ref=src, dst_ref=dst, send_sem=send_sems.at[a, k],
                                                recv_sem=recv_sems.at[a, k], device_id=to, device_id_type=MESH)

        me_dev = (x, y, c)
        sent = []
        first = {}
        for a in range(n):
            for k, peer in enumerate([sibling, xn, yn] + ([] if split[a] else [diag])):
                first[a, k] = copy(a, k, own_refs[a], block(a, me_dev), peer)
                first[a, k].start()
                sent.append(first[a, k])
        from_sibling = []
        for a in range(n):
            first[a, 1].wait_recv()
            sent.append(copy(a, 4, block(a, xn), block(a, xn), sibling))
            if split[a]:
                sent.append(copy(a, 6, block(a, xn, 0), block(a, xn, 0), yn))
            first[a, 2].wait_recv()
            sent.append(copy(a, 5, block(a, yn), block(a, yn), sibling))
            if split[a]:
                sent.append(copy(a, 7, block(a, yn, 1), block(a, yn, 1), xn))
            for cp in sent[-(4 if split[a] else 2):]:
                cp.start()
        for a in range(n):
            if split[a]:
                copy(a, 6, block(a, diag, 0), block(a, diag, 0), yn).wait_recv()
                sent.append(copy(a, 8, block(a, diag, 0), block(a, diag, 0), sibling))
                sent[-1].start()
                copy(a, 7, block(a, diag, 1), block(a, diag, 1), xn).wait_recv()
                sent.append(copy(a, 3, block(a, diag, 1), block(a, diag, 1), sibling))
                sent[-1].start()
            else:
                first[a, 3].wait_recv()
                sent.append(copy(a, 8, block(a, diag), block(a, diag), sibling))
                sent[-1].start()
        for a in range(n):
            first[a, 0].wait_recv()
            copy(a, 4, block(a, xn), block(a, xn), sibling).wait_recv()
            copy(a, 5, block(a, yn), block(a, yn), sibling).wait_recv()
            if split[a]:
                copy(a, 8, block(a, diag, 0), block(a, diag, 0), sibling).wait_recv()
                copy(a, 3, block(a, diag, 1), block(a, diag, 1), sibling).wait_recv()
            else:
                copy(a, 8, block(a, diag), block(a, diag), sibling).wait_recv()
        for cp in sent:
            cp.wait_send()

    launch()
    return zone_refs


class _ShardedWeights:
    def __init__(self, w_in, w_out, conv_w, w_up, w_down, me, csel):
        self.me, self.csel = me, csel
        padded = jnp.pad(jnp.transpose(w_in), ((0, SUPER - PER), (0, 0)))
        own_rows = _rows_rotated(padded, jnp.reshape(2 * me, (1,)).astype(jnp.int32), "w_in_super_slab")
        self.in_ref, self.conv_ref = _sequencer_gather([own_rows, conv_w], [True, False], me, 7,
                                                       "gather_w_in_sequencer")
        (self.out_ref,) = _sequencer_gather([w_out.astype(BF16)], [True], me, 8, "gather_w_out_sequencer")
        (self.up_ref,) = _sequencer_gather([w_up.astype(BF16)], [True], me, 9, "gather_w_up_sequencer")
        down = w_down.astype(BF16)
        self.down_refs = [_sequencer_gather([down[:, h * (D_MODEL // 2):(h + 1) * (D_MODEL // 2)]], [True], me, 10 + h,
                                            f"gather_w_down_{h}_sequencer")[0] for h in range(2)]
        self.reduces = {}
        self.pairs = {}

    def mixer_weights(self, after):
        g_conv = self.conv_ref[...]
        conv_w_f = jnp.concatenate([g_conv[i] for i in range(N_DEV)], axis=1)
        return _w_in_from_super_slabs(self.in_ref[...]), conv_w_f, None

    def out_weight(self, after):
        return self.out_ref[...].reshape(D_MODEL, D_MODEL)

    def up_weight(self, after):
        return self.up_ref[...]

    def down_weight(self, h, after):
        return self.down_refs[h][...].reshape(D_FF, D_MODEL // 2)

    def _chips_start(self, slabs, from_sibling, rows, tag):
        sums = [_pair_add(s, r, self.csel, tr, f"pair_add_{tag}_{i}")
                for i, (s, r, tr) in enumerate(zip(slabs, from_sibling, rows))]
        lands = [lax.empty((3,) + s.shape[1:], s.dtype) for s in sums]
        self.reduces[tag] = _remote_start(sums, lands, _chips_plan, 3 * len(sums), f"reduce_start_{tag}")
        return self.reduces[tag][4]

    def mlp_grads(self, h2, du, act, dx3b):
        def send(part, tag, after):
            st = _remote_start([part], [lax.empty(part.shape, part.dtype)], _pair4_plan, 4,
                               f"reduce_pair_start_{tag}", after=after)
            return st

        def received(st, after, tag):
            return _remote_wait(st, after, _pair4_plan, f"reduce_pair_wait_{tag}")[1][0]

        def to_chips(sums, tag):
            self.reduces[tag] = _remote_start([sums], [lax.empty((3,) + sums.shape[1:], sums.dtype)], _chips_plan, 3,
                                              f"reduce_start_{tag}")
            return self.reduces[tag][4]

        up_send = _grad_w_up(h2, du, "grad_w_up_send", sel=(self.csel, True))
        st_up = send(up_send, "up", None)
        down_send = _grad_w_down(act, dx3b, "grad_w_down_send", sel=(self.csel, True), after=st_up[4])
        st_down = send(down_send, "down", None)
        up_sum = _grad_w_up(h2, du, "grad_w_up_keep", sel=(self.csel, False), add=received(st_up, down_send, "up"),
                            after=st_down[4])
        token = to_chips(up_sum, "up")
        down_sum = _grad_w_down(act, dx3b, "grad_w_down_keep", sel=(self.csel, False),
                                add=received(st_down, up_sum, "down"), after=token)
        return to_chips(down_sum, "down")

    def _pair_start(self, slabs, tag):
        land = lax.empty((4,) + slabs.shape[1:], slabs.dtype)
        self.pairs[tag] = _remote_start([slabs], [land], _pair_plan, 4, f"reduce_pair_start_{tag}")
        return self.pairs[tag][4]

    def grad_sent(self, tag, after):
        slabs, from_sibling = _remote_wait(self.pairs[tag], after, _pair_plan, f"reduce_pair_wait_{tag}")
        return self._chips_start(slabs, from_sibling, [slabs[0].shape[1]], tag)

    def out_grad(self, g_out):
        return self._pair_start(g_out.reshape(N_DEV, D_MODEL // N_DEV, D_MODEL), "out")

    def in_grad(self, g_in):
        return self._pair_start(
            jnp.stack([_natural_rows(g_in, SUPER_STEP * j, SUPER_STEP * j + SUPER) for j in range(N_DEV)]), "in")

    def small_start(self, small):
        self.st_small = _remote_start([small], [_landing(small, self.me)], _everyone_plan, N_DEV - 1, "gather_start_small")

    def small_end(self, after):
        return _remote_wait(self.st_small, after, _everyone_plan, "gather_small_wait")[1][0]

    def reduce_end(self, tag, after):
        return _remote_wait(self.reduces[tag], after, _chips_plan, f"reduce_wait_{tag}")


def kernel(x, mix_norm_g, w_in, conv_w, conv_b, dt_bias, A_log, D_skip, ssm_norm_g, attn_sinks, attn_out_norm_g, w_out, mlp_norm_g, w_up, w_down, final_norm_g, loss_target, m_mix_norm_g, m_w_in, m_conv_w, m_conv_b, m_dt_bias, m_A_log, m_D_skip, m_ssm_norm_g, m_attn_sinks, m_attn_out_norm_g, m_w_out, m_mlp_norm_g, m_w_up, m_w_down, m_final_norm_g, v_mix_norm_g, v_w_in, v_conv_w, v_conv_b, v_dt_bias, v_A_log, v_D_skip, v_ssm_norm_g, v_attn_sinks, v_attn_out_norm_g, v_w_out, v_mlp_norm_g, v_w_up, v_w_down, v_final_norm_g):
    xi, yi, ci = _coords()
    me = 4 * xi + 2 * yi + ci
    csel = jnp.reshape(ci, (1,)).astype(jnp.int32)
    qsel = jnp.reshape(2 * xi + yi, (1,)).astype(jnp.int32)
    w = dict(mix_norm_g=mix_norm_g, conv_b=conv_b, dt_bias=dt_bias, A_log=A_log, D_skip=D_skip,
             ssm_norm_g=ssm_norm_g, attn_sinks=attn_sinks, attn_out_norm_g=attn_out_norm_g, mlp_norm_g=mlp_norm_g,
             final_norm_g=final_norm_g)
    hooks = _ShardedWeights(w_in[0], w_out[0], conv_w[0], w_up[0], w_down[0], me, csel)
    p = dict(w)
    dx, small = _local_step(x[0], loss_target[0], p, hooks)
    hooks.small_start(small)
    big = {}
    after = dx
    for name, wt, mt, vt, tile in [
            ("up", w_up, m_w_up, v_w_up, (512, SLAB)), ("down", w_down, m_w_down, v_w_down, (256, D_MODEL)),
            ("out", w_out, m_w_out, v_w_out, (256, D_MODEL))]:
        (chip_sums,), (from_chips,) = hooks.reduce_end(name, after)
        res = _adamw_big(wt[0], mt[0], vt[0], chip_sums, from_chips, qsel, tile, f"adamw_w_{name}")
        big["w_" + name] = tuple(r[None] for r in res)
        after = res[0]
    (chip_sums,), (from_chips,) = hooks.reduce_end("in", after)
    g_super = _sum_partials(chip_sums, from_chips, qsel, 512, "grad_w_in_sum")
    g_in = lax.dynamic_slice(g_super, (2 * me, 0), (PER, D_MODEL))
    res = _adamw_tiled(jnp.transpose(w_in[0]), g_in, jnp.transpose(m_w_in[0]), jnp.transpose(v_w_in[0]), 512,
                       "adamw_w_in")
    big["w_in"] = tuple(jnp.transpose(r)[None] for r in (g_in, *res))
    after = res[0]
    gsum = _small_sum(hooks.small_end(after), "small_sum")
    loss = gsum[9, 64]
    gs = _unpack_small(gsum, CONV_DIM)
    cw = CONV_DIM // N_DEV
    g_conv_shard = lax.dynamic_slice(gsum[5:9, :], (0, me * cw), (CONV_K, cw))

    def pack(s):
        return _pack_small(s["mix_norm_g"], s["conv_b"], s["ssm_norm_g"], s["attn_out_norm_g"], s["mlp_norm_g"],
                           s["final_norm_g"], s["conv_w"][0], s["dt_bias"], s["A_log"], s["D_skip"], s["attn_sinks"])

    wp = pack(dict(w, conv_w=conv_w))
    mp = pack(dict(mix_norm_g=m_mix_norm_g, conv_b=m_conv_b, ssm_norm_g=m_ssm_norm_g,
                   attn_out_norm_g=m_attn_out_norm_g, mlp_norm_g=m_mlp_norm_g, final_norm_g=m_final_norm_g,
                   conv_w=m_conv_w, dt_bias=m_dt_bias, A_log=m_A_log, D_skip=m_D_skip, attn_sinks=m_attn_sinks))
    vp = pack(dict(mix_norm_g=v_mix_norm_g, conv_b=v_conv_b, ssm_norm_g=v_ssm_norm_g,
                   attn_out_norm_g=v_attn_out_norm_g, mlp_norm_g=v_mlp_norm_g, final_norm_g=v_final_norm_g,
                   conv_w=v_conv_w, dt_bias=v_dt_bias, A_log=v_A_log, D_skip=v_D_skip, attn_sinks=v_attn_sinks))
    gp = jnp.concatenate([gsum[0:5], jnp.pad(g_conv_shard, ((0, 0), (0, D_MODEL - cw))), gsum[9:10],
                          jnp.zeros((SMALL_ROWS - 10, D_MODEL), F32)], axis=0)
    dp, mnp, vnp = _adamw_small(wp, gp, mp, vp, "adamw_small")
    grads = dict(gs, conv_w=g_conv_shard[None])
    deltas = _unpack_small(dp, cw)
    new_m = _unpack_small(mnp, cw)
    new_v = _unpack_small(vnp, cw)
    for k, name in enumerate(["w_in", "w_out", "w_up", "w_down"]):
        grads[name], deltas[name], new_m[name], new_v[name] = big[name]
    return (loss, dx[None], *[grads[n] for n in WEIGHT_ORDER], *[deltas[n] for n in WEIGHT_ORDER],
            *[new_m[n] for n in WEIGHT_ORDER], *[new_v[n] for n in WEIGHT_ORDER])
```

```python
import jax
import jax.numpy as jnp
from jax import lax
from jax.experimental import pallas as pl
from jax.experimental.pallas import tpu as pltpu
from jax.experimental.pallas import tpu_sc as plsc

F32 = jnp.float32
BF16 = jnp.bfloat16
MESH = pl.DeviceIdType.MESH

EPS = 1e-5
D_MODEL = 2048
D_INNER = 1024
N_HEADS = 16
HEAD_DIM = 64
N_GROUPS = 4
D_STATE = 128
CHUNK = 128
CONV_K = 4
CONV_DIM = 2048
ATTN_W = 1024
KV_W = 128
WINDOW = 128
D_FF = 8192
IN_PROJ = 4368
N_DEV = 8
NP = 4608
OFF_Z, OFF_X, OFF_B, OFF_C, OFF_Q, OFF_K, OFF_V, OFF_DT = 0, 1024, 2048, 2560, 3072, 4096, 4224, 4352
NAT_DT = 3072

ADAM_LR = 0.001
ADAM_B1 = 0.9
ADAM_B2 = 0.999
ADAM_EPS = 1e-08
ADAM_WD = 0.01
ADAM_STEP = 10

VMEM_LIMIT = 52 * 1024 * 1024
SMALL_ROWS = 16
NEG = -1e30


def _cparams(sem=None):
    return pltpu.CompilerParams(dimension_semantics=sem, vmem_limit_bytes=VMEM_LIMIT)


def _split3(v):
    hi = v.astype(BF16)
    rest = v - hi.astype(F32)
    mid = rest.astype(BF16)
    return hi, mid, (rest - mid.astype(F32)).astype(BF16)


def _hdot(a, b, data):
    if data == "a":
        sel = b.astype(BF16)
        return sum(_dot_nn(part, sel) for part in _split3(a))
    sel = a.astype(BF16)
    return sum(_dot_nn(sel, part) for part in _split3(b))


def _dot_nn(a, b):
    return lax.dot_general(a, b, (((1,), (0,)), ((), ())), preferred_element_type=F32)


def _dot_nt(a, b):
    return lax.dot_general(a, b, (((1,), (1,)), ((), ())), preferred_element_type=F32)


def _dot_tn(a, b):
    return lax.dot_general(a, b, (((0,), (0,)), ((), ())), preferred_element_type=F32)


def _softplus(v):
    return jnp.maximum(v, 0.0) + jnp.log1p(jnp.exp(-jnp.abs(v)))


def _sigmoid(v):
    return 1.0 / (1.0 + jnp.exp(-v))


def _matmul(a, b, *, mode, grid, a_spec, b_spec, out_shapes, out_specs, tile, name,
            extras=(), extra_specs=(), epilogue=None, after=None, dot_fn=None, prefetch=None):
    nk = grid[2]
    n_ex = len(extras)
    n_out = len(out_shapes)
    bs, b_specs = (b, b_spec) if isinstance(b, tuple) else ((b,), (b_spec,))
    n_in = 1 + len(bs)
    dot = dot_fn if dot_fn is not None else {"nn": _dot_nn, "nt": _dot_nt, "tn": _dot_tn}[mode]

    def finish(acc, ex_refs, out_refs):
        res = (acc,) if epilogue is None else epilogue(acc, *[e[...] for e in ex_refs])
        for o, r in zip(out_refs, res):
            o[...] = r.astype(o.dtype)

    def body(*refs):
        ex_refs = refs[n_in:n_in + n_ex]
        out_refs = refs[n_in + n_ex:n_in + n_ex + n_out]
        part = dot(*[r[...].astype(BF16) for r in refs[:n_in]])
        if nk == 1:
            finish(part, ex_refs, out_refs)
        else:
            acc_ref = refs[-1]
            k = pl.program_id(2)

            @pl.when(k == 0)
            def _():
                acc_ref[...] = part

            @pl.when(k > 0)
            def _():
                acc_ref[...] += part

            @pl.when(k == nk - 1)
            def _():
                finish(acc_ref[...], ex_refs, out_refs)

    scratch = [] if nk == 1 else [pltpu.VMEM(tile, F32)]
    n_pre = 0 if prefetch is None else 1
    tok_specs = [] if after is None else [pl.BlockSpec((8, 128), lambda *_: (0, 0))]
    tok_args = [] if after is None else [after]

    def body_with_token(*refs):
        refs = refs[n_pre:]
        body(*refs[:n_in + n_ex], *refs[n_in + n_ex + len(tok_args):])

    in_specs = [a_spec, *b_specs, *extra_specs, *tok_specs]
    params = _cparams(("parallel", "parallel", "arbitrary"))
    if prefetch is None:
        return pl.pallas_call(
            body_with_token, grid=grid, in_specs=in_specs, out_specs=list(out_specs), out_shape=list(out_shapes),
            scratch_shapes=scratch, name=name, compiler_params=params)(a, *bs, *extras, *tok_args)
    return pl.pallas_call(
        body_with_token,
        grid_spec=pltpu.PrefetchScalarGridSpec(num_scalar_prefetch=1, grid=grid, in_specs=in_specs,
                                               out_specs=list(out_specs), scratch_shapes=scratch),
        out_shape=list(out_shapes), name=name, compiler_params=params)(prefetch, a, *bs, *extras, *tok_args)


def _mm_simple(a, b, *, mode, M, N, K, tm, tn, tk, out_dtype, name, extras=(), epilogue=None, n_out=1,
               out_dtypes=None, after=None):
    grid = (M // tm, N // tn, K // tk)
    if mode == "nn":
        a_spec = pl.BlockSpec((tm, tk), lambda i, j, k: (i, k))
        b_spec = pl.BlockSpec((tk, tn), lambda i, j, k: (k, j))
    elif mode == "nt":
        a_spec = pl.BlockSpec((tm, tk), lambda i, j, k: (i, k))
        b_spec = pl.BlockSpec((tn, tk), lambda i, j, k: (j, k))
    else:
        a_spec = pl.BlockSpec((tk, tm), lambda i, j, k: (k, i))
        b_spec = pl.BlockSpec((tk, tn), lambda i, j, k: (k, j))
    o_spec = pl.BlockSpec((tm, tn), lambda i, j, k: (i, j))
    dts = out_dtypes if out_dtypes is not None else [out_dtype] * n_out
    return _matmul(a, b, mode=mode, grid=grid, a_spec=a_spec, b_spec=b_spec,
                   out_shapes=[jax.ShapeDtypeStruct((M, N), d) for d in dts],
                   out_specs=[o_spec] * len(dts), tile=(tm, tn), name=name,
                   extras=extras, extra_specs=[o_spec] * len(extras), epilogue=epilogue, after=after)


ROW_BLOCK = 256


def _rmsnorm_fwd(x, g, name):
    T, D = x.shape

    def body(x_ref, g_ref, o_ref):
        xf = x_ref[...]
        r = lax.rsqrt(jnp.mean(xf * xf, axis=-1, keepdims=True) + EPS)
        o_ref[...] = (xf * r * g_ref[...]).astype(BF16)

    return pl.pallas_call(
        body, grid=(T // ROW_BLOCK,),
        in_specs=[pl.BlockSpec((ROW_BLOCK, D), lambda i: (i, 0)), pl.BlockSpec((1, D), lambda i: (0, 0))],
        out_specs=pl.BlockSpec((ROW_BLOCK, D), lambda i: (i, 0)),
        out_shape=jax.ShapeDtypeStruct((T, D), BF16), name=name, compiler_params=_cparams(("parallel",)),
    )(x, g)


def _rmsnorm_bwd(dh, x, g, dres, name):
    T, D = x.shape

    def body(dh_ref, x_ref, g_ref, dres_ref, dx_ref, dxb_ref, dg_ref):
        i = pl.program_id(0)
        xf = x_ref[...]
        r = lax.rsqrt(jnp.mean(xf * xf, axis=-1, keepdims=True) + EPS)
        xh = xf * r
        d = dh_ref[...]

        @pl.when(i == 0)
        def _():
            dg_ref[...] = jnp.zeros_like(dg_ref)

        dg_ref[...] += jnp.sum(d * xh, axis=0, keepdims=True)
        dxh = d * g_ref[...]
        dx = r * (dxh - xh * jnp.mean(dxh * xh, axis=-1, keepdims=True)) + dres_ref[...]
        dx_ref[...] = dx
        dxb_ref[...] = dx.astype(BF16)

    row = pl.BlockSpec((ROW_BLOCK, D), lambda i: (i, 0))
    vec = pl.BlockSpec((1, D), lambda i: (0, 0))
    return pl.pallas_call(
        body, grid=(T // ROW_BLOCK,), in_specs=[row, row, vec, row], out_specs=[row, row, vec],
        out_shape=[jax.ShapeDtypeStruct((T, D), F32), jax.ShapeDtypeStruct((T, D), BF16),
                   jax.ShapeDtypeStruct((1, D), F32)],
        name=name, compiler_params=_cparams(("arbitrary",)),
    )(dh, x, g, dres)


def _matmul_rmsnorm_bwd(d, w, x, g, dres, name, after=None, tm=512, tk=1536):
    T, K = d.shape
    D = w.shape[1]
    nk = K // tk

    def body(d_ref, w_ref, x_ref, g_ref, dres_ref, *rest):
        dx_ref, dg_ref, acc_ref = rest[-3:]
        i, k = pl.program_id(0), pl.program_id(1)
        part = _dot_nn(d_ref[...].astype(BF16), w_ref[...].astype(BF16))

        @pl.when(k == 0)
        def _():
            acc_ref[...] = part

        @pl.when(k > 0)
        def _():
            acc_ref[...] += part

        @pl.when(k == nk - 1)
        def _():
            xf = x_ref[...]
            r = lax.rsqrt(jnp.mean(xf * xf, axis=-1, keepdims=True) + EPS)
            xh = xf * r
            dh = acc_ref[...]

            @pl.when(i == 0)
            def _():
                dg_ref[...] = jnp.zeros_like(dg_ref)

            dg_ref[...] += jnp.sum(dh * xh, axis=0, keepdims=True)
            dxh = dh * g_ref[...]
            dx_ref[...] = r * (dxh - xh * jnp.mean(dxh * xh, axis=-1, keepdims=True)) + dres_ref[...]

    row = pl.BlockSpec((tm, D), lambda i, k: (i, 0))
    vec = pl.BlockSpec((1, D), lambda i, k: (0, 0))
    tok_specs = [] if after is None else [pl.BlockSpec((8, 128), lambda i, k: (0, 0))]
    return pl.pallas_call(
        body, grid=(T // tm, nk),
        in_specs=[pl.BlockSpec((tm, tk), lambda i, k: (i, k)), pl.BlockSpec((tk, D), lambda i, k: (k, 0)), row, vec,
                  row, *tok_specs],
        out_specs=[row, vec],
        out_shape=[jax.ShapeDtypeStruct((T, D), F32), jax.ShapeDtypeStruct((1, D), F32)],
        scratch_shapes=[pltpu.VMEM((tm, D), F32)], name=name, compiler_params=_cparams(("arbitrary", "arbitrary")),
    )(d, w, x, g, dres, *([] if after is None else [after]))


def _final_loss(x3_halves, tgt, g, name):
    T, D = tgt.shape

    def body(xa_ref, xb_ref, t_ref, g_ref, loss_ref, dg_ref, dx_ref, dxb_ref):
        i = pl.program_id(0)
        xf = jnp.concatenate([xa_ref[...], xb_ref[...]], axis=1)
        r = lax.rsqrt(jnp.mean(xf * xf, axis=-1, keepdims=True) + EPS)
        xh = xf * r
        gg = g_ref[...]
        err = xh * gg - t_ref[...]

        @pl.when(i == 0)
        def _():
            dg_ref[...] = jnp.zeros_like(dg_ref)
            loss_ref[...] = jnp.zeros_like(loss_ref)

        part = jnp.sum(jnp.sum(err * err, axis=-1, keepdims=True), axis=0, keepdims=True) * (0.5 / D)
        loss_ref[...] += jnp.broadcast_to(part, loss_ref.shape)
        dout = err * (1.0 / D)
        dg_ref[...] += jnp.sum(dout * xh, axis=0, keepdims=True)
        dxh = dout * gg
        dx = r * (dxh - xh * jnp.mean(dxh * xh, axis=-1, keepdims=True))
        dx_ref[...] = dx
        dxb_ref[...] = dx.astype(BF16)

    row = pl.BlockSpec((ROW_BLOCK, D), lambda i: (i, 0))
    vec = pl.BlockSpec((1, D), lambda i: (0, 0))
    return pl.pallas_call(
        body, grid=(T // ROW_BLOCK,),
        in_specs=[pl.BlockSpec((ROW_BLOCK, D // 2), lambda i: (i, 0))] * 2 + [row, vec],
        out_specs=[pl.BlockSpec((1, 128), lambda i: (0, 0)), vec, row, row],
        out_shape=[jax.ShapeDtypeStruct((1, 128), F32), jax.ShapeDtypeStruct((1, D), F32),
                   jax.ShapeDtypeStruct((T, D), F32), jax.ShapeDtypeStruct((T, D), BF16)],
        name=name, compiler_params=_cparams(("arbitrary",)),
    )(*x3_halves, tgt, g)


CONV_BLOCK = 256


def _conv_apply(u, w, b):
    row = lax.broadcasted_iota(jnp.int32, u.shape, 0)
    acc = b + w[CONV_K - 1:CONV_K, :] * u
    shifted = []
    for j in range(1, CONV_K):
        uj = jnp.where(row >= j, pltpu.roll(u, j, axis=0), 0.0)
        shifted.append(uj)
        acc = acc + w[CONV_K - 1 - j:CONV_K - j, :] * uj
    return acc, shifted


def _conv_fwd(proj, conv_w, conv_b, name):
    T = proj.shape[0]
    cb0 = OFF_X // CONV_BLOCK

    def body(u_ref, w_ref, b_ref, o_ref):
        c, _ = _conv_apply(u_ref[...], w_ref[...], b_ref[...])
        o_ref[...] = c * _sigmoid(c)

    return pl.pallas_call(
        body, grid=(CONV_DIM // CONV_BLOCK,),
        in_specs=[pl.BlockSpec((T, CONV_BLOCK), lambda j: (0, cb0 + j)),
                  pl.BlockSpec((CONV_K, CONV_BLOCK), lambda j: (0, j)),
                  pl.BlockSpec((1, CONV_BLOCK), lambda j: (0, j))],
        out_specs=pl.BlockSpec((T, CONV_BLOCK), lambda j: (0, j)),
        out_shape=jax.ShapeDtypeStruct((T, CONV_DIM), F32), name=name, compiler_params=_cparams(("parallel",)),
    )(proj, conv_w, conv_b)


def _conv_bwd(proj, dact, conv_w, conv_b, dproj, name):
    T = proj.shape[0]
    cb0 = OFF_X // CONV_BLOCK

    def body(u_ref, d_ref, w_ref, b_ref, _, du_ref, dw_ref, db_ref):
        u = u_ref[...]
        w = w_ref[...]
        c, shifted = _conv_apply(u, w, b_ref[...])
        sg = _sigmoid(c)
        dc = d_ref[...] * sg * (1.0 + c * (1.0 - sg))
        row = lax.broadcasted_iota(jnp.int32, u.shape, 0)
        du = w[CONV_K - 1:CONV_K, :] * dc
        dw_ref[CONV_K - 1:CONV_K, :] = jnp.sum(dc * u, axis=0, keepdims=True)
        for j in range(1, CONV_K):
            dcj = jnp.where(row < T - j, pltpu.roll(dc, T - j, axis=0), 0.0)
            du = du + w[CONV_K - 1 - j:CONV_K - j, :] * dcj
            dw_ref[CONV_K - 1 - j:CONV_K - j, :] = jnp.sum(dc * shifted[j - 1], axis=0, keepdims=True)
        db_ref[...] = jnp.sum(dc, axis=0, keepdims=True)
        du_ref[...] = du.astype(BF16)

    return pl.pallas_call(
        body, grid=(CONV_DIM // CONV_BLOCK,),
        in_specs=[pl.BlockSpec((T, CONV_BLOCK), lambda j: (0, cb0 + j)),
                  pl.BlockSpec((T, CONV_BLOCK), lambda j: (0, j)),
                  pl.BlockSpec((CONV_K, CONV_BLOCK), lambda j: (0, j)),
                  pl.BlockSpec((1, CONV_BLOCK), lambda j: (0, j)), pl.BlockSpec(memory_space=pl.ANY)],
        out_specs=[pl.BlockSpec((T, CONV_BLOCK), lambda j: (0, cb0 + j)),
                   pl.BlockSpec((CONV_K, CONV_BLOCK), lambda j: (0, j)),
                   pl.BlockSpec((1, CONV_BLOCK), lambda j: (0, j))],
        out_shape=[jax.ShapeDtypeStruct(dproj.shape, BF16), jax.ShapeDtypeStruct((CONV_K, CONV_DIM), F32),
                   jax.ShapeDtypeStruct((1, CONV_DIM), F32)],
        input_output_aliases={4: 0}, name=name, compiler_params=_cparams(("parallel",)),
    )(proj, dact, conv_w, conv_b, dproj)


GROUP_W = D_INNER // N_GROUPS
HEADS_PER_GROUP = N_HEADS // N_GROUPS


def _expand_mat():
    h = lax.broadcasted_iota(jnp.int32, (N_HEADS, D_INNER), 0)
    j = lax.broadcasted_iota(jnp.int32, (N_HEADS, D_INNER), 1)
    return (j // HEAD_DIM == h).astype(F32)


def _reduce_mat(g):
    j = lax.broadcasted_iota(jnp.int32, (GROUP_W, N_HEADS), 0)
    h = lax.broadcasted_iota(jnp.int32, (GROUP_W, N_HEADS), 1)
    return (g * HEADS_PER_GROUP + j // HEAD_DIM == h).astype(F32)


def _col16(v, h):
    lane = lax.broadcasted_iota(jnp.int32, v.shape, 1)
    return jnp.sum(jnp.where(lane == h, v, 0.0), axis=1, keepdims=True)


def _ssd_pre(dt_raw, dtT_raw, dtb, dtbT, alog, alogT):
    Q = CHUNK
    xdt = dt_raw + dtb
    dt = _softplus(xdt)
    dtT = _softplus(dtT_raw + dtbT)
    A = -jnp.exp(alog)
    AT = -jnp.exp(alogT)
    row = lax.broadcasted_iota(jnp.int32, (Q, Q), 0)
    col = lax.broadcasted_iota(jnp.int32, (Q, Q), 1)
    tril = (row >= col).astype(F32)
    triu = (row <= col).astype(F32)
    cs = _hdot(tril, dt * A, "b")
    csT = _hdot(dtT * AT, triu, "a")
    return xdt, dt, A, cs, csT, row >= col, triu


def _decay_matrix(cs, csT, h, causal):
    seg = _col16(cs, h) - csT[h:h + 1, :]
    return jnp.where(causal, jnp.exp(jnp.minimum(seg, 0.0)), 0.0)


def _ssd_in_specs(nc, rev):
    def cidx(c):
        return (nc - 1 - c) if rev else c

    return [
        pl.BlockSpec((CHUNK, D_INNER), lambda c: (cidx(c), 0)),
        pl.BlockSpec((CHUNK, 512), lambda c: (cidx(c), 2)),
        pl.BlockSpec((CHUNK, 512), lambda c: (cidx(c), 3)),
        pl.BlockSpec((CHUNK, D_INNER), lambda c: (cidx(c), 0)),
        pl.BlockSpec((CHUNK, 128), lambda c: (cidx(c), OFF_DT // 128)),
        pl.BlockSpec((N_HEADS, CHUNK), lambda c: (0, cidx(c))),
        pl.BlockSpec((1, N_HEADS), lambda c: (0, 0)),
        pl.BlockSpec((N_HEADS, 1), lambda c: (0, 0)),
        pl.BlockSpec((1, N_HEADS), lambda c: (0, 0)),
        pl.BlockSpec((N_HEADS, 1), lambda c: (0, 0)),
        pl.BlockSpec((1, D_INNER), lambda c: (0, 0)),
        pl.BlockSpec((1, D_INNER), lambda c: (0, 0)),
    ]


def _ssd_fwd(xbc, proj, dtT, dtb, dtbT, alog, alogT, dfull, ng, name):
    T = xbc.shape[0]
    nc = T // CHUNK
    Q = CHUNK

    def body(xs_ref, B_ref, C_ref, z_ref, dt_ref, dtT_ref, dtb_ref, dtbT_ref, al_ref, alT_ref, df_ref, ng_ref,
             y_ref, ypre_ref, hs_ref, h_scr):
        c = pl.program_id(0)

        @pl.when(c == 0)
        def _():
            h_scr[...] = jnp.zeros_like(h_scr)

        _, dt, _, cs, csT, causal, _ = _ssd_pre(dt_ref[:, :N_HEADS], dtT_ref[...], dtb_ref[...], dtbT_ref[...],
                                                al_ref[...], alT_ref[...])
        ex = _expand_mat()
        dt_full = _hdot(dt, ex, "a")
        cs_full = _hdot(cs, ex, "a")
        cs_last = cs_full[Q - 1:Q, :]
        xs = xs_ref[...]
        xd = xs * dt_full
        e_full = jnp.exp(cs_full)
        dec_full = jnp.exp(cs_last - cs_full)
        cd_full = jnp.exp(cs_last)
        lane_head = lax.broadcasted_iota(jnp.int32, (1, GROUP_W), 1) // HEAD_DIM
        for g in range(N_GROUPS):
            sl = slice(g * GROUP_W, (g + 1) * GROUP_W)
            Bg = B_ref[:, g * D_STATE:(g + 1) * D_STATE].astype(BF16)
            Cg = C_ref[:, g * D_STATE:(g + 1) * D_STATE].astype(BF16)
            CB = _dot_nt(Cg, Bg)
            hg = h_scr[g]
            yoff = _dot_nn(Cg, hg.astype(BF16)) * e_full[:, sl]
            xd_g = xd[:, sl]
            S = _dot_tn(Bg, (xd_g * dec_full[:, sl]).astype(BF16))
            xd_b = xd_g.astype(BF16)
            ydiag = jnp.zeros((Q, GROUP_W), F32)
            for r in range(HEADS_PER_GROUP):
                Lm = _decay_matrix(cs, csT, g * HEADS_PER_GROUP + r, causal)
                Gm = (CB * Lm).astype(BF16)
                ydiag = ydiag + _dot_nn(Gm, jnp.where(lane_head == r, xd_b, jnp.zeros_like(xd_b)))
            hs_ref[0, g] = hg
            h_scr[g] = hg * cd_full[:, sl] + S
            ypre = ydiag + yoff + xs[:, sl] * df_ref[:, sl]
            ypre_ref[:, sl] = ypre
            zg = z_ref[:, sl]
            yz = ypre * zg * _sigmoid(zg)
            rn = lax.rsqrt(jnp.mean(yz * yz, axis=-1, keepdims=True) + EPS)
            y_ref[:, sl] = (yz * rn * ng_ref[:, sl]).astype(BF16)

    return pl.pallas_call(
        body, grid=(nc,), in_specs=_ssd_in_specs(nc, False),
        out_specs=[pl.BlockSpec((CHUNK, D_INNER), lambda c: (c, 0)),
                   pl.BlockSpec((CHUNK, D_INNER), lambda c: (c, 0)),
                   pl.BlockSpec((1, N_GROUPS, D_STATE, GROUP_W), lambda c: (c, 0, 0, 0))],
        out_shape=[jax.ShapeDtypeStruct((T, D_INNER + ATTN_W), BF16), jax.ShapeDtypeStruct((T, D_INNER), F32),
                   jax.ShapeDtypeStruct((nc, N_GROUPS, D_STATE, GROUP_W), F32)],
        scratch_shapes=[pltpu.VMEM((N_GROUPS, D_STATE, GROUP_W), F32)],
        name=name, compiler_params=_cparams(("arbitrary",)),
    )(xbc, xbc, xbc, proj, proj, dtT, dtb, dtbT, alog, alogT, dfull, ng)


def _ssd_bwd(xbc, proj, dtT, dtb, dtbT, alog, alogT, dfull, ng, ypre, hs, dy, name):
    T = xbc.shape[0]
    nc = T // CHUNK
    Q = CHUNK

    def body(xs_ref, B_ref, C_ref, z_ref, dt_ref, dtT_ref, dtb_ref, dtbT_ref, al_ref, alT_ref, df_ref, ng_ref,
             ypre_ref, hs_ref, dy_ref,
             dz_ref, dxbc_ref, ddtb_ref, dal_ref, dD_ref, dng_ref, dh_scr):
        step = pl.program_id(0)

        @pl.when(step == 0)
        def _():
            dh_scr[...] = jnp.zeros_like(dh_scr)
            ddtb_ref[...] = jnp.zeros_like(ddtb_ref)
            dal_ref[...] = jnp.zeros_like(dal_ref)
            dD_ref[...] = jnp.zeros_like(dD_ref)
            dng_ref[...] = jnp.zeros_like(dng_ref)

        xdt, dt, A, cs, csT, causal, triu = _ssd_pre(dt_ref[:, :N_HEADS], dtT_ref[...], dtb_ref[...],
                                                    dtbT_ref[...], al_ref[...], alT_ref[...])
        ex = _expand_mat()
        dt_full = _hdot(dt, ex, "a")
        cs_full = _hdot(cs, ex, "a")
        cs_last = cs_full[Q - 1:Q, :]
        xs = xs_ref[...]
        xd = xs * dt_full
        e_full = jnp.exp(cs_full)
        dec_full = jnp.exp(cs_last - cs_full)
        cd_full = jnp.exp(cs_last)
        lane_head = lax.broadcasted_iota(jnp.int32, (1, GROUP_W), 1) // HEAD_DIM
        is_last = lax.broadcasted_iota(jnp.int32, (Q, 1), 0) == Q - 1
        dcs16 = jnp.zeros((Q, N_HEADS), F32)
        ddtx16 = jnp.zeros((Q, N_HEADS), F32)
        dD16 = jnp.zeros((8, N_HEADS), F32)
        lane16 = lax.broadcasted_iota(jnp.int32, (1, N_HEADS), 1)
        sub16 = lax.broadcasted_iota(jnp.int32, (N_HEADS, 1), 0)
        col_sums = jnp.zeros((N_HEADS, Q), F32)
        for g in range(N_GROUPS):
            sl = slice(g * GROUP_W, (g + 1) * GROUP_W)
            red = _reduce_mat(g)
            ypre_g = ypre_ref[:, sl]
            zg = z_ref[:, sl]
            sg = _sigmoid(zg)
            silu = zg * sg
            yz = ypre_g * silu
            rn = lax.rsqrt(jnp.mean(yz * yz, axis=-1, keepdims=True) + EPS)
            yh = yz * rn
            dy_g = dy_ref[:, sl]
            dng_ref[:, sl] += jnp.sum(dy_g * yh, axis=0, keepdims=True)
            dyh = dy_g * ng_ref[:, sl]
            dyz = rn * (dyh - yh * jnp.mean(dyh * yh, axis=-1, keepdims=True))
            dY = dyz * silu
            dz_ref[:, sl] = (dyz * ypre_g * sg * (1.0 + zg * (1.0 - sg))).astype(BF16)
            xs_g = xs[:, sl]
            xd_g = xd[:, sl]
            dec_g = dec_full[:, sl]
            cd_g = cd_full[:, sl]
            d_g = df_ref[:, sl]
            Bg = B_ref[:, g * D_STATE:(g + 1) * D_STATE].astype(BF16)
            Cg = C_ref[:, g * D_STATE:(g + 1) * D_STATE].astype(BF16)
            CB = _dot_nt(Cg, Bg)
            hg = hs_ref[0, g]
            hgb = hg.astype(BF16)
            yoff = _dot_nn(Cg, hgb) * e_full[:, sl]
            dhn = dh_scr[g]
            dhnb = dhn.astype(BF16)
            dYE = (dY * e_full[:, sl]).astype(BF16)
            dC = _dot_nt(dYE, hgb)
            dh_direct = _dot_tn(Cg, dYE)
            dXdd = _dot_nn(Bg, dhnb)
            dB = _dot_nt((xd_g * dec_g).astype(BF16), dhnb)
            dcd = jnp.sum(dhn * hg, axis=0, keepdims=True)
            dh_scr[g] = dh_direct + cd_g * dhn
            dYb = dY.astype(BF16)
            xd_b = xd_g.astype(BF16)
            dCB = jnp.zeros((Q, Q), F32)
            dXd = dXdd * dec_g
            for r in range(HEADS_PER_GROUP):
                h = g * HEADS_PER_GROUP + r
                Lm = _decay_matrix(cs, csT, h, causal)
                Gf = CB * Lm
                dYr = jnp.where(lane_head == r, dYb, jnp.zeros_like(dYb))
                dG = _dot_nt(dYr, xd_b)
                dCB = dCB + dG * Lm
                dXd = dXd + _dot_tn(Gf.astype(BF16), dYr)
                Mm = dG * Gf
                dcs16 = dcs16 + jnp.where(lane16 == h, jnp.sum(Mm, axis=1, keepdims=True), 0.0)
                col_sums = col_sums + jnp.where(sub16 == h, jnp.sum(Mm, axis=0, keepdims=True), 0.0)
            dCBb = dCB.astype(BF16)
            dC = dC + _dot_nn(dCBb, Bg)
            dB = dB + _dot_tn(dCBb, Cg)
            w_state = dXdd * dec_g * xd_g
            t_last = jnp.sum(w_state, axis=0, keepdims=True) + dcd * cd_g
            dcs_g = dY * yoff - w_state + jnp.where(is_last, t_last, 0.0)
            dcs16 = dcs16 + _hdot(dcs_g, red, "a")
            ddtx16 = ddtx16 + _hdot(dXd * xs_g, red, "a")
            dD16 = dD16 + _hdot(jnp.broadcast_to(jnp.sum(dY * xs_g, axis=0, keepdims=True), (8, GROUP_W)), red, "a")
            dxbc_ref[:, sl] = dXd * dt_full[:, sl] + dY * d_g
            dxbc_ref[:, D_INNER + g * D_STATE:D_INNER + (g + 1) * D_STATE] = dB
            dxbc_ref[:, D_INNER + 512 + g * D_STATE:D_INNER + 512 + (g + 1) * D_STATE] = dC
        eye = (lax.broadcasted_iota(jnp.int32, (N_HEADS, N_HEADS), 0)
               == lax.broadcasted_iota(jnp.int32, (N_HEADS, N_HEADS), 1)).astype(BF16)
        dcs16 = dcs16 - sum(_dot_tn(part, eye) for part in _split3(col_sums))
        da = _hdot(triu, dcs16, "b")
        ddt = da * A + ddtx16
        ddt_raw = ddt * _sigmoid(xdt)
        pr = lax.broadcasted_iota(jnp.int32, (N_HEADS, 128), 0)
        pc = lax.broadcasted_iota(jnp.int32, (N_HEADS, 128), 1)
        dz_ref[:, D_INNER:OFF_DT] = jnp.zeros((Q, OFF_DT - D_INNER), BF16)
        dz_ref[:, OFF_DT:OFF_DT + 128] = _hdot(ddt_raw, (pr == pc).astype(F32), "a").astype(BF16)
        dz_ref[:, OFF_DT + 128:] = jnp.zeros((Q, NP - OFF_DT - 128), BF16)
        ddtb_ref[...] += jnp.sum(ddt_raw, axis=0, keepdims=True)
        dal_ref[...] += jnp.sum(da * dt, axis=0, keepdims=True) * A
        dD_ref[...] += dD16[0:1, :]

    def rc(c):
        return nc - 1 - c

    in_specs = _ssd_in_specs(nc, True) + [
        pl.BlockSpec((CHUNK, D_INNER), lambda c: (rc(c), 0)),
        pl.BlockSpec((1, N_GROUPS, D_STATE, GROUP_W), lambda c: (rc(c), 0, 0, 0)),
        pl.BlockSpec((CHUNK, D_INNER), lambda c: (rc(c), 0)),
    ]
    small = pl.BlockSpec((1, N_HEADS), lambda c: (0, 0))
    return pl.pallas_call(
        body, grid=(nc,), in_specs=in_specs,
        out_specs=[pl.BlockSpec((CHUNK, NP), lambda c: (rc(c), 0)),
                   pl.BlockSpec((CHUNK, CONV_DIM), lambda c: (rc(c), 0)),
                   small, small, small,
                   pl.BlockSpec((1, D_INNER), lambda c: (0, 0))],
        out_shape=[jax.ShapeDtypeStruct((T, NP), BF16), jax.ShapeDtypeStruct((T, CONV_DIM), F32),
                   jax.ShapeDtypeStruct((1, N_HEADS), F32), jax.ShapeDtypeStruct((1, N_HEADS), F32),
                   jax.ShapeDtypeStruct((1, N_HEADS), F32), jax.ShapeDtypeStruct((1, D_INNER), F32)],
        scratch_shapes=[pltpu.VMEM((N_GROUPS, D_STATE, GROUP_W), F32)],
        name=name, compiler_params=_cparams(("arbitrary",)),
    )(xbc, xbc, xbc, proj, proj, dtT, dtb, dtbT, alog, alogT, dfull, ng, ypre, hs, dy)


N_PAIRS = ATTN_W // 128
PAIRS_PER_KV = N_PAIRS // 2
ATTN_SCALE = HEAD_DIM ** -0.5


def _kv_variants(kk):
    lo = lax.broadcasted_iota(jnp.int32, kk.shape, 1) < HEAD_DIM
    zero = jnp.zeros_like(kk)
    k00 = jnp.where(lo, kk, zero)
    k11 = jnp.where(lo, zero, kk)
    k01 = pltpu.roll(k00, HEAD_DIM, axis=1)
    k10 = pltpu.roll(k11, HEAD_DIM, axis=1)
    return [[k00.astype(BF16), k01.astype(BF16)], [k10.astype(BF16), k11.astype(BF16)]]


LOG2E = 1.4426950408889634


def _own_block():
    i = lax.broadcasted_iota(jnp.int32, (WINDOW, WINDOW), 0)
    j = lax.broadcasted_iota(jnp.int32, (WINDOW, WINDOW), 1)
    return j <= i


def _fold(own, a):
    return jnp.where(own, a[:, WINDOW:], a[:, :WINDOW])


def _attn_probs(qp, kvar, own, prev_bias, sk):
    s = _dot_nt(qp, kvar)
    sb = jnp.where(own, s[:, WINDOW:], s[:, :WINDOW] + prev_bias) * (ATTN_SCALE * LOG2E)
    sk2 = sk * LOG2E
    m = jnp.maximum(jnp.max(sb, axis=1, keepdims=True), sk2)
    pe = jnp.exp2(sb - m)
    es = jnp.exp2(sk2 - m)
    den = jnp.sum(pe, axis=1, keepdims=True) + es
    inv = 1.0 / den
    return pe * inv, es * inv


def _unfold(own, a):
    zero = jnp.zeros_like(a)
    return jnp.where(own, zero, a), jnp.where(own, a, zero)


def _sink(sinks, r):
    lane = lax.broadcasted_iota(jnp.int32, sinks.shape, 1)
    return jnp.sum(jnp.where(lane == r, sinks, 0.0), axis=1, keepdims=True)


def _kv_specs():
    return [pl.BlockSpec((WINDOW, KV_W), lambda n: (jnp.maximum(n - 1, 0), OFF_K // KV_W)),
            pl.BlockSpec((WINDOW, KV_W), lambda n: (n, OFF_K // KV_W)),
            pl.BlockSpec((WINDOW, KV_W), lambda n: (jnp.maximum(n - 1, 0), OFF_V // KV_W)),
            pl.BlockSpec((WINDOW, KV_W), lambda n: (n, OFF_V // KV_W))]


def _attn_fwd(proj, sinks, og, ycat, name):
    T = proj.shape[0]
    nb = T // WINDOW

    def body(q_ref, kp_ref, kc_ref, vp_ref, vc_ref, s_ref, og_ref, _, y_ref, o_ref):
        n = pl.program_id(0)
        kv = _kv_variants(jnp.concatenate([kp_ref[...], kc_ref[...]], axis=0))
        vv = _kv_variants(jnp.concatenate([vp_ref[...], vc_ref[...]], axis=0))
        own = _own_block()
        prev_bias = jnp.where(n > 0, 0.0, NEG)
        sinks_v = s_ref[...]
        ssq = jnp.zeros((WINDOW, 1), F32)
        for p in range(N_PAIRS):
            j = p // PAIRS_PER_KV
            qp = q_ref[:, p * 128:(p + 1) * 128].astype(BF16)
            o_pair = jnp.zeros((WINDOW, 128), F32)
            for par in range(2):
                pn, _ = _attn_probs(qp, kv[j][par], own, prev_bias, _sink(sinks_v, 2 * p + par))
                p_prev, p_own = _unfold(own, pn.astype(BF16))
                o_pair = o_pair + _dot_nn(p_prev, vv[j][par][:WINDOW]) + _dot_nn(p_own, vv[j][par][WINDOW:])
            o_ref[:, p * 128:(p + 1) * 128] = o_pair
            ssq = ssq + jnp.sum(o_pair * o_pair, axis=1, keepdims=True)
        rn = lax.rsqrt(ssq * (1.0 / ATTN_W) + EPS)
        y_ref[...] = (o_ref[...] * rn * og_ref[...]).astype(BF16)

    return pl.pallas_call(
        body, grid=(nb,),
        in_specs=[pl.BlockSpec((WINDOW, ATTN_W), lambda n: (n, OFF_Q // ATTN_W)), *_kv_specs(),
                  pl.BlockSpec((1, N_HEADS), lambda n: (0, 0)), pl.BlockSpec((1, ATTN_W), lambda n: (0, 0)), ANY],
        out_specs=[pl.BlockSpec((WINDOW, ATTN_W), lambda n: (n, 1)), pl.BlockSpec((WINDOW, ATTN_W), lambda n: (n, 0))],
        out_shape=[jax.ShapeDtypeStruct(ycat.shape, BF16), jax.ShapeDtypeStruct((T, ATTN_W), F32)],
        input_output_aliases={7: 0}, name=name, compiler_params=_cparams(("parallel",)),
    )(proj, proj, proj, proj, proj, sinks, og, ycat)


def _attn_bwd(proj, sinks, og, o, dy, dproj, name):
    T = proj.shape[0]
    nb = T // WINDOW

    def body(q_ref, kp_ref, kc_ref, vp_ref, vc_ref, s_ref, og_ref, o_ref, dy_ref, _,
             dq_ref, dk_ref, dv_ref, ds_ref, dog_ref, qt_scr, dot_scr, ds_scr, p_scr):
        n = pl.program_id(0)

        @pl.when(n == 0)
        def _():
            dk_ref[...] = jnp.zeros_like(dk_ref)
            dv_ref[...] = jnp.zeros_like(dv_ref)
            ds_ref[...] = jnp.zeros_like(ds_ref)
            dog_ref[...] = jnp.zeros_like(dog_ref)

        kv = _kv_variants(jnp.concatenate([kp_ref[...], kc_ref[...]], axis=0))
        vv = _kv_variants(jnp.concatenate([vp_ref[...], vc_ref[...]], axis=0))
        own = _own_block()
        prev_bias = jnp.where(n > 0, 0.0, NEG)
        sinks_v = s_ref[...]
        of = o_ref[...]
        rn = lax.rsqrt(jnp.mean(of * of, axis=-1, keepdims=True) + EPS)
        oh = of * rn
        dyf = dy_ref[...]
        dog_ref[...] += jnp.sum(dyf * oh, axis=0, keepdims=True)
        doh = dyf * og_ref[...]
        do = rn * (doh - oh * jnp.mean(doh * oh, axis=-1, keepdims=True))
        lane = lax.broadcasted_iota(jnp.int32, (1, 128), 1)
        lane16 = lax.broadcasted_iota(jnp.int32, (1, N_HEADS), 1)
        dsink = jnp.zeros((1, N_HEADS), F32)
        for p in range(N_PAIRS):
            j = p // PAIRS_PER_KV
            q_f = q_ref[:, p * 128:(p + 1) * 128]
            qp = q_f.astype(BF16)
            q_t = q_f.T.astype(BF16)
            do_p = do[:, p * 128:(p + 1) * 128]
            o_p = of[:, p * 128:(p + 1) * 128]
            do_b = do_p.astype(BF16)
            do_t = do_p.T.astype(BF16)
            prod = do_p * o_p
            dq_pair = jnp.zeros((WINDOW, 128), F32)
            for par in range(2):
                r = 2 * p + par
                half = (lane < HEAD_DIM) if par == 0 else (lane >= HEAD_DIM)
                pn, ps = _attn_probs(qp, kv[j][par], own, prev_bias, _sink(sinks_v, r))
                delta = jnp.sum(jnp.where(half, prod, 0.0), axis=1, keepdims=True)
                dP = _fold(own, _dot_nt(do_b, vv[j][par]))
                dS = pn * (dP - delta)
                dsink = dsink + jnp.where(lane16 == r, -jnp.sum(ps * delta, axis=0, keepdims=True), 0.0)
                dS_parts = _unfold(own, (dS * ATTN_SCALE).astype(BF16))
                p_parts = _unfold(own, pn.astype(BF16))
                at = ((p % PAIRS_PER_KV) * 2 + par) * WINDOW
                qt_scr[j, :, at:at + WINDOW] = q_t[par * HEAD_DIM:(par + 1) * HEAD_DIM]
                dot_scr[j, :, at:at + WINDOW] = do_t[par * HEAD_DIM:(par + 1) * HEAD_DIM]
                for blk in range(2):
                    dq_pair = dq_pair + _dot_nn(dS_parts[blk], kv[j][par][blk * WINDOW:(blk + 1) * WINDOW])
                    ds_scr[j, blk, at:at + WINDOW, :] = dS_parts[blk]
                    p_scr[j, blk, at:at + WINDOW, :] = p_parts[blk]
            dq_ref[:, p * 128:(p + 1) * 128] = dq_pair.astype(BF16)
        rows = [pl.multiple_of(jnp.maximum(n - 1, 0) * WINDOW, WINDOW), pl.multiple_of(n * WINDOW, WINDOW)]
        for lhs, rhs, ref in [(qt_scr, ds_scr, dk_ref), (dot_scr, p_scr, dv_ref)]:
            for blk in range(2):
                both_t = jnp.concatenate([_dot_nn(lhs[j], rhs[j, blk]) for j in range(2)], axis=0)
                ref[pl.ds(rows[blk], WINDOW), :] += both_t.T
        ds_ref[...] += dsink

    full_kv = pl.BlockSpec((T, KV_W), lambda n: (0, 0))
    blk = pl.BlockSpec((WINDOW, ATTN_W), lambda n: (n, 0))
    return pl.pallas_call(
        body, grid=(nb,),
        in_specs=[pl.BlockSpec((WINDOW, ATTN_W), lambda n: (n, OFF_Q // ATTN_W)), *_kv_specs(),
                  pl.BlockSpec((1, N_HEADS), lambda n: (0, 0)), pl.BlockSpec((1, ATTN_W), lambda n: (0, 0)),
                  blk, pl.BlockSpec((WINDOW, ATTN_W), lambda n: (n, 1)), ANY],
        out_specs=[pl.BlockSpec((WINDOW, ATTN_W), lambda n: (n, OFF_Q // ATTN_W)), full_kv, full_kv,
                   pl.BlockSpec((1, N_HEADS), lambda n: (0, 0)), pl.BlockSpec((1, ATTN_W), lambda n: (0, 0))],
        out_shape=[jax.ShapeDtypeStruct(dproj.shape, BF16), jax.ShapeDtypeStruct((T, KV_W), F32),
                   jax.ShapeDtypeStruct((T, KV_W), F32), jax.ShapeDtypeStruct((1, N_HEADS), F32),
                   jax.ShapeDtypeStruct((1, ATTN_W), F32)],
        scratch_shapes=[pltpu.VMEM((2, HEAD_DIM, 8 * WINDOW), BF16), pltpu.VMEM((2, HEAD_DIM, 8 * WINDOW), BF16),
                        pltpu.VMEM((2, 2, 8 * WINDOW, WINDOW), BF16), pltpu.VMEM((2, 2, 8 * WINDOW, WINDOW), BF16)],
        input_output_aliases={9: 0}, name=name, compiler_params=_cparams(("arbitrary",)),
    )(proj, proj, proj, proj, proj, sinks, og, o, dy, dproj)


ANY = pl.BlockSpec(memory_space=pl.ANY)


def _coords():
    return lax.axis_index("x"), lax.axis_index("y"), lax.axis_index("c")


HBM = pl.BlockSpec(memory_space=pltpu.HBM)
SEM = pl.BlockSpec(memory_space=pltpu.SEMAPHORE)
EFFECT = pltpu.SideEffectType.DATAFLOW_SIDE_EFFECTING


def _in_hbm(a):
    return pltpu.with_memory_space_constraint(a, pltpu.HBM)


def _remote_start(srcs, lands, plan, n_copies, name, after=None):
    ns, nb = len(srcs), len(srcs) + len(lands)
    n_after = 0 if after is None else 1

    def body(*refs):
        src_refs, land_refs = refs[:ns], refs[ns:nb]
        send_sems, recv_sems = refs[nb + n_after], refs[nb + n_after + 1]
        token = refs[-1]
        x, y, c = _coords()
        for i, (sv, dv, dev) in enumerate(plan(src_refs, land_refs, x, y, c)):
            pltpu.make_async_remote_copy(src_ref=sv, dst_ref=dv, send_sem=send_sems.at[i], recv_sem=recv_sems.at[i],
                                         device_id=dev, device_id_type=MESH).start()
        token[...] = jnp.zeros_like(token)

    bufs = list(srcs) + list(lands)
    outs = pl.pallas_call(
        body, name=name,
        out_shape=(pltpu.SemaphoreType.DMA((n_copies,)), pltpu.SemaphoreType.DMA((n_copies,)),
                   *[pltpu.HBM(b.shape, b.dtype) for b in bufs], jax.ShapeDtypeStruct((8, 128), F32)),
        in_specs=[HBM] * nb + [ANY] * n_after,
        out_specs=(SEM, SEM, *[HBM] * nb, pl.BlockSpec(memory_space=pltpu.VMEM)),
        input_output_aliases={i: 2 + i for i in range(nb)},
        compiler_params=pltpu.CompilerParams(has_side_effects=EFFECT),
    )(*[_in_hbm(b) for b in bufs], *([] if after is None else [after]))
    return outs[0], outs[1], list(outs[2:2 + ns]), list(outs[2 + ns:2 + nb]), outs[-1]


def _remote_wait(started, after, plan, name):
    send_sems, recv_sems, srcs, lands, _ = started
    ns, nb = len(srcs), len(srcs) + len(lands)

    def body(*refs):
        src_refs, land_refs = refs[:ns], refs[ns:nb]
        send_sems, recv_sems = refs[nb], refs[nb + 1]
        x, y, c = _coords()
        for i, (sv, dv, dev) in enumerate(plan(src_refs, land_refs, x, y, c)):
            cp = pltpu.make_async_remote_copy(src_ref=sv, dst_ref=dv, send_sem=send_sems.at[i],
                                              recv_sem=recv_sems.at[i], device_id=dev, device_id_type=MESH)
            cp.wait_send()
            cp.wait_recv()

    bufs = list(srcs) + list(lands)
    outs = pl.pallas_call(
        body, name=name, out_shape=tuple(pltpu.HBM(b.shape, b.dtype) for b in bufs),
        in_specs=[HBM] * nb + [SEM, SEM, ANY], out_specs=tuple([HBM] * nb),
        input_output_aliases={i: i for i in range(nb)},
        compiler_params=pltpu.CompilerParams(has_side_effects=EFFECT),
    )(*bufs, send_sems, recv_sems, after)
    return list(outs[:ns]), list(outs[ns:])


def _pair_plan(src_refs, land_refs, x, y, c):
    plan = []
    for s, l in zip(src_refs, land_refs):
        for q in range(4):
            plan.append((s.at[2 * q + (1 - c)], l.at[q], (x, y, 1 - c)))
    return plan


def _pair4_plan(src_refs, land_refs, x, y, c):
    plan = []
    for s, l in zip(src_refs, land_refs):
        for q in range(4):
            plan.append((s.at[q], l.at[q], (x, y, 1 - c)))
    return plan


def _chips_plan(src_refs, land_refs, x, y, c):
    plan = []
    for s, l in zip(src_refs, land_refs):
        for k, (tx, ty) in enumerate([(1 - x, y), (x, 1 - y), (1 - x, 1 - y)]):
            plan.append((s.at[2 * tx + ty], l.at[k], (tx, ty, c)))
    return plan


def _everyone_plan(src_refs, land_refs, x, y, c):
    me = 4 * x + 2 * y + c
    plan = []
    for s, l in zip(src_refs, land_refs):
        for fx, fy, fc in [(0, 0, 1), (1, 0, 0), (1, 0, 1), (0, 1, 0), (0, 1, 1), (1, 1, 0), (1, 1, 1)]:
            dev = ((1 - x) if fx else x, (1 - y) if fy else y, (1 - c) if fc else c)
            plan.append((s, l.at[me], dev))
    return plan


def _pair_add(g8, r1, csel, tr, name):
    _, R, C = r1.shape
    g4 = g8.reshape(4, 2, R, C)

    def body(c_ref, g_ref, r_ref, o_ref):
        o_ref[...] = (g_ref[...].astype(F32) + r_ref[...].astype(F32)).astype(BF16)

    return pl.pallas_call(
        body,
        grid_spec=pltpu.PrefetchScalarGridSpec(
            num_scalar_prefetch=1, grid=(4, R // tr),
            in_specs=[pl.BlockSpec((None, None, tr, C), lambda q, i, cs: (q, cs[0], i, 0)),
                      pl.BlockSpec((None, tr, C), lambda q, i, cs: (q, i, 0))],
            out_specs=pl.BlockSpec((None, tr, C), lambda q, i, cs: (q, i, 0))),
        out_shape=jax.ShapeDtypeStruct((4, R, C), BF16), name=name,
        compiler_params=_cparams(("parallel", "parallel")),
    )(csel, g4, r1)


def _adamw_math(w, g, m, v):
    m = ADAM_B1 * m + (1.0 - ADAM_B1) * g
    v = ADAM_B2 * v + (1.0 - ADAM_B2) * (g * g)
    m_hat = m / (1.0 - ADAM_B1 ** ADAM_STEP)
    v_hat = v / (1.0 - ADAM_B2 ** ADAM_STEP)
    delta = -ADAM_LR * (m_hat / (jnp.sqrt(v_hat) + ADAM_EPS) + ADAM_WD * w)
    return delta, m, v


def _adamw_big(w, m, v, p4, r3, qsel, tile, name):
    R, C = w.shape
    tr, tc = tile

    def body(q_ref, w_ref, m_ref, v_ref, p_ref, r_ref, g_out, d_out, m_out, v_out):
        g = p_ref[...].astype(F32) + r_ref[0].astype(F32) + r_ref[1].astype(F32) + r_ref[2].astype(F32)
        d, mn, vn = _adamw_math(w_ref[...], g, m_ref[...], v_ref[...])
        g_out[...] = g
        d_out[...] = d
        m_out[...] = mn
        v_out[...] = vn

    blk = pl.BlockSpec((tr, tc), lambda i, j, qs: (i, j))
    return pl.pallas_call(
        body,
        grid_spec=pltpu.PrefetchScalarGridSpec(
            num_scalar_prefetch=1, grid=(R // tr, C // tc),
            in_specs=[blk, blk, blk, pl.BlockSpec((None, tr, tc), lambda i, j, qs: (qs[0], i, j)),
                      pl.BlockSpec((3, tr, tc), lambda i, j, qs: (0, i, j))],
            out_specs=[blk, blk, blk, blk]),
        out_shape=[jax.ShapeDtypeStruct((R, C), F32)] * 4, name=name,
        compiler_params=_cparams(("parallel", "parallel")),
    )(qsel, w, m, v, p4, r3)


def _sum_partials(p4, r3, qsel, tc, name):
    _, R, C = p4.shape

    def body(q_ref, p_ref, r_ref, o_ref):
        o_ref[...] = p_ref[...].astype(F32) + r_ref[0].astype(F32) + r_ref[1].astype(F32) + r_ref[2].astype(F32)

    return pl.pallas_call(
        body,
        grid_spec=pltpu.PrefetchScalarGridSpec(
            num_scalar_prefetch=1, grid=(C // tc,),
            in_specs=[pl.BlockSpec((None, R, tc), lambda j, qs: (qs[0], 0, j)),
                      pl.BlockSpec((3, R, tc), lambda j, qs: (0, 0, j))],
            out_specs=pl.BlockSpec((R, tc), lambda j, qs: (0, j))),
        out_shape=jax.ShapeDtypeStruct((R, C), F32), name=name, compiler_params=_cparams(("parallel",)),
    )(qsel, p4, r3)


def _adamw_tiled(w, g, m, v, tc, name):
    R, C = w.shape

    def body(w_ref, g_ref, m_ref, v_ref, d_out, m_out, v_out):
        d, mn, vn = _adamw_math(w_ref[...], g_ref[...], m_ref[...], v_ref[...])
        d_out[...] = d
        m_out[...] = mn
        v_out[...] = vn

    blk = pl.BlockSpec((R, tc), lambda j: (0, j))
    return pl.pallas_call(
        body, grid=(C // tc,), in_specs=[blk] * 4, out_specs=[blk] * 3,
        out_shape=[jax.ShapeDtypeStruct((R, C), F32)] * 3, name=name, compiler_params=_cparams(("parallel",)),
    )(w, g, m, v)


def _small_sum(parts, name):
    def body(p_ref, o_ref):
        acc = p_ref[0]
        for d in range(1, N_DEV):
            acc = acc + p_ref[d]
        o_ref[...] = acc

    return pl.pallas_call(
        body, out_shape=jax.ShapeDtypeStruct(parts.shape[1:], F32), name=name,
        compiler_params=_cparams(),
    )(parts)


def _adamw_small(w, g, m, v, name):
    def body(w_ref, g_ref, m_ref, v_ref, d_out, m_out, v_out):
        d, mn, vn = _adamw_math(w_ref[...], g_ref[...], m_ref[...], v_ref[...])
        d_out[...] = d
        m_out[...] = mn
        v_out[...] = vn

    return pl.pallas_call(
        body, out_shape=[jax.ShapeDtypeStruct(w.shape, F32)] * 3, name=name, compiler_params=_cparams(),
    )(w, g, m, v)


def _row(*pieces):
    r = jnp.concatenate([p.reshape(1, -1) for p in pieces], axis=1)
    return jnp.pad(r, ((0, 0), (0, D_MODEL - r.shape[1])))


def _pack_small(mix, convb, ssmg, attng, mlpg, fing, convw, dtb, alog, dsk, sinks, extra=None):
    last = [dtb, alog, dsk, sinks] + ([extra] if extra is not None else [])
    rows = [_row(mix), _row(convb), _row(ssmg, attng), _row(mlpg), _row(fing),
            jnp.pad(convw, ((0, 0), (0, D_MODEL - convw.shape[1]))), _row(*last)]
    packed = jnp.concatenate(rows, axis=0)
    return jnp.pad(packed, ((0, SMALL_ROWS - packed.shape[0]), (0, 0)))


def _unpack_small(p, conv_n):
    return dict(
        mix_norm_g=p[0:1, :], conv_b=p[1:2, :], ssm_norm_g=p[2:3, :D_INNER], attn_out_norm_g=p[2:3, D_INNER:],
        mlp_norm_g=p[3:4, :], final_norm_g=p[4, :], conv_w=p[5:9, :conv_n][None],
        dt_bias=p[9:10, 0:16], A_log=p[9:10, 16:32], D_skip=p[9:10, 32:48], attn_sinks=p[9:10, 48:64])


WEIGHT_ORDER = ["mix_norm_g", "w_in", "conv_w", "conv_b", "dt_bias", "A_log", "D_skip", "ssm_norm_g", "attn_sinks",
                "attn_out_norm_g", "w_out", "mlp_norm_g", "w_up", "w_down", "final_norm_g"]


def _to_my_columns(w_nat):
    pad = jnp.zeros((w_nat.shape[0], NP - IN_PROJ), w_nat.dtype)
    return jnp.concatenate([w_nat[:, :NAT_DT], w_nat[:, NAT_DT + N_HEADS:], w_nat[:, NAT_DT:NAT_DT + N_HEADS], pad],
                           axis=1)


PER = IN_PROJ // N_DEV
SUPER_STEP = 544
SUPER = 576


def _natural_rows(g, lo, hi):
    segments = [(0, NAT_DT, 0), (NAT_DT, NAT_DT + N_HEADS, OFF_DT - NAT_DT), (NAT_DT + N_HEADS, IN_PROJ, -N_HEADS),
                (IN_PROJ, NP, 0)]
    pieces = [g[max(lo, a) + shift:min(hi, b) + shift] for a, b, shift in segments if max(lo, a) < min(hi, b)]
    return pieces[0] if len(pieces) == 1 else jnp.concatenate(pieces, axis=0)


def _w_in_from_super_slabs(sup):
    seam = SUPER - SUPER_STEP
    units = []
    for i in range(N_DEV):
        base = SUPER_STEP * i
        units.append((base, base + seam, sup[i, :seam] if i == 0 else sup[i - 1, SUPER_STEP:] + sup[i, :seam]))
        units.append((base + seam, base + SUPER_STEP, sup[i, seam:SUPER_STEP]))
    units.append((SUPER_STEP * N_DEV, SUPER_STEP * N_DEV + seam, sup[N_DEV - 1, SUPER_STEP:]))

    def natural(lo, hi):
        return [rows[max(lo, a) - a:min(hi, b) - a] for a, b, rows in units if max(lo, a) < min(hi, b)]

    pieces = natural(0, NAT_DT) + natural(NAT_DT + N_HEADS, IN_PROJ) + natural(NAT_DT, NAT_DT + N_HEADS)
    return jnp.concatenate(pieces + [jnp.zeros((NP - IN_PROJ, D_MODEL), sup.dtype)], axis=0)


def _to_natural_columns(w_my):
    return jnp.concatenate([w_my[:, :NAT_DT], w_my[:, OFF_DT:OFF_DT + N_HEADS], w_my[:, NAT_DT:OFF_DT]], axis=1)


SLAB = 1024


def _grad_w_up(h2, du, name, sel=None, add=None, after=None):
    T, D = h2.shape
    if sel is None:
        pick, n_slab, pre = (lambda j, *cs: j), N_DEV, None
    else:
        pre, other = sel
        pick, n_slab = (lambda j, cs: 2 * j + ((1 - cs[0]) if other else cs[0])), 4
    o_spec = pl.BlockSpec((None, SLAB, SLAB), lambda i, j, k, *cs: (j, i, 0))
    return _matmul(
        h2, du, mode="tn", grid=(D // SLAB, n_slab, 1),
        a_spec=pl.BlockSpec((T, SLAB), lambda i, j, k, *cs: (0, i)),
        b_spec=pl.BlockSpec((T, SLAB), lambda i, j, k, *cs: (0, pick(j, *cs))),
        out_shapes=[jax.ShapeDtypeStruct((n_slab, D, SLAB), BF16)], out_specs=[o_spec], tile=(SLAB, SLAB), name=name,
        extras=() if add is None else (add,), extra_specs=() if add is None else (o_spec,),
        epilogue=None if add is None else (lambda acc, r: (acc + r.astype(F32),)), after=after, prefetch=pre)[0]


def _grad_w_down(act, dx3b, name, sel=None, add=None, after=None):
    T, D = dx3b.shape
    if sel is None:
        pick, n_slab, pre = (lambda i, *cs: i), N_DEV, None
    else:
        pre, other = sel
        pick, n_slab = (lambda i, cs: 2 * i + ((1 - cs[0]) if other else cs[0])), 4
    o_spec = pl.BlockSpec((None, SLAB, SLAB), lambda i, j, k, *cs: (i, 0, j))
    return _matmul(
        act, dx3b, mode="tn", grid=(n_slab, D // SLAB, 1),
        a_spec=pl.BlockSpec((T, SLAB), lambda i, j, k, *cs: (0, pick(i, *cs))),
        b_spec=pl.BlockSpec((T, SLAB), lambda i, j, k, *cs: (0, j)),
        out_shapes=[jax.ShapeDtypeStruct((n_slab, SLAB, D), BF16)], out_specs=[o_spec], tile=(SLAB, SLAB), name=name,
        extras=() if add is None else (add,), extra_specs=() if add is None else (o_spec,),
        epilogue=None if add is None else (lambda acc, r: (acc + r.astype(F32),)), after=after, prefetch=pre)[0]


class _FixedWeights:
    def __init__(self, w_in_p, w_out_f, w_up_s, w_down_f, conv_w_f):
        self.w = (w_in_p, w_out_f, w_up_s, w_down_f, conv_w_f)
        self.grads = {}

    def mixer_weights(self, after):
        return self.w[0], self.w[4], None

    def out_weight(self, after):
        return self.w[1]

    def up_weight(self, after):
        return self.w[2]

    def down_weight(self, h, after):
        return self.w[3][:, h * (D_MODEL // 2):(h + 1) * (D_MODEL // 2)]

    def mlp_grads(self, h2, du, act, dx3b):
        self.grads.update(w_up=_grad_w_up(h2, du, "grad_w_up"),
                          w_down=_grad_w_down(act, dx3b, "grad_w_down").reshape(D_FF, D_MODEL))
        return None

    def grad_sent(self, tag, after):
        return None

    def out_grad(self, g_out):
        self.grads.update(w_out=g_out)
        return None

    def in_grad(self, g_in):
        self.grads.update(w_in=g_in)
        return None


def _local_step(x, tgt, p, hooks):
    T = x.shape[0]
    D = D_MODEL
    h1 = _rmsnorm_fwd(x, p["mix_norm_g"], "norm_mix")
    w_in_t, conv_w_f, token = hooks.mixer_weights(h1)
    (proj,) = _mm_simple(h1, w_in_t, mode="nt", M=T, N=NP, K=D, tm=min(T, 1024), tn=1536, tk=D, out_dtype=F32,
                         name="in_proj", after=token)
    xbc = _conv_fwd(proj, conv_w_f, p["conv_b"], "conv_fwd")
    dtT = proj[:, OFF_DT:OFF_DT + N_HEADS].T
    dtbT = p["dt_bias"].T
    alogT = p["A_log"].T
    dfull = jnp.repeat(p["D_skip"], HEAD_DIM, axis=1)
    ycat, ypre, hs = _ssd_fwd(xbc, proj, dtT, p["dt_bias"], dtbT, p["A_log"], alogT, dfull, p["ssm_norm_g"],
                              "ssd_fwd")
    ycat, o_att = _attn_fwd(proj, p["attn_sinks"], p["attn_out_norm_g"], ycat, "attn_fwd")
    w_out_f = hooks.out_weight(ycat)
    tm = min(T, 1024)
    (x2,) = _mm_simple(ycat, w_out_f, mode="nn", M=T, N=D, K=D, tm=tm, tn=1024, tk=D, out_dtype=F32, name="out_proj",
                       extras=(x,), epilogue=lambda acc, res: (acc + res,))
    h2 = _rmsnorm_fwd(x2, p["mlp_norm_g"], "norm_mlp")
    w_up_s = hooks.up_weight(h2)
    grid = (T // tm, N_DEV, 1)
    u, act = _matmul(
        h2, w_up_s, mode="nn", grid=grid,
        a_spec=pl.BlockSpec((tm, D), lambda i, j, k: (i, 0)),
        b_spec=pl.BlockSpec((None, D, 1024), lambda i, j, k: (j, 0, 0)),
        out_shapes=[jax.ShapeDtypeStruct((T, D_FF), F32), jax.ShapeDtypeStruct((T, D_FF), BF16)],
        out_specs=[pl.BlockSpec((tm, 1024), lambda i, j, k: (i, j))] * 2, tile=(tm, 1024), name="mlp_up",
        epilogue=lambda acc: (acc, jnp.square(jnp.maximum(acc, 0.0))))
    half = D // 2
    w_down_halves, x3_halves = [], []
    for h in range(2):
        w_down_halves.append(hooks.down_weight(h, act if h == 0 else x3_halves[0]))
        x3_halves.append(_matmul(
            act, w_down_halves[h], mode="nn", grid=(T // tm, 1, D_FF // 2048),
            a_spec=pl.BlockSpec((tm, 2048), lambda i, j, k: (i, k)),
            b_spec=pl.BlockSpec((2048, half), lambda i, j, k: (k, 0)),
            out_shapes=[jax.ShapeDtypeStruct((T, half), F32)],
            out_specs=[pl.BlockSpec((tm, half), lambda i, j, k: (i, 0))], tile=(tm, half), name=f"mlp_down_{h}",
            extras=(x2,), extra_specs=[pl.BlockSpec((tm, half), lambda i, j, k, h=h: (i, h))],
            epilogue=lambda acc, res: (acc + res,))[0])
    loss_part, d_fin, dx3, dx3b = _final_loss(x3_halves, tgt, p["final_norm_g"].reshape(1, D), "loss_head")
    (du,) = _matmul(
        dx3b, tuple(w_down_halves), mode="nt", grid=(T // tm, D_FF // 1024, 1),
        a_spec=pl.BlockSpec((tm, D), lambda i, j, k: (i, 0)),
        b_spec=(pl.BlockSpec((1024, half), lambda i, j, k: (j, 0)),) * 2,
        out_shapes=[jax.ShapeDtypeStruct((T, D_FF), BF16)],
        out_specs=[pl.BlockSpec((tm, 1024), lambda i, j, k: (i, j))], tile=(tm, 1024), name="mlp_down_bwd",
        extras=(u,), extra_specs=[pl.BlockSpec((tm, 1024), lambda i, j, k: (i, j))],
        epilogue=lambda acc, uu: (acc * (2.0 * jnp.maximum(uu, 0.0)),),
        dot_fn=lambda a, b0, b1: _dot_nt(a[:, :half], b0) + _dot_nt(a[:, half:], b1))
    token = hooks.mlp_grads(h2, du, act, dx3b)
    (dh2,) = _matmul(
        du, w_up_s, mode="nt", grid=(T // tm, D // 1024, N_DEV // 2),
        a_spec=pl.BlockSpec((tm, 2048), lambda i, j, k: (i, k)),
        b_spec=pl.BlockSpec((2, 1024, 1024), lambda i, j, k: (k, j, 0)),
        out_shapes=[jax.ShapeDtypeStruct((T, D), F32)],
        out_specs=[pl.BlockSpec((tm, 1024), lambda i, j, k: (i, j))], tile=(tm, 1024), name="mlp_up_bwd",
        after=token, dot_fn=lambda a, b: _dot_nt(a[:, :1024], b[0]) + _dot_nt(a[:, 1024:], b[1]))
    dx2, dx2b, d_mlp = _rmsnorm_bwd(dh2, x2, p["mlp_norm_g"], dx3, "norm_mlp_bwd")
    (g_out,) = _mm_simple(ycat, dx2b, mode="tn", M=D, N=D, K=T, tm=1024, tn=1024, tk=T, out_dtype=BF16,
                          name="grad_w_out")
    token = hooks.out_grad(g_out)
    (dy,) = _mm_simple(dx2b, w_out_f, mode="nt", M=T, N=D, K=D, tm=tm, tn=1024, tk=D, out_dtype=F32,
                       name="out_proj_bwd", after=token)
    token = hooks.grad_sent("out", dy)
    ssm_g = p["ssm_norm_g"] if token is None else p["ssm_norm_g"] + token[0:1, 0:1]
    dproj, dxbc_act, d_dtb, d_alog, d_dskip, d_ssmg = _ssd_bwd(
        xbc, proj, dtT, p["dt_bias"], dtbT, p["A_log"], alogT, dfull, ssm_g, ypre, hs, dy, "ssd_bwd")
    dproj, d_convw, d_convb = _conv_bwd(proj, dxbc_act, conv_w_f, p["conv_b"], dproj, "conv_bwd")
    dproj, dk, dv, d_sinks, d_attng = _attn_bwd(proj, p["attn_sinks"], p["attn_out_norm_g"], o_att, dy, dproj,
                                                "attn_bwd")
    dproj = lax.dynamic_update_slice(dproj, jnp.concatenate([dk, dv], axis=1).astype(BF16), (0, OFF_K))
    (g_in,) = _mm_simple(dproj, h1, mode="tn", M=NP, N=D, K=T, tm=1536, tn=1024, tk=T, out_dtype=BF16,
                         name="grad_w_in")
    token = hooks.in_grad(g_in)
    dx, d_mix = _matmul_rmsnorm_bwd(dproj, w_in_t, x, p["mix_norm_g"], dx2, "in_proj_bwd", after=token,
                                    tm=256, tk=2304)
    token = hooks.grad_sent("in", dx)
    small = _pack_small(d_mix, d_convb, d_ssmg, d_attng, d_mlp, d_fin, d_convw, d_dtb, d_alog, d_dskip, d_sinks,
                        extra=loss_part[:, 0:1])
    return dx, small, token


def _rows_rotated(v, shift, name):
    R, C = v.shape
    tc = 512

    def body(s_ref, v_ref, o_ref):
        o_ref[...] = pltpu.roll(v_ref[...], s_ref[0], axis=0).astype(BF16)

    return pl.pallas_call(
        body,
        grid_spec=pltpu.PrefetchScalarGridSpec(
            num_scalar_prefetch=1, grid=(C // tc,), in_specs=[pl.BlockSpec((R, tc), lambda j, s: (0, j))],
            out_specs=pl.BlockSpec((R, tc), lambda j, s: (0, j))),
        out_shape=jax.ShapeDtypeStruct((R, C), BF16), name=name, compiler_params=_cparams(("parallel",)),
    )(shift, v)


def _landing(own, me):
    zone = lax.empty((N_DEV,) + own.shape, own.dtype)
    return lax.dynamic_update_slice(zone, own[None], (me,) + (0,) * own.ndim)


def _sequencer_gather(owns, split, me, collective_id, name):
    n = len(owns)
    zone_refs = [jax.new_ref(_landing(o, me), memory_space=pltpu.MemorySpace.HBM) for o in owns]
    own_refs = [jax.new_ref(o, memory_space=pltpu.MemorySpace.HBM) for o in owns]
    N_COPIES = 9

    @pl.kernel(mesh=plsc.ScalarSubcoreMesh(axis_name="sequencer", num_cores=1), name=name,
               scratch_types=(pltpu.SemaphoreType.DMA((n, N_COPIES)), pltpu.SemaphoreType.DMA((n, N_COPIES))),
               compiler_params=pltpu.CompilerParams(collective_id=collective_id))
    def launch(send_sems, recv_sems):
        x, y, c = _coords()
        sibling, xn, yn, diag = (x, y, 1 - c), (1 - x, y, c), (x, 1 - y, c), (1 - x, 1 - y, c)
        barrier = pltpu.get_barrier_semaphore()
        for peer in [sibling, xn, yn, diag]:
            pl.semaphore_signal(barrier, inc=1, device_id=peer, device_id_type=MESH)
        pl.semaphore_wait(barrier, 4)

        def block(a, dev, half=None):
            ref = zone_refs[a].at[4 * dev[0] + 2 * dev[1] + dev[2]]
            if half is None:
                return ref
            rows = owns[a].shape[0] // 2
            return ref.at[pl.ds(half * rows, rows)]

        def copy(a, k, src, dst, to):
            return pltpu.make_async_remote_copy(src_ref=src, dst_ref=dst, send_sem=send_sems.at[a, k],
                                                recv_sem=recv_sems.at[a, k], device_id=to, device_id_type=MESH)

        me_dev = (x, y, c)
        sent = []
        first = {}
        for a in range(n):
            for k, peer in enumerate([sibling, xn, yn] + ([] if split[a] else [diag])):
                first[a, k] = copy(a, k, own_refs[a], block(a, me_dev), peer)
                first[a, k].start()
                sent.append(first[a, k])
        from_sibling = []
        for a in range(n):
            first[a, 1].wait_recv()
            sent.append(copy(a, 4, block(a, xn), block(a, xn), sibling))
            if split[a]:
                sent.append(copy(a, 6, block(a, xn, 0), block(a, xn, 0), yn))
            first[a, 2].wait_recv()
            sent.append(copy(a, 5, block(a, yn), block(a, yn), sibling))
            if split[a]:
                sent.append(copy(a, 7, block(a, yn, 1), block(a, yn, 1), xn))
            for cp in sent[-(4 if split[a] else 2):]:
                cp.start()
        for a in range(n):
            if split[a]:
                copy(a, 6, block(a, diag, 0), block(a, diag, 0), yn).wait_recv()
                sent.append(copy(a, 8, block(a, diag, 0), block(a, diag, 0), sibling))
                sent[-1].start()
                copy(a, 7, block(a, diag, 1), block(a, diag, 1), xn).wait_recv()
                sent.append(copy(a, 3, block(a, diag, 1), block(a, diag, 1), sibling))
                sent[-1].start()
            else:
                first[a, 3].wait_recv()
                sent.append(copy(a, 8, block(a, diag), block(a, diag), sibling))
                sent[-1].start()
        for a in range(n):
            first[a, 0].wait_recv()
            copy(a, 4, block(a, xn), block(a, xn), sibling).wait_recv()
            copy(a, 5, block(a, yn), block(a, yn), sibling).wait_recv()
            if split[a]:
                copy(a, 8, block(a, diag, 0), block(a, diag, 0), sibling).wait_recv()
                copy(a, 3, block(a, diag, 1), block(a, diag, 1), sibling).wait_recv()
            else:
                copy(a, 8, block(a, diag), block(a, diag), sibling).wait_recv()
        for cp in sent:
            cp.wait_send()

    launch()
    return zone_refs


class _ShardedWeights:
    def __init__(self, w_in, w_out, conv_w, w_up, w_down, me, csel):
        self.me, self.csel = me, csel
        padded = jnp.pad(jnp.transpose(w_in), ((0, SUPER - PER), (0, 0)))
        own_rows = _rows_rotated(padded, jnp.reshape(2 * me, (1,)).astype(jnp.int32), "w_in_super_slab")
        self.in_ref, self.conv_ref = _sequencer_gather([own_rows, conv_w], [True, False], me, 7,
                                                       "gather_w_in_sequencer")
        (self.out_ref,) = _sequencer_gather([w_out.astype(BF16)], [True], me, 8, "gather_w_out_sequencer")
        (self.up_ref,) = _sequencer_gather([w_up.astype(BF16)], [True], me, 9, "gather_w_up_sequencer")
        down = w_down.astype(BF16)
        self.down_refs = [_sequencer_gather([down[:, h * (D_MODEL // 2):(h + 1) * (D_MODEL // 2)]], [True], me, 10 + h,
                                            f"gather_w_down_{h}_sequencer")[0] for h in range(2)]
        self.reduces = {}
        self.pairs = {}

    def mixer_weights(self, after):
        g_conv = self.conv_ref[...]
        conv_w_f = jnp.concatenate([g_conv[i] for i in range(N_DEV)], axis=1)
        return _w_in_from_super_slabs(self.in_ref[...]), conv_w_f, None

    def out_weight(self, after):
        return self.out_ref[...].reshape(D_MODEL, D_MODEL)

    def up_weight(self, after):
        return self.up_ref[...]

    def down_weight(self, h, after):
        return self.down_refs[h][...].reshape(D_FF, D_MODEL // 2)

    def _chips_start(self, slabs, from_sibling, rows, tag):
        sums = [_pair_add(s, r, self.csel, tr, f"pair_add_{tag}_{i}")
                for i, (s, r, tr) in enumerate(zip(slabs, from_sibling, rows))]
        lands = [lax.empty((3,) + s.shape[1:], s.dtype) for s in sums]
        self.reduces[tag] = _remote_start(sums, lands, _chips_plan, 3 * len(sums), f"reduce_start_{tag}")
        return self.reduces[tag][4]

    def mlp_grads(self, h2, du, act, dx3b):
        def send(part, tag, after):
            st = _remote_start([part], [lax.empty(part.shape, part.dtype)], _pair4_plan, 4,
                               f"reduce_pair_start_{tag}", after=after)
            return st

        def received(st, after, tag):
            return _remote_wait(st, after, _pair4_plan, f"reduce_pair_wait_{tag}")[1][0]

        def to_chips(sums, tag):
            self.reduces[tag] = _remote_start([sums], [lax.empty((3,) + sums.shape[1:], sums.dtype)], _chips_plan, 3,
                                              f"reduce_start_{tag}")
            return self.reduces[tag][4]

        up_send = _grad_w_up(h2, du, "grad_w_up_send", sel=(self.csel, True))
        st_up = send(up_send, "up", None)
        down_send = _grad_w_down(act, dx3b, "grad_w_down_send", sel=(self.csel, True), after=st_up[4])
        st_down = send(down_send, "down", None)
        up_sum = _grad_w_up(h2, du, "grad_w_up_keep", sel=(self.csel, False), add=received(st_up, down_send, "up"),
                            after=st_down[4])
        token = to_chips(up_sum, "up")
        down_sum = _grad_w_down(act, dx3b, "grad_w_down_keep", sel=(self.csel, False),
                                add=received(st_down, up_sum, "down"), after=token)
        return to_chips(down_sum, "down")

    def _pair_start(self, slabs, tag):
        land = lax.empty((4,) + slabs.shape[1:], slabs.dtype)
        self.pairs[tag] = _remote_start([slabs], [land], _pair_plan, 4, f"reduce_pair_start_{tag}")
        return self.pairs[tag][4]

    def grad_sent(self, tag, after):
        slabs, from_sibling = _remote_wait(self.pairs[tag], after, _pair_plan, f"reduce_pair_wait_{tag}")
        return self._chips_start(slabs, from_sibling, [slabs[0].shape[1]], tag)

    def out_grad(self, g_out):
        return self._pair_start(g_out.reshape(N_DEV, D_MODEL // N_DEV, D_MODEL), "out")

    def in_grad(self, g_in):
        return self._pair_start(
            jnp.stack([_natural_rows(g_in, SUPER_STEP * j, SUPER_STEP * j + SUPER) for j in range(N_DEV)]), "in")

    def small_start(self, small, after):
        self.st_small = _remote_start([small], [_landing(small, self.me)], _everyone_plan, N_DEV - 1,
                                      "gather_start_small", after=after)

    def small_end(self, after):
        return _remote_wait(self.st_small, after, _everyone_plan, "gather_small_wait")[1][0]

    def reduce_end(self, tag, after):
        return _remote_wait(self.reduces[tag], after, _chips_plan, f"reduce_wait_{tag}")


def kernel(x, mix_norm_g, w_in, conv_w, conv_b, dt_bias, A_log, D_skip, ssm_norm_g, attn_sinks, attn_out_norm_g, w_out, mlp_norm_g, w_up, w_down, final_norm_g, loss_target, m_mix_norm_g, m_w_in, m_conv_w, m_conv_b, m_dt_bias, m_A_log, m_D_skip, m_ssm_norm_g, m_attn_sinks, m_attn_out_norm_g, m_w_out, m_mlp_norm_g, m_w_up, m_w_down, m_final_norm_g, v_mix_norm_g, v_w_in, v_conv_w, v_conv_b, v_dt_bias, v_A_log, v_D_skip, v_ssm_norm_g, v_attn_sinks, v_attn_out_norm_g, v_w_out, v_mlp_norm_g, v_w_up, v_w_down, v_final_norm_g):
    xi, yi, ci = _coords()
    me = 4 * xi + 2 * yi + ci
    csel = jnp.reshape(ci, (1,)).astype(jnp.int32)
    qsel = jnp.reshape(2 * xi + yi, (1,)).astype(jnp.int32)
    w = dict(mix_norm_g=mix_norm_g, conv_b=conv_b, dt_bias=dt_bias, A_log=A_log, D_skip=D_skip,
             ssm_norm_g=ssm_norm_g, attn_sinks=attn_sinks, attn_out_norm_g=attn_out_norm_g, mlp_norm_g=mlp_norm_g,
             final_norm_g=final_norm_g)
    hooks = _ShardedWeights(w_in[0], w_out[0], conv_w[0], w_up[0], w_down[0], me, csel)
    p = dict(w)
    dx, small, token = _local_step(x[0], loss_target[0], p, hooks)
    hooks.small_start(small, token)
    big = {}
    after = dx
    for name, wt, mt, vt, tile in [
            ("up", w_up, m_w_up, v_w_up, (512, SLAB)), ("down", w_down, m_w_down, v_w_down, (256, D_MODEL)),
            ("out", w_out, m_w_out, v_w_out, (256, D_MODEL))]:
        (chip_sums,), (from_chips,) = hooks.reduce_end(name, after)
        res = _adamw_big(wt[0], mt[0], vt[0], chip_sums, from_chips, qsel, tile, f"adamw_w_{name}")
        big["w_" + name] = tuple(r[None] for r in res)
        after = res[0]
    (chip_sums,), (from_chips,) = hooks.reduce_end("in", after)
    g_super = _sum_partials(chip_sums, from_chips, qsel, 512, "grad_w_in_sum")
    g_in = lax.dynamic_slice(g_super, (2 * me, 0), (PER, D_MODEL))
    res = _adamw_tiled(jnp.transpose(w_in[0]), g_in, jnp.transpose(m_w_in[0]), jnp.transpose(v_w_in[0]), 512,
                       "adamw_w_in")
    big["w_in"] = tuple(jnp.transpose(r)[None] for r in (g_in, *res))
    after = res[0]
    gsum = _small_sum(hooks.small_end(after), "small_sum")
    loss = gsum[9, 64]
    gs = _unpack_small(gsum, CONV_DIM)
    cw = CONV_DIM // N_DEV
    g_conv_shard = lax.dynamic_slice(gsum[5:9, :], (0, me * cw), (CONV_K, cw))

    def pack(s):
        return _pack_small(s["mix_norm_g"], s["conv_b"], s["ssm_norm_g"], s["attn_out_norm_g"], s["mlp_norm_g"],
                           s["final_norm_g"], s["conv_w"][0], s["dt_bias"], s["A_log"], s["D_skip"], s["attn_sinks"])

    wp = pack(dict(w, conv_w=conv_w))
    mp = pack(dict(mix_norm_g=m_mix_norm_g, conv_b=m_conv_b, ssm_norm_g=m_ssm_norm_g,
                   attn_out_norm_g=m_attn_out_norm_g, mlp_norm_g=m_mlp_norm_g, final_norm_g=m_final_norm_g,
                   conv_w=m_conv_w, dt_bias=m_dt_bias, A_log=m_A_log, D_skip=m_D_skip, attn_sinks=m_attn_sinks))
    vp = pack(dict(mix_norm_g=v_mix_norm_g, conv_b=v_conv_b, ssm_norm_g=v_ssm_norm_g,
                   attn_out_norm_g=v_attn_out_norm_g, mlp_norm_g=v_mlp_norm_g, final_norm_g=v_final_norm_g,
                   conv_w=v_conv_w, dt_bias=v_dt_bias, A_log=v_A_log, D_skip=v_D_skip, attn_sinks=v_attn_sinks))
    gp = jnp.concatenate([gsum[0:5], jnp.pad(g_conv_shard, ((0, 0), (0, D_MODEL - cw))), gsum[9:10],
                          jnp.zeros((SMALL_ROWS - 10, D_MODEL), F32)], axis=0)
    dp, mnp, vnp = _adamw_small(wp, gp, mp, vp, "adamw_small")
    grads = dict(gs, conv_w=g_conv_shard[None])
    deltas = _unpack_small(dp, cw)
    new_m = _unpack_small(mnp, cw)
    new_v = _unpack_small(vnp, cw)
    for k, name in enumerate(["w_in", "w_out", "w_up", "w_down"]):
        grads[name], deltas[name], new_m[name], new_v[name] = big[name]
    return (loss, dx[None], *[grads[n] for n in WEIGHT_ORDER], *[deltas[n] for n in WEIGHT_ORDER],
            *[new_m[n] for n in WEIGHT_ORDER], *[new_v[n] for n in WEIGHT_ORDER])
```

```python
import jax
import jax.numpy as jnp
from jax import lax
from jax.experimental import pallas as pl
from jax.experimental.pallas import tpu as pltpu
from jax.experimental.pallas import tpu_sc as plsc

F32 = jnp.float32
BF16 = jnp.bfloat16
MESH = pl.DeviceIdType.MESH

EPS = 1e-5
D_MODEL = 2048
D_INNER = 1024
N_HEADS = 16
HEAD_DIM = 64
N_GROUPS = 4
D_STATE = 128
CHUNK = 128
CONV_K = 4
CONV_DIM = 2048
ATTN_W = 1024
KV_W = 128
WINDOW = 128
D_FF = 8192
IN_PROJ = 4368
N_DEV = 8
NP = 4608
OFF_Z, OFF_X, OFF_B, OFF_C, OFF_Q, OFF_K, OFF_V, OFF_DT = 0, 1024, 2048, 2560, 3072, 4096, 4224, 4352
NAT_DT = 3072

ADAM_LR = 0.001
ADAM_B1 = 0.9
ADAM_B2 = 0.999
ADAM_EPS = 1e-08
ADAM_WD = 0.01
ADAM_STEP = 10

VMEM_LIMIT = 52 * 1024 * 1024
SMALL_ROWS = 16
NEG = -1e30


def _cparams(sem=None):
    return pltpu.CompilerParams(dimension_semantics=sem, vmem_limit_bytes=VMEM_LIMIT)


def _split3(v):
    hi = v.astype(BF16)
    rest = v - hi.astype(F32)
    mid = rest.astype(BF16)
    return hi, mid, (rest - mid.astype(F32)).astype(BF16)


def _hdot(a, b, data):
    if data == "a":
        sel = b.astype(BF16)
        return sum(_dot_nn(part, sel) for part in _split3(a))
    sel = a.astype(BF16)
    return sum(_dot_nn(sel, part) for part in _split3(b))


def _dot_nn(a, b):
    return lax.dot_general(a, b, (((1,), (0,)), ((), ())), preferred_element_type=F32)


def _dot_nt(a, b):
    return lax.dot_general(a, b, (((1,), (1,)), ((), ())), preferred_element_type=F32)


def _dot_tn(a, b):
    return lax.dot_general(a, b, (((0,), (0,)), ((), ())), preferred_element_type=F32)


def _softplus(v):
    return jnp.maximum(v, 0.0) + jnp.log1p(jnp.exp(-jnp.abs(v)))


def _sigmoid(v):
    return 1.0 / (1.0 + jnp.exp(-v))


def _matmul(a, b, *, mode, grid, a_spec, b_spec, out_shapes, out_specs, tile, name,
            extras=(), extra_specs=(), epilogue=None, after=None, dot_fn=None, prefetch=None):
    nk = grid[2]
    n_ex = len(extras)
    n_out = len(out_shapes)
    bs, b_specs = (b, b_spec) if isinstance(b, tuple) else ((b,), (b_spec,))
    n_in = 1 + len(bs)
    dot = dot_fn if dot_fn is not None else {"nn": _dot_nn, "nt": _dot_nt, "tn": _dot_tn}[mode]

    def finish(acc, ex_refs, out_refs):
        res = (acc,) if epilogue is None else epilogue(acc, *[e[...] for e in ex_refs])
        for o, r in zip(out_refs, res):
            o[...] = r.astype(o.dtype)

    def body(*refs):
        ex_refs = refs[n_in:n_in + n_ex]
        out_refs = refs[n_in + n_ex:n_in + n_ex + n_out]
        part = dot(*[r[...].astype(BF16) for r in refs[:n_in]])
        if nk == 1:
            finish(part, ex_refs, out_refs)
        else:
            acc_ref = refs[-1]
            k = pl.program_id(2)

            @pl.when(k == 0)
            def _():
                acc_ref[...] = part

            @pl.when(k > 0)
            def _():
                acc_ref[...] += part

            @pl.when(k == nk - 1)
            def _():
                finish(acc_ref[...], ex_refs, out_refs)

    scratch = [] if nk == 1 else [pltpu.VMEM(tile, F32)]
    n_pre = 0 if prefetch is None else 1
    tok_specs = [] if after is None else [pl.BlockSpec((8, 128), lambda *_: (0, 0))]
    tok_args = [] if after is None else [after]

    def body_with_token(*refs):
        refs = refs[n_pre:]
        body(*refs[:n_in + n_ex], *refs[n_in + n_ex + len(tok_args):])

    in_specs = [a_spec, *b_specs, *extra_specs, *tok_specs]
    params = _cparams(("parallel", "parallel", "arbitrary"))
    if prefetch is None:
        return pl.pallas_call(
            body_with_token, grid=grid, in_specs=in_specs, out_specs=list(out_specs), out_shape=list(out_shapes),
            scratch_shapes=scratch, name=name, compiler_params=params)(a, *bs, *extras, *tok_args)
    return pl.pallas_call(
        body_with_token,
        grid_spec=pltpu.PrefetchScalarGridSpec(num_scalar_prefetch=1, grid=grid, in_specs=in_specs,
                                               out_specs=list(out_specs), scratch_shapes=scratch),
        out_shape=list(out_shapes), name=name, compiler_params=params)(prefetch, a, *bs, *extras, *tok_args)


def _mm_simple(a, b, *, mode, M, N, K, tm, tn, tk, out_dtype, name, extras=(), epilogue=None, n_out=1,
               out_dtypes=None, after=None):
    grid = (M // tm, N // tn, K // tk)
    if mode == "nn":
        a_spec = pl.BlockSpec((tm, tk), lambda i, j, k: (i, k))
        b_spec = pl.BlockSpec((tk, tn), lambda i, j, k: (k, j))
    elif mode == "nt":
        a_spec = pl.BlockSpec((tm, tk), lambda i, j, k: (i, k))
        b_spec = pl.BlockSpec((tn, tk), lambda i, j, k: (j, k))
    else:
        a_spec = pl.BlockSpec((tk, tm), lambda i, j, k: (k, i))
        b_spec = pl.BlockSpec((tk, tn), lambda i, j, k: (k, j))
    o_spec = pl.BlockSpec((tm, tn), lambda i, j, k: (i, j))
    dts = out_dtypes if out_dtypes is not None else [out_dtype] * n_out
    return _matmul(a, b, mode=mode, grid=grid, a_spec=a_spec, b_spec=b_spec,
                   out_shapes=[jax.ShapeDtypeStruct((M, N), d) for d in dts],
                   out_specs=[o_spec] * len(dts), tile=(tm, tn), name=name,
                   extras=extras, extra_specs=[o_spec] * len(extras), epilogue=epilogue, after=after)


ROW_BLOCK = 256


def _rmsnorm_fwd(x, g, name):
    T, D = x.shape

    def body(x_ref, g_ref, o_ref):
        xf = x_ref[...]
        r = lax.rsqrt(jnp.mean(xf * xf, axis=-1, keepdims=True) + EPS)
        o_ref[...] = (xf * r * g_ref[...]).astype(BF16)

    return pl.pallas_call(
        body, grid=(T // ROW_BLOCK,),
        in_specs=[pl.BlockSpec((ROW_BLOCK, D), lambda i: (i, 0)), pl.BlockSpec((1, D), lambda i: (0, 0))],
        out_specs=pl.BlockSpec((ROW_BLOCK, D), lambda i: (i, 0)),
        out_shape=jax.ShapeDtypeStruct((T, D), BF16), name=name, compiler_params=_cparams(("parallel",)),
    )(x, g)


def _rmsnorm_bwd(dh, x, g, dres, name, with_bf16=True):
    T, D = x.shape

    def body(dh_ref, x_ref, g_ref, dres_ref, dx_ref, *rest):
        dg_ref = rest[-1]
        i = pl.program_id(0)
        xf = x_ref[...]
        r = lax.rsqrt(jnp.mean(xf * xf, axis=-1, keepdims=True) + EPS)
        xh = xf * r
        d = dh_ref[...]

        @pl.when(i == 0)
        def _():
            dg_ref[...] = jnp.zeros_like(dg_ref)

        dg_ref[...] += jnp.sum(d * xh, axis=0, keepdims=True)
        dxh = d * g_ref[...]
        dx = r * (dxh - xh * jnp.mean(dxh * xh, axis=-1, keepdims=True)) + dres_ref[...]
        dx_ref[...] = dx
        if with_bf16:
            rest[0][...] = dx.astype(BF16)

    row = pl.BlockSpec((ROW_BLOCK, D), lambda i: (i, 0))
    vec = pl.BlockSpec((1, D), lambda i: (0, 0))
    copies = [(row, jax.ShapeDtypeStruct((T, D), BF16))] if with_bf16 else []
    return pl.pallas_call(
        body, grid=(T // ROW_BLOCK,), in_specs=[row, row, vec, row],
        out_specs=[row, *[c[0] for c in copies], vec],
        out_shape=[jax.ShapeDtypeStruct((T, D), F32), *[c[1] for c in copies], jax.ShapeDtypeStruct((1, D), F32)],
        name=name, compiler_params=_cparams(("arbitrary",)),
    )(dh, x, g, dres)


def _final_loss(x3_halves, tgt, g, name):
    T, D = tgt.shape

    def body(xa_ref, xb_ref, t_ref, g_ref, loss_ref, dg_ref, dx_ref, dxb_ref):
        i = pl.program_id(0)
        xf = jnp.concatenate([xa_ref[...], xb_ref[...]], axis=1)
        r = lax.rsqrt(jnp.mean(xf * xf, axis=-1, keepdims=True) + EPS)
        xh = xf * r
        gg = g_ref[...]
        err = xh * gg - t_ref[...]

        @pl.when(i == 0)
        def _():
            dg_ref[...] = jnp.zeros_like(dg_ref)
            loss_ref[...] = jnp.zeros_like(loss_ref)

        part = jnp.sum(jnp.sum(err * err, axis=-1, keepdims=True), axis=0, keepdims=True) * (0.5 / D)
        loss_ref[...] += jnp.broadcast_to(part, loss_ref.shape)
        dout = err * (1.0 / D)
        dg_ref[...] += jnp.sum(dout * xh, axis=0, keepdims=True)
        dxh = dout * gg
        dx = r * (dxh - xh * jnp.mean(dxh * xh, axis=-1, keepdims=True))
        dx_ref[...] = dx
        dxb_ref[...] = dx.astype(BF16)

    row = pl.BlockSpec((ROW_BLOCK, D), lambda i: (i, 0))
    vec = pl.BlockSpec((1, D), lambda i: (0, 0))
    return pl.pallas_call(
        body, grid=(T // ROW_BLOCK,),
        in_specs=[pl.BlockSpec((ROW_BLOCK, D // 2), lambda i: (i, 0))] * 2 + [row, vec],
        out_specs=[pl.BlockSpec((1, 128), lambda i: (0, 0)), vec, row, row],
        out_shape=[jax.ShapeDtypeStruct((1, 128), F32), jax.ShapeDtypeStruct((1, D), F32),
                   jax.ShapeDtypeStruct((T, D), F32), jax.ShapeDtypeStruct((T, D), BF16)],
        name=name, compiler_params=_cparams(("arbitrary",)),
    )(*x3_halves, tgt, g)


CONV_BLOCK = 256


def _conv_apply(u, w, b):
    row = lax.broadcasted_iota(jnp.int32, u.shape, 0)
    acc = b + w[CONV_K - 1:CONV_K, :] * u
    shifted = []
    for j in range(1, CONV_K):
        uj = jnp.where(row >= j, pltpu.roll(u, j, axis=0), 0.0)
        shifted.append(uj)
        acc = acc + w[CONV_K - 1 - j:CONV_K - j, :] * uj
    return acc, shifted


def _conv_fwd(proj, conv_w, conv_b, name):
    T = proj.shape[0]
    cb0 = OFF_X // CONV_BLOCK

    def body(u_ref, w_ref, b_ref, o_ref):
        c, _ = _conv_apply(u_ref[...], w_ref[...], b_ref[...])
        o_ref[...] = c * _sigmoid(c)

    return pl.pallas_call(
        body, grid=(CONV_DIM // CONV_BLOCK,),
        in_specs=[pl.BlockSpec((T, CONV_BLOCK), lambda j: (0, cb0 + j)),
                  pl.BlockSpec((CONV_K, CONV_BLOCK), lambda j: (0, j)),
                  pl.BlockSpec((1, CONV_BLOCK), lambda j: (0, j))],
        out_specs=pl.BlockSpec((T, CONV_BLOCK), lambda j: (0, j)),
        out_shape=jax.ShapeDtypeStruct((T, CONV_DIM), F32), name=name, compiler_params=_cparams(("parallel",)),
    )(proj, conv_w, conv_b)


def _conv_bwd(proj, dact, conv_w, conv_b, dproj, name):
    T = proj.shape[0]
    cb0 = OFF_X // CONV_BLOCK

    def body(u_ref, d_ref, w_ref, b_ref, _, du_ref, dw_ref, db_ref):
        u = u_ref[...]
        w = w_ref[...]
        c, shifted = _conv_apply(u, w, b_ref[...])
        sg = _sigmoid(c)
        dc = d_ref[...] * sg * (1.0 + c * (1.0 - sg))
        row = lax.broadcasted_iota(jnp.int32, u.shape, 0)
        du = w[CONV_K - 1:CONV_K, :] * dc
        dw_ref[CONV_K - 1:CONV_K, :] = jnp.sum(dc * u, axis=0, keepdims=True)
        for j in range(1, CONV_K):
            dcj = jnp.where(row < T - j, pltpu.roll(dc, T - j, axis=0), 0.0)
            du = du + w[CONV_K - 1 - j:CONV_K - j, :] * dcj
            dw_ref[CONV_K - 1 - j:CONV_K - j, :] = jnp.sum(dc * shifted[j - 1], axis=0, keepdims=True)
        db_ref[...] = jnp.sum(dc, axis=0, keepdims=True)
        du_ref[...] = du.astype(BF16)

    return pl.pallas_call(
        body, grid=(CONV_DIM // CONV_BLOCK,),
        in_specs=[pl.BlockSpec((T, CONV_BLOCK), lambda j: (0, cb0 + j)),
                  pl.BlockSpec((T, CONV_BLOCK), lambda j: (0, j)),
                  pl.BlockSpec((CONV_K, CONV_BLOCK), lambda j: (0, j)),
                  pl.BlockSpec((1, CONV_BLOCK), lambda j: (0, j)), pl.BlockSpec(memory_space=pl.ANY)],
        out_specs=[pl.BlockSpec((T, CONV_BLOCK), lambda j: (0, cb0 + j)),
                   pl.BlockSpec((CONV_K, CONV_BLOCK), lambda j: (0, j)),
                   pl.BlockSpec((1, CONV_BLOCK), lambda j: (0, j))],
        out_shape=[jax.ShapeDtypeStruct(dproj.shape, BF16), jax.ShapeDtypeStruct((CONV_K, CONV_DIM), F32),
                   jax.ShapeDtypeStruct((1, CONV_DIM), F32)],
        input_output_aliases={4: 0}, name=name, compiler_params=_cparams(("parallel",)),
    )(proj, dact, conv_w, conv_b, dproj)


GROUP_W = D_INNER // N_GROUPS
HEADS_PER_GROUP = N_HEADS // N_GROUPS


def _expand_mat():
    h = lax.broadcasted_iota(jnp.int32, (N_HEADS, D_INNER), 0)
    j = lax.broadcasted_iota(jnp.int32, (N_HEADS, D_INNER), 1)
    return (j // HEAD_DIM == h).astype(F32)


def _reduce_mat(g):
    j = lax.broadcasted_iota(jnp.int32, (GROUP_W, N_HEADS), 0)
    h = lax.broadcasted_iota(jnp.int32, (GROUP_W, N_HEADS), 1)
    return (g * HEADS_PER_GROUP + j // HEAD_DIM == h).astype(F32)


def _col16(v, h):
    lane = lax.broadcasted_iota(jnp.int32, v.shape, 1)
    return jnp.sum(jnp.where(lane == h, v, 0.0), axis=1, keepdims=True)


def _ssd_pre(dt_raw, dtT_raw, dtb, dtbT, alog, alogT):
    Q = CHUNK
    xdt = dt_raw + dtb
    dt = _softplus(xdt)
    dtT = _softplus(dtT_raw + dtbT)
    A = -jnp.exp(alog)
    AT = -jnp.exp(alogT)
    row = lax.broadcasted_iota(jnp.int32, (Q, Q), 0)
    col = lax.broadcasted_iota(jnp.int32, (Q, Q), 1)
    tril = (row >= col).astype(F32)
    triu = (row <= col).astype(F32)
    cs = _hdot(tril, dt * A, "b")
    csT = _hdot(dtT * AT, triu, "a")
    return xdt, dt, A, cs, csT, row >= col, triu


def _decay_matrix(cs, csT, h, causal):
    seg = _col16(cs, h) - csT[h:h + 1, :]
    return jnp.where(causal, jnp.exp(jnp.minimum(seg, 0.0)), 0.0)


def _ssd_in_specs(nc, rev):
    def cidx(c):
        return (nc - 1 - c) if rev else c

    return [
        pl.BlockSpec((CHUNK, D_INNER), lambda c: (cidx(c), 0)),
        pl.BlockSpec((CHUNK, 512), lambda c: (cidx(c), 2)),
        pl.BlockSpec((CHUNK, 512), lambda c: (cidx(c), 3)),
        pl.BlockSpec((CHUNK, D_INNER), lambda c: (cidx(c), 0)),
        pl.BlockSpec((CHUNK, 128), lambda c: (cidx(c), OFF_DT // 128)),
        pl.BlockSpec((N_HEADS, CHUNK), lambda c: (0, cidx(c))),
        pl.BlockSpec((1, N_HEADS), lambda c: (0, 0)),
        pl.BlockSpec((N_HEADS, 1), lambda c: (0, 0)),
        pl.BlockSpec((1, N_HEADS), lambda c: (0, 0)),
        pl.BlockSpec((N_HEADS, 1), lambda c: (0, 0)),
        pl.BlockSpec((1, D_INNER), lambda c: (0, 0)),
        pl.BlockSpec((1, D_INNER), lambda c: (0, 0)),
    ]


def _ssd_fwd(xbc, proj, dtT, dtb, dtbT, alog, alogT, dfull, ng, name):
    T = xbc.shape[0]
    nc = T // CHUNK
    Q = CHUNK

    def body(xs_ref, B_ref, C_ref, z_ref, dt_ref, dtT_ref, dtb_ref, dtbT_ref, al_ref, alT_ref, df_ref, ng_ref,
             y_ref, ypre_ref, hs_ref, h_scr):
        c = pl.program_id(0)

        @pl.when(c == 0)
        def _():
            h_scr[...] = jnp.zeros_like(h_scr)

        _, dt, _, cs, csT, causal, _ = _ssd_pre(dt_ref[:, :N_HEADS], dtT_ref[...], dtb_ref[...], dtbT_ref[...],
                                                al_ref[...], alT_ref[...])
        ex = _expand_mat()
        dt_full = _hdot(dt, ex, "a")
        cs_full = _hdot(cs, ex, "a")
        cs_last = cs_full[Q - 1:Q, :]
        xs = xs_ref[...]
        xd = xs * dt_full
        e_full = jnp.exp(cs_full)
        dec_full = jnp.exp(cs_last - cs_full)
        cd_full = jnp.exp(cs_last)
        lane_head = lax.broadcasted_iota(jnp.int32, (1, GROUP_W), 1) // HEAD_DIM
        for g in range(N_GROUPS):
            sl = slice(g * GROUP_W, (g + 1) * GROUP_W)
            Bg = B_ref[:, g * D_STATE:(g + 1) * D_STATE].astype(BF16)
            Cg = C_ref[:, g * D_STATE:(g + 1) * D_STATE].astype(BF16)
            CB = _dot_nt(Cg, Bg)
            hg = h_scr[g]
            yoff = _dot_nn(Cg, hg.astype(BF16)) * e_full[:, sl]
            xd_g = xd[:, sl]
            S = _dot_tn(Bg, (xd_g * dec_full[:, sl]).astype(BF16))
            xd_b = xd_g.astype(BF16)
            ydiag = jnp.zeros((Q, GROUP_W), F32)
            for r in range(HEADS_PER_GROUP):
                Lm = _decay_matrix(cs, csT, g * HEADS_PER_GROUP + r, causal)
                Gm = (CB * Lm).astype(BF16)
                ydiag = ydiag + _dot_nn(Gm, jnp.where(lane_head == r, xd_b, jnp.zeros_like(xd_b)))
            hs_ref[0, g] = hg
            h_scr[g] = hg * cd_full[:, sl] + S
            ypre = ydiag + yoff + xs[:, sl] * df_ref[:, sl]
            ypre_ref[:, sl] = ypre
            zg = z_ref[:, sl]
            yz = ypre * zg * _sigmoid(zg)
            rn = lax.rsqrt(jnp.mean(yz * yz, axis=-1, keepdims=True) + EPS)
            y_ref[:, sl] = (yz * rn * ng_ref[:, sl]).astype(BF16)

    return pl.pallas_call(
        body, grid=(nc,), in_specs=_ssd_in_specs(nc, False),
        out_specs=[pl.BlockSpec((CHUNK, D_INNER), lambda c: (c, 0)),
                   pl.BlockSpec((CHUNK, D_INNER), lambda c: (c, 0)),
                   pl.BlockSpec((1, N_GROUPS, D_STATE, GROUP_W), lambda c: (c, 0, 0, 0))],
        out_shape=[jax.ShapeDtypeStruct((T, D_INNER + ATTN_W), BF16), jax.ShapeDtypeStruct((T, D_INNER), F32),
                   jax.ShapeDtypeStruct((nc, N_GROUPS, D_STATE, GROUP_W), F32)],
        scratch_shapes=[pltpu.VMEM((N_GROUPS, D_STATE, GROUP_W), F32)],
        name=name, compiler_params=_cparams(("arbitrary",)),
    )(xbc, xbc, xbc, proj, proj, dtT, dtb, dtbT, alog, alogT, dfull, ng)


def _ssd_bwd(xbc, proj, dtT, dtb, dtbT, alog, alogT, dfull, ng, ypre, hs, dy, name):
    T = xbc.shape[0]
    nc = T // CHUNK
    Q = CHUNK

    def body(xs_ref, B_ref, C_ref, z_ref, dt_ref, dtT_ref, dtb_ref, dtbT_ref, al_ref, alT_ref, df_ref, ng_ref,
             ypre_ref, hs_ref, dy_ref,
             dz_ref, dxbc_ref, ddtb_ref, dal_ref, dD_ref, dng_ref, dh_scr):
        step = pl.program_id(0)

        @pl.when(step == 0)
        def _():
            dh_scr[...] = jnp.zeros_like(dh_scr)
            ddtb_ref[...] = jnp.zeros_like(ddtb_ref)
            dal_ref[...] = jnp.zeros_like(dal_ref)
            dD_ref[...] = jnp.zeros_like(dD_ref)
            dng_ref[...] = jnp.zeros_like(dng_ref)

        xdt, dt, A, cs, csT, causal, triu = _ssd_pre(dt_ref[:, :N_HEADS], dtT_ref[...], dtb_ref[...],
                                                    dtbT_ref[...], al_ref[...], alT_ref[...])
        ex = _expand_mat()
        dt_full = _hdot(dt, ex, "a")
        cs_full = _hdot(cs, ex, "a")
        cs_last = cs_full[Q - 1:Q, :]
        xs = xs_ref[...]
        xd = xs * dt_full
        e_full = jnp.exp(cs_full)
        dec_full = jnp.exp(cs_last - cs_full)
        cd_full = jnp.exp(cs_last)
        lane_head = lax.broadcasted_iota(jnp.int32, (1, GROUP_W), 1) // HEAD_DIM
        is_last = lax.broadcasted_iota(jnp.int32, (Q, 1), 0) == Q - 1
        dcs16 = jnp.zeros((Q, N_HEADS), F32)
        ddtx16 = jnp.zeros((Q, N_HEADS), F32)
        dD16 = jnp.zeros((8, N_HEADS), F32)
        lane16 = lax.broadcasted_iota(jnp.int32, (1, N_HEADS), 1)
        sub16 = lax.broadcasted_iota(jnp.int32, (N_HEADS, 1), 0)
        col_sums = jnp.zeros((N_HEADS, Q), F32)
        for g in range(N_GROUPS):
            sl = slice(g * GROUP_W, (g + 1) * GROUP_W)
            red = _reduce_mat(g)
            ypre_g = ypre_ref[:, sl]
            zg = z_ref[:, sl]
            sg = _sigmoid(zg)
            silu = zg * sg
            yz = ypre_g * silu
            rn = lax.rsqrt(jnp.mean(yz * yz, axis=-1, keepdims=True) + EPS)
            yh = yz * rn
            dy_g = dy_ref[:, sl]
            dng_ref[:, sl] += jnp.sum(dy_g * yh, axis=0, keepdims=True)
            dyh = dy_g * ng_ref[:, sl]
            dyz = rn * (dyh - yh * jnp.mean(dyh * yh, axis=-1, keepdims=True))
            dY = dyz * silu
            dz_ref[:, sl] = (dyz * ypre_g * sg * (1.0 + zg * (1.0 - sg))).astype(BF16)
            xs_g = xs[:, sl]
            xd_g = xd[:, sl]
            dec_g = dec_full[:, sl]
            cd_g = cd_full[:, sl]
            d_g = df_ref[:, sl]
            Bg = B_ref[:, g * D_STATE:(g + 1) * D_STATE].astype(BF16)
            Cg = C_ref[:, g * D_STATE:(g + 1) * D_STATE].astype(BF16)
            CB = _dot_nt(Cg, Bg)
            hg = hs_ref[0, g]
            hgb = hg.astype(BF16)
            yoff = _dot_nn(Cg, hgb) * e_full[:, sl]
            dhn = dh_scr[g]
            dhnb = dhn.astype(BF16)
            dYE = (dY * e_full[:, sl]).astype(BF16)
            dC = _dot_nt(dYE, hgb)
            dh_direct = _dot_tn(Cg, dYE)
            dXdd = _dot_nn(Bg, dhnb)
            dB = _dot_nt((xd_g * dec_g).astype(BF16), dhnb)
            dcd = jnp.sum(dhn * hg, axis=0, keepdims=True)
            dh_scr[g] = dh_direct + cd_g * dhn
            dYb = dY.astype(BF16)
            xd_b = xd_g.astype(BF16)
            dCB = jnp.zeros((Q, Q), F32)
            dXd = dXdd * dec_g
            for r in range(HEADS_PER_GROUP):
                h = g * HEADS_PER_GROUP + r
                Lm = _decay_matrix(cs, csT, h, causal)
                Gf = CB * Lm
                dYr = jnp.where(lane_head == r, dYb, jnp.zeros_like(dYb))
                dG = _dot_nt(dYr, xd_b)
                dCB = dCB + dG * Lm
                dXd = dXd + _dot_tn(Gf.astype(BF16), dYr)
                Mm = dG * Gf
                dcs16 = dcs16 + jnp.where(lane16 == h, jnp.sum(Mm, axis=1, keepdims=True), 0.0)
                col_sums = col_sums + jnp.where(sub16 == h, jnp.sum(Mm, axis=0, keepdims=True), 0.0)
            dCBb = dCB.astype(BF16)
            dC = dC + _dot_nn(dCBb, Bg)
            dB = dB + _dot_tn(dCBb, Cg)
            w_state = dXdd * dec_g * xd_g
            t_last = jnp.sum(w_state, axis=0, keepdims=True) + dcd * cd_g
            dcs_g = dY * yoff - w_state + jnp.where(is_last, t_last, 0.0)
            dcs16 = dcs16 + _hdot(dcs_g, red, "a")
            ddtx16 = ddtx16 + _hdot(dXd * xs_g, red, "a")
            dD16 = dD16 + _hdot(jnp.broadcast_to(jnp.sum(dY * xs_g, axis=0, keepdims=True), (8, GROUP_W)), red, "a")
            dxbc_ref[:, sl] = dXd * dt_full[:, sl] + dY * d_g
            dxbc_ref[:, D_INNER + g * D_STATE:D_INNER + (g + 1) * D_STATE] = dB
            dxbc_ref[:, D_INNER + 512 + g * D_STATE:D_INNER + 512 + (g + 1) * D_STATE] = dC
        eye = (lax.broadcasted_iota(jnp.int32, (N_HEADS, N_HEADS), 0)
               == lax.broadcasted_iota(jnp.int32, (N_HEADS, N_HEADS), 1)).astype(BF16)
        dcs16 = dcs16 - sum(_dot_tn(part, eye) for part in _split3(col_sums))
        da = _hdot(triu, dcs16, "b")
        ddt = da * A + ddtx16
        ddt_raw = ddt * _sigmoid(xdt)
        pr = lax.broadcasted_iota(jnp.int32, (N_HEADS, 128), 0)
        pc = lax.broadcasted_iota(jnp.int32, (N_HEADS, 128), 1)
        dz_ref[:, D_INNER:OFF_DT] = jnp.zeros((Q, OFF_DT - D_INNER), BF16)
        dz_ref[:, OFF_DT:OFF_DT + 128] = _hdot(ddt_raw, (pr == pc).astype(F32), "a").astype(BF16)
        dz_ref[:, OFF_DT + 128:] = jnp.zeros((Q, NP - OFF_DT - 128), BF16)
        ddtb_ref[...] += jnp.sum(ddt_raw, axis=0, keepdims=True)
        dal_ref[...] += jnp.sum(da * dt, axis=0, keepdims=True) * A
        dD_ref[...] += dD16[0:1, :]

    def rc(c):
        return nc - 1 - c

    in_specs = _ssd_in_specs(nc, True) + [
        pl.BlockSpec((CHUNK, D_INNER), lambda c: (rc(c), 0)),
        pl.BlockSpec((1, N_GROUPS, D_STATE, GROUP_W), lambda c: (rc(c), 0, 0, 0)),
        pl.BlockSpec((CHUNK, D_INNER), lambda c: (rc(c), 0)),
    ]
    small = pl.BlockSpec((1, N_HEADS), lambda c: (0, 0))
    return pl.pallas_call(
        body, grid=(nc,), in_specs=in_specs,
        out_specs=[pl.BlockSpec((CHUNK, NP), lambda c: (rc(c), 0)),
                   pl.BlockSpec((CHUNK, CONV_DIM), lambda c: (rc(c), 0)),
                   small, small, small,
                   pl.BlockSpec((1, D_INNER), lambda c: (0, 0))],
        out_shape=[jax.ShapeDtypeStruct((T, NP), BF16), jax.ShapeDtypeStruct((T, CONV_DIM), F32),
                   jax.ShapeDtypeStruct((1, N_HEADS), F32), jax.ShapeDtypeStruct((1, N_HEADS), F32),
                   jax.ShapeDtypeStruct((1, N_HEADS), F32), jax.ShapeDtypeStruct((1, D_INNER), F32)],
        scratch_shapes=[pltpu.VMEM((N_GROUPS, D_STATE, GROUP_W), F32)],
        name=name, compiler_params=_cparams(("arbitrary",)),
    )(xbc, xbc, xbc, proj, proj, dtT, dtb, dtbT, alog, alogT, dfull, ng, ypre, hs, dy)


N_PAIRS = ATTN_W // 128
PAIRS_PER_KV = N_PAIRS // 2
ATTN_SCALE = HEAD_DIM ** -0.5


def _kv_variants(kk):
    lo = lax.broadcasted_iota(jnp.int32, kk.shape, 1) < HEAD_DIM
    zero = jnp.zeros_like(kk)
    k00 = jnp.where(lo, kk, zero)
    k11 = jnp.where(lo, zero, kk)
    k01 = pltpu.roll(k00, HEAD_DIM, axis=1)
    k10 = pltpu.roll(k11, HEAD_DIM, axis=1)
    return [[k00.astype(BF16), k01.astype(BF16)], [k10.astype(BF16), k11.astype(BF16)]]


LOG2E = 1.4426950408889634


def _own_block():
    i = lax.broadcasted_iota(jnp.int32, (WINDOW, WINDOW), 0)
    j = lax.broadcasted_iota(jnp.int32, (WINDOW, WINDOW), 1)
    return j <= i


def _fold(own, a):
    return jnp.where(own, a[:, WINDOW:], a[:, :WINDOW])


def _attn_probs(qp, kvar, own, prev_bias, sk):
    s = _dot_nt(qp, kvar)
    sb = jnp.where(own, s[:, WINDOW:], s[:, :WINDOW] + prev_bias) * (ATTN_SCALE * LOG2E)
    sk2 = sk * LOG2E
    m = jnp.maximum(jnp.max(sb, axis=1, keepdims=True), sk2)
    pe = jnp.exp2(sb - m)
    es = jnp.exp2(sk2 - m)
    den = jnp.sum(pe, axis=1, keepdims=True) + es
    inv = 1.0 / den
    return pe * inv, es * inv


def _unfold(own, a):
    zero = jnp.zeros_like(a)
    return jnp.where(own, zero, a), jnp.where(own, a, zero)


def _sink(sinks, r):
    lane = lax.broadcasted_iota(jnp.int32, sinks.shape, 1)
    return jnp.sum(jnp.where(lane == r, sinks, 0.0), axis=1, keepdims=True)


def _kv_specs():
    return [pl.BlockSpec((WINDOW, KV_W), lambda n: (jnp.maximum(n - 1, 0), OFF_K // KV_W)),
            pl.BlockSpec((WINDOW, KV_W), lambda n: (n, OFF_K // KV_W)),
            pl.BlockSpec((WINDOW, KV_W), lambda n: (jnp.maximum(n - 1, 0), OFF_V // KV_W)),
            pl.BlockSpec((WINDOW, KV_W), lambda n: (n, OFF_V // KV_W))]


def _attn_fwd(proj, sinks, og, ycat, name):
    T = proj.shape[0]
    nb = T // WINDOW

    def body(q_ref, kp_ref, kc_ref, vp_ref, vc_ref, s_ref, og_ref, _, y_ref, o_ref):
        n = pl.program_id(0)
        kv = _kv_variants(jnp.concatenate([kp_ref[...], kc_ref[...]], axis=0))
        vv = _kv_variants(jnp.concatenate([vp_ref[...], vc_ref[...]], axis=0))
        own = _own_block()
        prev_bias = jnp.where(n > 0, 0.0, NEG)
        sinks_v = s_ref[...]
        ssq = jnp.zeros((WINDOW, 1), F32)
        for p in range(N_PAIRS):
            j = p // PAIRS_PER_KV
            qp = q_ref[:, p * 128:(p + 1) * 128].astype(BF16)
            o_pair = jnp.zeros((WINDOW, 128), F32)
            for par in range(2):
                pn, _ = _attn_probs(qp, kv[j][par], own, prev_bias, _sink(sinks_v, 2 * p + par))
                p_prev, p_own = _unfold(own, pn.astype(BF16))
                o_pair = o_pair + _dot_nn(p_prev, vv[j][par][:WINDOW]) + _dot_nn(p_own, vv[j][par][WINDOW:])
            o_ref[:, p * 128:(p + 1) * 128] = o_pair
            ssq = ssq + jnp.sum(o_pair * o_pair, axis=1, keepdims=True)
        rn = lax.rsqrt(ssq * (1.0 / ATTN_W) + EPS)
        y_ref[...] = (o_ref[...] * rn * og_ref[...]).astype(BF16)

    return pl.pallas_call(
        body, grid=(nb,),
        in_specs=[pl.BlockSpec((WINDOW, ATTN_W), lambda n: (n, OFF_Q // ATTN_W)), *_kv_specs(),
                  pl.BlockSpec((1, N_HEADS), lambda n: (0, 0)), pl.BlockSpec((1, ATTN_W), lambda n: (0, 0)), ANY],
        out_specs=[pl.BlockSpec((WINDOW, ATTN_W), lambda n: (n, 1)), pl.BlockSpec((WINDOW, ATTN_W), lambda n: (n, 0))],
        out_shape=[jax.ShapeDtypeStruct(ycat.shape, BF16), jax.ShapeDtypeStruct((T, ATTN_W), F32)],
        input_output_aliases={7: 0}, name=name, compiler_params=_cparams(("parallel",)),
    )(proj, proj, proj, proj, proj, sinks, og, ycat)


def _attn_bwd(proj, sinks, og, o, dy, dproj, name):
    T = proj.shape[0]
    nb = T // WINDOW

    def body(q_ref, kp_ref, kc_ref, vp_ref, vc_ref, s_ref, og_ref, o_ref, dy_ref, _,
             dq_ref, dk_ref, dv_ref, ds_ref, dog_ref, qt_scr, dot_scr, ds_scr, p_scr):
        n = pl.program_id(0)

        @pl.when(n == 0)
        def _():
            dk_ref[...] = jnp.zeros_like(dk_ref)
            dv_ref[...] = jnp.zeros_like(dv_ref)
            ds_ref[...] = jnp.zeros_like(ds_ref)
            dog_ref[...] = jnp.zeros_like(dog_ref)

        kv = _kv_variants(jnp.concatenate([kp_ref[...], kc_ref[...]], axis=0))
        vv = _kv_variants(jnp.concatenate([vp_ref[...], vc_ref[...]], axis=0))
        own = _own_block()
        prev_bias = jnp.where(n > 0, 0.0, NEG)
        sinks_v = s_ref[...]
        of = o_ref[...]
        rn = lax.rsqrt(jnp.mean(of * of, axis=-1, keepdims=True) + EPS)
        oh = of * rn
        dyf = dy_ref[...]
        dog_ref[...] += jnp.sum(dyf * oh, axis=0, keepdims=True)
        doh = dyf * og_ref[...]
        do = rn * (doh - oh * jnp.mean(doh * oh, axis=-1, keepdims=True))
        lane = lax.broadcasted_iota(jnp.int32, (1, 128), 1)
        lane16 = lax.broadcasted_iota(jnp.int32, (1, N_HEADS), 1)
        dsink = jnp.zeros((1, N_HEADS), F32)
        for p in range(N_PAIRS):
            j = p // PAIRS_PER_KV
            q_f = q_ref[:, p * 128:(p + 1) * 128]
            qp = q_f.astype(BF16)
            q_t = q_f.T.astype(BF16)
            do_p = do[:, p * 128:(p + 1) * 128]
            o_p = of[:, p * 128:(p + 1) * 128]
            do_b = do_p.astype(BF16)
            do_t = do_p.T.astype(BF16)
            prod = do_p * o_p
            dq_pair = jnp.zeros((WINDOW, 128), F32)
            for par in range(2):
                r = 2 * p + par
                half = (lane < HEAD_DIM) if par == 0 else (lane >= HEAD_DIM)
                pn, ps = _attn_probs(qp, kv[j][par], own, prev_bias, _sink(sinks_v, r))
                delta = jnp.sum(jnp.where(half, prod, 0.0), axis=1, keepdims=True)
                dP = _fold(own, _dot_nt(do_b, vv[j][par]))
                dS = pn * (dP - delta)
                dsink = dsink + jnp.where(lane16 == r, -jnp.sum(ps * delta, axis=0, keepdims=True), 0.0)
                dS_parts = _unfold(own, (dS * ATTN_SCALE).astype(BF16))
                p_parts = _unfold(own, pn.astype(BF16))
                at = ((p % PAIRS_PER_KV) * 2 + par) * WINDOW
                qt_scr[j, :, at:at + WINDOW] = q_t[par * HEAD_DIM:(par + 1) * HEAD_DIM]
                dot_scr[j, :, at:at + WINDOW] = do_t[par * HEAD_DIM:(par + 1) * HEAD_DIM]
                for blk in range(2):
                    dq_pair = dq_pair + _dot_nn(dS_parts[blk], kv[j][par][blk * WINDOW:(blk + 1) * WINDOW])
                    ds_scr[j, blk, at:at + WINDOW, :] = dS_parts[blk]
                    p_scr[j, blk, at:at + WINDOW, :] = p_parts[blk]
            dq_ref[:, p * 128:(p + 1) * 128] = dq_pair.astype(BF16)
        rows = [pl.multiple_of(jnp.maximum(n - 1, 0) * WINDOW, WINDOW), pl.multiple_of(n * WINDOW, WINDOW)]
        for lhs, rhs, ref in [(qt_scr, ds_scr, dk_ref), (dot_scr, p_scr, dv_ref)]:
            for blk in range(2):
                both_t = jnp.concatenate([_dot_nn(lhs[j], rhs[j, blk]) for j in range(2)], axis=0)
                ref[pl.ds(rows[blk], WINDOW), :] += both_t.T
        ds_ref[...] += dsink

    full_kv = pl.BlockSpec((T, KV_W), lambda n: (0, 0))
    blk = pl.BlockSpec((WINDOW, ATTN_W), lambda n: (n, 0))
    return pl.pallas_call(
        body, grid=(nb,),
        in_specs=[pl.BlockSpec((WINDOW, ATTN_W), lambda n: (n, OFF_Q // ATTN_W)), *_kv_specs(),
                  pl.BlockSpec((1, N_HEADS), lambda n: (0, 0)), pl.BlockSpec((1, ATTN_W), lambda n: (0, 0)),
                  blk, pl.BlockSpec((WINDOW, ATTN_W), lambda n: (n, 1)), ANY],
        out_specs=[pl.BlockSpec((WINDOW, ATTN_W), lambda n: (n, OFF_Q // ATTN_W)), full_kv, full_kv,
                   pl.BlockSpec((1, N_HEADS), lambda n: (0, 0)), pl.BlockSpec((1, ATTN_W), lambda n: (0, 0))],
        out_shape=[jax.ShapeDtypeStruct(dproj.shape, BF16), jax.ShapeDtypeStruct((T, KV_W), F32),
                   jax.ShapeDtypeStruct((T, KV_W), F32), jax.ShapeDtypeStruct((1, N_HEADS), F32),
                   jax.ShapeDtypeStruct((1, ATTN_W), F32)],
        scratch_shapes=[pltpu.VMEM((2, HEAD_DIM, 8 * WINDOW), BF16), pltpu.VMEM((2, HEAD_DIM, 8 * WINDOW), BF16),
                        pltpu.VMEM((2, 2, 8 * WINDOW, WINDOW), BF16), pltpu.VMEM((2, 2, 8 * WINDOW, WINDOW), BF16)],
        input_output_aliases={9: 0}, name=name, compiler_params=_cparams(("arbitrary",)),
    )(proj, proj, proj, proj, proj, sinks, og, o, dy, dproj)


ANY = pl.BlockSpec(memory_space=pl.ANY)


def _coords():
    return lax.axis_index("x"), lax.axis_index("y"), lax.axis_index("c")


HBM = pl.BlockSpec(memory_space=pltpu.HBM)
SEM = pl.BlockSpec(memory_space=pltpu.SEMAPHORE)
EFFECT = pltpu.SideEffectType.DATAFLOW_SIDE_EFFECTING


def _in_hbm(a):
    return pltpu.with_memory_space_constraint(a, pltpu.HBM)


def _remote_start(srcs, lands, plan, n_copies, name, after=None):
    ns, nb = len(srcs), len(srcs) + len(lands)
    n_after = 0 if after is None else 1

    def body(*refs):
        src_refs, land_refs = refs[:ns], refs[ns:nb]
        send_sems, recv_sems = refs[nb + n_after], refs[nb + n_after + 1]
        token = refs[-1]
        x, y, c = _coords()
        for i, (sv, dv, dev) in enumerate(plan(src_refs, land_refs, x, y, c)):
            pltpu.make_async_remote_copy(src_ref=sv, dst_ref=dv, send_sem=send_sems.at[i], recv_sem=recv_sems.at[i],
                                         device_id=dev, device_id_type=MESH).start()
        token[...] = jnp.zeros_like(token)

    bufs = list(srcs) + list(lands)
    outs = pl.pallas_call(
        body, name=name,
        out_shape=(pltpu.SemaphoreType.DMA((n_copies,)), pltpu.SemaphoreType.DMA((n_copies,)),
                   *[pltpu.HBM(b.shape, b.dtype) for b in bufs], jax.ShapeDtypeStruct((8, 128), F32)),
        in_specs=[HBM] * nb + [ANY] * n_after,
        out_specs=(SEM, SEM, *[HBM] * nb, pl.BlockSpec(memory_space=pltpu.VMEM)),
        input_output_aliases={i: 2 + i for i in range(nb)},
        compiler_params=pltpu.CompilerParams(has_side_effects=EFFECT),
    )(*[_in_hbm(b) for b in bufs], *([] if after is None else [after]))
    return outs[0], outs[1], list(outs[2:2 + ns]), list(outs[2 + ns:2 + nb]), outs[-1]


def _remote_wait(started, after, plan, name):
    send_sems, recv_sems, srcs, lands, _ = started
    ns, nb = len(srcs), len(srcs) + len(lands)

    def body(*refs):
        src_refs, land_refs = refs[:ns], refs[ns:nb]
        send_sems, recv_sems = refs[nb], refs[nb + 1]
        x, y, c = _coords()
        for i, (sv, dv, dev) in enumerate(plan(src_refs, land_refs, x, y, c)):
            cp = pltpu.make_async_remote_copy(src_ref=sv, dst_ref=dv, send_sem=send_sems.at[i],
                                              recv_sem=recv_sems.at[i], device_id=dev, device_id_type=MESH)
            cp.wait_send()
            cp.wait_recv()

    bufs = list(srcs) + list(lands)
    outs = pl.pallas_call(
        body, name=name, out_shape=tuple(pltpu.HBM(b.shape, b.dtype) for b in bufs),
        in_specs=[HBM] * nb + [SEM, SEM, ANY], out_specs=tuple([HBM] * nb),
        input_output_aliases={i: i for i in range(nb)},
        compiler_params=pltpu.CompilerParams(has_side_effects=EFFECT),
    )(*bufs, send_sems, recv_sems, after)
    return list(outs[:ns]), list(outs[ns:])


def _pair_plan(src_refs, land_refs, x, y, c):
    plan = []
    for s, l in zip(src_refs, land_refs):
        for q in range(4):
            plan.append((s.at[2 * q + (1 - c)], l.at[q], (x, y, 1 - c)))
    return plan


def _pair4_plan(src_refs, land_refs, x, y, c):
    plan = []
    for s, l in zip(src_refs, land_refs):
        for q in range(4):
            plan.append((s.at[q], l.at[q], (x, y, 1 - c)))
    return plan


def _chips_plan(src_refs, land_refs, x, y, c):
    plan = []
    for s, l in zip(src_refs, land_refs):
        for k, (tx, ty) in enumerate([(1 - x, y), (x, 1 - y), (1 - x, 1 - y)]):
            plan.append((s.at[2 * tx + ty], l.at[k], (tx, ty, c)))
    return plan


def _everyone_plan(src_refs, land_refs, x, y, c):
    me = 4 * x + 2 * y + c
    plan = []
    for s, l in zip(src_refs, land_refs):
        for fx, fy, fc in [(0, 0, 1), (1, 0, 0), (1, 0, 1), (0, 1, 0), (0, 1, 1), (1, 1, 0), (1, 1, 1)]:
            dev = ((1 - x) if fx else x, (1 - y) if fy else y, (1 - c) if fc else c)
            plan.append((s, l.at[me], dev))
    return plan


def _pair_add(g8, r1, csel, tr, name):
    _, R, C = r1.shape
    g4 = g8.reshape(4, 2, R, C)

    def body(c_ref, g_ref, r_ref, o_ref):
        o_ref[...] = (g_ref[...].astype(F32) + r_ref[...].astype(F32)).astype(BF16)

    return pl.pallas_call(
        body,
        grid_spec=pltpu.PrefetchScalarGridSpec(
            num_scalar_prefetch=1, grid=(4, R // tr),
            in_specs=[pl.BlockSpec((None, None, tr, C), lambda q, i, cs: (q, cs[0], i, 0)),
                      pl.BlockSpec((None, tr, C), lambda q, i, cs: (q, i, 0))],
            out_specs=pl.BlockSpec((None, tr, C), lambda q, i, cs: (q, i, 0))),
        out_shape=jax.ShapeDtypeStruct((4, R, C), BF16), name=name,
        compiler_params=_cparams(("parallel", "parallel")),
    )(csel, g4, r1)


def _adamw_math(w, g, m, v):
    m = ADAM_B1 * m + (1.0 - ADAM_B1) * g
    v = ADAM_B2 * v + (1.0 - ADAM_B2) * (g * g)
    m_hat = m / (1.0 - ADAM_B1 ** ADAM_STEP)
    v_hat = v / (1.0 - ADAM_B2 ** ADAM_STEP)
    delta = -ADAM_LR * (m_hat / (jnp.sqrt(v_hat) + ADAM_EPS) + ADAM_WD * w)
    return delta, m, v


def _adamw_big(w, m, v, p4, r3, qsel, tile, name):
    R, C = w.shape
    tr, tc = tile

    def body(q_ref, w_ref, m_ref, v_ref, p_ref, r_ref, g_out, d_out, m_out, v_out):
        g = p_ref[...].astype(F32) + r_ref[0].astype(F32) + r_ref[1].astype(F32) + r_ref[2].astype(F32)
        d, mn, vn = _adamw_math(w_ref[...], g, m_ref[...], v_ref[...])
        g_out[...] = g
        d_out[...] = d
        m_out[...] = mn
        v_out[...] = vn

    blk = pl.BlockSpec((tr, tc), lambda i, j, qs: (i, j))
    return pl.pallas_call(
        body,
        grid_spec=pltpu.PrefetchScalarGridSpec(
            num_scalar_prefetch=1, grid=(R // tr, C // tc),
            in_specs=[blk, blk, blk, pl.BlockSpec((None, tr, tc), lambda i, j, qs: (qs[0], i, j)),
                      pl.BlockSpec((3, tr, tc), lambda i, j, qs: (0, i, j))],
            out_specs=[blk, blk, blk, blk]),
        out_shape=[jax.ShapeDtypeStruct((R, C), F32)] * 4, name=name,
        compiler_params=_cparams(("parallel", "parallel")),
    )(qsel, w, m, v, p4, r3)


def _sum_partials(p4, r3, qsel, tc, name):
    _, R, C = p4.shape

    def body(q_ref, p_ref, r_ref, o_ref):
        o_ref[...] = p_ref[...].astype(F32) + r_ref[0].astype(F32) + r_ref[1].astype(F32) + r_ref[2].astype(F32)

    return pl.pallas_call(
        body,
        grid_spec=pltpu.PrefetchScalarGridSpec(
            num_scalar_prefetch=1, grid=(C // tc,),
            in_specs=[pl.BlockSpec((None, R, tc), lambda j, qs: (qs[0], 0, j)),
                      pl.BlockSpec((3, R, tc), lambda j, qs: (0, 0, j))],
            out_specs=pl.BlockSpec((R, tc), lambda j, qs: (0, j))),
        out_shape=jax.ShapeDtypeStruct((R, C), F32), name=name, compiler_params=_cparams(("parallel",)),
    )(qsel, p4, r3)


def _adamw_tiled(w, g, m, v, tc, name):
    R, C = w.shape

    def body(w_ref, g_ref, m_ref, v_ref, d_out, m_out, v_out):
        d, mn, vn = _adamw_math(w_ref[...], g_ref[...], m_ref[...], v_ref[...])
        d_out[...] = d
        m_out[...] = mn
        v_out[...] = vn

    blk = pl.BlockSpec((R, tc), lambda j: (0, j))
    return pl.pallas_call(
        body, grid=(C // tc,), in_specs=[blk] * 4, out_specs=[blk] * 3,
        out_shape=[jax.ShapeDtypeStruct((R, C), F32)] * 3, name=name, compiler_params=_cparams(("parallel",)),
    )(w, g, m, v)


def _small_sum(parts, name):
    def body(p_ref, o_ref):
        acc = p_ref[0]
        for d in range(1, N_DEV):
            acc = acc + p_ref[d]
        o_ref[...] = acc

    return pl.pallas_call(
        body, out_shape=jax.ShapeDtypeStruct(parts.shape[1:], F32), name=name,
        compiler_params=_cparams(),
    )(parts)


def _adamw_small(w, g, m, v, name):
    def body(w_ref, g_ref, m_ref, v_ref, d_out, m_out, v_out):
        d, mn, vn = _adamw_math(w_ref[...], g_ref[...], m_ref[...], v_ref[...])
        d_out[...] = d
        m_out[...] = mn
        v_out[...] = vn

    return pl.pallas_call(
        body, out_shape=[jax.ShapeDtypeStruct(w.shape, F32)] * 3, name=name, compiler_params=_cparams(),
    )(w, g, m, v)


def _row(*pieces):
    r = jnp.concatenate([p.reshape(1, -1) for p in pieces], axis=1)
    return jnp.pad(r, ((0, 0), (0, D_MODEL - r.shape[1])))


def _pack_small(mix, convb, ssmg, attng, mlpg, fing, convw, dtb, alog, dsk, sinks, extra=None):
    last = [dtb, alog, dsk, sinks] + ([extra] if extra is not None else [])
    rows = [_row(mix), _row(convb), _row(ssmg, attng), _row(mlpg), _row(fing),
            jnp.pad(convw, ((0, 0), (0, D_MODEL - convw.shape[1]))), _row(*last)]
    packed = jnp.concatenate(rows, axis=0)
    return jnp.pad(packed, ((0, SMALL_ROWS - packed.shape[0]), (0, 0)))


def _unpack_small(p, conv_n):
    return dict(
        mix_norm_g=p[0:1, :], conv_b=p[1:2, :], ssm_norm_g=p[2:3, :D_INNER], attn_out_norm_g=p[2:3, D_INNER:],
        mlp_norm_g=p[3:4, :], final_norm_g=p[4, :], conv_w=p[5:9, :conv_n][None],
        dt_bias=p[9:10, 0:16], A_log=p[9:10, 16:32], D_skip=p[9:10, 32:48], attn_sinks=p[9:10, 48:64])


WEIGHT_ORDER = ["mix_norm_g", "w_in", "conv_w", "conv_b", "dt_bias", "A_log", "D_skip", "ssm_norm_g", "attn_sinks",
                "attn_out_norm_g", "w_out", "mlp_norm_g", "w_up", "w_down", "final_norm_g"]


def _to_my_columns(w_nat):
    pad = jnp.zeros((w_nat.shape[0], NP - IN_PROJ), w_nat.dtype)
    return jnp.concatenate([w_nat[:, :NAT_DT], w_nat[:, NAT_DT + N_HEADS:], w_nat[:, NAT_DT:NAT_DT + N_HEADS], pad],
                           axis=1)


PER = IN_PROJ // N_DEV
SUPER_STEP = 544
SUPER = 576


def _natural_rows(g, lo, hi):
    segments = [(0, NAT_DT, 0), (NAT_DT, NAT_DT + N_HEADS, OFF_DT - NAT_DT), (NAT_DT + N_HEADS, IN_PROJ, -N_HEADS),
                (IN_PROJ, NP, 0)]
    pieces = [g[max(lo, a) + shift:min(hi, b) + shift] for a, b, shift in segments if max(lo, a) < min(hi, b)]
    return pieces[0] if len(pieces) == 1 else jnp.concatenate(pieces, axis=0)


def _w_in_from_super_slabs(sup):
    seam = SUPER - SUPER_STEP
    units = []
    for i in range(N_DEV):
        base = SUPER_STEP * i
        units.append((base, base + seam, sup[i, :seam] if i == 0 else sup[i - 1, SUPER_STEP:] + sup[i, :seam]))
        units.append((base + seam, base + SUPER_STEP, sup[i, seam:SUPER_STEP]))
    units.append((SUPER_STEP * N_DEV, SUPER_STEP * N_DEV + seam, sup[N_DEV - 1, SUPER_STEP:]))

    def natural(lo, hi):
        return [rows[max(lo, a) - a:min(hi, b) - a] for a, b, rows in units if max(lo, a) < min(hi, b)]

    pieces = natural(0, NAT_DT) + natural(NAT_DT + N_HEADS, IN_PROJ) + natural(NAT_DT, NAT_DT + N_HEADS)
    return jnp.concatenate(pieces + [jnp.zeros((NP - IN_PROJ, D_MODEL), sup.dtype)], axis=0)


def _to_natural_columns(w_my):
    return jnp.concatenate([w_my[:, :NAT_DT], w_my[:, OFF_DT:OFF_DT + N_HEADS], w_my[:, NAT_DT:OFF_DT]], axis=1)


SLAB = 1024


def _grad_w_up(h2, du, name, sel=None, add=None, after=None):
    T, D = h2.shape
    if sel is None:
        pick, n_slab, pre = (lambda j, *cs: j), N_DEV, None
    else:
        pre, other = sel
        pick, n_slab = (lambda j, cs: 2 * j + ((1 - cs[0]) if other else cs[0])), 4
    o_spec = pl.BlockSpec((None, SLAB, SLAB), lambda i, j, k, *cs: (j, i, 0))
    return _matmul(
        h2, du, mode="tn", grid=(D // SLAB, n_slab, 1),
        a_spec=pl.BlockSpec((T, SLAB), lambda i, j, k, *cs: (0, i)),
        b_spec=pl.BlockSpec((T, SLAB), lambda i, j, k, *cs: (0, pick(j, *cs))),
        out_shapes=[jax.ShapeDtypeStruct((n_slab, D, SLAB), BF16)], out_specs=[o_spec], tile=(SLAB, SLAB), name=name,
        extras=() if add is None else (add,), extra_specs=() if add is None else (o_spec,),
        epilogue=None if add is None else (lambda acc, r: (acc + r.astype(F32),)), after=after, prefetch=pre)[0]


def _grad_w_down(act, dx3b, name, sel=None, add=None, after=None):
    T, D = dx3b.shape
    if sel is None:
        pick, n_slab, pre = (lambda i, *cs: i), N_DEV, None
    else:
        pre, other = sel
        pick, n_slab = (lambda i, cs: 2 * i + ((1 - cs[0]) if other else cs[0])), 4
    o_spec = pl.BlockSpec((None, SLAB, SLAB), lambda i, j, k, *cs: (i, 0, j))
    return _matmul(
        act, dx3b, mode="tn", grid=(n_slab, D // SLAB, 1),
        a_spec=pl.BlockSpec((T, SLAB), lambda i, j, k, *cs: (0, pick(i, *cs))),
        b_spec=pl.BlockSpec((T, SLAB), lambda i, j, k, *cs: (0, j)),
        out_shapes=[jax.ShapeDtypeStruct((n_slab, SLAB, D), BF16)], out_specs=[o_spec], tile=(SLAB, SLAB), name=name,
        extras=() if add is None else (add,), extra_specs=() if add is None else (o_spec,),
        epilogue=None if add is None else (lambda acc, r: (acc + r.astype(F32),)), after=after, prefetch=pre)[0]


class _FixedWeights:
    def __init__(self, w_in_p, w_out_f, w_up_s, w_down_f, conv_w_f):
        self.w = (w_in_p, w_out_f, w_up_s, w_down_f, conv_w_f)
        self.grads = {}

    def mixer_weights(self, after):
        return self.w[0], None

    def conv_weight(self, after):
        return self.w[4]

    def out_weight(self, after):
        return self.w[1]

    def up_weight(self, after):
        return self.w[2]

    def down_weight(self, h, after):
        return self.w[3][:, h * (D_MODEL // 2):(h + 1) * (D_MODEL // 2)]

    def mlp_grads(self, h2, du, act, dx3b):
        self.grads.update(w_up=_grad_w_up(h2, du, "grad_w_up"),
                          w_down=_grad_w_down(act, dx3b, "grad_w_down").reshape(D_FF, D_MODEL))
        return None

    def grad_sent(self, tag, after):
        return None

    def out_grad(self, g_out):
        self.grads.update(w_out=g_out)
        return None

    def in_grad(self, g_in):
        self.grads.update(w_in=g_in)
        return None


def _local_step(x, tgt, p, hooks):
    T = x.shape[0]
    D = D_MODEL
    h1 = _rmsnorm_fwd(x, p["mix_norm_g"], "norm_mix")
    w_in_t, token = hooks.mixer_weights(h1)
    (proj,) = _mm_simple(h1, w_in_t, mode="nt", M=T, N=NP, K=D, tm=min(T, 1024), tn=1536, tk=D, out_dtype=F32,
                         name="in_proj", after=token)
    conv_w_f = hooks.conv_weight(proj)
    xbc = _conv_fwd(proj, conv_w_f, p["conv_b"], "conv_fwd")
    dtT = proj[:, OFF_DT:OFF_DT + N_HEADS].T
    dtbT = p["dt_bias"].T
    alogT = p["A_log"].T
    dfull = jnp.repeat(p["D_skip"], HEAD_DIM, axis=1)
    ycat, ypre, hs = _ssd_fwd(xbc, proj, dtT, p["dt_bias"], dtbT, p["A_log"], alogT, dfull, p["ssm_norm_g"],
                              "ssd_fwd")
    ycat, o_att = _attn_fwd(proj, p["attn_sinks"], p["attn_out_norm_g"], ycat, "attn_fwd")
    w_out_f = hooks.out_weight(ycat)
    tm = min(T, 1024)
    (x2,) = _mm_simple(ycat, w_out_f, mode="nn", M=T, N=D, K=D, tm=tm, tn=1024, tk=D, out_dtype=F32, name="out_proj",
                       extras=(x,), epilogue=lambda acc, res: (acc + res,))
    h2 = _rmsnorm_fwd(x2, p["mlp_norm_g"], "norm_mlp")
    w_up_s = hooks.up_weight(h2)
    grid = (T // tm, N_DEV, 1)
    u, act = _matmul(
        h2, w_up_s, mode="nn", grid=grid,
        a_spec=pl.BlockSpec((tm, D), lambda i, j, k: (i, 0)),
        b_spec=pl.BlockSpec((None, D, 1024), lambda i, j, k: (j, 0, 0)),
        out_shapes=[jax.ShapeDtypeStruct((T, D_FF), F32), jax.ShapeDtypeStruct((T, D_FF), BF16)],
        out_specs=[pl.BlockSpec((tm, 1024), lambda i, j, k: (i, j))] * 2, tile=(tm, 1024), name="mlp_up",
        epilogue=lambda acc: (acc, jnp.square(jnp.maximum(acc, 0.0))))
    half = D // 2
    w_down_halves, x3_halves = [], []
    for h in range(2):
        w_down_halves.append(hooks.down_weight(h, act if h == 0 else x3_halves[0]))
        x3_halves.append(_matmul(
            act, w_down_halves[h], mode="nn", grid=(T // tm, 1, D_FF // 2048),
            a_spec=pl.BlockSpec((tm, 2048), lambda i, j, k: (i, k)),
            b_spec=pl.BlockSpec((2048, half), lambda i, j, k: (k, 0)),
            out_shapes=[jax.ShapeDtypeStruct((T, half), F32)],
            out_specs=[pl.BlockSpec((tm, half), lambda i, j, k: (i, 0))], tile=(tm, half), name=f"mlp_down_{h}",
            extras=(x2,), extra_specs=[pl.BlockSpec((tm, half), lambda i, j, k, h=h: (i, h))],
            epilogue=lambda acc, res: (acc + res,))[0])
    loss_part, d_fin, dx3, dx3b = _final_loss(x3_halves, tgt, p["final_norm_g"].reshape(1, D), "loss_head")
    (du,) = _matmul(
        dx3b, tuple(w_down_halves), mode="nt", grid=(T // tm, D_FF // 1024, 1),
        a_spec=pl.BlockSpec((tm, D), lambda i, j, k: (i, 0)),
        b_spec=(pl.BlockSpec((1024, half), lambda i, j, k: (j, 0)),) * 2,
        out_shapes=[jax.ShapeDtypeStruct((T, D_FF), BF16)],
        out_specs=[pl.BlockSpec((tm, 1024), lambda i, j, k: (i, j))], tile=(tm, 1024), name="mlp_down_bwd",
        extras=(u,), extra_specs=[pl.BlockSpec((tm, 1024), lambda i, j, k: (i, j))],
        epilogue=lambda acc, uu: (acc * (2.0 * jnp.maximum(uu, 0.0)),),
        dot_fn=lambda a, b0, b1: _dot_nt(a[:, :half], b0) + _dot_nt(a[:, half:], b1))
    token = hooks.mlp_grads(h2, du, act, dx3b)
    (dh2,) = _matmul(
        du, w_up_s, mode="nt", grid=(T // tm, D // 1024, N_DEV // 2),
        a_spec=pl.BlockSpec((tm, 2048), lambda i, j, k: (i, k)),
        b_spec=pl.BlockSpec((2, 1024, 1024), lambda i, j, k: (k, j, 0)),
        out_shapes=[jax.ShapeDtypeStruct((T, D), F32)],
        out_specs=[pl.BlockSpec((tm, 1024), lambda i, j, k: (i, j))], tile=(tm, 1024), name="mlp_up_bwd",
        after=token, dot_fn=lambda a, b: _dot_nt(a[:, :1024], b[0]) + _dot_nt(a[:, 1024:], b[1]))
    dx2, dx2b, d_mlp = _rmsnorm_bwd(dh2, x2, p["mlp_norm_g"], dx3, "norm_mlp_bwd")
    (g_out,) = _mm_simple(ycat, dx2b, mode="tn", M=D, N=D, K=T, tm=1024, tn=1024, tk=T, out_dtype=BF16,
                          name="grad_w_out")
    token = hooks.out_grad(g_out)
    (dy,) = _mm_simple(dx2b, w_out_f, mode="nt", M=T, N=D, K=D, tm=tm, tn=1024, tk=D, out_dtype=F32,
                       name="out_proj_bwd", after=token)
    token = hooks.grad_sent("out", dy)
    ssm_g = p["ssm_norm_g"] if token is None else p["ssm_norm_g"] + token[0:1, 0:1]
    dproj, dxbc_act, d_dtb, d_alog, d_dskip, d_ssmg = _ssd_bwd(
        xbc, proj, dtT, p["dt_bias"], dtbT, p["A_log"], alogT, dfull, ssm_g, ypre, hs, dy, "ssd_bwd")
    dproj, d_convw, d_convb = _conv_bwd(proj, dxbc_act, conv_w_f, p["conv_b"], dproj, "conv_bwd")
    dproj, dk, dv, d_sinks, d_attng = _attn_bwd(proj, p["attn_sinks"], p["attn_out_norm_g"], o_att, dy, dproj,
                                                "attn_bwd")
    dproj = lax.dynamic_update_slice(dproj, jnp.concatenate([dk, dv], axis=1).astype(BF16), (0, OFF_K))
    (g_in,) = _mm_simple(dproj, h1, mode="tn", M=NP, N=D, K=T, tm=1536, tn=1024, tk=T, out_dtype=BF16,
                         name="grad_w_in")
    token = hooks.in_grad(g_in)
    (dh1,) = _mm_simple(dproj, w_in_t, mode="nn", M=T, N=D, K=NP, tm=tm, tn=1024, tk=2304, out_dtype=F32,
                        name="in_proj_bwd", after=token)
    token = hooks.grad_sent("in", dh1)
    mix_g = p["mix_norm_g"] if token is None else p["mix_norm_g"] + token[0:1, 0:1]
    dx, d_mix = _rmsnorm_bwd(dh1, x, mix_g, dx2, "norm_mix_bwd", with_bf16=False)
    small = _pack_small(d_mix, d_convb, d_ssmg, d_attng, d_mlp, d_fin, d_convw, d_dtb, d_alog, d_dskip, d_sinks,
                        extra=loss_part[:, 0:1])
    return dx, small


def _rows_rotated(v, shift, name):
    R, C = v.shape
    tc = 512

    def body(s_ref, v_ref, o_ref):
        o_ref[...] = pltpu.roll(v_ref[...], s_ref[0], axis=0).astype(BF16)

    return pl.pallas_call(
        body,
        grid_spec=pltpu.PrefetchScalarGridSpec(
            num_scalar_prefetch=1, grid=(C // tc,), in_specs=[pl.BlockSpec((R, tc), lambda j, s: (0, j))],
            out_specs=pl.BlockSpec((R, tc), lambda j, s: (0, j))),
        out_shape=jax.ShapeDtypeStruct((R, C), BF16), name=name, compiler_params=_cparams(("parallel",)),
    )(shift, v)


def _landing(own, me):
    zone = lax.empty((N_DEV,) + own.shape, own.dtype)
    return lax.dynamic_update_slice(zone, own[None], (me,) + (0,) * own.ndim)


def _sequencer_gather(owns, split, me, collective_id, name):
    n = len(owns)
    zone_refs = [jax.new_ref(_landing(o, me), memory_space=pltpu.MemorySpace.HBM) for o in owns]
    own_refs = [jax.new_ref(o, memory_space=pltpu.MemorySpace.HBM) for o in owns]
    N_COPIES = 9

    @pl.kernel(mesh=plsc.ScalarSubcoreMesh(axis_name="sequencer", num_cores=1), name=name,
               scratch_types=(pltpu.SemaphoreType.DMA((n, N_COPIES)), pltpu.SemaphoreType.DMA((n, N_COPIES))),
               compiler_params=pltpu.CompilerParams(collective_id=collective_id))
    def launch(send_sems, recv_sems):
        x, y, c = _coords()
        sibling, xn, yn, diag = (x, y, 1 - c), (1 - x, y, c), (x, 1 - y, c), (1 - x, 1 - y, c)
        barrier = pltpu.get_barrier_semaphore()
        for peer in [sibling, xn, yn, diag]:
            pl.semaphore_signal(barrier, inc=1, device_id=peer, device_id_type=MESH)
        pl.semaphore_wait(barrier, 4)

        def block(a, dev, half=None):
            ref = zone_refs[a].at[4 * dev[0] + 2 * dev[1] + dev[2]]
            if half is None:
                return ref
            rows = owns[a].shape[0] // 2
            return ref.at[pl.ds(half * rows, rows)]

        def copy(a, k, src, dst, to):
            return pltpu.make_async_remote_copy(src_ref=src, dst_ref=dst, send_sem=send_sems.at[a, k],
                                                recv_sem=recv_sems.at[a, k], device_id=to, device_id_type=MESH)

        me_dev = (x, y, c)
        sent = []
        first = {}
        for a in range(n):
            for k, peer in enumerate([sibling, xn, yn] + ([] if split[a] else [diag])):
                first[a, k] = copy(a, k, own_refs[a], block(a, me_dev), peer)
                first[a, k].start()
                sent.append(first[a, k])
        from_sibling = []
        for a in range(n):
            first[a, 1].wait_recv()
            sent.append(copy(a, 4, block(a, xn), block(a, xn), sibling))
            if split[a]:
                sent.append(copy(a, 6, block(a, xn, 0), block(a, xn, 0), yn))
            first[a, 2].wait_recv()
            sent.append(copy(a, 5, block(a, yn), block(a, yn), sibling))
            if split[a]:
                sent.append(copy(a, 7, block(a, yn, 1), block(a, yn, 1), xn))
            for cp in sent[-(4 if split[a] else 2):]:
                cp.start()
        for a in range(n):
            if split[a]:
                copy(a, 6, block(a, diag, 0), block(a, diag, 0), yn).wait_recv()
                sent.append(copy(a, 8, block(a, diag, 0), block(a, diag, 0), sibling))
                sent[-1].start()
                copy(a, 7, block(a, diag, 1), block(a, diag, 1), xn).wait_recv()
                sent.append(copy(a, 3, block(a, diag, 1), block(a, diag, 1), sibling))
                sent[-1].start()
            else:
                first[a, 3].wait_recv()
                sent.append(copy(a, 8, block(a, diag), block(a, diag), sibling))
                sent[-1].start()
        for a in range(n):
            first[a, 0].wait_recv()
            copy(a, 4, block(a, xn), block(a, xn), sibling).wait_recv()
            copy(a, 5, block(a, yn), block(a, yn), sibling).wait_recv()
            if split[a]:
                copy(a, 8, block(a, diag, 0), block(a, diag, 0), sibling).wait_recv()
                copy(a, 3, block(a, diag, 1), block(a, diag, 1), sibling).wait_recv()
            else:
                copy(a, 8, block(a, diag), block(a, diag), sibling).wait_recv()
        for cp in sent:
            cp.wait_send()

    launch()
    return zone_refs


class _ShardedWeights:
    def __init__(self, w_in, w_out, conv_w, w_up, w_down, me, csel):
        self.me, self.csel = me, csel
        padded = jnp.pad(jnp.transpose(w_in), ((0, SUPER - PER), (0, 0)))
        own_rows = _rows_rotated(padded, jnp.reshape(2 * me, (1,)).astype(jnp.int32), "w_in_super_slab")
        (self.in_ref,) = _sequencer_gather([own_rows], [True], me, 7, "gather_w_in_sequencer")
        self.out_ref, self.conv_ref = _sequencer_gather([w_out.astype(BF16), conv_w], [True, False], me, 8,
                                                        "gather_w_out_sequencer")
        (self.up_ref,) = _sequencer_gather([w_up.astype(BF16)], [True], me, 9, "gather_w_up_sequencer")
        down = w_down.astype(BF16)
        self.down_refs = [_sequencer_gather([down[:, h * (D_MODEL // 2):(h + 1) * (D_MODEL // 2)]], [True], me, 10 + h,
                                            f"gather_w_down_{h}_sequencer")[0] for h in range(2)]
        self.reduces = {}
        self.pairs = {}

    def mixer_weights(self, after):
        return _w_in_from_super_slabs(self.in_ref[...]), None

    def conv_weight(self, after):
        g_conv = self.conv_ref[...]
        return jnp.concatenate([g_conv[i] for i in range(N_DEV)], axis=1)

    def out_weight(self, after):
        return self.out_ref[...].reshape(D_MODEL, D_MODEL)

    def up_weight(self, after):
        return self.up_ref[...]

    def down_weight(self, h, after):
        return self.down_refs[h][...].reshape(D_FF, D_MODEL // 2)

    def _chips_start(self, slabs, from_sibling, rows, tag):
        sums = [_pair_add(s, r, self.csel, tr, f"pair_add_{tag}_{i}")
                for i, (s, r, tr) in enumerate(zip(slabs, from_sibling, rows))]
        lands = [lax.empty((3,) + s.shape[1:], s.dtype) for s in sums]
        self.reduces[tag] = _remote_start(sums, lands, _chips_plan, 3 * len(sums), f"reduce_start_{tag}")
        return self.reduces[tag][4]

    def mlp_grads(self, h2, du, act, dx3b):
        def send(part, tag, after):
            st = _remote_start([part], [lax.empty(part.shape, part.dtype)], _pair4_plan, 4,
                               f"reduce_pair_start_{tag}", after=after)
            return st

        def received(st, after, tag):
            return _remote_wait(st, after, _pair4_plan, f"reduce_pair_wait_{tag}")[1][0]

        def to_chips(sums, tag):
            self.reduces[tag] = _remote_start([sums], [lax.empty((3,) + sums.shape[1:], sums.dtype)], _chips_plan, 3,
                                              f"reduce_start_{tag}")
            return self.reduces[tag][4]

        up_send = _grad_w_up(h2, du, "grad_w_up_send", sel=(self.csel, True))
        st_up = send(up_send, "up", None)
        down_send = _grad_w_down(act, dx3b, "grad_w_down_send", sel=(self.csel, True), after=st_up[4])
        st_down = send(down_send, "down", None)
        up_sum = _grad_w_up(h2, du, "grad_w_up_keep", sel=(self.csel, False), add=received(st_up, down_send, "up"),
                            after=st_down[4])
        token = to_chips(up_sum, "up")
        down_sum = _grad_w_down(act, dx3b, "grad_w_down_keep", sel=(self.csel, False),
                                add=received(st_down, up_sum, "down"), after=token)
        return to_chips(down_sum, "down")

    def _pair_start(self, slabs, tag):
        land = lax.empty((4,) + slabs.shape[1:], slabs.dtype)
        self.pairs[tag] = _remote_start([slabs], [land], _pair_plan, 4, f"reduce_pair_start_{tag}")
        return self.pairs[tag][4]

    def grad_sent(self, tag, after):
        slabs, from_sibling = _remote_wait(self.pairs[tag], after, _pair_plan, f"reduce_pair_wait_{tag}")
        return self._chips_start(slabs, from_sibling, [slabs[0].shape[1]], tag)

    def out_grad(self, g_out):
        return self._pair_start(g_out.reshape(N_DEV, D_MODEL // N_DEV, D_MODEL), "out")

    def in_grad(self, g_in):
        return self._pair_start(
            jnp.stack([_natural_rows(g_in, SUPER_STEP * j, SUPER_STEP * j + SUPER) for j in range(N_DEV)]), "in")

    def small_start(self, small):
        self.st_small = _remote_start([small], [_landing(small, self.me)], _everyone_plan, N_DEV - 1, "gather_start_small")

    def small_end(self, after):
        return _remote_wait(self.st_small, after, _everyone_plan, "gather_small_wait")[1][0]

    def reduce_end(self, tag, after):
        return _remote_wait(self.reduces[tag], after, _chips_plan, f"reduce_wait_{tag}")


def kernel(x, mix_norm_g, w_in, conv_w, conv_b, dt_bias, A_log, D_skip, ssm_norm_g, attn_sinks, attn_out_norm_g, w_out, mlp_norm_g, w_up, w_down, final_norm_g, loss_target, m_mix_norm_g, m_w_in, m_conv_w, m_conv_b, m_dt_bias, m_A_log, m_D_skip, m_ssm_norm_g, m_attn_sinks, m_attn_out_norm_g, m_w_out, m_mlp_norm_g, m_w_up, m_w_down, m_final_norm_g, v_mix_norm_g, v_w_in, v_conv_w, v_conv_b, v_dt_bias, v_A_log, v_D_skip, v_ssm_norm_g, v_attn_sinks, v_attn_out_norm_g, v_w_out, v_mlp_norm_g, v_w_up, v_w_down, v_final_norm_g):
    xi, yi, ci = _coords()
    me = 4 * xi + 2 * yi + ci
    csel = jnp.reshape(ci, (1,)).astype(jnp.int32)
    qsel = jnp.reshape(2 * xi + yi, (1,)).astype(jnp.int32)
    w = dict(mix_norm_g=mix_norm_g, conv_b=conv_b, dt_bias=dt_bias, A_log=A_log, D_skip=D_skip,
             ssm_norm_g=ssm_norm_g, attn_sinks=attn_sinks, attn_out_norm_g=attn_out_norm_g, mlp_norm_g=mlp_norm_g,
             final_norm_g=final_norm_g)
    hooks = _ShardedWeights(w_in[0], w_out[0], conv_w[0], w_up[0], w_down[0], me, csel)
    p = dict(w)
    dx, small = _local_step(x[0], loss_target[0], p, hooks)
    hooks.small_start(small)
    big = {}
    after = dx
    for name, wt, mt, vt, tile in [
            ("up", w_up, m_w_up, v_w_up, (512, SLAB)), ("down", w_down, m_w_down, v_w_down, (256, D_MODEL)),
            ("out", w_out, m_w_out, v_w_out, (256, D_MODEL))]:
        (chip_sums,), (from_chips,) = hooks.reduce_end(name, after)
        res = _adamw_big(wt[0], mt[0], vt[0], chip_sums, from_chips, qsel, tile, f"adamw_w_{name}")
        big["w_" + name] = tuple(r[None] for r in res)
        after = res[0]
    (chip_sums,), (from_chips,) = hooks.reduce_end("in", after)
    g_super = _sum_partials(chip_sums, from_chips, qsel, 512, "grad_w_in_sum")
    g_in = lax.dynamic_slice(g_super, (2 * me, 0), (PER, D_MODEL))
    res = _adamw_tiled(jnp.transpose(w_in[0]), g_in, jnp.transpose(m_w_in[0]), jnp.transpose(v_w_in[0]), 512,
                       "adamw_w_in")
    big["w_in"] = tuple(jnp.transpose(r)[None] for r in (g_in, *res))
    after = res[0]
    gsum = _small_sum(hooks.small_end(after), "small_sum")
    loss = gsum[9, 64]
    gs = _unpack_small(gsum, CONV_DIM)
    cw = CONV_DIM // N_DEV
    g_conv_shard = lax.dynamic_slice(gsum[5:9, :], (0, me * cw), (CONV_K, cw))

    def pack(s):
        return _pack_small(s["mix_norm_g"], s["conv_b"], s["ssm_norm_g"], s["attn_out_norm_g"], s["mlp_norm_g"],
                           s["final_norm_g"], s["conv_w"][0], s["dt_bias"], s["A_log"], s["D_skip"], s["attn_sinks"])

    wp = pack(dict(w, conv_w=conv_w))
    mp = pack(dict(mix_norm_g=m_mix_norm_g, conv_b=m_conv_b, ssm_norm_g=m_ssm_norm_g,
                   attn_out_norm_g=m_attn_out_norm_g, mlp_norm_g=m_mlp_norm_g, final_norm_g=m_final_norm_g,
                   conv_w=m_conv_w, dt_bias=m_dt_bias, A_log=m_A_log, D_skip=m_D_skip, attn_sinks=m_attn_sinks))
    vp = pack(dict(mix_norm_g=v_mix_norm_g, conv_b=v_conv_b, ssm_norm_g=v_ssm_norm_g,
                   attn_out_norm_g=v_attn_out_norm_g, mlp_norm_g=v_mlp_norm_g, final_norm_g=v_final_norm_g,
                   conv_w=v_conv_w, dt_bias=v_dt_bias, A_log=v_A_log, D_skip=v_D_skip, attn_sinks=v_attn_sinks))
    gp = jnp.concatenate([gsum[0:5], jnp.pad(g_conv_shard, ((0, 0), (0, D_MODEL - cw))), gsum[9:10],
                          jnp.zeros((SMALL_ROWS - 10, D_MODEL), F32)], axis=0)
    dp, mnp, vnp = _adamw_small(wp, gp, mp, vp, "adamw_small")
    grads = dict(gs, conv_w=g_conv_shard[None])
    deltas = _unpack_small(dp, cw)
    new_m = _unpack_small(mnp, cw)
    new_v = _unpack_small(vnp, cw)
    for k, name in enumerate(["w_in", "w_out", "w_up", "w_down"]):
        grads[name], deltas[name], new_m[name], new_v[name] = big[name]
    return (loss, dx[None], *[grads[n] for n in WEIGHT_ORDER], *[deltas[n] for n in WEIGHT_ORDER],
            *[new_m[n] for n in WEIGHT_ORDER], *[new_v[n] for n in WEIGHT_ORDER])
```

```python
import jax
import jax.numpy as jnp
from jax import lax
from jax.experimental import pallas as pl
from jax.experimental.pallas import tpu as pltpu
from jax.experimental.pallas import tpu_sc as plsc

F32 = jnp.float32
BF16 = jnp.bfloat16
MESH = pl.DeviceIdType.MESH

EPS = 1e-5
D_MODEL = 2048
D_INNER = 1024
N_HEADS = 16
HEAD_DIM = 64
N_GROUPS = 4
D_STATE = 128
CHUNK = 128
CONV_K = 4
CONV_DIM = 2048
ATTN_W = 1024
KV_W = 128
WINDOW = 128
D_FF = 8192
IN_PROJ = 4368
N_DEV = 8
NP = 4608
OFF_Z, OFF_X, OFF_B, OFF_C, OFF_Q, OFF_K, OFF_V, OFF_DT = 0, 1024, 2048, 2560, 3072, 4096, 4224, 4352
NAT_DT = 3072

ADAM_LR = 0.001
ADAM_B1 = 0.9
ADAM_B2 = 0.999
ADAM_EPS = 1e-08
ADAM_WD = 0.01
ADAM_STEP = 10

VMEM_LIMIT = 52 * 1024 * 1024
SMALL_ROWS = 16
NEG = -1e30


def _cparams(sem=None):
    return pltpu.CompilerParams(dimension_semantics=sem, vmem_limit_bytes=VMEM_LIMIT)


def _split3(v):
    hi = v.astype(BF16)
    rest = v - hi.astype(F32)
    mid = rest.astype(BF16)
    return hi, mid, (rest - mid.astype(F32)).astype(BF16)


def _hdot(a, b, data):
    if data == "a":
        sel = b.astype(BF16)
        return sum(_dot_nn(part, sel) for part in _split3(a))
    sel = a.astype(BF16)
    return sum(_dot_nn(sel, part) for part in _split3(b))


def _dot_nn(a, b):
    return lax.dot_general(a, b, (((1,), (0,)), ((), ())), preferred_element_type=F32)


def _dot_nt(a, b):
    return lax.dot_general(a, b, (((1,), (1,)), ((), ())), preferred_element_type=F32)


def _dot_tn(a, b):
    return lax.dot_general(a, b, (((0,), (0,)), ((), ())), preferred_element_type=F32)


def _softplus(v):
    return jnp.maximum(v, 0.0) + jnp.log1p(jnp.exp(-jnp.abs(v)))


def _sigmoid(v):
    return 1.0 / (1.0 + jnp.exp(-v))


def _matmul(a, b, *, mode, grid, a_spec, b_spec, out_shapes, out_specs, tile, name,
            extras=(), extra_specs=(), epilogue=None, after=None, dot_fn=None, prefetch=None):
    nk = grid[2]
    n_ex = len(extras)
    n_out = len(out_shapes)
    bs, b_specs = (b, b_spec) if isinstance(b, tuple) else ((b,), (b_spec,))
    n_in = 1 + len(bs)
    dot = dot_fn if dot_fn is not None else {"nn": _dot_nn, "nt": _dot_nt, "tn": _dot_tn}[mode]

    def finish(acc, ex_refs, out_refs):
        res = (acc,) if epilogue is None else epilogue(acc, *[e[...] for e in ex_refs])
        for o, r in zip(out_refs, res):
            o[...] = r.astype(o.dtype)

    def body(*refs):
        ex_refs = refs[n_in:n_in + n_ex]
        out_refs = refs[n_in + n_ex:n_in + n_ex + n_out]
        part = dot(*[r[...].astype(BF16) for r in refs[:n_in]])
        if nk == 1:
            finish(part, ex_refs, out_refs)
        else:
            acc_ref = refs[-1]
            k = pl.program_id(2)

            @pl.when(k == 0)
            def _():
                acc_ref[...] = part

            @pl.when(k > 0)
            def _():
                acc_ref[...] += part

            @pl.when(k == nk - 1)
            def _():
                finish(acc_ref[...], ex_refs, out_refs)

    scratch = [] if nk == 1 else [pltpu.VMEM(tile, F32)]
    n_pre = 0 if prefetch is None else 1
    tok_specs = [] if after is None else [pl.BlockSpec((8, 128), lambda *_: (0, 0))]
    tok_args = [] if after is None else [after]

    def body_with_token(*refs):
        refs = refs[n_pre:]
        body(*refs[:n_in + n_ex], *refs[n_in + n_ex + len(tok_args):])

    in_specs = [a_spec, *b_specs, *extra_specs, *tok_specs]
    params = _cparams(("parallel", "parallel", "arbitrary"))
    if prefetch is None:
        return pl.pallas_call(
            body_with_token, grid=grid, in_specs=in_specs, out_specs=list(out_specs), out_shape=list(out_shapes),
            scratch_shapes=scratch, name=name, compiler_params=params)(a, *bs, *extras, *tok_args)
    return pl.pallas_call(
        body_with_token,
        grid_spec=pltpu.PrefetchScalarGridSpec(num_scalar_prefetch=1, grid=grid, in_specs=in_specs,
                                               out_specs=list(out_specs), scratch_shapes=scratch),
        out_shape=list(out_shapes), name=name, compiler_params=params)(prefetch, a, *bs, *extras, *tok_args)


def _mm_simple(a, b, *, mode, M, N, K, tm, tn, tk, out_dtype, name, extras=(), epilogue=None, n_out=1,
               out_dtypes=None, after=None):
    grid = (M // tm, N // tn, K // tk)
    if mode == "nn":
        a_spec = pl.BlockSpec((tm, tk), lambda i, j, k: (i, k))
        b_spec = pl.BlockSpec((tk, tn), lambda i, j, k: (k, j))
    elif mode == "nt":
        a_spec = pl.BlockSpec((tm, tk), lambda i, j, k: (i, k))
        b_spec = pl.BlockSpec((tn, tk), lambda i, j, k: (j, k))
    else:
        a_spec = pl.BlockSpec((tk, tm), lambda i, j, k: (k, i))
        b_spec = pl.BlockSpec((tk, tn), lambda i, j, k: (k, j))
    o_spec = pl.BlockSpec((tm, tn), lambda i, j, k: (i, j))
    dts = out_dtypes if out_dtypes is not None else [out_dtype] * n_out
    return _matmul(a, b, mode=mode, grid=grid, a_spec=a_spec, b_spec=b_spec,
                   out_shapes=[jax.ShapeDtypeStruct((M, N), d) for d in dts],
                   out_specs=[o_spec] * len(dts), tile=(tm, tn), name=name,
                   extras=extras, extra_specs=[o_spec] * len(extras), epilogue=epilogue, after=after)


ROW_BLOCK = 256


def _rmsnorm_fwd(x, g, name):
    T, D = x.shape

    def body(x_ref, g_ref, o_ref):
        xf = x_ref[...]
        r = lax.rsqrt(jnp.mean(xf * xf, axis=-1, keepdims=True) + EPS)
        o_ref[...] = (xf * r * g_ref[...]).astype(BF16)

    return pl.pallas_call(
        body, grid=(T // ROW_BLOCK,),
        in_specs=[pl.BlockSpec((ROW_BLOCK, D), lambda i: (i, 0)), pl.BlockSpec((1, D), lambda i: (0, 0))],
        out_specs=pl.BlockSpec((ROW_BLOCK, D), lambda i: (i, 0)),
        out_shape=jax.ShapeDtypeStruct((T, D), BF16), name=name, compiler_params=_cparams(("parallel",)),
    )(x, g)


def _rmsnorm_bwd(dh, x, g, dres, name, with_bf16=True):
    T, D = x.shape

    def body(dh_ref, x_ref, g_ref, dres_ref, dx_ref, *rest):
        dg_ref = rest[-1]
        i = pl.program_id(0)
        xf = x_ref[...]
        r = lax.rsqrt(jnp.mean(xf * xf, axis=-1, keepdims=True) + EPS)
        xh = xf * r
        d = dh_ref[...]

        @pl.when(i == 0)
        def _():
            dg_ref[...] = jnp.zeros_like(dg_ref)

        dg_ref[...] += jnp.sum(d * xh, axis=0, keepdims=True)
        dxh = d * g_ref[...]
        dx = r * (dxh - xh * jnp.mean(dxh * xh, axis=-1, keepdims=True)) + dres_ref[...]
        dx_ref[...] = dx
        if with_bf16:
            rest[0][...] = dx.astype(BF16)

    row = pl.BlockSpec((ROW_BLOCK, D), lambda i: (i, 0))
    vec = pl.BlockSpec((1, D), lambda i: (0, 0))
    copies = [(row, jax.ShapeDtypeStruct((T, D), BF16))] if with_bf16 else []
    return pl.pallas_call(
        body, grid=(T // ROW_BLOCK,), in_specs=[row, row, vec, row],
        out_specs=[row, *[c[0] for c in copies], vec],
        out_shape=[jax.ShapeDtypeStruct((T, D), F32), *[c[1] for c in copies], jax.ShapeDtypeStruct((1, D), F32)],
        name=name, compiler_params=_cparams(("arbitrary",)),
    )(dh, x, g, dres)


def _final_loss(x3_halves, tgt, g, name):
    T, D = tgt.shape

    def body(xa_ref, xb_ref, t_ref, g_ref, loss_ref, dg_ref, dx_ref, dxb_ref):
        i = pl.program_id(0)
        xf = jnp.concatenate([xa_ref[...], xb_ref[...]], axis=1)
        r = lax.rsqrt(jnp.mean(xf * xf, axis=-1, keepdims=True) + EPS)
        xh = xf * r
        gg = g_ref[...]
        err = xh * gg - t_ref[...]

        @pl.when(i == 0)
        def _():
            dg_ref[...] = jnp.zeros_like(dg_ref)
            loss_ref[...] = jnp.zeros_like(loss_ref)

        part = jnp.sum(jnp.sum(err * err, axis=-1, keepdims=True), axis=0, keepdims=True) * (0.5 / D)
        loss_ref[...] += jnp.broadcast_to(part, loss_ref.shape)
        dout = err * (1.0 / D)
        dg_ref[...] += jnp.sum(dout * xh, axis=0, keepdims=True)
        dxh = dout * gg
        dx = r * (dxh - xh * jnp.mean(dxh * xh, axis=-1, keepdims=True))
        dx_ref[...] = dx
        dxb_ref[...] = dx.astype(BF16)

    row = pl.BlockSpec((ROW_BLOCK, D), lambda i: (i, 0))
    vec = pl.BlockSpec((1, D), lambda i: (0, 0))
    return pl.pallas_call(
        body, grid=(T // ROW_BLOCK,),
        in_specs=[pl.BlockSpec((ROW_BLOCK, D // 2), lambda i: (i, 0))] * 2 + [row, vec],
        out_specs=[pl.BlockSpec((1, 128), lambda i: (0, 0)), vec, row, row],
        out_shape=[jax.ShapeDtypeStruct((1, 128), F32), jax.ShapeDtypeStruct((1, D), F32),
                   jax.ShapeDtypeStruct((T, D), F32), jax.ShapeDtypeStruct((T, D), BF16)],
        name=name, compiler_params=_cparams(("arbitrary",)),
    )(*x3_halves, tgt, g)


CONV_BLOCK = 256


def _conv_apply(u, w, b):
    row = lax.broadcasted_iota(jnp.int32, u.shape, 0)
    acc = b + w[CONV_K - 1:CONV_K, :] * u
    shifted = []
    for j in range(1, CONV_K):
        uj = jnp.where(row >= j, pltpu.roll(u, j, axis=0), 0.0)
        shifted.append(uj)
        acc = acc + w[CONV_K - 1 - j:CONV_K - j, :] * uj
    return acc, shifted


def _conv_fwd(proj, conv_w, conv_b, name):
    T = proj.shape[0]
    cb0 = OFF_X // CONV_BLOCK

    def body(u_ref, w_ref, b_ref, o_ref):
        c, _ = _conv_apply(u_ref[...], w_ref[...], b_ref[...])
        o_ref[...] = c * _sigmoid(c)

    return pl.pallas_call(
        body, grid=(CONV_DIM // CONV_BLOCK,),
        in_specs=[pl.BlockSpec((T, CONV_BLOCK), lambda j: (0, cb0 + j)),
                  pl.BlockSpec((CONV_K, CONV_BLOCK), lambda j: (0, j)),
                  pl.BlockSpec((1, CONV_BLOCK), lambda j: (0, j))],
        out_specs=pl.BlockSpec((T, CONV_BLOCK), lambda j: (0, j)),
        out_shape=jax.ShapeDtypeStruct((T, CONV_DIM), F32), name=name, compiler_params=_cparams(("parallel",)),
    )(proj, conv_w, conv_b)


def _conv_bwd(proj, dact, conv_w, conv_b, dproj, name):
    T = proj.shape[0]
    cb0 = OFF_X // CONV_BLOCK

    def body(u_ref, d_ref, w_ref, b_ref, _, du_ref, dw_ref, db_ref):
        u = u_ref[...]
        w = w_ref[...]
        c, shifted = _conv_apply(u, w, b_ref[...])
        sg = _sigmoid(c)
        dc = d_ref[...] * sg * (1.0 + c * (1.0 - sg))
        row = lax.broadcasted_iota(jnp.int32, u.shape, 0)
        du = w[CONV_K - 1:CONV_K, :] * dc
        dw_ref[CONV_K - 1:CONV_K, :] = jnp.sum(dc * u, axis=0, keepdims=True)
        for j in range(1, CONV_K):
            dcj = jnp.where(row < T - j, pltpu.roll(dc, T - j, axis=0), 0.0)
            du = du + w[CONV_K - 1 - j:CONV_K - j, :] * dcj
            dw_ref[CONV_K - 1 - j:CONV_K - j, :] = jnp.sum(dc * shifted[j - 1], axis=0, keepdims=True)
        db_ref[...] = jnp.sum(dc, axis=0, keepdims=True)
        du_ref[...] = du.astype(BF16)

    return pl.pallas_call(
        body, grid=(CONV_DIM // CONV_BLOCK,),
        in_specs=[pl.BlockSpec((T, CONV_BLOCK), lambda j: (0, cb0 + j)),
                  pl.BlockSpec((T, CONV_BLOCK), lambda j: (0, j)),
                  pl.BlockSpec((CONV_K, CONV_BLOCK), lambda j: (0, j)),
                  pl.BlockSpec((1, CONV_BLOCK), lambda j: (0, j)), pl.BlockSpec(memory_space=pl.ANY)],
        out_specs=[pl.BlockSpec((T, CONV_BLOCK), lambda j: (0, cb0 + j)),
                   pl.BlockSpec((CONV_K, CONV_BLOCK), lambda j: (0, j)),
                   pl.BlockSpec((1, CONV_BLOCK), lambda j: (0, j))],
        out_shape=[jax.ShapeDtypeStruct(dproj.shape, BF16), jax.ShapeDtypeStruct((CONV_K, CONV_DIM), F32),
                   jax.ShapeDtypeStruct((1, CONV_DIM), F32)],
        input_output_aliases={4: 0}, name=name, compiler_params=_cparams(("parallel",)),
    )(proj, dact, conv_w, conv_b, dproj)


GROUP_W = D_INNER // N_GROUPS
HEADS_PER_GROUP = N_HEADS // N_GROUPS


def _expand_mat():
    h = lax.broadcasted_iota(jnp.int32, (N_HEADS, D_INNER), 0)
    j = lax.broadcasted_iota(jnp.int32, (N_HEADS, D_INNER), 1)
    return (j // HEAD_DIM == h).astype(F32)


def _reduce_mat(g):
    j = lax.broadcasted_iota(jnp.int32, (GROUP_W, N_HEADS), 0)
    h = lax.broadcasted_iota(jnp.int32, (GROUP_W, N_HEADS), 1)
    return (g * HEADS_PER_GROUP + j // HEAD_DIM == h).astype(F32)


def _col16(v, h):
    lane = lax.broadcasted_iota(jnp.int32, v.shape, 1)
    return jnp.sum(jnp.where(lane == h, v, 0.0), axis=1, keepdims=True)


def _ssd_pre(dt_raw, dtT_raw, dtb, dtbT, alog, alogT):
    Q = CHUNK
    xdt = dt_raw + dtb
    dt = _softplus(xdt)
    dtT = _softplus(dtT_raw + dtbT)
    A = -jnp.exp(alog)
    AT = -jnp.exp(alogT)
    row = lax.broadcasted_iota(jnp.int32, (Q, Q), 0)
    col = lax.broadcasted_iota(jnp.int32, (Q, Q), 1)
    tril = (row >= col).astype(F32)
    triu = (row <= col).astype(F32)
    cs = _hdot(tril, dt * A, "b")
    csT = _hdot(dtT * AT, triu, "a")
    return xdt, dt, A, cs, csT, row >= col, triu


def _decay_matrix(cs, csT, h, causal):
    seg = _col16(cs, h) - csT[h:h + 1, :]
    return jnp.where(causal, jnp.exp(jnp.minimum(seg, 0.0)), 0.0)


def _ssd_in_specs(nc, rev):
    def cidx(c):
        return (nc - 1 - c) if rev else c

    return [
        pl.BlockSpec((CHUNK, D_INNER), lambda c: (cidx(c), 0)),
        pl.BlockSpec((CHUNK, 512), lambda c: (cidx(c), 2)),
        pl.BlockSpec((CHUNK, 512), lambda c: (cidx(c), 3)),
        pl.BlockSpec((CHUNK, D_INNER), lambda c: (cidx(c), 0)),
        pl.BlockSpec((CHUNK, 128), lambda c: (cidx(c), OFF_DT // 128)),
        pl.BlockSpec((N_HEADS, CHUNK), lambda c: (0, cidx(c))),
        pl.BlockSpec((1, N_HEADS), lambda c: (0, 0)),
        pl.BlockSpec((N_HEADS, 1), lambda c: (0, 0)),
        pl.BlockSpec((1, N_HEADS), lambda c: (0, 0)),
        pl.BlockSpec((N_HEADS, 1), lambda c: (0, 0)),
        pl.BlockSpec((1, D_INNER), lambda c: (0, 0)),
        pl.BlockSpec((1, D_INNER), lambda c: (0, 0)),
    ]


def _ssd_fwd(xbc, proj, dtT, dtb, dtbT, alog, alogT, dfull, ng, name):
    T = xbc.shape[0]
    nc = T // CHUNK
    Q = CHUNK

    def body(xs_ref, B_ref, C_ref, z_ref, dt_ref, dtT_ref, dtb_ref, dtbT_ref, al_ref, alT_ref, df_ref, ng_ref,
             y_ref, ypre_ref, hs_ref, h_scr):
        c = pl.program_id(0)

        @pl.when(c == 0)
        def _():
            h_scr[...] = jnp.zeros_like(h_scr)

        _, dt, _, cs, csT, causal, _ = _ssd_pre(dt_ref[:, :N_HEADS], dtT_ref[...], dtb_ref[...], dtbT_ref[...],
                                                al_ref[...], alT_ref[...])
        ex = _expand_mat()
        dt_full = _hdot(dt, ex, "a")
        cs_full = _hdot(cs, ex, "a")
        cs_last = cs_full[Q - 1:Q, :]
        xs = xs_ref[...]
        xd = xs * dt_full
        e_full = jnp.exp(cs_full)
        dec_full = jnp.exp(cs_last - cs_full)
        cd_full = jnp.exp(cs_last)
        lane_head = lax.broadcasted_iota(jnp.int32, (1, GROUP_W), 1) // HEAD_DIM
        for g in range(N_GROUPS):
            sl = slice(g * GROUP_W, (g + 1) * GROUP_W)
            Bg = B_ref[:, g * D_STATE:(g + 1) * D_STATE].astype(BF16)
            Cg = C_ref[:, g * D_STATE:(g + 1) * D_STATE].astype(BF16)
            CB = _dot_nt(Cg, Bg)
            hg = h_scr[g]
            yoff = _dot_nn(Cg, hg.astype(BF16)) * e_full[:, sl]
            xd_g = xd[:, sl]
            S = _dot_tn(Bg, (xd_g * dec_full[:, sl]).astype(BF16))
            xd_b = xd_g.astype(BF16)
            ydiag = jnp.zeros((Q, GROUP_W), F32)
            for r in range(HEADS_PER_GROUP):
                Lm = _decay_matrix(cs, csT, g * HEADS_PER_GROUP + r, causal)
                Gm = (CB * Lm).astype(BF16)
                ydiag = ydiag + _dot_nn(Gm, jnp.where(lane_head == r, xd_b, jnp.zeros_like(xd_b)))
            hs_ref[0, g] = hg
            h_scr[g] = hg * cd_full[:, sl] + S
            ypre = ydiag + yoff + xs[:, sl] * df_ref[:, sl]
            ypre_ref[:, sl] = ypre
            zg = z_ref[:, sl]
            yz = ypre * zg * _sigmoid(zg)
            rn = lax.rsqrt(jnp.mean(yz * yz, axis=-1, keepdims=True) + EPS)
            y_ref[:, sl] = (yz * rn * ng_ref[:, sl]).astype(BF16)

    return pl.pallas_call(
        body, grid=(nc,), in_specs=_ssd_in_specs(nc, False),
        out_specs=[pl.BlockSpec((CHUNK, D_INNER), lambda c: (c, 0)),
                   pl.BlockSpec((CHUNK, D_INNER), lambda c: (c, 0)),
                   pl.BlockSpec((1, N_GROUPS, D_STATE, GROUP_W), lambda c: (c, 0, 0, 0))],
        out_shape=[jax.ShapeDtypeStruct((T, D_INNER + ATTN_W), BF16), jax.ShapeDtypeStruct((T, D_INNER), F32),
                   jax.ShapeDtypeStruct((nc, N_GROUPS, D_STATE, GROUP_W), F32)],
        scratch_shapes=[pltpu.VMEM((N_GROUPS, D_STATE, GROUP_W), F32)],
        name=name, compiler_params=_cparams(("arbitrary",)),
    )(xbc, xbc, xbc, proj, proj, dtT, dtb, dtbT, alog, alogT, dfull, ng)


def _ssd_bwd(xbc, proj, dtT, dtb, dtbT, alog, alogT, dfull, ng, ypre, hs, dy, name):
    T = xbc.shape[0]
    nc = T // CHUNK
    Q = CHUNK

    def body(xs_ref, B_ref, C_ref, z_ref, dt_ref, dtT_ref, dtb_ref, dtbT_ref, al_ref, alT_ref, df_ref, ng_ref,
             ypre_ref, hs_ref, dy_ref,
             dz_ref, dxbc_ref, ddtb_ref, dal_ref, dD_ref, dng_ref, dh_scr):
        step = pl.program_id(0)

        @pl.when(step == 0)
        def _():
            dh_scr[...] = jnp.zeros_like(dh_scr)
            ddtb_ref[...] = jnp.zeros_like(ddtb_ref)
            dal_ref[...] = jnp.zeros_like(dal_ref)
            dD_ref[...] = jnp.zeros_like(dD_ref)
            dng_ref[...] = jnp.zeros_like(dng_ref)

        xdt, dt, A, cs, csT, causal, triu = _ssd_pre(dt_ref[:, :N_HEADS], dtT_ref[...], dtb_ref[...],
                                                    dtbT_ref[...], al_ref[...], alT_ref[...])
        ex = _expand_mat()
        dt_full = _hdot(dt, ex, "a")
        cs_full = _hdot(cs, ex, "a")
        cs_last = cs_full[Q - 1:Q, :]
        xs = xs_ref[...]
        xd = xs * dt_full
        e_full = jnp.exp(cs_full)
        dec_full = jnp.exp(cs_last - cs_full)
        cd_full = jnp.exp(cs_last)
        lane_head = lax.broadcasted_iota(jnp.int32, (1, GROUP_W), 1) // HEAD_DIM
        is_last = lax.broadcasted_iota(jnp.int32, (Q, 1), 0) == Q - 1
        dcs16 = jnp.zeros((Q, N_HEADS), F32)
        ddtx16 = jnp.zeros((Q, N_HEADS), F32)
        dD16 = jnp.zeros((8, N_HEADS), F32)
        lane16 = lax.broadcasted_iota(jnp.int32, (1, N_HEADS), 1)
        sub16 = lax.broadcasted_iota(jnp.int32, (N_HEADS, 1), 0)
        col_sums = jnp.zeros((N_HEADS, Q), F32)
        for g in range(N_GROUPS):
            sl = slice(g * GROUP_W, (g + 1) * GROUP_W)
            red = _reduce_mat(g)
            ypre_g = ypre_ref[:, sl]
            zg = z_ref[:, sl]
            sg = _sigmoid(zg)
            silu = zg * sg
            yz = ypre_g * silu
            rn = lax.rsqrt(jnp.mean(yz * yz, axis=-1, keepdims=True) + EPS)
            yh = yz * rn
            dy_g = dy_ref[:, sl]
            dng_ref[:, sl] += jnp.sum(dy_g * yh, axis=0, keepdims=True)
            dyh = dy_g * ng_ref[:, sl]
            dyz = rn * (dyh - yh * jnp.mean(dyh * yh, axis=-1, keepdims=True))
            dY = dyz * silu
            dz_ref[:, sl] = (dyz * ypre_g * sg * (1.0 + zg * (1.0 - sg))).astype(BF16)
            xs_g = xs[:, sl]
            xd_g = xd[:, sl]
            dec_g = dec_full[:, sl]
            cd_g = cd_full[:, sl]
            d_g = df_ref[:, sl]
            Bg = B_ref[:, g * D_STATE:(g + 1) * D_STATE].astype(BF16)
            Cg = C_ref[:, g * D_STATE:(g + 1) * D_STATE].astype(BF16)
            CB = _dot_nt(Cg, Bg)
            hg = hs_ref[0, g]
            hgb = hg.astype(BF16)
            yoff = _dot_nn(Cg, hgb) * e_full[:, sl]
            dhn = dh_scr[g]
            dhnb = dhn.astype(BF16)
            dYE = (dY * e_full[:, sl]).astype(BF16)
            dC = _dot_nt(dYE, hgb)
            dh_direct = _dot_tn(Cg, dYE)
            dXdd = _dot_nn(Bg, dhnb)
            dB = _dot_nt((xd_g * dec_g).astype(BF16), dhnb)
            dcd = jnp.sum(dhn * hg, axis=0, keepdims=True)
            dh_scr[g] = dh_direct + cd_g * dhn
            dYb = dY.astype(BF16)
            xd_b = xd_g.astype(BF16)
            dCB = jnp.zeros((Q, Q), F32)
            dXd = dXdd * dec_g
            for r in range(HEADS_PER_GROUP):
                h = g * HEADS_PER_GROUP + r
                Lm = _decay_matrix(cs, csT, h, causal)
                Gf = CB * Lm
                dYr = jnp.where(lane_head == r, dYb, jnp.zeros_like(dYb))
                dG = _dot_nt(dYr, xd_b)
                dCB = dCB + dG * Lm
                dXd = dXd + _dot_tn(Gf.astype(BF16), dYr)
                Mm = dG * Gf
                dcs16 = dcs16 + jnp.where(lane16 == h, jnp.sum(Mm, axis=1, keepdims=True), 0.0)
                col_sums = col_sums + jnp.where(sub16 == h, jnp.sum(Mm, axis=0, keepdims=True), 0.0)
            dCBb = dCB.astype(BF16)
            dC = dC + _dot_nn(dCBb, Bg)
            dB = dB + _dot_tn(dCBb, Cg)
            w_state = dXdd * dec_g * xd_g
            t_last = jnp.sum(w_state, axis=0, keepdims=True) + dcd * cd_g
            dcs_g = dY * yoff - w_state + jnp.where(is_last, t_last, 0.0)
            dcs16 = dcs16 + _hdot(dcs_g, red, "a")
            ddtx16 = ddtx16 + _hdot(dXd * xs_g, red, "a")
            dD16 = dD16 + _hdot(jnp.broadcast_to(jnp.sum(dY * xs_g, axis=0, keepdims=True), (8, GROUP_W)), red, "a")
            dxbc_ref[:, sl] = dXd * dt_full[:, sl] + dY * d_g
            dxbc_ref[:, D_INNER + g * D_STATE:D_INNER + (g + 1) * D_STATE] = dB
            dxbc_ref[:, D_INNER + 512 + g * D_STATE:D_INNER + 512 + (g + 1) * D_STATE] = dC
        eye = (lax.broadcasted_iota(jnp.int32, (N_HEADS, N_HEADS), 0)
               == lax.broadcasted_iota(jnp.int32, (N_HEADS, N_HEADS), 1)).astype(BF16)
        dcs16 = dcs16 - sum(_dot_tn(part, eye) for part in _split3(col_sums))
        da = _hdot(triu, dcs16, "b")
        ddt = da * A + ddtx16
        ddt_raw = ddt * _sigmoid(xdt)
        pr = lax.broadcasted_iota(jnp.int32, (N_HEADS, 128), 0)
        pc = lax.broadcasted_iota(jnp.int32, (N_HEADS, 128), 1)
        dz_ref[:, D_INNER:OFF_DT] = jnp.zeros((Q, OFF_DT - D_INNER), BF16)
        dz_ref[:, OFF_DT:OFF_DT + 128] = _hdot(ddt_raw, (pr == pc).astype(F32), "a").astype(BF16)
        dz_ref[:, OFF_DT + 128:] = jnp.zeros((Q, NP - OFF_DT - 128), BF16)
        ddtb_ref[...] += jnp.sum(ddt_raw, axis=0, keepdims=True)
        dal_ref[...] += jnp.sum(da * dt, axis=0, keepdims=True) * A
        dD_ref[...] += dD16[0:1, :]

    def rc(c):
        return nc - 1 - c

    in_specs = _ssd_in_specs(nc, True) + [
        pl.BlockSpec((CHUNK, D_INNER), lambda c: (rc(c), 0)),
        pl.BlockSpec((1, N_GROUPS, D_STATE, GROUP_W), lambda c: (rc(c), 0, 0, 0)),
        pl.BlockSpec((CHUNK, D_INNER), lambda c: (rc(c), 0)),
    ]
    small = pl.BlockSpec((1, N_HEADS), lambda c: (0, 0))
    return pl.pallas_call(
        body, grid=(nc,), in_specs=in_specs,
        out_specs=[pl.BlockSpec((CHUNK, NP), lambda c: (rc(c), 0)),
                   pl.BlockSpec((CHUNK, CONV_DIM), lambda c: (rc(c), 0)),
                   small, small, small,
                   pl.BlockSpec((1, D_INNER), lambda c: (0, 0))],
        out_shape=[jax.ShapeDtypeStruct((T, NP), BF16), jax.ShapeDtypeStruct((T, CONV_DIM), F32),
                   jax.ShapeDtypeStruct((1, N_HEADS), F32), jax.ShapeDtypeStruct((1, N_HEADS), F32),
                   jax.ShapeDtypeStruct((1, N_HEADS), F32), jax.ShapeDtypeStruct((1, D_INNER), F32)],
        scratch_shapes=[pltpu.VMEM((N_GROUPS, D_STATE, GROUP_W), F32)],
        name=name, compiler_params=_cparams(("arbitrary",)),
    )(xbc, xbc, xbc, proj, proj, dtT, dtb, dtbT, alog, alogT, dfull, ng, ypre, hs, dy)


N_PAIRS = ATTN_W // 128
PAIRS_PER_KV = N_PAIRS // 2
ATTN_SCALE = HEAD_DIM ** -0.5


def _kv_variants(kk):
    lo = lax.broadcasted_iota(jnp.int32, kk.shape, 1) < HEAD_DIM
    zero = jnp.zeros_like(kk)
    k00 = jnp.where(lo, kk, zero)
    k11 = jnp.where(lo, zero, kk)
    k01 = pltpu.roll(k00, HEAD_DIM, axis=1)
    k10 = pltpu.roll(k11, HEAD_DIM, axis=1)
    return [[k00.astype(BF16), k01.astype(BF16)], [k10.astype(BF16), k11.astype(BF16)]]


LOG2E = 1.4426950408889634


def _own_block():
    i = lax.broadcasted_iota(jnp.int32, (WINDOW, WINDOW), 0)
    j = lax.broadcasted_iota(jnp.int32, (WINDOW, WINDOW), 1)
    return j <= i


def _fold(own, a):
    return jnp.where(own, a[:, WINDOW:], a[:, :WINDOW])


def _attn_probs(qp, kvar, own, prev_bias, sk):
    s = _dot_nt(qp, kvar)
    sb = jnp.where(own, s[:, WINDOW:], s[:, :WINDOW] + prev_bias) * (ATTN_SCALE * LOG2E)
    sk2 = sk * LOG2E
    m = jnp.maximum(jnp.max(sb, axis=1, keepdims=True), sk2)
    pe = jnp.exp2(sb - m)
    es = jnp.exp2(sk2 - m)
    den = jnp.sum(pe, axis=1, keepdims=True) + es
    inv = 1.0 / den
    return pe * inv, es * inv


def _unfold(own, a):
    zero = jnp.zeros_like(a)
    return jnp.where(own, zero, a), jnp.where(own, a, zero)


def _sink(sinks, r):
    lane = lax.broadcasted_iota(jnp.int32, sinks.shape, 1)
    return jnp.sum(jnp.where(lane == r, sinks, 0.0), axis=1, keepdims=True)


def _kv_specs():
    return [pl.BlockSpec((WINDOW, KV_W), lambda n: (jnp.maximum(n - 1, 0), OFF_K // KV_W)),
            pl.BlockSpec((WINDOW, KV_W), lambda n: (n, OFF_K // KV_W)),
            pl.BlockSpec((WINDOW, KV_W), lambda n: (jnp.maximum(n - 1, 0), OFF_V // KV_W)),
            pl.BlockSpec((WINDOW, KV_W), lambda n: (n, OFF_V // KV_W))]


def _attn_fwd(proj, sinks, og, ycat, name):
    T = proj.shape[0]
    nb = T // WINDOW

    def body(q_ref, kp_ref, kc_ref, vp_ref, vc_ref, s_ref, og_ref, _, y_ref, o_ref):
        n = pl.program_id(0)
        kv = _kv_variants(jnp.concatenate([kp_ref[...], kc_ref[...]], axis=0))
        vv = _kv_variants(jnp.concatenate([vp_ref[...], vc_ref[...]], axis=0))
        own = _own_block()
        prev_bias = jnp.where(n > 0, 0.0, NEG)
        sinks_v = s_ref[...]
        ssq = jnp.zeros((WINDOW, 1), F32)
        for p in range(N_PAIRS):
            j = p // PAIRS_PER_KV
            qp = q_ref[:, p * 128:(p + 1) * 128].astype(BF16)
            o_pair = jnp.zeros((WINDOW, 128), F32)
            for par in range(2):
                pn, _ = _attn_probs(qp, kv[j][par], own, prev_bias, _sink(sinks_v, 2 * p + par))
                p_prev, p_own = _unfold(own, pn.astype(BF16))
                o_pair = o_pair + _dot_nn(p_prev, vv[j][par][:WINDOW]) + _dot_nn(p_own, vv[j][par][WINDOW:])
            o_ref[:, p * 128:(p + 1) * 128] = o_pair
            ssq = ssq + jnp.sum(o_pair * o_pair, axis=1, keepdims=True)
        rn = lax.rsqrt(ssq * (1.0 / ATTN_W) + EPS)
        y_ref[...] = (o_ref[...] * rn * og_ref[...]).astype(BF16)

    return pl.pallas_call(
        body, grid=(nb,),
        in_specs=[pl.BlockSpec((WINDOW, ATTN_W), lambda n: (n, OFF_Q // ATTN_W)), *_kv_specs(),
                  pl.BlockSpec((1, N_HEADS), lambda n: (0, 0)), pl.BlockSpec((1, ATTN_W), lambda n: (0, 0)), ANY],
        out_specs=[pl.BlockSpec((WINDOW, ATTN_W), lambda n: (n, 1)), pl.BlockSpec((WINDOW, ATTN_W), lambda n: (n, 0))],
        out_shape=[jax.ShapeDtypeStruct(ycat.shape, BF16), jax.ShapeDtypeStruct((T, ATTN_W), F32)],
        input_output_aliases={7: 0}, name=name, compiler_params=_cparams(("parallel",)),
    )(proj, proj, proj, proj, proj, sinks, og, ycat)


def _attn_bwd(proj, sinks, og, o, dy, dproj, name):
    T = proj.shape[0]
    nb = T // WINDOW

    def body(q_ref, kp_ref, kc_ref, vp_ref, vc_ref, s_ref, og_ref, o_ref, dy_ref, _,
             dq_ref, dk_ref, dv_ref, ds_ref, dog_ref, qt_scr, dot_scr, ds_scr, p_scr):
        n = pl.program_id(0)

        @pl.when(n == 0)
        def _():
            dk_ref[...] = jnp.zeros_like(dk_ref)
            dv_ref[...] = jnp.zeros_like(dv_ref)
            ds_ref[...] = jnp.zeros_like(ds_ref)
            dog_ref[...] = jnp.zeros_like(dog_ref)

        kv = _kv_variants(jnp.concatenate([kp_ref[...], kc_ref[...]], axis=0))
        vv = _kv_variants(jnp.concatenate([vp_ref[...], vc_ref[...]], axis=0))
        own = _own_block()
        prev_bias = jnp.where(n > 0, 0.0, NEG)
        sinks_v = s_ref[...]
        of = o_ref[...]
        rn = lax.rsqrt(jnp.mean(of * of, axis=-1, keepdims=True) + EPS)
        oh = of * rn
        dyf = dy_ref[...]
        dog_ref[...] += jnp.sum(dyf * oh, axis=0, keepdims=True)
        doh = dyf * og_ref[...]
        do = rn * (doh - oh * jnp.mean(doh * oh, axis=-1, keepdims=True))
        lane = lax.broadcasted_iota(jnp.int32, (1, 128), 1)
        lane16 = lax.broadcasted_iota(jnp.int32, (1, N_HEADS), 1)
        dsink = jnp.zeros((1, N_HEADS), F32)
        for p in range(N_PAIRS):
            j = p // PAIRS_PER_KV
            q_f = q_ref[:, p * 128:(p + 1) * 128]
            qp = q_f.astype(BF16)
            q_t = q_f.T.astype(BF16)
            do_p = do[:, p * 128:(p + 1) * 128]
            o_p = of[:, p * 128:(p + 1) * 128]
            do_b = do_p.astype(BF16)
            do_t = do_p.T.astype(BF16)
            prod = do_p * o_p
            dq_pair = jnp.zeros((WINDOW, 128), F32)
            for par in range(2):
                r = 2 * p + par
                half = (lane < HEAD_DIM) if par == 0 else (lane >= HEAD_DIM)
                pn, ps = _attn_probs(qp, kv[j][par], own, prev_bias, _sink(sinks_v, r))
                delta = jnp.sum(jnp.where(half, prod, 0.0), axis=1, keepdims=True)
                dP = _fold(own, _dot_nt(do_b, vv[j][par]))
                dS = pn * (dP - delta)
                dsink = dsink + jnp.where(lane16 == r, -jnp.sum(ps * delta, axis=0, keepdims=True), 0.0)
                dS_parts = _unfold(own, (dS * ATTN_SCALE).astype(BF16))
                p_parts = _unfold(own, pn.astype(BF16))
                at = ((p % PAIRS_PER_KV) * 2 + par) * WINDOW
                qt_scr[j, :, at:at + WINDOW] = q_t[par * HEAD_DIM:(par + 1) * HEAD_DIM]
                dot_scr[j, :, at:at + WINDOW] = do_t[par * HEAD_DIM:(par + 1) * HEAD_DIM]
                for blk in range(2):
                    dq_pair = dq_pair + _dot_nn(dS_parts[blk], kv[j][par][blk * WINDOW:(blk + 1) * WINDOW])
                    ds_scr[j, blk, at:at + WINDOW, :] = dS_parts[blk]
                    p_scr[j, blk, at:at + WINDOW, :] = p_parts[blk]
            dq_ref[:, p * 128:(p + 1) * 128] = dq_pair.astype(BF16)
        rows = [pl.multiple_of(jnp.maximum(n - 1, 0) * WINDOW, WINDOW), pl.multiple_of(n * WINDOW, WINDOW)]
        for lhs, rhs, ref in [(qt_scr, ds_scr, dk_ref), (dot_scr, p_scr, dv_ref)]:
            for blk in range(2):
                both_t = jnp.concatenate([_dot_nn(lhs[j], rhs[j, blk]) for j in range(2)], axis=0)
                ref[pl.ds(rows[blk], WINDOW), :] += both_t.T
        ds_ref[...] += dsink

    full_kv = pl.BlockSpec((T, KV_W), lambda n: (0, 0))
    blk = pl.BlockSpec((WINDOW, ATTN_W), lambda n: (n, 0))
    return pl.pallas_call(
        body, grid=(nb,),
        in_specs=[pl.BlockSpec((WINDOW, ATTN_W), lambda n: (n, OFF_Q // ATTN_W)), *_kv_specs(),
                  pl.BlockSpec((1, N_HEADS), lambda n: (0, 0)), pl.BlockSpec((1, ATTN_W), lambda n: (0, 0)),
                  blk, pl.BlockSpec((WINDOW, ATTN_W), lambda n: (n, 1)), ANY],
        out_specs=[pl.BlockSpec((WINDOW, ATTN_W), lambda n: (n, OFF_Q // ATTN_W)), full_kv, full_kv,
                   pl.BlockSpec((1, N_HEADS), lambda n: (0, 0)), pl.BlockSpec((1, ATTN_W), lambda n: (0, 0))],
        out_shape=[jax.ShapeDtypeStruct(dproj.shape, BF16), jax.ShapeDtypeStruct((T, KV_W), F32),
                   jax.ShapeDtypeStruct((T, KV_W), F32), jax.ShapeDtypeStruct((1, N_HEADS), F32),
                   jax.ShapeDtypeStruct((1, ATTN_W), F32)],
        scratch_shapes=[pltpu.VMEM((2, HEAD_DIM, 8 * WINDOW), BF16), pltpu.VMEM((2, HEAD_DIM, 8 * WINDOW), BF16),
                        pltpu.VMEM((2, 2, 8 * WINDOW, WINDOW), BF16), pltpu.VMEM((2, 2, 8 * WINDOW, WINDOW), BF16)],
        input_output_aliases={9: 0}, name=name, compiler_params=_cparams(("arbitrary",)),
    )(proj, proj, proj, proj, proj, sinks, og, o, dy, dproj)


ANY = pl.BlockSpec(memory_space=pl.ANY)


def _coords():
    return lax.axis_index("x"), lax.axis_index("y"), lax.axis_index("c")


HBM = pl.BlockSpec(memory_space=pltpu.HBM)
SEM = pl.BlockSpec(memory_space=pltpu.SEMAPHORE)
EFFECT = pltpu.SideEffectType.DATAFLOW_SIDE_EFFECTING


def _in_hbm(a):
    return pltpu.with_memory_space_constraint(a, pltpu.HBM)


def _remote_start(srcs, lands, plan, n_copies, name, after=None):
    ns, nb = len(srcs), len(srcs) + len(lands)
    n_after = 0 if after is None else 1

    def body(*refs):
        src_refs, land_refs = refs[:ns], refs[ns:nb]
        send_sems, recv_sems = refs[nb + n_after], refs[nb + n_after + 1]
        token = refs[-1]
        x, y, c = _coords()
        for i, (sv, dv, dev) in enumerate(plan(src_refs, land_refs, x, y, c)):
            pltpu.make_async_remote_copy(src_ref=sv, dst_ref=dv, send_sem=send_sems.at[i], recv_sem=recv_sems.at[i],
                                         device_id=dev, device_id_type=MESH).start()
        token[...] = jnp.zeros_like(token)

    bufs = list(srcs) + list(lands)
    outs = pl.pallas_call(
        body, name=name,
        out_shape=(pltpu.SemaphoreType.DMA((n_copies,)), pltpu.SemaphoreType.DMA((n_copies,)),
                   *[pltpu.HBM(b.shape, b.dtype) for b in bufs], jax.ShapeDtypeStruct((8, 128), F32)),
        in_specs=[HBM] * nb + [ANY] * n_after,
        out_specs=(SEM, SEM, *[HBM] * nb, pl.BlockSpec(memory_space=pltpu.VMEM)),
        input_output_aliases={i: 2 + i for i in range(nb)},
        compiler_params=pltpu.CompilerParams(has_side_effects=EFFECT),
    )(*[_in_hbm(b) for b in bufs], *([] if after is None else [after]))
    return outs[0], outs[1], list(outs[2:2 + ns]), list(outs[2 + ns:2 + nb]), outs[-1]


def _remote_wait(started, after, plan, name):
    send_sems, recv_sems, srcs, lands, _ = started
    ns, nb = len(srcs), len(srcs) + len(lands)

    def body(*refs):
        src_refs, land_refs = refs[:ns], refs[ns:nb]
        send_sems, recv_sems = refs[nb], refs[nb + 1]
        x, y, c = _coords()
        for i, (sv, dv, dev) in enumerate(plan(src_refs, land_refs, x, y, c)):
            cp = pltpu.make_async_remote_copy(src_ref=sv, dst_ref=dv, send_sem=send_sems.at[i],
                                              recv_sem=recv_sems.at[i], device_id=dev, device_id_type=MESH)
            cp.wait_send()
            cp.wait_recv()

    bufs = list(srcs) + list(lands)
    outs = pl.pallas_call(
        body, name=name, out_shape=tuple(pltpu.HBM(b.shape, b.dtype) for b in bufs),
        in_specs=[HBM] * nb + [SEM, SEM, ANY], out_specs=tuple([HBM] * nb),
        input_output_aliases={i: i for i in range(nb)},
        compiler_params=pltpu.CompilerParams(has_side_effects=EFFECT),
    )(*bufs, send_sems, recv_sems, after)
    return list(outs[:ns]), list(outs[ns:])


def _pair_plan(src_refs, land_refs, x, y, c):
    plan = []
    for s, l in zip(src_refs, land_refs):
        for q in range(4):
            plan.append((s.at[2 * q + (1 - c)], l.at[q], (x, y, 1 - c)))
    return plan


def _pair4_plan(src_refs, land_refs, x, y, c):
    plan = []
    for s, l in zip(src_refs, land_refs):
        for q in range(4):
            plan.append((s.at[q], l.at[q], (x, y, 1 - c)))
    return plan


def _chips_plan(src_refs, land_refs, x, y, c):
    plan = []
    for s, l in zip(src_refs, land_refs):
        for k, (tx, ty) in enumerate([(1 - x, y), (x, 1 - y), (1 - x, 1 - y)]):
            plan.append((s.at[2 * tx + ty], l.at[k], (tx, ty, c)))
    return plan


def _everyone_plan(src_refs, land_refs, x, y, c):
    me = 4 * x + 2 * y + c
    plan = []
    for s, l in zip(src_refs, land_refs):
        for fx, fy, fc in [(0, 0, 1), (1, 0, 0), (1, 0, 1), (0, 1, 0), (0, 1, 1), (1, 1, 0), (1, 1, 1)]:
            dev = ((1 - x) if fx else x, (1 - y) if fy else y, (1 - c) if fc else c)
            plan.append((s, l.at[me], dev))
    return plan


def _pair_add(g8, r1, csel, tr, name):
    _, R, C = r1.shape
    g4 = g8.reshape(4, 2, R, C)

    def body(c_ref, g_ref, r_ref, o_ref):
        o_ref[...] = (g_ref[...].astype(F32) + r_ref[...].astype(F32)).astype(BF16)

    return pl.pallas_call(
        body,
        grid_spec=pltpu.PrefetchScalarGridSpec(
            num_scalar_prefetch=1, grid=(4, R // tr),
            in_specs=[pl.BlockSpec((None, None, tr, C), lambda q, i, cs: (q, cs[0], i, 0)),
                      pl.BlockSpec((None, tr, C), lambda q, i, cs: (q, i, 0))],
            out_specs=pl.BlockSpec((None, tr, C), lambda q, i, cs: (q, i, 0))),
        out_shape=jax.ShapeDtypeStruct((4, R, C), BF16), name=name,
        compiler_params=_cparams(("parallel", "parallel")),
    )(csel, g4, r1)


def _adamw_math(w, g, m, v):
    m = ADAM_B1 * m + (1.0 - ADAM_B1) * g
    v = ADAM_B2 * v + (1.0 - ADAM_B2) * (g * g)
    m_hat = m / (1.0 - ADAM_B1 ** ADAM_STEP)
    v_hat = v / (1.0 - ADAM_B2 ** ADAM_STEP)
    delta = -ADAM_LR * (m_hat / (jnp.sqrt(v_hat) + ADAM_EPS) + ADAM_WD * w)
    return delta, m, v


def _adamw_big(w, m, v, p4, r3, qsel, tile, name):
    R, C = w.shape
    tr, tc = tile

    def body(q_ref, w_ref, m_ref, v_ref, p_ref, r_ref, g_out, d_out, m_out, v_out):
        g = p_ref[...].astype(F32) + r_ref[0].astype(F32) + r_ref[1].astype(F32) + r_ref[2].astype(F32)
        d, mn, vn = _adamw_math(w_ref[...], g, m_ref[...], v_ref[...])
        g_out[...] = g
        d_out[...] = d
        m_out[...] = mn
        v_out[...] = vn

    blk = pl.BlockSpec((tr, tc), lambda i, j, qs: (i, j))
    return pl.pallas_call(
        body,
        grid_spec=pltpu.PrefetchScalarGridSpec(
            num_scalar_prefetch=1, grid=(R // tr, C // tc),
            in_specs=[blk, blk, blk, pl.BlockSpec((None, tr, tc), lambda i, j, qs: (qs[0], i, j)),
                      pl.BlockSpec((3, tr, tc), lambda i, j, qs: (0, i, j))],
            out_specs=[blk, blk, blk, blk]),
        out_shape=[jax.ShapeDtypeStruct((R, C), F32)] * 4, name=name,
        compiler_params=_cparams(("parallel", "parallel")),
    )(qsel, w, m, v, p4, r3)


def _sum_partials(p4, r3, qsel, tc, name):
    _, R, C = p4.shape

    def body(q_ref, p_ref, r_ref, o_ref):
        o_ref[...] = p_ref[...].astype(F32) + r_ref[0].astype(F32) + r_ref[1].astype(F32) + r_ref[2].astype(F32)

    return pl.pallas_call(
        body,
        grid_spec=pltpu.PrefetchScalarGridSpec(
            num_scalar_prefetch=1, grid=(C // tc,),
            in_specs=[pl.BlockSpec((None, R, tc), lambda j, qs: (qs[0], 0, j)),
                      pl.BlockSpec((3, R, tc), lambda j, qs: (0, 0, j))],
            out_specs=pl.BlockSpec((R, tc), lambda j, qs: (0, j))),
        out_shape=jax.ShapeDtypeStruct((R, C), F32), name=name, compiler_params=_cparams(("parallel",)),
    )(qsel, p4, r3)


def _adamw_tiled(w, g, m, v, tc, name):
    R, C = w.shape

    def body(w_ref, g_ref, m_ref, v_ref, d_out, m_out, v_out):
        d, mn, vn = _adamw_math(w_ref[...], g_ref[...], m_ref[...], v_ref[...])
        d_out[...] = d
        m_out[...] = mn
        v_out[...] = vn

    blk = pl.BlockSpec((R, tc), lambda j: (0, j))
    return pl.pallas_call(
        body, grid=(C // tc,), in_specs=[blk] * 4, out_specs=[blk] * 3,
        out_shape=[jax.ShapeDtypeStruct((R, C), F32)] * 3, name=name, compiler_params=_cparams(("parallel",)),
    )(w, g, m, v)


def _small_sum(parts, name):
    def body(p_ref, o_ref):
        acc = p_ref[0]
        for d in range(1, N_DEV):
            acc = acc + p_ref[d]
        o_ref[...] = acc

    return pl.pallas_call(
        body, out_shape=jax.ShapeDtypeStruct(parts.shape[1:], F32), name=name,
        compiler_params=_cparams(),
    )(parts)


def _adamw_small(w, g, m, v, name):
    def body(w_ref, g_ref, m_ref, v_ref, d_out, m_out, v_out):
        d, mn, vn = _adamw_math(w_ref[...], g_ref[...], m_ref[...], v_ref[...])
        d_out[...] = d
        m_out[...] = mn
        v_out[...] = vn

    return pl.pallas_call(
        body, out_shape=[jax.ShapeDtypeStruct(w.shape, F32)] * 3, name=name, compiler_params=_cparams(),
    )(w, g, m, v)


def _row(*pieces):
    r = jnp.concatenate([p.reshape(1, -1) for p in pieces], axis=1)
    return jnp.pad(r, ((0, 0), (0, D_MODEL - r.shape[1])))


def _pack_small(mix, convb, ssmg, attng, mlpg, fing, convw, dtb, alog, dsk, sinks, extra=None):
    last = [dtb, alog, dsk, sinks] + ([extra] if extra is not None else [])
    rows = [_row(mix), _row(convb), _row(ssmg, attng), _row(mlpg), _row(fing),
            jnp.pad(convw, ((0, 0), (0, D_MODEL - convw.shape[1]))), _row(*last)]
    packed = jnp.concatenate(rows, axis=0)
    return jnp.pad(packed, ((0, SMALL_ROWS - packed.shape[0]), (0, 0)))


def _unpack_small(p, conv_n):
    return dict(
        mix_norm_g=p[0:1, :], conv_b=p[1:2, :], ssm_norm_g=p[2:3, :D_INNER], attn_out_norm_g=p[2:3, D_INNER:],
        mlp_norm_g=p[3:4, :], final_norm_g=p[4, :], conv_w=p[5:9, :conv_n][None],
        dt_bias=p[9:10, 0:16], A_log=p[9:10, 16:32], D_skip=p[9:10, 32:48], attn_sinks=p[9:10, 48:64])


WEIGHT_ORDER = ["mix_norm_g", "w_in", "conv_w", "conv_b", "dt_bias", "A_log", "D_skip", "ssm_norm_g", "attn_sinks",
                "attn_out_norm_g", "w_out", "mlp_norm_g", "w_up", "w_down", "final_norm_g"]


def _to_my_columns(w_nat):
    pad = jnp.zeros((w_nat.shape[0], NP - IN_PROJ), w_nat.dtype)
    return jnp.concatenate([w_nat[:, :NAT_DT], w_nat[:, NAT_DT + N_HEADS:], w_nat[:, NAT_DT:NAT_DT + N_HEADS], pad],
                           axis=1)


PER = IN_PROJ // N_DEV
SUPER_STEP = 544
SUPER = 576


def _natural_rows(g, lo, hi):
    segments = [(0, NAT_DT, 0), (NAT_DT, NAT_DT + N_HEADS, OFF_DT - NAT_DT), (NAT_DT + N_HEADS, IN_PROJ, -N_HEADS),
                (IN_PROJ, NP, 0)]
    pieces = [g[max(lo, a) + shift:min(hi, b) + shift] for a, b, shift in segments if max(lo, a) < min(hi, b)]
    return pieces[0] if len(pieces) == 1 else jnp.concatenate(pieces, axis=0)


def _w_in_from_super_slabs(sup):
    seam = SUPER - SUPER_STEP
    units = []
    for i in range(N_DEV):
        base = SUPER_STEP * i
        units.append((base, base + seam, sup[i, :seam] if i == 0 else sup[i - 1, SUPER_STEP:] + sup[i, :seam]))
        units.append((base + seam, base + SUPER_STEP, sup[i, seam:SUPER_STEP]))
    units.append((SUPER_STEP * N_DEV, SUPER_STEP * N_DEV + seam, sup[N_DEV - 1, SUPER_STEP:]))

    def natural(lo, hi):
        return [rows[max(lo, a) - a:min(hi, b) - a] for a, b, rows in units if max(lo, a) < min(hi, b)]

    pieces = natural(0, NAT_DT) + natural(NAT_DT + N_HEADS, IN_PROJ) + natural(NAT_DT, NAT_DT + N_HEADS)
    return jnp.concatenate(pieces + [jnp.zeros((NP - IN_PROJ, D_MODEL), sup.dtype)], axis=0)


def _to_natural_columns(w_my):
    return jnp.concatenate([w_my[:, :NAT_DT], w_my[:, OFF_DT:OFF_DT + N_HEADS], w_my[:, NAT_DT:OFF_DT]], axis=1)


SLAB = 1024


def _grad_w_up(h2, du, name, sel=None, add=None, after=None):
    T, D = h2.shape
    if sel is None:
        pick, n_slab, pre = (lambda j, *cs: j), N_DEV, None
    else:
        pre, other = sel
        pick, n_slab = (lambda j, cs: 2 * j + ((1 - cs[0]) if other else cs[0])), 4
    o_spec = pl.BlockSpec((None, SLAB, SLAB), lambda i, j, k, *cs: (j, i, 0))
    return _matmul(
        h2, du, mode="tn", grid=(D // SLAB, n_slab, 1),
        a_spec=pl.BlockSpec((T, SLAB), lambda i, j, k, *cs: (0, i)),
        b_spec=pl.BlockSpec((T, SLAB), lambda i, j, k, *cs: (0, pick(j, *cs))),
        out_shapes=[jax.ShapeDtypeStruct((n_slab, D, SLAB), BF16)], out_specs=[o_spec], tile=(SLAB, SLAB), name=name,
        extras=() if add is None else (add,), extra_specs=() if add is None else (o_spec,),
        epilogue=None if add is None else (lambda acc, r: (acc + r.astype(F32),)), after=after, prefetch=pre)[0]


def _grad_w_down(act, dx3b, name, sel=None, add=None, after=None):
    T, D = dx3b.shape
    if sel is None:
        pick, n_slab, pre = (lambda i, *cs: i), N_DEV, None
    else:
        pre, other = sel
        pick, n_slab = (lambda i, cs: 2 * i + ((1 - cs[0]) if other else cs[0])), 4
    o_spec = pl.BlockSpec((None, SLAB, SLAB), lambda i, j, k, *cs: (i, 0, j))
    return _matmul(
        act, dx3b, mode="tn", grid=(n_slab, D // SLAB, 1),
        a_spec=pl.BlockSpec((T, SLAB), lambda i, j, k, *cs: (0, pick(i, *cs))),
        b_spec=pl.BlockSpec((T, SLAB), lambda i, j, k, *cs: (0, j)),
        out_shapes=[jax.ShapeDtypeStruct((n_slab, SLAB, D), BF16)], out_specs=[o_spec], tile=(SLAB, SLAB), name=name,
        extras=() if add is None else (add,), extra_specs=() if add is None else (o_spec,),
        epilogue=None if add is None else (lambda acc, r: (acc + r.astype(F32),)), after=after, prefetch=pre)[0]


class _FixedWeights:
    def __init__(self, w_in_p, w_out_f, w_up_s, w_down_f, conv_w_f):
        self.w = (w_in_p, w_out_f, w_up_s, w_down_f, conv_w_f)
        self.grads = {}

    def mixer_weights(self, after):
        return self.w[0], None

    def conv_weight(self, after):
        return self.w[4]

    def out_weight(self, after):
        return self.w[1]

    def up_weight(self, after):
        return self.w[2]

    def down_weight(self, h, after):
        return self.w[3][:, h * (D_MODEL // 2):(h + 1) * (D_MODEL // 2)]

    def mlp_grads(self, h2, du, act, dx3b):
        self.grads.update(w_up=_grad_w_up(h2, du, "grad_w_up"),
                          w_down=_grad_w_down(act, dx3b, "grad_w_down").reshape(D_FF, D_MODEL))
        return None

    def grad_sent(self, tag, after):
        return None

    def out_grad(self, g_out):
        self.grads.update(w_out=g_out)
        return None

    def in_grad(self, g_in):
        self.grads.update(w_in=g_in)
        return None


def _local_step(x, tgt, p, hooks):
    T = x.shape[0]
    D = D_MODEL
    h1 = _rmsnorm_fwd(x, p["mix_norm_g"], "norm_mix")
    w_in_t, token = hooks.mixer_weights(h1)
    (proj,) = _mm_simple(h1, w_in_t, mode="nt", M=T, N=NP, K=D, tm=min(T, 1024), tn=1536, tk=D, out_dtype=F32,
                         name="in_proj", after=token)
    conv_w_f = hooks.conv_weight(proj)
    xbc = _conv_fwd(proj, conv_w_f, p["conv_b"], "conv_fwd")
    dtT = proj[:, OFF_DT:OFF_DT + N_HEADS].T
    dtbT = p["dt_bias"].T
    alogT = p["A_log"].T
    dfull = jnp.repeat(p["D_skip"], HEAD_DIM, axis=1)
    ycat, ypre, hs = _ssd_fwd(xbc, proj, dtT, p["dt_bias"], dtbT, p["A_log"], alogT, dfull, p["ssm_norm_g"],
                              "ssd_fwd")
    ycat, o_att = _attn_fwd(proj, p["attn_sinks"], p["attn_out_norm_g"], ycat, "attn_fwd")
    w_out_f = hooks.out_weight(ycat)
    tm = min(T, 1024)
    def residual_and_norm(acc, res, gain):
        x2 = acc + res
        return x2, x2 * lax.rsqrt(jnp.mean(x2 * x2, axis=-1, keepdims=True) + EPS) * gain

    rows = min(T, 512)
    x2, h2 = _matmul(
        ycat, w_out_f, mode="nn", grid=(T // rows, 1, 1),
        a_spec=pl.BlockSpec((rows, D), lambda i, j, k: (i, 0)), b_spec=pl.BlockSpec((D, D), lambda i, j, k: (0, 0)),
        out_shapes=[jax.ShapeDtypeStruct((T, D), F32), jax.ShapeDtypeStruct((T, D), BF16)],
        out_specs=[pl.BlockSpec((rows, D), lambda i, j, k: (i, 0))] * 2, tile=(rows, D), name="out_proj",
        extras=(x, p["mlp_norm_g"]),
        extra_specs=[pl.BlockSpec((rows, D), lambda i, j, k: (i, 0)), pl.BlockSpec((1, D), lambda i, j, k: (0, 0))],
        epilogue=residual_and_norm)
    w_up_s = hooks.up_weight(h2)
    grid = (T // tm, N_DEV, 1)
    u, act = _matmul(
        h2, w_up_s, mode="nn", grid=grid,
        a_spec=pl.BlockSpec((tm, D), lambda i, j, k: (i, 0)),
        b_spec=pl.BlockSpec((None, D, 1024), lambda i, j, k: (j, 0, 0)),
        out_shapes=[jax.ShapeDtypeStruct((T, D_FF), F32), jax.ShapeDtypeStruct((T, D_FF), BF16)],
        out_specs=[pl.BlockSpec((tm, 1024), lambda i, j, k: (i, j))] * 2, tile=(tm, 1024), name="mlp_up",
        epilogue=lambda acc: (acc, jnp.square(jnp.maximum(acc, 0.0))))
    half = D // 2
    w_down_halves, x3_halves = [], []
    for h in range(2):
        w_down_halves.append(hooks.down_weight(h, act if h == 0 else x3_halves[0]))
        x3_halves.append(_matmul(
            act, w_down_halves[h], mode="nn", grid=(T // tm, 1, D_FF // 2048),
            a_spec=pl.BlockSpec((tm, 2048), lambda i, j, k: (i, k)),
            b_spec=pl.BlockSpec((2048, half), lambda i, j, k: (k, 0)),
            out_shapes=[jax.ShapeDtypeStruct((T, half), F32)],
            out_specs=[pl.BlockSpec((tm, half), lambda i, j, k: (i, 0))], tile=(tm, half), name=f"mlp_down_{h}",
            extras=(x2,), extra_specs=[pl.BlockSpec((tm, half), lambda i, j, k, h=h: (i, h))],
            epilogue=lambda acc, res: (acc + res,))[0])
    loss_part, d_fin, dx3, dx3b = _final_loss(x3_halves, tgt, p["final_norm_g"].reshape(1, D), "loss_head")
    (du,) = _matmul(
        dx3b, tuple(w_down_halves), mode="nt", grid=(T // tm, D_FF // 1024, 1),
        a_spec=pl.BlockSpec((tm, D), lambda i, j, k: (i, 0)),
        b_spec=(pl.BlockSpec((1024, half), lambda i, j, k: (j, 0)),) * 2,
        out_shapes=[jax.ShapeDtypeStruct((T, D_FF), BF16)],
        out_specs=[pl.BlockSpec((tm, 1024), lambda i, j, k: (i, j))], tile=(tm, 1024), name="mlp_down_bwd",
        extras=(u,), extra_specs=[pl.BlockSpec((tm, 1024), lambda i, j, k: (i, j))],
        epilogue=lambda acc, uu: (acc * (2.0 * jnp.maximum(uu, 0.0)),),
        dot_fn=lambda a, b0, b1: _dot_nt(a[:, :half], b0) + _dot_nt(a[:, half:], b1))
    token = hooks.mlp_grads(h2, du, act, dx3b)
    (dh2,) = _matmul(
        du, w_up_s, mode="nt", grid=(T // tm, D // 1024, N_DEV // 2),
        a_spec=pl.BlockSpec((tm, 2048), lambda i, j, k: (i, k)),
        b_spec=pl.BlockSpec((2, 1024, 1024), lambda i, j, k: (k, j, 0)),
        out_shapes=[jax.ShapeDtypeStruct((T, D), F32)],
        out_specs=[pl.BlockSpec((tm, 1024), lambda i, j, k: (i, j))], tile=(tm, 1024), name="mlp_up_bwd",
        after=token, dot_fn=lambda a, b: _dot_nt(a[:, :1024], b[0]) + _dot_nt(a[:, 1024:], b[1]))
    dx2, dx2b, d_mlp = _rmsnorm_bwd(dh2, x2, p["mlp_norm_g"], dx3, "norm_mlp_bwd")
    (g_out,) = _mm_simple(ycat, dx2b, mode="tn", M=D, N=D, K=T, tm=1024, tn=1024, tk=T, out_dtype=BF16,
                          name="grad_w_out")
    token = hooks.out_grad(g_out)
    (dy,) = _mm_simple(dx2b, w_out_f, mode="nt", M=T, N=D, K=D, tm=tm, tn=1024, tk=D, out_dtype=F32,
                       name="out_proj_bwd", after=token)
    token = hooks.grad_sent("out", dy)
    ssm_g = p["ssm_norm_g"] if token is None else p["ssm_norm_g"] + token[0:1, 0:1]
    dproj, dxbc_act, d_dtb, d_alog, d_dskip, d_ssmg = _ssd_bwd(
        xbc, proj, dtT, p["dt_bias"], dtbT, p["A_log"], alogT, dfull, ssm_g, ypre, hs, dy, "ssd_bwd")
    dproj, d_convw, d_convb = _conv_bwd(proj, dxbc_act, conv_w_f, p["conv_b"], dproj, "conv_bwd")
    dproj, dk, dv, d_sinks, d_attng = _attn_bwd(proj, p["attn_sinks"], p["attn_out_norm_g"], o_att, dy, dproj,
                                                "attn_bwd")
    dproj = lax.dynamic_update_slice(dproj, jnp.concatenate([dk, dv], axis=1).astype(BF16), (0, OFF_K))
    (g_in,) = _mm_simple(dproj, h1, mode="tn", M=NP, N=D, K=T, tm=1536, tn=1024, tk=T, out_dtype=BF16,
                         name="grad_w_in")
    token = hooks.in_grad(g_in)
    (dh1,) = _mm_simple(dproj, w_in_t, mode="nn", M=T, N=D, K=NP, tm=tm, tn=1024, tk=2304, out_dtype=F32,
                        name="in_proj_bwd", after=token)
    token = hooks.grad_sent("in", dh1)
    mix_g = p["mix_norm_g"] if token is None else p["mix_norm_g"] + token[0:1, 0:1]
    dx, d_mix = _rmsnorm_bwd(dh1, x, mix_g, dx2, "norm_mix_bwd", with_bf16=False)
    small = _pack_small(d_mix, d_convb, d_ssmg, d_attng, d_mlp, d_fin, d_convw, d_dtb, d_alog, d_dskip, d_sinks,
                        extra=loss_part[:, 0:1])
    return dx, small


def _rows_rotated(v, shift, name):
    R, C = v.shape
    tc = 512

    def body(s_ref, v_ref, o_ref):
        o_ref[...] = pltpu.roll(v_ref[...], s_ref[0], axis=0).astype(BF16)

    return pl.pallas_call(
        body,
        grid_spec=pltpu.PrefetchScalarGridSpec(
            num_scalar_prefetch=1, grid=(C // tc,), in_specs=[pl.BlockSpec((R, tc), lambda j, s: (0, j))],
            out_specs=pl.BlockSpec((R, tc), lambda j, s: (0, j))),
        out_shape=jax.ShapeDtypeStruct((R, C), BF16), name=name, compiler_params=_cparams(("parallel",)),
    )(shift, v)


def _landing(own, me):
    zone = lax.empty((N_DEV,) + own.shape, own.dtype)
    return lax.dynamic_update_slice(zone, own[None], (me,) + (0,) * own.ndim)


def _sequencer_gather(owns, split, me, collective_id, name):
    n = len(owns)
    zone_refs = [jax.new_ref(_landing(o, me), memory_space=pltpu.MemorySpace.HBM) for o in owns]
    own_refs = [jax.new_ref(o, memory_space=pltpu.MemorySpace.HBM) for o in owns]
    N_COPIES = 9

    @pl.kernel(mesh=plsc.ScalarSubcoreMesh(axis_name="sequencer", num_cores=1), name=name,
               scratch_types=(pltpu.SemaphoreType.DMA((n, N_COPIES)), pltpu.SemaphoreType.DMA((n, N_COPIES))),
               compiler_params=pltpu.CompilerParams(collective_id=collective_id))
    def launch(send_sems, recv_sems):
        x, y, c = _coords()
        sibling, xn, yn, diag = (x, y, 1 - c), (1 - x, y, c), (x, 1 - y, c), (1 - x, 1 - y, c)
        barrier = pltpu.get_barrier_semaphore()
        for peer in [sibling, xn, yn, diag]:
            pl.semaphore_signal(barrier, inc=1, device_id=peer, device_id_type=MESH)
        pl.semaphore_wait(barrier, 4)

        def block(a, dev, half=None):
            ref = zone_refs[a].at[4 * dev[0] + 2 * dev[1] + dev[2]]
            if half is None:
                return ref
            rows = owns[a].shape[0] // 2
            return ref.at[pl.ds(half * rows, rows)]

        def copy(a, k, src, dst, to):
            return pltpu.make_async_remote_copy(src_ref=src, dst_ref=dst, send_sem=send_sems.at[a, k],
                                                recv_sem=recv_sems.at[a, k], device_id=to, device_id_type=MESH)

        me_dev = (x, y, c)
        sent = []
        first = {}
        for a in range(n):
            for k, peer in enumerate([sibling, xn, yn] + ([] if split[a] else [diag])):
                first[a, k] = copy(a, k, own_refs[a], block(a, me_dev), peer)
                first[a, k].start()
                sent.append(first[a, k])
        from_sibling = []
        for a in range(n):
            first[a, 1].wait_recv()
            sent.append(copy(a, 4, block(a, xn), block(a, xn), sibling))
            if split[a]:
                sent.append(copy(a, 6, block(a, xn, 0), block(a, xn, 0), yn))
            first[a, 2].wait_recv()
            sent.append(copy(a, 5, block(a, yn), block(a, yn), sibling))
            if split[a]:
                sent.append(copy(a, 7, block(a, yn, 1), block(a, yn, 1), xn))
            for cp in sent[-(4 if split[a] else 2):]:
                cp.start()
        for a in range(n):
            if split[a]:
                copy(a, 6, block(a, diag, 0), block(a, diag, 0), yn).wait_recv()
                sent.append(copy(a, 8, block(a, diag, 0), block(a, diag, 0), sibling))
                sent[-1].start()
                copy(a, 7, block(a, diag, 1), block(a, diag, 1), xn).wait_recv()
                sent.append(copy(a, 3, block(a, diag, 1), block(a, diag, 1), sibling))
                sent[-1].start()
            else:
                first[a, 3].wait_recv()
                sent.append(copy(a, 8, block(a, diag), block(a, diag), sibling))
                sent[-1].start()
        for a in range(n):
            first[a, 0].wait_recv()
            copy(a, 4, block(a, xn), block(a, xn), sibling).wait_recv()
            copy(a, 5, block(a, yn), block(a, yn), sibling).wait_recv()
            if split[a]:
                copy(a, 8, block(a, diag, 0), block(a, diag, 0), sibling).wait_recv()
                copy(a, 3, block(a, diag, 1), block(a, diag, 1), sibling).wait_recv()
            else:
                copy(a, 8, block(a, diag), block(a, diag), sibling).wait_recv()
        for cp in sent:
            cp.wait_send()

    launch()
    return zone_refs


class _ShardedWeights:
    def __init__(self, w_in, w_out, conv_w, w_up, w_down, me, csel):
        self.me, self.csel = me, csel
        padded = jnp.pad(jnp.transpose(w_in), ((0, SUPER - PER), (0, 0)))
        own_rows = _rows_rotated(padded, jnp.reshape(2 * me, (1,)).astype(jnp.int32), "w_in_super_slab")
        (self.in_ref,) = _sequencer_gather([own_rows], [True], me, 7, "gather_w_in_sequencer")
        self.out_ref, self.conv_ref = _sequencer_gather([w_out.astype(BF16), conv_w], [True, False], me, 8,
                                                        "gather_w_out_sequencer")
        (self.up_ref,) = _sequencer_gather([w_up.astype(BF16)], [True], me, 9, "gather_w_up_sequencer")
        down = w_down.astype(BF16)
        self.down_refs = [_sequencer_gather([down[:, h * (D_MODEL // 2):(h + 1) * (D_MODEL // 2)]], [True], me, 10 + h,
                                            f"gather_w_down_{h}_sequencer")[0] for h in range(2)]
        self.reduces = {}
        self.pairs = {}

    def mixer_weights(self, after):
        return _w_in_from_super_slabs(self.in_ref[...]), None

    def conv_weight(self, after):
        g_conv = self.conv_ref[...]
        return jnp.concatenate([g_conv[i] for i in range(N_DEV)], axis=1)

    def out_weight(self, after):
        return self.out_ref[...].reshape(D_MODEL, D_MODEL)

    def up_weight(self, after):
        return self.up_ref[...]

    def down_weight(self, h, after):
        return self.down_refs[h][...].reshape(D_FF, D_MODEL // 2)

    def _chips_start(self, slabs, from_sibling, rows, tag):
        sums = [_pair_add(s, r, self.csel, tr, f"pair_add_{tag}_{i}")
                for i, (s, r, tr) in enumerate(zip(slabs, from_sibling, rows))]
        lands = [lax.empty((3,) + s.shape[1:], s.dtype) for s in sums]
        self.reduces[tag] = _remote_start(sums, lands, _chips_plan, 3 * len(sums), f"reduce_start_{tag}")
        return self.reduces[tag][4]

    def mlp_grads(self, h2, du, act, dx3b):
        def send(part, tag, after):
            st = _remote_start([part], [lax.empty(part.shape, part.dtype)], _pair4_plan, 4,
                               f"reduce_pair_start_{tag}", after=after)
            return st

        def received(st, after, tag):
            return _remote_wait(st, after, _pair4_plan, f"reduce_pair_wait_{tag}")[1][0]

        def to_chips(sums, tag):
            self.reduces[tag] = _remote_start([sums], [lax.empty((3,) + sums.shape[1:], sums.dtype)], _chips_plan, 3,
                                              f"reduce_start_{tag}")
            return self.reduces[tag][4]

        up_send = _grad_w_up(h2, du, "grad_w_up_send", sel=(self.csel, True))
        st_up = send(up_send, "up", None)
        down_send = _grad_w_down(act, dx3b, "grad_w_down_send", sel=(self.csel, True), after=st_up[4])
        st_down = send(down_send, "down", None)
        up_sum = _grad_w_up(h2, du, "grad_w_up_keep", sel=(self.csel, False), add=received(st_up, down_send, "up"),
                            after=st_down[4])
        token = to_chips(up_sum, "up")
        down_sum = _grad_w_down(act, dx3b, "grad_w_down_keep", sel=(self.csel, False),
                                add=received(st_down, up_sum, "down"), after=token)
        return to_chips(down_sum, "down")

    def _pair_start(self, slabs, tag):
        land = lax.empty((4,) + slabs.shape[1:], slabs.dtype)
        self.pairs[tag] = _remote_start([slabs], [land], _pair_plan, 4, f"reduce_pair_start_{tag}")
        return self.pairs[tag][4]

    def grad_sent(self, tag, after):
        slabs, from_sibling = _remote_wait(self.pairs[tag], after, _pair_plan, f"reduce_pair_wait_{tag}")
        return self._chips_start(slabs, from_sibling, [slabs[0].shape[1]], tag)

    def out_grad(self, g_out):
        return self._pair_start(g_out.reshape(N_DEV, D_MODEL // N_DEV, D_MODEL), "out")

    def in_grad(self, g_in):
        return self._pair_start(
            jnp.stack([_natural_rows(g_in, SUPER_STEP * j, SUPER_STEP * j + SUPER) for j in range(N_DEV)]), "in")

    def small_start(self, small):
        self.st_small = _remote_start([small], [_landing(small, self.me)], _everyone_plan, N_DEV - 1, "gather_start_small")

    def small_end(self, after):
        return _remote_wait(self.st_small, after, _everyone_plan, "gather_small_wait")[1][0]

    def reduce_end(self, tag, after):
        return _remote_wait(self.reduces[tag], after, _chips_plan, f"reduce_wait_{tag}")


def kernel(x, mix_norm_g, w_in, conv_w, conv_b, dt_bias, A_log, D_skip, ssm_norm_g, attn_sinks, attn_out_norm_g, w_out, mlp_norm_g, w_up, w_down, final_norm_g, loss_target, m_mix_norm_g, m_w_in, m_conv_w, m_conv_b, m_dt_bias, m_A_log, m_D_skip, m_ssm_norm_g, m_attn_sinks, m_attn_out_norm_g, m_w_out, m_mlp_norm_g, m_w_up, m_w_down, m_final_norm_g, v_mix_norm_g, v_w_in, v_conv_w, v_conv_b, v_dt_bias, v_A_log, v_D_skip, v_ssm_norm_g, v_attn_sinks, v_attn_out_norm_g, v_w_out, v_mlp_norm_g, v_w_up, v_w_down, v_final_norm_g):
    xi, yi, ci = _coords()
    me = 4 * xi + 2 * yi + ci
    csel = jnp.reshape(ci, (1,)).astype(jnp.int32)
    qsel = jnp.reshape(2 * xi + yi, (1,)).astype(jnp.int32)
    w = dict(mix_norm_g=mix_norm_g, conv_b=conv_b, dt_bias=dt_bias, A_log=A_log, D_skip=D_skip,
             ssm_norm_g=ssm_norm_g, attn_sinks=attn_sinks, attn_out_norm_g=attn_out_norm_g, mlp_norm_g=mlp_norm_g,
             final_norm_g=final_norm_g)
    hooks = _ShardedWeights(w_in[0], w_out[0], conv_w[0], w_up[0], w_down[0], me, csel)
    p = dict(w)
    dx, small = _local_step(x[0], loss_target[0], p, hooks)
    hooks.small_start(small)
    big = {}
    after = dx
    for name, wt, mt, vt, tile in [
            ("up", w_up, m_w_up, v_w_up, (512, SLAB)), ("down", w_down, m_w_down, v_w_down, (256, D_MODEL)),
            ("out", w_out, m_w_out, v_w_out, (256, D_MODEL))]:
        (chip_sums,), (from_chips,) = hooks.reduce_end(name, after)
        res = _adamw_big(wt[0], mt[0], vt[0], chip_sums, from_chips, qsel, tile, f"adamw_w_{name}")
        big["w_" + name] = tuple(r[None] for r in res)
        after = res[0]
    (chip_sums,), (from_chips,) = hooks.reduce_end("in", after)
    g_super = _sum_partials(chip_sums, from_chips, qsel, 512, "grad_w_in_sum")
    g_in = lax.dynamic_slice(g_super, (2 * me, 0), (PER, D_MODEL))
    res = _adamw_tiled(jnp.transpose(w_in[0]), g_in, jnp.transpose(m_w_in[0]), jnp.transpose(v_w_in[0]), 512,
                       "adamw_w_in")
    big["w_in"] = tuple(jnp.transpose(r)[None] for r in (g_in, *res))
    after = res[0]
    gsum = _small_sum(hooks.small_end(after), "small_sum")
    loss = gsum[9, 64]
    gs = _unpack_small(gsum, CONV_DIM)
    cw = CONV_DIM // N_DEV
    g_conv_shard = lax.dynamic_slice(gsum[5:9, :], (0, me * cw), (CONV_K, cw))

    def pack(s):
        return _pack_small(s["mix_norm_g"], s["conv_b"], s["ssm_norm_g"], s["attn_out_norm_g"], s["mlp_norm_g"],
                           s["final_norm_g"], s["conv_w"][0], s["dt_bias"], s["A_log"], s["D_skip"], s["attn_sinks"])

    wp = pack(dict(w, conv_w=conv_w))
    mp = pack(dict(mix_norm_g=m_mix_norm_g, conv_b=m_conv_b, ssm_norm_g=m_ssm_norm_g,
                   attn_out_norm_g=m_attn_out_norm_g, mlp_norm_g=m_mlp_norm_g, final_norm_g=m_final_norm_g,
                   conv_w=m_conv_w, dt_bias=m_dt_bias, A_log=m_A_log, D_skip=m_D_skip, attn_sinks=m_attn_sinks))
    vp = pack(dict(mix_norm_g=v_mix_norm_g, conv_b=v_conv_b, ssm_norm_g=v_ssm_norm_g,
                   attn_out_norm_g=v_attn_out_norm_g, mlp_norm_g=v_mlp_norm_g, final_norm_g=v_final_norm_g,
                   conv_w=v_conv_w, dt_bias=v_dt_bias, A_log=v_A_log, D_skip=v_D_skip, attn_sinks=v_attn_sinks))
    gp = jnp.concatenate([gsum[0:5], jnp.pad(g_conv_shard, ((0, 0), (0, D_MODEL - cw))), gsum[9:10],
                          jnp.zeros((SMALL_ROWS - 10, D_MODEL), F32)], axis=0)
    dp, mnp, vnp = _adamw_small(wp, gp, mp, vp, "adamw_small")
    grads = dict(gs, conv_w=g_conv_shard[None])
    deltas = _unpack_small(dp, cw)
    new_m = _unpack_small(mnp, cw)
    new_v = _unpack_small(vnp, cw)
    for k, name in enumerate(["w_in", "w_out", "w_up", "w_down"]):
        grads[name], deltas[name], new_m[name], new_v[name] = big[name]
    return (loss, dx[None], *[grads[n] for n in WEIGHT_ORDER], *[deltas[n] for n in WEIGHT_ORDER],
            *[new_m[n] for n in WEIGHT_ORDER], *[new_v[n] for n in WEIGHT_ORDER])
```

```python
import jax
import jax.numpy as jnp
from jax import lax
from jax.experimental import pallas as pl
from jax.experimental.pallas import tpu as pltpu
from jax.experimental.pallas import tpu_sc as plsc

F32 = jnp.float32
BF16 = jnp.bfloat16
MESH = pl.DeviceIdType.MESH

EPS = 1e-5
D_MODEL = 2048
D_INNER = 1024
N_HEADS = 16
HEAD_DIM = 64
N_GROUPS = 4
D_STATE = 128
CHUNK = 128
CONV_K = 4
CONV_DIM = 2048
ATTN_W = 1024
KV_W = 128
WINDOW = 128
D_FF = 8192
IN_PROJ = 4368
N_DEV = 8
NP = 4608
OFF_Z, OFF_X, OFF_B, OFF_C, OFF_Q, OFF_K, OFF_V, OFF_DT = 0, 1024, 2048, 2560, 3072, 4096, 4224, 4352
NAT_DT = 3072

ADAM_LR = 0.001
ADAM_B1 = 0.9
ADAM_B2 = 0.999
ADAM_EPS = 1e-08
ADAM_WD = 0.01
ADAM_STEP = 10

VMEM_LIMIT = 52 * 1024 * 1024
SMALL_ROWS = 16
NEG = -1e30


def _cparams(sem=None):
    return pltpu.CompilerParams(dimension_semantics=sem, vmem_limit_bytes=VMEM_LIMIT)


def _split3(v):
    hi = v.astype(BF16)
    rest = v - hi.astype(F32)
    mid = rest.astype(BF16)
    return hi, mid, (rest - mid.astype(F32)).astype(BF16)


def _hdot(a, b, data):
    if data == "a":
        sel = b.astype(BF16)
        return sum(_dot_nn(part, sel) for part in _split3(a))
    sel = a.astype(BF16)
    return sum(_dot_nn(sel, part) for part in _split3(b))


def _dot_nn(a, b):
    return lax.dot_general(a, b, (((1,), (0,)), ((), ())), preferred_element_type=F32)


def _dot_nt(a, b):
    return lax.dot_general(a, b, (((1,), (1,)), ((), ())), preferred_element_type=F32)


def _dot_tn(a, b):
    return lax.dot_general(a, b, (((0,), (0,)), ((), ())), preferred_element_type=F32)


def _softplus(v):
    return jnp.maximum(v, 0.0) + jnp.log1p(jnp.exp(-jnp.abs(v)))


def _sigmoid(v):
    return 1.0 / (1.0 + jnp.exp(-v))


def _matmul(a, b, *, mode, grid, a_spec, b_spec, out_shapes, out_specs, tile, name,
            extras=(), extra_specs=(), epilogue=None, after=None, dot_fn=None, prefetch=None):
    nk = grid[2]
    n_ex = len(extras)
    n_out = len(out_shapes)
    bs, b_specs = (b, b_spec) if isinstance(b, tuple) else ((b,), (b_spec,))
    n_in = 1 + len(bs)
    dot = dot_fn if dot_fn is not None else {"nn": _dot_nn, "nt": _dot_nt, "tn": _dot_tn}[mode]

    def finish(acc, ex_refs, out_refs):
        res = (acc,) if epilogue is None else epilogue(acc, *[e[...] for e in ex_refs])
        for o, r in zip(out_refs, res):
            o[...] = r.astype(o.dtype)

    def body(*refs):
        ex_refs = refs[n_in:n_in + n_ex]
        out_refs = refs[n_in + n_ex:n_in + n_ex + n_out]
        part = dot(*[r[...].astype(BF16) for r in refs[:n_in]])
        if nk == 1:
            finish(part, ex_refs, out_refs)
        else:
            acc_ref = refs[-1]
            k = pl.program_id(2)

            @pl.when(k == 0)
            def _():
                acc_ref[...] = part

            @pl.when(k > 0)
            def _():
                acc_ref[...] += part

            @pl.when(k == nk - 1)
            def _():
                finish(acc_ref[...], ex_refs, out_refs)

    scratch = [] if nk == 1 else [pltpu.VMEM(tile, F32)]
    n_pre = 0 if prefetch is None else 1
    tok_specs = [] if after is None else [pl.BlockSpec((8, 128), lambda *_: (0, 0))]
    tok_args = [] if after is None else [after]

    def body_with_token(*refs):
        refs = refs[n_pre:]
        body(*refs[:n_in + n_ex], *refs[n_in + n_ex + len(tok_args):])

    in_specs = [a_spec, *b_specs, *extra_specs, *tok_specs]
    params = _cparams(("parallel", "parallel", "arbitrary"))
    if prefetch is None:
        return pl.pallas_call(
            body_with_token, grid=grid, in_specs=in_specs, out_specs=list(out_specs), out_shape=list(out_shapes),
            scratch_shapes=scratch, name=name, compiler_params=params)(a, *bs, *extras, *tok_args)
    return pl.pallas_call(
        body_with_token,
        grid_spec=pltpu.PrefetchScalarGridSpec(num_scalar_prefetch=1, grid=grid, in_specs=in_specs,
                                               out_specs=list(out_specs), scratch_shapes=scratch),
        out_shape=list(out_shapes), name=name, compiler_params=params)(prefetch, a, *bs, *extras, *tok_args)


def _mm_simple(a, b, *, mode, M, N, K, tm, tn, tk, out_dtype, name, extras=(), epilogue=None, n_out=1,
               out_dtypes=None, after=None):
    grid = (M // tm, N // tn, K // tk)
    if mode == "nn":
        a_spec = pl.BlockSpec((tm, tk), lambda i, j, k: (i, k))
        b_spec = pl.BlockSpec((tk, tn), lambda i, j, k: (k, j))
    elif mode == "nt":
        a_spec = pl.BlockSpec((tm, tk), lambda i, j, k: (i, k))
        b_spec = pl.BlockSpec((tn, tk), lambda i, j, k: (j, k))
    else:
        a_spec = pl.BlockSpec((tk, tm), lambda i, j, k: (k, i))
        b_spec = pl.BlockSpec((tk, tn), lambda i, j, k: (k, j))
    o_spec = pl.BlockSpec((tm, tn), lambda i, j, k: (i, j))
    dts = out_dtypes if out_dtypes is not None else [out_dtype] * n_out
    return _matmul(a, b, mode=mode, grid=grid, a_spec=a_spec, b_spec=b_spec,
                   out_shapes=[jax.ShapeDtypeStruct((M, N), d) for d in dts],
                   out_specs=[o_spec] * len(dts), tile=(tm, tn), name=name,
                   extras=extras, extra_specs=[o_spec] * len(extras), epilogue=epilogue, after=after)


ROW_BLOCK = 256


def _rmsnorm_fwd(x, g, name):
    T, D = x.shape

    def body(x_ref, g_ref, o_ref):
        xf = x_ref[...]
        r = lax.rsqrt(jnp.mean(xf * xf, axis=-1, keepdims=True) + EPS)
        o_ref[...] = (xf * r * g_ref[...]).astype(BF16)

    return pl.pallas_call(
        body, grid=(T // ROW_BLOCK,),
        in_specs=[pl.BlockSpec((ROW_BLOCK, D), lambda i: (i, 0)), pl.BlockSpec((1, D), lambda i: (0, 0))],
        out_specs=pl.BlockSpec((ROW_BLOCK, D), lambda i: (i, 0)),
        out_shape=jax.ShapeDtypeStruct((T, D), BF16), name=name, compiler_params=_cparams(("parallel",)),
    )(x, g)


def _rmsnorm_bwd(dh, x, g, dres, name, with_bf16=True):
    T, D = x.shape

    def body(dh_ref, x_ref, g_ref, dres_ref, dx_ref, *rest):
        dg_ref = rest[-1]
        i = pl.program_id(0)
        xf = x_ref[...]
        r = lax.rsqrt(jnp.mean(xf * xf, axis=-1, keepdims=True) + EPS)
        xh = xf * r
        d = dh_ref[...]

        @pl.when(i == 0)
        def _():
            dg_ref[...] = jnp.zeros_like(dg_ref)

        dg_ref[...] += jnp.sum(d * xh, axis=0, keepdims=True)
        dxh = d * g_ref[...]
        dx = r * (dxh - xh * jnp.mean(dxh * xh, axis=-1, keepdims=True)) + dres_ref[...]
        dx_ref[...] = dx
        if with_bf16:
            rest[0][...] = dx.astype(BF16)

    row = pl.BlockSpec((ROW_BLOCK, D), lambda i: (i, 0))
    vec = pl.BlockSpec((1, D), lambda i: (0, 0))
    copies = [(row, jax.ShapeDtypeStruct((T, D), BF16))] if with_bf16 else []
    return pl.pallas_call(
        body, grid=(T // ROW_BLOCK,), in_specs=[row, row, vec, row],
        out_specs=[row, *[c[0] for c in copies], vec],
        out_shape=[jax.ShapeDtypeStruct((T, D), F32), *[c[1] for c in copies], jax.ShapeDtypeStruct((1, D), F32)],
        name=name, compiler_params=_cparams(("arbitrary",)),
    )(dh, x, g, dres)


def _final_loss(x3_halves, tgt, g, name):
    T, D = tgt.shape

    def body(xa_ref, xb_ref, t_ref, g_ref, loss_ref, dg_ref, dx_ref, dxb_ref):
        i = pl.program_id(0)
        xf = jnp.concatenate([xa_ref[...], xb_ref[...]], axis=1)
        r = lax.rsqrt(jnp.mean(xf * xf, axis=-1, keepdims=True) + EPS)
        xh = xf * r
        gg = g_ref[...]
        err = xh * gg - t_ref[...]

        @pl.when(i == 0)
        def _():
            dg_ref[...] = jnp.zeros_like(dg_ref)
            loss_ref[...] = jnp.zeros_like(loss_ref)

        part = jnp.sum(jnp.sum(err * err, axis=-1, keepdims=True), axis=0, keepdims=True) * (0.5 / D)
        loss_ref[...] += jnp.broadcast_to(part, loss_ref.shape)
        dout = err * (1.0 / D)
        dg_ref[...] += jnp.sum(dout * xh, axis=0, keepdims=True)
        dxh = dout * gg
        dx = r * (dxh - xh * jnp.mean(dxh * xh, axis=-1, keepdims=True))
        dx_ref[...] = dx
        dxb_ref[...] = dx.astype(BF16)

    row = pl.BlockSpec((ROW_BLOCK, D), lambda i: (i, 0))
    vec = pl.BlockSpec((1, D), lambda i: (0, 0))
    return pl.pallas_call(
        body, grid=(T // ROW_BLOCK,),
        in_specs=[pl.BlockSpec((ROW_BLOCK, D // 2), lambda i: (i, 0))] * 2 + [row, vec],
        out_specs=[pl.BlockSpec((1, 128), lambda i: (0, 0)), vec, row, row],
        out_shape=[jax.ShapeDtypeStruct((1, 128), F32), jax.ShapeDtypeStruct((1, D), F32),
                   jax.ShapeDtypeStruct((T, D), F32), jax.ShapeDtypeStruct((T, D), BF16)],
        name=name, compiler_params=_cparams(("arbitrary",)),
    )(*x3_halves, tgt, g)


CONV_BLOCK = 256


def _conv_apply(u, w, b):
    row = lax.broadcasted_iota(jnp.int32, u.shape, 0)
    acc = b + w[CONV_K - 1:CONV_K, :] * u
    shifted = []
    for j in range(1, CONV_K):
        uj = jnp.where(row >= j, pltpu.roll(u, j, axis=0), 0.0)
        shifted.append(uj)
        acc = acc + w[CONV_K - 1 - j:CONV_K - j, :] * uj
    return acc, shifted


def _conv_fwd(proj, conv_w, conv_b, name):
    T = proj.shape[0]
    cb0 = OFF_X // CONV_BLOCK

    def body(u_ref, w_ref, b_ref, o_ref):
        c, _ = _conv_apply(u_ref[...], w_ref[...], b_ref[...])
        o_ref[...] = c * _sigmoid(c)

    return pl.pallas_call(
        body, grid=(CONV_DIM // CONV_BLOCK,),
        in_specs=[pl.BlockSpec((T, CONV_BLOCK), lambda j: (0, cb0 + j)),
                  pl.BlockSpec((CONV_K, CONV_BLOCK), lambda j: (0, j)),
                  pl.BlockSpec((1, CONV_BLOCK), lambda j: (0, j))],
        out_specs=pl.BlockSpec((T, CONV_BLOCK), lambda j: (0, j)),
        out_shape=jax.ShapeDtypeStruct((T, CONV_DIM), F32), name=name, compiler_params=_cparams(("parallel",)),
    )(proj, conv_w, conv_b)


def _conv_bwd(proj, dact, conv_w, conv_b, dproj, name):
    T = proj.shape[0]
    cb0 = OFF_X // CONV_BLOCK

    def body(u_ref, d_ref, w_ref, b_ref, _, du_ref, dw_ref, db_ref):
        u = u_ref[...]
        w = w_ref[...]
        c, shifted = _conv_apply(u, w, b_ref[...])
        sg = _sigmoid(c)
        dc = d_ref[...] * sg * (1.0 + c * (1.0 - sg))
        row = lax.broadcasted_iota(jnp.int32, u.shape, 0)
        du = w[CONV_K - 1:CONV_K, :] * dc
        dw_ref[CONV_K - 1:CONV_K, :] = jnp.sum(dc * u, axis=0, keepdims=True)
        for j in range(1, CONV_K):
            dcj = jnp.where(row < T - j, pltpu.roll(dc, T - j, axis=0), 0.0)
            du = du + w[CONV_K - 1 - j:CONV_K - j, :] * dcj
            dw_ref[CONV_K - 1 - j:CONV_K - j, :] = jnp.sum(dc * shifted[j - 1], axis=0, keepdims=True)
        db_ref[...] = jnp.sum(dc, axis=0, keepdims=True)
        du_ref[...] = du.astype(BF16)

    return pl.pallas_call(
        body, grid=(CONV_DIM // CONV_BLOCK,),
        in_specs=[pl.BlockSpec((T, CONV_BLOCK), lambda j: (0, cb0 + j)),
                  pl.BlockSpec((T, CONV_BLOCK), lambda j: (0, j)),
                  pl.BlockSpec((CONV_K, CONV_BLOCK), lambda j: (0, j)),
                  pl.BlockSpec((1, CONV_BLOCK), lambda j: (0, j)), pl.BlockSpec(memory_space=pl.ANY)],
        out_specs=[pl.BlockSpec((T, CONV_BLOCK), lambda j: (0, cb0 + j)),
                   pl.BlockSpec((CONV_K, CONV_BLOCK), lambda j: (0, j)),
                   pl.BlockSpec((1, CONV_BLOCK), lambda j: (0, j))],
        out_shape=[jax.ShapeDtypeStruct(dproj.shape, BF16), jax.ShapeDtypeStruct((CONV_K, CONV_DIM), F32),
                   jax.ShapeDtypeStruct((1, CONV_DIM), F32)],
        input_output_aliases={4: 0}, name=name, compiler_params=_cparams(("parallel",)),
    )(proj, dact, conv_w, conv_b, dproj)


GROUP_W = D_INNER // N_GROUPS
HEADS_PER_GROUP = N_HEADS // N_GROUPS


def _expand_mat():
    h = lax.broadcasted_iota(jnp.int32, (N_HEADS, D_INNER), 0)
    j = lax.broadcasted_iota(jnp.int32, (N_HEADS, D_INNER), 1)
    return (j // HEAD_DIM == h).astype(F32)


def _reduce_mat(g):
    j = lax.broadcasted_iota(jnp.int32, (GROUP_W, N_HEADS), 0)
    h = lax.broadcasted_iota(jnp.int32, (GROUP_W, N_HEADS), 1)
    return (g * HEADS_PER_GROUP + j // HEAD_DIM == h).astype(F32)


def _col16(v, h):
    lane = lax.broadcasted_iota(jnp.int32, v.shape, 1)
    return jnp.sum(jnp.where(lane == h, v, 0.0), axis=1, keepdims=True)


def _ssd_pre(dt_raw, dtT_raw, dtb, dtbT, alog, alogT):
    Q = CHUNK
    xdt = dt_raw + dtb
    dt = _softplus(xdt)
    dtT = _softplus(dtT_raw + dtbT)
    A = -jnp.exp(alog)
    AT = -jnp.exp(alogT)
    row = lax.broadcasted_iota(jnp.int32, (Q, Q), 0)
    col = lax.broadcasted_iota(jnp.int32, (Q, Q), 1)
    tril = (row >= col).astype(F32)
    triu = (row <= col).astype(F32)
    cs = _hdot(tril, dt * A, "b")
    csT = _hdot(dtT * AT, triu, "a")
    return xdt, dt, A, cs, csT, row >= col, triu


def _decay_matrix(cs, csT, h, causal):
    seg = _col16(cs, h) - csT[h:h + 1, :]
    return jnp.where(causal, jnp.exp(jnp.minimum(seg, 0.0)), 0.0)


def _ssd_in_specs(nc, rev):
    def cidx(c):
        return (nc - 1 - c) if rev else c

    return [
        pl.BlockSpec((CHUNK, D_INNER), lambda c: (cidx(c), 0)),
        pl.BlockSpec((CHUNK, 512), lambda c: (cidx(c), 2)),
        pl.BlockSpec((CHUNK, 512), lambda c: (cidx(c), 3)),
        pl.BlockSpec((CHUNK, D_INNER), lambda c: (cidx(c), 0)),
        pl.BlockSpec((CHUNK, 128), lambda c: (cidx(c), OFF_DT // 128)),
        pl.BlockSpec((N_HEADS, CHUNK), lambda c: (0, cidx(c))),
        pl.BlockSpec((1, N_HEADS), lambda c: (0, 0)),
        pl.BlockSpec((N_HEADS, 1), lambda c: (0, 0)),
        pl.BlockSpec((1, N_HEADS), lambda c: (0, 0)),
        pl.BlockSpec((N_HEADS, 1), lambda c: (0, 0)),
        pl.BlockSpec((1, D_INNER), lambda c: (0, 0)),
        pl.BlockSpec((1, D_INNER), lambda c: (0, 0)),
    ]


def _ssd_fwd(xbc, proj, dtT, dtb, dtbT, alog, alogT, dfull, ng, name):
    T = xbc.shape[0]
    nc = T // CHUNK
    Q = CHUNK

    def body(xs_ref, B_ref, C_ref, z_ref, dt_ref, dtT_ref, dtb_ref, dtbT_ref, al_ref, alT_ref, df_ref, ng_ref,
             y_ref, ypre_ref, hs_ref, h_scr):
        c = pl.program_id(0)

        @pl.when(c == 0)
        def _():
            h_scr[...] = jnp.zeros_like(h_scr)

        _, dt, _, cs, csT, causal, _ = _ssd_pre(dt_ref[:, :N_HEADS], dtT_ref[...], dtb_ref[...], dtbT_ref[...],
                                                al_ref[...], alT_ref[...])
        ex = _expand_mat()
        dt_full = _hdot(dt, ex, "a")
        cs_full = _hdot(cs, ex, "a")
        cs_last = cs_full[Q - 1:Q, :]
        xs = xs_ref[...]
        xd = xs * dt_full
        e_full = jnp.exp(cs_full)
        dec_full = jnp.exp(cs_last - cs_full)
        cd_full = jnp.exp(cs_last)
        lane_head = lax.broadcasted_iota(jnp.int32, (1, GROUP_W), 1) // HEAD_DIM
        for g in range(N_GROUPS):
            sl = slice(g * GROUP_W, (g + 1) * GROUP_W)
            Bg = B_ref[:, g * D_STATE:(g + 1) * D_STATE].astype(BF16)
            Cg = C_ref[:, g * D_STATE:(g + 1) * D_STATE].astype(BF16)
            CB = _dot_nt(Cg, Bg)
            hg = h_scr[g]
            yoff = _dot_nn(Cg, hg.astype(BF16)) * e_full[:, sl]
            xd_g = xd[:, sl]
            S = _dot_tn(Bg, (xd_g * dec_full[:, sl]).astype(BF16))
            xd_b = xd_g.astype(BF16)
            ydiag = jnp.zeros((Q, GROUP_W), F32)
            for r in range(HEADS_PER_GROUP):
                Lm = _decay_matrix(cs, csT, g * HEADS_PER_GROUP + r, causal)
                Gm = (CB * Lm).astype(BF16)
                ydiag = ydiag + _dot_nn(Gm, jnp.where(lane_head == r, xd_b, jnp.zeros_like(xd_b)))
            hs_ref[0, g] = hg
            h_scr[g] = hg * cd_full[:, sl] + S
            ypre = ydiag + yoff + xs[:, sl] * df_ref[:, sl]
            ypre_ref[:, sl] = ypre
            zg = z_ref[:, sl]
            yz = ypre * zg * _sigmoid(zg)
            rn = lax.rsqrt(jnp.mean(yz * yz, axis=-1, keepdims=True) + EPS)
            y_ref[:, sl] = (yz * rn * ng_ref[:, sl]).astype(BF16)

    return pl.pallas_call(
        body, grid=(nc,), in_specs=_ssd_in_specs(nc, False),
        out_specs=[pl.BlockSpec((CHUNK, D_INNER), lambda c: (c, 0)),
                   pl.BlockSpec((CHUNK, D_INNER), lambda c: (c, 0)),
                   pl.BlockSpec((1, N_GROUPS, D_STATE, GROUP_W), lambda c: (c, 0, 0, 0))],
        out_shape=[jax.ShapeDtypeStruct((T, D_INNER + ATTN_W), BF16), jax.ShapeDtypeStruct((T, D_INNER), F32),
                   jax.ShapeDtypeStruct((nc, N_GROUPS, D_STATE, GROUP_W), F32)],
        scratch_shapes=[pltpu.VMEM((N_GROUPS, D_STATE, GROUP_W), F32)],
        name=name, compiler_params=_cparams(("arbitrary",)),
    )(xbc, xbc, xbc, proj, proj, dtT, dtb, dtbT, alog, alogT, dfull, ng)


def _ssd_bwd(xbc, proj, dtT, dtb, dtbT, alog, alogT, dfull, ng, ypre, hs, dy, name):
    T = xbc.shape[0]
    nc = T // CHUNK
    Q = CHUNK

    def body(xs_ref, B_ref, C_ref, z_ref, dt_ref, dtT_ref, dtb_ref, dtbT_ref, al_ref, alT_ref, df_ref, ng_ref,
             ypre_ref, hs_ref, dy_ref,
             dz_ref, dxbc_ref, ddtb_ref, dal_ref, dD_ref, dng_ref, dh_scr):
        step = pl.program_id(0)

        @pl.when(step == 0)
        def _():
            dh_scr[...] = jnp.zeros_like(dh_scr)
            ddtb_ref[...] = jnp.zeros_like(ddtb_ref)
            dal_ref[...] = jnp.zeros_like(dal_ref)
            dD_ref[...] = jnp.zeros_like(dD_ref)
            dng_ref[...] = jnp.zeros_like(dng_ref)

        xdt, dt, A, cs, csT, causal, triu = _ssd_pre(dt_ref[:, :N_HEADS], dtT_ref[...], dtb_ref[...],
                                                    dtbT_ref[...], al_ref[...], alT_ref[...])
        ex = _expand_mat()
        dt_full = _hdot(dt, ex, "a")
        cs_full = _hdot(cs, ex, "a")
        cs_last = cs_full[Q - 1:Q, :]
        xs = xs_ref[...]
        xd = xs * dt_full
        e_full = jnp.exp(cs_full)
        dec_full = jnp.exp(cs_last - cs_full)
        cd_full = jnp.exp(cs_last)
        lane_head = lax.broadcasted_iota(jnp.int32, (1, GROUP_W), 1) // HEAD_DIM
        is_last = lax.broadcasted_iota(jnp.int32, (Q, 1), 0) == Q - 1
        dcs16 = jnp.zeros((Q, N_HEADS), F32)
        ddtx16 = jnp.zeros((Q, N_HEADS), F32)
        dD16 = jnp.zeros((8, N_HEADS), F32)
        lane16 = lax.broadcasted_iota(jnp.int32, (1, N_HEADS), 1)
        sub16 = lax.broadcasted_iota(jnp.int32, (N_HEADS, 1), 0)
        col_sums = jnp.zeros((N_HEADS, Q), F32)
        for g in range(N_GROUPS):
            sl = slice(g * GROUP_W, (g + 1) * GROUP_W)
            red = _reduce_mat(g)
            ypre_g = ypre_ref[:, sl]
            zg = z_ref[:, sl]
            sg = _sigmoid(zg)
            silu = zg * sg
            yz = ypre_g * silu
            rn = lax.rsqrt(jnp.mean(yz * yz, axis=-1, keepdims=True) + EPS)
            yh = yz * rn
            dy_g = dy_ref[:, sl]
            dng_ref[:, sl] += jnp.sum(dy_g * yh, axis=0, keepdims=True)
            dyh = dy_g * ng_ref[:, sl]
            dyz = rn * (dyh - yh * jnp.mean(dyh * yh, axis=-1, keepdims=True))
            dY = dyz * silu
            dz_ref[:, sl] = (dyz * ypre_g * sg * (1.0 + zg * (1.0 - sg))).astype(BF16)
            xs_g = xs[:, sl]
            xd_g = xd[:, sl]
            dec_g = dec_full[:, sl]
            cd_g = cd_full[:, sl]
            d_g = df_ref[:, sl]
            Bg = B_ref[:, g * D_STATE:(g + 1) * D_STATE].astype(BF16)
            Cg = C_ref[:, g * D_STATE:(g + 1) * D_STATE].astype(BF16)
            CB = _dot_nt(Cg, Bg)
            hg = hs_ref[0, g]
            hgb = hg.astype(BF16)
            yoff = _dot_nn(Cg, hgb) * e_full[:, sl]
            dhn = dh_scr[g]
            dhnb = dhn.astype(BF16)
            dYE = (dY * e_full[:, sl]).astype(BF16)
            dC = _dot_nt(dYE, hgb)
            dh_direct = _dot_tn(Cg, dYE)
            dXdd = _dot_nn(Bg, dhnb)
            dB = _dot_nt((xd_g * dec_g).astype(BF16), dhnb)
            dcd = jnp.sum(dhn * hg, axis=0, keepdims=True)
            dh_scr[g] = dh_direct + cd_g * dhn
            dYb = dY.astype(BF16)
            xd_b = xd_g.astype(BF16)
            dCB = jnp.zeros((Q, Q), F32)
            dXd = dXdd * dec_g
            for r in range(HEADS_PER_GROUP):
                h = g * HEADS_PER_GROUP + r
                Lm = _decay_matrix(cs, csT, h, causal)
                Gf = CB * Lm
                dYr = jnp.where(lane_head == r, dYb, jnp.zeros_like(dYb))
                dG = _dot_nt(dYr, xd_b)
                dCB = dCB + dG * Lm
                dXd = dXd + _dot_tn(Gf.astype(BF16), dYr)
                Mm = dG * Gf
                dcs16 = dcs16 + jnp.where(lane16 == h, jnp.sum(Mm, axis=1, keepdims=True), 0.0)
                col_sums = col_sums + jnp.where(sub16 == h, jnp.sum(Mm, axis=0, keepdims=True), 0.0)
            dCBb = dCB.astype(BF16)
            dC = dC + _dot_nn(dCBb, Bg)
            dB = dB + _dot_tn(dCBb, Cg)
            w_state = dXdd * dec_g * xd_g
            t_last = jnp.sum(w_state, axis=0, keepdims=True) + dcd * cd_g
            dcs_g = dY * yoff - w_state + jnp.where(is_last, t_last, 0.0)
            dcs16 = dcs16 + _hdot(dcs_g, red, "a")
            ddtx16 = ddtx16 + _hdot(dXd * xs_g, red, "a")
            dD16 = dD16 + _hdot(jnp.broadcast_to(jnp.sum(dY * xs_g, axis=0, keepdims=True), (8, GROUP_W)), red, "a")
            dxbc_ref[:, sl] = dXd * dt_full[:, sl] + dY * d_g
            dxbc_ref[:, D_INNER + g * D_STATE:D_INNER + (g + 1) * D_STATE] = dB
            dxbc_ref[:, D_INNER + 512 + g * D_STATE:D_INNER + 512 + (g + 1) * D_STATE] = dC
        eye = (lax.broadcasted_iota(jnp.int32, (N_HEADS, N_HEADS), 0)
               == lax.broadcasted_iota(jnp.int32, (N_HEADS, N_HEADS), 1)).astype(BF16)
        dcs16 = dcs16 - sum(_dot_tn(part, eye) for part in _split3(col_sums))
        da = _hdot(triu, dcs16, "b")
        ddt = da * A + ddtx16
        ddt_raw = ddt * _sigmoid(xdt)
        pr = lax.broadcasted_iota(jnp.int32, (N_HEADS, 128), 0)
        pc = lax.broadcasted_iota(jnp.int32, (N_HEADS, 128), 1)
        dz_ref[:, D_INNER:OFF_DT] = jnp.zeros((Q, OFF_DT - D_INNER), BF16)
        dz_ref[:, OFF_DT:OFF_DT + 128] = _hdot(ddt_raw, (pr == pc).astype(F32), "a").astype(BF16)
        dz_ref[:, OFF_DT + 128:] = jnp.zeros((Q, NP - OFF_DT - 128), BF16)
        ddtb_ref[...] += jnp.sum(ddt_raw, axis=0, keepdims=True)
        dal_ref[...] += jnp.sum(da * dt, axis=0, keepdims=True) * A
        dD_ref[...] += dD16[0:1, :]

    def rc(c):
        return nc - 1 - c

    in_specs = _ssd_in_specs(nc, True) + [
        pl.BlockSpec((CHUNK, D_INNER), lambda c: (rc(c), 0)),
        pl.BlockSpec((1, N_GROUPS, D_STATE, GROUP_W), lambda c: (rc(c), 0, 0, 0)),
        pl.BlockSpec((CHUNK, D_INNER), lambda c: (rc(c), 0)),
    ]
    small = pl.BlockSpec((1, N_HEADS), lambda c: (0, 0))
    return pl.pallas_call(
        body, grid=(nc,), in_specs=in_specs,
        out_specs=[pl.BlockSpec((CHUNK, NP), lambda c: (rc(c), 0)),
                   pl.BlockSpec((CHUNK, CONV_DIM), lambda c: (rc(c), 0)),
                   small, small, small,
                   pl.BlockSpec((1, D_INNER), lambda c: (0, 0))],
        out_shape=[jax.ShapeDtypeStruct((T, NP), BF16), jax.ShapeDtypeStruct((T, CONV_DIM), F32),
                   jax.ShapeDtypeStruct((1, N_HEADS), F32), jax.ShapeDtypeStruct((1, N_HEADS), F32),
                   jax.ShapeDtypeStruct((1, N_HEADS), F32), jax.ShapeDtypeStruct((1, D_INNER), F32)],
        scratch_shapes=[pltpu.VMEM((N_GROUPS, D_STATE, GROUP_W), F32)],
        name=name, compiler_params=_cparams(("arbitrary",)),
    )(xbc, xbc, xbc, proj, proj, dtT, dtb, dtbT, alog, alogT, dfull, ng, ypre, hs, dy)


N_PAIRS = ATTN_W // 128
PAIRS_PER_KV = N_PAIRS // 2
ATTN_SCALE = HEAD_DIM ** -0.5


def _kv_variants(kk):
    lo = lax.broadcasted_iota(jnp.int32, kk.shape, 1) < HEAD_DIM
    zero = jnp.zeros_like(kk)
    k00 = jnp.where(lo, kk, zero)
    k11 = jnp.where(lo, zero, kk)
    k01 = pltpu.roll(k00, HEAD_DIM, axis=1)
    k10 = pltpu.roll(k11, HEAD_DIM, axis=1)
    return [[k00.astype(BF16), k01.astype(BF16)], [k10.astype(BF16), k11.astype(BF16)]]


LOG2E = 1.4426950408889634


def _own_block():
    i = lax.broadcasted_iota(jnp.int32, (WINDOW, WINDOW), 0)
    j = lax.broadcasted_iota(jnp.int32, (WINDOW, WINDOW), 1)
    return j <= i


def _fold(own, a):
    return jnp.where(own, a[:, WINDOW:], a[:, :WINDOW])


def _attn_probs(qp, kvar, own, prev_bias, sk):
    s = _dot_nt(qp, kvar)
    sb = jnp.where(own, s[:, WINDOW:], s[:, :WINDOW] + prev_bias) * (ATTN_SCALE * LOG2E)
    sk2 = sk * LOG2E
    m = jnp.maximum(jnp.max(sb, axis=1, keepdims=True), sk2)
    pe = jnp.exp2(sb - m)
    es = jnp.exp2(sk2 - m)
    den = jnp.sum(pe, axis=1, keepdims=True) + es
    inv = 1.0 / den
    return pe * inv, es * inv


def _unfold(own, a):
    zero = jnp.zeros_like(a)
    return jnp.where(own, zero, a), jnp.where(own, a, zero)


def _sink(sinks, r):
    lane = lax.broadcasted_iota(jnp.int32, sinks.shape, 1)
    return jnp.sum(jnp.where(lane == r, sinks, 0.0), axis=1, keepdims=True)


def _kv_specs():
    return [pl.BlockSpec((WINDOW, KV_W), lambda n: (jnp.maximum(n - 1, 0), OFF_K // KV_W)),
            pl.BlockSpec((WINDOW, KV_W), lambda n: (n, OFF_K // KV_W)),
            pl.BlockSpec((WINDOW, KV_W), lambda n: (jnp.maximum(n - 1, 0), OFF_V // KV_W)),
            pl.BlockSpec((WINDOW, KV_W), lambda n: (n, OFF_V // KV_W))]


def _attn_fwd(proj, sinks, og, ycat, name):
    T = proj.shape[0]
    nb = T // WINDOW

    def body(q_ref, kp_ref, kc_ref, vp_ref, vc_ref, s_ref, og_ref, _, y_ref, o_ref):
        n = pl.program_id(0)
        kv = _kv_variants(jnp.concatenate([kp_ref[...], kc_ref[...]], axis=0))
        vv = _kv_variants(jnp.concatenate([vp_ref[...], vc_ref[...]], axis=0))
        own = _own_block()
        prev_bias = jnp.where(n > 0, 0.0, NEG)
        sinks_v = s_ref[...]
        ssq = jnp.zeros((WINDOW, 1), F32)
        for p in range(N_PAIRS):
            j = p // PAIRS_PER_KV
            qp = q_ref[:, p * 128:(p + 1) * 128].astype(BF16)
            o_pair = jnp.zeros((WINDOW, 128), F32)
            for par in range(2):
                pn, _ = _attn_probs(qp, kv[j][par], own, prev_bias, _sink(sinks_v, 2 * p + par))
                p_prev, p_own = _unfold(own, pn.astype(BF16))
                o_pair = o_pair + _dot_nn(p_prev, vv[j][par][:WINDOW]) + _dot_nn(p_own, vv[j][par][WINDOW:])
            o_ref[:, p * 128:(p + 1) * 128] = o_pair
            ssq = ssq + jnp.sum(o_pair * o_pair, axis=1, keepdims=True)
        rn = lax.rsqrt(ssq * (1.0 / ATTN_W) + EPS)
        y_ref[...] = (o_ref[...] * rn * og_ref[...]).astype(BF16)

    return pl.pallas_call(
        body, grid=(nb,),
        in_specs=[pl.BlockSpec((WINDOW, ATTN_W), lambda n: (n, OFF_Q // ATTN_W)), *_kv_specs(),
                  pl.BlockSpec((1, N_HEADS), lambda n: (0, 0)), pl.BlockSpec((1, ATTN_W), lambda n: (0, 0)), ANY],
        out_specs=[pl.BlockSpec((WINDOW, ATTN_W), lambda n: (n, 1)), pl.BlockSpec((WINDOW, ATTN_W), lambda n: (n, 0))],
        out_shape=[jax.ShapeDtypeStruct(ycat.shape, BF16), jax.ShapeDtypeStruct((T, ATTN_W), F32)],
        input_output_aliases={7: 0}, name=name, compiler_params=_cparams(("parallel",)),
    )(proj, proj, proj, proj, proj, sinks, og, ycat)


def _attn_bwd(proj, sinks, og, o, dy, dproj, name):
    T = proj.shape[0]
    nb = T // WINDOW

    def body(q_ref, kp_ref, kc_ref, vp_ref, vc_ref, s_ref, og_ref, o_ref, dy_ref, _,
             dq_ref, dk_ref, dv_ref, ds_ref, dog_ref, qt_scr, dot_scr, ds_scr, p_scr):
        n = pl.program_id(0)

        @pl.when(n == 0)
        def _():
            dk_ref[...] = jnp.zeros_like(dk_ref)
            dv_ref[...] = jnp.zeros_like(dv_ref)
            ds_ref[...] = jnp.zeros_like(ds_ref)
            dog_ref[...] = jnp.zeros_like(dog_ref)

        kv = _kv_variants(jnp.concatenate([kp_ref[...], kc_ref[...]], axis=0))
        vv = _kv_variants(jnp.concatenate([vp_ref[...], vc_ref[...]], axis=0))
        own = _own_block()
        prev_bias = jnp.where(n > 0, 0.0, NEG)
        sinks_v = s_ref[...]
        of = o_ref[...]
        rn = lax.rsqrt(jnp.mean(of * of, axis=-1, keepdims=True) + EPS)
        oh = of * rn
        dyf = dy_ref[...]
        dog_ref[...] += jnp.sum(dyf * oh, axis=0, keepdims=True)
        doh = dyf * og_ref[...]
        do = rn * (doh - oh * jnp.mean(doh * oh, axis=-1, keepdims=True))
        lane = lax.broadcasted_iota(jnp.int32, (1, 128), 1)
        lane16 = lax.broadcasted_iota(jnp.int32, (1, N_HEADS), 1)
        dsink = jnp.zeros((1, N_HEADS), F32)
        for p in range(N_PAIRS):
            j = p // PAIRS_PER_KV
            q_f = q_ref[:, p * 128:(p + 1) * 128]
            qp = q_f.astype(BF16)
            q_t = q_f.T.astype(BF16)
            do_p = do[:, p * 128:(p + 1) * 128]
            o_p = of[:, p * 128:(p + 1) * 128]
            do_b = do_p.astype(BF16)
            do_t = do_p.T.astype(BF16)
            prod = do_p * o_p
            dq_pair = jnp.zeros((WINDOW, 128), F32)
            for par in range(2):
                r = 2 * p + par
                half = (lane < HEAD_DIM) if par == 0 else (lane >= HEAD_DIM)
                pn, ps = _attn_probs(qp, kv[j][par], own, prev_bias, _sink(sinks_v, r))
                delta = jnp.sum(jnp.where(half, prod, 0.0), axis=1, keepdims=True)
                dP = _fold(own, _dot_nt(do_b, vv[j][par]))
                dS = pn * (dP - delta)
                dsink = dsink + jnp.where(lane16 == r, -jnp.sum(ps * delta, axis=0, keepdims=True), 0.0)
                dS_parts = _unfold(own, dS.astype(BF16))
                p_parts = _unfold(own, pn.astype(BF16))
                at = ((p % PAIRS_PER_KV) * 2 + par) * WINDOW
                qt_scr[j, :, at:at + WINDOW] = q_t[par * HEAD_DIM:(par + 1) * HEAD_DIM]
                dot_scr[j, :, at:at + WINDOW] = do_t[par * HEAD_DIM:(par + 1) * HEAD_DIM]
                for blk in range(2):
                    dq_pair = dq_pair + _dot_nn(dS_parts[blk], kv[j][par][blk * WINDOW:(blk + 1) * WINDOW])
                    ds_scr[j, blk, at:at + WINDOW, :] = dS_parts[blk]
                    p_scr[j, blk, at:at + WINDOW, :] = p_parts[blk]
            dq_ref[:, p * 128:(p + 1) * 128] = (dq_pair * ATTN_SCALE).astype(BF16)
        rows = [pl.multiple_of(jnp.maximum(n - 1, 0) * WINDOW, WINDOW), pl.multiple_of(n * WINDOW, WINDOW)]
        for lhs, rhs, ref, scale in [(qt_scr, ds_scr, dk_ref, ATTN_SCALE), (dot_scr, p_scr, dv_ref, 1.0)]:
            for blk in range(2):
                both_t = jnp.concatenate([_dot_nn(lhs[j], rhs[j, blk]) for j in range(2)], axis=0)
                ref[pl.ds(rows[blk], WINDOW), :] += both_t.T * scale
        ds_ref[...] += dsink

    full_kv = pl.BlockSpec((T, KV_W), lambda n: (0, 0))
    blk = pl.BlockSpec((WINDOW, ATTN_W), lambda n: (n, 0))
    return pl.pallas_call(
        body, grid=(nb,),
        in_specs=[pl.BlockSpec((WINDOW, ATTN_W), lambda n: (n, OFF_Q // ATTN_W)), *_kv_specs(),
                  pl.BlockSpec((1, N_HEADS), lambda n: (0, 0)), pl.BlockSpec((1, ATTN_W), lambda n: (0, 0)),
                  blk, pl.BlockSpec((WINDOW, ATTN_W), lambda n: (n, 1)), ANY],
        out_specs=[pl.BlockSpec((WINDOW, ATTN_W), lambda n: (n, OFF_Q // ATTN_W)), full_kv, full_kv,
                   pl.BlockSpec((1, N_HEADS), lambda n: (0, 0)), pl.BlockSpec((1, ATTN_W), lambda n: (0, 0))],
        out_shape=[jax.ShapeDtypeStruct(dproj.shape, BF16), jax.ShapeDtypeStruct((T, KV_W), F32),
                   jax.ShapeDtypeStruct((T, KV_W), F32), jax.ShapeDtypeStruct((1, N_HEADS), F32),
                   jax.ShapeDtypeStruct((1, ATTN_W), F32)],
        scratch_shapes=[pltpu.VMEM((2, HEAD_DIM, 8 * WINDOW), BF16), pltpu.VMEM((2, HEAD_DIM, 8 * WINDOW), BF16),
                        pltpu.VMEM((2, 2, 8 * WINDOW, WINDOW), BF16), pltpu.VMEM((2, 2, 8 * WINDOW, WINDOW), BF16)],
        input_output_aliases={9: 0}, name=name, compiler_params=_cparams(("arbitrary",)),
    )(proj, proj, proj, proj, proj, sinks, og, o, dy, dproj)


ANY = pl.BlockSpec(memory_space=pl.ANY)


def _coords():
    return lax.axis_index("x"), lax.axis_index("y"), lax.axis_index("c")


HBM = pl.BlockSpec(memory_space=pltpu.HBM)
SEM = pl.BlockSpec(memory_space=pltpu.SEMAPHORE)
EFFECT = pltpu.SideEffectType.DATAFLOW_SIDE_EFFECTING


def _in_hbm(a):
    return pltpu.with_memory_space_constraint(a, pltpu.HBM)


def _remote_start(srcs, lands, plan, n_copies, name, after=None):
    ns, nb = len(srcs), len(srcs) + len(lands)
    n_after = 0 if after is None else 1

    def body(*refs):
        src_refs, land_refs = refs[:ns], refs[ns:nb]
        send_sems, recv_sems = refs[nb + n_after], refs[nb + n_after + 1]
        token = refs[-1]
        x, y, c = _coords()
        for i, (sv, dv, dev) in enumerate(plan(src_refs, land_refs, x, y, c)):
            pltpu.make_async_remote_copy(src_ref=sv, dst_ref=dv, send_sem=send_sems.at[i], recv_sem=recv_sems.at[i],
                                         device_id=dev, device_id_type=MESH).start()
        token[...] = jnp.zeros_like(token)

    bufs = list(srcs) + list(lands)
    outs = pl.pallas_call(
        body, name=name,
        out_shape=(pltpu.SemaphoreType.DMA((n_copies,)), pltpu.SemaphoreType.DMA((n_copies,)),
                   *[pltpu.HBM(b.shape, b.dtype) for b in bufs], jax.ShapeDtypeStruct((8, 128), F32)),
        in_specs=[HBM] * nb + [ANY] * n_after,
        out_specs=(SEM, SEM, *[HBM] * nb, pl.BlockSpec(memory_space=pltpu.VMEM)),
        input_output_aliases={i: 2 + i for i in range(nb)},
        compiler_params=pltpu.CompilerParams(has_side_effects=EFFECT),
    )(*[_in_hbm(b) for b in bufs], *([] if after is None else [after]))
    return outs[0], outs[1], list(outs[2:2 + ns]), list(outs[2 + ns:2 + nb]), outs[-1]


def _remote_wait(started, after, plan, name):
    send_sems, recv_sems, srcs, lands, _ = started
    ns, nb = len(srcs), len(srcs) + len(lands)

    def body(*refs):
        src_refs, land_refs = refs[:ns], refs[ns:nb]
        send_sems, recv_sems = refs[nb], refs[nb + 1]
        x, y, c = _coords()
        for i, (sv, dv, dev) in enumerate(plan(src_refs, land_refs, x, y, c)):
            cp = pltpu.make_async_remote_copy(src_ref=sv, dst_ref=dv, send_sem=send_sems.at[i],
                                              recv_sem=recv_sems.at[i], device_id=dev, device_id_type=MESH)
            cp.wait_send()
            cp.wait_recv()

    bufs = list(srcs) + list(lands)
    outs = pl.pallas_call(
        body, name=name, out_shape=tuple(pltpu.HBM(b.shape, b.dtype) for b in bufs),
        in_specs=[HBM] * nb + [SEM, SEM, ANY], out_specs=tuple([HBM] * nb),
        input_output_aliases={i: i for i in range(nb)},
        compiler_params=pltpu.CompilerParams(has_side_effects=EFFECT),
    )(*bufs, send_sems, recv_sems, after)
    return list(outs[:ns]), list(outs[ns:])


def _pair_plan(src_refs, land_refs, x, y, c):
    plan = []
    for s, l in zip(src_refs, land_refs):
        for q in range(4):
            plan.append((s.at[2 * q + (1 - c)], l.at[q], (x, y, 1 - c)))
    return plan


def _pair4_plan(src_refs, land_refs, x, y, c):
    plan = []
    for s, l in zip(src_refs, land_refs):
        for q in range(4):
            plan.append((s.at[q], l.at[q], (x, y, 1 - c)))
    return plan


def _chips_plan(src_refs, land_refs, x, y, c):
    plan = []
    for s, l in zip(src_refs, land_refs):
        for k, (tx, ty) in enumerate([(1 - x, y), (x, 1 - y), (1 - x, 1 - y)]):
            plan.append((s.at[2 * tx + ty], l.at[k], (tx, ty, c)))
    return plan


def _everyone_plan(src_refs, land_refs, x, y, c):
    me = 4 * x + 2 * y + c
    plan = []
    for s, l in zip(src_refs, land_refs):
        for fx, fy, fc in [(0, 0, 1), (1, 0, 0), (1, 0, 1), (0, 1, 0), (0, 1, 1), (1, 1, 0), (1, 1, 1)]:
            dev = ((1 - x) if fx else x, (1 - y) if fy else y, (1 - c) if fc else c)
            plan.append((s, l.at[me], dev))
    return plan


def _pair_add(g8, r1, csel, tr, name):
    _, R, C = r1.shape
    g4 = g8.reshape(4, 2, R, C)

    def body(c_ref, g_ref, r_ref, o_ref):
        o_ref[...] = (g_ref[...].astype(F32) + r_ref[...].astype(F32)).astype(BF16)

    return pl.pallas_call(
        body,
        grid_spec=pltpu.PrefetchScalarGridSpec(
            num_scalar_prefetch=1, grid=(4, R // tr),
            in_specs=[pl.BlockSpec((None, None, tr, C), lambda q, i, cs: (q, cs[0], i, 0)),
                      pl.BlockSpec((None, tr, C), lambda q, i, cs: (q, i, 0))],
            out_specs=pl.BlockSpec((None, tr, C), lambda q, i, cs: (q, i, 0))),
        out_shape=jax.ShapeDtypeStruct((4, R, C), BF16), name=name,
        compiler_params=_cparams(("parallel", "parallel")),
    )(csel, g4, r1)


def _adamw_math(w, g, m, v):
    m = ADAM_B1 * m + (1.0 - ADAM_B1) * g
    v = ADAM_B2 * v + (1.0 - ADAM_B2) * (g * g)
    m_hat = m / (1.0 - ADAM_B1 ** ADAM_STEP)
    v_hat = v / (1.0 - ADAM_B2 ** ADAM_STEP)
    delta = -ADAM_LR * (m_hat / (jnp.sqrt(v_hat) + ADAM_EPS) + ADAM_WD * w)
    return delta, m, v


def _adamw_big(w, m, v, p4, r3, qsel, tile, name):
    R, C = w.shape
    tr, tc = tile

    def body(q_ref, w_ref, m_ref, v_ref, p_ref, r_ref, g_out, d_out, m_out, v_out):
        g = p_ref[...].astype(F32) + r_ref[0].astype(F32) + r_ref[1].astype(F32) + r_ref[2].astype(F32)
        d, mn, vn = _adamw_math(w_ref[...], g, m_ref[...], v_ref[...])
        g_out[...] = g
        d_out[...] = d
        m_out[...] = mn
        v_out[...] = vn

    blk = pl.BlockSpec((tr, tc), lambda i, j, qs: (i, j))
    return pl.pallas_call(
        body,
        grid_spec=pltpu.PrefetchScalarGridSpec(
            num_scalar_prefetch=1, grid=(R // tr, C // tc),
            in_specs=[blk, blk, blk, pl.BlockSpec((None, tr, tc), lambda i, j, qs: (qs[0], i, j)),
                      pl.BlockSpec((3, tr, tc), lambda i, j, qs: (0, i, j))],
            out_specs=[blk, blk, blk, blk]),
        out_shape=[jax.ShapeDtypeStruct((R, C), F32)] * 4, name=name,
        compiler_params=_cparams(("parallel", "parallel")),
    )(qsel, w, m, v, p4, r3)


def _sum_partials(p4, r3, qsel, tc, name):
    _, R, C = p4.shape

    def body(q_ref, p_ref, r_ref, o_ref):
        o_ref[...] = p_ref[...].astype(F32) + r_ref[0].astype(F32) + r_ref[1].astype(F32) + r_ref[2].astype(F32)

    return pl.pallas_call(
        body,
        grid_spec=pltpu.PrefetchScalarGridSpec(
            num_scalar_prefetch=1, grid=(C // tc,),
            in_specs=[pl.BlockSpec((None, R, tc), lambda j, qs: (qs[0], 0, j)),
                      pl.BlockSpec((3, R, tc), lambda j, qs: (0, 0, j))],
            out_specs=pl.BlockSpec((R, tc), lambda j, qs: (0, j))),
        out_shape=jax.ShapeDtypeStruct((R, C), F32), name=name, compiler_params=_cparams(("parallel",)),
    )(qsel, p4, r3)


def _adamw_tiled(w, g, m, v, tc, name):
    R, C = w.shape

    def body(w_ref, g_ref, m_ref, v_ref, d_out, m_out, v_out):
        d, mn, vn = _adamw_math(w_ref[...], g_ref[...], m_ref[...], v_ref[...])
        d_out[...] = d
        m_out[...] = mn
        v_out[...] = vn

    blk = pl.BlockSpec((R, tc), lambda j: (0, j))
    return pl.pallas_call(
        body, grid=(C // tc,), in_specs=[blk] * 4, out_specs=[blk] * 3,
        out_shape=[jax.ShapeDtypeStruct((R, C), F32)] * 3, name=name, compiler_params=_cparams(("parallel",)),
    )(w, g, m, v)


def _small_sum(parts, name):
    def body(p_ref, o_ref):
        acc = p_ref[0]
        for d in range(1, N_DEV):
            acc = acc + p_ref[d]
        o_ref[...] = acc

    return pl.pallas_call(
        body, out_shape=jax.ShapeDtypeStruct(parts.shape[1:], F32), name=name,
        compiler_params=_cparams(),
    )(parts)


def _adamw_small(w, g, m, v, name):
    def body(w_ref, g_ref, m_ref, v_ref, d_out, m_out, v_out):
        d, mn, vn = _adamw_math(w_ref[...], g_ref[...], m_ref[...], v_ref[...])
        d_out[...] = d
        m_out[...] = mn
        v_out[...] = vn

    return pl.pallas_call(
        body, out_shape=[jax.ShapeDtypeStruct(w.shape, F32)] * 3, name=name, compiler_params=_cparams(),
    )(w, g, m, v)


def _row(*pieces):
    r = jnp.concatenate([p.reshape(1, -1) for p in pieces], axis=1)
    return jnp.pad(r, ((0, 0), (0, D_MODEL - r.shape[1])))


def _pack_small(mix, convb, ssmg, attng, mlpg, fing, convw, dtb, alog, dsk, sinks, extra=None):
    last = [dtb, alog, dsk, sinks] + ([extra] if extra is not None else [])
    rows = [_row(mix), _row(convb), _row(ssmg, attng), _row(mlpg), _row(fing),
            jnp.pad(convw, ((0, 0), (0, D_MODEL - convw.shape[1]))), _row(*last)]
    packed = jnp.concatenate(rows, axis=0)
    return jnp.pad(packed, ((0, SMALL_ROWS - packed.shape[0]), (0, 0)))


def _unpack_small(p, conv_n):
    return dict(
        mix_norm_g=p[0:1, :], conv_b=p[1:2, :], ssm_norm_g=p[2:3, :D_INNER], attn_out_norm_g=p[2:3, D_INNER:],
        mlp_norm_g=p[3:4, :], final_norm_g=p[4, :], conv_w=p[5:9, :conv_n][None],
        dt_bias=p[9:10, 0:16], A_log=p[9:10, 16:32], D_skip=p[9:10, 32:48], attn_sinks=p[9:10, 48:64])


WEIGHT_ORDER = ["mix_norm_g", "w_in", "conv_w", "conv_b", "dt_bias", "A_log", "D_skip", "ssm_norm_g", "attn_sinks",
                "attn_out_norm_g", "w_out", "mlp_norm_g", "w_up", "w_down", "final_norm_g"]


def _to_my_columns(w_nat):
    pad = jnp.zeros((w_nat.shape[0], NP - IN_PROJ), w_nat.dtype)
    return jnp.concatenate([w_nat[:, :NAT_DT], w_nat[:, NAT_DT + N_HEADS:], w_nat[:, NAT_DT:NAT_DT + N_HEADS], pad],
                           axis=1)


PER = IN_PROJ // N_DEV
SUPER_STEP = 544
SUPER = 576


def _natural_rows(g, lo, hi):
    segments = [(0, NAT_DT, 0), (NAT_DT, NAT_DT + N_HEADS, OFF_DT - NAT_DT), (NAT_DT + N_HEADS, IN_PROJ, -N_HEADS),
                (IN_PROJ, NP, 0)]
    pieces = [g[max(lo, a) + shift:min(hi, b) + shift] for a, b, shift in segments if max(lo, a) < min(hi, b)]
    return pieces[0] if len(pieces) == 1 else jnp.concatenate(pieces, axis=0)


def _w_in_from_super_slabs(sup):
    seam = SUPER - SUPER_STEP
    units = []
    for i in range(N_DEV):
        base = SUPER_STEP * i
        units.append((base, base + seam, sup[i, :seam] if i == 0 else sup[i - 1, SUPER_STEP:] + sup[i, :seam]))
        units.append((base + seam, base + SUPER_STEP, sup[i, seam:SUPER_STEP]))
    units.append((SUPER_STEP * N_DEV, SUPER_STEP * N_DEV + seam, sup[N_DEV - 1, SUPER_STEP:]))

    def natural(lo, hi):
        return [rows[max(lo, a) - a:min(hi, b) - a] for a, b, rows in units if max(lo, a) < min(hi, b)]

    pieces = natural(0, NAT_DT) + natural(NAT_DT + N_HEADS, IN_PROJ) + natural(NAT_DT, NAT_DT + N_HEADS)
    return jnp.concatenate(pieces + [jnp.zeros((NP - IN_PROJ, D_MODEL), sup.dtype)], axis=0)


def _to_natural_columns(w_my):
    return jnp.concatenate([w_my[:, :NAT_DT], w_my[:, OFF_DT:OFF_DT + N_HEADS], w_my[:, NAT_DT:OFF_DT]], axis=1)


SLAB = 1024


def _grad_w_up(h2, du, name, sel=None, add=None, after=None):
    T, D = h2.shape
    if sel is None:
        pick, n_slab, pre = (lambda j, *cs: j), N_DEV, None
    else:
        pre, other = sel
        pick, n_slab = (lambda j, cs: 2 * j + ((1 - cs[0]) if other else cs[0])), 4
    o_spec = pl.BlockSpec((None, SLAB, SLAB), lambda i, j, k, *cs: (j, i, 0))
    return _matmul(
        h2, du, mode="tn", grid=(D // SLAB, n_slab, 1),
        a_spec=pl.BlockSpec((T, SLAB), lambda i, j, k, *cs: (0, i)),
        b_spec=pl.BlockSpec((T, SLAB), lambda i, j, k, *cs: (0, pick(j, *cs))),
        out_shapes=[jax.ShapeDtypeStruct((n_slab, D, SLAB), BF16)], out_specs=[o_spec], tile=(SLAB, SLAB), name=name,
        extras=() if add is None else (add,), extra_specs=() if add is None else (o_spec,),
        epilogue=None if add is None else (lambda acc, r: (acc + r.astype(F32),)), after=after, prefetch=pre)[0]


def _grad_w_down(act, dx3b, name, sel=None, add=None, after=None):
    T, D = dx3b.shape
    if sel is None:
        pick, n_slab, pre = (lambda i, *cs: i), N_DEV, None
    else:
        pre, other = sel
        pick, n_slab = (lambda i, cs: 2 * i + ((1 - cs[0]) if other else cs[0])), 4
    o_spec = pl.BlockSpec((None, SLAB, SLAB), lambda i, j, k, *cs: (i, 0, j))
    return _matmul(
        act, dx3b, mode="tn", grid=(n_slab, D // SLAB, 1),
        a_spec=pl.BlockSpec((T, SLAB), lambda i, j, k, *cs: (0, pick(i, *cs))),
        b_spec=pl.BlockSpec((T, SLAB), lambda i, j, k, *cs: (0, j)),
        out_shapes=[jax.ShapeDtypeStruct((n_slab, SLAB, D), BF16)], out_specs=[o_spec], tile=(SLAB, SLAB), name=name,
        extras=() if add is None else (add,), extra_specs=() if add is None else (o_spec,),
        epilogue=None if add is None else (lambda acc, r: (acc + r.astype(F32),)), after=after, prefetch=pre)[0]


class _FixedWeights:
    def __init__(self, w_in_p, w_out_f, w_up_s, w_down_f, conv_w_f):
        self.w = (w_in_p, w_out_f, w_up_s, w_down_f, conv_w_f)
        self.grads = {}

    def mixer_weights(self, after):
        return self.w[0], None

    def conv_weight(self, after):
        return self.w[4]

    def out_weight(self, after):
        return self.w[1]

    def up_weight(self, after):
        return self.w[2]

    def down_weight(self, h, after):
        return self.w[3][:, h * (D_MODEL // 2):(h + 1) * (D_MODEL // 2)]

    def mlp_grads(self, h2, du, act, dx3b):
        self.grads.update(w_up=_grad_w_up(h2, du, "grad_w_up"),
                          w_down=_grad_w_down(act, dx3b, "grad_w_down").reshape(D_FF, D_MODEL))
        return None

    def grad_sent(self, tag, after):
        return None

    def out_grad(self, g_out):
        self.grads.update(w_out=g_out)
        return None

    def in_grad(self, g_in):
        self.grads.update(w_in=g_in)
        return None


def _local_step(x, tgt, p, hooks):
    T = x.shape[0]
    D = D_MODEL
    h1 = _rmsnorm_fwd(x, p["mix_norm_g"], "norm_mix")
    w_in_t, token = hooks.mixer_weights(h1)
    (proj,) = _mm_simple(h1, w_in_t, mode="nt", M=T, N=NP, K=D, tm=min(T, 1024), tn=1536, tk=D, out_dtype=F32,
                         name="in_proj", after=token)
    conv_w_f = hooks.conv_weight(proj)
    xbc = _conv_fwd(proj, conv_w_f, p["conv_b"], "conv_fwd")
    dtT = proj[:, OFF_DT:OFF_DT + N_HEADS].T
    dtbT = p["dt_bias"].T
    alogT = p["A_log"].T
    dfull = jnp.repeat(p["D_skip"], HEAD_DIM, axis=1)
    ycat, ypre, hs = _ssd_fwd(xbc, proj, dtT, p["dt_bias"], dtbT, p["A_log"], alogT, dfull, p["ssm_norm_g"],
                              "ssd_fwd")
    ycat, o_att = _attn_fwd(proj, p["attn_sinks"], p["attn_out_norm_g"], ycat, "attn_fwd")
    w_out_f = hooks.out_weight(ycat)
    tm = min(T, 1024)
    def residual_and_norm(acc, res, gain):
        x2 = acc + res
        return x2, x2 * lax.rsqrt(jnp.mean(x2 * x2, axis=-1, keepdims=True) + EPS) * gain

    rows = min(T, 512)
    x2, h2 = _matmul(
        ycat, w_out_f, mode="nn", grid=(T // rows, 1, 1),
        a_spec=pl.BlockSpec((rows, D), lambda i, j, k: (i, 0)), b_spec=pl.BlockSpec((D, D), lambda i, j, k: (0, 0)),
        out_shapes=[jax.ShapeDtypeStruct((T, D), F32), jax.ShapeDtypeStruct((T, D), BF16)],
        out_specs=[pl.BlockSpec((rows, D), lambda i, j, k: (i, 0))] * 2, tile=(rows, D), name="out_proj",
        extras=(x, p["mlp_norm_g"]),
        extra_specs=[pl.BlockSpec((rows, D), lambda i, j, k: (i, 0)), pl.BlockSpec((1, D), lambda i, j, k: (0, 0))],
        epilogue=residual_and_norm)
    w_up_s = hooks.up_weight(h2)
    grid = (T // tm, N_DEV, 1)
    u, act = _matmul(
        h2, w_up_s, mode="nn", grid=grid,
        a_spec=pl.BlockSpec((tm, D), lambda i, j, k: (i, 0)),
        b_spec=pl.BlockSpec((None, D, 1024), lambda i, j, k: (j, 0, 0)),
        out_shapes=[jax.ShapeDtypeStruct((T, D_FF), F32), jax.ShapeDtypeStruct((T, D_FF), BF16)],
        out_specs=[pl.BlockSpec((tm, 1024), lambda i, j, k: (i, j))] * 2, tile=(tm, 1024), name="mlp_up",
        epilogue=lambda acc: (acc, jnp.square(jnp.maximum(acc, 0.0))))
    half = D // 2
    w_down_halves, x3_halves = [], []
    for h in range(2):
        w_down_halves.append(hooks.down_weight(h, act if h == 0 else x3_halves[0]))
        x3_halves.append(_matmul(
            act, w_down_halves[h], mode="nn", grid=(T // tm, 1, D_FF // 2048),
            a_spec=pl.BlockSpec((tm, 2048), lambda i, j, k: (i, k)),
            b_spec=pl.BlockSpec((2048, half), lambda i, j, k: (k, 0)),
            out_shapes=[jax.ShapeDtypeStruct((T, half), F32)],
            out_specs=[pl.BlockSpec((tm, half), lambda i, j, k: (i, 0))], tile=(tm, half), name=f"mlp_down_{h}",
            extras=(x2,), extra_specs=[pl.BlockSpec((tm, half), lambda i, j, k, h=h: (i, h))],
            epilogue=lambda acc, res: (acc + res,))[0])
    loss_part, d_fin, dx3, dx3b = _final_loss(x3_halves, tgt, p["final_norm_g"].reshape(1, D), "loss_head")
    (du,) = _matmul(
        dx3b, tuple(w_down_halves), mode="nt", grid=(T // tm, D_FF // 1024, 1),
        a_spec=pl.BlockSpec((tm, D), lambda i, j, k: (i, 0)),
        b_spec=(pl.BlockSpec((1024, half), lambda i, j, k: (j, 0)),) * 2,
        out_shapes=[jax.ShapeDtypeStruct((T, D_FF), BF16)],
        out_specs=[pl.BlockSpec((tm, 1024), lambda i, j, k: (i, j))], tile=(tm, 1024), name="mlp_down_bwd",
        extras=(u,), extra_specs=[pl.BlockSpec((tm, 1024), lambda i, j, k: (i, j))],
        epilogue=lambda acc, uu: (acc * (2.0 * jnp.maximum(uu, 0.0)),),
        dot_fn=lambda a, b0, b1: _dot_nt(a[:, :half], b0) + _dot_nt(a[:, half:], b1))
    token = hooks.mlp_grads(h2, du, act, dx3b)
    (dh2,) = _matmul(
        du, w_up_s, mode="nt", grid=(T // tm, D // 1024, N_DEV // 4),
        a_spec=pl.BlockSpec((tm, 4096), lambda i, j, k: (i, k)),
        b_spec=pl.BlockSpec((4, 1024, 1024), lambda i, j, k: (k, j, 0)),
        out_shapes=[jax.ShapeDtypeStruct((T, D), F32)],
        out_specs=[pl.BlockSpec((tm, 1024), lambda i, j, k: (i, j))], tile=(tm, 1024), name="mlp_up_bwd",
        after=token, dot_fn=lambda a, b: sum(_dot_nt(a[:, s * SLAB:(s + 1) * SLAB], b[s]) for s in range(4)))
    dx2, dx2b, d_mlp = _rmsnorm_bwd(dh2, x2, p["mlp_norm_g"], dx3, "norm_mlp_bwd")
    (g_out,) = _mm_simple(ycat, dx2b, mode="tn", M=D, N=D, K=T, tm=1024, tn=1024, tk=T, out_dtype=BF16,
                          name="grad_w_out")
    token = hooks.out_grad(g_out)
    (dy,) = _mm_simple(dx2b, w_out_f, mode="nt", M=T, N=D, K=D, tm=tm, tn=1024, tk=D, out_dtype=F32,
                       name="out_proj_bwd", after=token)
    token = hooks.grad_sent("out", dy)
    ssm_g = p["ssm_norm_g"] if token is None else p["ssm_norm_g"] + token[0:1, 0:1]
    dproj, dxbc_act, d_dtb, d_alog, d_dskip, d_ssmg = _ssd_bwd(
        xbc, proj, dtT, p["dt_bias"], dtbT, p["A_log"], alogT, dfull, ssm_g, ypre, hs, dy, "ssd_bwd")
    dproj, d_convw, d_convb = _conv_bwd(proj, dxbc_act, conv_w_f, p["conv_b"], dproj, "conv_bwd")
    dproj, dk, dv, d_sinks, d_attng = _attn_bwd(proj, p["attn_sinks"], p["attn_out_norm_g"], o_att, dy, dproj,
                                                "attn_bwd")
    dproj = lax.dynamic_update_slice(dproj, jnp.concatenate([dk, dv], axis=1).astype(BF16), (0, OFF_K))
    (g_in,) = _mm_simple(dproj, h1, mode="tn", M=NP, N=D, K=T, tm=1536, tn=1024, tk=T, out_dtype=BF16,
                         name="grad_w_in")
    token = hooks.in_grad(g_in)
    (dh1,) = _mm_simple(dproj, w_in_t, mode="nn", M=T, N=D, K=NP, tm=tm, tn=1024, tk=2304, out_dtype=F32,
                        name="in_proj_bwd", after=token)
    token = hooks.grad_sent("in", dh1)
    mix_g = p["mix_norm_g"] if token is None else p["mix_norm_g"] + token[0:1, 0:1]
    dx, d_mix = _rmsnorm_bwd(dh1, x, mix_g, dx2, "norm_mix_bwd", with_bf16=False)
    small = _pack_small(d_mix, d_convb, d_ssmg, d_attng, d_mlp, d_fin, d_convw, d_dtb, d_alog, d_dskip, d_sinks,
                        extra=loss_part[:, 0:1])
    return dx, small


def _rows_rotated(v, shift, name):
    R, C = v.shape
    tc = 512

    def body(s_ref, v_ref, o_ref):
        o_ref[...] = pltpu.roll(v_ref[...], s_ref[0], axis=0).astype(BF16)

    return pl.pallas_call(
        body,
        grid_spec=pltpu.PrefetchScalarGridSpec(
            num_scalar_prefetch=1, grid=(C // tc,), in_specs=[pl.BlockSpec((R, tc), lambda j, s: (0, j))],
            out_specs=pl.BlockSpec((R, tc), lambda j, s: (0, j))),
        out_shape=jax.ShapeDtypeStruct((R, C), BF16), name=name, compiler_params=_cparams(("parallel",)),
    )(shift, v)


def _landing(own, me):
    zone = lax.empty((N_DEV,) + own.shape, own.dtype)
    return lax.dynamic_update_slice(zone, own[None], (me,) + (0,) * own.ndim)


def _sequencer_gather(owns, split, me, collective_id, name):
    n = len(owns)
    zone_refs = [jax.new_ref(_landing(o, me), memory_space=pltpu.MemorySpace.HBM) for o in owns]
    own_refs = [jax.new_ref(o, memory_space=pltpu.MemorySpace.HBM) for o in owns]
    N_COPIES = 9

    @pl.kernel(mesh=plsc.ScalarSubcoreMesh(axis_name="sequencer", num_cores=1), name=name,
               scratch_types=(pltpu.SemaphoreType.DMA((n, N_COPIES)), pltpu.SemaphoreType.DMA((n, N_COPIES))),
               compiler_params=pltpu.CompilerParams(collective_id=collective_id))
    def launch(send_sems, recv_sems):
        x, y, c = _coords()
        sibling, xn, yn, diag = (x, y, 1 - c), (1 - x, y, c), (x, 1 - y, c), (1 - x, 1 - y, c)
        barrier = pltpu.get_barrier_semaphore()
        for peer in [sibling, xn, yn, diag]:
            pl.semaphore_signal(barrier, inc=1, device_id=peer, device_id_type=MESH)
        pl.semaphore_wait(barrier, 4)

        def block(a, dev, half=None):
            ref = zone_refs[a].at[4 * dev[0] + 2 * dev[1] + dev[2]]
            if half is None:
                return ref
            rows = owns[a].shape[0] // 2
            return ref.at[pl.ds(half * rows, rows)]

        def copy(a, k, src, dst, to):
            return pltpu.make_async_remote_copy(src_ref=src, dst_ref=dst, send_sem=send_sems.at[a, k],
                                                recv_sem=recv_sems.at[a, k], device_id=to, device_id_type=MESH)

        me_dev = (x, y, c)
        sent = []
        first = {}
        for a in range(n):
            for k, peer in enumerate([sibling, xn, yn] + ([] if split[a] else [diag])):
                first[a, k] = copy(a, k, own_refs[a], block(a, me_dev), peer)
                first[a, k].start()
                sent.append(first[a, k])
        from_sibling = []
        for a in range(n):
            first[a, 1].wait_recv()
            sent.append(copy(a, 4, block(a, xn), block(a, xn), sibling))
            if split[a]:
                sent.append(copy(a, 6, block(a, xn, 0), block(a, xn, 0), yn))
            first[a, 2].wait_recv()
            sent.append(copy(a, 5, block(a, yn), block(a, yn), sibling))
            if split[a]:
                sent.append(copy(a, 7, block(a, yn, 1), block(a, yn, 1), xn))
            for cp in sent[-(4 if split[a] else 2):]:
                cp.start()
        for a in range(n):
            if split[a]:
                copy(a, 6, block(a, diag, 0), block(a, diag, 0), yn).wait_recv()
                sent.append(copy(a, 8, block(a, diag, 0), block(a, diag, 0), sibling))
                sent[-1].start()
                copy(a, 7, block(a, diag, 1), block(a, diag, 1), xn).wait_recv()
                sent.append(copy(a, 3, block(a, diag, 1), block(a, diag, 1), sibling))
                sent[-1].start()
            else:
                first[a, 3].wait_recv()
                sent.append(copy(a, 8, block(a, diag), block(a, diag), sibling))
                sent[-1].start()
        for a in range(n):
            first[a, 0].wait_recv()
            copy(a, 4, block(a, xn), block(a, xn), sibling).wait_recv()
            copy(a, 5, block(a, yn), block(a, yn), sibling).wait_recv()
            if split[a]:
                copy(a, 8, block(a, diag, 0), block(a, diag, 0), sibling).wait_recv()
                copy(a, 3, block(a, diag, 1), block(a, diag, 1), sibling).wait_recv()
            else:
                copy(a, 8, block(a, diag), block(a, diag), sibling).wait_recv()
        for cp in sent:
            cp.wait_send()

    launch()
    return zone_refs


class _ShardedWeights:
    def __init__(self, w_in, w_out, conv_w, w_up, w_down, me, csel):
        self.me, self.csel = me, csel
        padded = jnp.pad(jnp.transpose(w_in), ((0, SUPER - PER), (0, 0)))
        own_rows = _rows_rotated(padded, jnp.reshape(2 * me, (1,)).astype(jnp.int32), "w_in_super_slab")
        (self.in_ref,) = _sequencer_gather([own_rows], [True], me, 7, "gather_w_in_sequencer")
        self.out_ref, self.conv_ref = _sequencer_gather([w_out.astype(BF16), conv_w], [True, False], me, 8,
                                                        "gather_w_out_sequencer")
        (self.up_ref,) = _sequencer_gather([w_up.astype(BF16)], [True], me, 9, "gather_w_up_sequencer")
        down = w_down.astype(BF16)
        self.down_refs = [_sequencer_gather([down[:, h * (D_MODEL // 2):(h + 1) * (D_MODEL // 2)]], [True], me, 10 + h,
                                            f"gather_w_down_{h}_sequencer")[0] for h in range(2)]
        self.reduces = {}
        self.pairs = {}

    def mixer_weights(self, after):
        return _w_in_from_super_slabs(self.in_ref[...]), None

    def conv_weight(self, after):
        g_conv = self.conv_ref[...]
        return jnp.concatenate([g_conv[i] for i in range(N_DEV)], axis=1)

    def out_weight(self, after):
        return self.out_ref[...].reshape(D_MODEL, D_MODEL)

    def up_weight(self, after):
        return self.up_ref[...]

    def down_weight(self, h, after):
        return self.down_refs[h][...].reshape(D_FF, D_MODEL // 2)

    def _chips_start(self, slabs, from_sibling, rows, tag):
        sums = [_pair_add(s, r, self.csel, tr, f"pair_add_{tag}_{i}")
                for i, (s, r, tr) in enumerate(zip(slabs, from_sibling, rows))]
        lands = [lax.empty((3,) + s.shape[1:], s.dtype) for s in sums]
        self.reduces[tag] = _remote_start(sums, lands, _chips_plan, 3 * len(sums), f"reduce_start_{tag}")
        return self.reduces[tag][4]

    def mlp_grads(self, h2, du, act, dx3b):
        def send(part, tag, after):
            st = _remote_start([part], [lax.empty(part.shape, part.dtype)], _pair4_plan, 4,
                               f"reduce_pair_start_{tag}", after=after)
            return st

        def received(st, after, tag):
            return _remote_wait(st, after, _pair4_plan, f"reduce_pair_wait_{tag}")[1][0]

        def to_chips(sums, tag):
            self.reduces[tag] = _remote_start([sums], [lax.empty((3,) + sums.shape[1:], sums.dtype)], _chips_plan, 3,
                                              f"reduce_start_{tag}")
            return self.reduces[tag][4]

        up_send = _grad_w_up(h2, du, "grad_w_up_send", sel=(self.csel, True))
        st_up = send(up_send, "up", None)
        down_send = _grad_w_down(act, dx3b, "grad_w_down_send", sel=(self.csel, True), after=st_up[4])
        st_down = send(down_send, "down", None)
        up_sum = _grad_w_up(h2, du, "grad_w_up_keep", sel=(self.csel, False), add=received(st_up, down_send, "up"),
                            after=st_down[4])
        token = to_chips(up_sum, "up")
        down_sum = _grad_w_down(act, dx3b, "grad_w_down_keep", sel=(self.csel, False),
                                add=received(st_down, up_sum, "down"), after=token)
        return to_chips(down_sum, "down")

    def _pair_start(self, slabs, tag):
        land = lax.empty((4,) + slabs.shape[1:], slabs.dtype)
        self.pairs[tag] = _remote_start([slabs], [land], _pair_plan, 4, f"reduce_pair_start_{tag}")
        return self.pairs[tag][4]

    def grad_sent(self, tag, after):
        slabs, from_sibling = _remote_wait(self.pairs[tag], after, _pair_plan, f"reduce_pair_wait_{tag}")
        return self._chips_start(slabs, from_sibling, [slabs[0].shape[1]], tag)

    def out_grad(self, g_out):
        return self._pair_start(g_out.reshape(N_DEV, D_MODEL // N_DEV, D_MODEL), "out")

    def in_grad(self, g_in):
        return self._pair_start(
            jnp.stack([_natural_rows(g_in, SUPER_STEP * j, SUPER_STEP * j + SUPER) for j in range(N_DEV)]), "in")

    def small_start(self, small):
        self.st_small = _remote_start([small], [_landing(small, self.me)], _everyone_plan, N_DEV - 1, "gather_start_small")

    def small_end(self, after):
        return _remote_wait(self.st_small, after, _everyone_plan, "gather_small_wait")[1][0]

    def reduce_end(self, tag, after):
        return _remote_wait(self.reduces[tag], after, _chips_plan, f"reduce_wait_{tag}")


def kernel(x, mix_norm_g, w_in, conv_w, conv_b, dt_bias, A_log, D_skip, ssm_norm_g, attn_sinks, attn_out_norm_g, w_out, mlp_norm_g, w_up, w_down, final_norm_g, loss_target, m_mix_norm_g, m_w_in, m_conv_w, m_conv_b, m_dt_bias, m_A_log, m_D_skip, m_ssm_norm_g, m_attn_sinks, m_attn_out_norm_g, m_w_out, m_mlp_norm_g, m_w_up, m_w_down, m_final_norm_g, v_mix_norm_g, v_w_in, v_conv_w, v_conv_b, v_dt_bias, v_A_log, v_D_skip, v_ssm_norm_g, v_attn_sinks, v_attn_out_norm_g, v_w_out, v_mlp_norm_g, v_w_up, v_w_down, v_final_norm_g):
    xi, yi, ci = _coords()
    me = 4 * xi + 2 * yi + ci
    csel = jnp.reshape(ci, (1,)).astype(jnp.int32)
    qsel = jnp.reshape(2 * xi + yi, (1,)).astype(jnp.int32)
    w = dict(mix_norm_g=mix_norm_g, conv_b=conv_b, dt_bias=dt_bias, A_log=A_log, D_skip=D_skip,
             ssm_norm_g=ssm_norm_g, attn_sinks=attn_sinks, attn_out_norm_g=attn_out_norm_g, mlp_norm_g=mlp_norm_g,
             final_norm_g=final_norm_g)
    hooks = _ShardedWeights(w_in[0], w_out[0], conv_w[0], w_up[0], w_down[0], me, csel)
    p = dict(w)
    dx, small = _local_step(x[0], loss_target[0], p, hooks)
    hooks.small_start(small)
    big = {}
    after = dx
    for name, wt, mt, vt, tile in [
            ("up", w_up, m_w_up, v_w_up, (512, SLAB)), ("down", w_down, m_w_down, v_w_down, (256, D_MODEL)),
            ("out", w_out, m_w_out, v_w_out, (256, D_MODEL))]:
        (chip_sums,), (from_chips,) = hooks.reduce_end(name, after)
        res = _adamw_big(wt[0], mt[0], vt[0], chip_sums, from_chips, qsel, tile, f"adamw_w_{name}")
        big["w_" + name] = tuple(r[None] for r in res)
        after = res[0]
    (chip_sums,), (from_chips,) = hooks.reduce_end("in", after)
    g_super = _sum_partials(chip_sums, from_chips, qsel, 512, "grad_w_in_sum")
    g_in = lax.dynamic_slice(g_super, (2 * me, 0), (PER, D_MODEL))
    res = _adamw_tiled(jnp.transpose(w_in[0]), g_in, jnp.transpose(m_w_in[0]), jnp.transpose(v_w_in[0]), 512,
                       "adamw_w_in")
    big["w_in"] = tuple(jnp.transpose(r)[None] for r in (g_in, *res))
    after = res[0]
    gsum = _small_sum(hooks.small_end(after), "small_sum")
    loss = gsum[9, 64]
    gs = _unpack_small(gsum, CONV_DIM)
    cw = CONV_DIM // N_DEV
    g_conv_shard = lax.dynamic_slice(gsum[5:9, :], (0, me * cw), (CONV_K, cw))

    def pack(s):
        return _pack_small(s["mix_norm_g"], s["conv_b"], s["ssm_norm_g"], s["attn_out_norm_g"], s["mlp_norm_g"],
                           s["final_norm_g"], s["conv_w"][0], s["dt_bias"], s["A_log"], s["D_skip"], s["attn_sinks"])

    wp = pack(dict(w, conv_w=conv_w))
    mp = pack(dict(mix_norm_g=m_mix_norm_g, conv_b=m_conv_b, ssm_norm_g=m_ssm_norm_g,
                   attn_out_norm_g=m_attn_out_norm_g, mlp_norm_g=m_mlp_norm_g, final_norm_g=m_final_norm_g,
                   conv_w=m_conv_w, dt_bias=m_dt_bias, A_log=m_A_log, D_skip=m_D_skip, attn_sinks=m_attn_sinks))
    vp = pack(dict(mix_norm_g=v_mix_norm_g, conv_b=v_conv_b, ssm_norm_g=v_ssm_norm_g,
                   attn_out_norm_g=v_attn_out_norm_g, mlp_norm_g=v_mlp_norm_g, final_norm_g=v_final_norm_g,
                   conv_w=v_conv_w, dt_bias=v_dt_bias, A_log=v_A_log, D_skip=v_D_skip, attn_sinks=v_attn_sinks))
    gp = jnp.concatenate([gsum[0:5], jnp.pad(g_conv_shard, ((0, 0), (0, D_MODEL - cw))), gsum[9:10],
                          jnp.zeros((SMALL_ROWS - 10, D_MODEL), F32)], axis=0)
    dp, mnp, vnp = _adamw_small(wp, gp, mp, vp, "adamw_small")
    grads = dict(gs, conv_w=g_conv_shard[None])
    deltas = _unpack_small(dp, cw)
    new_m = _unpack_small(mnp, cw)
    new_v = _unpack_small(vnp, cw)
    for k, name in enumerate(["w_in", "w_out", "w_up", "w_down"]):
        grads[name], deltas[name], new_m[name], new_v[name] = big[name]
    return (loss, dx[None], *[grads[n] for n in WEIGHT_ORDER], *[deltas[n] for n in WEIGHT_ORDER],
            *[new_m[n] for n in WEIGHT_ORDER], *[new_v[n] for n in WEIGHT_ORDER])
```

```python
import jax
import jax.numpy as jnp
from jax import lax
from jax.experimental import pallas as pl
from jax.experimental.pallas import tpu as pltpu
from jax.experimental.pallas import tpu_sc as plsc

F32 = jnp.float32
BF16 = jnp.bfloat16
MESH = pl.DeviceIdType.MESH

EPS = 1e-5
D_MODEL = 2048
D_INNER = 1024
N_HEADS = 16
HEAD_DIM = 64
N_GROUPS = 4
D_STATE = 128
CHUNK = 128
CONV_K = 4
CONV_DIM = 2048
ATTN_W = 1024
KV_W = 128
WINDOW = 128
D_FF = 8192
IN_PROJ = 4368
N_DEV = 8
NP = 4608
OFF_Z, OFF_X, OFF_B, OFF_C, OFF_Q, OFF_K, OFF_V, OFF_DT = 0, 1024, 2048, 2560, 3072, 4096, 4224, 4352
NAT_DT = 3072

ADAM_LR = 0.001
ADAM_B1 = 0.9
ADAM_B2 = 0.999
ADAM_EPS = 1e-08
ADAM_WD = 0.01
ADAM_STEP = 10

VMEM_LIMIT = 52 * 1024 * 1024
SMALL_ROWS = 16
NEG = -1e30


def _cparams(sem=None):
    return pltpu.CompilerParams(dimension_semantics=sem, vmem_limit_bytes=VMEM_LIMIT)


def _split3(v):
    hi = v.astype(BF16)
    rest = v - hi.astype(F32)
    mid = rest.astype(BF16)
    return hi, mid, (rest - mid.astype(F32)).astype(BF16)


def _hdot(a, b, data):
    if data == "a":
        sel = b.astype(BF16)
        return sum(_dot_nn(part, sel) for part in _split3(a))
    sel = a.astype(BF16)
    return sum(_dot_nn(sel, part) for part in _split3(b))


def _dot_nn(a, b):
    return lax.dot_general(a, b, (((1,), (0,)), ((), ())), preferred_element_type=F32)


def _dot_nt(a, b):
    return lax.dot_general(a, b, (((1,), (1,)), ((), ())), preferred_element_type=F32)


def _dot_tn(a, b):
    return lax.dot_general(a, b, (((0,), (0,)), ((), ())), preferred_element_type=F32)


def _softplus(v):
    return jnp.maximum(v, 0.0) + jnp.log1p(jnp.exp(-jnp.abs(v)))


def _sigmoid(v):
    return 1.0 / (1.0 + jnp.exp(-v))


def _matmul(a, b, *, mode, grid, a_spec, b_spec, out_shapes, out_specs, tile, name,
            extras=(), extra_specs=(), epilogue=None, after=None, dot_fn=None, prefetch=None):
    nk = grid[2]
    n_ex = len(extras)
    n_out = len(out_shapes)
    bs, b_specs = (b, b_spec) if isinstance(b, tuple) else ((b,), (b_spec,))
    n_in = 1 + len(bs)
    dot = dot_fn if dot_fn is not None else {"nn": _dot_nn, "nt": _dot_nt, "tn": _dot_tn}[mode]

    def finish(acc, ex_refs, out_refs):
        res = (acc,) if epilogue is None else epilogue(acc, *[e[...] for e in ex_refs])
        for o, r in zip(out_refs, res):
            o[...] = r.astype(o.dtype)

    def body(*refs):
        ex_refs = refs[n_in:n_in + n_ex]
        out_refs = refs[n_in + n_ex:n_in + n_ex + n_out]
        part = dot(*[r[...].astype(BF16) for r in refs[:n_in]])
        if nk == 1:
            finish(part, ex_refs, out_refs)
        else:
            acc_ref = refs[-1]
            k = pl.program_id(2)

            @pl.when(k == 0)
            def _():
                acc_ref[...] = part

            @pl.when(k > 0)
            def _():
                acc_ref[...] += part

            @pl.when(k == nk - 1)
            def _():
                finish(acc_ref[...], ex_refs, out_refs)

    scratch = [] if nk == 1 else [pltpu.VMEM(tile, F32)]
    n_pre = 0 if prefetch is None else 1
    tok_specs = [] if after is None else [pl.BlockSpec((8, 128), lambda *_: (0, 0))]
    tok_args = [] if after is None else [after]

    def body_with_token(*refs):
        refs = refs[n_pre:]
        body(*refs[:n_in + n_ex], *refs[n_in + n_ex + len(tok_args):])

    in_specs = [a_spec, *b_specs, *extra_specs, *tok_specs]
    params = _cparams(("parallel", "parallel", "arbitrary"))
    if prefetch is None:
        return pl.pallas_call(
            body_with_token, grid=grid, in_specs=in_specs, out_specs=list(out_specs), out_shape=list(out_shapes),
            scratch_shapes=scratch, name=name, compiler_params=params)(a, *bs, *extras, *tok_args)
    return pl.pallas_call(
        body_with_token,
        grid_spec=pltpu.PrefetchScalarGridSpec(num_scalar_prefetch=1, grid=grid, in_specs=in_specs,
                                               out_specs=list(out_specs), scratch_shapes=scratch),
        out_shape=list(out_shapes), name=name, compiler_params=params)(prefetch, a, *bs, *extras, *tok_args)


def _mm_simple(a, b, *, mode, M, N, K, tm, tn, tk, out_dtype, name, extras=(), epilogue=None, n_out=1,
               out_dtypes=None, after=None):
    grid = (M // tm, N // tn, K // tk)
    if mode == "nn":
        a_spec = pl.BlockSpec((tm, tk), lambda i, j, k: (i, k))
        b_spec = pl.BlockSpec((tk, tn), lambda i, j, k: (k, j))
    elif mode == "nt":
        a_spec = pl.BlockSpec((tm, tk), lambda i, j, k: (i, k))
        b_spec = pl.BlockSpec((tn, tk), lambda i, j, k: (j, k))
    else:
        a_spec = pl.BlockSpec((tk, tm), lambda i, j, k: (k, i))
        b_spec = pl.BlockSpec((tk, tn), lambda i, j, k: (k, j))
    o_spec = pl.BlockSpec((tm, tn), lambda i, j, k: (i, j))
    dts = out_dtypes if out_dtypes is not None else [out_dtype] * n_out
    return _matmul(a, b, mode=mode, grid=grid, a_spec=a_spec, b_spec=b_spec,
                   out_shapes=[jax.ShapeDtypeStruct((M, N), d) for d in dts],
                   out_specs=[o_spec] * len(dts), tile=(tm, tn), name=name,
                   extras=extras, extra_specs=[o_spec] * len(extras), epilogue=epilogue, after=after)


ROW_BLOCK = 256


def _rmsnorm_fwd(x, g, name):
    T, D = x.shape

    def body(x_ref, g_ref, o_ref):
        xf = x_ref[...]
        r = lax.rsqrt(jnp.mean(xf * xf, axis=-1, keepdims=True) + EPS)
        o_ref[...] = (xf * r * g_ref[...]).astype(BF16)

    return pl.pallas_call(
        body, grid=(T // ROW_BLOCK,),
        in_specs=[pl.BlockSpec((ROW_BLOCK, D), lambda i: (i, 0)), pl.BlockSpec((1, D), lambda i: (0, 0))],
        out_specs=pl.BlockSpec((ROW_BLOCK, D), lambda i: (i, 0)),
        out_shape=jax.ShapeDtypeStruct((T, D), BF16), name=name, compiler_params=_cparams(("parallel",)),
    )(x, g)


def _rmsnorm_bwd(dh, x, g, dres, name, with_bf16=True):
    T, D = x.shape

    def body(dh_ref, x_ref, g_ref, dres_ref, dx_ref, *rest):
        dg_ref = rest[-1]
        i = pl.program_id(0)
        xf = x_ref[...]
        r = lax.rsqrt(jnp.mean(xf * xf, axis=-1, keepdims=True) + EPS)
        xh = xf * r
        d = dh_ref[...]

        @pl.when(i == 0)
        def _():
            dg_ref[...] = jnp.zeros_like(dg_ref)

        dg_ref[...] += jnp.sum(d * xh, axis=0, keepdims=True)
        dxh = d * g_ref[...]
        dx = r * (dxh - xh * jnp.mean(dxh * xh, axis=-1, keepdims=True)) + dres_ref[...]
        dx_ref[...] = dx
        if with_bf16:
            rest[0][...] = dx.astype(BF16)

    row = pl.BlockSpec((ROW_BLOCK, D), lambda i: (i, 0))
    vec = pl.BlockSpec((1, D), lambda i: (0, 0))
    copies = [(row, jax.ShapeDtypeStruct((T, D), BF16))] if with_bf16 else []
    return pl.pallas_call(
        body, grid=(T // ROW_BLOCK,), in_specs=[row, row, vec, row],
        out_specs=[row, *[c[0] for c in copies], vec],
        out_shape=[jax.ShapeDtypeStruct((T, D), F32), *[c[1] for c in copies], jax.ShapeDtypeStruct((1, D), F32)],
        name=name, compiler_params=_cparams(("arbitrary",)),
    )(dh, x, g, dres)


def _final_loss(x3_halves, tgt, g, name):
    T, D = tgt.shape

    def body(xa_ref, xb_ref, t_ref, g_ref, loss_ref, dg_ref, dx_ref, dxb_ref):
        i = pl.program_id(0)
        xf = jnp.concatenate([xa_ref[...], xb_ref[...]], axis=1)
        r = lax.rsqrt(jnp.mean(xf * xf, axis=-1, keepdims=True) + EPS)
        xh = xf * r
        gg = g_ref[...]
        err = xh * gg - t_ref[...]

        @pl.when(i == 0)
        def _():
            dg_ref[...] = jnp.zeros_like(dg_ref)
            loss_ref[...] = jnp.zeros_like(loss_ref)

        part = jnp.sum(jnp.sum(err * err, axis=-1, keepdims=True), axis=0, keepdims=True) * (0.5 / D)
        loss_ref[...] += jnp.broadcast_to(part, loss_ref.shape)
        dout = err * (1.0 / D)
        dg_ref[...] += jnp.sum(dout * xh, axis=0, keepdims=True)
        dxh = dout * gg
        dx = r * (dxh - xh * jnp.mean(dxh * xh, axis=-1, keepdims=True))
        dx_ref[...] = dx
        dxb_ref[...] = dx.astype(BF16)

    row = pl.BlockSpec((ROW_BLOCK, D), lambda i: (i, 0))
    vec = pl.BlockSpec((1, D), lambda i: (0, 0))
    return pl.pallas_call(
        body, grid=(T // ROW_BLOCK,),
        in_specs=[pl.BlockSpec((ROW_BLOCK, D // 2), lambda i: (i, 0))] * 2 + [row, vec],
        out_specs=[pl.BlockSpec((1, 128), lambda i: (0, 0)), vec, row, row],
        out_shape=[jax.ShapeDtypeStruct((1, 128), F32), jax.ShapeDtypeStruct((1, D), F32),
                   jax.ShapeDtypeStruct((T, D), F32), jax.ShapeDtypeStruct((T, D), BF16)],
        name=name, compiler_params=_cparams(("arbitrary",)),
    )(*x3_halves, tgt, g)


CONV_BLOCK = 256


def _conv_apply(u, w, b):
    row = lax.broadcasted_iota(jnp.int32, u.shape, 0)
    acc = b + w[CONV_K - 1:CONV_K, :] * u
    shifted = []
    for j in range(1, CONV_K):
        uj = jnp.where(row >= j, pltpu.roll(u, j, axis=0), 0.0)
        shifted.append(uj)
        acc = acc + w[CONV_K - 1 - j:CONV_K - j, :] * uj
    return acc, shifted


def _conv_fwd(proj, conv_w, conv_b, name):
    T = proj.shape[0]
    cb0 = OFF_X // CONV_BLOCK

    def body(u_ref, w_ref, b_ref, o_ref):
        c, _ = _conv_apply(u_ref[...], w_ref[...], b_ref[...])
        o_ref[...] = c * _sigmoid(c)

    return pl.pallas_call(
        body, grid=(CONV_DIM // CONV_BLOCK,),
        in_specs=[pl.BlockSpec((T, CONV_BLOCK), lambda j: (0, cb0 + j)),
                  pl.BlockSpec((CONV_K, CONV_BLOCK), lambda j: (0, j)),
                  pl.BlockSpec((1, CONV_BLOCK), lambda j: (0, j))],
        out_specs=pl.BlockSpec((T, CONV_BLOCK), lambda j: (0, j)),
        out_shape=jax.ShapeDtypeStruct((T, CONV_DIM), F32), name=name, compiler_params=_cparams(("parallel",)),
    )(proj, conv_w, conv_b)


def _conv_bwd(proj, dact, conv_w, conv_b, dproj, name):
    T = proj.shape[0]
    cb0 = OFF_X // CONV_BLOCK

    def body(u_ref, d_ref, w_ref, b_ref, _, du_ref, dw_ref, db_ref):
        u = u_ref[...]
        w = w_ref[...]
        c, shifted = _conv_apply(u, w, b_ref[...])
        sg = _sigmoid(c)
        dc = d_ref[...] * sg * (1.0 + c * (1.0 - sg))
        row = lax.broadcasted_iota(jnp.int32, u.shape, 0)
        du = w[CONV_K - 1:CONV_K, :] * dc
        dw_ref[CONV_K - 1:CONV_K, :] = jnp.sum(dc * u, axis=0, keepdims=True)
        for j in range(1, CONV_K):
            dcj = jnp.where(row < T - j, pltpu.roll(dc, T - j, axis=0), 0.0)
            du = du + w[CONV_K - 1 - j:CONV_K - j, :] * dcj
            dw_ref[CONV_K - 1 - j:CONV_K - j, :] = jnp.sum(dc * shifted[j - 1], axis=0, keepdims=True)
        db_ref[...] = jnp.sum(dc, axis=0, keepdims=True)
        du_ref[...] = du.astype(BF16)

    return pl.pallas_call(
        body, grid=(CONV_DIM // CONV_BLOCK,),
        in_specs=[pl.BlockSpec((T, CONV_BLOCK), lambda j: (0, cb0 + j)),
                  pl.BlockSpec((T, CONV_BLOCK), lambda j: (0, j)),
                  pl.BlockSpec((CONV_K, CONV_BLOCK), lambda j: (0, j)),
                  pl.BlockSpec((1, CONV_BLOCK), lambda j: (0, j)), pl.BlockSpec(memory_space=pl.ANY)],
        out_specs=[pl.BlockSpec((T, CONV_BLOCK), lambda j: (0, cb0 + j)),
                   pl.BlockSpec((CONV_K, CONV_BLOCK), lambda j: (0, j)),
                   pl.BlockSpec((1, CONV_BLOCK), lambda j: (0, j))],
        out_shape=[jax.ShapeDtypeStruct(dproj.shape, BF16), jax.ShapeDtypeStruct((CONV_K, CONV_DIM), F32),
                   jax.ShapeDtypeStruct((1, CONV_DIM), F32)],
        input_output_aliases={4: 0}, name=name, compiler_params=_cparams(("parallel",)),
    )(proj, dact, conv_w, conv_b, dproj)


GROUP_W = D_INNER // N_GROUPS
HEADS_PER_GROUP = N_HEADS // N_GROUPS


def _expand_mat():
    h = lax.broadcasted_iota(jnp.int32, (N_HEADS, D_INNER), 0)
    j = lax.broadcasted_iota(jnp.int32, (N_HEADS, D_INNER), 1)
    return (j // HEAD_DIM == h).astype(F32)


def _reduce_mat(g):
    j = lax.broadcasted_iota(jnp.int32, (GROUP_W, N_HEADS), 0)
    h = lax.broadcasted_iota(jnp.int32, (GROUP_W, N_HEADS), 1)
    return (g * HEADS_PER_GROUP + j // HEAD_DIM == h).astype(F32)


def _col16(v, h):
    lane = lax.broadcasted_iota(jnp.int32, v.shape, 1)
    return jnp.sum(jnp.where(lane == h, v, 0.0), axis=1, keepdims=True)


def _ssd_pre(dt_raw, dtT_raw, dtb, dtbT, alog, alogT):
    Q = CHUNK
    xdt = dt_raw + dtb
    dt = _softplus(xdt)
    dtT = _softplus(dtT_raw + dtbT)
    A = -jnp.exp(alog)
    AT = -jnp.exp(alogT)
    row = lax.broadcasted_iota(jnp.int32, (Q, Q), 0)
    col = lax.broadcasted_iota(jnp.int32, (Q, Q), 1)
    tril = (row >= col).astype(F32)
    triu = (row <= col).astype(F32)
    cs = _hdot(tril, dt * A, "b")
    csT = _hdot(dtT * AT, triu, "a")
    return xdt, dt, A, cs, csT, row >= col, triu


def _decay_matrix(cs, csT, h, causal):
    seg = _col16(cs, h) - csT[h:h + 1, :]
    return jnp.where(causal, jnp.exp(jnp.minimum(seg, 0.0)), 0.0)


def _ssd_in_specs(nc, rev):
    def cidx(c):
        return (nc - 1 - c) if rev else c

    return [
        pl.BlockSpec((CHUNK, D_INNER), lambda c: (cidx(c), 0)),
        pl.BlockSpec((CHUNK, 512), lambda c: (cidx(c), 2)),
        pl.BlockSpec((CHUNK, 512), lambda c: (cidx(c), 3)),
        pl.BlockSpec((CHUNK, D_INNER), lambda c: (cidx(c), 0)),
        pl.BlockSpec((CHUNK, 128), lambda c: (cidx(c), OFF_DT // 128)),
        pl.BlockSpec((N_HEADS, CHUNK), lambda c: (0, cidx(c))),
        pl.BlockSpec((1, N_HEADS), lambda c: (0, 0)),
        pl.BlockSpec((N_HEADS, 1), lambda c: (0, 0)),
        pl.BlockSpec((1, N_HEADS), lambda c: (0, 0)),
        pl.BlockSpec((N_HEADS, 1), lambda c: (0, 0)),
        pl.BlockSpec((1, D_INNER), lambda c: (0, 0)),
        pl.BlockSpec((1, D_INNER), lambda c: (0, 0)),
    ]


def _ssd_fwd(xbc, proj, dtT, dtb, dtbT, alog, alogT, dfull, ng, name):
    T = xbc.shape[0]
    nc = T // CHUNK
    Q = CHUNK

    def body(xs_ref, B_ref, C_ref, z_ref, dt_ref, dtT_ref, dtb_ref, dtbT_ref, al_ref, alT_ref, df_ref, ng_ref,
             y_ref, ypre_ref, hs_ref, h_scr):
        c = pl.program_id(0)

        @pl.when(c == 0)
        def _():
            h_scr[...] = jnp.zeros_like(h_scr)

        _, dt, _, cs, csT, causal, _ = _ssd_pre(dt_ref[:, :N_HEADS], dtT_ref[...], dtb_ref[...], dtbT_ref[...],
                                                al_ref[...], alT_ref[...])
        ex = _expand_mat()
        dt_full = _hdot(dt, ex, "a")
        cs_full = _hdot(cs, ex, "a")
        cs_last = cs_full[Q - 1:Q, :]
        xs = xs_ref[...]
        xd = xs * dt_full
        e_full = jnp.exp(cs_full)
        dec_full = jnp.exp(cs_last - cs_full)
        cd_full = jnp.exp(cs_last)
        lane_head = lax.broadcasted_iota(jnp.int32, (1, GROUP_W), 1) // HEAD_DIM
        for g in range(N_GROUPS):
            sl = slice(g * GROUP_W, (g + 1) * GROUP_W)
            Bg = B_ref[:, g * D_STATE:(g + 1) * D_STATE].astype(BF16)
            Cg = C_ref[:, g * D_STATE:(g + 1) * D_STATE].astype(BF16)
            CB = _dot_nt(Cg, Bg)
            hg = h_scr[g]
            yoff = _dot_nn(Cg, hg.astype(BF16)) * e_full[:, sl]
            xd_g = xd[:, sl]
            S = _dot_tn(Bg, (xd_g * dec_full[:, sl]).astype(BF16))
            xd_b = xd_g.astype(BF16)
            ydiag = jnp.zeros((Q, GROUP_W), F32)
            for r in range(HEADS_PER_GROUP):
                Lm = _decay_matrix(cs, csT, g * HEADS_PER_GROUP + r, causal)
                Gm = (CB * Lm).astype(BF16)
                ydiag = ydiag + _dot_nn(Gm, jnp.where(lane_head == r, xd_b, jnp.zeros_like(xd_b)))
            hs_ref[0, g] = hg
            h_scr[g] = hg * cd_full[:, sl] + S
            ypre = ydiag + yoff + xs[:, sl] * df_ref[:, sl]
            ypre_ref[:, sl] = ypre
            zg = z_ref[:, sl]
            yz = ypre * zg * _sigmoid(zg)
            rn = lax.rsqrt(jnp.mean(yz * yz, axis=-1, keepdims=True) + EPS)
            y_ref[:, sl] = (yz * rn * ng_ref[:, sl]).astype(BF16)

    return pl.pallas_call(
        body, grid=(nc,), in_specs=_ssd_in_specs(nc, False),
        out_specs=[pl.BlockSpec((CHUNK, D_INNER), lambda c: (c, 0)),
                   pl.BlockSpec((CHUNK, D_INNER), lambda c: (c, 0)),
                   pl.BlockSpec((1, N_GROUPS, D_STATE, GROUP_W), lambda c: (c, 0, 0, 0))],
        out_shape=[jax.ShapeDtypeStruct((T, D_INNER + ATTN_W), BF16), jax.ShapeDtypeStruct((T, D_INNER), F32),
                   jax.ShapeDtypeStruct((nc, N_GROUPS, D_STATE, GROUP_W), F32)],
        scratch_shapes=[pltpu.VMEM((N_GROUPS, D_STATE, GROUP_W), F32)],
        name=name, compiler_params=_cparams(("arbitrary",)),
    )(xbc, xbc, xbc, proj, proj, dtT, dtb, dtbT, alog, alogT, dfull, ng)


def _ssd_bwd(xbc, proj, dtT, dtb, dtbT, alog, alogT, dfull, ng, ypre, hs, dy, name):
    T = xbc.shape[0]
    nc = T // CHUNK
    Q = CHUNK

    def body(xs_ref, B_ref, C_ref, z_ref, dt_ref, dtT_ref, dtb_ref, dtbT_ref, al_ref, alT_ref, df_ref, ng_ref,
             ypre_ref, hs_ref, dy_ref,
             dz_ref, dxbc_ref, ddtb_ref, dal_ref, dD_ref, dng_ref, dh_scr):
        step = pl.program_id(0)

        @pl.when(step == 0)
        def _():
            dh_scr[...] = jnp.zeros_like(dh_scr)
            ddtb_ref[...] = jnp.zeros_like(ddtb_ref)
            dal_ref[...] = jnp.zeros_like(dal_ref)
            dD_ref[...] = jnp.zeros_like(dD_ref)
            dng_ref[...] = jnp.zeros_like(dng_ref)

        xdt, dt, A, cs, csT, causal, triu = _ssd_pre(dt_ref[:, :N_HEADS], dtT_ref[...], dtb_ref[...],
                                                    dtbT_ref[...], al_ref[...], alT_ref[...])
        ex = _expand_mat()
        dt_full = _hdot(dt, ex, "a")
        cs_full = _hdot(cs, ex, "a")
        cs_last = cs_full[Q - 1:Q, :]
        xs = xs_ref[...]
        xd = xs * dt_full
        e_full = jnp.exp(cs_full)
        dec_full = jnp.exp(cs_last - cs_full)
        cd_full = jnp.exp(cs_last)
        lane_head = lax.broadcasted_iota(jnp.int32, (1, GROUP_W), 1) // HEAD_DIM
        is_last = lax.broadcasted_iota(jnp.int32, (Q, 1), 0) == Q - 1
        dcs16 = jnp.zeros((Q, N_HEADS), F32)
        ddtx16 = jnp.zeros((Q, N_HEADS), F32)
        dD16 = jnp.zeros((8, N_HEADS), F32)
        lane16 = lax.broadcasted_iota(jnp.int32, (1, N_HEADS), 1)
        sub16 = lax.broadcasted_iota(jnp.int32, (N_HEADS, 1), 0)
        col_sums = jnp.zeros((N_HEADS, Q), F32)
        for g in range(N_GROUPS):
            sl = slice(g * GROUP_W, (g + 1) * GROUP_W)
            red = _reduce_mat(g)
            ypre_g = ypre_ref[:, sl]
            zg = z_ref[:, sl]
            sg = _sigmoid(zg)
            silu = zg * sg
            yz = ypre_g * silu
            rn = lax.rsqrt(jnp.mean(yz * yz, axis=-1, keepdims=True) + EPS)
            yh = yz * rn
            dy_g = dy_ref[:, sl]
            dng_ref[:, sl] += jnp.sum(dy_g * yh, axis=0, keepdims=True)
            dyh = dy_g * ng_ref[:, sl]
            dyz = rn * (dyh - yh * jnp.mean(dyh * yh, axis=-1, keepdims=True))
            dY = dyz * silu
            dz_ref[:, sl] = (dyz * ypre_g * sg * (1.0 + zg * (1.0 - sg))).astype(BF16)
            xs_g = xs[:, sl]
            xd_g = xd[:, sl]
            dec_g = dec_full[:, sl]
            cd_g = cd_full[:, sl]
            d_g = df_ref[:, sl]
            Bg = B_ref[:, g * D_STATE:(g + 1) * D_STATE].astype(BF16)
            Cg = C_ref[:, g * D_STATE:(g + 1) * D_STATE].astype(BF16)
            CB = _dot_nt(Cg, Bg)
            hg = hs_ref[0, g]
            hgb = hg.astype(BF16)
            yoff = _dot_nn(Cg, hgb) * e_full[:, sl]
            dhn = dh_scr[g]
            dhnb = dhn.astype(BF16)
            dYE = (dY * e_full[:, sl]).astype(BF16)
            dC = _dot_nt(dYE, hgb)
            dh_direct = _dot_tn(Cg, dYE)
            dXdd = _dot_nn(Bg, dhnb)
            dB = _dot_nt((xd_g * dec_g).astype(BF16), dhnb)
            dcd = jnp.sum(dhn * hg, axis=0, keepdims=True)
            dh_scr[g] = dh_direct + cd_g * dhn
            dYb = dY.astype(BF16)
            xd_b = xd_g.astype(BF16)
            dCB = jnp.zeros((Q, Q), F32)
            dXd = dXdd * dec_g
            for r in range(HEADS_PER_GROUP):
                h = g * HEADS_PER_GROUP + r
                Lm = _decay_matrix(cs, csT, h, causal)
                Gf = CB * Lm
                dYr = jnp.where(lane_head == r, dYb, jnp.zeros_like(dYb))
                dG = _dot_nt(dYr, xd_b)
                dCB = dCB + dG * Lm
                dXd = dXd + _dot_tn(Gf.astype(BF16), dYr)
                Mm = dG * Gf
                dcs16 = dcs16 + jnp.where(lane16 == h, jnp.sum(Mm, axis=1, keepdims=True), 0.0)
                col_sums = col_sums + jnp.where(sub16 == h, jnp.sum(Mm, axis=0, keepdims=True), 0.0)
            dCBb = dCB.astype(BF16)
            dC = dC + _dot_nn(dCBb, Bg)
            dB = dB + _dot_tn(dCBb, Cg)
            w_state = dXdd * dec_g * xd_g
            t_last = jnp.sum(w_state, axis=0, keepdims=True) + dcd * cd_g
            dcs_g = dY * yoff - w_state + jnp.where(is_last, t_last, 0.0)
            dcs16 = dcs16 + _hdot(dcs_g, red, "a")
            ddtx16 = ddtx16 + _hdot(dXd * xs_g, red, "a")
            dD16 = dD16 + _hdot(jnp.broadcast_to(jnp.sum(dY * xs_g, axis=0, keepdims=True), (8, GROUP_W)), red, "a")
            dxbc_ref[:, sl] = dXd * dt_full[:, sl] + dY * d_g
            dxbc_ref[:, D_INNER + g * D_STATE:D_INNER + (g + 1) * D_STATE] = dB
            dxbc_ref[:, D_INNER + 512 + g * D_STATE:D_INNER + 512 + (g + 1) * D_STATE] = dC
        eye = (lax.broadcasted_iota(jnp.int32, (N_HEADS, N_HEADS), 0)
               == lax.broadcasted_iota(jnp.int32, (N_HEADS, N_HEADS), 1)).astype(BF16)
        dcs16 = dcs16 - sum(_dot_tn(part, eye) for part in _split3(col_sums))
        da = _hdot(triu, dcs16, "b")
        ddt = da * A + ddtx16
        ddt_raw = ddt * _sigmoid(xdt)
        pr = lax.broadcasted_iota(jnp.int32, (N_HEADS, 128), 0)
        pc = lax.broadcasted_iota(jnp.int32, (N_HEADS, 128), 1)
        dz_ref[:, D_INNER:OFF_DT] = jnp.zeros((Q, OFF_DT - D_INNER), BF16)
        dz_ref[:, OFF_DT:OFF_DT + 128] = _hdot(ddt_raw, (pr == pc).astype(F32), "a").astype(BF16)
        dz_ref[:, OFF_DT + 128:] = jnp.zeros((Q, NP - OFF_DT - 128), BF16)
        ddtb_ref[...] += jnp.sum(ddt_raw, axis=0, keepdims=True)
        dal_ref[...] += jnp.sum(da * dt, axis=0, keepdims=True) * A
        dD_ref[...] += dD16[0:1, :]

    def rc(c):
        return nc - 1 - c

    in_specs = _ssd_in_specs(nc, True) + [
        pl.BlockSpec((CHUNK, D_INNER), lambda c: (rc(c), 0)),
        pl.BlockSpec((1, N_GROUPS, D_STATE, GROUP_W), lambda c: (rc(c), 0, 0, 0)),
        pl.BlockSpec((CHUNK, D_INNER), lambda c: (rc(c), 0)),
    ]
    small = pl.BlockSpec((1, N_HEADS), lambda c: (0, 0))
    return pl.pallas_call(
        body, grid=(nc,), in_specs=in_specs,
        out_specs=[pl.BlockSpec((CHUNK, NP), lambda c: (rc(c), 0)),
                   pl.BlockSpec((CHUNK, CONV_DIM), lambda c: (rc(c), 0)),
                   small, small, small,
                   pl.BlockSpec((1, D_INNER), lambda c: (0, 0))],
        out_shape=[jax.ShapeDtypeStruct((T, NP), BF16), jax.ShapeDtypeStruct((T, CONV_DIM), F32),
                   jax.ShapeDtypeStruct((1, N_HEADS), F32), jax.ShapeDtypeStruct((1, N_HEADS), F32),
                   jax.ShapeDtypeStruct((1, N_HEADS), F32), jax.ShapeDtypeStruct((1, D_INNER), F32)],
        scratch_shapes=[pltpu.VMEM((N_GROUPS, D_STATE, GROUP_W), F32)],
        name=name, compiler_params=_cparams(("arbitrary",)),
    )(xbc, xbc, xbc, proj, proj, dtT, dtb, dtbT, alog, alogT, dfull, ng, ypre, hs, dy)


N_PAIRS = ATTN_W // 128
PAIRS_PER_KV = N_PAIRS // 2
ATTN_SCALE = HEAD_DIM ** -0.5


def _kv_variants(kk):
    lo = lax.broadcasted_iota(jnp.int32, kk.shape, 1) < HEAD_DIM
    zero = jnp.zeros_like(kk)
    k00 = jnp.where(lo, kk, zero)
    k11 = jnp.where(lo, zero, kk)
    k01 = pltpu.roll(k00, HEAD_DIM, axis=1)
    k10 = pltpu.roll(k11, HEAD_DIM, axis=1)
    return [[k00.astype(BF16), k01.astype(BF16)], [k10.astype(BF16), k11.astype(BF16)]]


LOG2E = 1.4426950408889634


def _own_block():
    i = lax.broadcasted_iota(jnp.int32, (WINDOW, WINDOW), 0)
    j = lax.broadcasted_iota(jnp.int32, (WINDOW, WINDOW), 1)
    return j <= i


def _fold(own, a):
    return jnp.where(own, a[:, WINDOW:], a[:, :WINDOW])


def _attn_probs(qp, kvar, own, prev_bias, sk):
    s = _dot_nt(qp, kvar)
    sb = jnp.where(own, s[:, WINDOW:], s[:, :WINDOW] + prev_bias) * (ATTN_SCALE * LOG2E)
    sk2 = sk * LOG2E
    m = jnp.maximum(jnp.max(sb, axis=1, keepdims=True), sk2)
    pe = jnp.exp2(sb - m)
    es = jnp.exp2(sk2 - m)
    den = jnp.sum(pe, axis=1, keepdims=True) + es
    inv = 1.0 / den
    return pe * inv, es * inv


def _unfold(own, a):
    zero = jnp.zeros_like(a)
    return jnp.where(own, zero, a), jnp.where(own, a, zero)


def _sink(sinks, r):
    lane = lax.broadcasted_iota(jnp.int32, sinks.shape, 1)
    return jnp.sum(jnp.where(lane == r, sinks, 0.0), axis=1, keepdims=True)


def _kv_specs():
    return [pl.BlockSpec((WINDOW, KV_W), lambda n: (jnp.maximum(n - 1, 0), OFF_K // KV_W)),
            pl.BlockSpec((WINDOW, KV_W), lambda n: (n, OFF_K // KV_W)),
            pl.BlockSpec((WINDOW, KV_W), lambda n: (jnp.maximum(n - 1, 0), OFF_V // KV_W)),
            pl.BlockSpec((WINDOW, KV_W), lambda n: (n, OFF_V // KV_W))]


def _attn_fwd(proj, sinks, og, ycat, name):
    T = proj.shape[0]
    nb = T // WINDOW

    def body(q_ref, kp_ref, kc_ref, vp_ref, vc_ref, s_ref, og_ref, _, y_ref, o_ref):
        n = pl.program_id(0)
        kv = _kv_variants(jnp.concatenate([kp_ref[...], kc_ref[...]], axis=0))
        vv = _kv_variants(jnp.concatenate([vp_ref[...], vc_ref[...]], axis=0))
        own = _own_block()
        prev_bias = jnp.where(n > 0, 0.0, NEG)
        sinks_v = s_ref[...]
        ssq = jnp.zeros((WINDOW, 1), F32)
        for p in range(N_PAIRS):
            j = p // PAIRS_PER_KV
            qp = q_ref[:, p * 128:(p + 1) * 128].astype(BF16)
            o_pair = jnp.zeros((WINDOW, 128), F32)
            for par in range(2):
                pn, _ = _attn_probs(qp, kv[j][par], own, prev_bias, _sink(sinks_v, 2 * p + par))
                p_prev, p_own = _unfold(own, pn.astype(BF16))
                o_pair = o_pair + _dot_nn(p_prev, vv[j][par][:WINDOW]) + _dot_nn(p_own, vv[j][par][WINDOW:])
            o_ref[:, p * 128:(p + 1) * 128] = o_pair
            ssq = ssq + jnp.sum(o_pair * o_pair, axis=1, keepdims=True)
        rn = lax.rsqrt(ssq * (1.0 / ATTN_W) + EPS)
        y_ref[...] = (o_ref[...] * rn * og_ref[...]).astype(BF16)

    return pl.pallas_call(
        body, grid=(nb,),
        in_specs=[pl.BlockSpec((WINDOW, ATTN_W), lambda n: (n, OFF_Q // ATTN_W)), *_kv_specs(),
                  pl.BlockSpec((1, N_HEADS), lambda n: (0, 0)), pl.BlockSpec((1, ATTN_W), lambda n: (0, 0)), ANY],
        out_specs=[pl.BlockSpec((WINDOW, ATTN_W), lambda n: (n, 1)), pl.BlockSpec((WINDOW, ATTN_W), lambda n: (n, 0))],
        out_shape=[jax.ShapeDtypeStruct(ycat.shape, BF16), jax.ShapeDtypeStruct((T, ATTN_W), F32)],
        input_output_aliases={7: 0}, name=name, compiler_params=_cparams(("parallel",)),
    )(proj, proj, proj, proj, proj, sinks, og, ycat)


def _attn_bwd(proj, sinks, og, o, dy, dproj, name):
    T = proj.shape[0]
    nb = T // WINDOW

    def body(q_ref, kp_ref, kc_ref, vp_ref, vc_ref, s_ref, og_ref, o_ref, dy_ref, _,
             dq_ref, dk_ref, dv_ref, ds_ref, dog_ref, qt_scr, dot_scr, ds_scr, p_scr):
        n = pl.program_id(0)

        @pl.when(n == 0)
        def _():
            dk_ref[...] = jnp.zeros_like(dk_ref)
            dv_ref[...] = jnp.zeros_like(dv_ref)
            ds_ref[...] = jnp.zeros_like(ds_ref)
            dog_ref[...] = jnp.zeros_like(dog_ref)

        kv = _kv_variants(jnp.concatenate([kp_ref[...], kc_ref[...]], axis=0))
        vv = _kv_variants(jnp.concatenate([vp_ref[...], vc_ref[...]], axis=0))
        own = _own_block()
        prev_bias = jnp.where(n > 0, 0.0, NEG)
        sinks_v = s_ref[...]
        of = o_ref[...]
        rn = lax.rsqrt(jnp.mean(of * of, axis=-1, keepdims=True) + EPS)
        oh = of * rn
        dyf = dy_ref[...]
        dog_ref[...] += jnp.sum(dyf * oh, axis=0, keepdims=True)
        doh = dyf * og_ref[...]
        do = rn * (doh - oh * jnp.mean(doh * oh, axis=-1, keepdims=True))
        lane = lax.broadcasted_iota(jnp.int32, (1, 128), 1)
        lane16 = lax.broadcasted_iota(jnp.int32, (1, N_HEADS), 1)
        dsink = jnp.zeros((1, N_HEADS), F32)
        for p in range(N_PAIRS):
            j = p // PAIRS_PER_KV
            q_f = q_ref[:, p * 128:(p + 1) * 128]
            qp = q_f.astype(BF16)
            q_t = q_f.T.astype(BF16)
            do_p = do[:, p * 128:(p + 1) * 128]
            o_p = of[:, p * 128:(p + 1) * 128]
            do_b = do_p.astype(BF16)
            do_t = do_p.T.astype(BF16)
            prod = do_p * o_p
            dq_pair = jnp.zeros((WINDOW, 128), F32)
            for par in range(2):
                r = 2 * p + par
                half = (lane < HEAD_DIM) if par == 0 else (lane >= HEAD_DIM)
                pn, ps = _attn_probs(qp, kv[j][par], own, prev_bias, _sink(sinks_v, r))
                delta = jnp.sum(jnp.where(half, prod, 0.0), axis=1, keepdims=True)
                dP = _fold(own, _dot_nt(do_b, vv[j][par]))
                dS = pn * (dP - delta)
                dsink = dsink + jnp.where(lane16 == r, -jnp.sum(ps * delta, axis=0, keepdims=True), 0.0)
                dS_parts = _unfold(own, (dS * ATTN_SCALE).astype(BF16))
                p_parts = _unfold(own, pn.astype(BF16))
                at = ((p % PAIRS_PER_KV) * 2 + par) * WINDOW
                qt_scr[j, :, at:at + WINDOW] = q_t[par * HEAD_DIM:(par + 1) * HEAD_DIM]
                dot_scr[j, :, at:at + WINDOW] = do_t[par * HEAD_DIM:(par + 1) * HEAD_DIM]
                for blk in range(2):
                    dq_pair = dq_pair + _dot_nn(dS_parts[blk], kv[j][par][blk * WINDOW:(blk + 1) * WINDOW])
                    ds_scr[j, blk, at:at + WINDOW, :] = dS_parts[blk]
                    p_scr[j, blk, at:at + WINDOW, :] = p_parts[blk]
            dq_ref[:, p * 128:(p + 1) * 128] = dq_pair.astype(BF16)
        rows = [pl.multiple_of(jnp.maximum(n - 1, 0) * WINDOW, WINDOW), pl.multiple_of(n * WINDOW, WINDOW)]
        for lhs, rhs, ref in [(qt_scr, ds_scr, dk_ref), (dot_scr, p_scr, dv_ref)]:
            for blk in range(2):
                both_t = jnp.concatenate([_dot_nn(lhs[j], rhs[j, blk]) for j in range(2)], axis=0)
                ref[pl.ds(rows[blk], WINDOW), :] += both_t.T
        ds_ref[...] += dsink

    full_kv = pl.BlockSpec((T, KV_W), lambda n: (0, 0))
    blk = pl.BlockSpec((WINDOW, ATTN_W), lambda n: (n, 0))
    return pl.pallas_call(
        body, grid=(nb,),
        in_specs=[pl.BlockSpec((WINDOW, ATTN_W), lambda n: (n, OFF_Q // ATTN_W)), *_kv_specs(),
                  pl.BlockSpec((1, N_HEADS), lambda n: (0, 0)), pl.BlockSpec((1, ATTN_W), lambda n: (0, 0)),
                  blk, pl.BlockSpec((WINDOW, ATTN_W), lambda n: (n, 1)), ANY],
        out_specs=[pl.BlockSpec((WINDOW, ATTN_W), lambda n: (n, OFF_Q // ATTN_W)), full_kv, full_kv,
                   pl.BlockSpec((1, N_HEADS), lambda n: (0, 0)), pl.BlockSpec((1, ATTN_W), lambda n: (0, 0))],
        out_shape=[jax.ShapeDtypeStruct(dproj.shape, BF16), jax.ShapeDtypeStruct((T, KV_W), F32),
                   jax.ShapeDtypeStruct((T, KV_W), F32), jax.ShapeDtypeStruct((1, N_HEADS), F32),
                   jax.ShapeDtypeStruct((1, ATTN_W), F32)],
        scratch_shapes=[pltpu.VMEM((2, HEAD_DIM, 8 * WINDOW), BF16), pltpu.VMEM((2, HEAD_DIM, 8 * WINDOW), BF16),
                        pltpu.VMEM((2, 2, 8 * WINDOW, WINDOW), BF16), pltpu.VMEM((2, 2, 8 * WINDOW, WINDOW), BF16)],
        input_output_aliases={9: 0}, name=name, compiler_params=_cparams(("arbitrary",)),
    )(proj, proj, proj, proj, proj, sinks, og, o, dy, dproj)


ANY = pl.BlockSpec(memory_space=pl.ANY)


def _coords():
    return lax.axis_index("x"), lax.axis_index("y"), lax.axis_index("c")


HBM = pl.BlockSpec(memory_space=pltpu.HBM)
SEM = pl.BlockSpec(memory_space=pltpu.SEMAPHORE)
EFFECT = pltpu.SideEffectType.DATAFLOW_SIDE_EFFECTING


def _in_hbm(a):
    return pltpu.with_memory_space_constraint(a, pltpu.HBM)


def _remote_start(srcs, lands, plan, n_copies, name, after=None):
    ns, nb = len(srcs), len(srcs) + len(lands)
    n_after = 0 if after is None else 1

    def body(*refs):
        src_refs, land_refs = refs[:ns], refs[ns:nb]
        send_sems, recv_sems = refs[nb + n_after], refs[nb + n_after + 1]
        token = refs[-1]
        x, y, c = _coords()
        for i, (sv, dv, dev) in enumerate(plan(src_refs, land_refs, x, y, c)):
            pltpu.make_async_remote_copy(src_ref=sv, dst_ref=dv, send_sem=send_sems.at[i], recv_sem=recv_sems.at[i],
                                         device_id=dev, device_id_type=MESH).start()
        token[...] = jnp.zeros_like(token)

    bufs = list(srcs) + list(lands)
    outs = pl.pallas_call(
        body, name=name,
        out_shape=(pltpu.SemaphoreType.DMA((n_copies,)), pltpu.SemaphoreType.DMA((n_copies,)),
                   *[pltpu.HBM(b.shape, b.dtype) for b in bufs], jax.ShapeDtypeStruct((8, 128), F32)),
        in_specs=[HBM] * nb + [ANY] * n_after,
        out_specs=(SEM, SEM, *[HBM] * nb, pl.BlockSpec(memory_space=pltpu.VMEM)),
        input_output_aliases={i: 2 + i for i in range(nb)},
        compiler_params=pltpu.CompilerParams(has_side_effects=EFFECT),
    )(*[_in_hbm(b) for b in bufs], *([] if after is None else [after]))
    return outs[0], outs[1], list(outs[2:2 + ns]), list(outs[2 + ns:2 + nb]), outs[-1]


def _remote_wait(started, after, plan, name):
    send_sems, recv_sems, srcs, lands, _ = started
    ns, nb = len(srcs), len(srcs) + len(lands)

    def body(*refs):
        src_refs, land_refs = refs[:ns], refs[ns:nb]
        send_sems, recv_sems = refs[nb], refs[nb + 1]
        x, y, c = _coords()
        for i, (sv, dv, dev) in enumerate(plan(src_refs, land_refs, x, y, c)):
            cp = pltpu.make_async_remote_copy(src_ref=sv, dst_ref=dv, send_sem=send_sems.at[i],
                                              recv_sem=recv_sems.at[i], device_id=dev, device_id_type=MESH)
            cp.wait_send()
            cp.wait_recv()

    bufs = list(srcs) + list(lands)
    outs = pl.pallas_call(
        body, name=name, out_shape=tuple(pltpu.HBM(b.shape, b.dtype) for b in bufs),
        in_specs=[HBM] * nb + [SEM, SEM, ANY], out_specs=tuple([HBM] * nb),
        input_output_aliases={i: i for i in range(nb)},
        compiler_params=pltpu.CompilerParams(has_side_effects=EFFECT),
    )(*bufs, send_sems, recv_sems, after)
    return list(outs[:ns]), list(outs[ns:])


def _pair_plan(src_refs, land_refs, x, y, c):
    plan = []
    for s, l in zip(src_refs, land_refs):
        for q in range(4):
            plan.append((s.at[2 * q + (1 - c)], l.at[q], (x, y, 1 - c)))
    return plan


def _pair4_plan(src_refs, land_refs, x, y, c):
    plan = []
    for s, l in zip(src_refs, land_refs):
        for q in range(4):
            plan.append((s.at[q], l.at[q], (x, y, 1 - c)))
    return plan


def _chips_plan(src_refs, land_refs, x, y, c):
    plan = []
    for s, l in zip(src_refs, land_refs):
        for k, (tx, ty) in enumerate([(1 - x, y), (x, 1 - y), (1 - x, 1 - y)]):
            plan.append((s.at[2 * tx + ty], l.at[k], (tx, ty, c)))
    return plan


def _everyone_plan(src_refs, land_refs, x, y, c):
    me = 4 * x + 2 * y + c
    plan = []
    for s, l in zip(src_refs, land_refs):
        for fx, fy, fc in [(0, 0, 1), (1, 0, 0), (1, 0, 1), (0, 1, 0), (0, 1, 1), (1, 1, 0), (1, 1, 1)]:
            dev = ((1 - x) if fx else x, (1 - y) if fy else y, (1 - c) if fc else c)
            plan.append((s, l.at[me], dev))
    return plan


def _pair_add(g8, r1, csel, tr, name):
    _, R, C = r1.shape
    g4 = g8.reshape(4, 2, R, C)

    def body(c_ref, g_ref, r_ref, o_ref):
        o_ref[...] = (g_ref[...].astype(F32) + r_ref[...].astype(F32)).astype(BF16)

    return pl.pallas_call(
        body,
        grid_spec=pltpu.PrefetchScalarGridSpec(
            num_scalar_prefetch=1, grid=(4, R // tr),
            in_specs=[pl.BlockSpec((None, None, tr, C), lambda q, i, cs: (q, cs[0], i, 0)),
                      pl.BlockSpec((None, tr, C), lambda q, i, cs: (q, i, 0))],
            out_specs=pl.BlockSpec((None, tr, C), lambda q, i, cs: (q, i, 0))),
        out_shape=jax.ShapeDtypeStruct((4, R, C), BF16), name=name,
        compiler_params=_cparams(("parallel", "parallel")),
    )(csel, g4, r1)


def _adamw_math(w, g, m, v):
    m = ADAM_B1 * m + (1.0 - ADAM_B1) * g
    v = ADAM_B2 * v + (1.0 - ADAM_B2) * (g * g)
    m_hat = m / (1.0 - ADAM_B1 ** ADAM_STEP)
    v_hat = v / (1.0 - ADAM_B2 ** ADAM_STEP)
    delta = -ADAM_LR * (m_hat / (jnp.sqrt(v_hat) + ADAM_EPS) + ADAM_WD * w)
    return delta, m, v


def _adamw_big(w, m, v, p4, r3, qsel, tile, name):
    R, C = w.shape
    tr, tc = tile

    def body(q_ref, w_ref, m_ref, v_ref, p_ref, r_ref, g_out, d_out, m_out, v_out):
        g = p_ref[...].astype(F32) + r_ref[0].astype(F32) + r_ref[1].astype(F32) + r_ref[2].astype(F32)
        d, mn, vn = _adamw_math(w_ref[...], g, m_ref[...], v_ref[...])
        g_out[...] = g
        d_out[...] = d
        m_out[...] = mn
        v_out[...] = vn

    blk = pl.BlockSpec((tr, tc), lambda i, j, qs: (i, j))
    return pl.pallas_call(
        body,
        grid_spec=pltpu.PrefetchScalarGridSpec(
            num_scalar_prefetch=1, grid=(R // tr, C // tc),
            in_specs=[blk, blk, blk, pl.BlockSpec((None, tr, tc), lambda i, j, qs: (qs[0], i, j)),
                      pl.BlockSpec((3, tr, tc), lambda i, j, qs: (0, i, j))],
            out_specs=[blk, blk, blk, blk]),
        out_shape=[jax.ShapeDtypeStruct((R, C), F32)] * 4, name=name,
        compiler_params=_cparams(("parallel", "parallel")),
    )(qsel, w, m, v, p4, r3)


def _sum_partials(p4, r3, qsel, tc, name):
    _, R, C = p4.shape

    def body(q_ref, p_ref, r_ref, o_ref):
        o_ref[...] = p_ref[...].astype(F32) + r_ref[0].astype(F32) + r_ref[1].astype(F32) + r_ref[2].astype(F32)

    return pl.pallas_call(
        body,
        grid_spec=pltpu.PrefetchScalarGridSpec(
            num_scalar_prefetch=1, grid=(C // tc,),
            in_specs=[pl.BlockSpec((None, R, tc), lambda j, qs: (qs[0], 0, j)),
                      pl.BlockSpec((3, R, tc), lambda j, qs: (0, 0, j))],
            out_specs=pl.BlockSpec((R, tc), lambda j, qs: (0, j))),
        out_shape=jax.ShapeDtypeStruct((R, C), F32), name=name, compiler_params=_cparams(("parallel",)),
    )(qsel, p4, r3)


def _adamw_tiled(w, g, m, v, tc, name):
    R, C = w.shape

    def body(w_ref, g_ref, m_ref, v_ref, d_out, m_out, v_out):
        d, mn, vn = _adamw_math(w_ref[...], g_ref[...], m_ref[...], v_ref[...])
        d_out[...] = d
        m_out[...] = mn
        v_out[...] = vn

    blk = pl.BlockSpec((R, tc), lambda j: (0, j))
    return pl.pallas_call(
        body, grid=(C // tc,), in_specs=[blk] * 4, out_specs=[blk] * 3,
        out_shape=[jax.ShapeDtypeStruct((R, C), F32)] * 3, name=name, compiler_params=_cparams(("parallel",)),
    )(w, g, m, v)


def _small_sum(parts, name):
    def body(p_ref, o_ref):
        acc = p_ref[0]
        for d in range(1, N_DEV):
            acc = acc + p_ref[d]
        o_ref[...] = acc

    return pl.pallas_call(
        body, out_shape=jax.ShapeDtypeStruct(parts.shape[1:], F32), name=name,
        compiler_params=_cparams(),
    )(parts)


def _adamw_small(w, g, m, v, name):
    def body(w_ref, g_ref, m_ref, v_ref, d_out, m_out, v_out):
        d, mn, vn = _adamw_math(w_ref[...], g_ref[...], m_ref[...], v_ref[...])
        d_out[...] = d
        m_out[...] = mn
        v_out[...] = vn

    return pl.pallas_call(
        body, out_shape=[jax.ShapeDtypeStruct(w.shape, F32)] * 3, name=name, compiler_params=_cparams(),
    )(w, g, m, v)


def _row(*pieces):
    r = jnp.concatenate([p.reshape(1, -1) for p in pieces], axis=1)
    return jnp.pad(r, ((0, 0), (0, D_MODEL - r.shape[1])))


def _pack_small(mix, convb, ssmg, attng, mlpg, fing, convw, dtb, alog, dsk, sinks, extra=None):
    last = [dtb, alog, dsk, sinks] + ([extra] if extra is not None else [])
    rows = [_row(mix), _row(convb), _row(ssmg, attng), _row(mlpg), _row(fing),
            jnp.pad(convw, ((0, 0), (0, D_MODEL - convw.shape[1]))), _row(*last)]
    packed = jnp.concatenate(rows, axis=0)
    return jnp.pad(packed, ((0, SMALL_ROWS - packed.shape[0]), (0, 0)))


def _unpack_small(p, conv_n):
    return dict(
        mix_norm_g=p[0:1, :], conv_b=p[1:2, :], ssm_norm_g=p[2:3, :D_INNER], attn_out_norm_g=p[2:3, D_INNER:],
        mlp_norm_g=p[3:4, :], final_norm_g=p[4, :], conv_w=p[5:9, :conv_n][None],
        dt_bias=p[9:10, 0:16], A_log=p[9:10, 16:32], D_skip=p[9:10, 32:48], attn_sinks=p[9:10, 48:64])


WEIGHT_ORDER = ["mix_norm_g", "w_in", "conv_w", "conv_b", "dt_bias", "A_log", "D_skip", "ssm_norm_g", "attn_sinks",
                "attn_out_norm_g", "w_out", "mlp_norm_g", "w_up", "w_down", "final_norm_g"]


def _to_my_columns(w_nat):
    pad = jnp.zeros((w_nat.shape[0], NP - IN_PROJ), w_nat.dtype)
    return jnp.concatenate([w_nat[:, :NAT_DT], w_nat[:, NAT_DT + N_HEADS:], w_nat[:, NAT_DT:NAT_DT + N_HEADS], pad],
                           axis=1)


PER = IN_PROJ // N_DEV
SUPER_STEP = 544
SUPER = 576


def _natural_rows(g, lo, hi):
    segments = [(0, NAT_DT, 0), (NAT_DT, NAT_DT + N_HEADS, OFF_DT - NAT_DT), (NAT_DT + N_HEADS, IN_PROJ, -N_HEADS),
                (IN_PROJ, NP, 0)]
    pieces = [g[max(lo, a) + shift:min(hi, b) + shift] for a, b, shift in segments if max(lo, a) < min(hi, b)]
    return pieces[0] if len(pieces) == 1 else jnp.concatenate(pieces, axis=0)


def _w_in_from_super_slabs(sup):
    seam = SUPER - SUPER_STEP
    units = []
    for i in range(N_DEV):
        base = SUPER_STEP * i
        units.append((base, base + seam, sup[i, :seam] if i == 0 else sup[i - 1, SUPER_STEP:] + sup[i, :seam]))
        units.append((base + seam, base + SUPER_STEP, sup[i, seam:SUPER_STEP]))
    units.append((SUPER_STEP * N_DEV, SUPER_STEP * N_DEV + seam, sup[N_DEV - 1, SUPER_STEP:]))

    def natural(lo, hi):
        return [rows[max(lo, a) - a:min(hi, b) - a] for a, b, rows in units if max(lo, a) < min(hi, b)]

    pieces = natural(0, NAT_DT) + natural(NAT_DT + N_HEADS, IN_PROJ) + natural(NAT_DT, NAT_DT + N_HEADS)
    return jnp.concatenate(pieces + [jnp.zeros((NP - IN_PROJ, D_MODEL), sup.dtype)], axis=0)


def _to_natural_columns(w_my):
    return jnp.concatenate([w_my[:, :NAT_DT], w_my[:, OFF_DT:OFF_DT + N_HEADS], w_my[:, NAT_DT:OFF_DT]], axis=1)


SLAB = 1024


def _grad_w_up(h2, du, name, sel=None, add=None, after=None):
    T, D = h2.shape
    if sel is None:
        pick, n_slab, pre = (lambda j, *cs: j), N_DEV, None
    else:
        pre, other = sel
        pick, n_slab = (lambda j, cs: 2 * j + ((1 - cs[0]) if other else cs[0])), 4
    o_spec = pl.BlockSpec((None, D, SLAB), lambda i, j, k, *cs: (j, 0, 0))
    return _matmul(
        h2, du, mode="tn", grid=(1, n_slab, 1),
        a_spec=pl.BlockSpec((T, D), lambda i, j, k, *cs: (0, 0)),
        b_spec=pl.BlockSpec((T, SLAB), lambda i, j, k, *cs: (0, pick(j, *cs))),
        out_shapes=[jax.ShapeDtypeStruct((n_slab, D, SLAB), BF16)], out_specs=[o_spec], tile=(D, SLAB), name=name,
        extras=() if add is None else (add,), extra_specs=() if add is None else (o_spec,),
        epilogue=None if add is None else (lambda acc, r: (acc + r.astype(F32),)), after=after, prefetch=pre)[0]


def _grad_w_down(act, dx3b, name, sel=None, add=None, after=None):
    T, D = dx3b.shape
    if sel is None:
        pick, n_slab, pre = (lambda i, *cs: i), N_DEV, None
    else:
        pre, other = sel
        pick, n_slab = (lambda i, cs: 2 * i + ((1 - cs[0]) if other else cs[0])), 4
    o_spec = pl.BlockSpec((None, SLAB, D), lambda i, j, k, *cs: (i, 0, 0))
    return _matmul(
        act, dx3b, mode="tn", grid=(n_slab, 1, 1),
        a_spec=pl.BlockSpec((T, SLAB), lambda i, j, k, *cs: (0, pick(i, *cs))),
        b_spec=pl.BlockSpec((T, D), lambda i, j, k, *cs: (0, 0)),
        out_shapes=[jax.ShapeDtypeStruct((n_slab, SLAB, D), BF16)], out_specs=[o_spec], tile=(SLAB, D), name=name,
        extras=() if add is None else (add,), extra_specs=() if add is None else (o_spec,),
        epilogue=None if add is None else (lambda acc, r: (acc + r.astype(F32),)), after=after, prefetch=pre)[0]


class _FixedWeights:
    def __init__(self, w_in_p, w_out_f, w_up_s, w_down_f, conv_w_f):
        self.w = (w_in_p, w_out_f, w_up_s, w_down_f, conv_w_f)
        self.grads = {}

    def mixer_weights(self, after):
        return self.w[0], None

    def conv_weight(self, after):
        return self.w[4]

    def out_weight(self, after):
        return self.w[1]

    def up_weight(self, after):
        return self.w[2]

    def down_weight(self, h, after):
        return self.w[3][:, h * (D_MODEL // 2):(h + 1) * (D_MODEL // 2)]

    def mlp_grads(self, h2, du, act, dx3b):
        self.grads.update(w_up=_grad_w_up(h2, du, "grad_w_up"),
                          w_down=_grad_w_down(act, dx3b, "grad_w_down").reshape(D_FF, D_MODEL))
        return None

    def grad_sent(self, tag, after):
        return None

    def out_grad(self, g_out):
        self.grads.update(w_out=g_out)
        return None

    def in_grad(self, g_in):
        self.grads.update(w_in=g_in)
        return None


def _local_step(x, tgt, p, hooks):
    T = x.shape[0]
    D = D_MODEL
    h1 = _rmsnorm_fwd(x, p["mix_norm_g"], "norm_mix")
    w_in_t, token = hooks.mixer_weights(h1)
    (proj,) = _mm_simple(h1, w_in_t, mode="nt", M=T, N=NP, K=D, tm=min(T, 1024), tn=1536, tk=D, out_dtype=F32,
                         name="in_proj", after=token)
    conv_w_f = hooks.conv_weight(proj)
    xbc = _conv_fwd(proj, conv_w_f, p["conv_b"], "conv_fwd")
    dtT = proj[:, OFF_DT:OFF_DT + N_HEADS].T
    dtbT = p["dt_bias"].T
    alogT = p["A_log"].T
    dfull = jnp.repeat(p["D_skip"], HEAD_DIM, axis=1)
    ycat, ypre, hs = _ssd_fwd(xbc, proj, dtT, p["dt_bias"], dtbT, p["A_log"], alogT, dfull, p["ssm_norm_g"],
                              "ssd_fwd")
    ycat, o_att = _attn_fwd(proj, p["attn_sinks"], p["attn_out_norm_g"], ycat, "attn_fwd")
    w_out_f = hooks.out_weight(ycat)
    tm = min(T, 1024)
    def residual_and_norm(acc, res, gain):
        x2 = acc + res
        return x2, x2 * lax.rsqrt(jnp.mean(x2 * x2, axis=-1, keepdims=True) + EPS) * gain

    rows = min(T, 512)
    x2, h2 = _matmul(
        ycat, w_out_f, mode="nn", grid=(T // rows, 1, 1),
        a_spec=pl.BlockSpec((rows, D), lambda i, j, k: (i, 0)), b_spec=pl.BlockSpec((D, D), lambda i, j, k: (0, 0)),
        out_shapes=[jax.ShapeDtypeStruct((T, D), F32), jax.ShapeDtypeStruct((T, D), BF16)],
        out_specs=[pl.BlockSpec((rows, D), lambda i, j, k: (i, 0))] * 2, tile=(rows, D), name="out_proj",
        extras=(x, p["mlp_norm_g"]),
        extra_specs=[pl.BlockSpec((rows, D), lambda i, j, k: (i, 0)), pl.BlockSpec((1, D), lambda i, j, k: (0, 0))],
        epilogue=residual_and_norm)
    w_up_s = hooks.up_weight(h2)
    grid = (T // tm, N_DEV, 1)
    u, act = _matmul(
        h2, w_up_s, mode="nn", grid=grid,
        a_spec=pl.BlockSpec((tm, D), lambda i, j, k: (i, 0)),
        b_spec=pl.BlockSpec((None, D, 1024), lambda i, j, k: (j, 0, 0)),
        out_shapes=[jax.ShapeDtypeStruct((T, D_FF), F32), jax.ShapeDtypeStruct((T, D_FF), BF16)],
        out_specs=[pl.BlockSpec((tm, 1024), lambda i, j, k: (i, j))] * 2, tile=(tm, 1024), name="mlp_up",
        epilogue=lambda acc: (acc, jnp.square(jnp.maximum(acc, 0.0))))
    half = D // 2
    w_down_halves, x3_halves = [], []
    for h in range(2):
        w_down_halves.append(hooks.down_weight(h, act if h == 0 else x3_halves[0]))
        x3_halves.append(_matmul(
            act, w_down_halves[h], mode="nn", grid=(T // tm, 1, D_FF // 2048),
            a_spec=pl.BlockSpec((tm, 2048), lambda i, j, k: (i, k)),
            b_spec=pl.BlockSpec((2048, half), lambda i, j, k: (k, 0)),
            out_shapes=[jax.ShapeDtypeStruct((T, half), F32)],
            out_specs=[pl.BlockSpec((tm, half), lambda i, j, k: (i, 0))], tile=(tm, half), name=f"mlp_down_{h}",
            extras=(x2,), extra_specs=[pl.BlockSpec((tm, half), lambda i, j, k, h=h: (i, h))],
            epilogue=lambda acc, res: (acc + res,))[0])
    loss_part, d_fin, dx3, dx3b = _final_loss(x3_halves, tgt, p["final_norm_g"].reshape(1, D), "loss_head")
    (du,) = _matmul(
        dx3b, tuple(w_down_halves), mode="nt", grid=(T // tm, D_FF // 1024, 1),
        a_spec=pl.BlockSpec((tm, D), lambda i, j, k: (i, 0)),
        b_spec=(pl.BlockSpec((1024, half), lambda i, j, k: (j, 0)),) * 2,
        out_shapes=[jax.ShapeDtypeStruct((T, D_FF), BF16)],
        out_specs=[pl.BlockSpec((tm, 1024), lambda i, j, k: (i, j))], tile=(tm, 1024), name="mlp_down_bwd",
        extras=(u,), extra_specs=[pl.BlockSpec((tm, 1024), lambda i, j, k: (i, j))],
        epilogue=lambda acc, uu: (acc * (2.0 * jnp.maximum(uu, 0.0)),),
        dot_fn=lambda a, b0, b1: _dot_nt(a[:, :half], b0) + _dot_nt(a[:, half:], b1))
    token = hooks.mlp_grads(h2, du, act, dx3b)
    (dh2,) = _matmul(
        du, w_up_s, mode="nt", grid=(T // tm, D // 1024, N_DEV // 2),
        a_spec=pl.BlockSpec((tm, 2048), lambda i, j, k: (i, k)),
        b_spec=pl.BlockSpec((2, 1024, 1024), lambda i, j, k: (k, j, 0)),
        out_shapes=[jax.ShapeDtypeStruct((T, D), F32)],
        out_specs=[pl.BlockSpec((tm, 1024), lambda i, j, k: (i, j))], tile=(tm, 1024), name="mlp_up_bwd",
        after=token, dot_fn=lambda a, b: _dot_nt(a[:, :1024], b[0]) + _dot_nt(a[:, 1024:], b[1]))
    dx2, dx2b, d_mlp = _rmsnorm_bwd(dh2, x2, p["mlp_norm_g"], dx3, "norm_mlp_bwd")
    (g_out,) = _mm_simple(ycat, dx2b, mode="tn", M=D, N=D, K=T, tm=1024, tn=1024, tk=T, out_dtype=BF16,
                          name="grad_w_out")
    token = hooks.out_grad(g_out)
    (dy,) = _mm_simple(dx2b, w_out_f, mode="nt", M=T, N=D, K=D, tm=tm, tn=1024, tk=D, out_dtype=F32,
                       name="out_proj_bwd", after=token)
    token = hooks.grad_sent("out", dy)
    ssm_g = p["ssm_norm_g"] if token is None else p["ssm_norm_g"] + token[0:1, 0:1]
    dproj, dxbc_act, d_dtb, d_alog, d_dskip, d_ssmg = _ssd_bwd(
        xbc, proj, dtT, p["dt_bias"], dtbT, p["A_log"], alogT, dfull, ssm_g, ypre, hs, dy, "ssd_bwd")
    dproj, d_convw, d_convb = _conv_bwd(proj, dxbc_act, conv_w_f, p["conv_b"], dproj, "conv_bwd")
    dproj, dk, dv, d_sinks, d_attng = _attn_bwd(proj, p["attn_sinks"], p["attn_out_norm_g"], o_att, dy, dproj,
                                                "attn_bwd")
    dproj = lax.dynamic_update_slice(dproj, jnp.concatenate([dk, dv], axis=1).astype(BF16), (0, OFF_K))
    (g_in,) = _mm_simple(dproj, h1, mode="tn", M=NP, N=D, K=T, tm=1536, tn=1024, tk=T, out_dtype=BF16,
                         name="grad_w_in")
    token = hooks.in_grad(g_in)
    (dh1,) = _mm_simple(dproj, w_in_t, mode="nn", M=T, N=D, K=NP, tm=tm, tn=1024, tk=2304, out_dtype=F32,
                        name="in_proj_bwd", after=token)
    token = hooks.grad_sent("in", dh1)
    mix_g = p["mix_norm_g"] if token is None else p["mix_norm_g"] + token[0:1, 0:1]
    dx, d_mix = _rmsnorm_bwd(dh1, x, mix_g, dx2, "norm_mix_bwd", with_bf16=False)
    small = _pack_small(d_mix, d_convb, d_ssmg, d_attng, d_mlp, d_fin, d_convw, d_dtb, d_alog, d_dskip, d_sinks,
                        extra=loss_part[:, 0:1])
    return dx, small


def _rows_rotated(v, shift, name):
    R, C = v.shape
    tc = 512

    def body(s_ref, v_ref, o_ref):
        o_ref[...] = pltpu.roll(v_ref[...], s_ref[0], axis=0).astype(BF16)

    return pl.pallas_call(
        body,
        grid_spec=pltpu.PrefetchScalarGridSpec(
            num_scalar_prefetch=1, grid=(C // tc,), in_specs=[pl.BlockSpec((R, tc), lambda j, s: (0, j))],
            out_specs=pl.BlockSpec((R, tc), lambda j, s: (0, j))),
        out_shape=jax.ShapeDtypeStruct((R, C), BF16), name=name, compiler_params=_cparams(("parallel",)),
    )(shift, v)


def _landing(own, me):
    zone = lax.empty((N_DEV,) + own.shape, own.dtype)
    return lax.dynamic_update_slice(zone, own[None], (me,) + (0,) * own.ndim)


def _sequencer_gather(owns, split, me, collective_id, name):
    n = len(owns)
    zone_refs = [jax.new_ref(_landing(o, me), memory_space=pltpu.MemorySpace.HBM) for o in owns]
    own_refs = [jax.new_ref(o, memory_space=pltpu.MemorySpace.HBM) for o in owns]
    N_COPIES = 9

    @pl.kernel(mesh=plsc.ScalarSubcoreMesh(axis_name="sequencer", num_cores=1), name=name,
               scratch_types=(pltpu.SemaphoreType.DMA((n, N_COPIES)), pltpu.SemaphoreType.DMA((n, N_COPIES))),
               compiler_params=pltpu.CompilerParams(collective_id=collective_id))
    def launch(send_sems, recv_sems):
        x, y, c = _coords()
        sibling, xn, yn, diag = (x, y, 1 - c), (1 - x, y, c), (x, 1 - y, c), (1 - x, 1 - y, c)
        barrier = pltpu.get_barrier_semaphore()
        for peer in [sibling, xn, yn, diag]:
            pl.semaphore_signal(barrier, inc=1, device_id=peer, device_id_type=MESH)
        pl.semaphore_wait(barrier, 4)

        def block(a, dev, half=None):
            ref = zone_refs[a].at[4 * dev[0] + 2 * dev[1] + dev[2]]
            if half is None:
                return ref
            rows = owns[a].shape[0] // 2
            return ref.at[pl.ds(half * rows, rows)]

        def copy(a, k, src, dst, to):
            return pltpu.make_async_remote_copy(src_ref=src, dst_ref=dst, send_sem=send_sems.at[a, k],
                                                recv_sem=recv_sems.at[a, k], device_id=to, device_id_type=MESH)

        me_dev = (x, y, c)
        sent = []
        first = {}
        for a in range(n):
            for k, peer in enumerate([sibling, xn, yn] + ([] if split[a] else [diag])):
                first[a, k] = copy(a, k, own_refs[a], block(a, me_dev), peer)
                first[a, k].start()
                sent.append(first[a, k])
        from_sibling = []
        for a in range(n):
            first[a, 1].wait_recv()
            sent.append(copy(a, 4, block(a, xn), block(a, xn), sibling))
            if split[a]:
                sent.append(copy(a, 6, block(a, xn, 0), block(a, xn, 0), yn))
            first[a, 2].wait_recv()
            sent.append(copy(a, 5, block(a, yn), block(a, yn), sibling))
            if split[a]:
                sent.append(copy(a, 7, block(a, yn, 1), block(a, yn, 1), xn))
            for cp in sent[-(4 if split[a] else 2):]:
                cp.start()
        for a in range(n):
            if split[a]:
                copy(a, 6, block(a, diag, 0), block(a, diag, 0), yn).wait_recv()
                sent.append(copy(a, 8, block(a, diag, 0), block(a, diag, 0), sibling))
                sent[-1].start()
                copy(a, 7, block(a, diag, 1), block(a, diag, 1), xn).wait_recv()
                sent.append(copy(a, 3, block(a, diag, 1), block(a, diag, 1), sibling))
                sent[-1].start()
            else:
                first[a, 3].wait_recv()
                sent.append(copy(a, 8, block(a, diag), block(a, diag), sibling))
                sent[-1].start()
        for a in range(n):
            first[a, 0].wait_recv()
            copy(a, 4, block(a, xn), block(a, xn), sibling).wait_recv()
            copy(a, 5, block(a, yn), block(a, yn), sibling).wait_recv()
            if split[a]:
                copy(a, 8, block(a, diag, 0), block(a, diag, 0), sibling).wait_recv()
                copy(a, 3, block(a, diag, 1), block(a, diag, 1), sibling).wait_recv()
            else:
                copy(a, 8, block(a, diag), block(a, diag), sibling).wait_recv()
        for cp in sent:
            cp.wait_send()

    launch()
    return zone_refs


class _ShardedWeights:
    def __init__(self, w_in, w_out, conv_w, w_up, w_down, me, csel):
        self.me, self.csel = me, csel
        padded = jnp.pad(jnp.transpose(w_in), ((0, SUPER - PER), (0, 0)))
        own_rows = _rows_rotated(padded, jnp.reshape(2 * me, (1,)).astype(jnp.int32), "w_in_super_slab")
        (self.in_ref,) = _sequencer_gather([own_rows], [True], me, 7, "gather_w_in_sequencer")
        self.out_ref, self.conv_ref = _sequencer_gather([w_out.astype(BF16), conv_w], [True, False], me, 8,
                                                        "gather_w_out_sequencer")
        (self.up_ref,) = _sequencer_gather([w_up.astype(BF16)], [True], me, 9, "gather_w_up_sequencer")
        down = w_down.astype(BF16)
        self.down_refs = [_sequencer_gather([down[:, h * (D_MODEL // 2):(h + 1) * (D_MODEL // 2)]], [True], me, 10 + h,
                                            f"gather_w_down_{h}_sequencer")[0] for h in range(2)]
        self.reduces = {}
        self.pairs = {}

    def mixer_weights(self, after):
        return _w_in_from_super_slabs(self.in_ref[...]), None

    def conv_weight(self, after):
        g_conv = self.conv_ref[...]
        return jnp.concatenate([g_conv[i] for i in range(N_DEV)], axis=1)

    def out_weight(self, after):
        return self.out_ref[...].reshape(D_MODEL, D_MODEL)

    def up_weight(self, after):
        return self.up_ref[...]

    def down_weight(self, h, after):
        return self.down_refs[h][...].reshape(D_FF, D_MODEL // 2)

    def _chips_start(self, slabs, from_sibling, rows, tag):
        sums = [_pair_add(s, r, self.csel, tr, f"pair_add_{tag}_{i}")
                for i, (s, r, tr) in enumerate(zip(slabs, from_sibling, rows))]
        lands = [lax.empty((3,) + s.shape[1:], s.dtype) for s in sums]
        self.reduces[tag] = _remote_start(sums, lands, _chips_plan, 3 * len(sums), f"reduce_start_{tag}")
        return self.reduces[tag][4]

    def mlp_grads(self, h2, du, act, dx3b):
        def send(part, tag, after):
            st = _remote_start([part], [lax.empty(part.shape, part.dtype)], _pair4_plan, 4,
                               f"reduce_pair_start_{tag}", after=after)
            return st

        def received(st, after, tag):
            return _remote_wait(st, after, _pair4_plan, f"reduce_pair_wait_{tag}")[1][0]

        def to_chips(sums, tag):
            self.reduces[tag] = _remote_start([sums], [lax.empty((3,) + sums.shape[1:], sums.dtype)], _chips_plan, 3,
                                              f"reduce_start_{tag}")
            return self.reduces[tag][4]

        up_send = _grad_w_up(h2, du, "grad_w_up_send", sel=(self.csel, True))
        st_up = send(up_send, "up", None)
        down_send = _grad_w_down(act, dx3b, "grad_w_down_send", sel=(self.csel, True), after=st_up[4])
        st_down = send(down_send, "down", None)
        up_sum = _grad_w_up(h2, du, "grad_w_up_keep", sel=(self.csel, False), add=received(st_up, down_send, "up"),
                            after=st_down[4])
        token = to_chips(up_sum, "up")
        down_sum = _grad_w_down(act, dx3b, "grad_w_down_keep", sel=(self.csel, False),
                                add=received(st_down, up_sum, "down"), after=token)
        return to_chips(down_sum, "down")

    def _pair_start(self, slabs, tag):
        land = lax.empty((4,) + slabs.shape[1:], slabs.dtype)
        self.pairs[tag] = _remote_start([slabs], [land], _pair_plan, 4, f"reduce_pair_start_{tag}")
        return self.pairs[tag][4]

    def grad_sent(self, tag, after):
        slabs, from_sibling = _remote_wait(self.pairs[tag], after, _pair_plan, f"reduce_pair_wait_{tag}")
        return self._chips_start(slabs, from_sibling, [slabs[0].shape[1]], tag)

    def out_grad(self, g_out):
        return self._pair_start(g_out.reshape(N_DEV, D_MODEL // N_DEV, D_MODEL), "out")

    def in_grad(self, g_in):
        return self._pair_start(
            jnp.stack([_natural_rows(g_in, SUPER_STEP * j, SUPER_STEP * j + SUPER) for j in range(N_DEV)]), "in")

    def small_start(self, small):
        self.st_small = _remote_start([small], [_landing(small, self.me)], _everyone_plan, N_DEV - 1, "gather_start_small")

    def small_end(self, after):
        return _remote_wait(self.st_small, after, _everyone_plan, "gather_small_wait")[1][0]

    def reduce_end(self, tag, after):
        return _remote_wait(self.reduces[tag], after, _chips_plan, f"reduce_wait_{tag}")


def kernel(x, mix_norm_g, w_in, conv_w, conv_b, dt_bias, A_log, D_skip, ssm_norm_g, attn_sinks, attn_out_norm_g, w_out, mlp_norm_g, w_up, w_down, final_norm_g, loss_target, m_mix_norm_g, m_w_in, m_conv_w, m_conv_b, m_dt_bias, m_A_log, m_D_skip, m_ssm_norm_g, m_attn_sinks, m_attn_out_norm_g, m_w_out, m_mlp_norm_g, m_w_up, m_w_down, m_final_norm_g, v_mix_norm_g, v_w_in, v_conv_w, v_conv_b, v_dt_bias, v_A_log, v_D_skip, v_ssm_norm_g, v_attn_sinks, v_attn_out_norm_g, v_w_out, v_mlp_norm_g, v_w_up, v_w_down, v_final_norm_g):
    xi, yi, ci = _coords()
    me = 4 * xi + 2 * yi + ci
    csel = jnp.reshape(ci, (1,)).astype(jnp.int32)
    qsel = jnp.reshape(2 * xi + yi, (1,)).astype(jnp.int32)
    w = dict(mix_norm_g=mix_norm_g, conv_b=conv_b, dt_bias=dt_bias, A_log=A_log, D_skip=D_skip,
             ssm_norm_g=ssm_norm_g, attn_sinks=attn_sinks, attn_out_norm_g=attn_out_norm_g, mlp_norm_g=mlp_norm_g,
             final_norm_g=final_norm_g)
    hooks = _ShardedWeights(w_in[0], w_out[0], conv_w[0], w_up[0], w_down[0], me, csel)
    p = dict(w)
    dx, small = _local_step(x[0], loss_target[0], p, hooks)
    hooks.small_start(small)
    big = {}
    after = dx
    for name, wt, mt, vt, tile in [
            ("up", w_up, m_w_up, v_w_up, (512, SLAB)), ("down", w_down, m_w_down, v_w_down, (256, D_MODEL)),
            ("out", w_out, m_w_out, v_w_out, (256, D_MODEL))]:
        (chip_sums,), (from_chips,) = hooks.reduce_end(name, after)
        res = _adamw_big(wt[0], mt[0], vt[0], chip_sums, from_chips, qsel, tile, f"adamw_w_{name}")
        big["w_" + name] = tuple(r[None] for r in res)
        after = res[0]
    (chip_sums,), (from_chips,) = hooks.reduce_end("in", after)
    g_super = _sum_partials(chip_sums, from_chips, qsel, 512, "grad_w_in_sum")
    g_in = lax.dynamic_slice(g_super, (2 * me, 0), (PER, D_MODEL))
    res = _adamw_tiled(jnp.transpose(w_in[0]), g_in, jnp.transpose(m_w_in[0]), jnp.transpose(v_w_in[0]), 512,
                       "adamw_w_in")
    big["w_in"] = tuple(jnp.transpose(r)[None] for r in (g_in, *res))
    after = res[0]
    gsum = _small_sum(hooks.small_end(after), "small_sum")
    loss = gsum[9, 64]
    gs = _unpack_small(gsum, CONV_DIM)
    cw = CONV_DIM // N_DEV
    g_conv_shard = lax.dynamic_slice(gsum[5:9, :], (0, me * cw), (CONV_K, cw))

    def pack(s):
        return _pack_small(s["mix_norm_g"], s["conv_b"], s["ssm_norm_g"], s["attn_out_norm_g"], s["mlp_norm_g"],
                           s["final_norm_g"], s["conv_w"][0], s["dt_bias"], s["A_log"], s["D_skip"], s["attn_sinks"])

    wp = pack(dict(w, conv_w=conv_w))
    mp = pack(dict(mix_norm_g=m_mix_norm_g, conv_b=m_conv_b, ssm_norm_g=m_ssm_norm_g,
                   attn_out_norm_g=m_attn_out_norm_g, mlp_norm_g=m_mlp_norm_g, final_norm_g=m_final_norm_g,
                   conv_w=m_conv_w, dt_bias=m_dt_bias, A_log=m_A_log, D_skip=m_D_skip, attn_sinks=m_attn_sinks))
    vp = pack(dict(mix_norm_g=v_mix_norm_g, conv_b=v_conv_b, ssm_norm_g=v_ssm_norm_g,
                   attn_out_norm_g=v_attn_out_norm_g, mlp_norm_g=v_mlp_norm_g, final_norm_g=v_final_norm_g,
                   conv_w=v_conv_w, dt_bias=v_dt_bias, A_log=v_A_log, D_skip=v_D_skip, attn_sinks=v_attn_sinks))
    gp = jnp.concatenate([gsum[0:5], jnp.pad(g_conv_shard, ((0, 0), (0, D_MODEL - cw))), gsum[9:10],
                          jnp.zeros((SMALL_ROWS - 10, D_MODEL), F32)], axis=0)
    dp, mnp, vnp = _adamw_small(wp, gp, mp, vp, "adamw_small")
    grads = dict(gs, conv_w=g_conv_shard[None])
    deltas = _unpack_small(dp, cw)
    new_m = _unpack_small(mnp, cw)
    new_v = _unpack_small(vnp, cw)
    for k, name in enumerate(["w_in", "w_out", "w_up", "w_down"]):
        grads[name], deltas[name], new_m[name], new_v[name] = big[name]
    return (loss, dx[None], *[grads[n] for n in WEIGHT_ORDER], *[deltas[n] for n in WEIGHT_ORDER],
            *[new_m[n] for n in WEIGHT_ORDER], *[new_v[n] for n in WEIGHT_ORDER])
```

```python
import jax
import jax.numpy as jnp
from jax import lax
from jax.experimental import pallas as pl
from jax.experimental.pallas import tpu as pltpu
from jax.experimental.pallas import tpu_sc as plsc

F32 = jnp.float32
BF16 = jnp.bfloat16
MESH = pl.DeviceIdType.MESH

EPS = 1e-5
D_MODEL = 2048
D_INNER = 1024
N_HEADS = 16
HEAD_DIM = 64
N_GROUPS = 4
D_STATE = 128
CHUNK = 128
CONV_K = 4
CONV_DIM = 2048
ATTN_W = 1024
KV_W = 128
WINDOW = 128
D_FF = 8192
IN_PROJ = 4368
N_DEV = 8
NP = 4608
OFF_Z, OFF_X, OFF_B, OFF_C, OFF_Q, OFF_K, OFF_V, OFF_DT = 0, 1024, 2048, 2560, 3072, 4096, 4224, 4352
NAT_DT = 3072

ADAM_LR = 0.001
ADAM_B1 = 0.9
ADAM_B2 = 0.999
ADAM_EPS = 1e-08
ADAM_WD = 0.01
ADAM_STEP = 10

VMEM_LIMIT = 52 * 1024 * 1024
SMALL_ROWS = 16
NEG = -1e30


def _cparams(sem=None):
    return pltpu.CompilerParams(dimension_semantics=sem, vmem_limit_bytes=VMEM_LIMIT)


def _split3(v):
    hi = v.astype(BF16)
    rest = v - hi.astype(F32)
    mid = rest.astype(BF16)
    return hi, mid, (rest - mid.astype(F32)).astype(BF16)


def _hdot(a, b, data):
    if data == "a":
        sel = b.astype(BF16)
        return sum(_dot_nn(part, sel) for part in _split3(a))
    sel = a.astype(BF16)
    return sum(_dot_nn(sel, part) for part in _split3(b))


def _dot_nn(a, b):
    return lax.dot_general(a, b, (((1,), (0,)), ((), ())), preferred_element_type=F32)


def _dot_nt(a, b):
    return lax.dot_general(a, b, (((1,), (1,)), ((), ())), preferred_element_type=F32)


def _dot_tn(a, b):
    return lax.dot_general(a, b, (((0,), (0,)), ((), ())), preferred_element_type=F32)


def _softplus(v):
    return jnp.maximum(v, 0.0) + jnp.log1p(jnp.exp(-jnp.abs(v)))


def _sigmoid(v):
    return 1.0 / (1.0 + jnp.exp(-v))


def _matmul(a, b, *, mode, grid, a_spec, b_spec, out_shapes, out_specs, tile, name,
            extras=(), extra_specs=(), epilogue=None, after=None, dot_fn=None, prefetch=None):
    nk = grid[2]
    n_ex = len(extras)
    n_out = len(out_shapes)
    bs, b_specs = (b, b_spec) if isinstance(b, tuple) else ((b,), (b_spec,))
    n_in = 1 + len(bs)
    dot = dot_fn if dot_fn is not None else {"nn": _dot_nn, "nt": _dot_nt, "tn": _dot_tn}[mode]

    def finish(acc, ex_refs, out_refs):
        res = (acc,) if epilogue is None else epilogue(acc, *[e[...] for e in ex_refs])
        for o, r in zip(out_refs, res):
            o[...] = r.astype(o.dtype)

    def body(*refs):
        ex_refs = refs[n_in:n_in + n_ex]
        out_refs = refs[n_in + n_ex:n_in + n_ex + n_out]
        part = dot(*[r[...].astype(BF16) for r in refs[:n_in]])
        if nk == 1:
            finish(part, ex_refs, out_refs)
        else:
            acc_ref = refs[-1]
            k = pl.program_id(2)

            @pl.when(k == 0)
            def _():
                acc_ref[...] = part

            @pl.when(k > 0)
            def _():
                acc_ref[...] += part

            @pl.when(k == nk - 1)
            def _():
                finish(acc_ref[...], ex_refs, out_refs)

    scratch = [] if nk == 1 else [pltpu.VMEM(tile, F32)]
    n_pre = 0 if prefetch is None else 1
    tok_specs = [] if after is None else [pl.BlockSpec((8, 128), lambda *_: (0, 0))]
    tok_args = [] if after is None else [after]

    def body_with_token(*refs):
        refs = refs[n_pre:]
        body(*refs[:n_in + n_ex], *refs[n_in + n_ex + len(tok_args):])

    in_specs = [a_spec, *b_specs, *extra_specs, *tok_specs]
    params = _cparams(("parallel", "parallel", "arbitrary"))
    if prefetch is None:
        return pl.pallas_call(
            body_with_token, grid=grid, in_specs=in_specs, out_specs=list(out_specs), out_shape=list(out_shapes),
            scratch_shapes=scratch, name=name, compiler_params=params)(a, *bs, *extras, *tok_args)
    return pl.pallas_call(
        body_with_token,
        grid_spec=pltpu.PrefetchScalarGridSpec(num_scalar_prefetch=1, grid=grid, in_specs=in_specs,
                                               out_specs=list(out_specs), scratch_shapes=scratch),
        out_shape=list(out_shapes), name=name, compiler_params=params)(prefetch, a, *bs, *extras, *tok_args)


def _mm_simple(a, b, *, mode, M, N, K, tm, tn, tk, out_dtype, name, extras=(), epilogue=None, n_out=1,
               out_dtypes=None, after=None):
    grid = (M // tm, N // tn, K // tk)
    if mode == "nn":
        a_spec = pl.BlockSpec((tm, tk), lambda i, j, k: (i, k))
        b_spec = pl.BlockSpec((tk, tn), lambda i, j, k: (k, j))
    elif mode == "nt":
        a_spec = pl.BlockSpec((tm, tk), lambda i, j, k: (i, k))
        b_spec = pl.BlockSpec((tn, tk), lambda i, j, k: (j, k))
    else:
        a_spec = pl.BlockSpec((tk, tm), lambda i, j, k: (k, i))
        b_spec = pl.BlockSpec((tk, tn), lambda i, j, k: (k, j))
    o_spec = pl.BlockSpec((tm, tn), lambda i, j, k: (i, j))
    dts = out_dtypes if out_dtypes is not None else [out_dtype] * n_out
    return _matmul(a, b, mode=mode, grid=grid, a_spec=a_spec, b_spec=b_spec,
                   out_shapes=[jax.ShapeDtypeStruct((M, N), d) for d in dts],
                   out_specs=[o_spec] * len(dts), tile=(tm, tn), name=name,
                   extras=extras, extra_specs=[o_spec] * len(extras), epilogue=epilogue, after=after)


ROW_BLOCK = 256


def _rmsnorm_fwd(x, g, name):
    T, D = x.shape

    def body(x_ref, g_ref, o_ref):
        xf = x_ref[...]
        r = lax.rsqrt(jnp.mean(xf * xf, axis=-1, keepdims=True) + EPS)
        o_ref[...] = (xf * r * g_ref[...]).astype(BF16)

    return pl.pallas_call(
        body, grid=(T // ROW_BLOCK,),
        in_specs=[pl.BlockSpec((ROW_BLOCK, D), lambda i: (i, 0)), pl.BlockSpec((1, D), lambda i: (0, 0))],
        out_specs=pl.BlockSpec((ROW_BLOCK, D), lambda i: (i, 0)),
        out_shape=jax.ShapeDtypeStruct((T, D), BF16), name=name, compiler_params=_cparams(("parallel",)),
    )(x, g)


def _rmsnorm_bwd(dh, x, g, dres, name, with_bf16=True):
    T, D = x.shape

    def body(dh_ref, x_ref, g_ref, dres_ref, dx_ref, *rest):
        dg_ref = rest[-1]
        i = pl.program_id(0)
        xf = x_ref[...]
        r = lax.rsqrt(jnp.mean(xf * xf, axis=-1, keepdims=True) + EPS)
        xh = xf * r
        d = dh_ref[...]

        @pl.when(i == 0)
        def _():
            dg_ref[...] = jnp.zeros_like(dg_ref)

        dg_ref[...] += jnp.sum(d * xh, axis=0, keepdims=True)
        dxh = d * g_ref[...]
        dx = r * (dxh - xh * jnp.mean(dxh * xh, axis=-1, keepdims=True)) + dres_ref[...]
        dx_ref[...] = dx
        if with_bf16:
            rest[0][...] = dx.astype(BF16)

    row = pl.BlockSpec((ROW_BLOCK, D), lambda i: (i, 0))
    vec = pl.BlockSpec((1, D), lambda i: (0, 0))
    copies = [(row, jax.ShapeDtypeStruct((T, D), BF16))] if with_bf16 else []
    return pl.pallas_call(
        body, grid=(T // ROW_BLOCK,), in_specs=[row, row, vec, row],
        out_specs=[row, *[c[0] for c in copies], vec],
        out_shape=[jax.ShapeDtypeStruct((T, D), F32), *[c[1] for c in copies], jax.ShapeDtypeStruct((1, D), F32)],
        name=name, compiler_params=_cparams(("arbitrary",)),
    )(dh, x, g, dres)


def _final_loss(x3_halves, tgt, g, name):
    T, D = tgt.shape

    def body(xa_ref, xb_ref, t_ref, g_ref, loss_ref, dg_ref, dx_ref, dxb_ref):
        i = pl.program_id(0)
        xf = jnp.concatenate([xa_ref[...], xb_ref[...]], axis=1)
        r = lax.rsqrt(jnp.mean(xf * xf, axis=-1, keepdims=True) + EPS)
        xh = xf * r
        gg = g_ref[...]
        err = xh * gg - t_ref[...]

        @pl.when(i == 0)
        def _():
            dg_ref[...] = jnp.zeros_like(dg_ref)
            loss_ref[...] = jnp.zeros_like(loss_ref)

        part = jnp.sum(jnp.sum(err * err, axis=-1, keepdims=True), axis=0, keepdims=True) * (0.5 / D)
        loss_ref[...] += jnp.broadcast_to(part, loss_ref.shape)
        dout = err * (1.0 / D)
        dg_ref[...] += jnp.sum(dout * xh, axis=0, keepdims=True)
        dxh = dout * gg
        dx = r * (dxh - xh * jnp.mean(dxh * xh, axis=-1, keepdims=True))
        dx_ref[...] = dx
        dxb_ref[...] = dx.astype(BF16)

    row = pl.BlockSpec((ROW_BLOCK, D), lambda i: (i, 0))
    vec = pl.BlockSpec((1, D), lambda i: (0, 0))
    return pl.pallas_call(
        body, grid=(T // ROW_BLOCK,),
        in_specs=[pl.BlockSpec((ROW_BLOCK, D // 2), lambda i: (i, 0))] * 2 + [row, vec],
        out_specs=[pl.BlockSpec((1, 128), lambda i: (0, 0)), vec, row, row],
        out_shape=[jax.ShapeDtypeStruct((1, 128), F32), jax.ShapeDtypeStruct((1, D), F32),
                   jax.ShapeDtypeStruct((T, D), F32), jax.ShapeDtypeStruct((T, D), BF16)],
        name=name, compiler_params=_cparams(("arbitrary",)),
    )(*x3_halves, tgt, g)


CONV_BLOCK = 256


def _conv_apply(u, w, b):
    row = lax.broadcasted_iota(jnp.int32, u.shape, 0)
    acc = b + w[CONV_K - 1:CONV_K, :] * u
    shifted = []
    for j in range(1, CONV_K):
        uj = jnp.where(row >= j, pltpu.roll(u, j, axis=0), 0.0)
        shifted.append(uj)
        acc = acc + w[CONV_K - 1 - j:CONV_K - j, :] * uj
    return acc, shifted


def _conv_fwd(proj, conv_w, conv_b, name):
    T = proj.shape[0]
    cb0 = OFF_X // CONV_BLOCK

    def body(u_ref, w_ref, b_ref, o_ref):
        c, _ = _conv_apply(u_ref[...], w_ref[...], b_ref[...])
        o_ref[...] = c * _sigmoid(c)

    return pl.pallas_call(
        body, grid=(CONV_DIM // CONV_BLOCK,),
        in_specs=[pl.BlockSpec((T, CONV_BLOCK), lambda j: (0, cb0 + j)),
                  pl.BlockSpec((CONV_K, CONV_BLOCK), lambda j: (0, j)),
                  pl.BlockSpec((1, CONV_BLOCK), lambda j: (0, j))],
        out_specs=pl.BlockSpec((T, CONV_BLOCK), lambda j: (0, j)),
        out_shape=jax.ShapeDtypeStruct((T, CONV_DIM), F32), name=name, compiler_params=_cparams(("parallel",)),
    )(proj, conv_w, conv_b)


def _conv_bwd(proj, dact, conv_w, conv_b, dproj, name):
    T = proj.shape[0]
    cb0 = OFF_X // CONV_BLOCK

    def body(u_ref, d_ref, w_ref, b_ref, _, du_ref, dw_ref, db_ref):
        u = u_ref[...]
        w = w_ref[...]
        c, shifted = _conv_apply(u, w, b_ref[...])
        sg = _sigmoid(c)
        dc = d_ref[...] * sg * (1.0 + c * (1.0 - sg))
        row = lax.broadcasted_iota(jnp.int32, u.shape, 0)
        du = w[CONV_K - 1:CONV_K, :] * dc
        dw_ref[CONV_K - 1:CONV_K, :] = jnp.sum(dc * u, axis=0, keepdims=True)
        for j in range(1, CONV_K):
            dcj = jnp.where(row < T - j, pltpu.roll(dc, T - j, axis=0), 0.0)
            du = du + w[CONV_K - 1 - j:CONV_K - j, :] * dcj
            dw_ref[CONV_K - 1 - j:CONV_K - j, :] = jnp.sum(dc * shifted[j - 1], axis=0, keepdims=True)
        db_ref[...] = jnp.sum(dc, axis=0, keepdims=True)
        du_ref[...] = du.astype(BF16)

    return pl.pallas_call(
        body, grid=(CONV_DIM // CONV_BLOCK,),
        in_specs=[pl.BlockSpec((T, CONV_BLOCK), lambda j: (0, cb0 + j)),
                  pl.BlockSpec((T, CONV_BLOCK), lambda j: (0, j)),
                  pl.BlockSpec((CONV_K, CONV_BLOCK), lambda j: (0, j)),
                  pl.BlockSpec((1, CONV_BLOCK), lambda j: (0, j)), pl.BlockSpec(memory_space=pl.ANY)],
        out_specs=[pl.BlockSpec((T, CONV_BLOCK), lambda j: (0, cb0 + j)),
                   pl.BlockSpec((CONV_K, CONV_BLOCK), lambda j: (0, j)),
                   pl.BlockSpec((1, CONV_BLOCK), lambda j: (0, j))],
        out_shape=[jax.ShapeDtypeStruct(dproj.shape, BF16), jax.ShapeDtypeStruct((CONV_K, CONV_DIM), F32),
                   jax.ShapeDtypeStruct((1, CONV_DIM), F32)],
        input_output_aliases={4: 0}, name=name, compiler_params=_cparams(("parallel",)),
    )(proj, dact, conv_w, conv_b, dproj)


GROUP_W = D_INNER // N_GROUPS
HEADS_PER_GROUP = N_HEADS // N_GROUPS


def _expand_mat():
    h = lax.broadcasted_iota(jnp.int32, (N_HEADS, D_INNER), 0)
    j = lax.broadcasted_iota(jnp.int32, (N_HEADS, D_INNER), 1)
    return (j // HEAD_DIM == h).astype(F32)


def _reduce_mat(g):
    j = lax.broadcasted_iota(jnp.int32, (GROUP_W, N_HEADS), 0)
    h = lax.broadcasted_iota(jnp.int32, (GROUP_W, N_HEADS), 1)
    return (g * HEADS_PER_GROUP + j // HEAD_DIM == h).astype(F32)


def _col16(v, h):
    lane = lax.broadcasted_iota(jnp.int32, v.shape, 1)
    return jnp.sum(jnp.where(lane == h, v, 0.0), axis=1, keepdims=True)


def _ssd_pre(dt_raw, dtT_raw, dtb, dtbT, alog, alogT):
    Q = CHUNK
    xdt = dt_raw + dtb
    dt = _softplus(xdt)
    dtT = _softplus(dtT_raw + dtbT)
    A = -jnp.exp(alog)
    AT = -jnp.exp(alogT)
    row = lax.broadcasted_iota(jnp.int32, (Q, Q), 0)
    col = lax.broadcasted_iota(jnp.int32, (Q, Q), 1)
    tril = (row >= col).astype(F32)
    triu = (row <= col).astype(F32)
    cs = _hdot(tril, dt * A, "b")
    csT = _hdot(dtT * AT, triu, "a")
    return xdt, dt, A, cs, csT, row >= col, triu


def _decay_matrix(cs, csT, h, causal):
    seg = _col16(cs, h) - csT[h:h + 1, :]
    return jnp.where(causal, jnp.exp(jnp.minimum(seg, 0.0)), 0.0)


def _ssd_in_specs(nc, rev):
    def cidx(c):
        return (nc - 1 - c) if rev else c

    return [
        pl.BlockSpec((CHUNK, D_INNER), lambda c: (cidx(c), 0)),
        pl.BlockSpec((CHUNK, 512), lambda c: (cidx(c), 2)),
        pl.BlockSpec((CHUNK, 512), lambda c: (cidx(c), 3)),
        pl.BlockSpec((CHUNK, D_INNER), lambda c: (cidx(c), 0)),
        pl.BlockSpec((CHUNK, 128), lambda c: (cidx(c), OFF_DT // 128)),
        pl.BlockSpec((N_HEADS, CHUNK), lambda c: (0, cidx(c))),
        pl.BlockSpec((1, N_HEADS), lambda c: (0, 0)),
        pl.BlockSpec((N_HEADS, 1), lambda c: (0, 0)),
        pl.BlockSpec((1, N_HEADS), lambda c: (0, 0)),
        pl.BlockSpec((N_HEADS, 1), lambda c: (0, 0)),
        pl.BlockSpec((1, D_INNER), lambda c: (0, 0)),
        pl.BlockSpec((1, D_INNER), lambda c: (0, 0)),
    ]


def _ssd_fwd(xbc, proj, dtT, dtb, dtbT, alog, alogT, dfull, ng, name):
    T = xbc.shape[0]
    nc = T // CHUNK
    Q = CHUNK

    def body(xs_ref, B_ref, C_ref, z_ref, dt_ref, dtT_ref, dtb_ref, dtbT_ref, al_ref, alT_ref, df_ref, ng_ref,
             y_ref, ypre_ref, hs_ref, h_scr):
        c = pl.program_id(0)

        @pl.when(c == 0)
        def _():
            h_scr[...] = jnp.zeros_like(h_scr)

        _, dt, _, cs, csT, causal, _ = _ssd_pre(dt_ref[:, :N_HEADS], dtT_ref[...], dtb_ref[...], dtbT_ref[...],
                                                al_ref[...], alT_ref[...])
        ex = _expand_mat()
        dt_full = _hdot(dt, ex, "a")
        cs_full = _hdot(cs, ex, "a")
        cs_last = cs_full[Q - 1:Q, :]
        xs = xs_ref[...]
        xd = xs * dt_full
        e_full = jnp.exp(cs_full)
        dec_full = jnp.exp(cs_last - cs_full)
        cd_full = jnp.exp(cs_last)
        lane_head = lax.broadcasted_iota(jnp.int32, (1, GROUP_W), 1) // HEAD_DIM
        for g in range(N_GROUPS):
            sl = slice(g * GROUP_W, (g + 1) * GROUP_W)
            Bg = B_ref[:, g * D_STATE:(g + 1) * D_STATE].astype(BF16)
            Cg = C_ref[:, g * D_STATE:(g + 1) * D_STATE].astype(BF16)
            CB = _dot_nt(Cg, Bg)
            hg = h_scr[g]
            yoff = _dot_nn(Cg, hg.astype(BF16)) * e_full[:, sl]
            xd_g = xd[:, sl]
            S = _dot_tn(Bg, (xd_g * dec_full[:, sl]).astype(BF16))
            xd_b = xd_g.astype(BF16)
            ydiag = jnp.zeros((Q, GROUP_W), F32)
            for r in range(HEADS_PER_GROUP):
                Lm = _decay_matrix(cs, csT, g * HEADS_PER_GROUP + r, causal)
                Gm = (CB * Lm).astype(BF16)
                ydiag = ydiag + _dot_nn(Gm, jnp.where(lane_head == r, xd_b, jnp.zeros_like(xd_b)))
            hs_ref[0, g] = hg
            h_scr[g] = hg * cd_full[:, sl] + S
            ypre = ydiag + yoff + xs[:, sl] * df_ref[:, sl]
            ypre_ref[:, sl] = ypre
            zg = z_ref[:, sl]
            yz = ypre * zg * _sigmoid(zg)
            rn = lax.rsqrt(jnp.mean(yz * yz, axis=-1, keepdims=True) + EPS)
            y_ref[:, sl] = (yz * rn * ng_ref[:, sl]).astype(BF16)

    return pl.pallas_call(
        body, grid=(nc,), in_specs=_ssd_in_specs(nc, False),
        out_specs=[pl.BlockSpec((CHUNK, D_INNER), lambda c: (c, 0)),
                   pl.BlockSpec((CHUNK, D_INNER), lambda c: (c, 0)),
                   pl.BlockSpec((1, N_GROUPS, D_STATE, GROUP_W), lambda c: (c, 0, 0, 0))],
        out_shape=[jax.ShapeDtypeStruct((T, D_INNER + ATTN_W), BF16), jax.ShapeDtypeStruct((T, D_INNER), F32),
                   jax.ShapeDtypeStruct((nc, N_GROUPS, D_STATE, GROUP_W), F32)],
        scratch_shapes=[pltpu.VMEM((N_GROUPS, D_STATE, GROUP_W), F32)],
        name=name, compiler_params=_cparams(("arbitrary",)),
    )(xbc, xbc, xbc, proj, proj, dtT, dtb, dtbT, alog, alogT, dfull, ng)


def _ssd_bwd(xbc, proj, dtT, dtb, dtbT, alog, alogT, dfull, ng, ypre, hs, dy, name):
    T = xbc.shape[0]
    nc = T // CHUNK
    Q = CHUNK

    def body(xs_ref, B_ref, C_ref, z_ref, dt_ref, dtT_ref, dtb_ref, dtbT_ref, al_ref, alT_ref, df_ref, ng_ref,
             ypre_ref, hs_ref, dy_ref,
             dz_ref, dxbc_ref, ddtb_ref, dal_ref, dD_ref, dng_ref, dh_scr):
        step = pl.program_id(0)

        @pl.when(step == 0)
        def _():
            dh_scr[...] = jnp.zeros_like(dh_scr)
            ddtb_ref[...] = jnp.zeros_like(ddtb_ref)
            dal_ref[...] = jnp.zeros_like(dal_ref)
            dD_ref[...] = jnp.zeros_like(dD_ref)
            dng_ref[...] = jnp.zeros_like(dng_ref)

        xdt, dt, A, cs, csT, causal, triu = _ssd_pre(dt_ref[:, :N_HEADS], dtT_ref[...], dtb_ref[...],
                                                    dtbT_ref[...], al_ref[...], alT_ref[...])
        ex = _expand_mat()
        dt_full = _hdot(dt, ex, "a")
        cs_full = _hdot(cs, ex, "a")
        cs_last = cs_full[Q - 1:Q, :]
        xs = xs_ref[...]
        xd = xs * dt_full
        e_full = jnp.exp(cs_full)
        dec_full = jnp.exp(cs_last - cs_full)
        cd_full = jnp.exp(cs_last)
        lane_head = lax.broadcasted_iota(jnp.int32, (1, GROUP_W), 1) // HEAD_DIM
        is_last = lax.broadcasted_iota(jnp.int32, (Q, 1), 0) == Q - 1
        dcs16 = jnp.zeros((Q, N_HEADS), F32)
        ddtx16 = jnp.zeros((Q, N_HEADS), F32)
        dD16 = jnp.zeros((8, N_HEADS), F32)
        lane16 = lax.broadcasted_iota(jnp.int32, (1, N_HEADS), 1)
        sub16 = lax.broadcasted_iota(jnp.int32, (N_HEADS, 1), 0)
        col_sums = jnp.zeros((N_HEADS, Q), F32)
        for g in range(N_GROUPS):
            sl = slice(g * GROUP_W, (g + 1) * GROUP_W)
            red = _reduce_mat(g)
            ypre_g = ypre_ref[:, sl]
            zg = z_ref[:, sl]
            sg = _sigmoid(zg)
            silu = zg * sg
            yz = ypre_g * silu
            rn = lax.rsqrt(jnp.mean(yz * yz, axis=-1, keepdims=True) + EPS)
            yh = yz * rn
            dy_g = dy_ref[:, sl]
            dng_ref[:, sl] += jnp.sum(dy_g * yh, axis=0, keepdims=True)
            dyh = dy_g * ng_ref[:, sl]
            dyz = rn * (dyh - yh * jnp.mean(dyh * yh, axis=-1, keepdims=True))
            dY = dyz * silu
            dz_ref[:, sl] = (dyz * ypre_g * sg * (1.0 + zg * (1.0 - sg))).astype(BF16)
            xs_g = xs[:, sl]
            xd_g = xd[:, sl]
            dec_g = dec_full[:, sl]
            cd_g = cd_full[:, sl]
            d_g = df_ref[:, sl]
            Bg = B_ref[:, g * D_STATE:(g + 1) * D_STATE].astype(BF16)
            Cg = C_ref[:, g * D_STATE:(g + 1) * D_STATE].astype(BF16)
            CB = _dot_nt(Cg, Bg)
            hg = hs_ref[0, g]
            hgb = hg.astype(BF16)
            yoff = _dot_nn(Cg, hgb) * e_full[:, sl]
            dhn = dh_scr[g]
            dhnb = dhn.astype(BF16)
            dYE = (dY * e_full[:, sl]).astype(BF16)
            dC = _dot_nt(dYE, hgb)
            dh_direct = _dot_tn(Cg, dYE)
            dXdd = _dot_nn(Bg, dhnb)
            dB = _dot_nt((xd_g * dec_g).astype(BF16), dhnb)
            dcd = jnp.sum(dhn * hg, axis=0, keepdims=True)
            dh_scr[g] = dh_direct + cd_g * dhn
            dYb = dY.astype(BF16)
            xd_b = xd_g.astype(BF16)
            dCB = jnp.zeros((Q, Q), F32)
            dXd = dXdd * dec_g
            for r in range(HEADS_PER_GROUP):
                h = g * HEADS_PER_GROUP + r
                Lm = _decay_matrix(cs, csT, h, causal)
                Gf = CB * Lm
                dYr = jnp.where(lane_head == r, dYb, jnp.zeros_like(dYb))
                dG = _dot_nt(dYr, xd_b)
                dCB = dCB + dG * Lm
                dXd = dXd + _dot_tn(Gf.astype(BF16), dYr)
                Mm = dG * Gf
                dcs16 = dcs16 + jnp.where(lane16 == h, jnp.sum(Mm, axis=1, keepdims=True), 0.0)
                col_sums = col_sums + jnp.where(sub16 == h, jnp.sum(Mm, axis=0, keepdims=True), 0.0)
            dCBb = dCB.astype(BF16)
            dC = dC + _dot_nn(dCBb, Bg)
            dB = dB + _dot_tn(dCBb, Cg)
            w_state = dXdd * dec_g * xd_g
            t_last = jnp.sum(w_state, axis=0, keepdims=True) + dcd * cd_g
            dcs_g = dY * yoff - w_state + jnp.where(is_last, t_last, 0.0)
            dcs16 = dcs16 + _hdot(dcs_g, red, "a")
            ddtx16 = ddtx16 + _hdot(dXd * xs_g, red, "a")
            dD16 = dD16 + _hdot(jnp.broadcast_to(jnp.sum(dY * xs_g, axis=0, keepdims=True), (8, GROUP_W)), red, "a")
            dxbc_ref[:, sl] = dXd * dt_full[:, sl] + dY * d_g
            dxbc_ref[:, D_INNER + g * D_STATE:D_INNER + (g + 1) * D_STATE] = dB
            dxbc_ref[:, D_INNER + 512 + g * D_STATE:D_INNER + 512 + (g + 1) * D_STATE] = dC
        eye = (lax.broadcasted_iota(jnp.int32, (N_HEADS, N_HEADS), 0)
               == lax.broadcasted_iota(jnp.int32, (N_HEADS, N_HEADS), 1)).astype(BF16)
        dcs16 = dcs16 - sum(_dot_tn(part, eye) for part in _split3(col_sums))
        da = _hdot(triu, dcs16, "b")
        ddt = da * A + ddtx16
        ddt_raw = ddt * _sigmoid(xdt)
        pr = lax.broadcasted_iota(jnp.int32, (N_HEADS, 128), 0)
        pc = lax.broadcasted_iota(jnp.int32, (N_HEADS, 128), 1)
        dz_ref[:, D_INNER:OFF_DT] = jnp.zeros((Q, OFF_DT - D_INNER), BF16)
        dz_ref[:, OFF_DT:OFF_DT + 128] = _hdot(ddt_raw, (pr == pc).astype(F32), "a").astype(BF16)
        dz_ref[:, OFF_DT + 128:] = jnp.zeros((Q, NP - OFF_DT - 128), BF16)
        ddtb_ref[...] += jnp.sum(ddt_raw, axis=0, keepdims=True)
        dal_ref[...] += jnp.sum(da * dt, axis=0, keepdims=True) * A
        dD_ref[...] += dD16[0:1, :]

    def rc(c):
        return nc - 1 - c

    in_specs = _ssd_in_specs(nc, True) + [
        pl.BlockSpec((CHUNK, D_INNER), lambda c: (rc(c), 0)),
        pl.BlockSpec((1, N_GROUPS, D_STATE, GROUP_W), lambda c: (rc(c), 0, 0, 0)),
        pl.BlockSpec((CHUNK, D_INNER), lambda c: (rc(c), 0)),
    ]
    small = pl.BlockSpec((1, N_HEADS), lambda c: (0, 0))
    return pl.pallas_call(
        body, grid=(nc,), in_specs=in_specs,
        out_specs=[pl.BlockSpec((CHUNK, NP), lambda c: (rc(c), 0)),
                   pl.BlockSpec((CHUNK, CONV_DIM), lambda c: (rc(c), 0)),
                   small, small, small,
                   pl.BlockSpec((1, D_INNER), lambda c: (0, 0))],
        out_shape=[jax.ShapeDtypeStruct((T, NP), BF16), jax.ShapeDtypeStruct((T, CONV_DIM), F32),
                   jax.ShapeDtypeStruct((1, N_HEADS), F32), jax.ShapeDtypeStruct((1, N_HEADS), F32),
                   jax.ShapeDtypeStruct((1, N_HEADS), F32), jax.ShapeDtypeStruct((1, D_INNER), F32)],
        scratch_shapes=[pltpu.VMEM((N_GROUPS, D_STATE, GROUP_W), F32)],
        name=name, compiler_params=_cparams(("arbitrary",)),
    )(xbc, xbc, xbc, proj, proj, dtT, dtb, dtbT, alog, alogT, dfull, ng, ypre, hs, dy)


N_PAIRS = ATTN_W // 128
PAIRS_PER_KV = N_PAIRS // 2
ATTN_SCALE = HEAD_DIM ** -0.5


def _kv_variants(kk):
    lo = lax.broadcasted_iota(jnp.int32, kk.shape, 1) < HEAD_DIM
    zero = jnp.zeros_like(kk)
    k00 = jnp.where(lo, kk, zero)
    k11 = jnp.where(lo, zero, kk)
    k01 = pltpu.roll(k00, HEAD_DIM, axis=1)
    k10 = pltpu.roll(k11, HEAD_DIM, axis=1)
    return [[k00.astype(BF16), k01.astype(BF16)], [k10.astype(BF16), k11.astype(BF16)]]


LOG2E = 1.4426950408889634


def _own_block():
    i = lax.broadcasted_iota(jnp.int32, (WINDOW, WINDOW), 0)
    j = lax.broadcasted_iota(jnp.int32, (WINDOW, WINDOW), 1)
    return j <= i


def _fold(own, a):
    return jnp.where(own, a[:, WINDOW:], a[:, :WINDOW])


def _attn_probs(qp, kvar, own, prev_bias, sk):
    s = _dot_nt(qp, kvar)
    sb = jnp.where(own, s[:, WINDOW:], s[:, :WINDOW] + prev_bias) * (ATTN_SCALE * LOG2E)
    sk2 = sk * LOG2E
    m = jnp.maximum(jnp.max(sb, axis=1, keepdims=True), sk2)
    pe = jnp.exp2(sb - m)
    es = jnp.exp2(sk2 - m)
    den = jnp.sum(pe, axis=1, keepdims=True) + es
    inv = 1.0 / den
    return pe * inv, es * inv


def _unfold(own, a):
    zero = jnp.zeros_like(a)
    return jnp.where(own, zero, a), jnp.where(own, a, zero)


def _sink(sinks, r):
    lane = lax.broadcasted_iota(jnp.int32, sinks.shape, 1)
    return jnp.sum(jnp.where(lane == r, sinks, 0.0), axis=1, keepdims=True)


def _kv_specs():
    return [pl.BlockSpec((WINDOW, KV_W), lambda n: (jnp.maximum(n - 1, 0), OFF_K // KV_W)),
            pl.BlockSpec((WINDOW, KV_W), lambda n: (n, OFF_K // KV_W)),
            pl.BlockSpec((WINDOW, KV_W), lambda n: (jnp.maximum(n - 1, 0), OFF_V // KV_W)),
            pl.BlockSpec((WINDOW, KV_W), lambda n: (n, OFF_V // KV_W))]


def _attn_fwd(proj, sinks, og, ycat, name):
    T = proj.shape[0]
    nb = T // WINDOW

    def body(q_ref, kp_ref, kc_ref, vp_ref, vc_ref, s_ref, og_ref, _, y_ref, o_ref, p_ref, ps_ref):
        n = pl.program_id(0)
        kv = _kv_variants(jnp.concatenate([kp_ref[...], kc_ref[...]], axis=0))
        vv = _kv_variants(jnp.concatenate([vp_ref[...], vc_ref[...]], axis=0))
        own = _own_block()
        prev_bias = jnp.where(n > 0, 0.0, NEG)
        sinks_v = s_ref[...]
        lane = lax.broadcasted_iota(jnp.int32, (1, 128), 1)
        ssq = jnp.zeros((WINDOW, 1), F32)
        sink_probs = jnp.zeros((WINDOW, 128), F32)
        for p in range(N_PAIRS):
            j = p // PAIRS_PER_KV
            qp = q_ref[:, p * 128:(p + 1) * 128].astype(BF16)
            o_pair = jnp.zeros((WINDOW, 128), F32)
            for par in range(2):
                r = 2 * p + par
                pn, ps = _attn_probs(qp, kv[j][par], own, prev_bias, _sink(sinks_v, r))
                pb = pn.astype(BF16)
                p_ref[:, r * 128:(r + 1) * 128] = pb
                sink_probs = jnp.where(lane == r, ps, sink_probs)
                p_prev, p_own = _unfold(own, pb)
                o_pair = o_pair + _dot_nn(p_prev, vv[j][par][:WINDOW]) + _dot_nn(p_own, vv[j][par][WINDOW:])
            o_ref[:, p * 128:(p + 1) * 128] = o_pair
            ssq = ssq + jnp.sum(o_pair * o_pair, axis=1, keepdims=True)
        ps_ref[...] = sink_probs
        rn = lax.rsqrt(ssq * (1.0 / ATTN_W) + EPS)
        y_ref[...] = (o_ref[...] * rn * og_ref[...]).astype(BF16)

    return pl.pallas_call(
        body, grid=(nb,),
        in_specs=[pl.BlockSpec((WINDOW, ATTN_W), lambda n: (n, OFF_Q // ATTN_W)), *_kv_specs(),
                  pl.BlockSpec((1, N_HEADS), lambda n: (0, 0)), pl.BlockSpec((1, ATTN_W), lambda n: (0, 0)), ANY],
        out_specs=[pl.BlockSpec((WINDOW, ATTN_W), lambda n: (n, 1)), pl.BlockSpec((WINDOW, ATTN_W), lambda n: (n, 0)),
                   pl.BlockSpec((WINDOW, N_HEADS * 128), lambda n: (n, 0)), pl.BlockSpec((WINDOW, 128), lambda n: (n, 0))],
        out_shape=[jax.ShapeDtypeStruct(ycat.shape, BF16), jax.ShapeDtypeStruct((T, ATTN_W), F32),
                   jax.ShapeDtypeStruct((T, N_HEADS * 128), BF16), jax.ShapeDtypeStruct((T, 128), F32)],
        input_output_aliases={7: 0}, name=name, compiler_params=_cparams(("parallel",)),
    )(proj, proj, proj, proj, proj, sinks, og, ycat)


def _attn_bwd(proj, og, o, dy, probs, sink_probs, dproj, name):
    T = proj.shape[0]
    nb = T // WINDOW

    def body(q_ref, kp_ref, kc_ref, vp_ref, vc_ref, og_ref, o_ref, dy_ref, p_ref, ps_ref, _,
             dq_ref, dk_ref, dv_ref, ds_ref, dog_ref, qt_scr, dot_scr, ds_scr, p_scr):
        n = pl.program_id(0)

        @pl.when(n == 0)
        def _():
            dk_ref[...] = jnp.zeros_like(dk_ref)
            dv_ref[...] = jnp.zeros_like(dv_ref)
            ds_ref[...] = jnp.zeros_like(ds_ref)
            dog_ref[...] = jnp.zeros_like(dog_ref)

        kv = _kv_variants(jnp.concatenate([kp_ref[...], kc_ref[...]], axis=0))
        vv = _kv_variants(jnp.concatenate([vp_ref[...], vc_ref[...]], axis=0))
        own = _own_block()
        sink_probs_v = ps_ref[...]
        of = o_ref[...]
        rn = lax.rsqrt(jnp.mean(of * of, axis=-1, keepdims=True) + EPS)
        oh = of * rn
        dyf = dy_ref[...]
        dog_ref[...] += jnp.sum(dyf * oh, axis=0, keepdims=True)
        doh = dyf * og_ref[...]
        do = rn * (doh - oh * jnp.mean(doh * oh, axis=-1, keepdims=True))
        lane = lax.broadcasted_iota(jnp.int32, (1, 128), 1)
        lane16 = lax.broadcasted_iota(jnp.int32, (1, N_HEADS), 1)
        dsink = jnp.zeros((1, N_HEADS), F32)
        for p in range(N_PAIRS):
            j = p // PAIRS_PER_KV
            q_t = q_ref[:, p * 128:(p + 1) * 128].T.astype(BF16)
            do_p = do[:, p * 128:(p + 1) * 128]
            o_p = of[:, p * 128:(p + 1) * 128]
            do_b = do_p.astype(BF16)
            do_t = do_p.T.astype(BF16)
            prod = do_p * o_p
            dq_pair = jnp.zeros((WINDOW, 128), F32)
            for par in range(2):
                r = 2 * p + par
                half = (lane < HEAD_DIM) if par == 0 else (lane >= HEAD_DIM)
                pb = p_ref[:, r * 128:(r + 1) * 128]
                ps = jnp.sum(jnp.where(lane == r, sink_probs_v, 0.0), axis=1, keepdims=True)
                delta = jnp.sum(jnp.where(half, prod, 0.0), axis=1, keepdims=True)
                dP = _fold(own, _dot_nt(do_b, vv[j][par]))
                dS = pb.astype(F32) * (dP - delta)
                dsink = dsink + jnp.where(lane16 == r, -jnp.sum(ps * delta, axis=0, keepdims=True), 0.0)
                dS_parts = _unfold(own, (dS * ATTN_SCALE).astype(BF16))
                p_parts = _unfold(own, pb)
                at = ((p % PAIRS_PER_KV) * 2 + par) * WINDOW
                qt_scr[j, :, at:at + WINDOW] = q_t[par * HEAD_DIM:(par + 1) * HEAD_DIM]
                dot_scr[j, :, at:at + WINDOW] = do_t[par * HEAD_DIM:(par + 1) * HEAD_DIM]
                for blk in range(2):
                    dq_pair = dq_pair + _dot_nn(dS_parts[blk], kv[j][par][blk * WINDOW:(blk + 1) * WINDOW])
                    ds_scr[j, blk, at:at + WINDOW, :] = dS_parts[blk]
                    p_scr[j, blk, at:at + WINDOW, :] = p_parts[blk]
            dq_ref[:, p * 128:(p + 1) * 128] = dq_pair.astype(BF16)
        rows = [pl.multiple_of(jnp.maximum(n - 1, 0) * WINDOW, WINDOW), pl.multiple_of(n * WINDOW, WINDOW)]
        for lhs, rhs, ref in [(qt_scr, ds_scr, dk_ref), (dot_scr, p_scr, dv_ref)]:
            for blk in range(2):
                both_t = jnp.concatenate([_dot_nn(lhs[j], rhs[j, blk]) for j in range(2)], axis=0)
                ref[pl.ds(rows[blk], WINDOW), :] += both_t.T
        ds_ref[...] += dsink

    full_kv = pl.BlockSpec((T, KV_W), lambda n: (0, 0))
    blk = pl.BlockSpec((WINDOW, ATTN_W), lambda n: (n, 0))
    return pl.pallas_call(
        body, grid=(nb,),
        in_specs=[pl.BlockSpec((WINDOW, ATTN_W), lambda n: (n, OFF_Q // ATTN_W)), *_kv_specs(),
                  pl.BlockSpec((1, ATTN_W), lambda n: (0, 0)), blk, pl.BlockSpec((WINDOW, ATTN_W), lambda n: (n, 1)),
                  pl.BlockSpec((WINDOW, N_HEADS * 128), lambda n: (n, 0)),
                  pl.BlockSpec((WINDOW, 128), lambda n: (n, 0)), ANY],
        out_specs=[pl.BlockSpec((WINDOW, ATTN_W), lambda n: (n, OFF_Q // ATTN_W)), full_kv, full_kv,
                   pl.BlockSpec((1, N_HEADS), lambda n: (0, 0)), pl.BlockSpec((1, ATTN_W), lambda n: (0, 0))],
        out_shape=[jax.ShapeDtypeStruct(dproj.shape, BF16), jax.ShapeDtypeStruct((T, KV_W), F32),
                   jax.ShapeDtypeStruct((T, KV_W), F32), jax.ShapeDtypeStruct((1, N_HEADS), F32),
                   jax.ShapeDtypeStruct((1, ATTN_W), F32)],
        scratch_shapes=[pltpu.VMEM((2, HEAD_DIM, 8 * WINDOW), BF16), pltpu.VMEM((2, HEAD_DIM, 8 * WINDOW), BF16),
                        pltpu.VMEM((2, 2, 8 * WINDOW, WINDOW), BF16), pltpu.VMEM((2, 2, 8 * WINDOW, WINDOW), BF16)],
        input_output_aliases={10: 0}, name=name, compiler_params=_cparams(("arbitrary",)),
    )(proj, proj, proj, proj, proj, og, o, dy, probs, sink_probs, dproj)


ANY = pl.BlockSpec(memory_space=pl.ANY)


def _coords():
    return lax.axis_index("x"), lax.axis_index("y"), lax.axis_index("c")


HBM = pl.BlockSpec(memory_space=pltpu.HBM)
SEM = pl.BlockSpec(memory_space=pltpu.SEMAPHORE)
EFFECT = pltpu.SideEffectType.DATAFLOW_SIDE_EFFECTING


def _in_hbm(a):
    return pltpu.with_memory_space_constraint(a, pltpu.HBM)


def _remote_start(srcs, lands, plan, n_copies, name, after=None):
    ns, nb = len(srcs), len(srcs) + len(lands)
    n_after = 0 if after is None else 1

    def body(*refs):
        src_refs, land_refs = refs[:ns], refs[ns:nb]
        send_sems, recv_sems = refs[nb + n_after], refs[nb + n_after + 1]
        token = refs[-1]
        x, y, c = _coords()
        for i, (sv, dv, dev) in enumerate(plan(src_refs, land_refs, x, y, c)):
            pltpu.make_async_remote_copy(src_ref=sv, dst_ref=dv, send_sem=send_sems.at[i], recv_sem=recv_sems.at[i],
                                         device_id=dev, device_id_type=MESH).start()
        token[...] = jnp.zeros_like(token)

    bufs = list(srcs) + list(lands)
    outs = pl.pallas_call(
        body, name=name,
        out_shape=(pltpu.SemaphoreType.DMA((n_copies,)), pltpu.SemaphoreType.DMA((n_copies,)),
                   *[pltpu.HBM(b.shape, b.dtype) for b in bufs], jax.ShapeDtypeStruct((8, 128), F32)),
        in_specs=[HBM] * nb + [ANY] * n_after,
        out_specs=(SEM, SEM, *[HBM] * nb, pl.BlockSpec(memory_space=pltpu.VMEM)),
        input_output_aliases={i: 2 + i for i in range(nb)},
        compiler_params=pltpu.CompilerParams(has_side_effects=EFFECT),
    )(*[_in_hbm(b) for b in bufs], *([] if after is None else [after]))
    return outs[0], outs[1], list(outs[2:2 + ns]), list(outs[2 + ns:2 + nb]), outs[-1]


def _remote_wait(started, after, plan, name):
    send_sems, recv_sems, srcs, lands, _ = started
    ns, nb = len(srcs), len(srcs) + len(lands)

    def body(*refs):
        src_refs, land_refs = refs[:ns], refs[ns:nb]
        send_sems, recv_sems = refs[nb], refs[nb + 1]
        x, y, c = _coords()
        for i, (sv, dv, dev) in enumerate(plan(src_refs, land_refs, x, y, c)):
            cp = pltpu.make_async_remote_copy(src_ref=sv, dst_ref=dv, send_sem=send_sems.at[i],
                                              recv_sem=recv_sems.at[i], device_id=dev, device_id_type=MESH)
            cp.wait_send()
            cp.wait_recv()

    bufs = list(srcs) + list(lands)
    outs = pl.pallas_call(
        body, name=name, out_shape=tuple(pltpu.HBM(b.shape, b.dtype) for b in bufs),
        in_specs=[HBM] * nb + [SEM, SEM, ANY], out_specs=tuple([HBM] * nb),
        input_output_aliases={i: i for i in range(nb)},
        compiler_params=pltpu.CompilerParams(has_side_effects=EFFECT),
    )(*bufs, send_sems, recv_sems, after)
    return list(outs[:ns]), list(outs[ns:])


def _pair_plan(src_refs, land_refs, x, y, c):
    plan = []
    for s, l in zip(src_refs, land_refs):
        for q in range(4):
            plan.append((s.at[2 * q + (1 - c)], l.at[q], (x, y, 1 - c)))
    return plan


def _pair4_plan(src_refs, land_refs, x, y, c):
    plan = []
    for s, l in zip(src_refs, land_refs):
        for q in range(4):
            plan.append((s.at[q], l.at[q], (x, y, 1 - c)))
    return plan


def _chips_plan(src_refs, land_refs, x, y, c):
    plan = []
    for s, l in zip(src_refs, land_refs):
        for k, (tx, ty) in enumerate([(1 - x, y), (x, 1 - y), (1 - x, 1 - y)]):
            plan.append((s.at[2 * tx + ty], l.at[k], (tx, ty, c)))
    return plan


def _everyone_plan(src_refs, land_refs, x, y, c):
    me = 4 * x + 2 * y + c
    plan = []
    for s, l in zip(src_refs, land_refs):
        for fx, fy, fc in [(0, 0, 1), (1, 0, 0), (1, 0, 1), (0, 1, 0), (0, 1, 1), (1, 1, 0), (1, 1, 1)]:
            dev = ((1 - x) if fx else x, (1 - y) if fy else y, (1 - c) if fc else c)
            plan.append((s, l.at[me], dev))
    return plan


def _pair_add(g8, r1, csel, tr, name):
    _, R, C = r1.shape
    g4 = g8.reshape(4, 2, R, C)

    def body(c_ref, g_ref, r_ref, o_ref):
        o_ref[...] = (g_ref[...].astype(F32) + r_ref[...].astype(F32)).astype(BF16)

    return pl.pallas_call(
        body,
        grid_spec=pltpu.PrefetchScalarGridSpec(
            num_scalar_prefetch=1, grid=(4, R // tr),
            in_specs=[pl.BlockSpec((None, None, tr, C), lambda q, i, cs: (q, cs[0], i, 0)),
                      pl.BlockSpec((None, tr, C), lambda q, i, cs: (q, i, 0))],
            out_specs=pl.BlockSpec((None, tr, C), lambda q, i, cs: (q, i, 0))),
        out_shape=jax.ShapeDtypeStruct((4, R, C), BF16), name=name,
        compiler_params=_cparams(("parallel", "parallel")),
    )(csel, g4, r1)


def _adamw_math(w, g, m, v):
    m = ADAM_B1 * m + (1.0 - ADAM_B1) * g
    v = ADAM_B2 * v + (1.0 - ADAM_B2) * (g * g)
    m_hat = m / (1.0 - ADAM_B1 ** ADAM_STEP)
    v_hat = v / (1.0 - ADAM_B2 ** ADAM_STEP)
    delta = -ADAM_LR * (m_hat / (jnp.sqrt(v_hat) + ADAM_EPS) + ADAM_WD * w)
    return delta, m, v


def _adamw_big(w, m, v, p4, r3, qsel, tile, name):
    R, C = w.shape
    tr, tc = tile

    def body(q_ref, w_ref, m_ref, v_ref, p_ref, r_ref, g_out, d_out, m_out, v_out):
        g = p_ref[...].astype(F32) + r_ref[0].astype(F32) + r_ref[1].astype(F32) + r_ref[2].astype(F32)
        d, mn, vn = _adamw_math(w_ref[...], g, m_ref[...], v_ref[...])
        g_out[...] = g
        d_out[...] = d
        m_out[...] = mn
        v_out[...] = vn

    blk = pl.BlockSpec((tr, tc), lambda i, j, qs: (i, j))
    return pl.pallas_call(
        body,
        grid_spec=pltpu.PrefetchScalarGridSpec(
            num_scalar_prefetch=1, grid=(R // tr, C // tc),
            in_specs=[blk, blk, blk, pl.BlockSpec((None, tr, tc), lambda i, j, qs: (qs[0], i, j)),
                      pl.BlockSpec((3, tr, tc), lambda i, j, qs: (0, i, j))],
            out_specs=[blk, blk, blk, blk]),
        out_shape=[jax.ShapeDtypeStruct((R, C), F32)] * 4, name=name,
        compiler_params=_cparams(("parallel", "parallel")),
    )(qsel, w, m, v, p4, r3)


def _sum_partials(p4, r3, qsel, tc, name):
    _, R, C = p4.shape

    def body(q_ref, p_ref, r_ref, o_ref):
        o_ref[...] = p_ref[...].astype(F32) + r_ref[0].astype(F32) + r_ref[1].astype(F32) + r_ref[2].astype(F32)

    return pl.pallas_call(
        body,
        grid_spec=pltpu.PrefetchScalarGridSpec(
            num_scalar_prefetch=1, grid=(C // tc,),
            in_specs=[pl.BlockSpec((None, R, tc), lambda j, qs: (qs[0], 0, j)),
                      pl.BlockSpec((3, R, tc), lambda j, qs: (0, 0, j))],
            out_specs=pl.BlockSpec((R, tc), lambda j, qs: (0, j))),
        out_shape=jax.ShapeDtypeStruct((R, C), F32), name=name, compiler_params=_cparams(("parallel",)),
    )(qsel, p4, r3)


def _adamw_tiled(w, g, m, v, tc, name):
    R, C = w.shape

    def body(w_ref, g_ref, m_ref, v_ref, d_out, m_out, v_out):
        d, mn, vn = _adamw_math(w_ref[...], g_ref[...], m_ref[...], v_ref[...])
        d_out[...] = d
        m_out[...] = mn
        v_out[...] = vn

    blk = pl.BlockSpec((R, tc), lambda j: (0, j))
    return pl.pallas_call(
        body, grid=(C // tc,), in_specs=[blk] * 4, out_specs=[blk] * 3,
        out_shape=[jax.ShapeDtypeStruct((R, C), F32)] * 3, name=name, compiler_params=_cparams(("parallel",)),
    )(w, g, m, v)


def _small_sum(parts, name):
    def body(p_ref, o_ref):
        acc = p_ref[0]
        for d in range(1, N_DEV):
            acc = acc + p_ref[d]
        o_ref[...] = acc

    return pl.pallas_call(
        body, out_shape=jax.ShapeDtypeStruct(parts.shape[1:], F32), name=name,
        compiler_params=_cparams(),
    )(parts)


def _adamw_small(w, g, m, v, name):
    def body(w_ref, g_ref, m_ref, v_ref, d_out, m_out, v_out):
        d, mn, vn = _adamw_math(w_ref[...], g_ref[...], m_ref[...], v_ref[...])
        d_out[...] = d
        m_out[...] = mn
        v_out[...] = vn

    return pl.pallas_call(
        body, out_shape=[jax.ShapeDtypeStruct(w.shape, F32)] * 3, name=name, compiler_params=_cparams(),
    )(w, g, m, v)


def _row(*pieces):
    r = jnp.concatenate([p.reshape(1, -1) for p in pieces], axis=1)
    return jnp.pad(r, ((0, 0), (0, D_MODEL - r.shape[1])))


def _pack_small(mix, convb, ssmg, attng, mlpg, fing, convw, dtb, alog, dsk, sinks, extra=None):
    last = [dtb, alog, dsk, sinks] + ([extra] if extra is not None else [])
    rows = [_row(mix), _row(convb), _row(ssmg, attng), _row(mlpg), _row(fing),
            jnp.pad(convw, ((0, 0), (0, D_MODEL - convw.shape[1]))), _row(*last)]
    packed = jnp.concatenate(rows, axis=0)
    return jnp.pad(packed, ((0, SMALL_ROWS - packed.shape[0]), (0, 0)))


def _unpack_small(p, conv_n):
    return dict(
        mix_norm_g=p[0:1, :], conv_b=p[1:2, :], ssm_norm_g=p[2:3, :D_INNER], attn_out_norm_g=p[2:3, D_INNER:],
        mlp_norm_g=p[3:4, :], final_norm_g=p[4, :], conv_w=p[5:9, :conv_n][None],
        dt_bias=p[9:10, 0:16], A_log=p[9:10, 16:32], D_skip=p[9:10, 32:48], attn_sinks=p[9:10, 48:64])


WEIGHT_ORDER = ["mix_norm_g", "w_in", "conv_w", "conv_b", "dt_bias", "A_log", "D_skip", "ssm_norm_g", "attn_sinks",
                "attn_out_norm_g", "w_out", "mlp_norm_g", "w_up", "w_down", "final_norm_g"]


def _to_my_columns(w_nat):
    pad = jnp.zeros((w_nat.shape[0], NP - IN_PROJ), w_nat.dtype)
    return jnp.concatenate([w_nat[:, :NAT_DT], w_nat[:, NAT_DT + N_HEADS:], w_nat[:, NAT_DT:NAT_DT + N_HEADS], pad],
                           axis=1)


PER = IN_PROJ // N_DEV
SUPER_STEP = 544
SUPER = 576


def _natural_rows(g, lo, hi):
    segments = [(0, NAT_DT, 0), (NAT_DT, NAT_DT + N_HEADS, OFF_DT - NAT_DT), (NAT_DT + N_HEADS, IN_PROJ, -N_HEADS),
                (IN_PROJ, NP, 0)]
    pieces = [g[max(lo, a) + shift:min(hi, b) + shift] for a, b, shift in segments if max(lo, a) < min(hi, b)]
    return pieces[0] if len(pieces) == 1 else jnp.concatenate(pieces, axis=0)


def _w_in_from_super_slabs(sup):
    seam = SUPER - SUPER_STEP
    units = []
    for i in range(N_DEV):
        base = SUPER_STEP * i
        units.append((base, base + seam, sup[i, :seam] if i == 0 else sup[i - 1, SUPER_STEP:] + sup[i, :seam]))
        units.append((base + seam, base + SUPER_STEP, sup[i, seam:SUPER_STEP]))
    units.append((SUPER_STEP * N_DEV, SUPER_STEP * N_DEV + seam, sup[N_DEV - 1, SUPER_STEP:]))

    def natural(lo, hi):
        return [rows[max(lo, a) - a:min(hi, b) - a] for a, b, rows in units if max(lo, a) < min(hi, b)]

    pieces = natural(0, NAT_DT) + natural(NAT_DT + N_HEADS, IN_PROJ) + natural(NAT_DT, NAT_DT + N_HEADS)
    return jnp.concatenate(pieces + [jnp.zeros((NP - IN_PROJ, D_MODEL), sup.dtype)], axis=0)


def _to_natural_columns(w_my):
    return jnp.concatenate([w_my[:, :NAT_DT], w_my[:, OFF_DT:OFF_DT + N_HEADS], w_my[:, NAT_DT:OFF_DT]], axis=1)


SLAB = 1024


def _grad_w_up(h2, du, name, sel=None, add=None, after=None):
    T, D = h2.shape
    if sel is None:
        pick, n_slab, pre = (lambda j, *cs: j), N_DEV, None
    else:
        pre, other = sel
        pick, n_slab = (lambda j, cs: 2 * j + ((1 - cs[0]) if other else cs[0])), 4
    o_spec = pl.BlockSpec((None, D, SLAB), lambda i, j, k, *cs: (j, 0, 0))
    return _matmul(
        h2, du, mode="tn", grid=(1, n_slab, 1),
        a_spec=pl.BlockSpec((T, D), lambda i, j, k, *cs: (0, 0)),
        b_spec=pl.BlockSpec((T, SLAB), lambda i, j, k, *cs: (0, pick(j, *cs))),
        out_shapes=[jax.ShapeDtypeStruct((n_slab, D, SLAB), BF16)], out_specs=[o_spec], tile=(D, SLAB), name=name,
        extras=() if add is None else (add,), extra_specs=() if add is None else (o_spec,),
        epilogue=None if add is None else (lambda acc, r: (acc + r.astype(F32),)), after=after, prefetch=pre)[0]


def _grad_w_down(act, dx3b, name, sel=None, add=None, after=None):
    T, D = dx3b.shape
    if sel is None:
        pick, n_slab, pre = (lambda i, *cs: i), N_DEV, None
    else:
        pre, other = sel
        pick, n_slab = (lambda i, cs: 2 * i + ((1 - cs[0]) if other else cs[0])), 4
    o_spec = pl.BlockSpec((None, SLAB, D), lambda i, j, k, *cs: (i, 0, 0))
    return _matmul(
        act, dx3b, mode="tn", grid=(n_slab, 1, 1),
        a_spec=pl.BlockSpec((T, SLAB), lambda i, j, k, *cs: (0, pick(i, *cs))),
        b_spec=pl.BlockSpec((T, D), lambda i, j, k, *cs: (0, 0)),
        out_shapes=[jax.ShapeDtypeStruct((n_slab, SLAB, D), BF16)], out_specs=[o_spec], tile=(SLAB, D), name=name,
        extras=() if add is None else (add,), extra_specs=() if add is None else (o_spec,),
        epilogue=None if add is None else (lambda acc, r: (acc + r.astype(F32),)), after=after, prefetch=pre)[0]


class _FixedWeights:
    def __init__(self, w_in_p, w_out_f, w_up_s, w_down_f, conv_w_f):
        self.w = (w_in_p, w_out_f, w_up_s, w_down_f, conv_w_f)
        self.grads = {}

    def mixer_weights(self, after):
        return self.w[0], None

    def conv_weight(self, after):
        return self.w[4]

    def out_weight(self, after):
        return self.w[1]

    def up_weight(self, after):
        return self.w[2]

    def down_weight(self, h, after):
        return self.w[3][:, h * (D_MODEL // 2):(h + 1) * (D_MODEL // 2)]

    def mlp_grads(self, h2, du, act, dx3b):
        self.grads.update(w_up=_grad_w_up(h2, du, "grad_w_up"),
                          w_down=_grad_w_down(act, dx3b, "grad_w_down").reshape(D_FF, D_MODEL))
        return None

    def grad_sent(self, tag, after):
        return None

    def out_grad(self, g_out):
        self.grads.update(w_out=g_out)
        return None

    def in_grad(self, g_in):
        self.grads.update(w_in=g_in)
        return None


def _local_step(x, tgt, p, hooks):
    T = x.shape[0]
    D = D_MODEL
    h1 = _rmsnorm_fwd(x, p["mix_norm_g"], "norm_mix")
    w_in_t, token = hooks.mixer_weights(h1)
    (proj,) = _mm_simple(h1, w_in_t, mode="nt", M=T, N=NP, K=D, tm=min(T, 1024), tn=1536, tk=D, out_dtype=F32,
                         name="in_proj", after=token)
    conv_w_f = hooks.conv_weight(proj)
    xbc = _conv_fwd(proj, conv_w_f, p["conv_b"], "conv_fwd")
    dtT = proj[:, OFF_DT:OFF_DT + N_HEADS].T
    dtbT = p["dt_bias"].T
    alogT = p["A_log"].T
    dfull = jnp.repeat(p["D_skip"], HEAD_DIM, axis=1)
    ycat, ypre, hs = _ssd_fwd(xbc, proj, dtT, p["dt_bias"], dtbT, p["A_log"], alogT, dfull, p["ssm_norm_g"],
                              "ssd_fwd")
    ycat, o_att, probs, sink_probs = _attn_fwd(proj, p["attn_sinks"], p["attn_out_norm_g"], ycat, "attn_fwd")
    w_out_f = hooks.out_weight(ycat)
    tm = min(T, 1024)
    def residual_and_norm(acc, res, gain):
        x2 = acc + res
        return x2, x2 * lax.rsqrt(jnp.mean(x2 * x2, axis=-1, keepdims=True) + EPS) * gain

    rows = min(T, 512)
    x2, h2 = _matmul(
        ycat, w_out_f, mode="nn", grid=(T // rows, 1, 1),
        a_spec=pl.BlockSpec((rows, D), lambda i, j, k: (i, 0)), b_spec=pl.BlockSpec((D, D), lambda i, j, k: (0, 0)),
        out_shapes=[jax.ShapeDtypeStruct((T, D), F32), jax.ShapeDtypeStruct((T, D), BF16)],
        out_specs=[pl.BlockSpec((rows, D), lambda i, j, k: (i, 0))] * 2, tile=(rows, D), name="out_proj",
        extras=(x, p["mlp_norm_g"]),
        extra_specs=[pl.BlockSpec((rows, D), lambda i, j, k: (i, 0)), pl.BlockSpec((1, D), lambda i, j, k: (0, 0))],
        epilogue=residual_and_norm)
    w_up_s = hooks.up_weight(h2)
    grid = (T // tm, N_DEV, 1)
    u, act = _matmul(
        h2, w_up_s, mode="nn", grid=grid,
        a_spec=pl.BlockSpec((tm, D), lambda i, j, k: (i, 0)),
        b_spec=pl.BlockSpec((None, D, 1024), lambda i, j, k: (j, 0, 0)),
        out_shapes=[jax.ShapeDtypeStruct((T, D_FF), F32), jax.ShapeDtypeStruct((T, D_FF), BF16)],
        out_specs=[pl.BlockSpec((tm, 1024), lambda i, j, k: (i, j))] * 2, tile=(tm, 1024), name="mlp_up",
        epilogue=lambda acc: (acc, jnp.square(jnp.maximum(acc, 0.0))))
    half = D // 2
    w_down_halves, x3_halves = [], []
    for h in range(2):
        w_down_halves.append(hooks.down_weight(h, act if h == 0 else x3_halves[0]))
        x3_halves.append(_matmul(
            act, w_down_halves[h], mode="nn", grid=(T // tm, 1, D_FF // 2048),
            a_spec=pl.BlockSpec((tm, 2048), lambda i, j, k: (i, k)),
            b_spec=pl.BlockSpec((2048, half), lambda i, j, k: (k, 0)),
            out_shapes=[jax.ShapeDtypeStruct((T, half), F32)],
            out_specs=[pl.BlockSpec((tm, half), lambda i, j, k: (i, 0))], tile=(tm, half), name=f"mlp_down_{h}",
            extras=(x2,), extra_specs=[pl.BlockSpec((tm, half), lambda i, j, k, h=h: (i, h))],
            epilogue=lambda acc, res: (acc + res,))[0])
    loss_part, d_fin, dx3, dx3b = _final_loss(x3_halves, tgt, p["final_norm_g"].reshape(1, D), "loss_head")
    (du,) = _matmul(
        dx3b, tuple(w_down_halves), mode="nt", grid=(T // tm, D_FF // 1024, 1),
        a_spec=pl.BlockSpec((tm, D), lambda i, j, k: (i, 0)),
        b_spec=(pl.BlockSpec((1024, half), lambda i, j, k: (j, 0)),) * 2,
        out_shapes=[jax.ShapeDtypeStruct((T, D_FF), BF16)],
        out_specs=[pl.BlockSpec((tm, 1024), lambda i, j, k: (i, j))], tile=(tm, 1024), name="mlp_down_bwd",
        extras=(u,), extra_specs=[pl.BlockSpec((tm, 1024), lambda i, j, k: (i, j))],
        epilogue=lambda acc, uu: (acc * (2.0 * jnp.maximum(uu, 0.0)),),
        dot_fn=lambda a, b0, b1: _dot_nt(a[:, :half], b0) + _dot_nt(a[:, half:], b1))
    token = hooks.mlp_grads(h2, du, act, dx3b)
    (dh2,) = _matmul(
        du, w_up_s, mode="nt", grid=(T // tm, D // 1024, N_DEV // 2),
        a_spec=pl.BlockSpec((tm, 2048), lambda i, j, k: (i, k)),
        b_spec=pl.BlockSpec((2, 1024, 1024), lambda i, j, k: (k, j, 0)),
        out_shapes=[jax.ShapeDtypeStruct((T, D), F32)],
        out_specs=[pl.BlockSpec((tm, 1024), lambda i, j, k: (i, j))], tile=(tm, 1024), name="mlp_up_bwd",
        after=token, dot_fn=lambda a, b: _dot_nt(a[:, :1024], b[0]) + _dot_nt(a[:, 1024:], b[1]))
    dx2, dx2b, d_mlp = _rmsnorm_bwd(dh2, x2, p["mlp_norm_g"], dx3, "norm_mlp_bwd")
    (g_out,) = _mm_simple(ycat, dx2b, mode="tn", M=D, N=D, K=T, tm=1024, tn=1024, tk=T, out_dtype=BF16,
                          name="grad_w_out")
    token = hooks.out_grad(g_out)
    (dy,) = _mm_simple(dx2b, w_out_f, mode="nt", M=T, N=D, K=D, tm=tm, tn=1024, tk=D, out_dtype=F32,
                       name="out_proj_bwd", after=token)
    token = hooks.grad_sent("out", dy)
    ssm_g = p["ssm_norm_g"] if token is None else p["ssm_norm_g"] + token[0:1, 0:1]
    dproj, dxbc_act, d_dtb, d_alog, d_dskip, d_ssmg = _ssd_bwd(
        xbc, proj, dtT, p["dt_bias"], dtbT, p["A_log"], alogT, dfull, ssm_g, ypre, hs, dy, "ssd_bwd")
    dproj, d_convw, d_convb = _conv_bwd(proj, dxbc_act, conv_w_f, p["conv_b"], dproj, "conv_bwd")
    dproj, dk, dv, d_sinks, d_attng = _attn_bwd(proj, p["attn_out_norm_g"], o_att, dy, probs, sink_probs, dproj,
                                                "attn_bwd")
    dproj = lax.dynamic_update_slice(dproj, jnp.concatenate([dk, dv], axis=1).astype(BF16), (0, OFF_K))
    (g_in,) = _mm_simple(dproj, h1, mode="tn", M=NP, N=D, K=T, tm=1536, tn=1024, tk=T, out_dtype=BF16,
                         name="grad_w_in")
    token = hooks.in_grad(g_in)
    (dh1,) = _mm_simple(dproj, w_in_t, mode="nn", M=T, N=D, K=NP, tm=tm, tn=1024, tk=2304, out_dtype=F32,
                        name="in_proj_bwd", after=token)
    token = hooks.grad_sent("in", dh1)
    mix_g = p["mix_norm_g"] if token is None else p["mix_norm_g"] + token[0:1, 0:1]
    dx, d_mix = _rmsnorm_bwd(dh1, x, mix_g, dx2, "norm_mix_bwd", with_bf16=False)
    small = _pack_small(d_mix, d_convb, d_ssmg, d_attng, d_mlp, d_fin, d_convw, d_dtb, d_alog, d_dskip, d_sinks,
                        extra=loss_part[:, 0:1])
    return dx, small


def _rows_rotated(v, shift, name):
    R, C = v.shape
    tc = 512

    def body(s_ref, v_ref, o_ref):
        o_ref[...] = pltpu.roll(v_ref[...], s_ref[0], axis=0).astype(BF16)

    return pl.pallas_call(
        body,
        grid_spec=pltpu.PrefetchScalarGridSpec(
            num_scalar_prefetch=1, grid=(C // tc,), in_specs=[pl.BlockSpec((R, tc), lambda j, s: (0, j))],
            out_specs=pl.BlockSpec((R, tc), lambda j, s: (0, j))),
        out_shape=jax.ShapeDtypeStruct((R, C), BF16), name=name, compiler_params=_cparams(("parallel",)),
    )(shift, v)


def _landing(own, me):
    zone = lax.empty((N_DEV,) + own.shape, own.dtype)
    return lax.dynamic_update_slice(zone, own[None], (me,) + (0,) * own.ndim)


def _sequencer_gather(owns, split, me, collective_id, name):
    n = len(owns)
    zone_refs = [jax.new_ref(_landing(o, me), memory_space=pltpu.MemorySpace.HBM) for o in owns]
    own_refs = [jax.new_ref(o, memory_space=pltpu.MemorySpace.HBM) for o in owns]
    N_COPIES = 9

    @pl.kernel(mesh=plsc.ScalarSubcoreMesh(axis_name="sequencer", num_cores=1), name=name,
               scratch_types=(pltpu.SemaphoreType.DMA((n, N_COPIES)), pltpu.SemaphoreType.DMA((n, N_COPIES))),
               compiler_params=pltpu.CompilerParams(collective_id=collective_id))
    def launch(send_sems, recv_sems):
        x, y, c = _coords()
        sibling, xn, yn, diag = (x, y, 1 - c), (1 - x, y, c), (x, 1 - y, c), (1 - x, 1 - y, c)
        barrier = pltpu.get_barrier_semaphore()
        for peer in [sibling, xn, yn, diag]:
            pl.semaphore_signal(barrier, inc=1, device_id=peer, device_id_type=MESH)
        pl.semaphore_wait(barrier, 4)

        def block(a, dev, half=None):
            ref = zone_refs[a].at[4 * dev[0] + 2 * dev[1] + dev[2]]
            if half is None:
                return ref
            rows = owns[a].shape[0] // 2
            return ref.at[pl.ds(half * rows, rows)]

        def copy(a, k, src, dst, to):
            return pltpu.make_async_remote_copy(src_ref=src, dst_ref=dst, send_sem=send_sems.at[a, k],
                                                recv_sem=recv_sems.at[a, k], device_id=to, device_id_type=MESH)

        me_dev = (x, y, c)
        sent = []
        first = {}
        for a in range(n):
            for k, peer in enumerate([sibling, xn, yn] + ([] if split[a] else [diag])):
                first[a, k] = copy(a, k, own_refs[a], block(a, me_dev), peer)
                first[a, k].start()
                sent.append(first[a, k])
        from_sibling = []
        for a in range(n):
            first[a, 1].wait_recv()
            sent.append(copy(a, 4, block(a, xn), block(a, xn), sibling))
            if split[a]:
                sent.append(copy(a, 6, block(a, xn, 0), block(a, xn, 0), yn))
            first[a, 2].wait_recv()
            sent.append(copy(a, 5, block(a, yn), block(a, yn), sibling))
            if split[a]:
                sent.append(copy(a, 7, block(a, yn, 1), block(a, yn, 1), xn))
            for cp in sent[-(4 if split[a] else 2):]:
                cp.start()
        for a in range(n):
            if split[a]:
                copy(a, 6, block(a, diag, 0), block(a, diag, 0), yn).wait_recv()
                sent.append(copy(a, 8, block(a, diag, 0), block(a, diag, 0), sibling))
                sent[-1].start()
                copy(a, 7, block(a, diag, 1), block(a, diag, 1), xn).wait_recv()
                sent.append(copy(a, 3, block(a, diag, 1), block(a, diag, 1), sibling))
                sent[-1].start()
            else:
                first[a, 3].wait_recv()
                sent.append(copy(a, 8, block(a, diag), block(a, diag), sibling))
                sent[-1].start()
        for a in range(n):
            first[a, 0].wait_recv()
            copy(a, 4, block(a, xn), block(a, xn), sibling).wait_recv()
            copy(a, 5, block(a, yn), block(a, yn), sibling).wait_recv()
            if split[a]:
                copy(a, 8, block(a, diag, 0), block(a, diag, 0), sibling).wait_recv()
                copy(a, 3, block(a, diag, 1), block(a, diag, 1), sibling).wait_recv()
            else:
                copy(a, 8, block(a, diag), block(a, diag), sibling).wait_recv()
        for cp in sent:
            cp.wait_send()

    launch()
    return zone_refs


class _ShardedWeights:
    def __init__(self, w_in, w_out, conv_w, w_up, w_down, me, csel):
        self.me, self.csel = me, csel
        padded = jnp.pad(jnp.transpose(w_in), ((0, SUPER - PER), (0, 0)))
        own_rows = _rows_rotated(padded, jnp.reshape(2 * me, (1,)).astype(jnp.int32), "w_in_super_slab")
        (self.in_ref,) = _sequencer_gather([own_rows], [True], me, 7, "gather_w_in_sequencer")
        self.out_ref, self.conv_ref = _sequencer_gather([w_out.astype(BF16), conv_w], [True, False], me, 8,
                                                        "gather_w_out_sequencer")
        (self.up_ref,) = _sequencer_gather([w_up.astype(BF16)], [True], me, 9, "gather_w_up_sequencer")
        down = w_down.astype(BF16)
        self.down_refs = [_sequencer_gather([down[:, h * (D_MODEL // 2):(h + 1) * (D_MODEL // 2)]], [True], me, 10 + h,
                                            f"gather_w_down_{h}_sequencer")[0] for h in range(2)]
        self.reduces = {}
        self.pairs = {}

    def mixer_weights(self, after):
        return _w_in_from_super_slabs(self.in_ref[...]), None

    def conv_weight(self, after):
        g_conv = self.conv_ref[...]
        return jnp.concatenate([g_conv[i] for i in range(N_DEV)], axis=1)

    def out_weight(self, after):
        return self.out_ref[...].reshape(D_MODEL, D_MODEL)

    def up_weight(self, after):
        return self.up_ref[...]

    def down_weight(self, h, after):
        return self.down_refs[h][...].reshape(D_FF, D_MODEL // 2)

    def _chips_start(self, slabs, from_sibling, rows, tag):
        sums = [_pair_add(s, r, self.csel, tr, f"pair_add_{tag}_{i}")
                for i, (s, r, tr) in enumerate(zip(slabs, from_sibling, rows))]
        lands = [lax.empty((3,) + s.shape[1:], s.dtype) for s in sums]
        self.reduces[tag] = _remote_start(sums, lands, _chips_plan, 3 * len(sums), f"reduce_start_{tag}")
        return self.reduces[tag][4]

    def mlp_grads(self, h2, du, act, dx3b):
        def send(part, tag, after):
            st = _remote_start([part], [lax.empty(part.shape, part.dtype)], _pair4_plan, 4,
                               f"reduce_pair_start_{tag}", after=after)
            return st

        def received(st, after, tag):
            return _remote_wait(st, after, _pair4_plan, f"reduce_pair_wait_{tag}")[1][0]

        def to_chips(sums, tag):
            self.reduces[tag] = _remote_start([sums], [lax.empty((3,) + sums.shape[1:], sums.dtype)], _chips_plan, 3,
                                              f"reduce_start_{tag}")
            return self.reduces[tag][4]

        up_send = _grad_w_up(h2, du, "grad_w_up_send", sel=(self.csel, True))
        st_up = send(up_send, "up", None)
        down_send = _grad_w_down(act, dx3b, "grad_w_down_send", sel=(self.csel, True), after=st_up[4])
        st_down = send(down_send, "down", None)
        up_sum = _grad_w_up(h2, du, "grad_w_up_keep", sel=(self.csel, False), add=received(st_up, down_send, "up"),
                            after=st_down[4])
        token = to_chips(up_sum, "up")
        down_sum = _grad_w_down(act, dx3b, "grad_w_down_keep", sel=(self.csel, False),
                                add=received(st_down, up_sum, "down"), after=token)
        return to_chips(down_sum, "down")

    def _pair_start(self, slabs, tag):
        land = lax.empty((4,) + slabs.shape[1:], slabs.dtype)
        self.pairs[tag] = _remote_start([slabs], [land], _pair_plan, 4, f"reduce_pair_start_{tag}")
        return self.pairs[tag][4]

    def grad_sent(self, tag, after):
        slabs, from_sibling = _remote_wait(self.pairs[tag], after, _pair_plan, f"reduce_pair_wait_{tag}")
        return self._chips_start(slabs, from_sibling, [slabs[0].shape[1]], tag)

    def out_grad(self, g_out):
        return self._pair_start(g_out.reshape(N_DEV, D_MODEL // N_DEV, D_MODEL), "out")

    def in_grad(self, g_in):
        return self._pair_start(
            jnp.stack([_natural_rows(g_in, SUPER_STEP * j, SUPER_STEP * j + SUPER) for j in range(N_DEV)]), "in")

    def small_start(self, small):
        self.st_small = _remote_start([small], [_landing(small, self.me)], _everyone_plan, N_DEV - 1, "gather_start_small")

    def small_end(self, after):
        return _remote_wait(self.st_small, after, _everyone_plan, "gather_small_wait")[1][0]

    def reduce_end(self, tag, after):
        return _remote_wait(self.reduces[tag], after, _chips_plan, f"reduce_wait_{tag}")


def kernel(x, mix_norm_g, w_in, conv_w, conv_b, dt_bias, A_log, D_skip, ssm_norm_g, attn_sinks, attn_out_norm_g, w_out, mlp_norm_g, w_up, w_down, final_norm_g, loss_target, m_mix_norm_g, m_w_in, m_conv_w, m_conv_b, m_dt_bias, m_A_log, m_D_skip, m_ssm_norm_g, m_attn_sinks, m_attn_out_norm_g, m_w_out, m_mlp_norm_g, m_w_up, m_w_down, m_final_norm_g, v_mix_norm_g, v_w_in, v_conv_w, v_conv_b, v_dt_bias, v_A_log, v_D_skip, v_ssm_norm_g, v_attn_sinks, v_attn_out_norm_g, v_w_out, v_mlp_norm_g, v_w_up, v_w_down, v_final_norm_g):
    xi, yi, ci = _coords()
    me = 4 * xi + 2 * yi + ci
    csel = jnp.reshape(ci, (1,)).astype(jnp.int32)
    qsel = jnp.reshape(2 * xi + yi, (1,)).astype(jnp.int32)
    w = dict(mix_norm_g=mix_norm_g, conv_b=conv_b, dt_bias=dt_bias, A_log=A_log, D_skip=D_skip,
             ssm_norm_g=ssm_norm_g, attn_sinks=attn_sinks, attn_out_norm_g=attn_out_norm_g, mlp_norm_g=mlp_norm_g,
             final_norm_g=final_norm_g)
    hooks = _ShardedWeights(w_in[0], w_out[0], conv_w[0], w_up[0], w_down[0], me, csel)
    p = dict(w)
    dx, small = _local_step(x[0], loss_target[0], p, hooks)
    hooks.small_start(small)
    big = {}
    after = dx
    for name, wt, mt, vt, tile in [
            ("up", w_up, m_w_up, v_w_up, (512, SLAB)), ("down", w_down, m_w_down, v_w_down, (256, D_MODEL)),
            ("out", w_out, m_w_out, v_w_out, (256, D_MODEL))]:
        (chip_sums,), (from_chips,) = hooks.reduce_end(name, after)
        res = _adamw_big(wt[0], mt[0], vt[0], chip_sums, from_chips, qsel, tile, f"adamw_w_{name}")
        big["w_" + name] = tuple(r[None] for r in res)
        after = res[0]
    (chip_sums,), (from_chips,) = hooks.reduce_end("in", after)
    g_super = _sum_partials(chip_sums, from_chips, qsel, 512, "grad_w_in_sum")
    g_in = lax.dynamic_slice(g_super, (2 * me, 0), (PER, D_MODEL))
    res = _adamw_tiled(jnp.transpose(w_in[0]), g_in, jnp.transpose(m_w_in[0]), jnp.transpose(v_w_in[0]), 512,
                       "adamw_w_in")
    big["w_in"] = tuple(jnp.transpose(r)[None] for r in (g_in, *res))
    after = res[0]
    gsum = _small_sum(hooks.small_end(after), "small_sum")
    loss = gsum[9, 64]
    gs = _unpack_small(gsum, CONV_DIM)
    cw = CONV_DIM // N_DEV
    g_conv_shard = lax.dynamic_slice(gsum[5:9, :], (0, me * cw), (CONV_K, cw))

    def pack(s):
        return _pack_small(s["mix_norm_g"], s["conv_b"], s["ssm_norm_g"], s["attn_out_norm_g"], s["mlp_norm_g"],
                           s["final_norm_g"], s["conv_w"][0], s["dt_bias"], s["A_log"], s["D_skip"], s["attn_sinks"])

    wp = pack(dict(w, conv_w=conv_w))
    mp = pack(dict(mix_norm_g=m_mix_norm_g, conv_b=m_conv_b, ssm_norm_g=m_ssm_norm_g,
                   attn_out_norm_g=m_attn_out_norm_g, mlp_norm_g=m_mlp_norm_g, final_norm_g=m_final_norm_g,
                   conv_w=m_conv_w, dt_bias=m_dt_bias, A_log=m_A_log, D_skip=m_D_skip, attn_sinks=m_attn_sinks))
    vp = pack(dict(mix_norm_g=v_mix_norm_g, conv_b=v_conv_b, ssm_norm_g=v_ssm_norm_g,
                   attn_out_norm_g=v_attn_out_norm_g, mlp_norm_g=v_mlp_norm_g, final_norm_g=v_final_norm_g,
                   conv_w=v_conv_w, dt_bias=v_dt_bias, A_log=v_A_log, D_skip=v_D_skip, attn_sinks=v_attn_sinks))
    gp = jnp.concatenate([gsum[0:5], jnp.pad(g_conv_shard, ((0, 0), (0, D_MODEL - cw))), gsum[9:10],
                          jnp.zeros((SMALL_ROWS - 10, D_MODEL), F32)], axis=0)
    dp, mnp, vnp = _adamw_small(wp, gp, mp, vp, "adamw_small")
    grads = dict(gs, conv_w=g_conv_shard[None])
    deltas = _unpack_small(dp, cw)
    new_m = _unpack_small(mnp, cw)
    new_v = _unpack_small(vnp, cw)
    for k, name in enumerate(["w_in", "w_out", "w_up", "w_down"]):
        grads[name], deltas[name], new_m[name], new_v[name] = big[name]
    return (loss, dx[None], *[grads[n] for n in WEIGHT_ORDER], *[deltas[n] for n in WEIGHT_ORDER],
            *[new_m[n] for n in WEIGHT_ORDER], *[new_v[n] for n in WEIGHT_ORDER])
```

```python
import jax
import jax.numpy as jnp
from jax import lax
from jax.experimental import pallas as pl
from jax.experimental.pallas import tpu as pltpu
from jax.experimental.pallas import tpu_sc as plsc

F32 = jnp.float32
BF16 = jnp.bfloat16
MESH = pl.DeviceIdType.MESH

EPS = 1e-5
D_MODEL = 2048
D_INNER = 1024
N_HEADS = 16
HEAD_DIM = 64
N_GROUPS = 4
D_STATE = 128
CHUNK = 128
CONV_K = 4
CONV_DIM = 2048
ATTN_W = 1024
KV_W = 128
WINDOW = 128
D_FF = 8192
IN_PROJ = 4368
N_DEV = 8
NP = 4608
OFF_Z, OFF_X, OFF_B, OFF_C, OFF_Q, OFF_K, OFF_V, OFF_DT = 0, 1024, 2048, 2560, 3072, 4096, 4224, 4352
NAT_DT = 3072

ADAM_LR = 0.001
ADAM_B1 = 0.9
ADAM_B2 = 0.999
ADAM_EPS = 1e-08
ADAM_WD = 0.01
ADAM_STEP = 10

VMEM_LIMIT = 52 * 1024 * 1024
SMALL_ROWS = 16
NEG = -1e30


def _cparams(sem=None):
    return pltpu.CompilerParams(dimension_semantics=sem, vmem_limit_bytes=VMEM_LIMIT)


def _split3(v):
    hi = v.astype(BF16)
    rest = v - hi.astype(F32)
    mid = rest.astype(BF16)
    return hi, mid, (rest - mid.astype(F32)).astype(BF16)


def _hdot(a, b, data):
    if data == "a":
        sel = b.astype(BF16)
        return sum(_dot_nn(part, sel) for part in _split3(a))
    sel = a.astype(BF16)
    return sum(_dot_nn(sel, part) for part in _split3(b))


def _dot_nn(a, b):
    return lax.dot_general(a, b, (((1,), (0,)), ((), ())), preferred_element_type=F32)


def _dot_nt(a, b):
    return lax.dot_general(a, b, (((1,), (1,)), ((), ())), preferred_element_type=F32)


def _dot_tn(a, b):
    return lax.dot_general(a, b, (((0,), (0,)), ((), ())), preferred_element_type=F32)


def _softplus(v):
    return jnp.maximum(v, 0.0) + jnp.log1p(jnp.exp(-jnp.abs(v)))


def _sigmoid(v):
    return 1.0 / (1.0 + jnp.exp(-v))


def _matmul(a, b, *, mode, grid, a_spec, b_spec, out_shapes, out_specs, tile, name,
            extras=(), extra_specs=(), epilogue=None, after=None, dot_fn=None, prefetch=None):
    nk = grid[2]
    n_ex = len(extras)
    n_out = len(out_shapes)
    bs, b_specs = (b, b_spec) if isinstance(b, tuple) else ((b,), (b_spec,))
    n_in = 1 + len(bs)
    dot = dot_fn if dot_fn is not None else {"nn": _dot_nn, "nt": _dot_nt, "tn": _dot_tn}[mode]

    def finish(acc, ex_refs, out_refs):
        res = (acc,) if epilogue is None else epilogue(acc, *[e[...] for e in ex_refs])
        for o, r in zip(out_refs, res):
            o[...] = r.astype(o.dtype)

    def body(*refs):
        ex_refs = refs[n_in:n_in + n_ex]
        out_refs = refs[n_in + n_ex:n_in + n_ex + n_out]
        part = dot(*[r[...].astype(BF16) for r in refs[:n_in]])
        if nk == 1:
            finish(part, ex_refs, out_refs)
        else:
            acc_ref = refs[-1]
            k = pl.program_id(2)

            @pl.when(k == 0)
            def _():
                acc_ref[...] = part

            @pl.when(k > 0)
            def _():
                acc_ref[...] += part

            @pl.when(k == nk - 1)
            def _():
                finish(acc_ref[...], ex_refs, out_refs)

    scratch = [] if nk == 1 else [pltpu.VMEM(tile, F32)]
    n_pre = 0 if prefetch is None else 1
    tok_specs = [] if after is None else [pl.BlockSpec((8, 128), lambda *_: (0, 0))]
    tok_args = [] if after is None else [after]

    def body_with_token(*refs):
        refs = refs[n_pre:]
        body(*refs[:n_in + n_ex], *refs[n_in + n_ex + len(tok_args):])

    in_specs = [a_spec, *b_specs, *extra_specs, *tok_specs]
    params = _cparams(("parallel", "parallel", "arbitrary"))
    if prefetch is None:
        return pl.pallas_call(
            body_with_token, grid=grid, in_specs=in_specs, out_specs=list(out_specs), out_shape=list(out_shapes),
            scratch_shapes=scratch, name=name, compiler_params=params)(a, *bs, *extras, *tok_args)
    return pl.pallas_call(
        body_with_token,
        grid_spec=pltpu.PrefetchScalarGridSpec(num_scalar_prefetch=1, grid=grid, in_specs=in_specs,
                                               out_specs=list(out_specs), scratch_shapes=scratch),
        out_shape=list(out_shapes), name=name, compiler_params=params)(prefetch, a, *bs, *extras, *tok_args)


def _mm_simple(a, b, *, mode, M, N, K, tm, tn, tk, out_dtype, name, extras=(), epilogue=None, n_out=1,
               out_dtypes=None, after=None):
    grid = (M // tm, N // tn, K // tk)
    if mode == "nn":
        a_spec = pl.BlockSpec((tm, tk), lambda i, j, k: (i, k))
        b_spec = pl.BlockSpec((tk, tn), lambda i, j, k: (k, j))
    elif mode == "nt":
        a_spec = pl.BlockSpec((tm, tk), lambda i, j, k: (i, k))
        b_spec = pl.BlockSpec((tn, tk), lambda i, j, k: (j, k))
    else:
        a_spec = pl.BlockSpec((tk, tm), lambda i, j, k: (k, i))
        b_spec = pl.BlockSpec((tk, tn), lambda i, j, k: (k, j))
    o_spec = pl.BlockSpec((tm, tn), lambda i, j, k: (i, j))
    dts = out_dtypes if out_dtypes is not None else [out_dtype] * n_out
    return _matmul(a, b, mode=mode, grid=grid, a_spec=a_spec, b_spec=b_spec,
                   out_shapes=[jax.ShapeDtypeStruct((M, N), d) for d in dts],
                   out_specs=[o_spec] * len(dts), tile=(tm, tn), name=name,
                   extras=extras, extra_specs=[o_spec] * len(extras), epilogue=epilogue, after=after)


ROW_BLOCK = 256


def _rmsnorm_fwd(x, g, name):
    T, D = x.shape

    def body(x_ref, g_ref, o_ref):
        xf = x_ref[...]
        r = lax.rsqrt(jnp.mean(xf * xf, axis=-1, keepdims=True) + EPS)
        o_ref[...] = (xf * r * g_ref[...]).astype(BF16)

    return pl.pallas_call(
        body, grid=(T // ROW_BLOCK,),
        in_specs=[pl.BlockSpec((ROW_BLOCK, D), lambda i: (i, 0)), pl.BlockSpec((1, D), lambda i: (0, 0))],
        out_specs=pl.BlockSpec((ROW_BLOCK, D), lambda i: (i, 0)),
        out_shape=jax.ShapeDtypeStruct((T, D), BF16), name=name, compiler_params=_cparams(("parallel",)),
    )(x, g)


def _rmsnorm_bwd(dh, x, g, dres, name, with_bf16=True):
    T, D = x.shape

    def body(dh_ref, x_ref, g_ref, dres_ref, dx_ref, *rest):
        dg_ref = rest[-1]
        i = pl.program_id(0)
        xf = x_ref[...]
        r = lax.rsqrt(jnp.mean(xf * xf, axis=-1, keepdims=True) + EPS)
        xh = xf * r
        d = dh_ref[...]

        @pl.when(i == 0)
        def _():
            dg_ref[...] = jnp.zeros_like(dg_ref)

        dg_ref[...] += jnp.sum(d * xh, axis=0, keepdims=True)
        dxh = d * g_ref[...]
        dx = r * (dxh - xh * jnp.mean(dxh * xh, axis=-1, keepdims=True)) + dres_ref[...]
        dx_ref[...] = dx
        if with_bf16:
            rest[0][...] = dx.astype(BF16)

    row = pl.BlockSpec((ROW_BLOCK, D), lambda i: (i, 0))
    vec = pl.BlockSpec((1, D), lambda i: (0, 0))
    copies = [(row, jax.ShapeDtypeStruct((T, D), BF16))] if with_bf16 else []
    return pl.pallas_call(
        body, grid=(T // ROW_BLOCK,), in_specs=[row, row, vec, row],
        out_specs=[row, *[c[0] for c in copies], vec],
        out_shape=[jax.ShapeDtypeStruct((T, D), F32), *[c[1] for c in copies], jax.ShapeDtypeStruct((1, D), F32)],
        name=name, compiler_params=_cparams(("arbitrary",)),
    )(dh, x, g, dres)


def _final_loss(x3_halves, tgt, g, name):
    T, D = tgt.shape

    def body(xa_ref, xb_ref, t_ref, g_ref, loss_ref, dg_ref, dx_ref, dxb_ref):
        i = pl.program_id(0)
        xf = jnp.concatenate([xa_ref[...], xb_ref[...]], axis=1)
        r = lax.rsqrt(jnp.mean(xf * xf, axis=-1, keepdims=True) + EPS)
        xh = xf * r
        gg = g_ref[...]
        err = xh * gg - t_ref[...]

        @pl.when(i == 0)
        def _():
            dg_ref[...] = jnp.zeros_like(dg_ref)
            loss_ref[...] = jnp.zeros_like(loss_ref)

        part = jnp.sum(jnp.sum(err * err, axis=-1, keepdims=True), axis=0, keepdims=True) * (0.5 / D)
        loss_ref[...] += jnp.broadcast_to(part, loss_ref.shape)
        dout = err * (1.0 / D)
        dg_ref[...] += jnp.sum(dout * xh, axis=0, keepdims=True)
        dxh = dout * gg
        dx = r * (dxh - xh * jnp.mean(dxh * xh, axis=-1, keepdims=True))
        dx_ref[...] = dx
        dxb_ref[...] = dx.astype(BF16)

    row = pl.BlockSpec((ROW_BLOCK, D), lambda i: (i, 0))
    vec = pl.BlockSpec((1, D), lambda i: (0, 0))
    return pl.pallas_call(
        body, grid=(T // ROW_BLOCK,),
        in_specs=[pl.BlockSpec((ROW_BLOCK, D // 2), lambda i: (i, 0))] * 2 + [row, vec],
        out_specs=[pl.BlockSpec((1, 128), lambda i: (0, 0)), vec, row, row],
        out_shape=[jax.ShapeDtypeStruct((1, 128), F32), jax.ShapeDtypeStruct((1, D), F32),
                   jax.ShapeDtypeStruct((T, D), F32), jax.ShapeDtypeStruct((T, D), BF16)],
        name=name, compiler_params=_cparams(("arbitrary",)),
    )(*x3_halves, tgt, g)


CONV_BLOCK = 256


def _conv_apply(u, w, b):
    row = lax.broadcasted_iota(jnp.int32, u.shape, 0)
    acc = b + w[CONV_K - 1:CONV_K, :] * u
    shifted = []
    for j in range(1, CONV_K):
        uj = jnp.where(row >= j, pltpu.roll(u, j, axis=0), 0.0)
        shifted.append(uj)
        acc = acc + w[CONV_K - 1 - j:CONV_K - j, :] * uj
    return acc, shifted


def _conv_fwd(proj, conv_w, conv_b, name):
    T = proj.shape[0]
    cb0 = OFF_X // CONV_BLOCK

    def body(u_ref, w_ref, b_ref, o_ref, ds_ref):
        c, _ = _conv_apply(u_ref[...], w_ref[...], b_ref[...])
        sg = _sigmoid(c)
        o_ref[...] = c * sg
        ds_ref[...] = sg * (1.0 + c * (1.0 - sg))

    out = pl.BlockSpec((T, CONV_BLOCK), lambda j: (0, j))
    return pl.pallas_call(
        body, grid=(CONV_DIM // CONV_BLOCK,),
        in_specs=[pl.BlockSpec((T, CONV_BLOCK), lambda j: (0, cb0 + j)),
                  pl.BlockSpec((CONV_K, CONV_BLOCK), lambda j: (0, j)),
                  pl.BlockSpec((1, CONV_BLOCK), lambda j: (0, j))],
        out_specs=[out, out], out_shape=[jax.ShapeDtypeStruct((T, CONV_DIM), F32)] * 2,
        name=name, compiler_params=_cparams(("parallel",)),
    )(proj, conv_w, conv_b)


def _conv_bwd(proj, dact, dsilu, conv_w, dproj, name):
    T = proj.shape[0]
    cb0 = OFF_X // CONV_BLOCK

    def body(u_ref, d_ref, s_ref, w_ref, _, du_ref, dw_ref, db_ref):
        u = u_ref[...]
        w = w_ref[...]
        dc = d_ref[...] * s_ref[...]
        row = lax.broadcasted_iota(jnp.int32, u.shape, 0)
        du = w[CONV_K - 1:CONV_K, :] * dc
        dw_ref[CONV_K - 1:CONV_K, :] = jnp.sum(dc * u, axis=0, keepdims=True)
        for j in range(1, CONV_K):
            dcj = jnp.where(row < T - j, pltpu.roll(dc, T - j, axis=0), 0.0)
            du = du + w[CONV_K - 1 - j:CONV_K - j, :] * dcj
            dw_ref[CONV_K - 1 - j:CONV_K - j, :] = jnp.sum(dcj * u, axis=0, keepdims=True)
        db_ref[...] = jnp.sum(dc, axis=0, keepdims=True)
        du_ref[...] = du.astype(BF16)

    blk = pl.BlockSpec((T, CONV_BLOCK), lambda j: (0, j))
    return pl.pallas_call(
        body, grid=(CONV_DIM // CONV_BLOCK,),
        in_specs=[pl.BlockSpec((T, CONV_BLOCK), lambda j: (0, cb0 + j)), blk, blk,
                  pl.BlockSpec((CONV_K, CONV_BLOCK), lambda j: (0, j)), pl.BlockSpec(memory_space=pl.ANY)],
        out_specs=[pl.BlockSpec((T, CONV_BLOCK), lambda j: (0, cb0 + j)),
                   pl.BlockSpec((CONV_K, CONV_BLOCK), lambda j: (0, j)),
                   pl.BlockSpec((1, CONV_BLOCK), lambda j: (0, j))],
        out_shape=[jax.ShapeDtypeStruct(dproj.shape, BF16), jax.ShapeDtypeStruct((CONV_K, CONV_DIM), F32),
                   jax.ShapeDtypeStruct((1, CONV_DIM), F32)],
        input_output_aliases={4: 0}, name=name, compiler_params=_cparams(("parallel",)),
    )(proj, dact, dsilu, conv_w, dproj)


GROUP_W = D_INNER // N_GROUPS
HEADS_PER_GROUP = N_HEADS // N_GROUPS


def _expand_mat():
    h = lax.broadcasted_iota(jnp.int32, (N_HEADS, D_INNER), 0)
    j = lax.broadcasted_iota(jnp.int32, (N_HEADS, D_INNER), 1)
    return (j // HEAD_DIM == h).astype(F32)


def _reduce_mat(g):
    j = lax.broadcasted_iota(jnp.int32, (GROUP_W, N_HEADS), 0)
    h = lax.broadcasted_iota(jnp.int32, (GROUP_W, N_HEADS), 1)
    return (g * HEADS_PER_GROUP + j // HEAD_DIM == h).astype(F32)


def _col16(v, h):
    lane = lax.broadcasted_iota(jnp.int32, v.shape, 1)
    return jnp.sum(jnp.where(lane == h, v, 0.0), axis=1, keepdims=True)


def _ssd_pre(dt_raw, dtT_raw, dtb, dtbT, alog, alogT):
    Q = CHUNK
    xdt = dt_raw + dtb
    dt = _softplus(xdt)
    dtT = _softplus(dtT_raw + dtbT)
    A = -jnp.exp(alog)
    AT = -jnp.exp(alogT)
    row = lax.broadcasted_iota(jnp.int32, (Q, Q), 0)
    col = lax.broadcasted_iota(jnp.int32, (Q, Q), 1)
    tril = (row >= col).astype(F32)
    triu = (row <= col).astype(F32)
    cs = _hdot(tril, dt * A, "b")
    csT = _hdot(dtT * AT, triu, "a")
    return xdt, dt, A, cs, csT, row >= col, triu


def _decay_matrix(cs, csT, h, causal):
    seg = _col16(cs, h) - csT[h:h + 1, :]
    return jnp.where(causal, jnp.exp(jnp.minimum(seg, 0.0)), 0.0)


def _ssd_in_specs(nc, rev):
    def cidx(c):
        return (nc - 1 - c) if rev else c

    return [
        pl.BlockSpec((CHUNK, D_INNER), lambda c: (cidx(c), 0)),
        pl.BlockSpec((CHUNK, 512), lambda c: (cidx(c), 2)),
        pl.BlockSpec((CHUNK, 512), lambda c: (cidx(c), 3)),
        pl.BlockSpec((CHUNK, D_INNER), lambda c: (cidx(c), 0)),
        pl.BlockSpec((CHUNK, 128), lambda c: (cidx(c), OFF_DT // 128)),
        pl.BlockSpec((N_HEADS, CHUNK), lambda c: (0, cidx(c))),
        pl.BlockSpec((1, N_HEADS), lambda c: (0, 0)),
        pl.BlockSpec((N_HEADS, 1), lambda c: (0, 0)),
        pl.BlockSpec((1, N_HEADS), lambda c: (0, 0)),
        pl.BlockSpec((N_HEADS, 1), lambda c: (0, 0)),
        pl.BlockSpec((1, D_INNER), lambda c: (0, 0)),
        pl.BlockSpec((1, D_INNER), lambda c: (0, 0)),
    ]


def _ssd_fwd(xbc, proj, dtT, dtb, dtbT, alog, alogT, dfull, ng, name):
    T = xbc.shape[0]
    nc = T // CHUNK
    Q = CHUNK

    def body(xs_ref, B_ref, C_ref, z_ref, dt_ref, dtT_ref, dtb_ref, dtbT_ref, al_ref, alT_ref, df_ref, ng_ref,
             y_ref, ypre_ref, hs_ref, h_scr):
        c = pl.program_id(0)

        @pl.when(c == 0)
        def _():
            h_scr[...] = jnp.zeros_like(h_scr)

        _, dt, _, cs, csT, causal, _ = _ssd_pre(dt_ref[:, :N_HEADS], dtT_ref[...], dtb_ref[...], dtbT_ref[...],
                                                al_ref[...], alT_ref[...])
        ex = _expand_mat()
        dt_full = _hdot(dt, ex, "a")
        cs_full = _hdot(cs, ex, "a")
        cs_last = cs_full[Q - 1:Q, :]
        xs = xs_ref[...]
        xd = xs * dt_full
        e_full = jnp.exp(cs_full)
        dec_full = jnp.exp(cs_last - cs_full)
        cd_full = jnp.exp(cs_last)
        lane_head = lax.broadcasted_iota(jnp.int32, (1, GROUP_W), 1) // HEAD_DIM
        for g in range(N_GROUPS):
            sl = slice(g * GROUP_W, (g + 1) * GROUP_W)
            Bg = B_ref[:, g * D_STATE:(g + 1) * D_STATE].astype(BF16)
            Cg = C_ref[:, g * D_STATE:(g + 1) * D_STATE].astype(BF16)
            CB = _dot_nt(Cg, Bg)
            hg = h_scr[g]
            yoff = _dot_nn(Cg, hg.astype(BF16)) * e_full[:, sl]
            xd_g = xd[:, sl]
            S = _dot_tn(Bg, (xd_g * dec_full[:, sl]).astype(BF16))
            xd_b = xd_g.astype(BF16)
            ydiag = jnp.zeros((Q, GROUP_W), F32)
            for r in range(HEADS_PER_GROUP):
                Lm = _decay_matrix(cs, csT, g * HEADS_PER_GROUP + r, causal)
                Gm = (CB * Lm).astype(BF16)
                ydiag = ydiag + _dot_nn(Gm, jnp.where(lane_head == r, xd_b, jnp.zeros_like(xd_b)))
            hs_ref[0, g] = hg
            h_scr[g] = hg * cd_full[:, sl] + S
            ypre = ydiag + yoff + xs[:, sl] * df_ref[:, sl]
            ypre_ref[:, sl] = ypre
            zg = z_ref[:, sl]
            yz = ypre * zg * _sigmoid(zg)
            rn = lax.rsqrt(jnp.mean(yz * yz, axis=-1, keepdims=True) + EPS)
            y_ref[:, sl] = (yz * rn * ng_ref[:, sl]).astype(BF16)

    return pl.pallas_call(
        body, grid=(nc,), in_specs=_ssd_in_specs(nc, False),
        out_specs=[pl.BlockSpec((CHUNK, D_INNER), lambda c: (c, 0)),
                   pl.BlockSpec((CHUNK, D_INNER), lambda c: (c, 0)),
                   pl.BlockSpec((1, N_GROUPS, D_STATE, GROUP_W), lambda c: (c, 0, 0, 0))],
        out_shape=[jax.ShapeDtypeStruct((T, D_INNER + ATTN_W), BF16), jax.ShapeDtypeStruct((T, D_INNER), F32),
                   jax.ShapeDtypeStruct((nc, N_GROUPS, D_STATE, GROUP_W), F32)],
        scratch_shapes=[pltpu.VMEM((N_GROUPS, D_STATE, GROUP_W), F32)],
        name=name, compiler_params=_cparams(("arbitrary",)),
    )(xbc, xbc, xbc, proj, proj, dtT, dtb, dtbT, alog, alogT, dfull, ng)


def _ssd_bwd(xbc, proj, dtT, dtb, dtbT, alog, alogT, dfull, ng, ypre, hs, dy, name):
    T = xbc.shape[0]
    nc = T // CHUNK
    Q = CHUNK

    def body(xs_ref, B_ref, C_ref, z_ref, dt_ref, dtT_ref, dtb_ref, dtbT_ref, al_ref, alT_ref, df_ref, ng_ref,
             ypre_ref, hs_ref, dy_ref,
             dz_ref, dxbc_ref, ddtb_ref, dal_ref, dD_ref, dng_ref, dh_scr):
        step = pl.program_id(0)

        @pl.when(step == 0)
        def _():
            dh_scr[...] = jnp.zeros_like(dh_scr)
            ddtb_ref[...] = jnp.zeros_like(ddtb_ref)
            dal_ref[...] = jnp.zeros_like(dal_ref)
            dD_ref[...] = jnp.zeros_like(dD_ref)
            dng_ref[...] = jnp.zeros_like(dng_ref)

        xdt, dt, A, cs, csT, causal, triu = _ssd_pre(dt_ref[:, :N_HEADS], dtT_ref[...], dtb_ref[...],
                                                    dtbT_ref[...], al_ref[...], alT_ref[...])
        ex = _expand_mat()
        dt_full = _hdot(dt, ex, "a")
        cs_full = _hdot(cs, ex, "a")
        cs_last = cs_full[Q - 1:Q, :]
        xs = xs_ref[...]
        xd = xs * dt_full
        e_full = jnp.exp(cs_full)
        dec_full = jnp.exp(cs_last - cs_full)
        cd_full = jnp.exp(cs_last)
        lane_head = lax.broadcasted_iota(jnp.int32, (1, GROUP_W), 1) // HEAD_DIM
        is_last = lax.broadcasted_iota(jnp.int32, (Q, 1), 0) == Q - 1
        dcs16 = jnp.zeros((Q, N_HEADS), F32)
        ddtx16 = jnp.zeros((Q, N_HEADS), F32)
        dD16 = jnp.zeros((8, N_HEADS), F32)
        lane16 = lax.broadcasted_iota(jnp.int32, (1, N_HEADS), 1)
        sub16 = lax.broadcasted_iota(jnp.int32, (N_HEADS, 1), 0)
        col_sums = jnp.zeros((N_HEADS, Q), F32)
        for g in range(N_GROUPS):
            sl = slice(g * GROUP_W, (g + 1) * GROUP_W)
            red = _reduce_mat(g)
            ypre_g = ypre_ref[:, sl]
            zg = z_ref[:, sl]
            sg = _sigmoid(zg)
            silu = zg * sg
            yz = ypre_g * silu
            rn = lax.rsqrt(jnp.mean(yz * yz, axis=-1, keepdims=True) + EPS)
            yh = yz * rn
            dy_g = dy_ref[:, sl]
            dng_ref[:, sl] += jnp.sum(dy_g * yh, axis=0, keepdims=True)
            dyh = dy_g * ng_ref[:, sl]
            dyz = rn * (dyh - yh * jnp.mean(dyh * yh, axis=-1, keepdims=True))
            dY = dyz * silu
            dz_ref[:, sl] = (dyz * ypre_g * sg * (1.0 + zg * (1.0 - sg))).astype(BF16)
            xs_g = xs[:, sl]
            xd_g = xd[:, sl]
            dec_g = dec_full[:, sl]
            cd_g = cd_full[:, sl]
            d_g = df_ref[:, sl]
            Bg = B_ref[:, g * D_STATE:(g + 1) * D_STATE].astype(BF16)
            Cg = C_ref[:, g * D_STATE:(g + 1) * D_STATE].astype(BF16)
            CB = _dot_nt(Cg, Bg)
            hg = hs_ref[0, g]
            hgb = hg.astype(BF16)
            yoff = _dot_nn(Cg, hgb) * e_full[:, sl]
            dhn = dh_scr[g]
            dhnb = dhn.astype(BF16)
            dYE = (dY * e_full[:, sl]).astype(BF16)
            dC = _dot_nt(dYE, hgb)
            dh_direct = _dot_tn(Cg, dYE)
            dXdd = _dot_nn(Bg, dhnb)
            dB = _dot_nt((xd_g * dec_g).astype(BF16), dhnb)
            dcd = jnp.sum(dhn * hg, axis=0, keepdims=True)
            dh_scr[g] = dh_direct + cd_g * dhn
            dYb = dY.astype(BF16)
            xd_b = xd_g.astype(BF16)
            dCB = jnp.zeros((Q, Q), F32)
            dXd = dXdd * dec_g
            for r in range(HEADS_PER_GROUP):
                h = g * HEADS_PER_GROUP + r
                Lm = _decay_matrix(cs, csT, h, causal)
                Gf = CB * Lm
                dYr = jnp.where(lane_head == r, dYb, jnp.zeros_like(dYb))
                dG = _dot_nt(dYr, xd_b)
                dCB = dCB + dG * Lm
                dXd = dXd + _dot_tn(Gf.astype(BF16), dYr)
                Mm = dG * Gf
                dcs16 = dcs16 + jnp.where(lane16 == h, jnp.sum(Mm, axis=1, keepdims=True), 0.0)
                col_sums = col_sums + jnp.where(sub16 == h, jnp.sum(Mm, axis=0, keepdims=True), 0.0)
            dCBb = dCB.astype(BF16)
            dC = dC + _dot_nn(dCBb, Bg)
            dB = dB + _dot_tn(dCBb, Cg)
            w_state = dXdd * dec_g * xd_g
            t_last = jnp.sum(w_state, axis=0, keepdims=True) + dcd * cd_g
            dcs_g = dY * yoff - w_state + jnp.where(is_last, t_last, 0.0)
            dcs16 = dcs16 + _hdot(dcs_g, red, "a")
            ddtx16 = ddtx16 + _hdot(dXd * xs_g, red, "a")
            dD16 = dD16 + _hdot(jnp.broadcast_to(jnp.sum(dY * xs_g, axis=0, keepdims=True), (8, GROUP_W)), red, "a")
            dxbc_ref[:, sl] = dXd * dt_full[:, sl] + dY * d_g
            dxbc_ref[:, D_INNER + g * D_STATE:D_INNER + (g + 1) * D_STATE] = dB
            dxbc_ref[:, D_INNER + 512 + g * D_STATE:D_INNER + 512 + (g + 1) * D_STATE] = dC
        eye = (lax.broadcasted_iota(jnp.int32, (N_HEADS, N_HEADS), 0)
               == lax.broadcasted_iota(jnp.int32, (N_HEADS, N_HEADS), 1)).astype(BF16)
        dcs16 = dcs16 - sum(_dot_tn(part, eye) for part in _split3(col_sums))
        da = _hdot(triu, dcs16, "b")
        ddt = da * A + ddtx16
        ddt_raw = ddt * _sigmoid(xdt)
        pr = lax.broadcasted_iota(jnp.int32, (N_HEADS, 128), 0)
        pc = lax.broadcasted_iota(jnp.int32, (N_HEADS, 128), 1)
        dz_ref[:, D_INNER:OFF_DT] = jnp.zeros((Q, OFF_DT - D_INNER), BF16)
        dz_ref[:, OFF_DT:OFF_DT + 128] = _hdot(ddt_raw, (pr == pc).astype(F32), "a").astype(BF16)
        dz_ref[:, OFF_DT + 128:] = jnp.zeros((Q, NP - OFF_DT - 128), BF16)
        ddtb_ref[...] += jnp.sum(ddt_raw, axis=0, keepdims=True)
        dal_ref[...] += jnp.sum(da * dt, axis=0, keepdims=True) * A
        dD_ref[...] += dD16[0:1, :]

    def rc(c):
        return nc - 1 - c

    in_specs = _ssd_in_specs(nc, True) + [
        pl.BlockSpec((CHUNK, D_INNER), lambda c: (rc(c), 0)),
        pl.BlockSpec((1, N_GROUPS, D_STATE, GROUP_W), lambda c: (rc(c), 0, 0, 0)),
        pl.BlockSpec((CHUNK, D_INNER), lambda c: (rc(c), 0)),
    ]
    small = pl.BlockSpec((1, N_HEADS), lambda c: (0, 0))
    return pl.pallas_call(
        body, grid=(nc,), in_specs=in_specs,
        out_specs=[pl.BlockSpec((CHUNK, NP), lambda c: (rc(c), 0)),
                   pl.BlockSpec((CHUNK, CONV_DIM), lambda c: (rc(c), 0)),
                   small, small, small,
                   pl.BlockSpec((1, D_INNER), lambda c: (0, 0))],
        out_shape=[jax.ShapeDtypeStruct((T, NP), BF16), jax.ShapeDtypeStruct((T, CONV_DIM), F32),
                   jax.ShapeDtypeStruct((1, N_HEADS), F32), jax.ShapeDtypeStruct((1, N_HEADS), F32),
                   jax.ShapeDtypeStruct((1, N_HEADS), F32), jax.ShapeDtypeStruct((1, D_INNER), F32)],
        scratch_shapes=[pltpu.VMEM((N_GROUPS, D_STATE, GROUP_W), F32)],
        name=name, compiler_params=_cparams(("arbitrary",)),
    )(xbc, xbc, xbc, proj, proj, dtT, dtb, dtbT, alog, alogT, dfull, ng, ypre, hs, dy)


N_PAIRS = ATTN_W // 128
PAIRS_PER_KV = N_PAIRS // 2
ATTN_SCALE = HEAD_DIM ** -0.5


def _kv_variants(kk):
    lo = lax.broadcasted_iota(jnp.int32, kk.shape, 1) < HEAD_DIM
    zero = jnp.zeros_like(kk)
    k00 = jnp.where(lo, kk, zero)
    k11 = jnp.where(lo, zero, kk)
    k01 = pltpu.roll(k00, HEAD_DIM, axis=1)
    k10 = pltpu.roll(k11, HEAD_DIM, axis=1)
    return [[k00.astype(BF16), k01.astype(BF16)], [k10.astype(BF16), k11.astype(BF16)]]


LOG2E = 1.4426950408889634


def _own_block():
    i = lax.broadcasted_iota(jnp.int32, (WINDOW, WINDOW), 0)
    j = lax.broadcasted_iota(jnp.int32, (WINDOW, WINDOW), 1)
    return j <= i


def _fold(own, a):
    return jnp.where(own, a[:, WINDOW:], a[:, :WINDOW])


def _attn_probs(qp, kvar, own, prev_bias, sk):
    s = _dot_nt(qp, kvar)
    sb = jnp.where(own, s[:, WINDOW:], s[:, :WINDOW] + prev_bias) * (ATTN_SCALE * LOG2E)
    sk2 = sk * LOG2E
    m = jnp.maximum(jnp.max(sb, axis=1, keepdims=True), sk2)
    pe = jnp.exp2(sb - m)
    es = jnp.exp2(sk2 - m)
    den = jnp.sum(pe, axis=1, keepdims=True) + es
    inv = 1.0 / den
    return pe * inv, es * inv


def _unfold(own, a):
    zero = jnp.zeros_like(a)
    return jnp.where(own, zero, a), jnp.where(own, a, zero)


def _sink(sinks, r):
    lane = lax.broadcasted_iota(jnp.int32, sinks.shape, 1)
    return jnp.sum(jnp.where(lane == r, sinks, 0.0), axis=1, keepdims=True)


def _kv_specs():
    return [pl.BlockSpec((WINDOW, KV_W), lambda n: (jnp.maximum(n - 1, 0), OFF_K // KV_W)),
            pl.BlockSpec((WINDOW, KV_W), lambda n: (n, OFF_K // KV_W)),
            pl.BlockSpec((WINDOW, KV_W), lambda n: (jnp.maximum(n - 1, 0), OFF_V // KV_W)),
            pl.BlockSpec((WINDOW, KV_W), lambda n: (n, OFF_V // KV_W))]


def _attn_fwd(proj, sinks, og, ycat, name):
    T = proj.shape[0]
    nb = T // WINDOW

    def body(q_ref, kp_ref, kc_ref, vp_ref, vc_ref, s_ref, og_ref, _, y_ref, o_ref, p_ref, ps_ref):
        n = pl.program_id(0)
        kv = _kv_variants(jnp.concatenate([kp_ref[...], kc_ref[...]], axis=0))
        vv = _kv_variants(jnp.concatenate([vp_ref[...], vc_ref[...]], axis=0))
        own = _own_block()
        prev_bias = jnp.where(n > 0, 0.0, NEG)
        sinks_v = s_ref[...]
        lane = lax.broadcasted_iota(jnp.int32, (1, 128), 1)
        ssq = jnp.zeros((WINDOW, 1), F32)
        sink_probs = jnp.zeros((WINDOW, 128), F32)
        for p in range(N_PAIRS):
            j = p // PAIRS_PER_KV
            qp = q_ref[:, p * 128:(p + 1) * 128].astype(BF16)
            o_pair = jnp.zeros((WINDOW, 128), F32)
            for par in range(2):
                r = 2 * p + par
                pn, ps = _attn_probs(qp, kv[j][par], own, prev_bias, _sink(sinks_v, r))
                pb = pn.astype(BF16)
                p_ref[:, r * 128:(r + 1) * 128] = pb
                sink_probs = jnp.where(lane == r, ps, sink_probs)
                p_prev, p_own = _unfold(own, pb)
                o_pair = o_pair + _dot_nn(p_prev, vv[j][par][:WINDOW]) + _dot_nn(p_own, vv[j][par][WINDOW:])
            o_ref[:, p * 128:(p + 1) * 128] = o_pair
            ssq = ssq + jnp.sum(o_pair * o_pair, axis=1, keepdims=True)
        ps_ref[...] = sink_probs
        rn = lax.rsqrt(ssq * (1.0 / ATTN_W) + EPS)
        y_ref[...] = (o_ref[...] * rn * og_ref[...]).astype(BF16)

    return pl.pallas_call(
        body, grid=(nb,),
        in_specs=[pl.BlockSpec((WINDOW, ATTN_W), lambda n: (n, OFF_Q // ATTN_W)), *_kv_specs(),
                  pl.BlockSpec((1, N_HEADS), lambda n: (0, 0)), pl.BlockSpec((1, ATTN_W), lambda n: (0, 0)), ANY],
        out_specs=[pl.BlockSpec((WINDOW, ATTN_W), lambda n: (n, 1)), pl.BlockSpec((WINDOW, ATTN_W), lambda n: (n, 0)),
                   pl.BlockSpec((WINDOW, N_HEADS * 128), lambda n: (n, 0)), pl.BlockSpec((WINDOW, 128), lambda n: (n, 0))],
        out_shape=[jax.ShapeDtypeStruct(ycat.shape, BF16), jax.ShapeDtypeStruct((T, ATTN_W), F32),
                   jax.ShapeDtypeStruct((T, N_HEADS * 128), BF16), jax.ShapeDtypeStruct((T, 128), F32)],
        input_output_aliases={7: 0}, name=name, compiler_params=_cparams(("parallel",)),
    )(proj, proj, proj, proj, proj, sinks, og, ycat)


def _attn_bwd(proj, og, o, dy, probs, sink_probs, dproj, name):
    T = proj.shape[0]
    nb = T // WINDOW

    def body(q_ref, kp_ref, kc_ref, vp_ref, vc_ref, og_ref, o_ref, dy_ref, p_ref, ps_ref, _,
             dq_ref, dk_ref, dv_ref, ds_ref, dog_ref, qt_scr, dot_scr, ds_scr, p_scr):
        n = pl.program_id(0)

        @pl.when(n == 0)
        def _():
            dk_ref[...] = jnp.zeros_like(dk_ref)
            dv_ref[...] = jnp.zeros_like(dv_ref)
            ds_ref[...] = jnp.zeros_like(ds_ref)
            dog_ref[...] = jnp.zeros_like(dog_ref)

        kv = _kv_variants(jnp.concatenate([kp_ref[...], kc_ref[...]], axis=0))
        vv = _kv_variants(jnp.concatenate([vp_ref[...], vc_ref[...]], axis=0))
        own = _own_block()
        sink_probs_v = ps_ref[...]
        of = o_ref[...]
        rn = lax.rsqrt(jnp.mean(of * of, axis=-1, keepdims=True) + EPS)
        oh = of * rn
        dyf = dy_ref[...]
        dog_ref[...] += jnp.sum(dyf * oh, axis=0, keepdims=True)
        doh = dyf * og_ref[...]
        do = rn * (doh - oh * jnp.mean(doh * oh, axis=-1, keepdims=True))
        lane = lax.broadcasted_iota(jnp.int32, (1, 128), 1)
        lane16 = lax.broadcasted_iota(jnp.int32, (1, N_HEADS), 1)
        dsink = jnp.zeros((1, N_HEADS), F32)
        for p in range(N_PAIRS):
            j = p // PAIRS_PER_KV
            q_t = q_ref[:, p * 128:(p + 1) * 128].T.astype(BF16)
            do_p = do[:, p * 128:(p + 1) * 128]
            o_p = of[:, p * 128:(p + 1) * 128]
            do_b = do_p.astype(BF16)
            do_t = do_p.T.astype(BF16)
            prod = do_p * o_p
            dq_pair = jnp.zeros((WINDOW, 128), F32)
            for par in range(2):
                r = 2 * p + par
                half = (lane < HEAD_DIM) if par == 0 else (lane >= HEAD_DIM)
                pb = p_ref[:, r * 128:(r + 1) * 128]
                ps = jnp.sum(jnp.where(lane == r, sink_probs_v, 0.0), axis=1, keepdims=True)
                delta = jnp.sum(jnp.where(half, prod, 0.0), axis=1, keepdims=True)
                dP = _fold(own, _dot_nt(do_b, vv[j][par]))
                dS = pb.astype(F32) * (dP - delta)
                dsink = dsink + jnp.where(lane16 == r, -jnp.sum(ps * delta, axis=0, keepdims=True), 0.0)
                dS_parts = _unfold(own, (dS * ATTN_SCALE).astype(BF16))
                p_parts = _unfold(own, pb)
                at = ((p % PAIRS_PER_KV) * 2 + par) * WINDOW
                qt_scr[j, :, at:at + WINDOW] = q_t[par * HEAD_DIM:(par + 1) * HEAD_DIM]
                dot_scr[j, :, at:at + WINDOW] = do_t[par * HEAD_DIM:(par + 1) * HEAD_DIM]
                for blk in range(2):
                    dq_pair = dq_pair + _dot_nn(dS_parts[blk], kv[j][par][blk * WINDOW:(blk + 1) * WINDOW])
                    ds_scr[j, blk, at:at + WINDOW, :] = dS_parts[blk]
                    p_scr[j, blk, at:at + WINDOW, :] = p_parts[blk]
            dq_ref[:, p * 128:(p + 1) * 128] = dq_pair.astype(BF16)
        rows = [pl.multiple_of(jnp.maximum(n - 1, 0) * WINDOW, WINDOW), pl.multiple_of(n * WINDOW, WINDOW)]
        for lhs, rhs, ref in [(qt_scr, ds_scr, dk_ref), (dot_scr, p_scr, dv_ref)]:
            for blk in range(2):
                both_t = jnp.concatenate([_dot_nn(lhs[j], rhs[j, blk]) for j in range(2)], axis=0)
                ref[pl.ds(rows[blk], WINDOW), :] += both_t.T
        ds_ref[...] += dsink

    full_kv = pl.BlockSpec((T, KV_W), lambda n: (0, 0))
    blk = pl.BlockSpec((WINDOW, ATTN_W), lambda n: (n, 0))
    return pl.pallas_call(
        body, grid=(nb,),
        in_specs=[pl.BlockSpec((WINDOW, ATTN_W), lambda n: (n, OFF_Q // ATTN_W)), *_kv_specs(),
                  pl.BlockSpec((1, ATTN_W), lambda n: (0, 0)), blk, pl.BlockSpec((WINDOW, ATTN_W), lambda n: (n, 1)),
                  pl.BlockSpec((WINDOW, N_HEADS * 128), lambda n: (n, 0)),
                  pl.BlockSpec((WINDOW, 128), lambda n: (n, 0)), ANY],
        out_specs=[pl.BlockSpec((WINDOW, ATTN_W), lambda n: (n, OFF_Q // ATTN_W)), full_kv, full_kv,
                   pl.BlockSpec((1, N_HEADS), lambda n: (0, 0)), pl.BlockSpec((1, ATTN_W), lambda n: (0, 0))],
        out_shape=[jax.ShapeDtypeStruct(dproj.shape, BF16), jax.ShapeDtypeStruct((T, KV_W), F32),
                   jax.ShapeDtypeStruct((T, KV_W), F32), jax.ShapeDtypeStruct((1, N_HEADS), F32),
                   jax.ShapeDtypeStruct((1, ATTN_W), F32)],
        scratch_shapes=[pltpu.VMEM((2, HEAD_DIM, 8 * WINDOW), BF16), pltpu.VMEM((2, HEAD_DIM, 8 * WINDOW), BF16),
                        pltpu.VMEM((2, 2, 8 * WINDOW, WINDOW), BF16), pltpu.VMEM((2, 2, 8 * WINDOW, WINDOW), BF16)],
        input_output_aliases={10: 0}, name=name, compiler_params=_cparams(("arbitrary",)),
    )(proj, proj, proj, proj, proj, og, o, dy, probs, sink_probs, dproj)


ANY = pl.BlockSpec(memory_space=pl.ANY)


def _coords():
    return lax.axis_index("x"), lax.axis_index("y"), lax.axis_index("c")


HBM = pl.BlockSpec(memory_space=pltpu.HBM)
SEM = pl.BlockSpec(memory_space=pltpu.SEMAPHORE)
EFFECT = pltpu.SideEffectType.DATAFLOW_SIDE_EFFECTING


def _in_hbm(a):
    return pltpu.with_memory_space_constraint(a, pltpu.HBM)


def _remote_start(srcs, lands, plan, n_copies, name, after=None):
    ns, nb = len(srcs), len(srcs) + len(lands)
    n_after = 0 if after is None else 1

    def body(*refs):
        src_refs, land_refs = refs[:ns], refs[ns:nb]
        send_sems, recv_sems = refs[nb + n_after], refs[nb + n_after + 1]
        token = refs[-1]
        x, y, c = _coords()
        for i, (sv, dv, dev) in enumerate(plan(src_refs, land_refs, x, y, c)):
            pltpu.make_async_remote_copy(src_ref=sv, dst_ref=dv, send_sem=send_sems.at[i], recv_sem=recv_sems.at[i],
                                         device_id=dev, device_id_type=MESH).start()
        token[...] = jnp.zeros_like(token)

    bufs = list(srcs) + list(lands)
    outs = pl.pallas_call(
        body, name=name,
        out_shape=(pltpu.SemaphoreType.DMA((n_copies,)), pltpu.SemaphoreType.DMA((n_copies,)),
                   *[pltpu.HBM(b.shape, b.dtype) for b in bufs], jax.ShapeDtypeStruct((8, 128), F32)),
        in_specs=[HBM] * nb + [ANY] * n_after,
        out_specs=(SEM, SEM, *[HBM] * nb, pl.BlockSpec(memory_space=pltpu.VMEM)),
        input_output_aliases={i: 2 + i for i in range(nb)},
        compiler_params=pltpu.CompilerParams(has_side_effects=EFFECT),
    )(*[_in_hbm(b) for b in bufs], *([] if after is None else [after]))
    return outs[0], outs[1], list(outs[2:2 + ns]), list(outs[2 + ns:2 + nb]), outs[-1]


def _remote_wait(started, after, plan, name):
    send_sems, recv_sems, srcs, lands, _ = started
    ns, nb = len(srcs), len(srcs) + len(lands)

    def body(*refs):
        src_refs, land_refs = refs[:ns], refs[ns:nb]
        send_sems, recv_sems = refs[nb], refs[nb + 1]
        x, y, c = _coords()
        for i, (sv, dv, dev) in enumerate(plan(src_refs, land_refs, x, y, c)):
            cp = pltpu.make_async_remote_copy(src_ref=sv, dst_ref=dv, send_sem=send_sems.at[i],
                                              recv_sem=recv_sems.at[i], device_id=dev, device_id_type=MESH)
            cp.wait_send()
            cp.wait_recv()

    bufs = list(srcs) + list(lands)
    outs = pl.pallas_call(
        body, name=name, out_shape=tuple(pltpu.HBM(b.shape, b.dtype) for b in bufs),
        in_specs=[HBM] * nb + [SEM, SEM, ANY], out_specs=tuple([HBM] * nb),
        input_output_aliases={i: i for i in range(nb)},
        compiler_params=pltpu.CompilerParams(has_side_effects=EFFECT),
    )(*bufs, send_sems, recv_sems, after)
    return list(outs[:ns]), list(outs[ns:])


def _pair_plan(src_refs, land_refs, x, y, c):
    plan = []
    for s, l in zip(src_refs, land_refs):
        for q in range(4):
            plan.append((s.at[2 * q + (1 - c)], l.at[q], (x, y, 1 - c)))
    return plan


def _pair4_plan(src_refs, land_refs, x, y, c):
    plan = []
    for s, l in zip(src_refs, land_refs):
        for q in range(4):
            plan.append((s.at[q], l.at[q], (x, y, 1 - c)))
    return plan


def _chips_plan(src_refs, land_refs, x, y, c):
    plan = []
    for s, l in zip(src_refs, land_refs):
        for k, (tx, ty) in enumerate([(1 - x, y), (x, 1 - y), (1 - x, 1 - y)]):
            plan.append((s.at[2 * tx + ty], l.at[k], (tx, ty, c)))
    return plan


def _everyone_plan(src_refs, land_refs, x, y, c):
    me = 4 * x + 2 * y + c
    plan = []
    for s, l in zip(src_refs, land_refs):
        for fx, fy, fc in [(0, 0, 1), (1, 0, 0), (1, 0, 1), (0, 1, 0), (0, 1, 1), (1, 1, 0), (1, 1, 1)]:
            dev = ((1 - x) if fx else x, (1 - y) if fy else y, (1 - c) if fc else c)
            plan.append((s, l.at[me], dev))
    return plan


def _pair_add(g8, r1, csel, tr, name):
    _, R, C = r1.shape
    g4 = g8.reshape(4, 2, R, C)

    def body(c_ref, g_ref, r_ref, o_ref):
        o_ref[...] = (g_ref[...].astype(F32) + r_ref[...].astype(F32)).astype(BF16)

    return pl.pallas_call(
        body,
        grid_spec=pltpu.PrefetchScalarGridSpec(
            num_scalar_prefetch=1, grid=(4, R // tr),
            in_specs=[pl.BlockSpec((None, None, tr, C), lambda q, i, cs: (q, cs[0], i, 0)),
                      pl.BlockSpec((None, tr, C), lambda q, i, cs: (q, i, 0))],
            out_specs=pl.BlockSpec((None, tr, C), lambda q, i, cs: (q, i, 0))),
        out_shape=jax.ShapeDtypeStruct((4, R, C), BF16), name=name,
        compiler_params=_cparams(("parallel", "parallel")),
    )(csel, g4, r1)


def _adamw_math(w, g, m, v):
    m = ADAM_B1 * m + (1.0 - ADAM_B1) * g
    v = ADAM_B2 * v + (1.0 - ADAM_B2) * (g * g)
    m_hat = m / (1.0 - ADAM_B1 ** ADAM_STEP)
    v_hat = v / (1.0 - ADAM_B2 ** ADAM_STEP)
    delta = -ADAM_LR * (m_hat / (jnp.sqrt(v_hat) + ADAM_EPS) + ADAM_WD * w)
    return delta, m, v


def _adamw_big(w, m, v, p4, r3, qsel, tile, name):
    R, C = w.shape
    tr, tc = tile

    def body(q_ref, w_ref, m_ref, v_ref, p_ref, r_ref, g_out, d_out, m_out, v_out):
        g = p_ref[...].astype(F32) + r_ref[0].astype(F32) + r_ref[1].astype(F32) + r_ref[2].astype(F32)
        d, mn, vn = _adamw_math(w_ref[...], g, m_ref[...], v_ref[...])
        g_out[...] = g
        d_out[...] = d
        m_out[...] = mn
        v_out[...] = vn

    blk = pl.BlockSpec((tr, tc), lambda i, j, qs: (i, j))
    return pl.pallas_call(
        body,
        grid_spec=pltpu.PrefetchScalarGridSpec(
            num_scalar_prefetch=1, grid=(R // tr, C // tc),
            in_specs=[blk, blk, blk, pl.BlockSpec((None, tr, tc), lambda i, j, qs: (qs[0], i, j)),
                      pl.BlockSpec((3, tr, tc), lambda i, j, qs: (0, i, j))],
            out_specs=[blk, blk, blk, blk]),
        out_shape=[jax.ShapeDtypeStruct((R, C), F32)] * 4, name=name,
        compiler_params=_cparams(("parallel", "parallel")),
    )(qsel, w, m, v, p4, r3)


def _sum_partials(p4, r3, qsel, tc, name):
    _, R, C = p4.shape

    def body(q_ref, p_ref, r_ref, o_ref):
        o_ref[...] = p_ref[...].astype(F32) + r_ref[0].astype(F32) + r_ref[1].astype(F32) + r_ref[2].astype(F32)

    return pl.pallas_call(
        body,
        grid_spec=pltpu.PrefetchScalarGridSpec(
            num_scalar_prefetch=1, grid=(C // tc,),
            in_specs=[pl.BlockSpec((None, R, tc), lambda j, qs: (qs[0], 0, j)),
                      pl.BlockSpec((3, R, tc), lambda j, qs: (0, 0, j))],
            out_specs=pl.BlockSpec((R, tc), lambda j, qs: (0, j))),
        out_shape=jax.ShapeDtypeStruct((R, C), F32), name=name, compiler_params=_cparams(("parallel",)),
    )(qsel, p4, r3)


def _adamw_tiled(w, g, m, v, tc, name):
    R, C = w.shape

    def body(w_ref, g_ref, m_ref, v_ref, d_out, m_out, v_out):
        d, mn, vn = _adamw_math(w_ref[...], g_ref[...], m_ref[...], v_ref[...])
        d_out[...] = d
        m_out[...] = mn
        v_out[...] = vn

    blk = pl.BlockSpec((R, tc), lambda j: (0, j))
    return pl.pallas_call(
        body, grid=(C // tc,), in_specs=[blk] * 4, out_specs=[blk] * 3,
        out_shape=[jax.ShapeDtypeStruct((R, C), F32)] * 3, name=name, compiler_params=_cparams(("parallel",)),
    )(w, g, m, v)


def _small_sum(parts, name):
    def body(p_ref, o_ref):
        acc = p_ref[0]
        for d in range(1, N_DEV):
            acc = acc + p_ref[d]
        o_ref[...] = acc

    return pl.pallas_call(
        body, out_shape=jax.ShapeDtypeStruct(parts.shape[1:], F32), name=name,
        compiler_params=_cparams(),
    )(parts)


def _adamw_small(w, g, m, v, name):
    def body(w_ref, g_ref, m_ref, v_ref, d_out, m_out, v_out):
        d, mn, vn = _adamw_math(w_ref[...], g_ref[...], m_ref[...], v_ref[...])
        d_out[...] = d
        m_out[...] = mn
        v_out[...] = vn

    return pl.pallas_call(
        body, out_shape=[jax.ShapeDtypeStruct(w.shape, F32)] * 3, name=name, compiler_params=_cparams(),
    )(w, g, m, v)


def _row(*pieces):
    r = jnp.concatenate([p.reshape(1, -1) for p in pieces], axis=1)
    return jnp.pad(r, ((0, 0), (0, D_MODEL - r.shape[1])))


def _pack_small(mix, convb, ssmg, attng, mlpg, fing, convw, dtb, alog, dsk, sinks, extra=None):
    last = [dtb, alog, dsk, sinks] + ([extra] if extra is not None else [])
    rows = [_row(mix), _row(convb), _row(ssmg, attng), _row(mlpg), _row(fing),
            jnp.pad(convw, ((0, 0), (0, D_MODEL - convw.shape[1]))), _row(*last)]
    packed = jnp.concatenate(rows, axis=0)
    return jnp.pad(packed, ((0, SMALL_ROWS - packed.shape[0]), (0, 0)))


def _unpack_small(p, conv_n):
    return dict(
        mix_norm_g=p[0:1, :], conv_b=p[1:2, :], ssm_norm_g=p[2:3, :D_INNER], attn_out_norm_g=p[2:3, D_INNER:],
        mlp_norm_g=p[3:4, :], final_norm_g=p[4, :], conv_w=p[5:9, :conv_n][None],
        dt_bias=p[9:10, 0:16], A_log=p[9:10, 16:32], D_skip=p[9:10, 32:48], attn_sinks=p[9:10, 48:64])


WEIGHT_ORDER = ["mix_norm_g", "w_in", "conv_w", "conv_b", "dt_bias", "A_log", "D_skip", "ssm_norm_g", "attn_sinks",
                "attn_out_norm_g", "w_out", "mlp_norm_g", "w_up", "w_down", "final_norm_g"]


def _to_my_columns(w_nat):
    pad = jnp.zeros((w_nat.shape[0], NP - IN_PROJ), w_nat.dtype)
    return jnp.concatenate([w_nat[:, :NAT_DT], w_nat[:, NAT_DT + N_HEADS:], w_nat[:, NAT_DT:NAT_DT + N_HEADS], pad],
                           axis=1)


PER = IN_PROJ // N_DEV
SUPER_STEP = 544
SUPER = 576


def _natural_rows(g, lo, hi):
    segments = [(0, NAT_DT, 0), (NAT_DT, NAT_DT + N_HEADS, OFF_DT - NAT_DT), (NAT_DT + N_HEADS, IN_PROJ, -N_HEADS),
                (IN_PROJ, NP, 0)]
    pieces = [g[max(lo, a) + shift:min(hi, b) + shift] for a, b, shift in segments if max(lo, a) < min(hi, b)]
    return pieces[0] if len(pieces) == 1 else jnp.concatenate(pieces, axis=0)


def _w_in_from_super_slabs(sup):
    seam = SUPER - SUPER_STEP
    units = []
    for i in range(N_DEV):
        base = SUPER_STEP * i
        units.append((base, base + seam, sup[i, :seam] if i == 0 else sup[i - 1, SUPER_STEP:] + sup[i, :seam]))
        units.append((base + seam, base + SUPER_STEP, sup[i, seam:SUPER_STEP]))
    units.append((SUPER_STEP * N_DEV, SUPER_STEP * N_DEV + seam, sup[N_DEV - 1, SUPER_STEP:]))

    def natural(lo, hi):
        return [rows[max(lo, a) - a:min(hi, b) - a] for a, b, rows in units if max(lo, a) < min(hi, b)]

    pieces = natural(0, NAT_DT) + natural(NAT_DT + N_HEADS, IN_PROJ) + natural(NAT_DT, NAT_DT + N_HEADS)
    return jnp.concatenate(pieces + [jnp.zeros((NP - IN_PROJ, D_MODEL), sup.dtype)], axis=0)


def _to_natural_columns(w_my):
    return jnp.concatenate([w_my[:, :NAT_DT], w_my[:, OFF_DT:OFF_DT + N_HEADS], w_my[:, NAT_DT:OFF_DT]], axis=1)


SLAB = 1024


def _grad_w_up(h2, du, name, sel=None, add=None, after=None):
    T, D = h2.shape
    if sel is None:
        pick, n_slab, pre = (lambda j, *cs: j), N_DEV, None
    else:
        pre, other = sel
        pick, n_slab = (lambda j, cs: 2 * j + ((1 - cs[0]) if other else cs[0])), 4
    o_spec = pl.BlockSpec((None, D, SLAB), lambda i, j, k, *cs: (j, 0, 0))
    return _matmul(
        h2, du, mode="tn", grid=(1, n_slab, 1),
        a_spec=pl.BlockSpec((T, D), lambda i, j, k, *cs: (0, 0)),
        b_spec=pl.BlockSpec((T, SLAB), lambda i, j, k, *cs: (0, pick(j, *cs))),
        out_shapes=[jax.ShapeDtypeStruct((n_slab, D, SLAB), BF16)], out_specs=[o_spec], tile=(D, SLAB), name=name,
        extras=() if add is None else (add,), extra_specs=() if add is None else (o_spec,),
        epilogue=None if add is None else (lambda acc, r: (acc + r.astype(F32),)), after=after, prefetch=pre)[0]


def _grad_w_down(act, dx3b, name, sel=None, add=None, after=None):
    T, D = dx3b.shape
    if sel is None:
        pick, n_slab, pre = (lambda i, *cs: i), N_DEV, None
    else:
        pre, other = sel
        pick, n_slab = (lambda i, cs: 2 * i + ((1 - cs[0]) if other else cs[0])), 4
    o_spec = pl.BlockSpec((None, SLAB, D), lambda i, j, k, *cs: (i, 0, 0))
    return _matmul(
        act, dx3b, mode="tn", grid=(n_slab, 1, 1),
        a_spec=pl.BlockSpec((T, SLAB), lambda i, j, k, *cs: (0, pick(i, *cs))),
        b_spec=pl.BlockSpec((T, D), lambda i, j, k, *cs: (0, 0)),
        out_shapes=[jax.ShapeDtypeStruct((n_slab, SLAB, D), BF16)], out_specs=[o_spec], tile=(SLAB, D), name=name,
        extras=() if add is None else (add,), extra_specs=() if add is None else (o_spec,),
        epilogue=None if add is None else (lambda acc, r: (acc + r.astype(F32),)), after=after, prefetch=pre)[0]


class _FixedWeights:
    def __init__(self, w_in_p, w_out_f, w_up_s, w_down_f, conv_w_f):
        self.w = (w_in_p, w_out_f, w_up_s, w_down_f, conv_w_f)
        self.grads = {}

    def mixer_weights(self, after):
        return self.w[0], None

    def conv_weight(self, after):
        return self.w[4]

    def out_weight(self, after):
        return self.w[1]

    def up_weight(self, after):
        return self.w[2]

    def down_weight(self, h, after):
        return self.w[3][:, h * (D_MODEL // 2):(h + 1) * (D_MODEL // 2)]

    def mlp_grads(self, h2, du, act, dx3b):
        self.grads.update(w_up=_grad_w_up(h2, du, "grad_w_up"),
                          w_down=_grad_w_down(act, dx3b, "grad_w_down").reshape(D_FF, D_MODEL))
        return None

    def grad_sent(self, tag, after):
        return None

    def out_grad(self, g_out):
        self.grads.update(w_out=g_out)
        return None

    def in_grad(self, g_in):
        self.grads.update(w_in=g_in)
        return None


def _local_step(x, tgt, p, hooks):
    T = x.shape[0]
    D = D_MODEL
    h1 = _rmsnorm_fwd(x, p["mix_norm_g"], "norm_mix")
    w_in_t, token = hooks.mixer_weights(h1)
    (proj,) = _mm_simple(h1, w_in_t, mode="nt", M=T, N=NP, K=D, tm=min(T, 1024), tn=1536, tk=D, out_dtype=F32,
                         name="in_proj", after=token)
    conv_w_f = hooks.conv_weight(proj)
    xbc, dsilu = _conv_fwd(proj, conv_w_f, p["conv_b"], "conv_fwd")
    dtT = proj[:, OFF_DT:OFF_DT + N_HEADS].T
    dtbT = p["dt_bias"].T
    alogT = p["A_log"].T
    dfull = jnp.repeat(p["D_skip"], HEAD_DIM, axis=1)
    ycat, ypre, hs = _ssd_fwd(xbc, proj, dtT, p["dt_bias"], dtbT, p["A_log"], alogT, dfull, p["ssm_norm_g"],
                              "ssd_fwd")
    ycat, o_att, probs, sink_probs = _attn_fwd(proj, p["attn_sinks"], p["attn_out_norm_g"], ycat, "attn_fwd")
    w_out_f = hooks.out_weight(ycat)
    tm = min(T, 1024)
    def residual_and_norm(acc, res, gain):
        x2 = acc + res
        return x2, x2 * lax.rsqrt(jnp.mean(x2 * x2, axis=-1, keepdims=True) + EPS) * gain

    rows = min(T, 512)
    x2, h2 = _matmul(
        ycat, w_out_f, mode="nn", grid=(T // rows, 1, 1),
        a_spec=pl.BlockSpec((rows, D), lambda i, j, k: (i, 0)), b_spec=pl.BlockSpec((D, D), lambda i, j, k: (0, 0)),
        out_shapes=[jax.ShapeDtypeStruct((T, D), F32), jax.ShapeDtypeStruct((T, D), BF16)],
        out_specs=[pl.BlockSpec((rows, D), lambda i, j, k: (i, 0))] * 2, tile=(rows, D), name="out_proj",
        extras=(x, p["mlp_norm_g"]),
        extra_specs=[pl.BlockSpec((rows, D), lambda i, j, k: (i, 0)), pl.BlockSpec((1, D), lambda i, j, k: (0, 0))],
        epilogue=residual_and_norm)
    w_up_s = hooks.up_weight(h2)
    grid = (T // tm, N_DEV, 1)
    u, act = _matmul(
        h2, w_up_s, mode="nn", grid=grid,
        a_spec=pl.BlockSpec((tm, D), lambda i, j, k: (i, 0)),
        b_spec=pl.BlockSpec((None, D, 1024), lambda i, j, k: (j, 0, 0)),
        out_shapes=[jax.ShapeDtypeStruct((T, D_FF), F32), jax.ShapeDtypeStruct((T, D_FF), BF16)],
        out_specs=[pl.BlockSpec((tm, 1024), lambda i, j, k: (i, j))] * 2, tile=(tm, 1024), name="mlp_up",
        epilogue=lambda acc: (acc, jnp.square(jnp.maximum(acc, 0.0))))
    half = D // 2
    w_down_halves, x3_halves = [], []
    for h in range(2):
        w_down_halves.append(hooks.down_weight(h, act if h == 0 else x3_halves[0]))
        x3_halves.append(_matmul(
            act, w_down_halves[h], mode="nn", grid=(T // tm, 1, D_FF // 2048),
            a_spec=pl.BlockSpec((tm, 2048), lambda i, j, k: (i, k)),
            b_spec=pl.BlockSpec((2048, half), lambda i, j, k: (k, 0)),
            out_shapes=[jax.ShapeDtypeStruct((T, half), F32)],
            out_specs=[pl.BlockSpec((tm, half), lambda i, j, k: (i, 0))], tile=(tm, half), name=f"mlp_down_{h}",
            extras=(x2,), extra_specs=[pl.BlockSpec((tm, half), lambda i, j, k, h=h: (i, h))],
            epilogue=lambda acc, res: (acc + res,))[0])
    loss_part, d_fin, dx3, dx3b = _final_loss(x3_halves, tgt, p["final_norm_g"].reshape(1, D), "loss_head")
    (du,) = _matmul(
        dx3b, tuple(w_down_halves), mode="nt", grid=(T // tm, D_FF // 1024, 1),
        a_spec=pl.BlockSpec((tm, D), lambda i, j, k: (i, 0)),
        b_spec=(pl.BlockSpec((1024, half), lambda i, j, k: (j, 0)),) * 2,
        out_shapes=[jax.ShapeDtypeStruct((T, D_FF), BF16)],
        out_specs=[pl.BlockSpec((tm, 1024), lambda i, j, k: (i, j))], tile=(tm, 1024), name="mlp_down_bwd",
        extras=(u,), extra_specs=[pl.BlockSpec((tm, 1024), lambda i, j, k: (i, j))],
        epilogue=lambda acc, uu: (acc * (2.0 * jnp.maximum(uu, 0.0)),),
        dot_fn=lambda a, b0, b1: _dot_nt(a[:, :half], b0) + _dot_nt(a[:, half:], b1))
    token = hooks.mlp_grads(h2, du, act, dx3b)
    (dh2,) = _matmul(
        du, w_up_s, mode="nt", grid=(T // tm, D // 1024, N_DEV // 2),
        a_spec=pl.BlockSpec((tm, 2048), lambda i, j, k: (i, k)),
        b_spec=pl.BlockSpec((2, 1024, 1024), lambda i, j, k: (k, j, 0)),
        out_shapes=[jax.ShapeDtypeStruct((T, D), F32)],
        out_specs=[pl.BlockSpec((tm, 1024), lambda i, j, k: (i, j))], tile=(tm, 1024), name="mlp_up_bwd",
        after=token, dot_fn=lambda a, b: _dot_nt(a[:, :1024], b[0]) + _dot_nt(a[:, 1024:], b[1]))
    dx2, dx2b, d_mlp = _rmsnorm_bwd(dh2, x2, p["mlp_norm_g"], dx3, "norm_mlp_bwd")
    (g_out,) = _mm_simple(ycat, dx2b, mode="tn", M=D, N=D, K=T, tm=1024, tn=1024, tk=T, out_dtype=BF16,
                          name="grad_w_out")
    token = hooks.out_grad(g_out)
    (dy,) = _mm_simple(dx2b, w_out_f, mode="nt", M=T, N=D, K=D, tm=tm, tn=1024, tk=D, out_dtype=F32,
                       name="out_proj_bwd", after=token)
    token = hooks.grad_sent("out", dy)
    ssm_g = p["ssm_norm_g"] if token is None else p["ssm_norm_g"] + token[0:1, 0:1]
    dproj, dxbc_act, d_dtb, d_alog, d_dskip, d_ssmg = _ssd_bwd(
        xbc, proj, dtT, p["dt_bias"], dtbT, p["A_log"], alogT, dfull, ssm_g, ypre, hs, dy, "ssd_bwd")
    dproj, d_convw, d_convb = _conv_bwd(proj, dxbc_act, dsilu, conv_w_f, dproj, "conv_bwd")
    dproj, dk, dv, d_sinks, d_attng = _attn_bwd(proj, p["attn_out_norm_g"], o_att, dy, probs, sink_probs, dproj,
                                                "attn_bwd")
    dproj = lax.dynamic_update_slice(dproj, jnp.concatenate([dk, dv], axis=1).astype(BF16), (0, OFF_K))
    (g_in,) = _mm_simple(dproj, h1, mode="tn", M=NP, N=D, K=T, tm=1536, tn=1024, tk=T, out_dtype=BF16,
                         name="grad_w_in")
    token = hooks.in_grad(g_in)
    (dh1,) = _mm_simple(dproj, w_in_t, mode="nn", M=T, N=D, K=NP, tm=tm, tn=1024, tk=2304, out_dtype=F32,
                        name="in_proj_bwd", after=token)
    token = hooks.grad_sent("in", dh1)
    mix_g = p["mix_norm_g"] if token is None else p["mix_norm_g"] + token[0:1, 0:1]
    dx, d_mix = _rmsnorm_bwd(dh1, x, mix_g, dx2, "norm_mix_bwd", with_bf16=False)
    small = _pack_small(d_mix, d_convb, d_ssmg, d_attng, d_mlp, d_fin, d_convw, d_dtb, d_alog, d_dskip, d_sinks,
                        extra=loss_part[:, 0:1])
    return dx, small


def _rows_rotated(v, shift, name):
    R, C = v.shape
    tc = 512

    def body(s_ref, v_ref, o_ref):
        o_ref[...] = pltpu.roll(v_ref[...], s_ref[0], axis=0).astype(BF16)

    return pl.pallas_call(
        body,
        grid_spec=pltpu.PrefetchScalarGridSpec(
            num_scalar_prefetch=1, grid=(C // tc,), in_specs=[pl.BlockSpec((R, tc), lambda j, s: (0, j))],
            out_specs=pl.BlockSpec((R, tc), lambda j, s: (0, j))),
        out_shape=jax.ShapeDtypeStruct((R, C), BF16), name=name, compiler_params=_cparams(("parallel",)),
    )(shift, v)


def _landing(own, me):
    zone = lax.empty((N_DEV,) + own.shape, own.dtype)
    return lax.dynamic_update_slice(zone, own[None], (me,) + (0,) * own.ndim)


def _sequencer_gather(owns, split, me, collective_id, name):
    n = len(owns)
    zone_refs = [jax.new_ref(_landing(o, me), memory_space=pltpu.MemorySpace.HBM) for o in owns]
    own_refs = [jax.new_ref(o, memory_space=pltpu.MemorySpace.HBM) for o in owns]
    N_COPIES = 9

    @pl.kernel(mesh=plsc.ScalarSubcoreMesh(axis_name="sequencer", num_cores=1), name=name,
               scratch_types=(pltpu.SemaphoreType.DMA((n, N_COPIES)), pltpu.SemaphoreType.DMA((n, N_COPIES))),
               compiler_params=pltpu.CompilerParams(collective_id=collective_id))
    def launch(send_sems, recv_sems):
        x, y, c = _coords()
        sibling, xn, yn, diag = (x, y, 1 - c), (1 - x, y, c), (x, 1 - y, c), (1 - x, 1 - y, c)
        barrier = pltpu.get_barrier_semaphore()
        for peer in [sibling, xn, yn, diag]:
            pl.semaphore_signal(barrier, inc=1, device_id=peer, device_id_type=MESH)
        pl.semaphore_wait(barrier, 4)

        def block(a, dev, half=None):
            ref = zone_refs[a].at[4 * dev[0] + 2 * dev[1] + dev[2]]
            if half is None:
                return ref
            rows = owns[a].shape[0] // 2
            return ref.at[pl.ds(half * rows, rows)]

        def copy(a, k, src, dst, to):
            return pltpu.make_async_remote_copy(src_ref=src, dst_ref=dst, send_sem=send_sems.at[a, k],
                                                recv_sem=recv_sems.at[a, k], device_id=to, device_id_type=MESH)

        me_dev = (x, y, c)
        sent = []
        first = {}
        for a in range(n):
            for k, peer in enumerate([sibling, xn, yn] + ([] if split[a] else [diag])):
                first[a, k] = copy(a, k, own_refs[a], block(a, me_dev), peer)
                first[a, k].start()
                sent.append(first[a, k])
        from_sibling = []
        for a in range(n):
            first[a, 1].wait_recv()
            sent.append(copy(a, 4, block(a, xn), block(a, xn), sibling))
            if split[a]:
                sent.append(copy(a, 6, block(a, xn, 0), block(a, xn, 0), yn))
            first[a, 2].wait_recv()
            sent.append(copy(a, 5, block(a, yn), block(a, yn), sibling))
            if split[a]:
                sent.append(copy(a, 7, block(a, yn, 1), block(a, yn, 1), xn))
            for cp in sent[-(4 if split[a] else 2):]:
                cp.start()
        for a in range(n):
            if split[a]:
                copy(a, 6, block(a, diag, 0), block(a, diag, 0), yn).wait_recv()
                sent.append(copy(a, 8, block(a, diag, 0), block(a, diag, 0), sibling))
                sent[-1].start()
                copy(a, 7, block(a, diag, 1), block(a, diag, 1), xn).wait_recv()
                sent.append(copy(a, 3, block(a, diag, 1), block(a, diag, 1), sibling))
                sent[-1].start()
            else:
                first[a, 3].wait_recv()
                sent.append(copy(a, 8, block(a, diag), block(a, diag), sibling))
                sent[-1].start()
        for a in range(n):
            first[a, 0].wait_recv()
            copy(a, 4, block(a, xn), block(a, xn), sibling).wait_recv()
            copy(a, 5, block(a, yn), block(a, yn), sibling).wait_recv()
            if split[a]:
                copy(a, 8, block(a, diag, 0), block(a, diag, 0), sibling).wait_recv()
                copy(a, 3, block(a, diag, 1), block(a, diag, 1), sibling).wait_recv()
            else:
                copy(a, 8, block(a, diag), block(a, diag), sibling).wait_recv()
        for cp in sent:
            cp.wait_send()

    launch()
    return zone_refs


class _ShardedWeights:
    def __init__(self, w_in, w_out, conv_w, w_up, w_down, me, csel):
        self.me, self.csel = me, csel
        padded = jnp.pad(jnp.transpose(w_in), ((0, SUPER - PER), (0, 0)))
        own_rows = _rows_rotated(padded, jnp.reshape(2 * me, (1,)).astype(jnp.int32), "w_in_super_slab")
        (self.in_ref,) = _sequencer_gather([own_rows], [True], me, 7, "gather_w_in_sequencer")
        self.out_ref, self.conv_ref = _sequencer_gather([w_out.astype(BF16), conv_w], [True, False], me, 8,
                                                        "gather_w_out_sequencer")
        (self.up_ref,) = _sequencer_gather([w_up.astype(BF16)], [True], me, 9, "gather_w_up_sequencer")
        down = w_down.astype(BF16)
        self.down_refs = [_sequencer_gather([down[:, h * (D_MODEL // 2):(h + 1) * (D_MODEL // 2)]], [True], me, 10 + h,
                                            f"gather_w_down_{h}_sequencer")[0] for h in range(2)]
        self.reduces = {}
        self.pairs = {}

    def mixer_weights(self, after):
        return _w_in_from_super_slabs(self.in_ref[...]), None

    def conv_weight(self, after):
        g_conv = self.conv_ref[...]
        return jnp.concatenate([g_conv[i] for i in range(N_DEV)], axis=1)

    def out_weight(self, after):
        return self.out_ref[...].reshape(D_MODEL, D_MODEL)

    def up_weight(self, after):
        return self.up_ref[...]

    def down_weight(self, h, after):
        return self.down_refs[h][...].reshape(D_FF, D_MODEL // 2)

    def _chips_start(self, slabs, from_sibling, rows, tag):
        sums = [_pair_add(s, r, self.csel, tr, f"pair_add_{tag}_{i}")
                for i, (s, r, tr) in enumerate(zip(slabs, from_sibling, rows))]
        lands = [lax.empty((3,) + s.shape[1:], s.dtype) for s in sums]
        self.reduces[tag] = _remote_start(sums, lands, _chips_plan, 3 * len(sums), f"reduce_start_{tag}")
        return self.reduces[tag][4]

    def mlp_grads(self, h2, du, act, dx3b):
        def send(part, tag, after):
            st = _remote_start([part], [lax.empty(part.shape, part.dtype)], _pair4_plan, 4,
                               f"reduce_pair_start_{tag}", after=after)
            return st

        def received(st, after, tag):
            return _remote_wait(st, after, _pair4_plan, f"reduce_pair_wait_{tag}")[1][0]

        def to_chips(sums, tag):
            self.reduces[tag] = _remote_start([sums], [lax.empty((3,) + sums.shape[1:], sums.dtype)], _chips_plan, 3,
                                              f"reduce_start_{tag}")
            return self.reduces[tag][4]

        up_send = _grad_w_up(h2, du, "grad_w_up_send", sel=(self.csel, True))
        st_up = send(up_send, "up", None)
        down_send = _grad_w_down(act, dx3b, "grad_w_down_send", sel=(self.csel, True), after=st_up[4])
        st_down = send(down_send, "down", None)
        up_sum = _grad_w_up(h2, du, "grad_w_up_keep", sel=(self.csel, False), add=received(st_up, down_send, "up"),
                            after=st_down[4])
        token = to_chips(up_sum, "up")
        down_sum = _grad_w_down(act, dx3b, "grad_w_down_keep", sel=(self.csel, False),
                                add=received(st_down, up_sum, "down"), after=token)
        return to_chips(down_sum, "down")

    def _pair_start(self, slabs, tag):
        land = lax.empty((4,) + slabs.shape[1:], slabs.dtype)
        self.pairs[tag] = _remote_start([slabs], [land], _pair_plan, 4, f"reduce_pair_start_{tag}")
        return self.pairs[tag][4]

    def grad_sent(self, tag, after):
        slabs, from_sibling = _remote_wait(self.pairs[tag], after, _pair_plan, f"reduce_pair_wait_{tag}")
        return self._chips_start(slabs, from_sibling, [slabs[0].shape[1]], tag)

    def out_grad(self, g_out):
        return self._pair_start(g_out.reshape(N_DEV, D_MODEL // N_DEV, D_MODEL), "out")

    def in_grad(self, g_in):
        return self._pair_start(
            jnp.stack([_natural_rows(g_in, SUPER_STEP * j, SUPER_STEP * j + SUPER) for j in range(N_DEV)]), "in")

    def small_start(self, small):
        self.st_small = _remote_start([small], [_landing(small, self.me)], _everyone_plan, N_DEV - 1, "gather_start_small")

    def small_end(self, after):
        return _remote_wait(self.st_small, after, _everyone_plan, "gather_small_wait")[1][0]

    def reduce_end(self, tag, after):
        return _remote_wait(self.reduces[tag], after, _chips_plan, f"reduce_wait_{tag}")


def kernel(x, mix_norm_g, w_in, conv_w, conv_b, dt_bias, A_log, D_skip, ssm_norm_g, attn_sinks, attn_out_norm_g, w_out, mlp_norm_g, w_up, w_down, final_norm_g, loss_target, m_mix_norm_g, m_w_in, m_conv_w, m_conv_b, m_dt_bias, m_A_log, m_D_skip, m_ssm_norm_g, m_attn_sinks, m_attn_out_norm_g, m_w_out, m_mlp_norm_g, m_w_up, m_w_down, m_final_norm_g, v_mix_norm_g, v_w_in, v_conv_w, v_conv_b, v_dt_bias, v_A_log, v_D_skip, v_ssm_norm_g, v_attn_sinks, v_attn_out_norm_g, v_w_out, v_mlp_norm_g, v_w_up, v_w_down, v_final_norm_g):
    xi, yi, ci = _coords()
    me = 4 * xi + 2 * yi + ci
    csel = jnp.reshape(ci, (1,)).astype(jnp.int32)
    qsel = jnp.reshape(2 * xi + yi, (1,)).astype(jnp.int32)
    w = dict(mix_norm_g=mix_norm_g, conv_b=conv_b, dt_bias=dt_bias, A_log=A_log, D_skip=D_skip,
             ssm_norm_g=ssm_norm_g, attn_sinks=attn_sinks, attn_out_norm_g=attn_out_norm_g, mlp_norm_g=mlp_norm_g,
             final_norm_g=final_norm_g)
    hooks = _ShardedWeights(w_in[0], w_out[0], conv_w[0], w_up[0], w_down[0], me, csel)
    p = dict(w)
    dx, small = _local_step(x[0], loss_target[0], p, hooks)
    hooks.small_start(small)
    big = {}
    after = dx
    for name, wt, mt, vt, tile in [
            ("up", w_up, m_w_up, v_w_up, (512, SLAB)), ("down", w_down, m_w_down, v_w_down, (256, D_MODEL)),
            ("out", w_out, m_w_out, v_w_out, (256, D_MODEL))]:
        (chip_sums,), (from_chips,) = hooks.reduce_end(name, after)
        res = _adamw_big(wt[0], mt[0], vt[0], chip_sums, from_chips, qsel, tile, f"adamw_w_{name}")
        big["w_" + name] = tuple(r[None] for r in res)
        after = res[0]
    (chip_sums,), (from_chips,) = hooks.reduce_end("in", after)
    g_super = _sum_partials(chip_sums, from_chips, qsel, 512, "grad_w_in_sum")
    g_in = lax.dynamic_slice(g_super, (2 * me, 0), (PER, D_MODEL))
    res = _adamw_tiled(jnp.transpose(w_in[0]), g_in, jnp.transpose(m_w_in[0]), jnp.transpose(v_w_in[0]), 512,
                       "adamw_w_in")
    big["w_in"] = tuple(jnp.transpose(r)[None] for r in (g_in, *res))
    after = res[0]
    gsum = _small_sum(hooks.small_end(after), "small_sum")
    loss = gsum[9, 64]
    gs = _unpack_small(gsum, CONV_DIM)
    cw = CONV_DIM // N_DEV
    g_conv_shard = lax.dynamic_slice(gsum[5:9, :], (0, me * cw), (CONV_K, cw))

    def pack(s):
        return _pack_small(s["mix_norm_g"], s["conv_b"], s["ssm_norm_g"], s["attn_out_norm_g"], s["mlp_norm_g"],
                           s["final_norm_g"], s["conv_w"][0], s["dt_bias"], s["A_log"], s["D_skip"], s["attn_sinks"])

    wp = pack(dict(w, conv_w=conv_w))
    mp = pack(dict(mix_norm_g=m_mix_norm_g, conv_b=m_conv_b, ssm_norm_g=m_ssm_norm_g,
                   attn_out_norm_g=m_attn_out_norm_g, mlp_norm_g=m_mlp_norm_g, final_norm_g=m_final_norm_g,
                   conv_w=m_conv_w, dt_bias=m_dt_bias, A_log=m_A_log, D_skip=m_D_skip, attn_sinks=m_attn_sinks))
    vp = pack(dict(mix_norm_g=v_mix_norm_g, conv_b=v_conv_b, ssm_norm_g=v_ssm_norm_g,
                   attn_out_norm_g=v_attn_out_norm_g, mlp_norm_g=v_mlp_norm_g, final_norm_g=v_final_norm_g,
                   conv_w=v_conv_w, dt_bias=v_dt_bias, A_log=v_A_log, D_skip=v_D_skip, attn_sinks=v_attn_sinks))
    gp = jnp.concatenate([gsum[0:5], jnp.pad(g_conv_shard, ((0, 0), (0, D_MODEL - cw))), gsum[9:10],
                          jnp.zeros((SMALL_ROWS - 10, D_MODEL), F32)], axis=0)
    dp, mnp, vnp = _adamw_small(wp, gp, mp, vp, "adamw_small")
    grads = dict(gs, conv_w=g_conv_shard[None])
    deltas = _unpack_small(dp, cw)
    new_m = _unpack_small(mnp, cw)
    new_v = _unpack_small(vnp, cw)
    for k, name in enumerate(["w_in", "w_out", "w_up", "w_down"]):
        grads[name], deltas[name], new_m[name], new_v[name] = big[name]
    return (loss, dx[None], *[grads[n] for n in WEIGHT_ORDER], *[deltas[n] for n in WEIGHT_ORDER],
            *[new_m[n] for n in WEIGHT_ORDER], *[new_v[n] for n in WEIGHT_ORDER])
```

```python
import jax
import jax.numpy as jnp
from jax import lax
from jax.experimental import pallas as pl
from jax.experimental.pallas import tpu as pltpu
from jax.experimental.pallas import tpu_sc as plsc

F32 = jnp.float32
BF16 = jnp.bfloat16
MESH = pl.DeviceIdType.MESH

EPS = 1e-5
D_MODEL = 2048
D_INNER = 1024
N_HEADS = 16
HEAD_DIM = 64
N_GROUPS = 4
D_STATE = 128
CHUNK = 128
CONV_K = 4
CONV_DIM = 2048
ATTN_W = 1024
KV_W = 128
WINDOW = 128
D_FF = 8192
IN_PROJ = 4368
N_DEV = 8
NP = 4608
OFF_Z, OFF_X, OFF_B, OFF_C, OFF_Q, OFF_K, OFF_V, OFF_DT = 0, 1024, 2048, 2560, 3072, 4096, 4224, 4352
NAT_DT = 3072

ADAM_LR = 0.001
ADAM_B1 = 0.9
ADAM_B2 = 0.999
ADAM_EPS = 1e-08
ADAM_WD = 0.01
ADAM_STEP = 10

VMEM_LIMIT = 52 * 1024 * 1024
SMALL_ROWS = 16
NEG = -1e30


def _cparams(sem=None):
    return pltpu.CompilerParams(dimension_semantics=sem, vmem_limit_bytes=VMEM_LIMIT)


def _split3(v):
    hi = v.astype(BF16)
    rest = v - hi.astype(F32)
    mid = rest.astype(BF16)
    return hi, mid, (rest - mid.astype(F32)).astype(BF16)


def _hdot(a, b, data):
    if data == "a":
        sel = b.astype(BF16)
        return sum(_dot_nn(part, sel) for part in _split3(a))
    sel = a.astype(BF16)
    return sum(_dot_nn(sel, part) for part in _split3(b))


def _dot_nn(a, b):
    return lax.dot_general(a, b, (((1,), (0,)), ((), ())), preferred_element_type=F32)


def _dot_nt(a, b):
    return lax.dot_general(a, b, (((1,), (1,)), ((), ())), preferred_element_type=F32)


def _dot_tn(a, b):
    return lax.dot_general(a, b, (((0,), (0,)), ((), ())), preferred_element_type=F32)


def _softplus(v):
    return jnp.maximum(v, 0.0) + jnp.log1p(jnp.exp(-jnp.abs(v)))


def _sigmoid(v):
    return 1.0 / (1.0 + jnp.exp(-v))


def _matmul(a, b, *, mode, grid, a_spec, b_spec, out_shapes, out_specs, tile, name,
            extras=(), extra_specs=(), epilogue=None, after=None, dot_fn=None, prefetch=None):
    nk = grid[2]
    n_ex = len(extras)
    n_out = len(out_shapes)
    bs, b_specs = (b, b_spec) if isinstance(b, tuple) else ((b,), (b_spec,))
    n_in = 1 + len(bs)
    dot = dot_fn if dot_fn is not None else {"nn": _dot_nn, "nt": _dot_nt, "tn": _dot_tn}[mode]

    def finish(acc, ex_refs, out_refs):
        res = (acc,) if epilogue is None else epilogue(acc, *[e[...] for e in ex_refs])
        for o, r in zip(out_refs, res):
            o[...] = r.astype(o.dtype)

    def body(*refs):
        ex_refs = refs[n_in:n_in + n_ex]
        out_refs = refs[n_in + n_ex:n_in + n_ex + n_out]
        part = dot(*[r[...].astype(BF16) for r in refs[:n_in]])
        if nk == 1:
            finish(part, ex_refs, out_refs)
        else:
            acc_ref = refs[-1]
            k = pl.program_id(2)

            @pl.when(k == 0)
            def _():
                acc_ref[...] = part

            @pl.when(k > 0)
            def _():
                acc_ref[...] += part

            @pl.when(k == nk - 1)
            def _():
                finish(acc_ref[...], ex_refs, out_refs)

    scratch = [] if nk == 1 else [pltpu.VMEM(tile, F32)]
    n_pre = 0 if prefetch is None else 1
    tok_specs = [] if after is None else [pl.BlockSpec((8, 128), lambda *_: (0, 0))]
    tok_args = [] if after is None else [after]

    def body_with_token(*refs):
        refs = refs[n_pre:]
        body(*refs[:n_in + n_ex], *refs[n_in + n_ex + len(tok_args):])

    in_specs = [a_spec, *b_specs, *extra_specs, *tok_specs]
    params = _cparams(("parallel", "parallel", "arbitrary"))
    if prefetch is None:
        return pl.pallas_call(
            body_with_token, grid=grid, in_specs=in_specs, out_specs=list(out_specs), out_shape=list(out_shapes),
            scratch_shapes=scratch, name=name, compiler_params=params)(a, *bs, *extras, *tok_args)
    return pl.pallas_call(
        body_with_token,
        grid_spec=pltpu.PrefetchScalarGridSpec(num_scalar_prefetch=1, grid=grid, in_specs=in_specs,
                                               out_specs=list(out_specs), scratch_shapes=scratch),
        out_shape=list(out_shapes), name=name, compiler_params=params)(prefetch, a, *bs, *extras, *tok_args)


def _mm_simple(a, b, *, mode, M, N, K, tm, tn, tk, out_dtype, name, extras=(), epilogue=None, n_out=1,
               out_dtypes=None, after=None):
    grid = (M // tm, N // tn, K // tk)
    if mode == "nn":
        a_spec = pl.BlockSpec((tm, tk), lambda i, j, k: (i, k))
        b_spec = pl.BlockSpec((tk, tn), lambda i, j, k: (k, j))
    elif mode == "nt":
        a_spec = pl.BlockSpec((tm, tk), lambda i, j, k: (i, k))
        b_spec = pl.BlockSpec((tn, tk), lambda i, j, k: (j, k))
    else:
        a_spec = pl.BlockSpec((tk, tm), lambda i, j, k: (k, i))
        b_spec = pl.BlockSpec((tk, tn), lambda i, j, k: (k, j))
    o_spec = pl.BlockSpec((tm, tn), lambda i, j, k: (i, j))
    dts = out_dtypes if out_dtypes is not None else [out_dtype] * n_out
    return _matmul(a, b, mode=mode, grid=grid, a_spec=a_spec, b_spec=b_spec,
                   out_shapes=[jax.ShapeDtypeStruct((M, N), d) for d in dts],
                   out_specs=[o_spec] * len(dts), tile=(tm, tn), name=name,
                   extras=extras, extra_specs=[o_spec] * len(extras), epilogue=epilogue, after=after)


ROW_BLOCK = 256


def _rmsnorm_fwd(x, g, name):
    T, D = x.shape

    def body(x_ref, g_ref, o_ref):
        xf = x_ref[...]
        r = lax.rsqrt(jnp.mean(xf * xf, axis=-1, keepdims=True) + EPS)
        o_ref[...] = (xf * r * g_ref[...]).astype(BF16)

    return pl.pallas_call(
        body, grid=(T // ROW_BLOCK,),
        in_specs=[pl.BlockSpec((ROW_BLOCK, D), lambda i: (i, 0)), pl.BlockSpec((1, D), lambda i: (0, 0))],
        out_specs=pl.BlockSpec((ROW_BLOCK, D), lambda i: (i, 0)),
        out_shape=jax.ShapeDtypeStruct((T, D), BF16), name=name, compiler_params=_cparams(("parallel",)),
    )(x, g)


def _rmsnorm_bwd(dh, x, g, dres, name, with_bf16=True):
    T, D = x.shape

    def body(dh_ref, x_ref, g_ref, dres_ref, dx_ref, *rest):
        dg_ref = rest[-1]
        i = pl.program_id(0)

        @pl.when(i == 0)
        def _():
            dg_ref[...] = jnp.zeros_like(dg_ref)

        g = g_ref[...]

        def rows(s, dg):
            base = pl.multiple_of(s * 16, 16)
            for half in range(2):
                sl = pl.ds(base + 8 * half, 8)
                xf = x_ref[sl, :]
                r = lax.rsqrt(jnp.mean(xf * xf, axis=-1, keepdims=True) + EPS)
                xh = xf * r
                d = dh_ref[sl, :]
                dg = dg + d * xh
                dxh = d * g
                dx_ref[sl, :] = r * (dxh - xh * jnp.mean(dxh * xh, axis=-1, keepdims=True)) + dres_ref[sl, :]
            if with_bf16:
                rest[0][pl.ds(base, 16), :] = dx_ref[pl.ds(base, 16), :].astype(BF16)
            return dg

        dg = lax.fori_loop(0, ROW_BLOCK // 16, rows, jnp.zeros((8, D), F32))
        dg_ref[...] += jnp.sum(dg, axis=0, keepdims=True)

    row = pl.BlockSpec((ROW_BLOCK, D), lambda i: (i, 0))
    vec = pl.BlockSpec((1, D), lambda i: (0, 0))
    copies = [(row, jax.ShapeDtypeStruct((T, D), BF16))] if with_bf16 else []
    return pl.pallas_call(
        body, grid=(T // ROW_BLOCK,), in_specs=[row, row, vec, row],
        out_specs=[row, *[c[0] for c in copies], vec],
        out_shape=[jax.ShapeDtypeStruct((T, D), F32), *[c[1] for c in copies], jax.ShapeDtypeStruct((1, D), F32)],
        name=name, compiler_params=_cparams(("arbitrary",)),
    )(dh, x, g, dres)


def _final_loss(x3_halves, tgt, g, name):
    T, D = tgt.shape

    def body(xa_ref, xb_ref, t_ref, g_ref, loss_ref, dg_ref, dx_ref, dxb_ref):
        i = pl.program_id(0)
        xf = jnp.concatenate([xa_ref[...], xb_ref[...]], axis=1)
        r = lax.rsqrt(jnp.mean(xf * xf, axis=-1, keepdims=True) + EPS)
        xh = xf * r
        gg = g_ref[...]
        err = xh * gg - t_ref[...]

        @pl.when(i == 0)
        def _():
            dg_ref[...] = jnp.zeros_like(dg_ref)
            loss_ref[...] = jnp.zeros_like(loss_ref)

        part = jnp.sum(jnp.sum(err * err, axis=-1, keepdims=True), axis=0, keepdims=True) * (0.5 / D)
        loss_ref[...] += jnp.broadcast_to(part, loss_ref.shape)
        dout = err * (1.0 / D)
        dg_ref[...] += jnp.sum(dout * xh, axis=0, keepdims=True)
        dxh = dout * gg
        dx = r * (dxh - xh * jnp.mean(dxh * xh, axis=-1, keepdims=True))
        dx_ref[...] = dx
        dxb_ref[...] = dx.astype(BF16)

    row = pl.BlockSpec((ROW_BLOCK, D), lambda i: (i, 0))
    vec = pl.BlockSpec((1, D), lambda i: (0, 0))
    return pl.pallas_call(
        body, grid=(T // ROW_BLOCK,),
        in_specs=[pl.BlockSpec((ROW_BLOCK, D // 2), lambda i: (i, 0))] * 2 + [row, vec],
        out_specs=[pl.BlockSpec((1, 128), lambda i: (0, 0)), vec, row, row],
        out_shape=[jax.ShapeDtypeStruct((1, 128), F32), jax.ShapeDtypeStruct((1, D), F32),
                   jax.ShapeDtypeStruct((T, D), F32), jax.ShapeDtypeStruct((T, D), BF16)],
        name=name, compiler_params=_cparams(("arbitrary",)),
    )(*x3_halves, tgt, g)


CONV_BLOCK = 256


def _conv_apply(u, w, b):
    row = lax.broadcasted_iota(jnp.int32, u.shape, 0)
    acc = b + w[CONV_K - 1:CONV_K, :] * u
    shifted = []
    for j in range(1, CONV_K):
        uj = jnp.where(row >= j, pltpu.roll(u, j, axis=0), 0.0)
        shifted.append(uj)
        acc = acc + w[CONV_K - 1 - j:CONV_K - j, :] * uj
    return acc, shifted


def _conv_fwd(proj, conv_w, conv_b, name):
    T = proj.shape[0]
    cb0 = OFF_X // CONV_BLOCK

    def body(u_ref, w_ref, b_ref, o_ref, ds_ref):
        c, _ = _conv_apply(u_ref[...], w_ref[...], b_ref[...])
        sg = _sigmoid(c)
        o_ref[...] = c * sg
        ds_ref[...] = sg * (1.0 + c * (1.0 - sg))

    out = pl.BlockSpec((T, CONV_BLOCK), lambda j: (0, j))
    return pl.pallas_call(
        body, grid=(CONV_DIM // CONV_BLOCK,),
        in_specs=[pl.BlockSpec((T, CONV_BLOCK), lambda j: (0, cb0 + j)),
                  pl.BlockSpec((CONV_K, CONV_BLOCK), lambda j: (0, j)),
                  pl.BlockSpec((1, CONV_BLOCK), lambda j: (0, j))],
        out_specs=[out, out], out_shape=[jax.ShapeDtypeStruct((T, CONV_DIM), F32)] * 2,
        name=name, compiler_params=_cparams(("parallel",)),
    )(proj, conv_w, conv_b)


def _conv_bwd(proj, dact, dsilu, conv_w, dproj, name):
    T = proj.shape[0]
    cb0 = OFF_X // CONV_BLOCK

    def body(u_ref, d_ref, s_ref, w_ref, _, du_ref, dw_ref, db_ref):
        u = u_ref[...]
        w = w_ref[...]
        dc = d_ref[...] * s_ref[...]
        row = lax.broadcasted_iota(jnp.int32, u.shape, 0)
        du = w[CONV_K - 1:CONV_K, :] * dc
        dw_ref[CONV_K - 1:CONV_K, :] = jnp.sum(dc * u, axis=0, keepdims=True)
        for j in range(1, CONV_K):
            dcj = jnp.where(row < T - j, pltpu.roll(dc, T - j, axis=0), 0.0)
            du = du + w[CONV_K - 1 - j:CONV_K - j, :] * dcj
            dw_ref[CONV_K - 1 - j:CONV_K - j, :] = jnp.sum(dcj * u, axis=0, keepdims=True)
        db_ref[...] = jnp.sum(dc, axis=0, keepdims=True)
        du_ref[...] = du.astype(BF16)

    blk = pl.BlockSpec((T, CONV_BLOCK), lambda j: (0, j))
    return pl.pallas_call(
        body, grid=(CONV_DIM // CONV_BLOCK,),
        in_specs=[pl.BlockSpec((T, CONV_BLOCK), lambda j: (0, cb0 + j)), blk, blk,
                  pl.BlockSpec((CONV_K, CONV_BLOCK), lambda j: (0, j)), pl.BlockSpec(memory_space=pl.ANY)],
        out_specs=[pl.BlockSpec((T, CONV_BLOCK), lambda j: (0, cb0 + j)),
                   pl.BlockSpec((CONV_K, CONV_BLOCK), lambda j: (0, j)),
                   pl.BlockSpec((1, CONV_BLOCK), lambda j: (0, j))],
        out_shape=[jax.ShapeDtypeStruct(dproj.shape, BF16), jax.ShapeDtypeStruct((CONV_K, CONV_DIM), F32),
                   jax.ShapeDtypeStruct((1, CONV_DIM), F32)],
        input_output_aliases={4: 0}, name=name, compiler_params=_cparams(("parallel",)),
    )(proj, dact, dsilu, conv_w, dproj)


GROUP_W = D_INNER // N_GROUPS
HEADS_PER_GROUP = N_HEADS // N_GROUPS


def _expand_mat():
    h = lax.broadcasted_iota(jnp.int32, (N_HEADS, D_INNER), 0)
    j = lax.broadcasted_iota(jnp.int32, (N_HEADS, D_INNER), 1)
    return (j // HEAD_DIM == h).astype(F32)


def _reduce_mat(g):
    j = lax.broadcasted_iota(jnp.int32, (GROUP_W, N_HEADS), 0)
    h = lax.broadcasted_iota(jnp.int32, (GROUP_W, N_HEADS), 1)
    return (g * HEADS_PER_GROUP + j // HEAD_DIM == h).astype(F32)


def _col16(v, h):
    lane = lax.broadcasted_iota(jnp.int32, v.shape, 1)
    return jnp.sum(jnp.where(lane == h, v, 0.0), axis=1, keepdims=True)


def _ssd_pre(dt_raw, dtT_raw, dtb, dtbT, alog, alogT):
    Q = CHUNK
    xdt = dt_raw + dtb
    dt = _softplus(xdt)
    dtT = _softplus(dtT_raw + dtbT)
    A = -jnp.exp(alog)
    AT = -jnp.exp(alogT)
    row = lax.broadcasted_iota(jnp.int32, (Q, Q), 0)
    col = lax.broadcasted_iota(jnp.int32, (Q, Q), 1)
    tril = (row >= col).astype(F32)
    triu = (row <= col).astype(F32)
    cs = _hdot(tril, dt * A, "b")
    csT = _hdot(dtT * AT, triu, "a")
    return xdt, dt, A, cs, csT, row >= col, triu


def _decay_matrix(cs, csT, h, causal):
    seg = _col16(cs, h) - csT[h:h + 1, :]
    return jnp.where(causal, jnp.exp(jnp.minimum(seg, 0.0)), 0.0)


def _ssd_in_specs(nc, rev):
    def cidx(c):
        return (nc - 1 - c) if rev else c

    return [
        pl.BlockSpec((CHUNK, D_INNER), lambda c: (cidx(c), 0)),
        pl.BlockSpec((CHUNK, 512), lambda c: (cidx(c), 2)),
        pl.BlockSpec((CHUNK, 512), lambda c: (cidx(c), 3)),
        pl.BlockSpec((CHUNK, D_INNER), lambda c: (cidx(c), 0)),
        pl.BlockSpec((CHUNK, 128), lambda c: (cidx(c), OFF_DT // 128)),
        pl.BlockSpec((N_HEADS, CHUNK), lambda c: (0, cidx(c))),
        pl.BlockSpec((1, N_HEADS), lambda c: (0, 0)),
        pl.BlockSpec((N_HEADS, 1), lambda c: (0, 0)),
        pl.BlockSpec((1, N_HEADS), lambda c: (0, 0)),
        pl.BlockSpec((N_HEADS, 1), lambda c: (0, 0)),
        pl.BlockSpec((1, D_INNER), lambda c: (0, 0)),
        pl.BlockSpec((1, D_INNER), lambda c: (0, 0)),
    ]


def _ssd_fwd(xbc, proj, dtT, dtb, dtbT, alog, alogT, dfull, ng, name):
    T = xbc.shape[0]
    nc = T // CHUNK
    Q = CHUNK

    def body(xs_ref, B_ref, C_ref, z_ref, dt_ref, dtT_ref, dtb_ref, dtbT_ref, al_ref, alT_ref, df_ref, ng_ref,
             y_ref, ypre_ref, hs_ref, h_scr):
        c = pl.program_id(0)

        @pl.when(c == 0)
        def _():
            h_scr[...] = jnp.zeros_like(h_scr)

        _, dt, _, cs, csT, causal, _ = _ssd_pre(dt_ref[:, :N_HEADS], dtT_ref[...], dtb_ref[...], dtbT_ref[...],
                                                al_ref[...], alT_ref[...])
        ex = _expand_mat()
        dt_full = _hdot(dt, ex, "a")
        cs_full = _hdot(cs, ex, "a")
        cs_last = cs_full[Q - 1:Q, :]
        xs = xs_ref[...]
        xd = xs * dt_full
        e_full = jnp.exp(cs_full)
        dec_full = jnp.exp(cs_last - cs_full)
        cd_full = jnp.exp(cs_last)
        lane_head = lax.broadcasted_iota(jnp.int32, (1, GROUP_W), 1) // HEAD_DIM
        for g in range(N_GROUPS):
            sl = slice(g * GROUP_W, (g + 1) * GROUP_W)
            Bg = B_ref[:, g * D_STATE:(g + 1) * D_STATE].astype(BF16)
            Cg = C_ref[:, g * D_STATE:(g + 1) * D_STATE].astype(BF16)
            CB = _dot_nt(Cg, Bg)
            hg = h_scr[g]
            yoff = _dot_nn(Cg, hg.astype(BF16)) * e_full[:, sl]
            xd_g = xd[:, sl]
            S = _dot_tn(Bg, (xd_g * dec_full[:, sl]).astype(BF16))
            xd_b = xd_g.astype(BF16)
            ydiag = jnp.zeros((Q, GROUP_W), F32)
            for r in range(HEADS_PER_GROUP):
                Lm = _decay_matrix(cs, csT, g * HEADS_PER_GROUP + r, causal)
                Gm = (CB * Lm).astype(BF16)
                ydiag = ydiag + _dot_nn(Gm, jnp.where(lane_head == r, xd_b, jnp.zeros_like(xd_b)))
            hs_ref[0, g] = hg
            h_scr[g] = hg * cd_full[:, sl] + S
            ypre = ydiag + yoff + xs[:, sl] * df_ref[:, sl]
            ypre_ref[:, sl] = ypre
            zg = z_ref[:, sl]
            yz = ypre * zg * _sigmoid(zg)
            rn = lax.rsqrt(jnp.mean(yz * yz, axis=-1, keepdims=True) + EPS)
            y_ref[:, sl] = (yz * rn * ng_ref[:, sl]).astype(BF16)

    return pl.pallas_call(
        body, grid=(nc,), in_specs=_ssd_in_specs(nc, False),
        out_specs=[pl.BlockSpec((CHUNK, D_INNER), lambda c: (c, 0)),
                   pl.BlockSpec((CHUNK, D_INNER), lambda c: (c, 0)),
                   pl.BlockSpec((1, N_GROUPS, D_STATE, GROUP_W), lambda c: (c, 0, 0, 0))],
        out_shape=[jax.ShapeDtypeStruct((T, D_INNER + ATTN_W), BF16), jax.ShapeDtypeStruct((T, D_INNER), F32),
                   jax.ShapeDtypeStruct((nc, N_GROUPS, D_STATE, GROUP_W), F32)],
        scratch_shapes=[pltpu.VMEM((N_GROUPS, D_STATE, GROUP_W), F32)],
        name=name, compiler_params=_cparams(("arbitrary",)),
    )(xbc, xbc, xbc, proj, proj, dtT, dtb, dtbT, alog, alogT, dfull, ng)


def _ssd_bwd(xbc, proj, dtT, dtb, dtbT, alog, alogT, dfull, ng, ypre, hs, dy, name):
    T = xbc.shape[0]
    nc = T // CHUNK
    Q = CHUNK

    def body(xs_ref, B_ref, C_ref, z_ref, dt_ref, dtT_ref, dtb_ref, dtbT_ref, al_ref, alT_ref, df_ref, ng_ref,
             ypre_ref, hs_ref, dy_ref,
             dz_ref, dxbc_ref, ddtb_ref, dal_ref, dD_ref, dng_ref, dh_scr):
        step = pl.program_id(0)

        @pl.when(step == 0)
        def _():
            dh_scr[...] = jnp.zeros_like(dh_scr)
            ddtb_ref[...] = jnp.zeros_like(ddtb_ref)
            dal_ref[...] = jnp.zeros_like(dal_ref)
            dD_ref[...] = jnp.zeros_like(dD_ref)
            dng_ref[...] = jnp.zeros_like(dng_ref)

        xdt, dt, A, cs, csT, causal, triu = _ssd_pre(dt_ref[:, :N_HEADS], dtT_ref[...], dtb_ref[...],
                                                    dtbT_ref[...], al_ref[...], alT_ref[...])
        ex = _expand_mat()
        dt_full = _hdot(dt, ex, "a")
        cs_full = _hdot(cs, ex, "a")
        cs_last = cs_full[Q - 1:Q, :]
        xs = xs_ref[...]
        xd = xs * dt_full
        e_full = jnp.exp(cs_full)
        dec_full = jnp.exp(cs_last - cs_full)
        cd_full = jnp.exp(cs_last)
        lane_head = lax.broadcasted_iota(jnp.int32, (1, GROUP_W), 1) // HEAD_DIM
        is_last = lax.broadcasted_iota(jnp.int32, (Q, 1), 0) == Q - 1
        dcs16 = jnp.zeros((Q, N_HEADS), F32)
        ddtx16 = jnp.zeros((Q, N_HEADS), F32)
        dD16 = jnp.zeros((8, N_HEADS), F32)
        lane16 = lax.broadcasted_iota(jnp.int32, (1, N_HEADS), 1)
        sub16 = lax.broadcasted_iota(jnp.int32, (N_HEADS, 1), 0)
        col_sums = jnp.zeros((N_HEADS, Q), F32)
        for g in range(N_GROUPS):
            sl = slice(g * GROUP_W, (g + 1) * GROUP_W)
            red = _reduce_mat(g)
            ypre_g = ypre_ref[:, sl]
            zg = z_ref[:, sl]
            sg = _sigmoid(zg)
            silu = zg * sg
            yz = ypre_g * silu
            rn = lax.rsqrt(jnp.mean(yz * yz, axis=-1, keepdims=True) + EPS)
            yh = yz * rn
            dy_g = dy_ref[:, sl]
            dng_ref[:, sl] += jnp.sum(dy_g * yh, axis=0, keepdims=True)
            dyh = dy_g * ng_ref[:, sl]
            dyz = rn * (dyh - yh * jnp.mean(dyh * yh, axis=-1, keepdims=True))
            dY = dyz * silu
            dz_ref[:, sl] = (dyz * ypre_g * sg * (1.0 + zg * (1.0 - sg))).astype(BF16)
            xs_g = xs[:, sl]
            xd_g = xd[:, sl]
            dec_g = dec_full[:, sl]
            cd_g = cd_full[:, sl]
            d_g = df_ref[:, sl]
            Bg = B_ref[:, g * D_STATE:(g + 1) * D_STATE].astype(BF16)
            Cg = C_ref[:, g * D_STATE:(g + 1) * D_STATE].astype(BF16)
            CB = _dot_nt(Cg, Bg)
            hg = hs_ref[0, g]
            hgb = hg.astype(BF16)
            yoff = _dot_nn(Cg, hgb) * e_full[:, sl]
            dhn = dh_scr[g]
            dhnb = dhn.astype(BF16)
            dYE = (dY * e_full[:, sl]).astype(BF16)
            dC = _dot_nt(dYE, hgb)
            dh_direct = _dot_tn(Cg, dYE)
            dXdd = _dot_nn(Bg, dhnb)
            dB = _dot_nt((xd_g * dec_g).astype(BF16), dhnb)
            dcd = jnp.sum(dhn * hg, axis=0, keepdims=True)
            dh_scr[g] = dh_direct + cd_g * dhn
            dYb = dY.astype(BF16)
            xd_b = xd_g.astype(BF16)
            dCB = jnp.zeros((Q, Q), F32)
            dXd = dXdd * dec_g
            for r in range(HEADS_PER_GROUP):
                h = g * HEADS_PER_GROUP + r
                Lm = _decay_matrix(cs, csT, h, causal)
                Gf = CB * Lm
                dYr = jnp.where(lane_head == r, dYb, jnp.zeros_like(dYb))
                dG = _dot_nt(dYr, xd_b)
                dCB = dCB + dG * Lm
                dXd = dXd + _dot_tn(Gf.astype(BF16), dYr)
                Mm = dG * Gf
                dcs16 = dcs16 + jnp.where(lane16 == h, jnp.sum(Mm, axis=1, keepdims=True), 0.0)
                col_sums = col_sums + jnp.where(sub16 == h, jnp.sum(Mm, axis=0, keepdims=True), 0.0)
            dCBb = dCB.astype(BF16)
            dC = dC + _dot_nn(dCBb, Bg)
            dB = dB + _dot_tn(dCBb, Cg)
            w_state = dXdd * dec_g * xd_g
            t_last = jnp.sum(w_state, axis=0, keepdims=True) + dcd * cd_g
            dcs_g = dY * yoff - w_state + jnp.where(is_last, t_last, 0.0)
            dcs16 = dcs16 + _hdot(dcs_g, red, "a")
            ddtx16 = ddtx16 + _hdot(dXd * xs_g, red, "a")
            dD16 = dD16 + _hdot(jnp.broadcast_to(jnp.sum(dY * xs_g, axis=0, keepdims=True), (8, GROUP_W)), red, "a")
            dxbc_ref[:, sl] = dXd * dt_full[:, sl] + dY * d_g
            dxbc_ref[:, D_INNER + g * D_STATE:D_INNER + (g + 1) * D_STATE] = dB
            dxbc_ref[:, D_INNER + 512 + g * D_STATE:D_INNER + 512 + (g + 1) * D_STATE] = dC
        eye = (lax.broadcasted_iota(jnp.int32, (N_HEADS, N_HEADS), 0)
               == lax.broadcasted_iota(jnp.int32, (N_HEADS, N_HEADS), 1)).astype(BF16)
        dcs16 = dcs16 - sum(_dot_tn(part, eye) for part in _split3(col_sums))
        da = _hdot(triu, dcs16, "b")
        ddt = da * A + ddtx16
        ddt_raw = ddt * _sigmoid(xdt)
        pr = lax.broadcasted_iota(jnp.int32, (N_HEADS, 128), 0)
        pc = lax.broadcasted_iota(jnp.int32, (N_HEADS, 128), 1)
        dz_ref[:, D_INNER:OFF_DT] = jnp.zeros((Q, OFF_DT - D_INNER), BF16)
        dz_ref[:, OFF_DT:OFF_DT + 128] = _hdot(ddt_raw, (pr == pc).astype(F32), "a").astype(BF16)
        dz_ref[:, OFF_DT + 128:] = jnp.zeros((Q, NP - OFF_DT - 128), BF16)
        ddtb_ref[...] += jnp.sum(ddt_raw, axis=0, keepdims=True)
        dal_ref[...] += jnp.sum(da * dt, axis=0, keepdims=True) * A
        dD_ref[...] += dD16[0:1, :]

    def rc(c):
        return nc - 1 - c

    in_specs = _ssd_in_specs(nc, True) + [
        pl.BlockSpec((CHUNK, D_INNER), lambda c: (rc(c), 0)),
        pl.BlockSpec((1, N_GROUPS, D_STATE, GROUP_W), lambda c: (rc(c), 0, 0, 0)),
        pl.BlockSpec((CHUNK, D_INNER), lambda c: (rc(c), 0)),
    ]
    small = pl.BlockSpec((1, N_HEADS), lambda c: (0, 0))
    return pl.pallas_call(
        body, grid=(nc,), in_specs=in_specs,
        out_specs=[pl.BlockSpec((CHUNK, NP), lambda c: (rc(c), 0)),
                   pl.BlockSpec((CHUNK, CONV_DIM), lambda c: (rc(c), 0)),
                   small, small, small,
                   pl.BlockSpec((1, D_INNER), lambda c: (0, 0))],
        out_shape=[jax.ShapeDtypeStruct((T, NP), BF16), jax.ShapeDtypeStruct((T, CONV_DIM), F32),
                   jax.ShapeDtypeStruct((1, N_HEADS), F32), jax.ShapeDtypeStruct((1, N_HEADS), F32),
                   jax.ShapeDtypeStruct((1, N_HEADS), F32), jax.ShapeDtypeStruct((1, D_INNER), F32)],
        scratch_shapes=[pltpu.VMEM((N_GROUPS, D_STATE, GROUP_W), F32)],
        name=name, compiler_params=_cparams(("arbitrary",)),
    )(xbc, xbc, xbc, proj, proj, dtT, dtb, dtbT, alog, alogT, dfull, ng, ypre, hs, dy)


N_PAIRS = ATTN_W // 128
PAIRS_PER_KV = N_PAIRS // 2
ATTN_SCALE = HEAD_DIM ** -0.5


def _kv_variants(kk):
    lo = lax.broadcasted_iota(jnp.int32, kk.shape, 1) < HEAD_DIM
    zero = jnp.zeros_like(kk)
    k00 = jnp.where(lo, kk, zero)
    k11 = jnp.where(lo, zero, kk)
    k01 = pltpu.roll(k00, HEAD_DIM, axis=1)
    k10 = pltpu.roll(k11, HEAD_DIM, axis=1)
    return [[k00.astype(BF16), k01.astype(BF16)], [k10.astype(BF16), k11.astype(BF16)]]


LOG2E = 1.4426950408889634


def _own_block():
    i = lax.broadcasted_iota(jnp.int32, (WINDOW, WINDOW), 0)
    j = lax.broadcasted_iota(jnp.int32, (WINDOW, WINDOW), 1)
    return j <= i


def _fold(own, a):
    return jnp.where(own, a[:, WINDOW:], a[:, :WINDOW])


def _attn_probs(qp, kvar, own, prev_bias, sk):
    s = _dot_nt(qp, kvar)
    sb = jnp.where(own, s[:, WINDOW:], s[:, :WINDOW] + prev_bias) * (ATTN_SCALE * LOG2E)
    sk2 = sk * LOG2E
    m = jnp.maximum(jnp.max(sb, axis=1, keepdims=True), sk2)
    pe = jnp.exp2(sb - m)
    es = jnp.exp2(sk2 - m)
    den = jnp.sum(pe, axis=1, keepdims=True) + es
    inv = 1.0 / den
    return pe * inv, es * inv


def _unfold(own, a):
    zero = jnp.zeros_like(a)
    return jnp.where(own, zero, a), jnp.where(own, a, zero)


def _sink(sinks, r):
    lane = lax.broadcasted_iota(jnp.int32, sinks.shape, 1)
    return jnp.sum(jnp.where(lane == r, sinks, 0.0), axis=1, keepdims=True)


def _kv_specs():
    return [pl.BlockSpec((WINDOW, KV_W), lambda n: (jnp.maximum(n - 1, 0), OFF_K // KV_W)),
            pl.BlockSpec((WINDOW, KV_W), lambda n: (n, OFF_K // KV_W)),
            pl.BlockSpec((WINDOW, KV_W), lambda n: (jnp.maximum(n - 1, 0), OFF_V // KV_W)),
            pl.BlockSpec((WINDOW, KV_W), lambda n: (n, OFF_V // KV_W))]


def _attn_fwd(proj, sinks, og, ycat, name):
    T = proj.shape[0]
    nb = T // WINDOW

    def body(q_ref, kp_ref, kc_ref, vp_ref, vc_ref, s_ref, og_ref, _, y_ref, o_ref, p_ref, ps_ref):
        n = pl.program_id(0)
        kv = _kv_variants(jnp.concatenate([kp_ref[...], kc_ref[...]], axis=0))
        vv = _kv_variants(jnp.concatenate([vp_ref[...], vc_ref[...]], axis=0))
        own = _own_block()
        prev_bias = jnp.where(n > 0, 0.0, NEG)
        sinks_v = s_ref[...]
        lane = lax.broadcasted_iota(jnp.int32, (1, 128), 1)
        ssq = jnp.zeros((WINDOW, 1), F32)
        sink_probs = jnp.zeros((WINDOW, 128), F32)
        for p in range(N_PAIRS):
            j = p // PAIRS_PER_KV
            qp = q_ref[:, p * 128:(p + 1) * 128].astype(BF16)
            o_pair = jnp.zeros((WINDOW, 128), F32)
            for par in range(2):
                r = 2 * p + par
                pn, ps = _attn_probs(qp, kv[j][par], own, prev_bias, _sink(sinks_v, r))
                pb = pn.astype(BF16)
                p_ref[:, r * 128:(r + 1) * 128] = pb
                sink_probs = jnp.where(lane == r, ps, sink_probs)
                p_prev, p_own = _unfold(own, pb)
                o_pair = o_pair + _dot_nn(p_prev, vv[j][par][:WINDOW]) + _dot_nn(p_own, vv[j][par][WINDOW:])
            o_ref[:, p * 128:(p + 1) * 128] = o_pair
            ssq = ssq + jnp.sum(o_pair * o_pair, axis=1, keepdims=True)
        ps_ref[...] = sink_probs
        rn = lax.rsqrt(ssq * (1.0 / ATTN_W) + EPS)
        y_ref[...] = (o_ref[...] * rn * og_ref[...]).astype(BF16)

    return pl.pallas_call(
        body, grid=(nb,),
        in_specs=[pl.BlockSpec((WINDOW, ATTN_W), lambda n: (n, OFF_Q // ATTN_W)), *_kv_specs(),
                  pl.BlockSpec((1, N_HEADS), lambda n: (0, 0)), pl.BlockSpec((1, ATTN_W), lambda n: (0, 0)), ANY],
        out_specs=[pl.BlockSpec((WINDOW, ATTN_W), lambda n: (n, 1)), pl.BlockSpec((WINDOW, ATTN_W), lambda n: (n, 0)),
                   pl.BlockSpec((WINDOW, N_HEADS * 128), lambda n: (n, 0)), pl.BlockSpec((WINDOW, 128), lambda n: (n, 0))],
        out_shape=[jax.ShapeDtypeStruct(ycat.shape, BF16), jax.ShapeDtypeStruct((T, ATTN_W), F32),
                   jax.ShapeDtypeStruct((T, N_HEADS * 128), BF16), jax.ShapeDtypeStruct((T, 128), F32)],
        input_output_aliases={7: 0}, name=name, compiler_params=_cparams(("parallel",)),
    )(proj, proj, proj, proj, proj, sinks, og, ycat)


def _attn_bwd(proj, og, o, dy, probs, sink_probs, dproj, name):
    T = proj.shape[0]
    nb = T // WINDOW

    def body(q_ref, kp_ref, kc_ref, vp_ref, vc_ref, og_ref, o_ref, dy_ref, p_ref, ps_ref, _,
             dq_ref, dk_ref, dv_ref, ds_ref, dog_ref, qt_scr, dot_scr, ds_scr, p_scr):
        n = pl.program_id(0)

        @pl.when(n == 0)
        def _():
            dk_ref[...] = jnp.zeros_like(dk_ref)
            dv_ref[...] = jnp.zeros_like(dv_ref)
            ds_ref[...] = jnp.zeros_like(ds_ref)
            dog_ref[...] = jnp.zeros_like(dog_ref)

        kv = _kv_variants(jnp.concatenate([kp_ref[...], kc_ref[...]], axis=0))
        vv = _kv_variants(jnp.concatenate([vp_ref[...], vc_ref[...]], axis=0))
        own = _own_block()
        sink_probs_v = ps_ref[...]
        of = o_ref[...]
        rn = lax.rsqrt(jnp.mean(of * of, axis=-1, keepdims=True) + EPS)
        oh = of * rn
        dyf = dy_ref[...]
        dog_ref[...] += jnp.sum(dyf * oh, axis=0, keepdims=True)
        doh = dyf * og_ref[...]
        do = rn * (doh - oh * jnp.mean(doh * oh, axis=-1, keepdims=True))
        lane = lax.broadcasted_iota(jnp.int32, (1, 128), 1)
        lane16 = lax.broadcasted_iota(jnp.int32, (1, N_HEADS), 1)
        dsink = jnp.zeros((1, N_HEADS), F32)
        for p in range(N_PAIRS):
            j = p // PAIRS_PER_KV
            q_t = q_ref[:, p * 128:(p + 1) * 128].T.astype(BF16)
            do_p = do[:, p * 128:(p + 1) * 128]
            o_p = of[:, p * 128:(p + 1) * 128]
            do_b = do_p.astype(BF16)
            do_t = do_p.T.astype(BF16)
            prod = do_p * o_p
            dq_pair = jnp.zeros((WINDOW, 128), F32)
            for par in range(2):
                r = 2 * p + par
                half = (lane < HEAD_DIM) if par == 0 else (lane >= HEAD_DIM)
                pb = p_ref[:, r * 128:(r + 1) * 128]
                ps = jnp.sum(jnp.where(lane == r, sink_probs_v, 0.0), axis=1, keepdims=True)
                delta = jnp.sum(jnp.where(half, prod, 0.0), axis=1, keepdims=True)
                dP = _fold(own, _dot_nt(do_b, vv[j][par]))
                dS = pb.astype(F32) * (dP - delta)
                dsink = dsink + jnp.where(lane16 == r, -jnp.sum(ps * delta, axis=0, keepdims=True), 0.0)
                dS_parts = _unfold(own, (dS * ATTN_SCALE).astype(BF16))
                p_parts = _unfold(own, pb)
                at = ((p % PAIRS_PER_KV) * 2 + par) * WINDOW
                qt_scr[j, :, at:at + WINDOW] = q_t[par * HEAD_DIM:(par + 1) * HEAD_DIM]
                dot_scr[j, :, at:at + WINDOW] = do_t[par * HEAD_DIM:(par + 1) * HEAD_DIM]
                for blk in range(2):
                    dq_pair = dq_pair + _dot_nn(dS_parts[blk], kv[j][par][blk * WINDOW:(blk + 1) * WINDOW])
                    ds_scr[j, blk, at:at + WINDOW, :] = dS_parts[blk]
                    p_scr[j, blk, at:at + WINDOW, :] = p_parts[blk]
            dq_ref[:, p * 128:(p + 1) * 128] = dq_pair.astype(BF16)
        rows = [pl.multiple_of(jnp.maximum(n - 1, 0) * WINDOW, WINDOW), pl.multiple_of(n * WINDOW, WINDOW)]
        for lhs, rhs, ref in [(qt_scr, ds_scr, dk_ref), (dot_scr, p_scr, dv_ref)]:
            for blk in range(2):
                both_t = jnp.concatenate([_dot_nn(lhs[j], rhs[j, blk]) for j in range(2)], axis=0)
                ref[pl.ds(rows[blk], WINDOW), :] += both_t.T
        ds_ref[...] += dsink

    full_kv = pl.BlockSpec((T, KV_W), lambda n: (0, 0))
    blk = pl.BlockSpec((WINDOW, ATTN_W), lambda n: (n, 0))
    return pl.pallas_call(
        body, grid=(nb,),
        in_specs=[pl.BlockSpec((WINDOW, ATTN_W), lambda n: (n, OFF_Q // ATTN_W)), *_kv_specs(),
                  pl.BlockSpec((1, ATTN_W), lambda n: (0, 0)), blk, pl.BlockSpec((WINDOW, ATTN_W), lambda n: (n, 1)),
                  pl.BlockSpec((WINDOW, N_HEADS * 128), lambda n: (n, 0)),
                  pl.BlockSpec((WINDOW, 128), lambda n: (n, 0)), ANY],
        out_specs=[pl.BlockSpec((WINDOW, ATTN_W), lambda n: (n, OFF_Q // ATTN_W)), full_kv, full_kv,
                   pl.BlockSpec((1, N_HEADS), lambda n: (0, 0)), pl.BlockSpec((1, ATTN_W), lambda n: (0, 0))],
        out_shape=[jax.ShapeDtypeStruct(dproj.shape, BF16), jax.ShapeDtypeStruct((T, KV_W), F32),
                   jax.ShapeDtypeStruct((T, KV_W), F32), jax.ShapeDtypeStruct((1, N_HEADS), F32),
                   jax.ShapeDtypeStruct((1, ATTN_W), F32)],
        scratch_shapes=[pltpu.VMEM((2, HEAD_DIM, 8 * WINDOW), BF16), pltpu.VMEM((2, HEAD_DIM, 8 * WINDOW), BF16),
                        pltpu.VMEM((2, 2, 8 * WINDOW, WINDOW), BF16), pltpu.VMEM((2, 2, 8 * WINDOW, WINDOW), BF16)],
        input_output_aliases={10: 0}, name=name, compiler_params=_cparams(("arbitrary",)),
    )(proj, proj, proj, proj, proj, og, o, dy, probs, sink_probs, dproj)


ANY = pl.BlockSpec(memory_space=pl.ANY)


def _coords():
    return lax.axis_index("x"), lax.axis_index("y"), lax.axis_index("c")


HBM = pl.BlockSpec(memory_space=pltpu.HBM)
SEM = pl.BlockSpec(memory_space=pltpu.SEMAPHORE)
EFFECT = pltpu.SideEffectType.DATAFLOW_SIDE_EFFECTING


def _in_hbm(a):
    return pltpu.with_memory_space_constraint(a, pltpu.HBM)


def _remote_start(srcs, lands, plan, n_copies, name, after=None):
    ns, nb = len(srcs), len(srcs) + len(lands)
    n_after = 0 if after is None else 1

    def body(*refs):
        src_refs, land_refs = refs[:ns], refs[ns:nb]
        send_sems, recv_sems = refs[nb + n_after], refs[nb + n_after + 1]
        token = refs[-1]
        x, y, c = _coords()
        for i, (sv, dv, dev) in enumerate(plan(src_refs, land_refs, x, y, c)):
            pltpu.make_async_remote_copy(src_ref=sv, dst_ref=dv, send_sem=send_sems.at[i], recv_sem=recv_sems.at[i],
                                         device_id=dev, device_id_type=MESH).start()
        token[...] = jnp.zeros_like(token)

    bufs = list(srcs) + list(lands)
    outs = pl.pallas_call(
        body, name=name,
        out_shape=(pltpu.SemaphoreType.DMA((n_copies,)), pltpu.SemaphoreType.DMA((n_copies,)),
                   *[pltpu.HBM(b.shape, b.dtype) for b in bufs], jax.ShapeDtypeStruct((8, 128), F32)),
        in_specs=[HBM] * nb + [ANY] * n_after,
        out_specs=(SEM, SEM, *[HBM] * nb, pl.BlockSpec(memory_space=pltpu.VMEM)),
        input_output_aliases={i: 2 + i for i in range(nb)},
        compiler_params=pltpu.CompilerParams(has_side_effects=EFFECT),
    )(*[_in_hbm(b) for b in bufs], *([] if after is None else [after]))
    return outs[0], outs[1], list(outs[2:2 + ns]), list(outs[2 + ns:2 + nb]), outs[-1]


def _remote_wait(started, after, plan, name):
    send_sems, recv_sems, srcs, lands, _ = started
    ns, nb = len(srcs), len(srcs) + len(lands)

    def body(*refs):
        src_refs, land_refs = refs[:ns], refs[ns:nb]
        send_sems, recv_sems = refs[nb], refs[nb + 1]
        x, y, c = _coords()
        for i, (sv, dv, dev) in enumerate(plan(src_refs, land_refs, x, y, c)):
            cp = pltpu.make_async_remote_copy(src_ref=sv, dst_ref=dv, send_sem=send_sems.at[i],
                                              recv_sem=recv_sems.at[i], device_id=dev, device_id_type=MESH)
            cp.wait_send()
            cp.wait_recv()

    bufs = list(srcs) + list(lands)
    outs = pl.pallas_call(
        body, name=name, out_shape=tuple(pltpu.HBM(b.shape, b.dtype) for b in bufs),
        in_specs=[HBM] * nb + [SEM, SEM, ANY], out_specs=tuple([HBM] * nb),
        input_output_aliases={i: i for i in range(nb)},
        compiler_params=pltpu.CompilerParams(has_side_effects=EFFECT),
    )(*bufs, send_sems, recv_sems, after)
    return list(outs[:ns]), list(outs[ns:])


def _pair_plan(src_refs, land_refs, x, y, c):
    plan = []
    for s, l in zip(src_refs, land_refs):
        for q in range(4):
            plan.append((s.at[2 * q + (1 - c)], l.at[q], (x, y, 1 - c)))
    return plan


def _pair4_plan(src_refs, land_refs, x, y, c):
    plan = []
    for s, l in zip(src_refs, land_refs):
        for q in range(4):
            plan.append((s.at[q], l.at[q], (x, y, 1 - c)))
    return plan


def _chips_plan(src_refs, land_refs, x, y, c):
    plan = []
    for s, l in zip(src_refs, land_refs):
        for k, (tx, ty) in enumerate([(1 - x, y), (x, 1 - y), (1 - x, 1 - y)]):
            plan.append((s.at[2 * tx + ty], l.at[k], (tx, ty, c)))
    return plan


def _everyone_plan(src_refs, land_refs, x, y, c):
    me = 4 * x + 2 * y + c
    plan = []
    for s, l in zip(src_refs, land_refs):
        for fx, fy, fc in [(0, 0, 1), (1, 0, 0), (1, 0, 1), (0, 1, 0), (0, 1, 1), (1, 1, 0), (1, 1, 1)]:
            dev = ((1 - x) if fx else x, (1 - y) if fy else y, (1 - c) if fc else c)
            plan.append((s, l.at[me], dev))
    return plan


def _pair_add(g8, r1, csel, tr, name):
    _, R, C = r1.shape
    g4 = g8.reshape(4, 2, R, C)

    def body(c_ref, g_ref, r_ref, o_ref):
        o_ref[...] = (g_ref[...].astype(F32) + r_ref[...].astype(F32)).astype(BF16)

    return pl.pallas_call(
        body,
        grid_spec=pltpu.PrefetchScalarGridSpec(
            num_scalar_prefetch=1, grid=(4, R // tr),
            in_specs=[pl.BlockSpec((None, None, tr, C), lambda q, i, cs: (q, cs[0], i, 0)),
                      pl.BlockSpec((None, tr, C), lambda q, i, cs: (q, i, 0))],
            out_specs=pl.BlockSpec((None, tr, C), lambda q, i, cs: (q, i, 0))),
        out_shape=jax.ShapeDtypeStruct((4, R, C), BF16), name=name,
        compiler_params=_cparams(("parallel", "parallel")),
    )(csel, g4, r1)


def _adamw_math(w, g, m, v):
    m = ADAM_B1 * m + (1.0 - ADAM_B1) * g
    v = ADAM_B2 * v + (1.0 - ADAM_B2) * (g * g)
    m_hat = m / (1.0 - ADAM_B1 ** ADAM_STEP)
    v_hat = v / (1.0 - ADAM_B2 ** ADAM_STEP)
    delta = -ADAM_LR * (m_hat / (jnp.sqrt(v_hat) + ADAM_EPS) + ADAM_WD * w)
    return delta, m, v


def _adamw_big(w, m, v, p4, r3, qsel, tile, name):
    R, C = w.shape
    tr, tc = tile

    def body(q_ref, w_ref, m_ref, v_ref, p_ref, r_ref, g_out, d_out, m_out, v_out):
        g = p_ref[...].astype(F32) + r_ref[0].astype(F32) + r_ref[1].astype(F32) + r_ref[2].astype(F32)
        d, mn, vn = _adamw_math(w_ref[...], g, m_ref[...], v_ref[...])
        g_out[...] = g
        d_out[...] = d
        m_out[...] = mn
        v_out[...] = vn

    blk = pl.BlockSpec((tr, tc), lambda i, j, qs: (i, j))
    return pl.pallas_call(
        body,
        grid_spec=pltpu.PrefetchScalarGridSpec(
            num_scalar_prefetch=1, grid=(R // tr, C // tc),
            in_specs=[blk, blk, blk, pl.BlockSpec((None, tr, tc), lambda i, j, qs: (qs[0], i, j)),
                      pl.BlockSpec((3, tr, tc), lambda i, j, qs: (0, i, j))],
            out_specs=[blk, blk, blk, blk]),
        out_shape=[jax.ShapeDtypeStruct((R, C), F32)] * 4, name=name,
        compiler_params=_cparams(("parallel", "parallel")),
    )(qsel, w, m, v, p4, r3)


def _sum_partials(p4, r3, qsel, tc, name):
    _, R, C = p4.shape

    def body(q_ref, p_ref, r_ref, o_ref):
        o_ref[...] = p_ref[...].astype(F32) + r_ref[0].astype(F32) + r_ref[1].astype(F32) + r_ref[2].astype(F32)

    return pl.pallas_call(
        body,
        grid_spec=pltpu.PrefetchScalarGridSpec(
            num_scalar_prefetch=1, grid=(C // tc,),
            in_specs=[pl.BlockSpec((None, R, tc), lambda j, qs: (qs[0], 0, j)),
                      pl.BlockSpec((3, R, tc), lambda j, qs: (0, 0, j))],
            out_specs=pl.BlockSpec((R, tc), lambda j, qs: (0, j))),
        out_shape=jax.ShapeDtypeStruct((R, C), F32), name=name, compiler_params=_cparams(("parallel",)),
    )(qsel, p4, r3)


def _adamw_tiled(w, g, m, v, tc, name):
    R, C = w.shape

    def body(w_ref, g_ref, m_ref, v_ref, d_out, m_out, v_out):
        d, mn, vn = _adamw_math(w_ref[...], g_ref[...], m_ref[...], v_ref[...])
        d_out[...] = d
        m_out[...] = mn
        v_out[...] = vn

    blk = pl.BlockSpec((R, tc), lambda j: (0, j))
    return pl.pallas_call(
        body, grid=(C // tc,), in_specs=[blk] * 4, out_specs=[blk] * 3,
        out_shape=[jax.ShapeDtypeStruct((R, C), F32)] * 3, name=name, compiler_params=_cparams(("parallel",)),
    )(w, g, m, v)


def _small_sum(parts, name):
    def body(p_ref, o_ref):
        acc = p_ref[0]
        for d in range(1, N_DEV):
            acc = acc + p_ref[d]
        o_ref[...] = acc

    return pl.pallas_call(
        body, out_shape=jax.ShapeDtypeStruct(parts.shape[1:], F32), name=name,
        compiler_params=_cparams(),
    )(parts)


def _adamw_small(w, g, m, v, name):
    def body(w_ref, g_ref, m_ref, v_ref, d_out, m_out, v_out):
        d, mn, vn = _adamw_math(w_ref[...], g_ref[...], m_ref[...], v_ref[...])
        d_out[...] = d
        m_out[...] = mn
        v_out[...] = vn

    return pl.pallas_call(
        body, out_shape=[jax.ShapeDtypeStruct(w.shape, F32)] * 3, name=name, compiler_params=_cparams(),
    )(w, g, m, v)


def _row(*pieces):
    r = jnp.concatenate([p.reshape(1, -1) for p in pieces], axis=1)
    return jnp.pad(r, ((0, 0), (0, D_MODEL - r.shape[1])))


def _pack_small(mix, convb, ssmg, attng, mlpg, fing, convw, dtb, alog, dsk, sinks, extra=None):
    last = [dtb, alog, dsk, sinks] + ([extra] if extra is not None else [])
    rows = [_row(mix), _row(convb), _row(ssmg, attng), _row(mlpg), _row(fing),
            jnp.pad(convw, ((0, 0), (0, D_MODEL - convw.shape[1]))), _row(*last)]
    packed = jnp.concatenate(rows, axis=0)
    return jnp.pad(packed, ((0, SMALL_ROWS - packed.shape[0]), (0, 0)))


def _unpack_small(p, conv_n):
    return dict(
        mix_norm_g=p[0:1, :], conv_b=p[1:2, :], ssm_norm_g=p[2:3, :D_INNER], attn_out_norm_g=p[2:3, D_INNER:],
        mlp_norm_g=p[3:4, :], final_norm_g=p[4, :], conv_w=p[5:9, :conv_n][None],
        dt_bias=p[9:10, 0:16], A_log=p[9:10, 16:32], D_skip=p[9:10, 32:48], attn_sinks=p[9:10, 48:64])


WEIGHT_ORDER = ["mix_norm_g", "w_in", "conv_w", "conv_b", "dt_bias", "A_log", "D_skip", "ssm_norm_g", "attn_sinks",
                "attn_out_norm_g", "w_out", "mlp_norm_g", "w_up", "w_down", "final_norm_g"]


def _to_my_columns(w_nat):
    pad = jnp.zeros((w_nat.shape[0], NP - IN_PROJ), w_nat.dtype)
    return jnp.concatenate([w_nat[:, :NAT_DT], w_nat[:, NAT_DT + N_HEADS:], w_nat[:, NAT_DT:NAT_DT + N_HEADS], pad],
                           axis=1)


PER = IN_PROJ // N_DEV
SUPER_STEP = 544
SUPER = 576


def _natural_rows(g, lo, hi):
    segments = [(0, NAT_DT, 0), (NAT_DT, NAT_DT + N_HEADS, OFF_DT - NAT_DT), (NAT_DT + N_HEADS, IN_PROJ, -N_HEADS),
                (IN_PROJ, NP, 0)]
    pieces = [g[max(lo, a) + shift:min(hi, b) + shift] for a, b, shift in segments if max(lo, a) < min(hi, b)]
    return pieces[0] if len(pieces) == 1 else jnp.concatenate(pieces, axis=0)


def _w_in_from_super_slabs(sup):
    seam = SUPER - SUPER_STEP
    units = []
    for i in range(N_DEV):
        base = SUPER_STEP * i
        units.append((base, base + seam, sup[i, :seam] if i == 0 else sup[i - 1, SUPER_STEP:] + sup[i, :seam]))
        units.append((base + seam, base + SUPER_STEP, sup[i, seam:SUPER_STEP]))
    units.append((SUPER_STEP * N_DEV, SUPER_STEP * N_DEV + seam, sup[N_DEV - 1, SUPER_STEP:]))

    def natural(lo, hi):
        return [rows[max(lo, a) - a:min(hi, b) - a] for a, b, rows in units if max(lo, a) < min(hi, b)]

    pieces = natural(0, NAT_DT) + natural(NAT_DT + N_HEADS, IN_PROJ) + natural(NAT_DT, NAT_DT + N_HEADS)
    return jnp.concatenate(pieces + [jnp.zeros((NP - IN_PROJ, D_MODEL), sup.dtype)], axis=0)


def _to_natural_columns(w_my):
    return jnp.concatenate([w_my[:, :NAT_DT], w_my[:, OFF_DT:OFF_DT + N_HEADS], w_my[:, NAT_DT:OFF_DT]], axis=1)


SLAB = 1024


def _grad_w_up(h2, du, name, sel=None, add=None, after=None):
    T, D = h2.shape
    if sel is None:
        pick, n_slab, pre = (lambda j, *cs: j), N_DEV, None
    else:
        pre, other = sel
        pick, n_slab = (lambda j, cs: 2 * j + ((1 - cs[0]) if other else cs[0])), 4
    o_spec = pl.BlockSpec((None, D, SLAB), lambda i, j, k, *cs: (j, 0, 0))
    return _matmul(
        h2, du, mode="tn", grid=(1, n_slab, 1),
        a_spec=pl.BlockSpec((T, D), lambda i, j, k, *cs: (0, 0)),
        b_spec=pl.BlockSpec((T, SLAB), lambda i, j, k, *cs: (0, pick(j, *cs))),
        out_shapes=[jax.ShapeDtypeStruct((n_slab, D, SLAB), BF16)], out_specs=[o_spec], tile=(D, SLAB), name=name,
        extras=() if add is None else (add,), extra_specs=() if add is None else (o_spec,),
        epilogue=None if add is None else (lambda acc, r: (acc + r.astype(F32),)), after=after, prefetch=pre)[0]


def _grad_w_down(act, dx3b, name, sel=None, add=None, after=None):
    T, D = dx3b.shape
    if sel is None:
        pick, n_slab, pre = (lambda i, *cs: i), N_DEV, None
    else:
        pre, other = sel
        pick, n_slab = (lambda i, cs: 2 * i + ((1 - cs[0]) if other else cs[0])), 4
    o_spec = pl.BlockSpec((None, SLAB, D), lambda i, j, k, *cs: (i, 0, 0))
    return _matmul(
        act, dx3b, mode="tn", grid=(n_slab, 1, 1),
        a_spec=pl.BlockSpec((T, SLAB), lambda i, j, k, *cs: (0, pick(i, *cs))),
        b_spec=pl.BlockSpec((T, D), lambda i, j, k, *cs: (0, 0)),
        out_shapes=[jax.ShapeDtypeStruct((n_slab, SLAB, D), BF16)], out_specs=[o_spec], tile=(SLAB, D), name=name,
        extras=() if add is None else (add,), extra_specs=() if add is None else (o_spec,),
        epilogue=None if add is None else (lambda acc, r: (acc + r.astype(F32),)), after=after, prefetch=pre)[0]


class _FixedWeights:
    def __init__(self, w_in_p, w_out_f, w_up_s, w_down_f, conv_w_f):
        self.w = (w_in_p, w_out_f, w_up_s, w_down_f, conv_w_f)
        self.grads = {}

    def mixer_weights(self, after):
        return self.w[0], None

    def conv_weight(self, after):
        return self.w[4]

    def out_weight(self, after):
        return self.w[1]

    def up_weight(self, after):
        return self.w[2]

    def down_weight(self, h, after):
        return self.w[3][:, h * (D_MODEL // 2):(h + 1) * (D_MODEL // 2)]

    def mlp_grads(self, h2, du, act, dx3b):
        self.grads.update(w_up=_grad_w_up(h2, du, "grad_w_up"),
                          w_down=_grad_w_down(act, dx3b, "grad_w_down").reshape(D_FF, D_MODEL))
        return None

    def grad_sent(self, tag, after):
        return None

    def out_grad(self, g_out):
        self.grads.update(w_out=g_out)
        return None

    def in_grad(self, g_in):
        self.grads.update(w_in=g_in)
        return None


def _local_step(x, tgt, p, hooks):
    T = x.shape[0]
    D = D_MODEL
    h1 = _rmsnorm_fwd(x, p["mix_norm_g"], "norm_mix")
    w_in_t, token = hooks.mixer_weights(h1)
    (proj,) = _mm_simple(h1, w_in_t, mode="nt", M=T, N=NP, K=D, tm=min(T, 1024), tn=1536, tk=D, out_dtype=F32,
                         name="in_proj", after=token)
    conv_w_f = hooks.conv_weight(proj)
    xbc, dsilu = _conv_fwd(proj, conv_w_f, p["conv_b"], "conv_fwd")
    dtT = proj[:, OFF_DT:OFF_DT + N_HEADS].T
    dtbT = p["dt_bias"].T
    alogT = p["A_log"].T
    dfull = jnp.repeat(p["D_skip"], HEAD_DIM, axis=1)
    ycat, ypre, hs = _ssd_fwd(xbc, proj, dtT, p["dt_bias"], dtbT, p["A_log"], alogT, dfull, p["ssm_norm_g"],
                              "ssd_fwd")
    ycat, o_att, probs, sink_probs = _attn_fwd(proj, p["attn_sinks"], p["attn_out_norm_g"], ycat, "attn_fwd")
    w_out_f = hooks.out_weight(ycat)
    tm = min(T, 1024)
    def residual_and_norm(acc, res, gain):
        x2 = acc + res
        return x2, x2 * lax.rsqrt(jnp.mean(x2 * x2, axis=-1, keepdims=True) + EPS) * gain

    rows = min(T, 512)
    x2, h2 = _matmul(
        ycat, w_out_f, mode="nn", grid=(T // rows, 1, 1),
        a_spec=pl.BlockSpec((rows, D), lambda i, j, k: (i, 0)), b_spec=pl.BlockSpec((D, D), lambda i, j, k: (0, 0)),
        out_shapes=[jax.ShapeDtypeStruct((T, D), F32), jax.ShapeDtypeStruct((T, D), BF16)],
        out_specs=[pl.BlockSpec((rows, D), lambda i, j, k: (i, 0))] * 2, tile=(rows, D), name="out_proj",
        extras=(x, p["mlp_norm_g"]),
        extra_specs=[pl.BlockSpec((rows, D), lambda i, j, k: (i, 0)), pl.BlockSpec((1, D), lambda i, j, k: (0, 0))],
        epilogue=residual_and_norm)
    w_up_s = hooks.up_weight(h2)
    grid = (T // tm, N_DEV, 1)
    u, act = _matmul(
        h2, w_up_s, mode="nn", grid=grid,
        a_spec=pl.BlockSpec((tm, D), lambda i, j, k: (i, 0)),
        b_spec=pl.BlockSpec((None, D, 1024), lambda i, j, k: (j, 0, 0)),
        out_shapes=[jax.ShapeDtypeStruct((T, D_FF), F32), jax.ShapeDtypeStruct((T, D_FF), BF16)],
        out_specs=[pl.BlockSpec((tm, 1024), lambda i, j, k: (i, j))] * 2, tile=(tm, 1024), name="mlp_up",
        epilogue=lambda acc: (acc, jnp.square(jnp.maximum(acc, 0.0))))
    half = D // 2
    w_down_halves, x3_halves = [], []
    for h in range(2):
        w_down_halves.append(hooks.down_weight(h, act if h == 0 else x3_halves[0]))
        x3_halves.append(_matmul(
            act, w_down_halves[h], mode="nn", grid=(T // tm, 1, D_FF // 2048),
            a_spec=pl.BlockSpec((tm, 2048), lambda i, j, k: (i, k)),
            b_spec=pl.BlockSpec((2048, half), lambda i, j, k: (k, 0)),
            out_shapes=[jax.ShapeDtypeStruct((T, half), F32)],
            out_specs=[pl.BlockSpec((tm, half), lambda i, j, k: (i, 0))], tile=(tm, half), name=f"mlp_down_{h}",
            extras=(x2,), extra_specs=[pl.BlockSpec((tm, half), lambda i, j, k, h=h: (i, h))],
            epilogue=lambda acc, res: (acc + res,))[0])
    loss_part, d_fin, dx3, dx3b = _final_loss(x3_halves, tgt, p["final_norm_g"].reshape(1, D), "loss_head")
    (du,) = _matmul(
        dx3b, tuple(w_down_halves), mode="nt", grid=(T // tm, D_FF // 1024, 1),
        a_spec=pl.BlockSpec((tm, D), lambda i, j, k: (i, 0)),
        b_spec=(pl.BlockSpec((1024, half), lambda i, j, k: (j, 0)),) * 2,
        out_shapes=[jax.ShapeDtypeStruct((T, D_FF), BF16)],
        out_specs=[pl.BlockSpec((tm, 1024), lambda i, j, k: (i, j))], tile=(tm, 1024), name="mlp_down_bwd",
        extras=(u,), extra_specs=[pl.BlockSpec((tm, 1024), lambda i, j, k: (i, j))],
        epilogue=lambda acc, uu: (acc * (2.0 * jnp.maximum(uu, 0.0)),),
        dot_fn=lambda a, b0, b1: _dot_nt(a[:, :half], b0) + _dot_nt(a[:, half:], b1))
    token = hooks.mlp_grads(h2, du, act, dx3b)
    (dh2,) = _matmul(
        du, w_up_s, mode="nt", grid=(T // tm, D // 1024, N_DEV // 2),
        a_spec=pl.BlockSpec((tm, 2048), lambda i, j, k: (i, k)),
        b_spec=pl.BlockSpec((2, 1024, 1024), lambda i, j, k: (k, j, 0)),
        out_shapes=[jax.ShapeDtypeStruct((T, D), F32)],
        out_specs=[pl.BlockSpec((tm, 1024), lambda i, j, k: (i, j))], tile=(tm, 1024), name="mlp_up_bwd",
        after=token, dot_fn=lambda a, b: _dot_nt(a[:, :1024], b[0]) + _dot_nt(a[:, 1024:], b[1]))
    dx2, dx2b, d_mlp = _rmsnorm_bwd(dh2, x2, p["mlp_norm_g"], dx3, "norm_mlp_bwd")
    (g_out,) = _mm_simple(ycat, dx2b, mode="tn", M=D, N=D, K=T, tm=1024, tn=1024, tk=T, out_dtype=BF16,
                          name="grad_w_out")
    token = hooks.out_grad(g_out)
    (dy,) = _mm_simple(dx2b, w_out_f, mode="nt", M=T, N=D, K=D, tm=tm, tn=1024, tk=D, out_dtype=F32,
                       name="out_proj_bwd", after=token)
    token = hooks.grad_sent("out", dy)
    ssm_g = p["ssm_norm_g"] if token is None else p["ssm_norm_g"] + token[0:1, 0:1]
    dproj, dxbc_act, d_dtb, d_alog, d_dskip, d_ssmg = _ssd_bwd(
        xbc, proj, dtT, p["dt_bias"], dtbT, p["A_log"], alogT, dfull, ssm_g, ypre, hs, dy, "ssd_bwd")
    dproj, d_convw, d_convb = _conv_bwd(proj, dxbc_act, dsilu, conv_w_f, dproj, "conv_bwd")
    dproj, dk, dv, d_sinks, d_attng = _attn_bwd(proj, p["attn_out_norm_g"], o_att, dy, probs, sink_probs, dproj,
                                                "attn_bwd")
    dproj = lax.dynamic_update_slice(dproj, jnp.concatenate([dk, dv], axis=1).astype(BF16), (0, OFF_K))
    (g_in,) = _mm_simple(dproj, h1, mode="tn", M=NP, N=D, K=T, tm=1536, tn=1024, tk=T, out_dtype=BF16,
                         name="grad_w_in")
    token = hooks.in_grad(g_in)
    (dh1,) = _mm_simple(dproj, w_in_t, mode="nn", M=T, N=D, K=NP, tm=tm, tn=1024, tk=2304, out_dtype=F32,
                        name="in_proj_bwd", after=token)
    token = hooks.grad_sent("in", dh1)
    mix_g = p["mix_norm_g"] if token is None else p["mix_norm_g"] + token[0:1, 0:1]
    dx, d_mix = _rmsnorm_bwd(dh1, x, mix_g, dx2, "norm_mix_bwd", with_bf16=False)
    small = _pack_small(d_mix, d_convb, d_ssmg, d_attng, d_mlp, d_fin, d_convw, d_dtb, d_alog, d_dskip, d_sinks,
                        extra=loss_part[:, 0:1])
    return dx, small


def _rows_rotated(v, shift, name):
    R, C = v.shape
    tc = 512

    def body(s_ref, v_ref, o_ref):
        o_ref[...] = pltpu.roll(v_ref[...], s_ref[0], axis=0).astype(BF16)

    return pl.pallas_call(
        body,
        grid_spec=pltpu.PrefetchScalarGridSpec(
            num_scalar_prefetch=1, grid=(C // tc,), in_specs=[pl.BlockSpec((R, tc), lambda j, s: (0, j))],
            out_specs=pl.BlockSpec((R, tc), lambda j, s: (0, j))),
        out_shape=jax.ShapeDtypeStruct((R, C), BF16), name=name, compiler_params=_cparams(("parallel",)),
    )(shift, v)


def _landing(own, me):
    zone = lax.empty((N_DEV,) + own.shape, own.dtype)
    return lax.dynamic_update_slice(zone, own[None], (me,) + (0,) * own.ndim)


def _sequencer_gather(owns, split, me, collective_id, name):
    n = len(owns)
    zone_refs = [jax.new_ref(_landing(o, me), memory_space=pltpu.MemorySpace.HBM) for o in owns]
    own_refs = [jax.new_ref(o, memory_space=pltpu.MemorySpace.HBM) for o in owns]
    N_COPIES = 9

    @pl.kernel(mesh=plsc.ScalarSubcoreMesh(axis_name="sequencer", num_cores=1), name=name,
               scratch_types=(pltpu.SemaphoreType.DMA((n, N_COPIES)), pltpu.SemaphoreType.DMA((n, N_COPIES))),
               compiler_params=pltpu.CompilerParams(collective_id=collective_id))
    def launch(send_sems, recv_sems):
        x, y, c = _coords()
        sibling, xn, yn, diag = (x, y, 1 - c), (1 - x, y, c), (x, 1 - y, c), (1 - x, 1 - y, c)
        barrier = pltpu.get_barrier_semaphore()
        for peer in [sibling, xn, yn, diag]:
            pl.semaphore_signal(barrier, inc=1, device_id=peer, device_id_type=MESH)
        pl.semaphore_wait(barrier, 4)

        def block(a, dev, half=None):
            ref = zone_refs[a].at[4 * dev[0] + 2 * dev[1] + dev[2]]
            if half is None:
                return ref
            rows = owns[a].shape[0] // 2
            return ref.at[pl.ds(half * rows, rows)]

        def copy(a, k, src, dst, to):
            return pltpu.make_async_remote_copy(src_ref=src, dst_ref=dst, send_sem=send_sems.at[a, k],
                                                recv_sem=recv_sems.at[a, k], device_id=to, device_id_type=MESH)

        me_dev = (x, y, c)
        sent = []
        first = {}
        for a in range(n):
            for k, peer in enumerate([sibling, xn, yn] + ([] if split[a] else [diag])):
                first[a, k] = copy(a, k, own_refs[a], block(a, me_dev), peer)
                first[a, k].start()
                sent.append(first[a, k])
        from_sibling = []
        for a in range(n):
            first[a, 1].wait_recv()
            sent.append(copy(a, 4, block(a, xn), block(a, xn), sibling))
            if split[a]:
                sent.append(copy(a, 6, block(a, xn, 0), block(a, xn, 0), yn))
            first[a, 2].wait_recv()
            sent.append(copy(a, 5, block(a, yn), block(a, yn), sibling))
            if split[a]:
                sent.append(copy(a, 7, block(a, yn, 1), block(a, yn, 1), xn))
            for cp in sent[-(4 if split[a] else 2):]:
                cp.start()
        for a in range(n):
            if split[a]:
                copy(a, 6, block(a, diag, 0), block(a, diag, 0), yn).wait_recv()
                sent.append(copy(a, 8, block(a, diag, 0), block(a, diag, 0), sibling))
                sent[-1].start()
                copy(a, 7, block(a, diag, 1), block(a, diag, 1), xn).wait_recv()
                sent.append(copy(a, 3, block(a, diag, 1), block(a, diag, 1), sibling))
                sent[-1].start()
            else:
                first[a, 3].wait_recv()
                sent.append(copy(a, 8, block(a, diag), block(a, diag), sibling))
                sent[-1].start()
        for a in range(n):
            first[a, 0].wait_recv()
            copy(a, 4, block(a, xn), block(a, xn), sibling).wait_recv()
            copy(a, 5, block(a, yn), block(a, yn), sibling).wait_recv()
            if split[a]:
                copy(a, 8, block(a, diag, 0), block(a, diag, 0), sibling).wait_recv()
                copy(a, 3, block(a, diag, 1), block(a, diag, 1), sibling).wait_recv()
            else:
                copy(a, 8, block(a, diag), block(a, diag), sibling).wait_recv()
        for cp in sent:
            cp.wait_send()

    launch()
    return zone_refs


class _ShardedWeights:
    def __init__(self, w_in, w_out, conv_w, w_up, w_down, me, csel):
        self.me, self.csel = me, csel
        padded = jnp.pad(jnp.transpose(w_in), ((0, SUPER - PER), (0, 0)))
        own_rows = _rows_rotated(padded, jnp.reshape(2 * me, (1,)).astype(jnp.int32), "w_in_super_slab")
        (self.in_ref,) = _sequencer_gather([own_rows], [True], me, 7, "gather_w_in_sequencer")
        self.out_ref, self.conv_ref = _sequencer_gather([w_out.astype(BF16), conv_w], [True, False], me, 8,
                                                        "gather_w_out_sequencer")
        (self.up_ref,) = _sequencer_gather([w_up.astype(BF16)], [True], me, 9, "gather_w_up_sequencer")
        down = w_down.astype(BF16)
        self.down_refs = [_sequencer_gather([down[:, h * (D_MODEL // 2):(h + 1) * (D_MODEL // 2)]], [True], me, 10 + h,
                                            f"gather_w_down_{h}_sequencer")[0] for h in range(2)]
        self.reduces = {}
        self.pairs = {}

    def mixer_weights(self, after):
        return _w_in_from_super_slabs(self.in_ref[...]), None

    def conv_weight(self, after):
        g_conv = self.conv_ref[...]
        return jnp.concatenate([g_conv[i] for i in range(N_DEV)], axis=1)

    def out_weight(self, after):
        return self.out_ref[...].reshape(D_MODEL, D_MODEL)

    def up_weight(self, after):
        return self.up_ref[...]

    def down_weight(self, h, after):
        return self.down_refs[h][...].reshape(D_FF, D_MODEL // 2)

    def _chips_start(self, slabs, from_sibling, rows, tag):
        sums = [_pair_add(s, r, self.csel, tr, f"pair_add_{tag}_{i}")
                for i, (s, r, tr) in enumerate(zip(slabs, from_sibling, rows))]
        lands = [lax.empty((3,) + s.shape[1:], s.dtype) for s in sums]
        self.reduces[tag] = _remote_start(sums, lands, _chips_plan, 3 * len(sums), f"reduce_start_{tag}")
        return self.reduces[tag][4]

    def mlp_grads(self, h2, du, act, dx3b):
        def send(part, tag, after):
            st = _remote_start([part], [lax.empty(part.shape, part.dtype)], _pair4_plan, 4,
                               f"reduce_pair_start_{tag}", after=after)
            return st

        def received(st, after, tag):
            return _remote_wait(st, after, _pair4_plan, f"reduce_pair_wait_{tag}")[1][0]

        def to_chips(sums, tag):
            self.reduces[tag] = _remote_start([sums], [lax.empty((3,) + sums.shape[1:], sums.dtype)], _chips_plan, 3,
                                              f"reduce_start_{tag}")
            return self.reduces[tag][4]

        up_send = _grad_w_up(h2, du, "grad_w_up_send", sel=(self.csel, True))
        st_up = send(up_send, "up", None)
        down_send = _grad_w_down(act, dx3b, "grad_w_down_send", sel=(self.csel, True), after=st_up[4])
        st_down = send(down_send, "down", None)
        up_sum = _grad_w_up(h2, du, "grad_w_up_keep", sel=(self.csel, False), add=received(st_up, down_send, "up"),
                            after=st_down[4])
        token = to_chips(up_sum, "up")
        down_sum = _grad_w_down(act, dx3b, "grad_w_down_keep", sel=(self.csel, False),
                                add=received(st_down, up_sum, "down"), after=token)
        return to_chips(down_sum, "down")

    def _pair_start(self, slabs, tag):
        land = lax.empty((4,) + slabs.shape[1:], slabs.dtype)
        self.pairs[tag] = _remote_start([slabs], [land], _pair_plan, 4, f"reduce_pair_start_{tag}")
        return self.pairs[tag][4]

    def grad_sent(self, tag, after):
        slabs, from_sibling = _remote_wait(self.pairs[tag], after, _pair_plan, f"reduce_pair_wait_{tag}")
        return self._chips_start(slabs, from_sibling, [slabs[0].shape[1]], tag)

    def out_grad(self, g_out):
        return self._pair_start(g_out.reshape(N_DEV, D_MODEL // N_DEV, D_MODEL), "out")

    def in_grad(self, g_in):
        return self._pair_start(
            jnp.stack([_natural_rows(g_in, SUPER_STEP * j, SUPER_STEP * j + SUPER) for j in range(N_DEV)]), "in")

    def small_start(self, small):
        self.st_small = _remote_start([small], [_landing(small, self.me)], _everyone_plan, N_DEV - 1, "gather_start_small")

    def small_end(self, after):
        return _remote_wait(self.st_small, after, _everyone_plan, "gather_small_wait")[1][0]

    def reduce_end(self, tag, after):
        return _remote_wait(self.reduces[tag], after, _chips_plan, f"reduce_wait_{tag}")


def kernel(x, mix_norm_g, w_in, conv_w, conv_b, dt_bias, A_log, D_skip, ssm_norm_g, attn_sinks, attn_out_norm_g, w_out, mlp_norm_g, w_up, w_down, final_norm_g, loss_target, m_mix_norm_g, m_w_in, m_conv_w, m_conv_b, m_dt_bias, m_A_log, m_D_skip, m_ssm_norm_g, m_attn_sinks, m_attn_out_norm_g, m_w_out, m_mlp_norm_g, m_w_up, m_w_down, m_final_norm_g, v_mix_norm_g, v_w_in, v_conv_w, v_conv_b, v_dt_bias, v_A_log, v_D_skip, v_ssm_norm_g, v_attn_sinks, v_attn_out_norm_g, v_w_out, v_mlp_norm_g, v_w_up, v_w_down, v_final_norm_g):
    xi, yi, ci = _coords()
    me = 4 * xi + 2 * yi + ci
    csel = jnp.reshape(ci, (1,)).astype(jnp.int32)
    qsel = jnp.reshape(2 * xi + yi, (1,)).astype(jnp.int32)
    w = dict(mix_norm_g=mix_norm_g, conv_b=conv_b, dt_bias=dt_bias, A_log=A_log, D_skip=D_skip,
             ssm_norm_g=ssm_norm_g, attn_sinks=attn_sinks, attn_out_norm_g=attn_out_norm_g, mlp_norm_g=mlp_norm_g,
             final_norm_g=final_norm_g)
    hooks = _ShardedWeights(w_in[0], w_out[0], conv_w[0], w_up[0], w_down[0], me, csel)
    p = dict(w)
    dx, small = _local_step(x[0], loss_target[0], p, hooks)
    hooks.small_start(small)
    big = {}
    after = dx
    for name, wt, mt, vt, tile in [
            ("up", w_up, m_w_up, v_w_up, (512, SLAB)), ("down", w_down, m_w_down, v_w_down, (256, D_MODEL)),
            ("out", w_out, m_w_out, v_w_out, (256, D_MODEL))]:
        (chip_sums,), (from_chips,) = hooks.reduce_end(name, after)
        res = _adamw_big(wt[0], mt[0], vt[0], chip_sums, from_chips, qsel, tile, f"adamw_w_{name}")
        big["w_" + name] = tuple(r[None] for r in res)
        after = res[0]
    (chip_sums,), (from_chips,) = hooks.reduce_end("in", after)
    g_super = _sum_partials(chip_sums, from_chips, qsel, 512, "grad_w_in_sum")
    g_in = lax.dynamic_slice(g_super, (2 * me, 0), (PER, D_MODEL))
    res = _adamw_tiled(jnp.transpose(w_in[0]), g_in, jnp.transpose(m_w_in[0]), jnp.transpose(v_w_in[0]), 512,
                       "adamw_w_in")
    big["w_in"] = tuple(jnp.transpose(r)[None] for r in (g_in, *res))
    after = res[0]
    gsum = _small_sum(hooks.small_end(after), "small_sum")
    loss = gsum[9, 64]
    gs = _unpack_small(gsum, CONV_DIM)
    cw = CONV_DIM // N_DEV
    g_conv_shard = lax.dynamic_slice(gsum[5:9, :], (0, me * cw), (CONV_K, cw))

    def pack(s):
        return _pack_small(s["mix_norm_g"], s["conv_b"], s["ssm_norm_g"], s["attn_out_norm_g"], s["mlp_norm_g"],
                           s["final_norm_g"], s["conv_w"][0], s["dt_bias"], s["A_log"], s["D_skip"], s["attn_sinks"])

    wp = pack(dict(w, conv_w=conv_w))
    mp = pack(dict(mix_norm_g=m_mix_norm_g, conv_b=m_conv_b, ssm_norm_g=m_ssm_norm_g,
                   attn_out_norm_g=m_attn_out_norm_g, mlp_norm_g=m_mlp_norm_g, final_norm_g=m_final_norm_g,
                   conv_w=m_conv_w, dt_bias=m_dt_bias, A_log=m_A_log, D_skip=m_D_skip, attn_sinks=m_attn_sinks))
    vp = pack(dict(mix_norm_g=v_mix_norm_g, conv_b=v_conv_b, ssm_norm_g=v_ssm_norm_g,
                   attn_out_norm_g=v_attn_out_norm_g, mlp_norm_g=v_mlp_norm_g, final_norm_g=v_final_norm_g,
                   conv_w=v_conv_w, dt_bias=v_dt_bias, A_log=v_A_log, D_skip=v_D_skip, attn_sinks=v_attn_sinks))
    gp = jnp.concatenate([gsum[0:5], jnp.pad(g_conv_shard, ((0, 0), (0, D_MODEL - cw))), gsum[9:10],
                          jnp.zeros((SMALL_ROWS - 10, D_MODEL), F32)], axis=0)
    dp, mnp, vnp = _adamw_small(wp, gp, mp, vp, "adamw_small")
    grads = dict(gs, conv_w=g_conv_shard[None])
    deltas = _unpack_small(dp, cw)
    new_m = _unpack_small(mnp, cw)
    new_v = _unpack_small(vnp, cw)
    for k, name in enumerate(["w_in", "w_out", "w_up", "w_down"]):
        grads[name], deltas[name], new_m[name], new_v[name] = big[name]
    return (loss, dx[None], *[grads[n] for n in WEIGHT_ORDER], *[deltas[n] for n in WEIGHT_ORDER],
            *[new_m[n] for n in WEIGHT_ORDER], *[new_v[n] for n in WEIGHT_ORDER])
```

```python
import jax
import jax.numpy as jnp
from jax import lax
from jax.experimental import pallas as pl
from jax.experimental.pallas import tpu as pltpu
from jax.experimental.pallas import tpu_sc as plsc

F32 = jnp.float32
BF16 = jnp.bfloat16
MESH = pl.DeviceIdType.MESH

EPS = 1e-5
D_MODEL = 2048
D_INNER = 1024
N_HEADS = 16
HEAD_DIM = 64
N_GROUPS = 4
D_STATE = 128
CHUNK = 128
CONV_K = 4
CONV_DIM = 2048
ATTN_W = 1024
KV_W = 128
WINDOW = 128
D_FF = 8192
IN_PROJ = 4368
N_DEV = 8
NP = 4608
OFF_Z, OFF_X, OFF_B, OFF_C, OFF_Q, OFF_K, OFF_V, OFF_DT = 0, 1024, 2048, 2560, 3072, 4096, 4224, 4352
NAT_DT = 3072

ADAM_LR = 0.001
ADAM_B1 = 0.9
ADAM_B2 = 0.999
ADAM_EPS = 1e-08
ADAM_WD = 0.01
ADAM_STEP = 10

VMEM_LIMIT = 52 * 1024 * 1024
SMALL_ROWS = 16
NEG = -1e30


def _cparams(sem=None):
    return pltpu.CompilerParams(dimension_semantics=sem, vmem_limit_bytes=VMEM_LIMIT)


def _split3(v):
    hi = v.astype(BF16)
    rest = v - hi.astype(F32)
    mid = rest.astype(BF16)
    return hi, mid, (rest - mid.astype(F32)).astype(BF16)


def _hdot(a, b, data):
    if data == "a":
        sel = b.astype(BF16)
        return sum(_dot_nn(part, sel) for part in _split3(a))
    sel = a.astype(BF16)
    return sum(_dot_nn(sel, part) for part in _split3(b))


def _dot_nn(a, b):
    return lax.dot_general(a, b, (((1,), (0,)), ((), ())), preferred_element_type=F32)


def _dot_nt(a, b):
    return lax.dot_general(a, b, (((1,), (1,)), ((), ())), preferred_element_type=F32)


def _dot_tn(a, b):
    return lax.dot_general(a, b, (((0,), (0,)), ((), ())), preferred_element_type=F32)


def _softplus(v):
    return jnp.maximum(v, 0.0) + jnp.log1p(jnp.exp(-jnp.abs(v)))


def _sigmoid(v):
    return 1.0 / (1.0 + jnp.exp(-v))


def _matmul(a, b, *, mode, grid, a_spec, b_spec, out_shapes, out_specs, tile, name,
            extras=(), extra_specs=(), epilogue=None, after=None, dot_fn=None, prefetch=None):
    nk = grid[2]
    n_ex = len(extras)
    n_out = len(out_shapes)
    bs, b_specs = (b, b_spec) if isinstance(b, tuple) else ((b,), (b_spec,))
    n_in = 1 + len(bs)
    dot = dot_fn if dot_fn is not None else {"nn": _dot_nn, "nt": _dot_nt, "tn": _dot_tn}[mode]

    def finish(acc, ex_refs, out_refs):
        res = (acc,) if epilogue is None else epilogue(acc, *[e[...] for e in ex_refs])
        for o, r in zip(out_refs, res):
            o[...] = r.astype(o.dtype)

    def body(*refs):
        ex_refs = refs[n_in:n_in + n_ex]
        out_refs = refs[n_in + n_ex:n_in + n_ex + n_out]
        part = dot(*[r[...].astype(BF16) for r in refs[:n_in]])
        if nk == 1:
            finish(part, ex_refs, out_refs)
        else:
            acc_ref = refs[-1]
            k = pl.program_id(2)

            @pl.when(k == 0)
            def _():
                acc_ref[...] = part

            @pl.when(k > 0)
            def _():
                acc_ref[...] += part

            @pl.when(k == nk - 1)
            def _():
                finish(acc_ref[...], ex_refs, out_refs)

    scratch = [] if nk == 1 else [pltpu.VMEM(tile, F32)]
    n_pre = 0 if prefetch is None else 1
    tok_specs = [] if after is None else [pl.BlockSpec((8, 128), lambda *_: (0, 0))]
    tok_args = [] if after is None else [after]

    def body_with_token(*refs):
        refs = refs[n_pre:]
        body(*refs[:n_in + n_ex], *refs[n_in + n_ex + len(tok_args):])

    in_specs = [a_spec, *b_specs, *extra_specs, *tok_specs]
    params = _cparams(("parallel", "parallel", "arbitrary"))
    if prefetch is None:
        return pl.pallas_call(
            body_with_token, grid=grid, in_specs=in_specs, out_specs=list(out_specs), out_shape=list(out_shapes),
            scratch_shapes=scratch, name=name, compiler_params=params)(a, *bs, *extras, *tok_args)
    return pl.pallas_call(
        body_with_token,
        grid_spec=pltpu.PrefetchScalarGridSpec(num_scalar_prefetch=1, grid=grid, in_specs=in_specs,
                                               out_specs=list(out_specs), scratch_shapes=scratch),
        out_shape=list(out_shapes), name=name, compiler_params=params)(prefetch, a, *bs, *extras, *tok_args)


def _mm_simple(a, b, *, mode, M, N, K, tm, tn, tk, out_dtype, name, extras=(), epilogue=None, n_out=1,
               out_dtypes=None, after=None):
    grid = (M // tm, N // tn, K // tk)
    if mode == "nn":
        a_spec = pl.BlockSpec((tm, tk), lambda i, j, k: (i, k))
        b_spec = pl.BlockSpec((tk, tn), lambda i, j, k: (k, j))
    elif mode == "nt":
        a_spec = pl.BlockSpec((tm, tk), lambda i, j, k: (i, k))
        b_spec = pl.BlockSpec((tn, tk), lambda i, j, k: (j, k))
    else:
        a_spec = pl.BlockSpec((tk, tm), lambda i, j, k: (k, i))
        b_spec = pl.BlockSpec((tk, tn), lambda i, j, k: (k, j))
    o_spec = pl.BlockSpec((tm, tn), lambda i, j, k: (i, j))
    dts = out_dtypes if out_dtypes is not None else [out_dtype] * n_out
    return _matmul(a, b, mode=mode, grid=grid, a_spec=a_spec, b_spec=b_spec,
                   out_shapes=[jax.ShapeDtypeStruct((M, N), d) for d in dts],
                   out_specs=[o_spec] * len(dts), tile=(tm, tn), name=name,
                   extras=extras, extra_specs=[o_spec] * len(extras), epilogue=epilogue, after=after)


ROW_BLOCK = 256


def _rmsnorm_fwd(x, g, name):
    T, D = x.shape

    def body(x_ref, g_ref, o_ref):
        xf = x_ref[...]
        r = lax.rsqrt(jnp.mean(xf * xf, axis=-1, keepdims=True) + EPS)
        o_ref[...] = (xf * r * g_ref[...]).astype(BF16)

    return pl.pallas_call(
        body, grid=(T // ROW_BLOCK,),
        in_specs=[pl.BlockSpec((ROW_BLOCK, D), lambda i: (i, 0)), pl.BlockSpec((1, D), lambda i: (0, 0))],
        out_specs=pl.BlockSpec((ROW_BLOCK, D), lambda i: (i, 0)),
        out_shape=jax.ShapeDtypeStruct((T, D), BF16), name=name, compiler_params=_cparams(("parallel",)),
    )(x, g)


def _rmsnorm_bwd(dh, x, g, dres, name, with_bf16=True):
    T, D = x.shape

    def body(dh_ref, x_ref, g_ref, dres_ref, dx_ref, *rest):
        dg_ref = rest[-1]
        i = pl.program_id(0)
        xf = x_ref[...]
        r = lax.rsqrt(jnp.mean(xf * xf, axis=-1, keepdims=True) + EPS)
        xh = xf * r
        d = dh_ref[...]

        @pl.when(i == 0)
        def _():
            dg_ref[...] = jnp.zeros_like(dg_ref)

        dg_ref[...] += jnp.sum(d * xh, axis=0, keepdims=True)
        dxh = d * g_ref[...]
        dx = r * (dxh - xh * jnp.mean(dxh * xh, axis=-1, keepdims=True)) + dres_ref[...]
        dx_ref[...] = dx
        if with_bf16:
            rest[0][...] = dx.astype(BF16)

    row = pl.BlockSpec((ROW_BLOCK, D), lambda i: (i, 0))
    vec = pl.BlockSpec((1, D), lambda i: (0, 0))
    copies = [(row, jax.ShapeDtypeStruct((T, D), BF16))] if with_bf16 else []
    return pl.pallas_call(
        body, grid=(T // ROW_BLOCK,), in_specs=[row, row, vec, row],
        out_specs=[row, *[c[0] for c in copies], vec],
        out_shape=[jax.ShapeDtypeStruct((T, D), F32), *[c[1] for c in copies], jax.ShapeDtypeStruct((1, D), F32)],
        name=name, compiler_params=_cparams(("arbitrary",)),
    )(dh, x, g, dres)


def _final_loss(x3_halves, tgt, g, name):
    T, D = tgt.shape

    def body(xa_ref, xb_ref, t_ref, g_ref, loss_ref, dg_ref, dx_ref, dxb_ref):
        i = pl.program_id(0)
        xf = jnp.concatenate([xa_ref[...], xb_ref[...]], axis=1)
        r = lax.rsqrt(jnp.mean(xf * xf, axis=-1, keepdims=True) + EPS)
        xh = xf * r
        gg = g_ref[...]
        err = xh * gg - t_ref[...]

        @pl.when(i == 0)
        def _():
            dg_ref[...] = jnp.zeros_like(dg_ref)
            loss_ref[...] = jnp.zeros_like(loss_ref)

        part = jnp.sum(jnp.sum(err * err, axis=-1, keepdims=True), axis=0, keepdims=True) * (0.5 / D)
        loss_ref[...] += jnp.broadcast_to(part, loss_ref.shape)
        dout = err * (1.0 / D)
        dg_ref[...] += jnp.sum(dout * xh, axis=0, keepdims=True)
        dxh = dout * gg
        dx = r * (dxh - xh * jnp.mean(dxh * xh, axis=-1, keepdims=True))
        dx_ref[...] = dx
        dxb_ref[...] = dx.astype(BF16)

    row = pl.BlockSpec((ROW_BLOCK, D), lambda i: (i, 0))
    vec = pl.BlockSpec((1, D), lambda i: (0, 0))
    return pl.pallas_call(
        body, grid=(T // ROW_BLOCK,),
        in_specs=[pl.BlockSpec((ROW_BLOCK, D // 2), lambda i: (i, 0))] * 2 + [row, vec],
        out_specs=[pl.BlockSpec((1, 128), lambda i: (0, 0)), vec, row, row],
        out_shape=[jax.ShapeDtypeStruct((1, 128), F32), jax.ShapeDtypeStruct((1, D), F32),
                   jax.ShapeDtypeStruct((T, D), F32), jax.ShapeDtypeStruct((T, D), BF16)],
        name=name, compiler_params=_cparams(("arbitrary",)),
    )(*x3_halves, tgt, g)


CONV_BLOCK = 256


def _conv_apply(u, w, b):
    row = lax.broadcasted_iota(jnp.int32, u.shape, 0)
    acc = b + w[CONV_K - 1:CONV_K, :] * u
    shifted = []
    for j in range(1, CONV_K):
        uj = jnp.where(row >= j, pltpu.roll(u, j, axis=0), 0.0)
        shifted.append(uj)
        acc = acc + w[CONV_K - 1 - j:CONV_K - j, :] * uj
    return acc, shifted


def _conv_fwd(proj, conv_w, conv_b, name):
    T = proj.shape[0]
    cb0 = OFF_X // CONV_BLOCK

    def body(u_ref, w_ref, b_ref, o_ref, ds_ref):
        c, _ = _conv_apply(u_ref[...], w_ref[...], b_ref[...])
        sg = _sigmoid(c)
        o_ref[...] = c * sg
        ds_ref[...] = sg * (1.0 + c * (1.0 - sg))

    out = pl.BlockSpec((T, CONV_BLOCK), lambda j: (0, j))
    return pl.pallas_call(
        body, grid=(CONV_DIM // CONV_BLOCK,),
        in_specs=[pl.BlockSpec((T, CONV_BLOCK), lambda j: (0, cb0 + j)),
                  pl.BlockSpec((CONV_K, CONV_BLOCK), lambda j: (0, j)),
                  pl.BlockSpec((1, CONV_BLOCK), lambda j: (0, j))],
        out_specs=[out, out], out_shape=[jax.ShapeDtypeStruct((T, CONV_DIM), F32)] * 2,
        name=name, compiler_params=_cparams(("parallel",)),
    )(proj, conv_w, conv_b)


def _conv_bwd(proj, dact, dsilu, conv_w, dproj, name):
    T = proj.shape[0]
    cb0 = OFF_X // CONV_BLOCK

    def body(u_ref, d_ref, s_ref, w_ref, _, du_ref, dw_ref, db_ref):
        u = u_ref[...]
        w = w_ref[...]
        dc = d_ref[...] * s_ref[...]
        row = lax.broadcasted_iota(jnp.int32, u.shape, 0)
        du = w[CONV_K - 1:CONV_K, :] * dc
        dw_ref[CONV_K - 1:CONV_K, :] = jnp.sum(dc * u, axis=0, keepdims=True)
        for j in range(1, CONV_K):
            dcj = jnp.where(row < T - j, pltpu.roll(dc, T - j, axis=0), 0.0)
            du = du + w[CONV_K - 1 - j:CONV_K - j, :] * dcj
            dw_ref[CONV_K - 1 - j:CONV_K - j, :] = jnp.sum(dcj * u, axis=0, keepdims=True)
        db_ref[...] = jnp.sum(dc, axis=0, keepdims=True)
        du_ref[...] = du.astype(BF16)

    blk = pl.BlockSpec((T, CONV_BLOCK), lambda j: (0, j))
    return pl.pallas_call(
        body, grid=(CONV_DIM // CONV_BLOCK,),
        in_specs=[pl.BlockSpec((T, CONV_BLOCK), lambda j: (0, cb0 + j)), blk, blk,
                  pl.BlockSpec((CONV_K, CONV_BLOCK), lambda j: (0, j)), pl.BlockSpec(memory_space=pl.ANY)],
        out_specs=[pl.BlockSpec((T, CONV_BLOCK), lambda j: (0, cb0 + j)),
                   pl.BlockSpec((CONV_K, CONV_BLOCK), lambda j: (0, j)),
                   pl.BlockSpec((1, CONV_BLOCK), lambda j: (0, j))],
        out_shape=[jax.ShapeDtypeStruct(dproj.shape, BF16), jax.ShapeDtypeStruct((CONV_K, CONV_DIM), F32),
                   jax.ShapeDtypeStruct((1, CONV_DIM), F32)],
        input_output_aliases={4: 0}, name=name, compiler_params=_cparams(("parallel",)),
    )(proj, dact, dsilu, conv_w, dproj)


GROUP_W = D_INNER // N_GROUPS
HEADS_PER_GROUP = N_HEADS // N_GROUPS


def _expand_mat():
    h = lax.broadcasted_iota(jnp.int32, (N_HEADS, D_INNER), 0)
    j = lax.broadcasted_iota(jnp.int32, (N_HEADS, D_INNER), 1)
    return (j // HEAD_DIM == h).astype(F32)


def _reduce_mat(g):
    j = lax.broadcasted_iota(jnp.int32, (GROUP_W, N_HEADS), 0)
    h = lax.broadcasted_iota(jnp.int32, (GROUP_W, N_HEADS), 1)
    return (g * HEADS_PER_GROUP + j // HEAD_DIM == h).astype(F32)


def _col16(v, h):
    lane = lax.broadcasted_iota(jnp.int32, v.shape, 1)
    return jnp.sum(jnp.where(lane == h, v, 0.0), axis=1, keepdims=True)


def _ssd_pre(dt_raw, dtT_raw, dtb, dtbT, alog, alogT):
    Q = CHUNK
    xdt = dt_raw + dtb
    dt = _softplus(xdt)
    dtT = _softplus(dtT_raw + dtbT)
    A = -jnp.exp(alog)
    AT = -jnp.exp(alogT)
    row = lax.broadcasted_iota(jnp.int32, (Q, Q), 0)
    col = lax.broadcasted_iota(jnp.int32, (Q, Q), 1)
    tril = (row >= col).astype(F32)
    triu = (row <= col).astype(F32)
    cs = _hdot(tril, dt * A, "b")
    csT = _hdot(dtT * AT, triu, "a")
    return xdt, dt, A, cs, csT, row >= col, triu


def _decay_matrix(cs, csT, h, causal):
    seg = _col16(cs, h) - csT[h:h + 1, :]
    return jnp.where(causal, jnp.exp(jnp.minimum(seg, 0.0)), 0.0)


def _ssd_in_specs(nc, rev):
    def cidx(c):
        return (nc - 1 - c) if rev else c

    return [
        pl.BlockSpec((CHUNK, D_INNER), lambda c: (cidx(c), 0)),
        pl.BlockSpec((CHUNK, 512), lambda c: (cidx(c), 2)),
        pl.BlockSpec((CHUNK, 512), lambda c: (cidx(c), 3)),
        pl.BlockSpec((CHUNK, D_INNER), lambda c: (cidx(c), 0)),
        pl.BlockSpec((CHUNK, 128), lambda c: (cidx(c), OFF_DT // 128)),
        pl.BlockSpec((N_HEADS, CHUNK), lambda c: (0, cidx(c))),
        pl.BlockSpec((1, N_HEADS), lambda c: (0, 0)),
        pl.BlockSpec((N_HEADS, 1), lambda c: (0, 0)),
        pl.BlockSpec((1, N_HEADS), lambda c: (0, 0)),
        pl.BlockSpec((N_HEADS, 1), lambda c: (0, 0)),
        pl.BlockSpec((1, D_INNER), lambda c: (0, 0)),
        pl.BlockSpec((1, D_INNER), lambda c: (0, 0)),
    ]


def _ssd_fwd(xbc, proj, dtT, dtb, dtbT, alog, alogT, dfull, ng, name):
    T = xbc.shape[0]
    nc = T // CHUNK
    Q = CHUNK

    def body(xs_ref, B_ref, C_ref, z_ref, dt_ref, dtT_ref, dtb_ref, dtbT_ref, al_ref, alT_ref, df_ref, ng_ref,
             y_ref, ypre_ref, hs_ref, h_scr):
        c = pl.program_id(0)

        @pl.when(c == 0)
        def _():
            h_scr[...] = jnp.zeros_like(h_scr)

        _, dt, _, cs, csT, causal, _ = _ssd_pre(dt_ref[:, :N_HEADS], dtT_ref[...], dtb_ref[...], dtbT_ref[...],
                                                al_ref[...], alT_ref[...])
        ex = _expand_mat()
        dt_full = _hdot(dt, ex, "a")
        cs_full = _hdot(cs, ex, "a")
        cs_last = cs_full[Q - 1:Q, :]
        xs = xs_ref[...]
        xd = xs * dt_full
        e_full = jnp.exp(cs_full)
        dec_full = jnp.exp(cs_last - cs_full)
        cd_full = jnp.exp(cs_last)
        lane_head = lax.broadcasted_iota(jnp.int32, (1, GROUP_W), 1) // HEAD_DIM
        for g in range(N_GROUPS):
            sl = slice(g * GROUP_W, (g + 1) * GROUP_W)
            Bg = B_ref[:, g * D_STATE:(g + 1) * D_STATE].astype(BF16)
            Cg = C_ref[:, g * D_STATE:(g + 1) * D_STATE].astype(BF16)
            CB = _dot_nt(Cg, Bg)
            hg = h_scr[g]
            yoff = _dot_nn(Cg, hg.astype(BF16)) * e_full[:, sl]
            xd_g = xd[:, sl]
            S = _dot_tn(Bg, (xd_g * dec_full[:, sl]).astype(BF16))
            xd_b = xd_g.astype(BF16)
            ydiag = jnp.zeros((Q, GROUP_W), F32)
            for r in range(HEADS_PER_GROUP):
                Lm = _decay_matrix(cs, csT, g * HEADS_PER_GROUP + r, causal)
                Gm = (CB * Lm).astype(BF16)
                ydiag = ydiag + _dot_nn(Gm, jnp.where(lane_head == r, xd_b, jnp.zeros_like(xd_b)))
            hs_ref[0, g] = hg
            h_scr[g] = hg * cd_full[:, sl] + S
            ypre = ydiag + yoff + xs[:, sl] * df_ref[:, sl]
            ypre_ref[:, sl] = ypre
            zg = z_ref[:, sl]
            yz = ypre * zg * _sigmoid(zg)
            rn = lax.rsqrt(jnp.mean(yz * yz, axis=-1, keepdims=True) + EPS)
            y_ref[:, sl] = (yz * rn * ng_ref[:, sl]).astype(BF16)

    return pl.pallas_call(
        body, grid=(nc,), in_specs=_ssd_in_specs(nc, False),
        out_specs=[pl.BlockSpec((CHUNK, D_INNER), lambda c: (c, 0)),
                   pl.BlockSpec((CHUNK, D_INNER), lambda c: (c, 0)),
                   pl.BlockSpec((1, N_GROUPS, D_STATE, GROUP_W), lambda c: (c, 0, 0, 0))],
        out_shape=[jax.ShapeDtypeStruct((T, D_INNER + ATTN_W), BF16), jax.ShapeDtypeStruct((T, D_INNER), F32),
                   jax.ShapeDtypeStruct((nc, N_GROUPS, D_STATE, GROUP_W), F32)],
        scratch_shapes=[pltpu.VMEM((N_GROUPS, D_STATE, GROUP_W), F32)],
        name=name, compiler_params=_cparams(("arbitrary",)),
    )(xbc, xbc, xbc, proj, proj, dtT, dtb, dtbT, alog, alogT, dfull, ng)


def _ssd_bwd(xbc, proj, dtT, dtb, dtbT, alog, alogT, dfull, ng, ypre, hs, dy, name):
    T = xbc.shape[0]
    nc = T // CHUNK
    Q = CHUNK

    def body(xs_ref, B_ref, C_ref, z_ref, dt_ref, dtT_ref, dtb_ref, dtbT_ref, al_ref, alT_ref, df_ref, ng_ref,
             ypre_ref, hs_ref, dy_ref,
             dz_ref, dxbc_ref, ddtb_ref, dal_ref, dD_ref, dng_ref, dh_scr):
        step = pl.program_id(0)

        @pl.when(step == 0)
        def _():
            dh_scr[...] = jnp.zeros_like(dh_scr)
            ddtb_ref[...] = jnp.zeros_like(ddtb_ref)
            dal_ref[...] = jnp.zeros_like(dal_ref)
            dD_ref[...] = jnp.zeros_like(dD_ref)
            dng_ref[...] = jnp.zeros_like(dng_ref)

        xdt, dt, A, cs, csT, causal, triu = _ssd_pre(dt_ref[:, :N_HEADS], dtT_ref[...], dtb_ref[...],
                                                    dtbT_ref[...], al_ref[...], alT_ref[...])
        ex = _expand_mat()
        dt_full = _hdot(dt, ex, "a")
        cs_full = _hdot(cs, ex, "a")
        cs_last = cs_full[Q - 1:Q, :]
        xs = xs_ref[...]
        xd = xs * dt_full
        e_full = jnp.exp(cs_full)
        dec_full = jnp.exp(cs_last - cs_full)
        cd_full = jnp.exp(cs_last)
        lane_head = lax.broadcasted_iota(jnp.int32, (1, GROUP_W), 1) // HEAD_DIM
        is_last = lax.broadcasted_iota(jnp.int32, (Q, 1), 0) == Q - 1
        dcs16 = jnp.zeros((Q, N_HEADS), F32)
        ddtx16 = jnp.zeros((Q, N_HEADS), F32)
        dD16 = jnp.zeros((8, N_HEADS), F32)
        lane16 = lax.broadcasted_iota(jnp.int32, (1, N_HEADS), 1)
        sub16 = lax.broadcasted_iota(jnp.int32, (N_HEADS, 1), 0)
        col_sums = jnp.zeros((N_HEADS, Q), F32)
        for g in range(N_GROUPS):
            sl = slice(g * GROUP_W, (g + 1) * GROUP_W)
            red = _reduce_mat(g)
            ypre_g = ypre_ref[:, sl]
            zg = z_ref[:, sl]
            sg = _sigmoid(zg)
            silu = zg * sg
            yz = ypre_g * silu
            rn = lax.rsqrt(jnp.mean(yz * yz, axis=-1, keepdims=True) + EPS)
            yh = yz * rn
            dy_g = dy_ref[:, sl]
            dng_ref[:, sl] += jnp.sum(dy_g * yh, axis=0, keepdims=True)
            dyh = dy_g * ng_ref[:, sl]
            dyz = rn * (dyh - yh * jnp.mean(dyh * yh, axis=-1, keepdims=True))
            dY = dyz * silu
            dz_ref[:, sl] = (dyz * ypre_g * sg * (1.0 + zg * (1.0 - sg))).astype(BF16)
            xs_g = xs[:, sl]
            xd_g = xd[:, sl]
            dec_g = dec_full[:, sl]
            cd_g = cd_full[:, sl]
            d_g = df_ref[:, sl]
            Bg = B_ref[:, g * D_STATE:(g + 1) * D_STATE].astype(BF16)
            Cg = C_ref[:, g * D_STATE:(g + 1) * D_STATE].astype(BF16)
            CB = _dot_nt(Cg, Bg)
            hg = hs_ref[0, g]
            hgb = hg.astype(BF16)
            yoff = _dot_nn(Cg, hgb) * e_full[:, sl]
            dhn = dh_scr[g]
            dhnb = dhn.astype(BF16)
            dYE = (dY * e_full[:, sl]).astype(BF16)
            dC = _dot_nt(dYE, hgb)
            dh_direct = _dot_tn(Cg, dYE)
            dXdd = _dot_nn(Bg, dhnb)
            dB = _dot_nt((xd_g * dec_g).astype(BF16), dhnb)
            dcd = jnp.sum(dhn * hg, axis=0, keepdims=True)
            dh_scr[g] = dh_direct + cd_g * dhn
            dYb = dY.astype(BF16)
            xd_b = xd_g.astype(BF16)
            dCB = jnp.zeros((Q, Q), F32)
            dXd = dXdd * dec_g
            for r in range(HEADS_PER_GROUP):
                h = g * HEADS_PER_GROUP + r
                Lm = _decay_matrix(cs, csT, h, causal)
                Gf = CB * Lm
                dYr = jnp.where(lane_head == r, dYb, jnp.zeros_like(dYb))
                dG = _dot_nt(dYr, xd_b)
                dCB = dCB + dG * Lm
                dXd = dXd + _dot_tn(Gf.astype(BF16), dYr)
                Mm = dG * Gf
                dcs16 = dcs16 + jnp.where(lane16 == h, jnp.sum(Mm, axis=1, keepdims=True), 0.0)
                col_sums = col_sums + jnp.where(sub16 == h, jnp.sum(Mm, axis=0, keepdims=True), 0.0)
            dCBb = dCB.astype(BF16)
            dC = dC + _dot_nn(dCBb, Bg)
            dB = dB + _dot_tn(dCBb, Cg)
            w_state = dXdd * dec_g * xd_g
            t_last = jnp.sum(w_state, axis=0, keepdims=True) + dcd * cd_g
            dcs_g = dY * yoff - w_state + jnp.where(is_last, t_last, 0.0)
            dcs16 = dcs16 + _hdot(dcs_g, red, "a")
            ddtx16 = ddtx16 + _hdot(dXd * xs_g, red, "a")
            dD16 = dD16 + _hdot(jnp.broadcast_to(jnp.sum(dY * xs_g, axis=0, keepdims=True), (8, GROUP_W)), red, "a")
            dxbc_ref[:, sl] = dXd * dt_full[:, sl] + dY * d_g
            dxbc_ref[:, D_INNER + g * D_STATE:D_INNER + (g + 1) * D_STATE] = dB
            dxbc_ref[:, D_INNER + 512 + g * D_STATE:D_INNER + 512 + (g + 1) * D_STATE] = dC
        eye = (lax.broadcasted_iota(jnp.int32, (N_HEADS, N_HEADS), 0)
               == lax.broadcasted_iota(jnp.int32, (N_HEADS, N_HEADS), 1)).astype(BF16)
        dcs16 = dcs16 - sum(_dot_tn(part, eye) for part in _split3(col_sums))
        da = _hdot(triu, dcs16, "b")
        ddt = da * A + ddtx16
        ddt_raw = ddt * _sigmoid(xdt)
        pr = lax.broadcasted_iota(jnp.int32, (N_HEADS, 128), 0)
        pc = lax.broadcasted_iota(jnp.int32, (N_HEADS, 128), 1)
        dz_ref[:, D_INNER:OFF_DT] = jnp.zeros((Q, OFF_DT - D_INNER), BF16)
        dz_ref[:, OFF_DT:OFF_DT + 128] = _hdot(ddt_raw, (pr == pc).astype(F32), "a").astype(BF16)
        dz_ref[:, OFF_DT + 128:] = jnp.zeros((Q, NP - OFF_DT - 128), BF16)
        ddtb_ref[...] += jnp.sum(ddt_raw, axis=0, keepdims=True)
        dal_ref[...] += jnp.sum(da * dt, axis=0, keepdims=True) * A
        dD_ref[...] += dD16[0:1, :]

    def rc(c):
        return nc - 1 - c

    in_specs = _ssd_in_specs(nc, True) + [
        pl.BlockSpec((CHUNK, D_INNER), lambda c: (rc(c), 0)),
        pl.BlockSpec((1, N_GROUPS, D_STATE, GROUP_W), lambda c: (rc(c), 0, 0, 0)),
        pl.BlockSpec((CHUNK, D_INNER), lambda c: (rc(c), 0)),
    ]
    small = pl.BlockSpec((1, N_HEADS), lambda c: (0, 0))
    return pl.pallas_call(
        body, grid=(nc,), in_specs=in_specs,
        out_specs=[pl.BlockSpec((CHUNK, NP), lambda c: (rc(c), 0)),
                   pl.BlockSpec((CHUNK, CONV_DIM), lambda c: (rc(c), 0)),
                   small, small, small,
                   pl.BlockSpec((1, D_INNER), lambda c: (0, 0))],
        out_shape=[jax.ShapeDtypeStruct((T, NP), BF16), jax.ShapeDtypeStruct((T, CONV_DIM), F32),
                   jax.ShapeDtypeStruct((1, N_HEADS), F32), jax.ShapeDtypeStruct((1, N_HEADS), F32),
                   jax.ShapeDtypeStruct((1, N_HEADS), F32), jax.ShapeDtypeStruct((1, D_INNER), F32)],
        scratch_shapes=[pltpu.VMEM((N_GROUPS, D_STATE, GROUP_W), F32)],
        name=name, compiler_params=_cparams(("arbitrary",)),
    )(xbc, xbc, xbc, proj, proj, dtT, dtb, dtbT, alog, alogT, dfull, ng, ypre, hs, dy)


N_PAIRS = ATTN_W // 128
PAIRS_PER_KV = N_PAIRS // 2
ATTN_SCALE = HEAD_DIM ** -0.5


def _kv_variants(kk):
    lo = lax.broadcasted_iota(jnp.int32, kk.shape, 1) < HEAD_DIM
    zero = jnp.zeros_like(kk)
    k00 = jnp.where(lo, kk, zero)
    k11 = jnp.where(lo, zero, kk)
    k01 = pltpu.roll(k00, HEAD_DIM, axis=1)
    k10 = pltpu.roll(k11, HEAD_DIM, axis=1)
    return [[k00.astype(BF16), k01.astype(BF16)], [k10.astype(BF16), k11.astype(BF16)]]


LOG2E = 1.4426950408889634


def _own_block():
    i = lax.broadcasted_iota(jnp.int32, (WINDOW, WINDOW), 0)
    j = lax.broadcasted_iota(jnp.int32, (WINDOW, WINDOW), 1)
    return j <= i


def _fold(own, a):
    return jnp.where(own, a[:, WINDOW:], a[:, :WINDOW])


def _attn_probs(qp, kvar, own, prev_bias, sk):
    s = _dot_nt(qp, kvar)
    sb = jnp.where(own, s[:, WINDOW:], s[:, :WINDOW] + prev_bias) * (ATTN_SCALE * LOG2E)
    sk2 = sk * LOG2E
    m = jnp.maximum(jnp.max(sb, axis=1, keepdims=True), sk2)
    pe = jnp.exp2(sb - m)
    es = jnp.exp2(sk2 - m)
    den = jnp.sum(pe, axis=1, keepdims=True) + es
    inv = 1.0 / den
    return pe * inv, es * inv


def _unfold(own, a):
    zero = jnp.zeros_like(a)
    return jnp.where(own, zero, a), jnp.where(own, a, zero)


def _sink(sinks, r):
    lane = lax.broadcasted_iota(jnp.int32, sinks.shape, 1)
    return jnp.sum(jnp.where(lane == r, sinks, 0.0), axis=1, keepdims=True)


def _kv_specs():
    return [pl.BlockSpec((WINDOW, KV_W), lambda n: (jnp.maximum(n - 1, 0), OFF_K // KV_W)),
            pl.BlockSpec((WINDOW, KV_W), lambda n: (n, OFF_K // KV_W)),
            pl.BlockSpec((WINDOW, KV_W), lambda n: (jnp.maximum(n - 1, 0), OFF_V // KV_W)),
            pl.BlockSpec((WINDOW, KV_W), lambda n: (n, OFF_V // KV_W))]


def _attn_fwd(proj, sinks, og, ycat, name):
    T = proj.shape[0]
    nb = T // WINDOW

    def body(q_ref, kp_ref, kc_ref, vp_ref, vc_ref, s_ref, og_ref, _, y_ref, o_ref, p_ref, ps_ref):
        n = pl.program_id(0)
        kv = _kv_variants(jnp.concatenate([kp_ref[...], kc_ref[...]], axis=0))
        vv = _kv_variants(jnp.concatenate([vp_ref[...], vc_ref[...]], axis=0))
        own = _own_block()
        prev_bias = jnp.where(n > 0, 0.0, NEG)
        sinks_v = s_ref[...]
        lane = lax.broadcasted_iota(jnp.int32, (1, 128), 1)
        ssq = jnp.zeros((WINDOW, 1), F32)
        sink_probs = jnp.zeros((WINDOW, 128), F32)
        for p in range(N_PAIRS):
            j = p // PAIRS_PER_KV
            qp = q_ref[:, p * 128:(p + 1) * 128].astype(BF16)
            o_pair = jnp.zeros((WINDOW, 128), F32)
            for par in range(2):
                r = 2 * p + par
                pn, ps = _attn_probs(qp, kv[j][par], own, prev_bias, _sink(sinks_v, r))
                pb = pn.astype(BF16)
                p_ref[:, r * 128:(r + 1) * 128] = pb
                sink_probs = jnp.where(lane == r, ps, sink_probs)
                p_prev, p_own = _unfold(own, pb)
                o_pair = o_pair + _dot_nn(p_prev, vv[j][par][:WINDOW]) + _dot_nn(p_own, vv[j][par][WINDOW:])
            o_ref[:, p * 128:(p + 1) * 128] = o_pair
            ssq = ssq + jnp.sum(o_pair * o_pair, axis=1, keepdims=True)
        ps_ref[...] = sink_probs
        rn = lax.rsqrt(ssq * (1.0 / ATTN_W) + EPS)
        y_ref[...] = (o_ref[...] * rn * og_ref[...]).astype(BF16)

    return pl.pallas_call(
        body, grid=(nb,),
        in_specs=[pl.BlockSpec((WINDOW, ATTN_W), lambda n: (n, OFF_Q // ATTN_W)), *_kv_specs(),
                  pl.BlockSpec((1, N_HEADS), lambda n: (0, 0)), pl.BlockSpec((1, ATTN_W), lambda n: (0, 0)), ANY],
        out_specs=[pl.BlockSpec((WINDOW, ATTN_W), lambda n: (n, 1)), pl.BlockSpec((WINDOW, ATTN_W), lambda n: (n, 0)),
                   pl.BlockSpec((WINDOW, N_HEADS * 128), lambda n: (n, 0)), pl.BlockSpec((WINDOW, 128), lambda n: (n, 0))],
        out_shape=[jax.ShapeDtypeStruct(ycat.shape, BF16), jax.ShapeDtypeStruct((T, ATTN_W), F32),
                   jax.ShapeDtypeStruct((T, N_HEADS * 128), BF16), jax.ShapeDtypeStruct((T, 128), F32)],
        input_output_aliases={7: 0}, name=name, compiler_params=_cparams(("parallel",)),
    )(proj, proj, proj, proj, proj, sinks, og, ycat)


def _attn_bwd(proj, og, o, dy, probs, sink_probs, dproj, name):
    T = proj.shape[0]
    nb = T // WINDOW

    def body(q_ref, kp_ref, kc_ref, vp_ref, vc_ref, og_ref, o_ref, dy_ref, p_ref, ps_ref, _,
             dq_ref, dk_ref, dv_ref, ds_ref, dog_ref, qt_scr, dot_scr, ds_scr, p_scr):
        n = pl.program_id(0)

        @pl.when(n == 0)
        def _():
            dk_ref[...] = jnp.zeros_like(dk_ref)
            dv_ref[...] = jnp.zeros_like(dv_ref)
            ds_ref[...] = jnp.zeros_like(ds_ref)
            dog_ref[...] = jnp.zeros_like(dog_ref)

        kv = _kv_variants(jnp.concatenate([kp_ref[...], kc_ref[...]], axis=0))
        vv = _kv_variants(jnp.concatenate([vp_ref[...], vc_ref[...]], axis=0))
        own = _own_block()
        sink_probs_v = ps_ref[...]
        of = o_ref[...]
        rn = lax.rsqrt(jnp.mean(of * of, axis=-1, keepdims=True) + EPS)
        oh = of * rn
        dyf = dy_ref[...]
        dog_ref[...] += jnp.sum(dyf * oh, axis=0, keepdims=True)
        doh = dyf * og_ref[...]
        do = rn * (doh - oh * jnp.mean(doh * oh, axis=-1, keepdims=True))
        lane = lax.broadcasted_iota(jnp.int32, (1, 128), 1)
        lane16 = lax.broadcasted_iota(jnp.int32, (1, N_HEADS), 1)
        dsink = jnp.zeros((1, N_HEADS), F32)
        for p in range(N_PAIRS):
            j = p // PAIRS_PER_KV
            q_t = q_ref[:, p * 128:(p + 1) * 128].T.astype(BF16)
            do_p = do[:, p * 128:(p + 1) * 128]
            o_p = of[:, p * 128:(p + 1) * 128]
            do_b = do_p.astype(BF16)
            do_t = do_p.T.astype(BF16)
            prod = do_p * o_p
            dq_pair = jnp.zeros((WINDOW, 128), F32)
            for par in range(2):
                r = 2 * p + par
                half = (lane < HEAD_DIM) if par == 0 else (lane >= HEAD_DIM)
                pb = p_ref[:, r * 128:(r + 1) * 128]
                ps = jnp.sum(jnp.where(lane == r, sink_probs_v, 0.0), axis=1, keepdims=True)
                delta = jnp.sum(jnp.where(half, prod, 0.0), axis=1, keepdims=True)
                dP = _fold(own, _dot_nt(do_b, vv[j][par]))
                dS = pb.astype(F32) * (dP - delta)
                dsink = dsink + jnp.where(lane16 == r, -jnp.sum(ps * delta, axis=0, keepdims=True), 0.0)
                dS_parts = _unfold(own, (dS * ATTN_SCALE).astype(BF16))
                p_parts = _unfold(own, pb)
                at = ((p % PAIRS_PER_KV) * 2 + par) * WINDOW
                qt_scr[j, :, at:at + WINDOW] = q_t[par * HEAD_DIM:(par + 1) * HEAD_DIM]
                dot_scr[j, :, at:at + WINDOW] = do_t[par * HEAD_DIM:(par + 1) * HEAD_DIM]
                for blk in range(2):
                    dq_pair = dq_pair + _dot_nn(dS_parts[blk], kv[j][par][blk * WINDOW:(blk + 1) * WINDOW])
                    ds_scr[j, blk, at:at + WINDOW, :] = dS_parts[blk]
                    p_scr[j, blk, at:at + WINDOW, :] = p_parts[blk]
            dq_ref[:, p * 128:(p + 1) * 128] = dq_pair.astype(BF16)
        rows = [pl.multiple_of(jnp.maximum(n - 1, 0) * WINDOW, WINDOW), pl.multiple_of(n * WINDOW, WINDOW)]
        for lhs, rhs, ref in [(qt_scr, ds_scr, dk_ref), (dot_scr, p_scr, dv_ref)]:
            for blk in range(2):
                both_t = jnp.concatenate([_dot_nn(lhs[j], rhs[j, blk]) for j in range(2)], axis=0)
                ref[pl.ds(rows[blk], WINDOW), :] += both_t.T
        ds_ref[...] += dsink

    full_kv = pl.BlockSpec((T, KV_W), lambda n: (0, 0))
    blk = pl.BlockSpec((WINDOW, ATTN_W), lambda n: (n, 0))
    return pl.pallas_call(
        body, grid=(nb,),
        in_specs=[pl.BlockSpec((WINDOW, ATTN_W), lambda n: (n, OFF_Q // ATTN_W)), *_kv_specs(),
                  pl.BlockSpec((1, ATTN_W), lambda n: (0, 0)), blk, pl.BlockSpec((WINDOW, ATTN_W), lambda n: (n, 1)),
                  pl.BlockSpec((WINDOW, N_HEADS * 128), lambda n: (n, 0)),
                  pl.BlockSpec((WINDOW, 128), lambda n: (n, 0)), ANY],
        out_specs=[pl.BlockSpec((WINDOW, ATTN_W), lambda n: (n, OFF_Q // ATTN_W)), full_kv, full_kv,
                   pl.BlockSpec((1, N_HEADS), lambda n: (0, 0)), pl.BlockSpec((1, ATTN_W), lambda n: (0, 0))],
        out_shape=[jax.ShapeDtypeStruct(dproj.shape, BF16), jax.ShapeDtypeStruct((T, KV_W), F32),
                   jax.ShapeDtypeStruct((T, KV_W), F32), jax.ShapeDtypeStruct((1, N_HEADS), F32),
                   jax.ShapeDtypeStruct((1, ATTN_W), F32)],
        scratch_shapes=[pltpu.VMEM((2, HEAD_DIM, 8 * WINDOW), BF16), pltpu.VMEM((2, HEAD_DIM, 8 * WINDOW), BF16),
                        pltpu.VMEM((2, 2, 8 * WINDOW, WINDOW), BF16), pltpu.VMEM((2, 2, 8 * WINDOW, WINDOW), BF16)],
        input_output_aliases={10: 0}, name=name, compiler_params=_cparams(("arbitrary",)),
    )(proj, proj, proj, proj, proj, og, o, dy, probs, sink_probs, dproj)


ANY = pl.BlockSpec(memory_space=pl.ANY)


def _coords():
    return lax.axis_index("x"), lax.axis_index("y"), lax.axis_index("c")


HBM = pl.BlockSpec(memory_space=pltpu.HBM)
SEM = pl.BlockSpec(memory_space=pltpu.SEMAPHORE)
EFFECT = pltpu.SideEffectType.DATAFLOW_SIDE_EFFECTING


def _in_hbm(a):
    return pltpu.with_memory_space_constraint(a, pltpu.HBM)


def _remote_start(srcs, lands, plan, n_copies, name, after=None):
    ns, nb = len(srcs), len(srcs) + len(lands)
    n_after = 0 if after is None else 1

    def body(*refs):
        src_refs, land_refs = refs[:ns], refs[ns:nb]
        send_sems, recv_sems = refs[nb + n_after], refs[nb + n_after + 1]
        token = refs[-1]
        x, y, c = _coords()
        for i, (sv, dv, dev) in enumerate(plan(src_refs, land_refs, x, y, c)):
            pltpu.make_async_remote_copy(src_ref=sv, dst_ref=dv, send_sem=send_sems.at[i], recv_sem=recv_sems.at[i],
                                         device_id=dev, device_id_type=MESH).start()
        token[...] = jnp.zeros_like(token)

    bufs = list(srcs) + list(lands)
    outs = pl.pallas_call(
        body, name=name,
        out_shape=(pltpu.SemaphoreType.DMA((n_copies,)), pltpu.SemaphoreType.DMA((n_copies,)),
                   *[pltpu.HBM(b.shape, b.dtype) for b in bufs], jax.ShapeDtypeStruct((8, 128), F32)),
        in_specs=[HBM] * nb + [ANY] * n_after,
        out_specs=(SEM, SEM, *[HBM] * nb, pl.BlockSpec(memory_space=pltpu.VMEM)),
        input_output_aliases={i: 2 + i for i in range(nb)},
        compiler_params=pltpu.CompilerParams(has_side_effects=EFFECT),
    )(*[_in_hbm(b) for b in bufs], *([] if after is None else [after]))
    return outs[0], outs[1], list(outs[2:2 + ns]), list(outs[2 + ns:2 + nb]), outs[-1]


def _remote_wait(started, after, plan, name):
    send_sems, recv_sems, srcs, lands, _ = started
    ns, nb = len(srcs), len(srcs) + len(lands)

    def body(*refs):
        src_refs, land_refs = refs[:ns], refs[ns:nb]
        send_sems, recv_sems = refs[nb], refs[nb + 1]
        x, y, c = _coords()
        for i, (sv, dv, dev) in enumerate(plan(src_refs, land_refs, x, y, c)):
            cp = pltpu.make_async_remote_copy(src_ref=sv, dst_ref=dv, send_sem=send_sems.at[i],
                                              recv_sem=recv_sems.at[i], device_id=dev, device_id_type=MESH)
            cp.wait_send()
            cp.wait_recv()

    bufs = list(srcs) + list(lands)
    outs = pl.pallas_call(
        body, name=name, out_shape=tuple(pltpu.HBM(b.shape, b.dtype) for b in bufs),
        in_specs=[HBM] * nb + [SEM, SEM, ANY], out_specs=tuple([HBM] * nb),
        input_output_aliases={i: i for i in range(nb)},
        compiler_params=pltpu.CompilerParams(has_side_effects=EFFECT),
    )(*bufs, send_sems, recv_sems, after)
    return list(outs[:ns]), list(outs[ns:])


def _pair_plan(src_refs, land_refs, x, y, c):
    plan = []
    for s, l in zip(src_refs, land_refs):
        for q in range(4):
            plan.append((s.at[2 * q + (1 - c)], l.at[q], (x, y, 1 - c)))
    return plan


def _pair4_plan(src_refs, land_refs, x, y, c):
    plan = []
    for s, l in zip(src_refs, land_refs):
        for q in range(4):
            plan.append((s.at[q], l.at[q], (x, y, 1 - c)))
    return plan


def _chips_plan(src_refs, land_refs, x, y, c):
    plan = []
    for s, l in zip(src_refs, land_refs):
        for k, (tx, ty) in enumerate([(1 - x, y), (x, 1 - y), (1 - x, 1 - y)]):
            plan.append((s.at[2 * tx + ty], l.at[k], (tx, ty, c)))
    return plan


def _everyone_plan(src_refs, land_refs, x, y, c):
    me = 4 * x + 2 * y + c
    plan = []
    for s, l in zip(src_refs, land_refs):
        for fx, fy, fc in [(0, 0, 1), (1, 0, 0), (1, 0, 1), (0, 1, 0), (0, 1, 1), (1, 1, 0), (1, 1, 1)]:
            dev = ((1 - x) if fx else x, (1 - y) if fy else y, (1 - c) if fc else c)
            plan.append((s, l.at[me], dev))
    return plan


def _pair_add(g8, r1, csel, tr, name):
    _, R, C = r1.shape
    g4 = g8.reshape(4, 2, R, C)

    def body(c_ref, g_ref, r_ref, o_ref):
        o_ref[...] = (g_ref[...].astype(F32) + r_ref[...].astype(F32)).astype(BF16)

    return pl.pallas_call(
        body,
        grid_spec=pltpu.PrefetchScalarGridSpec(
            num_scalar_prefetch=1, grid=(4, R // tr),
            in_specs=[pl.BlockSpec((None, None, tr, C), lambda q, i, cs: (q, cs[0], i, 0)),
                      pl.BlockSpec((None, tr, C), lambda q, i, cs: (q, i, 0))],
            out_specs=pl.BlockSpec((None, tr, C), lambda q, i, cs: (q, i, 0))),
        out_shape=jax.ShapeDtypeStruct((4, R, C), BF16), name=name,
        compiler_params=_cparams(("parallel", "parallel")),
    )(csel, g4, r1)


def _adamw_math(w, g, m, v):
    m = ADAM_B1 * m + (1.0 - ADAM_B1) * g
    v = ADAM_B2 * v + (1.0 - ADAM_B2) * (g * g)
    m_hat = m / (1.0 - ADAM_B1 ** ADAM_STEP)
    v_hat = v / (1.0 - ADAM_B2 ** ADAM_STEP)
    delta = -ADAM_LR * (m_hat / (jnp.sqrt(v_hat) + ADAM_EPS) + ADAM_WD * w)
    return delta, m, v


def _adamw_big(w, m, v, p4, r3, qsel, tile, name):
    R, C = w.shape
    tr, tc = tile

    def body(q_ref, w_ref, m_ref, v_ref, p_ref, r_ref, g_out, d_out, m_out, v_out):
        g = p_ref[...].astype(F32) + r_ref[0].astype(F32) + r_ref[1].astype(F32) + r_ref[2].astype(F32)
        d, mn, vn = _adamw_math(w_ref[...], g, m_ref[...], v_ref[...])
        g_out[...] = g
        d_out[...] = d
        m_out[...] = mn
        v_out[...] = vn

    blk = pl.BlockSpec((tr, tc), lambda i, j, qs: (i, j))
    return pl.pallas_call(
        body,
        grid_spec=pltpu.PrefetchScalarGridSpec(
            num_scalar_prefetch=1, grid=(R // tr, C // tc),
            in_specs=[blk, blk, blk, pl.BlockSpec((None, tr, tc), lambda i, j, qs: (qs[0], i, j)),
                      pl.BlockSpec((3, tr, tc), lambda i, j, qs: (0, i, j))],
            out_specs=[blk, blk, blk, blk]),
        out_shape=[jax.ShapeDtypeStruct((R, C), F32)] * 4, name=name,
        compiler_params=_cparams(("parallel", "parallel")),
    )(qsel, w, m, v, p4, r3)


def _sum_partials(p4, r3, qsel, tc, name):
    _, R, C = p4.shape

    def body(q_ref, p_ref, r_ref, o_ref):
        o_ref[...] = p_ref[...].astype(F32) + r_ref[0].astype(F32) + r_ref[1].astype(F32) + r_ref[2].astype(F32)

    return pl.pallas_call(
        body,
        grid_spec=pltpu.PrefetchScalarGridSpec(
            num_scalar_prefetch=1, grid=(C // tc,),
            in_specs=[pl.BlockSpec((None, R, tc), lambda j, qs: (qs[0], 0, j)),
                      pl.BlockSpec((3, R, tc), lambda j, qs: (0, 0, j))],
            out_specs=pl.BlockSpec((R, tc), lambda j, qs: (0, j))),
        out_shape=jax.ShapeDtypeStruct((R, C), F32), name=name, compiler_params=_cparams(("parallel",)),
    )(qsel, p4, r3)


def _adamw_tiled(w, g, m, v, tr, name):
    R, C = w.shape

    def body(w_ref, g_ref, m_ref, v_ref, d_out, m_out, v_out):
        d, mn, vn = _adamw_math(w_ref[...], g_ref[...], m_ref[...], v_ref[...])
        d_out[...] = d
        m_out[...] = mn
        v_out[...] = vn

    blk = pl.BlockSpec((tr, C), lambda i: (i, 0))
    return pl.pallas_call(
        body, grid=(R // tr,), in_specs=[blk] * 4, out_specs=[blk] * 3,
        out_shape=[jax.ShapeDtypeStruct((R, C), F32)] * 3, name=name, compiler_params=_cparams(("parallel",)),
    )(w, g, m, v)


def _small_sum(parts, name):
    def body(p_ref, o_ref):
        acc = p_ref[0]
        for d in range(1, N_DEV):
            acc = acc + p_ref[d]
        o_ref[...] = acc

    return pl.pallas_call(
        body, out_shape=jax.ShapeDtypeStruct(parts.shape[1:], F32), name=name,
        compiler_params=_cparams(),
    )(parts)


def _adamw_small(w, g, m, v, name):
    def body(w_ref, g_ref, m_ref, v_ref, d_out, m_out, v_out):
        d, mn, vn = _adamw_math(w_ref[...], g_ref[...], m_ref[...], v_ref[...])
        d_out[...] = d
        m_out[...] = mn
        v_out[...] = vn

    return pl.pallas_call(
        body, out_shape=[jax.ShapeDtypeStruct(w.shape, F32)] * 3, name=name, compiler_params=_cparams(),
    )(w, g, m, v)


def _row(*pieces):
    r = jnp.concatenate([p.reshape(1, -1) for p in pieces], axis=1)
    return jnp.pad(r, ((0, 0), (0, D_MODEL - r.shape[1])))


def _pack_small(mix, convb, ssmg, attng, mlpg, fing, convw, dtb, alog, dsk, sinks, extra=None):
    last = [dtb, alog, dsk, sinks] + ([extra] if extra is not None else [])
    rows = [_row(mix), _row(convb), _row(ssmg, attng), _row(mlpg), _row(fing),
            jnp.pad(convw, ((0, 0), (0, D_MODEL - convw.shape[1]))), _row(*last)]
    packed = jnp.concatenate(rows, axis=0)
    return jnp.pad(packed, ((0, SMALL_ROWS - packed.shape[0]), (0, 0)))


def _unpack_small(p, conv_n):
    return dict(
        mix_norm_g=p[0:1, :], conv_b=p[1:2, :], ssm_norm_g=p[2:3, :D_INNER], attn_out_norm_g=p[2:3, D_INNER:],
        mlp_norm_g=p[3:4, :], final_norm_g=p[4, :], conv_w=p[5:9, :conv_n][None],
        dt_bias=p[9:10, 0:16], A_log=p[9:10, 16:32], D_skip=p[9:10, 32:48], attn_sinks=p[9:10, 48:64])


WEIGHT_ORDER = ["mix_norm_g", "w_in", "conv_w", "conv_b", "dt_bias", "A_log", "D_skip", "ssm_norm_g", "attn_sinks",
                "attn_out_norm_g", "w_out", "mlp_norm_g", "w_up", "w_down", "final_norm_g"]


def _to_my_columns(w_nat):
    pad = jnp.zeros((w_nat.shape[0], NP - IN_PROJ), w_nat.dtype)
    return jnp.concatenate([w_nat[:, :NAT_DT], w_nat[:, NAT_DT + N_HEADS:], w_nat[:, NAT_DT:NAT_DT + N_HEADS], pad],
                           axis=1)


PER = IN_PROJ // N_DEV
SUPER_STEP = 544
SUPER = 576


def _natural_rows(g, lo, hi):
    segments = [(0, NAT_DT, 0), (NAT_DT, NAT_DT + N_HEADS, OFF_DT - NAT_DT), (NAT_DT + N_HEADS, IN_PROJ, -N_HEADS),
                (IN_PROJ, NP, 0)]
    pieces = [g[max(lo, a) + shift:min(hi, b) + shift] for a, b, shift in segments if max(lo, a) < min(hi, b)]
    return pieces[0] if len(pieces) == 1 else jnp.concatenate(pieces, axis=0)


def _w_in_from_super_slabs(sup):
    seam = SUPER - SUPER_STEP
    units = []
    for i in range(N_DEV):
        base = SUPER_STEP * i
        units.append((base, base + seam, sup[i, :seam] if i == 0 else sup[i - 1, SUPER_STEP:] + sup[i, :seam]))
        units.append((base + seam, base + SUPER_STEP, sup[i, seam:SUPER_STEP]))
    units.append((SUPER_STEP * N_DEV, SUPER_STEP * N_DEV + seam, sup[N_DEV - 1, SUPER_STEP:]))

    def natural(lo, hi):
        return [rows[max(lo, a) - a:min(hi, b) - a] for a, b, rows in units if max(lo, a) < min(hi, b)]

    pieces = natural(0, NAT_DT) + natural(NAT_DT + N_HEADS, IN_PROJ) + natural(NAT_DT, NAT_DT + N_HEADS)
    return jnp.concatenate(pieces + [jnp.zeros((NP - IN_PROJ, D_MODEL), sup.dtype)], axis=0)


def _to_natural_columns(w_my):
    return jnp.concatenate([w_my[:, :NAT_DT], w_my[:, OFF_DT:OFF_DT + N_HEADS], w_my[:, NAT_DT:OFF_DT]], axis=1)


SLAB = 1024


def _grad_w_up(h2, du, name, sel=None, add=None, after=None):
    T, D = h2.shape
    if sel is None:
        pick, n_slab, pre = (lambda j, *cs: j), N_DEV, None
    else:
        pre, other = sel
        pick, n_slab = (lambda j, cs: 2 * j + ((1 - cs[0]) if other else cs[0])), 4
    o_spec = pl.BlockSpec((None, D, SLAB), lambda i, j, k, *cs: (j, 0, 0))
    return _matmul(
        h2, du, mode="tn", grid=(1, n_slab, 1),
        a_spec=pl.BlockSpec((T, D), lambda i, j, k, *cs: (0, 0)),
        b_spec=pl.BlockSpec((T, SLAB), lambda i, j, k, *cs: (0, pick(j, *cs))),
        out_shapes=[jax.ShapeDtypeStruct((n_slab, D, SLAB), BF16)], out_specs=[o_spec], tile=(D, SLAB), name=name,
        extras=() if add is None else (add,), extra_specs=() if add is None else (o_spec,),
        epilogue=None if add is None else (lambda acc, r: (acc + r.astype(F32),)), after=after, prefetch=pre)[0]


def _grad_w_down(act, dx3b, name, sel=None, add=None, after=None):
    T, D = dx3b.shape
    if sel is None:
        pick, n_slab, pre = (lambda i, *cs: i), N_DEV, None
    else:
        pre, other = sel
        pick, n_slab = (lambda i, cs: 2 * i + ((1 - cs[0]) if other else cs[0])), 4
    o_spec = pl.BlockSpec((None, SLAB, D), lambda i, j, k, *cs: (i, 0, 0))
    return _matmul(
        act, dx3b, mode="tn", grid=(n_slab, 1, 1),
        a_spec=pl.BlockSpec((T, SLAB), lambda i, j, k, *cs: (0, pick(i, *cs))),
        b_spec=pl.BlockSpec((T, D), lambda i, j, k, *cs: (0, 0)),
        out_shapes=[jax.ShapeDtypeStruct((n_slab, SLAB, D), BF16)], out_specs=[o_spec], tile=(SLAB, D), name=name,
        extras=() if add is None else (add,), extra_specs=() if add is None else (o_spec,),
        epilogue=None if add is None else (lambda acc, r: (acc + r.astype(F32),)), after=after, prefetch=pre)[0]


class _FixedWeights:
    def __init__(self, w_in_p, w_out_f, w_up_s, w_down_f, conv_w_f):
        self.w = (w_in_p, w_out_f, w_up_s, w_down_f, conv_w_f)
        self.grads = {}

    def mixer_weights(self, after):
        return self.w[0], None

    def conv_weight(self, after):
        return self.w[4]

    def out_weight(self, after):
        return self.w[1]

    def up_weight(self, after):
        return self.w[2]

    def down_weight(self, h, after):
        return self.w[3][:, h * (D_MODEL // 2):(h + 1) * (D_MODEL // 2)]

    def mlp_grads(self, h2, du, act, dx3b):
        self.grads.update(w_up=_grad_w_up(h2, du, "grad_w_up"),
                          w_down=_grad_w_down(act, dx3b, "grad_w_down").reshape(D_FF, D_MODEL))
        return None

    def grad_sent(self, tag, after):
        return None

    def out_grad(self, g_out):
        self.grads.update(w_out=g_out)
        return None

    def in_grad(self, g_in):
        self.grads.update(w_in=g_in)
        return None


def _local_step(x, tgt, p, hooks):
    T = x.shape[0]
    D = D_MODEL
    h1 = _rmsnorm_fwd(x, p["mix_norm_g"], "norm_mix")
    w_in_t, token = hooks.mixer_weights(h1)
    (proj,) = _mm_simple(h1, w_in_t, mode="nt", M=T, N=NP, K=D, tm=min(T, 1024), tn=1536, tk=D, out_dtype=F32,
                         name="in_proj", after=token)
    conv_w_f = hooks.conv_weight(proj)
    xbc, dsilu = _conv_fwd(proj, conv_w_f, p["conv_b"], "conv_fwd")
    dtT = proj[:, OFF_DT:OFF_DT + N_HEADS].T
    dtbT = p["dt_bias"].T
    alogT = p["A_log"].T
    dfull = jnp.repeat(p["D_skip"], HEAD_DIM, axis=1)
    ycat, ypre, hs = _ssd_fwd(xbc, proj, dtT, p["dt_bias"], dtbT, p["A_log"], alogT, dfull, p["ssm_norm_g"],
                              "ssd_fwd")
    ycat, o_att, probs, sink_probs = _attn_fwd(proj, p["attn_sinks"], p["attn_out_norm_g"], ycat, "attn_fwd")
    w_out_f = hooks.out_weight(ycat)
    tm = min(T, 1024)
    def residual_and_norm(acc, res, gain):
        x2 = acc + res
        return x2, x2 * lax.rsqrt(jnp.mean(x2 * x2, axis=-1, keepdims=True) + EPS) * gain

    rows = min(T, 512)
    x2, h2 = _matmul(
        ycat, w_out_f, mode="nn", grid=(T // rows, 1, 1),
        a_spec=pl.BlockSpec((rows, D), lambda i, j, k: (i, 0)), b_spec=pl.BlockSpec((D, D), lambda i, j, k: (0, 0)),
        out_shapes=[jax.ShapeDtypeStruct((T, D), F32), jax.ShapeDtypeStruct((T, D), BF16)],
        out_specs=[pl.BlockSpec((rows, D), lambda i, j, k: (i, 0))] * 2, tile=(rows, D), name="out_proj",
        extras=(x, p["mlp_norm_g"]),
        extra_specs=[pl.BlockSpec((rows, D), lambda i, j, k: (i, 0)), pl.BlockSpec((1, D), lambda i, j, k: (0, 0))],
        epilogue=residual_and_norm)
    w_up_s = hooks.up_weight(h2)
    grid = (T // tm, N_DEV, 1)
    u, act = _matmul(
        h2, w_up_s, mode="nn", grid=grid,
        a_spec=pl.BlockSpec((tm, D), lambda i, j, k: (i, 0)),
        b_spec=pl.BlockSpec((None, D, 1024), lambda i, j, k: (j, 0, 0)),
        out_shapes=[jax.ShapeDtypeStruct((T, D_FF), F32), jax.ShapeDtypeStruct((T, D_FF), BF16)],
        out_specs=[pl.BlockSpec((tm, 1024), lambda i, j, k: (i, j))] * 2, tile=(tm, 1024), name="mlp_up",
        epilogue=lambda acc: (acc, jnp.square(jnp.maximum(acc, 0.0))))
    half = D // 2
    w_down_halves, x3_halves = [], []
    for h in range(2):
        w_down_halves.append(hooks.down_weight(h, act if h == 0 else x3_halves[0]))
        x3_halves.append(_matmul(
            act, w_down_halves[h], mode="nn", grid=(T // tm, 1, D_FF // 2048),
            a_spec=pl.BlockSpec((tm, 2048), lambda i, j, k: (i, k)),
            b_spec=pl.BlockSpec((2048, half), lambda i, j, k: (k, 0)),
            out_shapes=[jax.ShapeDtypeStruct((T, half), F32)],
            out_specs=[pl.BlockSpec((tm, half), lambda i, j, k: (i, 0))], tile=(tm, half), name=f"mlp_down_{h}",
            extras=(x2,), extra_specs=[pl.BlockSpec((tm, half), lambda i, j, k, h=h: (i, h))],
            epilogue=lambda acc, res: (acc + res,))[0])
    loss_part, d_fin, dx3, dx3b = _final_loss(x3_halves, tgt, p["final_norm_g"].reshape(1, D), "loss_head")
    (du,) = _matmul(
        dx3b, tuple(w_down_halves), mode="nt", grid=(T // tm, D_FF // 1024, 1),
        a_spec=pl.BlockSpec((tm, D), lambda i, j, k: (i, 0)),
        b_spec=(pl.BlockSpec((1024, half), lambda i, j, k: (j, 0)),) * 2,
        out_shapes=[jax.ShapeDtypeStruct((T, D_FF), BF16)],
        out_specs=[pl.BlockSpec((tm, 1024), lambda i, j, k: (i, j))], tile=(tm, 1024), name="mlp_down_bwd",
        extras=(u,), extra_specs=[pl.BlockSpec((tm, 1024), lambda i, j, k: (i, j))],
        epilogue=lambda acc, uu: (acc * (2.0 * jnp.maximum(uu, 0.0)),),
        dot_fn=lambda a, b0, b1: _dot_nt(a[:, :half], b0) + _dot_nt(a[:, half:], b1))
    token = hooks.mlp_grads(h2, du, act, dx3b)
    (dh2,) = _matmul(
        du, w_up_s, mode="nt", grid=(T // tm, D // 1024, N_DEV // 2),
        a_spec=pl.BlockSpec((tm, 2048), lambda i, j, k: (i, k)),
        b_spec=pl.BlockSpec((2, 1024, 1024), lambda i, j, k: (k, j, 0)),
        out_shapes=[jax.ShapeDtypeStruct((T, D), F32)],
        out_specs=[pl.BlockSpec((tm, 1024), lambda i, j, k: (i, j))], tile=(tm, 1024), name="mlp_up_bwd",
        after=token, dot_fn=lambda a, b: _dot_nt(a[:, :1024], b[0]) + _dot_nt(a[:, 1024:], b[1]))
    dx2, dx2b, d_mlp = _rmsnorm_bwd(dh2, x2, p["mlp_norm_g"], dx3, "norm_mlp_bwd")
    (g_out,) = _mm_simple(ycat, dx2b, mode="tn", M=D, N=D, K=T, tm=1024, tn=1024, tk=T, out_dtype=BF16,
                          name="grad_w_out")
    token = hooks.out_grad(g_out)
    (dy,) = _mm_simple(dx2b, w_out_f, mode="nt", M=T, N=D, K=D, tm=tm, tn=1024, tk=D, out_dtype=F32,
                       name="out_proj_bwd", after=token)
    token = hooks.grad_sent("out", dy)
    ssm_g = p["ssm_norm_g"] if token is None else p["ssm_norm_g"] + token[0:1, 0:1]
    dproj, dxbc_act, d_dtb, d_alog, d_dskip, d_ssmg = _ssd_bwd(
        xbc, proj, dtT, p["dt_bias"], dtbT, p["A_log"], alogT, dfull, ssm_g, ypre, hs, dy, "ssd_bwd")
    dproj, d_convw, d_convb = _conv_bwd(proj, dxbc_act, dsilu, conv_w_f, dproj, "conv_bwd")
    dproj, dk, dv, d_sinks, d_attng = _attn_bwd(proj, p["attn_out_norm_g"], o_att, dy, probs, sink_probs, dproj,
                                                "attn_bwd")
    dproj = lax.dynamic_update_slice(dproj, jnp.concatenate([dk, dv], axis=1).astype(BF16), (0, OFF_K))
    (g_in,) = _mm_simple(dproj, h1, mode="tn", M=NP, N=D, K=T, tm=1536, tn=1024, tk=T, out_dtype=BF16,
                         name="grad_w_in")
    token = hooks.in_grad(g_in)
    (dh1,) = _mm_simple(dproj, w_in_t, mode="nn", M=T, N=D, K=NP, tm=tm, tn=1024, tk=2304, out_dtype=F32,
                        name="in_proj_bwd", after=token)
    token = hooks.grad_sent("in", dh1)
    mix_g = p["mix_norm_g"] if token is None else p["mix_norm_g"] + token[0:1, 0:1]
    dx, d_mix = _rmsnorm_bwd(dh1, x, mix_g, dx2, "norm_mix_bwd", with_bf16=False)
    small = _pack_small(d_mix, d_convb, d_ssmg, d_attng, d_mlp, d_fin, d_convw, d_dtb, d_alog, d_dskip, d_sinks,
                        extra=loss_part[:, 0:1])
    return dx, small


def _rows_rotated(v, shift, name):
    R, C = v.shape
    tc = 512

    def body(s_ref, v_ref, o_ref):
        o_ref[...] = pltpu.roll(v_ref[...], s_ref[0], axis=0).astype(BF16)

    return pl.pallas_call(
        body,
        grid_spec=pltpu.PrefetchScalarGridSpec(
            num_scalar_prefetch=1, grid=(C // tc,), in_specs=[pl.BlockSpec((R, tc), lambda j, s: (0, j))],
            out_specs=pl.BlockSpec((R, tc), lambda j, s: (0, j))),
        out_shape=jax.ShapeDtypeStruct((R, C), BF16), name=name, compiler_params=_cparams(("parallel",)),
    )(shift, v)


def _landing(own, me):
    zone = lax.empty((N_DEV,) + own.shape, own.dtype)
    return lax.dynamic_update_slice(zone, own[None], (me,) + (0,) * own.ndim)


def _sequencer_gather(owns, split, me, collective_id, name):
    n = len(owns)
    zone_refs = [jax.new_ref(_landing(o, me), memory_space=pltpu.MemorySpace.HBM) for o in owns]
    own_refs = [jax.new_ref(o, memory_space=pltpu.MemorySpace.HBM) for o in owns]
    N_COPIES = 9

    @pl.kernel(mesh=plsc.ScalarSubcoreMesh(axis_name="sequencer", num_cores=1), name=name,
               scratch_types=(pltpu.SemaphoreType.DMA((n, N_COPIES)), pltpu.SemaphoreType.DMA((n, N_COPIES))),
               compiler_params=pltpu.CompilerParams(collective_id=collective_id))
    def launch(send_sems, recv_sems):
        x, y, c = _coords()
        sibling, xn, yn, diag = (x, y, 1 - c), (1 - x, y, c), (x, 1 - y, c), (1 - x, 1 - y, c)
        barrier = pltpu.get_barrier_semaphore()
        for peer in [sibling, xn, yn, diag]:
            pl.semaphore_signal(barrier, inc=1, device_id=peer, device_id_type=MESH)
        pl.semaphore_wait(barrier, 4)

        def block(a, dev, half=None):
            ref = zone_refs[a].at[4 * dev[0] + 2 * dev[1] + dev[2]]
            if half is None:
                return ref
            rows = owns[a].shape[0] // 2
            return ref.at[pl.ds(half * rows, rows)]

        def copy(a, k, src, dst, to):
            return pltpu.make_async_remote_copy(src_ref=src, dst_ref=dst, send_sem=send_sems.at[a, k],
                                                recv_sem=recv_sems.at[a, k], device_id=to, device_id_type=MESH)

        me_dev = (x, y, c)
        sent = []
        first = {}
        for a in range(n):
            for k, peer in enumerate([sibling, xn, yn] + ([] if split[a] else [diag])):
                first[a, k] = copy(a, k, own_refs[a], block(a, me_dev), peer)
                first[a, k].start()
                sent.append(first[a, k])
        from_sibling = []
        for a in range(n):
            first[a, 1].wait_recv()
            sent.append(copy(a, 4, block(a, xn), block(a, xn), sibling))
            if split[a]:
                sent.append(copy(a, 6, block(a, xn, 0), block(a, xn, 0), yn))
            first[a, 2].wait_recv()
            sent.append(copy(a, 5, block(a, yn), block(a, yn), sibling))
            if split[a]:
                sent.append(copy(a, 7, block(a, yn, 1), block(a, yn, 1), xn))
            for cp in sent[-(4 if split[a] else 2):]:
                cp.start()
        for a in range(n):
            if split[a]:
                copy(a, 6, block(a, diag, 0), block(a, diag, 0), yn).wait_recv()
                sent.append(copy(a, 8, block(a, diag, 0), block(a, diag, 0), sibling))
                sent[-1].start()
                copy(a, 7, block(a, diag, 1), block(a, diag, 1), xn).wait_recv()
                sent.append(copy(a, 3, block(a, diag, 1), block(a, diag, 1), sibling))
                sent[-1].start()
            else:
                first[a, 3].wait_recv()
                sent.append(copy(a, 8, block(a, diag), block(a, diag), sibling))
                sent[-1].start()
        for a in range(n):
            first[a, 0].wait_recv()
            copy(a, 4, block(a, xn), block(a, xn), sibling).wait_recv()
            copy(a, 5, block(a, yn), block(a, yn), sibling).wait_recv()
            if split[a]:
                copy(a, 8, block(a, diag, 0), block(a, diag, 0), sibling).wait_recv()
                copy(a, 3, block(a, diag, 1), block(a, diag, 1), sibling).wait_recv()
            else:
                copy(a, 8, block(a, diag), block(a, diag), sibling).wait_recv()
        for cp in sent:
            cp.wait_send()

    launch()
    return zone_refs


class _ShardedWeights:
    def __init__(self, w_in, w_out, conv_w, w_up, w_down, me, csel):
        self.me, self.csel = me, csel
        padded = jnp.pad(jnp.transpose(w_in), ((0, SUPER - PER), (0, 0)))
        own_rows = _rows_rotated(padded, jnp.reshape(2 * me, (1,)).astype(jnp.int32), "w_in_super_slab")
        (self.in_ref,) = _sequencer_gather([own_rows], [True], me, 7, "gather_w_in_sequencer")
        self.out_ref, self.conv_ref = _sequencer_gather([w_out.astype(BF16), conv_w], [True, False], me, 8,
                                                        "gather_w_out_sequencer")
        (self.up_ref,) = _sequencer_gather([w_up.astype(BF16)], [True], me, 9, "gather_w_up_sequencer")
        down = w_down.astype(BF16)
        self.down_refs = [_sequencer_gather([down[:, h * (D_MODEL // 2):(h + 1) * (D_MODEL // 2)]], [True], me, 10 + h,
                                            f"gather_w_down_{h}_sequencer")[0] for h in range(2)]
        self.reduces = {}
        self.pairs = {}

    def mixer_weights(self, after):
        return _w_in_from_super_slabs(self.in_ref[...]), None

    def conv_weight(self, after):
        g_conv = self.conv_ref[...]
        return jnp.concatenate([g_conv[i] for i in range(N_DEV)], axis=1)

    def out_weight(self, after):
        return self.out_ref[...].reshape(D_MODEL, D_MODEL)

    def up_weight(self, after):
        return self.up_ref[...]

    def down_weight(self, h, after):
        return self.down_refs[h][...].reshape(D_FF, D_MODEL // 2)

    def _chips_start(self, slabs, from_sibling, rows, tag):
        sums = [_pair_add(s, r, self.csel, tr, f"pair_add_{tag}_{i}")
                for i, (s, r, tr) in enumerate(zip(slabs, from_sibling, rows))]
        lands = [lax.empty((3,) + s.shape[1:], s.dtype) for s in sums]
        self.reduces[tag] = _remote_start(sums, lands, _chips_plan, 3 * len(sums), f"reduce_start_{tag}")
        return self.reduces[tag][4]

    def mlp_grads(self, h2, du, act, dx3b):
        def send(part, tag, after):
            st = _remote_start([part], [lax.empty(part.shape, part.dtype)], _pair4_plan, 4,
                               f"reduce_pair_start_{tag}", after=after)
            return st

        def received(st, after, tag):
            return _remote_wait(st, after, _pair4_plan, f"reduce_pair_wait_{tag}")[1][0]

        def to_chips(sums, tag):
            self.reduces[tag] = _remote_start([sums], [lax.empty((3,) + sums.shape[1:], sums.dtype)], _chips_plan, 3,
                                              f"reduce_start_{tag}")
            return self.reduces[tag][4]

        up_send = _grad_w_up(h2, du, "grad_w_up_send", sel=(self.csel, True))
        st_up = send(up_send, "up", None)
        down_send = _grad_w_down(act, dx3b, "grad_w_down_send", sel=(self.csel, True), after=st_up[4])
        st_down = send(down_send, "down", None)
        up_sum = _grad_w_up(h2, du, "grad_w_up_keep", sel=(self.csel, False), add=received(st_up, down_send, "up"),
                            after=st_down[4])
        token = to_chips(up_sum, "up")
        down_sum = _grad_w_down(act, dx3b, "grad_w_down_keep", sel=(self.csel, False),
                                add=received(st_down, up_sum, "down"), after=token)
        return to_chips(down_sum, "down")

    def _pair_start(self, slabs, tag):
        land = lax.empty((4,) + slabs.shape[1:], slabs.dtype)
        self.pairs[tag] = _remote_start([slabs], [land], _pair_plan, 4, f"reduce_pair_start_{tag}")
        return self.pairs[tag][4]

    def grad_sent(self, tag, after):
        slabs, from_sibling = _remote_wait(self.pairs[tag], after, _pair_plan, f"reduce_pair_wait_{tag}")
        return self._chips_start(slabs, from_sibling, [slabs[0].shape[1]], tag)

    def out_grad(self, g_out):
        return self._pair_start(g_out.reshape(N_DEV, D_MODEL // N_DEV, D_MODEL), "out")

    def in_grad(self, g_in):
        return self._pair_start(
            jnp.stack([_natural_rows(g_in, SUPER_STEP * j, SUPER_STEP * j + SUPER) for j in range(N_DEV)]), "in")

    def small_start(self, small):
        self.st_small = _remote_start([small], [_landing(small, self.me)], _everyone_plan, N_DEV - 1, "gather_start_small")

    def small_end(self, after):
        return _remote_wait(self.st_small, after, _everyone_plan, "gather_small_wait")[1][0]

    def reduce_end(self, tag, after):
        return _remote_wait(self.reduces[tag], after, _chips_plan, f"reduce_wait_{tag}")


def kernel(x, mix_norm_g, w_in, conv_w, conv_b, dt_bias, A_log, D_skip, ssm_norm_g, attn_sinks, attn_out_norm_g, w_out, mlp_norm_g, w_up, w_down, final_norm_g, loss_target, m_mix_norm_g, m_w_in, m_conv_w, m_conv_b, m_dt_bias, m_A_log, m_D_skip, m_ssm_norm_g, m_attn_sinks, m_attn_out_norm_g, m_w_out, m_mlp_norm_g, m_w_up, m_w_down, m_final_norm_g, v_mix_norm_g, v_w_in, v_conv_w, v_conv_b, v_dt_bias, v_A_log, v_D_skip, v_ssm_norm_g, v_attn_sinks, v_attn_out_norm_g, v_w_out, v_mlp_norm_g, v_w_up, v_w_down, v_final_norm_g):
    xi, yi, ci = _coords()
    me = 4 * xi + 2 * yi + ci
    csel = jnp.reshape(ci, (1,)).astype(jnp.int32)
    qsel = jnp.reshape(2 * xi + yi, (1,)).astype(jnp.int32)
    w = dict(mix_norm_g=mix_norm_g, conv_b=conv_b, dt_bias=dt_bias, A_log=A_log, D_skip=D_skip,
             ssm_norm_g=ssm_norm_g, attn_sinks=attn_sinks, attn_out_norm_g=attn_out_norm_g, mlp_norm_g=mlp_norm_g,
             final_norm_g=final_norm_g)
    hooks = _ShardedWeights(w_in[0], w_out[0], conv_w[0], w_up[0], w_down[0], me, csel)
    p = dict(w)
    dx, small = _local_step(x[0], loss_target[0], p, hooks)
    hooks.small_start(small)
    big = {}
    after = dx
    for name, wt, mt, vt, tile in [
            ("up", w_up, m_w_up, v_w_up, (512, SLAB)), ("down", w_down, m_w_down, v_w_down, (256, D_MODEL)),
            ("out", w_out, m_w_out, v_w_out, (256, D_MODEL))]:
        (chip_sums,), (from_chips,) = hooks.reduce_end(name, after)
        res = _adamw_big(wt[0], mt[0], vt[0], chip_sums, from_chips, qsel, tile, f"adamw_w_{name}")
        big["w_" + name] = tuple(r[None] for r in res)
        after = res[0]
    (chip_sums,), (from_chips,) = hooks.reduce_end("in", after)
    g_super = _sum_partials(chip_sums, from_chips, qsel, 512, "grad_w_in_sum")
    g_in = jnp.transpose(lax.dynamic_slice(g_super, (2 * me, 0), (PER, D_MODEL)))
    res = _adamw_tiled(w_in[0], g_in, m_w_in[0], v_w_in[0], 512, "adamw_w_in")
    big["w_in"] = tuple(r[None] for r in (g_in, *res))
    after = res[0]
    gsum = _small_sum(hooks.small_end(after), "small_sum")
    loss = gsum[9, 64]
    gs = _unpack_small(gsum, CONV_DIM)
    cw = CONV_DIM // N_DEV
    g_conv_shard = lax.dynamic_slice(gsum[5:9, :], (0, me * cw), (CONV_K, cw))

    def pack(s):
        return _pack_small(s["mix_norm_g"], s["conv_b"], s["ssm_norm_g"], s["attn_out_norm_g"], s["mlp_norm_g"],
                           s["final_norm_g"], s["conv_w"][0], s["dt_bias"], s["A_log"], s["D_skip"], s["attn_sinks"])

    wp = pack(dict(w, conv_w=conv_w))
    mp = pack(dict(mix_norm_g=m_mix_norm_g, conv_b=m_conv_b, ssm_norm_g=m_ssm_norm_g,
                   attn_out_norm_g=m_attn_out_norm_g, mlp_norm_g=m_mlp_norm_g, final_norm_g=m_final_norm_g,
                   conv_w=m_conv_w, dt_bias=m_dt_bias, A_log=m_A_log, D_skip=m_D_skip, attn_sinks=m_attn_sinks))
    vp = pack(dict(mix_norm_g=v_mix_norm_g, conv_b=v_conv_b, ssm_norm_g=v_ssm_norm_g,
                   attn_out_norm_g=v_attn_out_norm_g, mlp_norm_g=v_mlp_norm_g, final_norm_g=v_final_norm_g,
                   conv_w=v_conv_w, dt_bias=v_dt_bias, A_log=v_A_log, D_skip=v_D_skip, attn_sinks=v_attn_sinks))
    gp = jnp.concatenate([gsum[0:5], jnp.pad(g_conv_shard, ((0, 0), (0, D_MODEL - cw))), gsum[9:10],
                          jnp.zeros((SMALL_ROWS - 10, D_MODEL), F32)], axis=0)
    dp, mnp, vnp = _adamw_small(wp, gp, mp, vp, "adamw_small")
    grads = dict(gs, conv_w=g_conv_shard[None])
    deltas = _unpack_small(dp, cw)
    new_m = _unpack_small(mnp, cw)
    new_v = _unpack_small(vnp, cw)
    for k, name in enumerate(["w_in", "w_out", "w_up", "w_down"]):
        grads[name], deltas[name], new_m[name], new_v[name] = big[name]
    return (loss, dx[None], *[grads[n] for n in WEIGHT_ORDER], *[deltas[n] for n in WEIGHT_ORDER],
            *[new_m[n] for n in WEIGHT_ORDER], *[new_v[n] for n in WEIGHT_ORDER])
```

```python
import jax
import jax.numpy as jnp
from jax import lax
from jax.experimental import pallas as pl
from jax.experimental.pallas import tpu as pltpu
from jax.experimental.pallas import tpu_sc as plsc

F32 = jnp.float32
BF16 = jnp.bfloat16
MESH = pl.DeviceIdType.MESH

EPS = 1e-5
D_MODEL = 2048
D_INNER = 1024
N_HEADS = 16
HEAD_DIM = 64
N_GROUPS = 4
D_STATE = 128
CHUNK = 128
CONV_K = 4
CONV_DIM = 2048
ATTN_W = 1024
KV_W = 128
WINDOW = 128
D_FF = 8192
IN_PROJ = 4368
N_DEV = 8
NP = 4608
OFF_Z, OFF_X, OFF_B, OFF_C, OFF_Q, OFF_K, OFF_V, OFF_DT = 0, 1024, 2048, 2560, 3072, 4096, 4224, 4352
NAT_DT = 3072

ADAM_LR = 0.001
ADAM_B1 = 0.9
ADAM_B2 = 0.999
ADAM_EPS = 1e-08
ADAM_WD = 0.01
ADAM_STEP = 10

VMEM_LIMIT = 52 * 1024 * 1024
SMALL_ROWS = 16
NEG = -1e30


def _cparams(sem=None):
    return pltpu.CompilerParams(dimension_semantics=sem, vmem_limit_bytes=VMEM_LIMIT)


def _split3(v):
    hi = v.astype(BF16)
    rest = v - hi.astype(F32)
    mid = rest.astype(BF16)
    return hi, mid, (rest - mid.astype(F32)).astype(BF16)


def _hdot(a, b, data):
    if data == "a":
        sel = b.astype(BF16)
        return sum(_dot_nn(part, sel) for part in _split3(a))
    sel = a.astype(BF16)
    return sum(_dot_nn(sel, part) for part in _split3(b))


def _dot_nn(a, b):
    return lax.dot_general(a, b, (((1,), (0,)), ((), ())), preferred_element_type=F32)


def _dot_nt(a, b):
    return lax.dot_general(a, b, (((1,), (1,)), ((), ())), preferred_element_type=F32)


def _dot_tn(a, b):
    return lax.dot_general(a, b, (((0,), (0,)), ((), ())), preferred_element_type=F32)


def _softplus(v):
    return jnp.maximum(v, 0.0) + jnp.log1p(jnp.exp(-jnp.abs(v)))


def _sigmoid(v):
    return 1.0 / (1.0 + jnp.exp(-v))


def _matmul(a, b, *, mode, grid, a_spec, b_spec, out_shapes, out_specs, tile, name,
            extras=(), extra_specs=(), epilogue=None, after=None, dot_fn=None, prefetch=None):
    nk = grid[2]
    n_ex = len(extras)
    n_out = len(out_shapes)
    bs, b_specs = (b, b_spec) if isinstance(b, tuple) else ((b,), (b_spec,))
    n_in = 1 + len(bs)
    dot = dot_fn if dot_fn is not None else {"nn": _dot_nn, "nt": _dot_nt, "tn": _dot_tn}[mode]

    def finish(acc, ex_refs, out_refs):
        res = (acc,) if epilogue is None else epilogue(acc, *[e[...] for e in ex_refs])
        for o, r in zip(out_refs, res):
            o[...] = r.astype(o.dtype)

    def body(*refs):
        ex_refs = refs[n_in:n_in + n_ex]
        out_refs = refs[n_in + n_ex:n_in + n_ex + n_out]
        part = dot(*[r[...].astype(BF16) for r in refs[:n_in]])
        if nk == 1:
            finish(part, ex_refs, out_refs)
        else:
            acc_ref = refs[-1]
            k = pl.program_id(2)

            @pl.when(k == 0)
            def _():
                acc_ref[...] = part

            @pl.when(k > 0)
            def _():
                acc_ref[...] += part

            @pl.when(k == nk - 1)
            def _():
                finish(acc_ref[...], ex_refs, out_refs)

    scratch = [] if nk == 1 else [pltpu.VMEM(tile, F32)]
    n_pre = 0 if prefetch is None else 1
    tok_specs = [] if after is None else [pl.BlockSpec((8, 128), lambda *_: (0, 0))]
    tok_args = [] if after is None else [after]

    def body_with_token(*refs):
        refs = refs[n_pre:]
        body(*refs[:n_in + n_ex], *refs[n_in + n_ex + len(tok_args):])

    in_specs = [a_spec, *b_specs, *extra_specs, *tok_specs]
    params = _cparams(("parallel", "parallel", "arbitrary"))
    if prefetch is None:
        return pl.pallas_call(
            body_with_token, grid=grid, in_specs=in_specs, out_specs=list(out_specs), out_shape=list(out_shapes),
            scratch_shapes=scratch, name=name, compiler_params=params)(a, *bs, *extras, *tok_args)
    return pl.pallas_call(
        body_with_token,
        grid_spec=pltpu.PrefetchScalarGridSpec(num_scalar_prefetch=1, grid=grid, in_specs=in_specs,
                                               out_specs=list(out_specs), scratch_shapes=scratch),
        out_shape=list(out_shapes), name=name, compiler_params=params)(prefetch, a, *bs, *extras, *tok_args)


def _mm_simple(a, b, *, mode, M, N, K, tm, tn, tk, out_dtype, name, extras=(), epilogue=None, n_out=1,
               out_dtypes=None, after=None):
    grid = (M // tm, N // tn, K // tk)
    if mode == "nn":
        a_spec = pl.BlockSpec((tm, tk), lambda i, j, k: (i, k))
        b_spec = pl.BlockSpec((tk, tn), lambda i, j, k: (k, j))
    elif mode == "nt":
        a_spec = pl.BlockSpec((tm, tk), lambda i, j, k: (i, k))
        b_spec = pl.BlockSpec((tn, tk), lambda i, j, k: (j, k))
    else:
        a_spec = pl.BlockSpec((tk, tm), lambda i, j, k: (k, i))
        b_spec = pl.BlockSpec((tk, tn), lambda i, j, k: (k, j))
    o_spec = pl.BlockSpec((tm, tn), lambda i, j, k: (i, j))
    dts = out_dtypes if out_dtypes is not None else [out_dtype] * n_out
    return _matmul(a, b, mode=mode, grid=grid, a_spec=a_spec, b_spec=b_spec,
                   out_shapes=[jax.ShapeDtypeStruct((M, N), d) for d in dts],
                   out_specs=[o_spec] * len(dts), tile=(tm, tn), name=name,
                   extras=extras, extra_specs=[o_spec] * len(extras), epilogue=epilogue, after=after)


ROW_BLOCK = 256


def _rmsnorm_fwd(x, g, name):
    T, D = x.shape

    def body(x_ref, g_ref, o_ref):
        xf = x_ref[...]
        r = lax.rsqrt(jnp.mean(xf * xf, axis=-1, keepdims=True) + EPS)
        o_ref[...] = (xf * r * g_ref[...]).astype(BF16)

    return pl.pallas_call(
        body, grid=(T // ROW_BLOCK,),
        in_specs=[pl.BlockSpec((ROW_BLOCK, D), lambda i: (i, 0)), pl.BlockSpec((1, D), lambda i: (0, 0))],
        out_specs=pl.BlockSpec((ROW_BLOCK, D), lambda i: (i, 0)),
        out_shape=jax.ShapeDtypeStruct((T, D), BF16), name=name, compiler_params=_cparams(("parallel",)),
    )(x, g)


def _rmsnorm_bwd(dh, x, g, dres, name, with_bf16=True):
    T, D = x.shape

    def body(dh_ref, x_ref, g_ref, dres_ref, dx_ref, *rest):
        dg_ref = rest[-1]
        i = pl.program_id(0)
        xf = x_ref[...]
        r = lax.rsqrt(jnp.mean(xf * xf, axis=-1, keepdims=True) + EPS)
        xh = xf * r
        d = dh_ref[...]

        @pl.when(i == 0)
        def _():
            dg_ref[...] = jnp.zeros_like(dg_ref)

        dg_ref[...] += jnp.sum(d * xh, axis=0, keepdims=True)
        dxh = d * g_ref[...]
        dx = r * (dxh - xh * jnp.mean(dxh * xh, axis=-1, keepdims=True)) + dres_ref[...]
        dx_ref[...] = dx
        if with_bf16:
            rest[0][...] = dx.astype(BF16)

    row = pl.BlockSpec((ROW_BLOCK, D), lambda i: (i, 0))
    vec = pl.BlockSpec((1, D), lambda i: (0, 0))
    copies = [(row, jax.ShapeDtypeStruct((T, D), BF16))] if with_bf16 else []
    return pl.pallas_call(
        body, grid=(T // ROW_BLOCK,), in_specs=[row, row, vec, row],
        out_specs=[row, *[c[0] for c in copies], vec],
        out_shape=[jax.ShapeDtypeStruct((T, D), F32), *[c[1] for c in copies], jax.ShapeDtypeStruct((1, D), F32)],
        name=name, compiler_params=_cparams(("arbitrary",)),
    )(dh, x, g, dres)


def _final_loss(x3_halves, tgt, g, name):
    T, D = tgt.shape

    def body(xa_ref, xb_ref, t_ref, g_ref, loss_ref, dg_ref, dx_ref, dxb_ref):
        i = pl.program_id(0)
        xf = jnp.concatenate([xa_ref[...], xb_ref[...]], axis=1)
        r = lax.rsqrt(jnp.mean(xf * xf, axis=-1, keepdims=True) + EPS)
        xh = xf * r
        gg = g_ref[...]
        err = xh * gg - t_ref[...]

        @pl.when(i == 0)
        def _():
            dg_ref[...] = jnp.zeros_like(dg_ref)
            loss_ref[...] = jnp.zeros_like(loss_ref)

        part = jnp.sum(jnp.sum(err * err, axis=-1, keepdims=True), axis=0, keepdims=True) * (0.5 / D)
        loss_ref[...] += jnp.broadcast_to(part, loss_ref.shape)
        dout = err * (1.0 / D)
        dg_ref[...] += jnp.sum(dout * xh, axis=0, keepdims=True)
        dxh = dout * gg
        dx = r * (dxh - xh * jnp.mean(dxh * xh, axis=-1, keepdims=True))
        dx_ref[...] = dx
        dxb_ref[...] = dx.astype(BF16)

    row = pl.BlockSpec((ROW_BLOCK, D), lambda i: (i, 0))
    vec = pl.BlockSpec((1, D), lambda i: (0, 0))
    return pl.pallas_call(
        body, grid=(T // ROW_BLOCK,),
        in_specs=[pl.BlockSpec((ROW_BLOCK, D // 2), lambda i: (i, 0))] * 2 + [row, vec],
        out_specs=[pl.BlockSpec((1, 128), lambda i: (0, 0)), vec, row, row],
        out_shape=[jax.ShapeDtypeStruct((1, 128), F32), jax.ShapeDtypeStruct((1, D), F32),
                   jax.ShapeDtypeStruct((T, D), F32), jax.ShapeDtypeStruct((T, D), BF16)],
        name=name, compiler_params=_cparams(("arbitrary",)),
    )(*x3_halves, tgt, g)


CONV_BLOCK = 256


def _conv_apply(u, w, b):
    row = lax.broadcasted_iota(jnp.int32, u.shape, 0)
    acc = b + w[CONV_K - 1:CONV_K, :] * u
    shifted = []
    for j in range(1, CONV_K):
        uj = jnp.where(row >= j, pltpu.roll(u, j, axis=0), 0.0)
        shifted.append(uj)
        acc = acc + w[CONV_K - 1 - j:CONV_K - j, :] * uj
    return acc, shifted


def _conv_fwd(proj, conv_w, conv_b, name):
    T = proj.shape[0]
    cb0 = OFF_X // CONV_BLOCK

    def body(u_ref, w_ref, b_ref, o_ref, ds_ref):
        c, _ = _conv_apply(u_ref[...], w_ref[...], b_ref[...])
        sg = _sigmoid(c)
        o_ref[...] = c * sg
        ds_ref[...] = sg * (1.0 + c * (1.0 - sg))

    out = pl.BlockSpec((T, CONV_BLOCK), lambda j: (0, j))
    return pl.pallas_call(
        body, grid=(CONV_DIM // CONV_BLOCK,),
        in_specs=[pl.BlockSpec((T, CONV_BLOCK), lambda j: (0, cb0 + j)),
                  pl.BlockSpec((CONV_K, CONV_BLOCK), lambda j: (0, j)),
                  pl.BlockSpec((1, CONV_BLOCK), lambda j: (0, j))],
        out_specs=[out, out], out_shape=[jax.ShapeDtypeStruct((T, CONV_DIM), F32)] * 2,
        name=name, compiler_params=_cparams(("parallel",)),
    )(proj, conv_w, conv_b)


def _conv_bwd(proj, dact, dsilu, conv_w, dproj, name):
    T = proj.shape[0]
    cb0 = OFF_X // CONV_BLOCK

    def body(u_ref, d_ref, s_ref, w_ref, _, du_ref, dw_ref, db_ref):
        u = u_ref[...]
        w = w_ref[...]
        dc = d_ref[...] * s_ref[...]
        row = lax.broadcasted_iota(jnp.int32, u.shape, 0)
        du = w[CONV_K - 1:CONV_K, :] * dc
        dw_ref[CONV_K - 1:CONV_K, :] = jnp.sum(dc * u, axis=0, keepdims=True)
        for j in range(1, CONV_K):
            dcj = jnp.where(row < T - j, pltpu.roll(dc, T - j, axis=0), 0.0)
            du = du + w[CONV_K - 1 - j:CONV_K - j, :] * dcj
            dw_ref[CONV_K - 1 - j:CONV_K - j, :] = jnp.sum(dcj * u, axis=0, keepdims=True)
        db_ref[...] = jnp.sum(dc, axis=0, keepdims=True)
        du_ref[...] = du.astype(BF16)

    blk = pl.BlockSpec((T, CONV_BLOCK), lambda j: (0, j))
    return pl.pallas_call(
        body, grid=(CONV_DIM // CONV_BLOCK,),
        in_specs=[pl.BlockSpec((T, CONV_BLOCK), lambda j: (0, cb0 + j)), blk, blk,
                  pl.BlockSpec((CONV_K, CONV_BLOCK), lambda j: (0, j)), pl.BlockSpec(memory_space=pl.ANY)],
        out_specs=[pl.BlockSpec((T, CONV_BLOCK), lambda j: (0, cb0 + j)),
                   pl.BlockSpec((CONV_K, CONV_BLOCK), lambda j: (0, j)),
                   pl.BlockSpec((1, CONV_BLOCK), lambda j: (0, j))],
        out_shape=[jax.ShapeDtypeStruct(dproj.shape, BF16), jax.ShapeDtypeStruct((CONV_K, CONV_DIM), F32),
                   jax.ShapeDtypeStruct((1, CONV_DIM), F32)],
        input_output_aliases={4: 0}, name=name, compiler_params=_cparams(("parallel",)),
    )(proj, dact, dsilu, conv_w, dproj)


GROUP_W = D_INNER // N_GROUPS
HEADS_PER_GROUP = N_HEADS // N_GROUPS


def _expand_mat():
    h = lax.broadcasted_iota(jnp.int32, (N_HEADS, D_INNER), 0)
    j = lax.broadcasted_iota(jnp.int32, (N_HEADS, D_INNER), 1)
    return (j // HEAD_DIM == h).astype(F32)


def _reduce_mat(g):
    j = lax.broadcasted_iota(jnp.int32, (GROUP_W, N_HEADS), 0)
    h = lax.broadcasted_iota(jnp.int32, (GROUP_W, N_HEADS), 1)
    return (g * HEADS_PER_GROUP + j // HEAD_DIM == h).astype(F32)


def _col16(v, h):
    lane = lax.broadcasted_iota(jnp.int32, v.shape, 1)
    return jnp.sum(jnp.where(lane == h, v, 0.0), axis=1, keepdims=True)


def _ssd_pre(dt_raw, dtT_raw, dtb, dtbT, alog, alogT):
    Q = CHUNK
    xdt = dt_raw + dtb
    dt = _softplus(xdt)
    dtT = _softplus(dtT_raw + dtbT)
    A = -jnp.exp(alog)
    AT = -jnp.exp(alogT)
    row = lax.broadcasted_iota(jnp.int32, (Q, Q), 0)
    col = lax.broadcasted_iota(jnp.int32, (Q, Q), 1)
    tril = (row >= col).astype(F32)
    triu = (row <= col).astype(F32)
    cs = _hdot(tril, dt * A, "b")
    csT = _hdot(dtT * AT, triu, "a")
    return xdt, dt, A, cs, csT, row >= col, triu


def _decay_matrix(cs, csT, h, causal):
    seg = _col16(cs, h) - csT[h:h + 1, :]
    return jnp.where(causal, jnp.exp(jnp.minimum(seg, 0.0)), 0.0)


def _ssd_in_specs(nc, rev):
    def cidx(c):
        return (nc - 1 - c) if rev else c

    return [
        pl.BlockSpec((CHUNK, D_INNER), lambda c: (cidx(c), 0)),
        pl.BlockSpec((CHUNK, 512), lambda c: (cidx(c), 2)),
        pl.BlockSpec((CHUNK, 512), lambda c: (cidx(c), 3)),
        pl.BlockSpec((CHUNK, D_INNER), lambda c: (cidx(c), 0)),
        pl.BlockSpec((CHUNK, 128), lambda c: (cidx(c), OFF_DT // 128)),
        pl.BlockSpec((N_HEADS, CHUNK), lambda c: (0, cidx(c))),
        pl.BlockSpec((1, N_HEADS), lambda c: (0, 0)),
        pl.BlockSpec((N_HEADS, 1), lambda c: (0, 0)),
        pl.BlockSpec((1, N_HEADS), lambda c: (0, 0)),
        pl.BlockSpec((N_HEADS, 1), lambda c: (0, 0)),
        pl.BlockSpec((1, D_INNER), lambda c: (0, 0)),
        pl.BlockSpec((1, D_INNER), lambda c: (0, 0)),
    ]


def _ssd_fwd(xbc, proj, dtT, dtb, dtbT, alog, alogT, dfull, ng, name):
    T = xbc.shape[0]
    nc = T // CHUNK
    Q = CHUNK

    def body(xs_ref, B_ref, C_ref, z_ref, dt_ref, dtT_ref, dtb_ref, dtbT_ref, al_ref, alT_ref, df_ref, ng_ref,
             y_ref, ypre_ref, hs_ref, h_scr):
        c = pl.program_id(0)

        @pl.when(c == 0)
        def _():
            h_scr[...] = jnp.zeros_like(h_scr)

        _, dt, _, cs, csT, causal, _ = _ssd_pre(dt_ref[:, :N_HEADS], dtT_ref[...], dtb_ref[...], dtbT_ref[...],
                                                al_ref[...], alT_ref[...])
        ex = _expand_mat()
        dt_full = _hdot(dt, ex, "a")
        cs_full = _hdot(cs, ex, "a")
        cs_last = cs_full[Q - 1:Q, :]
        xs = xs_ref[...]
        xd = xs * dt_full
        e_full = jnp.exp(cs_full)
        dec_full = jnp.exp(cs_last - cs_full)
        cd_full = jnp.exp(cs_last)
        lane_head = lax.broadcasted_iota(jnp.int32, (1, GROUP_W), 1) // HEAD_DIM
        for g in range(N_GROUPS):
            sl = slice(g * GROUP_W, (g + 1) * GROUP_W)
            Bg = B_ref[:, g * D_STATE:(g + 1) * D_STATE].astype(BF16)
            Cg = C_ref[:, g * D_STATE:(g + 1) * D_STATE].astype(BF16)
            CB = _dot_nt(Cg, Bg)
            hg = h_scr[g]
            yoff = _dot_nn(Cg, hg.astype(BF16)) * e_full[:, sl]
            xd_g = xd[:, sl]
            S = _dot_tn(Bg, (xd_g * dec_full[:, sl]).astype(BF16))
            xd_b = xd_g.astype(BF16)
            ydiag = jnp.zeros((Q, GROUP_W), F32)
            for r in range(HEADS_PER_GROUP):
                Lm = _decay_matrix(cs, csT, g * HEADS_PER_GROUP + r, causal)
                Gm = (CB * Lm).astype(BF16)
                ydiag = ydiag + _dot_nn(Gm, jnp.where(lane_head == r, xd_b, jnp.zeros_like(xd_b)))
            hs_ref[0, g] = hg
            h_scr[g] = hg * cd_full[:, sl] + S
            ypre = ydiag + yoff + xs[:, sl] * df_ref[:, sl]
            ypre_ref[:, sl] = ypre
            zg = z_ref[:, sl]
            yz = ypre * zg * _sigmoid(zg)
            rn = lax.rsqrt(jnp.mean(yz * yz, axis=-1, keepdims=True) + EPS)
            y_ref[:, sl] = (yz * rn * ng_ref[:, sl]).astype(BF16)

    return pl.pallas_call(
        body, grid=(nc,), in_specs=_ssd_in_specs(nc, False),
        out_specs=[pl.BlockSpec((CHUNK, D_INNER), lambda c: (c, 0)),
                   pl.BlockSpec((CHUNK, D_INNER), lambda c: (c, 0)),
                   pl.BlockSpec((1, N_GROUPS, D_STATE, GROUP_W), lambda c: (c, 0, 0, 0))],
        out_shape=[jax.ShapeDtypeStruct((T, D_INNER + ATTN_W), BF16), jax.ShapeDtypeStruct((T, D_INNER), F32),
                   jax.ShapeDtypeStruct((nc, N_GROUPS, D_STATE, GROUP_W), F32)],
        scratch_shapes=[pltpu.VMEM((N_GROUPS, D_STATE, GROUP_W), F32)],
        name=name, compiler_params=_cparams(("arbitrary",)),
    )(xbc, xbc, xbc, proj, proj, dtT, dtb, dtbT, alog, alogT, dfull, ng)


def _ssd_bwd(xbc, proj, dtT, dtb, dtbT, alog, alogT, dfull, ng, ypre, hs, dy, name):
    T = xbc.shape[0]
    nc = T // CHUNK
    Q = CHUNK

    def body(xs_ref, B_ref, C_ref, z_ref, dt_ref, dtT_ref, dtb_ref, dtbT_ref, al_ref, alT_ref, df_ref, ng_ref,
             ypre_ref, hs_ref, dy_ref,
             dz_ref, dxbc_ref, ddtb_ref, dal_ref, dD_ref, dng_ref, dh_scr):
        step = pl.program_id(0)

        @pl.when(step == 0)
        def _():
            dh_scr[...] = jnp.zeros_like(dh_scr)
            ddtb_ref[...] = jnp.zeros_like(ddtb_ref)
            dal_ref[...] = jnp.zeros_like(dal_ref)
            dD_ref[...] = jnp.zeros_like(dD_ref)
            dng_ref[...] = jnp.zeros_like(dng_ref)

        xdt, dt, A, cs, csT, causal, triu = _ssd_pre(dt_ref[:, :N_HEADS], dtT_ref[...], dtb_ref[...],
                                                    dtbT_ref[...], al_ref[...], alT_ref[...])
        ex = _expand_mat()
        dt_full = _hdot(dt, ex, "a")
        cs_full = _hdot(cs, ex, "a")
        cs_last = cs_full[Q - 1:Q, :]
        xs = xs_ref[...]
        xd = xs * dt_full
        e_full = jnp.exp(cs_full)
        dec_full = jnp.exp(cs_last - cs_full)
        cd_full = jnp.exp(cs_last)
        lane_head = lax.broadcasted_iota(jnp.int32, (1, GROUP_W), 1) // HEAD_DIM
        is_last = lax.broadcasted_iota(jnp.int32, (Q, 1), 0) == Q - 1
        dcs16 = jnp.zeros((Q, N_HEADS), F32)
        ddtx16 = jnp.zeros((Q, N_HEADS), F32)
        dD16 = jnp.zeros((8, N_HEADS), F32)
        lane16 = lax.broadcasted_iota(jnp.int32, (1, N_HEADS), 1)
        sub16 = lax.broadcasted_iota(jnp.int32, (N_HEADS, 1), 0)
        col_sums = jnp.zeros((N_HEADS, Q), F32)
        for g in range(N_GROUPS):
            sl = slice(g * GROUP_W, (g + 1) * GROUP_W)
            red = _reduce_mat(g)
            ypre_g = ypre_ref[:, sl]
            zg = z_ref[:, sl]
            sg = _sigmoid(zg)
            silu = zg * sg
            yz = ypre_g * silu
            rn = lax.rsqrt(jnp.mean(yz * yz, axis=-1, keepdims=True) + EPS)
            yh = yz * rn
            dy_g = dy_ref[:, sl]
            dng_ref[:, sl] += jnp.sum(dy_g * yh, axis=0, keepdims=True)
            dyh = dy_g * ng_ref[:, sl]
            dyz = rn * (dyh - yh * jnp.mean(dyh * yh, axis=-1, keepdims=True))
            dY = dyz * silu
            dz_ref[:, sl] = (dyz * ypre_g * sg * (1.0 + zg * (1.0 - sg))).astype(BF16)
            xs_g = xs[:, sl]
            xd_g = xd[:, sl]
            dec_g = dec_full[:, sl]
            cd_g = cd_full[:, sl]
            d_g = df_ref[:, sl]
            Bg = B_ref[:, g * D_STATE:(g + 1) * D_STATE].astype(BF16)
            Cg = C_ref[:, g * D_STATE:(g + 1) * D_STATE].astype(BF16)
            CB = _dot_nt(Cg, Bg)
            hg = hs_ref[0, g]
            hgb = hg.astype(BF16)
            yoff = _dot_nn(Cg, hgb) * e_full[:, sl]
            dhn = dh_scr[g]
            dhnb = dhn.astype(BF16)
            dYE = (dY * e_full[:, sl]).astype(BF16)
            dC = _dot_nt(dYE, hgb)
            dh_direct = _dot_tn(Cg, dYE)
            dXdd = _dot_nn(Bg, dhnb)
            dB = _dot_nt((xd_g * dec_g).astype(BF16), dhnb)
            dcd = jnp.sum(dhn * hg, axis=0, keepdims=True)
            dh_scr[g] = dh_direct + cd_g * dhn
            dYb = dY.astype(BF16)
            xd_b = xd_g.astype(BF16)
            dCB = jnp.zeros((Q, Q), F32)
            dXd = dXdd * dec_g
            for r in range(HEADS_PER_GROUP):
                h = g * HEADS_PER_GROUP + r
                Lm = _decay_matrix(cs, csT, h, causal)
                Gf = CB * Lm
                dYr = jnp.where(lane_head == r, dYb, jnp.zeros_like(dYb))
                dG = _dot_nt(dYr, xd_b)
                dCB = dCB + dG * Lm
                dXd = dXd + _dot_tn(Gf.astype(BF16), dYr)
                Mm = dG * Gf
                dcs16 = dcs16 + jnp.where(lane16 == h, jnp.sum(Mm, axis=1, keepdims=True), 0.0)
                col_sums = col_sums + jnp.where(sub16 == h, jnp.sum(Mm, axis=0, keepdims=True), 0.0)
            dCBb = dCB.astype(BF16)
            dC = dC + _dot_nn(dCBb, Bg)
            dB = dB + _dot_tn(dCBb, Cg)
            w_state = dXdd * dec_g * xd_g
            t_last = jnp.sum(w_state, axis=0, keepdims=True) + dcd * cd_g
            dcs_g = dY * yoff - w_state + jnp.where(is_last, t_last, 0.0)
            dcs16 = dcs16 + _hdot(dcs_g, red, "a")
            ddtx16 = ddtx16 + _hdot(dXd * xs_g, red, "a")
            dD16 = dD16 + _hdot(jnp.broadcast_to(jnp.sum(dY * xs_g, axis=0, keepdims=True), (8, GROUP_W)), red, "a")
            dxbc_ref[:, sl] = dXd * dt_full[:, sl] + dY * d_g
            dxbc_ref[:, D_INNER + g * D_STATE:D_INNER + (g + 1) * D_STATE] = dB
            dxbc_ref[:, D_INNER + 512 + g * D_STATE:D_INNER + 512 + (g + 1) * D_STATE] = dC
        eye = (lax.broadcasted_iota(jnp.int32, (N_HEADS, N_HEADS), 0)
               == lax.broadcasted_iota(jnp.int32, (N_HEADS, N_HEADS), 1)).astype(BF16)
        dcs16 = dcs16 - sum(_dot_tn(part, eye) for part in _split3(col_sums))
        da = _hdot(triu, dcs16, "b")
        ddt = da * A + ddtx16
        ddt_raw = ddt * _sigmoid(xdt)
        pr = lax.broadcasted_iota(jnp.int32, (N_HEADS, 128), 0)
        pc = lax.broadcasted_iota(jnp.int32, (N_HEADS, 128), 1)
        dz_ref[:, D_INNER:OFF_DT] = jnp.zeros((Q, OFF_DT - D_INNER), BF16)
        dz_ref[:, OFF_DT:OFF_DT + 128] = _hdot(ddt_raw, (pr == pc).astype(F32), "a").astype(BF16)
        dz_ref[:, OFF_DT + 128:] = jnp.zeros((Q, NP - OFF_DT - 128), BF16)
        ddtb_ref[...] += jnp.sum(ddt_raw, axis=0, keepdims=True)
        dal_ref[...] += jnp.sum(da * dt, axis=0, keepdims=True) * A
        dD_ref[...] += dD16[0:1, :]

    def rc(c):
        return nc - 1 - c

    in_specs = _ssd_in_specs(nc, True) + [
        pl.BlockSpec((CHUNK, D_INNER), lambda c: (rc(c), 0)),
        pl.BlockSpec((1, N_GROUPS, D_STATE, GROUP_W), lambda c: (rc(c), 0, 0, 0)),
        pl.BlockSpec((CHUNK, D_INNER), lambda c: (rc(c), 0)),
    ]
    small = pl.BlockSpec((1, N_HEADS), lambda c: (0, 0))
    return pl.pallas_call(
        body, grid=(nc,), in_specs=in_specs,
        out_specs=[pl.BlockSpec((CHUNK, NP), lambda c: (rc(c), 0)),
                   pl.BlockSpec((CHUNK, CONV_DIM), lambda c: (rc(c), 0)),
                   small, small, small,
                   pl.BlockSpec((1, D_INNER), lambda c: (0, 0))],
        out_shape=[jax.ShapeDtypeStruct((T, NP), BF16), jax.ShapeDtypeStruct((T, CONV_DIM), F32),
                   jax.ShapeDtypeStruct((1, N_HEADS), F32), jax.ShapeDtypeStruct((1, N_HEADS), F32),
                   jax.ShapeDtypeStruct((1, N_HEADS), F32), jax.ShapeDtypeStruct((1, D_INNER), F32)],
        scratch_shapes=[pltpu.VMEM((N_GROUPS, D_STATE, GROUP_W), F32)],
        name=name, compiler_params=_cparams(("arbitrary",)),
    )(xbc, xbc, xbc, proj, proj, dtT, dtb, dtbT, alog, alogT, dfull, ng, ypre, hs, dy)


N_PAIRS = ATTN_W // 128
PAIRS_PER_KV = N_PAIRS // 2
ATTN_SCALE = HEAD_DIM ** -0.5


def _kv_variants(kk):
    lo = lax.broadcasted_iota(jnp.int32, kk.shape, 1) < HEAD_DIM
    zero = jnp.zeros_like(kk)
    k00 = jnp.where(lo, kk, zero)
    k11 = jnp.where(lo, zero, kk)
    k01 = pltpu.roll(k00, HEAD_DIM, axis=1)
    k10 = pltpu.roll(k11, HEAD_DIM, axis=1)
    return [[k00.astype(BF16), k01.astype(BF16)], [k10.astype(BF16), k11.astype(BF16)]]


LOG2E = 1.4426950408889634


def _own_block():
    i = lax.broadcasted_iota(jnp.int32, (WINDOW, WINDOW), 0)
    j = lax.broadcasted_iota(jnp.int32, (WINDOW, WINDOW), 1)
    return j <= i


def _fold(own, a):
    return jnp.where(own, a[:, WINDOW:], a[:, :WINDOW])


def _attn_probs(qp, kvar, own, prev_bias, sk):
    s = _dot_nt(qp, kvar)
    sb = jnp.where(own, s[:, WINDOW:], s[:, :WINDOW] + prev_bias) * (ATTN_SCALE * LOG2E)
    sk2 = sk * LOG2E
    m = jnp.maximum(jnp.max(sb, axis=1, keepdims=True), sk2)
    pe = jnp.exp2(sb - m)
    es = jnp.exp2(sk2 - m)
    den = jnp.sum(pe, axis=1, keepdims=True) + es
    inv = 1.0 / den
    return pe * inv, es * inv


def _unfold(own, a):
    zero = jnp.zeros_like(a)
    return jnp.where(own, zero, a), jnp.where(own, a, zero)


def _sink(sinks, r):
    lane = lax.broadcasted_iota(jnp.int32, sinks.shape, 1)
    return jnp.sum(jnp.where(lane == r, sinks, 0.0), axis=1, keepdims=True)


def _kv_specs():
    return [pl.BlockSpec((WINDOW, KV_W), lambda n: (jnp.maximum(n - 1, 0), OFF_K // KV_W)),
            pl.BlockSpec((WINDOW, KV_W), lambda n: (n, OFF_K // KV_W)),
            pl.BlockSpec((WINDOW, KV_W), lambda n: (jnp.maximum(n - 1, 0), OFF_V // KV_W)),
            pl.BlockSpec((WINDOW, KV_W), lambda n: (n, OFF_V // KV_W))]


def _attn_fwd(proj, sinks, og, ycat, name):
    T = proj.shape[0]
    nb = T // WINDOW

    def body(q_ref, kp_ref, kc_ref, vp_ref, vc_ref, s_ref, og_ref, _, y_ref, o_ref, p_ref, ps_ref):
        n = pl.program_id(0)
        kv = _kv_variants(jnp.concatenate([kp_ref[...], kc_ref[...]], axis=0))
        vv = _kv_variants(jnp.concatenate([vp_ref[...], vc_ref[...]], axis=0))
        own = _own_block()
        prev_bias = jnp.where(n > 0, 0.0, NEG)
        sinks_v = s_ref[...]
        lane = lax.broadcasted_iota(jnp.int32, (1, 128), 1)
        ssq = jnp.zeros((WINDOW, 1), F32)
        sink_probs = jnp.zeros((WINDOW, 128), F32)
        for p in range(N_PAIRS):
            j = p // PAIRS_PER_KV
            qp = q_ref[:, p * 128:(p + 1) * 128].astype(BF16)
            o_pair = jnp.zeros((WINDOW, 128), F32)
            for par in range(2):
                r = 2 * p + par
                pn, ps = _attn_probs(qp, kv[j][par], own, prev_bias, _sink(sinks_v, r))
                pb = pn.astype(BF16)
                p_ref[:, r * 128:(r + 1) * 128] = pb
                sink_probs = jnp.where(lane == r, ps, sink_probs)
                p_prev, p_own = _unfold(own, pb)
                o_pair = o_pair + _dot_nn(p_prev, vv[j][par][:WINDOW]) + _dot_nn(p_own, vv[j][par][WINDOW:])
            o_ref[:, p * 128:(p + 1) * 128] = o_pair
            ssq = ssq + jnp.sum(o_pair * o_pair, axis=1, keepdims=True)
        ps_ref[...] = sink_probs
        rn = lax.rsqrt(ssq * (1.0 / ATTN_W) + EPS)
        y_ref[...] = (o_ref[...] * rn * og_ref[...]).astype(BF16)

    return pl.pallas_call(
        body, grid=(nb,),
        in_specs=[pl.BlockSpec((WINDOW, ATTN_W), lambda n: (n, OFF_Q // ATTN_W)), *_kv_specs(),
                  pl.BlockSpec((1, N_HEADS), lambda n: (0, 0)), pl.BlockSpec((1, ATTN_W), lambda n: (0, 0)), ANY],
        out_specs=[pl.BlockSpec((WINDOW, ATTN_W), lambda n: (n, 1)), pl.BlockSpec((WINDOW, ATTN_W), lambda n: (n, 0)),
                   pl.BlockSpec((WINDOW, N_HEADS * 128), lambda n: (n, 0)), pl.BlockSpec((WINDOW, 128), lambda n: (n, 0))],
        out_shape=[jax.ShapeDtypeStruct(ycat.shape, BF16), jax.ShapeDtypeStruct((T, ATTN_W), F32),
                   jax.ShapeDtypeStruct((T, N_HEADS * 128), BF16), jax.ShapeDtypeStruct((T, 128), F32)],
        input_output_aliases={7: 0}, name=name, compiler_params=_cparams(("parallel",)),
    )(proj, proj, proj, proj, proj, sinks, og, ycat)


def _attn_bwd(proj, og, o, dy, probs, sink_probs, dproj, name):
    T = proj.shape[0]
    nb = T // WINDOW

    def body(q_ref, kp_ref, kc_ref, vp_ref, vc_ref, og_ref, o_ref, dy_ref, p_ref, ps_ref, _,
             dq_ref, dk_ref, dv_ref, ds_ref, dog_ref, qt_scr, dot_scr, ds_scr, p_scr):
        n = pl.program_id(0)

        @pl.when(n == 0)
        def _():
            dk_ref[...] = jnp.zeros_like(dk_ref)
            dv_ref[...] = jnp.zeros_like(dv_ref)
            ds_ref[...] = jnp.zeros_like(ds_ref)
            dog_ref[...] = jnp.zeros_like(dog_ref)

        kv = _kv_variants(jnp.concatenate([kp_ref[...], kc_ref[...]], axis=0))
        vv = _kv_variants(jnp.concatenate([vp_ref[...], vc_ref[...]], axis=0))
        own = _own_block()
        sink_probs_v = ps_ref[...]
        of = o_ref[...]
        rn = lax.rsqrt(jnp.mean(of * of, axis=-1, keepdims=True) + EPS)
        oh = of * rn
        dyf = dy_ref[...]
        dog_ref[...] += jnp.sum(dyf * oh, axis=0, keepdims=True)
        doh = dyf * og_ref[...]
        do = rn * (doh - oh * jnp.mean(doh * oh, axis=-1, keepdims=True))
        lane = lax.broadcasted_iota(jnp.int32, (1, 128), 1)
        lane16 = lax.broadcasted_iota(jnp.int32, (1, N_HEADS), 1)
        dsink = jnp.zeros((1, N_HEADS), F32)
        for p in range(N_PAIRS):
            j = p // PAIRS_PER_KV
            q_t = q_ref[:, p * 128:(p + 1) * 128].T.astype(BF16)
            do_p = do[:, p * 128:(p + 1) * 128]
            o_p = of[:, p * 128:(p + 1) * 128]
            do_b = do_p.astype(BF16)
            do_t = do_p.T.astype(BF16)
            prod = do_p * o_p
            dq_pair = jnp.zeros((WINDOW, 128), F32)
            for par in range(2):
                r = 2 * p + par
                half = (lane < HEAD_DIM) if par == 0 else (lane >= HEAD_DIM)
                pb = p_ref[:, r * 128:(r + 1) * 128]
                ps = jnp.sum(jnp.where(lane == r, sink_probs_v, 0.0), axis=1, keepdims=True)
                delta = jnp.sum(jnp.where(half, prod, 0.0), axis=1, keepdims=True)
                dP = _fold(own, _dot_nt(do_b, vv[j][par]))
                dS = pb.astype(F32) * (dP - delta)
                dsink = dsink + jnp.where(lane16 == r, -jnp.sum(ps * delta, axis=0, keepdims=True), 0.0)
                dS_parts = _unfold(own, (dS * ATTN_SCALE).astype(BF16))
                p_parts = _unfold(own, pb)
                at = ((p % PAIRS_PER_KV) * 2 + par) * WINDOW
                qt_scr[j, :, at:at + WINDOW] = q_t[par * HEAD_DIM:(par + 1) * HEAD_DIM]
                dot_scr[j, :, at:at + WINDOW] = do_t[par * HEAD_DIM:(par + 1) * HEAD_DIM]
                for blk in range(2):
                    dq_pair = dq_pair + _dot_nn(dS_parts[blk], kv[j][par][blk * WINDOW:(blk + 1) * WINDOW])
                    ds_scr[j, blk, at:at + WINDOW, :] = dS_parts[blk]
                    p_scr[j, blk, at:at + WINDOW, :] = p_parts[blk]
            dq_ref[:, p * 128:(p + 1) * 128] = dq_pair.astype(BF16)
        rows = [pl.multiple_of(jnp.maximum(n - 1, 0) * WINDOW, WINDOW), pl.multiple_of(n * WINDOW, WINDOW)]
        for lhs, rhs, ref in [(qt_scr, ds_scr, dk_ref), (dot_scr, p_scr, dv_ref)]:
            for blk in range(2):
                both_t = jnp.concatenate([_dot_nn(lhs[j], rhs[j, blk]) for j in range(2)], axis=0)
                ref[pl.ds(rows[blk], WINDOW), :] += both_t.T
        ds_ref[...] += dsink

    full_kv = pl.BlockSpec((T, KV_W), lambda n: (0, 0))
    blk = pl.BlockSpec((WINDOW, ATTN_W), lambda n: (n, 0))
    return pl.pallas_call(
        body, grid=(nb,),
        in_specs=[pl.BlockSpec((WINDOW, ATTN_W), lambda n: (n, OFF_Q // ATTN_W)), *_kv_specs(),
                  pl.BlockSpec((1, ATTN_W), lambda n: (0, 0)), blk, pl.BlockSpec((WINDOW, ATTN_W), lambda n: (n, 1)),
                  pl.BlockSpec((WINDOW, N_HEADS * 128), lambda n: (n, 0)),
                  pl.BlockSpec((WINDOW, 128), lambda n: (n, 0)), ANY],
        out_specs=[pl.BlockSpec((WINDOW, ATTN_W), lambda n: (n, OFF_Q // ATTN_W)), full_kv, full_kv,
                   pl.BlockSpec((1, N_HEADS), lambda n: (0, 0)), pl.BlockSpec((1, ATTN_W), lambda n: (0, 0))],
        out_shape=[jax.ShapeDtypeStruct(dproj.shape, BF16), jax.ShapeDtypeStruct((T, KV_W), F32),
                   jax.ShapeDtypeStruct((T, KV_W), F32), jax.ShapeDtypeStruct((1, N_HEADS), F32),
                   jax.ShapeDtypeStruct((1, ATTN_W), F32)],
        scratch_shapes=[pltpu.VMEM((2, HEAD_DIM, 8 * WINDOW), BF16), pltpu.VMEM((2, HEAD_DIM, 8 * WINDOW), BF16),
                        pltpu.VMEM((2, 2, 8 * WINDOW, WINDOW), BF16), pltpu.VMEM((2, 2, 8 * WINDOW, WINDOW), BF16)],
        input_output_aliases={10: 0}, name=name, compiler_params=_cparams(("arbitrary",)),
    )(proj, proj, proj, proj, proj, og, o, dy, probs, sink_probs, dproj)


ANY = pl.BlockSpec(memory_space=pl.ANY)


def _coords():
    return lax.axis_index("x"), lax.axis_index("y"), lax.axis_index("c")


HBM = pl.BlockSpec(memory_space=pltpu.HBM)
SEM = pl.BlockSpec(memory_space=pltpu.SEMAPHORE)
EFFECT = pltpu.SideEffectType.DATAFLOW_SIDE_EFFECTING


def _in_hbm(a):
    return pltpu.with_memory_space_constraint(a, pltpu.HBM)


def _remote_start(srcs, lands, plan, n_copies, name, after=None):
    ns, nb = len(srcs), len(srcs) + len(lands)
    n_after = 0 if after is None else 1

    def body(*refs):
        src_refs, land_refs = refs[:ns], refs[ns:nb]
        send_sems, recv_sems = refs[nb + n_after], refs[nb + n_after + 1]
        token = refs[-1]
        x, y, c = _coords()
        for i, (sv, dv, dev) in enumerate(plan(src_refs, land_refs, x, y, c)):
            pltpu.make_async_remote_copy(src_ref=sv, dst_ref=dv, send_sem=send_sems.at[i], recv_sem=recv_sems.at[i],
                                         device_id=dev, device_id_type=MESH).start()
        token[...] = jnp.zeros_like(token)

    bufs = list(srcs) + list(lands)
    outs = pl.pallas_call(
        body, name=name,
        out_shape=(pltpu.SemaphoreType.DMA((n_copies,)), pltpu.SemaphoreType.DMA((n_copies,)),
                   *[pltpu.HBM(b.shape, b.dtype) for b in bufs], jax.ShapeDtypeStruct((8, 128), F32)),
        in_specs=[HBM] * nb + [ANY] * n_after,
        out_specs=(SEM, SEM, *[HBM] * nb, pl.BlockSpec(memory_space=pltpu.VMEM)),
        input_output_aliases={i: 2 + i for i in range(nb)},
        compiler_params=pltpu.CompilerParams(has_side_effects=EFFECT),
    )(*[_in_hbm(b) for b in bufs], *([] if after is None else [after]))
    return outs[0], outs[1], list(outs[2:2 + ns]), list(outs[2 + ns:2 + nb]), outs[-1]


def _remote_wait(started, after, plan, name):
    send_sems, recv_sems, srcs, lands, _ = started
    ns, nb = len(srcs), len(srcs) + len(lands)

    def body(*refs):
        src_refs, land_refs = refs[:ns], refs[ns:nb]
        send_sems, recv_sems = refs[nb], refs[nb + 1]
        x, y, c = _coords()
        for i, (sv, dv, dev) in enumerate(plan(src_refs, land_refs, x, y, c)):
            cp = pltpu.make_async_remote_copy(src_ref=sv, dst_ref=dv, send_sem=send_sems.at[i],
                                              recv_sem=recv_sems.at[i], device_id=dev, device_id_type=MESH)
            cp.wait_send()
            cp.wait_recv()

    bufs = list(srcs) + list(lands)
    outs = pl.pallas_call(
        body, name=name, out_shape=tuple(pltpu.HBM(b.shape, b.dtype) for b in bufs),
        in_specs=[HBM] * nb + [SEM, SEM, ANY], out_specs=tuple([HBM] * nb),
        input_output_aliases={i: i for i in range(nb)},
        compiler_params=pltpu.CompilerParams(has_side_effects=EFFECT),
    )(*bufs, send_sems, recv_sems, after)
    return list(outs[:ns]), list(outs[ns:])


def _pair_plan(src_refs, land_refs, x, y, c):
    plan = []
    for s, l in zip(src_refs, land_refs):
        for q in range(4):
            plan.append((s.at[2 * q + (1 - c)], l.at[q], (x, y, 1 - c)))
    return plan


def _pair4_plan(src_refs, land_refs, x, y, c):
    plan = []
    for s, l in zip(src_refs, land_refs):
        for q in range(4):
            plan.append((s.at[q], l.at[q], (x, y, 1 - c)))
    return plan


def _chips_plan(src_refs, land_refs, x, y, c):
    plan = []
    for s, l in zip(src_refs, land_refs):
        for k, (tx, ty) in enumerate([(1 - x, y), (x, 1 - y), (1 - x, 1 - y)]):
            plan.append((s.at[2 * tx + ty], l.at[k], (tx, ty, c)))
    return plan


def _everyone_plan(src_refs, land_refs, x, y, c):
    me = 4 * x + 2 * y + c
    plan = []
    for s, l in zip(src_refs, land_refs):
        for fx, fy, fc in [(0, 0, 1), (1, 0, 0), (1, 0, 1), (0, 1, 0), (0, 1, 1), (1, 1, 0), (1, 1, 1)]:
            dev = ((1 - x) if fx else x, (1 - y) if fy else y, (1 - c) if fc else c)
            plan.append((s, l.at[me], dev))
    return plan


def _pair_add(g8, r1, csel, tr, name):
    _, R, C = r1.shape
    g4 = g8.reshape(4, 2, R, C)

    def body(c_ref, g_ref, r_ref, o_ref):
        o_ref[...] = (g_ref[...].astype(F32) + r_ref[...].astype(F32)).astype(BF16)

    return pl.pallas_call(
        body,
        grid_spec=pltpu.PrefetchScalarGridSpec(
            num_scalar_prefetch=1, grid=(4, R // tr),
            in_specs=[pl.BlockSpec((None, None, tr, C), lambda q, i, cs: (q, cs[0], i, 0)),
                      pl.BlockSpec((None, tr, C), lambda q, i, cs: (q, i, 0))],
            out_specs=pl.BlockSpec((None, tr, C), lambda q, i, cs: (q, i, 0))),
        out_shape=jax.ShapeDtypeStruct((4, R, C), BF16), name=name,
        compiler_params=_cparams(("parallel", "parallel")),
    )(csel, g4, r1)


def _adamw_math(w, g, m, v):
    m = ADAM_B1 * m + (1.0 - ADAM_B1) * g
    v = ADAM_B2 * v + (1.0 - ADAM_B2) * (g * g)
    m_hat = m / (1.0 - ADAM_B1 ** ADAM_STEP)
    v_hat = v / (1.0 - ADAM_B2 ** ADAM_STEP)
    delta = -ADAM_LR * (m_hat / (jnp.sqrt(v_hat) + ADAM_EPS) + ADAM_WD * w)
    return delta, m, v


def _adamw_big(w, m, v, p4, r3, qsel, tile, name):
    R, C = w.shape
    tr, tc = tile

    def body(q_ref, w_ref, m_ref, v_ref, p_ref, r_ref, g_out, d_out, m_out, v_out):
        g = p_ref[...].astype(F32) + r_ref[0].astype(F32) + r_ref[1].astype(F32) + r_ref[2].astype(F32)
        d, mn, vn = _adamw_math(w_ref[...], g, m_ref[...], v_ref[...])
        g_out[...] = g
        d_out[...] = d
        m_out[...] = mn
        v_out[...] = vn

    blk = pl.BlockSpec((tr, tc), lambda i, j, qs: (i, j))
    return pl.pallas_call(
        body,
        grid_spec=pltpu.PrefetchScalarGridSpec(
            num_scalar_prefetch=1, grid=(R // tr, C // tc),
            in_specs=[blk, blk, blk, pl.BlockSpec((None, tr, tc), lambda i, j, qs: (qs[0], i, j)),
                      pl.BlockSpec((3, tr, tc), lambda i, j, qs: (0, i, j))],
            out_specs=[blk, blk, blk, blk]),
        out_shape=[jax.ShapeDtypeStruct((R, C), F32)] * 4, name=name,
        compiler_params=_cparams(("parallel", "parallel")),
    )(qsel, w, m, v, p4, r3)


def _sum_partials(p4, r3, qsel, tc, name):
    _, R, C = p4.shape

    def body(q_ref, p_ref, r_ref, o_ref):
        o_ref[...] = p_ref[...].astype(F32) + r_ref[0].astype(F32) + r_ref[1].astype(F32) + r_ref[2].astype(F32)

    return pl.pallas_call(
        body,
        grid_spec=pltpu.PrefetchScalarGridSpec(
            num_scalar_prefetch=1, grid=(C // tc,),
            in_specs=[pl.BlockSpec((None, R, tc), lambda j, qs: (qs[0], 0, j)),
                      pl.BlockSpec((3, R, tc), lambda j, qs: (0, 0, j))],
            out_specs=pl.BlockSpec((R, tc), lambda j, qs: (0, j))),
        out_shape=jax.ShapeDtypeStruct((R, C), F32), name=name, compiler_params=_cparams(("parallel",)),
    )(qsel, p4, r3)


def _adamw_tiled(w, g, m, v, tc, name):
    R, C = w.shape

    def body(w_ref, g_ref, m_ref, v_ref, g_out, d_out, m_out, v_out, buf, sems):
        j = pl.program_id(0)
        gv = g_ref[...]
        d, mn, vn = _adamw_math(w_ref[...], gv, m_ref[...], v_ref[...])
        cols = pl.ds(pl.multiple_of(j * tc, tc), tc)
        copies = []
        for k, (val, out) in enumerate([(gv, g_out), (d, d_out), (mn, m_out), (vn, v_out)]):
            buf[k] = val
            copies.append(pltpu.make_async_copy(buf.at[k], out.at[:, 0, cols], sems.at[k]))
            copies[-1].start()
        for c in copies:
            c.wait()

    blk = pl.BlockSpec((R, tc), lambda j: (0, j))
    return pl.pallas_call(
        body, grid=(C // tc,), in_specs=[blk] * 4, out_specs=[ANY] * 4,
        out_shape=[jax.ShapeDtypeStruct((R, 1, C), F32)] * 4,
        scratch_shapes=[pltpu.VMEM((4, R, tc), F32), pltpu.SemaphoreType.DMA((4,))],
        name=name, compiler_params=_cparams(("arbitrary",)),
    )(w, g, m, v)


def _small_sum(parts, name):
    def body(p_ref, o_ref):
        acc = p_ref[0]
        for d in range(1, N_DEV):
            acc = acc + p_ref[d]
        o_ref[...] = acc

    return pl.pallas_call(
        body, out_shape=jax.ShapeDtypeStruct(parts.shape[1:], F32), name=name,
        compiler_params=_cparams(),
    )(parts)


def _adamw_small(w, g, m, v, name):
    def body(w_ref, g_ref, m_ref, v_ref, d_out, m_out, v_out):
        d, mn, vn = _adamw_math(w_ref[...], g_ref[...], m_ref[...], v_ref[...])
        d_out[...] = d
        m_out[...] = mn
        v_out[...] = vn

    return pl.pallas_call(
        body, out_shape=[jax.ShapeDtypeStruct(w.shape, F32)] * 3, name=name, compiler_params=_cparams(),
    )(w, g, m, v)


def _row(*pieces):
    r = jnp.concatenate([p.reshape(1, -1) for p in pieces], axis=1)
    return jnp.pad(r, ((0, 0), (0, D_MODEL - r.shape[1])))


def _pack_small(mix, convb, ssmg, attng, mlpg, fing, convw, dtb, alog, dsk, sinks, extra=None):
    last = [dtb, alog, dsk, sinks] + ([extra] if extra is not None else [])
    rows = [_row(mix), _row(convb), _row(ssmg, attng), _row(mlpg), _row(fing),
            jnp.pad(convw, ((0, 0), (0, D_MODEL - convw.shape[1]))), _row(*last)]
    packed = jnp.concatenate(rows, axis=0)
    return jnp.pad(packed, ((0, SMALL_ROWS - packed.shape[0]), (0, 0)))


def _unpack_small(p, conv_n):
    return dict(
        mix_norm_g=p[0:1, :], conv_b=p[1:2, :], ssm_norm_g=p[2:3, :D_INNER], attn_out_norm_g=p[2:3, D_INNER:],
        mlp_norm_g=p[3:4, :], final_norm_g=p[4, :], conv_w=p[5:9, :conv_n][None],
        dt_bias=p[9:10, 0:16], A_log=p[9:10, 16:32], D_skip=p[9:10, 32:48], attn_sinks=p[9:10, 48:64])


WEIGHT_ORDER = ["mix_norm_g", "w_in", "conv_w", "conv_b", "dt_bias", "A_log", "D_skip", "ssm_norm_g", "attn_sinks",
                "attn_out_norm_g", "w_out", "mlp_norm_g", "w_up", "w_down", "final_norm_g"]


def _to_my_columns(w_nat):
    pad = jnp.zeros((w_nat.shape[0], NP - IN_PROJ), w_nat.dtype)
    return jnp.concatenate([w_nat[:, :NAT_DT], w_nat[:, NAT_DT + N_HEADS:], w_nat[:, NAT_DT:NAT_DT + N_HEADS], pad],
                           axis=1)


PER = IN_PROJ // N_DEV
SUPER_STEP = 544
SUPER = 576


def _natural_rows(g, lo, hi):
    segments = [(0, NAT_DT, 0), (NAT_DT, NAT_DT + N_HEADS, OFF_DT - NAT_DT), (NAT_DT + N_HEADS, IN_PROJ, -N_HEADS),
                (IN_PROJ, NP, 0)]
    pieces = [g[max(lo, a) + shift:min(hi, b) + shift] for a, b, shift in segments if max(lo, a) < min(hi, b)]
    return pieces[0] if len(pieces) == 1 else jnp.concatenate(pieces, axis=0)


def _w_in_from_super_slabs(sup):
    seam = SUPER - SUPER_STEP
    units = []
    for i in range(N_DEV):
        base = SUPER_STEP * i
        units.append((base, base + seam, sup[i, :seam] if i == 0 else sup[i - 1, SUPER_STEP:] + sup[i, :seam]))
        units.append((base + seam, base + SUPER_STEP, sup[i, seam:SUPER_STEP]))
    units.append((SUPER_STEP * N_DEV, SUPER_STEP * N_DEV + seam, sup[N_DEV - 1, SUPER_STEP:]))

    def natural(lo, hi):
        return [rows[max(lo, a) - a:min(hi, b) - a] for a, b, rows in units if max(lo, a) < min(hi, b)]

    pieces = natural(0, NAT_DT) + natural(NAT_DT + N_HEADS, IN_PROJ) + natural(NAT_DT, NAT_DT + N_HEADS)
    return jnp.concatenate(pieces + [jnp.zeros((NP - IN_PROJ, D_MODEL), sup.dtype)], axis=0)


def _to_natural_columns(w_my):
    return jnp.concatenate([w_my[:, :NAT_DT], w_my[:, OFF_DT:OFF_DT + N_HEADS], w_my[:, NAT_DT:OFF_DT]], axis=1)


SLAB = 1024


def _grad_w_up(h2, du, name, sel=None, add=None, after=None):
    T, D = h2.shape
    if sel is None:
        pick, n_slab, pre = (lambda j, *cs: j), N_DEV, None
    else:
        pre, other = sel
        pick, n_slab = (lambda j, cs: 2 * j + ((1 - cs[0]) if other else cs[0])), 4
    o_spec = pl.BlockSpec((None, D, SLAB), lambda i, j, k, *cs: (j, 0, 0))
    return _matmul(
        h2, du, mode="tn", grid=(1, n_slab, 1),
        a_spec=pl.BlockSpec((T, D), lambda i, j, k, *cs: (0, 0)),
        b_spec=pl.BlockSpec((T, SLAB), lambda i, j, k, *cs: (0, pick(j, *cs))),
        out_shapes=[jax.ShapeDtypeStruct((n_slab, D, SLAB), BF16)], out_specs=[o_spec], tile=(D, SLAB), name=name,
        extras=() if add is None else (add,), extra_specs=() if add is None else (o_spec,),
        epilogue=None if add is None else (lambda acc, r: (acc + r.astype(F32),)), after=after, prefetch=pre)[0]


def _grad_w_down(act, dx3b, name, sel=None, add=None, after=None):
    T, D = dx3b.shape
    if sel is None:
        pick, n_slab, pre = (lambda i, *cs: i), N_DEV, None
    else:
        pre, other = sel
        pick, n_slab = (lambda i, cs: 2 * i + ((1 - cs[0]) if other else cs[0])), 4
    o_spec = pl.BlockSpec((None, SLAB, D), lambda i, j, k, *cs: (i, 0, 0))
    return _matmul(
        act, dx3b, mode="tn", grid=(n_slab, 1, 1),
        a_spec=pl.BlockSpec((T, SLAB), lambda i, j, k, *cs: (0, pick(i, *cs))),
        b_spec=pl.BlockSpec((T, D), lambda i, j, k, *cs: (0, 0)),
        out_shapes=[jax.ShapeDtypeStruct((n_slab, SLAB, D), BF16)], out_specs=[o_spec], tile=(SLAB, D), name=name,
        extras=() if add is None else (add,), extra_specs=() if add is None else (o_spec,),
        epilogue=None if add is None else (lambda acc, r: (acc + r.astype(F32),)), after=after, prefetch=pre)[0]


class _FixedWeights:
    def __init__(self, w_in_p, w_out_f, w_up_s, w_down_f, conv_w_f):
        self.w = (w_in_p, w_out_f, w_up_s, w_down_f, conv_w_f)
        self.grads = {}

    def mixer_weights(self, after):
        return self.w[0], None

    def conv_weight(self, after):
        return self.w[4]

    def out_weight(self, after):
        return self.w[1]

    def up_weight(self, after):
        return self.w[2]

    def down_weight(self, h, after):
        return self.w[3][:, h * (D_MODEL // 2):(h + 1) * (D_MODEL // 2)]

    def mlp_grads(self, h2, du, act, dx3b):
        self.grads.update(w_up=_grad_w_up(h2, du, "grad_w_up"),
                          w_down=_grad_w_down(act, dx3b, "grad_w_down").reshape(D_FF, D_MODEL))
        return None

    def grad_sent(self, tag, after):
        return None

    def out_grad(self, g_out):
        self.grads.update(w_out=g_out)
        return None

    def in_grad(self, g_in):
        self.grads.update(w_in=g_in)
        return None


def _local_step(x, tgt, p, hooks):
    T = x.shape[0]
    D = D_MODEL
    h1 = _rmsnorm_fwd(x, p["mix_norm_g"], "norm_mix")
    w_in_t, token = hooks.mixer_weights(h1)
    (proj,) = _mm_simple(h1, w_in_t, mode="nt", M=T, N=NP, K=D, tm=min(T, 1024), tn=1536, tk=D, out_dtype=F32,
                         name="in_proj", after=token)
    conv_w_f = hooks.conv_weight(proj)
    xbc, dsilu = _conv_fwd(proj, conv_w_f, p["conv_b"], "conv_fwd")
    dtT = proj[:, OFF_DT:OFF_DT + N_HEADS].T
    dtbT = p["dt_bias"].T
    alogT = p["A_log"].T
    dfull = jnp.repeat(p["D_skip"], HEAD_DIM, axis=1)
    ycat, ypre, hs = _ssd_fwd(xbc, proj, dtT, p["dt_bias"], dtbT, p["A_log"], alogT, dfull, p["ssm_norm_g"],
                              "ssd_fwd")
    ycat, o_att, probs, sink_probs = _attn_fwd(proj, p["attn_sinks"], p["attn_out_norm_g"], ycat, "attn_fwd")
    w_out_f = hooks.out_weight(ycat)
    tm = min(T, 1024)
    def residual_and_norm(acc, res, gain):
        x2 = acc + res
        return x2, x2 * lax.rsqrt(jnp.mean(x2 * x2, axis=-1, keepdims=True) + EPS) * gain

    rows = min(T, 512)
    x2, h2 = _matmul(
        ycat, w_out_f, mode="nn", grid=(T // rows, 1, 1),
        a_spec=pl.BlockSpec((rows, D), lambda i, j, k: (i, 0)), b_spec=pl.BlockSpec((D, D), lambda i, j, k: (0, 0)),
        out_shapes=[jax.ShapeDtypeStruct((T, D), F32), jax.ShapeDtypeStruct((T, D), BF16)],
        out_specs=[pl.BlockSpec((rows, D), lambda i, j, k: (i, 0))] * 2, tile=(rows, D), name="out_proj",
        extras=(x, p["mlp_norm_g"]),
        extra_specs=[pl.BlockSpec((rows, D), lambda i, j, k: (i, 0)), pl.BlockSpec((1, D), lambda i, j, k: (0, 0))],
        epilogue=residual_and_norm)
    w_up_s = hooks.up_weight(h2)
    grid = (T // tm, N_DEV, 1)
    u, act = _matmul(
        h2, w_up_s, mode="nn", grid=grid,
        a_spec=pl.BlockSpec((tm, D), lambda i, j, k: (i, 0)),
        b_spec=pl.BlockSpec((None, D, 1024), lambda i, j, k: (j, 0, 0)),
        out_shapes=[jax.ShapeDtypeStruct((T, D_FF), F32), jax.ShapeDtypeStruct((T, D_FF), BF16)],
        out_specs=[pl.BlockSpec((tm, 1024), lambda i, j, k: (i, j))] * 2, tile=(tm, 1024), name="mlp_up",
        epilogue=lambda acc: (acc, jnp.square(jnp.maximum(acc, 0.0))))
    half = D // 2
    w_down_halves, x3_halves = [], []
    for h in range(2):
        w_down_halves.append(hooks.down_weight(h, act if h == 0 else x3_halves[0]))
        x3_halves.append(_matmul(
            act, w_down_halves[h], mode="nn", grid=(T // tm, 1, D_FF // 2048),
            a_spec=pl.BlockSpec((tm, 2048), lambda i, j, k: (i, k)),
            b_spec=pl.BlockSpec((2048, half), lambda i, j, k: (k, 0)),
            out_shapes=[jax.ShapeDtypeStruct((T, half), F32)],
            out_specs=[pl.BlockSpec((tm, half), lambda i, j, k: (i, 0))], tile=(tm, half), name=f"mlp_down_{h}",
            extras=(x2,), extra_specs=[pl.BlockSpec((tm, half), lambda i, j, k, h=h: (i, h))],
            epilogue=lambda acc, res: (acc + res,))[0])
    loss_part, d_fin, dx3, dx3b = _final_loss(x3_halves, tgt, p["final_norm_g"].reshape(1, D), "loss_head")
    (du,) = _matmul(
        dx3b, tuple(w_down_halves), mode="nt", grid=(T // tm, D_FF // 1024, 1),
        a_spec=pl.BlockSpec((tm, D), lambda i, j, k: (i, 0)),
        b_spec=(pl.BlockSpec((1024, half), lambda i, j, k: (j, 0)),) * 2,
        out_shapes=[jax.ShapeDtypeStruct((T, D_FF), BF16)],
        out_specs=[pl.BlockSpec((tm, 1024), lambda i, j, k: (i, j))], tile=(tm, 1024), name="mlp_down_bwd",
        extras=(u,), extra_specs=[pl.BlockSpec((tm, 1024), lambda i, j, k: (i, j))],
        epilogue=lambda acc, uu: (acc * (2.0 * jnp.maximum(uu, 0.0)),),
        dot_fn=lambda a, b0, b1: _dot_nt(a[:, :half], b0) + _dot_nt(a[:, half:], b1))
    token = hooks.mlp_grads(h2, du, act, dx3b)
    (dh2,) = _matmul(
        du, w_up_s, mode="nt", grid=(T // tm, D // 1024, N_DEV // 2),
        a_spec=pl.BlockSpec((tm, 2048), lambda i, j, k: (i, k)),
        b_spec=pl.BlockSpec((2, 1024, 1024), lambda i, j, k: (k, j, 0)),
        out_shapes=[jax.ShapeDtypeStruct((T, D), F32)],
        out_specs=[pl.BlockSpec((tm, 1024), lambda i, j, k: (i, j))], tile=(tm, 1024), name="mlp_up_bwd",
        after=token, dot_fn=lambda a, b: _dot_nt(a[:, :1024], b[0]) + _dot_nt(a[:, 1024:], b[1]))
    dx2, dx2b, d_mlp = _rmsnorm_bwd(dh2, x2, p["mlp_norm_g"], dx3, "norm_mlp_bwd")
    (g_out,) = _mm_simple(ycat, dx2b, mode="tn", M=D, N=D, K=T, tm=1024, tn=1024, tk=T, out_dtype=BF16,
                          name="grad_w_out")
    token = hooks.out_grad(g_out)
    (dy,) = _mm_simple(dx2b, w_out_f, mode="nt", M=T, N=D, K=D, tm=tm, tn=1024, tk=D, out_dtype=F32,
                       name="out_proj_bwd", after=token)
    token = hooks.grad_sent("out", dy)
    ssm_g = p["ssm_norm_g"] if token is None else p["ssm_norm_g"] + token[0:1, 0:1]
    dproj, dxbc_act, d_dtb, d_alog, d_dskip, d_ssmg = _ssd_bwd(
        xbc, proj, dtT, p["dt_bias"], dtbT, p["A_log"], alogT, dfull, ssm_g, ypre, hs, dy, "ssd_bwd")
    dproj, d_convw, d_convb = _conv_bwd(proj, dxbc_act, dsilu, conv_w_f, dproj, "conv_bwd")
    dproj, dk, dv, d_sinks, d_attng = _attn_bwd(proj, p["attn_out_norm_g"], o_att, dy, probs, sink_probs, dproj,
                                                "attn_bwd")
    dproj = lax.dynamic_update_slice(dproj, jnp.concatenate([dk, dv], axis=1).astype(BF16), (0, OFF_K))
    (g_in,) = _mm_simple(dproj, h1, mode="tn", M=NP, N=D, K=T, tm=1536, tn=1024, tk=T, out_dtype=BF16,
                         name="grad_w_in")
    token = hooks.in_grad(g_in)
    (dh1,) = _mm_simple(dproj, w_in_t, mode="nn", M=T, N=D, K=NP, tm=tm, tn=1024, tk=2304, out_dtype=F32,
                        name="in_proj_bwd", after=token)
    token = hooks.grad_sent("in", dh1)
    mix_g = p["mix_norm_g"] if token is None else p["mix_norm_g"] + token[0:1, 0:1]
    dx, d_mix = _rmsnorm_bwd(dh1, x, mix_g, dx2, "norm_mix_bwd", with_bf16=False)
    small = _pack_small(d_mix, d_convb, d_ssmg, d_attng, d_mlp, d_fin, d_convw, d_dtb, d_alog, d_dskip, d_sinks,
                        extra=loss_part[:, 0:1])
    return dx, small


def _rows_rotated(v, shift, name):
    R, C = v.shape
    tc = 512

    def body(s_ref, v_ref, o_ref):
        o_ref[...] = pltpu.roll(v_ref[...], s_ref[0], axis=0).astype(BF16)

    return pl.pallas_call(
        body,
        grid_spec=pltpu.PrefetchScalarGridSpec(
            num_scalar_prefetch=1, grid=(C // tc,), in_specs=[pl.BlockSpec((R, tc), lambda j, s: (0, j))],
            out_specs=pl.BlockSpec((R, tc), lambda j, s: (0, j))),
        out_shape=jax.ShapeDtypeStruct((R, C), BF16), name=name, compiler_params=_cparams(("parallel",)),
    )(shift, v)


def _landing(own, me):
    zone = lax.empty((N_DEV,) + own.shape, own.dtype)
    return lax.dynamic_update_slice(zone, own[None], (me,) + (0,) * own.ndim)


def _sequencer_gather(owns, split, me, collective_id, name):
    n = len(owns)
    zone_refs = [jax.new_ref(_landing(o, me), memory_space=pltpu.MemorySpace.HBM) for o in owns]
    own_refs = [jax.new_ref(o, memory_space=pltpu.MemorySpace.HBM) for o in owns]
    N_COPIES = 9

    @pl.kernel(mesh=plsc.ScalarSubcoreMesh(axis_name="sequencer", num_cores=1), name=name,
               scratch_types=(pltpu.SemaphoreType.DMA((n, N_COPIES)), pltpu.SemaphoreType.DMA((n, N_COPIES))),
               compiler_params=pltpu.CompilerParams(collective_id=collective_id))
    def launch(send_sems, recv_sems):
        x, y, c = _coords()
        sibling, xn, yn, diag = (x, y, 1 - c), (1 - x, y, c), (x, 1 - y, c), (1 - x, 1 - y, c)
        barrier = pltpu.get_barrier_semaphore()
        for peer in [sibling, xn, yn, diag]:
            pl.semaphore_signal(barrier, inc=1, device_id=peer, device_id_type=MESH)
        pl.semaphore_wait(barrier, 4)

        def block(a, dev, half=None):
            ref = zone_refs[a].at[4 * dev[0] + 2 * dev[1] + dev[2]]
            if half is None:
                return ref
            rows = owns[a].shape[0] // 2
            return ref.at[pl.ds(half * rows, rows)]

        def copy(a, k, src, dst, to):
            return pltpu.make_async_remote_copy(src_ref=src, dst_ref=dst, send_sem=send_sems.at[a, k],
                                                recv_sem=recv_sems.at[a, k], device_id=to, device_id_type=MESH)

        me_dev = (x, y, c)
        sent = []
        first = {}
        for a in range(n):
            for k, peer in enumerate([sibling, xn, yn] + ([] if split[a] else [diag])):
                first[a, k] = copy(a, k, own_refs[a], block(a, me_dev), peer)
                first[a, k].start()
                sent.append(first[a, k])
        from_sibling = []
        for a in range(n):
            first[a, 1].wait_recv()
            sent.append(copy(a, 4, block(a, xn), block(a, xn), sibling))
            if split[a]:
                sent.append(copy(a, 6, block(a, xn, 0), block(a, xn, 0), yn))
            first[a, 2].wait_recv()
            sent.append(copy(a, 5, block(a, yn), block(a, yn), sibling))
            if split[a]:
                sent.append(copy(a, 7, block(a, yn, 1), block(a, yn, 1), xn))
            for cp in sent[-(4 if split[a] else 2):]:
                cp.start()
        for a in range(n):
            if split[a]:
                copy(a, 6, block(a, diag, 0), block(a, diag, 0), yn).wait_recv()
                sent.append(copy(a, 8, block(a, diag, 0), block(a, diag, 0), sibling))
                sent[-1].start()
                copy(a, 7, block(a, diag, 1), block(a, diag, 1), xn).wait_recv()
                sent.append(copy(a, 3, block(a, diag, 1), block(a, diag, 1), sibling))
                sent[-1].start()
            else:
                first[a, 3].wait_recv()
                sent.append(copy(a, 8, block(a, diag), block(a, diag), sibling))
                sent[-1].start()
        for a in range(n):
            first[a, 0].wait_recv()
            copy(a, 4, block(a, xn), block(a, xn), sibling).wait_recv()
            copy(a, 5, block(a, yn), block(a, yn), sibling).wait_recv()
            if split[a]:
                copy(a, 8, block(a, diag, 0), block(a, diag, 0), sibling).wait_recv()
                copy(a, 3, block(a, diag, 1), block(a, diag, 1), sibling).wait_recv()
            else:
                copy(a, 8, block(a, diag), block(a, diag), sibling).wait_recv()
        for cp in sent:
            cp.wait_send()

    launch()
    return zone_refs


class _ShardedWeights:
    def __init__(self, w_in, w_out, conv_w, w_up, w_down, me, csel):
        self.me, self.csel = me, csel
        padded = jnp.pad(jnp.transpose(w_in), ((0, SUPER - PER), (0, 0)))
        own_rows = _rows_rotated(padded, jnp.reshape(2 * me, (1,)).astype(jnp.int32), "w_in_super_slab")
        (self.in_ref,) = _sequencer_gather([own_rows], [True], me, 7, "gather_w_in_sequencer")
        self.out_ref, self.conv_ref = _sequencer_gather([w_out.astype(BF16), conv_w], [True, False], me, 8,
                                                        "gather_w_out_sequencer")
        (self.up_ref,) = _sequencer_gather([w_up.astype(BF16)], [True], me, 9, "gather_w_up_sequencer")
        down = w_down.astype(BF16)
        self.down_refs = [_sequencer_gather([down[:, h * (D_MODEL // 2):(h + 1) * (D_MODEL // 2)]], [True], me, 10 + h,
                                            f"gather_w_down_{h}_sequencer")[0] for h in range(2)]
        self.reduces = {}
        self.pairs = {}

    def mixer_weights(self, after):
        return _w_in_from_super_slabs(self.in_ref[...]), None

    def conv_weight(self, after):
        g_conv = self.conv_ref[...]
        return jnp.concatenate([g_conv[i] for i in range(N_DEV)], axis=1)

    def out_weight(self, after):
        return self.out_ref[...].reshape(D_MODEL, D_MODEL)

    def up_weight(self, after):
        return self.up_ref[...]

    def down_weight(self, h, after):
        return self.down_refs[h][...].reshape(D_FF, D_MODEL // 2)

    def _chips_start(self, slabs, from_sibling, rows, tag):
        sums = [_pair_add(s, r, self.csel, tr, f"pair_add_{tag}_{i}")
                for i, (s, r, tr) in enumerate(zip(slabs, from_sibling, rows))]
        lands = [lax.empty((3,) + s.shape[1:], s.dtype) for s in sums]
        self.reduces[tag] = _remote_start(sums, lands, _chips_plan, 3 * len(sums), f"reduce_start_{tag}")
        return self.reduces[tag][4]

    def mlp_grads(self, h2, du, act, dx3b):
        def send(part, tag, after):
            st = _remote_start([part], [lax.empty(part.shape, part.dtype)], _pair4_plan, 4,
                               f"reduce_pair_start_{tag}", after=after)
            return st

        def received(st, after, tag):
            return _remote_wait(st, after, _pair4_plan, f"reduce_pair_wait_{tag}")[1][0]

        def to_chips(sums, tag):
            self.reduces[tag] = _remote_start([sums], [lax.empty((3,) + sums.shape[1:], sums.dtype)], _chips_plan, 3,
                                              f"reduce_start_{tag}")
            return self.reduces[tag][4]

        up_send = _grad_w_up(h2, du, "grad_w_up_send", sel=(self.csel, True))
        st_up = send(up_send, "up", None)
        down_send = _grad_w_down(act, dx3b, "grad_w_down_send", sel=(self.csel, True), after=st_up[4])
        st_down = send(down_send, "down", None)
        up_sum = _grad_w_up(h2, du, "grad_w_up_keep", sel=(self.csel, False), add=received(st_up, down_send, "up"),
                            after=st_down[4])
        token = to_chips(up_sum, "up")
        down_sum = _grad_w_down(act, dx3b, "grad_w_down_keep", sel=(self.csel, False),
                                add=received(st_down, up_sum, "down"), after=token)
        return to_chips(down_sum, "down")

    def _pair_start(self, slabs, tag):
        land = lax.empty((4,) + slabs.shape[1:], slabs.dtype)
        self.pairs[tag] = _remote_start([slabs], [land], _pair_plan, 4, f"reduce_pair_start_{tag}")
        return self.pairs[tag][4]

    def grad_sent(self, tag, after):
        slabs, from_sibling = _remote_wait(self.pairs[tag], after, _pair_plan, f"reduce_pair_wait_{tag}")
        return self._chips_start(slabs, from_sibling, [slabs[0].shape[1]], tag)

    def out_grad(self, g_out):
        return self._pair_start(g_out.reshape(N_DEV, D_MODEL // N_DEV, D_MODEL), "out")

    def in_grad(self, g_in):
        return self._pair_start(
            jnp.stack([_natural_rows(g_in, SUPER_STEP * j, SUPER_STEP * j + SUPER) for j in range(N_DEV)]), "in")

    def small_start(self, small):
        self.st_small = _remote_start([small], [_landing(small, self.me)], _everyone_plan, N_DEV - 1, "gather_start_small")

    def small_end(self, after):
        return _remote_wait(self.st_small, after, _everyone_plan, "gather_small_wait")[1][0]

    def reduce_end(self, tag, after):
        return _remote_wait(self.reduces[tag], after, _chips_plan, f"reduce_wait_{tag}")


def kernel(x, mix_norm_g, w_in, conv_w, conv_b, dt_bias, A_log, D_skip, ssm_norm_g, attn_sinks, attn_out_norm_g, w_out, mlp_norm_g, w_up, w_down, final_norm_g, loss_target, m_mix_norm_g, m_w_in, m_conv_w, m_conv_b, m_dt_bias, m_A_log, m_D_skip, m_ssm_norm_g, m_attn_sinks, m_attn_out_norm_g, m_w_out, m_mlp_norm_g, m_w_up, m_w_down, m_final_norm_g, v_mix_norm_g, v_w_in, v_conv_w, v_conv_b, v_dt_bias, v_A_log, v_D_skip, v_ssm_norm_g, v_attn_sinks, v_attn_out_norm_g, v_w_out, v_mlp_norm_g, v_w_up, v_w_down, v_final_norm_g):
    xi, yi, ci = _coords()
    me = 4 * xi + 2 * yi + ci
    csel = jnp.reshape(ci, (1,)).astype(jnp.int32)
    qsel = jnp.reshape(2 * xi + yi, (1,)).astype(jnp.int32)
    w = dict(mix_norm_g=mix_norm_g, conv_b=conv_b, dt_bias=dt_bias, A_log=A_log, D_skip=D_skip,
             ssm_norm_g=ssm_norm_g, attn_sinks=attn_sinks, attn_out_norm_g=attn_out_norm_g, mlp_norm_g=mlp_norm_g,
             final_norm_g=final_norm_g)
    hooks = _ShardedWeights(w_in[0], w_out[0], conv_w[0], w_up[0], w_down[0], me, csel)
    p = dict(w)
    dx, small = _local_step(x[0], loss_target[0], p, hooks)
    hooks.small_start(small)
    big = {}
    after = dx
    for name, wt, mt, vt, tile in [
            ("up", w_up, m_w_up, v_w_up, (512, SLAB)), ("down", w_down, m_w_down, v_w_down, (256, D_MODEL)),
            ("out", w_out, m_w_out, v_w_out, (256, D_MODEL))]:
        (chip_sums,), (from_chips,) = hooks.reduce_end(name, after)
        res = _adamw_big(wt[0], mt[0], vt[0], chip_sums, from_chips, qsel, tile, f"adamw_w_{name}")
        big["w_" + name] = tuple(r[None] for r in res)
        after = res[0]
    (chip_sums,), (from_chips,) = hooks.reduce_end("in", after)
    g_super = _sum_partials(chip_sums, from_chips, qsel, 512, "grad_w_in_sum")
    g_in = lax.dynamic_slice(g_super, (2 * me, 0), (PER, D_MODEL))
    res = _adamw_tiled(jnp.transpose(w_in[0]), g_in, jnp.transpose(m_w_in[0]), jnp.transpose(v_w_in[0]), 512,
                       "adamw_w_in")
    big["w_in"] = tuple(jnp.transpose(r, (1, 2, 0)) for r in res)
    after = res[0]
    gsum = _small_sum(hooks.small_end(after), "small_sum")
    loss = gsum[9, 64]
    gs = _unpack_small(gsum, CONV_DIM)
    cw = CONV_DIM // N_DEV
    g_conv_shard = lax.dynamic_slice(gsum[5:9, :], (0, me * cw), (CONV_K, cw))

    def pack(s):
        return _pack_small(s["mix_norm_g"], s["conv_b"], s["ssm_norm_g"], s["attn_out_norm_g"], s["mlp_norm_g"],
                           s["final_norm_g"], s["conv_w"][0], s["dt_bias"], s["A_log"], s["D_skip"], s["attn_sinks"])

    wp = pack(dict(w, conv_w=conv_w))
    mp = pack(dict(mix_norm_g=m_mix_norm_g, conv_b=m_conv_b, ssm_norm_g=m_ssm_norm_g,
                   attn_out_norm_g=m_attn_out_norm_g, mlp_norm_g=m_mlp_norm_g, final_norm_g=m_final_norm_g,
                   conv_w=m_conv_w, dt_bias=m_dt_bias, A_log=m_A_log, D_skip=m_D_skip, attn_sinks=m_attn_sinks))
    vp = pack(dict(mix_norm_g=v_mix_norm_g, conv_b=v_conv_b, ssm_norm_g=v_ssm_norm_g,
                   attn_out_norm_g=v_attn_out_norm_g, mlp_norm_g=v_mlp_norm_g, final_norm_g=v_final_norm_g,
                   conv_w=v_conv_w, dt_bias=v_dt_bias, A_log=v_A_log, D_skip=v_D_skip, attn_sinks=v_attn_sinks))
    gp = jnp.concatenate([gsum[0:5], jnp.pad(g_conv_shard, ((0, 0), (0, D_MODEL - cw))), gsum[9:10],
                          jnp.zeros((SMALL_ROWS - 10, D_MODEL), F32)], axis=0)
    dp, mnp, vnp = _adamw_small(wp, gp, mp, vp, "adamw_small")
    grads = dict(gs, conv_w=g_conv_shard[None])
    deltas = _unpack_small(dp, cw)
    new_m = _unpack_small(mnp, cw)
    new_v = _unpack_small(vnp, cw)
    for k, name in enumerate(["w_in", "w_out", "w_up", "w_down"]):
        grads[name], deltas[name], new_m[name], new_v[name] = big[name]
    return (loss, dx[None], *[grads[n] for n in WEIGHT_ORDER], *[deltas[n] for n in WEIGHT_ORDER],
            *[new_m[n] for n in WEIGHT_ORDER], *[new_v[n] for n in WEIGHT_ORDER])
```

```python
import jax
import jax.numpy as jnp
from jax import lax
from jax.experimental import pallas as pl
from jax.experimental.pallas import tpu as pltpu
from jax.experimental.pallas import tpu_sc as plsc

F32 = jnp.float32
BF16 = jnp.bfloat16
MESH = pl.DeviceIdType.MESH

EPS = 1e-5
D_MODEL = 2048
D_INNER = 1024
N_HEADS = 16
HEAD_DIM = 64
N_GROUPS = 4
D_STATE = 128
CHUNK = 128
CONV_K = 4
CONV_DIM = 2048
ATTN_W = 1024
KV_W = 128
WINDOW = 128
D_FF = 8192
IN_PROJ = 4368
N_DEV = 8
NP = 4608
OFF_Z, OFF_X, OFF_B, OFF_C, OFF_Q, OFF_K, OFF_V, OFF_DT = 0, 1024, 2048, 2560, 3072, 4096, 4224, 4352
NAT_DT = 3072

ADAM_LR = 0.001
ADAM_B1 = 0.9
ADAM_B2 = 0.999
ADAM_EPS = 1e-08
ADAM_WD = 0.01
ADAM_STEP = 10

VMEM_LIMIT = 52 * 1024 * 1024
SMALL_ROWS = 16
NEG = -1e30


def _cparams(sem=None):
    return pltpu.CompilerParams(dimension_semantics=sem, vmem_limit_bytes=VMEM_LIMIT)


def _split3(v):
    hi = v.astype(BF16)
    rest = v - hi.astype(F32)
    mid = rest.astype(BF16)
    return hi, mid, (rest - mid.astype(F32)).astype(BF16)


def _hdot(a, b, data):
    if data == "a":
        sel = b.astype(BF16)
        return sum(_dot_nn(part, sel) for part in _split3(a))
    sel = a.astype(BF16)
    return sum(_dot_nn(sel, part) for part in _split3(b))


def _dot_nn(a, b):
    return lax.dot_general(a, b, (((1,), (0,)), ((), ())), preferred_element_type=F32)


def _dot_nt(a, b):
    return lax.dot_general(a, b, (((1,), (1,)), ((), ())), preferred_element_type=F32)


def _dot_tn(a, b):
    return lax.dot_general(a, b, (((0,), (0,)), ((), ())), preferred_element_type=F32)


def _softplus(v):
    return jnp.maximum(v, 0.0) + jnp.log1p(jnp.exp(-jnp.abs(v)))


def _sigmoid(v):
    return 1.0 / (1.0 + jnp.exp(-v))


def _matmul(a, b, *, mode, grid, a_spec, b_spec, out_shapes, out_specs, tile, name,
            extras=(), extra_specs=(), epilogue=None, after=None, dot_fn=None, prefetch=None):
    nk = grid[2]
    n_ex = len(extras)
    n_out = len(out_shapes)
    bs, b_specs = (b, b_spec) if isinstance(b, tuple) else ((b,), (b_spec,))
    n_in = 1 + len(bs)
    dot = dot_fn if dot_fn is not None else {"nn": _dot_nn, "nt": _dot_nt, "tn": _dot_tn}[mode]

    def finish(acc, ex_refs, out_refs):
        res = (acc,) if epilogue is None else epilogue(acc, *[e[...] for e in ex_refs])
        for o, r in zip(out_refs, res):
            o[...] = r.astype(o.dtype)

    def body(*refs):
        ex_refs = refs[n_in:n_in + n_ex]
        out_refs = refs[n_in + n_ex:n_in + n_ex + n_out]
        part = dot(*[r[...].astype(BF16) for r in refs[:n_in]])
        if nk == 1:
            finish(part, ex_refs, out_refs)
        else:
            acc_ref = refs[-1]
            k = pl.program_id(2)

            @pl.when(k == 0)
            def _():
                acc_ref[...] = part

            @pl.when(k > 0)
            def _():
                acc_ref[...] += part

            @pl.when(k == nk - 1)
            def _():
                finish(acc_ref[...], ex_refs, out_refs)

    scratch = [] if nk == 1 else [pltpu.VMEM(tile, F32)]
    n_pre = 0 if prefetch is None else 1
    tok_specs = [] if after is None else [pl.BlockSpec((8, 128), lambda *_: (0, 0))]
    tok_args = [] if after is None else [after]

    def body_with_token(*refs):
        refs = refs[n_pre:]
        body(*refs[:n_in + n_ex], *refs[n_in + n_ex + len(tok_args):])

    in_specs = [a_spec, *b_specs, *extra_specs, *tok_specs]
    params = _cparams(("parallel", "parallel", "arbitrary"))
    if prefetch is None:
        return pl.pallas_call(
            body_with_token, grid=grid, in_specs=in_specs, out_specs=list(out_specs), out_shape=list(out_shapes),
            scratch_shapes=scratch, name=name, compiler_params=params)(a, *bs, *extras, *tok_args)
    return pl.pallas_call(
        body_with_token,
        grid_spec=pltpu.PrefetchScalarGridSpec(num_scalar_prefetch=1, grid=grid, in_specs=in_specs,
                                               out_specs=list(out_specs), scratch_shapes=scratch),
        out_shape=list(out_shapes), name=name, compiler_params=params)(prefetch, a, *bs, *extras, *tok_args)


def _mm_simple(a, b, *, mode, M, N, K, tm, tn, tk, out_dtype, name, extras=(), epilogue=None, n_out=1,
               out_dtypes=None, after=None):
    grid = (M // tm, N // tn, K // tk)
    if mode == "nn":
        a_spec = pl.BlockSpec((tm, tk), lambda i, j, k: (i, k))
        b_spec = pl.BlockSpec((tk, tn), lambda i, j, k: (k, j))
    elif mode == "nt":
        a_spec = pl.BlockSpec((tm, tk), lambda i, j, k: (i, k))
        b_spec = pl.BlockSpec((tn, tk), lambda i, j, k: (j, k))
    else:
        a_spec = pl.BlockSpec((tk, tm), lambda i, j, k: (k, i))
        b_spec = pl.BlockSpec((tk, tn), lambda i, j, k: (k, j))
    o_spec = pl.BlockSpec((tm, tn), lambda i, j, k: (i, j))
    dts = out_dtypes if out_dtypes is not None else [out_dtype] * n_out
    return _matmul(a, b, mode=mode, grid=grid, a_spec=a_spec, b_spec=b_spec,
                   out_shapes=[jax.ShapeDtypeStruct((M, N), d) for d in dts],
                   out_specs=[o_spec] * len(dts), tile=(tm, tn), name=name,
                   extras=extras, extra_specs=[o_spec] * len(extras), epilogue=epilogue, after=after)


ROW_BLOCK = 256


def _rmsnorm_fwd(x, g, name):
    T, D = x.shape

    def body(x_ref, g_ref, o_ref):
        xf = x_ref[...]
        r = lax.rsqrt(jnp.mean(xf * xf, axis=-1, keepdims=True) + EPS)
        o_ref[...] = (xf * r * g_ref[...]).astype(BF16)

    return pl.pallas_call(
        body, grid=(T // ROW_BLOCK,),
        in_specs=[pl.BlockSpec((ROW_BLOCK, D), lambda i: (i, 0)), pl.BlockSpec((1, D), lambda i: (0, 0))],
        out_specs=pl.BlockSpec((ROW_BLOCK, D), lambda i: (i, 0)),
        out_shape=jax.ShapeDtypeStruct((T, D), BF16), name=name, compiler_params=_cparams(("parallel",)),
    )(x, g)


def _rmsnorm_bwd(dh, x, g, dres, name, with_bf16=True):
    T, D = x.shape

    def body(dh_ref, x_ref, g_ref, dres_ref, dx_ref, *rest):
        dg_ref = rest[-1]
        i = pl.program_id(0)
        xf = x_ref[...]
        r = lax.rsqrt(jnp.mean(xf * xf, axis=-1, keepdims=True) + EPS)
        xh = xf * r
        d = dh_ref[...]

        @pl.when(i == 0)
        def _():
            dg_ref[...] = jnp.zeros_like(dg_ref)

        dg_ref[...] += jnp.sum(d * xh, axis=0, keepdims=True)
        dxh = d * g_ref[...]
        dx = r * (dxh - xh * jnp.mean(dxh * xh, axis=-1, keepdims=True)) + dres_ref[...]
        dx_ref[...] = dx
        if with_bf16:
            rest[0][...] = dx.astype(BF16)

    row = pl.BlockSpec((ROW_BLOCK, D), lambda i: (i, 0))
    vec = pl.BlockSpec((1, D), lambda i: (0, 0))
    copies = [(row, jax.ShapeDtypeStruct((T, D), BF16))] if with_bf16 else []
    return pl.pallas_call(
        body, grid=(T // ROW_BLOCK,), in_specs=[row, row, vec, row],
        out_specs=[row, *[c[0] for c in copies], vec],
        out_shape=[jax.ShapeDtypeStruct((T, D), F32), *[c[1] for c in copies], jax.ShapeDtypeStruct((1, D), F32)],
        name=name, compiler_params=_cparams(("arbitrary",)),
    )(dh, x, g, dres)


def _final_loss(x3_halves, tgt, g, name):
    T, D = tgt.shape

    def body(xa_ref, xb_ref, t_ref, g_ref, loss_ref, dg_ref, dx_ref, dxb_ref):
        i = pl.program_id(0)
        xf = jnp.concatenate([xa_ref[...], xb_ref[...]], axis=1)
        r = lax.rsqrt(jnp.mean(xf * xf, axis=-1, keepdims=True) + EPS)
        xh = xf * r
        gg = g_ref[...]
        err = xh * gg - t_ref[...]

        @pl.when(i == 0)
        def _():
            dg_ref[...] = jnp.zeros_like(dg_ref)
            loss_ref[...] = jnp.zeros_like(loss_ref)

        part = jnp.sum(jnp.sum(err * err, axis=-1, keepdims=True), axis=0, keepdims=True) * (0.5 / D)
        loss_ref[...] += jnp.broadcast_to(part, loss_ref.shape)
        dout = err * (1.0 / D)
        dg_ref[...] += jnp.sum(dout * xh, axis=0, keepdims=True)
        dxh = dout * gg
        dx = r * (dxh - xh * jnp.mean(dxh * xh, axis=-1, keepdims=True))
        dx_ref[...] = dx
        dxb_ref[...] = dx.astype(BF16)

    row = pl.BlockSpec((ROW_BLOCK, D), lambda i: (i, 0))
    vec = pl.BlockSpec((1, D), lambda i: (0, 0))
    return pl.pallas_call(
        body, grid=(T // ROW_BLOCK,),
        in_specs=[pl.BlockSpec((ROW_BLOCK, D // 2), lambda i: (i, 0))] * 2 + [row, vec],
        out_specs=[pl.BlockSpec((1, 128), lambda i: (0, 0)), vec, row, row],
        out_shape=[jax.ShapeDtypeStruct((1, 128), F32), jax.ShapeDtypeStruct((1, D), F32),
                   jax.ShapeDtypeStruct((T, D), F32), jax.ShapeDtypeStruct((T, D), BF16)],
        name=name, compiler_params=_cparams(("arbitrary",)),
    )(*x3_halves, tgt, g)


CONV_BLOCK = 256


def _conv_apply(u, w, b):
    row = lax.broadcasted_iota(jnp.int32, u.shape, 0)
    acc = b + w[CONV_K - 1:CONV_K, :] * u
    shifted = []
    for j in range(1, CONV_K):
        uj = jnp.where(row >= j, pltpu.roll(u, j, axis=0), 0.0)
        shifted.append(uj)
        acc = acc + w[CONV_K - 1 - j:CONV_K - j, :] * uj
    return acc, shifted


def _conv_fwd(proj, conv_w, conv_b, name):
    T = proj.shape[0]
    cb0 = OFF_X // CONV_BLOCK

    def body(u_ref, w_ref, b_ref, o_ref, ds_ref):
        c, _ = _conv_apply(u_ref[...], w_ref[...], b_ref[...])
        sg = _sigmoid(c)
        o_ref[...] = c * sg
        ds_ref[...] = sg * (1.0 + c * (1.0 - sg))

    out = pl.BlockSpec((T, CONV_BLOCK), lambda j: (0, j))
    return pl.pallas_call(
        body, grid=(CONV_DIM // CONV_BLOCK,),
        in_specs=[pl.BlockSpec((T, CONV_BLOCK), lambda j: (0, cb0 + j)),
                  pl.BlockSpec((CONV_K, CONV_BLOCK), lambda j: (0, j)),
                  pl.BlockSpec((1, CONV_BLOCK), lambda j: (0, j))],
        out_specs=[out, out], out_shape=[jax.ShapeDtypeStruct((T, CONV_DIM), F32)] * 2,
        name=name, compiler_params=_cparams(("parallel",)),
    )(proj, conv_w, conv_b)


def _conv_bwd(proj, dact, dsilu, conv_w, dproj, name):
    T = proj.shape[0]
    cb0 = OFF_X // CONV_BLOCK

    def body(u_ref, d_ref, s_ref, w_ref, _, du_ref, dw_ref, db_ref):
        u = u_ref[...]
        w = w_ref[...]
        dc = d_ref[...] * s_ref[...]
        row = lax.broadcasted_iota(jnp.int32, u.shape, 0)
        du = w[CONV_K - 1:CONV_K, :] * dc
        dw_ref[CONV_K - 1:CONV_K, :] = jnp.sum(dc * u, axis=0, keepdims=True)
        for j in range(1, CONV_K):
            dcj = jnp.where(row < T - j, pltpu.roll(dc, T - j, axis=0), 0.0)
            du = du + w[CONV_K - 1 - j:CONV_K - j, :] * dcj
            dw_ref[CONV_K - 1 - j:CONV_K - j, :] = jnp.sum(dcj * u, axis=0, keepdims=True)
        db_ref[...] = jnp.sum(dc, axis=0, keepdims=True)
        du_ref[...] = du.astype(BF16)

    blk = pl.BlockSpec((T, CONV_BLOCK), lambda j: (0, j))
    return pl.pallas_call(
        body, grid=(CONV_DIM // CONV_BLOCK,),
        in_specs=[pl.BlockSpec((T, CONV_BLOCK), lambda j: (0, cb0 + j)), blk, blk,
                  pl.BlockSpec((CONV_K, CONV_BLOCK), lambda j: (0, j)), pl.BlockSpec(memory_space=pl.ANY)],
        out_specs=[pl.BlockSpec((T, CONV_BLOCK), lambda j: (0, cb0 + j)),
                   pl.BlockSpec((CONV_K, CONV_BLOCK), lambda j: (0, j)),
                   pl.BlockSpec((1, CONV_BLOCK), lambda j: (0, j))],
        out_shape=[jax.ShapeDtypeStruct(dproj.shape, BF16), jax.ShapeDtypeStruct((CONV_K, CONV_DIM), F32),
                   jax.ShapeDtypeStruct((1, CONV_DIM), F32)],
        input_output_aliases={4: 0}, name=name, compiler_params=_cparams(("parallel",)),
    )(proj, dact, dsilu, conv_w, dproj)


GROUP_W = D_INNER // N_GROUPS
HEADS_PER_GROUP = N_HEADS // N_GROUPS


def _expand_mat():
    h = lax.broadcasted_iota(jnp.int32, (N_HEADS, D_INNER), 0)
    j = lax.broadcasted_iota(jnp.int32, (N_HEADS, D_INNER), 1)
    return (j // HEAD_DIM == h).astype(F32)


def _reduce_mat(g):
    j = lax.broadcasted_iota(jnp.int32, (GROUP_W, N_HEADS), 0)
    h = lax.broadcasted_iota(jnp.int32, (GROUP_W, N_HEADS), 1)
    return (g * HEADS_PER_GROUP + j // HEAD_DIM == h).astype(F32)


def _col16(v, h):
    lane = lax.broadcasted_iota(jnp.int32, v.shape, 1)
    return jnp.sum(jnp.where(lane == h, v, 0.0), axis=1, keepdims=True)


def _ssd_pre(dt_raw, dtT_raw, dtb, dtbT, alog, alogT):
    Q = CHUNK
    xdt = dt_raw + dtb
    dt = _softplus(xdt)
    dtT = _softplus(dtT_raw + dtbT)
    A = -jnp.exp(alog)
    AT = -jnp.exp(alogT)
    row = lax.broadcasted_iota(jnp.int32, (Q, Q), 0)
    col = lax.broadcasted_iota(jnp.int32, (Q, Q), 1)
    tril = (row >= col).astype(F32)
    triu = (row <= col).astype(F32)
    cs = _hdot(tril, dt * A, "b")
    csT = _hdot(dtT * AT, triu, "a")
    return xdt, dt, A, cs, csT, row >= col, triu


def _decay_matrix(cs, csT, h, causal):
    seg = _col16(cs, h) - csT[h:h + 1, :]
    return jnp.where(causal, jnp.exp(jnp.minimum(seg, 0.0)), 0.0)


def _ssd_in_specs(nc, rev):
    def cidx(c):
        return (nc - 1 - c) if rev else c

    return [
        pl.BlockSpec((CHUNK, D_INNER), lambda c: (cidx(c), 0)),
        pl.BlockSpec((CHUNK, 512), lambda c: (cidx(c), 2)),
        pl.BlockSpec((CHUNK, 512), lambda c: (cidx(c), 3)),
        pl.BlockSpec((CHUNK, D_INNER), lambda c: (cidx(c), 0)),
        pl.BlockSpec((CHUNK, 128), lambda c: (cidx(c), OFF_DT // 128)),
        pl.BlockSpec((N_HEADS, CHUNK), lambda c: (0, cidx(c))),
        pl.BlockSpec((1, N_HEADS), lambda c: (0, 0)),
        pl.BlockSpec((N_HEADS, 1), lambda c: (0, 0)),
        pl.BlockSpec((1, N_HEADS), lambda c: (0, 0)),
        pl.BlockSpec((N_HEADS, 1), lambda c: (0, 0)),
        pl.BlockSpec((1, D_INNER), lambda c: (0, 0)),
        pl.BlockSpec((1, D_INNER), lambda c: (0, 0)),
    ]


def _ssd_fwd(xbc, proj, dtT, dtb, dtbT, alog, alogT, dfull, ng, name):
    T = xbc.shape[0]
    nc = T // CHUNK
    Q = CHUNK

    def body(xs_ref, B_ref, C_ref, z_ref, dt_ref, dtT_ref, dtb_ref, dtbT_ref, al_ref, alT_ref, df_ref, ng_ref,
             y_ref, ypre_ref, hs_ref, h_scr):
        c = pl.program_id(0)

        @pl.when(c == 0)
        def _():
            h_scr[...] = jnp.zeros_like(h_scr)

        _, dt, _, cs, csT, causal, _ = _ssd_pre(dt_ref[:, :N_HEADS], dtT_ref[...], dtb_ref[...], dtbT_ref[...],
                                                al_ref[...], alT_ref[...])
        ex = _expand_mat()
        dt_full = _hdot(dt, ex, "a")
        cs_full = _hdot(cs, ex, "a")
        cs_last = cs_full[Q - 1:Q, :]
        xs = xs_ref[...]
        xd = xs * dt_full
        e_full = jnp.exp(cs_full)
        dec_full = jnp.exp(cs_last - cs_full)
        cd_full = jnp.exp(cs_last)
        lane_head = lax.broadcasted_iota(jnp.int32, (1, GROUP_W), 1) // HEAD_DIM
        for g in range(N_GROUPS):
            sl = slice(g * GROUP_W, (g + 1) * GROUP_W)
            Bg = B_ref[:, g * D_STATE:(g + 1) * D_STATE].astype(BF16)
            Cg = C_ref[:, g * D_STATE:(g + 1) * D_STATE].astype(BF16)
            CB = _dot_nt(Cg, Bg)
            hg = h_scr[g]
            yoff = _dot_nn(Cg, hg.astype(BF16)) * e_full[:, sl]
            xd_g = xd[:, sl]
            S = _dot_tn(Bg, (xd_g * dec_full[:, sl]).astype(BF16))
            xd_b = xd_g.astype(BF16)
            ydiag = jnp.zeros((Q, GROUP_W), F32)
            for r in range(HEADS_PER_GROUP):
                Lm = _decay_matrix(cs, csT, g * HEADS_PER_GROUP + r, causal)
                Gm = (CB * Lm).astype(BF16)
                ydiag = ydiag + _dot_nn(Gm, jnp.where(lane_head == r, xd_b, jnp.zeros_like(xd_b)))
            hs_ref[0, g] = hg
            h_scr[g] = hg * cd_full[:, sl] + S
            ypre = ydiag + yoff + xs[:, sl] * df_ref[:, sl]
            ypre_ref[:, sl] = ypre
            zg = z_ref[:, sl]
            yz = ypre * zg * _sigmoid(zg)
            rn = lax.rsqrt(jnp.mean(yz * yz, axis=-1, keepdims=True) + EPS)
            y_ref[:, sl] = (yz * rn * ng_ref[:, sl]).astype(BF16)

    return pl.pallas_call(
        body, grid=(nc,), in_specs=_ssd_in_specs(nc, False),
        out_specs=[pl.BlockSpec((CHUNK, D_INNER), lambda c: (c, 0)),
                   pl.BlockSpec((CHUNK, D_INNER), lambda c: (c, 0)),
                   pl.BlockSpec((1, N_GROUPS, D_STATE, GROUP_W), lambda c: (c, 0, 0, 0))],
        out_shape=[jax.ShapeDtypeStruct((T, D_INNER + ATTN_W), BF16), jax.ShapeDtypeStruct((T, D_INNER), F32),
                   jax.ShapeDtypeStruct((nc, N_GROUPS, D_STATE, GROUP_W), F32)],
        scratch_shapes=[pltpu.VMEM((N_GROUPS, D_STATE, GROUP_W), F32)],
        name=name, compiler_params=_cparams(("arbitrary",)),
    )(xbc, xbc, xbc, proj, proj, dtT, dtb, dtbT, alog, alogT, dfull, ng)


def _ssd_bwd(xbc, proj, dtT, dtb, dtbT, alog, alogT, dfull, ng, ypre, hs, dy, name):
    T = xbc.shape[0]
    nc = T // CHUNK
    Q = CHUNK

    def body(xs_ref, B_ref, C_ref, z_ref, dt_ref, dtT_ref, dtb_ref, dtbT_ref, al_ref, alT_ref, df_ref, ng_ref,
             ypre_ref, hs_ref, dy_ref,
             dz_ref, dxbc_ref, ddtb_ref, dal_ref, dD_ref, dng_ref, dh_scr):
        step = pl.program_id(0)

        @pl.when(step == 0)
        def _():
            dh_scr[...] = jnp.zeros_like(dh_scr)
            ddtb_ref[...] = jnp.zeros_like(ddtb_ref)
            dal_ref[...] = jnp.zeros_like(dal_ref)
            dD_ref[...] = jnp.zeros_like(dD_ref)
            dng_ref[...] = jnp.zeros_like(dng_ref)

        xdt, dt, A, cs, csT, causal, triu = _ssd_pre(dt_ref[:, :N_HEADS], dtT_ref[...], dtb_ref[...],
                                                    dtbT_ref[...], al_ref[...], alT_ref[...])
        ex = _expand_mat()
        dt_full = _hdot(dt, ex, "a")
        cs_full = _hdot(cs, ex, "a")
        cs_last = cs_full[Q - 1:Q, :]
        xs = xs_ref[...]
        xd = xs * dt_full
        e_full = jnp.exp(cs_full)
        dec_full = jnp.exp(cs_last - cs_full)
        cd_full = jnp.exp(cs_last)
        lane_head = lax.broadcasted_iota(jnp.int32, (1, GROUP_W), 1) // HEAD_DIM
        is_last = lax.broadcasted_iota(jnp.int32, (Q, 1), 0) == Q - 1
        dcs16 = jnp.zeros((Q, N_HEADS), F32)
        ddtx16 = jnp.zeros((Q, N_HEADS), F32)
        dD16 = jnp.zeros((8, N_HEADS), F32)
        lane16 = lax.broadcasted_iota(jnp.int32, (1, N_HEADS), 1)
        sub16 = lax.broadcasted_iota(jnp.int32, (N_HEADS, 1), 0)
        col_sums = jnp.zeros((N_HEADS, Q), F32)
        for g in range(N_GROUPS):
            sl = slice(g * GROUP_W, (g + 1) * GROUP_W)
            red = _reduce_mat(g)
            ypre_g = ypre_ref[:, sl]
            zg = z_ref[:, sl]
            sg = _sigmoid(zg)
            silu = zg * sg
            yz = ypre_g * silu
            rn = lax.rsqrt(jnp.mean(yz * yz, axis=-1, keepdims=True) + EPS)
            yh = yz * rn
            dy_g = dy_ref[:, sl]
            dng_ref[:, sl] += jnp.sum(dy_g * yh, axis=0, keepdims=True)
            dyh = dy_g * ng_ref[:, sl]
            dyz = rn * (dyh - yh * jnp.mean(dyh * yh, axis=-1, keepdims=True))
            dY = dyz * silu
            dz_ref[:, sl] = (dyz * ypre_g * sg * (1.0 + zg * (1.0 - sg))).astype(BF16)
            xs_g = xs[:, sl]
            xd_g = xd[:, sl]
            dec_g = dec_full[:, sl]
            cd_g = cd_full[:, sl]
            d_g = df_ref[:, sl]
            Bg = B_ref[:, g * D_STATE:(g + 1) * D_STATE].astype(BF16)
            Cg = C_ref[:, g * D_STATE:(g + 1) * D_STATE].astype(BF16)
            CB = _dot_nt(Cg, Bg)
            hg = hs_ref[0, g]
            hgb = hg.astype(BF16)
            yoff = _dot_nn(Cg, hgb) * e_full[:, sl]
            dhn = dh_scr[g]
            dhnb = dhn.astype(BF16)
            dYE = (dY * e_full[:, sl]).astype(BF16)
            dC = _dot_nt(dYE, hgb)
            dh_direct = _dot_tn(Cg, dYE)
            dXdd = _dot_nn(Bg, dhnb)
            dB = _dot_nt((xd_g * dec_g).astype(BF16), dhnb)
            dcd = jnp.sum(dhn * hg, axis=0, keepdims=True)
            dh_scr[g] = dh_direct + cd_g * dhn
            dYb = dY.astype(BF16)
            xd_b = xd_g.astype(BF16)
            dCB = jnp.zeros((Q, Q), F32)
            dXd = dXdd * dec_g
            for r in range(HEADS_PER_GROUP):
                h = g * HEADS_PER_GROUP + r
                Lm = _decay_matrix(cs, csT, h, causal)
                Gf = CB * Lm
                dYr = jnp.where(lane_head == r, dYb, jnp.zeros_like(dYb))
                dG = _dot_nt(dYr, xd_b)
                dCB = dCB + dG * Lm
                dXd = dXd + _dot_tn(Gf.astype(BF16), dYr)
                Mm = dG * Gf
                dcs16 = dcs16 + jnp.where(lane16 == h, jnp.sum(Mm, axis=1, keepdims=True), 0.0)
                col_sums = col_sums + jnp.where(sub16 == h, jnp.sum(Mm, axis=0, keepdims=True), 0.0)
            dCBb = dCB.astype(BF16)
            dC = dC + _dot_nn(dCBb, Bg)
            dB = dB + _dot_tn(dCBb, Cg)
            w_state = dXdd * dec_g * xd_g
            t_last = jnp.sum(w_state, axis=0, keepdims=True) + dcd * cd_g
            dcs_g = dY * yoff - w_state + jnp.where(is_last, t_last, 0.0)
            dcs16 = dcs16 + _hdot(dcs_g, red, "a")
            ddtx16 = ddtx16 + _hdot(dXd * xs_g, red, "a")
            dD16 = dD16 + _hdot(jnp.broadcast_to(jnp.sum(dY * xs_g, axis=0, keepdims=True), (8, GROUP_W)), red, "a")
            dxbc_ref[:, sl] = dXd * dt_full[:, sl] + dY * d_g
            dxbc_ref[:, D_INNER + g * D_STATE:D_INNER + (g + 1) * D_STATE] = dB
            dxbc_ref[:, D_INNER + 512 + g * D_STATE:D_INNER + 512 + (g + 1) * D_STATE] = dC
        eye = (lax.broadcasted_iota(jnp.int32, (N_HEADS, N_HEADS), 0)
               == lax.broadcasted_iota(jnp.int32, (N_HEADS, N_HEADS), 1)).astype(BF16)
        dcs16 = dcs16 - sum(_dot_tn(part, eye) for part in _split3(col_sums))
        da = _hdot(triu, dcs16, "b")
        ddt = da * A + ddtx16
        ddt_raw = ddt * _sigmoid(xdt)
        pr = lax.broadcasted_iota(jnp.int32, (N_HEADS, 128), 0)
        pc = lax.broadcasted_iota(jnp.int32, (N_HEADS, 128), 1)
        dz_ref[:, D_INNER:OFF_DT] = jnp.zeros((Q, OFF_DT - D_INNER), BF16)
        dz_ref[:, OFF_DT:OFF_DT + 128] = _hdot(ddt_raw, (pr == pc).astype(F32), "a").astype(BF16)
        dz_ref[:, OFF_DT + 128:] = jnp.zeros((Q, NP - OFF_DT - 128), BF16)
        ddtb_ref[...] += jnp.sum(ddt_raw, axis=0, keepdims=True)
        dal_ref[...] += jnp.sum(da * dt, axis=0, keepdims=True) * A
        dD_ref[...] += dD16[0:1, :]

    def rc(c):
        return nc - 1 - c

    in_specs = _ssd_in_specs(nc, True) + [
        pl.BlockSpec((CHUNK, D_INNER), lambda c: (rc(c), 0)),
        pl.BlockSpec((1, N_GROUPS, D_STATE, GROUP_W), lambda c: (rc(c), 0, 0, 0)),
        pl.BlockSpec((CHUNK, D_INNER), lambda c: (rc(c), 0)),
    ]
    small = pl.BlockSpec((1, N_HEADS), lambda c: (0, 0))
    return pl.pallas_call(
        body, grid=(nc,), in_specs=in_specs,
        out_specs=[pl.BlockSpec((CHUNK, NP), lambda c: (rc(c), 0)),
                   pl.BlockSpec((CHUNK, CONV_DIM), lambda c: (rc(c), 0)),
                   small, small, small,
                   pl.BlockSpec((1, D_INNER), lambda c: (0, 0))],
        out_shape=[jax.ShapeDtypeStruct((T, NP), BF16), jax.ShapeDtypeStruct((T, CONV_DIM), F32),
                   jax.ShapeDtypeStruct((1, N_HEADS), F32), jax.ShapeDtypeStruct((1, N_HEADS), F32),
                   jax.ShapeDtypeStruct((1, N_HEADS), F32), jax.ShapeDtypeStruct((1, D_INNER), F32)],
        scratch_shapes=[pltpu.VMEM((N_GROUPS, D_STATE, GROUP_W), F32)],
        name=name, compiler_params=_cparams(("arbitrary",)),
    )(xbc, xbc, xbc, proj, proj, dtT, dtb, dtbT, alog, alogT, dfull, ng, ypre, hs, dy)


N_PAIRS = ATTN_W // 128
PAIRS_PER_KV = N_PAIRS // 2
ATTN_SCALE = HEAD_DIM ** -0.5


def _kv_variants(kk):
    lo = lax.broadcasted_iota(jnp.int32, kk.shape, 1) < HEAD_DIM
    zero = jnp.zeros_like(kk)
    k00 = jnp.where(lo, kk, zero)
    k11 = jnp.where(lo, zero, kk)
    k01 = pltpu.roll(k00, HEAD_DIM, axis=1)
    k10 = pltpu.roll(k11, HEAD_DIM, axis=1)
    return [[k00.astype(BF16), k01.astype(BF16)], [k10.astype(BF16), k11.astype(BF16)]]


LOG2E = 1.4426950408889634


def _own_block():
    i = lax.broadcasted_iota(jnp.int32, (WINDOW, WINDOW), 0)
    j = lax.broadcasted_iota(jnp.int32, (WINDOW, WINDOW), 1)
    return j <= i


def _fold(own, a):
    return jnp.where(own, a[:, WINDOW:], a[:, :WINDOW])


def _attn_probs(qp, kvar, own, prev_bias, sk):
    s = _dot_nt(qp, kvar)
    sb = jnp.where(own, s[:, WINDOW:], s[:, :WINDOW] + prev_bias) * (ATTN_SCALE * LOG2E)
    sk2 = sk * LOG2E
    m = jnp.maximum(jnp.max(sb, axis=1, keepdims=True), sk2)
    pe = jnp.exp2(sb - m)
    es = jnp.exp2(sk2 - m)
    den = jnp.sum(pe, axis=1, keepdims=True) + es
    inv = 1.0 / den
    return pe * inv, es * inv


def _unfold(own, a):
    zero = jnp.zeros_like(a)
    return jnp.where(own, zero, a), jnp.where(own, a, zero)


def _sink(sinks, r):
    lane = lax.broadcasted_iota(jnp.int32, sinks.shape, 1)
    return jnp.sum(jnp.where(lane == r, sinks, 0.0), axis=1, keepdims=True)


def _kv_specs():
    return [pl.BlockSpec((WINDOW, KV_W), lambda n: (jnp.maximum(n - 1, 0), OFF_K // KV_W)),
            pl.BlockSpec((WINDOW, KV_W), lambda n: (n, OFF_K // KV_W)),
            pl.BlockSpec((WINDOW, KV_W), lambda n: (jnp.maximum(n - 1, 0), OFF_V // KV_W)),
            pl.BlockSpec((WINDOW, KV_W), lambda n: (n, OFF_V // KV_W))]


def _attn_fwd(proj, sinks, og, ycat, name):
    T = proj.shape[0]
    nb = T // WINDOW

    def body(q_ref, kp_ref, kc_ref, vp_ref, vc_ref, s_ref, og_ref, _, y_ref, o_ref, p_ref, ps_ref):
        n = pl.program_id(0)
        kv = _kv_variants(jnp.concatenate([kp_ref[...], kc_ref[...]], axis=0))
        vv = _kv_variants(jnp.concatenate([vp_ref[...], vc_ref[...]], axis=0))
        own = _own_block()
        prev_bias = jnp.where(n > 0, 0.0, NEG)
        sinks_v = s_ref[...]
        lane = lax.broadcasted_iota(jnp.int32, (1, 128), 1)
        ssq = jnp.zeros((WINDOW, 1), F32)
        sink_probs = jnp.zeros((WINDOW, 128), F32)
        for p in range(N_PAIRS):
            j = p // PAIRS_PER_KV
            qp = q_ref[:, p * 128:(p + 1) * 128].astype(BF16)
            o_pair = jnp.zeros((WINDOW, 128), F32)
            for par in range(2):
                r = 2 * p + par
                pn, ps = _attn_probs(qp, kv[j][par], own, prev_bias, _sink(sinks_v, r))
                pb = pn.astype(BF16)
                p_ref[:, r * 128:(r + 1) * 128] = pb
                sink_probs = jnp.where(lane == r, ps, sink_probs)
                p_prev, p_own = _unfold(own, pb)
                o_pair = o_pair + _dot_nn(p_prev, vv[j][par][:WINDOW]) + _dot_nn(p_own, vv[j][par][WINDOW:])
            o_ref[:, p * 128:(p + 1) * 128] = o_pair
            ssq = ssq + jnp.sum(o_pair * o_pair, axis=1, keepdims=True)
        ps_ref[...] = sink_probs
        rn = lax.rsqrt(ssq * (1.0 / ATTN_W) + EPS)
        y_ref[...] = (o_ref[...] * rn * og_ref[...]).astype(BF16)

    return pl.pallas_call(
        body, grid=(nb,),
        in_specs=[pl.BlockSpec((WINDOW, ATTN_W), lambda n: (n, OFF_Q // ATTN_W)), *_kv_specs(),
                  pl.BlockSpec((1, N_HEADS), lambda n: (0, 0)), pl.BlockSpec((1, ATTN_W), lambda n: (0, 0)), ANY],
        out_specs=[pl.BlockSpec((WINDOW, ATTN_W), lambda n: (n, 1)), pl.BlockSpec((WINDOW, ATTN_W), lambda n: (n, 0)),
                   pl.BlockSpec((WINDOW, N_HEADS * 128), lambda n: (n, 0)), pl.BlockSpec((WINDOW, 128), lambda n: (n, 0))],
        out_shape=[jax.ShapeDtypeStruct(ycat.shape, BF16), jax.ShapeDtypeStruct((T, ATTN_W), F32),
                   jax.ShapeDtypeStruct((T, N_HEADS * 128), BF16), jax.ShapeDtypeStruct((T, 128), F32)],
        input_output_aliases={7: 0}, name=name, compiler_params=_cparams(("parallel",)),
    )(proj, proj, proj, proj, proj, sinks, og, ycat)


def _attn_bwd(proj, og, o, dy, probs, sink_probs, dproj, name):
    T = proj.shape[0]
    nb = T // WINDOW

    def body(q_ref, kp_ref, kc_ref, vp_ref, vc_ref, og_ref, o_ref, dy_ref, p_ref, ps_ref, _,
             dq_ref, dk_ref, dv_ref, ds_ref, dog_ref, qt_scr, dot_scr, ds_scr, p_scr):
        n = pl.program_id(0)

        @pl.when(n == 0)
        def _():
            dk_ref[...] = jnp.zeros_like(dk_ref)
            dv_ref[...] = jnp.zeros_like(dv_ref)
            ds_ref[...] = jnp.zeros_like(ds_ref)
            dog_ref[...] = jnp.zeros_like(dog_ref)

        kv = _kv_variants(jnp.concatenate([kp_ref[...], kc_ref[...]], axis=0))
        vv = _kv_variants(jnp.concatenate([vp_ref[...], vc_ref[...]], axis=0))
        own = _own_block()
        sink_probs_v = ps_ref[...]
        of = o_ref[...]
        rn = lax.rsqrt(jnp.mean(of * of, axis=-1, keepdims=True) + EPS)
        oh = of * rn
        dyf = dy_ref[...]
        dog_ref[...] += jnp.sum(dyf * oh, axis=0, keepdims=True)
        doh = dyf * og_ref[...]
        do = rn * (doh - oh * jnp.mean(doh * oh, axis=-1, keepdims=True))
        lane = lax.broadcasted_iota(jnp.int32, (1, 128), 1)
        lane16 = lax.broadcasted_iota(jnp.int32, (1, N_HEADS), 1)
        dsink = jnp.zeros((1, N_HEADS), F32)
        for p in range(N_PAIRS):
            j = p // PAIRS_PER_KV
            q_t = q_ref[:, p * 128:(p + 1) * 128].T.astype(BF16)
            do_p = do[:, p * 128:(p + 1) * 128]
            o_p = of[:, p * 128:(p + 1) * 128]
            do_b = do_p.astype(BF16)
            do_t = do_p.T.astype(BF16)
            prod = do_p * o_p
            dq_pair = jnp.zeros((WINDOW, 128), F32)
            for par in range(2):
                r = 2 * p + par
                half = (lane < HEAD_DIM) if par == 0 else (lane >= HEAD_DIM)
                pb = p_ref[:, r * 128:(r + 1) * 128]
                ps = jnp.sum(jnp.where(lane == r, sink_probs_v, 0.0), axis=1, keepdims=True)
                delta = jnp.sum(jnp.where(half, prod, 0.0), axis=1, keepdims=True)
                dP = _fold(own, _dot_nt(do_b, vv[j][par]))
                dS = pb.astype(F32) * (dP - delta)
                dsink = dsink + jnp.where(lane16 == r, -jnp.sum(ps * delta, axis=0, keepdims=True), 0.0)
                dS_parts = _unfold(own, (dS * ATTN_SCALE).astype(BF16))
                p_parts = _unfold(own, pb)
                at = ((p % PAIRS_PER_KV) * 2 + par) * WINDOW
                qt_scr[j, :, at:at + WINDOW] = q_t[par * HEAD_DIM:(par + 1) * HEAD_DIM]
                dot_scr[j, :, at:at + WINDOW] = do_t[par * HEAD_DIM:(par + 1) * HEAD_DIM]
                for blk in range(2):
                    dq_pair = dq_pair + _dot_nn(dS_parts[blk], kv[j][par][blk * WINDOW:(blk + 1) * WINDOW])
                    ds_scr[j, blk, at:at + WINDOW, :] = dS_parts[blk]
                    p_scr[j, blk, at:at + WINDOW, :] = p_parts[blk]
            dq_ref[:, p * 128:(p + 1) * 128] = dq_pair.astype(BF16)
        rows = [pl.multiple_of(jnp.maximum(n - 1, 0) * WINDOW, WINDOW), pl.multiple_of(n * WINDOW, WINDOW)]
        for lhs, rhs, ref in [(qt_scr, ds_scr, dk_ref), (dot_scr, p_scr, dv_ref)]:
            for blk in range(2):
                both_t = jnp.concatenate([_dot_nn(lhs[j], rhs[j, blk]) for j in range(2)], axis=0)
                ref[pl.ds(rows[blk], WINDOW), :] += both_t.T
        ds_ref[...] += dsink

    full_kv = pl.BlockSpec((T, KV_W), lambda n: (0, 0))
    blk = pl.BlockSpec((WINDOW, ATTN_W), lambda n: (n, 0))
    return pl.pallas_call(
        body, grid=(nb,),
        in_specs=[pl.BlockSpec((WINDOW, ATTN_W), lambda n: (n, OFF_Q // ATTN_W)), *_kv_specs(),
                  pl.BlockSpec((1, ATTN_W), lambda n: (0, 0)), blk, pl.BlockSpec((WINDOW, ATTN_W), lambda n: (n, 1)),
                  pl.BlockSpec((WINDOW, N_HEADS * 128), lambda n: (n, 0)),
                  pl.BlockSpec((WINDOW, 128), lambda n: (n, 0)), ANY],
        out_specs=[pl.BlockSpec((WINDOW, ATTN_W), lambda n: (n, OFF_Q // ATTN_W)), full_kv, full_kv,
                   pl.BlockSpec((1, N_HEADS), lambda n: (0, 0)), pl.BlockSpec((1, ATTN_W), lambda n: (0, 0))],
        out_shape=[jax.ShapeDtypeStruct(dproj.shape, BF16), jax.ShapeDtypeStruct((T, KV_W), F32),
                   jax.ShapeDtypeStruct((T, KV_W), F32), jax.ShapeDtypeStruct((1, N_HEADS), F32),
                   jax.ShapeDtypeStruct((1, ATTN_W), F32)],
        scratch_shapes=[pltpu.VMEM((2, HEAD_DIM, 8 * WINDOW), BF16), pltpu.VMEM((2, HEAD_DIM, 8 * WINDOW), BF16),
                        pltpu.VMEM((2, 2, 8 * WINDOW, WINDOW), BF16), pltpu.VMEM((2, 2, 8 * WINDOW, WINDOW), BF16)],
        input_output_aliases={10: 0}, name=name, compiler_params=_cparams(("arbitrary",)),
    )(proj, proj, proj, proj, proj, og, o, dy, probs, sink_probs, dproj)


ANY = pl.BlockSpec(memory_space=pl.ANY)


def _coords():
    return lax.axis_index("x"), lax.axis_index("y"), lax.axis_index("c")


HBM = pl.BlockSpec(memory_space=pltpu.HBM)
SEM = pl.BlockSpec(memory_space=pltpu.SEMAPHORE)
EFFECT = pltpu.SideEffectType.DATAFLOW_SIDE_EFFECTING


def _in_hbm(a):
    return pltpu.with_memory_space_constraint(a, pltpu.HBM)


def _remote_start(srcs, lands, plan, n_copies, name, after=None):
    ns, nb = len(srcs), len(srcs) + len(lands)
    n_after = 0 if after is None else 1

    def body(*refs):
        src_refs, land_refs = refs[:ns], refs[ns:nb]
        send_sems, recv_sems = refs[nb + n_after], refs[nb + n_after + 1]
        token = refs[-1]
        x, y, c = _coords()
        for i, (sv, dv, dev) in enumerate(plan(src_refs, land_refs, x, y, c)):
            pltpu.make_async_remote_copy(src_ref=sv, dst_ref=dv, send_sem=send_sems.at[i], recv_sem=recv_sems.at[i],
                                         device_id=dev, device_id_type=MESH).start()
        token[...] = jnp.zeros_like(token)

    bufs = list(srcs) + list(lands)
    outs = pl.pallas_call(
        body, name=name,
        out_shape=(pltpu.SemaphoreType.DMA((n_copies,)), pltpu.SemaphoreType.DMA((n_copies,)),
                   *[pltpu.HBM(b.shape, b.dtype) for b in bufs], jax.ShapeDtypeStruct((8, 128), F32)),
        in_specs=[HBM] * nb + [ANY] * n_after,
        out_specs=(SEM, SEM, *[HBM] * nb, pl.BlockSpec(memory_space=pltpu.VMEM)),
        input_output_aliases={i: 2 + i for i in range(nb)},
        compiler_params=pltpu.CompilerParams(has_side_effects=EFFECT),
    )(*[_in_hbm(b) for b in bufs], *([] if after is None else [after]))
    return outs[0], outs[1], list(outs[2:2 + ns]), list(outs[2 + ns:2 + nb]), outs[-1]


def _remote_wait(started, after, plan, name):
    send_sems, recv_sems, srcs, lands, _ = started
    ns, nb = len(srcs), len(srcs) + len(lands)

    def body(*refs):
        src_refs, land_refs = refs[:ns], refs[ns:nb]
        send_sems, recv_sems = refs[nb], refs[nb + 1]
        x, y, c = _coords()
        for i, (sv, dv, dev) in enumerate(plan(src_refs, land_refs, x, y, c)):
            cp = pltpu.make_async_remote_copy(src_ref=sv, dst_ref=dv, send_sem=send_sems.at[i],
                                              recv_sem=recv_sems.at[i], device_id=dev, device_id_type=MESH)
            cp.wait_send()
            cp.wait_recv()

    bufs = list(srcs) + list(lands)
    outs = pl.pallas_call(
        body, name=name, out_shape=tuple(pltpu.HBM(b.shape, b.dtype) for b in bufs),
        in_specs=[HBM] * nb + [SEM, SEM, ANY], out_specs=tuple([HBM] * nb),
        input_output_aliases={i: i for i in range(nb)},
        compiler_params=pltpu.CompilerParams(has_side_effects=EFFECT),
    )(*bufs, send_sems, recv_sems, after)
    return list(outs[:ns]), list(outs[ns:])


def _pair_plan(src_refs, land_refs, x, y, c):
    plan = []
    for s, l in zip(src_refs, land_refs):
        for q in range(4):
            plan.append((s.at[2 * q + (1 - c)], l.at[q], (x, y, 1 - c)))
    return plan


def _pair4_plan(src_refs, land_refs, x, y, c):
    plan = []
    for s, l in zip(src_refs, land_refs):
        for q in range(4):
            plan.append((s.at[q], l.at[q], (x, y, 1 - c)))
    return plan


def _chips_plan(src_refs, land_refs, x, y, c):
    plan = []
    for s, l in zip(src_refs, land_refs):
        for k, (tx, ty) in enumerate([(1 - x, y), (x, 1 - y), (1 - x, 1 - y)]):
            plan.append((s.at[2 * tx + ty], l.at[k], (tx, ty, c)))
    return plan


def _everyone_plan(src_refs, land_refs, x, y, c):
    me = 4 * x + 2 * y + c
    plan = []
    for s, l in zip(src_refs, land_refs):
        for fx, fy, fc in [(0, 0, 1), (1, 0, 0), (1, 0, 1), (0, 1, 0), (0, 1, 1), (1, 1, 0), (1, 1, 1)]:
            dev = ((1 - x) if fx else x, (1 - y) if fy else y, (1 - c) if fc else c)
            plan.append((s, l.at[me], dev))
    return plan


def _pair_add(g8, r1, csel, tr, name):
    _, R, C = r1.shape
    g4 = g8.reshape(4, 2, R, C)

    def body(c_ref, g_ref, r_ref, o_ref):
        o_ref[...] = (g_ref[...].astype(F32) + r_ref[...].astype(F32)).astype(BF16)

    return pl.pallas_call(
        body,
        grid_spec=pltpu.PrefetchScalarGridSpec(
            num_scalar_prefetch=1, grid=(4, R // tr),
            in_specs=[pl.BlockSpec((None, None, tr, C), lambda q, i, cs: (q, cs[0], i, 0)),
                      pl.BlockSpec((None, tr, C), lambda q, i, cs: (q, i, 0))],
            out_specs=pl.BlockSpec((None, tr, C), lambda q, i, cs: (q, i, 0))),
        out_shape=jax.ShapeDtypeStruct((4, R, C), BF16), name=name,
        compiler_params=_cparams(("parallel", "parallel")),
    )(csel, g4, r1)


def _adamw_math(w, g, m, v):
    m = ADAM_B1 * m + (1.0 - ADAM_B1) * g
    v = ADAM_B2 * v + (1.0 - ADAM_B2) * (g * g)
    m_hat = m / (1.0 - ADAM_B1 ** ADAM_STEP)
    v_hat = v / (1.0 - ADAM_B2 ** ADAM_STEP)
    delta = -ADAM_LR * (m_hat / (jnp.sqrt(v_hat) + ADAM_EPS) + ADAM_WD * w)
    return delta, m, v


def _adamw_big(w, m, v, p4, r3, qsel, tile, name):
    R, C = w.shape
    tr, tc = tile

    def body(q_ref, w_ref, m_ref, v_ref, p_ref, r_ref, g_out, d_out, m_out, v_out):
        g = p_ref[...].astype(F32) + r_ref[0].astype(F32) + r_ref[1].astype(F32) + r_ref[2].astype(F32)
        d, mn, vn = _adamw_math(w_ref[...], g, m_ref[...], v_ref[...])
        g_out[...] = g
        d_out[...] = d
        m_out[...] = mn
        v_out[...] = vn

    blk = pl.BlockSpec((tr, tc), lambda i, j, qs: (i, j))
    return pl.pallas_call(
        body,
        grid_spec=pltpu.PrefetchScalarGridSpec(
            num_scalar_prefetch=1, grid=(R // tr, C // tc),
            in_specs=[blk, blk, blk, pl.BlockSpec((None, tr, tc), lambda i, j, qs: (qs[0], i, j)),
                      pl.BlockSpec((3, tr, tc), lambda i, j, qs: (0, i, j))],
            out_specs=[blk, blk, blk, blk]),
        out_shape=[jax.ShapeDtypeStruct((R, C), F32)] * 4, name=name,
        compiler_params=_cparams(("parallel", "parallel")),
    )(qsel, w, m, v, p4, r3)


def _sum_partials(p4, r3, qsel, tc, name):
    _, R, C = p4.shape

    def body(q_ref, p_ref, r_ref, o_ref):
        o_ref[...] = p_ref[...].astype(F32) + r_ref[0].astype(F32) + r_ref[1].astype(F32) + r_ref[2].astype(F32)

    return pl.pallas_call(
        body,
        grid_spec=pltpu.PrefetchScalarGridSpec(
            num_scalar_prefetch=1, grid=(C // tc,),
            in_specs=[pl.BlockSpec((None, R, tc), lambda j, qs: (qs[0], 0, j)),
                      pl.BlockSpec((3, R, tc), lambda j, qs: (0, 0, j))],
            out_specs=pl.BlockSpec((R, tc), lambda j, qs: (0, j))),
        out_shape=jax.ShapeDtypeStruct((R, C), F32), name=name, compiler_params=_cparams(("parallel",)),
    )(qsel, p4, r3)


def _adamw_tiled(w, g_super, shift, m, v, tc, name):
    R, C = w.shape
    RS = g_super.shape[0]
    n = C // tc

    def body(s_ref, w_ref, g_ref, m_ref, v_ref, g_out, d_out, m_out, v_out, buf, sems):
        j = pl.program_id(0)
        slot = j % 2
        outs = (g_out, d_out, m_out, v_out)

        def copies(step, at):
            cols = pl.ds(pl.multiple_of(step * tc, tc), tc)
            return [pltpu.make_async_copy(buf.at[at, k], out.at[:, 0, cols], sems.at[at, k])
                    for k, out in enumerate(outs)]

        gv = pltpu.roll(g_ref[...], RS - s_ref[0], axis=0)[:R]
        d, mn, vn = _adamw_math(w_ref[...], gv, m_ref[...], v_ref[...])

        @pl.when(j >= 2)
        def _():
            for c in copies(j - 2, slot):
                c.wait()

        for k, val in enumerate((gv, d, mn, vn)):
            buf[slot, k] = val
        for c in copies(j, slot):
            c.start()

        @pl.when(j == n - 1)
        def _():
            for c in (copies(j - 1, 1 - slot) if n > 1 else []) + copies(j, slot):
                c.wait()

    blk = pl.BlockSpec((R, tc), lambda j, s: (0, j))
    return pl.pallas_call(
        body,
        grid_spec=pltpu.PrefetchScalarGridSpec(
            num_scalar_prefetch=1, grid=(n,),
            in_specs=[blk, pl.BlockSpec((RS, tc), lambda j, s: (0, j)), blk, blk], out_specs=[ANY] * 4,
            scratch_shapes=[pltpu.VMEM((2, 4, R, tc), F32), pltpu.SemaphoreType.DMA((2, 4))]),
        out_shape=[jax.ShapeDtypeStruct((R, 1, C), F32)] * 4,
        name=name, compiler_params=_cparams(("arbitrary",)),
    )(shift, w, g_super, m, v)


def _small_sum(parts, name):
    def body(p_ref, o_ref):
        acc = p_ref[0]
        for d in range(1, N_DEV):
            acc = acc + p_ref[d]
        o_ref[...] = acc

    return pl.pallas_call(
        body, out_shape=jax.ShapeDtypeStruct(parts.shape[1:], F32), name=name,
        compiler_params=_cparams(),
    )(parts)


def _adamw_small(w, g, m, v, name):
    def body(w_ref, g_ref, m_ref, v_ref, d_out, m_out, v_out):
        d, mn, vn = _adamw_math(w_ref[...], g_ref[...], m_ref[...], v_ref[...])
        d_out[...] = d
        m_out[...] = mn
        v_out[...] = vn

    return pl.pallas_call(
        body, out_shape=[jax.ShapeDtypeStruct(w.shape, F32)] * 3, name=name, compiler_params=_cparams(),
    )(w, g, m, v)


def _row(*pieces):
    r = jnp.concatenate([p.reshape(1, -1) for p in pieces], axis=1)
    return jnp.pad(r, ((0, 0), (0, D_MODEL - r.shape[1])))


def _pack_small(mix, convb, ssmg, attng, mlpg, fing, convw, dtb, alog, dsk, sinks, extra=None):
    last = [dtb, alog, dsk, sinks] + ([extra] if extra is not None else [])
    rows = [_row(mix), _row(convb), _row(ssmg, attng), _row(mlpg), _row(fing),
            jnp.pad(convw, ((0, 0), (0, D_MODEL - convw.shape[1]))), _row(*last)]
    packed = jnp.concatenate(rows, axis=0)
    return jnp.pad(packed, ((0, SMALL_ROWS - packed.shape[0]), (0, 0)))


def _unpack_small(p, conv_n):
    return dict(
        mix_norm_g=p[0:1, :], conv_b=p[1:2, :], ssm_norm_g=p[2:3, :D_INNER], attn_out_norm_g=p[2:3, D_INNER:],
        mlp_norm_g=p[3:4, :], final_norm_g=p[4, :], conv_w=p[5:9, :conv_n][None],
        dt_bias=p[9:10, 0:16], A_log=p[9:10, 16:32], D_skip=p[9:10, 32:48], attn_sinks=p[9:10, 48:64])


WEIGHT_ORDER = ["mix_norm_g", "w_in", "conv_w", "conv_b", "dt_bias", "A_log", "D_skip", "ssm_norm_g", "attn_sinks",
                "attn_out_norm_g", "w_out", "mlp_norm_g", "w_up", "w_down", "final_norm_g"]


def _to_my_columns(w_nat):
    pad = jnp.zeros((w_nat.shape[0], NP - IN_PROJ), w_nat.dtype)
    return jnp.concatenate([w_nat[:, :NAT_DT], w_nat[:, NAT_DT + N_HEADS:], w_nat[:, NAT_DT:NAT_DT + N_HEADS], pad],
                           axis=1)


PER = IN_PROJ // N_DEV
SUPER_STEP = 544
SUPER = 576


def _natural_rows(g, lo, hi):
    segments = [(0, NAT_DT, 0), (NAT_DT, NAT_DT + N_HEADS, OFF_DT - NAT_DT), (NAT_DT + N_HEADS, IN_PROJ, -N_HEADS),
                (IN_PROJ, NP, 0)]
    pieces = [g[max(lo, a) + shift:min(hi, b) + shift] for a, b, shift in segments if max(lo, a) < min(hi, b)]
    return pieces[0] if len(pieces) == 1 else jnp.concatenate(pieces, axis=0)


def _w_in_from_super_slabs(sup):
    seam = SUPER - SUPER_STEP
    units = []
    for i in range(N_DEV):
        base = SUPER_STEP * i
        units.append((base, base + seam, sup[i, :seam] if i == 0 else sup[i - 1, SUPER_STEP:] + sup[i, :seam]))
        units.append((base + seam, base + SUPER_STEP, sup[i, seam:SUPER_STEP]))
    units.append((SUPER_STEP * N_DEV, SUPER_STEP * N_DEV + seam, sup[N_DEV - 1, SUPER_STEP:]))

    def natural(lo, hi):
        return [rows[max(lo, a) - a:min(hi, b) - a] for a, b, rows in units if max(lo, a) < min(hi, b)]

    pieces = natural(0, NAT_DT) + natural(NAT_DT + N_HEADS, IN_PROJ) + natural(NAT_DT, NAT_DT + N_HEADS)
    return jnp.concatenate(pieces + [jnp.zeros((NP - IN_PROJ, D_MODEL), sup.dtype)], axis=0)


def _to_natural_columns(w_my):
    return jnp.concatenate([w_my[:, :NAT_DT], w_my[:, OFF_DT:OFF_DT + N_HEADS], w_my[:, NAT_DT:OFF_DT]], axis=1)


SLAB = 1024


def _grad_w_up(h2, du, name, sel=None, add=None, after=None):
    T, D = h2.shape
    if sel is None:
        pick, n_slab, pre = (lambda j, *cs: j), N_DEV, None
    else:
        pre, other = sel
        pick, n_slab = (lambda j, cs: 2 * j + ((1 - cs[0]) if other else cs[0])), 4
    o_spec = pl.BlockSpec((None, D, SLAB), lambda i, j, k, *cs: (j, 0, 0))
    return _matmul(
        h2, du, mode="tn", grid=(1, n_slab, 1),
        a_spec=pl.BlockSpec((T, D), lambda i, j, k, *cs: (0, 0)),
        b_spec=pl.BlockSpec((T, SLAB), lambda i, j, k, *cs: (0, pick(j, *cs))),
        out_shapes=[jax.ShapeDtypeStruct((n_slab, D, SLAB), BF16)], out_specs=[o_spec], tile=(D, SLAB), name=name,
        extras=() if add is None else (add,), extra_specs=() if add is None else (o_spec,),
        epilogue=None if add is None else (lambda acc, r: (acc + r.astype(F32),)), after=after, prefetch=pre)[0]


def _grad_w_down(act, dx3b, name, sel=None, add=None, after=None):
    T, D = dx3b.shape
    if sel is None:
        pick, n_slab, pre = (lambda i, *cs: i), N_DEV, None
    else:
        pre, other = sel
        pick, n_slab = (lambda i, cs: 2 * i + ((1 - cs[0]) if other else cs[0])), 4
    o_spec = pl.BlockSpec((None, SLAB, D), lambda i, j, k, *cs: (i, 0, 0))
    return _matmul(
        act, dx3b, mode="tn", grid=(n_slab, 1, 1),
        a_spec=pl.BlockSpec((T, SLAB), lambda i, j, k, *cs: (0, pick(i, *cs))),
        b_spec=pl.BlockSpec((T, D), lambda i, j, k, *cs: (0, 0)),
        out_shapes=[jax.ShapeDtypeStruct((n_slab, SLAB, D), BF16)], out_specs=[o_spec], tile=(SLAB, D), name=name,
        extras=() if add is None else (add,), extra_specs=() if add is None else (o_spec,),
        epilogue=None if add is None else (lambda acc, r: (acc + r.astype(F32),)), after=after, prefetch=pre)[0]


class _FixedWeights:
    def __init__(self, w_in_p, w_out_f, w_up_s, w_down_f, conv_w_f):
        self.w = (w_in_p, w_out_f, w_up_s, w_down_f, conv_w_f)
        self.grads = {}

    def mixer_weights(self, after):
        return self.w[0], None

    def conv_weight(self, after):
        return self.w[4]

    def out_weight(self, after):
        return self.w[1]

    def up_weight(self, after):
        return self.w[2]

    def down_weight(self, h, after):
        return self.w[3][:, h * (D_MODEL // 2):(h + 1) * (D_MODEL // 2)]

    def mlp_grads(self, h2, du, act, dx3b):
        self.grads.update(w_up=_grad_w_up(h2, du, "grad_w_up"),
                          w_down=_grad_w_down(act, dx3b, "grad_w_down").reshape(D_FF, D_MODEL))
        return None

    def grad_sent(self, tag, after):
        return None

    def out_grad(self, g_out):
        self.grads.update(w_out=g_out)
        return None

    def in_grad(self, g_in):
        self.grads.update(w_in=g_in)
        return None


def _local_step(x, tgt, p, hooks):
    T = x.shape[0]
    D = D_MODEL
    h1 = _rmsnorm_fwd(x, p["mix_norm_g"], "norm_mix")
    w_in_t, token = hooks.mixer_weights(h1)
    (proj,) = _mm_simple(h1, w_in_t, mode="nt", M=T, N=NP, K=D, tm=min(T, 1024), tn=1536, tk=D, out_dtype=F32,
                         name="in_proj", after=token)
    conv_w_f = hooks.conv_weight(proj)
    xbc, dsilu = _conv_fwd(proj, conv_w_f, p["conv_b"], "conv_fwd")
    dtT = proj[:, OFF_DT:OFF_DT + N_HEADS].T
    dtbT = p["dt_bias"].T
    alogT = p["A_log"].T
    dfull = jnp.repeat(p["D_skip"], HEAD_DIM, axis=1)
    ycat, ypre, hs = _ssd_fwd(xbc, proj, dtT, p["dt_bias"], dtbT, p["A_log"], alogT, dfull, p["ssm_norm_g"],
                              "ssd_fwd")
    ycat, o_att, probs, sink_probs = _attn_fwd(proj, p["attn_sinks"], p["attn_out_norm_g"], ycat, "attn_fwd")
    w_out_f = hooks.out_weight(ycat)
    tm = min(T, 1024)
    def residual_and_norm(acc, res, gain):
        x2 = acc + res
        return x2, x2 * lax.rsqrt(jnp.mean(x2 * x2, axis=-1, keepdims=True) + EPS) * gain

    rows = min(T, 512)
    x2, h2 = _matmul(
        ycat, w_out_f, mode="nn", grid=(T // rows, 1, 1),
        a_spec=pl.BlockSpec((rows, D), lambda i, j, k: (i, 0)), b_spec=pl.BlockSpec((D, D), lambda i, j, k: (0, 0)),
        out_shapes=[jax.ShapeDtypeStruct((T, D), F32), jax.ShapeDtypeStruct((T, D), BF16)],
        out_specs=[pl.BlockSpec((rows, D), lambda i, j, k: (i, 0))] * 2, tile=(rows, D), name="out_proj",
        extras=(x, p["mlp_norm_g"]),
        extra_specs=[pl.BlockSpec((rows, D), lambda i, j, k: (i, 0)), pl.BlockSpec((1, D), lambda i, j, k: (0, 0))],
        epilogue=residual_and_norm)
    w_up_s = hooks.up_weight(h2)
    grid = (T // tm, N_DEV, 1)
    u, act = _matmul(
        h2, w_up_s, mode="nn", grid=grid,
        a_spec=pl.BlockSpec((tm, D), lambda i, j, k: (i, 0)),
        b_spec=pl.BlockSpec((None, D, 1024), lambda i, j, k: (j, 0, 0)),
        out_shapes=[jax.ShapeDtypeStruct((T, D_FF), F32), jax.ShapeDtypeStruct((T, D_FF), BF16)],
        out_specs=[pl.BlockSpec((tm, 1024), lambda i, j, k: (i, j))] * 2, tile=(tm, 1024), name="mlp_up",
        epilogue=lambda acc: (acc, jnp.square(jnp.maximum(acc, 0.0))))
    half = D // 2
    w_down_halves, x3_halves = [], []
    for h in range(2):
        w_down_halves.append(hooks.down_weight(h, act if h == 0 else x3_halves[0]))
        x3_halves.append(_matmul(
            act, w_down_halves[h], mode="nn", grid=(T // tm, 1, D_FF // 2048),
            a_spec=pl.BlockSpec((tm, 2048), lambda i, j, k: (i, k)),
            b_spec=pl.BlockSpec((2048, half), lambda i, j, k: (k, 0)),
            out_shapes=[jax.ShapeDtypeStruct((T, half), F32)],
            out_specs=[pl.BlockSpec((tm, half), lambda i, j, k: (i, 0))], tile=(tm, half), name=f"mlp_down_{h}",
            extras=(x2,), extra_specs=[pl.BlockSpec((tm, half), lambda i, j, k, h=h: (i, h))],
            epilogue=lambda acc, res: (acc + res,))[0])
    loss_part, d_fin, dx3, dx3b = _final_loss(x3_halves, tgt, p["final_norm_g"].reshape(1, D), "loss_head")
    (du,) = _matmul(
        dx3b, tuple(w_down_halves), mode="nt", grid=(T // tm, D_FF // 1024, 1),
        a_spec=pl.BlockSpec((tm, D), lambda i, j, k: (i, 0)),
        b_spec=(pl.BlockSpec((1024, half), lambda i, j, k: (j, 0)),) * 2,
        out_shapes=[jax.ShapeDtypeStruct((T, D_FF), BF16)],
        out_specs=[pl.BlockSpec((tm, 1024), lambda i, j, k: (i, j))], tile=(tm, 1024), name="mlp_down_bwd",
        extras=(u,), extra_specs=[pl.BlockSpec((tm, 1024), lambda i, j, k: (i, j))],
        epilogue=lambda acc, uu: (acc * (2.0 * jnp.maximum(uu, 0.0)),),
        dot_fn=lambda a, b0, b1: _dot_nt(a[:, :half], b0) + _dot_nt(a[:, half:], b1))
    token = hooks.mlp_grads(h2, du, act, dx3b)
    (dh2,) = _matmul(
        du, w_up_s, mode="nt", grid=(T // tm, D // 1024, N_DEV // 2),
        a_spec=pl.BlockSpec((tm, 2048), lambda i, j, k: (i, k)),
        b_spec=pl.BlockSpec((2, 1024, 1024), lambda i, j, k: (k, j, 0)),
        out_shapes=[jax.ShapeDtypeStruct((T, D), F32)],
        out_specs=[pl.BlockSpec((tm, 1024), lambda i, j, k: (i, j))], tile=(tm, 1024), name="mlp_up_bwd",
        after=token, dot_fn=lambda a, b: _dot_nt(a[:, :1024], b[0]) + _dot_nt(a[:, 1024:], b[1]))
    dx2, dx2b, d_mlp = _rmsnorm_bwd(dh2, x2, p["mlp_norm_g"], dx3, "norm_mlp_bwd")
    (g_out,) = _mm_simple(ycat, dx2b, mode="tn", M=D, N=D, K=T, tm=1024, tn=1024, tk=T, out_dtype=BF16,
                          name="grad_w_out")
    token = hooks.out_grad(g_out)
    (dy,) = _mm_simple(dx2b, w_out_f, mode="nt", M=T, N=D, K=D, tm=tm, tn=1024, tk=D, out_dtype=F32,
                       name="out_proj_bwd", after=token)
    token = hooks.grad_sent("out", dy)
    ssm_g = p["ssm_norm_g"] if token is None else p["ssm_norm_g"] + token[0:1, 0:1]
    dproj, dxbc_act, d_dtb, d_alog, d_dskip, d_ssmg = _ssd_bwd(
        xbc, proj, dtT, p["dt_bias"], dtbT, p["A_log"], alogT, dfull, ssm_g, ypre, hs, dy, "ssd_bwd")
    dproj, d_convw, d_convb = _conv_bwd(proj, dxbc_act, dsilu, conv_w_f, dproj, "conv_bwd")
    dproj, dk, dv, d_sinks, d_attng = _attn_bwd(proj, p["attn_out_norm_g"], o_att, dy, probs, sink_probs, dproj,
                                                "attn_bwd")
    dproj = lax.dynamic_update_slice(dproj, jnp.concatenate([dk, dv], axis=1).astype(BF16), (0, OFF_K))
    (g_in,) = _mm_simple(dproj, h1, mode="tn", M=NP, N=D, K=T, tm=1536, tn=1024, tk=T, out_dtype=BF16,
                         name="grad_w_in")
    token = hooks.in_grad(g_in)
    (dh1,) = _mm_simple(dproj, w_in_t, mode="nn", M=T, N=D, K=NP, tm=tm, tn=1024, tk=2304, out_dtype=F32,
                        name="in_proj_bwd", after=token)
    token = hooks.grad_sent("in", dh1)
    mix_g = p["mix_norm_g"] if token is None else p["mix_norm_g"] + token[0:1, 0:1]
    dx, d_mix = _rmsnorm_bwd(dh1, x, mix_g, dx2, "norm_mix_bwd", with_bf16=False)
    small = _pack_small(d_mix, d_convb, d_ssmg, d_attng, d_mlp, d_fin, d_convw, d_dtb, d_alog, d_dskip, d_sinks,
                        extra=loss_part[:, 0:1])
    return dx, small


def _rows_rotated(v, shift, name):
    R, C = v.shape
    tc = 512

    def body(s_ref, v_ref, o_ref):
        o_ref[...] = pltpu.roll(v_ref[...], s_ref[0], axis=0).astype(BF16)

    return pl.pallas_call(
        body,
        grid_spec=pltpu.PrefetchScalarGridSpec(
            num_scalar_prefetch=1, grid=(C // tc,), in_specs=[pl.BlockSpec((R, tc), lambda j, s: (0, j))],
            out_specs=pl.BlockSpec((R, tc), lambda j, s: (0, j))),
        out_shape=jax.ShapeDtypeStruct((R, C), BF16), name=name, compiler_params=_cparams(("parallel",)),
    )(shift, v)


def _landing(own, me):
    zone = lax.empty((N_DEV,) + own.shape, own.dtype)
    return lax.dynamic_update_slice(zone, own[None], (me,) + (0,) * own.ndim)


def _sequencer_gather(owns, split, me, collective_id, name):
    n = len(owns)
    zone_refs = [jax.new_ref(_landing(o, me), memory_space=pltpu.MemorySpace.HBM) for o in owns]
    own_refs = [jax.new_ref(o, memory_space=pltpu.MemorySpace.HBM) for o in owns]
    N_COPIES = 9

    @pl.kernel(mesh=plsc.ScalarSubcoreMesh(axis_name="sequencer", num_cores=1), name=name,
               scratch_types=(pltpu.SemaphoreType.DMA((n, N_COPIES)), pltpu.SemaphoreType.DMA((n, N_COPIES))),
               compiler_params=pltpu.CompilerParams(collective_id=collective_id))
    def launch(send_sems, recv_sems):
        x, y, c = _coords()
        sibling, xn, yn, diag = (x, y, 1 - c), (1 - x, y, c), (x, 1 - y, c), (1 - x, 1 - y, c)
        barrier = pltpu.get_barrier_semaphore()
        for peer in [sibling, xn, yn, diag]:
            pl.semaphore_signal(barrier, inc=1, device_id=peer, device_id_type=MESH)
        pl.semaphore_wait(barrier, 4)

        def block(a, dev, half=None):
            ref = zone_refs[a].at[4 * dev[0] + 2 * dev[1] + dev[2]]
            if half is None:
                return ref
            rows = owns[a].shape[0] // 2
            return ref.at[pl.ds(half * rows, rows)]

        def copy(a, k, src, dst, to):
            return pltpu.make_async_remote_copy(src_ref=src, dst_ref=dst, send_sem=send_sems.at[a, k],
                                                recv_sem=recv_sems.at[a, k], device_id=to, device_id_type=MESH)

        me_dev = (x, y, c)
        sent = []
        first = {}
        for a in range(n):
            for k, peer in enumerate([sibling, xn, yn] + ([] if split[a] else [diag])):
                first[a, k] = copy(a, k, own_refs[a], block(a, me_dev), peer)
                first[a, k].start()
                sent.append(first[a, k])
        from_sibling = []
        for a in range(n):
            first[a, 1].wait_recv()
            sent.append(copy(a, 4, block(a, xn), block(a, xn), sibling))
            if split[a]:
                sent.append(copy(a, 6, block(a, xn, 0), block(a, xn, 0), yn))
            first[a, 2].wait_recv()
            sent.append(copy(a, 5, block(a, yn), block(a, yn), sibling))
            if split[a]:
                sent.append(copy(a, 7, block(a, yn, 1), block(a, yn, 1), xn))
            for cp in sent[-(4 if split[a] else 2):]:
                cp.start()
        for a in range(n):
            if split[a]:
                copy(a, 6, block(a, diag, 0), block(a, diag, 0), yn).wait_recv()
                sent.append(copy(a, 8, block(a, diag, 0), block(a, diag, 0), sibling))
                sent[-1].start()
                copy(a, 7, block(a, diag, 1), block(a, diag, 1), xn).wait_recv()
                sent.append(copy(a, 3, block(a, diag, 1), block(a, diag, 1), sibling))
                sent[-1].start()
            else:
                first[a, 3].wait_recv()
                sent.append(copy(a, 8, block(a, diag), block(a, diag), sibling))
                sent[-1].start()
        for a in range(n):
            first[a, 0].wait_recv()
            copy(a, 4, block(a, xn), block(a, xn), sibling).wait_recv()
            copy(a, 5, block(a, yn), block(a, yn), sibling).wait_recv()
            if split[a]:
                copy(a, 8, block(a, diag, 0), block(a, diag, 0), sibling).wait_recv()
                copy(a, 3, block(a, diag, 1), block(a, diag, 1), sibling).wait_recv()
            else:
                copy(a, 8, block(a, diag), block(a, diag), sibling).wait_recv()
        for cp in sent:
            cp.wait_send()

    launch()
    return zone_refs


class _ShardedWeights:
    def __init__(self, w_in, w_out, conv_w, w_up, w_down, me, csel):
        self.me, self.csel = me, csel
        padded = jnp.pad(jnp.transpose(w_in), ((0, SUPER - PER), (0, 0)))
        own_rows = _rows_rotated(padded, jnp.reshape(2 * me, (1,)).astype(jnp.int32), "w_in_super_slab")
        (self.in_ref,) = _sequencer_gather([own_rows], [True], me, 7, "gather_w_in_sequencer")
        self.out_ref, self.conv_ref = _sequencer_gather([w_out.astype(BF16), conv_w], [True, False], me, 8,
                                                        "gather_w_out_sequencer")
        (self.up_ref,) = _sequencer_gather([w_up.astype(BF16)], [True], me, 9, "gather_w_up_sequencer")
        down = w_down.astype(BF16)
        self.down_refs = [_sequencer_gather([down[:, h * (D_MODEL // 2):(h + 1) * (D_MODEL // 2)]], [True], me, 10 + h,
                                            f"gather_w_down_{h}_sequencer")[0] for h in range(2)]
        self.reduces = {}
        self.pairs = {}

    def mixer_weights(self, after):
        return _w_in_from_super_slabs(self.in_ref[...]), None

    def conv_weight(self, after):
        g_conv = self.conv_ref[...]
        return jnp.concatenate([g_conv[i] for i in range(N_DEV)], axis=1)

    def out_weight(self, after):
        return self.out_ref[...].reshape(D_MODEL, D_MODEL)

    def up_weight(self, after):
        return self.up_ref[...]

    def down_weight(self, h, after):
        return self.down_refs[h][...].reshape(D_FF, D_MODEL // 2)

    def _chips_start(self, slabs, from_sibling, rows, tag):
        sums = [_pair_add(s, r, self.csel, tr, f"pair_add_{tag}_{i}")
                for i, (s, r, tr) in enumerate(zip(slabs, from_sibling, rows))]
        lands = [lax.empty((3,) + s.shape[1:], s.dtype) for s in sums]
        self.reduces[tag] = _remote_start(sums, lands, _chips_plan, 3 * len(sums), f"reduce_start_{tag}")
        return self.reduces[tag][4]

    def mlp_grads(self, h2, du, act, dx3b):
        def send(part, tag, after):
            st = _remote_start([part], [lax.empty(part.shape, part.dtype)], _pair4_plan, 4,
                               f"reduce_pair_start_{tag}", after=after)
            return st

        def received(st, after, tag):
            return _remote_wait(st, after, _pair4_plan, f"reduce_pair_wait_{tag}")[1][0]

        def to_chips(sums, tag):
            self.reduces[tag] = _remote_start([sums], [lax.empty((3,) + sums.shape[1:], sums.dtype)], _chips_plan, 3,
                                              f"reduce_start_{tag}")
            return self.reduces[tag][4]

        up_send = _grad_w_up(h2, du, "grad_w_up_send", sel=(self.csel, True))
        st_up = send(up_send, "up", None)
        down_send = _grad_w_down(act, dx3b, "grad_w_down_send", sel=(self.csel, True), after=st_up[4])
        st_down = send(down_send, "down", None)
        up_sum = _grad_w_up(h2, du, "grad_w_up_keep", sel=(self.csel, False), add=received(st_up, down_send, "up"),
                            after=st_down[4])
        token = to_chips(up_sum, "up")
        down_sum = _grad_w_down(act, dx3b, "grad_w_down_keep", sel=(self.csel, False),
                                add=received(st_down, up_sum, "down"), after=token)
        return to_chips(down_sum, "down")

    def _pair_start(self, slabs, tag):
        land = lax.empty((4,) + slabs.shape[1:], slabs.dtype)
        self.pairs[tag] = _remote_start([slabs], [land], _pair_plan, 4, f"reduce_pair_start_{tag}")
        return self.pairs[tag][4]

    def grad_sent(self, tag, after):
        slabs, from_sibling = _remote_wait(self.pairs[tag], after, _pair_plan, f"reduce_pair_wait_{tag}")
        return self._chips_start(slabs, from_sibling, [slabs[0].shape[1]], tag)

    def out_grad(self, g_out):
        return self._pair_start(g_out.reshape(N_DEV, D_MODEL // N_DEV, D_MODEL), "out")

    def in_grad(self, g_in):
        return self._pair_start(
            jnp.stack([_natural_rows(g_in, SUPER_STEP * j, SUPER_STEP * j + SUPER) for j in range(N_DEV)]), "in")

    def small_start(self, small):
        self.st_small = _remote_start([small], [_landing(small, self.me)], _everyone_plan, N_DEV - 1, "gather_start_small")

    def small_end(self, after):
        return _remote_wait(self.st_small, after, _everyone_plan, "gather_small_wait")[1][0]

    def reduce_end(self, tag, after):
        return _remote_wait(self.reduces[tag], after, _chips_plan, f"reduce_wait_{tag}")


def kernel(x, mix_norm_g, w_in, conv_w, conv_b, dt_bias, A_log, D_skip, ssm_norm_g, attn_sinks, attn_out_norm_g, w_out, mlp_norm_g, w_up, w_down, final_norm_g, loss_target, m_mix_norm_g, m_w_in, m_conv_w, m_conv_b, m_dt_bias, m_A_log, m_D_skip, m_ssm_norm_g, m_attn_sinks, m_attn_out_norm_g, m_w_out, m_mlp_norm_g, m_w_up, m_w_down, m_final_norm_g, v_mix_norm_g, v_w_in, v_conv_w, v_conv_b, v_dt_bias, v_A_log, v_D_skip, v_ssm_norm_g, v_attn_sinks, v_attn_out_norm_g, v_w_out, v_mlp_norm_g, v_w_up, v_w_down, v_final_norm_g):
    xi, yi, ci = _coords()
    me = 4 * xi + 2 * yi + ci
    csel = jnp.reshape(ci, (1,)).astype(jnp.int32)
    qsel = jnp.reshape(2 * xi + yi, (1,)).astype(jnp.int32)
    w = dict(mix_norm_g=mix_norm_g, conv_b=conv_b, dt_bias=dt_bias, A_log=A_log, D_skip=D_skip,
             ssm_norm_g=ssm_norm_g, attn_sinks=attn_sinks, attn_out_norm_g=attn_out_norm_g, mlp_norm_g=mlp_norm_g,
             final_norm_g=final_norm_g)
    hooks = _ShardedWeights(w_in[0], w_out[0], conv_w[0], w_up[0], w_down[0], me, csel)
    p = dict(w)
    dx, small = _local_step(x[0], loss_target[0], p, hooks)
    hooks.small_start(small)
    big = {}
    after = dx
    for name, wt, mt, vt, tile in [
            ("up", w_up, m_w_up, v_w_up, (512, SLAB)), ("down", w_down, m_w_down, v_w_down, (256, D_MODEL)),
            ("out", w_out, m_w_out, v_w_out, (256, D_MODEL))]:
        (chip_sums,), (from_chips,) = hooks.reduce_end(name, after)
        res = _adamw_big(wt[0], mt[0], vt[0], chip_sums, from_chips, qsel, tile, f"adamw_w_{name}")
        big["w_" + name] = tuple(r[None] for r in res)
        after = res[0]
    (chip_sums,), (from_chips,) = hooks.reduce_end("in", after)
    g_super = _sum_partials(chip_sums, from_chips, qsel, 512, "grad_w_in_sum")
    res = _adamw_tiled(jnp.transpose(w_in[0]), g_super, jnp.reshape(2 * me, (1,)).astype(jnp.int32),
                       jnp.transpose(m_w_in[0]), jnp.transpose(v_w_in[0]), 512, "adamw_w_in")
    big["w_in"] = tuple(jnp.transpose(r, (1, 2, 0)) for r in res)
    after = res[0]
    gsum = _small_sum(hooks.small_end(after), "small_sum")
    loss = gsum[9, 64]
    gs = _unpack_small(gsum, CONV_DIM)
    cw = CONV_DIM // N_DEV
    g_conv_shard = lax.dynamic_slice(gsum[5:9, :], (0, me * cw), (CONV_K, cw))

    def pack(s):
        return _pack_small(s["mix_norm_g"], s["conv_b"], s["ssm_norm_g"], s["attn_out_norm_g"], s["mlp_norm_g"],
                           s["final_norm_g"], s["conv_w"][0], s["dt_bias"], s["A_log"], s["D_skip"], s["attn_sinks"])

    wp = pack(dict(w, conv_w=conv_w))
    mp = pack(dict(mix_norm_g=m_mix_norm_g, conv_b=m_conv_b, ssm_norm_g=m_ssm_norm_g,
                   attn_out_norm_g=m_attn_out_norm_g, mlp_norm_g=m_mlp_norm_g, final_norm_g=m_final_norm_g,
                   conv_w=m_conv_w, dt_bias=m_dt_bias, A_log=m_A_log, D_skip=m_D_skip, attn_sinks=m_attn_sinks))
    vp = pack(dict(mix_norm_g=v_mix_norm_g, conv_b=v_conv_b, ssm_norm_g=v_ssm_norm_g,
                   attn_out_norm_g=v_attn_out_norm_g, mlp_norm_g=v_mlp_norm_g, final_norm_g=v_final_norm_g,
                   conv_w=v_conv_w, dt_bias=v_dt_bias, A_log=v_A_log, D_skip=v_D_skip, attn_sinks=v_attn_sinks))
    gp = jnp.concatenate([gsum[0:5], jnp.pad(g_conv_shard, ((0, 0), (0, D_MODEL - cw))), gsum[9:10],
                          jnp.zeros((SMALL_ROWS - 10, D_MODEL), F32)], axis=0)
    dp, mnp, vnp = _adamw_small(wp, gp, mp, vp, "adamw_small")
    grads = dict(gs, conv_w=g_conv_shard[None])
    deltas = _unpack_small(dp, cw)
    new_m = _unpack_small(mnp, cw)
    new_v = _unpack_small(vnp, cw)
    for k, name in enumerate(["w_in", "w_out", "w_up", "w_down"]):
        grads[name], deltas[name], new_m[name], new_v[name] = big[name]
    return (loss, dx[None], *[grads[n] for n in WEIGHT_ORDER], *[deltas[n] for n in WEIGHT_ORDER],
            *[new_m[n] for n in WEIGHT_ORDER], *[new_v[n] for n in WEIGHT_ORDER])
```

```python
import jax
import jax.numpy as jnp
from jax import lax
from jax.experimental import pallas as pl
from jax.experimental.pallas import tpu as pltpu
from jax.experimental.pallas import tpu_sc as plsc

F32 = jnp.float32
BF16 = jnp.bfloat16
MESH = pl.DeviceIdType.MESH

EPS = 1e-5
D_MODEL = 2048
D_INNER = 1024
N_HEADS = 16
HEAD_DIM = 64
N_GROUPS = 4
D_STATE = 128
CHUNK = 128
CONV_K = 4
CONV_DIM = 2048
ATTN_W = 1024
KV_W = 128
WINDOW = 128
D_FF = 8192
IN_PROJ = 4368
N_DEV = 8
NP = 4608
OFF_Z, OFF_X, OFF_B, OFF_C, OFF_Q, OFF_K, OFF_V, OFF_DT = 0, 1024, 2048, 2560, 3072, 4096, 4224, 4352
NAT_DT = 3072

ADAM_LR = 0.001
ADAM_B1 = 0.9
ADAM_B2 = 0.999
ADAM_EPS = 1e-08
ADAM_WD = 0.01
ADAM_STEP = 10

VMEM_LIMIT = 52 * 1024 * 1024
SMALL_ROWS = 16
NEG = -1e30


def _cparams(sem=None):
    return pltpu.CompilerParams(dimension_semantics=sem, vmem_limit_bytes=VMEM_LIMIT)


def _split3(v):
    hi = v.astype(BF16)
    rest = v - hi.astype(F32)
    mid = rest.astype(BF16)
    return hi, mid, (rest - mid.astype(F32)).astype(BF16)


def _hdot(a, b, data):
    if data == "a":
        sel = b.astype(BF16)
        return sum(_dot_nn(part, sel) for part in _split3(a))
    sel = a.astype(BF16)
    return sum(_dot_nn(sel, part) for part in _split3(b))


def _dot_nn(a, b):
    return lax.dot_general(a, b, (((1,), (0,)), ((), ())), preferred_element_type=F32)


def _dot_nt(a, b):
    return lax.dot_general(a, b, (((1,), (1,)), ((), ())), preferred_element_type=F32)


def _dot_tn(a, b):
    return lax.dot_general(a, b, (((0,), (0,)), ((), ())), preferred_element_type=F32)


def _softplus(v):
    return jnp.maximum(v, 0.0) + jnp.log1p(jnp.exp(-jnp.abs(v)))


def _sigmoid(v):
    return 1.0 / (1.0 + jnp.exp(-v))


def _matmul(a, b, *, mode, grid, a_spec, b_spec, out_shapes, out_specs, tile, name,
            extras=(), extra_specs=(), epilogue=None, after=None, dot_fn=None, prefetch=None):
    nk = grid[2]
    n_ex = len(extras)
    n_out = len(out_shapes)
    bs, b_specs = (b, b_spec) if isinstance(b, tuple) else ((b,), (b_spec,))
    n_in = 1 + len(bs)
    dot = dot_fn if dot_fn is not None else {"nn": _dot_nn, "nt": _dot_nt, "tn": _dot_tn}[mode]

    def finish(acc, ex_refs, out_refs):
        res = (acc,) if epilogue is None else epilogue(acc, *[e[...] for e in ex_refs])
        for o, r in zip(out_refs, res):
            o[...] = r.astype(o.dtype)

    def body(*refs):
        ex_refs = refs[n_in:n_in + n_ex]
        out_refs = refs[n_in + n_ex:n_in + n_ex + n_out]
        part = dot(*[r[...].astype(BF16) for r in refs[:n_in]])
        if nk == 1:
            finish(part, ex_refs, out_refs)
        else:
            acc_ref = refs[-1]
            k = pl.program_id(2)

            @pl.when(k == 0)
            def _():
                acc_ref[...] = part

            @pl.when(k > 0)
            def _():
                acc_ref[...] += part

            @pl.when(k == nk - 1)
            def _():
                finish(acc_ref[...], ex_refs, out_refs)

    scratch = [] if nk == 1 else [pltpu.VMEM(tile, F32)]
    n_pre = 0 if prefetch is None else 1
    tok_specs = [] if after is None else [pl.BlockSpec((8, 128), lambda *_: (0, 0))]
    tok_args = [] if after is None else [after]

    def body_with_token(*refs):
        refs = refs[n_pre:]
        body(*refs[:n_in + n_ex], *refs[n_in + n_ex + len(tok_args):])

    in_specs = [a_spec, *b_specs, *extra_specs, *tok_specs]
    params = _cparams(("parallel", "parallel", "arbitrary"))
    if prefetch is None:
        return pl.pallas_call(
            body_with_token, grid=grid, in_specs=in_specs, out_specs=list(out_specs), out_shape=list(out_shapes),
            scratch_shapes=scratch, name=name, compiler_params=params)(a, *bs, *extras, *tok_args)
    return pl.pallas_call(
        body_with_token,
        grid_spec=pltpu.PrefetchScalarGridSpec(num_scalar_prefetch=1, grid=grid, in_specs=in_specs,
                                               out_specs=list(out_specs), scratch_shapes=scratch),
        out_shape=list(out_shapes), name=name, compiler_params=params)(prefetch, a, *bs, *extras, *tok_args)


def _mm_simple(a, b, *, mode, M, N, K, tm, tn, tk, out_dtype, name, extras=(), epilogue=None, n_out=1,
               out_dtypes=None, after=None):
    grid = (M // tm, N // tn, K // tk)
    if mode == "nn":
        a_spec = pl.BlockSpec((tm, tk), lambda i, j, k: (i, k))
        b_spec = pl.BlockSpec((tk, tn), lambda i, j, k: (k, j))
    elif mode == "nt":
        a_spec = pl.BlockSpec((tm, tk), lambda i, j, k: (i, k))
        b_spec = pl.BlockSpec((tn, tk), lambda i, j, k: (j, k))
    else:
        a_spec = pl.BlockSpec((tk, tm), lambda i, j, k: (k, i))
        b_spec = pl.BlockSpec((tk, tn), lambda i, j, k: (k, j))
    o_spec = pl.BlockSpec((tm, tn), lambda i, j, k: (i, j))
    dts = out_dtypes if out_dtypes is not None else [out_dtype] * n_out
    return _matmul(a, b, mode=mode, grid=grid, a_spec=a_spec, b_spec=b_spec,
                   out_shapes=[jax.ShapeDtypeStruct((M, N), d) for d in dts],
                   out_specs=[o_spec] * len(dts), tile=(tm, tn), name=name,
                   extras=extras, extra_specs=[o_spec] * len(extras), epilogue=epilogue, after=after)


ROW_BLOCK = 256


def _rmsnorm_fwd(x, g, name):
    T, D = x.shape

    def body(x_ref, g_ref, o_ref):
        xf = x_ref[...]
        r = lax.rsqrt(jnp.mean(xf * xf, axis=-1, keepdims=True) + EPS)
        o_ref[...] = (xf * r * g_ref[...]).astype(BF16)

    return pl.pallas_call(
        body, grid=(T // ROW_BLOCK,),
        in_specs=[pl.BlockSpec((ROW_BLOCK, D), lambda i: (i, 0)), pl.BlockSpec((1, D), lambda i: (0, 0))],
        out_specs=pl.BlockSpec((ROW_BLOCK, D), lambda i: (i, 0)),
        out_shape=jax.ShapeDtypeStruct((T, D), BF16), name=name, compiler_params=_cparams(("parallel",)),
    )(x, g)


def _rmsnorm_bwd(dh, x, g, dres, name, with_bf16=True):
    T, D = x.shape

    def body(dh_ref, x_ref, g_ref, dres_ref, dx_ref, *rest):
        dg_ref = rest[-1]
        i = pl.program_id(0)
        xf = x_ref[...]
        r = lax.rsqrt(jnp.mean(xf * xf, axis=-1, keepdims=True) + EPS)
        xh = xf * r
        d = dh_ref[...]

        @pl.when(i == 0)
        def _():
            dg_ref[...] = jnp.zeros_like(dg_ref)

        dg_ref[...] += jnp.sum(d * xh, axis=0, keepdims=True)
        dxh = d * g_ref[...]
        dx = r * (dxh - xh * jnp.mean(dxh * xh, axis=-1, keepdims=True)) + dres_ref[...]
        dx_ref[...] = dx
        if with_bf16:
            rest[0][...] = dx.astype(BF16)

    row = pl.BlockSpec((ROW_BLOCK, D), lambda i: (i, 0))
    vec = pl.BlockSpec((1, D), lambda i: (0, 0))
    copies = [(row, jax.ShapeDtypeStruct((T, D), BF16))] if with_bf16 else []
    return pl.pallas_call(
        body, grid=(T // ROW_BLOCK,), in_specs=[row, row, vec, row],
        out_specs=[row, *[c[0] for c in copies], vec],
        out_shape=[jax.ShapeDtypeStruct((T, D), F32), *[c[1] for c in copies], jax.ShapeDtypeStruct((1, D), F32)],
        name=name, compiler_params=_cparams(("arbitrary",)),
    )(dh, x, g, dres)


def _final_loss(x3_halves, tgt, g, name):
    T, D = tgt.shape

    def body(xa_ref, xb_ref, t_ref, g_ref, loss_ref, dg_ref, dx_ref, dxb_ref):
        i = pl.program_id(0)
        xf = jnp.concatenate([xa_ref[...], xb_ref[...]], axis=1)
        r = lax.rsqrt(jnp.mean(xf * xf, axis=-1, keepdims=True) + EPS)
        xh = xf * r
        gg = g_ref[...]
        err = xh * gg - t_ref[...]

        @pl.when(i == 0)
        def _():
            dg_ref[...] = jnp.zeros_like(dg_ref)
            loss_ref[...] = jnp.zeros_like(loss_ref)

        part = jnp.sum(jnp.sum(err * err, axis=-1, keepdims=True), axis=0, keepdims=True) * (0.5 / D)
        loss_ref[...] += jnp.broadcast_to(part, loss_ref.shape)
        dout = err * (1.0 / D)
        dg_ref[...] += jnp.sum(dout * xh, axis=0, keepdims=True)
        dxh = dout * gg
        dx = r * (dxh - xh * jnp.mean(dxh * xh, axis=-1, keepdims=True))
        dx_ref[...] = dx
        dxb_ref[...] = dx.astype(BF16)

    row = pl.BlockSpec((ROW_BLOCK, D), lambda i: (i, 0))
    vec = pl.BlockSpec((1, D), lambda i: (0, 0))
    return pl.pallas_call(
        body, grid=(T // ROW_BLOCK,),
        in_specs=[pl.BlockSpec((ROW_BLOCK, D // 2), lambda i: (i, 0))] * 2 + [row, vec],
        out_specs=[pl.BlockSpec((1, 128), lambda i: (0, 0)), vec, row, row],
        out_shape=[jax.ShapeDtypeStruct((1, 128), F32), jax.ShapeDtypeStruct((1, D), F32),
                   jax.ShapeDtypeStruct((T, D), F32), jax.ShapeDtypeStruct((T, D), BF16)],
        name=name, compiler_params=_cparams(("arbitrary",)),
    )(*x3_halves, tgt, g)


CONV_BLOCK = 256


def _conv_apply(u, w, b):
    row = lax.broadcasted_iota(jnp.int32, u.shape, 0)
    acc = b + w[CONV_K - 1:CONV_K, :] * u
    shifted = []
    for j in range(1, CONV_K):
        uj = jnp.where(row >= j, pltpu.roll(u, j, axis=0), 0.0)
        shifted.append(uj)
        acc = acc + w[CONV_K - 1 - j:CONV_K - j, :] * uj
    return acc, shifted


def _conv_fwd(proj, conv_w, conv_b, name):
    T = proj.shape[0]
    cb0 = OFF_X // CONV_BLOCK

    def body(u_ref, w_ref, b_ref, o_ref, ds_ref):
        c, _ = _conv_apply(u_ref[...], w_ref[...], b_ref[...])
        sg = _sigmoid(c)
        o_ref[...] = c * sg
        ds_ref[...] = sg * (1.0 + c * (1.0 - sg))

    out = pl.BlockSpec((T, CONV_BLOCK), lambda j: (0, j))
    return pl.pallas_call(
        body, grid=(CONV_DIM // CONV_BLOCK,),
        in_specs=[pl.BlockSpec((T, CONV_BLOCK), lambda j: (0, cb0 + j)),
                  pl.BlockSpec((CONV_K, CONV_BLOCK), lambda j: (0, j)),
                  pl.BlockSpec((1, CONV_BLOCK), lambda j: (0, j))],
        out_specs=[out, out], out_shape=[jax.ShapeDtypeStruct((T, CONV_DIM), F32)] * 2,
        name=name, compiler_params=_cparams(("parallel",)),
    )(proj, conv_w, conv_b)


def _conv_bwd(proj, dact, dsilu, conv_w, dproj, name):
    T = proj.shape[0]
    cb0 = OFF_X // CONV_BLOCK

    def body(u_ref, d_ref, s_ref, w_ref, _, du_ref, dw_ref, db_ref):
        u = u_ref[...]
        w = w_ref[...]
        dc = d_ref[...] * s_ref[...]
        row = lax.broadcasted_iota(jnp.int32, u.shape, 0)
        du = w[CONV_K - 1:CONV_K, :] * dc
        dw_ref[CONV_K - 1:CONV_K, :] = jnp.sum(dc * u, axis=0, keepdims=True)
        for j in range(1, CONV_K):
            dcj = jnp.where(row < T - j, pltpu.roll(dc, T - j, axis=0), 0.0)
            du = du + w[CONV_K - 1 - j:CONV_K - j, :] * dcj
            dw_ref[CONV_K - 1 - j:CONV_K - j, :] = jnp.sum(dcj * u, axis=0, keepdims=True)
        db_ref[...] = jnp.sum(dc, axis=0, keepdims=True)
        du_ref[...] = du.astype(BF16)

    blk = pl.BlockSpec((T, CONV_BLOCK), lambda j: (0, j))
    return pl.pallas_call(
        body, grid=(CONV_DIM // CONV_BLOCK,),
        in_specs=[pl.BlockSpec((T, CONV_BLOCK), lambda j: (0, cb0 + j)), blk, blk,
                  pl.BlockSpec((CONV_K, CONV_BLOCK), lambda j: (0, j)), pl.BlockSpec(memory_space=pl.ANY)],
        out_specs=[pl.BlockSpec((T, CONV_BLOCK), lambda j: (0, cb0 + j)),
                   pl.BlockSpec((CONV_K, CONV_BLOCK), lambda j: (0, j)),
                   pl.BlockSpec((1, CONV_BLOCK), lambda j: (0, j))],
        out_shape=[jax.ShapeDtypeStruct(dproj.shape, BF16), jax.ShapeDtypeStruct((CONV_K, CONV_DIM), F32),
                   jax.ShapeDtypeStruct((1, CONV_DIM), F32)],
        input_output_aliases={4: 0}, name=name, compiler_params=_cparams(("parallel",)),
    )(proj, dact, dsilu, conv_w, dproj)


GROUP_W = D_INNER // N_GROUPS
HEADS_PER_GROUP = N_HEADS // N_GROUPS


def _expand_mat():
    h = lax.broadcasted_iota(jnp.int32, (N_HEADS, D_INNER), 0)
    j = lax.broadcasted_iota(jnp.int32, (N_HEADS, D_INNER), 1)
    return (j // HEAD_DIM == h).astype(F32)


def _reduce_mat(g):
    j = lax.broadcasted_iota(jnp.int32, (GROUP_W, N_HEADS), 0)
    h = lax.broadcasted_iota(jnp.int32, (GROUP_W, N_HEADS), 1)
    return (g * HEADS_PER_GROUP + j // HEAD_DIM == h).astype(F32)


def _col16(v, h):
    lane = lax.broadcasted_iota(jnp.int32, v.shape, 1)
    return jnp.sum(jnp.where(lane == h, v, 0.0), axis=1, keepdims=True)


def _ssd_pre(dt_raw, dtT_raw, dtb, dtbT, alog, alogT):
    Q = CHUNK
    xdt = dt_raw + dtb
    dt = _softplus(xdt)
    dtT = _softplus(dtT_raw + dtbT)
    A = -jnp.exp(alog)
    AT = -jnp.exp(alogT)
    row = lax.broadcasted_iota(jnp.int32, (Q, Q), 0)
    col = lax.broadcasted_iota(jnp.int32, (Q, Q), 1)
    tril = (row >= col).astype(F32)
    triu = (row <= col).astype(F32)
    cs = _hdot(tril, dt * A, "b")
    csT = _hdot(dtT * AT, triu, "a")
    return xdt, dt, A, cs, csT, row >= col, triu


def _decay_matrix(cs, csT, h, causal):
    seg = _col16(cs, h) - csT[h:h + 1, :]
    return jnp.where(causal, jnp.exp(jnp.minimum(seg, 0.0)), 0.0)


def _ssd_in_specs(nc, rev):
    def cidx(c):
        return (nc - 1 - c) if rev else c

    return [
        pl.BlockSpec((CHUNK, D_INNER), lambda c: (cidx(c), 0)),
        pl.BlockSpec((CHUNK, 512), lambda c: (cidx(c), 2)),
        pl.BlockSpec((CHUNK, 512), lambda c: (cidx(c), 3)),
        pl.BlockSpec((CHUNK, D_INNER), lambda c: (cidx(c), 0)),
        pl.BlockSpec((CHUNK, 128), lambda c: (cidx(c), OFF_DT // 128)),
        pl.BlockSpec((N_HEADS, CHUNK), lambda c: (0, cidx(c))),
        pl.BlockSpec((1, N_HEADS), lambda c: (0, 0)),
        pl.BlockSpec((N_HEADS, 1), lambda c: (0, 0)),
        pl.BlockSpec((1, N_HEADS), lambda c: (0, 0)),
        pl.BlockSpec((N_HEADS, 1), lambda c: (0, 0)),
        pl.BlockSpec((1, D_INNER), lambda c: (0, 0)),
        pl.BlockSpec((1, D_INNER), lambda c: (0, 0)),
    ]


def _ssd_fwd(xbc, proj, dtT, dtb, dtbT, alog, alogT, dfull, ng, name):
    T = xbc.shape[0]
    nc = T // CHUNK
    Q = CHUNK

    def body(xs_ref, B_ref, C_ref, z_ref, dt_ref, dtT_ref, dtb_ref, dtbT_ref, al_ref, alT_ref, df_ref, ng_ref,
             y_ref, ypre_ref, hs_ref, h_scr):
        c = pl.program_id(0)

        @pl.when(c == 0)
        def _():
            h_scr[...] = jnp.zeros_like(h_scr)

        _, dt, _, cs, csT, causal, _ = _ssd_pre(dt_ref[:, :N_HEADS], dtT_ref[...], dtb_ref[...], dtbT_ref[...],
                                                al_ref[...], alT_ref[...])
        ex = _expand_mat()
        dt_full = _hdot(dt, ex, "a")
        cs_full = _hdot(cs, ex, "a")
        cs_last = cs_full[Q - 1:Q, :]
        xs = xs_ref[...]
        xd = xs * dt_full
        e_full = jnp.exp(cs_full)
        dec_full = jnp.exp(cs_last - cs_full)
        cd_full = jnp.exp(cs_last)
        lane_head = lax.broadcasted_iota(jnp.int32, (1, GROUP_W), 1) // HEAD_DIM
        for g in range(N_GROUPS):
            sl = slice(g * GROUP_W, (g + 1) * GROUP_W)
            Bg = B_ref[:, g * D_STATE:(g + 1) * D_STATE].astype(BF16)
            Cg = C_ref[:, g * D_STATE:(g + 1) * D_STATE].astype(BF16)
            CB = _dot_nt(Cg, Bg)
            hg = h_scr[g]
            yoff = _dot_nn(Cg, hg.astype(BF16)) * e_full[:, sl]
            xd_g = xd[:, sl]
            S = _dot_tn(Bg, (xd_g * dec_full[:, sl]).astype(BF16))
            xd_b = xd_g.astype(BF16)
            ydiag = jnp.zeros((Q, GROUP_W), F32)
            for r in range(HEADS_PER_GROUP):
                Lm = _decay_matrix(cs, csT, g * HEADS_PER_GROUP + r, causal)
                Gm = (CB * Lm).astype(BF16)
                ydiag = ydiag + _dot_nn(Gm, jnp.where(lane_head == r, xd_b, jnp.zeros_like(xd_b)))
            hs_ref[0, g] = hg
            h_scr[g] = hg * cd_full[:, sl] + S
            ypre = ydiag + yoff + xs[:, sl] * df_ref[:, sl]
            ypre_ref[:, sl] = ypre
            zg = z_ref[:, sl]
            yz = ypre * zg * _sigmoid(zg)
            rn = lax.rsqrt(jnp.mean(yz * yz, axis=-1, keepdims=True) + EPS)
            y_ref[:, sl] = (yz * rn * ng_ref[:, sl]).astype(BF16)

    return pl.pallas_call(
        body, grid=(nc,), in_specs=_ssd_in_specs(nc, False),
        out_specs=[pl.BlockSpec((CHUNK, D_INNER), lambda c: (c, 0)),
                   pl.BlockSpec((CHUNK, D_INNER), lambda c: (c, 0)),
                   pl.BlockSpec((1, N_GROUPS, D_STATE, GROUP_W), lambda c: (c, 0, 0, 0))],
        out_shape=[jax.ShapeDtypeStruct((T, D_INNER + ATTN_W), BF16), jax.ShapeDtypeStruct((T, D_INNER), F32),
                   jax.ShapeDtypeStruct((nc, N_GROUPS, D_STATE, GROUP_W), F32)],
        scratch_shapes=[pltpu.VMEM((N_GROUPS, D_STATE, GROUP_W), F32)],
        name=name, compiler_params=_cparams(("arbitrary",)),
    )(xbc, xbc, xbc, proj, proj, dtT, dtb, dtbT, alog, alogT, dfull, ng)


def _ssd_bwd(xbc, proj, dtT, dtb, dtbT, alog, alogT, dfull, ng, ypre, hs, dy, name):
    T = xbc.shape[0]
    nc = T // CHUNK
    Q = CHUNK

    def body(xs_ref, B_ref, C_ref, z_ref, dt_ref, dtT_ref, dtb_ref, dtbT_ref, al_ref, alT_ref, df_ref, ng_ref,
             ypre_ref, hs_ref, dy_ref,
             dz_ref, dxbc_ref, ddtb_ref, dal_ref, dD_ref, dng_ref, dh_scr):
        step = pl.program_id(0)

        @pl.when(step == 0)
        def _():
            dh_scr[...] = jnp.zeros_like(dh_scr)
            ddtb_ref[...] = jnp.zeros_like(ddtb_ref)
            dal_ref[...] = jnp.zeros_like(dal_ref)
            dD_ref[...] = jnp.zeros_like(dD_ref)
            dng_ref[...] = jnp.zeros_like(dng_ref)

        xdt, dt, A, cs, csT, causal, triu = _ssd_pre(dt_ref[:, :N_HEADS], dtT_ref[...], dtb_ref[...],
                                                    dtbT_ref[...], al_ref[...], alT_ref[...])
        ex = _expand_mat()
        dt_full = _hdot(dt, ex, "a")
        cs_full = _hdot(cs, ex, "a")
        cs_last = cs_full[Q - 1:Q, :]
        xs = xs_ref[...]
        xd = xs * dt_full
        e_full = jnp.exp(cs_full)
        dec_full = jnp.exp(cs_last - cs_full)
        cd_full = jnp.exp(cs_last)
        lane_head = lax.broadcasted_iota(jnp.int32, (1, GROUP_W), 1) // HEAD_DIM
        is_last = lax.broadcasted_iota(jnp.int32, (Q, 1), 0) == Q - 1
        dcs16 = jnp.zeros((Q, N_HEADS), F32)
        ddtx16 = jnp.zeros((Q, N_HEADS), F32)
        dD16 = jnp.zeros((8, N_HEADS), F32)
        lane16 = lax.broadcasted_iota(jnp.int32, (1, N_HEADS), 1)
        sub16 = lax.broadcasted_iota(jnp.int32, (N_HEADS, 1), 0)
        col_sums = jnp.zeros((N_HEADS, Q), F32)
        for g in range(N_GROUPS):
            sl = slice(g * GROUP_W, (g + 1) * GROUP_W)
            red = _reduce_mat(g)
            ypre_g = ypre_ref[:, sl]
            zg = z_ref[:, sl]
            sg = _sigmoid(zg)
            silu = zg * sg
            yz = ypre_g * silu
            rn = lax.rsqrt(jnp.mean(yz * yz, axis=-1, keepdims=True) + EPS)
            yh = yz * rn
            dy_g = dy_ref[:, sl]
            dng_ref[:, sl] += jnp.sum(dy_g * yh, axis=0, keepdims=True)
            dyh = dy_g * ng_ref[:, sl]
            dyz = rn * (dyh - yh * jnp.mean(dyh * yh, axis=-1, keepdims=True))
            dY = dyz * silu
            dz_ref[:, sl] = (dyz * ypre_g * sg * (1.0 + zg * (1.0 - sg))).astype(BF16)
            xs_g = xs[:, sl]
            xd_g = xd[:, sl]
            dec_g = dec_full[:, sl]
            cd_g = cd_full[:, sl]
            d_g = df_ref[:, sl]
            Bg = B_ref[:, g * D_STATE:(g + 1) * D_STATE].astype(BF16)
            Cg = C_ref[:, g * D_STATE:(g + 1) * D_STATE].astype(BF16)
            CB = _dot_nt(Cg, Bg)
            hg = hs_ref[0, g]
            hgb = hg.astype(BF16)
            yoff = _dot_nn(Cg, hgb) * e_full[:, sl]
            dhn = dh_scr[g]
            dhnb = dhn.astype(BF16)
            dYE = (dY * e_full[:, sl]).astype(BF16)
            dC = _dot_nt(dYE, hgb)
            dh_direct = _dot_tn(Cg, dYE)
            dXdd = _dot_nn(Bg, dhnb)
            dB = _dot_nt((xd_g * dec_g).astype(BF16), dhnb)
            dcd = jnp.sum(dhn * hg, axis=0, keepdims=True)
            dh_scr[g] = dh_direct + cd_g * dhn
            dYb = dY.astype(BF16)
            xd_b = xd_g.astype(BF16)
            dCB = jnp.zeros((Q, Q), F32)
            dXd = dXdd * dec_g
            for r in range(HEADS_PER_GROUP):
                h = g * HEADS_PER_GROUP + r
                Lm = _decay_matrix(cs, csT, h, causal)
                Gf = CB * Lm
                dYr = jnp.where(lane_head == r, dYb, jnp.zeros_like(dYb))
                dG = _dot_nt(dYr, xd_b)
                dCB = dCB + dG * Lm
                dXd = dXd + _dot_tn(Gf.astype(BF16), dYr)
                Mm = dG * Gf
                dcs16 = dcs16 + jnp.where(lane16 == h, jnp.sum(Mm, axis=1, keepdims=True), 0.0)
                col_sums = col_sums + jnp.where(sub16 == h, jnp.sum(Mm, axis=0, keepdims=True), 0.0)
            dCBb = dCB.astype(BF16)
            dC = dC + _dot_nn(dCBb, Bg)
            dB = dB + _dot_tn(dCBb, Cg)
            w_state = dXdd * dec_g * xd_g
            t_last = jnp.sum(w_state, axis=0, keepdims=True) + dcd * cd_g
            dcs_g = dY * yoff - w_state + jnp.where(is_last, t_last, 0.0)
            dcs16 = dcs16 + _hdot(dcs_g, red, "a")
            ddtx16 = ddtx16 + _hdot(dXd * xs_g, red, "a")
            dD16 = dD16 + _hdot(jnp.broadcast_to(jnp.sum(dY * xs_g, axis=0, keepdims=True), (8, GROUP_W)), red, "a")
            dxbc_ref[:, sl] = dXd * dt_full[:, sl] + dY * d_g
            dxbc_ref[:, D_INNER + g * D_STATE:D_INNER + (g + 1) * D_STATE] = dB
            dxbc_ref[:, D_INNER + 512 + g * D_STATE:D_INNER + 512 + (g + 1) * D_STATE] = dC
        eye = (lax.broadcasted_iota(jnp.int32, (N_HEADS, N_HEADS), 0)
               == lax.broadcasted_iota(jnp.int32, (N_HEADS, N_HEADS), 1)).astype(BF16)
        dcs16 = dcs16 - sum(_dot_tn(part, eye) for part in _split3(col_sums))
        da = _hdot(triu, dcs16, "b")
        ddt = da * A + ddtx16
        ddt_raw = ddt * _sigmoid(xdt)
        pr = lax.broadcasted_iota(jnp.int32, (N_HEADS, 128), 0)
        pc = lax.broadcasted_iota(jnp.int32, (N_HEADS, 128), 1)
        dz_ref[:, D_INNER:OFF_DT] = jnp.zeros((Q, OFF_DT - D_INNER), BF16)
        dz_ref[:, OFF_DT:OFF_DT + 128] = _hdot(ddt_raw, (pr == pc).astype(F32), "a").astype(BF16)
        dz_ref[:, OFF_DT + 128:] = jnp.zeros((Q, NP - OFF_DT - 128), BF16)
        ddtb_ref[...] += jnp.sum(ddt_raw, axis=0, keepdims=True)
        dal_ref[...] += jnp.sum(da * dt, axis=0, keepdims=True) * A
        dD_ref[...] += dD16[0:1, :]

    def rc(c):
        return nc - 1 - c

    in_specs = _ssd_in_specs(nc, True) + [
        pl.BlockSpec((CHUNK, D_INNER), lambda c: (rc(c), 0)),
        pl.BlockSpec((1, N_GROUPS, D_STATE, GROUP_W), lambda c: (rc(c), 0, 0, 0)),
        pl.BlockSpec((CHUNK, D_INNER), lambda c: (rc(c), 0)),
    ]
    small = pl.BlockSpec((1, N_HEADS), lambda c: (0, 0))
    return pl.pallas_call(
        body, grid=(nc,), in_specs=in_specs,
        out_specs=[pl.BlockSpec((CHUNK, NP), lambda c: (rc(c), 0)),
                   pl.BlockSpec((CHUNK, CONV_DIM), lambda c: (rc(c), 0)),
                   small, small, small,
                   pl.BlockSpec((1, D_INNER), lambda c: (0, 0))],
        out_shape=[jax.ShapeDtypeStruct((T, NP), BF16), jax.ShapeDtypeStruct((T, CONV_DIM), F32),
                   jax.ShapeDtypeStruct((1, N_HEADS), F32), jax.ShapeDtypeStruct((1, N_HEADS), F32),
                   jax.ShapeDtypeStruct((1, N_HEADS), F32), jax.ShapeDtypeStruct((1, D_INNER), F32)],
        scratch_shapes=[pltpu.VMEM((N_GROUPS, D_STATE, GROUP_W), F32)],
        name=name, compiler_params=_cparams(("arbitrary",)),
    )(xbc, xbc, xbc, proj, proj, dtT, dtb, dtbT, alog, alogT, dfull, ng, ypre, hs, dy)


N_PAIRS = ATTN_W // 128
PAIRS_PER_KV = N_PAIRS // 2
ATTN_SCALE = HEAD_DIM ** -0.5


def _kv_variants(kk):
    lo = lax.broadcasted_iota(jnp.int32, kk.shape, 1) < HEAD_DIM
    zero = jnp.zeros_like(kk)
    k00 = jnp.where(lo, kk, zero)
    k11 = jnp.where(lo, zero, kk)
    k01 = pltpu.roll(k00, HEAD_DIM, axis=1)
    k10 = pltpu.roll(k11, HEAD_DIM, axis=1)
    return [[k00.astype(BF16), k01.astype(BF16)], [k10.astype(BF16), k11.astype(BF16)]]


LOG2E = 1.4426950408889634


def _own_block():
    i = lax.broadcasted_iota(jnp.int32, (WINDOW, WINDOW), 0)
    j = lax.broadcasted_iota(jnp.int32, (WINDOW, WINDOW), 1)
    return j <= i


def _fold(own, a):
    return jnp.where(own, a[:, WINDOW:], a[:, :WINDOW])


def _attn_probs(qp, kvar, own, prev_bias, sk):
    s = _dot_nt(qp, kvar)
    sb = jnp.where(own, s[:, WINDOW:], s[:, :WINDOW] + prev_bias) * (ATTN_SCALE * LOG2E)
    sk2 = sk * LOG2E
    m = jnp.maximum(jnp.max(sb, axis=1, keepdims=True), sk2)
    pe = jnp.exp2(sb - m)
    es = jnp.exp2(sk2 - m)
    den = jnp.sum(pe, axis=1, keepdims=True) + es
    inv = 1.0 / den
    return pe * inv, es * inv


def _unfold(own, a):
    zero = jnp.zeros_like(a)
    return jnp.where(own, zero, a), jnp.where(own, a, zero)


def _sink(sinks, r):
    lane = lax.broadcasted_iota(jnp.int32, sinks.shape, 1)
    return jnp.sum(jnp.where(lane == r, sinks, 0.0), axis=1, keepdims=True)


def _kv_specs():
    return [pl.BlockSpec((WINDOW, KV_W), lambda n: (jnp.maximum(n - 1, 0), OFF_K // KV_W)),
            pl.BlockSpec((WINDOW, KV_W), lambda n: (n, OFF_K // KV_W)),
            pl.BlockSpec((WINDOW, KV_W), lambda n: (jnp.maximum(n - 1, 0), OFF_V // KV_W)),
            pl.BlockSpec((WINDOW, KV_W), lambda n: (n, OFF_V // KV_W))]


def _attn_fwd(proj, sinks, og, ycat, name):
    T = proj.shape[0]
    nb = T // WINDOW

    def body(q_ref, kp_ref, kc_ref, vp_ref, vc_ref, s_ref, og_ref, _, y_ref, o_ref, p_ref, ps_ref):
        n = pl.program_id(0)
        kv = _kv_variants(jnp.concatenate([kp_ref[...], kc_ref[...]], axis=0))
        vv = _kv_variants(jnp.concatenate([vp_ref[...], vc_ref[...]], axis=0))
        own = _own_block()
        prev_bias = jnp.where(n > 0, 0.0, NEG)
        sinks_v = s_ref[...]
        lane = lax.broadcasted_iota(jnp.int32, (1, 128), 1)
        ssq = jnp.zeros((WINDOW, 1), F32)
        sink_probs = jnp.zeros((WINDOW, 128), F32)
        for p in range(N_PAIRS):
            j = p // PAIRS_PER_KV
            qp = q_ref[:, p * 128:(p + 1) * 128].astype(BF16)
            o_pair = jnp.zeros((WINDOW, 128), F32)
            for par in range(2):
                r = 2 * p + par
                pn, ps = _attn_probs(qp, kv[j][par], own, prev_bias, _sink(sinks_v, r))
                pb = pn.astype(BF16)
                p_ref[:, r * 128:(r + 1) * 128] = pb
                sink_probs = jnp.where(lane == r, ps, sink_probs)
                p_prev, p_own = _unfold(own, pb)
                o_pair = o_pair + _dot_nn(p_prev, vv[j][par][:WINDOW]) + _dot_nn(p_own, vv[j][par][WINDOW:])
            o_ref[:, p * 128:(p + 1) * 128] = o_pair
            ssq = ssq + jnp.sum(o_pair * o_pair, axis=1, keepdims=True)
        ps_ref[...] = sink_probs
        rn = lax.rsqrt(ssq * (1.0 / ATTN_W) + EPS)
        y_ref[...] = (o_ref[...] * rn * og_ref[...]).astype(BF16)

    return pl.pallas_call(
        body, grid=(nb,),
        in_specs=[pl.BlockSpec((WINDOW, ATTN_W), lambda n: (n, OFF_Q // ATTN_W)), *_kv_specs(),
                  pl.BlockSpec((1, N_HEADS), lambda n: (0, 0)), pl.BlockSpec((1, ATTN_W), lambda n: (0, 0)), ANY],
        out_specs=[pl.BlockSpec((WINDOW, ATTN_W), lambda n: (n, 1)), pl.BlockSpec((WINDOW, ATTN_W), lambda n: (n, 0)),
                   pl.BlockSpec((WINDOW, N_HEADS * 128), lambda n: (n, 0)), pl.BlockSpec((WINDOW, 128), lambda n: (n, 0))],
        out_shape=[jax.ShapeDtypeStruct(ycat.shape, BF16), jax.ShapeDtypeStruct((T, ATTN_W), F32),
                   jax.ShapeDtypeStruct((T, N_HEADS * 128), BF16), jax.ShapeDtypeStruct((T, 128), F32)],
        input_output_aliases={7: 0}, name=name, compiler_params=_cparams(("parallel",)),
    )(proj, proj, proj, proj, proj, sinks, og, ycat)


def _attn_bwd(proj, og, o, dy, probs, sink_probs, dproj, name):
    T = proj.shape[0]
    nb = T // WINDOW

    def body(q_ref, kp_ref, kc_ref, vp_ref, vc_ref, og_ref, o_ref, dy_ref, p_ref, ps_ref, _,
             dq_ref, dk_ref, dv_ref, ds_ref, dog_ref, qt_scr, dot_scr, ds_scr, p_scr):
        n = pl.program_id(0)

        @pl.when(n == 0)
        def _():
            dk_ref[...] = jnp.zeros_like(dk_ref)
            dv_ref[...] = jnp.zeros_like(dv_ref)
            ds_ref[...] = jnp.zeros_like(ds_ref)
            dog_ref[...] = jnp.zeros_like(dog_ref)

        kv = _kv_variants(jnp.concatenate([kp_ref[...], kc_ref[...]], axis=0))
        vv = _kv_variants(jnp.concatenate([vp_ref[...], vc_ref[...]], axis=0))
        own = _own_block()
        sink_probs_v = ps_ref[...]
        of = o_ref[...]
        rn = lax.rsqrt(jnp.mean(of * of, axis=-1, keepdims=True) + EPS)
        oh = of * rn
        dyf = dy_ref[...]
        dog_ref[...] += jnp.sum(dyf * oh, axis=0, keepdims=True)
        doh = dyf * og_ref[...]
        do = rn * (doh - oh * jnp.mean(doh * oh, axis=-1, keepdims=True))
        lane = lax.broadcasted_iota(jnp.int32, (1, 128), 1)
        lane16 = lax.broadcasted_iota(jnp.int32, (1, N_HEADS), 1)
        dsink = jnp.zeros((1, N_HEADS), F32)
        for p in range(N_PAIRS):
            j = p // PAIRS_PER_KV
            q_t = q_ref[:, p * 128:(p + 1) * 128].T.astype(BF16)
            do_p = do[:, p * 128:(p + 1) * 128]
            o_p = of[:, p * 128:(p + 1) * 128]
            do_b = do_p.astype(BF16)
            do_t = do_p.T.astype(BF16)
            prod = do_p * o_p
            dq_pair = jnp.zeros((WINDOW, 128), F32)
            for par in range(2):
                r = 2 * p + par
                half = (lane < HEAD_DIM) if par == 0 else (lane >= HEAD_DIM)
                pb = p_ref[:, r * 128:(r + 1) * 128]
                ps = jnp.sum(jnp.where(lane == r, sink_probs_v, 0.0), axis=1, keepdims=True)
                delta = jnp.sum(jnp.where(half, prod, 0.0), axis=1, keepdims=True)
                dP = _fold(own, _dot_nt(do_b, vv[j][par]))
                dS = pb.astype(F32) * (dP - delta)
                dsink = dsink + jnp.where(lane16 == r, -jnp.sum(ps * delta, axis=0, keepdims=True), 0.0)
                dS_parts = _unfold(own, (dS * ATTN_SCALE).astype(BF16))
                p_parts = _unfold(own, pb)
                at = ((p % PAIRS_PER_KV) * 2 + par) * WINDOW
                qt_scr[j, :, at:at + WINDOW] = q_t[par * HEAD_DIM:(par + 1) * HEAD_DIM]
                dot_scr[j, :, at:at + WINDOW] = do_t[par * HEAD_DIM:(par + 1) * HEAD_DIM]
                for blk in range(2):
                    dq_pair = dq_pair + _dot_nn(dS_parts[blk], kv[j][par][blk * WINDOW:(blk + 1) * WINDOW])
                    ds_scr[j, blk, at:at + WINDOW, :] = dS_parts[blk]
                    p_scr[j, blk, at:at + WINDOW, :] = p_parts[blk]
            dq_ref[:, p * 128:(p + 1) * 128] = dq_pair.astype(BF16)
        rows = [pl.multiple_of(jnp.maximum(n - 1, 0) * WINDOW, WINDOW), pl.multiple_of(n * WINDOW, WINDOW)]
        for lhs, rhs, ref in [(qt_scr, ds_scr, dk_ref), (dot_scr, p_scr, dv_ref)]:
            for blk in range(2):
                both_t = jnp.concatenate([_dot_nn(lhs[j], rhs[j, blk]) for j in range(2)], axis=0)
                ref[pl.ds(rows[blk], WINDOW), :] += both_t.T
        ds_ref[...] += dsink

    full_kv = pl.BlockSpec((T, KV_W), lambda n: (0, 0))
    blk = pl.BlockSpec((WINDOW, ATTN_W), lambda n: (n, 0))
    return pl.pallas_call(
        body, grid=(nb,),
        in_specs=[pl.BlockSpec((WINDOW, ATTN_W), lambda n: (n, OFF_Q // ATTN_W)), *_kv_specs(),
                  pl.BlockSpec((1, ATTN_W), lambda n: (0, 0)), blk, pl.BlockSpec((WINDOW, ATTN_W), lambda n: (n, 1)),
                  pl.BlockSpec((WINDOW, N_HEADS * 128), lambda n: (n, 0)),
                  pl.BlockSpec((WINDOW, 128), lambda n: (n, 0)), ANY],
        out_specs=[pl.BlockSpec((WINDOW, ATTN_W), lambda n: (n, OFF_Q // ATTN_W)), full_kv, full_kv,
                   pl.BlockSpec((1, N_HEADS), lambda n: (0, 0)), pl.BlockSpec((1, ATTN_W), lambda n: (0, 0))],
        out_shape=[jax.ShapeDtypeStruct(dproj.shape, BF16), jax.ShapeDtypeStruct((T, KV_W), F32),
                   jax.ShapeDtypeStruct((T, KV_W), F32), jax.ShapeDtypeStruct((1, N_HEADS), F32),
                   jax.ShapeDtypeStruct((1, ATTN_W), F32)],
        scratch_shapes=[pltpu.VMEM((2, HEAD_DIM, 8 * WINDOW), BF16), pltpu.VMEM((2, HEAD_DIM, 8 * WINDOW), BF16),
                        pltpu.VMEM((2, 2, 8 * WINDOW, WINDOW), BF16), pltpu.VMEM((2, 2, 8 * WINDOW, WINDOW), BF16)],
        input_output_aliases={10: 0}, name=name, compiler_params=_cparams(("arbitrary",)),
    )(proj, proj, proj, proj, proj, og, o, dy, probs, sink_probs, dproj)


ANY = pl.BlockSpec(memory_space=pl.ANY)


def _coords():
    return lax.axis_index("x"), lax.axis_index("y"), lax.axis_index("c")


HBM = pl.BlockSpec(memory_space=pltpu.HBM)
SEM = pl.BlockSpec(memory_space=pltpu.SEMAPHORE)
EFFECT = pltpu.SideEffectType.DATAFLOW_SIDE_EFFECTING


def _in_hbm(a):
    return pltpu.with_memory_space_constraint(a, pltpu.HBM)


def _remote_start(srcs, lands, plan, n_copies, name, after=None):
    ns, nb = len(srcs), len(srcs) + len(lands)
    n_after = 0 if after is None else 1

    def body(*refs):
        src_refs, land_refs = refs[:ns], refs[ns:nb]
        send_sems, recv_sems = refs[nb + n_after], refs[nb + n_after + 1]
        token = refs[-1]
        x, y, c = _coords()
        for i, (sv, dv, dev) in enumerate(plan(src_refs, land_refs, x, y, c)):
            pltpu.make_async_remote_copy(src_ref=sv, dst_ref=dv, send_sem=send_sems.at[i], recv_sem=recv_sems.at[i],
                                         device_id=dev, device_id_type=MESH).start()
        token[...] = jnp.zeros_like(token)

    bufs = list(srcs) + list(lands)
    outs = pl.pallas_call(
        body, name=name,
        out_shape=(pltpu.SemaphoreType.DMA((n_copies,)), pltpu.SemaphoreType.DMA((n_copies,)),
                   *[pltpu.HBM(b.shape, b.dtype) for b in bufs], jax.ShapeDtypeStruct((8, 128), F32)),
        in_specs=[HBM] * nb + [ANY] * n_after,
        out_specs=(SEM, SEM, *[HBM] * nb, pl.BlockSpec(memory_space=pltpu.VMEM)),
        input_output_aliases={i: 2 + i for i in range(nb)},
        compiler_params=pltpu.CompilerParams(has_side_effects=EFFECT),
    )(*[_in_hbm(b) for b in bufs], *([] if after is None else [after]))
    return outs[0], outs[1], list(outs[2:2 + ns]), list(outs[2 + ns:2 + nb]), outs[-1]


def _remote_wait(started, after, plan, name):
    send_sems, recv_sems, srcs, lands, _ = started
    ns, nb = len(srcs), len(srcs) + len(lands)

    def body(*refs):
        src_refs, land_refs = refs[:ns], refs[ns:nb]
        send_sems, recv_sems = refs[nb], refs[nb + 1]
        x, y, c = _coords()
        for i, (sv, dv, dev) in enumerate(plan(src_refs, land_refs, x, y, c)):
            cp = pltpu.make_async_remote_copy(src_ref=sv, dst_ref=dv, send_sem=send_sems.at[i],
                                              recv_sem=recv_sems.at[i], device_id=dev, device_id_type=MESH)
            cp.wait_send()
            cp.wait_recv()

    bufs = list(srcs) + list(lands)
    outs = pl.pallas_call(
        body, name=name, out_shape=tuple(pltpu.HBM(b.shape, b.dtype) for b in bufs),
        in_specs=[HBM] * nb + [SEM, SEM, ANY], out_specs=tuple([HBM] * nb),
        input_output_aliases={i: i for i in range(nb)},
        compiler_params=pltpu.CompilerParams(has_side_effects=EFFECT),
    )(*bufs, send_sems, recv_sems, after)
    return list(outs[:ns]), list(outs[ns:])


def _pair_plan(src_refs, land_refs, x, y, c):
    plan = []
    for s, l in zip(src_refs, land_refs):
        for q in range(4):
            plan.append((s.at[2 * q + (1 - c)], l.at[q], (x, y, 1 - c)))
    return plan


def _pair4_plan(src_refs, land_refs, x, y, c):
    plan = []
    for s, l in zip(src_refs, land_refs):
        for q in range(4):
            plan.append((s.at[q], l.at[q], (x, y, 1 - c)))
    return plan


def _chips_plan(src_refs, land_refs, x, y, c):
    plan = []
    for s, l in zip(src_refs, land_refs):
        for k, (tx, ty) in enumerate([(1 - x, y), (x, 1 - y), (1 - x, 1 - y)]):
            plan.append((s.at[2 * tx + ty], l.at[k], (tx, ty, c)))
    return plan


def _everyone_plan(src_refs, land_refs, x, y, c):
    me = 4 * x + 2 * y + c
    plan = []
    for s, l in zip(src_refs, land_refs):
        for fx, fy, fc in [(0, 0, 1), (1, 0, 0), (1, 0, 1), (0, 1, 0), (0, 1, 1), (1, 1, 0), (1, 1, 1)]:
            dev = ((1 - x) if fx else x, (1 - y) if fy else y, (1 - c) if fc else c)
            plan.append((s, l.at[me], dev))
    return plan


def _pair_add(g8, r1, csel, tr, name):
    _, R, C = r1.shape
    g4 = g8.reshape(4, 2, R, C)

    def body(c_ref, g_ref, r_ref, o_ref):
        o_ref[...] = (g_ref[...].astype(F32) + r_ref[...].astype(F32)).astype(BF16)

    return pl.pallas_call(
        body,
        grid_spec=pltpu.PrefetchScalarGridSpec(
            num_scalar_prefetch=1, grid=(4, R // tr),
            in_specs=[pl.BlockSpec((None, None, tr, C), lambda q, i, cs: (q, cs[0], i, 0)),
                      pl.BlockSpec((None, tr, C), lambda q, i, cs: (q, i, 0))],
            out_specs=pl.BlockSpec((None, tr, C), lambda q, i, cs: (q, i, 0))),
        out_shape=jax.ShapeDtypeStruct((4, R, C), BF16), name=name,
        compiler_params=_cparams(("parallel", "parallel")),
    )(csel, g4, r1)


def _adamw_math(w, g, m, v):
    m = ADAM_B1 * m + (1.0 - ADAM_B1) * g
    v = ADAM_B2 * v + (1.0 - ADAM_B2) * (g * g)
    m_hat = m / (1.0 - ADAM_B1 ** ADAM_STEP)
    v_hat = v / (1.0 - ADAM_B2 ** ADAM_STEP)
    delta = -ADAM_LR * (m_hat / (jnp.sqrt(v_hat) + ADAM_EPS) + ADAM_WD * w)
    return delta, m, v


def _adamw_big(w, m, v, p4, r3, qsel, tile, name):
    R, C = w.shape
    tr, tc = tile

    def body(q_ref, w_ref, m_ref, v_ref, p_ref, r_ref, g_out, d_out, m_out, v_out):
        g = p_ref[...].astype(F32) + r_ref[0].astype(F32) + r_ref[1].astype(F32) + r_ref[2].astype(F32)
        d, mn, vn = _adamw_math(w_ref[...], g, m_ref[...], v_ref[...])
        g_out[...] = g
        d_out[...] = d
        m_out[...] = mn
        v_out[...] = vn

    blk = pl.BlockSpec((tr, tc), lambda i, j, qs: (i, j))
    return pl.pallas_call(
        body,
        grid_spec=pltpu.PrefetchScalarGridSpec(
            num_scalar_prefetch=1, grid=(R // tr, C // tc),
            in_specs=[blk, blk, blk, pl.BlockSpec((None, tr, tc), lambda i, j, qs: (qs[0], i, j)),
                      pl.BlockSpec((3, tr, tc), lambda i, j, qs: (0, i, j))],
            out_specs=[blk, blk, blk, blk]),
        out_shape=[jax.ShapeDtypeStruct((R, C), F32)] * 4, name=name,
        compiler_params=_cparams(("parallel", "parallel")),
    )(qsel, w, m, v, p4, r3)


def _adamw_tiled(w, p4, r3, sel, m, v, tc, name):
    R, C = w.shape
    RS = p4.shape[1]
    n = C // tc

    def body(s_ref, w_ref, p_ref, r_ref, m_ref, v_ref, g_out, d_out, m_out, v_out, buf, sems):
        j = pl.program_id(0)
        slot = j % 2
        outs = (g_out, d_out, m_out, v_out)

        def copies(step, at):
            cols = pl.ds(pl.multiple_of(step * tc, tc), tc)
            return [pltpu.make_async_copy(buf.at[at, k], out.at[:, 0, cols], sems.at[at, k])
                    for k, out in enumerate(outs)]

        g_sum = p_ref[...].astype(F32) + r_ref[0].astype(F32) + r_ref[1].astype(F32) + r_ref[2].astype(F32)
        gv = pltpu.roll(g_sum, RS - s_ref[1], axis=0)[:R]
        d, mn, vn = _adamw_math(w_ref[...], gv, m_ref[...], v_ref[...])

        @pl.when(j >= 2)
        def _():
            for c in copies(j - 2, slot):
                c.wait()

        for k, val in enumerate((gv, d, mn, vn)):
            buf[slot, k] = val
        for c in copies(j, slot):
            c.start()

        @pl.when(j == n - 1)
        def _():
            for c in (copies(j - 1, 1 - slot) if n > 1 else []) + copies(j, slot):
                c.wait()

    blk = pl.BlockSpec((R, tc), lambda j, s: (0, j))
    return pl.pallas_call(
        body,
        grid_spec=pltpu.PrefetchScalarGridSpec(
            num_scalar_prefetch=1, grid=(n,),
            in_specs=[blk, pl.BlockSpec((None, RS, tc), lambda j, s: (s[0], 0, j)),
                      pl.BlockSpec((3, RS, tc), lambda j, s: (0, 0, j)), blk, blk], out_specs=[ANY] * 4,
            scratch_shapes=[pltpu.VMEM((2, 4, R, tc), F32), pltpu.SemaphoreType.DMA((2, 4))]),
        out_shape=[jax.ShapeDtypeStruct((R, 1, C), F32)] * 4,
        name=name, compiler_params=_cparams(("arbitrary",)),
    )(sel, w, p4, r3, m, v)


def _small_sum(parts, name):
    def body(p_ref, o_ref):
        acc = p_ref[0]
        for d in range(1, N_DEV):
            acc = acc + p_ref[d]
        o_ref[...] = acc

    return pl.pallas_call(
        body, out_shape=jax.ShapeDtypeStruct(parts.shape[1:], F32), name=name,
        compiler_params=_cparams(),
    )(parts)


def _adamw_small(w, g, m, v, name):
    def body(w_ref, g_ref, m_ref, v_ref, d_out, m_out, v_out):
        d, mn, vn = _adamw_math(w_ref[...], g_ref[...], m_ref[...], v_ref[...])
        d_out[...] = d
        m_out[...] = mn
        v_out[...] = vn

    return pl.pallas_call(
        body, out_shape=[jax.ShapeDtypeStruct(w.shape, F32)] * 3, name=name, compiler_params=_cparams(),
    )(w, g, m, v)


def _row(*pieces):
    r = jnp.concatenate([p.reshape(1, -1) for p in pieces], axis=1)
    return jnp.pad(r, ((0, 0), (0, D_MODEL - r.shape[1])))


def _pack_small(mix, convb, ssmg, attng, mlpg, fing, convw, dtb, alog, dsk, sinks, extra=None):
    last = [dtb, alog, dsk, sinks] + ([extra] if extra is not None else [])
    rows = [_row(mix), _row(convb), _row(ssmg, attng), _row(mlpg), _row(fing),
            jnp.pad(convw, ((0, 0), (0, D_MODEL - convw.shape[1]))), _row(*last)]
    packed = jnp.concatenate(rows, axis=0)
    return jnp.pad(packed, ((0, SMALL_ROWS - packed.shape[0]), (0, 0)))


def _unpack_small(p, conv_n):
    return dict(
        mix_norm_g=p[0:1, :], conv_b=p[1:2, :], ssm_norm_g=p[2:3, :D_INNER], attn_out_norm_g=p[2:3, D_INNER:],
        mlp_norm_g=p[3:4, :], final_norm_g=p[4, :], conv_w=p[5:9, :conv_n][None],
        dt_bias=p[9:10, 0:16], A_log=p[9:10, 16:32], D_skip=p[9:10, 32:48], attn_sinks=p[9:10, 48:64])


WEIGHT_ORDER = ["mix_norm_g", "w_in", "conv_w", "conv_b", "dt_bias", "A_log", "D_skip", "ssm_norm_g", "attn_sinks",
                "attn_out_norm_g", "w_out", "mlp_norm_g", "w_up", "w_down", "final_norm_g"]


def _to_my_columns(w_nat):
    pad = jnp.zeros((w_nat.shape[0], NP - IN_PROJ), w_nat.dtype)
    return jnp.concatenate([w_nat[:, :NAT_DT], w_nat[:, NAT_DT + N_HEADS:], w_nat[:, NAT_DT:NAT_DT + N_HEADS], pad],
                           axis=1)


PER = IN_PROJ // N_DEV
SUPER_STEP = 544
SUPER = 576


def _natural_rows(g, lo, hi):
    segments = [(0, NAT_DT, 0), (NAT_DT, NAT_DT + N_HEADS, OFF_DT - NAT_DT), (NAT_DT + N_HEADS, IN_PROJ, -N_HEADS),
                (IN_PROJ, NP, 0)]
    pieces = [g[max(lo, a) + shift:min(hi, b) + shift] for a, b, shift in segments if max(lo, a) < min(hi, b)]
    return pieces[0] if len(pieces) == 1 else jnp.concatenate(pieces, axis=0)


def _w_in_from_super_slabs(sup):
    seam = SUPER - SUPER_STEP
    units = []
    for i in range(N_DEV):
        base = SUPER_STEP * i
        units.append((base, base + seam, sup[i, :seam] if i == 0 else sup[i - 1, SUPER_STEP:] + sup[i, :seam]))
        units.append((base + seam, base + SUPER_STEP, sup[i, seam:SUPER_STEP]))
    units.append((SUPER_STEP * N_DEV, SUPER_STEP * N_DEV + seam, sup[N_DEV - 1, SUPER_STEP:]))

    def natural(lo, hi):
        return [rows[max(lo, a) - a:min(hi, b) - a] for a, b, rows in units if max(lo, a) < min(hi, b)]

    pieces = natural(0, NAT_DT) + natural(NAT_DT + N_HEADS, IN_PROJ) + natural(NAT_DT, NAT_DT + N_HEADS)
    return jnp.concatenate(pieces + [jnp.zeros((NP - IN_PROJ, D_MODEL), sup.dtype)], axis=0)


def _to_natural_columns(w_my):
    return jnp.concatenate([w_my[:, :NAT_DT], w_my[:, OFF_DT:OFF_DT + N_HEADS], w_my[:, NAT_DT:OFF_DT]], axis=1)


SLAB = 1024


def _grad_w_up(h2, du, name, sel=None, add=None, after=None):
    T, D = h2.shape
    if sel is None:
        pick, n_slab, pre = (lambda j, *cs: j), N_DEV, None
    else:
        pre, other = sel
        pick, n_slab = (lambda j, cs: 2 * j + ((1 - cs[0]) if other else cs[0])), 4
    o_spec = pl.BlockSpec((None, D, SLAB), lambda i, j, k, *cs: (j, 0, 0))
    return _matmul(
        h2, du, mode="tn", grid=(1, n_slab, 1),
        a_spec=pl.BlockSpec((T, D), lambda i, j, k, *cs: (0, 0)),
        b_spec=pl.BlockSpec((T, SLAB), lambda i, j, k, *cs: (0, pick(j, *cs))),
        out_shapes=[jax.ShapeDtypeStruct((n_slab, D, SLAB), BF16)], out_specs=[o_spec], tile=(D, SLAB), name=name,
        extras=() if add is None else (add,), extra_specs=() if add is None else (o_spec,),
        epilogue=None if add is None else (lambda acc, r: (acc + r.astype(F32),)), after=after, prefetch=pre)[0]


def _grad_w_down(act, dx3b, name, sel=None, add=None, after=None):
    T, D = dx3b.shape
    if sel is None:
        pick, n_slab, pre = (lambda i, *cs: i), N_DEV, None
    else:
        pre, other = sel
        pick, n_slab = (lambda i, cs: 2 * i + ((1 - cs[0]) if other else cs[0])), 4
    o_spec = pl.BlockSpec((None, SLAB, D), lambda i, j, k, *cs: (i, 0, 0))
    return _matmul(
        act, dx3b, mode="tn", grid=(n_slab, 1, 1),
        a_spec=pl.BlockSpec((T, SLAB), lambda i, j, k, *cs: (0, pick(i, *cs))),
        b_spec=pl.BlockSpec((T, D), lambda i, j, k, *cs: (0, 0)),
        out_shapes=[jax.ShapeDtypeStruct((n_slab, SLAB, D), BF16)], out_specs=[o_spec], tile=(SLAB, D), name=name,
        extras=() if add is None else (add,), extra_specs=() if add is None else (o_spec,),
        epilogue=None if add is None else (lambda acc, r: (acc + r.astype(F32),)), after=after, prefetch=pre)[0]


class _FixedWeights:
    def __init__(self, w_in_p, w_out_f, w_up_s, w_down_f, conv_w_f):
        self.w = (w_in_p, w_out_f, w_up_s, w_down_f, conv_w_f)
        self.grads = {}

    def mixer_weights(self, after):
        return self.w[0], None

    def conv_weight(self, after):
        return self.w[4]

    def out_weight(self, after):
        return self.w[1]

    def up_weight(self, after):
        return self.w[2]

    def down_weight(self, h, after):
        return self.w[3][:, h * (D_MODEL // 2):(h + 1) * (D_MODEL // 2)]

    def mlp_grads(self, h2, du, act, dx3b):
        self.grads.update(w_up=_grad_w_up(h2, du, "grad_w_up"),
                          w_down=_grad_w_down(act, dx3b, "grad_w_down").reshape(D_FF, D_MODEL))
        return None

    def grad_sent(self, tag, after):
        return None

    def out_grad(self, g_out):
        self.grads.update(w_out=g_out)
        return None

    def in_grad(self, g_in):
        self.grads.update(w_in=g_in)
        return None


def _local_step(x, tgt, p, hooks):
    T = x.shape[0]
    D = D_MODEL
    h1 = _rmsnorm_fwd(x, p["mix_norm_g"], "norm_mix")
    w_in_t, token = hooks.mixer_weights(h1)
    (proj,) = _mm_simple(h1, w_in_t, mode="nt", M=T, N=NP, K=D, tm=min(T, 1024), tn=1536, tk=D, out_dtype=F32,
                         name="in_proj", after=token)
    conv_w_f = hooks.conv_weight(proj)
    xbc, dsilu = _conv_fwd(proj, conv_w_f, p["conv_b"], "conv_fwd")
    dtT = proj[:, OFF_DT:OFF_DT + N_HEADS].T
    dtbT = p["dt_bias"].T
    alogT = p["A_log"].T
    dfull = jnp.repeat(p["D_skip"], HEAD_DIM, axis=1)
    ycat, ypre, hs = _ssd_fwd(xbc, proj, dtT, p["dt_bias"], dtbT, p["A_log"], alogT, dfull, p["ssm_norm_g"],
                              "ssd_fwd")
    ycat, o_att, probs, sink_probs = _attn_fwd(proj, p["attn_sinks"], p["attn_out_norm_g"], ycat, "attn_fwd")
    w_out_f = hooks.out_weight(ycat)
    tm = min(T, 1024)
    def residual_and_norm(acc, res, gain):
        x2 = acc + res
        return x2, x2 * lax.rsqrt(jnp.mean(x2 * x2, axis=-1, keepdims=True) + EPS) * gain

    rows = min(T, 512)
    x2, h2 = _matmul(
        ycat, w_out_f, mode="nn", grid=(T // rows, 1, 1),
        a_spec=pl.BlockSpec((rows, D), lambda i, j, k: (i, 0)), b_spec=pl.BlockSpec((D, D), lambda i, j, k: (0, 0)),
        out_shapes=[jax.ShapeDtypeStruct((T, D), F32), jax.ShapeDtypeStruct((T, D), BF16)],
        out_specs=[pl.BlockSpec((rows, D), lambda i, j, k: (i, 0))] * 2, tile=(rows, D), name="out_proj",
        extras=(x, p["mlp_norm_g"]),
        extra_specs=[pl.BlockSpec((rows, D), lambda i, j, k: (i, 0)), pl.BlockSpec((1, D), lambda i, j, k: (0, 0))],
        epilogue=residual_and_norm)
    w_up_s = hooks.up_weight(h2)
    grid = (T // tm, N_DEV, 1)
    u, act = _matmul(
        h2, w_up_s, mode="nn", grid=grid,
        a_spec=pl.BlockSpec((tm, D), lambda i, j, k: (i, 0)),
        b_spec=pl.BlockSpec((None, D, 1024), lambda i, j, k: (j, 0, 0)),
        out_shapes=[jax.ShapeDtypeStruct((T, D_FF), F32), jax.ShapeDtypeStruct((T, D_FF), BF16)],
        out_specs=[pl.BlockSpec((tm, 1024), lambda i, j, k: (i, j))] * 2, tile=(tm, 1024), name="mlp_up",
        epilogue=lambda acc: (acc, jnp.square(jnp.maximum(acc, 0.0))))
    half = D // 2
    w_down_halves, x3_halves = [], []
    for h in range(2):
        w_down_halves.append(hooks.down_weight(h, act if h == 0 else x3_halves[0]))
        x3_halves.append(_matmul(
            act, w_down_halves[h], mode="nn", grid=(T // tm, 1, D_FF // 2048),
            a_spec=pl.BlockSpec((tm, 2048), lambda i, j, k: (i, k)),
            b_spec=pl.BlockSpec((2048, half), lambda i, j, k: (k, 0)),
            out_shapes=[jax.ShapeDtypeStruct((T, half), F32)],
            out_specs=[pl.BlockSpec((tm, half), lambda i, j, k: (i, 0))], tile=(tm, half), name=f"mlp_down_{h}",
            extras=(x2,), extra_specs=[pl.BlockSpec((tm, half), lambda i, j, k, h=h: (i, h))],
            epilogue=lambda acc, res: (acc + res,))[0])
    loss_part, d_fin, dx3, dx3b = _final_loss(x3_halves, tgt, p["final_norm_g"].reshape(1, D), "loss_head")
    (du,) = _matmul(
        dx3b, tuple(w_down_halves), mode="nt", grid=(T // tm, D_FF // 1024, 1),
        a_spec=pl.BlockSpec((tm, D), lambda i, j, k: (i, 0)),
        b_spec=(pl.BlockSpec((1024, half), lambda i, j, k: (j, 0)),) * 2,
        out_shapes=[jax.ShapeDtypeStruct((T, D_FF), BF16)],
        out_specs=[pl.BlockSpec((tm, 1024), lambda i, j, k: (i, j))], tile=(tm, 1024), name="mlp_down_bwd",
        extras=(u,), extra_specs=[pl.BlockSpec((tm, 1024), lambda i, j, k: (i, j))],
        epilogue=lambda acc, uu: (acc * (2.0 * jnp.maximum(uu, 0.0)),),
        dot_fn=lambda a, b0, b1: _dot_nt(a[:, :half], b0) + _dot_nt(a[:, half:], b1))
    token = hooks.mlp_grads(h2, du, act, dx3b)
    (dh2,) = _matmul(
        du, w_up_s, mode="nt", grid=(T // tm, D // 1024, N_DEV // 2),
        a_spec=pl.BlockSpec((tm, 2048), lambda i, j, k: (i, k)),
        b_spec=pl.BlockSpec((2, 1024, 1024), lambda i, j, k: (k, j, 0)),
        out_shapes=[jax.ShapeDtypeStruct((T, D), F32)],
        out_specs=[pl.BlockSpec((tm, 1024), lambda i, j, k: (i, j))], tile=(tm, 1024), name="mlp_up_bwd",
        after=token, dot_fn=lambda a, b: _dot_nt(a[:, :1024], b[0]) + _dot_nt(a[:, 1024:], b[1]))
    dx2, dx2b, d_mlp = _rmsnorm_bwd(dh2, x2, p["mlp_norm_g"], dx3, "norm_mlp_bwd")
    (g_out,) = _mm_simple(ycat, dx2b, mode="tn", M=D, N=D, K=T, tm=1024, tn=1024, tk=T, out_dtype=BF16,
                          name="grad_w_out")
    token = hooks.out_grad(g_out)
    (dy,) = _mm_simple(dx2b, w_out_f, mode="nt", M=T, N=D, K=D, tm=tm, tn=1024, tk=D, out_dtype=F32,
                       name="out_proj_bwd", after=token)
    token = hooks.grad_sent("out", dy)
    ssm_g = p["ssm_norm_g"] if token is None else p["ssm_norm_g"] + token[0:1, 0:1]
    dproj, dxbc_act, d_dtb, d_alog, d_dskip, d_ssmg = _ssd_bwd(
        xbc, proj, dtT, p["dt_bias"], dtbT, p["A_log"], alogT, dfull, ssm_g, ypre, hs, dy, "ssd_bwd")
    dproj, d_convw, d_convb = _conv_bwd(proj, dxbc_act, dsilu, conv_w_f, dproj, "conv_bwd")
    dproj, dk, dv, d_sinks, d_attng = _attn_bwd(proj, p["attn_out_norm_g"], o_att, dy, probs, sink_probs, dproj,
                                                "attn_bwd")
    dproj = lax.dynamic_update_slice(dproj, jnp.concatenate([dk, dv], axis=1).astype(BF16), (0, OFF_K))
    (g_in,) = _mm_simple(dproj, h1, mode="tn", M=NP, N=D, K=T, tm=1536, tn=1024, tk=T, out_dtype=BF16,
                         name="grad_w_in")
    token = hooks.in_grad(g_in)
    (dh1,) = _mm_simple(dproj, w_in_t, mode="nn", M=T, N=D, K=NP, tm=tm, tn=1024, tk=2304, out_dtype=F32,
                        name="in_proj_bwd", after=token)
    token = hooks.grad_sent("in", dh1)
    mix_g = p["mix_norm_g"] if token is None else p["mix_norm_g"] + token[0:1, 0:1]
    dx, d_mix = _rmsnorm_bwd(dh1, x, mix_g, dx2, "norm_mix_bwd", with_bf16=False)
    small = _pack_small(d_mix, d_convb, d_ssmg, d_attng, d_mlp, d_fin, d_convw, d_dtb, d_alog, d_dskip, d_sinks,
                        extra=loss_part[:, 0:1])
    return dx, small


def _rows_rotated(v, shift, name):
    R, C = v.shape
    tc = 512

    def body(s_ref, v_ref, o_ref):
        o_ref[...] = pltpu.roll(v_ref[...], s_ref[0], axis=0).astype(BF16)

    return pl.pallas_call(
        body,
        grid_spec=pltpu.PrefetchScalarGridSpec(
            num_scalar_prefetch=1, grid=(C // tc,), in_specs=[pl.BlockSpec((R, tc), lambda j, s: (0, j))],
            out_specs=pl.BlockSpec((R, tc), lambda j, s: (0, j))),
        out_shape=jax.ShapeDtypeStruct((R, C), BF16), name=name, compiler_params=_cparams(("parallel",)),
    )(shift, v)


def _landing(own, me):
    zone = lax.empty((N_DEV,) + own.shape, own.dtype)
    return lax.dynamic_update_slice(zone, own[None], (me,) + (0,) * own.ndim)


def _sequencer_gather(owns, split, me, collective_id, name):
    n = len(owns)
    zone_refs = [jax.new_ref(_landing(o, me), memory_space=pltpu.MemorySpace.HBM) for o in owns]
    own_refs = [jax.new_ref(o, memory_space=pltpu.MemorySpace.HBM) for o in owns]
    N_COPIES = 9

    @pl.kernel(mesh=plsc.ScalarSubcoreMesh(axis_name="sequencer", num_cores=1), name=name,
               scratch_types=(pltpu.SemaphoreType.DMA((n, N_COPIES)), pltpu.SemaphoreType.DMA((n, N_COPIES))),
               compiler_params=pltpu.CompilerParams(collective_id=collective_id))
    def launch(send_sems, recv_sems):
        x, y, c = _coords()
        sibling, xn, yn, diag = (x, y, 1 - c), (1 - x, y, c), (x, 1 - y, c), (1 - x, 1 - y, c)
        barrier = pltpu.get_barrier_semaphore()
        for peer in [sibling, xn, yn, diag]:
            pl.semaphore_signal(barrier, inc=1, device_id=peer, device_id_type=MESH)
        pl.semaphore_wait(barrier, 4)

        def block(a, dev, half=None):
            ref = zone_refs[a].at[4 * dev[0] + 2 * dev[1] + dev[2]]
            if half is None:
                return ref
            rows = owns[a].shape[0] // 2
            return ref.at[pl.ds(half * rows, rows)]

        def copy(a, k, src, dst, to):
            return pltpu.make_async_remote_copy(src_ref=src, dst_ref=dst, send_sem=send_sems.at[a, k],
                                                recv_sem=recv_sems.at[a, k], device_id=to, device_id_type=MESH)

        me_dev = (x, y, c)
        sent = []
        first = {}
        for a in range(n):
            for k, peer in enumerate([sibling, xn, yn] + ([] if split[a] else [diag])):
                first[a, k] = copy(a, k, own_refs[a], block(a, me_dev), peer)
                first[a, k].start()
                sent.append(first[a, k])
        from_sibling = []
        for a in range(n):
            first[a, 1].wait_recv()
            sent.append(copy(a, 4, block(a, xn), block(a, xn), sibling))
            if split[a]:
                sent.append(copy(a, 6, block(a, xn, 0), block(a, xn, 0), yn))
            first[a, 2].wait_recv()
            sent.append(copy(a, 5, block(a, yn), block(a, yn), sibling))
            if split[a]:
                sent.append(copy(a, 7, block(a, yn, 1), block(a, yn, 1), xn))
            for cp in sent[-(4 if split[a] else 2):]:
                cp.start()
        for a in range(n):
            if split[a]:
                copy(a, 6, block(a, diag, 0), block(a, diag, 0), yn).wait_recv()
                sent.append(copy(a, 8, block(a, diag, 0), block(a, diag, 0), sibling))
                sent[-1].start()
                copy(a, 7, block(a, diag, 1), block(a, diag, 1), xn).wait_recv()
                sent.append(copy(a, 3, block(a, diag, 1), block(a, diag, 1), sibling))
                sent[-1].start()
            else:
                first[a, 3].wait_recv()
                sent.append(copy(a, 8, block(a, diag), block(a, diag), sibling))
                sent[-1].start()
        for a in range(n):
            first[a, 0].wait_recv()
            copy(a, 4, block(a, xn), block(a, xn), sibling).wait_recv()
            copy(a, 5, block(a, yn), block(a, yn), sibling).wait_recv()
            if split[a]:
                copy(a, 8, block(a, diag, 0), block(a, diag, 0), sibling).wait_recv()
                copy(a, 3, block(a, diag, 1), block(a, diag, 1), sibling).wait_recv()
            else:
                copy(a, 8, block(a, diag), block(a, diag), sibling).wait_recv()
        for cp in sent:
            cp.wait_send()

    launch()
    return zone_refs


class _ShardedWeights:
    def __init__(self, w_in, w_out, conv_w, w_up, w_down, me, csel):
        self.me, self.csel = me, csel
        padded = jnp.pad(jnp.transpose(w_in), ((0, SUPER - PER), (0, 0)))
        own_rows = _rows_rotated(padded, jnp.reshape(2 * me, (1,)).astype(jnp.int32), "w_in_super_slab")
        (self.in_ref,) = _sequencer_gather([own_rows], [True], me, 7, "gather_w_in_sequencer")
        self.out_ref, self.conv_ref = _sequencer_gather([w_out.astype(BF16), conv_w], [True, False], me, 8,
                                                        "gather_w_out_sequencer")
        (self.up_ref,) = _sequencer_gather([w_up.astype(BF16)], [True], me, 9, "gather_w_up_sequencer")
        down = w_down.astype(BF16)
        self.down_refs = [_sequencer_gather([down[:, h * (D_MODEL // 2):(h + 1) * (D_MODEL // 2)]], [True], me, 10 + h,
                                            f"gather_w_down_{h}_sequencer")[0] for h in range(2)]
        self.reduces = {}
        self.pairs = {}

    def mixer_weights(self, after):
        return _w_in_from_super_slabs(self.in_ref[...]), None

    def conv_weight(self, after):
        g_conv = self.conv_ref[...]
        return jnp.concatenate([g_conv[i] for i in range(N_DEV)], axis=1)

    def out_weight(self, after):
        return self.out_ref[...].reshape(D_MODEL, D_MODEL)

    def up_weight(self, after):
        return self.up_ref[...]

    def down_weight(self, h, after):
        return self.down_refs[h][...].reshape(D_FF, D_MODEL // 2)

    def _chips_start(self, slabs, from_sibling, rows, tag):
        sums = [_pair_add(s, r, self.csel, tr, f"pair_add_{tag}_{i}")
                for i, (s, r, tr) in enumerate(zip(slabs, from_sibling, rows))]
        lands = [lax.empty((3,) + s.shape[1:], s.dtype) for s in sums]
        self.reduces[tag] = _remote_start(sums, lands, _chips_plan, 3 * len(sums), f"reduce_start_{tag}")
        return self.reduces[tag][4]

    def mlp_grads(self, h2, du, act, dx3b):
        def send(part, tag, after):
            st = _remote_start([part], [lax.empty(part.shape, part.dtype)], _pair4_plan, 4,
                               f"reduce_pair_start_{tag}", after=after)
            return st

        def received(st, after, tag):
            return _remote_wait(st, after, _pair4_plan, f"reduce_pair_wait_{tag}")[1][0]

        def to_chips(sums, tag):
            self.reduces[tag] = _remote_start([sums], [lax.empty((3,) + sums.shape[1:], sums.dtype)], _chips_plan, 3,
                                              f"reduce_start_{tag}")
            return self.reduces[tag][4]

        up_send = _grad_w_up(h2, du, "grad_w_up_send", sel=(self.csel, True))
        st_up = send(up_send, "up", None)
        down_send = _grad_w_down(act, dx3b, "grad_w_down_send", sel=(self.csel, True), after=st_up[4])
        st_down = send(down_send, "down", None)
        up_sum = _grad_w_up(h2, du, "grad_w_up_keep", sel=(self.csel, False), add=received(st_up, down_send, "up"),
                            after=st_down[4])
        token = to_chips(up_sum, "up")
        down_sum = _grad_w_down(act, dx3b, "grad_w_down_keep", sel=(self.csel, False),
                                add=received(st_down, up_sum, "down"), after=token)
        return to_chips(down_sum, "down")

    def _pair_start(self, slabs, tag):
        land = lax.empty((4,) + slabs.shape[1:], slabs.dtype)
        self.pairs[tag] = _remote_start([slabs], [land], _pair_plan, 4, f"reduce_pair_start_{tag}")
        return self.pairs[tag][4]

    def grad_sent(self, tag, after):
        slabs, from_sibling = _remote_wait(self.pairs[tag], after, _pair_plan, f"reduce_pair_wait_{tag}")
        return self._chips_start(slabs, from_sibling, [slabs[0].shape[1]], tag)

    def out_grad(self, g_out):
        return self._pair_start(g_out.reshape(N_DEV, D_MODEL // N_DEV, D_MODEL), "out")

    def in_grad(self, g_in):
        return self._pair_start(
            jnp.stack([_natural_rows(g_in, SUPER_STEP * j, SUPER_STEP * j + SUPER) for j in range(N_DEV)]), "in")

    def small_start(self, small):
        self.st_small = _remote_start([small], [_landing(small, self.me)], _everyone_plan, N_DEV - 1, "gather_start_small")

    def small_end(self, after):
        return _remote_wait(self.st_small, after, _everyone_plan, "gather_small_wait")[1][0]

    def reduce_end(self, tag, after):
        return _remote_wait(self.reduces[tag], after, _chips_plan, f"reduce_wait_{tag}")


def kernel(x, mix_norm_g, w_in, conv_w, conv_b, dt_bias, A_log, D_skip, ssm_norm_g, attn_sinks, attn_out_norm_g, w_out, mlp_norm_g, w_up, w_down, final_norm_g, loss_target, m_mix_norm_g, m_w_in, m_conv_w, m_conv_b, m_dt_bias, m_A_log, m_D_skip, m_ssm_norm_g, m_attn_sinks, m_attn_out_norm_g, m_w_out, m_mlp_norm_g, m_w_up, m_w_down, m_final_norm_g, v_mix_norm_g, v_w_in, v_conv_w, v_conv_b, v_dt_bias, v_A_log, v_D_skip, v_ssm_norm_g, v_attn_sinks, v_attn_out_norm_g, v_w_out, v_mlp_norm_g, v_w_up, v_w_down, v_final_norm_g):
    xi, yi, ci = _coords()
    me = 4 * xi + 2 * yi + ci
    csel = jnp.reshape(ci, (1,)).astype(jnp.int32)
    qsel = jnp.reshape(2 * xi + yi, (1,)).astype(jnp.int32)
    w = dict(mix_norm_g=mix_norm_g, conv_b=conv_b, dt_bias=dt_bias, A_log=A_log, D_skip=D_skip,
             ssm_norm_g=ssm_norm_g, attn_sinks=attn_sinks, attn_out_norm_g=attn_out_norm_g, mlp_norm_g=mlp_norm_g,
             final_norm_g=final_norm_g)
    hooks = _ShardedWeights(w_in[0], w_out[0], conv_w[0], w_up[0], w_down[0], me, csel)
    p = dict(w)
    dx, small = _local_step(x[0], loss_target[0], p, hooks)
    hooks.small_start(small)
    big = {}
    after = dx
    for name, wt, mt, vt, tile in [
            ("up", w_up, m_w_up, v_w_up, (512, SLAB)), ("down", w_down, m_w_down, v_w_down, (256, D_MODEL)),
            ("out", w_out, m_w_out, v_w_out, (256, D_MODEL))]:
        (chip_sums,), (from_chips,) = hooks.reduce_end(name, after)
        res = _adamw_big(wt[0], mt[0], vt[0], chip_sums, from_chips, qsel, tile, f"adamw_w_{name}")
        big["w_" + name] = tuple(r[None] for r in res)
        after = res[0]
    (chip_sums,), (from_chips,) = hooks.reduce_end("in", after)
    sel = jnp.concatenate([qsel, jnp.reshape(2 * me, (1,)).astype(jnp.int32)])
    res = _adamw_tiled(jnp.transpose(w_in[0]), chip_sums, from_chips, sel, jnp.transpose(m_w_in[0]),
                       jnp.transpose(v_w_in[0]), 512, "adamw_w_in")
    big["w_in"] = tuple(jnp.transpose(r, (1, 2, 0)) for r in res)
    after = res[0]
    gsum = _small_sum(hooks.small_end(after), "small_sum")
    loss = gsum[9, 64]
    gs = _unpack_small(gsum, CONV_DIM)
    cw = CONV_DIM // N_DEV
    g_conv_shard = lax.dynamic_slice(gsum[5:9, :], (0, me * cw), (CONV_K, cw))

    def pack(s):
        return _pack_small(s["mix_norm_g"], s["conv_b"], s["ssm_norm_g"], s["attn_out_norm_g"], s["mlp_norm_g"],
                           s["final_norm_g"], s["conv_w"][0], s["dt_bias"], s["A_log"], s["D_skip"], s["attn_sinks"])

    wp = pack(dict(w, conv_w=conv_w))
    mp = pack(dict(mix_norm_g=m_mix_norm_g, conv_b=m_conv_b, ssm_norm_g=m_ssm_norm_g,
                   attn_out_norm_g=m_attn_out_norm_g, mlp_norm_g=m_mlp_norm_g, final_norm_g=m_final_norm_g,
                   conv_w=m_conv_w, dt_bias=m_dt_bias, A_log=m_A_log, D_skip=m_D_skip, attn_sinks=m_attn_sinks))
    vp = pack(dict(mix_norm_g=v_mix_norm_g, conv_b=v_conv_b, ssm_norm_g=v_ssm_norm_g,
                   attn_out_norm_g=v_attn_out_norm_g, mlp_norm_g=v_mlp_norm_g, final_norm_g=v_final_norm_g,
                   conv_w=v_conv_w, dt_bias=v_dt_bias, A_log=v_A_log, D_skip=v_D_skip, attn_sinks=v_attn_sinks))
    gp = jnp.concatenate([gsum[0:5], jnp.pad(g_conv_shard, ((0, 0), (0, D_MODEL - cw))), gsum[9:10],
                          jnp.zeros((SMALL_ROWS - 10, D_MODEL), F32)], axis=0)
    dp, mnp, vnp = _adamw_small(wp, gp, mp, vp, "adamw_small")
    grads = dict(gs, conv_w=g_conv_shard[None])
    deltas = _unpack_small(dp, cw)
    new_m = _unpack_small(mnp, cw)
    new_v = _unpack_small(vnp, cw)
    for k, name in enumerate(["w_in", "w_out", "w_up", "w_down"]):
        grads[name], deltas[name], new_m[name], new_v[name] = big[name]
    return (loss, dx[None], *[grads[n] for n in WEIGHT_ORDER], *[deltas[n] for n in WEIGHT_ORDER],
            *[new_m[n] for n in WEIGHT_ORDER], *[new_v[n] for n in WEIGHT_ORDER])
```

```python
import jax
import jax.numpy as jnp
from jax import lax
from jax.experimental import pallas as pl
from jax.experimental.pallas import tpu as pltpu
from jax.experimental.pallas import tpu_sc as plsc

F32 = jnp.float32
BF16 = jnp.bfloat16
MESH = pl.DeviceIdType.MESH

EPS = 1e-5
D_MODEL = 2048
D_INNER = 1024
N_HEADS = 16
HEAD_DIM = 64
N_GROUPS = 4
D_STATE = 128
CHUNK = 128
CONV_K = 4
CONV_DIM = 2048
ATTN_W = 1024
KV_W = 128
WINDOW = 128
D_FF = 8192
IN_PROJ = 4368
N_DEV = 8
NP = 4608
OFF_Z, OFF_X, OFF_B, OFF_C, OFF_Q, OFF_K, OFF_V, OFF_DT = 0, 1024, 2048, 2560, 3072, 4096, 4224, 4352
NAT_DT = 3072

ADAM_LR = 0.001
ADAM_B1 = 0.9
ADAM_B2 = 0.999
ADAM_EPS = 1e-08
ADAM_WD = 0.01
ADAM_STEP = 10

VMEM_LIMIT = 52 * 1024 * 1024
SMALL_ROWS = 16
NEG = -1e30


def _cparams(sem=None):
    return pltpu.CompilerParams(dimension_semantics=sem, vmem_limit_bytes=VMEM_LIMIT)


def _split3(v):
    hi = v.astype(BF16)
    rest = v - hi.astype(F32)
    mid = rest.astype(BF16)
    return hi, mid, (rest - mid.astype(F32)).astype(BF16)


def _hdot(a, b, data):
    if data == "a":
        sel = b.astype(BF16)
        return sum(_dot_nn(part, sel) for part in _split3(a))
    sel = a.astype(BF16)
    return sum(_dot_nn(sel, part) for part in _split3(b))


def _dot_nn(a, b):
    return lax.dot_general(a, b, (((1,), (0,)), ((), ())), preferred_element_type=F32)


def _dot_nt(a, b):
    return lax.dot_general(a, b, (((1,), (1,)), ((), ())), preferred_element_type=F32)


def _dot_tn(a, b):
    return lax.dot_general(a, b, (((0,), (0,)), ((), ())), preferred_element_type=F32)


def _softplus(v):
    return jnp.maximum(v, 0.0) + jnp.log1p(jnp.exp(-jnp.abs(v)))


def _sigmoid(v):
    return 1.0 / (1.0 + jnp.exp(-v))


def _matmul(a, b, *, mode, grid, a_spec, b_spec, out_shapes, out_specs, tile, name,
            extras=(), extra_specs=(), epilogue=None, after=None, dot_fn=None, prefetch=None):
    nk = grid[2]
    n_ex = len(extras)
    n_out = len(out_shapes)
    bs, b_specs = (b, b_spec) if isinstance(b, tuple) else ((b,), (b_spec,))
    n_in = 1 + len(bs)
    dot = dot_fn if dot_fn is not None else {"nn": _dot_nn, "nt": _dot_nt, "tn": _dot_tn}[mode]

    def finish(acc, ex_refs, out_refs):
        res = (acc,) if epilogue is None else epilogue(acc, *[e[...] for e in ex_refs])
        for o, r in zip(out_refs, res):
            o[...] = r.astype(o.dtype)

    def body(*refs):
        ex_refs = refs[n_in:n_in + n_ex]
        out_refs = refs[n_in + n_ex:n_in + n_ex + n_out]
        part = dot(*[r[...].astype(BF16) for r in refs[:n_in]])
        if nk == 1:
            finish(part, ex_refs, out_refs)
        else:
            acc_ref = refs[-1]
            k = pl.program_id(2)

            @pl.when(k == 0)
            def _():
                acc_ref[...] = part

            @pl.when(k > 0)
            def _():
                acc_ref[...] += part

            @pl.when(k == nk - 1)
            def _():
                finish(acc_ref[...], ex_refs, out_refs)

    scratch = [] if nk == 1 else [pltpu.VMEM(tile, F32)]
    n_pre = 0 if prefetch is None else 1
    tok_specs = [] if after is None else [pl.BlockSpec((8, 128), lambda *_: (0, 0))]
    tok_args = [] if after is None else [after]

    def body_with_token(*refs):
        refs = refs[n_pre:]
        body(*refs[:n_in + n_ex], *refs[n_in + n_ex + len(tok_args):])

    in_specs = [a_spec, *b_specs, *extra_specs, *tok_specs]
    params = _cparams(("parallel", "parallel", "arbitrary"))
    if prefetch is None:
        return pl.pallas_call(
            body_with_token, grid=grid, in_specs=in_specs, out_specs=list(out_specs), out_shape=list(out_shapes),
            scratch_shapes=scratch, name=name, compiler_params=params)(a, *bs, *extras, *tok_args)
    return pl.pallas_call(
        body_with_token,
        grid_spec=pltpu.PrefetchScalarGridSpec(num_scalar_prefetch=1, grid=grid, in_specs=in_specs,
                                               out_specs=list(out_specs), scratch_shapes=scratch),
        out_shape=list(out_shapes), name=name, compiler_params=params)(prefetch, a, *bs, *extras, *tok_args)


def _mm_simple(a, b, *, mode, M, N, K, tm, tn, tk, out_dtype, name, extras=(), epilogue=None, n_out=1,
               out_dtypes=None, after=None):
    grid = (M // tm, N // tn, K // tk)
    if mode == "nn":
        a_spec = pl.BlockSpec((tm, tk), lambda i, j, k: (i, k))
        b_spec = pl.BlockSpec((tk, tn), lambda i, j, k: (k, j))
    elif mode == "nt":
        a_spec = pl.BlockSpec((tm, tk), lambda i, j, k: (i, k))
        b_spec = pl.BlockSpec((tn, tk), lambda i, j, k: (j, k))
    else:
        a_spec = pl.BlockSpec((tk, tm), lambda i, j, k: (k, i))
        b_spec = pl.BlockSpec((tk, tn), lambda i, j, k: (k, j))
    o_spec = pl.BlockSpec((tm, tn), lambda i, j, k: (i, j))
    dts = out_dtypes if out_dtypes is not None else [out_dtype] * n_out
    return _matmul(a, b, mode=mode, grid=grid, a_spec=a_spec, b_spec=b_spec,
                   out_shapes=[jax.ShapeDtypeStruct((M, N), d) for d in dts],
                   out_specs=[o_spec] * len(dts), tile=(tm, tn), name=name,
                   extras=extras, extra_specs=[o_spec] * len(extras), epilogue=epilogue, after=after)


ROW_BLOCK = 256


def _rmsnorm_fwd(x, g, name):
    T, D = x.shape

    def body(x_ref, g_ref, o_ref):
        xf = x_ref[...]
        r = lax.rsqrt(jnp.mean(xf * xf, axis=-1, keepdims=True) + EPS)
        o_ref[...] = (xf * r * g_ref[...]).astype(BF16)

    return pl.pallas_call(
        body, grid=(T // ROW_BLOCK,),
        in_specs=[pl.BlockSpec((ROW_BLOCK, D), lambda i: (i, 0)), pl.BlockSpec((1, D), lambda i: (0, 0))],
        out_specs=pl.BlockSpec((ROW_BLOCK, D), lambda i: (i, 0)),
        out_shape=jax.ShapeDtypeStruct((T, D), BF16), name=name, compiler_params=_cparams(("parallel",)),
    )(x, g)


def _rmsnorm_bwd(dh, x, g, dres, name, with_bf16=True):
    T, D = x.shape

    n = T // ROW_BLOCK
    SLOTS = 3

    def body(dh_ref, x_ref, g_ref, dres_ref, dx_ref, *rest):
        buf, sems = rest[-2:]
        dg_ref = rest[-3]
        i = pl.program_id(0)

        def fetch(step, slot):
            rows = pl.ds(pl.multiple_of(step * ROW_BLOCK, ROW_BLOCK), ROW_BLOCK)
            return [pltpu.make_async_copy(src.at[rows], buf.at[slot, k], sems.at[slot, k])
                    for k, src in enumerate((dh_ref, x_ref, dres_ref))]

        @pl.when(i == 0)
        def _():
            dg_ref[...] = jnp.zeros_like(dg_ref)
            for s in range(min(SLOTS - 1, n)):
                for c in fetch(s, s):
                    c.start()

        @pl.when(i + SLOTS - 1 < n)
        def _():
            for c in fetch(i + SLOTS - 1, (i + SLOTS - 1) % SLOTS):
                c.start()

        slot = i % SLOTS
        for c in fetch(i, slot):
            c.wait()
        xf = buf[slot, 1]
        r = lax.rsqrt(jnp.mean(xf * xf, axis=-1, keepdims=True) + EPS)
        xh = xf * r
        d = buf[slot, 0]
        dg_ref[...] += jnp.sum(d * xh, axis=0, keepdims=True)
        dxh = d * g_ref[...]
        dx = r * (dxh - xh * jnp.mean(dxh * xh, axis=-1, keepdims=True)) + buf[slot, 2]
        dx_ref[...] = dx
        if with_bf16:
            rest[0][...] = dx.astype(BF16)

    row = pl.BlockSpec((ROW_BLOCK, D), lambda i: (i, 0))
    vec = pl.BlockSpec((1, D), lambda i: (0, 0))
    copies = [(row, jax.ShapeDtypeStruct((T, D), BF16))] if with_bf16 else []
    return pl.pallas_call(
        body, grid=(n,), in_specs=[ANY, ANY, vec, ANY],
        out_specs=[row, *[c[0] for c in copies], vec],
        out_shape=[jax.ShapeDtypeStruct((T, D), F32), *[c[1] for c in copies], jax.ShapeDtypeStruct((1, D), F32)],
        scratch_shapes=[pltpu.VMEM((SLOTS, 3, ROW_BLOCK, D), F32), pltpu.SemaphoreType.DMA((SLOTS, 3))],
        name=name, compiler_params=_cparams(("arbitrary",)),
    )(dh, x, g, dres)


def _final_loss(x3_halves, tgt, g, name):
    T, D = tgt.shape

    def body(xa_ref, xb_ref, t_ref, g_ref, loss_ref, dg_ref, dx_ref, dxb_ref):
        i = pl.program_id(0)
        xf = jnp.concatenate([xa_ref[...], xb_ref[...]], axis=1)
        r = lax.rsqrt(jnp.mean(xf * xf, axis=-1, keepdims=True) + EPS)
        xh = xf * r
        gg = g_ref[...]
        err = xh * gg - t_ref[...]

        @pl.when(i == 0)
        def _():
            dg_ref[...] = jnp.zeros_like(dg_ref)
            loss_ref[...] = jnp.zeros_like(loss_ref)

        part = jnp.sum(jnp.sum(err * err, axis=-1, keepdims=True), axis=0, keepdims=True) * (0.5 / D)
        loss_ref[...] += jnp.broadcast_to(part, loss_ref.shape)
        dout = err * (1.0 / D)
        dg_ref[...] += jnp.sum(dout * xh, axis=0, keepdims=True)
        dxh = dout * gg
        dx = r * (dxh - xh * jnp.mean(dxh * xh, axis=-1, keepdims=True))
        dx_ref[...] = dx
        dxb_ref[...] = dx.astype(BF16)

    row = pl.BlockSpec((ROW_BLOCK, D), lambda i: (i, 0))
    vec = pl.BlockSpec((1, D), lambda i: (0, 0))
    return pl.pallas_call(
        body, grid=(T // ROW_BLOCK,),
        in_specs=[pl.BlockSpec((ROW_BLOCK, D // 2), lambda i: (i, 0))] * 2 + [row, vec],
        out_specs=[pl.BlockSpec((1, 128), lambda i: (0, 0)), vec, row, row],
        out_shape=[jax.ShapeDtypeStruct((1, 128), F32), jax.ShapeDtypeStruct((1, D), F32),
                   jax.ShapeDtypeStruct((T, D), F32), jax.ShapeDtypeStruct((T, D), BF16)],
        name=name, compiler_params=_cparams(("arbitrary",)),
    )(*x3_halves, tgt, g)


CONV_BLOCK = 256


def _conv_apply(u, w, b):
    row = lax.broadcasted_iota(jnp.int32, u.shape, 0)
    acc = b + w[CONV_K - 1:CONV_K, :] * u
    shifted = []
    for j in range(1, CONV_K):
        uj = jnp.where(row >= j, pltpu.roll(u, j, axis=0), 0.0)
        shifted.append(uj)
        acc = acc + w[CONV_K - 1 - j:CONV_K - j, :] * uj
    return acc, shifted


def _conv_fwd(proj, conv_w, conv_b, name):
    T = proj.shape[0]
    cb0 = OFF_X // CONV_BLOCK

    def body(u_ref, w_ref, b_ref, o_ref, ds_ref):
        c, _ = _conv_apply(u_ref[...], w_ref[...], b_ref[...])
        sg = _sigmoid(c)
        o_ref[...] = c * sg
        ds_ref[...] = sg * (1.0 + c * (1.0 - sg))

    out = pl.BlockSpec((T, CONV_BLOCK), lambda j: (0, j))
    return pl.pallas_call(
        body, grid=(CONV_DIM // CONV_BLOCK,),
        in_specs=[pl.BlockSpec((T, CONV_BLOCK), lambda j: (0, cb0 + j)),
                  pl.BlockSpec((CONV_K, CONV_BLOCK), lambda j: (0, j)),
                  pl.BlockSpec((1, CONV_BLOCK), lambda j: (0, j))],
        out_specs=[out, out], out_shape=[jax.ShapeDtypeStruct((T, CONV_DIM), F32)] * 2,
        name=name, compiler_params=_cparams(("parallel",)),
    )(proj, conv_w, conv_b)


def _conv_bwd(proj, dact, dsilu, conv_w, dproj, name):
    T = proj.shape[0]
    cb0 = OFF_X // CONV_BLOCK

    def body(u_ref, d_ref, s_ref, w_ref, _, du_ref, dw_ref, db_ref):
        u = u_ref[...]
        w = w_ref[...]
        dc = d_ref[...] * s_ref[...]
        row = lax.broadcasted_iota(jnp.int32, u.shape, 0)
        du = w[CONV_K - 1:CONV_K, :] * dc
        dw_ref[CONV_K - 1:CONV_K, :] = jnp.sum(dc * u, axis=0, keepdims=True)
        for j in range(1, CONV_K):
            dcj = jnp.where(row < T - j, pltpu.roll(dc, T - j, axis=0), 0.0)
            du = du + w[CONV_K - 1 - j:CONV_K - j, :] * dcj
            dw_ref[CONV_K - 1 - j:CONV_K - j, :] = jnp.sum(dcj * u, axis=0, keepdims=True)
        db_ref[...] = jnp.sum(dc, axis=0, keepdims=True)
        du_ref[...] = du.astype(BF16)

    blk = pl.BlockSpec((T, CONV_BLOCK), lambda j: (0, j))
    return pl.pallas_call(
        body, grid=(CONV_DIM // CONV_BLOCK,),
        in_specs=[pl.BlockSpec((T, CONV_BLOCK), lambda j: (0, cb0 + j)), blk, blk,
                  pl.BlockSpec((CONV_K, CONV_BLOCK), lambda j: (0, j)), pl.BlockSpec(memory_space=pl.ANY)],
        out_specs=[pl.BlockSpec((T, CONV_BLOCK), lambda j: (0, cb0 + j)),
                   pl.BlockSpec((CONV_K, CONV_BLOCK), lambda j: (0, j)),
                   pl.BlockSpec((1, CONV_BLOCK), lambda j: (0, j))],
        out_shape=[jax.ShapeDtypeStruct(dproj.shape, BF16), jax.ShapeDtypeStruct((CONV_K, CONV_DIM), F32),
                   jax.ShapeDtypeStruct((1, CONV_DIM), F32)],
        input_output_aliases={4: 0}, name=name, compiler_params=_cparams(("parallel",)),
    )(proj, dact, dsilu, conv_w, dproj)


GROUP_W = D_INNER // N_GROUPS
HEADS_PER_GROUP = N_HEADS // N_GROUPS


def _expand_mat():
    h = lax.broadcasted_iota(jnp.int32, (N_HEADS, D_INNER), 0)
    j = lax.broadcasted_iota(jnp.int32, (N_HEADS, D_INNER), 1)
    return (j // HEAD_DIM == h).astype(F32)


def _reduce_mat(g):
    j = lax.broadcasted_iota(jnp.int32, (GROUP_W, N_HEADS), 0)
    h = lax.broadcasted_iota(jnp.int32, (GROUP_W, N_HEADS), 1)
    return (g * HEADS_PER_GROUP + j // HEAD_DIM == h).astype(F32)


def _col16(v, h):
    lane = lax.broadcasted_iota(jnp.int32, v.shape, 1)
    return jnp.sum(jnp.where(lane == h, v, 0.0), axis=1, keepdims=True)


def _ssd_pre(dt_raw, dtT_raw, dtb, dtbT, alog, alogT):
    Q = CHUNK
    xdt = dt_raw + dtb
    dt = _softplus(xdt)
    dtT = _softplus(dtT_raw + dtbT)
    A = -jnp.exp(alog)
    AT = -jnp.exp(alogT)
    row = lax.broadcasted_iota(jnp.int32, (Q, Q), 0)
    col = lax.broadcasted_iota(jnp.int32, (Q, Q), 1)
    tril = (row >= col).astype(F32)
    triu = (row <= col).astype(F32)
    cs = _hdot(tril, dt * A, "b")
    csT = _hdot(dtT * AT, triu, "a")
    return xdt, dt, A, cs, csT, row >= col, triu


def _decay_matrix(cs, csT, h, causal):
    seg = _col16(cs, h) - csT[h:h + 1, :]
    return jnp.where(causal, jnp.exp(jnp.minimum(seg, 0.0)), 0.0)


def _ssd_in_specs(nc, rev):
    def cidx(c):
        return (nc - 1 - c) if rev else c

    return [
        pl.BlockSpec((CHUNK, D_INNER), lambda c: (cidx(c), 0)),
        pl.BlockSpec((CHUNK, 512), lambda c: (cidx(c), 2)),
        pl.BlockSpec((CHUNK, 512), lambda c: (cidx(c), 3)),
        pl.BlockSpec((CHUNK, D_INNER), lambda c: (cidx(c), 0)),
        pl.BlockSpec((CHUNK, 128), lambda c: (cidx(c), OFF_DT // 128)),
        pl.BlockSpec((N_HEADS, CHUNK), lambda c: (0, cidx(c))),
        pl.BlockSpec((1, N_HEADS), lambda c: (0, 0)),
        pl.BlockSpec((N_HEADS, 1), lambda c: (0, 0)),
        pl.BlockSpec((1, N_HEADS), lambda c: (0, 0)),
        pl.BlockSpec((N_HEADS, 1), lambda c: (0, 0)),
        pl.BlockSpec((1, D_INNER), lambda c: (0, 0)),
        pl.BlockSpec((1, D_INNER), lambda c: (0, 0)),
    ]


def _ssd_fwd(xbc, proj, dtT, dtb, dtbT, alog, alogT, dfull, ng, name):
    T = xbc.shape[0]
    nc = T // CHUNK
    Q = CHUNK

    def body(xs_ref, B_ref, C_ref, z_ref, dt_ref, dtT_ref, dtb_ref, dtbT_ref, al_ref, alT_ref, df_ref, ng_ref,
             y_ref, ypre_ref, hs_ref, h_scr):
        c = pl.program_id(0)

        @pl.when(c == 0)
        def _():
            h_scr[...] = jnp.zeros_like(h_scr)

        _, dt, _, cs, csT, causal, _ = _ssd_pre(dt_ref[:, :N_HEADS], dtT_ref[...], dtb_ref[...], dtbT_ref[...],
                                                al_ref[...], alT_ref[...])
        ex = _expand_mat()
        dt_full = _hdot(dt, ex, "a")
        cs_full = _hdot(cs, ex, "a")
        cs_last = cs_full[Q - 1:Q, :]
        xs = xs_ref[...]
        xd = xs * dt_full
        e_full = jnp.exp(cs_full)
        dec_full = jnp.exp(cs_last - cs_full)
        cd_full = jnp.exp(cs_last)
        lane_head = lax.broadcasted_iota(jnp.int32, (1, GROUP_W), 1) // HEAD_DIM
        for g in range(N_GROUPS):
            sl = slice(g * GROUP_W, (g + 1) * GROUP_W)
            Bg = B_ref[:, g * D_STATE:(g + 1) * D_STATE].astype(BF16)
            Cg = C_ref[:, g * D_STATE:(g + 1) * D_STATE].astype(BF16)
            CB = _dot_nt(Cg, Bg)
            hg = h_scr[g]
            yoff = _dot_nn(Cg, hg.astype(BF16)) * e_full[:, sl]
            xd_g = xd[:, sl]
            S = _dot_tn(Bg, (xd_g * dec_full[:, sl]).astype(BF16))
            xd_b = xd_g.astype(BF16)
            ydiag = jnp.zeros((Q, GROUP_W), F32)
            for r in range(HEADS_PER_GROUP):
                Lm = _decay_matrix(cs, csT, g * HEADS_PER_GROUP + r, causal)
                Gm = (CB * Lm).astype(BF16)
                ydiag = ydiag + _dot_nn(Gm, jnp.where(lane_head == r, xd_b, jnp.zeros_like(xd_b)))
            hs_ref[0, g] = hg
            h_scr[g] = hg * cd_full[:, sl] + S
            ypre = ydiag + yoff + xs[:, sl] * df_ref[:, sl]
            ypre_ref[:, sl] = ypre
            zg = z_ref[:, sl]
            yz = ypre * zg * _sigmoid(zg)
            rn = lax.rsqrt(jnp.mean(yz * yz, axis=-1, keepdims=True) + EPS)
            y_ref[:, sl] = (yz * rn * ng_ref[:, sl]).astype(BF16)

    return pl.pallas_call(
        body, grid=(nc,), in_specs=_ssd_in_specs(nc, False),
        out_specs=[pl.BlockSpec((CHUNK, D_INNER), lambda c: (c, 0)),
                   pl.BlockSpec((CHUNK, D_INNER), lambda c: (c, 0)),
                   pl.BlockSpec((1, N_GROUPS, D_STATE, GROUP_W), lambda c: (c, 0, 0, 0))],
        out_shape=[jax.ShapeDtypeStruct((T, D_INNER + ATTN_W), BF16), jax.ShapeDtypeStruct((T, D_INNER), F32),
                   jax.ShapeDtypeStruct((nc, N_GROUPS, D_STATE, GROUP_W), F32)],
        scratch_shapes=[pltpu.VMEM((N_GROUPS, D_STATE, GROUP_W), F32)],
        name=name, compiler_params=_cparams(("arbitrary",)),
    )(xbc, xbc, xbc, proj, proj, dtT, dtb, dtbT, alog, alogT, dfull, ng)


def _ssd_bwd(xbc, proj, dtT, dtb, dtbT, alog, alogT, dfull, ng, ypre, hs, dy, name):
    T = xbc.shape[0]
    nc = T // CHUNK
    Q = CHUNK

    def body(xs_ref, B_ref, C_ref, z_ref, dt_ref, dtT_ref, dtb_ref, dtbT_ref, al_ref, alT_ref, df_ref, ng_ref,
             ypre_ref, hs_ref, dy_ref,
             dz_ref, dxbc_ref, ddtb_ref, dal_ref, dD_ref, dng_ref, dh_scr):
        step = pl.program_id(0)

        @pl.when(step == 0)
        def _():
            dh_scr[...] = jnp.zeros_like(dh_scr)
            ddtb_ref[...] = jnp.zeros_like(ddtb_ref)
            dal_ref[...] = jnp.zeros_like(dal_ref)
            dD_ref[...] = jnp.zeros_like(dD_ref)
            dng_ref[...] = jnp.zeros_like(dng_ref)

        xdt, dt, A, cs, csT, causal, triu = _ssd_pre(dt_ref[:, :N_HEADS], dtT_ref[...], dtb_ref[...],
                                                    dtbT_ref[...], al_ref[...], alT_ref[...])
        ex = _expand_mat()
        dt_full = _hdot(dt, ex, "a")
        cs_full = _hdot(cs, ex, "a")
        cs_last = cs_full[Q - 1:Q, :]
        xs = xs_ref[...]
        xd = xs * dt_full
        e_full = jnp.exp(cs_full)
        dec_full = jnp.exp(cs_last - cs_full)
        cd_full = jnp.exp(cs_last)
        lane_head = lax.broadcasted_iota(jnp.int32, (1, GROUP_W), 1) // HEAD_DIM
        is_last = lax.broadcasted_iota(jnp.int32, (Q, 1), 0) == Q - 1
        dcs16 = jnp.zeros((Q, N_HEADS), F32)
        ddtx16 = jnp.zeros((Q, N_HEADS), F32)
        dD16 = jnp.zeros((8, N_HEADS), F32)
        lane16 = lax.broadcasted_iota(jnp.int32, (1, N_HEADS), 1)
        sub16 = lax.broadcasted_iota(jnp.int32, (N_HEADS, 1), 0)
        col_sums = jnp.zeros((N_HEADS, Q), F32)
        for g in range(N_GROUPS):
            sl = slice(g * GROUP_W, (g + 1) * GROUP_W)
            red = _reduce_mat(g)
            ypre_g = ypre_ref[:, sl]
            zg = z_ref[:, sl]
            sg = _sigmoid(zg)
            silu = zg * sg
            yz = ypre_g * silu
            rn = lax.rsqrt(jnp.mean(yz * yz, axis=-1, keepdims=True) + EPS)
            yh = yz * rn
            dy_g = dy_ref[:, sl]
            dng_ref[:, sl] += jnp.sum(dy_g * yh, axis=0, keepdims=True)
            dyh = dy_g * ng_ref[:, sl]
            dyz = rn * (dyh - yh * jnp.mean(dyh * yh, axis=-1, keepdims=True))
            dY = dyz * silu
            dz_ref[:, sl] = (dyz * ypre_g * sg * (1.0 + zg * (1.0 - sg))).astype(BF16)
            xs_g = xs[:, sl]
            xd_g = xd[:, sl]
            dec_g = dec_full[:, sl]
            cd_g = cd_full[:, sl]
            d_g = df_ref[:, sl]
            Bg = B_ref[:, g * D_STATE:(g + 1) * D_STATE].astype(BF16)
            Cg = C_ref[:, g * D_STATE:(g + 1) * D_STATE].astype(BF16)
            CB = _dot_nt(Cg, Bg)
            hg = hs_ref[0, g]
            hgb = hg.astype(BF16)
            yoff = _dot_nn(Cg, hgb) * e_full[:, sl]
            dhn = dh_scr[g]
            dhnb = dhn.astype(BF16)
            dYE = (dY * e_full[:, sl]).astype(BF16)
            dC = _dot_nt(dYE, hgb)
            dh_direct = _dot_tn(Cg, dYE)
            dXdd = _dot_nn(Bg, dhnb)
            dB = _dot_nt((xd_g * dec_g).astype(BF16), dhnb)
            dcd = jnp.sum(dhn * hg, axis=0, keepdims=True)
            dh_scr[g] = dh_direct + cd_g * dhn
            dYb = dY.astype(BF16)
            xd_b = xd_g.astype(BF16)
            dCB = jnp.zeros((Q, Q), F32)
            dXd = dXdd * dec_g
            for r in range(HEADS_PER_GROUP):
                h = g * HEADS_PER_GROUP + r
                Lm = _decay_matrix(cs, csT, h, causal)
                Gf = CB * Lm
                dYr = jnp.where(lane_head == r, dYb, jnp.zeros_like(dYb))
                dG = _dot_nt(dYr, xd_b)
                dCB = dCB + dG * Lm
                dXd = dXd + _dot_tn(Gf.astype(BF16), dYr)
                Mm = dG * Gf
                dcs16 = dcs16 + jnp.where(lane16 == h, jnp.sum(Mm, axis=1, keepdims=True), 0.0)
                col_sums = col_sums + jnp.where(sub16 == h, jnp.sum(Mm, axis=0, keepdims=True), 0.0)
            dCBb = dCB.astype(BF16)
            dC = dC + _dot_nn(dCBb, Bg)
            dB = dB + _dot_tn(dCBb, Cg)
            w_state = dXdd * dec_g * xd_g
            t_last = jnp.sum(w_state, axis=0, keepdims=True) + dcd * cd_g
            dcs_g = dY * yoff - w_state + jnp.where(is_last, t_last, 0.0)
            dcs16 = dcs16 + _hdot(dcs_g, red, "a")
            ddtx16 = ddtx16 + _hdot(dXd * xs_g, red, "a")
            dD16 = dD16 + _hdot(jnp.broadcast_to(jnp.sum(dY * xs_g, axis=0, keepdims=True), (8, GROUP_W)), red, "a")
            dxbc_ref[:, sl] = dXd * dt_full[:, sl] + dY * d_g
            dxbc_ref[:, D_INNER + g * D_STATE:D_INNER + (g + 1) * D_STATE] = dB
            dxbc_ref[:, D_INNER + 512 + g * D_STATE:D_INNER + 512 + (g + 1) * D_STATE] = dC
        eye = (lax.broadcasted_iota(jnp.int32, (N_HEADS, N_HEADS), 0)
               == lax.broadcasted_iota(jnp.int32, (N_HEADS, N_HEADS), 1)).astype(BF16)
        dcs16 = dcs16 - sum(_dot_tn(part, eye) for part in _split3(col_sums))
        da = _hdot(triu, dcs16, "b")
        ddt = da * A + ddtx16
        ddt_raw = ddt * _sigmoid(xdt)
        pr = lax.broadcasted_iota(jnp.int32, (N_HEADS, 128), 0)
        pc = lax.broadcasted_iota(jnp.int32, (N_HEADS, 128), 1)
        dz_ref[:, D_INNER:OFF_DT] = jnp.zeros((Q, OFF_DT - D_INNER), BF16)
        dz_ref[:, OFF_DT:OFF_DT + 128] = _hdot(ddt_raw, (pr == pc).astype(F32), "a").astype(BF16)
        dz_ref[:, OFF_DT + 128:] = jnp.zeros((Q, NP - OFF_DT - 128), BF16)
        ddtb_ref[...] += jnp.sum(ddt_raw, axis=0, keepdims=True)
        dal_ref[...] += jnp.sum(da * dt, axis=0, keepdims=True) * A
        dD_ref[...] += dD16[0:1, :]

    def rc(c):
        return nc - 1 - c

    in_specs = _ssd_in_specs(nc, True) + [
        pl.BlockSpec((CHUNK, D_INNER), lambda c: (rc(c), 0)),
        pl.BlockSpec((1, N_GROUPS, D_STATE, GROUP_W), lambda c: (rc(c), 0, 0, 0)),
        pl.BlockSpec((CHUNK, D_INNER), lambda c: (rc(c), 0)),
    ]
    small = pl.BlockSpec((1, N_HEADS), lambda c: (0, 0))
    return pl.pallas_call(
        body, grid=(nc,), in_specs=in_specs,
        out_specs=[pl.BlockSpec((CHUNK, NP), lambda c: (rc(c), 0)),
                   pl.BlockSpec((CHUNK, CONV_DIM), lambda c: (rc(c), 0)),
                   small, small, small,
                   pl.BlockSpec((1, D_INNER), lambda c: (0, 0))],
        out_shape=[jax.ShapeDtypeStruct((T, NP), BF16), jax.ShapeDtypeStruct((T, CONV_DIM), F32),
                   jax.ShapeDtypeStruct((1, N_HEADS), F32), jax.ShapeDtypeStruct((1, N_HEADS), F32),
                   jax.ShapeDtypeStruct((1, N_HEADS), F32), jax.ShapeDtypeStruct((1, D_INNER), F32)],
        scratch_shapes=[pltpu.VMEM((N_GROUPS, D_STATE, GROUP_W), F32)],
        name=name, compiler_params=_cparams(("arbitrary",)),
    )(xbc, xbc, xbc, proj, proj, dtT, dtb, dtbT, alog, alogT, dfull, ng, ypre, hs, dy)


N_PAIRS = ATTN_W // 128
PAIRS_PER_KV = N_PAIRS // 2
ATTN_SCALE = HEAD_DIM ** -0.5


def _kv_variants(kk):
    lo = lax.broadcasted_iota(jnp.int32, kk.shape, 1) < HEAD_DIM
    zero = jnp.zeros_like(kk)
    k00 = jnp.where(lo, kk, zero)
    k11 = jnp.where(lo, zero, kk)
    k01 = pltpu.roll(k00, HEAD_DIM, axis=1)
    k10 = pltpu.roll(k11, HEAD_DIM, axis=1)
    return [[k00.astype(BF16), k01.astype(BF16)], [k10.astype(BF16), k11.astype(BF16)]]


LOG2E = 1.4426950408889634


def _own_block():
    i = lax.broadcasted_iota(jnp.int32, (WINDOW, WINDOW), 0)
    j = lax.broadcasted_iota(jnp.int32, (WINDOW, WINDOW), 1)
    return j <= i


def _fold(own, a):
    return jnp.where(own, a[:, WINDOW:], a[:, :WINDOW])


def _attn_probs(qp, kvar, own, prev_bias, sk):
    s = _dot_nt(qp, kvar)
    sb = jnp.where(own, s[:, WINDOW:], s[:, :WINDOW] + prev_bias) * (ATTN_SCALE * LOG2E)
    sk2 = sk * LOG2E
    m = jnp.maximum(jnp.max(sb, axis=1, keepdims=True), sk2)
    pe = jnp.exp2(sb - m)
    es = jnp.exp2(sk2 - m)
    den = jnp.sum(pe, axis=1, keepdims=True) + es
    inv = 1.0 / den
    return pe * inv, es * inv


def _unfold(own, a):
    zero = jnp.zeros_like(a)
    return jnp.where(own, zero, a), jnp.where(own, a, zero)


def _sink(sinks, r):
    lane = lax.broadcasted_iota(jnp.int32, sinks.shape, 1)
    return jnp.sum(jnp.where(lane == r, sinks, 0.0), axis=1, keepdims=True)


def _kv_specs():
    return [pl.BlockSpec((WINDOW, KV_W), lambda n: (jnp.maximum(n - 1, 0), OFF_K // KV_W)),
            pl.BlockSpec((WINDOW, KV_W), lambda n: (n, OFF_K // KV_W)),
            pl.BlockSpec((WINDOW, KV_W), lambda n: (jnp.maximum(n - 1, 0), OFF_V // KV_W)),
            pl.BlockSpec((WINDOW, KV_W), lambda n: (n, OFF_V // KV_W))]


def _attn_fwd(proj, sinks, og, ycat, name):
    T = proj.shape[0]
    nb = T // WINDOW

    def body(q_ref, kp_ref, kc_ref, vp_ref, vc_ref, s_ref, og_ref, _, y_ref, o_ref, p_ref, ps_ref):
        n = pl.program_id(0)
        kv = _kv_variants(jnp.concatenate([kp_ref[...], kc_ref[...]], axis=0))
        vv = _kv_variants(jnp.concatenate([vp_ref[...], vc_ref[...]], axis=0))
        own = _own_block()
        prev_bias = jnp.where(n > 0, 0.0, NEG)
        sinks_v = s_ref[...]
        lane = lax.broadcasted_iota(jnp.int32, (1, 128), 1)
        ssq = jnp.zeros((WINDOW, 1), F32)
        sink_probs = jnp.zeros((WINDOW, 128), F32)
        for p in range(N_PAIRS):
            j = p // PAIRS_PER_KV
            qp = q_ref[:, p * 128:(p + 1) * 128].astype(BF16)
            o_pair = jnp.zeros((WINDOW, 128), F32)
            for par in range(2):
                r = 2 * p + par
                pn, ps = _attn_probs(qp, kv[j][par], own, prev_bias, _sink(sinks_v, r))
                pb = pn.astype(BF16)
                p_ref[:, r * 128:(r + 1) * 128] = pb
                sink_probs = jnp.where(lane == r, ps, sink_probs)
                p_prev, p_own = _unfold(own, pb)
                o_pair = o_pair + _dot_nn(p_prev, vv[j][par][:WINDOW]) + _dot_nn(p_own, vv[j][par][WINDOW:])
            o_ref[:, p * 128:(p + 1) * 128] = o_pair
            ssq = ssq + jnp.sum(o_pair * o_pair, axis=1, keepdims=True)
        ps_ref[...] = sink_probs
        rn = lax.rsqrt(ssq * (1.0 / ATTN_W) + EPS)
        y_ref[...] = (o_ref[...] * rn * og_ref[...]).astype(BF16)

    return pl.pallas_call(
        body, grid=(nb,),
        in_specs=[pl.BlockSpec((WINDOW, ATTN_W), lambda n: (n, OFF_Q // ATTN_W)), *_kv_specs(),
                  pl.BlockSpec((1, N_HEADS), lambda n: (0, 0)), pl.BlockSpec((1, ATTN_W), lambda n: (0, 0)), ANY],
        out_specs=[pl.BlockSpec((WINDOW, ATTN_W), lambda n: (n, 1)), pl.BlockSpec((WINDOW, ATTN_W), lambda n: (n, 0)),
                   pl.BlockSpec((WINDOW, N_HEADS * 128), lambda n: (n, 0)), pl.BlockSpec((WINDOW, 128), lambda n: (n, 0))],
        out_shape=[jax.ShapeDtypeStruct(ycat.shape, BF16), jax.ShapeDtypeStruct((T, ATTN_W), F32),
                   jax.ShapeDtypeStruct((T, N_HEADS * 128), BF16), jax.ShapeDtypeStruct((T, 128), F32)],
        input_output_aliases={7: 0}, name=name, compiler_params=_cparams(("parallel",)),
    )(proj, proj, proj, proj, proj, sinks, og, ycat)


def _attn_bwd(proj, og, o, dy, probs, sink_probs, dproj, name):
    T = proj.shape[0]
    nb = T // WINDOW

    def body(q_ref, kp_ref, kc_ref, vp_ref, vc_ref, og_ref, o_ref, dy_ref, p_ref, ps_ref, _,
             dq_ref, dk_ref, dv_ref, ds_ref, dog_ref, qt_scr, dot_scr, ds_scr, p_scr):
        n = pl.program_id(0)

        @pl.when(n == 0)
        def _():
            dk_ref[...] = jnp.zeros_like(dk_ref)
            dv_ref[...] = jnp.zeros_like(dv_ref)
            ds_ref[...] = jnp.zeros_like(ds_ref)
            dog_ref[...] = jnp.zeros_like(dog_ref)

        kv = _kv_variants(jnp.concatenate([kp_ref[...], kc_ref[...]], axis=0))
        vv = _kv_variants(jnp.concatenate([vp_ref[...], vc_ref[...]], axis=0))
        own = _own_block()
        sink_probs_v = ps_ref[...]
        of = o_ref[...]
        rn = lax.rsqrt(jnp.mean(of * of, axis=-1, keepdims=True) + EPS)
        oh = of * rn
        dyf = dy_ref[...]
        dog_ref[...] += jnp.sum(dyf * oh, axis=0, keepdims=True)
        doh = dyf * og_ref[...]
        do = rn * (doh - oh * jnp.mean(doh * oh, axis=-1, keepdims=True))
        lane = lax.broadcasted_iota(jnp.int32, (1, 128), 1)
        lane16 = lax.broadcasted_iota(jnp.int32, (1, N_HEADS), 1)
        dsink = jnp.zeros((1, N_HEADS), F32)
        for p in range(N_PAIRS):
            j = p // PAIRS_PER_KV
            q_t = q_ref[:, p * 128:(p + 1) * 128].T.astype(BF16)
            do_p = do[:, p * 128:(p + 1) * 128]
            o_p = of[:, p * 128:(p + 1) * 128]
            do_b = do_p.astype(BF16)
            do_t = do_p.T.astype(BF16)
            prod = do_p * o_p
            dq_pair = jnp.zeros((WINDOW, 128), F32)
            for par in range(2):
                r = 2 * p + par
                half = (lane < HEAD_DIM) if par == 0 else (lane >= HEAD_DIM)
                pb = p_ref[:, r * 128:(r + 1) * 128]
                ps = jnp.sum(jnp.where(lane == r, sink_probs_v, 0.0), axis=1, keepdims=True)
                delta = jnp.sum(jnp.where(half, prod, 0.0), axis=1, keepdims=True)
                dP = _fold(own, _dot_nt(do_b, vv[j][par]))
                dS = pb.astype(F32) * (dP - delta)
                dsink = dsink + jnp.where(lane16 == r, -jnp.sum(ps * delta, axis=0, keepdims=True), 0.0)
                dS_parts = _unfold(own, (dS * ATTN_SCALE).astype(BF16))
                p_parts = _unfold(own, pb)
                at = ((p % PAIRS_PER_KV) * 2 + par) * WINDOW
                qt_scr[j, :, at:at + WINDOW] = q_t[par * HEAD_DIM:(par + 1) * HEAD_DIM]
                dot_scr[j, :, at:at + WINDOW] = do_t[par * HEAD_DIM:(par + 1) * HEAD_DIM]
                for blk in range(2):
                    dq_pair = dq_pair + _dot_nn(dS_parts[blk], kv[j][par][blk * WINDOW:(blk + 1) * WINDOW])
                    ds_scr[j, blk, at:at + WINDOW, :] = dS_parts[blk]
                    p_scr[j, blk, at:at + WINDOW, :] = p_parts[blk]
            dq_ref[:, p * 128:(p + 1) * 128] = dq_pair.astype(BF16)
        rows = [pl.multiple_of(jnp.maximum(n - 1, 0) * WINDOW, WINDOW), pl.multiple_of(n * WINDOW, WINDOW)]
        for lhs, rhs, ref in [(qt_scr, ds_scr, dk_ref), (dot_scr, p_scr, dv_ref)]:
            for blk in range(2):
                both_t = jnp.concatenate([_dot_nn(lhs[j], rhs[j, blk]) for j in range(2)], axis=0)
                ref[pl.ds(rows[blk], WINDOW), :] += both_t.T
        ds_ref[...] += dsink

    full_kv = pl.BlockSpec((T, KV_W), lambda n: (0, 0))
    blk = pl.BlockSpec((WINDOW, ATTN_W), lambda n: (n, 0))
    return pl.pallas_call(
        body, grid=(nb,),
        in_specs=[pl.BlockSpec((WINDOW, ATTN_W), lambda n: (n, OFF_Q // ATTN_W)), *_kv_specs(),
                  pl.BlockSpec((1, ATTN_W), lambda n: (0, 0)), blk, pl.BlockSpec((WINDOW, ATTN_W), lambda n: (n, 1)),
                  pl.BlockSpec((WINDOW, N_HEADS * 128), lambda n: (n, 0)),
                  pl.BlockSpec((WINDOW, 128), lambda n: (n, 0)), ANY],
        out_specs=[pl.BlockSpec((WINDOW, ATTN_W), lambda n: (n, OFF_Q // ATTN_W)), full_kv, full_kv,
                   pl.BlockSpec((1, N_HEADS), lambda n: (0, 0)), pl.BlockSpec((1, ATTN_W), lambda n: (0, 0))],
        out_shape=[jax.ShapeDtypeStruct(dproj.shape, BF16), jax.ShapeDtypeStruct((T, KV_W), F32),
                   jax.ShapeDtypeStruct((T, KV_W), F32), jax.ShapeDtypeStruct((1, N_HEADS), F32),
                   jax.ShapeDtypeStruct((1, ATTN_W), F32)],
        scratch_shapes=[pltpu.VMEM((2, HEAD_DIM, 8 * WINDOW), BF16), pltpu.VMEM((2, HEAD_DIM, 8 * WINDOW), BF16),
                        pltpu.VMEM((2, 2, 8 * WINDOW, WINDOW), BF16), pltpu.VMEM((2, 2, 8 * WINDOW, WINDOW), BF16)],
        input_output_aliases={10: 0}, name=name, compiler_params=_cparams(("arbitrary",)),
    )(proj, proj, proj, proj, proj, og, o, dy, probs, sink_probs, dproj)


ANY = pl.BlockSpec(memory_space=pl.ANY)


def _coords():
    return lax.axis_index("x"), lax.axis_index("y"), lax.axis_index("c")


HBM = pl.BlockSpec(memory_space=pltpu.HBM)
SEM = pl.BlockSpec(memory_space=pltpu.SEMAPHORE)
EFFECT = pltpu.SideEffectType.DATAFLOW_SIDE_EFFECTING


def _in_hbm(a):
    return pltpu.with_memory_space_constraint(a, pltpu.HBM)


def _remote_start(srcs, lands, plan, n_copies, name, after=None):
    ns, nb = len(srcs), len(srcs) + len(lands)
    n_after = 0 if after is None else 1

    def body(*refs):
        src_refs, land_refs = refs[:ns], refs[ns:nb]
        send_sems, recv_sems = refs[nb + n_after], refs[nb + n_after + 1]
        token = refs[-1]
        x, y, c = _coords()
        for i, (sv, dv, dev) in enumerate(plan(src_refs, land_refs, x, y, c)):
            pltpu.make_async_remote_copy(src_ref=sv, dst_ref=dv, send_sem=send_sems.at[i], recv_sem=recv_sems.at[i],
                                         device_id=dev, device_id_type=MESH).start()
        token[...] = jnp.zeros_like(token)

    bufs = list(srcs) + list(lands)
    outs = pl.pallas_call(
        body, name=name,
        out_shape=(pltpu.SemaphoreType.DMA((n_copies,)), pltpu.SemaphoreType.DMA((n_copies,)),
                   *[pltpu.HBM(b.shape, b.dtype) for b in bufs], jax.ShapeDtypeStruct((8, 128), F32)),
        in_specs=[HBM] * nb + [ANY] * n_after,
        out_specs=(SEM, SEM, *[HBM] * nb, pl.BlockSpec(memory_space=pltpu.VMEM)),
        input_output_aliases={i: 2 + i for i in range(nb)},
        compiler_params=pltpu.CompilerParams(has_side_effects=EFFECT),
    )(*[_in_hbm(b) for b in bufs], *([] if after is None else [after]))
    return outs[0], outs[1], list(outs[2:2 + ns]), list(outs[2 + ns:2 + nb]), outs[-1]


def _remote_wait(started, after, plan, name):
    send_sems, recv_sems, srcs, lands, _ = started
    ns, nb = len(srcs), len(srcs) + len(lands)

    def body(*refs):
        src_refs, land_refs = refs[:ns], refs[ns:nb]
        send_sems, recv_sems = refs[nb], refs[nb + 1]
        x, y, c = _coords()
        for i, (sv, dv, dev) in enumerate(plan(src_refs, land_refs, x, y, c)):
            cp = pltpu.make_async_remote_copy(src_ref=sv, dst_ref=dv, send_sem=send_sems.at[i],
                                              recv_sem=recv_sems.at[i], device_id=dev, device_id_type=MESH)
            cp.wait_send()
            cp.wait_recv()

    bufs = list(srcs) + list(lands)
    outs = pl.pallas_call(
        body, name=name, out_shape=tuple(pltpu.HBM(b.shape, b.dtype) for b in bufs),
        in_specs=[HBM] * nb + [SEM, SEM, ANY], out_specs=tuple([HBM] * nb),
        input_output_aliases={i: i for i in range(nb)},
        compiler_params=pltpu.CompilerParams(has_side_effects=EFFECT),
    )(*bufs, send_sems, recv_sems, after)
    return list(outs[:ns]), list(outs[ns:])


def _pair_plan(src_refs, land_refs, x, y, c):
    plan = []
    for s, l in zip(src_refs, land_refs):
        for q in range(4):
            plan.append((s.at[2 * q + (1 - c)], l.at[q], (x, y, 1 - c)))
    return plan


def _pair4_plan(src_refs, land_refs, x, y, c):
    plan = []
    for s, l in zip(src_refs, land_refs):
        for q in range(4):
            plan.append((s.at[q], l.at[q], (x, y, 1 - c)))
    return plan


def _chips_plan(src_refs, land_refs, x, y, c):
    plan = []
    for s, l in zip(src_refs, land_refs):
        for k, (tx, ty) in enumerate([(1 - x, y), (x, 1 - y), (1 - x, 1 - y)]):
            plan.append((s.at[2 * tx + ty], l.at[k], (tx, ty, c)))
    return plan


def _everyone_plan(src_refs, land_refs, x, y, c):
    me = 4 * x + 2 * y + c
    plan = []
    for s, l in zip(src_refs, land_refs):
        for fx, fy, fc in [(0, 0, 1), (1, 0, 0), (1, 0, 1), (0, 1, 0), (0, 1, 1), (1, 1, 0), (1, 1, 1)]:
            dev = ((1 - x) if fx else x, (1 - y) if fy else y, (1 - c) if fc else c)
            plan.append((s, l.at[me], dev))
    return plan


def _pair_add(g8, r1, csel, tr, name):
    _, R, C = r1.shape
    g4 = g8.reshape(4, 2, R, C)

    def body(c_ref, g_ref, r_ref, o_ref):
        o_ref[...] = (g_ref[...].astype(F32) + r_ref[...].astype(F32)).astype(BF16)

    return pl.pallas_call(
        body,
        grid_spec=pltpu.PrefetchScalarGridSpec(
            num_scalar_prefetch=1, grid=(4, R // tr),
            in_specs=[pl.BlockSpec((None, None, tr, C), lambda q, i, cs: (q, cs[0], i, 0)),
                      pl.BlockSpec((None, tr, C), lambda q, i, cs: (q, i, 0))],
            out_specs=pl.BlockSpec((None, tr, C), lambda q, i, cs: (q, i, 0))),
        out_shape=jax.ShapeDtypeStruct((4, R, C), BF16), name=name,
        compiler_params=_cparams(("parallel", "parallel")),
    )(csel, g4, r1)


def _adamw_math(w, g, m, v):
    m = ADAM_B1 * m + (1.0 - ADAM_B1) * g
    v = ADAM_B2 * v + (1.0 - ADAM_B2) * (g * g)
    m_hat = m / (1.0 - ADAM_B1 ** ADAM_STEP)
    v_hat = v / (1.0 - ADAM_B2 ** ADAM_STEP)
    delta = -ADAM_LR * (m_hat / (jnp.sqrt(v_hat) + ADAM_EPS) + ADAM_WD * w)
    return delta, m, v


def _adamw_big(w, m, v, p4, r3, qsel, tile, name):
    R, C = w.shape
    tr, tc = tile

    def body(q_ref, w_ref, m_ref, v_ref, p_ref, r_ref, g_out, d_out, m_out, v_out):
        g = p_ref[...].astype(F32) + r_ref[0].astype(F32) + r_ref[1].astype(F32) + r_ref[2].astype(F32)
        d, mn, vn = _adamw_math(w_ref[...], g, m_ref[...], v_ref[...])
        g_out[...] = g
        d_out[...] = d
        m_out[...] = mn
        v_out[...] = vn

    blk = pl.BlockSpec((tr, tc), lambda i, j, qs: (i, j))
    return pl.pallas_call(
        body,
        grid_spec=pltpu.PrefetchScalarGridSpec(
            num_scalar_prefetch=1, grid=(R // tr, C // tc),
            in_specs=[blk, blk, blk, pl.BlockSpec((None, tr, tc), lambda i, j, qs: (qs[0], i, j)),
                      pl.BlockSpec((3, tr, tc), lambda i, j, qs: (0, i, j))],
            out_specs=[blk, blk, blk, blk]),
        out_shape=[jax.ShapeDtypeStruct((R, C), F32)] * 4, name=name,
        compiler_params=_cparams(("parallel", "parallel")),
    )(qsel, w, m, v, p4, r3)


def _adamw_tiled(w, p4, r3, sel, m, v, tc, name):
    R, C = w.shape
    RS = p4.shape[1]
    n = C // tc

    def body(s_ref, w_ref, p_ref, r_ref, m_ref, v_ref, g_out, d_out, m_out, v_out, buf, sems):
        j = pl.program_id(0)
        slot = j % 2
        outs = (g_out, d_out, m_out, v_out)

        def copies(step, at):
            cols = pl.ds(pl.multiple_of(step * tc, tc), tc)
            return [pltpu.make_async_copy(buf.at[at, k], out.at[:, 0, cols], sems.at[at, k])
                    for k, out in enumerate(outs)]

        g_sum = p_ref[...].astype(F32) + r_ref[0].astype(F32) + r_ref[1].astype(F32) + r_ref[2].astype(F32)
        gv = pltpu.roll(g_sum, RS - s_ref[1], axis=0)[:R]
        d, mn, vn = _adamw_math(w_ref[...], gv, m_ref[...], v_ref[...])

        @pl.when(j >= 2)
        def _():
            for c in copies(j - 2, slot):
                c.wait()

        for k, val in enumerate((gv, d, mn, vn)):
            buf[slot, k] = val
        for c in copies(j, slot):
            c.start()

        @pl.when(j == n - 1)
        def _():
            for c in (copies(j - 1, 1 - slot) if n > 1 else []) + copies(j, slot):
                c.wait()

    blk = pl.BlockSpec((R, tc), lambda j, s: (0, j))
    return pl.pallas_call(
        body,
        grid_spec=pltpu.PrefetchScalarGridSpec(
            num_scalar_prefetch=1, grid=(n,),
            in_specs=[blk, pl.BlockSpec((None, RS, tc), lambda j, s: (s[0], 0, j)),
                      pl.BlockSpec((3, RS, tc), lambda j, s: (0, 0, j)), blk, blk], out_specs=[ANY] * 4,
            scratch_shapes=[pltpu.VMEM((2, 4, R, tc), F32), pltpu.SemaphoreType.DMA((2, 4))]),
        out_shape=[jax.ShapeDtypeStruct((R, 1, C), F32)] * 4,
        name=name, compiler_params=_cparams(("arbitrary",)),
    )(sel, w, p4, r3, m, v)


def _small_sum(parts, name):
    def body(p_ref, o_ref):
        acc = p_ref[0]
        for d in range(1, N_DEV):
            acc = acc + p_ref[d]
        o_ref[...] = acc

    return pl.pallas_call(
        body, out_shape=jax.ShapeDtypeStruct(parts.shape[1:], F32), name=name,
        compiler_params=_cparams(),
    )(parts)


def _adamw_small(w, g, m, v, name):
    def body(w_ref, g_ref, m_ref, v_ref, d_out, m_out, v_out):
        d, mn, vn = _adamw_math(w_ref[...], g_ref[...], m_ref[...], v_ref[...])
        d_out[...] = d
        m_out[...] = mn
        v_out[...] = vn

    return pl.pallas_call(
        body, out_shape=[jax.ShapeDtypeStruct(w.shape, F32)] * 3, name=name, compiler_params=_cparams(),
    )(w, g, m, v)


def _row(*pieces):
    r = jnp.concatenate([p.reshape(1, -1) for p in pieces], axis=1)
    return jnp.pad(r, ((0, 0), (0, D_MODEL - r.shape[1])))


def _pack_small(mix, convb, ssmg, attng, mlpg, fing, convw, dtb, alog, dsk, sinks, extra=None):
    last = [dtb, alog, dsk, sinks] + ([extra] if extra is not None else [])
    rows = [_row(mix), _row(convb), _row(ssmg, attng), _row(mlpg), _row(fing),
            jnp.pad(convw, ((0, 0), (0, D_MODEL - convw.shape[1]))), _row(*last)]
    packed = jnp.concatenate(rows, axis=0)
    return jnp.pad(packed, ((0, SMALL_ROWS - packed.shape[0]), (0, 0)))


def _unpack_small(p, conv_n):
    return dict(
        mix_norm_g=p[0:1, :], conv_b=p[1:2, :], ssm_norm_g=p[2:3, :D_INNER], attn_out_norm_g=p[2:3, D_INNER:],
        mlp_norm_g=p[3:4, :], final_norm_g=p[4, :], conv_w=p[5:9, :conv_n][None],
        dt_bias=p[9:10, 0:16], A_log=p[9:10, 16:32], D_skip=p[9:10, 32:48], attn_sinks=p[9:10, 48:64])


WEIGHT_ORDER = ["mix_norm_g", "w_in", "conv_w", "conv_b", "dt_bias", "A_log", "D_skip", "ssm_norm_g", "attn_sinks",
                "attn_out_norm_g", "w_out", "mlp_norm_g", "w_up", "w_down", "final_norm_g"]


def _to_my_columns(w_nat):
    pad = jnp.zeros((w_nat.shape[0], NP - IN_PROJ), w_nat.dtype)
    return jnp.concatenate([w_nat[:, :NAT_DT], w_nat[:, NAT_DT + N_HEADS:], w_nat[:, NAT_DT:NAT_DT + N_HEADS], pad],
                           axis=1)


PER = IN_PROJ // N_DEV
SUPER_STEP = 544
SUPER = 576


def _natural_rows(g, lo, hi):
    segments = [(0, NAT_DT, 0), (NAT_DT, NAT_DT + N_HEADS, OFF_DT - NAT_DT), (NAT_DT + N_HEADS, IN_PROJ, -N_HEADS),
                (IN_PROJ, NP, 0)]
    pieces = [g[max(lo, a) + shift:min(hi, b) + shift] for a, b, shift in segments if max(lo, a) < min(hi, b)]
    return pieces[0] if len(pieces) == 1 else jnp.concatenate(pieces, axis=0)


def _w_in_from_super_slabs(sup):
    seam = SUPER - SUPER_STEP
    units = []
    for i in range(N_DEV):
        base = SUPER_STEP * i
        units.append((base, base + seam, sup[i, :seam] if i == 0 else sup[i - 1, SUPER_STEP:] + sup[i, :seam]))
        units.append((base + seam, base + SUPER_STEP, sup[i, seam:SUPER_STEP]))
    units.append((SUPER_STEP * N_DEV, SUPER_STEP * N_DEV + seam, sup[N_DEV - 1, SUPER_STEP:]))

    def natural(lo, hi):
        return [rows[max(lo, a) - a:min(hi, b) - a] for a, b, rows in units if max(lo, a) < min(hi, b)]

    pieces = natural(0, NAT_DT) + natural(NAT_DT + N_HEADS, IN_PROJ) + natural(NAT_DT, NAT_DT + N_HEADS)
    return jnp.concatenate(pieces + [jnp.zeros((NP - IN_PROJ, D_MODEL), sup.dtype)], axis=0)


def _to_natural_columns(w_my):
    return jnp.concatenate([w_my[:, :NAT_DT], w_my[:, OFF_DT:OFF_DT + N_HEADS], w_my[:, NAT_DT:OFF_DT]], axis=1)


SLAB = 1024


def _grad_w_up(h2, du, name, sel=None, add=None, after=None):
    T, D = h2.shape
    if sel is None:
        pick, n_slab, pre = (lambda j, *cs: j), N_DEV, None
    else:
        pre, other = sel
        pick, n_slab = (lambda j, cs: 2 * j + ((1 - cs[0]) if other else cs[0])), 4
    o_spec = pl.BlockSpec((None, D, SLAB), lambda i, j, k, *cs: (j, 0, 0))
    return _matmul(
        h2, du, mode="tn", grid=(1, n_slab, 1),
        a_spec=pl.BlockSpec((T, D), lambda i, j, k, *cs: (0, 0)),
        b_spec=pl.BlockSpec((T, SLAB), lambda i, j, k, *cs: (0, pick(j, *cs))),
        out_shapes=[jax.ShapeDtypeStruct((n_slab, D, SLAB), BF16)], out_specs=[o_spec], tile=(D, SLAB), name=name,
        extras=() if add is None else (add,), extra_specs=() if add is None else (o_spec,),
        epilogue=None if add is None else (lambda acc, r: (acc + r.astype(F32),)), after=after, prefetch=pre)[0]


def _grad_w_down(act, dx3b, name, sel=None, add=None, after=None):
    T, D = dx3b.shape
    if sel is None:
        pick, n_slab, pre = (lambda i, *cs: i), N_DEV, None
    else:
        pre, other = sel
        pick, n_slab = (lambda i, cs: 2 * i + ((1 - cs[0]) if other else cs[0])), 4
    o_spec = pl.BlockSpec((None, SLAB, D), lambda i, j, k, *cs: (i, 0, 0))
    return _matmul(
        act, dx3b, mode="tn", grid=(n_slab, 1, 1),
        a_spec=pl.BlockSpec((T, SLAB), lambda i, j, k, *cs: (0, pick(i, *cs))),
        b_spec=pl.BlockSpec((T, D), lambda i, j, k, *cs: (0, 0)),
        out_shapes=[jax.ShapeDtypeStruct((n_slab, SLAB, D), BF16)], out_specs=[o_spec], tile=(SLAB, D), name=name,
        extras=() if add is None else (add,), extra_specs=() if add is None else (o_spec,),
        epilogue=None if add is None else (lambda acc, r: (acc + r.astype(F32),)), after=after, prefetch=pre)[0]


class _FixedWeights:
    def __init__(self, w_in_p, w_out_f, w_up_s, w_down_f, conv_w_f):
        self.w = (w_in_p, w_out_f, w_up_s, w_down_f, conv_w_f)
        self.grads = {}

    def mixer_weights(self, after):
        return self.w[0], None

    def conv_weight(self, after):
        return self.w[4]

    def out_weight(self, after):
        return self.w[1]

    def up_weight(self, after):
        return self.w[2]

    def down_weight(self, h, after):
        return self.w[3][:, h * (D_MODEL // 2):(h + 1) * (D_MODEL // 2)]

    def mlp_grads(self, h2, du, act, dx3b):
        self.grads.update(w_up=_grad_w_up(h2, du, "grad_w_up"),
                          w_down=_grad_w_down(act, dx3b, "grad_w_down").reshape(D_FF, D_MODEL))
        return None

    def grad_sent(self, tag, after):
        return None

    def out_grad(self, g_out):
        self.grads.update(w_out=g_out)
        return None

    def in_grad(self, g_in):
        self.grads.update(w_in=g_in)
        return None


def _local_step(x, tgt, p, hooks):
    T = x.shape[0]
    D = D_MODEL
    h1 = _rmsnorm_fwd(x, p["mix_norm_g"], "norm_mix")
    w_in_t, token = hooks.mixer_weights(h1)
    (proj,) = _mm_simple(h1, w_in_t, mode="nt", M=T, N=NP, K=D, tm=min(T, 1024), tn=1536, tk=D, out_dtype=F32,
                         name="in_proj", after=token)
    conv_w_f = hooks.conv_weight(proj)
    xbc, dsilu = _conv_fwd(proj, conv_w_f, p["conv_b"], "conv_fwd")
    dtT = proj[:, OFF_DT:OFF_DT + N_HEADS].T
    dtbT = p["dt_bias"].T
    alogT = p["A_log"].T
    dfull = jnp.repeat(p["D_skip"], HEAD_DIM, axis=1)
    ycat, ypre, hs = _ssd_fwd(xbc, proj, dtT, p["dt_bias"], dtbT, p["A_log"], alogT, dfull, p["ssm_norm_g"],
                              "ssd_fwd")
    ycat, o_att, probs, sink_probs = _attn_fwd(proj, p["attn_sinks"], p["attn_out_norm_g"], ycat, "attn_fwd")
    w_out_f = hooks.out_weight(ycat)
    tm = min(T, 1024)
    def residual_and_norm(acc, res, gain):
        x2 = acc + res
        return x2, x2 * lax.rsqrt(jnp.mean(x2 * x2, axis=-1, keepdims=True) + EPS) * gain

    rows = min(T, 512)
    x2, h2 = _matmul(
        ycat, w_out_f, mode="nn", grid=(T // rows, 1, 1),
        a_spec=pl.BlockSpec((rows, D), lambda i, j, k: (i, 0)), b_spec=pl.BlockSpec((D, D), lambda i, j, k: (0, 0)),
        out_shapes=[jax.ShapeDtypeStruct((T, D), F32), jax.ShapeDtypeStruct((T, D), BF16)],
        out_specs=[pl.BlockSpec((rows, D), lambda i, j, k: (i, 0))] * 2, tile=(rows, D), name="out_proj",
        extras=(x, p["mlp_norm_g"]),
        extra_specs=[pl.BlockSpec((rows, D), lambda i, j, k: (i, 0)), pl.BlockSpec((1, D), lambda i, j, k: (0, 0))],
        epilogue=residual_and_norm)
    w_up_s = hooks.up_weight(h2)
    grid = (T // tm, N_DEV, 1)
    u, act = _matmul(
        h2, w_up_s, mode="nn", grid=grid,
        a_spec=pl.BlockSpec((tm, D), lambda i, j, k: (i, 0)),
        b_spec=pl.BlockSpec((None, D, 1024), lambda i, j, k: (j, 0, 0)),
        out_shapes=[jax.ShapeDtypeStruct((T, D_FF), F32), jax.ShapeDtypeStruct((T, D_FF), BF16)],
        out_specs=[pl.BlockSpec((tm, 1024), lambda i, j, k: (i, j))] * 2, tile=(tm, 1024), name="mlp_up",
        epilogue=lambda acc: (acc, jnp.square(jnp.maximum(acc, 0.0))))
    half = D // 2
    w_down_halves, x3_halves = [], []
    for h in range(2):
        w_down_halves.append(hooks.down_weight(h, act if h == 0 else x3_halves[0]))
        x3_halves.append(_matmul(
            act, w_down_halves[h], mode="nn", grid=(T // tm, 1, D_FF // 2048),
            a_spec=pl.BlockSpec((tm, 2048), lambda i, j, k: (i, k)),
            b_spec=pl.BlockSpec((2048, half), lambda i, j, k: (k, 0)),
            out_shapes=[jax.ShapeDtypeStruct((T, half), F32)],
            out_specs=[pl.BlockSpec((tm, half), lambda i, j, k: (i, 0))], tile=(tm, half), name=f"mlp_down_{h}",
            extras=(x2,), extra_specs=[pl.BlockSpec((tm, half), lambda i, j, k, h=h: (i, h))],
            epilogue=lambda acc, res: (acc + res,))[0])
    loss_part, d_fin, dx3, dx3b = _final_loss(x3_halves, tgt, p["final_norm_g"].reshape(1, D), "loss_head")
    (du,) = _matmul(
        dx3b, tuple(w_down_halves), mode="nt", grid=(T // tm, D_FF // 1024, 1),
        a_spec=pl.BlockSpec((tm, D), lambda i, j, k: (i, 0)),
        b_spec=(pl.BlockSpec((1024, half), lambda i, j, k: (j, 0)),) * 2,
        out_shapes=[jax.ShapeDtypeStruct((T, D_FF), BF16)],
        out_specs=[pl.BlockSpec((tm, 1024), lambda i, j, k: (i, j))], tile=(tm, 1024), name="mlp_down_bwd",
        extras=(u,), extra_specs=[pl.BlockSpec((tm, 1024), lambda i, j, k: (i, j))],
        epilogue=lambda acc, uu: (acc * (2.0 * jnp.maximum(uu, 0.0)),),
        dot_fn=lambda a, b0, b1: _dot_nt(a[:, :half], b0) + _dot_nt(a[:, half:], b1))
    token = hooks.mlp_grads(h2, du, act, dx3b)
    (dh2,) = _matmul(
        du, w_up_s, mode="nt", grid=(T // tm, D // 1024, N_DEV // 2),
        a_spec=pl.BlockSpec((tm, 2048), lambda i, j, k: (i, k)),
        b_spec=pl.BlockSpec((2, 1024, 1024), lambda i, j, k: (k, j, 0)),
        out_shapes=[jax.ShapeDtypeStruct((T, D), F32)],
        out_specs=[pl.BlockSpec((tm, 1024), lambda i, j, k: (i, j))], tile=(tm, 1024), name="mlp_up_bwd",
        after=token, dot_fn=lambda a, b: _dot_nt(a[:, :1024], b[0]) + _dot_nt(a[:, 1024:], b[1]))
    dx2, dx2b, d_mlp = _rmsnorm_bwd(dh2, x2, p["mlp_norm_g"], dx3, "norm_mlp_bwd")
    (g_out,) = _mm_simple(ycat, dx2b, mode="tn", M=D, N=D, K=T, tm=1024, tn=1024, tk=T, out_dtype=BF16,
                          name="grad_w_out")
    token = hooks.out_grad(g_out)
    (dy,) = _mm_simple(dx2b, w_out_f, mode="nt", M=T, N=D, K=D, tm=tm, tn=1024, tk=D, out_dtype=F32,
                       name="out_proj_bwd", after=token)
    token = hooks.grad_sent("out", dy)
    ssm_g = p["ssm_norm_g"] if token is None else p["ssm_norm_g"] + token[0:1, 0:1]
    dproj, dxbc_act, d_dtb, d_alog, d_dskip, d_ssmg = _ssd_bwd(
        xbc, proj, dtT, p["dt_bias"], dtbT, p["A_log"], alogT, dfull, ssm_g, ypre, hs, dy, "ssd_bwd")
    dproj, d_convw, d_convb = _conv_bwd(proj, dxbc_act, dsilu, conv_w_f, dproj, "conv_bwd")
    dproj, dk, dv, d_sinks, d_attng = _attn_bwd(proj, p["attn_out_norm_g"], o_att, dy, probs, sink_probs, dproj,
                                                "attn_bwd")
    dproj = lax.dynamic_update_slice(dproj, jnp.concatenate([dk, dv], axis=1).astype(BF16), (0, OFF_K))
    (g_in,) = _mm_simple(dproj, h1, mode="tn", M=NP, N=D, K=T, tm=1536, tn=1024, tk=T, out_dtype=BF16,
                         name="grad_w_in")
    token = hooks.in_grad(g_in)
    (dh1,) = _mm_simple(dproj, w_in_t, mode="nn", M=T, N=D, K=NP, tm=tm, tn=1024, tk=2304, out_dtype=F32,
                        name="in_proj_bwd", after=token)
    token = hooks.grad_sent("in", dh1)
    mix_g = p["mix_norm_g"] if token is None else p["mix_norm_g"] + token[0:1, 0:1]
    dx, d_mix = _rmsnorm_bwd(dh1, x, mix_g, dx2, "norm_mix_bwd", with_bf16=False)
    small = _pack_small(d_mix, d_convb, d_ssmg, d_attng, d_mlp, d_fin, d_convw, d_dtb, d_alog, d_dskip, d_sinks,
                        extra=loss_part[:, 0:1])
    return dx, small


def _rows_rotated(v, shift, name):
    R, C = v.shape
    tc = 512

    def body(s_ref, v_ref, o_ref):
        o_ref[...] = pltpu.roll(v_ref[...], s_ref[0], axis=0).astype(BF16)

    return pl.pallas_call(
        body,
        grid_spec=pltpu.PrefetchScalarGridSpec(
            num_scalar_prefetch=1, grid=(C // tc,), in_specs=[pl.BlockSpec((R, tc), lambda j, s: (0, j))],
            out_specs=pl.BlockSpec((R, tc), lambda j, s: (0, j))),
        out_shape=jax.ShapeDtypeStruct((R, C), BF16), name=name, compiler_params=_cparams(("parallel",)),
    )(shift, v)


def _landing(own, me):
    zone = lax.empty((N_DEV,) + own.shape, own.dtype)
    return lax.dynamic_update_slice(zone, own[None], (me,) + (0,) * own.ndim)


def _sequencer_gather(owns, split, me, collective_id, name):
    n = len(owns)
    zone_refs = [jax.new_ref(_landing(o, me), memory_space=pltpu.MemorySpace.HBM) for o in owns]
    own_refs = [jax.new_ref(o, memory_space=pltpu.MemorySpace.HBM) for o in owns]
    N_COPIES = 9

    @pl.kernel(mesh=plsc.ScalarSubcoreMesh(axis_name="sequencer", num_cores=1), name=name,
               scratch_types=(pltpu.SemaphoreType.DMA((n, N_COPIES)), pltpu.SemaphoreType.DMA((n, N_COPIES))),
               compiler_params=pltpu.CompilerParams(collective_id=collective_id))
    def launch(send_sems, recv_sems):
        x, y, c = _coords()
        sibling, xn, yn, diag = (x, y, 1 - c), (1 - x, y, c), (x, 1 - y, c), (1 - x, 1 - y, c)
        barrier = pltpu.get_barrier_semaphore()
        for peer in [sibling, xn, yn, diag]:
            pl.semaphore_signal(barrier, inc=1, device_id=peer, device_id_type=MESH)
        pl.semaphore_wait(barrier, 4)

        def block(a, dev, half=None):
            ref = zone_refs[a].at[4 * dev[0] + 2 * dev[1] + dev[2]]
            if half is None:
                return ref
            rows = owns[a].shape[0] // 2
            return ref.at[pl.ds(half * rows, rows)]

        def copy(a, k, src, dst, to):
            return pltpu.make_async_remote_copy(src_ref=src, dst_ref=dst, send_sem=send_sems.at[a, k],
                                                recv_sem=recv_sems.at[a, k], device_id=to, device_id_type=MESH)

        me_dev = (x, y, c)
        sent = []
        first = {}
        for a in range(n):
            for k, peer in enumerate([sibling, xn, yn] + ([] if split[a] else [diag])):
                first[a, k] = copy(a, k, own_refs[a], block(a, me_dev), peer)
                first[a, k].start()
                sent.append(first[a, k])
        from_sibling = []
        for a in range(n):
            first[a, 1].wait_recv()
            sent.append(copy(a, 4, block(a, xn), block(a, xn), sibling))
            if split[a]:
                sent.append(copy(a, 6, block(a, xn, 0), block(a, xn, 0), yn))
            first[a, 2].wait_recv()
            sent.append(copy(a, 5, block(a, yn), block(a, yn), sibling))
            if split[a]:
                sent.append(copy(a, 7, block(a, yn, 1), block(a, yn, 1), xn))
            for cp in sent[-(4 if split[a] else 2):]:
                cp.start()
        for a in range(n):
            if split[a]:
                copy(a, 6, block(a, diag, 0), block(a, diag, 0), yn).wait_recv()
                sent.append(copy(a, 8, block(a, diag, 0), block(a, diag, 0), sibling))
                sent[-1].start()
                copy(a, 7, block(a, diag, 1), block(a, diag, 1), xn).wait_recv()
                sent.append(copy(a, 3, block(a, diag, 1), block(a, diag, 1), sibling))
                sent[-1].start()
            else:
                first[a, 3].wait_recv()
                sent.append(copy(a, 8, block(a, diag), block(a, diag), sibling))
                sent[-1].start()
        for a in range(n):
            first[a, 0].wait_recv()
            copy(a, 4, block(a, xn), block(a, xn), sibling).wait_recv()
            copy(a, 5, block(a, yn), block(a, yn), sibling).wait_recv()
            if split[a]:
                copy(a, 8, block(a, diag, 0), block(a, diag, 0), sibling).wait_recv()
                copy(a, 3, block(a, diag, 1), block(a, diag, 1), sibling).wait_recv()
            else:
                copy(a, 8, block(a, diag), block(a, diag), sibling).wait_recv()
        for cp in sent:
            cp.wait_send()

    launch()
    return zone_refs


class _ShardedWeights:
    def __init__(self, w_in, w_out, conv_w, w_up, w_down, me, csel):
        self.me, self.csel = me, csel
        padded = jnp.pad(jnp.transpose(w_in), ((0, SUPER - PER), (0, 0)))
        own_rows = _rows_rotated(padded, jnp.reshape(2 * me, (1,)).astype(jnp.int32), "w_in_super_slab")
        (self.in_ref,) = _sequencer_gather([own_rows], [True], me, 7, "gather_w_in_sequencer")
        self.out_ref, self.conv_ref = _sequencer_gather([w_out.astype(BF16), conv_w], [True, False], me, 8,
                                                        "gather_w_out_sequencer")
        (self.up_ref,) = _sequencer_gather([w_up.astype(BF16)], [True], me, 9, "gather_w_up_sequencer")
        down = w_down.astype(BF16)
        self.down_refs = [_sequencer_gather([down[:, h * (D_MODEL // 2):(h + 1) * (D_MODEL // 2)]], [True], me, 10 + h,
                                            f"gather_w_down_{h}_sequencer")[0] for h in range(2)]
        self.reduces = {}
        self.pairs = {}

    def mixer_weights(self, after):
        return _w_in_from_super_slabs(self.in_ref[...]), None

    def conv_weight(self, after):
        g_conv = self.conv_ref[...]
        return jnp.concatenate([g_conv[i] for i in range(N_DEV)], axis=1)

    def out_weight(self, after):
        return self.out_ref[...].reshape(D_MODEL, D_MODEL)

    def up_weight(self, after):
        return self.up_ref[...]

    def down_weight(self, h, after):
        return self.down_refs[h][...].reshape(D_FF, D_MODEL // 2)

    def _chips_start(self, slabs, from_sibling, rows, tag):
        sums = [_pair_add(s, r, self.csel, tr, f"pair_add_{tag}_{i}")
                for i, (s, r, tr) in enumerate(zip(slabs, from_sibling, rows))]
        lands = [lax.empty((3,) + s.shape[1:], s.dtype) for s in sums]
        self.reduces[tag] = _remote_start(sums, lands, _chips_plan, 3 * len(sums), f"reduce_start_{tag}")
        return self.reduces[tag][4]

    def mlp_grads(self, h2, du, act, dx3b):
        def send(part, tag, after):
            st = _remote_start([part], [lax.empty(part.shape, part.dtype)], _pair4_plan, 4,
                               f"reduce_pair_start_{tag}", after=after)
            return st

        def received(st, after, tag):
            return _remote_wait(st, after, _pair4_plan, f"reduce_pair_wait_{tag}")[1][0]

        def to_chips(sums, tag):
            self.reduces[tag] = _remote_start([sums], [lax.empty((3,) + sums.shape[1:], sums.dtype)], _chips_plan, 3,
                                              f"reduce_start_{tag}")
            return self.reduces[tag][4]

        up_send = _grad_w_up(h2, du, "grad_w_up_send", sel=(self.csel, True))
        st_up = send(up_send, "up", None)
        down_send = _grad_w_down(act, dx3b, "grad_w_down_send", sel=(self.csel, True), after=st_up[4])
        st_down = send(down_send, "down", None)
        up_sum = _grad_w_up(h2, du, "grad_w_up_keep", sel=(self.csel, False), add=received(st_up, down_send, "up"),
                            after=st_down[4])
        token = to_chips(up_sum, "up")
        down_sum = _grad_w_down(act, dx3b, "grad_w_down_keep", sel=(self.csel, False),
                                add=received(st_down, up_sum, "down"), after=token)
        return to_chips(down_sum, "down")

    def _pair_start(self, slabs, tag):
        land = lax.empty((4,) + slabs.shape[1:], slabs.dtype)
        self.pairs[tag] = _remote_start([slabs], [land], _pair_plan, 4, f"reduce_pair_start_{tag}")
        return self.pairs[tag][4]

    def grad_sent(self, tag, after):
        slabs, from_sibling = _remote_wait(self.pairs[tag], after, _pair_plan, f"reduce_pair_wait_{tag}")
        return self._chips_start(slabs, from_sibling, [slabs[0].shape[1]], tag)

    def out_grad(self, g_out):
        return self._pair_start(g_out.reshape(N_DEV, D_MODEL // N_DEV, D_MODEL), "out")

    def in_grad(self, g_in):
        return self._pair_start(
            jnp.stack([_natural_rows(g_in, SUPER_STEP * j, SUPER_STEP * j + SUPER) for j in range(N_DEV)]), "in")

    def small_start(self, small):
        self.st_small = _remote_start([small], [_landing(small, self.me)], _everyone_plan, N_DEV - 1, "gather_start_small")

    def small_end(self, after):
        return _remote_wait(self.st_small, after, _everyone_plan, "gather_small_wait")[1][0]

    def reduce_end(self, tag, after):
        return _remote_wait(self.reduces[tag], after, _chips_plan, f"reduce_wait_{tag}")


def kernel(x, mix_norm_g, w_in, conv_w, conv_b, dt_bias, A_log, D_skip, ssm_norm_g, attn_sinks, attn_out_norm_g, w_out, mlp_norm_g, w_up, w_down, final_norm_g, loss_target, m_mix_norm_g, m_w_in, m_conv_w, m_conv_b, m_dt_bias, m_A_log, m_D_skip, m_ssm_norm_g, m_attn_sinks, m_attn_out_norm_g, m_w_out, m_mlp_norm_g, m_w_up, m_w_down, m_final_norm_g, v_mix_norm_g, v_w_in, v_conv_w, v_conv_b, v_dt_bias, v_A_log, v_D_skip, v_ssm_norm_g, v_attn_sinks, v_attn_out_norm_g, v_w_out, v_mlp_norm_g, v_w_up, v_w_down, v_final_norm_g):
    xi, yi, ci = _coords()
    me = 4 * xi + 2 * yi + ci
    csel = jnp.reshape(ci, (1,)).astype(jnp.int32)
    qsel = jnp.reshape(2 * xi + yi, (1,)).astype(jnp.int32)
    w = dict(mix_norm_g=mix_norm_g, conv_b=conv_b, dt_bias=dt_bias, A_log=A_log, D_skip=D_skip,
             ssm_norm_g=ssm_norm_g, attn_sinks=attn_sinks, attn_out_norm_g=attn_out_norm_g, mlp_norm_g=mlp_norm_g,
             final_norm_g=final_norm_g)
    hooks = _ShardedWeights(w_in[0], w_out[0], conv_w[0], w_up[0], w_down[0], me, csel)
    p = dict(w)
    dx, small = _local_step(x[0], loss_target[0], p, hooks)
    hooks.small_start(small)
    big = {}
    after = dx
    for name, wt, mt, vt, tile in [
            ("up", w_up, m_w_up, v_w_up, (512, SLAB)), ("down", w_down, m_w_down, v_w_down, (256, D_MODEL)),
            ("out", w_out, m_w_out, v_w_out, (256, D_MODEL))]:
        (chip_sums,), (from_chips,) = hooks.reduce_end(name, after)
        res = _adamw_big(wt[0], mt[0], vt[0], chip_sums, from_chips, qsel, tile, f"adamw_w_{name}")
        big["w_" + name] = tuple(r[None] for r in res)
        after = res[0]
    (chip_sums,), (from_chips,) = hooks.reduce_end("in", after)
    sel = jnp.concatenate([qsel, jnp.reshape(2 * me, (1,)).astype(jnp.int32)])
    res = _adamw_tiled(jnp.transpose(w_in[0]), chip_sums, from_chips, sel, jnp.transpose(m_w_in[0]),
                       jnp.transpose(v_w_in[0]), 512, "adamw_w_in")
    big["w_in"] = tuple(jnp.transpose(r, (1, 2, 0)) for r in res)
    after = res[0]
    gsum = _small_sum(hooks.small_end(after), "small_sum")
    loss = gsum[9, 64]
    gs = _unpack_small(gsum, CONV_DIM)
    cw = CONV_DIM // N_DEV
    g_conv_shard = lax.dynamic_slice(gsum[5:9, :], (0, me * cw), (CONV_K, cw))

    def pack(s):
        return _pack_small(s["mix_norm_g"], s["conv_b"], s["ssm_norm_g"], s["attn_out_norm_g"], s["mlp_norm_g"],
                           s["final_norm_g"], s["conv_w"][0], s["dt_bias"], s["A_log"], s["D_skip"], s["attn_sinks"])

    wp = pack(dict(w, conv_w=conv_w))
    mp = pack(dict(mix_norm_g=m_mix_norm_g, conv_b=m_conv_b, ssm_norm_g=m_ssm_norm_g,
                   attn_out_norm_g=m_attn_out_norm_g, mlp_norm_g=m_mlp_norm_g, final_norm_g=m_final_norm_g,
                   conv_w=m_conv_w, dt_bias=m_dt_bias, A_log=m_A_log, D_skip=m_D_skip, attn_sinks=m_attn_sinks))
    vp = pack(dict(mix_norm_g=v_mix_norm_g, conv_b=v_conv_b, ssm_norm_g=v_ssm_norm_g,
                   attn_out_norm_g=v_attn_out_norm_g, mlp_norm_g=v_mlp_norm_g, final_norm_g=v_final_norm_g,
                   conv_w=v_conv_w, dt_bias=v_dt_bias, A_log=v_A_log, D_skip=v_D_skip, attn_sinks=v_attn_sinks))
    gp = jnp.concatenate([gsum[0:5], jnp.pad(g_conv_shard, ((0, 0), (0, D_MODEL - cw))), gsum[9:10],
                          jnp.zeros((SMALL_ROWS - 10, D_MODEL), F32)], axis=0)
    dp, mnp, vnp = _adamw_small(wp, gp, mp, vp, "adamw_small")
    grads = dict(gs, conv_w=g_conv_shard[None])
    deltas = _unpack_small(dp, cw)
    new_m = _unpack_small(mnp, cw)
    new_v = _unpack_small(vnp, cw)
    for k, name in enumerate(["w_in", "w_out", "w_up", "w_down"]):
        grads[name], deltas[name], new_m[name], new_v[name] = big[name]
    return (loss, dx[None], *[grads[n] for n in WEIGHT_ORDER], *[deltas[n] for n in WEIGHT_ORDER],
            *[new_m[n] for n in WEIGHT_ORDER], *[new_v[n] for n in WEIGHT_ORDER])
```
